```python
import jax, jax.numpy as jnp
from jax import lax
import numpy as np

D_MODEL = 2048
BATCH = 8
SEQ = 2048
DEPTH = 1

D_MIX = D_MODEL
HG_HEADS = 8
HG_KEY = 128
HG_VAL = 128
HG_KW = HG_HEADS * HG_KEY
HG_WIDTH = HG_HEADS * HG_VAL
ATT_HEADS = 8
ATT_HEAD_DIM = 128
ATT_WIDTH = ATT_HEADS * ATT_HEAD_DIM
DILATED_PATTERNS = ((128, 1), (512, 4), (2048, 16))
N_BACK = 128
ATT_BLOCK = N_BACK
CHUNK = 64
D_FF = 5632
CONV_WIDTH = 3
EPS = 1e-6
IN_SPLITS = (HG_KW, 2 * HG_KW, 2 * HG_KW + HG_WIDTH, 2 * HG_KW + 2 * HG_WIDTH,
             2 * HG_KW + 2 * HG_WIDTH + ATT_WIDTH, 2 * HG_KW + 2 * HG_WIDTH + 2 * ATT_WIDTH)
IN_COLS = 2 * HG_KW + 2 * HG_WIDTH + 3 * ATT_WIDTH

kernel_name = "hymba_hgrn2_dilated_alibi_convffn_adaln"


def _rms(x, w):
    xf = x.astype(jnp.float32)
    y = xf * lax.rsqrt(jnp.mean(xf * xf, axis=-1, keepdims=True) + EPS)
    return (y * w.astype(jnp.float32)).astype(x.dtype)


def _hgrn2(hq, hf, hi, hg, lb, norm_w):
    b, s, _ = hq.shape
    f32 = jnp.float32
    q = jax.nn.silu(hq.astype(f32)).reshape(b, s, HG_HEADS, HG_KEY)
    f = lb.astype(f32) + (1.0 - lb.astype(f32)) * jax.nn.sigmoid(hf.astype(f32))
    f = f.reshape(b, s, HG_HEADS, HG_KEY)
    g = jnp.log(f)
    k = 1.0 - f
    v = hi.astype(f32).reshape(b, s, HG_HEADS, HG_VAL)
    nc = s // CHUNK

    def to_chunks(t):
        return t.reshape(b, nc, CHUNK, HG_HEADS, t.shape[-1]).transpose(1, 0, 3, 2, 4)

    causal = jnp.tril(jnp.ones((CHUNK, CHUNK), dtype=bool))

    def step(state, inp):
        qc, kc, vc, gc = inp
        G = jnp.cumsum(gc, axis=2)
        o_inter = jnp.einsum('bhtk,bhkv->bhtv', qc * jnp.exp(G), state)
        diff = G[:, :, :, None, :] - G[:, :, None, :, :]
        decay = jnp.exp(jnp.where(causal[:, :, None], diff, -jnp.inf))
        A = jnp.einsum('bhtk,bhtsk,bhsk->bhts', qc, decay, kc)
        o_intra = jnp.einsum('bhts,bhsv->bhtv', A, vc)
        G_last = G[:, :, -1:, :]
        state = (jnp.exp(G_last[:, :, 0, :])[..., None] * state
                 + jnp.einsum('bhsk,bhsv->bhkv', kc * jnp.exp(G_last - G), vc))
        return state, o_inter + o_intra

    state0 = jnp.zeros((b, HG_HEADS, HG_KEY, HG_VAL), f32)
    _, o = lax.scan(step, state0, (to_chunks(q), to_chunks(k), to_chunks(v), to_chunks(g)))
    o = o.transpose(1, 0, 3, 2, 4).reshape(b, s, HG_HEADS, HG_VAL)
    o = _rms(o, norm_w).reshape(b, s, HG_WIDTH) * jax.nn.silu(hg.astype(f32))
    return o.astype(hg.dtype)


def _dilated_branch(q, k, v, slopes, dil):
    b, s, h, dh = q.shape
    L = s // dil
    nblk = -(-L // ATT_BLOCK)
    Lp = nblk * ATT_BLOCK

    def split(t):
        t = t.reshape(b, L, dil, h, dh).transpose(0, 2, 3, 1, 4)
        t = jnp.pad(t, ((0, 0), (0, 0), (0, 0), (0, Lp - L), (0, 0)))
        return t.reshape(b, dil, h, nblk, ATT_BLOCK, dh)

    def with_prev(t):
        prev = jnp.pad(t, ((0, 0), (0, 0), (0, 0), (1, 0), (0, 0), (0, 0)))[:, :, :, :-1]
        return jnp.concatenate([prev, t], axis=4)

    qb = split(q)
    kc = with_prev(split(k))
    vc = with_prev(split(v))
    scores = jnp.einsum('brhnqd,brhnkd->brhnqk', qb, kc).astype(jnp.float32) * (dh ** -0.5)

    qi = jnp.arange(ATT_BLOCK)[:, None]
    kj = jnp.arange(2 * ATT_BLOCK)[None, :]
    steps = qi + ATT_BLOCK - kj
    blk = jnp.arange(nblk)[:, None, None]
    valid = (steps >= 0) & (steps <= N_BACK) & ((blk > 0) | (kj >= ATT_BLOCK))
    alibi = -slopes[:, None, None, None] * (steps * dil).astype(jnp.float32)[None, None]
    scores = jnp.where(valid, scores + alibi, -jnp.inf)
    m = jnp.max(scores, axis=-1, keepdims=True)
    p = jnp.exp(scores - m)
    den = jnp.sum(p, axis=-1, keepdims=True)
    o = jnp.einsum('brhnqk,brhnkd->brhnqd', p, vc.astype(jnp.float32)) / den
    lse = (m + jnp.log(den))[..., 0]

    def merge(t):
        t = t.reshape(b, dil, h, Lp, *t.shape[5:])[:, :, :, :L]
        t = jnp.moveaxis(t, 3, 1)
        return t.reshape(b, s, h, *t.shape[4:])

    return merge(o), merge(lse)


def _dilated_mixture(aq, ak, av, q_norm_w, k_norm_w):
    b, s, _ = aq.shape
    q = _rms(aq.reshape(b, s, ATT_HEADS, ATT_HEAD_DIM), q_norm_w)
    k = _rms(ak.reshape(b, s, ATT_HEADS, ATT_HEAD_DIM), k_norm_w)
    v = av.reshape(b, s, ATT_HEADS, ATT_HEAD_DIM)
    slopes = jnp.exp2(-8.0 * jnp.arange(1, ATT_HEADS + 1, dtype=jnp.float32) / ATT_HEADS)
    outs, lses = [], []
    for _, dil in DILATED_PATTERNS:
        o_p, l_p = _dilated_branch(q, k, v, slopes, dil)
        outs.append(o_p)
        lses.append(l_p)
    weights = jax.nn.softmax(jnp.stack(lses), axis=0)
    o = jnp.einsum('pbsh,pbshd->bshd', weights, jnp.stack(outs))
    return o.reshape(b, s, ATT_WIDTH).astype(aq.dtype)


def _causal_dwconv(a, w, bias):
    s = a.shape[1]
    ap = jnp.pad(a, ((0, 0), (CONV_WIDTH - 1, 0), (0, 0)))
    y = bias
    for j in range(CONV_WIDTH):
        y = y + ap[:, j:j + s] * w[j]
    return y


def _layer(x, mod, norm1_w, w_in, lb, hg_norm_w, q_norm_w, k_norm_w, w_out,
           norm2_w, w_up, conv_w, conv_b, w_down):
    shift1, scale1, gate1, shift2, scale2, gate2 = jnp.split(mod, 6, axis=-1)
    h = _rms(x, norm1_w) * (1.0 + scale1[:, None]) + shift1[:, None]
    proj = h @ w_in
    hq, hf, hi, hg, aq, ak, av = jnp.split(proj, IN_SPLITS, axis=-1)
    a_out = _hgrn2(hq, hf, hi, hg, lb, hg_norm_w)
    b_out = _dilated_mixture(aq, ak, av, q_norm_w, k_norm_w)
    mix = jnp.concatenate([a_out, b_out], axis=-1) @ w_out
    x = x + gate1[:, None] * mix

    h2 = _rms(x, norm2_w) * (1.0 + scale2[:, None]) + shift2[:, None]
    u = h2 @ w_up
    a, g = jnp.split(u, 2, axis=-1)
    y = jax.nn.silu(_causal_dwconv(a, conv_w, conv_b)) * g
    return x + gate2[:, None] * (y @ w_down)


def _fwd_setup_inputs(seed: int = 0) -> dict:
    key = jax.random.key(seed)
    ks = jax.random.split(key, 16)
    f32 = jnp.float32
    nrm = lambda k, shp, sc: jax.random.normal(k, shp, f32) * sc
    D = D_MODEL
    return {
        "x": nrm(ks[0], (BATCH, SEQ, D), 1.0),
        "c": nrm(ks[1], (BATCH, D), 1.0),
        "w_ada": nrm(ks[2], (DEPTH, D, 6 * D), D ** -0.5),
        "b_ada": nrm(ks[3], (DEPTH, 6 * D), 0.02),
        "norm1_w": 1.0 + nrm(ks[4], (DEPTH, D), 0.02),
        "w_in": nrm(ks[5], (DEPTH, D, IN_COLS), D ** -0.5),
        "lb_logits": nrm(ks[6], (DEPTH + 1, HG_KW), 1.0),
        "hg_norm_w": 1.0 + nrm(ks[7], (DEPTH, HG_VAL), 0.02),
        "q_norm_w": 1.0 + nrm(ks[8], (DEPTH, ATT_HEAD_DIM), 0.02),
        "k_norm_w": 1.0 + nrm(ks[9], (DEPTH, ATT_HEAD_DIM), 0.02),
        "w_out": nrm(ks[10], (DEPTH, D_MIX, D), D_MIX ** -0.5),
        "norm2_w": 1.0 + nrm(ks[11], (DEPTH, D), 0.02),
        "w_up": nrm(ks[12], (DEPTH, D, 2 * D_FF), D ** -0.5),
        "conv_w": nrm(ks[13], (DEPTH, CONV_WIDTH, D_FF), CONV_WIDTH ** -0.5),
        "conv_b": nrm(ks[14], (DEPTH, D_FF), 0.02),
        "w_down": nrm(ks[15], (DEPTH, D_FF, D), D_FF ** -0.5),
    }


def _fwd_reference(x, c, w_ada, b_ada, norm1_w, w_in, lb_logits, hg_norm_w, q_norm_w, k_norm_w,
              w_out, norm2_w, w_up, conv_w, conv_b, w_down):
    lb_all = jnp.cumsum(jax.nn.softmax(lb_logits.astype(jnp.float32), axis=0), axis=0)
    c_act = jax.nn.silu(c)
    for l in range(DEPTH):
        mod = c_act @ w_ada[l] + b_ada[l]
        x = _layer(x, mod, norm1_w[l], w_in[l], lb_all[l], hg_norm_w[l], q_norm_w[l],
                   k_norm_w[l], w_out[l], norm2_w[l], w_up[l], conv_w[l], conv_b[l], w_down[l])
    return x


import jax as _jax
import jax.numpy as _jnp

TWIN_FORMAT = 'train_step'
FWD_PARAMS = ['x', 'c', 'w_ada', 'b_ada', 'norm1_w', 'w_in', 'lb_logits', 'hg_norm_w', 'q_norm_w', 'k_norm_w', 'w_out', 'norm2_w', 'w_up', 'conv_w', 'conv_b', 'w_down']
TWIN_WEIGHTS = ['w_ada', 'b_ada', 'norm1_w', 'w_in', 'lb_logits', 'hg_norm_w', 'q_norm_w', 'k_norm_w', 'w_out', 'norm2_w', 'w_up', 'conv_w', 'conv_b', 'w_down']
TWIN_DIFF_INPUT = 'x'
TWIN_INPUTS = ['x', 'c', 'w_ada', 'b_ada', 'norm1_w', 'w_in', 'lb_logits', 'hg_norm_w', 'q_norm_w', 'k_norm_w', 'w_out', 'norm2_w', 'w_up', 'conv_w', 'conv_b', 'w_down', 'loss_target', 'm_w_ada', 'm_b_ada', 'm_norm1_w', 'm_w_in', 'm_lb_logits', 'm_hg_norm_w', 'm_q_norm_w', 'm_k_norm_w', 'm_w_out', 'm_norm2_w', 'm_w_up', 'm_conv_w', 'm_conv_b', 'm_w_down', 'v_w_ada', 'v_b_ada', 'v_norm1_w', 'v_w_in', 'v_lb_logits', 'v_hg_norm_w', 'v_q_norm_w', 'v_k_norm_w', 'v_w_out', 'v_norm2_w', 'v_w_up', 'v_conv_w', 'v_conv_b', 'v_w_down']
TWIN_OUTPUTS = ['loss', 'grad_x', 'grad_w_ada', 'grad_b_ada', 'grad_norm1_w', 'grad_w_in', 'grad_lb_logits', 'grad_hg_norm_w', 'grad_q_norm_w', 'grad_k_norm_w', 'grad_w_out', 'grad_norm2_w', 'grad_w_up', 'grad_conv_w', 'grad_conv_b', 'grad_w_down', 'delta_w_ada', 'delta_b_ada', 'delta_norm1_w', 'delta_w_in', 'delta_lb_logits', 'delta_hg_norm_w', 'delta_q_norm_w', 'delta_k_norm_w', 'delta_w_out', 'delta_norm2_w', 'delta_w_up', 'delta_conv_w', 'delta_conv_b', 'delta_w_down', 'new_m_w_ada', 'new_m_b_ada', 'new_m_norm1_w', 'new_m_w_in', 'new_m_lb_logits', 'new_m_hg_norm_w', 'new_m_q_norm_w', 'new_m_k_norm_w', 'new_m_w_out', 'new_m_norm2_w', 'new_m_w_up', 'new_m_conv_w', 'new_m_conv_b', 'new_m_w_down', 'new_v_w_ada', 'new_v_b_ada', 'new_v_norm1_w', 'new_v_w_in', 'new_v_lb_logits', 'new_v_hg_norm_w', 'new_v_q_norm_w', 'new_v_k_norm_w', 'new_v_w_out', 'new_v_norm2_w', 'new_v_w_up', 'new_v_conv_w', 'new_v_conv_b', 'new_v_w_down']
TWIN_LEAF_KINDS = {'loss': 'loss', 'grad_x': 'grad_x', 'grad_w_ada': 'grad_w', 'grad_b_ada': 'grad_w', 'grad_norm1_w': 'grad_w', 'grad_w_in': 'grad_w', 'grad_lb_logits': 'grad_w', 'grad_hg_norm_w': 'grad_w', 'grad_q_norm_w': 'grad_w', 'grad_k_norm_w': 'grad_w', 'grad_w_out': 'grad_w', 'grad_norm2_w': 'grad_w', 'grad_w_up': 'grad_w', 'grad_conv_w': 'grad_w', 'grad_conv_b': 'grad_w', 'grad_w_down': 'grad_w', 'delta_w_ada': 'delta_w', 'delta_b_ada': 'delta_w', 'delta_norm1_w': 'delta_w', 'delta_w_in': 'delta_w', 'delta_lb_logits': 'delta_w', 'delta_hg_norm_w': 'delta_w', 'delta_q_norm_w': 'delta_w', 'delta_k_norm_w': 'delta_w', 'delta_w_out': 'delta_w', 'delta_norm2_w': 'delta_w', 'delta_w_up': 'delta_w', 'delta_conv_w': 'delta_w', 'delta_conv_b': 'delta_w', 'delta_w_down': 'delta_w', 'new_m_w_ada': 'new_m', 'new_m_b_ada': 'new_m', 'new_m_norm1_w': 'new_m', 'new_m_w_in': 'new_m', 'new_m_lb_logits': 'new_m', 'new_m_hg_norm_w': 'new_m', 'new_m_q_norm_w': 'new_m', 'new_m_k_norm_w': 'new_m', 'new_m_w_out': 'new_m', 'new_m_norm2_w': 'new_m', 'new_m_w_up': 'new_m', 'new_m_conv_w': 'new_m', 'new_m_conv_b': 'new_m', 'new_m_w_down': 'new_m', 'new_v_w_ada': 'new_v', 'new_v_b_ada': 'new_v', 'new_v_norm1_w': 'new_v', 'new_v_w_in': 'new_v', 'new_v_lb_logits': 'new_v', 'new_v_hg_norm_w': 'new_v', 'new_v_q_norm_w': 'new_v', 'new_v_k_norm_w': 'new_v', 'new_v_w_out': 'new_v', 'new_v_norm2_w': 'new_v', 'new_v_w_up': 'new_v', 'new_v_conv_w': 'new_v', 'new_v_conv_b': 'new_v', 'new_v_w_down': 'new_v'}


def _forward(args):
    return _fwd_reference(*[args[k] for k in FWD_PARAMS])


def _output_shape():
    out = _jax.eval_shape(lambda: _forward(_fwd_setup_inputs(0)))
    return out.shape, out.dtype

N_MICROBATCH = 1
ADAM_LR = 0.001
ADAM_B1 = 0.9
ADAM_B2 = 0.999
ADAM_EPS = 1e-08
ADAM_WD = 0.01
ADAM_STEP = 10
PER_EXAMPLE_BATCH_AXIS = {'x': 0, 'c': 0, 'loss_target': 0}
SHARED_INPUTS = []
_WEIGHT_DTYPES = {'w_ada': _jnp.float32, 'b_ada': _jnp.float32, 'norm1_w': _jnp.float32, 'w_in': _jnp.float32, 'lb_logits': _jnp.float32, 'hg_norm_w': _jnp.float32, 'q_norm_w': _jnp.float32, 'k_norm_w': _jnp.float32, 'w_out': _jnp.float32, 'norm2_w': _jnp.float32, 'w_up': _jnp.float32, 'conv_w': _jnp.float32, 'conv_b': _jnp.float32, 'w_down': _jnp.float32}
MOMENT_SCALE = {'w_ada': 1.555256e+00, 'b_ada': 3.259878e+00, 'norm1_w': 1.275015e+00, 'w_in': 7.361820e-01, 'lb_logits': 9.144759e-03, 'hg_norm_w': 1.692863e+01, 'q_norm_w': 6.432001e-01, 'k_norm_w': 6.443966e-01, 'w_out': 1.107693e+00, 'norm2_w': 7.404985e+00, 'w_up': 5.936083e-01, 'conv_w': 1.226745e+00, 'conv_b': 9.863663e-01, 'w_down': 4.099380e-01}


def _to_microbatches(a, axis):
    t = _jnp.moveaxis(a, axis, 0)
    t = t.reshape((N_MICROBATCH, t.shape[0] // N_MICROBATCH) + t.shape[1:])
    return _jnp.moveaxis(t, 1, axis + 1)


def setup_inputs(seed: int = 0) -> dict:
    inp = _fwd_setup_inputs(seed)
    key = _jax.random.fold_in(_jax.random.key(seed), 7919)
    shape, _ = _output_shape()
    out = dict(inp)
    out["loss_target"] = _jax.random.normal(_jax.random.fold_in(key, 0), shape, _jnp.float32)
    for i, name in enumerate(TWIN_WEIGHTS):
        w = inp[name].astype(_jnp.float32)
        if MOMENT_SCALE is None:
            s = _jnp.sqrt(_jnp.mean(_jnp.square(w)) + 1e-30)
        else:
            s = MOMENT_SCALE[name]
        km, kv = _jax.random.split(_jax.random.fold_in(key, i + 1))
        out[name] = w
        out["m_" + name] = s * _jax.random.normal(km, w.shape, _jnp.float32)
        out["v_" + name] = (s * s) * _jax.random.uniform(kv, w.shape, _jnp.float32, 0.5, 1.5)
    if N_MICROBATCH > 1:
        for name, axis in PER_EXAMPLE_BATCH_AXIS.items():
            out[name] = _to_microbatches(out[name], axis)
    return {'x': out['x'], 'c': out['c'], 'w_ada': out['w_ada'], 'b_ada': out['b_ada'], 'norm1_w': out['norm1_w'], 'w_in': out['w_in'], 'lb_logits': out['lb_logits'], 'hg_norm_w': out['hg_norm_w'], 'q_norm_w': out['q_norm_w'], 'k_norm_w': out['k_norm_w'], 'w_out': out['w_out'], 'norm2_w': out['norm2_w'], 'w_up': out['w_up'], 'conv_w': out['conv_w'], 'conv_b': out['conv_b'], 'w_down': out['w_down'], 'loss_target': out['loss_target'], 'm_w_ada': out['m_w_ada'], 'm_b_ada': out['m_b_ada'], 'm_norm1_w': out['m_norm1_w'], 'm_w_in': out['m_w_in'], 'm_lb_logits': out['m_lb_logits'], 'm_hg_norm_w': out['m_hg_norm_w'], 'm_q_norm_w': out['m_q_norm_w'], 'm_k_norm_w': out['m_k_norm_w'], 'm_w_out': out['m_w_out'], 'm_norm2_w': out['m_norm2_w'], 'm_w_up': out['m_w_up'], 'm_conv_w': out['m_conv_w'], 'm_conv_b': out['m_conv_b'], 'm_w_down': out['m_w_down'], 'v_w_ada': out['v_w_ada'], 'v_b_ada': out['v_b_ada'], 'v_norm1_w': out['v_norm1_w'], 'v_w_in': out['v_w_in'], 'v_lb_logits': out['v_lb_logits'], 'v_hg_norm_w': out['v_hg_norm_w'], 'v_q_norm_w': out['v_q_norm_w'], 'v_k_norm_w': out['v_k_norm_w'], 'v_w_out': out['v_w_out'], 'v_norm2_w': out['v_norm2_w'], 'v_w_up': out['v_w_up'], 'v_conv_w': out['v_conv_w'], 'v_conv_b': out['v_conv_b'], 'v_w_down': out['v_w_down']}


def _loss(weights, diff, rest, loss_target):
    with _jax.named_scope("forward"):
        args = {**rest, TWIN_DIFF_INPUT: diff, **{k: w.astype(_WEIGHT_DTYPES[k]) for k, w in weights.items()}}
        y = _forward(args)
    with _jax.named_scope("loss_head"):
        err = _jnp.square(y.astype(_jnp.float32) - loss_target)
        return 0.5 * _jnp.sum(_jnp.mean(err, axis=-1)) if err.ndim else 0.5 * err


def _adamw(w, g, m, v):
    m = ADAM_B1 * m + (1.0 - ADAM_B1) * g
    v = ADAM_B2 * v + (1.0 - ADAM_B2) * _jnp.square(g)
    m_hat = m / (1.0 - ADAM_B1 ** ADAM_STEP)
    v_hat = v / (1.0 - ADAM_B2 ** ADAM_STEP)
    delta = -ADAM_LR * (m_hat / (_jnp.sqrt(v_hat) + ADAM_EPS) + ADAM_WD * w)
    return delta, m, v


def reference(x, c, w_ada, b_ada, norm1_w, w_in, lb_logits, hg_norm_w, q_norm_w, k_norm_w, w_out, norm2_w, w_up, conv_w, conv_b, w_down, loss_target, m_w_ada, m_b_ada, m_norm1_w, m_w_in, m_lb_logits, m_hg_norm_w, m_q_norm_w, m_k_norm_w, m_w_out, m_norm2_w, m_w_up, m_conv_w, m_conv_b, m_w_down, v_w_ada, v_b_ada, v_norm1_w, v_w_in, v_lb_logits, v_hg_norm_w, v_q_norm_w, v_k_norm_w, v_w_out, v_norm2_w, v_w_up, v_conv_w, v_conv_b, v_w_down):
    given = dict(x=x, c=c, w_ada=w_ada, b_ada=b_ada, norm1_w=norm1_w, w_in=w_in, lb_logits=lb_logits, hg_norm_w=hg_norm_w, q_norm_w=q_norm_w, k_norm_w=k_norm_w, w_out=w_out, norm2_w=norm2_w, w_up=w_up, conv_w=conv_w, conv_b=conv_b, w_down=w_down, loss_target=loss_target, m_w_ada=m_w_ada, m_b_ada=m_b_ada, m_norm1_w=m_norm1_w, m_w_in=m_w_in, m_lb_logits=m_lb_logits, m_hg_norm_w=m_hg_norm_w, m_q_norm_w=m_q_norm_w, m_k_norm_w=m_k_norm_w, m_w_out=m_w_out, m_norm2_w=m_norm2_w, m_w_up=m_w_up, m_conv_w=m_conv_w, m_conv_b=m_conv_b, m_w_down=m_w_down, v_w_ada=v_w_ada, v_b_ada=v_b_ada, v_norm1_w=v_norm1_w, v_w_in=v_w_in, v_lb_logits=v_lb_logits, v_hg_norm_w=v_hg_norm_w, v_q_norm_w=v_q_norm_w, v_k_norm_w=v_k_norm_w, v_w_out=v_w_out, v_norm2_w=v_norm2_w, v_w_up=v_w_up, v_conv_w=v_conv_w, v_conv_b=v_conv_b, v_w_down=v_w_down)
    weights = {n: given[n] for n in TWIN_WEIGHTS}
    shared = {n: given[n] for n in SHARED_INPUTS}
    per_example = {n: given[n] for n in ['x', 'c']}
    grad_fn = _jax.value_and_grad(_loss, argnums=(0, 1))

    def one_microbatch(ex, loss_target):
        ex = dict(ex)
        diff = ex.pop(TWIN_DIFF_INPUT)
        return grad_fn(weights, diff, {**shared, **ex}, loss_target)

    if N_MICROBATCH == 1:
        loss, (grad_w, grad_x) = one_microbatch(per_example, given["loss_target"])
    else:
        def body(carry, xs):
            loss_sum, grad_sum = carry
            l_k, (gw_k, gx_k) = one_microbatch(xs[0], xs[1])
            with _jax.named_scope("update"):
                return (loss_sum + l_k, _jax.tree.map(_jnp.add, grad_sum, gw_k)), gx_k

        init = (_jnp.zeros((), _jnp.float32), _jax.tree.map(_jnp.zeros_like, weights))
        (loss, grad_w), grad_x = _jax.lax.scan(body, init, (per_example, given["loss_target"]))
    with _jax.named_scope("update"):
        delta_w, new_m, new_v = {}, {}, {}
        for n in TWIN_WEIGHTS:
            delta_w[n], new_m[n], new_v[n] = _adamw(weights[n], grad_w[n], given["m_" + n], given["v_" + n])
    return (loss, grad_x, *[grad_w[n] for n in TWIN_WEIGHTS], *[delta_w[n] for n in TWIN_WEIGHTS],
            *[new_m[n] for n in TWIN_WEIGHTS], *[new_v[n] for n in TWIN_WEIGHTS])
```

```python
import functools
import math

import jax
import jax.numpy as jnp
from jax import lax
from jax.experimental import pallas as pl
from jax.experimental.pallas import tpu as pltpu

F32 = jnp.float32
BF16 = jnp.bfloat16

S = 2048
D = 2048
H = 8
DH = 128
HW = H * DH
CH = 64
NCH = S // CH
DFF = 5632
INC = 7 * HW
BLK = 128
DILS = (1, 4, 16)
EPS = 1e-6
NEG = -1e30
N_CHIPS = 4
N_DEV = 8

ADAM_LR = 0.001
ADAM_B1 = 0.9
ADAM_B2 = 0.999
ADAM_EPS = 1e-08
ADAM_WD = 0.01
ADAM_STEP = 10
ADAM_BLOCK_BYTES = 1 << 20

V7X_VMEM_BYTES = 64 * 1024 * 1024
VMEM_LIMIT = (V7X_VMEM_BYTES * 3) // 4
MESH = pl.DeviceIdType.MESH
HBM_SPEC = pl.BlockSpec(memory_space=pltpu.HBM)
VMEM_SPEC = pl.BlockSpec(memory_space=pltpu.VMEM)

NN = (((1,), (0,)), ((), ()))
NT = (((1,), (1,)), ((), ()))
TN = (((0,), (0,)), ((), ()))


def _params(n_grid):
    return pltpu.CompilerParams(dimension_semantics=("arbitrary",) * n_grid, vmem_limit_bytes=VMEM_LIMIT)


def _dot(a, b, dims=NN):
    return lax.dot_general(a.astype(BF16), b.astype(BF16), dims, preferred_element_type=F32)


def _dot_f32(a, b):
    return lax.dot_general(a, b, NN, precision=lax.Precision.HIGHEST, preferred_element_type=F32)


def _sigmoid(x):
    return 1.0 / (1.0 + jnp.exp(-x))


def _pick(n, cands):
    for t in cands:
        if n % t == 0:
            return t
    raise ValueError(n)


def _matmul(a, b, mode, out_dtype, name):
    if mode == "nn":
        (m, k), (_, n) = a.shape, b.shape
    elif mode == "nt":
        (m, k), (n, _) = a.shape, b.shape
    else:
        (k, m), (_, n) = a.shape, b.shape
    tm = _pick(m, (1024, 512))
    tn = _pick(n, (1024, 512))
    tk = _pick(k, (1024, 512))
    nk = k // tk
    dims = {"nn": NN, "nt": NT, "tn": TN}[mode]

    def body(a_ref, b_ref, o_ref, acc_ref):
        kk = pl.program_id(2)

        @pl.when(kk == 0)
        def _():
            acc_ref[...] = jnp.zeros_like(acc_ref)

        acc_ref[...] += lax.dot_general(a_ref[...], b_ref[...], dims, preferred_element_type=F32)

        @pl.when(kk == nk - 1)
        def _():
            o_ref[...] = acc_ref[...].astype(o_ref.dtype)

    if mode == "tn":
        a_spec = pl.BlockSpec((tk, tm), lambda i, j, kk: (kk, i))
    else:
        a_spec = pl.BlockSpec((tm, tk), lambda i, j, kk: (i, kk))
    if mode == "nt":
        b_spec = pl.BlockSpec((tn, tk), lambda i, j, kk: (j, kk))
    else:
        b_spec = pl.BlockSpec((tk, tn), lambda i, j, kk: (kk, j))
    return pl.pallas_call(
        body, name=name,
        out_shape=jax.ShapeDtypeStruct((m, n), out_dtype),
        grid=(m // tm, n // tn, nk),
        in_specs=[a_spec, b_spec],
        out_specs=pl.BlockSpec((tm, tn), lambda i, j, kk: (i, j)),
        scratch_shapes=[pltpu.VMEM((tm, tn), F32)],
        compiler_params=_params(3),
    )(a, b)


TR = 256


def _row_spec(cols=D):
    return pl.BlockSpec((TR, cols), lambda i: (i, 0))


def _vec_spec(cols=D):
    return pl.BlockSpec((1, cols), lambda i: (0, 0))


def _norm_mod(x, nw, scale, shift, name):
    def body(x_ref, nw_ref, sc_ref, sh_ref, h_ref):
        xv = x_ref[...]
        r = lax.rsqrt(jnp.mean(xv * xv, axis=-1, keepdims=True) + EPS)
        h_ref[...] = ((xv * r) * nw_ref[...] * (1.0 + sc_ref[...]) + sh_ref[...]).astype(BF16)

    return pl.pallas_call(
        body, name=name, out_shape=jax.ShapeDtypeStruct((S, D), BF16), grid=(S // TR,),
        in_specs=[_row_spec(), _vec_spec(), _vec_spec(), _vec_spec()], out_specs=_row_spec(),
        compiler_params=_params(1),
    )(x, nw, scale, shift)


def _resid_norm_mod(x, mix, gate, nw, scale, shift, name):
    def body(x_ref, mix_ref, g_ref, nw_ref, sc_ref, sh_ref, x1_ref, h_ref):
        xv = x_ref[...] + g_ref[...] * mix_ref[...]
        x1_ref[...] = xv
        r = lax.rsqrt(jnp.mean(xv * xv, axis=-1, keepdims=True) + EPS)
        h_ref[...] = ((xv * r) * nw_ref[...] * (1.0 + sc_ref[...]) + sh_ref[...]).astype(BF16)

    return pl.pallas_call(
        body, name=name,
        out_shape=(jax.ShapeDtypeStruct((S, D), F32), jax.ShapeDtypeStruct((S, D), BF16)), grid=(S // TR,),
        in_specs=[_row_spec(), _row_spec(), _vec_spec(), _vec_spec(), _vec_spec(), _vec_spec()],
        out_specs=(_row_spec(), _row_spec()),
        compiler_params=_params(1),
    )(x, mix, gate, nw, scale, shift)


def _norm_mod_bwd(dh, xin, nw, scale, dres, name, mix=None, gate=None):
    with_gate = mix is not None

    def body(*refs):
        if with_gate:
            dh_ref, x_ref, nw_ref, sc_ref, dres_ref, mix_ref, g_ref, dx_ref, dsh_ref, dsc_ref, dnw_ref, dg_ref, dmix_ref = refs
        else:
            dh_ref, x_ref, nw_ref, sc_ref, dres_ref, dx_ref, dsh_ref, dsc_ref, dnw_ref = refs
        i = pl.program_id(0)
        dhv = dh_ref[...]
        xv = x_ref[...]
        r = lax.rsqrt(jnp.mean(xv * xv, axis=-1, keepdims=True) + EPS)
        xn = xv * r
        nwv = nw_ref[...]
        one_sc = 1.0 + sc_ref[...]
        dxn = dhv * nwv * one_sc
        dx = dres_ref[...] + r * (dxn - xn * jnp.mean(dxn * xn, axis=-1, keepdims=True))
        dx_ref[...] = dx

        @pl.when(i == 0)
        def _():
            dsh_ref[...] = jnp.zeros_like(dsh_ref)
            dsc_ref[...] = jnp.zeros_like(dsc_ref)
            dnw_ref[...] = jnp.zeros_like(dnw_ref)
            if with_gate:
                dg_ref[...] = jnp.zeros_like(dg_ref)

        dsh_ref[...] += jnp.sum(dhv, axis=0, keepdims=True)
        dsc_ref[...] += jnp.sum(dhv * xn * nwv, axis=0, keepdims=True)
        dnw_ref[...] += jnp.sum(dhv * xn * one_sc, axis=0, keepdims=True)
        if with_gate:
            dg_ref[...] += jnp.sum(dx * mix_ref[...], axis=0, keepdims=True)
            dmix_ref[...] = (dx * g_ref[...]).astype(BF16)

    vec = jax.ShapeDtypeStruct((1, D), F32)
    ins = [dh, xin, nw, scale, dres]
    in_specs = [_row_spec(), _row_spec(), _vec_spec(), _vec_spec(), _row_spec()]
    outs = [jax.ShapeDtypeStruct((S, D), F32), vec, vec, vec]
    out_specs = [_row_spec(), _vec_spec(), _vec_spec(), _vec_spec()]
    if with_gate:
        ins += [mix, gate]
        in_specs += [_row_spec(), _vec_spec()]
        outs += [vec, jax.ShapeDtypeStruct((S, D), BF16)]
        out_specs += [_vec_spec(), _row_spec()]
    return pl.pallas_call(
        body, name=name, out_shape=tuple(outs), grid=(S // TR,), in_specs=in_specs, out_specs=tuple(out_specs),
        compiler_params=_params(1),
    )(*ins)


def _tri(lower):
    row = lax.broadcasted_iota(jnp.int32, (CH, CH), 0)
    col = lax.broadcasted_iota(jnp.int32, (CH, CH), 1)
    return (row >= col) if lower else (col >= row)


def _hgrn_gates(hq, hf, lb):
    sig = _sigmoid(hf)
    f = lb + (1.0 - lb) * sig
    g = jnp.log(f)
    sq = _sigmoid(hq)
    return sig, f, g, 1.0 - f, hq * sq, sq


def _proj_col_spec(group):
    return pl.BlockSpec((S, DH), lambda h: (0, group * H + h))


def _hgrn_fwd(proj, lb_logits, norm_w):
    def body(hq_ref, hf_ref, hi_ref, hg_ref, lbl_ref, nw_ref, out_ref, oraw_ref, st_ref, s_ref):
        lb = 1.0 / (1.0 + jnp.exp(lbl_ref[1:2, :] - lbl_ref[0:1, :]))
        nw = nw_ref[...]
        lower = _tri(True)
        ltri = lower.astype(F32)
        s_ref[...] = jnp.zeros_like(s_ref)

        def chunk(n, carry):
            rows = pl.ds(pl.multiple_of(n * CH, CH), CH)
            hq, hf, v, hg = hq_ref[rows, :], hf_ref[rows, :], hi_ref[rows, :], hg_ref[rows, :]
            _, _, g, kk, q, _ = _hgrn_gates(hq, hf, lb)
            gc = _dot_f32(ltri, g)
            gl = jnp.sum(g, axis=0, keepdims=True)
            qe = q * jnp.exp(gc)
            ke = kk * jnp.exp(gl - gc)
            qh = q * jnp.exp(gc - 0.5 * gl)
            kh = kk * jnp.exp(0.5 * gl - gc)
            st = s_ref[...]
            st_ref[0, pl.ds(n, 1)] = st[None]
            a = jnp.where(lower, _dot(qh, kh, NT), 0.0)
            o = _dot(qe, st, NT) + _dot(a, v)
            s_ref[...] = st * jnp.exp(gl) + _dot(v, ke, TN)
            oraw_ref[rows, :] = o
            rs = lax.rsqrt(jnp.mean(o * o, axis=-1, keepdims=True) + EPS)
            out_ref[rows, :] = (o * rs * nw * (hg * _sigmoid(hg))).astype(BF16)
            return carry

        lax.fori_loop(0, NCH, chunk, 0)

    head_spec = pl.BlockSpec((S, DH), lambda h: (0, h))
    return pl.pallas_call(
        body, name="hgrn_fwd",
        out_shape=(jax.ShapeDtypeStruct((S, HW), BF16), jax.ShapeDtypeStruct((S, HW), F32),
                   jax.ShapeDtypeStruct((H, NCH, DH, DH), F32)),
        grid=(H,),
        in_specs=[_proj_col_spec(0), _proj_col_spec(1), _proj_col_spec(2), _proj_col_spec(3),
                  pl.BlockSpec((2, DH), lambda h: (0, h)), pl.BlockSpec((1, DH), lambda h: (0, 0))],
        out_specs=(head_spec, head_spec, pl.BlockSpec((1, NCH, DH, DH), lambda h: (h, 0, 0, 0))),
        scratch_shapes=[pltpu.VMEM((DH, DH), F32)],
        compiler_params=_params(1),
    )(proj, proj, proj, proj, lb_logits, norm_w)


def _hgrn_bwd(proj, lb_logits, norm_w, oraw, states, dout):
    def body(hq_ref, hf_ref, hi_ref, hg_ref, lbl_ref, nw_ref, oraw_ref, st_ref, do_ref,
             dhq_ref, dhf_ref, dhi_ref, dhg_ref, dlb_ref, dnw_ref, ds_ref, acc_ref):
        h = pl.program_id(0)
        lb = 1.0 / (1.0 + jnp.exp(lbl_ref[1:2, :] - lbl_ref[0:1, :]))
        nw = nw_ref[...]
        lower = _tri(True)
        ltri = lower.astype(F32)
        utri = _tri(False).astype(F32)
        last = lax.broadcasted_iota(jnp.int32, (CH, DH), 0) == CH - 1
        ds_ref[...] = jnp.zeros_like(ds_ref)
        acc_ref[...] = jnp.zeros_like(acc_ref)

        @pl.when(h == 0)
        def _():
            dnw_ref[...] = jnp.zeros_like(dnw_ref)

        def chunk(i, carry):
            n = NCH - 1 - i
            rows = pl.ds(pl.multiple_of(n * CH, CH), CH)
            hq, hf, v, hg = hq_ref[rows, :], hf_ref[rows, :], hi_ref[rows, :], hg_ref[rows, :]
            sig, f, g, kk, q, sq = _hgrn_gates(hq, hf, lb)
            gc = _dot_f32(ltri, g)
            gl = jnp.sum(g, axis=0, keepdims=True)
            eg = jnp.exp(gc)
            ek = jnp.exp(gl - gc)
            gam = jnp.exp(gl)
            ph = jnp.exp(gc - 0.5 * gl)
            pk = jnp.exp(0.5 * gl - gc)
            qe, ke = q * eg, kk * ek
            qh, kh = q * ph, kk * pk
            st = st_ref[0, pl.ds(n, 1)][0]
            dst = ds_ref[...]
            a = jnp.where(lower, _dot(qh, kh, NT), 0.0)
            o = oraw_ref[rows, :]
            rs = lax.rsqrt(jnp.mean(o * o, axis=-1, keepdims=True) + EPS)
            xh = o * rs
            sg = _sigmoid(hg)
            dgo = do_ref[rows, :]
            d_on = dgo * (hg * sg)
            dhg_ref[rows, :] = (dgo * xh * nw * (sg * (1.0 + hg * (1.0 - sg)))).astype(BF16)
            acc_ref[1:2, :] += jnp.sum(d_on * xh, axis=0, keepdims=True)
            dy = d_on * nw
            do = rs * (dy - xh * jnp.mean(dy * xh, axis=-1, keepdims=True))
            dqe = _dot(do, st)
            da = jnp.where(lower, _dot(do, v, NT), 0.0)
            dv = _dot(a, do, TN) + _dot(ke, dst, NT)
            dke = _dot(v, dst)
            dgam = jnp.sum(dst * st, axis=0, keepdims=True)
            dqh = _dot(da, kh)
            dkh = _dot(da, qh, TN)
            dq = dqh * ph + dqe * eg
            dk = dkh * pk + dke * ek
            dgl = jnp.sum(dke * ke, axis=0, keepdims=True) + dgam * gam
            dgc = dqh * qh - dkh * kh + dqe * qe - dke * ke + jnp.where(last, dgl, 0.0)
            dg = _dot_f32(utri, dgc)
            df = dg / f - dk
            dhf_ref[rows, :] = (df * (1.0 - lb) * sig * (1.0 - sig)).astype(BF16)
            acc_ref[0:1, :] += jnp.sum(df * (1.0 - sig), axis=0, keepdims=True)
            dhq_ref[rows, :] = (dq * (sq * (1.0 + hq * (1.0 - sq)))).astype(BF16)
            dhi_ref[rows, :] = dv.astype(BF16)
            ds_ref[...] = dst * gam + _dot(do, qe, TN)
            return carry

        lax.fori_loop(0, NCH, chunk, 0)
        dlb_ref[...] = acc_ref[0:1, :]
        dnw_ref[...] += acc_ref[1:2, :]

    head_spec = pl.BlockSpec((S, DH), lambda h: (0, h))
    head_out = jax.ShapeDtypeStruct((S, HW), BF16)
    return pl.pallas_call(
        body, name="hgrn_bwd",
        out_shape=(head_out, head_out, head_out, head_out,
                   jax.ShapeDtypeStruct((1, HW), F32), jax.ShapeDtypeStruct((1, DH), F32)),
        grid=(H,),
        in_specs=[_proj_col_spec(0), _proj_col_spec(1), _proj_col_spec(2), _proj_col_spec(3),
                  pl.BlockSpec((2, DH), lambda h: (0, h)), pl.BlockSpec((1, DH), lambda h: (0, 0)),
                  head_spec, pl.BlockSpec((1, NCH, DH, DH), lambda h: (h, 0, 0, 0)), head_spec],
        out_specs=(head_spec, head_spec, head_spec, head_spec,
                   pl.BlockSpec((1, DH), lambda h: (0, h)), pl.BlockSpec((1, DH), lambda h: (0, 0))),
        scratch_shapes=[pltpu.VMEM((DH, DH), F32), pltpu.VMEM((8, DH), F32)],
        compiler_params=_params(1),
    )(proj, proj, proj, proj, lb_logits, norm_w, oraw, states, dout)


ATT_SCALE = DH ** -0.5


def _rms_rows(x, w):
    rs = lax.rsqrt(jnp.mean(x * x, axis=-1, keepdims=True) + EPS)
    return x * rs, rs


def _att_masks():
    qi = lax.broadcasted_iota(jnp.int32, (BLK, BLK), 0)
    kj = lax.broadcasted_iota(jnp.int32, (BLK, BLK), 1)
    steps_c = qi - kj
    steps_p = qi - kj + BLK
    return steps_c, steps_p, steps_c >= 0, steps_p <= BLK


def _slope(h, dil):
    return float(2.0 ** (-(h + 1))) * dil


def _attn_fwd(proj, q_w, k_w, dil):
    length = S // dil
    nblk = length // BLK
    pv = proj.reshape(length, dil * INC)

    def body(q_ref, kc_ref, kp_ref, vc_ref, vp_ref, qw_ref, kw_ref, o_ref, lse_ref):
        n = pl.program_id(1)
        steps_c, steps_p, ok_c, ok_p = _att_masks()
        ok_p = jnp.logical_and(ok_p, n > 0)
        fc, fp = steps_c.astype(F32), steps_p.astype(F32)
        for h in range(H):
            cols = slice(h * DH, (h + 1) * DH)
            qn = _rms_rows(q_ref[:, cols], None)[0] * qw_ref[...]
            kcn = _rms_rows(kc_ref[:, cols], None)[0] * kw_ref[...]
            kpn = _rms_rows(kp_ref[:, cols], None)[0] * kw_ref[...]
            sc = jnp.where(ok_c, _dot(qn, kcn, NT) * ATT_SCALE - _slope(h, dil) * fc, NEG)
            sp = jnp.where(ok_p, _dot(qn, kpn, NT) * ATT_SCALE - _slope(h, dil) * fp, NEG)
            m = jnp.maximum(jnp.max(sc, axis=-1, keepdims=True), jnp.max(sp, axis=-1, keepdims=True))
            pc, pp = jnp.exp(sc - m), jnp.exp(sp - m)
            den = jnp.sum(pc, axis=-1, keepdims=True) + jnp.sum(pp, axis=-1, keepdims=True)
            o_ref[:, cols] = (_dot(pc, vc_ref[:, cols]) + _dot(pp, vp_ref[:, cols])) / den
            lse_ref[:, cols] = jnp.broadcast_to(m + jnp.log(den), (BLK, DH))

    def spec(group, prev):
        if prev:
            return pl.BlockSpec((BLK, HW), lambda r, n: (jnp.maximum(n - 1, 0), r * 7 + group))
        return pl.BlockSpec((BLK, HW), lambda r, n: (n, r * 7 + group))

    out = jax.ShapeDtypeStruct((length, dil * HW), F32)
    out_spec = pl.BlockSpec((BLK, HW), lambda r, n: (n, r))
    wspec = pl.BlockSpec((1, DH), lambda r, n: (0, 0))
    o, lse = pl.pallas_call(
        body, name=f"attn_fwd_d{dil}", out_shape=(out, out), grid=(dil, nblk),
        in_specs=[spec(4, False), spec(5, False), spec(5, True), spec(6, False), spec(6, True), wspec, wspec],
        out_specs=(out_spec, out_spec),
        compiler_params=_params(2),
    )(pv, pv, pv, pv, pv, q_w, k_w)
    return o.reshape(S, HW), lse.reshape(S, HW)


def _attn_merge(outs, lses):
    def body(o1, o2, o3, l1, l2, l3, ob_ref, of_ref, lse_ref):
        a, b, c = l1[...], l2[...], l3[...]
        m = jnp.maximum(jnp.maximum(a, b), c)
        ea, eb, ec = jnp.exp(a - m), jnp.exp(b - m), jnp.exp(c - m)
        den = ea + eb + ec
        o = (ea * o1[...] + eb * o2[...] + ec * o3[...]) / den
        ob_ref[...] = o.astype(BF16)
        of_ref[...] = o
        lse_ref[...] = m + jnp.log(den)

    f = jax.ShapeDtypeStruct((S, HW), F32)
    return pl.pallas_call(
        body, name="attn_merge", out_shape=(jax.ShapeDtypeStruct((S, HW), BF16), f, f), grid=(S // TR,),
        in_specs=[_row_spec(HW)] * 6, out_specs=(_row_spec(HW),) * 3,
        compiler_params=_params(1),
    )(*outs, *lses)


def _attn_bwd(proj, q_w, k_w, o, lse, do, dil, acc):
    length = S // dil
    nblk = length // BLK
    pv = proj.reshape(length, dil * INC)
    view = lambda t: t.reshape(length, dil * HW)
    has_acc = acc is not None

    def body(*refs):
        (q_ref, qn_ref, kp_ref, kc_ref, vp_ref, vc_ref, o_ref, on_ref, l_ref, ln_ref, do_ref, don_ref,
         qw_ref, kw_ref) = refs[:14]
        refs = refs[14:]
        if has_acc:
            aq_ref, ak_ref, av_ref, aqw_ref, akw_ref = refs[:5]
            refs = refs[5:]
        dq_ref, dk_ref, dv_ref, dqw_ref, dkw_ref = refs
        r, m = pl.program_id(0), pl.program_id(1)
        steps_c, steps_p, ok_c, ok_p = _att_masks()
        ok_prev = jnp.logical_and(ok_p, m > 0)
        ok_next = jnp.logical_and(ok_p, m < nblk - 1)
        fc, fp = steps_c.astype(F32), steps_p.astype(F32)
        qw, kw = qw_ref[...], kw_ref[...]

        @pl.when(jnp.logical_and(r == 0, m == 0))
        def _():
            if has_acc:
                dqw_ref[...] = aqw_ref[...]
                dkw_ref[...] = akw_ref[...]
            else:
                dqw_ref[...] = jnp.zeros_like(dqw_ref)
                dkw_ref[...] = jnp.zeros_like(dkw_ref)

        for h in range(H):
            cols = slice(h * DH, (h + 1) * DH)
            sl = _slope(h, dil)
            qx, qrs = _rms_rows(q_ref[:, cols], None)
            kx, krs = _rms_rows(kc_ref[:, cols], None)
            qn, kcn = qx * qw, kx * kw
            qnn = _rms_rows(qn_ref[:, cols], None)[0] * qw
            kpn = _rms_rows(kp_ref[:, cols], None)[0] * kw
            vc, vp = vc_ref[:, cols], vp_ref[:, cols]
            dov, donv = do_ref[:, cols], don_ref[:, cols]
            lse_m, lse_n = l_ref[:, cols][:, 0:1], ln_ref[:, cols][:, 0:1]
            delta_m = jnp.sum(dov * o_ref[:, cols], axis=-1, keepdims=True)
            delta_n = jnp.sum(donv * on_ref[:, cols], axis=-1, keepdims=True)
            p_c = jnp.where(ok_c, jnp.exp(_dot(qn, kcn, NT) * ATT_SCALE - sl * fc - lse_m), 0.0)
            p_p = jnp.where(ok_prev, jnp.exp(_dot(qn, kpn, NT) * ATT_SCALE - sl * fp - lse_m), 0.0)
            p_n = jnp.where(ok_next, jnp.exp(_dot(qnn, kcn, NT) * ATT_SCALE - sl * fp - lse_n), 0.0)
            ds_c = p_c * (_dot(dov, vc, NT) - delta_m)
            ds_p = p_p * (_dot(dov, vp, NT) - delta_m)
            ds_n = p_n * (_dot(donv, vc, NT) - delta_n)
            dqn = (_dot(ds_c, kcn) + _dot(ds_p, kpn)) * ATT_SCALE
            dkn = (_dot(ds_c, qn, TN) + _dot(ds_n, qnn, TN)) * ATT_SCALE
            dv = _dot(p_c, dov, TN) + _dot(p_n, donv, TN)
            dqw_ref[...] += jnp.sum(dqn * qx, axis=0, keepdims=True)
            dkw_ref[...] += jnp.sum(dkn * kx, axis=0, keepdims=True)
            dyq, dyk = dqn * qw, dkn * kw
            daq = qrs * (dyq - qx * jnp.mean(dyq * qx, axis=-1, keepdims=True))
            dak = krs * (dyk - kx * jnp.mean(dyk * kx, axis=-1, keepdims=True))
            if has_acc:
                daq, dak, dv = daq + aq_ref[:, cols], dak + ak_ref[:, cols], dv + av_ref[:, cols]
            dq_ref[:, cols] = daq
            dk_ref[:, cols] = dak
            dv_ref[:, cols] = dv

    def pspec(group, shift):
        if shift < 0:
            return pl.BlockSpec((BLK, HW), lambda r, m: (jnp.maximum(m - 1, 0), r * 7 + group))
        if shift > 0:
            return pl.BlockSpec((BLK, HW), lambda r, m: (jnp.minimum(m + 1, nblk - 1), r * 7 + group))
        return pl.BlockSpec((BLK, HW), lambda r, m: (m, r * 7 + group))

    cur = pl.BlockSpec((BLK, HW), lambda r, m: (m, r))
    nxt = pl.BlockSpec((BLK, HW), lambda r, m: (jnp.minimum(m + 1, nblk - 1), r))
    wspec = pl.BlockSpec((1, DH), lambda r, m: (0, 0))
    ins = [pv, pv, pv, pv, pv, pv, view(o), view(o), view(lse), view(lse), view(do), view(do), q_w, k_w]
    in_specs = [pspec(4, 0), pspec(4, 1), pspec(5, -1), pspec(5, 0), pspec(6, -1), pspec(6, 0),
                cur, nxt, cur, nxt, cur, nxt, wspec, wspec]
    if has_acc:
        ins += [view(acc[0]), view(acc[1]), view(acc[2]), acc[3], acc[4]]
        in_specs += [cur, cur, cur, wspec, wspec]
    big = jax.ShapeDtypeStruct((length, dil * HW), F32)
    small = jax.ShapeDtypeStruct((1, DH), F32)
    dq, dk, dv, dqw, dkw = pl.pallas_call(
        body, name=f"attn_bwd_d{dil}", out_shape=(big, big, big, small, small), grid=(dil, nblk),
        in_specs=in_specs, out_specs=(cur, cur, cur, wspec, wspec),
        compiler_params=_params(2),
    )(*ins)
    return dq.reshape(S, HW), dk.reshape(S, HW), dv.reshape(S, HW), dqw, dkw


TC = 512
NCT = DFF // TC


def _shift_rows(a, k):
    rows = lax.broadcasted_iota(jnp.int32, a.shape, 0)
    rolled = pltpu.roll(a, k % S, 0)
    if k > 0:
        return jnp.where(rows >= k, rolled, 0.0)
    return jnp.where(rows < S + k, rolled, 0.0)


def _conv_pre(a, w_ref, b_ref):
    return b_ref[...] + w_ref[0:1, :] * _shift_rows(a, 2) + w_ref[1:2, :] * _shift_rows(a, 1) + w_ref[2:3, :] * a


def _conv_gate_fwd(u, conv_w, conv_b):
    def body(a_ref, g_ref, w_ref, b_ref, y_ref):
        pre = _conv_pre(a_ref[...], w_ref, b_ref)
        y_ref[...] = (pre * _sigmoid(pre) * g_ref[...]).astype(BF16)

    return pl.pallas_call(
        body, name="conv_gate_fwd", out_shape=jax.ShapeDtypeStruct((S, DFF), BF16), grid=(NCT,),
        in_specs=[pl.BlockSpec((S, TC), lambda i: (0, i)), pl.BlockSpec((S, TC), lambda i: (0, NCT + i)),
                  pl.BlockSpec((3, TC), lambda i: (0, i)), pl.BlockSpec((1, TC), lambda i: (0, i))],
        out_specs=pl.BlockSpec((S, TC), lambda i: (0, i)),
        compiler_params=_params(1),
    )(u, u, conv_w, conv_b)


def _conv_gate_bwd(dy, u, conv_w, conv_b):
    def body(dy_ref, a_ref, g_ref, w_ref, b_ref, da_ref, dg_ref, db_ref, dw_ref):
        a = a_ref[...]
        dyv = dy_ref[...]
        pre = _conv_pre(a, w_ref, b_ref)
        sg = _sigmoid(pre)
        dg_ref[...] = (dyv * pre * sg).astype(BF16)
        dpre = dyv * g_ref[...] * (sg * (1.0 + pre * (1.0 - sg)))
        da = w_ref[2:3, :] * dpre + w_ref[1:2, :] * _shift_rows(dpre, -1) + w_ref[0:1, :] * _shift_rows(dpre, -2)
        da_ref[...] = da.astype(BF16)
        db_ref[...] = jnp.sum(dpre, axis=0, keepdims=True)
        dw_ref[0:1, :] = jnp.sum(dpre * _shift_rows(a, 2), axis=0, keepdims=True)
        dw_ref[1:2, :] = jnp.sum(dpre * _shift_rows(a, 1), axis=0, keepdims=True)
        dw_ref[2:3, :] = jnp.sum(dpre * a, axis=0, keepdims=True)

    col = pl.BlockSpec((S, TC), lambda i: (0, i))
    half = jax.ShapeDtypeStruct((S, DFF), BF16)
    return pl.pallas_call(
        body, name="conv_gate_bwd",
        out_shape=(half, half, jax.ShapeDtypeStruct((1, DFF), F32), jax.ShapeDtypeStruct((3, DFF), F32)),
        grid=(NCT,),
        in_specs=[col, col, pl.BlockSpec((S, TC), lambda i: (0, NCT + i)),
                  pl.BlockSpec((3, TC), lambda i: (0, i)), pl.BlockSpec((1, TC), lambda i: (0, i))],
        out_specs=(col, col, pl.BlockSpec((1, TC), lambda i: (0, i)), pl.BlockSpec((3, TC), lambda i: (0, i))),
        compiler_params=_params(1),
    )(dy, u, u, conv_w, conv_b)


def _out_loss(z, x1, target, gate):
    def body(z_ref, x_ref, t_ref, g_ref, do_ref, dz_ref, dg_ref, loss_ref):
        i = pl.program_id(0)
        zv = z_ref[...]
        gv = g_ref[...]
        err = x_ref[...] + gv * zv - t_ref[...]
        dout = err * (1.0 / D)
        do_ref[...] = dout
        dz_ref[...] = (dout * gv).astype(BF16)

        @pl.when(i == 0)
        def _():
            dg_ref[...] = jnp.zeros_like(dg_ref)
            loss_ref[...] = jnp.zeros_like(loss_ref)

        dg_ref[...] += jnp.sum(dout * zv, axis=0, keepdims=True)
        part = jnp.sum(jnp.sum(err * err, axis=0, keepdims=True), axis=-1, keepdims=True) * (0.5 / D)
        lane = lax.broadcasted_iota(jnp.int32, (1, 128), 1)
        loss_ref[...] += jnp.where(lane == 0, part, 0.0)

    return pl.pallas_call(
        body, name="out_loss",
        out_shape=(jax.ShapeDtypeStruct((S, D), F32), jax.ShapeDtypeStruct((S, D), BF16),
                   jax.ShapeDtypeStruct((1, D), F32), jax.ShapeDtypeStruct((1, 128), F32)),
        grid=(S // TR,),
        in_specs=[_row_spec(), _row_spec(), _row_spec(), _vec_spec()],
        out_specs=(_row_spec(), _row_spec(), _vec_spec(), _vec_spec(128)),
        compiler_params=_params(1),
    )(z, x1, target, gate)


ADA_COLS = 6 * D // N_CHIPS
TA = 512


def _ada_fwd(c_all, w_shard, b_shard):
    def body(c_ref, w_ref, b_ref, o_ref):
        cv = c_ref[...]
        o_ref[...] = _dot_f32(cv * _sigmoid(cv), w_ref[...]) + b_ref[...]

    return pl.pallas_call(
        body, name="ada_fwd", out_shape=jax.ShapeDtypeStruct((N_DEV, ADA_COLS), F32), grid=(ADA_COLS // TA,),
        in_specs=[pl.BlockSpec((N_DEV, D), lambda j: (0, 0)), pl.BlockSpec((D, TA), lambda j: (0, j)),
                  pl.BlockSpec((1, TA), lambda j: (0, j))],
        out_specs=pl.BlockSpec((N_DEV, TA), lambda j: (0, j)),
        compiler_params=_params(1),
    )(c_all, w_shard, b_shard)


def _ada_bwd(c_all, dmod_shard):
    def body(c_ref, d_ref, o_ref):
        cv = c_ref[...]
        o_ref[...] = lax.dot_general(cv * _sigmoid(cv), d_ref[...], TN, precision=lax.Precision.HIGHEST,
                                     preferred_element_type=F32)

    return pl.pallas_call(
        body, name="ada_bwd", out_shape=jax.ShapeDtypeStruct((D, ADA_COLS), F32), grid=(ADA_COLS // TA,),
        in_specs=[pl.BlockSpec((N_DEV, D), lambda j: (0, 0)), pl.BlockSpec((N_DEV, TA), lambda j: (0, j))],
        out_specs=pl.BlockSpec((D, TA), lambda j: (0, j)),
        compiler_params=_params(1),
    )(c_all, dmod_shard)


def _sum_rows(g, name):
    rows, n = g.shape

    def body(g_ref, o_ref):
        acc = g_ref[0:1, :]
        for i in range(1, rows):
            acc = acc + g_ref[i:i + 1, :]
        o_ref[...] = acc

    return pl.pallas_call(
        body, name=name, out_shape=jax.ShapeDtypeStruct((1, n), F32),
        in_specs=[VMEM_SPEC], out_specs=VMEM_SPEC,
    )(g)


def _lb_grad(lb_logits, dlb):
    def body(l_ref, d_ref, o_ref):
        p = 1.0 / (1.0 + jnp.exp(l_ref[1:2, :] - l_ref[0:1, :]))
        t = d_ref[...] * p * (1.0 - p)
        o_ref[0:1, :] = t
        o_ref[1:2, :] = -t

    return pl.pallas_call(
        body, name="lb_grad", out_shape=jax.ShapeDtypeStruct((2, HW), F32),
        in_specs=[VMEM_SPEC, VMEM_SPEC], out_specs=VMEM_SPEC,
    )(lb_logits, dlb)


def _adamw(w, g, m, v, name):
    rows, cols = w.shape
    tr = rows
    if rows * cols * 4 > ADAM_BLOCK_BYTES:
        tr = _pick(rows, [t for t in (256, 128, 64, 32, 16, 8) if t * cols * 4 <= ADAM_BLOCK_BYTES])

    def body(w_ref, g_ref, m_ref, v_ref, d_ref, mo_ref, vo_ref):
        gv = g_ref[...]
        m2 = ADAM_B1 * m_ref[...] + (1.0 - ADAM_B1) * gv
        v2 = ADAM_B2 * v_ref[...] + (1.0 - ADAM_B2) * (gv * gv)
        m_hat = m2 / (1.0 - ADAM_B1 ** ADAM_STEP)
        v_hat = v2 / (1.0 - ADAM_B2 ** ADAM_STEP)
        d_ref[...] = -ADAM_LR * (m_hat / (jnp.sqrt(v_hat) + ADAM_EPS) + ADAM_WD * w_ref[...])
        mo_ref[...] = m2
        vo_ref[...] = v2

    spec = pl.BlockSpec((tr, cols), lambda i: (i, 0))
    out = jax.ShapeDtypeStruct((rows, cols), F32)
    return pl.pallas_call(
        body, name=name, out_shape=(out, out, out), grid=(rows // tr,),
        in_specs=[spec] * 4, out_specs=(spec,) * 3,
        compiler_params=_params(1),
    )(w, g, m, v)


def _add_pair(a, b, name):
    n, r, c = a.shape
    tr = _pick(r, (256, 176, 128, 64, 32, 16))

    def body(a_ref, b_ref, o_ref):
        o_ref[...] = (a_ref[...].astype(F32) + b_ref[...].astype(F32)).astype(BF16)

    spec = pl.BlockSpec((1, tr, c), lambda j, i: (j, i, 0))
    return pl.pallas_call(
        body, name=name, out_shape=jax.ShapeDtypeStruct(a.shape, BF16), grid=(n, r // tr),
        in_specs=[spec, spec], out_specs=spec, compiler_params=_params(2),
    )(a, b)


def _add_four(parts, name):
    _, r, c = parts.shape
    tr = _pick(r, (256, 176, 128, 64, 32, 16))

    def body(p_ref, o_ref):
        o_ref[...] = ((p_ref[0].astype(F32) + p_ref[1].astype(F32)) + p_ref[2].astype(F32)) + p_ref[3].astype(F32)

    return pl.pallas_call(
        body, name=name, out_shape=jax.ShapeDtypeStruct((r, c), F32), grid=(r // tr,),
        in_specs=[pl.BlockSpec((4, tr, c), lambda i: (0, i, 0))], out_specs=pl.BlockSpec((tr, c), lambda i: (i, 0)),
        compiler_params=_params(1),
    )(parts)


def _sequence_step(x, target, mod, norm1_w, w_in, lb_logits, hg_norm_w, q_norm_w, k_norm_w, w_out,
                   norm2_w, w_up, conv_w, conv_b, w_down):
    shift1, scale1, gate1, shift2, scale2, gate2 = [mod[:, i * D:(i + 1) * D] for i in range(6)]

    h = _norm_mod(x, norm1_w, scale1, shift1, "norm1_fwd")
    proj = _matmul(h, w_in, "nn", F32, "proj_fwd")
    a_out, o_raw, states = _hgrn_fwd(proj, lb_logits, hg_norm_w)
    att = [_attn_fwd(proj, q_norm_w, k_norm_w, dil) for dil in DILS]
    b_out, b_out_f32, lse = _attn_merge([t[0] for t in att], [t[1] for t in att])
    cat = jnp.concatenate([a_out, b_out], axis=1)
    mix = _matmul(cat, w_out, "nn", F32, "mix_fwd")
    x1, h2 = _resid_norm_mod(x, mix, gate1, norm2_w, scale2, shift2, "norm2_fwd")
    u = _matmul(h2, w_up, "nn", F32, "up_fwd")
    yact = _conv_gate_fwd(u, conv_w, conv_b)
    z = _matmul(yact, w_down, "nn", F32, "down_fwd")
    dout, dz, dgate2, loss_row = _out_loss(z, x1, target, gate2)

    dyact = _matmul(dz, w_down, "nt", BF16, "down_bwd_x")
    g_down = _matmul(yact, dz, "tn", BF16, "down_bwd_w")
    da, dg, dconv_b, dconv_w = _conv_gate_bwd(dyact, u, conv_w, conv_b)
    du = jnp.concatenate([da, dg], axis=1)
    dh2 = _matmul(du, w_up, "nt", F32, "up_bwd_x")
    g_up = _matmul(h2, du, "tn", BF16, "up_bwd_w")
    dx1, dshift2, dscale2, dnorm2, dgate1, dmix = _norm_mod_bwd(
        dh2, x1, norm2_w, scale2, dout, "norm2_bwd", mix=mix, gate=gate1)
    dcat = _matmul(dmix, w_out, "nt", F32, "mix_bwd_x")
    g_out = _matmul(cat, dmix, "tn", BF16, "mix_bwd_w")
    dhq, dhf, dhi, dhg, dlb, dhg_norm = _hgrn_bwd(proj, lb_logits, hg_norm_w, o_raw, states, dcat[:, :HW])
    do_att = dcat[:, HW:]
    acc = None
    for dil in DILS:
        acc = _attn_bwd(proj, q_norm_w, k_norm_w, b_out_f32, lse, do_att, dil, acc)
    daq, dak, dav, dq_norm, dk_norm = acc
    dproj = jnp.concatenate([dhq, dhf, dhi, dhg, daq.astype(BF16), dak.astype(BF16), dav.astype(BF16)], axis=1)
    dh = _matmul(dproj, w_in, "nt", F32, "proj_bwd_x")
    g_in = _matmul(h, dproj, "tn", BF16, "proj_bwd_w")
    grad_x, dshift1, dscale1, dnorm1 = _norm_mod_bwd(dh, x, norm1_w, scale1, dx1, "norm1_bwd")

    dmod = jnp.concatenate([dshift1, dscale1, dgate1, dshift2, dscale2, dgate2], axis=1)
    small = dict(norm1_w=dnorm1, lb=dlb, hg_norm_w=dhg_norm, q_norm_w=dq_norm, k_norm_w=dk_norm,
                 norm2_w=dnorm2, conv_b=dconv_b, conv_w=dconv_w)
    return loss_row, grad_x, dmod, (g_in, g_out, g_up, g_down), small


def _place():
    x, y, c = lax.axis_index("x"), lax.axis_index("y"), lax.axis_index("c")
    others = [(1 - x, y), (x, 1 - y), (1 - x, 1 - y)]
    return x, y, c, others


def _remote(src, dst, send_sems, recv_sems, k, to):
    return pltpu.make_async_remote_copy(src_ref=src, dst_ref=dst, send_sem=send_sems.at[k], recv_sem=recv_sems.at[k],
                                        device_id=to, device_id_type=MESH)


def _all_gather_rows(block, name):
    m_per, n = block.shape

    def body(x_ref, out_ref, send_sems, recv_sems, local_sem):
        x, y, c, chips = _place()
        me, sibling = (x, y, c), (x, y, 1 - c)

        def rows(px, py, pc):
            return out_ref.at[pl.ds((4 * px + 2 * py + pc) * m_per, m_per), :]

        def copy(k, blk, to, src=None):
            return _remote(rows(*blk) if src is None else src, rows(*blk), send_sems, recv_sems, k, to)

        mine = pltpu.make_async_copy(x_ref, rows(*me), local_sem)
        mine.start()
        first = [copy(0, me, sibling, src=x_ref)]
        first += [copy(1 + j, me, (*chip, c), src=x_ref) for j, chip in enumerate(chips)]
        for cp in first:
            cp.start()
        passed = [copy(4 + j, (*chip, c), sibling) for j, chip in enumerate(chips)]
        for j, chip in enumerate(chips):
            copy(1 + j, (*chip, c), me).wait_recv()
            passed[j].start()
        copy(0, sibling, me).wait_recv()
        for j, chip in enumerate(chips):
            copy(4 + j, (*chip, 1 - c), me).wait_recv()
        for cp in first + passed:
            cp.wait_send()
        mine.wait()

    return pl.pallas_call(
        body, name=name, out_shape=jax.ShapeDtypeStruct((N_DEV * m_per, n), block.dtype),
        in_specs=[VMEM_SPEC], out_specs=VMEM_SPEC,
        scratch_shapes=[pltpu.SemaphoreType.DMA((7,)), pltpu.SemaphoreType.DMA((7,)), pltpu.SemaphoreType.DMA],
    )(block)


class _Geo:
    def __init__(self, kind, shard_shape):
        self.kind = kind
        self.shard_shape = tuple(shard_shape)
        self.hr, self.hc = shard_shape[0] // 2, shard_shape[1]
        if kind == "col":
            self.full_shape = (shard_shape[0], N_CHIPS * shard_shape[1])
        else:
            self.full_shape = (N_CHIPS * shard_shape[0], shard_shape[1])

    def full_shard(self, ref, j):
        if self.kind == "col":
            return ref.at[:, pl.ds(pl.multiple_of(j * self.hc, 128), self.hc)]
        return ref.at[pl.ds(pl.multiple_of(j * 2 * self.hr, 16), 2 * self.hr), :]

    def full_half(self, ref, j, c):
        if self.kind == "col":
            return ref.at[pl.ds(pl.multiple_of(c * self.hr, 16), self.hr),
                          pl.ds(pl.multiple_of(j * self.hc, 128), self.hc)]
        return ref.at[pl.ds(pl.multiple_of((2 * j + c) * self.hr, 16), self.hr), :]

    def shard_half(self, ref, c):
        return ref.at[pl.ds(pl.multiple_of(c * self.hr, 16), self.hr), :]


WEIGHT_KINDS = ("col", "row", "col", "row")
NW = len(WEIGHT_KINDS)


def _comm_call(body, name, ins, outs, n_remote, n_local):
    return pl.pallas_call(
        body, name=name, out_shape=tuple(outs),
        in_specs=[HBM_SPEC] * len(ins), out_specs=tuple([HBM_SPEC] * len(outs)),
        scratch_shapes=[pltpu.SemaphoreType.DMA((n_remote,)), pltpu.SemaphoreType.DMA((n_remote,)),
                        pltpu.SemaphoreType.DMA((n_local,))],
    )(*ins)


def _gather_weights(shards):
    geos = [_Geo(k, s.shape) for k, s in zip(WEIGHT_KINDS, shards)]

    def body(*refs):
        sh, full = refs[:NW], refs[NW:2 * NW]
        send_sems, recv_sems, local_sems = refs[2 * NW:]
        x, y, c, chips = _place()
        j = 2 * x + y
        sibling = (x, y, 1 - c)
        local = [pltpu.make_async_copy(sh[w], geos[w].full_shard(full[w], j), local_sems.at[w]) for w in range(NW)]
        for cp in local:
            cp.start()
        first = []
        for w in range(NW):
            for k, chip in enumerate(chips):
                first.append(_remote(geos[w].shard_half(sh[w], c), geos[w].full_half(full[w], j, c),
                                     send_sems, recv_sems, 6 * w + k, (*chip, c)))
        for cp in first:
            cp.start()
        passed = []
        for w in range(NW):
            for k, (ox, oy) in enumerate(chips):
                landed = geos[w].full_half(full[w], 2 * ox + oy, c)
                _remote(landed, landed, send_sems, recv_sems, 6 * w + k, (ox, oy, c)).wait_recv()
                fwd = _remote(landed, landed, send_sems, recv_sems, 6 * w + 3 + k, sibling)
                fwd.start()
                passed.append(fwd)
        for w in range(NW):
            for k, (ox, oy) in enumerate(chips):
                other = geos[w].full_half(full[w], 2 * ox + oy, 1 - c)
                _remote(other, other, send_sems, recv_sems, 6 * w + 3 + k, sibling).wait_recv()
        for cp in first + passed:
            cp.wait_send()
        for cp in local:
            cp.wait()

    outs = [jax.ShapeDtypeStruct(g.full_shape, BF16) for g in geos]
    return _comm_call(body, "gather_weights", shards, outs, 6 * NW, NW)


def _swap_halves(grads):
    geos = [_Geo(k, (g.shape[0], g.shape[1] // N_CHIPS) if k == "col" else (g.shape[0] // N_CHIPS, g.shape[1]))
            for k, g in zip(WEIGHT_KINDS, grads)]

    def body(*refs):
        full, own, got = refs[:NW], refs[NW:2 * NW], refs[2 * NW:3 * NW]
        send_sems, recv_sems, local_sems = refs[3 * NW:]
        x, y, c, _ = _place()
        sibling = (x, y, 1 - c)
        copies = []
        for w in range(NW):
            for j in range(N_CHIPS):
                k = N_CHIPS * w + j
                copies.append(pltpu.make_async_copy(geos[w].full_half(full[w], j, c), own[w].at[j], local_sems.at[k]))
                copies.append(_remote(geos[w].full_half(full[w], j, 1 - c), got[w].at[j], send_sems, recv_sems, k, sibling))
        for cp in copies:
            cp.start()
        for cp in copies:
            cp.wait()

    outs = [jax.ShapeDtypeStruct((N_CHIPS, g.hr, g.hc), BF16) for g in geos]
    res = _comm_call(body, "grad_swap_halves", grads, outs + outs, N_CHIPS * NW, N_CHIPS * NW)
    return res[:NW], res[NW:], geos


def _scatter_to_chips(parts):
    def body(*refs):
        src, dst = refs[:NW], refs[NW:2 * NW]
        send_sems, recv_sems, local_sems = refs[2 * NW:]
        x, y, c, chips = _place()
        j = 2 * x + y
        copies = []
        for w in range(NW):
            copies.append(pltpu.make_async_copy(src[w].at[j], dst[w].at[0], local_sems.at[w]))
            for k, (ox, oy) in enumerate(chips):
                copies.append(_remote(src[w].at[2 * ox + oy], dst[w].at[1 + k], send_sems, recv_sems, 3 * w + k, (ox, oy, c)))
        for cp in copies:
            cp.start()
        for cp in copies:
            cp.wait()

    outs = [jax.ShapeDtypeStruct(p.shape, BF16) for p in parts]
    return _comm_call(body, "grad_scatter_chips", parts, outs, 3 * NW, NW)


def _join_halves(halves):
    def body(*refs):
        src, dst = refs[:NW], refs[NW:2 * NW]
        send_sems, recv_sems, local_sems = refs[2 * NW:]
        x, y, c, _ = _place()
        sibling = (x, y, 1 - c)
        copies = []
        for w in range(NW):
            hr = halves[w].shape[0]
            mine = dst[w].at[pl.ds(pl.multiple_of(c * hr, 8), hr), :]
            copies.append(pltpu.make_async_copy(src[w], mine, local_sems.at[w]))
            copies.append(_remote(src[w], mine, send_sems, recv_sems, w, sibling))
        for cp in copies:
            cp.start()
        for cp in copies:
            cp.wait()

    outs = [jax.ShapeDtypeStruct((2 * h.shape[0], h.shape[1]), F32) for h in halves]
    return _comm_call(body, "grad_join_halves", halves, outs, NW, NW)


def _reduce_scatter(grads):
    own, got, _ = _swap_halves(grads)
    pair = [_add_pair(a, b, f"grad_pair_sum_{w}") for w, (a, b) in enumerate(zip(own, got))]
    parts = _scatter_to_chips(pair)
    halves = [_add_four(p, f"grad_chip_sum_{w}") for w, p in enumerate(parts)]
    return _join_halves(halves)


CONVW_SHARD = 3 * DFF // N_CHIPS
COND_PAD = 8 * 896
SMALL_SIZES = (("norm1_w", D), ("lb", HW), ("hg_norm_w", DH), ("q_norm_w", DH), ("k_norm_w", DH),
               ("norm2_w", D), ("conv_b", DFF), ("conv_w", 3 * DFF), ("loss", 128))
SMALL_TOTAL = sum(n for _, n in SMALL_SIZES)
STATS_PAD = 8 * 5120


def kernel(x, c, w_ada, b_ada, norm1_w, w_in, lb_logits, hg_norm_w, q_norm_w, k_norm_w, w_out, norm2_w, w_up, conv_w, conv_b, w_down, loss_target, m_w_ada, m_b_ada, m_norm1_w, m_w_in, m_lb_logits, m_hg_norm_w, m_q_norm_w, m_k_norm_w, m_w_out, m_norm2_w, m_w_up, m_conv_w, m_conv_b, m_w_down, v_w_ada, v_b_ada, v_norm1_w, v_w_in, v_lb_logits, v_hg_norm_w, v_q_norm_w, v_k_norm_w, v_w_out, v_norm2_w, v_w_up, v_conv_w, v_conv_b, v_w_down):
    chip = 2 * lax.axis_index("x") + lax.axis_index("y")
    dev = 2 * chip + lax.axis_index("c")

    cond = jnp.concatenate([c, conv_w[0].reshape(1, CONVW_SHARD), jnp.zeros((1, COND_PAD - D - CONVW_SHARD), F32)], axis=1)
    cond_all = _all_gather_rows(cond.reshape(8, COND_PAD // 8), "gather_cond").reshape(N_DEV, COND_PAD)
    c_all = cond_all[:, :D]
    conv_w_full = jnp.concatenate(
        [cond_all[2 * j, D:D + CONVW_SHARD].reshape(3, DFF // N_CHIPS) for j in range(N_CHIPS)], axis=1)

    b_shard = lax.dynamic_slice_in_dim(b_ada, chip * ADA_COLS, ADA_COLS, axis=1)
    mod_cols = _all_gather_rows(_ada_fwd(c_all, w_ada[0], b_shard), "gather_mod")
    mod_all = jnp.concatenate([mod_cols[16 * j:16 * j + 8] for j in range(N_CHIPS)], axis=1)
    mod = lax.dynamic_slice_in_dim(mod_all, dev, 1, axis=0)

    w_full = _gather_weights([w_in[0].astype(BF16), w_out[0].astype(BF16), w_up[0].astype(BF16), w_down[0].astype(BF16)])
    loss_row, grad_x, dmod, g_full, small = _sequence_step(
        x[0], loss_target[0], mod, norm1_w, w_full[0], lb_logits, hg_norm_w, q_norm_w, k_norm_w, w_full[1],
        norm2_w, w_full[2], conv_w_full, conv_b, w_full[3])
    g_in, g_out, g_up, g_down = _reduce_scatter(list(g_full))

    small["conv_w"] = small["conv_w"].reshape(1, 3 * DFF)
    small["loss"] = loss_row
    stats = jnp.concatenate([dmod] + [small[k] for k, _ in SMALL_SIZES]
                            + [jnp.zeros((1, STATS_PAD - 6 * D - SMALL_TOTAL), F32)], axis=1)
    stats_all = _all_gather_rows(stats.reshape(8, STATS_PAD // 8), "gather_stats").reshape(N_DEV, STATS_PAD)
    dmod_all = stats_all[:, :6 * D]
    sums = _sum_rows(stats_all[:, 6 * D:6 * D + SMALL_TOTAL], "small_grad_sum")
    g_small, off = {}, 0
    for k, n in SMALL_SIZES:
        g_small[k] = sums[:, off:off + n]
        off += n
    loss = g_small["loss"][0, 0]
    g_b_ada = _sum_rows(dmod_all, "b_ada_grad_sum")
    g_w_ada = _ada_bwd(c_all, lax.dynamic_slice_in_dim(dmod_all, chip * ADA_COLS, ADA_COLS, axis=1))
    g_lb = _lb_grad(lb_logits, g_small["lb"])
    g_conv_w = lax.dynamic_slice_in_dim(g_small["conv_w"].reshape(3, DFF), chip * (DFF // N_CHIPS), DFF // N_CHIPS, axis=1)

    names = ["w_ada", "b_ada", "norm1_w", "w_in", "lb_logits", "hg_norm_w", "q_norm_w", "k_norm_w", "w_out",
             "norm2_w", "w_up", "conv_w", "conv_b", "w_down"]
    lead = {"w_ada", "w_in", "w_out", "w_up", "conv_w", "w_down"}
    w = dict(w_ada=w_ada, b_ada=b_ada, norm1_w=norm1_w, w_in=w_in, lb_logits=lb_logits, hg_norm_w=hg_norm_w,
             q_norm_w=q_norm_w, k_norm_w=k_norm_w, w_out=w_out, norm2_w=norm2_w, w_up=w_up, conv_w=conv_w,
             conv_b=conv_b, w_down=w_down)
    m = dict(w_ada=m_w_ada, b_ada=m_b_ada, norm1_w=m_norm1_w, w_in=m_w_in, lb_logits=m_lb_logits,
             hg_norm_w=m_hg_norm_w, q_norm_w=m_q_norm_w, k_norm_w=m_k_norm_w, w_out=m_w_out, norm2_w=m_norm2_w,
             w_up=m_w_up, conv_w=m_conv_w, conv_b=m_conv_b, w_down=m_w_down)
    v = dict(w_ada=v_w_ada, b_ada=v_b_ada, norm1_w=v_norm1_w, w_in=v_w_in, lb_logits=v_lb_logits,
             hg_norm_w=v_hg_norm_w, q_norm_w=v_q_norm_w, k_norm_w=v_k_norm_w, w_out=v_w_out, norm2_w=v_norm2_w,
             w_up=v_w_up, conv_w=v_conv_w, conv_b=v_conv_b, w_down=v_w_down)
    g = dict(w_ada=g_w_ada, b_ada=g_b_ada, norm1_w=g_small["norm1_w"], w_in=g_in, lb_logits=g_lb,
             hg_norm_w=g_small["hg_norm_w"], q_norm_w=g_small["q_norm_w"], k_norm_w=g_small["k_norm_w"], w_out=g_out,
             norm2_w=g_small["norm2_w"], w_up=g_up, conv_w=g_conv_w, conv_b=g_small["conv_b"], w_down=g_down)

    grads, deltas, new_m, new_v = [], [], [], []
    for n in names:
        strip = (lambda t: t[0]) if n in lead else (lambda t: t)
        wrap = (lambda t: t[None]) if n in lead else (lambda t: t)
        d_n, m_n, v_n = _adamw(strip(w[n]), g[n], strip(m[n]), strip(v[n]), f"adamw_{n}")
        grads.append(wrap(g[n]))
        deltas.append(wrap(d_n))
        new_m.append(wrap(m_n))
        new_v.append(wrap(v_n))
    return (loss, grad_x[None], *grads, *deltas, *new_m, *new_v)
```

```python
import functools
import math

import jax
import jax.numpy as jnp
from jax import lax
from jax.experimental import pallas as pl
from jax.experimental.pallas import tpu as pltpu

F32 = jnp.float32
BF16 = jnp.bfloat16

S = 2048
D = 2048
H = 8
DH = 128
HW = H * DH
CH = 64
NCH = S // CH
DFF = 5632
INC = 7 * HW
BLK = 128
DILS = (1, 4, 16)
EPS = 1e-6
NEG = -1e30
N_CHIPS = 4
N_DEV = 8

ADAM_LR = 0.001
ADAM_B1 = 0.9
ADAM_B2 = 0.999
ADAM_EPS = 1e-08
ADAM_WD = 0.01
ADAM_STEP = 10
ADAM_BLOCK_BYTES = 1 << 20

V7X_VMEM_BYTES = 64 * 1024 * 1024
VMEM_LIMIT = (V7X_VMEM_BYTES * 3) // 4
MESH = pl.DeviceIdType.MESH
HBM_SPEC = pl.BlockSpec(memory_space=pltpu.HBM)
VMEM_SPEC = pl.BlockSpec(memory_space=pltpu.VMEM)

NN = (((1,), (0,)), ((), ()))
NT = (((1,), (1,)), ((), ()))
TN = (((0,), (0,)), ((), ()))


def _params(n_grid):
    return pltpu.CompilerParams(dimension_semantics=("arbitrary",) * n_grid, vmem_limit_bytes=VMEM_LIMIT)


def _dot(a, b, dims=NN):
    return lax.dot_general(a.astype(BF16), b.astype(BF16), dims, preferred_element_type=F32)


def _dot_f32(a, b):
    return lax.dot_general(a, b, NN, precision=lax.Precision.HIGHEST, preferred_element_type=F32)


def _sigmoid(x):
    return 1.0 / (1.0 + jnp.exp(-x))


def _pick(n, cands):
    for t in cands:
        if n % t == 0:
            return t
    raise ValueError(n)


def _matmul(a, b, mode, out_dtype, name):
    if mode == "nn":
        (m, k), (_, n) = a.shape, b.shape
    elif mode == "nt":
        (m, k), (n, _) = a.shape, b.shape
    else:
        (k, m), (_, n) = a.shape, b.shape
    tm = _pick(m, (1024, 512))
    tn = _pick(n, (1024, 512))
    tk = _pick(k, (1024, 512))
    nk = k // tk
    dims = {"nn": NN, "nt": NT, "tn": TN}[mode]

    def body(a_ref, b_ref, o_ref, acc_ref):
        kk = pl.program_id(2)

        @pl.when(kk == 0)
        def _():
            acc_ref[...] = jnp.zeros_like(acc_ref)

        acc_ref[...] += lax.dot_general(a_ref[...], b_ref[...], dims, preferred_element_type=F32)

        @pl.when(kk == nk - 1)
        def _():
            o_ref[...] = acc_ref[...].astype(o_ref.dtype)

    if mode == "tn":
        a_spec = pl.BlockSpec((tk, tm), lambda i, j, kk: (kk, i))
    else:
        a_spec = pl.BlockSpec((tm, tk), lambda i, j, kk: (i, kk))
    if mode == "nt":
        b_spec = pl.BlockSpec((tn, tk), lambda i, j, kk: (j, kk))
    else:
        b_spec = pl.BlockSpec((tk, tn), lambda i, j, kk: (kk, j))
    return pl.pallas_call(
        body, name=name,
        out_shape=jax.ShapeDtypeStruct((m, n), out_dtype),
        grid=(m // tm, n // tn, nk),
        in_specs=[a_spec, b_spec],
        out_specs=pl.BlockSpec((tm, tn), lambda i, j, kk: (i, j)),
        scratch_shapes=[pltpu.VMEM((tm, tn), F32)],
        compiler_params=_params(3),
    )(a, b)


TR = 256


def _row_spec(cols=D):
    return pl.BlockSpec((TR, cols), lambda i: (i, 0))


def _vec_spec(cols=D):
    return pl.BlockSpec((1, cols), lambda i: (0, 0))


def _norm_mod(x, nw, scale, shift, name):
    def body(x_ref, nw_ref, sc_ref, sh_ref, h_ref):
        xv = x_ref[...]
        r = lax.rsqrt(jnp.mean(xv * xv, axis=-1, keepdims=True) + EPS)
        h_ref[...] = ((xv * r) * nw_ref[...] * (1.0 + sc_ref[...]) + sh_ref[...]).astype(BF16)

    return pl.pallas_call(
        body, name=name, out_shape=jax.ShapeDtypeStruct((S, D), BF16), grid=(S // TR,),
        in_specs=[_row_spec(), _vec_spec(), _vec_spec(), _vec_spec()], out_specs=_row_spec(),
        compiler_params=_params(1),
    )(x, nw, scale, shift)


def _resid_norm_mod(x, mix, gate, nw, scale, shift, name):
    def body(x_ref, mix_ref, g_ref, nw_ref, sc_ref, sh_ref, x1_ref, h_ref):
        xv = x_ref[...] + g_ref[...] * mix_ref[...]
        x1_ref[...] = xv
        r = lax.rsqrt(jnp.mean(xv * xv, axis=-1, keepdims=True) + EPS)
        h_ref[...] = ((xv * r) * nw_ref[...] * (1.0 + sc_ref[...]) + sh_ref[...]).astype(BF16)

    return pl.pallas_call(
        body, name=name,
        out_shape=(jax.ShapeDtypeStruct((S, D), F32), jax.ShapeDtypeStruct((S, D), BF16)), grid=(S // TR,),
        in_specs=[_row_spec(), _row_spec(), _vec_spec(), _vec_spec(), _vec_spec(), _vec_spec()],
        out_specs=(_row_spec(), _row_spec()),
        compiler_params=_params(1),
    )(x, mix, gate, nw, scale, shift)


def _norm_mod_bwd(dh, xin, nw, scale, dres, name, mix=None, gate=None):
    with_gate = mix is not None

    def body(*refs):
        if with_gate:
            dh_ref, x_ref, nw_ref, sc_ref, dres_ref, mix_ref, g_ref, dx_ref, dsh_ref, dsc_ref, dnw_ref, dg_ref, dmix_ref = refs
        else:
            dh_ref, x_ref, nw_ref, sc_ref, dres_ref, dx_ref, dsh_ref, dsc_ref, dnw_ref = refs
        i = pl.program_id(0)
        dhv = dh_ref[...]
        xv = x_ref[...]
        r = lax.rsqrt(jnp.mean(xv * xv, axis=-1, keepdims=True) + EPS)
        xn = xv * r
        nwv = nw_ref[...]
        one_sc = 1.0 + sc_ref[...]
        dxn = dhv * nwv * one_sc
        dx = dres_ref[...] + r * (dxn - xn * jnp.mean(dxn * xn, axis=-1, keepdims=True))
        dx_ref[...] = dx

        @pl.when(i == 0)
        def _():
            dsh_ref[...] = jnp.zeros_like(dsh_ref)
            dsc_ref[...] = jnp.zeros_like(dsc_ref)
            dnw_ref[...] = jnp.zeros_like(dnw_ref)
            if with_gate:
                dg_ref[...] = jnp.zeros_like(dg_ref)

        dsh_ref[...] += jnp.sum(dhv, axis=0, keepdims=True)
        dsc_ref[...] += jnp.sum(dhv * xn * nwv, axis=0, keepdims=True)
        dnw_ref[...] += jnp.sum(dhv * xn * one_sc, axis=0, keepdims=True)
        if with_gate:
            dg_ref[...] += jnp.sum(dx * mix_ref[...], axis=0, keepdims=True)
            dmix_ref[...] = (dx * g_ref[...]).astype(BF16)

    vec = jax.ShapeDtypeStruct((1, D), F32)
    ins = [dh, xin, nw, scale, dres]
    in_specs = [_row_spec(), _row_spec(), _vec_spec(), _vec_spec(), _row_spec()]
    outs = [jax.ShapeDtypeStruct((S, D), F32), vec, vec, vec]
    out_specs = [_row_spec(), _vec_spec(), _vec_spec(), _vec_spec()]
    if with_gate:
        ins += [mix, gate]
        in_specs += [_row_spec(), _vec_spec()]
        outs += [vec, jax.ShapeDtypeStruct((S, D), BF16)]
        out_specs += [_vec_spec(), _row_spec()]
    return pl.pallas_call(
        body, name=name, out_shape=tuple(outs), grid=(S // TR,), in_specs=in_specs, out_specs=tuple(out_specs),
        compiler_params=_params(1),
    )(*ins)


def _tri(lower):
    row = lax.broadcasted_iota(jnp.int32, (CH, CH), 0)
    col = lax.broadcasted_iota(jnp.int32, (CH, CH), 1)
    return (row >= col) if lower else (col >= row)


def _hgrn_gates(hq, hf, lb):
    sig = _sigmoid(hf)
    f = lb + (1.0 - lb) * sig
    g = jnp.log(f)
    sq = _sigmoid(hq)
    return sig, f, g, 1.0 - f, hq * sq, sq


def _proj_col_spec(group):
    return pl.BlockSpec((S, DH), lambda h: (0, group * H + h))


def _hgrn_fwd(proj, lb_logits, norm_w):
    def body(hq_ref, hf_ref, hi_ref, hg_ref, lbl_ref, nw_ref, out_ref, oraw_ref, st_ref, s_ref):
        lb = 1.0 / (1.0 + jnp.exp(lbl_ref[1:2, :] - lbl_ref[0:1, :]))
        nw = nw_ref[...]
        lower = _tri(True)
        ltri = lower.astype(F32)
        s_ref[...] = jnp.zeros_like(s_ref)

        def chunk(n, carry):
            rows = pl.ds(pl.multiple_of(n * CH, CH), CH)
            hq, hf, v, hg = hq_ref[rows, :], hf_ref[rows, :], hi_ref[rows, :], hg_ref[rows, :]
            _, _, g, kk, q, _ = _hgrn_gates(hq, hf, lb)
            gc = _dot_f32(ltri, g)
            gl = jnp.sum(g, axis=0, keepdims=True)
            qe = q * jnp.exp(gc)
            ke = kk * jnp.exp(gl - gc)
            qh = q * jnp.exp(gc - 0.5 * gl)
            kh = kk * jnp.exp(0.5 * gl - gc)
            st = s_ref[...]
            st_ref[0, pl.ds(n, 1)] = st[None]
            a = jnp.where(lower, _dot(qh, kh, NT), 0.0)
            o = _dot(qe, st, NT) + _dot(a, v)
            s_ref[...] = st * jnp.exp(gl) + _dot(v, ke, TN)
            oraw_ref[rows, :] = o
            rs = lax.rsqrt(jnp.mean(o * o, axis=-1, keepdims=True) + EPS)
            out_ref[rows, :] = (o * rs * nw * (hg * _sigmoid(hg))).astype(BF16)
            return carry

        lax.fori_loop(0, NCH, chunk, 0)

    head_spec = pl.BlockSpec((S, DH), lambda h: (0, h))
    return pl.pallas_call(
        body, name="hgrn_fwd",
        out_shape=(jax.ShapeDtypeStruct((S, HW), BF16), jax.ShapeDtypeStruct((S, HW), F32),
                   jax.ShapeDtypeStruct((H, NCH, DH, DH), F32)),
        grid=(H,),
        in_specs=[_proj_col_spec(0), _proj_col_spec(1), _proj_col_spec(2), _proj_col_spec(3),
                  pl.BlockSpec((2, DH), lambda h: (0, h)), pl.BlockSpec((1, DH), lambda h: (0, 0))],
        out_specs=(head_spec, head_spec, pl.BlockSpec((1, NCH, DH, DH), lambda h: (h, 0, 0, 0))),
        scratch_shapes=[pltpu.VMEM((DH, DH), F32)],
        compiler_params=_params(1),
    )(proj, proj, proj, proj, lb_logits, norm_w)


def _hgrn_bwd(proj, lb_logits, norm_w, oraw, states, dout):
    def body(hq_ref, hf_ref, hi_ref, hg_ref, lbl_ref, nw_ref, oraw_ref, st_ref, do_ref,
             dhq_ref, dhf_ref, dhi_ref, dhg_ref, dlb_ref, dnw_ref, ds_ref, acc_ref):
        h = pl.program_id(0)
        lb = 1.0 / (1.0 + jnp.exp(lbl_ref[1:2, :] - lbl_ref[0:1, :]))
        nw = nw_ref[...]
        lower = _tri(True)
        ltri = lower.astype(F32)
        utri = _tri(False).astype(F32)
        last = lax.broadcasted_iota(jnp.int32, (CH, DH), 0) == CH - 1
        ds_ref[...] = jnp.zeros_like(ds_ref)
        acc_ref[...] = jnp.zeros_like(acc_ref)

        @pl.when(h == 0)
        def _():
            dnw_ref[...] = jnp.zeros_like(dnw_ref)

        def chunk(i, carry):
            n = NCH - 1 - i
            rows = pl.ds(pl.multiple_of(n * CH, CH), CH)
            hq, hf, v, hg = hq_ref[rows, :], hf_ref[rows, :], hi_ref[rows, :], hg_ref[rows, :]
            sig, f, g, kk, q, sq = _hgrn_gates(hq, hf, lb)
            gc = _dot_f32(ltri, g)
            gl = jnp.sum(g, axis=0, keepdims=True)
            eg = jnp.exp(gc)
            ek = jnp.exp(gl - gc)
            gam = jnp.exp(gl)
            ph = jnp.exp(gc - 0.5 * gl)
            pk = jnp.exp(0.5 * gl - gc)
            qe, ke = q * eg, kk * ek
            qh, kh = q * ph, kk * pk
            st = st_ref[0, pl.ds(n, 1)][0]
            dst = ds_ref[...]
            a = jnp.where(lower, _dot(qh, kh, NT), 0.0)
            o = oraw_ref[rows, :]
            rs = lax.rsqrt(jnp.mean(o * o, axis=-1, keepdims=True) + EPS)
            xh = o * rs
            sg = _sigmoid(hg)
            dgo = do_ref[rows, :]
            d_on = dgo * (hg * sg)
            dhg_ref[rows, :] = (dgo * xh * nw * (sg * (1.0 + hg * (1.0 - sg)))).astype(BF16)
            acc_ref[1:2, :] += jnp.sum(d_on * xh, axis=0, keepdims=True)
            dy = d_on * nw
            do = rs * (dy - xh * jnp.mean(dy * xh, axis=-1, keepdims=True))
            dqe = _dot(do, st)
            da = jnp.where(lower, _dot(do, v, NT), 0.0)
            dv = _dot(a, do, TN) + _dot(ke, dst, NT)
            dke = _dot(v, dst)
            dgam = jnp.sum(dst * st, axis=0, keepdims=True)
            dqh = _dot(da, kh)
            dkh = _dot(da, qh, TN)
            dq = dqh * ph + dqe * eg
            dk = dkh * pk + dke * ek
            dgl = jnp.sum(dke * ke, axis=0, keepdims=True) + dgam * gam
            dgc = dqh * qh - dkh * kh + dqe * qe - dke * ke + jnp.where(last, dgl, 0.0)
            dg = _dot_f32(utri, dgc)
            df = dg / f - dk
            dhf_ref[rows, :] = (df * (1.0 - lb) * sig * (1.0 - sig)).astype(BF16)
            acc_ref[0:1, :] += jnp.sum(df * (1.0 - sig), axis=0, keepdims=True)
            dhq_ref[rows, :] = (dq * (sq * (1.0 + hq * (1.0 - sq)))).astype(BF16)
            dhi_ref[rows, :] = dv.astype(BF16)
            ds_ref[...] = dst * gam + _dot(do, qe, TN)
            return carry

        lax.fori_loop(0, NCH, chunk, 0)
        dlb_ref[...] = acc_ref[0:1, :]
        dnw_ref[...] += acc_ref[1:2, :]

    head_spec = pl.BlockSpec((S, DH), lambda h: (0, h))
    head_out = jax.ShapeDtypeStruct((S, HW), BF16)
    return pl.pallas_call(
        body, name="hgrn_bwd",
        out_shape=(head_out, head_out, head_out, head_out,
                   jax.ShapeDtypeStruct((1, HW), F32), jax.ShapeDtypeStruct((1, DH), F32)),
        grid=(H,),
        in_specs=[_proj_col_spec(0), _proj_col_spec(1), _proj_col_spec(2), _proj_col_spec(3),
                  pl.BlockSpec((2, DH), lambda h: (0, h)), pl.BlockSpec((1, DH), lambda h: (0, 0)),
                  head_spec, pl.BlockSpec((1, NCH, DH, DH), lambda h: (h, 0, 0, 0)), head_spec],
        out_specs=(head_spec, head_spec, head_spec, head_spec,
                   pl.BlockSpec((1, DH), lambda h: (0, h)), pl.BlockSpec((1, DH), lambda h: (0, 0))),
        scratch_shapes=[pltpu.VMEM((DH, DH), F32), pltpu.VMEM((8, DH), F32)],
        compiler_params=_params(1),
    )(proj, proj, proj, proj, lb_logits, norm_w, oraw, states, dout)


ATT_SCALE = DH ** -0.5


def _rms_rows(x, w):
    rs = lax.rsqrt(jnp.mean(x * x, axis=-1, keepdims=True) + EPS)
    return x * rs, rs


def _att_masks():
    qi = lax.broadcasted_iota(jnp.int32, (BLK, BLK), 0)
    kj = lax.broadcasted_iota(jnp.int32, (BLK, BLK), 1)
    steps_c = qi - kj
    steps_p = qi - kj + BLK
    return steps_c, steps_p, steps_c >= 0, steps_p <= BLK


def _slope(h, dil):
    return float(2.0 ** (-(h + 1))) * dil


def _attn_fwd(proj, q_w, k_w, dil):
    length = S // dil
    nblk = length // BLK
    pv = proj.reshape(length, dil * INC)

    def body(q_ref, kc_ref, kp_ref, vc_ref, vp_ref, qw_ref, kw_ref, o_ref, lse_ref):
        n = pl.program_id(1)
        steps_c, steps_p, ok_c, ok_p = _att_masks()
        ok_p = jnp.logical_and(ok_p, n > 0)
        fc, fp = steps_c.astype(F32), steps_p.astype(F32)
        for h in range(H):
            cols = slice(h * DH, (h + 1) * DH)
            qn = _rms_rows(q_ref[:, cols], None)[0] * qw_ref[...]
            kcn = _rms_rows(kc_ref[:, cols], None)[0] * kw_ref[...]
            kpn = _rms_rows(kp_ref[:, cols], None)[0] * kw_ref[...]
            sc = jnp.where(ok_c, _dot(qn, kcn, NT) * ATT_SCALE - _slope(h, dil) * fc, NEG)
            sp = jnp.where(ok_p, _dot(qn, kpn, NT) * ATT_SCALE - _slope(h, dil) * fp, NEG)
            m = jnp.maximum(jnp.max(sc, axis=-1, keepdims=True), jnp.max(sp, axis=-1, keepdims=True))
            pc, pp = jnp.exp(sc - m), jnp.exp(sp - m)
            den = jnp.sum(pc, axis=-1, keepdims=True) + jnp.sum(pp, axis=-1, keepdims=True)
            o_ref[:, cols] = (_dot(pc, vc_ref[:, cols]) + _dot(pp, vp_ref[:, cols])) / den
            lse_ref[:, cols] = jnp.broadcast_to(m + jnp.log(den), (BLK, DH))

    def spec(group, prev):
        if prev:
            return pl.BlockSpec((BLK, HW), lambda r, n: (jnp.maximum(n - 1, 0), r * 7 + group))
        return pl.BlockSpec((BLK, HW), lambda r, n: (n, r * 7 + group))

    out = jax.ShapeDtypeStruct((length, dil * HW), F32)
    out_spec = pl.BlockSpec((BLK, HW), lambda r, n: (n, r))
    wspec = pl.BlockSpec((1, DH), lambda r, n: (0, 0))
    o, lse = pl.pallas_call(
        body, name=f"attn_fwd_d{dil}", out_shape=(out, out), grid=(dil, nblk),
        in_specs=[spec(4, False), spec(5, False), spec(5, True), spec(6, False), spec(6, True), wspec, wspec],
        out_specs=(out_spec, out_spec),
        compiler_params=_params(2),
    )(pv, pv, pv, pv, pv, q_w, k_w)
    return o.reshape(S, HW), lse.reshape(S, HW)


def _attn_merge(outs, lses):
    def body(o1, o2, o3, l1, l2, l3, ob_ref, of_ref, lse_ref):
        a, b, c = l1[...], l2[...], l3[...]
        m = jnp.maximum(jnp.maximum(a, b), c)
        ea, eb, ec = jnp.exp(a - m), jnp.exp(b - m), jnp.exp(c - m)
        den = ea + eb + ec
        o = (ea * o1[...] + eb * o2[...] + ec * o3[...]) / den
        ob_ref[...] = o.astype(BF16)
        of_ref[...] = o
        lse_ref[...] = m + jnp.log(den)

    f = jax.ShapeDtypeStruct((S, HW), F32)
    return pl.pallas_call(
        body, name="attn_merge", out_shape=(jax.ShapeDtypeStruct((S, HW), BF16), f, f), grid=(S // TR,),
        in_specs=[_row_spec(HW)] * 6, out_specs=(_row_spec(HW),) * 3,
        compiler_params=_params(1),
    )(*outs, *lses)


def _attn_bwd(proj, q_w, k_w, o, lse, do, dil, acc):
    length = S // dil
    nblk = length // BLK
    pv = proj.reshape(length, dil * INC)
    view = lambda t: t.reshape(length, dil * HW)
    has_acc = acc is not None

    def body(*refs):
        (q_ref, qn_ref, kp_ref, kc_ref, vp_ref, vc_ref, o_ref, on_ref, l_ref, ln_ref, do_ref, don_ref,
         qw_ref, kw_ref) = refs[:14]
        refs = refs[14:]
        if has_acc:
            aq_ref, ak_ref, av_ref, aqw_ref, akw_ref = refs[:5]
            refs = refs[5:]
        dq_ref, dk_ref, dv_ref, dqw_ref, dkw_ref = refs
        r, m = pl.program_id(0), pl.program_id(1)
        steps_c, steps_p, ok_c, ok_p = _att_masks()
        ok_prev = jnp.logical_and(ok_p, m > 0)
        ok_next = jnp.logical_and(ok_p, m < nblk - 1)
        fc, fp = steps_c.astype(F32), steps_p.astype(F32)
        qw, kw = qw_ref[...], kw_ref[...]

        @pl.when(jnp.logical_and(r == 0, m == 0))
        def _():
            if has_acc:
                dqw_ref[...] = aqw_ref[...]
                dkw_ref[...] = akw_ref[...]
            else:
                dqw_ref[...] = jnp.zeros_like(dqw_ref)
                dkw_ref[...] = jnp.zeros_like(dkw_ref)

        for h in range(H):
            cols = slice(h * DH, (h + 1) * DH)
            sl = _slope(h, dil)
            qx, qrs = _rms_rows(q_ref[:, cols], None)
            kx, krs = _rms_rows(kc_ref[:, cols], None)
            qn, kcn = qx * qw, kx * kw
            qnn = _rms_rows(qn_ref[:, cols], None)[0] * qw
            kpn = _rms_rows(kp_ref[:, cols], None)[0] * kw
            vc, vp = vc_ref[:, cols], vp_ref[:, cols]
            dov, donv = do_ref[:, cols], don_ref[:, cols]
            lse_m, lse_n = l_ref[:, cols][:, 0:1], ln_ref[:, cols][:, 0:1]
            delta_m = jnp.sum(dov * o_ref[:, cols], axis=-1, keepdims=True)
            delta_n = jnp.sum(donv * on_ref[:, cols], axis=-1, keepdims=True)
            p_c = jnp.where(ok_c, jnp.exp(_dot(qn, kcn, NT) * ATT_SCALE - sl * fc - lse_m), 0.0)
            p_p = jnp.where(ok_prev, jnp.exp(_dot(qn, kpn, NT) * ATT_SCALE - sl * fp - lse_m), 0.0)
            p_n = jnp.where(ok_next, jnp.exp(_dot(qnn, kcn, NT) * ATT_SCALE - sl * fp - lse_n), 0.0)
            ds_c = p_c * (_dot(dov, vc, NT) - delta_m)
            ds_p = p_p * (_dot(dov, vp, NT) - delta_m)
            ds_n = p_n * (_dot(donv, vc, NT) - delta_n)
            dqn = (_dot(ds_c, kcn) + _dot(ds_p, kpn)) * ATT_SCALE
            dkn = (_dot(ds_c, qn, TN) + _dot(ds_n, qnn, TN)) * ATT_SCALE
            dv = _dot(p_c, dov, TN) + _dot(p_n, donv, TN)
            dqw_ref[...] += jnp.sum(dqn * qx, axis=0, keepdims=True)
            dkw_ref[...] += jnp.sum(dkn * kx, axis=0, keepdims=True)
            dyq, dyk = dqn * qw, dkn * kw
            daq = qrs * (dyq - qx * jnp.mean(dyq * qx, axis=-1, keepdims=True))
            dak = krs * (dyk - kx * jnp.mean(dyk * kx, axis=-1, keepdims=True))
            if has_acc:
                daq, dak, dv = daq + aq_ref[:, cols], dak + ak_ref[:, cols], dv + av_ref[:, cols]
            dq_ref[:, cols] = daq
            dk_ref[:, cols] = dak
            dv_ref[:, cols] = dv

    def pspec(group, shift):
        if shift < 0:
            return pl.BlockSpec((BLK, HW), lambda r, m: (jnp.maximum(m - 1, 0), r * 7 + group))
        if shift > 0:
            return pl.BlockSpec((BLK, HW), lambda r, m: (jnp.minimum(m + 1, nblk - 1), r * 7 + group))
        return pl.BlockSpec((BLK, HW), lambda r, m: (m, r * 7 + group))

    cur = pl.BlockSpec((BLK, HW), lambda r, m: (m, r))
    nxt = pl.BlockSpec((BLK, HW), lambda r, m: (jnp.minimum(m + 1, nblk - 1), r))
    wspec = pl.BlockSpec((1, DH), lambda r, m: (0, 0))
    ins = [pv, pv, pv, pv, pv, pv, view(o), view(o), view(lse), view(lse), view(do), view(do), q_w, k_w]
    in_specs = [pspec(4, 0), pspec(4, 1), pspec(5, -1), pspec(5, 0), pspec(6, -1), pspec(6, 0),
                cur, nxt, cur, nxt, cur, nxt, wspec, wspec]
    if has_acc:
        ins += [view(acc[0]), view(acc[1]), view(acc[2]), acc[3], acc[4]]
        in_specs += [cur, cur, cur, wspec, wspec]
    big = jax.ShapeDtypeStruct((length, dil * HW), F32)
    small = jax.ShapeDtypeStruct((1, DH), F32)
    dq, dk, dv, dqw, dkw = pl.pallas_call(
        body, name=f"attn_bwd_d{dil}", out_shape=(big, big, big, small, small), grid=(dil, nblk),
        in_specs=in_specs, out_specs=(cur, cur, cur, wspec, wspec),
        compiler_params=_params(2),
    )(*ins)
    return dq.reshape(S, HW), dk.reshape(S, HW), dv.reshape(S, HW), dqw, dkw


TC = 512
NCT = DFF // TC


def _shift_rows(a, k):
    rows = lax.broadcasted_iota(jnp.int32, a.shape, 0)
    rolled = pltpu.roll(a, k % S, 0)
    if k > 0:
        return jnp.where(rows >= k, rolled, 0.0)
    return jnp.where(rows < S + k, rolled, 0.0)


def _conv_pre(a, w_ref, b_ref):
    return b_ref[...] + w_ref[0:1, :] * _shift_rows(a, 2) + w_ref[1:2, :] * _shift_rows(a, 1) + w_ref[2:3, :] * a


def _conv_gate_fwd(u, conv_w, conv_b):
    def body(a_ref, g_ref, w_ref, b_ref, y_ref):
        pre = _conv_pre(a_ref[...], w_ref, b_ref)
        y_ref[...] = (pre * _sigmoid(pre) * g_ref[...]).astype(BF16)

    return pl.pallas_call(
        body, name="conv_gate_fwd", out_shape=jax.ShapeDtypeStruct((S, DFF), BF16), grid=(NCT,),
        in_specs=[pl.BlockSpec((S, TC), lambda i: (0, i)), pl.BlockSpec((S, TC), lambda i: (0, NCT + i)),
                  pl.BlockSpec((3, TC), lambda i: (0, i)), pl.BlockSpec((1, TC), lambda i: (0, i))],
        out_specs=pl.BlockSpec((S, TC), lambda i: (0, i)),
        compiler_params=_params(1),
    )(u, u, conv_w, conv_b)


def _conv_gate_bwd(dy, u, conv_w, conv_b):
    def body(dy_ref, a_ref, g_ref, w_ref, b_ref, da_ref, dg_ref, db_ref, dw_ref):
        a = a_ref[...]
        dyv = dy_ref[...]
        pre = _conv_pre(a, w_ref, b_ref)
        sg = _sigmoid(pre)
        dg_ref[...] = (dyv * pre * sg).astype(BF16)
        dpre = dyv * g_ref[...] * (sg * (1.0 + pre * (1.0 - sg)))
        da = w_ref[2:3, :] * dpre + w_ref[1:2, :] * _shift_rows(dpre, -1) + w_ref[0:1, :] * _shift_rows(dpre, -2)
        da_ref[...] = da.astype(BF16)
        db_ref[...] = jnp.sum(dpre, axis=0, keepdims=True)
        dw_ref[0:1, :] = jnp.sum(dpre * _shift_rows(a, 2), axis=0, keepdims=True)
        dw_ref[1:2, :] = jnp.sum(dpre * _shift_rows(a, 1), axis=0, keepdims=True)
        dw_ref[2:3, :] = jnp.sum(dpre * a, axis=0, keepdims=True)

    col = pl.BlockSpec((S, TC), lambda i: (0, i))
    half = jax.ShapeDtypeStruct((S, DFF), BF16)
    return pl.pallas_call(
        body, name="conv_gate_bwd",
        out_shape=(half, half, jax.ShapeDtypeStruct((1, DFF), F32), jax.ShapeDtypeStruct((3, DFF), F32)),
        grid=(NCT,),
        in_specs=[col, col, pl.BlockSpec((S, TC), lambda i: (0, NCT + i)),
                  pl.BlockSpec((3, TC), lambda i: (0, i)), pl.BlockSpec((1, TC), lambda i: (0, i))],
        out_specs=(col, col, pl.BlockSpec((1, TC), lambda i: (0, i)), pl.BlockSpec((3, TC), lambda i: (0, i))),
        compiler_params=_params(1),
    )(dy, u, u, conv_w, conv_b)


def _out_loss(z, x1, target, gate):
    def body(z_ref, x_ref, t_ref, g_ref, do_ref, dz_ref, dg_ref, loss_ref):
        i = pl.program_id(0)
        zv = z_ref[...]
        gv = g_ref[...]
        err = x_ref[...] + gv * zv - t_ref[...]
        dout = err * (1.0 / D)
        do_ref[...] = dout
        dz_ref[...] = (dout * gv).astype(BF16)

        @pl.when(i == 0)
        def _():
            dg_ref[...] = jnp.zeros_like(dg_ref)
            loss_ref[...] = jnp.zeros_like(loss_ref)

        dg_ref[...] += jnp.sum(dout * zv, axis=0, keepdims=True)
        part = jnp.sum(jnp.sum(err * err, axis=0, keepdims=True), axis=-1, keepdims=True) * (0.5 / D)
        lane = lax.broadcasted_iota(jnp.int32, (1, 128), 1)
        loss_ref[...] += jnp.where(lane == 0, part, 0.0)

    return pl.pallas_call(
        body, name="out_loss",
        out_shape=(jax.ShapeDtypeStruct((S, D), F32), jax.ShapeDtypeStruct((S, D), BF16),
                   jax.ShapeDtypeStruct((1, D), F32), jax.ShapeDtypeStruct((1, 128), F32)),
        grid=(S // TR,),
        in_specs=[_row_spec(), _row_spec(), _row_spec(), _vec_spec()],
        out_specs=(_row_spec(), _row_spec(), _vec_spec(), _vec_spec(128)),
        compiler_params=_params(1),
    )(z, x1, target, gate)


ADA_COLS = 6 * D // N_CHIPS
TA = 512


def _ada_fwd(c_all, w_shard, b_shard):
    def body(c_ref, w_ref, b_ref, o_ref):
        cv = c_ref[...]
        o_ref[...] = _dot_f32(cv * _sigmoid(cv), w_ref[...]) + b_ref[...]

    return pl.pallas_call(
        body, name="ada_fwd", out_shape=jax.ShapeDtypeStruct((N_DEV, ADA_COLS), F32), grid=(ADA_COLS // TA,),
        in_specs=[pl.BlockSpec((N_DEV, D), lambda j: (0, 0)), pl.BlockSpec((D, TA), lambda j: (0, j)),
                  pl.BlockSpec((1, TA), lambda j: (0, j))],
        out_specs=pl.BlockSpec((N_DEV, TA), lambda j: (0, j)),
        compiler_params=_params(1),
    )(c_all, w_shard, b_shard)


def _ada_bwd(c_all, dmod_shard):
    def body(c_ref, d_ref, o_ref):
        cv = c_ref[...]
        o_ref[...] = lax.dot_general(cv * _sigmoid(cv), d_ref[...], TN, precision=lax.Precision.HIGHEST,
                                     preferred_element_type=F32)

    return pl.pallas_call(
        body, name="ada_bwd", out_shape=jax.ShapeDtypeStruct((D, ADA_COLS), F32), grid=(ADA_COLS // TA,),
        in_specs=[pl.BlockSpec((N_DEV, D), lambda j: (0, 0)), pl.BlockSpec((N_DEV, TA), lambda j: (0, j))],
        out_specs=pl.BlockSpec((D, TA), lambda j: (0, j)),
        compiler_params=_params(1),
    )(c_all, dmod_shard)


def _sum_rows(g, name):
    rows, n = g.shape

    def body(g_ref, o_ref):
        acc = g_ref[0:1, :]
        for i in range(1, rows):
            acc = acc + g_ref[i:i + 1, :]
        o_ref[...] = acc

    return pl.pallas_call(
        body, name=name, out_shape=jax.ShapeDtypeStruct((1, n), F32),
        in_specs=[VMEM_SPEC], out_specs=VMEM_SPEC,
    )(g)


def _lb_grad(lb_logits, dlb):
    def body(l_ref, d_ref, o_ref):
        p = 1.0 / (1.0 + jnp.exp(l_ref[1:2, :] - l_ref[0:1, :]))
        t = d_ref[...] * p * (1.0 - p)
        o_ref[0:1, :] = t
        o_ref[1:2, :] = -t

    return pl.pallas_call(
        body, name="lb_grad", out_shape=jax.ShapeDtypeStruct((2, HW), F32),
        in_specs=[VMEM_SPEC, VMEM_SPEC], out_specs=VMEM_SPEC,
    )(lb_logits, dlb)


def _adamw(w, g, m, v, name):
    rows, cols = w.shape
    tr = rows
    if rows * cols * 4 > ADAM_BLOCK_BYTES:
        tr = _pick(rows, [t for t in (256, 128, 64, 32, 16, 8) if t * cols * 4 <= ADAM_BLOCK_BYTES])

    def body(w_ref, g_ref, m_ref, v_ref, d_ref, mo_ref, vo_ref):
        gv = g_ref[...]
        m2 = ADAM_B1 * m_ref[...] + (1.0 - ADAM_B1) * gv
        v2 = ADAM_B2 * v_ref[...] + (1.0 - ADAM_B2) * (gv * gv)
        m_hat = m2 / (1.0 - ADAM_B1 ** ADAM_STEP)
        v_hat = v2 / (1.0 - ADAM_B2 ** ADAM_STEP)
        d_ref[...] = -ADAM_LR * (m_hat / (jnp.sqrt(v_hat) + ADAM_EPS) + ADAM_WD * w_ref[...])
        mo_ref[...] = m2
        vo_ref[...] = v2

    spec = pl.BlockSpec((tr, cols), lambda i: (i, 0))
    out = jax.ShapeDtypeStruct((rows, cols), F32)
    return pl.pallas_call(
        body, name=name, out_shape=(out, out, out), grid=(rows // tr,),
        in_specs=[spec] * 4, out_specs=(spec,) * 3,
        compiler_params=_params(1),
    )(w, g, m, v)


def _sequence_step(x, target, mod, norm1_w, w_in, lb_logits, hg_norm_w, q_norm_w, k_norm_w, w_out,
                   norm2_w, w_up, conv_w, conv_b, w_down):
    shift1, scale1, gate1, shift2, scale2, gate2 = [mod[:, i * D:(i + 1) * D] for i in range(6)]

    h = _norm_mod(x, norm1_w, scale1, shift1, "norm1_fwd")
    proj = _matmul(h, w_in, "nn", F32, "proj_fwd")
    a_out, o_raw, states = _hgrn_fwd(proj, lb_logits, hg_norm_w)
    att = [_attn_fwd(proj, q_norm_w, k_norm_w, dil) for dil in DILS]
    b_out, b_out_f32, lse = _attn_merge([t[0] for t in att], [t[1] for t in att])
    cat = jnp.concatenate([a_out, b_out], axis=1)
    mix = _matmul(cat, w_out, "nn", F32, "mix_fwd")
    x1, h2 = _resid_norm_mod(x, mix, gate1, norm2_w, scale2, shift2, "norm2_fwd")
    u = _matmul(h2, w_up, "nn", F32, "up_fwd")
    yact = _conv_gate_fwd(u, conv_w, conv_b)
    z = _matmul(yact, w_down, "nn", F32, "down_fwd")
    dout, dz, dgate2, loss_row = _out_loss(z, x1, target, gate2)

    dyact = _matmul(dz, w_down, "nt", BF16, "down_bwd_x")
    g_down = _matmul(yact, dz, "tn", BF16, "down_bwd_w")
    da, dg, dconv_b, dconv_w = _conv_gate_bwd(dyact, u, conv_w, conv_b)
    du = jnp.concatenate([da, dg], axis=1)
    dh2 = _matmul(du, w_up, "nt", F32, "up_bwd_x")
    g_up = _matmul(h2, du, "tn", BF16, "up_bwd_w")
    dx1, dshift2, dscale2, dnorm2, dgate1, dmix = _norm_mod_bwd(
        dh2, x1, norm2_w, scale2, dout, "norm2_bwd", mix=mix, gate=gate1)
    dcat = _matmul(dmix, w_out, "nt", F32, "mix_bwd_x")
    g_out = _matmul(cat, dmix, "tn", BF16, "mix_bwd_w")
    dhq, dhf, dhi, dhg, dlb, dhg_norm = _hgrn_bwd(proj, lb_logits, hg_norm_w, o_raw, states, dcat[:, :HW])
    do_att = dcat[:, HW:]
    acc = None
    for dil in DILS:
        acc = _attn_bwd(proj, q_norm_w, k_norm_w, b_out_f32, lse, do_att, dil, acc)
    daq, dak, dav, dq_norm, dk_norm = acc
    dproj = jnp.concatenate([dhq, dhf, dhi, dhg, daq.astype(BF16), dak.astype(BF16), dav.astype(BF16)], axis=1)
    dh = _matmul(dproj, w_in, "nt", F32, "proj_bwd_x")
    g_in = _matmul(h, dproj, "tn", BF16, "proj_bwd_w")
    grad_x, dshift1, dscale1, dnorm1 = _norm_mod_bwd(dh, x, norm1_w, scale1, dx1, "norm1_bwd")

    dmod = jnp.concatenate([dshift1, dscale1, dgate1, dshift2, dscale2, dgate2], axis=1)
    small = dict(norm1_w=dnorm1, lb=dlb, hg_norm_w=dhg_norm, q_norm_w=dq_norm, k_norm_w=dk_norm,
                 norm2_w=dnorm2, conv_b=dconv_b, conv_w=dconv_w)
    return loss_row, grad_x, dmod, (g_in, g_out, g_up, g_down), small


def _place():
    x, y, c = lax.axis_index("x"), lax.axis_index("y"), lax.axis_index("c")
    others = [(1 - x, y), (x, 1 - y), (1 - x, 1 - y)]
    return x, y, c, others


def _remote(src, dst, send_sems, recv_sems, k, to):
    return pltpu.make_async_remote_copy(src_ref=src, dst_ref=dst, send_sem=send_sems.at[k], recv_sem=recv_sems.at[k],
                                        device_id=to, device_id_type=MESH)


def _all_gather_rows(block, name):
    m_per, n = block.shape

    def body(x_ref, out_ref, send_sems, recv_sems, local_sem):
        x, y, c, chips = _place()
        me, sibling = (x, y, c), (x, y, 1 - c)

        def rows(px, py, pc):
            return out_ref.at[pl.ds((4 * px + 2 * py + pc) * m_per, m_per), :]

        def copy(k, blk, to, src=None):
            return _remote(rows(*blk) if src is None else src, rows(*blk), send_sems, recv_sems, k, to)

        mine = pltpu.make_async_copy(x_ref, rows(*me), local_sem)
        mine.start()
        first = [copy(0, me, sibling, src=x_ref)]
        first += [copy(1 + j, me, (*chip, c), src=x_ref) for j, chip in enumerate(chips)]
        for cp in first:
            cp.start()
        passed = [copy(4 + j, (*chip, c), sibling) for j, chip in enumerate(chips)]
        for j, chip in enumerate(chips):
            copy(1 + j, (*chip, c), me).wait_recv()
            passed[j].start()
        copy(0, sibling, me).wait_recv()
        for j, chip in enumerate(chips):
            copy(4 + j, (*chip, 1 - c), me).wait_recv()
        for cp in first + passed:
            cp.wait_send()
        mine.wait()

    return pl.pallas_call(
        body, name=name, out_shape=jax.ShapeDtypeStruct((N_DEV * m_per, n), block.dtype),
        in_specs=[VMEM_SPEC], out_specs=VMEM_SPEC,
        scratch_shapes=[pltpu.SemaphoreType.DMA((7,)), pltpu.SemaphoreType.DMA((7,)), pltpu.SemaphoreType.DMA],
    )(block)


class _Geo:
    def __init__(self, kind, shard_shape):
        self.kind = kind
        self.shard_shape = tuple(shard_shape)
        self.hr, self.hc = shard_shape[0] // 2, shard_shape[1]
        if kind == "col":
            self.full_shape = (shard_shape[0], N_CHIPS * shard_shape[1])
        else:
            self.full_shape = (N_CHIPS * shard_shape[0], shard_shape[1])

    def full_shard(self, ref, j):
        if self.kind == "col":
            return ref.at[:, pl.ds(pl.multiple_of(j * self.hc, 128), self.hc)]
        return ref.at[pl.ds(pl.multiple_of(j * 2 * self.hr, 16), 2 * self.hr), :]

    def full_half(self, ref, j, c):
        if self.kind == "col":
            return ref.at[pl.ds(pl.multiple_of(c * self.hr, 16), self.hr),
                          pl.ds(pl.multiple_of(j * self.hc, 128), self.hc)]
        return ref.at[pl.ds(pl.multiple_of((2 * j + c) * self.hr, 16), self.hr), :]

    def shard_half(self, ref, c):
        return ref.at[pl.ds(pl.multiple_of(c * self.hr, 16), self.hr), :]


WEIGHT_KINDS = ("col", "row", "col", "row")
NW = len(WEIGHT_KINDS)


def _comm_call(body, name, ins, outs, n_remote, aliases=None):
    return pl.pallas_call(
        body, name=name, out_shape=tuple(outs),
        in_specs=[HBM_SPEC] * len(ins), out_specs=tuple([HBM_SPEC] * len(outs)),
        input_output_aliases=aliases or {},
        scratch_shapes=[pltpu.SemaphoreType.DMA((n_remote,)), pltpu.SemaphoreType.DMA((n_remote,))],
    )(*ins)


ROW_TILES = (256, 176, 128, 64, 32, 16)


def _cast_into_full(shard, geo, place, name):
    rs, cs = shard.shape
    tr = _pick(rs, ROW_TILES)
    nb = rs // tr

    def body(place_ref, s_ref, o_ref):
        o_ref[...] = s_ref[...].astype(BF16)

    if geo.kind == "col":
        out_map = lambda i, place_ref: (i, place_ref[0])
    else:
        out_map = lambda i, place_ref: (place_ref[0] * nb + i, 0)
    return pl.pallas_call(
        body, name=name, out_shape=jax.ShapeDtypeStruct(geo.full_shape, BF16),
        grid_spec=pltpu.PrefetchScalarGridSpec(
            num_scalar_prefetch=1, grid=(nb,),
            in_specs=[pl.BlockSpec((tr, cs), lambda i, place_ref: (i, 0))],
            out_specs=pl.BlockSpec((tr, cs), out_map)),
        compiler_params=_params(1),
    )(place, shard)


def _gather_weights(fulls, geos):
    def body(*refs):
        full = refs[NW:2 * NW]
        send_sems, recv_sems = refs[2 * NW:]
        x, y, c, chips = _place()
        j = 2 * x + y
        sibling = (x, y, 1 - c)
        first = []
        for w in range(NW):
            mine = geos[w].full_half(full[w], j, c)
            for k, chip in enumerate(chips):
                first.append(_remote(mine, mine, send_sems, recv_sems, 6 * w + k, (*chip, c)))
        for cp in first:
            cp.start()
        passed = []
        for w in range(NW):
            for k, (ox, oy) in enumerate(chips):
                landed = geos[w].full_half(full[w], 2 * ox + oy, c)
                _remote(landed, landed, send_sems, recv_sems, 6 * w + k, (ox, oy, c)).wait_recv()
                fwd = _remote(landed, landed, send_sems, recv_sems, 6 * w + 3 + k, sibling)
                fwd.start()
                passed.append(fwd)
        for w in range(NW):
            for k, (ox, oy) in enumerate(chips):
                other = geos[w].full_half(full[w], 2 * ox + oy, 1 - c)
                _remote(other, other, send_sems, recv_sems, 6 * w + 3 + k, sibling).wait_recv()
        for cp in first + passed:
            cp.wait_send()

    outs = [jax.ShapeDtypeStruct(g.full_shape, BF16) for g in geos]
    return _comm_call(body, "gather_weights", fulls, outs, 6 * NW, aliases={w: w for w in range(NW)})


def _swap_halves(grads, geos):
    def body(*refs):
        full, got = refs[:NW], refs[NW:2 * NW]
        send_sems, recv_sems = refs[2 * NW:]
        x, y, c, _ = _place()
        sibling = (x, y, 1 - c)
        copies = []
        for w in range(NW):
            for j in range(N_CHIPS):
                copies.append(_remote(geos[w].full_half(full[w], j, 1 - c), got[w].at[j], send_sems, recv_sems,
                                      N_CHIPS * w + j, sibling))
        for cp in copies:
            cp.start()
        for cp in copies:
            cp.wait()

    outs = [jax.ShapeDtypeStruct((N_CHIPS, g.hr, g.hc), BF16) for g in geos]
    return _comm_call(body, "grad_swap_halves", grads, outs, N_CHIPS * NW)


def _add_pair(grad, got, geo, place, name):
    tr = _pick(geo.hr, ROW_TILES)
    nb = geo.hr // tr

    def body(place_ref, a_ref, b_ref, o_ref):
        o_ref[0] = (a_ref[...].astype(F32) + b_ref[0].astype(F32)).astype(BF16)

    if geo.kind == "col":
        own_map = lambda j, i, place_ref: (place_ref[1] * nb + i, j)
    else:
        own_map = lambda j, i, place_ref: ((2 * j + place_ref[1]) * nb + i, 0)
    spec = pl.BlockSpec((1, tr, geo.hc), lambda j, i, place_ref: (j, i, 0))
    return pl.pallas_call(
        body, name=name, out_shape=jax.ShapeDtypeStruct((N_CHIPS, geo.hr, geo.hc), BF16),
        grid_spec=pltpu.PrefetchScalarGridSpec(
            num_scalar_prefetch=1, grid=(N_CHIPS, nb),
            in_specs=[pl.BlockSpec((tr, geo.hc), own_map), spec], out_specs=spec),
        compiler_params=_params(2),
    )(place, grad, got)


def _scatter_to_chips(pairs):
    def body(*refs):
        src, dst = refs[:NW], refs[NW:2 * NW]
        send_sems, recv_sems = refs[2 * NW:]
        x, y, c, chips = _place()
        copies = []
        for w in range(NW):
            for k, (ox, oy) in enumerate(chips):
                copies.append(_remote(src[w].at[2 * ox + oy], dst[w].at[k], send_sems, recv_sems, 3 * w + k, (ox, oy, c)))
        for cp in copies:
            cp.start()
        for cp in copies:
            cp.wait()

    outs = [jax.ShapeDtypeStruct((3,) + p.shape[1:], BF16) for p in pairs]
    return _comm_call(body, "grad_scatter_chips", pairs, outs, 3 * NW)


def _add_four(pair, recv, geo, place, name):
    tr = _pick(geo.hr, ROW_TILES)
    nb = geo.hr // tr

    def body(place_ref, p_ref, r_ref, o_ref):
        o_ref[...] = ((p_ref[0].astype(F32) + r_ref[0].astype(F32)) + r_ref[1].astype(F32)) + r_ref[2].astype(F32)

    return pl.pallas_call(
        body, name=name, out_shape=jax.ShapeDtypeStruct((2 * geo.hr, geo.hc), F32),
        grid_spec=pltpu.PrefetchScalarGridSpec(
            num_scalar_prefetch=1, grid=(nb,),
            in_specs=[pl.BlockSpec((1, tr, geo.hc), lambda i, place_ref: (place_ref[0], i, 0)),
                      pl.BlockSpec((3, tr, geo.hc), lambda i, place_ref: (0, i, 0))],
            out_specs=pl.BlockSpec((tr, geo.hc), lambda i, place_ref: (place_ref[1] * nb + i, 0))),
        compiler_params=_params(1),
    )(place, pair, recv)


def _join_halves(shards, geos):
    def body(*refs):
        dst = refs[NW:2 * NW]
        send_sems, recv_sems = refs[2 * NW:]
        x, y, c, _ = _place()
        copies = []
        for w in range(NW):
            mine = dst[w].at[pl.ds(pl.multiple_of(c * geos[w].hr, 8), geos[w].hr), :]
            copies.append(_remote(mine, mine, send_sems, recv_sems, w, (x, y, 1 - c)))
        for cp in copies:
            cp.start()
        for cp in copies:
            cp.wait_send()
        for w in range(NW):
            other = dst[w].at[pl.ds(pl.multiple_of((1 - c) * geos[w].hr, 8), geos[w].hr), :]
            _remote(other, other, send_sems, recv_sems, w, (x, y, 1 - c)).wait_recv()

    outs = [jax.ShapeDtypeStruct(s.shape, F32) for s in shards]
    return _comm_call(body, "grad_join_halves", shards, outs, NW, aliases={w: w for w in range(NW)})


def _reduce_scatter(grads, geos, place):
    got = _swap_halves(grads, geos)
    pairs = [_add_pair(grads[w], got[w], geos[w], place, f"grad_pair_sum_{w}") for w in range(NW)]
    recv = _scatter_to_chips(pairs)
    shards = [_add_four(pairs[w], recv[w], geos[w], place, f"grad_chip_sum_{w}") for w in range(NW)]
    return _join_halves(shards, geos)


CONVW_SHARD = 3 * DFF // N_CHIPS
COND_PAD = 8 * 896
SMALL_SIZES = (("norm1_w", D), ("lb", HW), ("hg_norm_w", DH), ("q_norm_w", DH), ("k_norm_w", DH),
               ("norm2_w", D), ("conv_b", DFF), ("conv_w", 3 * DFF), ("loss", 128))
SMALL_TOTAL = sum(n for _, n in SMALL_SIZES)
STATS_PAD = 8 * 5120


def kernel(x, c, w_ada, b_ada, norm1_w, w_in, lb_logits, hg_norm_w, q_norm_w, k_norm_w, w_out, norm2_w, w_up, conv_w, conv_b, w_down, loss_target, m_w_ada, m_b_ada, m_norm1_w, m_w_in, m_lb_logits, m_hg_norm_w, m_q_norm_w, m_k_norm_w, m_w_out, m_norm2_w, m_w_up, m_conv_w, m_conv_b, m_w_down, v_w_ada, v_b_ada, v_norm1_w, v_w_in, v_lb_logits, v_hg_norm_w, v_q_norm_w, v_k_norm_w, v_w_out, v_norm2_w, v_w_up, v_conv_w, v_conv_b, v_w_down):
    chip = 2 * lax.axis_index("x") + lax.axis_index("y")
    dev = 2 * chip + lax.axis_index("c")

    cond = jnp.concatenate([c, conv_w[0].reshape(1, CONVW_SHARD), jnp.zeros((1, COND_PAD - D - CONVW_SHARD), F32)], axis=1)
    cond_all = _all_gather_rows(cond.reshape(8, COND_PAD // 8), "gather_cond").reshape(N_DEV, COND_PAD)
    c_all = cond_all[:, :D]
    conv_w_full = jnp.concatenate(
        [cond_all[2 * j, D:D + CONVW_SHARD].reshape(3, DFF // N_CHIPS) for j in range(N_CHIPS)], axis=1)

    b_shard = lax.dynamic_slice_in_dim(b_ada, chip * ADA_COLS, ADA_COLS, axis=1)
    mod_cols = _all_gather_rows(_ada_fwd(c_all, w_ada[0], b_shard), "gather_mod")
    mod_all = jnp.concatenate([mod_cols[16 * j:16 * j + 8] for j in range(N_CHIPS)], axis=1)
    mod = lax.dynamic_slice_in_dim(mod_all, dev, 1, axis=0)

    place = jnp.stack([chip, lax.axis_index("c")]).astype(jnp.int32)
    shards = [w_in[0], w_out[0], w_up[0], w_down[0]]
    geos = [_Geo(k, s.shape) for k, s in zip(WEIGHT_KINDS, shards)]
    w_full = _gather_weights([_cast_into_full(s, g, place, f"cast_shard_{i}") for i, (s, g) in enumerate(zip(shards, geos))], geos)
    loss_row, grad_x, dmod, g_full, small = _sequence_step(
        x[0], loss_target[0], mod, norm1_w, w_full[0], lb_logits, hg_norm_w, q_norm_w, k_norm_w, w_full[1],
        norm2_w, w_full[2], conv_w_full, conv_b, w_full[3])
    g_in, g_out, g_up, g_down = _reduce_scatter(list(g_full), geos, place)

    small["conv_w"] = small["conv_w"].reshape(1, 3 * DFF)
    small["loss"] = loss_row
    stats = jnp.concatenate([dmod] + [small[k] for k, _ in SMALL_SIZES]
                            + [jnp.zeros((1, STATS_PAD - 6 * D - SMALL_TOTAL), F32)], axis=1)
    stats_all = _all_gather_rows(stats.reshape(8, STATS_PAD // 8), "gather_stats").reshape(N_DEV, STATS_PAD)
    dmod_all = stats_all[:, :6 * D]
    sums = _sum_rows(stats_all[:, 6 * D:6 * D + SMALL_TOTAL], "small_grad_sum")
    g_small, off = {}, 0
    for k, n in SMALL_SIZES:
        g_small[k] = sums[:, off:off + n]
        off += n
    loss = g_small["loss"][0, 0]
    g_b_ada = _sum_rows(dmod_all, "b_ada_grad_sum")
    g_w_ada = _ada_bwd(c_all, lax.dynamic_slice_in_dim(dmod_all, chip * ADA_COLS, ADA_COLS, axis=1))
    g_lb = _lb_grad(lb_logits, g_small["lb"])
    g_conv_w = lax.dynamic_slice_in_dim(g_small["conv_w"].reshape(3, DFF), chip * (DFF // N_CHIPS), DFF // N_CHIPS, axis=1)

    names = ["w_ada", "b_ada", "norm1_w", "w_in", "lb_logits", "hg_norm_w", "q_norm_w", "k_norm_w", "w_out",
             "norm2_w", "w_up", "conv_w", "conv_b", "w_down"]
    lead = {"w_ada", "w_in", "w_out", "w_up", "conv_w", "w_down"}
    w = dict(w_ada=w_ada, b_ada=b_ada, norm1_w=norm1_w, w_in=w_in, lb_logits=lb_logits, hg_norm_w=hg_norm_w,
             q_norm_w=q_norm_w, k_norm_w=k_norm_w, w_out=w_out, norm2_w=norm2_w, w_up=w_up, conv_w=conv_w,
             conv_b=conv_b, w_down=w_down)
    m = dict(w_ada=m_w_ada, b_ada=m_b_ada, norm1_w=m_norm1_w, w_in=m_w_in, lb_logits=m_lb_logits,
             hg_norm_w=m_hg_norm_w, q_norm_w=m_q_norm_w, k_norm_w=m_k_norm_w, w_out=m_w_out, norm2_w=m_norm2_w,
             w_up=m_w_up, conv_w=m_conv_w, conv_b=m_conv_b, w_down=m_w_down)
    v = dict(w_ada=v_w_ada, b_ada=v_b_ada, norm1_w=v_norm1_w, w_in=v_w_in, lb_logits=v_lb_logits,
             hg_norm_w=v_hg_norm_w, q_norm_w=v_q_norm_w, k_norm_w=v_k_norm_w, w_out=v_w_out, norm2_w=v_norm2_w,
             w_up=v_w_up, conv_w=v_conv_w, conv_b=v_conv_b, w_down=v_w_down)
    g = dict(w_ada=g_w_ada, b_ada=g_b_ada, norm1_w=g_small["norm1_w"], w_in=g_in, lb_logits=g_lb,
             hg_norm_w=g_small["hg_norm_w"], q_norm_w=g_small["q_norm_w"], k_norm_w=g_small["k_norm_w"], w_out=g_out,
             norm2_w=g_small["norm2_w"], w_up=g_up, conv_w=g_conv_w, conv_b=g_small["conv_b"], w_down=g_down)

    grads, deltas, new_m, new_v = [], [], [], []
    for n in names:
        strip = (lambda t: t[0]) if n in lead else (lambda t: t)
        wrap = (lambda t: t[None]) if n in lead else (lambda t: t)
        d_n, m_n, v_n = _adamw(strip(w[n]), g[n], strip(m[n]), strip(v[n]), f"adamw_{n}")
        grads.append(wrap(g[n]))
        deltas.append(wrap(d_n))
        new_m.append(wrap(m_n))
        new_v.append(wrap(v_n))
    return (loss, grad_x[None], *grads, *deltas, *new_m, *new_v)
```

```python
import functools
import math

import jax
import jax.numpy as jnp
from jax import lax
from jax.experimental import pallas as pl
from jax.experimental.pallas import tpu as pltpu

F32 = jnp.float32
BF16 = jnp.bfloat16

S = 2048
D = 2048
H = 8
DH = 128
HW = H * DH
CH = 64
NCH = S // CH
DFF = 5632
INC = 7 * HW
BLK = 128
DILS = (1, 4, 16)
EPS = 1e-6
NEG = -1e30
N_CHIPS = 4
N_DEV = 8

ADAM_LR = 0.001
ADAM_B1 = 0.9
ADAM_B2 = 0.999
ADAM_EPS = 1e-08
ADAM_WD = 0.01
ADAM_STEP = 10
ADAM_BLOCK_BYTES = 1 << 20

V7X_VMEM_BYTES = 64 * 1024 * 1024
VMEM_LIMIT = (V7X_VMEM_BYTES * 3) // 4
MESH = pl.DeviceIdType.MESH
HBM_SPEC = pl.BlockSpec(memory_space=pltpu.HBM)
VMEM_SPEC = pl.BlockSpec(memory_space=pltpu.VMEM)

NN = (((1,), (0,)), ((), ()))
NT = (((1,), (1,)), ((), ()))
TN = (((0,), (0,)), ((), ()))


def _params(n_grid):
    return pltpu.CompilerParams(dimension_semantics=("arbitrary",) * n_grid, vmem_limit_bytes=VMEM_LIMIT)


def _dot(a, b, dims=NN):
    return lax.dot_general(a.astype(BF16), b.astype(BF16), dims, preferred_element_type=F32)


def _dot_f32(a, b):
    return lax.dot_general(a, b, NN, precision=lax.Precision.HIGHEST, preferred_element_type=F32)


def _sigmoid(x):
    return 1.0 / (1.0 + jnp.exp(-x))


def _pick(n, cands):
    for t in cands:
        if n % t == 0:
            return t
    raise ValueError(n)


def _matmul(a, b, mode, out_dtype, name):
    if mode == "nn":
        (m, k), (_, n) = a.shape, b.shape
    elif mode == "nt":
        (m, k), (n, _) = a.shape, b.shape
    else:
        (k, m), (_, n) = a.shape, b.shape
    tm = _pick(m, (1024, 512))
    tn = _pick(n, (1024, 512))
    tk = _pick(k, (1024, 512))
    nk = k // tk
    dims = {"nn": NN, "nt": NT, "tn": TN}[mode]

    def body(a_ref, b_ref, o_ref, acc_ref):
        kk = pl.program_id(2)

        @pl.when(kk == 0)
        def _():
            acc_ref[...] = jnp.zeros_like(acc_ref)

        acc_ref[...] += lax.dot_general(a_ref[...], b_ref[...], dims, preferred_element_type=F32)

        @pl.when(kk == nk - 1)
        def _():
            o_ref[...] = acc_ref[...].astype(o_ref.dtype)

    if mode == "tn":
        a_spec = pl.BlockSpec((tk, tm), lambda i, j, kk: (kk, i))
    else:
        a_spec = pl.BlockSpec((tm, tk), lambda i, j, kk: (i, kk))
    if mode == "nt":
        b_spec = pl.BlockSpec((tn, tk), lambda i, j, kk: (j, kk))
    else:
        b_spec = pl.BlockSpec((tk, tn), lambda i, j, kk: (kk, j))
    return pl.pallas_call(
        body, name=name,
        out_shape=jax.ShapeDtypeStruct((m, n), out_dtype),
        grid=(m // tm, n // tn, nk),
        in_specs=[a_spec, b_spec],
        out_specs=pl.BlockSpec((tm, tn), lambda i, j, kk: (i, j)),
        scratch_shapes=[pltpu.VMEM((tm, tn), F32)],
        compiler_params=_params(3),
    )(a, b)


TR = 256


def _row_spec(cols=D):
    return pl.BlockSpec((TR, cols), lambda i: (i, 0))


def _vec_spec(cols=D):
    return pl.BlockSpec((1, cols), lambda i: (0, 0))


def _norm_mod(x, nw, scale, shift, name):
    def body(x_ref, nw_ref, sc_ref, sh_ref, h_ref):
        xv = x_ref[...]
        r = lax.rsqrt(jnp.mean(xv * xv, axis=-1, keepdims=True) + EPS)
        h_ref[...] = ((xv * r) * nw_ref[...] * (1.0 + sc_ref[...]) + sh_ref[...]).astype(BF16)

    return pl.pallas_call(
        body, name=name, out_shape=jax.ShapeDtypeStruct((S, D), BF16), grid=(S // TR,),
        in_specs=[_row_spec(), _vec_spec(), _vec_spec(), _vec_spec()], out_specs=_row_spec(),
        compiler_params=_params(1),
    )(x, nw, scale, shift)


def _resid_norm_mod(x, mix, gate, nw, scale, shift, name):
    def body(x_ref, mix_ref, g_ref, nw_ref, sc_ref, sh_ref, x1_ref, h_ref):
        xv = x_ref[...] + g_ref[...] * mix_ref[...]
        x1_ref[...] = xv
        r = lax.rsqrt(jnp.mean(xv * xv, axis=-1, keepdims=True) + EPS)
        h_ref[...] = ((xv * r) * nw_ref[...] * (1.0 + sc_ref[...]) + sh_ref[...]).astype(BF16)

    return pl.pallas_call(
        body, name=name,
        out_shape=(jax.ShapeDtypeStruct((S, D), F32), jax.ShapeDtypeStruct((S, D), BF16)), grid=(S // TR,),
        in_specs=[_row_spec(), _row_spec(), _vec_spec(), _vec_spec(), _vec_spec(), _vec_spec()],
        out_specs=(_row_spec(), _row_spec()),
        compiler_params=_params(1),
    )(x, mix, gate, nw, scale, shift)


def _norm_mod_bwd(dh, xin, nw, scale, dres, name, mix=None, gate=None):
    with_gate = mix is not None

    def body(*refs):
        if with_gate:
            dh_ref, x_ref, nw_ref, sc_ref, dres_ref, mix_ref, g_ref, dx_ref, dsh_ref, dsc_ref, dnw_ref, dg_ref, dmix_ref = refs
        else:
            dh_ref, x_ref, nw_ref, sc_ref, dres_ref, dx_ref, dsh_ref, dsc_ref, dnw_ref = refs
        i = pl.program_id(0)
        dhv = dh_ref[...]
        xv = x_ref[...]
        r = lax.rsqrt(jnp.mean(xv * xv, axis=-1, keepdims=True) + EPS)
        xn = xv * r
        nwv = nw_ref[...]
        one_sc = 1.0 + sc_ref[...]
        dxn = dhv * nwv * one_sc
        dx = dres_ref[...] + r * (dxn - xn * jnp.mean(dxn * xn, axis=-1, keepdims=True))
        dx_ref[...] = dx

        @pl.when(i == 0)
        def _():
            dsh_ref[...] = jnp.zeros_like(dsh_ref)
            dsc_ref[...] = jnp.zeros_like(dsc_ref)
            dnw_ref[...] = jnp.zeros_like(dnw_ref)
            if with_gate:
                dg_ref[...] = jnp.zeros_like(dg_ref)

        dsh_ref[...] += jnp.sum(dhv, axis=0, keepdims=True)
        dsc_ref[...] += jnp.sum(dhv * xn * nwv, axis=0, keepdims=True)
        dnw_ref[...] += jnp.sum(dhv * xn * one_sc, axis=0, keepdims=True)
        if with_gate:
            dg_ref[...] += jnp.sum(dx * mix_ref[...], axis=0, keepdims=True)
            dmix_ref[...] = (dx * g_ref[...]).astype(BF16)

    vec = jax.ShapeDtypeStruct((1, D), F32)
    ins = [dh, xin, nw, scale, dres]
    in_specs = [_row_spec(), _row_spec(), _vec_spec(), _vec_spec(), _row_spec()]
    outs = [jax.ShapeDtypeStruct((S, D), F32), vec, vec, vec]
    out_specs = [_row_spec(), _vec_spec(), _vec_spec(), _vec_spec()]
    if with_gate:
        ins += [mix, gate]
        in_specs += [_row_spec(), _vec_spec()]
        outs += [vec, jax.ShapeDtypeStruct((S, D), BF16)]
        out_specs += [_vec_spec(), _row_spec()]
    return pl.pallas_call(
        body, name=name, out_shape=tuple(outs), grid=(S // TR,), in_specs=in_specs, out_specs=tuple(out_specs),
        compiler_params=_params(1),
    )(*ins)


def _tri(lower):
    row = lax.broadcasted_iota(jnp.int32, (CH, CH), 0)
    col = lax.broadcasted_iota(jnp.int32, (CH, CH), 1)
    return (row >= col) if lower else (col >= row)


def _hgrn_gates(hq, hf, lb):
    sig = _sigmoid(hf)
    f = lb + (1.0 - lb) * sig
    g = jnp.log(f)
    sq = _sigmoid(hq)
    return sig, f, g, 1.0 - f, hq * sq, sq


def _proj_col_spec(group):
    return pl.BlockSpec((S, DH), lambda h: (0, group * H + h))


def _hgrn_fwd(proj, lb_logits, norm_w):
    def body(hq_ref, hf_ref, hi_ref, hg_ref, lbl_ref, nw_ref, out_ref, oraw_ref, st_ref, s_ref):
        lb = 1.0 / (1.0 + jnp.exp(lbl_ref[1:2, :] - lbl_ref[0:1, :]))
        nw = nw_ref[...]
        lower = _tri(True)
        ltri = lower.astype(F32)
        s_ref[...] = jnp.zeros_like(s_ref)

        def chunk(n, carry):
            rows = pl.ds(pl.multiple_of(n * CH, CH), CH)
            hq, hf, v, hg = hq_ref[rows, :], hf_ref[rows, :], hi_ref[rows, :], hg_ref[rows, :]
            _, _, g, kk, q, _ = _hgrn_gates(hq, hf, lb)
            gc = _dot_f32(ltri, g)
            gl = jnp.sum(g, axis=0, keepdims=True)
            qe = q * jnp.exp(gc)
            ke = kk * jnp.exp(gl - gc)
            qh = q * jnp.exp(gc - 0.5 * gl)
            kh = kk * jnp.exp(0.5 * gl - gc)
            st = s_ref[...]
            st_ref[0, pl.ds(n, 1)] = st[None]
            a = jnp.where(lower, _dot(qh, kh, NT), 0.0)
            o = _dot(qe, st, NT) + _dot(a, v)
            s_ref[...] = st * jnp.exp(gl) + _dot(v, ke, TN)
            oraw_ref[rows, :] = o
            rs = lax.rsqrt(jnp.mean(o * o, axis=-1, keepdims=True) + EPS)
            out_ref[rows, :] = (o * rs * nw * (hg * _sigmoid(hg))).astype(BF16)
            return carry

        lax.fori_loop(0, NCH, chunk, 0)

    head_spec = pl.BlockSpec((S, DH), lambda h: (0, h))
    return pl.pallas_call(
        body, name="hgrn_fwd",
        out_shape=(jax.ShapeDtypeStruct((S, 2 * HW), BF16), jax.ShapeDtypeStruct((S, HW), F32),
                   jax.ShapeDtypeStruct((H, NCH, DH, DH), F32)),
        grid=(H,),
        in_specs=[_proj_col_spec(0), _proj_col_spec(1), _proj_col_spec(2), _proj_col_spec(3),
                  pl.BlockSpec((2, DH), lambda h: (0, h)), pl.BlockSpec((1, DH), lambda h: (0, 0))],
        out_specs=(head_spec, head_spec, pl.BlockSpec((1, NCH, DH, DH), lambda h: (h, 0, 0, 0))),
        scratch_shapes=[pltpu.VMEM((DH, DH), F32)],
        compiler_params=_params(1),
    )(proj, proj, proj, proj, lb_logits, norm_w)


def _hgrn_bwd(proj, lb_logits, norm_w, oraw, states, dout):
    def body(hq_ref, hf_ref, hi_ref, hg_ref, lbl_ref, nw_ref, oraw_ref, st_ref, do_ref,
             dhq_ref, dhf_ref, dhi_ref, dhg_ref, dlb_ref, dnw_ref, ds_ref, acc_ref):
        h = pl.program_id(0)
        lb = 1.0 / (1.0 + jnp.exp(lbl_ref[1:2, :] - lbl_ref[0:1, :]))
        nw = nw_ref[...]
        lower = _tri(True)
        ltri = lower.astype(F32)
        utri = _tri(False).astype(F32)
        last = lax.broadcasted_iota(jnp.int32, (CH, DH), 0) == CH - 1
        ds_ref[...] = jnp.zeros_like(ds_ref)
        acc_ref[...] = jnp.zeros_like(acc_ref)

        @pl.when(h == 0)
        def _():
            dnw_ref[...] = jnp.zeros_like(dnw_ref)

        def chunk(i, carry):
            n = NCH - 1 - i
            rows = pl.ds(pl.multiple_of(n * CH, CH), CH)
            hq, hf, v, hg = hq_ref[rows, :], hf_ref[rows, :], hi_ref[rows, :], hg_ref[rows, :]
            sig, f, g, kk, q, sq = _hgrn_gates(hq, hf, lb)
            gc = _dot_f32(ltri, g)
            gl = jnp.sum(g, axis=0, keepdims=True)
            eg = jnp.exp(gc)
            ek = jnp.exp(gl - gc)
            gam = jnp.exp(gl)
            ph = jnp.exp(gc - 0.5 * gl)
            pk = jnp.exp(0.5 * gl - gc)
            qe, ke = q * eg, kk * ek
            qh, kh = q * ph, kk * pk
            st = st_ref[0, pl.ds(n, 1)][0]
            dst = ds_ref[...]
            a = jnp.where(lower, _dot(qh, kh, NT), 0.0)
            o = oraw_ref[rows, :]
            rs = lax.rsqrt(jnp.mean(o * o, axis=-1, keepdims=True) + EPS)
            xh = o * rs
            sg = _sigmoid(hg)
            dgo = do_ref[rows, :]
            d_on = dgo * (hg * sg)
            dhg_ref[rows, :] = (dgo * xh * nw * (sg * (1.0 + hg * (1.0 - sg)))).astype(BF16)
            acc_ref[1:2, :] += jnp.sum(d_on * xh, axis=0, keepdims=True)
            dy = d_on * nw
            do = rs * (dy - xh * jnp.mean(dy * xh, axis=-1, keepdims=True))
            dqe = _dot(do, st)
            da = jnp.where(lower, _dot(do, v, NT), 0.0)
            dv = _dot(a, do, TN) + _dot(ke, dst, NT)
            dke = _dot(v, dst)
            dgam = jnp.sum(dst * st, axis=0, keepdims=True)
            dqh = _dot(da, kh)
            dkh = _dot(da, qh, TN)
            dq = dqh * ph + dqe * eg
            dk = dkh * pk + dke * ek
            dgl = jnp.sum(dke * ke, axis=0, keepdims=True) + dgam * gam
            dgc = dqh * qh - dkh * kh + dqe * qe - dke * ke + jnp.where(last, dgl, 0.0)
            dg = _dot_f32(utri, dgc)
            df = dg / f - dk
            dhf_ref[rows, :] = (df * (1.0 - lb) * sig * (1.0 - sig)).astype(BF16)
            acc_ref[0:1, :] += jnp.sum(df * (1.0 - sig), axis=0, keepdims=True)
            dhq_ref[rows, :] = (dq * (sq * (1.0 + hq * (1.0 - sq)))).astype(BF16)
            dhi_ref[rows, :] = dv.astype(BF16)
            ds_ref[...] = dst * gam + _dot(do, qe, TN)
            return carry

        lax.fori_loop(0, NCH, chunk, 0)
        dlb_ref[...] = acc_ref[0:1, :]
        dnw_ref[...] += acc_ref[1:2, :]

    head_spec = pl.BlockSpec((S, DH), lambda h: (0, h))
    head_out = jax.ShapeDtypeStruct((S, HW), BF16)
    return pl.pallas_call(
        body, name="hgrn_bwd",
        out_shape=(head_out, head_out, head_out, head_out,
                   jax.ShapeDtypeStruct((1, HW), F32), jax.ShapeDtypeStruct((1, DH), F32)),
        grid=(H,),
        in_specs=[_proj_col_spec(0), _proj_col_spec(1), _proj_col_spec(2), _proj_col_spec(3),
                  pl.BlockSpec((2, DH), lambda h: (0, h)), pl.BlockSpec((1, DH), lambda h: (0, 0)),
                  head_spec, pl.BlockSpec((1, NCH, DH, DH), lambda h: (h, 0, 0, 0)), head_spec],
        out_specs=(head_spec, head_spec, head_spec, head_spec,
                   pl.BlockSpec((1, DH), lambda h: (0, h)), pl.BlockSpec((1, DH), lambda h: (0, 0))),
        scratch_shapes=[pltpu.VMEM((DH, DH), F32), pltpu.VMEM((8, DH), F32)],
        compiler_params=_params(1),
    )(proj, proj, proj, proj, lb_logits, norm_w, oraw, states, dout)


ATT_SCALE = DH ** -0.5
HEADS_PER_STEP = {1: 8, 4: 1, 16: 1}


def _sub_rows(ref, r, dil, cols):
    if dil == 1:
        return ref[:, cols]
    return ref[pl.ds(r, BLK, stride=dil), cols]


def _set_sub_rows(ref, r, dil, cols, val):
    if dil == 1:
        ref[:, cols] = val
    else:
        ref[pl.ds(r, BLK, stride=dil), cols] = val


def _slopes(dil):
    hp = HEADS_PER_STEP[dil]
    s = jnp.asarray([dil * 2.0 ** (-(h + 1)) for h in range(H)], F32)
    return jnp.broadcast_to(s[:, None], (H, DH)).reshape(H // hp, hp, DH)


def _rms_rows(x, w):
    rs = lax.rsqrt(jnp.mean(x * x, axis=-1, keepdims=True) + EPS)
    return x * rs, rs


def _att_masks():
    qi = lax.broadcasted_iota(jnp.int32, (BLK, BLK), 0)
    kj = lax.broadcasted_iota(jnp.int32, (BLK, BLK), 1)
    steps_c = qi - kj
    steps_p = qi - kj + BLK
    return steps_c, steps_p, steps_c >= 0, steps_p <= BLK


def _attn_fwd(proj, q_w, k_w, dil):
    hp = HEADS_PER_STEP[dil]
    rows, ng, nb = BLK * dil, H // hp, S // (BLK * dil)

    def body(q_ref, kc_ref, kp_ref, vc_ref, vp_ref, qw_ref, kw_ref, sl_ref, o_ref, lse_ref):
        n = pl.program_id(0)
        steps_c, steps_p, ok_c, ok_p = _att_masks()
        ok_p = jnp.logical_and(ok_p, n > 0)
        fc, fp = steps_c.astype(F32), steps_p.astype(F32)
        qw, kw = qw_ref[...], kw_ref[...]
        for hh in range(hp):
            cols = slice(hh * DH, (hh + 1) * DH)
            sl = sl_ref[0, hh:hh + 1, :]
            for r in range(dil):
                qn = _rms_rows(_sub_rows(q_ref, r, dil, cols), None)[0] * qw
                kcn = _rms_rows(_sub_rows(kc_ref, r, dil, cols), None)[0] * kw
                kpn = _rms_rows(_sub_rows(kp_ref, r, dil, cols), None)[0] * kw
                sc = jnp.where(ok_c, _dot(qn, kcn, NT) * ATT_SCALE - sl * fc, NEG)
                sp = jnp.where(ok_p, _dot(qn, kpn, NT) * ATT_SCALE - sl * fp, NEG)
                m = jnp.maximum(jnp.max(sc, axis=-1, keepdims=True), jnp.max(sp, axis=-1, keepdims=True))
                pc, pp = jnp.exp(sc - m), jnp.exp(sp - m)
                den = jnp.sum(pc, axis=-1, keepdims=True) + jnp.sum(pp, axis=-1, keepdims=True)
                o = (_dot(pc, _sub_rows(vc_ref, r, dil, cols)) + _dot(pp, _sub_rows(vp_ref, r, dil, cols))) / den
                _set_sub_rows(o_ref, r, dil, cols, o)
                _set_sub_rows(lse_ref, r, dil, cols, jnp.broadcast_to(m + jnp.log(den), (BLK, DH)))

    def spec(group, prev):
        if prev:
            return pl.BlockSpec((rows, hp * DH), lambda n, g: (jnp.maximum(n - 1, 0), group * ng + g))
        return pl.BlockSpec((rows, hp * DH), lambda n, g: (n, group * ng + g))

    out = jax.ShapeDtypeStruct((S, HW), F32)
    out_spec = pl.BlockSpec((rows, hp * DH), lambda n, g: (n, g))
    wspec = pl.BlockSpec((1, DH), lambda n, g: (0, 0))
    return pl.pallas_call(
        body, name=f"attn_fwd_d{dil}", out_shape=(out, out), grid=(nb, ng),
        in_specs=[spec(4, False), spec(5, False), spec(5, True), spec(6, False), spec(6, True), wspec, wspec,
                  pl.BlockSpec((1, hp, DH), lambda n, g: (g, 0, 0))],
        out_specs=(out_spec, out_spec),
        compiler_params=_params(2),
    )(proj, proj, proj, proj, proj, q_w, k_w, _slopes(dil))


def _attn_merge(outs, lses, cat):
    def body(o1, o2, o3, l1, l2, l3, cat_ref, ob_ref, of_ref, lse_ref):
        a, b, c = l1[...], l2[...], l3[...]
        m = jnp.maximum(jnp.maximum(a, b), c)
        ea, eb, ec = jnp.exp(a - m), jnp.exp(b - m), jnp.exp(c - m)
        den = ea + eb + ec
        o = (ea * o1[...] + eb * o2[...] + ec * o3[...]) / den
        ob_ref[...] = o.astype(BF16)
        of_ref[...] = o
        lse_ref[...] = m + jnp.log(den)

    f = jax.ShapeDtypeStruct((S, HW), F32)
    return pl.pallas_call(
        body, name="attn_merge", out_shape=(jax.ShapeDtypeStruct((S, 2 * HW), BF16), f, f), grid=(S // TR,),
        in_specs=[_row_spec(HW)] * 6 + [pl.BlockSpec(memory_space=pl.ANY)],
        out_specs=(pl.BlockSpec((TR, HW), lambda i: (i, 1)), _row_spec(HW), _row_spec(HW)),
        input_output_aliases={6: 0},
        compiler_params=_params(1),
    )(*outs, *lses, cat)


def _attn_bwd(proj, q_w, k_w, o, lse, do, dil, acc):
    hp = HEADS_PER_STEP[dil]
    rows, ng, nb = BLK * dil, H // hp, S // (BLK * dil)
    has_acc = acc is not None

    def body(*refs):
        (q_ref, qn_ref, kp_ref, kc_ref, vp_ref, vc_ref, o_ref, on_ref, l_ref, ln_ref, do_ref, don_ref,
         qw_ref, kw_ref, sl_ref) = refs[:15]
        refs = refs[15:]
        if has_acc:
            aq_ref, ak_ref, av_ref, aqw_ref, akw_ref = refs[:5]
            refs = refs[5:]
        dq_ref, dk_ref, dv_ref, dqw_ref, dkw_ref = refs
        m, g = pl.program_id(0), pl.program_id(1)
        steps_c, steps_p, ok_c, ok_p = _att_masks()
        ok_prev = jnp.logical_and(ok_p, m > 0)
        ok_next = jnp.logical_and(ok_p, m < nb - 1)
        fc, fp = steps_c.astype(F32), steps_p.astype(F32)
        qw, kw = qw_ref[...], kw_ref[...]

        @pl.when(jnp.logical_and(m == 0, g == 0))
        def _():
            if has_acc:
                dqw_ref[...] = aqw_ref[...]
                dkw_ref[...] = akw_ref[...]
            else:
                dqw_ref[...] = jnp.zeros_like(dqw_ref)
                dkw_ref[...] = jnp.zeros_like(dkw_ref)

        for hh in range(hp):
            cols = slice(hh * DH, (hh + 1) * DH)
            sl = sl_ref[0, hh:hh + 1, :]
            for r in range(dil):
                sub = lambda ref: _sub_rows(ref, r, dil, cols)
                qx, qrs = _rms_rows(sub(q_ref), None)
                kx, krs = _rms_rows(sub(kc_ref), None)
                qn, kcn = qx * qw, kx * kw
                qnn = _rms_rows(sub(qn_ref), None)[0] * qw
                kpn = _rms_rows(sub(kp_ref), None)[0] * kw
                vc, vp = sub(vc_ref), sub(vp_ref)
                dov, donv = sub(do_ref), sub(don_ref)
                lse_m, lse_n = sub(l_ref)[:, 0:1], sub(ln_ref)[:, 0:1]
                delta_m = jnp.sum(dov * sub(o_ref), axis=-1, keepdims=True)
                delta_n = jnp.sum(donv * sub(on_ref), axis=-1, keepdims=True)
                p_c = jnp.where(ok_c, jnp.exp(_dot(qn, kcn, NT) * ATT_SCALE - sl * fc - lse_m), 0.0)
                p_p = jnp.where(ok_prev, jnp.exp(_dot(qn, kpn, NT) * ATT_SCALE - sl * fp - lse_m), 0.0)
                p_n = jnp.where(ok_next, jnp.exp(_dot(qnn, kcn, NT) * ATT_SCALE - sl * fp - lse_n), 0.0)
                ds_c = p_c * (_dot(dov, vc, NT) - delta_m)
                ds_p = p_p * (_dot(dov, vp, NT) - delta_m)
                ds_n = p_n * (_dot(donv, vc, NT) - delta_n)
                dqn = (_dot(ds_c, kcn) + _dot(ds_p, kpn)) * ATT_SCALE
                dkn = (_dot(ds_c, qn, TN) + _dot(ds_n, qnn, TN)) * ATT_SCALE
                dv = _dot(p_c, dov, TN) + _dot(p_n, donv, TN)
                dqw_ref[...] += jnp.sum(dqn * qx, axis=0, keepdims=True)
                dkw_ref[...] += jnp.sum(dkn * kx, axis=0, keepdims=True)
                dyq, dyk = dqn * qw, dkn * kw
                daq = qrs * (dyq - qx * jnp.mean(dyq * qx, axis=-1, keepdims=True))
                dak = krs * (dyk - kx * jnp.mean(dyk * kx, axis=-1, keepdims=True))
                if has_acc:
                    daq, dak, dv = daq + sub(aq_ref), dak + sub(ak_ref), dv + sub(av_ref)
                _set_sub_rows(dq_ref, r, dil, cols, daq)
                _set_sub_rows(dk_ref, r, dil, cols, dak)
                _set_sub_rows(dv_ref, r, dil, cols, dv)

    def shifted(shift, m):
        if shift < 0:
            return jnp.maximum(m - 1, 0)
        if shift > 0:
            return jnp.minimum(m + 1, nb - 1)
        return m

    def pspec(group, shift):
        return pl.BlockSpec((rows, hp * DH), lambda m, g: (shifted(shift, m), group * ng + g))

    cur = pl.BlockSpec((rows, hp * DH), lambda m, g: (m, g))
    nxt = pl.BlockSpec((rows, hp * DH), lambda m, g: (shifted(1, m), g))
    wspec = pl.BlockSpec((1, DH), lambda m, g: (0, 0))
    ins = [proj, proj, proj, proj, proj, proj, o, o, lse, lse, do, do, q_w, k_w, _slopes(dil)]
    in_specs = [pspec(4, 0), pspec(4, 1), pspec(5, -1), pspec(5, 0), pspec(6, -1), pspec(6, 0),
                cur, nxt, cur, nxt, pspec(1, 0), pspec(1, 1), wspec, wspec,
                pl.BlockSpec((1, hp, DH), lambda m, g: (g, 0, 0))]
    if has_acc:
        ins += list(acc)
        in_specs += [cur, cur, cur, wspec, wspec]
    big = jax.ShapeDtypeStruct((S, HW), F32)
    small = jax.ShapeDtypeStruct((1, DH), F32)
    return pl.pallas_call(
        body, name=f"attn_bwd_d{dil}", out_shape=(big, big, big, small, small), grid=(nb, ng),
        in_specs=in_specs, out_specs=(cur, cur, cur, wspec, wspec),
        compiler_params=_params(2),
    )(*ins)


TC = 512
NCT = DFF // TC


def _shift_rows(a, k):
    rows = lax.broadcasted_iota(jnp.int32, a.shape, 0)
    rolled = pltpu.roll(a, k % S, 0)
    if k > 0:
        return jnp.where(rows >= k, rolled, 0.0)
    return jnp.where(rows < S + k, rolled, 0.0)


def _conv_pre(a, w_ref, b_ref):
    return b_ref[...] + w_ref[0:1, :] * _shift_rows(a, 2) + w_ref[1:2, :] * _shift_rows(a, 1) + w_ref[2:3, :] * a


def _conv_gate_fwd(u, conv_w, conv_b):
    def body(a_ref, g_ref, w_ref, b_ref, y_ref):
        pre = _conv_pre(a_ref[...], w_ref, b_ref)
        y_ref[...] = (pre * _sigmoid(pre) * g_ref[...]).astype(BF16)

    return pl.pallas_call(
        body, name="conv_gate_fwd", out_shape=jax.ShapeDtypeStruct((S, DFF), BF16), grid=(NCT,),
        in_specs=[pl.BlockSpec((S, TC), lambda i: (0, i)), pl.BlockSpec((S, TC), lambda i: (0, NCT + i)),
                  pl.BlockSpec((3, TC), lambda i: (0, i)), pl.BlockSpec((1, TC), lambda i: (0, i))],
        out_specs=pl.BlockSpec((S, TC), lambda i: (0, i)),
        compiler_params=_params(1),
    )(u, u, conv_w, conv_b)


def _conv_gate_bwd(dy, u, conv_w, conv_b):
    def body(dy_ref, a_ref, g_ref, w_ref, b_ref, da_ref, dg_ref, db_ref, dw_ref):
        a = a_ref[...]
        dyv = dy_ref[...]
        pre = _conv_pre(a, w_ref, b_ref)
        sg = _sigmoid(pre)
        dg_ref[...] = (dyv * pre * sg).astype(BF16)
        dpre = dyv * g_ref[...] * (sg * (1.0 + pre * (1.0 - sg)))
        da = w_ref[2:3, :] * dpre + w_ref[1:2, :] * _shift_rows(dpre, -1) + w_ref[0:1, :] * _shift_rows(dpre, -2)
        da_ref[...] = da.astype(BF16)
        db_ref[...] = jnp.sum(dpre, axis=0, keepdims=True)
        dw_ref[0:1, :] = jnp.sum(dpre * _shift_rows(a, 2), axis=0, keepdims=True)
        dw_ref[1:2, :] = jnp.sum(dpre * _shift_rows(a, 1), axis=0, keepdims=True)
        dw_ref[2:3, :] = jnp.sum(dpre * a, axis=0, keepdims=True)

    col = pl.BlockSpec((S, TC), lambda i: (0, i))
    half = jax.ShapeDtypeStruct((S, DFF), BF16)
    return pl.pallas_call(
        body, name="conv_gate_bwd",
        out_shape=(half, half, jax.ShapeDtypeStruct((1, DFF), F32), jax.ShapeDtypeStruct((3, DFF), F32)),
        grid=(NCT,),
        in_specs=[col, col, pl.BlockSpec((S, TC), lambda i: (0, NCT + i)),
                  pl.BlockSpec((3, TC), lambda i: (0, i)), pl.BlockSpec((1, TC), lambda i: (0, i))],
        out_specs=(col, col, pl.BlockSpec((1, TC), lambda i: (0, i)), pl.BlockSpec((3, TC), lambda i: (0, i))),
        compiler_params=_params(1),
    )(dy, u, u, conv_w, conv_b)


def _out_loss(z, x1, target, gate):
    def body(z_ref, x_ref, t_ref, g_ref, do_ref, dz_ref, dg_ref, loss_ref):
        i = pl.program_id(0)
        zv = z_ref[...]
        gv = g_ref[...]
        err = x_ref[...] + gv * zv - t_ref[...]
        dout = err * (1.0 / D)
        do_ref[...] = dout
        dz_ref[...] = (dout * gv).astype(BF16)

        @pl.when(i == 0)
        def _():
            dg_ref[...] = jnp.zeros_like(dg_ref)
            loss_ref[...] = jnp.zeros_like(loss_ref)

        dg_ref[...] += jnp.sum(dout * zv, axis=0, keepdims=True)
        part = jnp.sum(jnp.sum(err * err, axis=0, keepdims=True), axis=-1, keepdims=True) * (0.5 / D)
        lane = lax.broadcasted_iota(jnp.int32, (1, 128), 1)
        loss_ref[...] += jnp.where(lane == 0, part, 0.0)

    return pl.pallas_call(
        body, name="out_loss",
        out_shape=(jax.ShapeDtypeStruct((S, D), F32), jax.ShapeDtypeStruct((S, D), BF16),
                   jax.ShapeDtypeStruct((1, D), F32), jax.ShapeDtypeStruct((1, 128), F32)),
        grid=(S // TR,),
        in_specs=[_row_spec(), _row_spec(), _row_spec(), _vec_spec()],
        out_specs=(_row_spec(), _row_spec(), _vec_spec(), _vec_spec(128)),
        compiler_params=_params(1),
    )(z, x1, target, gate)


ADA_COLS = 6 * D // N_CHIPS
TA = 512


def _ada_fwd(c_all, w_shard, b_shard):
    def body(c_ref, w_ref, b_ref, o_ref):
        cv = c_ref[...]
        o_ref[...] = _dot_f32(cv * _sigmoid(cv), w_ref[...]) + b_ref[...]

    return pl.pallas_call(
        body, name="ada_fwd", out_shape=jax.ShapeDtypeStruct((N_DEV, ADA_COLS), F32), grid=(ADA_COLS // TA,),
        in_specs=[pl.BlockSpec((N_DEV, D), lambda j: (0, 0)), pl.BlockSpec((D, TA), lambda j: (0, j)),
                  pl.BlockSpec((1, TA), lambda j: (0, j))],
        out_specs=pl.BlockSpec((N_DEV, TA), lambda j: (0, j)),
        compiler_params=_params(1),
    )(c_all, w_shard, b_shard)


def _ada_bwd(c_all, dmod_shard):
    def body(c_ref, d_ref, o_ref):
        cv = c_ref[...]
        o_ref[...] = lax.dot_general(cv * _sigmoid(cv), d_ref[...], TN, precision=lax.Precision.HIGHEST,
                                     preferred_element_type=F32)

    return pl.pallas_call(
        body, name="ada_bwd", out_shape=jax.ShapeDtypeStruct((D, ADA_COLS), F32), grid=(ADA_COLS // TA,),
        in_specs=[pl.BlockSpec((N_DEV, D), lambda j: (0, 0)), pl.BlockSpec((N_DEV, TA), lambda j: (0, j))],
        out_specs=pl.BlockSpec((D, TA), lambda j: (0, j)),
        compiler_params=_params(1),
    )(c_all, dmod_shard)


def _sum_rows(g, name):
    rows, n = g.shape

    def body(g_ref, o_ref):
        acc = g_ref[0:1, :]
        for i in range(1, rows):
            acc = acc + g_ref[i:i + 1, :]
        o_ref[...] = acc

    return pl.pallas_call(
        body, name=name, out_shape=jax.ShapeDtypeStruct((1, n), F32),
        in_specs=[VMEM_SPEC], out_specs=VMEM_SPEC,
    )(g)


def _lb_grad(lb_logits, dlb):
    def body(l_ref, d_ref, o_ref):
        p = 1.0 / (1.0 + jnp.exp(l_ref[1:2, :] - l_ref[0:1, :]))
        t = d_ref[...] * p * (1.0 - p)
        o_ref[0:1, :] = t
        o_ref[1:2, :] = -t

    return pl.pallas_call(
        body, name="lb_grad", out_shape=jax.ShapeDtypeStruct((2, HW), F32),
        in_specs=[VMEM_SPEC, VMEM_SPEC], out_specs=VMEM_SPEC,
    )(lb_logits, dlb)


def _adamw(w, g, m, v, name):
    rows, cols = w.shape
    tr = rows
    if rows * cols * 4 > ADAM_BLOCK_BYTES:
        tr = _pick(rows, [t for t in (256, 128, 64, 32, 16, 8) if t * cols * 4 <= ADAM_BLOCK_BYTES])

    def body(w_ref, g_ref, m_ref, v_ref, d_ref, mo_ref, vo_ref):
        gv = g_ref[...]
        m2 = ADAM_B1 * m_ref[...] + (1.0 - ADAM_B1) * gv
        v2 = ADAM_B2 * v_ref[...] + (1.0 - ADAM_B2) * (gv * gv)
        m_hat = m2 / (1.0 - ADAM_B1 ** ADAM_STEP)
        v_hat = v2 / (1.0 - ADAM_B2 ** ADAM_STEP)
        d_ref[...] = -ADAM_LR * (m_hat / (jnp.sqrt(v_hat) + ADAM_EPS) + ADAM_WD * w_ref[...])
        mo_ref[...] = m2
        vo_ref[...] = v2

    spec = pl.BlockSpec((tr, cols), lambda i: (i, 0))
    out = jax.ShapeDtypeStruct((rows, cols), F32)
    return pl.pallas_call(
        body, name=name, out_shape=(out, out, out), grid=(rows // tr,),
        in_specs=[spec] * 4, out_specs=(spec,) * 3,
        compiler_params=_params(1),
    )(w, g, m, v)


def _sequence_step(x, target, mod, norm1_w, w_in, lb_logits, hg_norm_w, q_norm_w, k_norm_w, w_out,
                   norm2_w, w_up, conv_w, conv_b, w_down):
    shift1, scale1, gate1, shift2, scale2, gate2 = [mod[:, i * D:(i + 1) * D] for i in range(6)]

    h = _norm_mod(x, norm1_w, scale1, shift1, "norm1_fwd")
    proj = _matmul(h, w_in, "nn", F32, "proj_fwd")
    cat, o_raw, states = _hgrn_fwd(proj, lb_logits, hg_norm_w)
    att = [_attn_fwd(proj, q_norm_w, k_norm_w, dil) for dil in DILS]
    cat, b_out_f32, lse = _attn_merge([t[0] for t in att], [t[1] for t in att], cat)
    mix = _matmul(cat, w_out, "nn", F32, "mix_fwd")
    x1, h2 = _resid_norm_mod(x, mix, gate1, norm2_w, scale2, shift2, "norm2_fwd")
    u = _matmul(h2, w_up, "nn", F32, "up_fwd")
    yact = _conv_gate_fwd(u, conv_w, conv_b)
    z = _matmul(yact, w_down, "nn", F32, "down_fwd")
    dout, dz, dgate2, loss_row = _out_loss(z, x1, target, gate2)

    dyact = _matmul(dz, w_down, "nt", BF16, "down_bwd_x")
    g_down = _matmul(yact, dz, "tn", BF16, "down_bwd_w")
    da, dg, dconv_b, dconv_w = _conv_gate_bwd(dyact, u, conv_w, conv_b)
    du = jnp.concatenate([da, dg], axis=1)
    dh2 = _matmul(du, w_up, "nt", F32, "up_bwd_x")
    g_up = _matmul(h2, du, "tn", BF16, "up_bwd_w")
    dx1, dshift2, dscale2, dnorm2, dgate1, dmix = _norm_mod_bwd(
        dh2, x1, norm2_w, scale2, dout, "norm2_bwd", mix=mix, gate=gate1)
    dcat = _matmul(dmix, w_out, "nt", F32, "mix_bwd_x")
    g_out = _matmul(cat, dmix, "tn", BF16, "mix_bwd_w")
    dhq, dhf, dhi, dhg, dlb, dhg_norm = _hgrn_bwd(proj, lb_logits, hg_norm_w, o_raw, states, dcat)
    acc = None
    for dil in DILS:
        acc = _attn_bwd(proj, q_norm_w, k_norm_w, b_out_f32, lse, dcat, dil, acc)
    daq, dak, dav, dq_norm, dk_norm = acc
    dproj = jnp.concatenate([dhq, dhf, dhi, dhg, daq.astype(BF16), dak.astype(BF16), dav.astype(BF16)], axis=1)
    dh = _matmul(dproj, w_in, "nt", F32, "proj_bwd_x")
    g_in = _matmul(h, dproj, "tn", BF16, "proj_bwd_w")
    grad_x, dshift1, dscale1, dnorm1 = _norm_mod_bwd(dh, x, norm1_w, scale1, dx1, "norm1_bwd")

    dmod = jnp.concatenate([dshift1, dscale1, dgate1, dshift2, dscale2, dgate2], axis=1)
    small = dict(norm1_w=dnorm1, lb=dlb, hg_norm_w=dhg_norm, q_norm_w=dq_norm, k_norm_w=dk_norm,
                 norm2_w=dnorm2, conv_b=dconv_b, conv_w=dconv_w)
    return loss_row, grad_x, dmod, (g_in, g_out, g_up, g_down), small


def _place():
    x, y, c = lax.axis_index("x"), lax.axis_index("y"), lax.axis_index("c")
    others = [(1 - x, y), (x, 1 - y), (1 - x, 1 - y)]
    return x, y, c, others


def _remote(src, dst, send_sems, recv_sems, k, to):
    return pltpu.make_async_remote_copy(src_ref=src, dst_ref=dst, send_sem=send_sems.at[k], recv_sem=recv_sems.at[k],
                                        device_id=to, device_id_type=MESH)


def _all_gather_rows(block, name):
    m_per, n = block.shape

    def body(x_ref, out_ref, send_sems, recv_sems, local_sem):
        x, y, c, chips = _place()
        me, sibling = (x, y, c), (x, y, 1 - c)

        def rows(px, py, pc):
            return out_ref.at[pl.ds((4 * px + 2 * py + pc) * m_per, m_per), :]

        def copy(k, blk, to, src=None):
            return _remote(rows(*blk) if src is None else src, rows(*blk), send_sems, recv_sems, k, to)

        mine = pltpu.make_async_copy(x_ref, rows(*me), local_sem)
        mine.start()
        first = [copy(0, me, sibling, src=x_ref)]
        first += [copy(1 + j, me, (*chip, c), src=x_ref) for j, chip in enumerate(chips)]
        for cp in first:
            cp.start()
        passed = [copy(4 + j, (*chip, c), sibling) for j, chip in enumerate(chips)]
        for j, chip in enumerate(chips):
            copy(1 + j, (*chip, c), me).wait_recv()
            passed[j].start()
        copy(0, sibling, me).wait_recv()
        for j, chip in enumerate(chips):
            copy(4 + j, (*chip, 1 - c), me).wait_recv()
        for cp in first + passed:
            cp.wait_send()
        mine.wait()

    return pl.pallas_call(
        body, name=name, out_shape=jax.ShapeDtypeStruct((N_DEV * m_per, n), block.dtype),
        in_specs=[VMEM_SPEC], out_specs=VMEM_SPEC,
        scratch_shapes=[pltpu.SemaphoreType.DMA((7,)), pltpu.SemaphoreType.DMA((7,)), pltpu.SemaphoreType.DMA],
    )(block)


class _Geo:
    def __init__(self, kind, shard_shape):
        self.kind = kind
        self.shard_shape = tuple(shard_shape)
        self.hr, self.hc = shard_shape[0] // 2, shard_shape[1]
        if kind == "col":
            self.full_shape = (shard_shape[0], N_CHIPS * shard_shape[1])
        else:
            self.full_shape = (N_CHIPS * shard_shape[0], shard_shape[1])

    def full_shard(self, ref, j):
        if self.kind == "col":
            return ref.at[:, pl.ds(pl.multiple_of(j * self.hc, 128), self.hc)]
        return ref.at[pl.ds(pl.multiple_of(j * 2 * self.hr, 16), 2 * self.hr), :]

    def full_half(self, ref, j, c):
        if self.kind == "col":
            return ref.at[pl.ds(pl.multiple_of(c * self.hr, 16), self.hr),
                          pl.ds(pl.multiple_of(j * self.hc, 128), self.hc)]
        return ref.at[pl.ds(pl.multiple_of((2 * j + c) * self.hr, 16), self.hr), :]

    def shard_half(self, ref, c):
        return ref.at[pl.ds(pl.multiple_of(c * self.hr, 16), self.hr), :]


WEIGHT_KINDS = ("col", "row", "col", "row")
NW = len(WEIGHT_KINDS)


def _comm_call(body, name, ins, outs, n_remote, aliases=None):
    return pl.pallas_call(
        body, name=name, out_shape=tuple(outs),
        in_specs=[HBM_SPEC] * len(ins), out_specs=tuple([HBM_SPEC] * len(outs)),
        input_output_aliases=aliases or {},
        scratch_shapes=[pltpu.SemaphoreType.DMA((n_remote,)), pltpu.SemaphoreType.DMA((n_remote,))],
    )(*ins)


ROW_TILES = (256, 176, 128, 64, 32, 16)


def _cast_into_full(shard, geo, place, name):
    rs, cs = shard.shape
    tr = _pick(rs, ROW_TILES)
    nb = rs // tr

    def body(place_ref, s_ref, o_ref):
        o_ref[...] = s_ref[...].astype(BF16)

    if geo.kind == "col":
        out_map = lambda i, place_ref: (i, place_ref[0])
    else:
        out_map = lambda i, place_ref: (place_ref[0] * nb + i, 0)
    return pl.pallas_call(
        body, name=name, out_shape=jax.ShapeDtypeStruct(geo.full_shape, BF16),
        grid_spec=pltpu.PrefetchScalarGridSpec(
            num_scalar_prefetch=1, grid=(nb,),
            in_specs=[pl.BlockSpec((tr, cs), lambda i, place_ref: (i, 0))],
            out_specs=pl.BlockSpec((tr, cs), out_map)),
        compiler_params=_params(1),
    )(place, shard)


def _gather_weights(fulls, geos):
    def body(*refs):
        full = refs[NW:2 * NW]
        send_sems, recv_sems = refs[2 * NW:]
        x, y, c, chips = _place()
        j = 2 * x + y
        sibling = (x, y, 1 - c)
        first = []
        for w in range(NW):
            mine = geos[w].full_half(full[w], j, c)
            for k, chip in enumerate(chips):
                first.append(_remote(mine, mine, send_sems, recv_sems, 6 * w + k, (*chip, c)))
        for cp in first:
            cp.start()
        passed = []
        for w in range(NW):
            for k, (ox, oy) in enumerate(chips):
                landed = geos[w].full_half(full[w], 2 * ox + oy, c)
                _remote(landed, landed, send_sems, recv_sems, 6 * w + k, (ox, oy, c)).wait_recv()
                fwd = _remote(landed, landed, send_sems, recv_sems, 6 * w + 3 + k, sibling)
                fwd.start()
                passed.append(fwd)
        for w in range(NW):
            for k, (ox, oy) in enumerate(chips):
                other = geos[w].full_half(full[w], 2 * ox + oy, 1 - c)
                _remote(other, other, send_sems, recv_sems, 6 * w + 3 + k, sibling).wait_recv()
        for cp in first + passed:
            cp.wait_send()

    outs = [jax.ShapeDtypeStruct(g.full_shape, BF16) for g in geos]
    return _comm_call(body, "gather_weights", fulls, outs, 6 * NW, aliases={w: w for w in range(NW)})


def _swap_halves(grads, geos):
    def body(*refs):
        full, got = refs[:NW], refs[NW:2 * NW]
        send_sems, recv_sems = refs[2 * NW:]
        x, y, c, _ = _place()
        sibling = (x, y, 1 - c)
        copies = []
        for w in range(NW):
            for j in range(N_CHIPS):
                copies.append(_remote(geos[w].full_half(full[w], j, 1 - c), got[w].at[j], send_sems, recv_sems,
                                      N_CHIPS * w + j, sibling))
        for cp in copies:
            cp.start()
        for cp in copies:
            cp.wait()

    outs = [jax.ShapeDtypeStruct((N_CHIPS, g.hr, g.hc), BF16) for g in geos]
    return _comm_call(body, "grad_swap_halves", grads, outs, N_CHIPS * NW)


def _add_pair(grad, got, geo, place, name):
    tr = _pick(geo.hr, ROW_TILES)
    nb = geo.hr // tr

    def body(place_ref, a_ref, b_ref, o_ref):
        o_ref[0] = (a_ref[...].astype(F32) + b_ref[0].astype(F32)).astype(BF16)

    if geo.kind == "col":
        own_map = lambda j, i, place_ref: (place_ref[1] * nb + i, j)
    else:
        own_map = lambda j, i, place_ref: ((2 * j + place_ref[1]) * nb + i, 0)
    spec = pl.BlockSpec((1, tr, geo.hc), lambda j, i, place_ref: (j, i, 0))
    return pl.pallas_call(
        body, name=name, out_shape=jax.ShapeDtypeStruct((N_CHIPS, geo.hr, geo.hc), BF16),
        grid_spec=pltpu.PrefetchScalarGridSpec(
            num_scalar_prefetch=1, grid=(N_CHIPS, nb),
            in_specs=[pl.BlockSpec((tr, geo.hc), own_map), spec], out_specs=spec),
        compiler_params=_params(2),
    )(place, grad, got)


def _scatter_to_chips(pairs):
    def body(*refs):
        src, dst = refs[:NW], refs[NW:2 * NW]
        send_sems, recv_sems = refs[2 * NW:]
        x, y, c, chips = _place()
        copies = []
        for w in range(NW):
            for k, (ox, oy) in enumerate(chips):
                copies.append(_remote(src[w].at[2 * ox + oy], dst[w].at[k], send_sems, recv_sems, 3 * w + k, (ox, oy, c)))
        for cp in copies:
            cp.start()
        for cp in copies:
            cp.wait()

    outs = [jax.ShapeDtypeStruct((3,) + p.shape[1:], BF16) for p in pairs]
    return _comm_call(body, "grad_scatter_chips", pairs, outs, 3 * NW)


def _add_four(pair, recv, geo, place, name):
    tr = _pick(geo.hr, ROW_TILES)
    nb = geo.hr // tr

    def body(place_ref, p_ref, r_ref, o_ref):
        o_ref[...] = ((p_ref[0].astype(F32) + r_ref[0].astype(F32)) + r_ref[1].astype(F32)) + r_ref[2].astype(F32)

    return pl.pallas_call(
        body, name=name, out_shape=jax.ShapeDtypeStruct((2 * geo.hr, geo.hc), F32),
        grid_spec=pltpu.PrefetchScalarGridSpec(
            num_scalar_prefetch=1, grid=(nb,),
            in_specs=[pl.BlockSpec((1, tr, geo.hc), lambda i, place_ref: (place_ref[0], i, 0)),
                      pl.BlockSpec((3, tr, geo.hc), lambda i, place_ref: (0, i, 0))],
            out_specs=pl.BlockSpec((tr, geo.hc), lambda i, place_ref: (place_ref[1] * nb + i, 0))),
        compiler_params=_params(1),
    )(place, pair, recv)


def _join_halves(shards, geos):
    def body(*refs):
        dst = refs[NW:2 * NW]
        send_sems, recv_sems = refs[2 * NW:]
        x, y, c, _ = _place()
        copies = []
        for w in range(NW):
            mine = dst[w].at[pl.ds(pl.multiple_of(c * geos[w].hr, 8), geos[w].hr), :]
            copies.append(_remote(mine, mine, send_sems, recv_sems, w, (x, y, 1 - c)))
        for cp in copies:
            cp.start()
        for cp in copies:
            cp.wait_send()
        for w in range(NW):
            other = dst[w].at[pl.ds(pl.multiple_of((1 - c) * geos[w].hr, 8), geos[w].hr), :]
            _remote(other, other, send_sems, recv_sems, w, (x, y, 1 - c)).wait_recv()

    outs = [jax.ShapeDtypeStruct(s.shape, F32) for s in shards]
    return _comm_call(body, "grad_join_halves", shards, outs, NW, aliases={w: w for w in range(NW)})


def _reduce_scatter(grads, geos, place):
    got = _swap_halves(grads, geos)
    pairs = [_add_pair(grads[w], got[w], geos[w], place, f"grad_pair_sum_{w}") for w in range(NW)]
    recv = _scatter_to_chips(pairs)
    shards = [_add_four(pairs[w], recv[w], geos[w], place, f"grad_chip_sum_{w}") for w in range(NW)]
    return _join_halves(shards, geos)


CONVW_SHARD = 3 * DFF // N_CHIPS
COND_PAD = 8 * 896
SMALL_SIZES = (("norm1_w", D), ("lb", HW), ("hg_norm_w", DH), ("q_norm_w", DH), ("k_norm_w", DH),
               ("norm2_w", D), ("conv_b", DFF), ("conv_w", 3 * DFF), ("loss", 128))
SMALL_TOTAL = sum(n for _, n in SMALL_SIZES)
STATS_PAD = 8 * 5120


def kernel(x, c, w_ada, b_ada, norm1_w, w_in, lb_logits, hg_norm_w, q_norm_w, k_norm_w, w_out, norm2_w, w_up, conv_w, conv_b, w_down, loss_target, m_w_ada, m_b_ada, m_norm1_w, m_w_in, m_lb_logits, m_hg_norm_w, m_q_norm_w, m_k_norm_w, m_w_out, m_norm2_w, m_w_up, m_conv_w, m_conv_b, m_w_down, v_w_ada, v_b_ada, v_norm1_w, v_w_in, v_lb_logits, v_hg_norm_w, v_q_norm_w, v_k_norm_w, v_w_out, v_norm2_w, v_w_up, v_conv_w, v_conv_b, v_w_down):
    chip = 2 * lax.axis_index("x") + lax.axis_index("y")
    dev = 2 * chip + lax.axis_index("c")

    cond = jnp.concatenate([c, conv_w[0].reshape(1, CONVW_SHARD), jnp.zeros((1, COND_PAD - D - CONVW_SHARD), F32)], axis=1)
    cond_all = _all_gather_rows(cond.reshape(8, COND_PAD // 8), "gather_cond").reshape(N_DEV, COND_PAD)
    c_all = cond_all[:, :D]
    conv_w_full = jnp.concatenate(
        [cond_all[2 * j, D:D + CONVW_SHARD].reshape(3, DFF // N_CHIPS) for j in range(N_CHIPS)], axis=1)

    b_shard = lax.dynamic_slice_in_dim(b_ada, chip * ADA_COLS, ADA_COLS, axis=1)
    mod_cols = _all_gather_rows(_ada_fwd(c_all, w_ada[0], b_shard), "gather_mod")
    mod_all = jnp.concatenate([mod_cols[16 * j:16 * j + 8] for j in range(N_CHIPS)], axis=1)
    mod = lax.dynamic_slice_in_dim(mod_all, dev, 1, axis=0)

    place = jnp.stack([chip, lax.axis_index("c")]).astype(jnp.int32)
    shards = [w_in[0], w_out[0], w_up[0], w_down[0]]
    geos = [_Geo(k, s.shape) for k, s in zip(WEIGHT_KINDS, shards)]
    w_full = _gather_weights([_cast_into_full(s, g, place, f"cast_shard_{i}") for i, (s, g) in enumerate(zip(shards, geos))], geos)
    loss_row, grad_x, dmod, g_full, small = _sequence_step(
        x[0], loss_target[0], mod, norm1_w, w_full[0], lb_logits, hg_norm_w, q_norm_w, k_norm_w, w_full[1],
        norm2_w, w_full[2], conv_w_full, conv_b, w_full[3])
    g_in, g_out, g_up, g_down = _reduce_scatter(list(g_full), geos, place)

    small["conv_w"] = small["conv_w"].reshape(1, 3 * DFF)
    small["loss"] = loss_row
    stats = jnp.concatenate([dmod] + [small[k] for k, _ in SMALL_SIZES]
                            + [jnp.zeros((1, STATS_PAD - 6 * D - SMALL_TOTAL), F32)], axis=1)
    stats_all = _all_gather_rows(stats.reshape(8, STATS_PAD // 8), "gather_stats").reshape(N_DEV, STATS_PAD)
    dmod_all = stats_all[:, :6 * D]
    sums = _sum_rows(stats_all[:, 6 * D:6 * D + SMALL_TOTAL], "small_grad_sum")
    g_small, off = {}, 0
    for k, n in SMALL_SIZES:
        g_small[k] = sums[:, off:off + n]
        off += n
    loss = g_small["loss"][0, 0]
    g_b_ada = _sum_rows(dmod_all, "b_ada_grad_sum")
    g_w_ada = _ada_bwd(c_all, lax.dynamic_slice_in_dim(dmod_all, chip * ADA_COLS, ADA_COLS, axis=1))
    g_lb = _lb_grad(lb_logits, g_small["lb"])
    g_conv_w = lax.dynamic_slice_in_dim(g_small["conv_w"].reshape(3, DFF), chip * (DFF // N_CHIPS), DFF // N_CHIPS, axis=1)

    names = ["w_ada", "b_ada", "norm1_w", "w_in", "lb_logits", "hg_norm_w", "q_norm_w", "k_norm_w", "w_out",
             "norm2_w", "w_up", "conv_w", "conv_b", "w_down"]
    lead = {"w_ada", "w_in", "w_out", "w_up", "conv_w", "w_down"}
    w = dict(w_ada=w_ada, b_ada=b_ada, norm1_w=norm1_w, w_in=w_in, lb_logits=lb_logits, hg_norm_w=hg_norm_w,
             q_norm_w=q_norm_w, k_norm_w=k_norm_w, w_out=w_out, norm2_w=norm2_w, w_up=w_up, conv_w=conv_w,
             conv_b=conv_b, w_down=w_down)
    m = dict(w_ada=m_w_ada, b_ada=m_b_ada, norm1_w=m_norm1_w, w_in=m_w_in, lb_logits=m_lb_logits,
             hg_norm_w=m_hg_norm_w, q_norm_w=m_q_norm_w, k_norm_w=m_k_norm_w, w_out=m_w_out, norm2_w=m_norm2_w,
             w_up=m_w_up, conv_w=m_conv_w, conv_b=m_conv_b, w_down=m_w_down)
    v = dict(w_ada=v_w_ada, b_ada=v_b_ada, norm1_w=v_norm1_w, w_in=v_w_in, lb_logits=v_lb_logits,
             hg_norm_w=v_hg_norm_w, q_norm_w=v_q_norm_w, k_norm_w=v_k_norm_w, w_out=v_w_out, norm2_w=v_norm2_w,
             w_up=v_w_up, conv_w=v_conv_w, conv_b=v_conv_b, w_down=v_w_down)
    g = dict(w_ada=g_w_ada, b_ada=g_b_ada, norm1_w=g_small["norm1_w"], w_in=g_in, lb_logits=g_lb,
             hg_norm_w=g_small["hg_norm_w"], q_norm_w=g_small["q_norm_w"], k_norm_w=g_small["k_norm_w"], w_out=g_out,
             norm2_w=g_small["norm2_w"], w_up=g_up, conv_w=g_conv_w, conv_b=g_small["conv_b"], w_down=g_down)

    grads, deltas, new_m, new_v = [], [], [], []
    for n in names:
        strip = (lambda t: t[0]) if n in lead else (lambda t: t)
        wrap = (lambda t: t[None]) if n in lead else (lambda t: t)
        d_n, m_n, v_n = _adamw(strip(w[n]), g[n], strip(m[n]), strip(v[n]), f"adamw_{n}")
        grads.append(wrap(g[n]))
        deltas.append(wrap(d_n))
        new_m.append(wrap(m_n))
        new_v.append(wrap(v_n))
    return (loss, grad_x[None], *grads, *deltas, *new_m, *new_v)
```

```python
import functools
import math

import jax
import jax.numpy as jnp
from jax import lax
from jax.experimental import pallas as pl
from jax.experimental.pallas import tpu as pltpu

F32 = jnp.float32
BF16 = jnp.bfloat16

S = 2048
D = 2048
H = 8
DH = 128
HW = H * DH
CH = 64
NCH = S // CH
DFF = 5632
INC = 7 * HW
BLK = 128
DILS = (1, 4, 16)
EPS = 1e-6
NEG = -1e30
N_CHIPS = 4
N_DEV = 8

ADAM_LR = 0.001
ADAM_B1 = 0.9
ADAM_B2 = 0.999
ADAM_EPS = 1e-08
ADAM_WD = 0.01
ADAM_STEP = 10
ADAM_BLOCK_BYTES = 1 << 20

V7X_VMEM_BYTES = 64 * 1024 * 1024
VMEM_LIMIT = (V7X_VMEM_BYTES * 3) // 4
MESH = pl.DeviceIdType.MESH
HBM_SPEC = pl.BlockSpec(memory_space=pltpu.HBM)
VMEM_SPEC = pl.BlockSpec(memory_space=pltpu.VMEM)

NN = (((1,), (0,)), ((), ()))
NT = (((1,), (1,)), ((), ()))
TN = (((0,), (0,)), ((), ()))


def _params(n_grid):
    return pltpu.CompilerParams(dimension_semantics=("arbitrary",) * n_grid, vmem_limit_bytes=VMEM_LIMIT)


def _dot(a, b, dims=NN):
    return lax.dot_general(a.astype(BF16), b.astype(BF16), dims, preferred_element_type=F32)


def _dot_f32(a, b):
    return lax.dot_general(a, b, NN, precision=lax.Precision.HIGHEST, preferred_element_type=F32)


def _sigmoid(x):
    return 1.0 / (1.0 + jnp.exp(-x))


def _pick(n, cands):
    for t in cands:
        if n % t == 0:
            return t
    raise ValueError(n)


def _matmul(a, b, mode, out_dtype, name, phase=None):
    if mode == "nn":
        (m, k), (_, n) = a.shape, b.shape
    elif mode == "nt":
        (m, k), (n, _) = a.shape, b.shape
    else:
        (k, m), (_, n) = a.shape, b.shape
    tm = _pick(m, (1024, 512))
    tn = _pick(n, (1024, 512))
    tk = _pick(k, (1024, 512))
    nk = k // tk
    dims = {"nn": NN, "nt": NT, "tn": TN}[mode]

    def body(a_ref, b_ref, o_ref, acc_ref):
        kk = pl.program_id(2)

        @pl.when(kk == 0)
        def _():
            acc_ref[...] = jnp.zeros_like(acc_ref)

        acc_ref[...] += lax.dot_general(a_ref[...], b_ref[...], dims, preferred_element_type=F32)

        @pl.when(kk == nk - 1)
        def _():
            o_ref[...] = acc_ref[...].astype(o_ref.dtype)

    if mode == "tn":
        a_spec = pl.BlockSpec((tk, tm), lambda i, j, kk: (kk, i))
    else:
        a_spec = pl.BlockSpec((tm, tk), lambda i, j, kk: (i, kk))
    if mode == "nt":
        b_spec = pl.BlockSpec((tn, tk), lambda i, j, kk: (j, kk))
    else:
        b_spec = pl.BlockSpec((tk, tn), lambda i, j, kk: (kk, j))
    return _pcall(
        body, [a, b], phase, name=name,
        out_shape=jax.ShapeDtypeStruct((m, n), out_dtype),
        grid=(m // tm, n // tn, nk),
        in_specs=[a_spec, b_spec],
        out_specs=pl.BlockSpec((tm, tn), lambda i, j, kk: (i, j)),
        scratch_shapes=[pltpu.VMEM((tm, tn), F32)],
        compiler_params=_params(3),
    )


TR = 256


def _row_spec(cols=D):
    return pl.BlockSpec((TR, cols), lambda i: (i, 0))


def _vec_spec(cols=D):
    return pl.BlockSpec((1, cols), lambda i: (0, 0))


def _norm_mod(x, nw, scale, shift, name):
    def body(x_ref, nw_ref, sc_ref, sh_ref, h_ref):
        xv = x_ref[...]
        r = lax.rsqrt(jnp.mean(xv * xv, axis=-1, keepdims=True) + EPS)
        h_ref[...] = ((xv * r) * nw_ref[...] * (1.0 + sc_ref[...]) + sh_ref[...]).astype(BF16)

    return pl.pallas_call(
        body, name=name, out_shape=jax.ShapeDtypeStruct((S, D), BF16), grid=(S // TR,),
        in_specs=[_row_spec(), _vec_spec(), _vec_spec(), _vec_spec()], out_specs=_row_spec(),
        compiler_params=_params(1),
    )(x, nw, scale, shift)


def _resid_norm_mod(x, mix, gate, nw, scale, shift, name):
    def body(x_ref, mix_ref, g_ref, nw_ref, sc_ref, sh_ref, x1_ref, h_ref):
        xv = x_ref[...] + g_ref[...] * mix_ref[...]
        x1_ref[...] = xv
        r = lax.rsqrt(jnp.mean(xv * xv, axis=-1, keepdims=True) + EPS)
        h_ref[...] = ((xv * r) * nw_ref[...] * (1.0 + sc_ref[...]) + sh_ref[...]).astype(BF16)

    return pl.pallas_call(
        body, name=name,
        out_shape=(jax.ShapeDtypeStruct((S, D), F32), jax.ShapeDtypeStruct((S, D), BF16)), grid=(S // TR,),
        in_specs=[_row_spec(), _row_spec(), _vec_spec(), _vec_spec(), _vec_spec(), _vec_spec()],
        out_specs=(_row_spec(), _row_spec()),
        compiler_params=_params(1),
    )(x, mix, gate, nw, scale, shift)


def _norm_mod_bwd(dh, xin, nw, scale, dres, name, mix=None, gate=None, phase=None):
    with_gate = mix is not None

    def body(*refs):
        if with_gate:
            dh_ref, x_ref, nw_ref, sc_ref, dres_ref, mix_ref, g_ref, dx_ref, dsh_ref, dsc_ref, dnw_ref, dg_ref, dmix_ref = refs
        else:
            dh_ref, x_ref, nw_ref, sc_ref, dres_ref, dx_ref, dsh_ref, dsc_ref, dnw_ref = refs
        i = pl.program_id(0)
        dhv = dh_ref[...]
        xv = x_ref[...]
        r = lax.rsqrt(jnp.mean(xv * xv, axis=-1, keepdims=True) + EPS)
        xn = xv * r
        nwv = nw_ref[...]
        one_sc = 1.0 + sc_ref[...]
        dxn = dhv * nwv * one_sc
        dx = dres_ref[...] + r * (dxn - xn * jnp.mean(dxn * xn, axis=-1, keepdims=True))
        dx_ref[...] = dx

        @pl.when(i == 0)
        def _():
            dsh_ref[...] = jnp.zeros_like(dsh_ref)
            dsc_ref[...] = jnp.zeros_like(dsc_ref)
            dnw_ref[...] = jnp.zeros_like(dnw_ref)
            if with_gate:
                dg_ref[...] = jnp.zeros_like(dg_ref)

        dsh_ref[...] += jnp.sum(dhv, axis=0, keepdims=True)
        dsc_ref[...] += jnp.sum(dhv * xn * nwv, axis=0, keepdims=True)
        dnw_ref[...] += jnp.sum(dhv * xn * one_sc, axis=0, keepdims=True)
        if with_gate:
            dg_ref[...] += jnp.sum(dx * mix_ref[...], axis=0, keepdims=True)
            dmix_ref[...] = (dx * g_ref[...]).astype(BF16)

    vec = jax.ShapeDtypeStruct((1, D), F32)
    ins = [dh, xin, nw, scale, dres]
    in_specs = [_row_spec(), _row_spec(), _vec_spec(), _vec_spec(), _row_spec()]
    outs = [jax.ShapeDtypeStruct((S, D), F32), vec, vec, vec]
    out_specs = [_row_spec(), _vec_spec(), _vec_spec(), _vec_spec()]
    if with_gate:
        ins += [mix, gate]
        in_specs += [_row_spec(), _vec_spec()]
        outs += [vec, jax.ShapeDtypeStruct((S, D), BF16)]
        out_specs += [_vec_spec(), _row_spec()]
    return _pcall(
        body, ins, phase, name=name, out_shape=tuple(outs), grid=(S // TR,), in_specs=in_specs,
        out_specs=tuple(out_specs), compiler_params=_params(1),
    )


def _tri(lower):
    row = lax.broadcasted_iota(jnp.int32, (CH, CH), 0)
    col = lax.broadcasted_iota(jnp.int32, (CH, CH), 1)
    return (row >= col) if lower else (col >= row)


def _hgrn_gates(hq, hf, lb):
    sig = _sigmoid(hf)
    f = lb + (1.0 - lb) * sig
    g = jnp.log(f)
    sq = _sigmoid(hq)
    return sig, f, g, 1.0 - f, hq * sq, sq


def _proj_col_spec(group):
    return pl.BlockSpec((S, DH), lambda h: (0, group * H + h))


def _hgrn_fwd(proj, lb_logits, norm_w, phase=None):
    def body(hq_ref, hf_ref, hi_ref, hg_ref, lbl_ref, nw_ref, out_ref, oraw_ref, st_ref, s_ref):
        lb = 1.0 / (1.0 + jnp.exp(lbl_ref[1:2, :] - lbl_ref[0:1, :]))
        nw = nw_ref[...]
        lower = _tri(True)
        ltri = lower.astype(F32)
        s_ref[...] = jnp.zeros_like(s_ref)

        def chunk(n, carry):
            rows = pl.ds(pl.multiple_of(n * CH, CH), CH)
            hq, hf, v, hg = hq_ref[rows, :], hf_ref[rows, :], hi_ref[rows, :], hg_ref[rows, :]
            _, _, g, kk, q, _ = _hgrn_gates(hq, hf, lb)
            gc = _dot_f32(ltri, g)
            gl = jnp.sum(g, axis=0, keepdims=True)
            qe = q * jnp.exp(gc)
            ke = kk * jnp.exp(gl - gc)
            qh = q * jnp.exp(gc - 0.5 * gl)
            kh = kk * jnp.exp(0.5 * gl - gc)
            st = s_ref[...]
            st_ref[0, pl.ds(n, 1)] = st[None]
            a = jnp.where(lower, _dot(qh, kh, NT), 0.0)
            o = _dot(qe, st, NT) + _dot(a, v)
            s_ref[...] = st * jnp.exp(gl) + _dot(v, ke, TN)
            oraw_ref[rows, :] = o
            rs = lax.rsqrt(jnp.mean(o * o, axis=-1, keepdims=True) + EPS)
            out_ref[rows, :] = (o * rs * nw * (hg * _sigmoid(hg))).astype(BF16)
            return carry

        lax.fori_loop(0, NCH, chunk, 0)

    head_spec = pl.BlockSpec((S, DH), lambda h: (0, h))
    return _pcall(
        body, [proj, proj, proj, proj, lb_logits, norm_w], phase, name="hgrn_fwd",
        out_shape=(jax.ShapeDtypeStruct((S, 2 * HW), BF16), jax.ShapeDtypeStruct((S, HW), F32),
                   jax.ShapeDtypeStruct((H, NCH, DH, DH), F32)),
        grid=(H,),
        in_specs=[_proj_col_spec(0), _proj_col_spec(1), _proj_col_spec(2), _proj_col_spec(3),
                  pl.BlockSpec((2, DH), lambda h: (0, h)), pl.BlockSpec((1, DH), lambda h: (0, 0))],
        out_specs=(head_spec, head_spec, pl.BlockSpec((1, NCH, DH, DH), lambda h: (h, 0, 0, 0))),
        scratch_shapes=[pltpu.VMEM((DH, DH), F32)],
        compiler_params=_params(1),
    )


def _hgrn_bwd(proj, lb_logits, norm_w, oraw, states, dout, phase=None):
    def body(hq_ref, hf_ref, hi_ref, hg_ref, lbl_ref, nw_ref, oraw_ref, st_ref, do_ref,
             dhq_ref, dhf_ref, dhi_ref, dhg_ref, dlb_ref, dnw_ref, ds_ref, acc_ref):
        h = pl.program_id(0)
        lb = 1.0 / (1.0 + jnp.exp(lbl_ref[1:2, :] - lbl_ref[0:1, :]))
        nw = nw_ref[...]
        lower = _tri(True)
        ltri = lower.astype(F32)
        utri = _tri(False).astype(F32)
        last = lax.broadcasted_iota(jnp.int32, (CH, DH), 0) == CH - 1
        ds_ref[...] = jnp.zeros_like(ds_ref)
        acc_ref[...] = jnp.zeros_like(acc_ref)

        @pl.when(h == 0)
        def _():
            dnw_ref[...] = jnp.zeros_like(dnw_ref)

        def chunk(i, carry):
            n = NCH - 1 - i
            rows = pl.ds(pl.multiple_of(n * CH, CH), CH)
            hq, hf, v, hg = hq_ref[rows, :], hf_ref[rows, :], hi_ref[rows, :], hg_ref[rows, :]
            sig, f, g, kk, q, sq = _hgrn_gates(hq, hf, lb)
            gc = _dot_f32(ltri, g)
            gl = jnp.sum(g, axis=0, keepdims=True)
            eg = jnp.exp(gc)
            ek = jnp.exp(gl - gc)
            gam = jnp.exp(gl)
            ph = jnp.exp(gc - 0.5 * gl)
            pk = jnp.exp(0.5 * gl - gc)
            qe, ke = q * eg, kk * ek
            qh, kh = q * ph, kk * pk
            st = st_ref[0, pl.ds(n, 1)][0]
            dst = ds_ref[...]
            a = jnp.where(lower, _dot(qh, kh, NT), 0.0)
            o = oraw_ref[rows, :]
            rs = lax.rsqrt(jnp.mean(o * o, axis=-1, keepdims=True) + EPS)
            xh = o * rs
            sg = _sigmoid(hg)
            dgo = do_ref[rows, :]
            d_on = dgo * (hg * sg)
            dhg_ref[rows, :] = (dgo * xh * nw * (sg * (1.0 + hg * (1.0 - sg)))).astype(BF16)
            acc_ref[1:2, :] += jnp.sum(d_on * xh, axis=0, keepdims=True)
            dy = d_on * nw
            do = rs * (dy - xh * jnp.mean(dy * xh, axis=-1, keepdims=True))
            dqe = _dot(do, st)
            da = jnp.where(lower, _dot(do, v, NT), 0.0)
            dv = _dot(a, do, TN) + _dot(ke, dst, NT)
            dke = _dot(v, dst)
            dgam = jnp.sum(dst * st, axis=0, keepdims=True)
            dqh = _dot(da, kh)
            dkh = _dot(da, qh, TN)
            dq = dqh * ph + dqe * eg
            dk = dkh * pk + dke * ek
            dgl = jnp.sum(dke * ke, axis=0, keepdims=True) + dgam * gam
            dgc = dqh * qh - dkh * kh + dqe * qe - dke * ke + jnp.where(last, dgl, 0.0)
            dg = _dot_f32(utri, dgc)
            df = dg / f - dk
            dhf_ref[rows, :] = (df * (1.0 - lb) * sig * (1.0 - sig)).astype(BF16)
            acc_ref[0:1, :] += jnp.sum(df * (1.0 - sig), axis=0, keepdims=True)
            dhq_ref[rows, :] = (dq * (sq * (1.0 + hq * (1.0 - sq)))).astype(BF16)
            dhi_ref[rows, :] = dv.astype(BF16)
            ds_ref[...] = dst * gam + _dot(do, qe, TN)
            return carry

        lax.fori_loop(0, NCH, chunk, 0)
        dlb_ref[...] = acc_ref[0:1, :]
        dnw_ref[...] += acc_ref[1:2, :]

    head_spec = pl.BlockSpec((S, DH), lambda h: (0, h))
    head_out = jax.ShapeDtypeStruct((S, HW), BF16)
    return _pcall(
        body, [proj, proj, proj, proj, lb_logits, norm_w, oraw, states, dout], phase, name="hgrn_bwd",
        out_shape=(head_out, head_out, head_out, head_out,
                   jax.ShapeDtypeStruct((1, HW), F32), jax.ShapeDtypeStruct((1, DH), F32)),
        grid=(H,),
        in_specs=[_proj_col_spec(0), _proj_col_spec(1), _proj_col_spec(2), _proj_col_spec(3),
                  pl.BlockSpec((2, DH), lambda h: (0, h)), pl.BlockSpec((1, DH), lambda h: (0, 0)),
                  head_spec, pl.BlockSpec((1, NCH, DH, DH), lambda h: (h, 0, 0, 0)), head_spec],
        out_specs=(head_spec, head_spec, head_spec, head_spec,
                   pl.BlockSpec((1, DH), lambda h: (0, h)), pl.BlockSpec((1, DH), lambda h: (0, 0))),
        scratch_shapes=[pltpu.VMEM((DH, DH), F32), pltpu.VMEM((8, DH), F32)],
        compiler_params=_params(1),
    )


ATT_SCALE = DH ** -0.5
HEADS_PER_STEP = {1: 8, 4: 1, 16: 1}


def _sub_rows(ref, r, dil, cols):
    if dil == 1:
        return ref[:, cols]
    return ref[pl.ds(r, BLK, stride=dil), cols]


def _set_sub_rows(ref, r, dil, cols, val):
    if dil == 1:
        ref[:, cols] = val
    else:
        ref[pl.ds(r, BLK, stride=dil), cols] = val


def _slopes(dil):
    hp = HEADS_PER_STEP[dil]
    s = jnp.asarray([dil * 2.0 ** (-(h + 1)) for h in range(H)], F32)
    return jnp.broadcast_to(s[:, None], (H, DH)).reshape(H // hp, hp, DH)


def _rms_rows(x, w):
    rs = lax.rsqrt(jnp.mean(x * x, axis=-1, keepdims=True) + EPS)
    return x * rs, rs


def _att_masks():
    qi = lax.broadcasted_iota(jnp.int32, (BLK, BLK), 0)
    kj = lax.broadcasted_iota(jnp.int32, (BLK, BLK), 1)
    steps_c = qi - kj
    steps_p = qi - kj + BLK
    return steps_c, steps_p, steps_c >= 0, steps_p <= BLK


def _attn_fwd(proj, q_w, k_w, dil, phase=None):
    hp = HEADS_PER_STEP[dil]
    rows, ng, nb = BLK * dil, H // hp, S // (BLK * dil)

    def body(q_ref, kc_ref, kp_ref, vc_ref, vp_ref, qw_ref, kw_ref, sl_ref, o_ref, lse_ref):
        n = pl.program_id(0)
        steps_c, steps_p, ok_c, ok_p = _att_masks()
        ok_p = jnp.logical_and(ok_p, n > 0)
        fc, fp = steps_c.astype(F32), steps_p.astype(F32)
        qw, kw = qw_ref[...], kw_ref[...]
        for hh in range(hp):
            cols = slice(hh * DH, (hh + 1) * DH)
            sl = sl_ref[0, hh:hh + 1, :]
            for r in range(dil):
                qn = _rms_rows(_sub_rows(q_ref, r, dil, cols), None)[0] * qw
                kcn = _rms_rows(_sub_rows(kc_ref, r, dil, cols), None)[0] * kw
                kpn = _rms_rows(_sub_rows(kp_ref, r, dil, cols), None)[0] * kw
                sc = jnp.where(ok_c, _dot(qn, kcn, NT) * ATT_SCALE - sl * fc, NEG)
                sp = jnp.where(ok_p, _dot(qn, kpn, NT) * ATT_SCALE - sl * fp, NEG)
                m = jnp.maximum(jnp.max(sc, axis=-1, keepdims=True), jnp.max(sp, axis=-1, keepdims=True))
                pc, pp = jnp.exp(sc - m), jnp.exp(sp - m)
                den = jnp.sum(pc, axis=-1, keepdims=True) + jnp.sum(pp, axis=-1, keepdims=True)
                o = (_dot(pc, _sub_rows(vc_ref, r, dil, cols)) + _dot(pp, _sub_rows(vp_ref, r, dil, cols))) / den
                _set_sub_rows(o_ref, r, dil, cols, o)
                _set_sub_rows(lse_ref, r, dil, cols, jnp.broadcast_to(m + jnp.log(den), (BLK, DH)))

    def spec(group, prev):
        if prev:
            return pl.BlockSpec((rows, hp * DH), lambda n, g: (jnp.maximum(n - 1, 0), group * ng + g))
        return pl.BlockSpec((rows, hp * DH), lambda n, g: (n, group * ng + g))

    out = jax.ShapeDtypeStruct((S, HW), F32)
    out_spec = pl.BlockSpec((rows, hp * DH), lambda n, g: (n, g))
    wspec = pl.BlockSpec((1, DH), lambda n, g: (0, 0))
    return _pcall(
        body, [proj, proj, proj, proj, proj, q_w, k_w, _slopes(dil)], phase,
        name=f"attn_fwd_d{dil}", out_shape=(out, out), grid=(nb, ng),
        in_specs=[spec(4, False), spec(5, False), spec(5, True), spec(6, False), spec(6, True), wspec, wspec,
                  pl.BlockSpec((1, hp, DH), lambda n, g: (g, 0, 0))],
        out_specs=(out_spec, out_spec),
        compiler_params=_params(2),
    )


def _attn_merge(outs, lses, cat):
    def body(o1, o2, o3, l1, l2, l3, cat_ref, ob_ref, of_ref, lse_ref):
        a, b, c = l1[...], l2[...], l3[...]
        m = jnp.maximum(jnp.maximum(a, b), c)
        ea, eb, ec = jnp.exp(a - m), jnp.exp(b - m), jnp.exp(c - m)
        den = ea + eb + ec
        o = (ea * o1[...] + eb * o2[...] + ec * o3[...]) / den
        ob_ref[...] = o.astype(BF16)
        of_ref[...] = o
        lse_ref[...] = m + jnp.log(den)

    f = jax.ShapeDtypeStruct((S, HW), F32)
    return pl.pallas_call(
        body, name="attn_merge", out_shape=(jax.ShapeDtypeStruct((S, 2 * HW), BF16), f, f), grid=(S // TR,),
        in_specs=[_row_spec(HW)] * 6 + [pl.BlockSpec(memory_space=pl.ANY)],
        out_specs=(pl.BlockSpec((TR, HW), lambda i: (i, 1)), _row_spec(HW), _row_spec(HW)),
        input_output_aliases={6: 0},
        compiler_params=_params(1),
    )(*outs, *lses, cat)


def _attn_bwd(proj, q_w, k_w, o, lse, do, dil, acc, phase=None):
    hp = HEADS_PER_STEP[dil]
    rows, ng, nb = BLK * dil, H // hp, S // (BLK * dil)
    has_acc = acc is not None

    def body(*refs):
        (q_ref, qn_ref, kp_ref, kc_ref, vp_ref, vc_ref, o_ref, on_ref, l_ref, ln_ref, do_ref, don_ref,
         qw_ref, kw_ref, sl_ref) = refs[:15]
        refs = refs[15:]
        if has_acc:
            aq_ref, ak_ref, av_ref, aqw_ref, akw_ref = refs[:5]
            refs = refs[5:]
        dq_ref, dk_ref, dv_ref, dqw_ref, dkw_ref = refs
        m, g = pl.program_id(0), pl.program_id(1)
        steps_c, steps_p, ok_c, ok_p = _att_masks()
        ok_prev = jnp.logical_and(ok_p, m > 0)
        ok_next = jnp.logical_and(ok_p, m < nb - 1)
        fc, fp = steps_c.astype(F32), steps_p.astype(F32)
        qw, kw = qw_ref[...], kw_ref[...]

        @pl.when(jnp.logical_and(m == 0, g == 0))
        def _():
            if has_acc:
                dqw_ref[...] = aqw_ref[...]
                dkw_ref[...] = akw_ref[...]
            else:
                dqw_ref[...] = jnp.zeros_like(dqw_ref)
                dkw_ref[...] = jnp.zeros_like(dkw_ref)

        for hh in range(hp):
            cols = slice(hh * DH, (hh + 1) * DH)
            sl = sl_ref[0, hh:hh + 1, :]
            for r in range(dil):
                sub = lambda ref: _sub_rows(ref, r, dil, cols)
                qx, qrs = _rms_rows(sub(q_ref), None)
                kx, krs = _rms_rows(sub(kc_ref), None)
                qn, kcn = qx * qw, kx * kw
                qnn = _rms_rows(sub(qn_ref), None)[0] * qw
                kpn = _rms_rows(sub(kp_ref), None)[0] * kw
                vc, vp = sub(vc_ref), sub(vp_ref)
                dov, donv = sub(do_ref), sub(don_ref)
                lse_m, lse_n = sub(l_ref)[:, 0:1], sub(ln_ref)[:, 0:1]
                delta_m = jnp.sum(dov * sub(o_ref), axis=-1, keepdims=True)
                delta_n = jnp.sum(donv * sub(on_ref), axis=-1, keepdims=True)
                p_c = jnp.where(ok_c, jnp.exp(_dot(qn, kcn, NT) * ATT_SCALE - sl * fc - lse_m), 0.0)
                p_p = jnp.where(ok_prev, jnp.exp(_dot(qn, kpn, NT) * ATT_SCALE - sl * fp - lse_m), 0.0)
                p_n = jnp.where(ok_next, jnp.exp(_dot(qnn, kcn, NT) * ATT_SCALE - sl * fp - lse_n), 0.0)
                ds_c = p_c * (_dot(dov, vc, NT) - delta_m)
                ds_p = p_p * (_dot(dov, vp, NT) - delta_m)
                ds_n = p_n * (_dot(donv, vc, NT) - delta_n)
                dqn = (_dot(ds_c, kcn) + _dot(ds_p, kpn)) * ATT_SCALE
                dkn = (_dot(ds_c, qn, TN) + _dot(ds_n, qnn, TN)) * ATT_SCALE
                dv = _dot(p_c, dov, TN) + _dot(p_n, donv, TN)
                dqw_ref[...] += jnp.sum(dqn * qx, axis=0, keepdims=True)
                dkw_ref[...] += jnp.sum(dkn * kx, axis=0, keepdims=True)
                dyq, dyk = dqn * qw, dkn * kw
                daq = qrs * (dyq - qx * jnp.mean(dyq * qx, axis=-1, keepdims=True))
                dak = krs * (dyk - kx * jnp.mean(dyk * kx, axis=-1, keepdims=True))
                if has_acc:
                    daq, dak, dv = daq + sub(aq_ref), dak + sub(ak_ref), dv + sub(av_ref)
                _set_sub_rows(dq_ref, r, dil, cols, daq)
                _set_sub_rows(dk_ref, r, dil, cols, dak)
                _set_sub_rows(dv_ref, r, dil, cols, dv)

    def shifted(shift, m):
        if shift < 0:
            return jnp.maximum(m - 1, 0)
        if shift > 0:
            return jnp.minimum(m + 1, nb - 1)
        return m

    def pspec(group, shift):
        return pl.BlockSpec((rows, hp * DH), lambda m, g: (shifted(shift, m), group * ng + g))

    cur = pl.BlockSpec((rows, hp * DH), lambda m, g: (m, g))
    nxt = pl.BlockSpec((rows, hp * DH), lambda m, g: (shifted(1, m), g))
    wspec = pl.BlockSpec((1, DH), lambda m, g: (0, 0))
    ins = [proj, proj, proj, proj, proj, proj, o, o, lse, lse, do, do, q_w, k_w, _slopes(dil)]
    in_specs = [pspec(4, 0), pspec(4, 1), pspec(5, -1), pspec(5, 0), pspec(6, -1), pspec(6, 0),
                cur, nxt, cur, nxt, pspec(1, 0), pspec(1, 1), wspec, wspec,
                pl.BlockSpec((1, hp, DH), lambda m, g: (g, 0, 0))]
    if has_acc:
        ins += list(acc)
        in_specs += [cur, cur, cur, wspec, wspec]
    big = jax.ShapeDtypeStruct((S, HW), F32)
    small = jax.ShapeDtypeStruct((1, DH), F32)
    return _pcall(
        body, ins, phase, name=f"attn_bwd_d{dil}", out_shape=(big, big, big, small, small), grid=(nb, ng),
        in_specs=in_specs, out_specs=(cur, cur, cur, wspec, wspec),
        compiler_params=_params(2),
    )


TC = 512
NCT = DFF // TC


def _shift_rows(a, k):
    rows = lax.broadcasted_iota(jnp.int32, a.shape, 0)
    rolled = pltpu.roll(a, k % S, 0)
    if k > 0:
        return jnp.where(rows >= k, rolled, 0.0)
    return jnp.where(rows < S + k, rolled, 0.0)


def _conv_pre(a, w_ref, b_ref):
    return b_ref[...] + w_ref[0:1, :] * _shift_rows(a, 2) + w_ref[1:2, :] * _shift_rows(a, 1) + w_ref[2:3, :] * a


def _conv_gate_fwd(u, conv_w, conv_b, phase=None):
    def body(a_ref, g_ref, w_ref, b_ref, y_ref):
        pre = _conv_pre(a_ref[...], w_ref, b_ref)
        y_ref[...] = (pre * _sigmoid(pre) * g_ref[...]).astype(BF16)

    return _pcall(
        body, [u, u, conv_w, conv_b], phase,
        name="conv_gate_fwd", out_shape=jax.ShapeDtypeStruct((S, DFF), BF16), grid=(NCT,),
        in_specs=[pl.BlockSpec((S, TC), lambda i: (0, i)), pl.BlockSpec((S, TC), lambda i: (0, NCT + i)),
                  pl.BlockSpec((3, TC), lambda i: (0, i)), pl.BlockSpec((1, TC), lambda i: (0, i))],
        out_specs=pl.BlockSpec((S, TC), lambda i: (0, i)),
        compiler_params=_params(1),
    )


def _conv_gate_bwd(dy, u, conv_w, conv_b, phase=None):
    def body(dy_ref, a_ref, g_ref, w_ref, b_ref, da_ref, dg_ref, db_ref, dw_ref):
        a = a_ref[...]
        dyv = dy_ref[...]
        pre = _conv_pre(a, w_ref, b_ref)
        sg = _sigmoid(pre)
        dg_ref[...] = (dyv * pre * sg).astype(BF16)
        dpre = dyv * g_ref[...] * (sg * (1.0 + pre * (1.0 - sg)))
        da = w_ref[2:3, :] * dpre + w_ref[1:2, :] * _shift_rows(dpre, -1) + w_ref[0:1, :] * _shift_rows(dpre, -2)
        da_ref[...] = da.astype(BF16)
        db_ref[...] = jnp.sum(dpre, axis=0, keepdims=True)
        dw_ref[0:1, :] = jnp.sum(dpre * _shift_rows(a, 2), axis=0, keepdims=True)
        dw_ref[1:2, :] = jnp.sum(dpre * _shift_rows(a, 1), axis=0, keepdims=True)
        dw_ref[2:3, :] = jnp.sum(dpre * a, axis=0, keepdims=True)

    col = pl.BlockSpec((S, TC), lambda i: (0, i))
    half = jax.ShapeDtypeStruct((S, DFF), BF16)
    return _pcall(
        body, [dy, u, u, conv_w, conv_b], phase, name="conv_gate_bwd",
        out_shape=(half, half, jax.ShapeDtypeStruct((1, DFF), F32), jax.ShapeDtypeStruct((3, DFF), F32)),
        grid=(NCT,),
        in_specs=[col, col, pl.BlockSpec((S, TC), lambda i: (0, NCT + i)),
                  pl.BlockSpec((3, TC), lambda i: (0, i)), pl.BlockSpec((1, TC), lambda i: (0, i))],
        out_specs=(col, col, pl.BlockSpec((1, TC), lambda i: (0, i)), pl.BlockSpec((3, TC), lambda i: (0, i))),
        compiler_params=_params(1),
    )


def _out_loss(z, x1, target, gate):
    def body(z_ref, x_ref, t_ref, g_ref, do_ref, dz_ref, dg_ref, loss_ref):
        i = pl.program_id(0)
        zv = z_ref[...]
        gv = g_ref[...]
        err = x_ref[...] + gv * zv - t_ref[...]
        dout = err * (1.0 / D)
        do_ref[...] = dout
        dz_ref[...] = (dout * gv).astype(BF16)

        @pl.when(i == 0)
        def _():
            dg_ref[...] = jnp.zeros_like(dg_ref)
            loss_ref[...] = jnp.zeros_like(loss_ref)

        dg_ref[...] += jnp.sum(dout * zv, axis=0, keepdims=True)
        part = jnp.sum(jnp.sum(err * err, axis=0, keepdims=True), axis=-1, keepdims=True) * (0.5 / D)
        lane = lax.broadcasted_iota(jnp.int32, (1, 128), 1)
        loss_ref[...] += jnp.where(lane == 0, part, 0.0)

    return pl.pallas_call(
        body, name="out_loss",
        out_shape=(jax.ShapeDtypeStruct((S, D), F32), jax.ShapeDtypeStruct((S, D), BF16),
                   jax.ShapeDtypeStruct((1, D), F32), jax.ShapeDtypeStruct((1, 128), F32)),
        grid=(S // TR,),
        in_specs=[_row_spec(), _row_spec(), _row_spec(), _vec_spec()],
        out_specs=(_row_spec(), _row_spec(), _vec_spec(), _vec_spec(128)),
        compiler_params=_params(1),
    )(z, x1, target, gate)


ADA_COLS = 6 * D // N_CHIPS
TA = 512


def _ada_fwd(c_all, w_shard, b_shard):
    def body(c_ref, w_ref, b_ref, o_ref):
        cv = c_ref[...]
        o_ref[...] = _dot_f32(cv * _sigmoid(cv), w_ref[...]) + b_ref[...]

    return pl.pallas_call(
        body, name="ada_fwd", out_shape=jax.ShapeDtypeStruct((N_DEV, ADA_COLS), F32), grid=(ADA_COLS // TA,),
        in_specs=[pl.BlockSpec((N_DEV, D), lambda j: (0, 0)), pl.BlockSpec((D, TA), lambda j: (0, j)),
                  pl.BlockSpec((1, TA), lambda j: (0, j))],
        out_specs=pl.BlockSpec((N_DEV, TA), lambda j: (0, j)),
        compiler_params=_params(1),
    )(c_all, w_shard, b_shard)


def _ada_bwd(c_all, dmod_shard):
    def body(c_ref, d_ref, o_ref):
        cv = c_ref[...]
        o_ref[...] = lax.dot_general(cv * _sigmoid(cv), d_ref[...], TN, precision=lax.Precision.HIGHEST,
                                     preferred_element_type=F32)

    return pl.pallas_call(
        body, name="ada_bwd", out_shape=jax.ShapeDtypeStruct((D, ADA_COLS), F32), grid=(ADA_COLS // TA,),
        in_specs=[pl.BlockSpec((N_DEV, D), lambda j: (0, 0)), pl.BlockSpec((N_DEV, TA), lambda j: (0, j))],
        out_specs=pl.BlockSpec((D, TA), lambda j: (0, j)),
        compiler_params=_params(1),
    )(c_all, dmod_shard)


def _sum_rows(g, name):
    rows, n = g.shape

    def body(g_ref, o_ref):
        acc = g_ref[0:1, :]
        for i in range(1, rows):
            acc = acc + g_ref[i:i + 1, :]
        o_ref[...] = acc

    return pl.pallas_call(
        body, name=name, out_shape=jax.ShapeDtypeStruct((1, n), F32),
        in_specs=[VMEM_SPEC], out_specs=VMEM_SPEC,
    )(g)


def _lb_grad(lb_logits, dlb):
    def body(l_ref, d_ref, o_ref):
        p = 1.0 / (1.0 + jnp.exp(l_ref[1:2, :] - l_ref[0:1, :]))
        t = d_ref[...] * p * (1.0 - p)
        o_ref[0:1, :] = t
        o_ref[1:2, :] = -t

    return pl.pallas_call(
        body, name="lb_grad", out_shape=jax.ShapeDtypeStruct((2, HW), F32),
        in_specs=[VMEM_SPEC, VMEM_SPEC], out_specs=VMEM_SPEC,
    )(lb_logits, dlb)


def _adamw(w, g, m, v, name):
    rows, cols = w.shape
    tr = rows
    if rows * cols * 4 > ADAM_BLOCK_BYTES:
        tr = _pick(rows, [t for t in (256, 128, 64, 32, 16, 8) if t * cols * 4 <= ADAM_BLOCK_BYTES])

    def body(w_ref, g_ref, m_ref, v_ref, d_ref, mo_ref, vo_ref):
        gv = g_ref[...]
        m2 = ADAM_B1 * m_ref[...] + (1.0 - ADAM_B1) * gv
        v2 = ADAM_B2 * v_ref[...] + (1.0 - ADAM_B2) * (gv * gv)
        m_hat = m2 / (1.0 - ADAM_B1 ** ADAM_STEP)
        v_hat = v2 / (1.0 - ADAM_B2 ** ADAM_STEP)
        d_ref[...] = -ADAM_LR * (m_hat / (jnp.sqrt(v_hat) + ADAM_EPS) + ADAM_WD * w_ref[...])
        mo_ref[...] = m2
        vo_ref[...] = v2

    spec = pl.BlockSpec((tr, cols), lambda i: (i, 0))
    out = jax.ShapeDtypeStruct((rows, cols), F32)
    return pl.pallas_call(
        body, name=name, out_shape=(out, out, out), grid=(rows // tr,),
        in_specs=[spec] * 4, out_specs=(spec,) * 3,
        compiler_params=_params(1),
    )(w, g, m, v)


W_IN, W_OUT, W_UP, W_DOWN = range(4)


def _sequence_step(x, target, mod, norm1_w, lb_logits, hg_norm_w, q_norm_w, k_norm_w, norm2_w, conv_w, conv_b, net):
    shift1, scale1, gate1, shift2, scale2, gate2 = [mod[:, i * D:(i + 1) * D] for i in range(6)]

    def run(name, fn, *args, **kw):
        phase = net.host(name)
        if phase is None:
            return fn(*args, **kw)
        res, comm = fn(*args, phase=phase, **kw)
        net.done(name, comm)
        return res

    h = _norm_mod(x, norm1_w, scale1, shift1, "norm1_fwd")
    proj = run("proj_fwd", _matmul, h, net.full[W_IN], "nn", F32, "proj_fwd")
    cat, o_raw, states = run("hgrn_fwd", _hgrn_fwd, proj, lb_logits, hg_norm_w)
    att = [run(f"attn_fwd_d{dil}", _attn_fwd, proj, q_norm_w, k_norm_w, dil) for dil in DILS]
    cat, b_out_f32, lse = _attn_merge([t[0] for t in att], [t[1] for t in att], cat)
    mix = run("mix_fwd", _matmul, cat, net.full[W_OUT], "nn", F32, "mix_fwd")
    x1, h2 = _resid_norm_mod(x, mix, gate1, norm2_w, scale2, shift2, "norm2_fwd")
    u = run("up_fwd", _matmul, h2, net.full[W_UP], "nn", F32, "up_fwd")
    yact = run("conv_gate_fwd", _conv_gate_fwd, u, conv_w, conv_b)
    z = _matmul(yact, net.full[W_DOWN], "nn", F32, "down_fwd")
    dout, dz, dgate2, loss_row = _out_loss(z, x1, target, gate2)

    net.grad[W_DOWN] = _matmul(yact, dz, "tn", BF16, "down_bwd_w")
    dyact = run("down_bwd_x", _matmul, dz, net.full[W_DOWN], "nt", BF16, "down_bwd_x")
    da, dg, dconv_b, dconv_w = run("conv_gate_bwd", _conv_gate_bwd, dyact, u, conv_w, conv_b)
    du = jnp.concatenate([da, dg], axis=1)
    net.grad[W_UP] = run("up_bwd_w", _matmul, h2, du, "tn", BF16, "up_bwd_w")
    dh2 = run("up_bwd_x", _matmul, du, net.full[W_UP], "nt", F32, "up_bwd_x")
    dx1, dshift2, dscale2, dnorm2, dgate1, dmix = run(
        "norm2_bwd", _norm_mod_bwd, dh2, x1, norm2_w, scale2, dout, "norm2_bwd", mix=mix, gate=gate1)
    net.grad[W_OUT] = _matmul(cat, dmix, "tn", BF16, "mix_bwd_w")
    dcat = run("mix_bwd_x", _matmul, dmix, net.full[W_OUT], "nt", F32, "mix_bwd_x")
    dhq, dhf, dhi, dhg, dlb, dhg_norm = run("hgrn_bwd", _hgrn_bwd, proj, lb_logits, hg_norm_w, o_raw, states, dcat)
    acc = None
    for dil in DILS:
        acc = run(f"attn_bwd_d{dil}", _attn_bwd, proj, q_norm_w, k_norm_w, b_out_f32, lse, dcat, dil, acc)
    daq, dak, dav, dq_norm, dk_norm = acc
    dproj = jnp.concatenate([dhq, dhf, dhi, dhg, daq.astype(BF16), dak.astype(BF16), dav.astype(BF16)], axis=1)
    net.grad[W_IN] = _matmul(h, dproj, "tn", BF16, "proj_bwd_w")
    dh = run("proj_bwd_x", _matmul, dproj, net.full[W_IN], "nt", F32, "proj_bwd_x")
    grad_x, dshift1, dscale1, dnorm1 = run("norm1_bwd", _norm_mod_bwd, dh, x, norm1_w, scale1, dx1, "norm1_bwd")

    dmod = jnp.concatenate([dshift1, dscale1, dgate1, dshift2, dscale2, dgate2], axis=1)
    small = dict(norm1_w=dnorm1, lb=dlb, hg_norm_w=dhg_norm, q_norm_w=dq_norm, k_norm_w=dk_norm,
                 norm2_w=dnorm2, conv_b=dconv_b, conv_w=dconv_w)
    return loss_row, grad_x, dmod, small


def _place():
    x, y, c = lax.axis_index("x"), lax.axis_index("y"), lax.axis_index("c")
    others = [(1 - x, y), (x, 1 - y), (1 - x, 1 - y)]
    return x, y, c, others


def _remote(src, dst, send_sems, recv_sems, k, to):
    return pltpu.make_async_remote_copy(src_ref=src, dst_ref=dst, send_sem=send_sems.at[k], recv_sem=recv_sems.at[k],
                                        device_id=to, device_id_type=MESH)


class _Phase:
    def __init__(self):
        self.ins, self.outs, self.aliases, self.groups, self.n = [], [], {}, [], 0

    def add(self, ins, outs, aliases, n, copies):
        i0, o0 = len(self.ins), len(self.outs)
        self.ins += list(ins)
        self.outs += list(outs)
        self.aliases.update({i0 + i: o0 + o for i, o in aliases.items()})
        self.groups.append((slice(i0, i0 + len(ins)), slice(o0, o0 + len(outs)), copies))
        self.n += n
        return slice(o0, o0 + len(outs))

    def build(self, cin, cout, send_sems, recv_sems, landing):
        res = []
        for si, so, copies in self.groups:
            for src, dst, land, peer in copies(cin[si], cout[so]):
                k = len(res)
                res.append(_remote(land, land, send_sems, recv_sems, k, peer) if landing
                           else _remote(src, dst, send_sems, recv_sems, k, peer))
        assert len(res) == self.n
        return res


def _pcall(body, args, phase=None, **kw):
    if phase is None or not phase.groups:
        res = pl.pallas_call(body, **kw)(*args)
        return res if phase is None else (res, ())
    grid = tuple(kw.pop("grid", ()))
    out_shape, out_specs = kw.pop("out_shape"), kw.pop("out_specs")
    single = not isinstance(out_shape, (tuple, list))
    out_shape = [out_shape] if single else list(out_shape)
    out_specs = [out_specs] if single else list(out_specs)
    scratch = list(kw.pop("scratch_shapes", ()))
    n_in, n_out, n_ci, n_co = len(args), len(out_shape), len(phase.ins), len(phase.outs)

    def hosted(*refs):
        ins, cin = refs[:n_in], refs[n_in:n_in + n_ci]
        outs = refs[n_in + n_ci:n_in + n_ci + n_out]
        cout = refs[n_in + n_ci + n_out:n_in + n_ci + n_out + n_co]
        scr, send_sems, recv_sems = refs[n_in + n_ci + n_out + n_co:-2], refs[-2], refs[-1]
        ids = [pl.program_id(a) for a in range(len(grid))]

        def start():
            for cp in phase.build(cin, cout, send_sems, recv_sems, False):
                cp.start()

        def finish():
            for cp in phase.build(cin, cout, send_sems, recv_sems, False):
                cp.wait_send()
            for cp in phase.build(cin, cout, send_sems, recv_sems, True):
                cp.wait_recv()

        if grid:
            first = functools.reduce(jnp.logical_and, [i == 0 for i in ids])
            last = functools.reduce(jnp.logical_and, [i == g - 1 for i, g in zip(ids, grid)])
            pl.when(first)(start)
            body(*ins, *outs, *scr)
            pl.when(last)(finish)
        else:
            start()
            body(*ins, *outs, *scr)
            finish()

    res = pl.pallas_call(
        hosted, out_shape=tuple(out_shape + phase.outs), grid=grid,
        in_specs=list(kw.pop("in_specs")) + [HBM_SPEC] * n_ci, out_specs=tuple(out_specs + [HBM_SPEC] * n_co),
        input_output_aliases={n_in + i: n_out + o for i, o in phase.aliases.items()},
        scratch_shapes=scratch + [pltpu.SemaphoreType.DMA((phase.n,)), pltpu.SemaphoreType.DMA((phase.n,))],
        **kw,
    )(*args, *phase.ins)
    outs = res[:n_out]
    return (outs[0] if single else tuple(outs)), tuple(res[n_out:])


def _comm_only(name, phase):
    return _pcall(lambda: None, [], phase, name=name, out_shape=(), in_specs=[], out_specs=())[1]


def _all_gather_rows(block, name):
    m_per, n = block.shape

    def body(x_ref, out_ref, send_sems, recv_sems, local_sem):
        x, y, c, chips = _place()
        me, sibling = (x, y, c), (x, y, 1 - c)

        def rows(px, py, pc):
            return out_ref.at[pl.ds((4 * px + 2 * py + pc) * m_per, m_per), :]

        def copy(k, blk, to, src=None):
            return _remote(rows(*blk) if src is None else src, rows(*blk), send_sems, recv_sems, k, to)

        mine = pltpu.make_async_copy(x_ref, rows(*me), local_sem)
        mine.start()
        first = [copy(0, me, sibling, src=x_ref)]
        first += [copy(1 + j, me, (*chip, c), src=x_ref) for j, chip in enumerate(chips)]
        for cp in first:
            cp.start()
        passed = [copy(4 + j, (*chip, c), sibling) for j, chip in enumerate(chips)]
        for j, chip in enumerate(chips):
            copy(1 + j, (*chip, c), me).wait_recv()
            passed[j].start()
        copy(0, sibling, me).wait_recv()
        for j, chip in enumerate(chips):
            copy(4 + j, (*chip, 1 - c), me).wait_recv()
        for cp in first + passed:
            cp.wait_send()
        mine.wait()

    return pl.pallas_call(
        body, name=name, out_shape=jax.ShapeDtypeStruct((N_DEV * m_per, n), block.dtype),
        in_specs=[VMEM_SPEC], out_specs=VMEM_SPEC,
        scratch_shapes=[pltpu.SemaphoreType.DMA((7,)), pltpu.SemaphoreType.DMA((7,)), pltpu.SemaphoreType.DMA],
    )(block)


class _Geo:
    def __init__(self, kind, shard_shape):
        self.kind = kind
        self.shard_shape = tuple(shard_shape)
        self.hr, self.hc = shard_shape[0] // 2, shard_shape[1]
        if kind == "col":
            self.full_shape = (shard_shape[0], N_CHIPS * shard_shape[1])
        else:
            self.full_shape = (N_CHIPS * shard_shape[0], shard_shape[1])

    def full_shard(self, ref, j):
        if self.kind == "col":
            return ref.at[:, pl.ds(pl.multiple_of(j * self.hc, 128), self.hc)]
        return ref.at[pl.ds(pl.multiple_of(j * 2 * self.hr, 16), 2 * self.hr), :]

    def full_half(self, ref, j, c):
        if self.kind == "col":
            return ref.at[pl.ds(pl.multiple_of(c * self.hr, 16), self.hr),
                          pl.ds(pl.multiple_of(j * self.hc, 128), self.hc)]
        return ref.at[pl.ds(pl.multiple_of((2 * j + c) * self.hr, 16), self.hr), :]

    def piece(self, ref, j, c, p0, p1, pieces):
        pr = self.hr // pieces
        off, n = p0 * pr, (p1 - p0) * pr
        if self.kind == "col":
            return ref.at[pl.ds(pl.multiple_of(c * self.hr + off, 16), n),
                          pl.ds(pl.multiple_of(j * self.hc, 128), self.hc)]
        return ref.at[pl.ds(pl.multiple_of((2 * j + c) * self.hr + off, 16), n), :]


WEIGHT_KINDS = ("col", "row", "col", "row")
NW = len(WEIGHT_KINDS)


def _comm_call(body, name, ins, outs, n_remote, aliases=None):
    return pl.pallas_call(
        body, name=name, out_shape=tuple(outs),
        in_specs=[HBM_SPEC] * len(ins), out_specs=tuple([HBM_SPEC] * len(outs)),
        input_output_aliases=aliases or {},
        scratch_shapes=[pltpu.SemaphoreType.DMA((n_remote,)), pltpu.SemaphoreType.DMA((n_remote,))],
    )(*ins)


ROW_TILES = (256, 176, 128, 64, 32, 16)


def _cast_into_full(shard, geo, place, name):
    rs, cs = shard.shape
    tr = _pick(rs, ROW_TILES)
    nb = rs // tr

    def body(place_ref, s_ref, o_ref):
        o_ref[...] = s_ref[...].astype(BF16)

    if geo.kind == "col":
        out_map = lambda i, place_ref: (i, place_ref[0])
    else:
        out_map = lambda i, place_ref: (place_ref[0] * nb + i, 0)
    return pl.pallas_call(
        body, name=name, out_shape=jax.ShapeDtypeStruct(geo.full_shape, BF16),
        grid_spec=pltpu.PrefetchScalarGridSpec(
            num_scalar_prefetch=1, grid=(nb,),
            in_specs=[pl.BlockSpec((tr, cs), lambda i, place_ref: (i, 0))],
            out_specs=pl.BlockSpec((tr, cs), out_map)),
        compiler_params=_params(1),
    )(place, shard)


def _gather_weights(fulls, geos, name):
    nw = len(fulls)

    def body(*refs):
        full = refs[nw:2 * nw]
        send_sems, recv_sems = refs[2 * nw:]
        x, y, c, chips = _place()
        j = 2 * x + y
        sibling = (x, y, 1 - c)
        first = []
        for w in range(nw):
            mine = geos[w].full_half(full[w], j, c)
            for k, chip in enumerate(chips):
                first.append(_remote(mine, mine, send_sems, recv_sems, 6 * w + k, (*chip, c)))
        for cp in first:
            cp.start()
        passed = []
        for w in range(nw):
            for k, (ox, oy) in enumerate(chips):
                landed = geos[w].full_half(full[w], 2 * ox + oy, c)
                _remote(landed, landed, send_sems, recv_sems, 6 * w + k, (ox, oy, c)).wait_recv()
                fwd = _remote(landed, landed, send_sems, recv_sems, 6 * w + 3 + k, sibling)
                fwd.start()
                passed.append(fwd)
        for w in range(nw):
            for k, (ox, oy) in enumerate(chips):
                other = geos[w].full_half(full[w], 2 * ox + oy, 1 - c)
                _remote(other, other, send_sems, recv_sems, 6 * w + 3 + k, sibling).wait_recv()
        for cp in first + passed:
            cp.wait_send()

    outs = [jax.ShapeDtypeStruct(g.full_shape, BF16) for g in geos]
    return _comm_call(body, name, fulls, outs, 6 * nw, aliases={w: w for w in range(nw)})


def _same(a):
    return jax.ShapeDtypeStruct(a.shape, a.dtype)


def _gather_ici(full, geo, p0, p1, pieces):
    def copies(ins, outs):
        x, y, c, chips = _place()
        mine = geo.piece(outs[0], 2 * x + y, c, p0, p1, pieces)
        return [(mine, mine, geo.piece(outs[0], 2 * ox + oy, c, p0, p1, pieces), (ox, oy, c)) for ox, oy in chips]

    return dict(ins=[full], outs=[_same(full)], aliases={0: 0}, n=3, copies=copies)


def _gather_d2d(full, geo):
    def copies(ins, outs):
        x, y, c, chips = _place()
        return [(geo.full_half(outs[0], 2 * ox + oy, c), geo.full_half(outs[0], 2 * ox + oy, c),
                 geo.full_half(outs[0], 2 * ox + oy, 1 - c), (x, y, 1 - c)) for ox, oy in chips]

    return dict(ins=[full], outs=[_same(full)], aliases={0: 0}, n=3, copies=copies)


def _swap_d2d(grad, geo):
    def copies(ins, outs):
        x, y, c, _ = _place()
        return [(geo.full_half(ins[0], j, 1 - c), outs[0].at[j], outs[0].at[j], (x, y, 1 - c)) for j in range(N_CHIPS)]

    return dict(ins=[grad], outs=[jax.ShapeDtypeStruct((N_CHIPS, geo.hr, geo.hc), BF16)], aliases={}, n=N_CHIPS,
                copies=copies)


def _scatter_ici(pair, recv, geo, p0, p1, pieces):
    pr = geo.hr // pieces
    rows = pl.ds(p0 * pr, (p1 - p0) * pr)

    def copies(ins, outs):
        x, y, c, chips = _place()
        return [(ins[0].at[2 * ox + oy, rows, :], outs[0].at[k, rows, :], outs[0].at[k, rows, :], (ox, oy, c))
                for k, (ox, oy) in enumerate(chips)]

    out = jax.ShapeDtypeStruct((3, geo.hr, geo.hc), BF16)
    if recv is None:
        return dict(ins=[pair], outs=[out], aliases={}, n=3, copies=copies)
    return dict(ins=[pair, recv], outs=[out], aliases={1: 0}, n=3, copies=copies)


def _join_d2d(shard, geo):
    def copies(ins, outs):
        x, y, c, _ = _place()
        mine = outs[0].at[pl.ds(pl.multiple_of(c * geo.hr, 8), geo.hr), :]
        other = outs[0].at[pl.ds(pl.multiple_of((1 - c) * geo.hr, 8), geo.hr), :]
        return [(mine, mine, other, (x, y, 1 - c))]

    return dict(ins=[shard], outs=[_same(shard)], aliases={0: 0}, n=1, copies=copies)


def _add_pair(grad, got, geo, place, name):
    tr = _pick(geo.hr, ROW_TILES)
    nb = geo.hr // tr

    def body(place_ref, a_ref, b_ref, o_ref):
        o_ref[0] = (a_ref[...].astype(F32) + b_ref[0].astype(F32)).astype(BF16)

    if geo.kind == "col":
        own_map = lambda j, i, place_ref: (place_ref[1] * nb + i, j)
    else:
        own_map = lambda j, i, place_ref: ((2 * j + place_ref[1]) * nb + i, 0)
    spec = pl.BlockSpec((1, tr, geo.hc), lambda j, i, place_ref: (j, i, 0))
    return pl.pallas_call(
        body, name=name, out_shape=jax.ShapeDtypeStruct((N_CHIPS, geo.hr, geo.hc), BF16),
        grid_spec=pltpu.PrefetchScalarGridSpec(
            num_scalar_prefetch=1, grid=(N_CHIPS, nb),
            in_specs=[pl.BlockSpec((tr, geo.hc), own_map), spec], out_specs=spec),
        compiler_params=_params(2),
    )(place, grad, got)


def _add_four(pair, recv, geo, place, name):
    tr = _pick(geo.hr, ROW_TILES)
    nb = geo.hr // tr

    def body(place_ref, p_ref, r_ref, o_ref):
        o_ref[...] = ((p_ref[0].astype(F32) + r_ref[0].astype(F32)) + r_ref[1].astype(F32)) + r_ref[2].astype(F32)

    return pl.pallas_call(
        body, name=name, out_shape=jax.ShapeDtypeStruct((2 * geo.hr, geo.hc), F32),
        grid_spec=pltpu.PrefetchScalarGridSpec(
            num_scalar_prefetch=1, grid=(nb,),
            in_specs=[pl.BlockSpec((1, tr, geo.hc), lambda i, place_ref: (place_ref[0], i, 0)),
                      pl.BlockSpec((3, tr, geo.hc), lambda i, place_ref: (0, i, 0))],
            out_specs=pl.BlockSpec((tr, geo.hc), lambda i, place_ref: (place_ref[1] * nb + i, 0))),
        compiler_params=_params(1),
    )(place, pair, recv)


PIECES = (4, 1, 8, 4)
SCHEDULE = {
    "proj_fwd": (("gather_ici", W_OUT, 0, 1), ("gather_ici", W_UP, 0, 1)),
    "hgrn_fwd": (("gather_ici", W_UP, 1, 5), ("gather_d2d", W_OUT)),
    "attn_fwd_d1": (("gather_ici", W_UP, 5, 6),),
    "attn_fwd_d4": (("gather_ici", W_UP, 6, 7),),
    "attn_fwd_d16": (("gather_ici", W_UP, 7, 8),),
    "mix_fwd": (("gather_d2d", W_UP),),
    "up_fwd": (("gather_ici", W_DOWN, 0, 4),),
    "conv_gate_fwd": (("gather_d2d", W_DOWN),),
    "down_bwd_x": (("swap", W_DOWN),),
    "conv_gate_bwd": (("scatter", W_DOWN, 0, 2),),
    "up_bwd_w": (("scatter", W_DOWN, 2, 4),),
    "up_bwd_x": (("swap", W_UP),),
    "norm2_bwd": (("scatter", W_UP, 0, 1), ("join", W_DOWN)),
    "mix_bwd_x": (("swap", W_OUT), ("scatter", W_UP, 1, 2)),
    "hgrn_bwd": (("scatter", W_UP, 2, 7),),
    "attn_bwd_d1": (("scatter", W_UP, 7, 8),),
    "attn_bwd_d4": (("scatter", W_OUT, 0, 1),),
    "attn_bwd_d16": (("join", W_UP), ("join", W_OUT)),
    "proj_bwd_x": (("swap", W_IN),),
    "norm1_bwd": (("scatter", W_IN, 0, 1),),
    "grad_in_scatter_rest": (("scatter", W_IN, 1, 4),),
    "grad_in_join": (("join", W_IN),),
}


class _Net:
    def __init__(self, shards, place):
        self.place = place
        self.geos = [_Geo(k, s.shape) for k, s in zip(WEIGHT_KINDS, shards)]
        self.full = [_cast_into_full(s, g, place, f"cast_shard_{w}") for w, (s, g) in enumerate(zip(shards, self.geos))]
        self.full[W_IN] = _gather_weights([self.full[W_IN]], [self.geos[W_IN]], "gather_w_in")[0]
        self.grad, self.pair, self.recv, self.shard = [None] * NW, [None] * NW, [None] * NW, [None] * NW
        self.slots = []

    def host(self, name):
        phase = _Phase()
        self.slots = []
        for item in SCHEDULE.get(name, ()):
            kind, w = item[0], item[1]
            geo = self.geos[w]
            if kind == "gather_ici":
                spec = _gather_ici(self.full[w], geo, item[2], item[3], PIECES[w])
            elif kind == "gather_d2d":
                spec = _gather_d2d(self.full[w], geo)
            elif kind == "swap":
                spec = _swap_d2d(self.grad[w], geo)
            elif kind == "scatter":
                spec = _scatter_ici(self.pair[w], self.recv[w], geo, item[2], item[3], PIECES[w])
            else:
                spec = _join_d2d(self.shard[w], geo)
            self.slots.append((item, phase.add(**spec)))
        return phase

    def done(self, name, comm):
        for item, sl in self.slots:
            kind, w = item[0], item[1]
            out = comm[sl][0]
            if kind in ("gather_ici", "gather_d2d"):
                self.full[w] = out
            elif kind == "swap":
                self.pair[w] = _add_pair(self.grad[w], out, self.geos[w], self.place, f"grad_pair_sum_{w}")
            elif kind == "scatter":
                self.recv[w] = out
                if item[3] == PIECES[w]:
                    self.shard[w] = _add_four(self.pair[w], out, self.geos[w], self.place, f"grad_chip_sum_{w}")
            else:
                self.shard[w] = out

    def alone(self, name):
        self.done(name, _comm_only(name, self.host(name)))


CONVW_SHARD = 3 * DFF // N_CHIPS
COND_PAD = 8 * 896
SMALL_SIZES = (("norm1_w", D), ("lb", HW), ("hg_norm_w", DH), ("q_norm_w", DH), ("k_norm_w", DH),
               ("norm2_w", D), ("conv_b", DFF), ("conv_w", 3 * DFF), ("loss", 128))
SMALL_TOTAL = sum(n for _, n in SMALL_SIZES)
STATS_PAD = 8 * 5120


def kernel(x, c, w_ada, b_ada, norm1_w, w_in, lb_logits, hg_norm_w, q_norm_w, k_norm_w, w_out, norm2_w, w_up, conv_w, conv_b, w_down, loss_target, m_w_ada, m_b_ada, m_norm1_w, m_w_in, m_lb_logits, m_hg_norm_w, m_q_norm_w, m_k_norm_w, m_w_out, m_norm2_w, m_w_up, m_conv_w, m_conv_b, m_w_down, v_w_ada, v_b_ada, v_norm1_w, v_w_in, v_lb_logits, v_hg_norm_w, v_q_norm_w, v_k_norm_w, v_w_out, v_norm2_w, v_w_up, v_conv_w, v_conv_b, v_w_down):
    chip = 2 * lax.axis_index("x") + lax.axis_index("y")
    dev = 2 * chip + lax.axis_index("c")

    cond = jnp.concatenate([c, conv_w[0].reshape(1, CONVW_SHARD), jnp.zeros((1, COND_PAD - D - CONVW_SHARD), F32)], axis=1)
    cond_all = _all_gather_rows(cond.reshape(8, COND_PAD // 8), "gather_cond").reshape(N_DEV, COND_PAD)
    c_all = cond_all[:, :D]
    conv_w_full = jnp.concatenate(
        [cond_all[2 * j, D:D + CONVW_SHARD].reshape(3, DFF // N_CHIPS) for j in range(N_CHIPS)], axis=1)

    b_shard = lax.dynamic_slice_in_dim(b_ada, chip * ADA_COLS, ADA_COLS, axis=1)
    mod_cols = _all_gather_rows(_ada_fwd(c_all, w_ada[0], b_shard), "gather_mod")
    mod_all = jnp.concatenate([mod_cols[16 * j:16 * j + 8] for j in range(N_CHIPS)], axis=1)
    mod = lax.dynamic_slice_in_dim(mod_all, dev, 1, axis=0)

    place = jnp.stack([chip, lax.axis_index("c")]).astype(jnp.int32)
    net = _Net([w_in[0], w_out[0], w_up[0], w_down[0]], place)
    loss_row, grad_x, dmod, small = _sequence_step(
        x[0], loss_target[0], mod, norm1_w, lb_logits, hg_norm_w, q_norm_w, k_norm_w, norm2_w, conv_w_full, conv_b, net)
    net.alone("grad_in_scatter_rest")
    net.alone("grad_in_join")
    g_in, g_out, g_up, g_down = net.shard

    small["conv_w"] = small["conv_w"].reshape(1, 3 * DFF)
    small["loss"] = loss_row
    stats = jnp.concatenate([dmod] + [small[k] for k, _ in SMALL_SIZES]
                            + [jnp.zeros((1, STATS_PAD - 6 * D - SMALL_TOTAL), F32)], axis=1)
    stats_all = _all_gather_rows(stats.reshape(8, STATS_PAD // 8), "gather_stats").reshape(N_DEV, STATS_PAD)
    dmod_all = stats_all[:, :6 * D]
    sums = _sum_rows(stats_all[:, 6 * D:6 * D + SMALL_TOTAL], "small_grad_sum")
    g_small, off = {}, 0
    for k, n in SMALL_SIZES:
        g_small[k] = sums[:, off:off + n]
        off += n
    loss = g_small["loss"][0, 0]
    g_b_ada = _sum_rows(dmod_all, "b_ada_grad_sum")
    g_w_ada = _ada_bwd(c_all, lax.dynamic_slice_in_dim(dmod_all, chip * ADA_COLS, ADA_COLS, axis=1))
    g_lb = _lb_grad(lb_logits, g_small["lb"])
    g_conv_w = lax.dynamic_slice_in_dim(g_small["conv_w"].reshape(3, DFF), chip * (DFF // N_CHIPS), DFF // N_CHIPS, axis=1)

    names = ["w_ada", "b_ada", "norm1_w", "w_in", "lb_logits", "hg_norm_w", "q_norm_w", "k_norm_w", "w_out",
             "norm2_w", "w_up", "conv_w", "conv_b", "w_down"]
    lead = {"w_ada", "w_in", "w_out", "w_up", "conv_w", "w_down"}
    w = dict(w_ada=w_ada, b_ada=b_ada, norm1_w=norm1_w, w_in=w_in, lb_logits=lb_logits, hg_norm_w=hg_norm_w,
             q_norm_w=q_norm_w, k_norm_w=k_norm_w, w_out=w_out, norm2_w=norm2_w, w_up=w_up, conv_w=conv_w,
             conv_b=conv_b, w_down=w_down)
    m = dict(w_ada=m_w_ada, b_ada=m_b_ada, norm1_w=m_norm1_w, w_in=m_w_in, lb_logits=m_lb_logits,
             hg_norm_w=m_hg_norm_w, q_norm_w=m_q_norm_w, k_norm_w=m_k_norm_w, w_out=m_w_out, norm2_w=m_norm2_w,
             w_up=m_w_up, conv_w=m_conv_w, conv_b=m_conv_b, w_down=m_w_down)
    v = dict(w_ada=v_w_ada, b_ada=v_b_ada, norm1_w=v_norm1_w, w_in=v_w_in, lb_logits=v_lb_logits,
             hg_norm_w=v_hg_norm_w, q_norm_w=v_q_norm_w, k_norm_w=v_k_norm_w, w_out=v_w_out, norm2_w=v_norm2_w,
             w_up=v_w_up, conv_w=v_conv_w, conv_b=v_conv_b, w_down=v_w_down)
    g = dict(w_ada=g_w_ada, b_ada=g_b_ada, norm1_w=g_small["norm1_w"], w_in=g_in, lb_logits=g_lb,
             hg_norm_w=g_small["hg_norm_w"], q_norm_w=g_small["q_norm_w"], k_norm_w=g_small["k_norm_w"], w_out=g_out,
             norm2_w=g_small["norm2_w"], w_up=g_up, conv_w=g_conv_w, conv_b=g_small["conv_b"], w_down=g_down)

    grads, deltas, new_m, new_v = [], [], [], []
    for n in names:
        strip = (lambda t: t[0]) if n in lead else (lambda t: t)
        wrap = (lambda t: t[None]) if n in lead else (lambda t: t)
        d_n, m_n, v_n = _adamw(strip(w[n]), g[n], strip(m[n]), strip(v[n]), f"adamw_{n}")
        grads.append(wrap(g[n]))
        deltas.append(wrap(d_n))
        new_m.append(wrap(m_n))
        new_v.append(wrap(v_n))
    return (loss, grad_x[None], *grads, *deltas, *new_m, *new_v)
```

```python
import functools
import math

import jax
import jax.numpy as jnp
from jax import lax
from jax.experimental import pallas as pl
from jax.experimental.pallas import tpu as pltpu

F32 = jnp.float32
BF16 = jnp.bfloat16

S = 2048
D = 2048
H = 8
DH = 128
HW = H * DH
CH = 64
NCH = S // CH
DFF = 5632
INC = 7 * HW
BLK = 128
DILS = (1, 4, 16)
EPS = 1e-6
NEG = -1e30
N_CHIPS = 4
N_DEV = 8

ADAM_LR = 0.001
ADAM_B1 = 0.9
ADAM_B2 = 0.999
ADAM_EPS = 1e-08
ADAM_WD = 0.01
ADAM_STEP = 10
ADAM_BLOCK_BYTES = 1 << 20

V7X_VMEM_BYTES = 64 * 1024 * 1024
VMEM_LIMIT = (V7X_VMEM_BYTES * 3) // 4
MESH = pl.DeviceIdType.MESH
HBM_SPEC = pl.BlockSpec(memory_space=pltpu.HBM)
VMEM_SPEC = pl.BlockSpec(memory_space=pltpu.VMEM)

NN = (((1,), (0,)), ((), ()))
NT = (((1,), (1,)), ((), ()))
TN = (((0,), (0,)), ((), ()))


def _params(n_grid):
    return pltpu.CompilerParams(dimension_semantics=("arbitrary",) * n_grid, vmem_limit_bytes=VMEM_LIMIT)


def _dot(a, b, dims=NN):
    return lax.dot_general(a.astype(BF16), b.astype(BF16), dims, preferred_element_type=F32)


def _dot_f32(a, b):
    return lax.dot_general(a, b, NN, precision=lax.Precision.HIGHEST, preferred_element_type=F32)


def _sigmoid(x):
    return 1.0 / (1.0 + jnp.exp(-x))


def _pick(n, cands):
    for t in cands:
        if n % t == 0:
            return t
    raise ValueError(n)


def _matmul(a, b, mode, out_dtype, name, phase=None):
    if mode == "nn":
        (m, k), (_, n) = a.shape, b.shape
    elif mode == "nt":
        (m, k), (n, _) = a.shape, b.shape
    else:
        (k, m), (_, n) = a.shape, b.shape
    tm = _pick(m, (1024, 512))
    tn = _pick(n, (1024, 512))
    tk = _pick(k, (2048, 2816, 1792, 1024, 512))
    nk = k // tk
    dims = {"nn": NN, "nt": NT, "tn": TN}[mode]

    def body(a_ref, b_ref, o_ref, *acc):
        part = lax.dot_general(a_ref[...], b_ref[...], dims, preferred_element_type=F32)
        if nk == 1:
            o_ref[...] = part.astype(o_ref.dtype)
            return
        acc_ref, kk = acc[0], pl.program_id(2)

        @pl.when(kk == 0)
        def _():
            acc_ref[...] = part

        @pl.when(jnp.logical_and(kk > 0, kk < nk - 1))
        def _():
            acc_ref[...] += part

        @pl.when(kk == nk - 1)
        def _():
            o_ref[...] = (acc_ref[...] + part).astype(o_ref.dtype)

    if mode == "tn":
        a_spec = pl.BlockSpec((tk, tm), lambda i, j, kk: (kk, i))
    else:
        a_spec = pl.BlockSpec((tm, tk), lambda i, j, kk: (i, kk))
    if mode == "nt":
        b_spec = pl.BlockSpec((tn, tk), lambda i, j, kk: (j, kk))
    else:
        b_spec = pl.BlockSpec((tk, tn), lambda i, j, kk: (kk, j))
    return _pcall(
        body, [a, b], phase, name=name,
        out_shape=jax.ShapeDtypeStruct((m, n), out_dtype),
        grid=(m // tm, n // tn, nk),
        in_specs=[a_spec, b_spec],
        out_specs=pl.BlockSpec((tm, tn), lambda i, j, kk: (i, j)),
        scratch_shapes=[pltpu.VMEM((tm, tn), F32)] if nk > 1 else [],
        compiler_params=_params(3),
    )


TR = 256


def _row_spec(cols=D):
    return pl.BlockSpec((TR, cols), lambda i: (i, 0))


def _vec_spec(cols=D):
    return pl.BlockSpec((1, cols), lambda i: (0, 0))


def _norm_mod(x, nw, scale, shift, name):
    def body(x_ref, nw_ref, sc_ref, sh_ref, h_ref):
        xv = x_ref[...]
        r = lax.rsqrt(jnp.mean(xv * xv, axis=-1, keepdims=True) + EPS)
        h_ref[...] = ((xv * r) * nw_ref[...] * (1.0 + sc_ref[...]) + sh_ref[...]).astype(BF16)

    return pl.pallas_call(
        body, name=name, out_shape=jax.ShapeDtypeStruct((S, D), BF16), grid=(S // TR,),
        in_specs=[_row_spec(), _vec_spec(), _vec_spec(), _vec_spec()], out_specs=_row_spec(),
        compiler_params=_params(1),
    )(x, nw, scale, shift)


def _resid_norm_mod(x, mix, gate, nw, scale, shift, name):
    def body(x_ref, mix_ref, g_ref, nw_ref, sc_ref, sh_ref, x1_ref, h_ref):
        xv = x_ref[...] + g_ref[...] * mix_ref[...]
        x1_ref[...] = xv
        r = lax.rsqrt(jnp.mean(xv * xv, axis=-1, keepdims=True) + EPS)
        h_ref[...] = ((xv * r) * nw_ref[...] * (1.0 + sc_ref[...]) + sh_ref[...]).astype(BF16)

    return pl.pallas_call(
        body, name=name,
        out_shape=(jax.ShapeDtypeStruct((S, D), F32), jax.ShapeDtypeStruct((S, D), BF16)), grid=(S // TR,),
        in_specs=[_row_spec(), _row_spec(), _vec_spec(), _vec_spec(), _vec_spec(), _vec_spec()],
        out_specs=(_row_spec(), _row_spec()),
        compiler_params=_params(1),
    )(x, mix, gate, nw, scale, shift)


def _norm_mod_bwd(dh, xin, nw, scale, dres, name, mix=None, gate=None, phase=None):
    with_gate = mix is not None

    def body(*refs):
        if with_gate:
            dh_ref, x_ref, nw_ref, sc_ref, dres_ref, mix_ref, g_ref, dx_ref, dsh_ref, dsc_ref, dnw_ref, dg_ref, dmix_ref = refs
        else:
            dh_ref, x_ref, nw_ref, sc_ref, dres_ref, dx_ref, dsh_ref, dsc_ref, dnw_ref = refs
        i = pl.program_id(0)
        dhv = dh_ref[...]
        xv = x_ref[...]
        r = lax.rsqrt(jnp.mean(xv * xv, axis=-1, keepdims=True) + EPS)
        xn = xv * r
        nwv = nw_ref[...]
        one_sc = 1.0 + sc_ref[...]
        dxn = dhv * nwv * one_sc
        dx = dres_ref[...] + r * (dxn - xn * jnp.mean(dxn * xn, axis=-1, keepdims=True))
        dx_ref[...] = dx

        @pl.when(i == 0)
        def _():
            dsh_ref[...] = jnp.zeros_like(dsh_ref)
            dsc_ref[...] = jnp.zeros_like(dsc_ref)
            dnw_ref[...] = jnp.zeros_like(dnw_ref)
            if with_gate:
                dg_ref[...] = jnp.zeros_like(dg_ref)

        dsh_ref[...] += jnp.sum(dhv, axis=0, keepdims=True)
        dsc_ref[...] += jnp.sum(dhv * xn * nwv, axis=0, keepdims=True)
        dnw_ref[...] += jnp.sum(dhv * xn * one_sc, axis=0, keepdims=True)
        if with_gate:
            dg_ref[...] += jnp.sum(dx * mix_ref[...], axis=0, keepdims=True)
            dmix_ref[...] = (dx * g_ref[...]).astype(BF16)

    vec = jax.ShapeDtypeStruct((1, D), F32)
    ins = [dh, xin, nw, scale, dres]
    in_specs = [_row_spec(), _row_spec(), _vec_spec(), _vec_spec(), _row_spec()]
    outs = [jax.ShapeDtypeStruct((S, D), F32), vec, vec, vec]
    out_specs = [_row_spec(), _vec_spec(), _vec_spec(), _vec_spec()]
    if with_gate:
        ins += [mix, gate]
        in_specs += [_row_spec(), _vec_spec()]
        outs += [vec, jax.ShapeDtypeStruct((S, D), BF16)]
        out_specs += [_vec_spec(), _row_spec()]
    return _pcall(
        body, ins, phase, name=name, out_shape=tuple(outs), grid=(S // TR,), in_specs=in_specs,
        out_specs=tuple(out_specs), compiler_params=_params(1),
    )


def _tri(lower):
    row = lax.broadcasted_iota(jnp.int32, (CH, CH), 0)
    col = lax.broadcasted_iota(jnp.int32, (CH, CH), 1)
    return (row >= col) if lower else (col >= row)


def _hgrn_gates(hq, hf, lb):
    sig = _sigmoid(hf)
    f = lb + (1.0 - lb) * sig
    g = jnp.log(f)
    sq = _sigmoid(hq)
    return sig, f, g, 1.0 - f, hq * sq, sq


HG_HP = 2


def _proj_col_spec(group):
    return pl.BlockSpec((S, HG_HP * DH), lambda h: (0, group * (H // HG_HP) + h))


def _hgrn_fwd(proj, lb_logits, norm_w, phase=None):
    def body(hq_ref, hf_ref, hi_ref, hg_ref, lbl_ref, nw_ref, out_ref, oraw_ref, st_ref, s_ref):
        nw = nw_ref[...]
        lower = _tri(True)
        ltri = lower.astype(F32)
        s_ref[...] = jnp.zeros_like(s_ref)

        def chunk(n, carry):
            rows = pl.ds(pl.multiple_of(n * CH, CH), CH)
            for j in range(HG_HP):
                cols = slice(j * DH, (j + 1) * DH)
                lb = 1.0 / (1.0 + jnp.exp(lbl_ref[1:2, cols] - lbl_ref[0:1, cols]))
                hq, hf, v, hg = hq_ref[rows, cols], hf_ref[rows, cols], hi_ref[rows, cols], hg_ref[rows, cols]
                _, _, g, kk, q, _ = _hgrn_gates(hq, hf, lb)
                gc = _dot_f32(ltri, g)
                gl = jnp.sum(g, axis=0, keepdims=True)
                qe = q * jnp.exp(gc)
                ke = kk * jnp.exp(gl - gc)
                qh = q * jnp.exp(gc - 0.5 * gl)
                kh = kk * jnp.exp(0.5 * gl - gc)
                st = s_ref[j]
                st_ref[j, pl.ds(n, 1)] = st[None]
                a = jnp.where(lower, _dot(qh, kh, NT), 0.0)
                o = _dot(qe, st, NT) + _dot(a, v)
                s_ref[j] = st * jnp.exp(gl) + _dot(v, ke, TN)
                oraw_ref[rows, cols] = o
                rs = lax.rsqrt(jnp.mean(o * o, axis=-1, keepdims=True) + EPS)
                out_ref[rows, cols] = (o * rs * nw * (hg * _sigmoid(hg))).astype(BF16)
            return carry

        lax.fori_loop(0, NCH, chunk, 0)

    head_spec = pl.BlockSpec((S, HG_HP * DH), lambda h: (0, h))
    return _pcall(
        body, [proj, proj, proj, proj, lb_logits, norm_w], phase, name="hgrn_fwd",
        out_shape=(jax.ShapeDtypeStruct((S, 2 * HW), BF16), jax.ShapeDtypeStruct((S, HW), F32),
                   jax.ShapeDtypeStruct((H, NCH, DH, DH), F32)),
        grid=(H // HG_HP,),
        in_specs=[_proj_col_spec(0), _proj_col_spec(1), _proj_col_spec(2), _proj_col_spec(3),
                  pl.BlockSpec((2, HG_HP * DH), lambda h: (0, h)), pl.BlockSpec((1, DH), lambda h: (0, 0))],
        out_specs=(head_spec, head_spec, pl.BlockSpec((HG_HP, NCH, DH, DH), lambda h: (h, 0, 0, 0))),
        scratch_shapes=[pltpu.VMEM((HG_HP, DH, DH), F32)],
        compiler_params=_params(1),
    )


def _hgrn_bwd(proj, lb_logits, norm_w, oraw, states, dout, phase=None):
    def body(hq_ref, hf_ref, hi_ref, hg_ref, lbl_ref, nw_ref, oraw_ref, st_ref, do_ref,
             dhq_ref, dhf_ref, dhi_ref, dhg_ref, dlb_ref, dnw_ref, ds_ref, acc_ref):
        h = pl.program_id(0)
        nw = nw_ref[...]
        lower = _tri(True)
        ltri = lower.astype(F32)
        utri = _tri(False).astype(F32)
        last = lax.broadcasted_iota(jnp.int32, (CH, DH), 0) == CH - 1
        ds_ref[...] = jnp.zeros_like(ds_ref)
        acc_ref[...] = jnp.zeros_like(acc_ref)

        @pl.when(h == 0)
        def _():
            dnw_ref[...] = jnp.zeros_like(dnw_ref)

        def chunk(i, carry):
            n = NCH - 1 - i
            rows = pl.ds(pl.multiple_of(n * CH, CH), CH)
            for j in range(HG_HP):
                cols = slice(j * DH, (j + 1) * DH)
                lb = 1.0 / (1.0 + jnp.exp(lbl_ref[1:2, cols] - lbl_ref[0:1, cols]))
                hq, hf, v, hg = hq_ref[rows, cols], hf_ref[rows, cols], hi_ref[rows, cols], hg_ref[rows, cols]
                sig, f, g, kk, q, sq = _hgrn_gates(hq, hf, lb)
                gc = _dot_f32(ltri, g)
                gl = jnp.sum(g, axis=0, keepdims=True)
                eg = jnp.exp(gc)
                ek = jnp.exp(gl - gc)
                gam = jnp.exp(gl)
                ph = jnp.exp(gc - 0.5 * gl)
                pk = jnp.exp(0.5 * gl - gc)
                qe, ke = q * eg, kk * ek
                qh, kh = q * ph, kk * pk
                st = st_ref[j, pl.ds(n, 1)][0]
                dst = ds_ref[j]
                a = jnp.where(lower, _dot(qh, kh, NT), 0.0)
                o = oraw_ref[rows, cols]
                rs = lax.rsqrt(jnp.mean(o * o, axis=-1, keepdims=True) + EPS)
                xh = o * rs
                sg = _sigmoid(hg)
                dgo = do_ref[rows, cols]
                d_on = dgo * (hg * sg)
                dhg_ref[rows, cols] = (dgo * xh * nw * (sg * (1.0 + hg * (1.0 - sg)))).astype(BF16)
                acc_ref[j, 1:2, :] += jnp.sum(d_on * xh, axis=0, keepdims=True)
                dy = d_on * nw
                do = rs * (dy - xh * jnp.mean(dy * xh, axis=-1, keepdims=True))
                dqe = _dot(do, st)
                da = jnp.where(lower, _dot(do, v, NT), 0.0)
                dv = _dot(a, do, TN) + _dot(ke, dst, NT)
                dke = _dot(v, dst)
                dgam = jnp.sum(dst * st, axis=0, keepdims=True)
                dqh = _dot(da, kh)
                dkh = _dot(da, qh, TN)
                dq = dqh * ph + dqe * eg
                dk = dkh * pk + dke * ek
                dgl = jnp.sum(dke * ke, axis=0, keepdims=True) + dgam * gam
                dgc = dqh * qh - dkh * kh + dqe * qe - dke * ke + jnp.where(last, dgl, 0.0)
                dg = _dot_f32(utri, dgc)
                df = dg / f - dk
                dhf_ref[rows, cols] = (df * (1.0 - lb) * sig * (1.0 - sig)).astype(BF16)
                acc_ref[j, 0:1, :] += jnp.sum(df * (1.0 - sig), axis=0, keepdims=True)
                dhq_ref[rows, cols] = (dq * (sq * (1.0 + hq * (1.0 - sq)))).astype(BF16)
                dhi_ref[rows, cols] = dv.astype(BF16)
                ds_ref[j] = dst * gam + _dot(do, qe, TN)
            return carry

        lax.fori_loop(0, NCH, chunk, 0)
        for j in range(HG_HP):
            dlb_ref[:, j * DH:(j + 1) * DH] = acc_ref[j, 0:1, :]
            dnw_ref[...] += acc_ref[j, 1:2, :]

    head_spec = pl.BlockSpec((S, HG_HP * DH), lambda h: (0, h))
    head_out = jax.ShapeDtypeStruct((S, HW), BF16)
    return _pcall(
        body, [proj, proj, proj, proj, lb_logits, norm_w, oraw, states, dout], phase, name="hgrn_bwd",
        out_shape=(head_out, head_out, head_out, head_out,
                   jax.ShapeDtypeStruct((1, HW), F32), jax.ShapeDtypeStruct((1, DH), F32)),
        grid=(H // HG_HP,),
        in_specs=[_proj_col_spec(0), _proj_col_spec(1), _proj_col_spec(2), _proj_col_spec(3),
                  pl.BlockSpec((2, HG_HP * DH), lambda h: (0, h)), pl.BlockSpec((1, DH), lambda h: (0, 0)),
                  head_spec, pl.BlockSpec((HG_HP, NCH, DH, DH), lambda h: (h, 0, 0, 0)), head_spec],
        out_specs=(head_spec, head_spec, head_spec, head_spec,
                   pl.BlockSpec((1, HG_HP * DH), lambda h: (0, h)), pl.BlockSpec((1, DH), lambda h: (0, 0))),
        scratch_shapes=[pltpu.VMEM((HG_HP, DH, DH), F32), pltpu.VMEM((HG_HP, 8, DH), F32)],
        compiler_params=_params(1),
    )


ATT_SCALE = DH ** -0.5
HEADS_PER_STEP = {1: 8, 4: 1, 16: 1}


def _sub_rows(ref, r, dil, cols):
    if dil == 1:
        return ref[:, cols]
    return ref[pl.ds(r, BLK, stride=dil), cols]


def _set_sub_rows(ref, r, dil, cols, val):
    if dil == 1:
        ref[:, cols] = val
    else:
        ref[pl.ds(r, BLK, stride=dil), cols] = val


def _slopes(dil):
    hp = HEADS_PER_STEP[dil]
    s = jnp.asarray([dil * 2.0 ** (-(h + 1)) for h in range(H)], F32)
    return jnp.broadcast_to(s[:, None], (H, DH)).reshape(H // hp, hp, DH)


def _rms_rows(x, w):
    rs = lax.rsqrt(jnp.mean(x * x, axis=-1, keepdims=True) + EPS)
    return x * rs, rs


def _att_masks():
    qi = lax.broadcasted_iota(jnp.int32, (BLK, BLK), 0)
    kj = lax.broadcasted_iota(jnp.int32, (BLK, BLK), 1)
    steps_c = qi - kj
    steps_p = qi - kj + BLK
    return steps_c, steps_p, steps_c >= 0, steps_p <= BLK


def _attn_fwd(proj, q_w, k_w, dil, phase=None):
    hp = HEADS_PER_STEP[dil]
    rows, ng, nb = BLK * dil, H // hp, S // (BLK * dil)

    def body(q_ref, kc_ref, kp_ref, vc_ref, vp_ref, qw_ref, kw_ref, sl_ref, o_ref, lse_ref):
        n = pl.program_id(0)
        steps_c, steps_p, ok_c, ok_p = _att_masks()
        ok_p = jnp.logical_and(ok_p, n > 0)
        fc, fp = steps_c.astype(F32), steps_p.astype(F32)
        qw, kw = qw_ref[...], kw_ref[...]
        for hh in range(hp):
            cols = slice(hh * DH, (hh + 1) * DH)
            sl = sl_ref[0, hh:hh + 1, :]
            for r in range(dil):
                qn = _rms_rows(_sub_rows(q_ref, r, dil, cols), None)[0] * qw
                kcn = _rms_rows(_sub_rows(kc_ref, r, dil, cols), None)[0] * kw
                kpn = _rms_rows(_sub_rows(kp_ref, r, dil, cols), None)[0] * kw
                sc = jnp.where(ok_c, _dot(qn, kcn, NT) * ATT_SCALE - sl * fc, NEG)
                sp = jnp.where(ok_p, _dot(qn, kpn, NT) * ATT_SCALE - sl * fp, NEG)
                m = jnp.maximum(jnp.max(sc, axis=-1, keepdims=True), jnp.max(sp, axis=-1, keepdims=True))
                pc, pp = jnp.exp(sc - m), jnp.exp(sp - m)
                den = jnp.sum(pc, axis=-1, keepdims=True) + jnp.sum(pp, axis=-1, keepdims=True)
                o = (_dot(pc, _sub_rows(vc_ref, r, dil, cols)) + _dot(pp, _sub_rows(vp_ref, r, dil, cols))) / den
                _set_sub_rows(o_ref, r, dil, cols, o)
                _set_sub_rows(lse_ref, r, dil, cols, jnp.broadcast_to(m + jnp.log(den), (BLK, DH)))

    def spec(group, prev):
        if prev:
            return pl.BlockSpec((rows, hp * DH), lambda n, g: (jnp.maximum(n - 1, 0), group * ng + g))
        return pl.BlockSpec((rows, hp * DH), lambda n, g: (n, group * ng + g))

    out = jax.ShapeDtypeStruct((S, HW), F32)
    out_spec = pl.BlockSpec((rows, hp * DH), lambda n, g: (n, g))
    wspec = pl.BlockSpec((1, DH), lambda n, g: (0, 0))
    return _pcall(
        body, [proj, proj, proj, proj, proj, q_w, k_w, _slopes(dil)], phase,
        name=f"attn_fwd_d{dil}", out_shape=(out, out), grid=(nb, ng),
        in_specs=[spec(4, False), spec(5, False), spec(5, True), spec(6, False), spec(6, True), wspec, wspec,
                  pl.BlockSpec((1, hp, DH), lambda n, g: (g, 0, 0))],
        out_specs=(out_spec, out_spec),
        compiler_params=_params(2),
    )


def _attn_merge(outs, lses, cat):
    def body(o1, o2, o3, l1, l2, l3, cat_ref, ob_ref, of_ref, lse_ref):
        a, b, c = l1[...], l2[...], l3[...]
        m = jnp.maximum(jnp.maximum(a, b), c)
        ea, eb, ec = jnp.exp(a - m), jnp.exp(b - m), jnp.exp(c - m)
        den = ea + eb + ec
        o = (ea * o1[...] + eb * o2[...] + ec * o3[...]) / den
        ob_ref[...] = o.astype(BF16)
        of_ref[...] = o
        lse_ref[...] = m + jnp.log(den)

    f = jax.ShapeDtypeStruct((S, HW), F32)
    return pl.pallas_call(
        body, name="attn_merge", out_shape=(jax.ShapeDtypeStruct((S, 2 * HW), BF16), f, f), grid=(S // TR,),
        in_specs=[_row_spec(HW)] * 6 + [pl.BlockSpec(memory_space=pl.ANY)],
        out_specs=(pl.BlockSpec((TR, HW), lambda i: (i, 1)), _row_spec(HW), _row_spec(HW)),
        input_output_aliases={6: 0},
        compiler_params=_params(1),
    )(*outs, *lses, cat)


def _attn_bwd(proj, q_w, k_w, o, lse, do, dil, acc, phase=None):
    hp = HEADS_PER_STEP[dil]
    rows, ng, nb = BLK * dil, H // hp, S // (BLK * dil)
    has_acc = acc is not None

    def body(*refs):
        (q_ref, qn_ref, kp_ref, kc_ref, vp_ref, vc_ref, o_ref, on_ref, l_ref, ln_ref, do_ref, don_ref,
         qw_ref, kw_ref, sl_ref) = refs[:15]
        refs = refs[15:]
        if has_acc:
            aq_ref, ak_ref, av_ref, aqw_ref, akw_ref = refs[:5]
            refs = refs[5:]
        dq_ref, dk_ref, dv_ref, dqw_ref, dkw_ref = refs
        m, g = pl.program_id(0), pl.program_id(1)
        steps_c, steps_p, ok_c, ok_p = _att_masks()
        ok_prev = jnp.logical_and(ok_p, m > 0)
        ok_next = jnp.logical_and(ok_p, m < nb - 1)
        fc, fp = steps_c.astype(F32), steps_p.astype(F32)
        qw, kw = qw_ref[...], kw_ref[...]

        @pl.when(jnp.logical_and(m == 0, g == 0))
        def _():
            if has_acc:
                dqw_ref[...] = aqw_ref[...]
                dkw_ref[...] = akw_ref[...]
            else:
                dqw_ref[...] = jnp.zeros_like(dqw_ref)
                dkw_ref[...] = jnp.zeros_like(dkw_ref)

        for hh in range(hp):
            cols = slice(hh * DH, (hh + 1) * DH)
            sl = sl_ref[0, hh:hh + 1, :]
            for r in range(dil):
                sub = lambda ref: _sub_rows(ref, r, dil, cols)
                qx, qrs = _rms_rows(sub(q_ref), None)
                kx, krs = _rms_rows(sub(kc_ref), None)
                qn, kcn = qx * qw, kx * kw
                qnn = _rms_rows(sub(qn_ref), None)[0] * qw
                kpn = _rms_rows(sub(kp_ref), None)[0] * kw
                vc, vp = sub(vc_ref), sub(vp_ref)
                dov, donv = sub(do_ref), sub(don_ref)
                lse_m, lse_n = sub(l_ref)[:, 0:1], sub(ln_ref)[:, 0:1]
                delta_m = jnp.sum(dov * sub(o_ref), axis=-1, keepdims=True)
                delta_n = jnp.sum(donv * sub(on_ref), axis=-1, keepdims=True)
                p_c = jnp.where(ok_c, jnp.exp(_dot(qn, kcn, NT) * ATT_SCALE - sl * fc - lse_m), 0.0)
                p_p = jnp.where(ok_prev, jnp.exp(_dot(qn, kpn, NT) * ATT_SCALE - sl * fp - lse_m), 0.0)
                p_n = jnp.where(ok_next, jnp.exp(_dot(qnn, kcn, NT) * ATT_SCALE - sl * fp - lse_n), 0.0)
                ds_c = p_c * (_dot(dov, vc, NT) - delta_m)
                ds_p = p_p * (_dot(dov, vp, NT) - delta_m)
                ds_n = p_n * (_dot(donv, vc, NT) - delta_n)
                dqn = (_dot(ds_c, kcn) + _dot(ds_p, kpn)) * ATT_SCALE
                dkn = (_dot(ds_c, qn, TN) + _dot(ds_n, qnn, TN)) * ATT_SCALE
                dv = _dot(p_c, dov, TN) + _dot(p_n, donv, TN)
                dqw_ref[...] += jnp.sum(dqn * qx, axis=0, keepdims=True)
                dkw_ref[...] += jnp.sum(dkn * kx, axis=0, keepdims=True)
                dyq, dyk = dqn * qw, dkn * kw
                daq = qrs * (dyq - qx * jnp.mean(dyq * qx, axis=-1, keepdims=True))
                dak = krs * (dyk - kx * jnp.mean(dyk * kx, axis=-1, keepdims=True))
                if has_acc:
                    daq, dak, dv = daq + sub(aq_ref), dak + sub(ak_ref), dv + sub(av_ref)
                _set_sub_rows(dq_ref, r, dil, cols, daq)
                _set_sub_rows(dk_ref, r, dil, cols, dak)
                _set_sub_rows(dv_ref, r, dil, cols, dv)

    def shifted(shift, m):
        if shift < 0:
            return jnp.maximum(m - 1, 0)
        if shift > 0:
            return jnp.minimum(m + 1, nb - 1)
        return m

    def pspec(group, shift):
        return pl.BlockSpec((rows, hp * DH), lambda m, g: (shifted(shift, m), group * ng + g))

    cur = pl.BlockSpec((rows, hp * DH), lambda m, g: (m, g))
    nxt = pl.BlockSpec((rows, hp * DH), lambda m, g: (shifted(1, m), g))
    wspec = pl.BlockSpec((1, DH), lambda m, g: (0, 0))
    ins = [proj, proj, proj, proj, proj, proj, o, o, lse, lse, do, do, q_w, k_w, _slopes(dil)]
    in_specs = [pspec(4, 0), pspec(4, 1), pspec(5, -1), pspec(5, 0), pspec(6, -1), pspec(6, 0),
                cur, nxt, cur, nxt, pspec(1, 0), pspec(1, 1), wspec, wspec,
                pl.BlockSpec((1, hp, DH), lambda m, g: (g, 0, 0))]
    if has_acc:
        ins += list(acc)
        in_specs += [cur, cur, cur, wspec, wspec]
    big = jax.ShapeDtypeStruct((S, HW), F32)
    small = jax.ShapeDtypeStruct((1, DH), F32)
    return _pcall(
        body, ins, phase, name=f"attn_bwd_d{dil}", out_shape=(big, big, big, small, small), grid=(nb, ng),
        in_specs=in_specs, out_specs=(cur, cur, cur, wspec, wspec),
        compiler_params=_params(2),
    )


TC = 512
NCT = DFF // TC


def _shift_rows(a, k):
    rows = lax.broadcasted_iota(jnp.int32, a.shape, 0)
    rolled = pltpu.roll(a, k % S, 0)
    if k > 0:
        return jnp.where(rows >= k, rolled, 0.0)
    return jnp.where(rows < S + k, rolled, 0.0)


def _conv_pre(a, w_ref, b_ref):
    return b_ref[...] + w_ref[0:1, :] * _shift_rows(a, 2) + w_ref[1:2, :] * _shift_rows(a, 1) + w_ref[2:3, :] * a


def _conv_gate_fwd(u, conv_w, conv_b, phase=None):
    def body(a_ref, g_ref, w_ref, b_ref, y_ref):
        pre = _conv_pre(a_ref[...].astype(F32), w_ref, b_ref)
        y_ref[...] = (pre * _sigmoid(pre) * g_ref[...].astype(F32)).astype(BF16)

    return _pcall(
        body, [u, u, conv_w, conv_b], phase,
        name="conv_gate_fwd", out_shape=jax.ShapeDtypeStruct((S, DFF), BF16), grid=(NCT,),
        in_specs=[pl.BlockSpec((S, TC), lambda i: (0, i)), pl.BlockSpec((S, TC), lambda i: (0, NCT + i)),
                  pl.BlockSpec((3, TC), lambda i: (0, i)), pl.BlockSpec((1, TC), lambda i: (0, i))],
        out_specs=pl.BlockSpec((S, TC), lambda i: (0, i)),
        compiler_params=_params(1),
    )


def _conv_gate_bwd(dy, u, conv_w, conv_b, phase=None):
    def body(dy_ref, a_ref, g_ref, w_ref, b_ref, da_ref, dg_ref, db_ref, dw_ref):
        a = a_ref[...].astype(F32)
        dyv = dy_ref[...].astype(F32)
        pre = _conv_pre(a, w_ref, b_ref)
        sg = _sigmoid(pre)
        dg_ref[...] = (dyv * pre * sg).astype(BF16)
        dpre = dyv * g_ref[...].astype(F32) * (sg * (1.0 + pre * (1.0 - sg)))
        da = w_ref[2:3, :] * dpre + w_ref[1:2, :] * _shift_rows(dpre, -1) + w_ref[0:1, :] * _shift_rows(dpre, -2)
        da_ref[...] = da.astype(BF16)
        db_ref[...] = jnp.sum(dpre, axis=0, keepdims=True)
        dw_ref[0:1, :] = jnp.sum(dpre * _shift_rows(a, 2), axis=0, keepdims=True)
        dw_ref[1:2, :] = jnp.sum(dpre * _shift_rows(a, 1), axis=0, keepdims=True)
        dw_ref[2:3, :] = jnp.sum(dpre * a, axis=0, keepdims=True)

    col = pl.BlockSpec((S, TC), lambda i: (0, i))
    half = jax.ShapeDtypeStruct((S, DFF), BF16)
    return _pcall(
        body, [dy, u, u, conv_w, conv_b], phase, name="conv_gate_bwd",
        out_shape=(half, half, jax.ShapeDtypeStruct((1, DFF), F32), jax.ShapeDtypeStruct((3, DFF), F32)),
        grid=(NCT,),
        in_specs=[col, col, pl.BlockSpec((S, TC), lambda i: (0, NCT + i)),
                  pl.BlockSpec((3, TC), lambda i: (0, i)), pl.BlockSpec((1, TC), lambda i: (0, i))],
        out_specs=(col, col, pl.BlockSpec((1, TC), lambda i: (0, i)), pl.BlockSpec((3, TC), lambda i: (0, i))),
        compiler_params=_params(1),
    )


def _out_loss(z, x1, target, gate):
    def body(z_ref, x_ref, t_ref, g_ref, do_ref, dz_ref, dg_ref, loss_ref):
        i = pl.program_id(0)
        zv = z_ref[...]
        gv = g_ref[...]
        err = x_ref[...] + gv * zv - t_ref[...]
        dout = err * (1.0 / D)
        do_ref[...] = dout
        dz_ref[...] = (dout * gv).astype(BF16)

        @pl.when(i == 0)
        def _():
            dg_ref[...] = jnp.zeros_like(dg_ref)
            loss_ref[...] = jnp.zeros_like(loss_ref)

        dg_ref[...] += jnp.sum(dout * zv, axis=0, keepdims=True)
        part = jnp.sum(jnp.sum(err * err, axis=0, keepdims=True), axis=-1, keepdims=True) * (0.5 / D)
        lane = lax.broadcasted_iota(jnp.int32, (1, 128), 1)
        loss_ref[...] += jnp.where(lane == 0, part, 0.0)

    return pl.pallas_call(
        body, name="out_loss",
        out_shape=(jax.ShapeDtypeStruct((S, D), F32), jax.ShapeDtypeStruct((S, D), BF16),
                   jax.ShapeDtypeStruct((1, D), F32), jax.ShapeDtypeStruct((1, 128), F32)),
        grid=(S // TR,),
        in_specs=[_row_spec(), _row_spec(), _row_spec(), _vec_spec()],
        out_specs=(_row_spec(), _row_spec(), _vec_spec(), _vec_spec(128)),
        compiler_params=_params(1),
    )(z, x1, target, gate)


ADA_COLS = 6 * D // N_CHIPS
TA = 512


def _ada_fwd(c_all, w_shard, b_shard):
    def body(c_ref, w_ref, b_ref, o_ref):
        cv = c_ref[...]
        o_ref[...] = _dot_f32(cv * _sigmoid(cv), w_ref[...]) + b_ref[...]

    return pl.pallas_call(
        body, name="ada_fwd", out_shape=jax.ShapeDtypeStruct((N_DEV, ADA_COLS), F32), grid=(ADA_COLS // TA,),
        in_specs=[pl.BlockSpec((N_DEV, D), lambda j: (0, 0)), pl.BlockSpec((D, TA), lambda j: (0, j)),
                  pl.BlockSpec((1, TA), lambda j: (0, j))],
        out_specs=pl.BlockSpec((N_DEV, TA), lambda j: (0, j)),
        compiler_params=_params(1),
    )(c_all, w_shard, b_shard)


def _ada_bwd(c_all, dmod_shard):
    def body(c_ref, d_ref, o_ref):
        cv = c_ref[...]
        o_ref[...] = lax.dot_general(cv * _sigmoid(cv), d_ref[...], TN, precision=lax.Precision.HIGHEST,
                                     preferred_element_type=F32)

    return pl.pallas_call(
        body, name="ada_bwd", out_shape=jax.ShapeDtypeStruct((D, ADA_COLS), F32), grid=(ADA_COLS // TA,),
        in_specs=[pl.BlockSpec((N_DEV, D), lambda j: (0, 0)), pl.BlockSpec((N_DEV, TA), lambda j: (0, j))],
        out_specs=pl.BlockSpec((D, TA), lambda j: (0, j)),
        compiler_params=_params(1),
    )(c_all, dmod_shard)


def _sum_rows(g, name):
    rows, n = g.shape

    def body(g_ref, o_ref):
        acc = g_ref[0:1, :]
        for i in range(1, rows):
            acc = acc + g_ref[i:i + 1, :]
        o_ref[...] = acc

    return pl.pallas_call(
        body, name=name, out_shape=jax.ShapeDtypeStruct((1, n), F32),
        in_specs=[VMEM_SPEC], out_specs=VMEM_SPEC,
    )(g)


def _lb_grad(lb_logits, dlb):
    def body(l_ref, d_ref, o_ref):
        p = 1.0 / (1.0 + jnp.exp(l_ref[1:2, :] - l_ref[0:1, :]))
        t = d_ref[...] * p * (1.0 - p)
        o_ref[0:1, :] = t
        o_ref[1:2, :] = -t

    return pl.pallas_call(
        body, name="lb_grad", out_shape=jax.ShapeDtypeStruct((2, HW), F32),
        in_specs=[VMEM_SPEC, VMEM_SPEC], out_specs=VMEM_SPEC,
    )(lb_logits, dlb)


def _adamw(w, g, m, v, name):
    rows, cols = w.shape
    tr = rows
    if rows * cols * 4 > ADAM_BLOCK_BYTES:
        tr = _pick(rows, [t for t in (256, 128, 64, 32, 16, 8) if t * cols * 4 <= ADAM_BLOCK_BYTES])

    def body(w_ref, g_ref, m_ref, v_ref, d_ref, mo_ref, vo_ref):
        gv = g_ref[...]
        m2 = ADAM_B1 * m_ref[...] + (1.0 - ADAM_B1) * gv
        v2 = ADAM_B2 * v_ref[...] + (1.0 - ADAM_B2) * (gv * gv)
        m_hat = m2 / (1.0 - ADAM_B1 ** ADAM_STEP)
        v_hat = v2 / (1.0 - ADAM_B2 ** ADAM_STEP)
        d_ref[...] = -ADAM_LR * (m_hat / (jnp.sqrt(v_hat) + ADAM_EPS) + ADAM_WD * w_ref[...])
        mo_ref[...] = m2
        vo_ref[...] = v2

    spec = pl.BlockSpec((tr, cols), lambda i: (i, 0))
    out = jax.ShapeDtypeStruct((rows, cols), F32)
    return pl.pallas_call(
        body, name=name, out_shape=(out, out, out), grid=(rows // tr,),
        in_specs=[spec] * 4, out_specs=(spec,) * 3,
        compiler_params=_params(1),
    )(w, g, m, v)


W_IN, W_OUT, W_UP, W_DOWN = range(4)


def _sequence_step(x, target, mod, norm1_w, lb_logits, hg_norm_w, q_norm_w, k_norm_w, norm2_w, conv_w, conv_b, net):
    shift1, scale1, gate1, shift2, scale2, gate2 = [mod[:, i * D:(i + 1) * D] for i in range(6)]

    def run(name, fn, *args, **kw):
        phase = net.host(name)
        if phase is None:
            return fn(*args, **kw)
        res, comm = fn(*args, phase=phase, **kw)
        net.done(name, comm)
        return res

    h = _norm_mod(x, norm1_w, scale1, shift1, "norm1_fwd")
    proj = run("proj_fwd", _matmul, h, net.full[W_IN], "nn", F32, "proj_fwd")
    cat, o_raw, states = run("hgrn_fwd", _hgrn_fwd, proj, lb_logits, hg_norm_w)
    att = [run(f"attn_fwd_d{dil}", _attn_fwd, proj, q_norm_w, k_norm_w, dil) for dil in DILS]
    cat, b_out_f32, lse = _attn_merge([t[0] for t in att], [t[1] for t in att], cat)
    mix = run("mix_fwd", _matmul, cat, net.full[W_OUT], "nn", F32, "mix_fwd")
    x1, h2 = _resid_norm_mod(x, mix, gate1, norm2_w, scale2, shift2, "norm2_fwd")
    u = run("up_fwd", _matmul, h2, net.full[W_UP], "nn", BF16, "up_fwd")
    yact = run("conv_gate_fwd", _conv_gate_fwd, u, conv_w, conv_b)
    z = _matmul(yact, net.full[W_DOWN], "nn", F32, "down_fwd")
    dout, dz, dgate2, loss_row = _out_loss(z, x1, target, gate2)

    net.grad[W_DOWN] = _matmul(yact, dz, "tn", BF16, "down_bwd_w")
    dyact = run("down_bwd_x", _matmul, dz, net.full[W_DOWN], "nt", BF16, "down_bwd_x")
    da, dg, dconv_b, dconv_w = run("conv_gate_bwd", _conv_gate_bwd, dyact, u, conv_w, conv_b)
    du = jnp.concatenate([da, dg], axis=1)
    net.grad[W_UP] = run("up_bwd_w", _matmul, h2, du, "tn", BF16, "up_bwd_w")
    dh2 = run("up_bwd_x", _matmul, du, net.full[W_UP], "nt", F32, "up_bwd_x")
    dx1, dshift2, dscale2, dnorm2, dgate1, dmix = run(
        "norm2_bwd", _norm_mod_bwd, dh2, x1, norm2_w, scale2, dout, "norm2_bwd", mix=mix, gate=gate1)
    net.grad[W_OUT] = _matmul(cat, dmix, "tn", BF16, "mix_bwd_w")
    dcat = run("mix_bwd_x", _matmul, dmix, net.full[W_OUT], "nt", F32, "mix_bwd_x")
    dhq, dhf, dhi, dhg, dlb, dhg_norm = run("hgrn_bwd", _hgrn_bwd, proj, lb_logits, hg_norm_w, o_raw, states, dcat)
    acc = None
    for dil in DILS:
        acc = run(f"attn_bwd_d{dil}", _attn_bwd, proj, q_norm_w, k_norm_w, b_out_f32, lse, dcat, dil, acc)
    daq, dak, dav, dq_norm, dk_norm = acc
    dproj = jnp.concatenate([dhq, dhf, dhi, dhg, daq.astype(BF16), dak.astype(BF16), dav.astype(BF16)], axis=1)
    net.grad[W_IN] = _matmul(h, dproj, "tn", BF16, "proj_bwd_w")
    dh = run("proj_bwd_x", _matmul, dproj, net.full[W_IN], "nt", F32, "proj_bwd_x")
    grad_x, dshift1, dscale1, dnorm1 = run("norm1_bwd", _norm_mod_bwd, dh, x, norm1_w, scale1, dx1, "norm1_bwd")

    dmod = jnp.concatenate([dshift1, dscale1, dgate1, dshift2, dscale2, dgate2], axis=1)
    small = dict(norm1_w=dnorm1, lb=dlb, hg_norm_w=dhg_norm, q_norm_w=dq_norm, k_norm_w=dk_norm,
                 norm2_w=dnorm2, conv_b=dconv_b, conv_w=dconv_w)
    return loss_row, grad_x, dmod, small


def _place():
    x, y, c = lax.axis_index("x"), lax.axis_index("y"), lax.axis_index("c")
    others = [(1 - x, y), (x, 1 - y), (1 - x, 1 - y)]
    return x, y, c, others


def _remote(src, dst, send_sems, recv_sems, k, to):
    return pltpu.make_async_remote_copy(src_ref=src, dst_ref=dst, send_sem=send_sems.at[k], recv_sem=recv_sems.at[k],
                                        device_id=to, device_id_type=MESH)


class _Phase:
    def __init__(self):
        self.ins, self.outs, self.aliases, self.groups, self.n = [], [], {}, [], 0

    def add(self, ins, outs, aliases, n, copies):
        i0, o0 = len(self.ins), len(self.outs)
        self.ins += list(ins)
        self.outs += list(outs)
        self.aliases.update({i0 + i: o0 + o for i, o in aliases.items()})
        self.groups.append((slice(i0, i0 + len(ins)), slice(o0, o0 + len(outs)), copies))
        self.n += n
        return slice(o0, o0 + len(outs))

    def build(self, cin, cout, send_sems, recv_sems, landing):
        res = []
        for si, so, copies in self.groups:
            for src, dst, land, peer in copies(cin[si], cout[so]):
                k = len(res)
                res.append(_remote(land, land, send_sems, recv_sems, k, peer) if landing
                           else _remote(src, dst, send_sems, recv_sems, k, peer))
        assert len(res) == self.n
        return res


def _pcall(body, args, phase=None, **kw):
    if phase is None or not phase.groups:
        res = pl.pallas_call(body, **kw)(*args)
        return res if phase is None else (res, ())
    grid = tuple(kw.pop("grid", ()))
    out_shape, out_specs = kw.pop("out_shape"), kw.pop("out_specs")
    single = not isinstance(out_shape, (tuple, list))
    out_shape = [out_shape] if single else list(out_shape)
    out_specs = [out_specs] if single else list(out_specs)
    scratch = list(kw.pop("scratch_shapes", ()))
    n_in, n_out, n_ci, n_co = len(args), len(out_shape), len(phase.ins), len(phase.outs)

    def hosted(*refs):
        ins, cin = refs[:n_in], refs[n_in:n_in + n_ci]
        outs = refs[n_in + n_ci:n_in + n_ci + n_out]
        cout = refs[n_in + n_ci + n_out:n_in + n_ci + n_out + n_co]
        scr, send_sems, recv_sems = refs[n_in + n_ci + n_out + n_co:-2], refs[-2], refs[-1]
        ids = [pl.program_id(a) for a in range(len(grid))]

        def start():
            for cp in phase.build(cin, cout, send_sems, recv_sems, False):
                cp.start()

        def finish():
            for cp in phase.build(cin, cout, send_sems, recv_sems, False):
                cp.wait_send()
            for cp in phase.build(cin, cout, send_sems, recv_sems, True):
                cp.wait_recv()

        if grid:
            first = functools.reduce(jnp.logical_and, [i == 0 for i in ids])
            last = functools.reduce(jnp.logical_and, [i == g - 1 for i, g in zip(ids, grid)])
            pl.when(first)(start)
            body(*ins, *outs, *scr)
            pl.when(last)(finish)
        else:
            start()
            body(*ins, *outs, *scr)
            finish()

    res = pl.pallas_call(
        hosted, out_shape=tuple(out_shape + phase.outs), grid=grid,
        in_specs=list(kw.pop("in_specs")) + [HBM_SPEC] * n_ci, out_specs=tuple(out_specs + [HBM_SPEC] * n_co),
        input_output_aliases={n_in + i: n_out + o for i, o in phase.aliases.items()},
        scratch_shapes=scratch + [pltpu.SemaphoreType.DMA((phase.n,)), pltpu.SemaphoreType.DMA((phase.n,))],
        **kw,
    )(*args, *phase.ins)
    outs = res[:n_out]
    return (outs[0] if single else tuple(outs)), tuple(res[n_out:])


def _comm_only(name, phase):
    return _pcall(lambda: None, [], phase, name=name, out_shape=(), in_specs=[], out_specs=())[1]


def _all_gather_rows(block, name):
    m_per, n = block.shape

    def body(x_ref, out_ref, send_sems, recv_sems, local_sem):
        x, y, c, chips = _place()
        me, sibling = (x, y, c), (x, y, 1 - c)

        def rows(px, py, pc):
            return out_ref.at[pl.ds((4 * px + 2 * py + pc) * m_per, m_per), :]

        def copy(k, blk, to, src=None):
            return _remote(rows(*blk) if src is None else src, rows(*blk), send_sems, recv_sems, k, to)

        mine = pltpu.make_async_copy(x_ref, rows(*me), local_sem)
        mine.start()
        first = [copy(0, me, sibling, src=x_ref)]
        first += [copy(1 + j, me, (*chip, c), src=x_ref) for j, chip in enumerate(chips)]
        for cp in first:
            cp.start()
        passed = [copy(4 + j, (*chip, c), sibling) for j, chip in enumerate(chips)]
        for j, chip in enumerate(chips):
            copy(1 + j, (*chip, c), me).wait_recv()
            passed[j].start()
        copy(0, sibling, me).wait_recv()
        for j, chip in enumerate(chips):
            copy(4 + j, (*chip, 1 - c), me).wait_recv()
        for cp in first + passed:
            cp.wait_send()
        mine.wait()

    return pl.pallas_call(
        body, name=name, out_shape=jax.ShapeDtypeStruct((N_DEV * m_per, n), block.dtype),
        in_specs=[VMEM_SPEC], out_specs=VMEM_SPEC,
        scratch_shapes=[pltpu.SemaphoreType.DMA((7,)), pltpu.SemaphoreType.DMA((7,)), pltpu.SemaphoreType.DMA],
    )(block)


class _Geo:
    def __init__(self, kind, shard_shape):
        self.kind = kind
        self.shard_shape = tuple(shard_shape)
        self.hr, self.hc = shard_shape[0] // 2, shard_shape[1]
        if kind == "col":
            self.full_shape = (shard_shape[0], N_CHIPS * shard_shape[1])
        else:
            self.full_shape = (N_CHIPS * shard_shape[0], shard_shape[1])

    def full_shard(self, ref, j):
        if self.kind == "col":
            return ref.at[:, pl.ds(pl.multiple_of(j * self.hc, 128), self.hc)]
        return ref.at[pl.ds(pl.multiple_of(j * 2 * self.hr, 16), 2 * self.hr), :]

    def full_half(self, ref, j, c):
        if self.kind == "col":
            return ref.at[pl.ds(pl.multiple_of(c * self.hr, 16), self.hr),
                          pl.ds(pl.multiple_of(j * self.hc, 128), self.hc)]
        return ref.at[pl.ds(pl.multiple_of((2 * j + c) * self.hr, 16), self.hr), :]

    def piece(self, ref, j, c, p0, p1, pieces):
        pr = self.hr // pieces
        off, n = p0 * pr, (p1 - p0) * pr
        if self.kind == "col":
            return ref.at[pl.ds(pl.multiple_of(c * self.hr + off, 16), n),
                          pl.ds(pl.multiple_of(j * self.hc, 128), self.hc)]
        return ref.at[pl.ds(pl.multiple_of((2 * j + c) * self.hr + off, 16), n), :]


WEIGHT_KINDS = ("col", "row", "col", "row")
NW = len(WEIGHT_KINDS)


def _comm_call(body, name, ins, outs, n_remote, aliases=None):
    return pl.pallas_call(
        body, name=name, out_shape=tuple(outs),
        in_specs=[HBM_SPEC] * len(ins), out_specs=tuple([HBM_SPEC] * len(outs)),
        input_output_aliases=aliases or {},
        scratch_shapes=[pltpu.SemaphoreType.DMA((n_remote,)), pltpu.SemaphoreType.DMA((n_remote,))],
    )(*ins)


ROW_TILES = (256, 176, 128, 64, 32, 16)


def _cast_into_full(shard, geo, place, name):
    rs, cs = shard.shape
    tr = _pick(rs, ROW_TILES)
    nb = rs // tr

    def body(place_ref, s_ref, o_ref):
        o_ref[...] = s_ref[...].astype(BF16)

    if geo.kind == "col":
        out_map = lambda i, place_ref: (i, place_ref[0])
    else:
        out_map = lambda i, place_ref: (place_ref[0] * nb + i, 0)
    return pl.pallas_call(
        body, name=name, out_shape=jax.ShapeDtypeStruct(geo.full_shape, BF16),
        grid_spec=pltpu.PrefetchScalarGridSpec(
            num_scalar_prefetch=1, grid=(nb,),
            in_specs=[pl.BlockSpec((tr, cs), lambda i, place_ref: (i, 0))],
            out_specs=pl.BlockSpec((tr, cs), out_map)),
        compiler_params=_params(1),
    )(place, shard)


def _gather_weights(fulls, geos, name):
    nw = len(fulls)

    def body(*refs):
        full = refs[nw:2 * nw]
        send_sems, recv_sems = refs[2 * nw:]
        x, y, c, chips = _place()
        j = 2 * x + y
        sibling = (x, y, 1 - c)
        first = []
        for w in range(nw):
            mine = geos[w].full_half(full[w], j, c)
            for k, chip in enumerate(chips):
                first.append(_remote(mine, mine, send_sems, recv_sems, 6 * w + k, (*chip, c)))
        for cp in first:
            cp.start()
        passed = []
        for w in range(nw):
            for k, (ox, oy) in enumerate(chips):
                landed = geos[w].full_half(full[w], 2 * ox + oy, c)
                _remote(landed, landed, send_sems, recv_sems, 6 * w + k, (ox, oy, c)).wait_recv()
                fwd = _remote(landed, landed, send_sems, recv_sems, 6 * w + 3 + k, sibling)
                fwd.start()
                passed.append(fwd)
        for w in range(nw):
            for k, (ox, oy) in enumerate(chips):
                other = geos[w].full_half(full[w], 2 * ox + oy, 1 - c)
                _remote(other, other, send_sems, recv_sems, 6 * w + 3 + k, sibling).wait_recv()
        for cp in first + passed:
            cp.wait_send()

    outs = [jax.ShapeDtypeStruct(g.full_shape, BF16) for g in geos]
    return _comm_call(body, name, fulls, outs, 6 * nw, aliases={w: w for w in range(nw)})


def _same(a):
    return jax.ShapeDtypeStruct(a.shape, a.dtype)


def _gather_ici(full, geo, p0, p1, pieces):
    def copies(ins, outs):
        x, y, c, chips = _place()
        mine = geo.piece(outs[0], 2 * x + y, c, p0, p1, pieces)
        return [(mine, mine, geo.piece(outs[0], 2 * ox + oy, c, p0, p1, pieces), (ox, oy, c)) for ox, oy in chips]

    return dict(ins=[full], outs=[_same(full)], aliases={0: 0}, n=3, copies=copies)


def _gather_d2d(full, geo):
    def copies(ins, outs):
        x, y, c, chips = _place()
        return [(geo.full_half(outs[0], 2 * ox + oy, c), geo.full_half(outs[0], 2 * ox + oy, c),
                 geo.full_half(outs[0], 2 * ox + oy, 1 - c), (x, y, 1 - c)) for ox, oy in chips]

    return dict(ins=[full], outs=[_same(full)], aliases={0: 0}, n=3, copies=copies)


def _swap_d2d(grad, geo):
    def copies(ins, outs):
        x, y, c, _ = _place()
        return [(geo.full_half(ins[0], j, 1 - c), outs[0].at[j], outs[0].at[j], (x, y, 1 - c)) for j in range(N_CHIPS)]

    return dict(ins=[grad], outs=[jax.ShapeDtypeStruct((N_CHIPS, geo.hr, geo.hc), BF16)], aliases={}, n=N_CHIPS,
                copies=copies)


def _scatter_ici(pair, recv, geo, p0, p1, pieces):
    pr = geo.hr // pieces
    rows = pl.ds(p0 * pr, (p1 - p0) * pr)

    def copies(ins, outs):
        x, y, c, chips = _place()
        return [(ins[0].at[2 * ox + oy, rows, :], outs[0].at[k, rows, :], outs[0].at[k, rows, :], (ox, oy, c))
                for k, (ox, oy) in enumerate(chips)]

    out = jax.ShapeDtypeStruct((3, geo.hr, geo.hc), BF16)
    if recv is None:
        return dict(ins=[pair], outs=[out], aliases={}, n=3, copies=copies)
    return dict(ins=[pair, recv], outs=[out], aliases={1: 0}, n=3, copies=copies)


def _join_d2d(shard, geo):
    def copies(ins, outs):
        x, y, c, _ = _place()
        mine = outs[0].at[pl.ds(pl.multiple_of(c * geo.hr, 8), geo.hr), :]
        other = outs[0].at[pl.ds(pl.multiple_of((1 - c) * geo.hr, 8), geo.hr), :]
        return [(mine, mine, other, (x, y, 1 - c))]

    return dict(ins=[shard], outs=[_same(shard)], aliases={0: 0}, n=1, copies=copies)


def _add_pair(grad, got, geo, place, name):
    tr = _pick(geo.hr, ROW_TILES)
    nb = geo.hr // tr

    def body(place_ref, a_ref, b_ref, o_ref):
        o_ref[0] = (a_ref[...].astype(F32) + b_ref[0].astype(F32)).astype(BF16)

    if geo.kind == "col":
        own_map = lambda j, i, place_ref: (place_ref[1] * nb + i, j)
    else:
        own_map = lambda j, i, place_ref: ((2 * j + place_ref[1]) * nb + i, 0)
    spec = pl.BlockSpec((1, tr, geo.hc), lambda j, i, place_ref: (j, i, 0))
    return pl.pallas_call(
        body, name=name, out_shape=jax.ShapeDtypeStruct((N_CHIPS, geo.hr, geo.hc), BF16),
        grid_spec=pltpu.PrefetchScalarGridSpec(
            num_scalar_prefetch=1, grid=(N_CHIPS, nb),
            in_specs=[pl.BlockSpec((tr, geo.hc), own_map), spec], out_specs=spec),
        compiler_params=_params(2),
    )(place, grad, got)


def _add_four(pair, recv, geo, place, name):
    tr = _pick(geo.hr, ROW_TILES)
    nb = geo.hr // tr

    def body(place_ref, p_ref, r_ref, o_ref):
        o_ref[...] = ((p_ref[0].astype(F32) + r_ref[0].astype(F32)) + r_ref[1].astype(F32)) + r_ref[2].astype(F32)

    return pl.pallas_call(
        body, name=name, out_shape=jax.ShapeDtypeStruct((2 * geo.hr, geo.hc), F32),
        grid_spec=pltpu.PrefetchScalarGridSpec(
            num_scalar_prefetch=1, grid=(nb,),
            in_specs=[pl.BlockSpec((1, tr, geo.hc), lambda i, place_ref: (place_ref[0], i, 0)),
                      pl.BlockSpec((3, tr, geo.hc), lambda i, place_ref: (0, i, 0))],
            out_specs=pl.BlockSpec((tr, geo.hc), lambda i, place_ref: (place_ref[1] * nb + i, 0))),
        compiler_params=_params(1),
    )(place, pair, recv)


PIECES = (4, 1, 8, 4)
SCHEDULE = {
    "proj_fwd": (("gather_ici", W_OUT, 0, 1), ("gather_ici", W_UP, 0, 1)),
    "hgrn_fwd": (("gather_ici", W_UP, 1, 5), ("gather_d2d", W_OUT)),
    "attn_fwd_d1": (("gather_ici", W_UP, 5, 6),),
    "attn_fwd_d4": (("gather_ici", W_UP, 6, 7),),
    "attn_fwd_d16": (("gather_ici", W_UP, 7, 8),),
    "mix_fwd": (("gather_d2d", W_UP),),
    "up_fwd": (("gather_ici", W_DOWN, 0, 4),),
    "conv_gate_fwd": (("gather_d2d", W_DOWN),),
    "down_bwd_x": (("swap", W_DOWN),),
    "conv_gate_bwd": (("scatter", W_DOWN, 0, 2),),
    "up_bwd_w": (("scatter", W_DOWN, 2, 4),),
    "up_bwd_x": (("swap", W_UP),),
    "norm2_bwd": (("scatter", W_UP, 0, 1), ("join", W_DOWN)),
    "mix_bwd_x": (("swap", W_OUT), ("scatter", W_UP, 1, 2)),
    "hgrn_bwd": (("scatter", W_UP, 2, 7),),
    "attn_bwd_d1": (("scatter", W_UP, 7, 8),),
    "attn_bwd_d4": (("scatter", W_OUT, 0, 1),),
    "attn_bwd_d16": (("join", W_UP), ("join", W_OUT)),
    "proj_bwd_x": (("swap", W_IN),),
    "norm1_bwd": (("scatter", W_IN, 0, 1),),
    "grad_in_scatter_rest": (("scatter", W_IN, 1, 4),),
    "grad_in_join": (("join", W_IN),),
}


class _Net:
    def __init__(self, shards, place):
        self.place = place
        self.geos = [_Geo(k, s.shape) for k, s in zip(WEIGHT_KINDS, shards)]
        self.full = [_cast_into_full(s, g, place, f"cast_shard_{w}") for w, (s, g) in enumerate(zip(shards, self.geos))]
        self.full[W_IN] = _gather_weights([self.full[W_IN]], [self.geos[W_IN]], "gather_w_in")[0]
        self.grad, self.pair, self.recv, self.shard = [None] * NW, [None] * NW, [None] * NW, [None] * NW
        self.slots = []

    def host(self, name):
        phase = _Phase()
        self.slots = []
        for item in SCHEDULE.get(name, ()):
            kind, w = item[0], item[1]
            geo = self.geos[w]
            if kind == "gather_ici":
                spec = _gather_ici(self.full[w], geo, item[2], item[3], PIECES[w])
            elif kind == "gather_d2d":
                spec = _gather_d2d(self.full[w], geo)
            elif kind == "swap":
                spec = _swap_d2d(self.grad[w], geo)
            elif kind == "scatter":
                spec = _scatter_ici(self.pair[w], self.recv[w], geo, item[2], item[3], PIECES[w])
            else:
                spec = _join_d2d(self.shard[w], geo)
            self.slots.append((item, phase.add(**spec)))
        return phase

    def done(self, name, comm):
        for item, sl in self.slots:
            kind, w = item[0], item[1]
            out = comm[sl][0]
            if kind in ("gather_ici", "gather_d2d"):
                self.full[w] = out
            elif kind == "swap":
                self.pair[w] = _add_pair(self.grad[w], out, self.geos[w], self.place, f"grad_pair_sum_{w}")
            elif kind == "scatter":
                self.recv[w] = out
                if item[3] == PIECES[w]:
                    self.shard[w] = _add_four(self.pair[w], out, self.geos[w], self.place, f"grad_chip_sum_{w}")
            else:
                self.shard[w] = out

    def alone(self, name):
        self.done(name, _comm_only(name, self.host(name)))


CONVW_SHARD = 3 * DFF // N_CHIPS
COND_PAD = 8 * 896
SMALL_SIZES = (("norm1_w", D), ("lb", HW), ("hg_norm_w", DH), ("q_norm_w", DH), ("k_norm_w", DH),
               ("norm2_w", D), ("conv_b", DFF), ("conv_w", 3 * DFF), ("loss", 128))
SMALL_TOTAL = sum(n for _, n in SMALL_SIZES)
STATS_PAD = 8 * 5120


def kernel(x, c, w_ada, b_ada, norm1_w, w_in, lb_logits, hg_norm_w, q_norm_w, k_norm_w, w_out, norm2_w, w_up, conv_w, conv_b, w_down, loss_target, m_w_ada, m_b_ada, m_norm1_w, m_w_in, m_lb_logits, m_hg_norm_w, m_q_norm_w, m_k_norm_w, m_w_out, m_norm2_w, m_w_up, m_conv_w, m_conv_b, m_w_down, v_w_ada, v_b_ada, v_norm1_w, v_w_in, v_lb_logits, v_hg_norm_w, v_q_norm_w, v_k_norm_w, v_w_out, v_norm2_w, v_w_up, v_conv_w, v_conv_b, v_w_down):
    chip = 2 * lax.axis_index("x") + lax.axis_index("y")
    dev = 2 * chip + lax.axis_index("c")

    cond = jnp.concatenate([c, conv_w[0].reshape(1, CONVW_SHARD), jnp.zeros((1, COND_PAD - D - CONVW_SHARD), F32)], axis=1)
    cond_all = _all_gather_rows(cond.reshape(8, COND_PAD // 8), "gather_cond").reshape(N_DEV, COND_PAD)
    c_all = cond_all[:, :D]
    conv_w_full = jnp.concatenate(
        [cond_all[2 * j, D:D + CONVW_SHARD].reshape(3, DFF // N_CHIPS) for j in range(N_CHIPS)], axis=1)

    b_shard = lax.dynamic_slice_in_dim(b_ada, chip * ADA_COLS, ADA_COLS, axis=1)
    mod_cols = _all_gather_rows(_ada_fwd(c_all, w_ada[0], b_shard), "gather_mod")
    mod_all = jnp.concatenate([mod_cols[16 * j:16 * j + 8] for j in range(N_CHIPS)], axis=1)
    mod = lax.dynamic_slice_in_dim(mod_all, dev, 1, axis=0)

    place = jnp.stack([chip, lax.axis_index("c")]).astype(jnp.int32)
    net = _Net([w_in[0], w_out[0], w_up[0], w_down[0]], place)
    loss_row, grad_x, dmod, small = _sequence_step(
        x[0], loss_target[0], mod, norm1_w, lb_logits, hg_norm_w, q_norm_w, k_norm_w, norm2_w, conv_w_full, conv_b, net)
    net.alone("grad_in_scatter_rest")
    net.alone("grad_in_join")
    g_in, g_out, g_up, g_down = net.shard

    small["conv_w"] = small["conv_w"].reshape(1, 3 * DFF)
    small["loss"] = loss_row
    stats = jnp.concatenate([dmod] + [small[k] for k, _ in SMALL_SIZES]
                            + [jnp.zeros((1, STATS_PAD - 6 * D - SMALL_TOTAL), F32)], axis=1)
    stats_all = _all_gather_rows(stats.reshape(8, STATS_PAD // 8), "gather_stats").reshape(N_DEV, STATS_PAD)
    dmod_all = stats_all[:, :6 * D]
    sums = _sum_rows(stats_all[:, 6 * D:6 * D + SMALL_TOTAL], "small_grad_sum")
    g_small, off = {}, 0
    for k, n in SMALL_SIZES:
        g_small[k] = sums[:, off:off + n]
        off += n
    loss = g_small["loss"][0, 0]
    g_b_ada = _sum_rows(dmod_all, "b_ada_grad_sum")
    g_w_ada = _ada_bwd(c_all, lax.dynamic_slice_in_dim(dmod_all, chip * ADA_COLS, ADA_COLS, axis=1))
    g_lb = _lb_grad(lb_logits, g_small["lb"])
    g_conv_w = lax.dynamic_slice_in_dim(g_small["conv_w"].reshape(3, DFF), chip * (DFF // N_CHIPS), DFF // N_CHIPS, axis=1)

    names = ["w_ada", "b_ada", "norm1_w", "w_in", "lb_logits", "hg_norm_w", "q_norm_w", "k_norm_w", "w_out",
             "norm2_w", "w_up", "conv_w", "conv_b", "w_down"]
    lead = {"w_ada", "w_in", "w_out", "w_up", "conv_w", "w_down"}
    w = dict(w_ada=w_ada, b_ada=b_ada, norm1_w=norm1_w, w_in=w_in, lb_logits=lb_logits, hg_norm_w=hg_norm_w,
             q_norm_w=q_norm_w, k_norm_w=k_norm_w, w_out=w_out, norm2_w=norm2_w, w_up=w_up, conv_w=conv_w,
             conv_b=conv_b, w_down=w_down)
    m = dict(w_ada=m_w_ada, b_ada=m_b_ada, norm1_w=m_norm1_w, w_in=m_w_in, lb_logits=m_lb_logits,
             hg_norm_w=m_hg_norm_w, q_norm_w=m_q_norm_w, k_norm_w=m_k_norm_w, w_out=m_w_out, norm2_w=m_norm2_w,
             w_up=m_w_up, conv_w=m_conv_w, conv_b=m_conv_b, w_down=m_w_down)
    v = dict(w_ada=v_w_ada, b_ada=v_b_ada, norm1_w=v_norm1_w, w_in=v_w_in, lb_logits=v_lb_logits,
             hg_norm_w=v_hg_norm_w, q_norm_w=v_q_norm_w, k_norm_w=v_k_norm_w, w_out=v_w_out, norm2_w=v_norm2_w,
             w_up=v_w_up, conv_w=v_conv_w, conv_b=v_conv_b, w_down=v_w_down)
    g = dict(w_ada=g_w_ada, b_ada=g_b_ada, norm1_w=g_small["norm1_w"], w_in=g_in, lb_logits=g_lb,
             hg_norm_w=g_small["hg_norm_w"], q_norm_w=g_small["q_norm_w"], k_norm_w=g_small["k_norm_w"], w_out=g_out,
             norm2_w=g_small["norm2_w"], w_up=g_up, conv_w=g_conv_w, conv_b=g_small["conv_b"], w_down=g_down)

    grads, deltas, new_m, new_v = [], [], [], []
    for n in names:
        strip = (lambda t: t[0]) if n in lead else (lambda t: t)
        wrap = (lambda t: t[None]) if n in lead else (lambda t: t)
        d_n, m_n, v_n = _adamw(strip(w[n]), g[n], strip(m[n]), strip(v[n]), f"adamw_{n}")
        grads.append(wrap(g[n]))
        deltas.append(wrap(d_n))
        new_m.append(wrap(m_n))
        new_v.append(wrap(v_n))
    return (loss, grad_x[None], *grads, *deltas, *new_m, *new_v)
```

```python
import functools
import math

import jax
import jax.numpy as jnp
from jax import lax
from jax.experimental import pallas as pl
from jax.experimental.pallas import tpu as pltpu

F32 = jnp.float32
BF16 = jnp.bfloat16

S = 2048
D = 2048
H = 8
DH = 128
HW = H * DH
CH = 64
NCH = S // CH
DFF = 5632
INC = 7 * HW
BLK = 128
DILS = (1, 4, 16)
EPS = 1e-6
NEG = -1e30
N_CHIPS = 4
N_DEV = 8

ADAM_LR = 0.001
ADAM_B1 = 0.9
ADAM_B2 = 0.999
ADAM_EPS = 1e-08
ADAM_WD = 0.01
ADAM_STEP = 10
ADAM_BLOCK_BYTES = 1 << 20

V7X_VMEM_BYTES = 64 * 1024 * 1024
VMEM_LIMIT = (V7X_VMEM_BYTES * 3) // 4
MESH = pl.DeviceIdType.MESH
HBM_SPEC = pl.BlockSpec(memory_space=pltpu.HBM)
VMEM_SPEC = pl.BlockSpec(memory_space=pltpu.VMEM)

NN = (((1,), (0,)), ((), ()))
NT = (((1,), (1,)), ((), ()))
TN = (((0,), (0,)), ((), ()))


def _params(n_grid):
    return pltpu.CompilerParams(dimension_semantics=("arbitrary",) * n_grid, vmem_limit_bytes=VMEM_LIMIT)


def _dot(a, b, dims=NN):
    return lax.dot_general(a.astype(BF16), b.astype(BF16), dims, preferred_element_type=F32)


def _dot_f32(a, b):
    return lax.dot_general(a, b, NN, precision=lax.Precision.HIGHEST, preferred_element_type=F32)


def _sigmoid(x):
    return 1.0 / (1.0 + jnp.exp(-x))


def _pick(n, cands):
    for t in cands:
        if n % t == 0:
            return t
    raise ValueError(n)


def _matmul(a, b, mode, out_dtype, name, phase=None, m_blocks=None):
    if mode == "nn":
        (m, k), (_, n) = a.shape, b.shape
    elif mode == "nt":
        (m, k), (n, _) = a.shape, b.shape
    else:
        (k, m), (_, n) = a.shape, b.shape
    tm = _pick(m, (1024, 512))
    a_block = lambda i: i
    if m_blocks is not None:
        tm, blocks = m_blocks
        m = tm * len(blocks)
        step = blocks[1] - blocks[0] if len(blocks) > 1 else 0
        assert all(blk == blocks[0] + step * i for i, blk in enumerate(blocks))
        a_block = lambda i: blocks[0] + step * i
    tn = _pick(n, (1024, 512))
    tk = _pick(k, (2048, 2816, 1792, 1024, 512))
    nk = k // tk
    dims = {"nn": NN, "nt": NT, "tn": TN}[mode]

    def body(a_ref, b_ref, o_ref, *acc):
        part = lax.dot_general(a_ref[...], b_ref[...], dims, preferred_element_type=F32)
        if nk == 1:
            o_ref[...] = part.astype(o_ref.dtype)
            return
        acc_ref, kk = acc[0], pl.program_id(2)

        @pl.when(kk == 0)
        def _():
            acc_ref[...] = part

        @pl.when(jnp.logical_and(kk > 0, kk < nk - 1))
        def _():
            acc_ref[...] += part

        @pl.when(kk == nk - 1)
        def _():
            o_ref[...] = (acc_ref[...] + part).astype(o_ref.dtype)

    if mode == "tn":
        a_spec = pl.BlockSpec((tk, tm), lambda i, j, kk: (kk, a_block(i)))
    else:
        a_spec = pl.BlockSpec((tm, tk), lambda i, j, kk: (i, kk))
    if mode == "nt":
        b_spec = pl.BlockSpec((tn, tk), lambda i, j, kk: (j, kk))
    else:
        b_spec = pl.BlockSpec((tk, tn), lambda i, j, kk: (kk, j))
    return _pcall(
        body, [a, b], phase, name=name,
        out_shape=jax.ShapeDtypeStruct((m, n), out_dtype),
        grid=(m // tm, n // tn, nk),
        in_specs=[a_spec, b_spec],
        out_specs=pl.BlockSpec((tm, tn), lambda i, j, kk: (i, j)),
        scratch_shapes=[pltpu.VMEM((tm, tn), F32)] if nk > 1 else [],
        compiler_params=_params(3),
    )


TR = 256


def _row_spec(cols=D):
    return pl.BlockSpec((TR, cols), lambda i: (i, 0))


def _vec_spec(cols=D):
    return pl.BlockSpec((1, cols), lambda i: (0, 0))


def _norm_mod(x, nw, scale, shift, name):
    def body(x_ref, nw_ref, sc_ref, sh_ref, h_ref):
        xv = x_ref[...]
        r = lax.rsqrt(jnp.mean(xv * xv, axis=-1, keepdims=True) + EPS)
        h_ref[...] = ((xv * r) * nw_ref[...] * (1.0 + sc_ref[...]) + sh_ref[...]).astype(BF16)

    return pl.pallas_call(
        body, name=name, out_shape=jax.ShapeDtypeStruct((S, D), BF16), grid=(S // TR,),
        in_specs=[_row_spec(), _vec_spec(), _vec_spec(), _vec_spec()], out_specs=_row_spec(),
        compiler_params=_params(1),
    )(x, nw, scale, shift)


def _resid_norm_mod(x, mix, gate, nw, scale, shift, name):
    def body(x_ref, mix_ref, g_ref, nw_ref, sc_ref, sh_ref, x1_ref, h_ref):
        xv = x_ref[...] + g_ref[...] * mix_ref[...]
        x1_ref[...] = xv
        r = lax.rsqrt(jnp.mean(xv * xv, axis=-1, keepdims=True) + EPS)
        h_ref[...] = ((xv * r) * nw_ref[...] * (1.0 + sc_ref[...]) + sh_ref[...]).astype(BF16)

    return pl.pallas_call(
        body, name=name,
        out_shape=(jax.ShapeDtypeStruct((S, D), F32), jax.ShapeDtypeStruct((S, D), BF16)), grid=(S // TR,),
        in_specs=[_row_spec(), _row_spec(), _vec_spec(), _vec_spec(), _vec_spec(), _vec_spec()],
        out_specs=(_row_spec(), _row_spec()),
        compiler_params=_params(1),
    )(x, mix, gate, nw, scale, shift)


def _norm_mod_bwd(dh, xin, nw, scale, dres, name, mix=None, gate=None, phase=None):
    with_gate = mix is not None

    def body(*refs):
        if with_gate:
            dh_ref, x_ref, nw_ref, sc_ref, dres_ref, mix_ref, g_ref, dx_ref, dsh_ref, dsc_ref, dnw_ref, dg_ref, dmix_ref = refs
        else:
            dh_ref, x_ref, nw_ref, sc_ref, dres_ref, dx_ref, dsh_ref, dsc_ref, dnw_ref = refs
        i = pl.program_id(0)
        dhv = dh_ref[...]
        xv = x_ref[...]
        r = lax.rsqrt(jnp.mean(xv * xv, axis=-1, keepdims=True) + EPS)
        xn = xv * r
        nwv = nw_ref[...]
        one_sc = 1.0 + sc_ref[...]
        dxn = dhv * nwv * one_sc
        dx = dres_ref[...] + r * (dxn - xn * jnp.mean(dxn * xn, axis=-1, keepdims=True))
        dx_ref[...] = dx

        @pl.when(i == 0)
        def _():
            dsh_ref[...] = jnp.zeros_like(dsh_ref)
            dsc_ref[...] = jnp.zeros_like(dsc_ref)
            dnw_ref[...] = jnp.zeros_like(dnw_ref)
            if with_gate:
                dg_ref[...] = jnp.zeros_like(dg_ref)

        dsh_ref[...] += jnp.sum(dhv, axis=0, keepdims=True)
        dsc_ref[...] += jnp.sum(dhv * xn * nwv, axis=0, keepdims=True)
        dnw_ref[...] += jnp.sum(dhv * xn * one_sc, axis=0, keepdims=True)
        if with_gate:
            dg_ref[...] += jnp.sum(dx * mix_ref[...], axis=0, keepdims=True)
            dmix_ref[...] = (dx * g_ref[...]).astype(BF16)

    vec = jax.ShapeDtypeStruct((1, D), F32)
    ins = [dh, xin, nw, scale, dres]
    in_specs = [_row_spec(), _row_spec(), _vec_spec(), _vec_spec(), _row_spec()]
    outs = [jax.ShapeDtypeStruct((S, D), F32), vec, vec, vec]
    out_specs = [_row_spec(), _vec_spec(), _vec_spec(), _vec_spec()]
    if with_gate:
        ins += [mix, gate]
        in_specs += [_row_spec(), _vec_spec()]
        outs += [vec, jax.ShapeDtypeStruct((S, D), BF16)]
        out_specs += [_vec_spec(), _row_spec()]
    return _pcall(
        body, ins, phase, name=name, out_shape=tuple(outs), grid=(S // TR,), in_specs=in_specs,
        out_specs=tuple(out_specs), compiler_params=_params(1),
    )


def _tri(lower):
    row = lax.broadcasted_iota(jnp.int32, (CH, CH), 0)
    col = lax.broadcasted_iota(jnp.int32, (CH, CH), 1)
    return (row >= col) if lower else (col >= row)


def _hgrn_gates(hq, hf, lb):
    sig = _sigmoid(hf)
    f = lb + (1.0 - lb) * sig
    g = jnp.log(f)
    sq = _sigmoid(hq)
    return sig, f, g, 1.0 - f, hq * sq, sq


HG_HP = 2


def _proj_col_spec(group):
    return pl.BlockSpec((S, HG_HP * DH), lambda h: (0, group * (H // HG_HP) + h))


def _hgrn_fwd(proj, lb_logits, norm_w, phase=None):
    def body(hq_ref, hf_ref, hi_ref, hg_ref, lbl_ref, nw_ref, out_ref, oraw_ref, st_ref, s_ref):
        nw = nw_ref[...]
        lower = _tri(True)
        ltri = lower.astype(F32)
        s_ref[...] = jnp.zeros_like(s_ref)

        def chunk(n, carry):
            rows = pl.ds(pl.multiple_of(n * CH, CH), CH)
            for j in range(HG_HP):
                cols = slice(j * DH, (j + 1) * DH)
                lb = 1.0 / (1.0 + jnp.exp(lbl_ref[1:2, cols] - lbl_ref[0:1, cols]))
                hq, hf, v, hg = hq_ref[rows, cols], hf_ref[rows, cols], hi_ref[rows, cols], hg_ref[rows, cols]
                _, _, g, kk, q, _ = _hgrn_gates(hq, hf, lb)
                gc = _dot_f32(ltri, g)
                gl = jnp.sum(g, axis=0, keepdims=True)
                qe = q * jnp.exp(gc)
                ke = kk * jnp.exp(gl - gc)
                qh = q * jnp.exp(gc - 0.5 * gl)
                kh = kk * jnp.exp(0.5 * gl - gc)
                st = s_ref[j]
                st_ref[j, pl.ds(n, 1)] = st[None]
                a = jnp.where(lower, _dot(qh, kh, NT), 0.0)
                o = _dot(qe, st, NT) + _dot(a, v)
                s_ref[j] = st * jnp.exp(gl) + _dot(v, ke, TN)
                oraw_ref[rows, cols] = o
                rs = lax.rsqrt(jnp.mean(o * o, axis=-1, keepdims=True) + EPS)
                out_ref[rows, cols] = (o * rs * nw * (hg * _sigmoid(hg))).astype(BF16)
            return carry

        lax.fori_loop(0, NCH, chunk, 0)

    head_spec = pl.BlockSpec((S, HG_HP * DH), lambda h: (0, h))
    return _pcall(
        body, [proj, proj, proj, proj, lb_logits, norm_w], phase, name="hgrn_fwd",
        out_shape=(jax.ShapeDtypeStruct((S, 2 * HW), BF16), jax.ShapeDtypeStruct((S, HW), F32),
                   jax.ShapeDtypeStruct((H, NCH, DH, DH), F32)),
        grid=(H // HG_HP,),
        in_specs=[_proj_col_spec(0), _proj_col_spec(1), _proj_col_spec(2), _proj_col_spec(3),
                  pl.BlockSpec((2, HG_HP * DH), lambda h: (0, h)), pl.BlockSpec((1, DH), lambda h: (0, 0))],
        out_specs=(head_spec, head_spec, pl.BlockSpec((HG_HP, NCH, DH, DH), lambda h: (h, 0, 0, 0))),
        scratch_shapes=[pltpu.VMEM((HG_HP, DH, DH), F32)],
        compiler_params=_params(1),
    )


def _hgrn_bwd(proj, lb_logits, norm_w, oraw, states, dout, phase=None):
    def body(hq_ref, hf_ref, hi_ref, hg_ref, lbl_ref, nw_ref, oraw_ref, st_ref, do_ref,
             dhq_ref, dhf_ref, dhi_ref, dhg_ref, dlb_ref, dnw_ref, ds_ref, acc_ref):
        h = pl.program_id(0)
        nw = nw_ref[...]
        lower = _tri(True)
        ltri = lower.astype(F32)
        utri = _tri(False).astype(F32)
        last = lax.broadcasted_iota(jnp.int32, (CH, DH), 0) == CH - 1
        ds_ref[...] = jnp.zeros_like(ds_ref)
        acc_ref[...] = jnp.zeros_like(acc_ref)

        @pl.when(h == 0)
        def _():
            dnw_ref[...] = jnp.zeros_like(dnw_ref)

        def chunk(i, carry):
            n = NCH - 1 - i
            rows = pl.ds(pl.multiple_of(n * CH, CH), CH)
            for j in range(HG_HP):
                cols = slice(j * DH, (j + 1) * DH)
                lb = 1.0 / (1.0 + jnp.exp(lbl_ref[1:2, cols] - lbl_ref[0:1, cols]))
                hq, hf, v, hg = hq_ref[rows, cols], hf_ref[rows, cols], hi_ref[rows, cols], hg_ref[rows, cols]
                sig, f, g, kk, q, sq = _hgrn_gates(hq, hf, lb)
                gc = _dot_f32(ltri, g)
                gl = jnp.sum(g, axis=0, keepdims=True)
                eg = jnp.exp(gc)
                ek = jnp.exp(gl - gc)
                gam = jnp.exp(gl)
                ph = jnp.exp(gc - 0.5 * gl)
                pk = jnp.exp(0.5 * gl - gc)
                qe, ke = q * eg, kk * ek
                qh, kh = q * ph, kk * pk
                st = st_ref[j, pl.ds(n, 1)][0]
                dst = ds_ref[j]
                a = jnp.where(lower, _dot(qh, kh, NT), 0.0)
                o = oraw_ref[rows, cols]
                rs = lax.rsqrt(jnp.mean(o * o, axis=-1, keepdims=True) + EPS)
                xh = o * rs
                sg = _sigmoid(hg)
                dgo = do_ref[rows, cols]
                d_on = dgo * (hg * sg)
                dhg_ref[rows, cols] = (dgo * xh * nw * (sg * (1.0 + hg * (1.0 - sg)))).astype(BF16)
                acc_ref[j, 1:2, :] += jnp.sum(d_on * xh, axis=0, keepdims=True)
                dy = d_on * nw
                do = rs * (dy - xh * jnp.mean(dy * xh, axis=-1, keepdims=True))
                dqe = _dot(do, st)
                da = jnp.where(lower, _dot(do, v, NT), 0.0)
                dv = _dot(a, do, TN) + _dot(ke, dst, NT)
                dke = _dot(v, dst)
                dgam = jnp.sum(dst * st, axis=0, keepdims=True)
                dqh = _dot(da, kh)
                dkh = _dot(da, qh, TN)
                dq = dqh * ph + dqe * eg
                dk = dkh * pk + dke * ek
                dgl = jnp.sum(dke * ke, axis=0, keepdims=True) + dgam * gam
                dgc = dqh * qh - dkh * kh + dqe * qe - dke * ke + jnp.where(last, dgl, 0.0)
                dg = _dot_f32(utri, dgc)
                df = dg / f - dk
                dhf_ref[rows, cols] = (df * (1.0 - lb) * sig * (1.0 - sig)).astype(BF16)
                acc_ref[j, 0:1, :] += jnp.sum(df * (1.0 - sig), axis=0, keepdims=True)
                dhq_ref[rows, cols] = (dq * (sq * (1.0 + hq * (1.0 - sq)))).astype(BF16)
                dhi_ref[rows, cols] = dv.astype(BF16)
                ds_ref[j] = dst * gam + _dot(do, qe, TN)
            return carry

        lax.fori_loop(0, NCH, chunk, 0)
        for j in range(HG_HP):
            dlb_ref[:, j * DH:(j + 1) * DH] = acc_ref[j, 0:1, :]
            dnw_ref[...] += acc_ref[j, 1:2, :]

    head_spec = pl.BlockSpec((S, HG_HP * DH), lambda h: (0, h))
    head_out = jax.ShapeDtypeStruct((S, HW), BF16)
    return _pcall(
        body, [proj, proj, proj, proj, lb_logits, norm_w, oraw, states, dout], phase, name="hgrn_bwd",
        out_shape=(head_out, head_out, head_out, head_out,
                   jax.ShapeDtypeStruct((1, HW), F32), jax.ShapeDtypeStruct((1, DH), F32)),
        grid=(H // HG_HP,),
        in_specs=[_proj_col_spec(0), _proj_col_spec(1), _proj_col_spec(2), _proj_col_spec(3),
                  pl.BlockSpec((2, HG_HP * DH), lambda h: (0, h)), pl.BlockSpec((1, DH), lambda h: (0, 0)),
                  head_spec, pl.BlockSpec((HG_HP, NCH, DH, DH), lambda h: (h, 0, 0, 0)), head_spec],
        out_specs=(head_spec, head_spec, head_spec, head_spec,
                   pl.BlockSpec((1, HG_HP * DH), lambda h: (0, h)), pl.BlockSpec((1, DH), lambda h: (0, 0))),
        scratch_shapes=[pltpu.VMEM((HG_HP, DH, DH), F32), pltpu.VMEM((HG_HP, 8, DH), F32)],
        compiler_params=_params(1),
    )


ATT_SCALE = DH ** -0.5
HEADS_PER_STEP = {1: 8, 4: 1, 16: 1}


def _sub_rows(ref, r, dil, cols):
    if dil == 1:
        return ref[:, cols]
    return ref[pl.ds(r, BLK, stride=dil), cols]


def _set_sub_rows(ref, r, dil, cols, val):
    if dil == 1:
        ref[:, cols] = val
    else:
        ref[pl.ds(r, BLK, stride=dil), cols] = val


def _slopes(dil):
    hp = HEADS_PER_STEP[dil]
    s = jnp.asarray([dil * 2.0 ** (-(h + 1)) for h in range(H)], F32)
    return jnp.broadcast_to(s[:, None], (H, DH)).reshape(H // hp, hp, DH)


def _rms_rows(x, w):
    rs = lax.rsqrt(jnp.mean(x * x, axis=-1, keepdims=True) + EPS)
    return x * rs, rs


def _att_masks():
    qi = lax.broadcasted_iota(jnp.int32, (BLK, BLK), 0)
    kj = lax.broadcasted_iota(jnp.int32, (BLK, BLK), 1)
    steps_c = qi - kj
    steps_p = qi - kj + BLK
    return steps_c, steps_p, steps_c >= 0, steps_p <= BLK


def _attn_fwd(proj, q_w, k_w, dil, phase=None):
    hp = HEADS_PER_STEP[dil]
    rows, ng, nb = BLK * dil, H // hp, S // (BLK * dil)

    def body(q_ref, kc_ref, kp_ref, vc_ref, vp_ref, qw_ref, kw_ref, sl_ref, o_ref, lse_ref):
        n = pl.program_id(0)
        steps_c, steps_p, ok_c, ok_p = _att_masks()
        ok_p = jnp.logical_and(ok_p, n > 0)
        fc, fp = steps_c.astype(F32), steps_p.astype(F32)
        qw, kw = qw_ref[...], kw_ref[...]
        for hh in range(hp):
            cols = slice(hh * DH, (hh + 1) * DH)
            sl = sl_ref[0, hh:hh + 1, :]
            for r in range(dil):
                qn = _rms_rows(_sub_rows(q_ref, r, dil, cols), None)[0] * qw
                kcn = _rms_rows(_sub_rows(kc_ref, r, dil, cols), None)[0] * kw
                kpn = _rms_rows(_sub_rows(kp_ref, r, dil, cols), None)[0] * kw
                sc = jnp.where(ok_c, _dot(qn, kcn, NT) * ATT_SCALE - sl * fc, NEG)
                sp = jnp.where(ok_p, _dot(qn, kpn, NT) * ATT_SCALE - sl * fp, NEG)
                m = jnp.maximum(jnp.max(sc, axis=-1, keepdims=True), jnp.max(sp, axis=-1, keepdims=True))
                pc, pp = jnp.exp(sc - m), jnp.exp(sp - m)
                den = jnp.sum(pc, axis=-1, keepdims=True) + jnp.sum(pp, axis=-1, keepdims=True)
                o = (_dot(pc, _sub_rows(vc_ref, r, dil, cols)) + _dot(pp, _sub_rows(vp_ref, r, dil, cols))) / den
                _set_sub_rows(o_ref, r, dil, cols, o)
                _set_sub_rows(lse_ref, r, dil, cols, jnp.broadcast_to(m + jnp.log(den), (BLK, DH)))

    def spec(group, prev):
        if prev:
            return pl.BlockSpec((rows, hp * DH), lambda n, g: (jnp.maximum(n - 1, 0), group * ng + g))
        return pl.BlockSpec((rows, hp * DH), lambda n, g: (n, group * ng + g))

    out = jax.ShapeDtypeStruct((S, HW), F32)
    out_spec = pl.BlockSpec((rows, hp * DH), lambda n, g: (n, g))
    wspec = pl.BlockSpec((1, DH), lambda n, g: (0, 0))
    return _pcall(
        body, [proj, proj, proj, proj, proj, q_w, k_w, _slopes(dil)], phase,
        name=f"attn_fwd_d{dil}", out_shape=(out, out), grid=(nb, ng),
        in_specs=[spec(4, False), spec(5, False), spec(5, True), spec(6, False), spec(6, True), wspec, wspec,
                  pl.BlockSpec((1, hp, DH), lambda n, g: (g, 0, 0))],
        out_specs=(out_spec, out_spec),
        compiler_params=_params(2),
    )


def _attn_merge(outs, lses, cat):
    def body(o1, o2, o3, l1, l2, l3, cat_ref, ob_ref, of_ref, lse_ref):
        a, b, c = l1[...], l2[...], l3[...]
        m = jnp.maximum(jnp.maximum(a, b), c)
        ea, eb, ec = jnp.exp(a - m), jnp.exp(b - m), jnp.exp(c - m)
        den = ea + eb + ec
        o = (ea * o1[...] + eb * o2[...] + ec * o3[...]) / den
        ob_ref[...] = o.astype(BF16)
        of_ref[...] = o
        lse_ref[...] = m + jnp.log(den)

    f = jax.ShapeDtypeStruct((S, HW), F32)
    return pl.pallas_call(
        body, name="attn_merge", out_shape=(jax.ShapeDtypeStruct((S, 2 * HW), BF16), f, f), grid=(S // TR,),
        in_specs=[_row_spec(HW)] * 6 + [pl.BlockSpec(memory_space=pl.ANY)],
        out_specs=(pl.BlockSpec((TR, HW), lambda i: (i, 1)), _row_spec(HW), _row_spec(HW)),
        input_output_aliases={6: 0},
        compiler_params=_params(1),
    )(*outs, *lses, cat)


def _attn_bwd(proj, q_w, k_w, o, lse, do, dil, acc, phase=None):
    hp = HEADS_PER_STEP[dil]
    rows, ng, nb = BLK * dil, H // hp, S // (BLK * dil)
    has_acc = acc is not None

    def body(*refs):
        (q_ref, qn_ref, kp_ref, kc_ref, vp_ref, vc_ref, o_ref, on_ref, l_ref, ln_ref, do_ref, don_ref,
         qw_ref, kw_ref, sl_ref) = refs[:15]
        refs = refs[15:]
        if has_acc:
            aq_ref, ak_ref, av_ref, aqw_ref, akw_ref = refs[:5]
            refs = refs[5:]
        dq_ref, dk_ref, dv_ref, dqw_ref, dkw_ref = refs
        m, g = pl.program_id(0), pl.program_id(1)
        steps_c, steps_p, ok_c, ok_p = _att_masks()
        ok_prev = jnp.logical_and(ok_p, m > 0)
        ok_next = jnp.logical_and(ok_p, m < nb - 1)
        fc, fp = steps_c.astype(F32), steps_p.astype(F32)
        qw, kw = qw_ref[...], kw_ref[...]

        @pl.when(jnp.logical_and(m == 0, g == 0))
        def _():
            if has_acc:
                dqw_ref[...] = aqw_ref[...]
                dkw_ref[...] = akw_ref[...]
            else:
                dqw_ref[...] = jnp.zeros_like(dqw_ref)
                dkw_ref[...] = jnp.zeros_like(dkw_ref)

        for hh in range(hp):
            cols = slice(hh * DH, (hh + 1) * DH)
            sl = sl_ref[0, hh:hh + 1, :]
            for r in range(dil):
                sub = lambda ref: _sub_rows(ref, r, dil, cols)
                qx, qrs = _rms_rows(sub(q_ref), None)
                kx, krs = _rms_rows(sub(kc_ref), None)
                qn, kcn = qx * qw, kx * kw
                qnn = _rms_rows(sub(qn_ref), None)[0] * qw
                kpn = _rms_rows(sub(kp_ref), None)[0] * kw
                vc, vp = sub(vc_ref), sub(vp_ref)
                dov, donv = sub(do_ref), sub(don_ref)
                lse_m, lse_n = sub(l_ref)[:, 0:1], sub(ln_ref)[:, 0:1]
                delta_m = jnp.sum(dov * sub(o_ref), axis=-1, keepdims=True)
                delta_n = jnp.sum(donv * sub(on_ref), axis=-1, keepdims=True)
                p_c = jnp.where(ok_c, jnp.exp(_dot(qn, kcn, NT) * ATT_SCALE - sl * fc - lse_m), 0.0)
                p_p = jnp.where(ok_prev, jnp.exp(_dot(qn, kpn, NT) * ATT_SCALE - sl * fp - lse_m), 0.0)
                p_n = jnp.where(ok_next, jnp.exp(_dot(qnn, kcn, NT) * ATT_SCALE - sl * fp - lse_n), 0.0)
                ds_c = p_c * (_dot(dov, vc, NT) - delta_m)
                ds_p = p_p * (_dot(dov, vp, NT) - delta_m)
                ds_n = p_n * (_dot(donv, vc, NT) - delta_n)
                dqn = (_dot(ds_c, kcn) + _dot(ds_p, kpn)) * ATT_SCALE
                dkn = (_dot(ds_c, qn, TN) + _dot(ds_n, qnn, TN)) * ATT_SCALE
                dv = _dot(p_c, dov, TN) + _dot(p_n, donv, TN)
                dqw_ref[...] += jnp.sum(dqn * qx, axis=0, keepdims=True)
                dkw_ref[...] += jnp.sum(dkn * kx, axis=0, keepdims=True)
                dyq, dyk = dqn * qw, dkn * kw
                daq = qrs * (dyq - qx * jnp.mean(dyq * qx, axis=-1, keepdims=True))
                dak = krs * (dyk - kx * jnp.mean(dyk * kx, axis=-1, keepdims=True))
                if has_acc:
                    daq, dak, dv = daq + sub(aq_ref), dak + sub(ak_ref), dv + sub(av_ref)
                _set_sub_rows(dq_ref, r, dil, cols, daq)
                _set_sub_rows(dk_ref, r, dil, cols, dak)
                _set_sub_rows(dv_ref, r, dil, cols, dv)

    def shifted(shift, m):
        if shift < 0:
            return jnp.maximum(m - 1, 0)
        if shift > 0:
            return jnp.minimum(m + 1, nb - 1)
        return m

    def pspec(group, shift):
        return pl.BlockSpec((rows, hp * DH), lambda m, g: (shifted(shift, m), group * ng + g))

    cur = pl.BlockSpec((rows, hp * DH), lambda m, g: (m, g))
    nxt = pl.BlockSpec((rows, hp * DH), lambda m, g: (shifted(1, m), g))
    wspec = pl.BlockSpec((1, DH), lambda m, g: (0, 0))
    ins = [proj, proj, proj, proj, proj, proj, o, o, lse, lse, do, do, q_w, k_w, _slopes(dil)]
    in_specs = [pspec(4, 0), pspec(4, 1), pspec(5, -1), pspec(5, 0), pspec(6, -1), pspec(6, 0),
                cur, nxt, cur, nxt, pspec(1, 0), pspec(1, 1), wspec, wspec,
                pl.BlockSpec((1, hp, DH), lambda m, g: (g, 0, 0))]
    if has_acc:
        ins += list(acc)
        in_specs += [cur, cur, cur, wspec, wspec]
    big = jax.ShapeDtypeStruct((S, HW), F32)
    small = jax.ShapeDtypeStruct((1, DH), F32)
    return _pcall(
        body, ins, phase, name=f"attn_bwd_d{dil}", out_shape=(big, big, big, small, small), grid=(nb, ng),
        in_specs=in_specs, out_specs=(cur, cur, cur, wspec, wspec),
        compiler_params=_params(2),
    )


TC = 512
NCT = DFF // TC


def _shift_rows(a, k):
    rows = lax.broadcasted_iota(jnp.int32, a.shape, 0)
    rolled = pltpu.roll(a, k % S, 0)
    if k > 0:
        return jnp.where(rows >= k, rolled, 0.0)
    return jnp.where(rows < S + k, rolled, 0.0)


def _conv_pre(a, w_ref, b_ref):
    return b_ref[...] + w_ref[0:1, :] * _shift_rows(a, 2) + w_ref[1:2, :] * _shift_rows(a, 1) + w_ref[2:3, :] * a


def _conv_gate_fwd(u, conv_w, conv_b, phase=None):
    def body(a_ref, g_ref, w_ref, b_ref, y_ref):
        pre = _conv_pre(a_ref[...].astype(F32), w_ref, b_ref)
        y_ref[...] = (pre * _sigmoid(pre) * g_ref[...].astype(F32)).astype(BF16)

    return _pcall(
        body, [u, u, conv_w, conv_b], phase,
        name="conv_gate_fwd", out_shape=jax.ShapeDtypeStruct((S, DFF), BF16), grid=(NCT,),
        in_specs=[pl.BlockSpec((S, TC), lambda i: (0, i)), pl.BlockSpec((S, TC), lambda i: (0, NCT + i)),
                  pl.BlockSpec((3, TC), lambda i: (0, i)), pl.BlockSpec((1, TC), lambda i: (0, i))],
        out_specs=pl.BlockSpec((S, TC), lambda i: (0, i)),
        compiler_params=_params(1),
    )


def _conv_gate_bwd(dy, u, conv_w, conv_b, phase=None):
    def body(dy_ref, a_ref, g_ref, w_ref, b_ref, da_ref, dg_ref, db_ref, dw_ref):
        a = a_ref[...].astype(F32)
        dyv = dy_ref[...].astype(F32)
        pre = _conv_pre(a, w_ref, b_ref)
        sg = _sigmoid(pre)
        dg_ref[...] = (dyv * pre * sg).astype(BF16)
        dpre = dyv * g_ref[...].astype(F32) * (sg * (1.0 + pre * (1.0 - sg)))
        da = w_ref[2:3, :] * dpre + w_ref[1:2, :] * _shift_rows(dpre, -1) + w_ref[0:1, :] * _shift_rows(dpre, -2)
        da_ref[...] = da.astype(BF16)
        db_ref[...] = jnp.sum(dpre, axis=0, keepdims=True)
        dw_ref[0:1, :] = jnp.sum(dpre * _shift_rows(a, 2), axis=0, keepdims=True)
        dw_ref[1:2, :] = jnp.sum(dpre * _shift_rows(a, 1), axis=0, keepdims=True)
        dw_ref[2:3, :] = jnp.sum(dpre * a, axis=0, keepdims=True)

    col = pl.BlockSpec((S, TC), lambda i: (0, i))
    half = jax.ShapeDtypeStruct((S, DFF), BF16)
    return _pcall(
        body, [dy, u, u, conv_w, conv_b], phase, name="conv_gate_bwd",
        out_shape=(half, half, jax.ShapeDtypeStruct((1, DFF), F32), jax.ShapeDtypeStruct((3, DFF), F32)),
        grid=(NCT,),
        in_specs=[col, col, pl.BlockSpec((S, TC), lambda i: (0, NCT + i)),
                  pl.BlockSpec((3, TC), lambda i: (0, i)), pl.BlockSpec((1, TC), lambda i: (0, i))],
        out_specs=(col, col, pl.BlockSpec((1, TC), lambda i: (0, i)), pl.BlockSpec((3, TC), lambda i: (0, i))),
        compiler_params=_params(1),
    )


def _out_loss(z, x1, target, gate):
    def body(z_ref, x_ref, t_ref, g_ref, do_ref, dz_ref, dg_ref, loss_ref):
        i = pl.program_id(0)
        zv = z_ref[...]
        gv = g_ref[...]
        err = x_ref[...] + gv * zv - t_ref[...]
        dout = err * (1.0 / D)
        do_ref[...] = dout
        dz_ref[...] = (dout * gv).astype(BF16)

        @pl.when(i == 0)
        def _():
            dg_ref[...] = jnp.zeros_like(dg_ref)
            loss_ref[...] = jnp.zeros_like(loss_ref)

        dg_ref[...] += jnp.sum(dout * zv, axis=0, keepdims=True)
        part = jnp.sum(jnp.sum(err * err, axis=0, keepdims=True), axis=-1, keepdims=True) * (0.5 / D)
        lane = lax.broadcasted_iota(jnp.int32, (1, 128), 1)
        loss_ref[...] += jnp.where(lane == 0, part, 0.0)

    return pl.pallas_call(
        body, name="out_loss",
        out_shape=(jax.ShapeDtypeStruct((S, D), F32), jax.ShapeDtypeStruct((S, D), BF16),
                   jax.ShapeDtypeStruct((1, D), F32), jax.ShapeDtypeStruct((1, 128), F32)),
        grid=(S // TR,),
        in_specs=[_row_spec(), _row_spec(), _row_spec(), _vec_spec()],
        out_specs=(_row_spec(), _row_spec(), _vec_spec(), _vec_spec(128)),
        compiler_params=_params(1),
    )(z, x1, target, gate)


ADA_COLS = 6 * D // N_CHIPS
TA = 512


def _ada_fwd(c_all, w_shard, b_shard):
    def body(c_ref, w_ref, b_ref, o_ref):
        cv = c_ref[...]
        o_ref[...] = _dot_f32(cv * _sigmoid(cv), w_ref[...]) + b_ref[...]

    return pl.pallas_call(
        body, name="ada_fwd", out_shape=jax.ShapeDtypeStruct((N_DEV, ADA_COLS), F32), grid=(ADA_COLS // TA,),
        in_specs=[pl.BlockSpec((N_DEV, D), lambda j: (0, 0)), pl.BlockSpec((D, TA), lambda j: (0, j)),
                  pl.BlockSpec((1, TA), lambda j: (0, j))],
        out_specs=pl.BlockSpec((N_DEV, TA), lambda j: (0, j)),
        compiler_params=_params(1),
    )(c_all, w_shard, b_shard)


def _ada_bwd(c_all, dmod_shard):
    def body(c_ref, d_ref, o_ref):
        cv = c_ref[...]
        o_ref[...] = lax.dot_general(cv * _sigmoid(cv), d_ref[...], TN, precision=lax.Precision.HIGHEST,
                                     preferred_element_type=F32)

    return pl.pallas_call(
        body, name="ada_bwd", out_shape=jax.ShapeDtypeStruct((D, ADA_COLS), F32), grid=(ADA_COLS // TA,),
        in_specs=[pl.BlockSpec((N_DEV, D), lambda j: (0, 0)), pl.BlockSpec((N_DEV, TA), lambda j: (0, j))],
        out_specs=pl.BlockSpec((D, TA), lambda j: (0, j)),
        compiler_params=_params(1),
    )(c_all, dmod_shard)


def _sum_rows(g, name):
    rows, n = g.shape

    def body(g_ref, o_ref):
        acc = g_ref[0:1, :]
        for i in range(1, rows):
            acc = acc + g_ref[i:i + 1, :]
        o_ref[...] = acc

    return pl.pallas_call(
        body, name=name, out_shape=jax.ShapeDtypeStruct((1, n), F32),
        in_specs=[VMEM_SPEC], out_specs=VMEM_SPEC,
    )(g)


def _lb_grad(lb_logits, dlb):
    def body(l_ref, d_ref, o_ref):
        p = 1.0 / (1.0 + jnp.exp(l_ref[1:2, :] - l_ref[0:1, :]))
        t = d_ref[...] * p * (1.0 - p)
        o_ref[0:1, :] = t
        o_ref[1:2, :] = -t

    return pl.pallas_call(
        body, name="lb_grad", out_shape=jax.ShapeDtypeStruct((2, HW), F32),
        in_specs=[VMEM_SPEC, VMEM_SPEC], out_specs=VMEM_SPEC,
    )(lb_logits, dlb)


def _adamw(w, g, m, v, name):
    rows, cols = w.shape
    tr = rows
    if rows * cols * 4 > ADAM_BLOCK_BYTES:
        tr = _pick(rows, [t for t in (256, 128, 64, 32, 16, 8) if t * cols * 4 <= ADAM_BLOCK_BYTES])

    def body(w_ref, g_ref, m_ref, v_ref, d_ref, mo_ref, vo_ref):
        gv = g_ref[...]
        m2 = ADAM_B1 * m_ref[...] + (1.0 - ADAM_B1) * gv
        v2 = ADAM_B2 * v_ref[...] + (1.0 - ADAM_B2) * (gv * gv)
        m_hat = m2 / (1.0 - ADAM_B1 ** ADAM_STEP)
        v_hat = v2 / (1.0 - ADAM_B2 ** ADAM_STEP)
        d_ref[...] = -ADAM_LR * (m_hat / (jnp.sqrt(v_hat) + ADAM_EPS) + ADAM_WD * w_ref[...])
        mo_ref[...] = m2
        vo_ref[...] = v2

    spec = pl.BlockSpec((tr, cols), lambda i: (i, 0))
    out = jax.ShapeDtypeStruct((rows, cols), F32)
    return pl.pallas_call(
        body, name=name, out_shape=(out, out, out), grid=(rows // tr,),
        in_specs=[spec] * 4, out_specs=(spec,) * 3,
        compiler_params=_params(1),
    )(w, g, m, v)


W_IN, W_OUT, W_UP, W_DOWN = range(4)
IN_CHUNKS = 4
W_INQ = 4


def _join_row_chunks(chunks):
    r = chunks[0].shape[0] // 2
    return jnp.concatenate([ch[c * r:(c + 1) * r] for c in range(2) for ch in chunks], axis=0)


def _sequence_step(x, target, mod, norm1_w, lb_logits, hg_norm_w, q_norm_w, k_norm_w, norm2_w, conv_w, conv_b, net):
    shift1, scale1, gate1, shift2, scale2, gate2 = [mod[:, i * D:(i + 1) * D] for i in range(6)]

    def run(name, fn, *args, **kw):
        phase = net.host(name)
        if phase is None:
            return fn(*args, **kw)
        res, comm = fn(*args, phase=phase, **kw)
        net.done(name, comm)
        return res

    h = _norm_mod(x, norm1_w, scale1, shift1, "norm1_fwd")
    proj = run("proj_fwd", _matmul, h, net.full[W_IN], "nn", F32, "proj_fwd")
    cat, o_raw, states = run("hgrn_fwd", _hgrn_fwd, proj, lb_logits, hg_norm_w)
    att = [run(f"attn_fwd_d{dil}", _attn_fwd, proj, q_norm_w, k_norm_w, dil) for dil in DILS]
    cat, b_out_f32, lse = _attn_merge([t[0] for t in att], [t[1] for t in att], cat)
    mix = run("mix_fwd", _matmul, cat, net.full[W_OUT], "nn", F32, "mix_fwd")
    x1, h2 = _resid_norm_mod(x, mix, gate1, norm2_w, scale2, shift2, "norm2_fwd")
    u = run("up_fwd", _matmul, h2, net.full[W_UP], "nn", BF16, "up_fwd")
    yact = run("conv_gate_fwd", _conv_gate_fwd, u, conv_w, conv_b)
    z = _matmul(yact, net.full[W_DOWN], "nn", F32, "down_fwd")
    dout, dz, dgate2, loss_row = _out_loss(z, x1, target, gate2)

    net.grad[W_DOWN] = _matmul(yact, dz, "tn", BF16, "down_bwd_w")
    dyact = run("down_bwd_x", _matmul, dz, net.full[W_DOWN], "nt", BF16, "down_bwd_x")
    da, dg, dconv_b, dconv_w = run("conv_gate_bwd", _conv_gate_bwd, dyact, u, conv_w, conv_b)
    du = jnp.concatenate([da, dg], axis=1)
    net.grad[W_UP] = run("up_bwd_w", _matmul, h2, du, "tn", BF16, "up_bwd_w")
    dh2 = run("up_bwd_x", _matmul, du, net.full[W_UP], "nt", F32, "up_bwd_x")
    dx1, dshift2, dscale2, dnorm2, dgate1, dmix = run(
        "norm2_bwd", _norm_mod_bwd, dh2, x1, norm2_w, scale2, dout, "norm2_bwd", mix=mix, gate=gate1)
    net.grad[W_OUT] = _matmul(cat, dmix, "tn", BF16, "mix_bwd_w")
    dcat = run("mix_bwd_x", _matmul, dmix, net.full[W_OUT], "nt", F32, "mix_bwd_x")
    dhq, dhf, dhi, dhg, dlb, dhg_norm = run("hgrn_bwd", _hgrn_bwd, proj, lb_logits, hg_norm_w, o_raw, states, dcat)
    acc = None
    for dil in DILS:
        acc = run(f"attn_bwd_d{dil}", _attn_bwd, proj, q_norm_w, k_norm_w, b_out_f32, lse, dcat, dil, acc)
    daq, dak, dav, dq_norm, dk_norm = acc
    dproj = jnp.concatenate([dhq, dhf, dhi, dhg, daq.astype(BF16), dak.astype(BF16), dav.astype(BF16)], axis=1)
    for q in range(IN_CHUNKS):
        net.grad[W_INQ + q] = run(f"proj_bwd_w_{q}", _matmul, h, dproj, "tn", BF16, f"proj_bwd_w_{q}",
                                  m_blocks=(D // (2 * IN_CHUNKS), [q, IN_CHUNKS + q]))
    dh = run("proj_bwd_x", _matmul, dproj, net.full[W_IN], "nt", F32, "proj_bwd_x")
    grad_x, dshift1, dscale1, dnorm1 = run("norm1_bwd", _norm_mod_bwd, dh, x, norm1_w, scale1, dx1, "norm1_bwd")

    dmod = jnp.concatenate([dshift1, dscale1, dgate1, dshift2, dscale2, dgate2], axis=1)
    small = dict(norm1_w=dnorm1, lb=dlb, hg_norm_w=dhg_norm, q_norm_w=dq_norm, k_norm_w=dk_norm,
                 norm2_w=dnorm2, conv_b=dconv_b, conv_w=dconv_w)
    return loss_row, grad_x, dmod, small


def _place():
    x, y, c = lax.axis_index("x"), lax.axis_index("y"), lax.axis_index("c")
    others = [(1 - x, y), (x, 1 - y), (1 - x, 1 - y)]
    return x, y, c, others


def _remote(src, dst, send_sems, recv_sems, k, to):
    return pltpu.make_async_remote_copy(src_ref=src, dst_ref=dst, send_sem=send_sems.at[k], recv_sem=recv_sems.at[k],
                                        device_id=to, device_id_type=MESH)


class _Phase:
    def __init__(self):
        self.ins, self.outs, self.aliases, self.groups, self.n = [], [], {}, [], 0

    def add(self, ins, outs, aliases, n, copies):
        i0, o0 = len(self.ins), len(self.outs)
        self.ins += list(ins)
        self.outs += list(outs)
        self.aliases.update({i0 + i: o0 + o for i, o in aliases.items()})
        self.groups.append((slice(i0, i0 + len(ins)), slice(o0, o0 + len(outs)), copies))
        self.n += n
        return slice(o0, o0 + len(outs))

    def build(self, cin, cout, send_sems, recv_sems, landing):
        res = []
        for si, so, copies in self.groups:
            for src, dst, land, peer in copies(cin[si], cout[so]):
                k = len(res)
                res.append(_remote(land, land, send_sems, recv_sems, k, peer) if landing
                           else _remote(src, dst, send_sems, recv_sems, k, peer))
        assert len(res) == self.n
        return res


def _pcall(body, args, phase=None, **kw):
    if phase is None or not phase.groups:
        res = pl.pallas_call(body, **kw)(*args)
        return res if phase is None else (res, ())
    grid = tuple(kw.pop("grid", ()))
    out_shape, out_specs = kw.pop("out_shape"), kw.pop("out_specs")
    single = not isinstance(out_shape, (tuple, list))
    out_shape = [out_shape] if single else list(out_shape)
    out_specs = [out_specs] if single else list(out_specs)
    scratch = list(kw.pop("scratch_shapes", ()))
    n_in, n_out, n_ci, n_co = len(args), len(out_shape), len(phase.ins), len(phase.outs)

    def hosted(*refs):
        ins, cin = refs[:n_in], refs[n_in:n_in + n_ci]
        outs = refs[n_in + n_ci:n_in + n_ci + n_out]
        cout = refs[n_in + n_ci + n_out:n_in + n_ci + n_out + n_co]
        scr, send_sems, recv_sems = refs[n_in + n_ci + n_out + n_co:-2], refs[-2], refs[-1]
        ids = [pl.program_id(a) for a in range(len(grid))]

        def start():
            for cp in phase.build(cin, cout, send_sems, recv_sems, False):
                cp.start()

        def finish():
            for cp in phase.build(cin, cout, send_sems, recv_sems, False):
                cp.wait_send()
            for cp in phase.build(cin, cout, send_sems, recv_sems, True):
                cp.wait_recv()

        if grid:
            first = functools.reduce(jnp.logical_and, [i == 0 for i in ids])
            last = functools.reduce(jnp.logical_and, [i == g - 1 for i, g in zip(ids, grid)])
            pl.when(first)(start)
            body(*ins, *outs, *scr)
            pl.when(last)(finish)
        else:
            start()
            body(*ins, *outs, *scr)
            finish()

    res = pl.pallas_call(
        hosted, out_shape=tuple(out_shape + phase.outs), grid=grid,
        in_specs=list(kw.pop("in_specs")) + [HBM_SPEC] * n_ci, out_specs=tuple(out_specs + [HBM_SPEC] * n_co),
        input_output_aliases={n_in + i: n_out + o for i, o in phase.aliases.items()},
        scratch_shapes=scratch + [pltpu.SemaphoreType.DMA((phase.n,)), pltpu.SemaphoreType.DMA((phase.n,))],
        **kw,
    )(*args, *phase.ins)
    outs = res[:n_out]
    return (outs[0] if single else tuple(outs)), tuple(res[n_out:])


def _comm_only(name, phase):
    return _pcall(lambda: None, [], phase, name=name, out_shape=(), in_specs=[], out_specs=())[1]


def _all_gather_rows(block, name):
    m_per, n = block.shape

    def body(x_ref, out_ref, send_sems, recv_sems, local_sem):
        x, y, c, chips = _place()
        me, sibling = (x, y, c), (x, y, 1 - c)

        def rows(px, py, pc):
            return out_ref.at[pl.ds((4 * px + 2 * py + pc) * m_per, m_per), :]

        def copy(k, blk, to, src=None):
            return _remote(rows(*blk) if src is None else src, rows(*blk), send_sems, recv_sems, k, to)

        mine = pltpu.make_async_copy(x_ref, rows(*me), local_sem)
        mine.start()
        first = [copy(0, me, sibling, src=x_ref)]
        first += [copy(1 + j, me, (*chip, c), src=x_ref) for j, chip in enumerate(chips)]
        for cp in first:
            cp.start()
        passed = [copy(4 + j, (*chip, c), sibling) for j, chip in enumerate(chips)]
        for j, chip in enumerate(chips):
            copy(1 + j, (*chip, c), me).wait_recv()
            passed[j].start()
        copy(0, sibling, me).wait_recv()
        for j, chip in enumerate(chips):
            copy(4 + j, (*chip, 1 - c), me).wait_recv()
        for cp in first + passed:
            cp.wait_send()
        mine.wait()

    return pl.pallas_call(
        body, name=name, out_shape=jax.ShapeDtypeStruct((N_DEV * m_per, n), block.dtype),
        in_specs=[VMEM_SPEC], out_specs=VMEM_SPEC,
        scratch_shapes=[pltpu.SemaphoreType.DMA((7,)), pltpu.SemaphoreType.DMA((7,)), pltpu.SemaphoreType.DMA],
    )(block)


class _Geo:
    def __init__(self, kind, shard_shape):
        self.kind = kind
        self.shard_shape = tuple(shard_shape)
        self.hr, self.hc = shard_shape[0] // 2, shard_shape[1]
        if kind == "col":
            self.full_shape = (shard_shape[0], N_CHIPS * shard_shape[1])
        else:
            self.full_shape = (N_CHIPS * shard_shape[0], shard_shape[1])

    def full_shard(self, ref, j):
        if self.kind == "col":
            return ref.at[:, pl.ds(pl.multiple_of(j * self.hc, 128), self.hc)]
        return ref.at[pl.ds(pl.multiple_of(j * 2 * self.hr, 16), 2 * self.hr), :]

    def full_half(self, ref, j, c):
        if self.kind == "col":
            return ref.at[pl.ds(pl.multiple_of(c * self.hr, 16), self.hr),
                          pl.ds(pl.multiple_of(j * self.hc, 128), self.hc)]
        return ref.at[pl.ds(pl.multiple_of((2 * j + c) * self.hr, 16), self.hr), :]

    def piece(self, ref, j, c, p0, p1, pieces):
        pr = self.hr // pieces
        off, n = p0 * pr, (p1 - p0) * pr
        if self.kind == "col":
            return ref.at[pl.ds(pl.multiple_of(c * self.hr + off, 16), n),
                          pl.ds(pl.multiple_of(j * self.hc, 128), self.hc)]
        return ref.at[pl.ds(pl.multiple_of((2 * j + c) * self.hr + off, 16), n), :]


WEIGHT_KINDS = ("col", "row", "col", "row")
NW = len(WEIGHT_KINDS)


def _comm_call(body, name, ins, outs, n_remote, aliases=None):
    return pl.pallas_call(
        body, name=name, out_shape=tuple(outs),
        in_specs=[HBM_SPEC] * len(ins), out_specs=tuple([HBM_SPEC] * len(outs)),
        input_output_aliases=aliases or {},
        scratch_shapes=[pltpu.SemaphoreType.DMA((n_remote,)), pltpu.SemaphoreType.DMA((n_remote,))],
    )(*ins)


ROW_TILES = (256, 176, 128, 64, 32, 16)


def _cast_into_full(shard, geo, place, name):
    rs, cs = shard.shape
    tr = _pick(rs, ROW_TILES)
    nb = rs // tr

    def body(place_ref, s_ref, o_ref):
        o_ref[...] = s_ref[...].astype(BF16)

    if geo.kind == "col":
        out_map = lambda i, place_ref: (i, place_ref[0])
    else:
        out_map = lambda i, place_ref: (place_ref[0] * nb + i, 0)
    return pl.pallas_call(
        body, name=name, out_shape=jax.ShapeDtypeStruct(geo.full_shape, BF16),
        grid_spec=pltpu.PrefetchScalarGridSpec(
            num_scalar_prefetch=1, grid=(nb,),
            in_specs=[pl.BlockSpec((tr, cs), lambda i, place_ref: (i, 0))],
            out_specs=pl.BlockSpec((tr, cs), out_map)),
        compiler_params=_params(1),
    )(place, shard)


def _gather_weights(fulls, geos, name):
    nw = len(fulls)

    def body(*refs):
        full = refs[nw:2 * nw]
        send_sems, recv_sems = refs[2 * nw:]
        x, y, c, chips = _place()
        j = 2 * x + y
        sibling = (x, y, 1 - c)
        first = []
        for w in range(nw):
            mine = geos[w].full_half(full[w], j, c)
            for k, chip in enumerate(chips):
                first.append(_remote(mine, mine, send_sems, recv_sems, 6 * w + k, (*chip, c)))
        for cp in first:
            cp.start()
        passed = []
        for w in range(nw):
            for k, (ox, oy) in enumerate(chips):
                landed = geos[w].full_half(full[w], 2 * ox + oy, c)
                _remote(landed, landed, send_sems, recv_sems, 6 * w + k, (ox, oy, c)).wait_recv()
                fwd = _remote(landed, landed, send_sems, recv_sems, 6 * w + 3 + k, sibling)
                fwd.start()
                passed.append(fwd)
        for w in range(nw):
            for k, (ox, oy) in enumerate(chips):
                other = geos[w].full_half(full[w], 2 * ox + oy, 1 - c)
                _remote(other, other, send_sems, recv_sems, 6 * w + 3 + k, sibling).wait_recv()
        for cp in first + passed:
            cp.wait_send()

    outs = [jax.ShapeDtypeStruct(g.full_shape, BF16) for g in geos]
    return _comm_call(body, name, fulls, outs, 6 * nw, aliases={w: w for w in range(nw)})


def _same(a):
    return jax.ShapeDtypeStruct(a.shape, a.dtype)


def _gather_ici(full, geo, p0, p1, pieces):
    def copies(ins, outs):
        x, y, c, chips = _place()
        mine = geo.piece(outs[0], 2 * x + y, c, p0, p1, pieces)
        return [(mine, mine, geo.piece(outs[0], 2 * ox + oy, c, p0, p1, pieces), (ox, oy, c)) for ox, oy in chips]

    return dict(ins=[full], outs=[_same(full)], aliases={0: 0}, n=3, copies=copies)


def _gather_d2d(full, geo):
    def copies(ins, outs):
        x, y, c, chips = _place()
        return [(geo.full_half(outs[0], 2 * ox + oy, c), geo.full_half(outs[0], 2 * ox + oy, c),
                 geo.full_half(outs[0], 2 * ox + oy, 1 - c), (x, y, 1 - c)) for ox, oy in chips]

    return dict(ins=[full], outs=[_same(full)], aliases={0: 0}, n=3, copies=copies)


def _swap_d2d(grad, geo):
    def copies(ins, outs):
        x, y, c, _ = _place()
        return [(geo.full_half(ins[0], j, 1 - c), outs[0].at[j], outs[0].at[j], (x, y, 1 - c)) for j in range(N_CHIPS)]

    return dict(ins=[grad], outs=[jax.ShapeDtypeStruct((N_CHIPS, geo.hr, geo.hc), BF16)], aliases={}, n=N_CHIPS,
                copies=copies)


def _scatter_ici(pair, recv, geo, p0, p1, pieces):
    pr = geo.hr // pieces
    rows = pl.ds(p0 * pr, (p1 - p0) * pr)

    def copies(ins, outs):
        x, y, c, chips = _place()
        return [(ins[0].at[2 * ox + oy, rows, :], outs[0].at[k, rows, :], outs[0].at[k, rows, :], (ox, oy, c))
                for k, (ox, oy) in enumerate(chips)]

    out = jax.ShapeDtypeStruct((3, geo.hr, geo.hc), BF16)
    if recv is None:
        return dict(ins=[pair], outs=[out], aliases={}, n=3, copies=copies)
    return dict(ins=[pair, recv], outs=[out], aliases={1: 0}, n=3, copies=copies)


def _join_d2d(shard, geo):
    def copies(ins, outs):
        x, y, c, _ = _place()
        mine = outs[0].at[pl.ds(pl.multiple_of(c * geo.hr, 8), geo.hr), :]
        other = outs[0].at[pl.ds(pl.multiple_of((1 - c) * geo.hr, 8), geo.hr), :]
        return [(mine, mine, other, (x, y, 1 - c))]

    return dict(ins=[shard], outs=[_same(shard)], aliases={0: 0}, n=1, copies=copies)


def _add_pair(grad, got, geo, place, name):
    tr = _pick(geo.hr, ROW_TILES)
    nb = geo.hr // tr

    def body(place_ref, a_ref, b_ref, o_ref):
        o_ref[0] = (a_ref[...].astype(F32) + b_ref[0].astype(F32)).astype(BF16)

    if geo.kind == "col":
        own_map = lambda j, i, place_ref: (place_ref[1] * nb + i, j)
    else:
        own_map = lambda j, i, place_ref: ((2 * j + place_ref[1]) * nb + i, 0)
    spec = pl.BlockSpec((1, tr, geo.hc), lambda j, i, place_ref: (j, i, 0))
    return pl.pallas_call(
        body, name=name, out_shape=jax.ShapeDtypeStruct((N_CHIPS, geo.hr, geo.hc), BF16),
        grid_spec=pltpu.PrefetchScalarGridSpec(
            num_scalar_prefetch=1, grid=(N_CHIPS, nb),
            in_specs=[pl.BlockSpec((tr, geo.hc), own_map), spec], out_specs=spec),
        compiler_params=_params(2),
    )(place, grad, got)


def _add_four(pair, recv, geo, place, name):
    tr = _pick(geo.hr, ROW_TILES)
    nb = geo.hr // tr

    def body(place_ref, p_ref, r_ref, o_ref):
        o_ref[...] = ((p_ref[0].astype(F32) + r_ref[0].astype(F32)) + r_ref[1].astype(F32)) + r_ref[2].astype(F32)

    return pl.pallas_call(
        body, name=name, out_shape=jax.ShapeDtypeStruct((2 * geo.hr, geo.hc), F32),
        grid_spec=pltpu.PrefetchScalarGridSpec(
            num_scalar_prefetch=1, grid=(nb,),
            in_specs=[pl.BlockSpec((1, tr, geo.hc), lambda i, place_ref: (place_ref[0], i, 0)),
                      pl.BlockSpec((3, tr, geo.hc), lambda i, place_ref: (0, i, 0))],
            out_specs=pl.BlockSpec((tr, geo.hc), lambda i, place_ref: (place_ref[1] * nb + i, 0))),
        compiler_params=_params(1),
    )(place, pair, recv)


PIECES = (4, 1, 8, 4) + (1,) * IN_CHUNKS
SCHEDULE = {
    "proj_fwd": (("gather_ici", W_OUT, 0, 1), ("gather_ici", W_UP, 0, 1)),
    "hgrn_fwd": (("gather_ici", W_UP, 1, 5), ("gather_d2d", W_OUT)),
    "attn_fwd_d1": (("gather_ici", W_UP, 5, 6),),
    "attn_fwd_d4": (("gather_ici", W_UP, 6, 7),),
    "attn_fwd_d16": (("gather_ici", W_UP, 7, 8),),
    "mix_fwd": (("gather_d2d", W_UP),),
    "up_fwd": (("gather_ici", W_DOWN, 0, 4),),
    "conv_gate_fwd": (("gather_d2d", W_DOWN),),
    "down_bwd_x": (("swap", W_DOWN),),
    "conv_gate_bwd": (("scatter", W_DOWN, 0, 2),),
    "up_bwd_w": (("scatter", W_DOWN, 2, 4),),
    "up_bwd_x": (("swap", W_UP),),
    "norm2_bwd": (("scatter", W_UP, 0, 1), ("join", W_DOWN)),
    "mix_bwd_x": (("swap", W_OUT), ("scatter", W_UP, 1, 2)),
    "hgrn_bwd": (("scatter", W_UP, 2, 7),),
    "attn_bwd_d1": (("scatter", W_UP, 7, 8),),
    "attn_bwd_d4": (("scatter", W_OUT, 0, 1),),
    "attn_bwd_d16": (("join", W_UP), ("join", W_OUT)),
    "proj_bwd_w_1": (("swap", W_INQ),),
    "proj_bwd_w_2": (("swap", W_INQ + 1),),
    "proj_bwd_w_3": (("swap", W_INQ + 2), ("scatter", W_INQ, 0, 1)),
    "proj_bwd_x": (("swap", W_INQ + 3), ("scatter", W_INQ + 1, 0, 1), ("scatter", W_INQ + 2, 0, 1)),
    "norm1_bwd": (("scatter", W_INQ + 3, 0, 1), ("join", W_INQ), ("join", W_INQ + 1), ("join", W_INQ + 2)),
    "grad_in_join": (("join", W_INQ + 3),),
}


class _Net:
    def __init__(self, shards, place):
        self.place = place
        self.geos = [_Geo(k, s.shape) for k, s in zip(WEIGHT_KINDS, shards)]
        self.full = [_cast_into_full(s, g, place, f"cast_shard_{w}") for w, (s, g) in enumerate(zip(shards, self.geos))]
        self.full[W_IN] = _gather_weights([self.full[W_IN]], [self.geos[W_IN]], "gather_w_in")[0]
        rows, cols = shards[W_IN].shape
        self.geos += [_Geo("col", (rows // IN_CHUNKS, cols))] * IN_CHUNKS
        n = NW + IN_CHUNKS
        self.grad, self.pair, self.recv, self.shard = [None] * n, [None] * n, [None] * n, [None] * n
        self.slots = []

    def host(self, name):
        phase = _Phase()
        self.slots = []
        for item in SCHEDULE.get(name, ()):
            kind, w = item[0], item[1]
            geo = self.geos[w]
            if kind == "gather_ici":
                spec = _gather_ici(self.full[w], geo, item[2], item[3], PIECES[w])
            elif kind == "gather_d2d":
                spec = _gather_d2d(self.full[w], geo)
            elif kind == "swap":
                spec = _swap_d2d(self.grad[w], geo)
            elif kind == "scatter":
                spec = _scatter_ici(self.pair[w], self.recv[w], geo, item[2], item[3], PIECES[w])
            else:
                spec = _join_d2d(self.shard[w], geo)
            self.slots.append((item, phase.add(**spec)))
        return phase

    def done(self, name, comm):
        for item, sl in self.slots:
            kind, w = item[0], item[1]
            out = comm[sl][0]
            if kind in ("gather_ici", "gather_d2d"):
                self.full[w] = out
            elif kind == "swap":
                self.pair[w] = _add_pair(self.grad[w], out, self.geos[w], self.place, f"grad_pair_sum_{w}")
            elif kind == "scatter":
                self.recv[w] = out
                if item[3] == PIECES[w]:
                    self.shard[w] = _add_four(self.pair[w], out, self.geos[w], self.place, f"grad_chip_sum_{w}")
            else:
                self.shard[w] = out

    def alone(self, name):
        self.done(name, _comm_only(name, self.host(name)))


CONVW_SHARD = 3 * DFF // N_CHIPS
COND_PAD = 8 * 896
SMALL_SIZES = (("norm1_w", D), ("lb", HW), ("hg_norm_w", DH), ("q_norm_w", DH), ("k_norm_w", DH),
               ("norm2_w", D), ("conv_b", DFF), ("conv_w", 3 * DFF), ("loss", 128))
SMALL_TOTAL = sum(n for _, n in SMALL_SIZES)
STATS_PAD = 8 * 5120


def kernel(x, c, w_ada, b_ada, norm1_w, w_in, lb_logits, hg_norm_w, q_norm_w, k_norm_w, w_out, norm2_w, w_up, conv_w, conv_b, w_down, loss_target, m_w_ada, m_b_ada, m_norm1_w, m_w_in, m_lb_logits, m_hg_norm_w, m_q_norm_w, m_k_norm_w, m_w_out, m_norm2_w, m_w_up, m_conv_w, m_conv_b, m_w_down, v_w_ada, v_b_ada, v_norm1_w, v_w_in, v_lb_logits, v_hg_norm_w, v_q_norm_w, v_k_norm_w, v_w_out, v_norm2_w, v_w_up, v_conv_w, v_conv_b, v_w_down):
    chip = 2 * lax.axis_index("x") + lax.axis_index("y")
    dev = 2 * chip + lax.axis_index("c")

    cond = jnp.concatenate([c, conv_w[0].reshape(1, CONVW_SHARD), jnp.zeros((1, COND_PAD - D - CONVW_SHARD), F32)], axis=1)
    cond_all = _all_gather_rows(cond.reshape(8, COND_PAD // 8), "gather_cond").reshape(N_DEV, COND_PAD)
    c_all = cond_all[:, :D]
    conv_w_full = jnp.concatenate(
        [cond_all[2 * j, D:D + CONVW_SHARD].reshape(3, DFF // N_CHIPS) for j in range(N_CHIPS)], axis=1)

    b_shard = lax.dynamic_slice_in_dim(b_ada, chip * ADA_COLS, ADA_COLS, axis=1)
    mod_cols = _all_gather_rows(_ada_fwd(c_all, w_ada[0], b_shard), "gather_mod")
    mod_all = jnp.concatenate([mod_cols[16 * j:16 * j + 8] for j in range(N_CHIPS)], axis=1)
    mod = lax.dynamic_slice_in_dim(mod_all, dev, 1, axis=0)

    place = jnp.stack([chip, lax.axis_index("c")]).astype(jnp.int32)
    net = _Net([w_in[0], w_out[0], w_up[0], w_down[0]], place)
    loss_row, grad_x, dmod, small = _sequence_step(
        x[0], loss_target[0], mod, norm1_w, lb_logits, hg_norm_w, q_norm_w, k_norm_w, norm2_w, conv_w_full, conv_b, net)
    net.alone("grad_in_join")
    g_out, g_up, g_down = net.shard[W_OUT], net.shard[W_UP], net.shard[W_DOWN]
    g_in = _join_row_chunks(net.shard[W_INQ:W_INQ + IN_CHUNKS])

    small["conv_w"] = small["conv_w"].reshape(1, 3 * DFF)
    small["loss"] = loss_row
    stats = jnp.concatenate([dmod] + [small[k] for k, _ in SMALL_SIZES]
                            + [jnp.zeros((1, STATS_PAD - 6 * D - SMALL_TOTAL), F32)], axis=1)
    stats_all = _all_gather_rows(stats.reshape(8, STATS_PAD // 8), "gather_stats").reshape(N_DEV, STATS_PAD)
    dmod_all = stats_all[:, :6 * D]
    sums = _sum_rows(stats_all[:, 6 * D:6 * D + SMALL_TOTAL], "small_grad_sum")
    g_small, off = {}, 0
    for k, n in SMALL_SIZES:
        g_small[k] = sums[:, off:off + n]
        off += n
    loss = g_small["loss"][0, 0]
    g_b_ada = _sum_rows(dmod_all, "b_ada_grad_sum")
    g_w_ada = _ada_bwd(c_all, lax.dynamic_slice_in_dim(dmod_all, chip * ADA_COLS, ADA_COLS, axis=1))
    g_lb = _lb_grad(lb_logits, g_small["lb"])
    g_conv_w = lax.dynamic_slice_in_dim(g_small["conv_w"].reshape(3, DFF), chip * (DFF // N_CHIPS), DFF // N_CHIPS, axis=1)

    names = ["w_ada", "b_ada", "norm1_w", "w_in", "lb_logits", "hg_norm_w", "q_norm_w", "k_norm_w", "w_out",
             "norm2_w", "w_up", "conv_w", "conv_b", "w_down"]
    lead = {"w_ada", "w_in", "w_out", "w_up", "conv_w", "w_down"}
    w = dict(w_ada=w_ada, b_ada=b_ada, norm1_w=norm1_w, w_in=w_in, lb_logits=lb_logits, hg_norm_w=hg_norm_w,
             q_norm_w=q_norm_w, k_norm_w=k_norm_w, w_out=w_out, norm2_w=norm2_w, w_up=w_up, conv_w=conv_w,
             conv_b=conv_b, w_down=w_down)
    m = dict(w_ada=m_w_ada, b_ada=m_b_ada, norm1_w=m_norm1_w, w_in=m_w_in, lb_logits=m_lb_logits,
             hg_norm_w=m_hg_norm_w, q_norm_w=m_q_norm_w, k_norm_w=m_k_norm_w, w_out=m_w_out, norm2_w=m_norm2_w,
             w_up=m_w_up, conv_w=m_conv_w, conv_b=m_conv_b, w_down=m_w_down)
    v = dict(w_ada=v_w_ada, b_ada=v_b_ada, norm1_w=v_norm1_w, w_in=v_w_in, lb_logits=v_lb_logits,
             hg_norm_w=v_hg_norm_w, q_norm_w=v_q_norm_w, k_norm_w=v_k_norm_w, w_out=v_w_out, norm2_w=v_norm2_w,
             w_up=v_w_up, conv_w=v_conv_w, conv_b=v_conv_b, w_down=v_w_down)
    g = dict(w_ada=g_w_ada, b_ada=g_b_ada, norm1_w=g_small["norm1_w"], w_in=g_in, lb_logits=g_lb,
             hg_norm_w=g_small["hg_norm_w"], q_norm_w=g_small["q_norm_w"], k_norm_w=g_small["k_norm_w"], w_out=g_out,
             norm2_w=g_small["norm2_w"], w_up=g_up, conv_w=g_conv_w, conv_b=g_small["conv_b"], w_down=g_down)

    grads, deltas, new_m, new_v = [], [], [], []
    for n in names:
        strip = (lambda t: t[0]) if n in lead else (lambda t: t)
        wrap = (lambda t: t[None]) if n in lead else (lambda t: t)
        d_n, m_n, v_n = _adamw(strip(w[n]), g[n], strip(m[n]), strip(v[n]), f"adamw_{n}")
        grads.append(wrap(g[n]))
        deltas.append(wrap(d_n))
        new_m.append(wrap(m_n))
        new_v.append(wrap(v_n))
    return (loss, grad_x[None], *grads, *deltas, *new_m, *new_v)
```

```python
import functools
import math

import jax
import jax.numpy as jnp
from jax import lax
from jax.experimental import pallas as pl
from jax.experimental.pallas import tpu as pltpu

F32 = jnp.float32
BF16 = jnp.bfloat16

S = 2048
D = 2048
H = 8
DH = 128
HW = H * DH
CH = 64
NCH = S // CH
DFF = 5632
INC = 7 * HW
BLK = 128
DILS = (1, 4, 16)
EPS = 1e-6
NEG = -1e30
N_CHIPS = 4
N_DEV = 8

ADAM_LR = 0.001
ADAM_B1 = 0.9
ADAM_B2 = 0.999
ADAM_EPS = 1e-08
ADAM_WD = 0.01
ADAM_STEP = 10
ADAM_BLOCK_BYTES = 1 << 20

V7X_VMEM_BYTES = 64 * 1024 * 1024
VMEM_LIMIT = (V7X_VMEM_BYTES * 3) // 4
MESH = pl.DeviceIdType.MESH
HBM_SPEC = pl.BlockSpec(memory_space=pltpu.HBM)
VMEM_SPEC = pl.BlockSpec(memory_space=pltpu.VMEM)

NN = (((1,), (0,)), ((), ()))
NT = (((1,), (1,)), ((), ()))
TN = (((0,), (0,)), ((), ()))


def _params(n_grid):
    return pltpu.CompilerParams(dimension_semantics=("arbitrary",) * n_grid, vmem_limit_bytes=VMEM_LIMIT)


def _dot(a, b, dims=NN):
    return lax.dot_general(a.astype(BF16), b.astype(BF16), dims, preferred_element_type=F32)


def _dot_f32(a, b):
    return lax.dot_general(a, b, NN, precision=lax.Precision.HIGHEST, preferred_element_type=F32)


def _sigmoid(x):
    return 1.0 / (1.0 + jnp.exp(-x))


def _pick(n, cands):
    for t in cands:
        if n % t == 0:
            return t
    raise ValueError(n)


def _matmul(a, b, mode, out_dtype, name, phase=None, m_blocks=None):
    if mode == "nn":
        (m, k), (_, n) = a.shape, b.shape
    elif mode == "nt":
        (m, k), (n, _) = a.shape, b.shape
    else:
        (k, m), (_, n) = a.shape, b.shape
    tm = _pick(m, (1024, 512))
    a_block = lambda i: i
    if m_blocks is not None:
        tm, blocks = m_blocks
        m = tm * len(blocks)
        step = blocks[1] - blocks[0] if len(blocks) > 1 else 0
        assert all(blk == blocks[0] + step * i for i, blk in enumerate(blocks))
        a_block = lambda i: blocks[0] + step * i
    tn = _pick(n, (1024, 512))
    tk = _pick(k, (2048, 2816, 1792, 1024, 512))
    nk = k // tk
    dims = {"nn": NN, "nt": NT, "tn": TN}[mode]

    def body(a_ref, b_ref, o_ref, *acc):
        part = lax.dot_general(a_ref[...], b_ref[...], dims, preferred_element_type=F32)
        if nk == 1:
            o_ref[...] = part.astype(o_ref.dtype)
            return
        acc_ref, kk = acc[0], pl.program_id(2)

        @pl.when(kk == 0)
        def _():
            acc_ref[...] = part

        @pl.when(jnp.logical_and(kk > 0, kk < nk - 1))
        def _():
            acc_ref[...] += part

        @pl.when(kk == nk - 1)
        def _():
            o_ref[...] = (acc_ref[...] + part).astype(o_ref.dtype)

    if mode == "tn":
        a_spec = pl.BlockSpec((tk, tm), lambda i, j, kk: (kk, a_block(i)))
    else:
        a_spec = pl.BlockSpec((tm, tk), lambda i, j, kk: (i, kk))
    if mode == "nt":
        b_spec = pl.BlockSpec((tn, tk), lambda i, j, kk: (j, kk))
    else:
        b_spec = pl.BlockSpec((tk, tn), lambda i, j, kk: (kk, j))
    return _pcall(
        body, [a, b], phase, name=name,
        out_shape=jax.ShapeDtypeStruct((m, n), out_dtype),
        grid=(m // tm, n // tn, nk),
        in_specs=[a_spec, b_spec],
        out_specs=pl.BlockSpec((tm, tn), lambda i, j, kk: (i, j)),
        scratch_shapes=[pltpu.VMEM((tm, tn), F32)] if nk > 1 else [],
        compiler_params=_params(3),
    )


TR = 256


def _row_spec(cols=D):
    return pl.BlockSpec((TR, cols), lambda i: (i, 0))


def _vec_spec(cols=D):
    return pl.BlockSpec((1, cols), lambda i: (0, 0))


def _norm_mod(x, nw, scale, shift, name):
    def body(x_ref, nw_ref, sc_ref, sh_ref, h_ref):
        xv = x_ref[...]
        r = lax.rsqrt(jnp.mean(xv * xv, axis=-1, keepdims=True) + EPS)
        h_ref[...] = ((xv * r) * nw_ref[...] * (1.0 + sc_ref[...]) + sh_ref[...]).astype(BF16)

    return pl.pallas_call(
        body, name=name, out_shape=jax.ShapeDtypeStruct((S, D), BF16), grid=(S // TR,),
        in_specs=[_row_spec(), _vec_spec(), _vec_spec(), _vec_spec()], out_specs=_row_spec(),
        compiler_params=_params(1),
    )(x, nw, scale, shift)


def _resid_norm_mod(x, mix, gate, nw, scale, shift, name):
    def body(x_ref, mix_ref, g_ref, nw_ref, sc_ref, sh_ref, x1_ref, h_ref):
        xv = x_ref[...] + g_ref[...] * mix_ref[...]
        x1_ref[...] = xv
        r = lax.rsqrt(jnp.mean(xv * xv, axis=-1, keepdims=True) + EPS)
        h_ref[...] = ((xv * r) * nw_ref[...] * (1.0 + sc_ref[...]) + sh_ref[...]).astype(BF16)

    return pl.pallas_call(
        body, name=name,
        out_shape=(jax.ShapeDtypeStruct((S, D), F32), jax.ShapeDtypeStruct((S, D), BF16)), grid=(S // TR,),
        in_specs=[_row_spec(), _row_spec(), _vec_spec(), _vec_spec(), _vec_spec(), _vec_spec()],
        out_specs=(_row_spec(), _row_spec()),
        compiler_params=_params(1),
    )(x, mix, gate, nw, scale, shift)


def _norm_mod_bwd(dh, xin, nw, scale, dres, name, mix=None, gate=None, phase=None):
    with_gate = mix is not None

    def body(*refs):
        if with_gate:
            dh_ref, x_ref, nw_ref, sc_ref, dres_ref, mix_ref, g_ref, dx_ref, dsh_ref, dsc_ref, dnw_ref, dg_ref, dmix_ref = refs
        else:
            dh_ref, x_ref, nw_ref, sc_ref, dres_ref, dx_ref, dsh_ref, dsc_ref, dnw_ref = refs
        i = pl.program_id(0)
        dhv = dh_ref[...]
        xv = x_ref[...]
        r = lax.rsqrt(jnp.mean(xv * xv, axis=-1, keepdims=True) + EPS)
        xn = xv * r
        nwv = nw_ref[...]
        one_sc = 1.0 + sc_ref[...]
        dxn = dhv * nwv * one_sc
        dx = dres_ref[...] + r * (dxn - xn * jnp.mean(dxn * xn, axis=-1, keepdims=True))
        dx_ref[...] = dx

        @pl.when(i == 0)
        def _():
            dsh_ref[...] = jnp.zeros_like(dsh_ref)
            dsc_ref[...] = jnp.zeros_like(dsc_ref)
            dnw_ref[...] = jnp.zeros_like(dnw_ref)
            if with_gate:
                dg_ref[...] = jnp.zeros_like(dg_ref)

        dsh_ref[...] += jnp.sum(dhv, axis=0, keepdims=True)
        dsc_ref[...] += jnp.sum(dhv * xn * nwv, axis=0, keepdims=True)
        dnw_ref[...] += jnp.sum(dhv * xn * one_sc, axis=0, keepdims=True)
        if with_gate:
            dg_ref[...] += jnp.sum(dx * mix_ref[...], axis=0, keepdims=True)
            dmix_ref[...] = (dx * g_ref[...]).astype(BF16)

    vec = jax.ShapeDtypeStruct((1, D), F32)
    ins = [dh, xin, nw, scale, dres]
    in_specs = [_row_spec(), _row_spec(), _vec_spec(), _vec_spec(), _row_spec()]
    outs = [jax.ShapeDtypeStruct((S, D), F32), vec, vec, vec]
    out_specs = [_row_spec(), _vec_spec(), _vec_spec(), _vec_spec()]
    if with_gate:
        ins += [mix, gate]
        in_specs += [_row_spec(), _vec_spec()]
        outs += [vec, jax.ShapeDtypeStruct((S, D), BF16)]
        out_specs += [_vec_spec(), _row_spec()]
    return _pcall(
        body, ins, phase, name=name, out_shape=tuple(outs), grid=(S // TR,), in_specs=in_specs,
        out_specs=tuple(out_specs), compiler_params=_params(1),
    )


def _tri(lower):
    row = lax.broadcasted_iota(jnp.int32, (CH, CH), 0)
    col = lax.broadcasted_iota(jnp.int32, (CH, CH), 1)
    return (row >= col) if lower else (col >= row)


def _hgrn_gates(hq, hf, lb):
    sig = _sigmoid(hf)
    f = lb + (1.0 - lb) * sig
    g = jnp.log(f)
    sq = _sigmoid(hq)
    return sig, f, g, 1.0 - f, hq * sq, sq


HG_HP = 2
HG_UNROLL = 4


def _proj_col_spec(group):
    return pl.BlockSpec((S, HG_HP * DH), lambda h: (0, group * (H // HG_HP) + h))


def _hgrn_fwd(proj, lb_logits, norm_w, phase=None):
    def body(hq_ref, hf_ref, hi_ref, hg_ref, lbl_ref, nw_ref, out_ref, oraw_ref, st_ref, s_ref):
        nw = nw_ref[...]
        lower = _tri(True)
        ltri = lower.astype(F32)
        s_ref[...] = jnp.zeros_like(s_ref)

        def chunk(n, carry):
            rows = pl.ds(pl.multiple_of(n * CH, CH), CH)
            for j in range(HG_HP):
                cols = slice(j * DH, (j + 1) * DH)
                lb = 1.0 / (1.0 + jnp.exp(lbl_ref[1:2, cols] - lbl_ref[0:1, cols]))
                hq, hf, v, hg = hq_ref[rows, cols], hf_ref[rows, cols], hi_ref[rows, cols], hg_ref[rows, cols]
                _, _, g, kk, q, _ = _hgrn_gates(hq, hf, lb)
                gc = _dot_f32(ltri, g)
                gl = jnp.sum(g, axis=0, keepdims=True)
                qe = q * jnp.exp(gc)
                ke = kk * jnp.exp(gl - gc)
                qh = q * jnp.exp(gc - 0.5 * gl)
                kh = kk * jnp.exp(0.5 * gl - gc)
                st = s_ref[j]
                st_ref[j, pl.ds(n, 1)] = st[None]
                a = jnp.where(lower, _dot(qh, kh, NT), 0.0)
                o = _dot(qe, st, NT) + _dot(a, v)
                s_ref[j] = st * jnp.exp(gl) + _dot(v, ke, TN)
                oraw_ref[rows, cols] = o
                rs = lax.rsqrt(jnp.mean(o * o, axis=-1, keepdims=True) + EPS)
                out_ref[rows, cols] = (o * rs * nw * (hg * _sigmoid(hg))).astype(BF16)
            return carry

        lax.fori_loop(0, NCH, chunk, 0, unroll=HG_UNROLL)

    head_spec = pl.BlockSpec((S, HG_HP * DH), lambda h: (0, h))
    return _pcall(
        body, [proj, proj, proj, proj, lb_logits, norm_w], phase, name="hgrn_fwd",
        out_shape=(jax.ShapeDtypeStruct((S, 2 * HW), BF16), jax.ShapeDtypeStruct((S, HW), F32),
                   jax.ShapeDtypeStruct((H, NCH, DH, DH), F32)),
        grid=(H // HG_HP,),
        in_specs=[_proj_col_spec(0), _proj_col_spec(1), _proj_col_spec(2), _proj_col_spec(3),
                  pl.BlockSpec((2, HG_HP * DH), lambda h: (0, h)), pl.BlockSpec((1, DH), lambda h: (0, 0))],
        out_specs=(head_spec, head_spec, pl.BlockSpec((HG_HP, NCH, DH, DH), lambda h: (h, 0, 0, 0))),
        scratch_shapes=[pltpu.VMEM((HG_HP, DH, DH), F32)],
        compiler_params=_params(1),
    )


def _hgrn_bwd(proj, lb_logits, norm_w, oraw, states, dout, phase=None):
    def body(hq_ref, hf_ref, hi_ref, hg_ref, lbl_ref, nw_ref, oraw_ref, st_ref, do_ref,
             dhq_ref, dhf_ref, dhi_ref, dhg_ref, dlb_ref, dnw_ref, ds_ref, acc_ref):
        h = pl.program_id(0)
        nw = nw_ref[...]
        lower = _tri(True)
        ltri = lower.astype(F32)
        utri = _tri(False).astype(F32)
        last = lax.broadcasted_iota(jnp.int32, (CH, DH), 0) == CH - 1
        ds_ref[...] = jnp.zeros_like(ds_ref)
        acc_ref[...] = jnp.zeros_like(acc_ref)

        @pl.when(h == 0)
        def _():
            dnw_ref[...] = jnp.zeros_like(dnw_ref)

        def chunk(i, carry):
            n = NCH - 1 - i
            rows = pl.ds(pl.multiple_of(n * CH, CH), CH)
            for j in range(HG_HP):
                cols = slice(j * DH, (j + 1) * DH)
                lb = 1.0 / (1.0 + jnp.exp(lbl_ref[1:2, cols] - lbl_ref[0:1, cols]))
                hq, hf, v, hg = hq_ref[rows, cols], hf_ref[rows, cols], hi_ref[rows, cols], hg_ref[rows, cols]
                sig, f, g, kk, q, sq = _hgrn_gates(hq, hf, lb)
                gc = _dot_f32(ltri, g)
                gl = jnp.sum(g, axis=0, keepdims=True)
                eg = jnp.exp(gc)
                ek = jnp.exp(gl - gc)
                gam = jnp.exp(gl)
                ph = jnp.exp(gc - 0.5 * gl)
                pk = jnp.exp(0.5 * gl - gc)
                qe, ke = q * eg, kk * ek
                qh, kh = q * ph, kk * pk
                st = st_ref[j, pl.ds(n, 1)][0]
                dst = ds_ref[j]
                a = jnp.where(lower, _dot(qh, kh, NT), 0.0)
                o = oraw_ref[rows, cols]
                rs = lax.rsqrt(jnp.mean(o * o, axis=-1, keepdims=True) + EPS)
                xh = o * rs
                sg = _sigmoid(hg)
                dgo = do_ref[rows, cols]
                d_on = dgo * (hg * sg)
                dhg_ref[rows, cols] = (dgo * xh * nw * (sg * (1.0 + hg * (1.0 - sg)))).astype(BF16)
                acc_ref[j, 1:2, :] += jnp.sum(d_on * xh, axis=0, keepdims=True)
                dy = d_on * nw
                do = rs * (dy - xh * jnp.mean(dy * xh, axis=-1, keepdims=True))
                dqe = _dot(do, st)
                da = jnp.where(lower, _dot(do, v, NT), 0.0)
                dv = _dot(a, do, TN) + _dot(ke, dst, NT)
                dke = _dot(v, dst)
                dgam = jnp.sum(dst * st, axis=0, keepdims=True)
                dqh = _dot(da, kh)
                dkh = _dot(da, qh, TN)
                dq = dqh * ph + dqe * eg
                dk = dkh * pk + dke * ek
                dgl = jnp.sum(dke * ke, axis=0, keepdims=True) + dgam * gam
                dgc = dqh * qh - dkh * kh + dqe * qe - dke * ke + jnp.where(last, dgl, 0.0)
                dg = _dot_f32(utri, dgc)
                df = dg / f - dk
                dhf_ref[rows, cols] = (df * (1.0 - lb) * sig * (1.0 - sig)).astype(BF16)
                acc_ref[j, 0:1, :] += jnp.sum(df * (1.0 - sig), axis=0, keepdims=True)
                dhq_ref[rows, cols] = (dq * (sq * (1.0 + hq * (1.0 - sq)))).astype(BF16)
                dhi_ref[rows, cols] = dv.astype(BF16)
                ds_ref[j] = dst * gam + _dot(do, qe, TN)
            return carry

        lax.fori_loop(0, NCH, chunk, 0, unroll=HG_UNROLL)
        for j in range(HG_HP):
            dlb_ref[:, j * DH:(j + 1) * DH] = acc_ref[j, 0:1, :]
            dnw_ref[...] += acc_ref[j, 1:2, :]

    head_spec = pl.BlockSpec((S, HG_HP * DH), lambda h: (0, h))
    head_out = jax.ShapeDtypeStruct((S, HW), BF16)
    return _pcall(
        body, [proj, proj, proj, proj, lb_logits, norm_w, oraw, states, dout], phase, name="hgrn_bwd",
        out_shape=(head_out, head_out, head_out, head_out,
                   jax.ShapeDtypeStruct((1, HW), F32), jax.ShapeDtypeStruct((1, DH), F32)),
        grid=(H // HG_HP,),
        in_specs=[_proj_col_spec(0), _proj_col_spec(1), _proj_col_spec(2), _proj_col_spec(3),
                  pl.BlockSpec((2, HG_HP * DH), lambda h: (0, h)), pl.BlockSpec((1, DH), lambda h: (0, 0)),
                  head_spec, pl.BlockSpec((HG_HP, NCH, DH, DH), lambda h: (h, 0, 0, 0)), head_spec],
        out_specs=(head_spec, head_spec, head_spec, head_spec,
                   pl.BlockSpec((1, HG_HP * DH), lambda h: (0, h)), pl.BlockSpec((1, DH), lambda h: (0, 0))),
        scratch_shapes=[pltpu.VMEM((HG_HP, DH, DH), F32), pltpu.VMEM((HG_HP, 8, DH), F32)],
        compiler_params=_params(1),
    )


ATT_SCALE = DH ** -0.5
HEADS_PER_STEP = {1: 8, 4: 1, 16: 1}


def _sub_rows(ref, r, dil, cols):
    if dil == 1:
        return ref[:, cols]
    return ref[pl.ds(r, BLK, stride=dil), cols]


def _set_sub_rows(ref, r, dil, cols, val):
    if dil == 1:
        ref[:, cols] = val
    else:
        ref[pl.ds(r, BLK, stride=dil), cols] = val


def _slopes(dil):
    hp = HEADS_PER_STEP[dil]
    s = jnp.asarray([dil * 2.0 ** (-(h + 1)) for h in range(H)], F32)
    return jnp.broadcast_to(s[:, None], (H, DH)).reshape(H // hp, hp, DH)


def _rms_rows(x, w):
    rs = lax.rsqrt(jnp.mean(x * x, axis=-1, keepdims=True) + EPS)
    return x * rs, rs


def _att_masks():
    qi = lax.broadcasted_iota(jnp.int32, (BLK, BLK), 0)
    kj = lax.broadcasted_iota(jnp.int32, (BLK, BLK), 1)
    steps_c = qi - kj
    steps_p = qi - kj + BLK
    return steps_c, steps_p, steps_c >= 0, steps_p <= BLK


def _attn_fwd(proj, q_w, k_w, dil, phase=None):
    hp = HEADS_PER_STEP[dil]
    rows, ng, nb = BLK * dil, H // hp, S // (BLK * dil)

    def body(q_ref, kc_ref, kp_ref, vc_ref, vp_ref, qw_ref, kw_ref, sl_ref, o_ref, lse_ref):
        n = pl.program_id(0)
        steps_c, steps_p, ok_c, ok_p = _att_masks()
        ok_p = jnp.logical_and(ok_p, n > 0)
        fc, fp = steps_c.astype(F32), steps_p.astype(F32)
        qw, kw = qw_ref[...], kw_ref[...]
        for hh in range(hp):
            cols = slice(hh * DH, (hh + 1) * DH)
            sl = sl_ref[0, hh:hh + 1, :]
            for r in range(dil):
                qn = _rms_rows(_sub_rows(q_ref, r, dil, cols), None)[0] * qw
                kcn = _rms_rows(_sub_rows(kc_ref, r, dil, cols), None)[0] * kw
                kpn = _rms_rows(_sub_rows(kp_ref, r, dil, cols), None)[0] * kw
                sc = jnp.where(ok_c, _dot(qn, kcn, NT) * ATT_SCALE - sl * fc, NEG)
                sp = jnp.where(ok_p, _dot(qn, kpn, NT) * ATT_SCALE - sl * fp, NEG)
                m = jnp.maximum(jnp.max(sc, axis=-1, keepdims=True), jnp.max(sp, axis=-1, keepdims=True))
                pc, pp = jnp.exp(sc - m), jnp.exp(sp - m)
                den = jnp.sum(pc, axis=-1, keepdims=True) + jnp.sum(pp, axis=-1, keepdims=True)
                o = (_dot(pc, _sub_rows(vc_ref, r, dil, cols)) + _dot(pp, _sub_rows(vp_ref, r, dil, cols))) / den
                _set_sub_rows(o_ref, r, dil, cols, o)
                _set_sub_rows(lse_ref, r, dil, cols, jnp.broadcast_to(m + jnp.log(den), (BLK, DH)))

    def spec(group, prev):
        if prev:
            return pl.BlockSpec((rows, hp * DH), lambda n, g: (jnp.maximum(n - 1, 0), group * ng + g))
        return pl.BlockSpec((rows, hp * DH), lambda n, g: (n, group * ng + g))

    out = jax.ShapeDtypeStruct((S, HW), F32)
    out_spec = pl.BlockSpec((rows, hp * DH), lambda n, g: (n, g))
    wspec = pl.BlockSpec((1, DH), lambda n, g: (0, 0))
    return _pcall(
        body, [proj, proj, proj, proj, proj, q_w, k_w, _slopes(dil)], phase,
        name=f"attn_fwd_d{dil}", out_shape=(out, out), grid=(nb, ng),
        in_specs=[spec(4, False), spec(5, False), spec(5, True), spec(6, False), spec(6, True), wspec, wspec,
                  pl.BlockSpec((1, hp, DH), lambda n, g: (g, 0, 0))],
        out_specs=(out_spec, out_spec),
        compiler_params=_params(2),
    )


def _attn_merge(outs, lses, cat):
    def body(o1, o2, o3, l1, l2, l3, cat_ref, ob_ref, of_ref, lse_ref):
        a, b, c = l1[...], l2[...], l3[...]
        m = jnp.maximum(jnp.maximum(a, b), c)
        ea, eb, ec = jnp.exp(a - m), jnp.exp(b - m), jnp.exp(c - m)
        den = ea + eb + ec
        o = (ea * o1[...] + eb * o2[...] + ec * o3[...]) / den
        ob_ref[...] = o.astype(BF16)
        of_ref[...] = o
        lse_ref[...] = m + jnp.log(den)

    f = jax.ShapeDtypeStruct((S, HW), F32)
    return pl.pallas_call(
        body, name="attn_merge", out_shape=(jax.ShapeDtypeStruct((S, 2 * HW), BF16), f, f), grid=(S // TR,),
        in_specs=[_row_spec(HW)] * 6 + [pl.BlockSpec(memory_space=pl.ANY)],
        out_specs=(pl.BlockSpec((TR, HW), lambda i: (i, 1)), _row_spec(HW), _row_spec(HW)),
        input_output_aliases={6: 0},
        compiler_params=_params(1),
    )(*outs, *lses, cat)


def _attn_bwd(proj, q_w, k_w, o, lse, do, dil, acc, phase=None):
    hp = HEADS_PER_STEP[dil]
    rows, ng, nb = BLK * dil, H // hp, S // (BLK * dil)
    has_acc = acc is not None

    def body(*refs):
        (q_ref, qn_ref, kp_ref, kc_ref, vp_ref, vc_ref, o_ref, on_ref, l_ref, ln_ref, do_ref, don_ref,
         qw_ref, kw_ref, sl_ref) = refs[:15]
        refs = refs[15:]
        if has_acc:
            aq_ref, ak_ref, av_ref, aqw_ref, akw_ref = refs[:5]
            refs = refs[5:]
        dq_ref, dk_ref, dv_ref, dqw_ref, dkw_ref = refs
        m, g = pl.program_id(0), pl.program_id(1)
        steps_c, steps_p, ok_c, ok_p = _att_masks()
        ok_prev = jnp.logical_and(ok_p, m > 0)
        ok_next = jnp.logical_and(ok_p, m < nb - 1)
        fc, fp = steps_c.astype(F32), steps_p.astype(F32)
        qw, kw = qw_ref[...], kw_ref[...]

        @pl.when(jnp.logical_and(m == 0, g == 0))
        def _():
            if has_acc:
                dqw_ref[...] = aqw_ref[...]
                dkw_ref[...] = akw_ref[...]
            else:
                dqw_ref[...] = jnp.zeros_like(dqw_ref)
                dkw_ref[...] = jnp.zeros_like(dkw_ref)

        for hh in range(hp):
            cols = slice(hh * DH, (hh + 1) * DH)
            sl = sl_ref[0, hh:hh + 1, :]
            for r in range(dil):
                sub = lambda ref: _sub_rows(ref, r, dil, cols)
                qx, qrs = _rms_rows(sub(q_ref), None)
                kx, krs = _rms_rows(sub(kc_ref), None)
                qn, kcn = qx * qw, kx * kw
                qnn = _rms_rows(sub(qn_ref), None)[0] * qw
                kpn = _rms_rows(sub(kp_ref), None)[0] * kw
                vc, vp = sub(vc_ref), sub(vp_ref)
                dov, donv = sub(do_ref), sub(don_ref)
                lse_m, lse_n = sub(l_ref)[:, 0:1], sub(ln_ref)[:, 0:1]
                delta_m = jnp.sum(dov * sub(o_ref), axis=-1, keepdims=True)
                delta_n = jnp.sum(donv * sub(on_ref), axis=-1, keepdims=True)
                p_c = jnp.where(ok_c, jnp.exp(_dot(qn, kcn, NT) * ATT_SCALE - sl * fc - lse_m), 0.0)
                p_p = jnp.where(ok_prev, jnp.exp(_dot(qn, kpn, NT) * ATT_SCALE - sl * fp - lse_m), 0.0)
                p_n = jnp.where(ok_next, jnp.exp(_dot(qnn, kcn, NT) * ATT_SCALE - sl * fp - lse_n), 0.0)
                ds_c = p_c * (_dot(dov, vc, NT) - delta_m)
                ds_p = p_p * (_dot(dov, vp, NT) - delta_m)
                ds_n = p_n * (_dot(donv, vc, NT) - delta_n)
                dqn = (_dot(ds_c, kcn) + _dot(ds_p, kpn)) * ATT_SCALE
                dkn = (_dot(ds_c, qn, TN) + _dot(ds_n, qnn, TN)) * ATT_SCALE
                dv = _dot(p_c, dov, TN) + _dot(p_n, donv, TN)
                dqw_ref[...] += jnp.sum(dqn * qx, axis=0, keepdims=True)
                dkw_ref[...] += jnp.sum(dkn * kx, axis=0, keepdims=True)
                dyq, dyk = dqn * qw, dkn * kw
                daq = qrs * (dyq - qx * jnp.mean(dyq * qx, axis=-1, keepdims=True))
                dak = krs * (dyk - kx * jnp.mean(dyk * kx, axis=-1, keepdims=True))
                if has_acc:
                    daq, dak, dv = daq + sub(aq_ref), dak + sub(ak_ref), dv + sub(av_ref)
                _set_sub_rows(dq_ref, r, dil, cols, daq)
                _set_sub_rows(dk_ref, r, dil, cols, dak)
                _set_sub_rows(dv_ref, r, dil, cols, dv)

    def shifted(shift, m):
        if shift < 0:
            return jnp.maximum(m - 1, 0)
        if shift > 0:
            return jnp.minimum(m + 1, nb - 1)
        return m

    def pspec(group, shift):
        return pl.BlockSpec((rows, hp * DH), lambda m, g: (shifted(shift, m), group * ng + g))

    cur = pl.BlockSpec((rows, hp * DH), lambda m, g: (m, g))
    nxt = pl.BlockSpec((rows, hp * DH), lambda m, g: (shifted(1, m), g))
    wspec = pl.BlockSpec((1, DH), lambda m, g: (0, 0))
    ins = [proj, proj, proj, proj, proj, proj, o, o, lse, lse, do, do, q_w, k_w, _slopes(dil)]
    in_specs = [pspec(4, 0), pspec(4, 1), pspec(5, -1), pspec(5, 0), pspec(6, -1), pspec(6, 0),
                cur, nxt, cur, nxt, pspec(1, 0), pspec(1, 1), wspec, wspec,
                pl.BlockSpec((1, hp, DH), lambda m, g: (g, 0, 0))]
    if has_acc:
        ins += list(acc)
        in_specs += [cur, cur, cur, wspec, wspec]
    big = jax.ShapeDtypeStruct((S, HW), F32)
    small = jax.ShapeDtypeStruct((1, DH), F32)
    return _pcall(
        body, ins, phase, name=f"attn_bwd_d{dil}", out_shape=(big, big, big, small, small), grid=(nb, ng),
        in_specs=in_specs, out_specs=(cur, cur, cur, wspec, wspec),
        compiler_params=_params(2),
    )


TC = 512
NCT = DFF // TC


def _shift_rows(a, k):
    rows = lax.broadcasted_iota(jnp.int32, a.shape, 0)
    rolled = pltpu.roll(a, k % S, 0)
    if k > 0:
        return jnp.where(rows >= k, rolled, 0.0)
    return jnp.where(rows < S + k, rolled, 0.0)


def _conv_pre(a, w_ref, b_ref):
    return b_ref[...] + w_ref[0:1, :] * _shift_rows(a, 2) + w_ref[1:2, :] * _shift_rows(a, 1) + w_ref[2:3, :] * a


def _conv_gate_fwd(u, conv_w, conv_b, phase=None):
    def body(a_ref, g_ref, w_ref, b_ref, y_ref):
        pre = _conv_pre(a_ref[...].astype(F32), w_ref, b_ref)
        y_ref[...] = (pre * _sigmoid(pre) * g_ref[...].astype(F32)).astype(BF16)

    return _pcall(
        body, [u, u, conv_w, conv_b], phase,
        name="conv_gate_fwd", out_shape=jax.ShapeDtypeStruct((S, DFF), BF16), grid=(NCT,),
        in_specs=[pl.BlockSpec((S, TC), lambda i: (0, i)), pl.BlockSpec((S, TC), lambda i: (0, NCT + i)),
                  pl.BlockSpec((3, TC), lambda i: (0, i)), pl.BlockSpec((1, TC), lambda i: (0, i))],
        out_specs=pl.BlockSpec((S, TC), lambda i: (0, i)),
        compiler_params=_params(1),
    )


def _conv_gate_bwd(dy, u, conv_w, conv_b, phase=None):
    def body(dy_ref, a_ref, g_ref, w_ref, b_ref, da_ref, dg_ref, db_ref, dw_ref):
        a = a_ref[...].astype(F32)
        dyv = dy_ref[...].astype(F32)
        pre = _conv_pre(a, w_ref, b_ref)
        sg = _sigmoid(pre)
        dg_ref[...] = (dyv * pre * sg).astype(BF16)
        dpre = dyv * g_ref[...].astype(F32) * (sg * (1.0 + pre * (1.0 - sg)))
        da = w_ref[2:3, :] * dpre + w_ref[1:2, :] * _shift_rows(dpre, -1) + w_ref[0:1, :] * _shift_rows(dpre, -2)
        da_ref[...] = da.astype(BF16)
        db_ref[...] = jnp.sum(dpre, axis=0, keepdims=True)
        dw_ref[0:1, :] = jnp.sum(dpre * _shift_rows(a, 2), axis=0, keepdims=True)
        dw_ref[1:2, :] = jnp.sum(dpre * _shift_rows(a, 1), axis=0, keepdims=True)
        dw_ref[2:3, :] = jnp.sum(dpre * a, axis=0, keepdims=True)

    col = pl.BlockSpec((S, TC), lambda i: (0, i))
    half = jax.ShapeDtypeStruct((S, DFF), BF16)
    return _pcall(
        body, [dy, u, u, conv_w, conv_b], phase, name="conv_gate_bwd",
        out_shape=(half, half, jax.ShapeDtypeStruct((1, DFF), F32), jax.ShapeDtypeStruct((3, DFF), F32)),
        grid=(NCT,),
        in_specs=[col, col, pl.BlockSpec((S, TC), lambda i: (0, NCT + i)),
                  pl.BlockSpec((3, TC), lambda i: (0, i)), pl.BlockSpec((1, TC), lambda i: (0, i))],
        out_specs=(col, col, pl.BlockSpec((1, TC), lambda i: (0, i)), pl.BlockSpec((3, TC), lambda i: (0, i))),
        compiler_params=_params(1),
    )


def _out_loss(z, x1, target, gate):
    def body(z_ref, x_ref, t_ref, g_ref, do_ref, dz_ref, dg_ref, loss_ref):
        i = pl.program_id(0)
        zv = z_ref[...]
        gv = g_ref[...]
        err = x_ref[...] + gv * zv - t_ref[...]
        dout = err * (1.0 / D)
        do_ref[...] = dout
        dz_ref[...] = (dout * gv).astype(BF16)

        @pl.when(i == 0)
        def _():
            dg_ref[...] = jnp.zeros_like(dg_ref)
            loss_ref[...] = jnp.zeros_like(loss_ref)

        dg_ref[...] += jnp.sum(dout * zv, axis=0, keepdims=True)
        part = jnp.sum(jnp.sum(err * err, axis=0, keepdims=True), axis=-1, keepdims=True) * (0.5 / D)
        lane = lax.broadcasted_iota(jnp.int32, (1, 128), 1)
        loss_ref[...] += jnp.where(lane == 0, part, 0.0)

    return pl.pallas_call(
        body, name="out_loss",
        out_shape=(jax.ShapeDtypeStruct((S, D), F32), jax.ShapeDtypeStruct((S, D), BF16),
                   jax.ShapeDtypeStruct((1, D), F32), jax.ShapeDtypeStruct((1, 128), F32)),
        grid=(S // TR,),
        in_specs=[_row_spec(), _row_spec(), _row_spec(), _vec_spec()],
        out_specs=(_row_spec(), _row_spec(), _vec_spec(), _vec_spec(128)),
        compiler_params=_params(1),
    )(z, x1, target, gate)


ADA_COLS = 6 * D // N_CHIPS
TA = 512


def _ada_fwd(c_all, w_shard, b_shard):
    def body(c_ref, w_ref, b_ref, o_ref):
        cv = c_ref[...]
        o_ref[...] = _dot_f32(cv * _sigmoid(cv), w_ref[...]) + b_ref[...]

    return pl.pallas_call(
        body, name="ada_fwd", out_shape=jax.ShapeDtypeStruct((N_DEV, ADA_COLS), F32), grid=(ADA_COLS // TA,),
        in_specs=[pl.BlockSpec((N_DEV, D), lambda j: (0, 0)), pl.BlockSpec((D, TA), lambda j: (0, j)),
                  pl.BlockSpec((1, TA), lambda j: (0, j))],
        out_specs=pl.BlockSpec((N_DEV, TA), lambda j: (0, j)),
        compiler_params=_params(1),
    )(c_all, w_shard, b_shard)


def _ada_bwd(c_all, dmod_shard):
    def body(c_ref, d_ref, o_ref):
        cv = c_ref[...]
        o_ref[...] = lax.dot_general(cv * _sigmoid(cv), d_ref[...], TN, precision=lax.Precision.HIGHEST,
                                     preferred_element_type=F32)

    return pl.pallas_call(
        body, name="ada_bwd", out_shape=jax.ShapeDtypeStruct((D, ADA_COLS), F32), grid=(ADA_COLS // TA,),
        in_specs=[pl.BlockSpec((N_DEV, D), lambda j: (0, 0)), pl.BlockSpec((N_DEV, TA), lambda j: (0, j))],
        out_specs=pl.BlockSpec((D, TA), lambda j: (0, j)),
        compiler_params=_params(1),
    )(c_all, dmod_shard)


def _sum_rows(g, name):
    rows, n = g.shape

    def body(g_ref, o_ref):
        acc = g_ref[0:1, :]
        for i in range(1, rows):
            acc = acc + g_ref[i:i + 1, :]
        o_ref[...] = acc

    return pl.pallas_call(
        body, name=name, out_shape=jax.ShapeDtypeStruct((1, n), F32),
        in_specs=[VMEM_SPEC], out_specs=VMEM_SPEC,
    )(g)


def _lb_grad(lb_logits, dlb):
    def body(l_ref, d_ref, o_ref):
        p = 1.0 / (1.0 + jnp.exp(l_ref[1:2, :] - l_ref[0:1, :]))
        t = d_ref[...] * p * (1.0 - p)
        o_ref[0:1, :] = t
        o_ref[1:2, :] = -t

    return pl.pallas_call(
        body, name="lb_grad", out_shape=jax.ShapeDtypeStruct((2, HW), F32),
        in_specs=[VMEM_SPEC, VMEM_SPEC], out_specs=VMEM_SPEC,
    )(lb_logits, dlb)


def _adamw(w, g, m, v, name):
    rows, cols = w.shape
    tr = rows
    if rows * cols * 4 > ADAM_BLOCK_BYTES:
        tr = _pick(rows, [t for t in (256, 128, 64, 32, 16, 8) if t * cols * 4 <= ADAM_BLOCK_BYTES])

    def body(w_ref, g_ref, m_ref, v_ref, d_ref, mo_ref, vo_ref):
        gv = g_ref[...]
        m2 = ADAM_B1 * m_ref[...] + (1.0 - ADAM_B1) * gv
        v2 = ADAM_B2 * v_ref[...] + (1.0 - ADAM_B2) * (gv * gv)
        m_hat = m2 / (1.0 - ADAM_B1 ** ADAM_STEP)
        v_hat = v2 / (1.0 - ADAM_B2 ** ADAM_STEP)
        d_ref[...] = -ADAM_LR * (m_hat / (jnp.sqrt(v_hat) + ADAM_EPS) + ADAM_WD * w_ref[...])
        mo_ref[...] = m2
        vo_ref[...] = v2

    spec = pl.BlockSpec((tr, cols), lambda i: (i, 0))
    out = jax.ShapeDtypeStruct((rows, cols), F32)
    return pl.pallas_call(
        body, name=name, out_shape=(out, out, out), grid=(rows // tr,),
        in_specs=[spec] * 4, out_specs=(spec,) * 3,
        compiler_params=_params(1),
    )(w, g, m, v)


W_IN, W_OUT, W_UP, W_DOWN = range(4)
IN_CHUNKS = 4
W_INQ = 4


def _join_row_chunks(chunks):
    return jnp.concatenate(chunks, axis=0)


def _sequence_step(x, target, mod, norm1_w, lb_logits, hg_norm_w, q_norm_w, k_norm_w, norm2_w, conv_w, conv_b, net):
    shift1, scale1, gate1, shift2, scale2, gate2 = [mod[:, i * D:(i + 1) * D] for i in range(6)]

    def run(name, fn, *args, **kw):
        phase = net.host(name)
        if phase is None:
            return fn(*args, **kw)
        res, comm = fn(*args, phase=phase, **kw)
        net.done(name, comm)
        return res

    h = _norm_mod(x, norm1_w, scale1, shift1, "norm1_fwd")
    proj = run("proj_fwd", _matmul, h, net.full[W_IN], "nn", F32, "proj_fwd")
    cat, o_raw, states = run("hgrn_fwd", _hgrn_fwd, proj, lb_logits, hg_norm_w)
    att = [run(f"attn_fwd_d{dil}", _attn_fwd, proj, q_norm_w, k_norm_w, dil) for dil in DILS]
    cat, b_out_f32, lse = _attn_merge([t[0] for t in att], [t[1] for t in att], cat)
    mix = run("mix_fwd", _matmul, cat, net.full[W_OUT], "nn", F32, "mix_fwd")
    x1, h2 = _resid_norm_mod(x, mix, gate1, norm2_w, scale2, shift2, "norm2_fwd")
    u = run("up_fwd", _matmul, h2, net.full[W_UP], "nn", BF16, "up_fwd")
    yact = run("conv_gate_fwd", _conv_gate_fwd, u, conv_w, conv_b)
    z = _matmul(yact, net.full[W_DOWN], "nn", F32, "down_fwd")
    dout, dz, dgate2, loss_row = _out_loss(z, x1, target, gate2)

    net.grad[W_DOWN] = _matmul(yact, dz, "tn", BF16, "down_bwd_w")
    dyact = run("down_bwd_x", _matmul, dz, net.full[W_DOWN], "nt", BF16, "down_bwd_x")
    da, dg, dconv_b, dconv_w = run("conv_gate_bwd", _conv_gate_bwd, dyact, u, conv_w, conv_b)
    du = jnp.concatenate([da, dg], axis=1)
    net.grad[W_UP] = run("up_bwd_w", _matmul, h2, du, "tn", BF16, "up_bwd_w")
    dh2 = run("up_bwd_x", _matmul, du, net.full[W_UP], "nt", F32, "up_bwd_x")
    dx1, dshift2, dscale2, dnorm2, dgate1, dmix = run(
        "norm2_bwd", _norm_mod_bwd, dh2, x1, norm2_w, scale2, dout, "norm2_bwd", mix=mix, gate=gate1)
    net.grad[W_OUT] = _matmul(cat, dmix, "tn", BF16, "mix_bwd_w")
    dcat = run("mix_bwd_x", _matmul, dmix, net.full[W_OUT], "nt", F32, "mix_bwd_x")
    dhq, dhf, dhi, dhg, dlb, dhg_norm = run("hgrn_bwd", _hgrn_bwd, proj, lb_logits, hg_norm_w, o_raw, states, dcat)
    acc = None
    for dil in DILS:
        acc = run(f"attn_bwd_d{dil}", _attn_bwd, proj, q_norm_w, k_norm_w, b_out_f32, lse, dcat, dil, acc)
    daq, dak, dav, dq_norm, dk_norm = acc
    dproj = jnp.concatenate([dhq, dhf, dhi, dhg, daq.astype(BF16), dak.astype(BF16), dav.astype(BF16)], axis=1)
    for q in range(IN_CHUNKS):
        net.grad[W_INQ + q] = run(f"proj_bwd_w_{q}", _matmul, h, dproj, "tn", BF16, f"proj_bwd_w_{q}",
                                  m_blocks=(D // IN_CHUNKS, [q]))
    dh = run("proj_bwd_x", _matmul, dproj, net.full[W_IN], "nt", F32, "proj_bwd_x")
    grad_x, dshift1, dscale1, dnorm1 = run("norm1_bwd", _norm_mod_bwd, dh, x, norm1_w, scale1, dx1, "norm1_bwd")

    dmod = jnp.concatenate([dshift1, dscale1, dgate1, dshift2, dscale2, dgate2], axis=1)
    small = dict(norm1_w=dnorm1, lb=dlb, hg_norm_w=dhg_norm, q_norm_w=dq_norm, k_norm_w=dk_norm,
                 norm2_w=dnorm2, conv_b=dconv_b, conv_w=dconv_w)
    return loss_row, grad_x, dmod, small


def _place():
    x, y, c = lax.axis_index("x"), lax.axis_index("y"), lax.axis_index("c")
    others = [(1 - x, y), (x, 1 - y), (1 - x, 1 - y)]
    return x, y, c, others


def _remote(src, dst, send_sems, recv_sems, k, to):
    return pltpu.make_async_remote_copy(src_ref=src, dst_ref=dst, send_sem=send_sems.at[k], recv_sem=recv_sems.at[k],
                                        device_id=to, device_id_type=MESH)


class _Phase:
    def __init__(self):
        self.ins, self.outs, self.aliases, self.groups, self.n = [], [], {}, [], 0

    def add(self, ins, outs, aliases, n, copies):
        i0, o0 = len(self.ins), len(self.outs)
        self.ins += list(ins)
        self.outs += list(outs)
        self.aliases.update({i0 + i: o0 + o for i, o in aliases.items()})
        self.groups.append((slice(i0, i0 + len(ins)), slice(o0, o0 + len(outs)), copies))
        self.n += n
        return slice(o0, o0 + len(outs))

    def build(self, cin, cout, send_sems, recv_sems, landing):
        res = []
        for si, so, copies in self.groups:
            for src, dst, land, peer in copies(cin[si], cout[so]):
                k = len(res)
                res.append(_remote(land, land, send_sems, recv_sems, k, peer) if landing
                           else _remote(src, dst, send_sems, recv_sems, k, peer))
        assert len(res) == self.n
        return res


def _pcall(body, args, phase=None, **kw):
    if phase is None or not phase.groups:
        res = pl.pallas_call(body, **kw)(*args)
        return res if phase is None else (res, ())
    grid = tuple(kw.pop("grid", ()))
    out_shape, out_specs = kw.pop("out_shape"), kw.pop("out_specs")
    single = not isinstance(out_shape, (tuple, list))
    out_shape = [out_shape] if single else list(out_shape)
    out_specs = [out_specs] if single else list(out_specs)
    scratch = list(kw.pop("scratch_shapes", ()))
    n_in, n_out, n_ci, n_co = len(args), len(out_shape), len(phase.ins), len(phase.outs)

    def hosted(*refs):
        ins, cin = refs[:n_in], refs[n_in:n_in + n_ci]
        outs = refs[n_in + n_ci:n_in + n_ci + n_out]
        cout = refs[n_in + n_ci + n_out:n_in + n_ci + n_out + n_co]
        scr, send_sems, recv_sems = refs[n_in + n_ci + n_out + n_co:-2], refs[-2], refs[-1]
        ids = [pl.program_id(a) for a in range(len(grid))]

        def start():
            for cp in phase.build(cin, cout, send_sems, recv_sems, False):
                cp.start()

        def finish():
            for cp in phase.build(cin, cout, send_sems, recv_sems, False):
                cp.wait_send()
            for cp in phase.build(cin, cout, send_sems, recv_sems, True):
                cp.wait_recv()

        if grid:
            first = functools.reduce(jnp.logical_and, [i == 0 for i in ids])
            last = functools.reduce(jnp.logical_and, [i == g - 1 for i, g in zip(ids, grid)])
            pl.when(first)(start)
            body(*ins, *outs, *scr)
            pl.when(last)(finish)
        else:
            start()
            body(*ins, *outs, *scr)
            finish()

    res = pl.pallas_call(
        hosted, out_shape=tuple(out_shape + phase.outs), grid=grid,
        in_specs=list(kw.pop("in_specs")) + [HBM_SPEC] * n_ci, out_specs=tuple(out_specs + [HBM_SPEC] * n_co),
        input_output_aliases={n_in + i: n_out + o for i, o in phase.aliases.items()},
        scratch_shapes=scratch + [pltpu.SemaphoreType.DMA((phase.n,)), pltpu.SemaphoreType.DMA((phase.n,))],
        **kw,
    )(*args, *phase.ins)
    outs = res[:n_out]
    return (outs[0] if single else tuple(outs)), tuple(res[n_out:])


def _comm_only(name, phase):
    return _pcall(lambda: None, [], phase, name=name, out_shape=(), in_specs=[], out_specs=())[1]


def _all_gather_rows(block, name):
    m_per, n = block.shape

    def body(x_ref, out_ref, send_sems, recv_sems, local_sem):
        x, y, c, chips = _place()
        me, sibling = (x, y, c), (x, y, 1 - c)

        def rows(px, py, pc):
            return out_ref.at[pl.ds((4 * px + 2 * py + pc) * m_per, m_per), :]

        def copy(k, blk, to, src=None):
            return _remote(rows(*blk) if src is None else src, rows(*blk), send_sems, recv_sems, k, to)

        mine = pltpu.make_async_copy(x_ref, rows(*me), local_sem)
        mine.start()
        first = [copy(0, me, sibling, src=x_ref)]
        first += [copy(1 + j, me, (*chip, c), src=x_ref) for j, chip in enumerate(chips)]
        for cp in first:
            cp.start()
        passed = [copy(4 + j, (*chip, c), sibling) for j, chip in enumerate(chips)]
        for j, chip in enumerate(chips):
            copy(1 + j, (*chip, c), me).wait_recv()
            passed[j].start()
        copy(0, sibling, me).wait_recv()
        for j, chip in enumerate(chips):
            copy(4 + j, (*chip, 1 - c), me).wait_recv()
        for cp in first + passed:
            cp.wait_send()
        mine.wait()

    return pl.pallas_call(
        body, name=name, out_shape=jax.ShapeDtypeStruct((N_DEV * m_per, n), block.dtype),
        in_specs=[VMEM_SPEC], out_specs=VMEM_SPEC,
        scratch_shapes=[pltpu.SemaphoreType.DMA((7,)), pltpu.SemaphoreType.DMA((7,)), pltpu.SemaphoreType.DMA],
    )(block)


class _Geo:
    def __init__(self, kind, shard_shape):
        self.kind = kind
        self.shard_shape = tuple(shard_shape)
        self.hr, self.hc = shard_shape[0] // 2, shard_shape[1]
        if kind == "col":
            self.full_shape = (shard_shape[0], N_CHIPS * shard_shape[1])
        else:
            self.full_shape = (N_CHIPS * shard_shape[0], shard_shape[1])

    def full_shard(self, ref, j):
        if self.kind == "col":
            return ref.at[:, pl.ds(pl.multiple_of(j * self.hc, 128), self.hc)]
        return ref.at[pl.ds(pl.multiple_of(j * 2 * self.hr, 16), 2 * self.hr), :]

    def full_half(self, ref, j, c):
        if self.kind == "col":
            return ref.at[pl.ds(pl.multiple_of(c * self.hr, 16), self.hr),
                          pl.ds(pl.multiple_of(j * self.hc, 128), self.hc)]
        return ref.at[pl.ds(pl.multiple_of((2 * j + c) * self.hr, 16), self.hr), :]

    def piece(self, ref, j, c, p0, p1, pieces):
        pr = self.hr // pieces
        off, n = p0 * pr, (p1 - p0) * pr
        if self.kind == "col":
            return ref.at[pl.ds(pl.multiple_of(c * self.hr + off, 16), n),
                          pl.ds(pl.multiple_of(j * self.hc, 128), self.hc)]
        return ref.at[pl.ds(pl.multiple_of((2 * j + c) * self.hr + off, 16), n), :]


WEIGHT_KINDS = ("col", "row", "col", "row")
NW = len(WEIGHT_KINDS)


def _comm_call(body, name, ins, outs, n_remote, aliases=None):
    return pl.pallas_call(
        body, name=name, out_shape=tuple(outs),
        in_specs=[HBM_SPEC] * len(ins), out_specs=tuple([HBM_SPEC] * len(outs)),
        input_output_aliases=aliases or {},
        scratch_shapes=[pltpu.SemaphoreType.DMA((n_remote,)), pltpu.SemaphoreType.DMA((n_remote,))],
    )(*ins)


ROW_TILES = (256, 176, 128, 64, 32, 16)


def _cast_into_full(shard, geo, place, name):
    rs, cs = shard.shape
    tr = _pick(rs, ROW_TILES)
    nb = rs // tr

    def body(place_ref, s_ref, o_ref):
        o_ref[...] = s_ref[...].astype(BF16)

    if geo.kind == "col":
        out_map = lambda i, place_ref: (i, place_ref[0])
    else:
        out_map = lambda i, place_ref: (place_ref[0] * nb + i, 0)
    return pl.pallas_call(
        body, name=name, out_shape=jax.ShapeDtypeStruct(geo.full_shape, BF16),
        grid_spec=pltpu.PrefetchScalarGridSpec(
            num_scalar_prefetch=1, grid=(nb,),
            in_specs=[pl.BlockSpec((tr, cs), lambda i, place_ref: (i, 0))],
            out_specs=pl.BlockSpec((tr, cs), out_map)),
        compiler_params=_params(1),
    )(place, shard)


def _gather_weights(fulls, geos, name):
    nw = len(fulls)

    def body(*refs):
        full = refs[nw:2 * nw]
        send_sems, recv_sems = refs[2 * nw:]
        x, y, c, chips = _place()
        j = 2 * x + y
        sibling = (x, y, 1 - c)
        first = []
        for w in range(nw):
            mine = geos[w].full_half(full[w], j, c)
            for k, chip in enumerate(chips):
                first.append(_remote(mine, mine, send_sems, recv_sems, 6 * w + k, (*chip, c)))
        for cp in first:
            cp.start()
        passed = []
        for w in range(nw):
            for k, (ox, oy) in enumerate(chips):
                landed = geos[w].full_half(full[w], 2 * ox + oy, c)
                _remote(landed, landed, send_sems, recv_sems, 6 * w + k, (ox, oy, c)).wait_recv()
                fwd = _remote(landed, landed, send_sems, recv_sems, 6 * w + 3 + k, sibling)
                fwd.start()
                passed.append(fwd)
        for w in range(nw):
            for k, (ox, oy) in enumerate(chips):
                other = geos[w].full_half(full[w], 2 * ox + oy, 1 - c)
                _remote(other, other, send_sems, recv_sems, 6 * w + 3 + k, sibling).wait_recv()
        for cp in first + passed:
            cp.wait_send()

    outs = [jax.ShapeDtypeStruct(g.full_shape, BF16) for g in geos]
    return _comm_call(body, name, fulls, outs, 6 * nw, aliases={w: w for w in range(nw)})


def _same(a):
    return jax.ShapeDtypeStruct(a.shape, a.dtype)


def _gather_ici(full, geo, p0, p1, pieces):
    def copies(ins, outs):
        x, y, c, chips = _place()
        mine = geo.piece(outs[0], 2 * x + y, c, p0, p1, pieces)
        return [(mine, mine, geo.piece(outs[0], 2 * ox + oy, c, p0, p1, pieces), (ox, oy, c)) for ox, oy in chips]

    return dict(ins=[full], outs=[_same(full)], aliases={0: 0}, n=3, copies=copies)


def _gather_d2d(full, geo):
    def copies(ins, outs):
        x, y, c, chips = _place()
        return [(geo.full_half(outs[0], 2 * ox + oy, c), geo.full_half(outs[0], 2 * ox + oy, c),
                 geo.full_half(outs[0], 2 * ox + oy, 1 - c), (x, y, 1 - c)) for ox, oy in chips]

    return dict(ins=[full], outs=[_same(full)], aliases={0: 0}, n=3, copies=copies)


def _swap_d2d(grad, geo):
    def copies(ins, outs):
        x, y, c, _ = _place()
        return [(geo.full_half(ins[0], j, 1 - c), outs[0].at[j], outs[0].at[j], (x, y, 1 - c)) for j in range(N_CHIPS)]

    return dict(ins=[grad], outs=[jax.ShapeDtypeStruct((N_CHIPS, geo.hr, geo.hc), BF16)], aliases={}, n=N_CHIPS,
                copies=copies)


def _scatter_ici(pair, recv, geo, p0, p1, pieces):
    pr = geo.hr // pieces
    rows = pl.ds(p0 * pr, (p1 - p0) * pr)

    def copies(ins, outs):
        x, y, c, chips = _place()
        return [(ins[0].at[2 * ox + oy, rows, :], outs[0].at[k, rows, :], outs[0].at[k, rows, :], (ox, oy, c))
                for k, (ox, oy) in enumerate(chips)]

    out = jax.ShapeDtypeStruct((3, geo.hr, geo.hc), BF16)
    if recv is None:
        return dict(ins=[pair], outs=[out], aliases={}, n=3, copies=copies)
    return dict(ins=[pair, recv], outs=[out], aliases={1: 0}, n=3, copies=copies)


def _join_d2d(shard, geo):
    def copies(ins, outs):
        x, y, c, _ = _place()
        mine = outs[0].at[pl.ds(pl.multiple_of(c * geo.hr, 8), geo.hr), :]
        other = outs[0].at[pl.ds(pl.multiple_of((1 - c) * geo.hr, 8), geo.hr), :]
        return [(mine, mine, other, (x, y, 1 - c))]

    return dict(ins=[shard], outs=[_same(shard)], aliases={0: 0}, n=1, copies=copies)


def _add_pair(grad, got, geo, place, name):
    tr = _pick(geo.hr, ROW_TILES)
    nb = geo.hr // tr

    def body(place_ref, a_ref, b_ref, o_ref):
        o_ref[0] = (a_ref[...].astype(F32) + b_ref[0].astype(F32)).astype(BF16)

    if geo.kind == "col":
        own_map = lambda j, i, place_ref: (place_ref[1] * nb + i, j)
    else:
        own_map = lambda j, i, place_ref: ((2 * j + place_ref[1]) * nb + i, 0)
    spec = pl.BlockSpec((1, tr, geo.hc), lambda j, i, place_ref: (j, i, 0))
    return pl.pallas_call(
        body, name=name, out_shape=jax.ShapeDtypeStruct((N_CHIPS, geo.hr, geo.hc), BF16),
        grid_spec=pltpu.PrefetchScalarGridSpec(
            num_scalar_prefetch=1, grid=(N_CHIPS, nb),
            in_specs=[pl.BlockSpec((tr, geo.hc), own_map), spec], out_specs=spec),
        compiler_params=_params(2),
    )(place, grad, got)


def _add_four(pair, recv, geo, place, name):
    tr = _pick(geo.hr, ROW_TILES)
    nb = geo.hr // tr

    def body(place_ref, p_ref, r_ref, o_ref):
        o_ref[...] = ((p_ref[0].astype(F32) + r_ref[0].astype(F32)) + r_ref[1].astype(F32)) + r_ref[2].astype(F32)

    return pl.pallas_call(
        body, name=name, out_shape=jax.ShapeDtypeStruct((2 * geo.hr, geo.hc), F32),
        grid_spec=pltpu.PrefetchScalarGridSpec(
            num_scalar_prefetch=1, grid=(nb,),
            in_specs=[pl.BlockSpec((1, tr, geo.hc), lambda i, place_ref: (place_ref[0], i, 0)),
                      pl.BlockSpec((3, tr, geo.hc), lambda i, place_ref: (0, i, 0))],
            out_specs=pl.BlockSpec((tr, geo.hc), lambda i, place_ref: (place_ref[1] * nb + i, 0))),
        compiler_params=_params(1),
    )(place, pair, recv)


PIECES = (4, 1, 8, 4) + (1,) * IN_CHUNKS
SCHEDULE = {
    "proj_fwd": (("gather_ici", W_OUT, 0, 1), ("gather_ici", W_UP, 0, 1)),
    "hgrn_fwd": (("gather_ici", W_UP, 1, 5), ("gather_d2d", W_OUT)),
    "attn_fwd_d1": (("gather_ici", W_UP, 5, 6),),
    "attn_fwd_d4": (("gather_ici", W_UP, 6, 7),),
    "attn_fwd_d16": (("gather_ici", W_UP, 7, 8),),
    "mix_fwd": (("gather_d2d", W_UP),),
    "up_fwd": (("gather_ici", W_DOWN, 0, 4),),
    "conv_gate_fwd": (("gather_d2d", W_DOWN),),
    "down_bwd_x": (("swap", W_DOWN),),
    "conv_gate_bwd": (("scatter", W_DOWN, 0, 2),),
    "up_bwd_w": (("scatter", W_DOWN, 2, 4),),
    "up_bwd_x": (("swap", W_UP),),
    "norm2_bwd": (("scatter", W_UP, 0, 1), ("join", W_DOWN)),
    "mix_bwd_x": (("swap", W_OUT), ("scatter", W_UP, 1, 2)),
    "hgrn_bwd": (("scatter", W_UP, 2, 7),),
    "attn_bwd_d1": (("scatter", W_UP, 7, 8),),
    "attn_bwd_d4": (("scatter", W_OUT, 0, 1),),
    "attn_bwd_d16": (("join", W_UP), ("join", W_OUT)),
    "proj_bwd_w_1": (("swap", W_INQ),),
    "proj_bwd_w_2": (("swap", W_INQ + 1),),
    "proj_bwd_w_3": (("swap", W_INQ + 2), ("scatter", W_INQ, 0, 1)),
    "proj_bwd_x": (("swap", W_INQ + 3), ("scatter", W_INQ + 1, 0, 1), ("scatter", W_INQ + 2, 0, 1)),
    "norm1_bwd": (("scatter", W_INQ + 3, 0, 1), ("join", W_INQ), ("join", W_INQ + 1), ("join", W_INQ + 2)),
    "grad_in_join": (("join", W_INQ + 3),),
}


class _Net:
    def __init__(self, shards, place):
        self.place = place
        self.geos = [_Geo(k, s.shape) for k, s in zip(WEIGHT_KINDS, shards)]
        self.full = [_cast_into_full(s, g, place, f"cast_shard_{w}") for w, (s, g) in enumerate(zip(shards, self.geos))]
        self.full[W_IN] = _gather_weights([self.full[W_IN]], [self.geos[W_IN]], "gather_w_in")[0]
        rows, cols = shards[W_IN].shape
        self.geos += [_Geo("col", (rows // IN_CHUNKS, cols))] * IN_CHUNKS
        n = NW + IN_CHUNKS
        self.grad, self.pair, self.recv, self.shard = [None] * n, [None] * n, [None] * n, [None] * n
        self.slots = []

    def host(self, name):
        phase = _Phase()
        self.slots = []
        for item in SCHEDULE.get(name, ()):
            kind, w = item[0], item[1]
            geo = self.geos[w]
            if kind == "gather_ici":
                spec = _gather_ici(self.full[w], geo, item[2], item[3], PIECES[w])
            elif kind == "gather_d2d":
                spec = _gather_d2d(self.full[w], geo)
            elif kind == "swap":
                spec = _swap_d2d(self.grad[w], geo)
            elif kind == "scatter":
                spec = _scatter_ici(self.pair[w], self.recv[w], geo, item[2], item[3], PIECES[w])
            else:
                spec = _join_d2d(self.shard[w], geo)
            self.slots.append((item, phase.add(**spec)))
        return phase

    def done(self, name, comm):
        for item, sl in self.slots:
            kind, w = item[0], item[1]
            out = comm[sl][0]
            if kind in ("gather_ici", "gather_d2d"):
                self.full[w] = out
            elif kind == "swap":
                self.pair[w] = _add_pair(self.grad[w], out, self.geos[w], self.place, f"grad_pair_sum_{w}")
            elif kind == "scatter":
                self.recv[w] = out
                if item[3] == PIECES[w]:
                    self.shard[w] = _add_four(self.pair[w], out, self.geos[w], self.place, f"grad_chip_sum_{w}")
            else:
                self.shard[w] = out

    def alone(self, name):
        self.done(name, _comm_only(name, self.host(name)))


CONVW_SHARD = 3 * DFF // N_CHIPS
COND_PAD = 8 * 896
SMALL_SIZES = (("norm1_w", D), ("lb", HW), ("hg_norm_w", DH), ("q_norm_w", DH), ("k_norm_w", DH),
               ("norm2_w", D), ("conv_b", DFF), ("conv_w", 3 * DFF), ("loss", 128))
SMALL_TOTAL = sum(n for _, n in SMALL_SIZES)
STATS_PAD = 8 * 5120


def kernel(x, c, w_ada, b_ada, norm1_w, w_in, lb_logits, hg_norm_w, q_norm_w, k_norm_w, w_out, norm2_w, w_up, conv_w, conv_b, w_down, loss_target, m_w_ada, m_b_ada, m_norm1_w, m_w_in, m_lb_logits, m_hg_norm_w, m_q_norm_w, m_k_norm_w, m_w_out, m_norm2_w, m_w_up, m_conv_w, m_conv_b, m_w_down, v_w_ada, v_b_ada, v_norm1_w, v_w_in, v_lb_logits, v_hg_norm_w, v_q_norm_w, v_k_norm_w, v_w_out, v_norm2_w, v_w_up, v_conv_w, v_conv_b, v_w_down):
    chip = 2 * lax.axis_index("x") + lax.axis_index("y")
    dev = 2 * chip + lax.axis_index("c")

    cond = jnp.concatenate([c, conv_w[0].reshape(1, CONVW_SHARD), jnp.zeros((1, COND_PAD - D - CONVW_SHARD), F32)], axis=1)
    cond_all = _all_gather_rows(cond.reshape(8, COND_PAD // 8), "gather_cond").reshape(N_DEV, COND_PAD)
    c_all = cond_all[:, :D]
    conv_w_full = jnp.concatenate(
        [cond_all[2 * j, D:D + CONVW_SHARD].reshape(3, DFF // N_CHIPS) for j in range(N_CHIPS)], axis=1)

    b_shard = lax.dynamic_slice_in_dim(b_ada, chip * ADA_COLS, ADA_COLS, axis=1)
    mod_cols = _all_gather_rows(_ada_fwd(c_all, w_ada[0], b_shard), "gather_mod")
    mod_all = jnp.concatenate([mod_cols[16 * j:16 * j + 8] for j in range(N_CHIPS)], axis=1)
    mod = lax.dynamic_slice_in_dim(mod_all, dev, 1, axis=0)

    place = jnp.stack([chip, lax.axis_index("c")]).astype(jnp.int32)
    net = _Net([w_in[0], w_out[0], w_up[0], w_down[0]], place)
    loss_row, grad_x, dmod, small = _sequence_step(
        x[0], loss_target[0], mod, norm1_w, lb_logits, hg_norm_w, q_norm_w, k_norm_w, norm2_w, conv_w_full, conv_b, net)
    net.alone("grad_in_join")
    g_out, g_up, g_down = net.shard[W_OUT], net.shard[W_UP], net.shard[W_DOWN]
    g_in = _join_row_chunks(net.shard[W_INQ:W_INQ + IN_CHUNKS])

    small["conv_w"] = small["conv_w"].reshape(1, 3 * DFF)
    small["loss"] = loss_row
    stats = jnp.concatenate([dmod] + [small[k] for k, _ in SMALL_SIZES]
                            + [jnp.zeros((1, STATS_PAD - 6 * D - SMALL_TOTAL), F32)], axis=1)
    stats_all = _all_gather_rows(stats.reshape(8, STATS_PAD // 8), "gather_stats").reshape(N_DEV, STATS_PAD)
    dmod_all = stats_all[:, :6 * D]
    sums = _sum_rows(stats_all[:, 6 * D:6 * D + SMALL_TOTAL], "small_grad_sum")
    g_small, off = {}, 0
    for k, n in SMALL_SIZES:
        g_small[k] = sums[:, off:off + n]
        off += n
    loss = g_small["loss"][0, 0]
    g_b_ada = _sum_rows(dmod_all, "b_ada_grad_sum")
    g_w_ada = _ada_bwd(c_all, lax.dynamic_slice_in_dim(dmod_all, chip * ADA_COLS, ADA_COLS, axis=1))
    g_lb = _lb_grad(lb_logits, g_small["lb"])
    g_conv_w = lax.dynamic_slice_in_dim(g_small["conv_w"].reshape(3, DFF), chip * (DFF // N_CHIPS), DFF // N_CHIPS, axis=1)

    names = ["w_ada", "b_ada", "norm1_w", "w_in", "lb_logits", "hg_norm_w", "q_norm_w", "k_norm_w", "w_out",
             "norm2_w", "w_up", "conv_w", "conv_b", "w_down"]
    lead = {"w_ada", "w_in", "w_out", "w_up", "conv_w", "w_down"}
    w = dict(w_ada=w_ada, b_ada=b_ada, norm1_w=norm1_w, w_in=w_in, lb_logits=lb_logits, hg_norm_w=hg_norm_w,
             q_norm_w=q_norm_w, k_norm_w=k_norm_w, w_out=w_out, norm2_w=norm2_w, w_up=w_up, conv_w=conv_w,
             conv_b=conv_b, w_down=w_down)
    m = dict(w_ada=m_w_ada, b_ada=m_b_ada, norm1_w=m_norm1_w, w_in=m_w_in, lb_logits=m_lb_logits,
             hg_norm_w=m_hg_norm_w, q_norm_w=m_q_norm_w, k_norm_w=m_k_norm_w, w_out=m_w_out, norm2_w=m_norm2_w,
             w_up=m_w_up, conv_w=m_conv_w, conv_b=m_conv_b, w_down=m_w_down)
    v = dict(w_ada=v_w_ada, b_ada=v_b_ada, norm1_w=v_norm1_w, w_in=v_w_in, lb_logits=v_lb_logits,
             hg_norm_w=v_hg_norm_w, q_norm_w=v_q_norm_w, k_norm_w=v_k_norm_w, w_out=v_w_out, norm2_w=v_norm2_w,
             w_up=v_w_up, conv_w=v_conv_w, conv_b=v_conv_b, w_down=v_w_down)
    g = dict(w_ada=g_w_ada, b_ada=g_b_ada, norm1_w=g_small["norm1_w"], w_in=g_in, lb_logits=g_lb,
             hg_norm_w=g_small["hg_norm_w"], q_norm_w=g_small["q_norm_w"], k_norm_w=g_small["k_norm_w"], w_out=g_out,
             norm2_w=g_small["norm2_w"], w_up=g_up, conv_w=g_conv_w, conv_b=g_small["conv_b"], w_down=g_down)

    grads, deltas, new_m, new_v = [], [], [], []
    for n in names:
        strip = (lambda t: t[0]) if n in lead else (lambda t: t)
        wrap = (lambda t: t[None]) if n in lead else (lambda t: t)
        d_n, m_n, v_n = _adamw(strip(w[n]), g[n], strip(m[n]), strip(v[n]), f"adamw_{n}")
        grads.append(wrap(g[n]))
        deltas.append(wrap(d_n))
        new_m.append(wrap(m_n))
        new_v.append(wrap(v_n))
    return (loss, grad_x[None], *grads, *deltas, *new_m, *new_v)
```

```python
import functools
import math

import jax
import jax.numpy as jnp
from jax import lax
from jax.experimental import pallas as pl
from jax.experimental.pallas import tpu as pltpu

F32 = jnp.float32
BF16 = jnp.bfloat16

S = 2048
D = 2048
H = 8
DH = 128
HW = H * DH
CH = 64
NCH = S // CH
DFF = 5632
INC = 7 * HW
BLK = 128
DILS = (1, 4, 16)
EPS = 1e-6
NEG = -1e30
N_CHIPS = 4
N_DEV = 8

ADAM_LR = 0.001
ADAM_B1 = 0.9
ADAM_B2 = 0.999
ADAM_EPS = 1e-08
ADAM_WD = 0.01
ADAM_STEP = 10
ADAM_BLOCK_BYTES = 1 << 20

V7X_VMEM_BYTES = 64 * 1024 * 1024
VMEM_LIMIT = (V7X_VMEM_BYTES * 3) // 4
MESH = pl.DeviceIdType.MESH
HBM_SPEC = pl.BlockSpec(memory_space=pltpu.HBM)
VMEM_SPEC = pl.BlockSpec(memory_space=pltpu.VMEM)

NN = (((1,), (0,)), ((), ()))
NT = (((1,), (1,)), ((), ()))
TN = (((0,), (0,)), ((), ()))


def _params(n_grid):
    return pltpu.CompilerParams(dimension_semantics=("arbitrary",) * n_grid, vmem_limit_bytes=VMEM_LIMIT)


def _dot(a, b, dims=NN):
    return lax.dot_general(a.astype(BF16), b.astype(BF16), dims, preferred_element_type=F32)


def _dot_f32(a, b):
    return lax.dot_general(a, b, NN, precision=lax.Precision.HIGHEST, preferred_element_type=F32)


def _sigmoid(x):
    return 1.0 / (1.0 + jnp.exp(-x))


def _pick(n, cands):
    for t in cands:
        if n % t == 0:
            return t
    raise ValueError(n)


def _matmul(a, b, mode, out_dtype, name, phase=None, m_blocks=None):
    if mode == "nn":
        (m, k), (_, n) = a.shape, b.shape
    elif mode == "nt":
        (m, k), (n, _) = a.shape, b.shape
    else:
        (k, m), (_, n) = a.shape, b.shape
    tm = _pick(m, (1024, 512))
    a_block = lambda i: i
    if m_blocks is not None:
        tm, blocks = m_blocks
        m = tm * len(blocks)
        step = blocks[1] - blocks[0] if len(blocks) > 1 else 0
        assert all(blk == blocks[0] + step * i for i, blk in enumerate(blocks))
        a_block = lambda i: blocks[0] + step * i
    tn = _pick(n, (1024, 512))
    tk = _pick(k, (2048, 2816, 1792, 1024, 512))
    nk = k // tk
    dims = {"nn": NN, "nt": NT, "tn": TN}[mode]

    def body(a_ref, b_ref, o_ref, *acc):
        part = lax.dot_general(a_ref[...], b_ref[...], dims, preferred_element_type=F32)
        if nk == 1:
            o_ref[...] = part.astype(o_ref.dtype)
            return
        acc_ref, kk = acc[0], pl.program_id(2)

        @pl.when(kk == 0)
        def _():
            acc_ref[...] = part

        @pl.when(jnp.logical_and(kk > 0, kk < nk - 1))
        def _():
            acc_ref[...] += part

        @pl.when(kk == nk - 1)
        def _():
            o_ref[...] = (acc_ref[...] + part).astype(o_ref.dtype)

    if mode == "tn":
        a_spec = pl.BlockSpec((tk, tm), lambda i, j, kk: (kk, a_block(i)))
    else:
        a_spec = pl.BlockSpec((tm, tk), lambda i, j, kk: (i, kk))
    if mode == "nt":
        b_spec = pl.BlockSpec((tn, tk), lambda i, j, kk: (j, kk))
    else:
        b_spec = pl.BlockSpec((tk, tn), lambda i, j, kk: (kk, j))
    return _pcall(
        body, [a, b], phase, name=name,
        out_shape=jax.ShapeDtypeStruct((m, n), out_dtype),
        grid=(m // tm, n // tn, nk),
        in_specs=[a_spec, b_spec],
        out_specs=pl.BlockSpec((tm, tn), lambda i, j, kk: (i, j)),
        scratch_shapes=[pltpu.VMEM((tm, tn), F32)] if nk > 1 else [],
        compiler_params=_params(3),
    )


TR = 256


def _row_spec(cols=D):
    return pl.BlockSpec((TR, cols), lambda i: (i, 0))


def _vec_spec(cols=D):
    return pl.BlockSpec((1, cols), lambda i: (0, 0))


def _norm_mod(x, nw, scale, shift, name):
    def body(x_ref, nw_ref, sc_ref, sh_ref, h_ref):
        xv = x_ref[...]
        r = lax.rsqrt(jnp.mean(xv * xv, axis=-1, keepdims=True) + EPS)
        h_ref[...] = ((xv * r) * nw_ref[...] * (1.0 + sc_ref[...]) + sh_ref[...]).astype(BF16)

    return pl.pallas_call(
        body, name=name, out_shape=jax.ShapeDtypeStruct((S, D), BF16), grid=(S // TR,),
        in_specs=[_row_spec(), _vec_spec(), _vec_spec(), _vec_spec()], out_specs=_row_spec(),
        compiler_params=_params(1),
    )(x, nw, scale, shift)


def _resid_norm_mod(x, mix, gate, nw, scale, shift, name):
    def body(x_ref, mix_ref, g_ref, nw_ref, sc_ref, sh_ref, x1_ref, h_ref):
        xv = x_ref[...] + g_ref[...] * mix_ref[...]
        x1_ref[...] = xv
        r = lax.rsqrt(jnp.mean(xv * xv, axis=-1, keepdims=True) + EPS)
        h_ref[...] = ((xv * r) * nw_ref[...] * (1.0 + sc_ref[...]) + sh_ref[...]).astype(BF16)

    return pl.pallas_call(
        body, name=name,
        out_shape=(jax.ShapeDtypeStruct((S, D), F32), jax.ShapeDtypeStruct((S, D), BF16)), grid=(S // TR,),
        in_specs=[_row_spec(), _row_spec(), _vec_spec(), _vec_spec(), _vec_spec(), _vec_spec()],
        out_specs=(_row_spec(), _row_spec()),
        compiler_params=_params(1),
    )(x, mix, gate, nw, scale, shift)


def _norm_mod_bwd(dh, xin, nw, scale, dres, name, mix=None, gate=None, phase=None):
    with_gate = mix is not None

    def body(*refs):
        if with_gate:
            dh_ref, x_ref, nw_ref, sc_ref, dres_ref, mix_ref, g_ref, dx_ref, dsh_ref, dsc_ref, dnw_ref, dg_ref, dmix_ref = refs
        else:
            dh_ref, x_ref, nw_ref, sc_ref, dres_ref, dx_ref, dsh_ref, dsc_ref, dnw_ref = refs
        i = pl.program_id(0)
        dhv = dh_ref[...]
        xv = x_ref[...]
        r = lax.rsqrt(jnp.mean(xv * xv, axis=-1, keepdims=True) + EPS)
        xn = xv * r
        nwv = nw_ref[...]
        one_sc = 1.0 + sc_ref[...]
        dxn = dhv * nwv * one_sc
        dx = dres_ref[...] + r * (dxn - xn * jnp.mean(dxn * xn, axis=-1, keepdims=True))
        dx_ref[...] = dx

        @pl.when(i == 0)
        def _():
            dsh_ref[...] = jnp.zeros_like(dsh_ref)
            dsc_ref[...] = jnp.zeros_like(dsc_ref)
            dnw_ref[...] = jnp.zeros_like(dnw_ref)
            if with_gate:
                dg_ref[...] = jnp.zeros_like(dg_ref)

        dsh_ref[...] += jnp.sum(dhv, axis=0, keepdims=True)
        dsc_ref[...] += jnp.sum(dhv * xn * nwv, axis=0, keepdims=True)
        dnw_ref[...] += jnp.sum(dhv * xn * one_sc, axis=0, keepdims=True)
        if with_gate:
            dg_ref[...] += jnp.sum(dx * mix_ref[...], axis=0, keepdims=True)
            dmix_ref[...] = (dx * g_ref[...]).astype(BF16)

    vec = jax.ShapeDtypeStruct((1, D), F32)
    ins = [dh, xin, nw, scale, dres]
    in_specs = [_row_spec(), _row_spec(), _vec_spec(), _vec_spec(), _row_spec()]
    outs = [jax.ShapeDtypeStruct((S, D), F32), vec, vec, vec]
    out_specs = [_row_spec(), _vec_spec(), _vec_spec(), _vec_spec()]
    if with_gate:
        ins += [mix, gate]
        in_specs += [_row_spec(), _vec_spec()]
        outs += [vec, jax.ShapeDtypeStruct((S, D), BF16)]
        out_specs += [_vec_spec(), _row_spec()]
    return _pcall(
        body, ins, phase, name=name, out_shape=tuple(outs), grid=(S // TR,), in_specs=in_specs,
        out_specs=tuple(out_specs), compiler_params=_params(1),
    )


def _tri(lower):
    row = lax.broadcasted_iota(jnp.int32, (CH, CH), 0)
    col = lax.broadcasted_iota(jnp.int32, (CH, CH), 1)
    return (row >= col) if lower else (col >= row)


def _hgrn_gates(hq, hf, lb):
    sig = _sigmoid(hf)
    f = lb + (1.0 - lb) * sig
    g = jnp.log(f)
    sq = _sigmoid(hq)
    return sig, f, g, 1.0 - f, hq * sq, sq


HG_HP = 2
HG_UNROLL = 4


def _proj_col_spec(group):
    return pl.BlockSpec((S, HG_HP * DH), lambda h: (0, group * (H // HG_HP) + h))


def _hgrn_fwd(proj, lb_logits, norm_w, phase=None):
    def body(hq_ref, hf_ref, hi_ref, hg_ref, lbl_ref, nw_ref, out_ref, oraw_ref, st_ref, s_ref):
        nw = nw_ref[...]
        lower = _tri(True)
        ltri = lower.astype(F32)
        s_ref[...] = jnp.zeros_like(s_ref)

        def chunk(n, carry):
            rows = pl.ds(pl.multiple_of(n * CH, CH), CH)
            for j in range(HG_HP):
                cols = slice(j * DH, (j + 1) * DH)
                lb = 1.0 / (1.0 + jnp.exp(lbl_ref[1:2, cols] - lbl_ref[0:1, cols]))
                hq, hf, v, hg = hq_ref[rows, cols], hf_ref[rows, cols], hi_ref[rows, cols], hg_ref[rows, cols]
                _, _, g, kk, q, _ = _hgrn_gates(hq, hf, lb)
                gc = _dot_f32(ltri, g)
                gl = jnp.sum(g, axis=0, keepdims=True)
                qe = q * jnp.exp(gc)
                ke = kk * jnp.exp(gl - gc)
                qh = q * jnp.exp(gc - 0.5 * gl)
                kh = kk * jnp.exp(0.5 * gl - gc)
                st = s_ref[j]
                st_ref[j, pl.ds(n, 1)] = st[None]
                a = jnp.where(lower, _dot(qh, kh, NT), 0.0)
                o = _dot(qe, st, NT) + _dot(a, v)
                s_ref[j] = st * jnp.exp(gl) + _dot(v, ke, TN)
                oraw_ref[rows, cols] = o
                rs = lax.rsqrt(jnp.mean(o * o, axis=-1, keepdims=True) + EPS)
                out_ref[rows, cols] = (o * rs * nw * (hg * _sigmoid(hg))).astype(BF16)
            return carry

        lax.fori_loop(0, NCH, chunk, 0, unroll=HG_UNROLL)

    head_spec = pl.BlockSpec((S, HG_HP * DH), lambda h: (0, h))
    return _pcall(
        body, [proj, proj, proj, proj, lb_logits, norm_w], phase, name="hgrn_fwd",
        out_shape=(jax.ShapeDtypeStruct((S, 2 * HW), BF16), jax.ShapeDtypeStruct((S, HW), F32),
                   jax.ShapeDtypeStruct((H, NCH, DH, DH), F32)),
        grid=(H // HG_HP,),
        in_specs=[_proj_col_spec(0), _proj_col_spec(1), _proj_col_spec(2), _proj_col_spec(3),
                  pl.BlockSpec((2, HG_HP * DH), lambda h: (0, h)), pl.BlockSpec((1, DH), lambda h: (0, 0))],
        out_specs=(head_spec, head_spec, pl.BlockSpec((HG_HP, NCH, DH, DH), lambda h: (h, 0, 0, 0))),
        scratch_shapes=[pltpu.VMEM((HG_HP, DH, DH), F32)],
        compiler_params=_params(1),
    )


def _hgrn_bwd(proj, lb_logits, norm_w, oraw, states, dout, phase=None):
    def body(hq_ref, hf_ref, hi_ref, hg_ref, lbl_ref, nw_ref, oraw_ref, st_ref, do_ref,
             dhq_ref, dhf_ref, dhi_ref, dhg_ref, dlb_ref, dnw_ref, ds_ref, acc_ref):
        h = pl.program_id(0)
        nw = nw_ref[...]
        lower = _tri(True)
        ltri = lower.astype(F32)
        utri = _tri(False).astype(F32)
        last = lax.broadcasted_iota(jnp.int32, (CH, DH), 0) == CH - 1
        ds_ref[...] = jnp.zeros_like(ds_ref)
        acc_ref[...] = jnp.zeros_like(acc_ref)

        @pl.when(h == 0)
        def _():
            dnw_ref[...] = jnp.zeros_like(dnw_ref)

        def chunk(i, carry):
            n = NCH - 1 - i
            rows = pl.ds(pl.multiple_of(n * CH, CH), CH)
            for j in range(HG_HP):
                cols = slice(j * DH, (j + 1) * DH)
                lb = 1.0 / (1.0 + jnp.exp(lbl_ref[1:2, cols] - lbl_ref[0:1, cols]))
                hq, hf, v, hg = hq_ref[rows, cols], hf_ref[rows, cols], hi_ref[rows, cols], hg_ref[rows, cols]
                sig, f, g, kk, q, sq = _hgrn_gates(hq, hf, lb)
                gc = _dot_f32(ltri, g)
                gl = jnp.sum(g, axis=0, keepdims=True)
                eg = jnp.exp(gc)
                ek = jnp.exp(gl - gc)
                gam = jnp.exp(gl)
                ph = jnp.exp(gc - 0.5 * gl)
                pk = jnp.exp(0.5 * gl - gc)
                qe, ke = q * eg, kk * ek
                qh, kh = q * ph, kk * pk
                st = st_ref[j, pl.ds(n, 1)][0]
                dst = ds_ref[j]
                a = jnp.where(lower, _dot(qh, kh, NT), 0.0)
                o = oraw_ref[rows, cols]
                rs = lax.rsqrt(jnp.mean(o * o, axis=-1, keepdims=True) + EPS)
                xh = o * rs
                sg = _sigmoid(hg)
                dgo = do_ref[rows, cols]
                d_on = dgo * (hg * sg)
                dhg_ref[rows, cols] = (dgo * xh * nw * (sg * (1.0 + hg * (1.0 - sg)))).astype(BF16)
                acc_ref[j, 1:2, :] += jnp.sum(d_on * xh, axis=0, keepdims=True)
                dy = d_on * nw
                do = rs * (dy - xh * jnp.mean(dy * xh, axis=-1, keepdims=True))
                dqe = _dot(do, st)
                da = jnp.where(lower, _dot(do, v, NT), 0.0)
                dv = _dot(a, do, TN) + _dot(ke, dst, NT)
                dke = _dot(v, dst)
                dgam = jnp.sum(dst * st, axis=0, keepdims=True)
                dqh = _dot(da, kh)
                dkh = _dot(da, qh, TN)
                dq = dqh * ph + dqe * eg
                dk = dkh * pk + dke * ek
                dgl = jnp.sum(dke * ke, axis=0, keepdims=True) + dgam * gam
                dgc = dqh * qh - dkh * kh + dqe * qe - dke * ke + jnp.where(last, dgl, 0.0)
                dg = _dot_f32(utri, dgc)
                df = dg / f - dk
                dhf_ref[rows, cols] = (df * (1.0 - lb) * sig * (1.0 - sig)).astype(BF16)
                acc_ref[j, 0:1, :] += jnp.sum(df * (1.0 - sig), axis=0, keepdims=True)
                dhq_ref[rows, cols] = (dq * (sq * (1.0 + hq * (1.0 - sq)))).astype(BF16)
                dhi_ref[rows, cols] = dv.astype(BF16)
                ds_ref[j] = dst * gam + _dot(do, qe, TN)
            return carry

        lax.fori_loop(0, NCH, chunk, 0, unroll=HG_UNROLL)
        for j in range(HG_HP):
            dlb_ref[:, j * DH:(j + 1) * DH] = acc_ref[j, 0:1, :]
            dnw_ref[...] += acc_ref[j, 1:2, :]

    head_spec = pl.BlockSpec((S, HG_HP * DH), lambda h: (0, h))
    head_out = jax.ShapeDtypeStruct((S, HW), BF16)
    return _pcall(
        body, [proj, proj, proj, proj, lb_logits, norm_w, oraw, states, dout], phase, name="hgrn_bwd",
        out_shape=(head_out, head_out, head_out, head_out,
                   jax.ShapeDtypeStruct((1, HW), F32), jax.ShapeDtypeStruct((1, DH), F32)),
        grid=(H // HG_HP,),
        in_specs=[_proj_col_spec(0), _proj_col_spec(1), _proj_col_spec(2), _proj_col_spec(3),
                  pl.BlockSpec((2, HG_HP * DH), lambda h: (0, h)), pl.BlockSpec((1, DH), lambda h: (0, 0)),
                  head_spec, pl.BlockSpec((HG_HP, NCH, DH, DH), lambda h: (h, 0, 0, 0)), head_spec],
        out_specs=(head_spec, head_spec, head_spec, head_spec,
                   pl.BlockSpec((1, HG_HP * DH), lambda h: (0, h)), pl.BlockSpec((1, DH), lambda h: (0, 0))),
        scratch_shapes=[pltpu.VMEM((HG_HP, DH, DH), F32), pltpu.VMEM((HG_HP, 8, DH), F32)],
        compiler_params=_params(1),
    )


ATT_SCALE = DH ** -0.5
HEADS_PER_STEP = {1: 8, 4: 1, 16: 1}


def _sub_rows(ref, r, dil, cols):
    if dil == 1:
        return ref[:, cols]
    return ref[pl.ds(r, BLK, stride=dil), cols]


def _set_sub_rows(ref, r, dil, cols, val):
    if dil == 1:
        ref[:, cols] = val
    else:
        ref[pl.ds(r, BLK, stride=dil), cols] = val


def _slopes(dil):
    hp = HEADS_PER_STEP[dil]
    s = jnp.asarray([dil * 2.0 ** (-(h + 1)) for h in range(H)], F32)
    return jnp.broadcast_to(s[:, None], (H, DH)).reshape(H // hp, hp, DH)


def _rms_rows(x, w):
    rs = lax.rsqrt(jnp.mean(x * x, axis=-1, keepdims=True) + EPS)
    return x * rs, rs


def _att_masks():
    qi = lax.broadcasted_iota(jnp.int32, (BLK, BLK), 0)
    kj = lax.broadcasted_iota(jnp.int32, (BLK, BLK), 1)
    steps_c = qi - kj
    steps_p = qi - kj + BLK
    return steps_c, steps_p, steps_c >= 0, steps_p <= BLK


def _attn_fwd(proj, q_w, k_w, dil, phase=None):
    hp = HEADS_PER_STEP[dil]
    rows, ng, nb = BLK * dil, H // hp, S // (BLK * dil)

    def body(q_ref, kc_ref, kp_ref, vc_ref, vp_ref, qw_ref, kw_ref, sl_ref, o_ref, lse_ref):
        n = pl.program_id(0)
        steps_c, steps_p, ok_c, ok_p = _att_masks()
        ok_p = jnp.logical_and(ok_p, n > 0)
        fc, fp = steps_c.astype(F32), steps_p.astype(F32)
        qw, kw = qw_ref[...], kw_ref[...]
        for hh in range(hp):
            cols = slice(hh * DH, (hh + 1) * DH)
            sl = sl_ref[0, hh:hh + 1, :]
            for r in range(dil):
                qn = _rms_rows(_sub_rows(q_ref, r, dil, cols), None)[0] * qw
                kcn = _rms_rows(_sub_rows(kc_ref, r, dil, cols), None)[0] * kw
                kpn = _rms_rows(_sub_rows(kp_ref, r, dil, cols), None)[0] * kw
                sc = jnp.where(ok_c, _dot(qn, kcn, NT) * ATT_SCALE - sl * fc, NEG)
                sp = jnp.where(ok_p, _dot(qn, kpn, NT) * ATT_SCALE - sl * fp, NEG)
                m = jnp.maximum(jnp.max(sc, axis=-1, keepdims=True), jnp.max(sp, axis=-1, keepdims=True))
                pc, pp = jnp.exp(sc - m), jnp.exp(sp - m)
                den = jnp.sum(pc, axis=-1, keepdims=True) + jnp.sum(pp, axis=-1, keepdims=True)
                o = (_dot(pc, _sub_rows(vc_ref, r, dil, cols)) + _dot(pp, _sub_rows(vp_ref, r, dil, cols))) / den
                _set_sub_rows(o_ref, r, dil, cols, o)
                _set_sub_rows(lse_ref, r, dil, cols, jnp.broadcast_to(m + jnp.log(den), (BLK, DH)))

    def spec(group, prev):
        if prev:
            return pl.BlockSpec((rows, hp * DH), lambda n, g: (jnp.maximum(n - 1, 0), group * ng + g))
        return pl.BlockSpec((rows, hp * DH), lambda n, g: (n, group * ng + g))

    out = jax.ShapeDtypeStruct((S, HW), F32)
    out_spec = pl.BlockSpec((rows, hp * DH), lambda n, g: (n, g))
    wspec = pl.BlockSpec((1, DH), lambda n, g: (0, 0))
    return _pcall(
        body, [proj, proj, proj, proj, proj, q_w, k_w, _slopes(dil)], phase,
        name=f"attn_fwd_d{dil}", out_shape=(out, out), grid=(nb, ng),
        in_specs=[spec(4, False), spec(5, False), spec(5, True), spec(6, False), spec(6, True), wspec, wspec,
                  pl.BlockSpec((1, hp, DH), lambda n, g: (g, 0, 0))],
        out_specs=(out_spec, out_spec),
        compiler_params=_params(2),
    )


def _attn_merge(outs, lses, cat):
    def body(o1, o2, o3, l1, l2, l3, cat_ref, ob_ref, of_ref, lse_ref):
        a, b, c = l1[...], l2[...], l3[...]
        m = jnp.maximum(jnp.maximum(a, b), c)
        ea, eb, ec = jnp.exp(a - m), jnp.exp(b - m), jnp.exp(c - m)
        den = ea + eb + ec
        o = (ea * o1[...] + eb * o2[...] + ec * o3[...]) / den
        ob_ref[...] = o.astype(BF16)
        of_ref[...] = o
        lse_ref[...] = m + jnp.log(den)

    f = jax.ShapeDtypeStruct((S, HW), F32)
    return pl.pallas_call(
        body, name="attn_merge", out_shape=(jax.ShapeDtypeStruct((S, 2 * HW), BF16), f, f), grid=(S // TR,),
        in_specs=[_row_spec(HW)] * 6 + [pl.BlockSpec(memory_space=pl.ANY)],
        out_specs=(pl.BlockSpec((TR, HW), lambda i: (i, 1)), _row_spec(HW), _row_spec(HW)),
        input_output_aliases={6: 0},
        compiler_params=_params(1),
    )(*outs, *lses, cat)


def _attn_bwd(proj, q_w, k_w, o, lse, do, dil, acc, phase=None):
    hp = HEADS_PER_STEP[dil]
    rows, ng, nb = BLK * dil, H // hp, S // (BLK * dil)
    has_acc = acc is not None

    def body(*refs):
        (q_ref, qn_ref, kp_ref, kc_ref, vp_ref, vc_ref, o_ref, on_ref, l_ref, ln_ref, do_ref, don_ref,
         qw_ref, kw_ref, sl_ref) = refs[:15]
        refs = refs[15:]
        if has_acc:
            aq_ref, ak_ref, av_ref, aqw_ref, akw_ref = refs[:5]
            refs = refs[5:]
        dq_ref, dk_ref, dv_ref, dqw_ref, dkw_ref = refs
        m, g = pl.program_id(0), pl.program_id(1)
        steps_c, steps_p, ok_c, ok_p = _att_masks()
        ok_prev = jnp.logical_and(ok_p, m > 0)
        ok_next = jnp.logical_and(ok_p, m < nb - 1)
        fc, fp = steps_c.astype(F32), steps_p.astype(F32)
        qw, kw = qw_ref[...], kw_ref[...]

        @pl.when(jnp.logical_and(m == 0, g == 0))
        def _():
            if has_acc:
                dqw_ref[...] = aqw_ref[...]
                dkw_ref[...] = akw_ref[...]
            else:
                dqw_ref[...] = jnp.zeros_like(dqw_ref)
                dkw_ref[...] = jnp.zeros_like(dkw_ref)

        for hh in range(hp):
            cols = slice(hh * DH, (hh + 1) * DH)
            sl = sl_ref[0, hh:hh + 1, :]
            for r in range(dil):
                sub = lambda ref: _sub_rows(ref, r, dil, cols)
                qx, qrs = _rms_rows(sub(q_ref), None)
                kx, krs = _rms_rows(sub(kc_ref), None)
                qn, kcn = qx * qw, kx * kw
                qnn = _rms_rows(sub(qn_ref), None)[0] * qw
                kpn = _rms_rows(sub(kp_ref), None)[0] * kw
                vc, vp = sub(vc_ref), sub(vp_ref)
                dov, donv = sub(do_ref), sub(don_ref)
                lse_m, lse_n = sub(l_ref)[:, 0:1], sub(ln_ref)[:, 0:1]
                delta_m = jnp.sum(dov * sub(o_ref), axis=-1, keepdims=True)
                delta_n = jnp.sum(donv * sub(on_ref), axis=-1, keepdims=True)
                p_c = jnp.where(ok_c, jnp.exp(_dot(qn, kcn, NT) * ATT_SCALE - sl * fc - lse_m), 0.0)
                p_p = jnp.where(ok_prev, jnp.exp(_dot(qn, kpn, NT) * ATT_SCALE - sl * fp - lse_m), 0.0)
                p_n = jnp.where(ok_next, jnp.exp(_dot(qnn, kcn, NT) * ATT_SCALE - sl * fp - lse_n), 0.0)
                ds_c = p_c * (_dot(dov, vc, NT) - delta_m)
                ds_p = p_p * (_dot(dov, vp, NT) - delta_m)
                ds_n = p_n * (_dot(donv, vc, NT) - delta_n)
                dqn = (_dot(ds_c, kcn) + _dot(ds_p, kpn)) * ATT_SCALE
                dkn = (_dot(ds_c, qn, TN) + _dot(ds_n, qnn, TN)) * ATT_SCALE
                dv = _dot(p_c, dov, TN) + _dot(p_n, donv, TN)
                dqw_ref[...] += jnp.sum(dqn * qx, axis=0, keepdims=True)
                dkw_ref[...] += jnp.sum(dkn * kx, axis=0, keepdims=True)
                dyq, dyk = dqn * qw, dkn * kw
                daq = qrs * (dyq - qx * jnp.mean(dyq * qx, axis=-1, keepdims=True))
                dak = krs * (dyk - kx * jnp.mean(dyk * kx, axis=-1, keepdims=True))
                if has_acc:
                    daq, dak, dv = daq + sub(aq_ref), dak + sub(ak_ref), dv + sub(av_ref)
                _set_sub_rows(dq_ref, r, dil, cols, daq)
                _set_sub_rows(dk_ref, r, dil, cols, dak)
                _set_sub_rows(dv_ref, r, dil, cols, dv)

    def shifted(shift, m):
        if shift < 0:
            return jnp.maximum(m - 1, 0)
        if shift > 0:
            return jnp.minimum(m + 1, nb - 1)
        return m

    def pspec(group, shift):
        return pl.BlockSpec((rows, hp * DH), lambda m, g: (shifted(shift, m), group * ng + g))

    cur = pl.BlockSpec((rows, hp * DH), lambda m, g: (m, g))
    nxt = pl.BlockSpec((rows, hp * DH), lambda m, g: (shifted(1, m), g))
    wspec = pl.BlockSpec((1, DH), lambda m, g: (0, 0))
    ins = [proj, proj, proj, proj, proj, proj, o, o, lse, lse, do, do, q_w, k_w, _slopes(dil)]
    in_specs = [pspec(4, 0), pspec(4, 1), pspec(5, -1), pspec(5, 0), pspec(6, -1), pspec(6, 0),
                cur, nxt, cur, nxt, pspec(1, 0), pspec(1, 1), wspec, wspec,
                pl.BlockSpec((1, hp, DH), lambda m, g: (g, 0, 0))]
    if has_acc:
        ins += list(acc)
        in_specs += [cur, cur, cur, wspec, wspec]
    big = jax.ShapeDtypeStruct((S, HW), F32)
    small = jax.ShapeDtypeStruct((1, DH), F32)
    return _pcall(
        body, ins, phase, name=f"attn_bwd_d{dil}", out_shape=(big, big, big, small, small), grid=(nb, ng),
        in_specs=in_specs, out_specs=(cur, cur, cur, wspec, wspec),
        compiler_params=_params(2),
    )


TC = 512
NCT = DFF // TC


def _shift_rows(a, k):
    rows = lax.broadcasted_iota(jnp.int32, a.shape, 0)
    rolled = pltpu.roll(a, k % S, 0)
    if k > 0:
        return jnp.where(rows >= k, rolled, 0.0)
    return jnp.where(rows < S + k, rolled, 0.0)


def _conv_pre(a, w_ref, b_ref):
    return b_ref[...] + w_ref[0:1, :] * _shift_rows(a, 2) + w_ref[1:2, :] * _shift_rows(a, 1) + w_ref[2:3, :] * a


def _conv_gate_fwd(u, conv_w, conv_b, phase=None):
    def body(a_ref, g_ref, w_ref, b_ref, y_ref):
        pre = _conv_pre(a_ref[...].astype(F32), w_ref, b_ref)
        y_ref[...] = (pre * _sigmoid(pre) * g_ref[...].astype(F32)).astype(BF16)

    return _pcall(
        body, [u, u, conv_w, conv_b], phase,
        name="conv_gate_fwd", out_shape=jax.ShapeDtypeStruct((S, DFF), BF16), grid=(NCT,),
        in_specs=[pl.BlockSpec((S, TC), lambda i: (0, i)), pl.BlockSpec((S, TC), lambda i: (0, NCT + i)),
                  pl.BlockSpec((3, TC), lambda i: (0, i)), pl.BlockSpec((1, TC), lambda i: (0, i))],
        out_specs=pl.BlockSpec((S, TC), lambda i: (0, i)),
        compiler_params=_params(1),
    )


def _conv_gate_bwd(dy, u, conv_w, conv_b, phase=None):
    def body(dy_ref, a_ref, g_ref, w_ref, b_ref, da_ref, dg_ref, db_ref, dw_ref):
        a = a_ref[...].astype(F32)
        dyv = dy_ref[...].astype(F32)
        pre = _conv_pre(a, w_ref, b_ref)
        sg = _sigmoid(pre)
        dg_ref[...] = (dyv * pre * sg).astype(BF16)
        dpre = dyv * g_ref[...].astype(F32) * (sg * (1.0 + pre * (1.0 - sg)))
        da = w_ref[2:3, :] * dpre + w_ref[1:2, :] * _shift_rows(dpre, -1) + w_ref[0:1, :] * _shift_rows(dpre, -2)
        da_ref[...] = da.astype(BF16)
        db_ref[...] = jnp.sum(dpre, axis=0, keepdims=True)
        dw_ref[0:1, :] = jnp.sum(dpre * _shift_rows(a, 2), axis=0, keepdims=True)
        dw_ref[1:2, :] = jnp.sum(dpre * _shift_rows(a, 1), axis=0, keepdims=True)
        dw_ref[2:3, :] = jnp.sum(dpre * a, axis=0, keepdims=True)

    col = pl.BlockSpec((S, TC), lambda i: (0, i))
    half = jax.ShapeDtypeStruct((S, DFF), BF16)
    return _pcall(
        body, [dy, u, u, conv_w, conv_b], phase, name="conv_gate_bwd",
        out_shape=(half, half, jax.ShapeDtypeStruct((1, DFF), F32), jax.ShapeDtypeStruct((3, DFF), F32)),
        grid=(NCT,),
        in_specs=[col, col, pl.BlockSpec((S, TC), lambda i: (0, NCT + i)),
                  pl.BlockSpec((3, TC), lambda i: (0, i)), pl.BlockSpec((1, TC), lambda i: (0, i))],
        out_specs=(col, col, pl.BlockSpec((1, TC), lambda i: (0, i)), pl.BlockSpec((3, TC), lambda i: (0, i))),
        compiler_params=_params(1),
    )


def _out_loss(z, x1, target, gate):
    def body(z_ref, x_ref, t_ref, g_ref, do_ref, dz_ref, dg_ref, loss_ref):
        i = pl.program_id(0)
        zv = z_ref[...]
        gv = g_ref[...]
        err = x_ref[...] + gv * zv - t_ref[...]
        dout = err * (1.0 / D)
        do_ref[...] = dout
        dz_ref[...] = (dout * gv).astype(BF16)

        @pl.when(i == 0)
        def _():
            dg_ref[...] = jnp.zeros_like(dg_ref)
            loss_ref[...] = jnp.zeros_like(loss_ref)

        dg_ref[...] += jnp.sum(dout * zv, axis=0, keepdims=True)
        part = jnp.sum(jnp.sum(err * err, axis=0, keepdims=True), axis=-1, keepdims=True) * (0.5 / D)
        lane = lax.broadcasted_iota(jnp.int32, (1, 128), 1)
        loss_ref[...] += jnp.where(lane == 0, part, 0.0)

    return pl.pallas_call(
        body, name="out_loss",
        out_shape=(jax.ShapeDtypeStruct((S, D), F32), jax.ShapeDtypeStruct((S, D), BF16),
                   jax.ShapeDtypeStruct((1, D), F32), jax.ShapeDtypeStruct((1, 128), F32)),
        grid=(S // TR,),
        in_specs=[_row_spec(), _row_spec(), _row_spec(), _vec_spec()],
        out_specs=(_row_spec(), _row_spec(), _vec_spec(), _vec_spec(128)),
        compiler_params=_params(1),
    )(z, x1, target, gate)


ADA_COLS = 6 * D // N_CHIPS
TA = 512


def _ada_fwd(c_all, w_shard, b_shard):
    def body(c_ref, w_ref, b_ref, o_ref):
        cv = c_ref[...]
        o_ref[...] = _dot_f32(cv * _sigmoid(cv), w_ref[...]) + b_ref[...]

    return pl.pallas_call(
        body, name="ada_fwd", out_shape=jax.ShapeDtypeStruct((N_DEV, ADA_COLS), F32), grid=(ADA_COLS // TA,),
        in_specs=[pl.BlockSpec((N_DEV, D), lambda j: (0, 0)), pl.BlockSpec((D, TA), lambda j: (0, j)),
                  pl.BlockSpec((1, TA), lambda j: (0, j))],
        out_specs=pl.BlockSpec((N_DEV, TA), lambda j: (0, j)),
        compiler_params=_params(1),
    )(c_all, w_shard, b_shard)


def _ada_bwd(c_all, dmod_shard):
    def body(c_ref, d_ref, o_ref):
        cv = c_ref[...]
        o_ref[...] = lax.dot_general(cv * _sigmoid(cv), d_ref[...], TN, precision=lax.Precision.HIGHEST,
                                     preferred_element_type=F32)

    return pl.pallas_call(
        body, name="ada_bwd", out_shape=jax.ShapeDtypeStruct((D, ADA_COLS), F32), grid=(ADA_COLS // TA,),
        in_specs=[pl.BlockSpec((N_DEV, D), lambda j: (0, 0)), pl.BlockSpec((N_DEV, TA), lambda j: (0, j))],
        out_specs=pl.BlockSpec((D, TA), lambda j: (0, j)),
        compiler_params=_params(1),
    )(c_all, dmod_shard)


def _sum_rows(g, name):
    rows, n = g.shape

    def body(g_ref, o_ref):
        acc = g_ref[0:1, :]
        for i in range(1, rows):
            acc = acc + g_ref[i:i + 1, :]
        o_ref[...] = acc

    return pl.pallas_call(
        body, name=name, out_shape=jax.ShapeDtypeStruct((1, n), F32),
        in_specs=[VMEM_SPEC], out_specs=VMEM_SPEC,
    )(g)


def _lb_grad(lb_logits, dlb):
    def body(l_ref, d_ref, o_ref):
        p = 1.0 / (1.0 + jnp.exp(l_ref[1:2, :] - l_ref[0:1, :]))
        t = d_ref[...] * p * (1.0 - p)
        o_ref[0:1, :] = t
        o_ref[1:2, :] = -t

    return pl.pallas_call(
        body, name="lb_grad", out_shape=jax.ShapeDtypeStruct((2, HW), F32),
        in_specs=[VMEM_SPEC, VMEM_SPEC], out_specs=VMEM_SPEC,
    )(lb_logits, dlb)


def _adamw(w, g, m, v, name):
    rows, cols = w.shape
    tr = rows
    if rows * cols * 4 > ADAM_BLOCK_BYTES:
        tr = _pick(rows, [t for t in (256, 128, 64, 32, 16, 8) if t * cols * 4 <= ADAM_BLOCK_BYTES])

    def body(w_ref, g_ref, m_ref, v_ref, d_ref, mo_ref, vo_ref):
        gv = g_ref[...]
        m2 = ADAM_B1 * m_ref[...] + (1.0 - ADAM_B1) * gv
        v2 = ADAM_B2 * v_ref[...] + (1.0 - ADAM_B2) * (gv * gv)
        m_hat = m2 / (1.0 - ADAM_B1 ** ADAM_STEP)
        v_hat = v2 / (1.0 - ADAM_B2 ** ADAM_STEP)
        d_ref[...] = -ADAM_LR * (m_hat / (jnp.sqrt(v_hat) + ADAM_EPS) + ADAM_WD * w_ref[...])
        mo_ref[...] = m2
        vo_ref[...] = v2

    spec = pl.BlockSpec((tr, cols), lambda i: (i, 0))
    out = jax.ShapeDtypeStruct((rows, cols), F32)
    return pl.pallas_call(
        body, name=name, out_shape=(out, out, out), grid=(rows // tr,),
        in_specs=[spec] * 4, out_specs=(spec,) * 3,
        compiler_params=_params(1),
    )(w, g, m, v)


W_IN, W_OUT, W_UP, W_DOWN = range(4)
IN_CHUNKS = 4
W_INQ = 4


def _join_row_chunks(chunks):
    return jnp.concatenate(chunks, axis=0)


def _sequence_step(x, target, mod, norm1_w, lb_logits, hg_norm_w, q_norm_w, k_norm_w, norm2_w, conv_w, conv_b, net):
    shift1, scale1, gate1, shift2, scale2, gate2 = [mod[:, i * D:(i + 1) * D] for i in range(6)]

    def run(name, fn, *args, **kw):
        phase = net.host(name)
        if phase is None:
            return fn(*args, **kw)
        res, comm = fn(*args, phase=phase, **kw)
        net.done(name, comm)
        return res

    h = _norm_mod(x, norm1_w, scale1, shift1, "norm1_fwd")
    proj = run("proj_fwd", _matmul, h, net.full[W_IN], "nn", F32, "proj_fwd")
    cat, o_raw, states = run("hgrn_fwd", _hgrn_fwd, proj, lb_logits, hg_norm_w)
    att = [run(f"attn_fwd_d{dil}", _attn_fwd, proj, q_norm_w, k_norm_w, dil) for dil in DILS]
    cat, b_out_f32, lse = _attn_merge([t[0] for t in att], [t[1] for t in att], cat)
    mix = run("mix_fwd", _matmul, cat, net.full[W_OUT], "nn", F32, "mix_fwd")
    x1, h2 = _resid_norm_mod(x, mix, gate1, norm2_w, scale2, shift2, "norm2_fwd")
    u = run("up_fwd", _matmul, h2, net.full[W_UP], "nn", BF16, "up_fwd")
    yact = run("conv_gate_fwd", _conv_gate_fwd, u, conv_w, conv_b)
    z = _matmul(yact, net.full[W_DOWN], "nn", F32, "down_fwd")
    dout, dz, dgate2, loss_row = _out_loss(z, x1, target, gate2)

    net.grad[W_DOWN] = _matmul(yact, dz, "tn", BF16, "down_bwd_w")
    dyact = run("down_bwd_x", _matmul, dz, net.full[W_DOWN], "nt", BF16, "down_bwd_x")
    da, dg, dconv_b, dconv_w = run("conv_gate_bwd", _conv_gate_bwd, dyact, u, conv_w, conv_b)
    du = jnp.concatenate([da, dg], axis=1)
    net.grad[W_UP] = run("up_bwd_w", _matmul, h2, du, "tn", BF16, "up_bwd_w")
    dh2 = run("up_bwd_x", _matmul, du, net.full[W_UP], "nt", F32, "up_bwd_x")
    dx1, dshift2, dscale2, dnorm2, dgate1, dmix = run(
        "norm2_bwd", _norm_mod_bwd, dh2, x1, norm2_w, scale2, dout, "norm2_bwd", mix=mix, gate=gate1)
    net.grad[W_OUT] = run("mix_bwd_w", _matmul, cat, dmix, "tn", BF16, "mix_bwd_w")
    dcat = run("mix_bwd_x", _matmul, dmix, net.full[W_OUT], "nt", F32, "mix_bwd_x")
    dhq, dhf, dhi, dhg, dlb, dhg_norm = run("hgrn_bwd", _hgrn_bwd, proj, lb_logits, hg_norm_w, o_raw, states, dcat)
    acc = None
    for dil in DILS:
        acc = run(f"attn_bwd_d{dil}", _attn_bwd, proj, q_norm_w, k_norm_w, b_out_f32, lse, dcat, dil, acc)
    daq, dak, dav, dq_norm, dk_norm = acc
    dproj = jnp.concatenate([dhq, dhf, dhi, dhg, daq.astype(BF16), dak.astype(BF16), dav.astype(BF16)], axis=1)
    for q in range(IN_CHUNKS):
        net.grad[W_INQ + q] = run(f"proj_bwd_w_{q}", _matmul, h, dproj, "tn", BF16, f"proj_bwd_w_{q}",
                                  m_blocks=(D // IN_CHUNKS, [q]))
    dh = run("proj_bwd_x", _matmul, dproj, net.full[W_IN], "nt", F32, "proj_bwd_x")
    grad_x, dshift1, dscale1, dnorm1 = run("norm1_bwd", _norm_mod_bwd, dh, x, norm1_w, scale1, dx1, "norm1_bwd")

    dmod = jnp.concatenate([dshift1, dscale1, dgate1, dshift2, dscale2, dgate2], axis=1)
    small = dict(norm1_w=dnorm1, lb=dlb, hg_norm_w=dhg_norm, q_norm_w=dq_norm, k_norm_w=dk_norm,
                 norm2_w=dnorm2, conv_b=dconv_b, conv_w=dconv_w)
    return loss_row, grad_x, dmod, small


def _place():
    x, y, c = lax.axis_index("x"), lax.axis_index("y"), lax.axis_index("c")
    others = [(1 - x, y), (x, 1 - y), (1 - x, 1 - y)]
    return x, y, c, others


def _remote(src, dst, send_sems, recv_sems, k, to):
    return pltpu.make_async_remote_copy(src_ref=src, dst_ref=dst, send_sem=send_sems.at[k], recv_sem=recv_sems.at[k],
                                        device_id=to, device_id_type=MESH)


class _Phase:
    def __init__(self):
        self.ins, self.outs, self.aliases, self.groups, self.n = [], [], {}, [], 0

    def add(self, ins, outs, aliases, n, copies):
        i0, o0 = len(self.ins), len(self.outs)
        self.ins += list(ins)
        self.outs += list(outs)
        self.aliases.update({i0 + i: o0 + o for i, o in aliases.items()})
        self.groups.append((slice(i0, i0 + len(ins)), slice(o0, o0 + len(outs)), copies))
        self.n += n
        return slice(o0, o0 + len(outs))

    def build(self, cin, cout, send_sems, recv_sems, landing):
        res = []
        for si, so, copies in self.groups:
            for src, dst, land, peer in copies(cin[si], cout[so]):
                k = len(res)
                res.append(_remote(land, land, send_sems, recv_sems, k, peer) if landing
                           else _remote(src, dst, send_sems, recv_sems, k, peer))
        assert len(res) == self.n
        return res


def _pcall(body, args, phase=None, **kw):
    if phase is None or not phase.groups:
        res = pl.pallas_call(body, **kw)(*args)
        return res if phase is None else (res, ())
    grid = tuple(kw.pop("grid", ()))
    out_shape, out_specs = kw.pop("out_shape"), kw.pop("out_specs")
    single = not isinstance(out_shape, (tuple, list))
    out_shape = [out_shape] if single else list(out_shape)
    out_specs = [out_specs] if single else list(out_specs)
    scratch = list(kw.pop("scratch_shapes", ()))
    n_in, n_out, n_ci, n_co = len(args), len(out_shape), len(phase.ins), len(phase.outs)

    def hosted(*refs):
        ins, cin = refs[:n_in], refs[n_in:n_in + n_ci]
        outs = refs[n_in + n_ci:n_in + n_ci + n_out]
        cout = refs[n_in + n_ci + n_out:n_in + n_ci + n_out + n_co]
        scr, send_sems, recv_sems = refs[n_in + n_ci + n_out + n_co:-2], refs[-2], refs[-1]
        ids = [pl.program_id(a) for a in range(len(grid))]

        def start():
            for cp in phase.build(cin, cout, send_sems, recv_sems, False):
                cp.start()

        def finish():
            for cp in phase.build(cin, cout, send_sems, recv_sems, False):
                cp.wait_send()
            for cp in phase.build(cin, cout, send_sems, recv_sems, True):
                cp.wait_recv()

        if grid:
            first = functools.reduce(jnp.logical_and, [i == 0 for i in ids])
            last = functools.reduce(jnp.logical_and, [i == g - 1 for i, g in zip(ids, grid)])
            pl.when(first)(start)
            body(*ins, *outs, *scr)
            pl.when(last)(finish)
        else:
            start()
            body(*ins, *outs, *scr)
            finish()

    res = pl.pallas_call(
        hosted, out_shape=tuple(out_shape + phase.outs), grid=grid,
        in_specs=list(kw.pop("in_specs")) + [HBM_SPEC] * n_ci, out_specs=tuple(out_specs + [HBM_SPEC] * n_co),
        input_output_aliases={n_in + i: n_out + o for i, o in phase.aliases.items()},
        scratch_shapes=scratch + [pltpu.SemaphoreType.DMA((phase.n,)), pltpu.SemaphoreType.DMA((phase.n,))],
        **kw,
    )(*args, *phase.ins)
    outs = res[:n_out]
    return (outs[0] if single else tuple(outs)), tuple(res[n_out:])


def _comm_only(name, phase):
    return _pcall(lambda: None, [], phase, name=name, out_shape=(), in_specs=[], out_specs=())[1]


def _all_gather_rows(block, name):
    m_per, n = block.shape

    def body(x_ref, out_ref, send_sems, recv_sems, local_sem):
        x, y, c, chips = _place()
        me, sibling = (x, y, c), (x, y, 1 - c)

        def rows(px, py, pc):
            return out_ref.at[pl.ds((4 * px + 2 * py + pc) * m_per, m_per), :]

        def copy(k, blk, to, src=None):
            return _remote(rows(*blk) if src is None else src, rows(*blk), send_sems, recv_sems, k, to)

        mine = pltpu.make_async_copy(x_ref, rows(*me), local_sem)
        mine.start()
        first = [copy(0, me, sibling, src=x_ref)]
        first += [copy(1 + j, me, (*chip, c), src=x_ref) for j, chip in enumerate(chips)]
        for cp in first:
            cp.start()
        passed = [copy(4 + j, (*chip, c), sibling) for j, chip in enumerate(chips)]
        for j, chip in enumerate(chips):
            copy(1 + j, (*chip, c), me).wait_recv()
            passed[j].start()
        copy(0, sibling, me).wait_recv()
        for j, chip in enumerate(chips):
            copy(4 + j, (*chip, 1 - c), me).wait_recv()
        for cp in first + passed:
            cp.wait_send()
        mine.wait()

    return pl.pallas_call(
        body, name=name, out_shape=jax.ShapeDtypeStruct((N_DEV * m_per, n), block.dtype),
        in_specs=[VMEM_SPEC], out_specs=VMEM_SPEC,
        scratch_shapes=[pltpu.SemaphoreType.DMA((7,)), pltpu.SemaphoreType.DMA((7,)), pltpu.SemaphoreType.DMA],
    )(block)


class _Geo:
    def __init__(self, kind, shard_shape):
        self.kind = kind
        self.shard_shape = tuple(shard_shape)
        self.hr, self.hc = shard_shape[0] // 2, shard_shape[1]
        if kind == "col":
            self.full_shape = (shard_shape[0], N_CHIPS * shard_shape[1])
        else:
            self.full_shape = (N_CHIPS * shard_shape[0], shard_shape[1])

    def full_shard(self, ref, j):
        if self.kind == "col":
            return ref.at[:, pl.ds(pl.multiple_of(j * self.hc, 128), self.hc)]
        return ref.at[pl.ds(pl.multiple_of(j * 2 * self.hr, 16), 2 * self.hr), :]

    def full_half(self, ref, j, c):
        if self.kind == "col":
            return ref.at[pl.ds(pl.multiple_of(c * self.hr, 16), self.hr),
                          pl.ds(pl.multiple_of(j * self.hc, 128), self.hc)]
        return ref.at[pl.ds(pl.multiple_of((2 * j + c) * self.hr, 16), self.hr), :]

    def piece(self, ref, j, c, p0, p1, pieces):
        pr = self.hr // pieces
        off, n = p0 * pr, (p1 - p0) * pr
        if self.kind == "col":
            return ref.at[pl.ds(pl.multiple_of(c * self.hr + off, 16), n),
                          pl.ds(pl.multiple_of(j * self.hc, 128), self.hc)]
        return ref.at[pl.ds(pl.multiple_of((2 * j + c) * self.hr + off, 16), n), :]


WEIGHT_KINDS = ("col", "row", "col", "row")
NW = len(WEIGHT_KINDS)


def _comm_call(body, name, ins, outs, n_remote, aliases=None):
    return pl.pallas_call(
        body, name=name, out_shape=tuple(outs),
        in_specs=[HBM_SPEC] * len(ins), out_specs=tuple([HBM_SPEC] * len(outs)),
        input_output_aliases=aliases or {},
        scratch_shapes=[pltpu.SemaphoreType.DMA((n_remote,)), pltpu.SemaphoreType.DMA((n_remote,))],
    )(*ins)


ROW_TILES = (256, 176, 128, 64, 32, 16)


def _cast_into_full(shard, geo, place, name):
    rs, cs = shard.shape
    tr = _pick(rs, ROW_TILES)
    nb = rs // tr

    def body(place_ref, s_ref, o_ref):
        o_ref[...] = s_ref[...].astype(BF16)

    if geo.kind == "col":
        out_map = lambda i, place_ref: (i, place_ref[0])
    else:
        out_map = lambda i, place_ref: (place_ref[0] * nb + i, 0)
    return pl.pallas_call(
        body, name=name, out_shape=jax.ShapeDtypeStruct(geo.full_shape, BF16),
        grid_spec=pltpu.PrefetchScalarGridSpec(
            num_scalar_prefetch=1, grid=(nb,),
            in_specs=[pl.BlockSpec((tr, cs), lambda i, place_ref: (i, 0))],
            out_specs=pl.BlockSpec((tr, cs), out_map)),
        compiler_params=_params(1),
    )(place, shard)


def _gather_weights(fulls, geos, name):
    nw = len(fulls)

    def body(*refs):
        full = refs[nw:2 * nw]
        send_sems, recv_sems = refs[2 * nw:]
        x, y, c, chips = _place()
        j = 2 * x + y
        sibling = (x, y, 1 - c)
        first = []
        for w in range(nw):
            mine = geos[w].full_half(full[w], j, c)
            for k, chip in enumerate(chips):
                first.append(_remote(mine, mine, send_sems, recv_sems, 6 * w + k, (*chip, c)))
        for cp in first:
            cp.start()
        passed = []
        for w in range(nw):
            for k, (ox, oy) in enumerate(chips):
                landed = geos[w].full_half(full[w], 2 * ox + oy, c)
                _remote(landed, landed, send_sems, recv_sems, 6 * w + k, (ox, oy, c)).wait_recv()
                fwd = _remote(landed, landed, send_sems, recv_sems, 6 * w + 3 + k, sibling)
                fwd.start()
                passed.append(fwd)
        for w in range(nw):
            for k, (ox, oy) in enumerate(chips):
                other = geos[w].full_half(full[w], 2 * ox + oy, 1 - c)
                _remote(other, other, send_sems, recv_sems, 6 * w + 3 + k, sibling).wait_recv()
        for cp in first + passed:
            cp.wait_send()

    outs = [jax.ShapeDtypeStruct(g.full_shape, BF16) for g in geos]
    return _comm_call(body, name, fulls, outs, 6 * nw, aliases={w: w for w in range(nw)})


def _same(a):
    return jax.ShapeDtypeStruct(a.shape, a.dtype)


def _gather_ici(full, geo, p0, p1, pieces):
    def copies(ins, outs):
        x, y, c, chips = _place()
        mine = geo.piece(outs[0], 2 * x + y, c, p0, p1, pieces)
        return [(mine, mine, geo.piece(outs[0], 2 * ox + oy, c, p0, p1, pieces), (ox, oy, c)) for ox, oy in chips]

    return dict(ins=[full], outs=[_same(full)], aliases={0: 0}, n=3, copies=copies)


def _gather_d2d(full, geo):
    def copies(ins, outs):
        x, y, c, chips = _place()
        return [(geo.full_half(outs[0], 2 * ox + oy, c), geo.full_half(outs[0], 2 * ox + oy, c),
                 geo.full_half(outs[0], 2 * ox + oy, 1 - c), (x, y, 1 - c)) for ox, oy in chips]

    return dict(ins=[full], outs=[_same(full)], aliases={0: 0}, n=3, copies=copies)


def _swap_d2d(grad, geo):
    def copies(ins, outs):
        x, y, c, _ = _place()
        return [(geo.full_half(ins[0], j, 1 - c), outs[0].at[j], outs[0].at[j], (x, y, 1 - c)) for j in range(N_CHIPS)]

    return dict(ins=[grad], outs=[jax.ShapeDtypeStruct((N_CHIPS, geo.hr, geo.hc), BF16)], aliases={}, n=N_CHIPS,
                copies=copies)


def _scatter_ici(pair, recv, geo, p0, p1, pieces):
    pr = geo.hr // pieces
    rows = pl.ds(p0 * pr, (p1 - p0) * pr)

    def copies(ins, outs):
        x, y, c, chips = _place()
        return [(ins[0].at[2 * ox + oy, rows, :], outs[0].at[k, rows, :], outs[0].at[k, rows, :], (ox, oy, c))
                for k, (ox, oy) in enumerate(chips)]

    out = jax.ShapeDtypeStruct((3, geo.hr, geo.hc), BF16)
    if recv is None:
        return dict(ins=[pair], outs=[out], aliases={}, n=3, copies=copies)
    return dict(ins=[pair, recv], outs=[out], aliases={1: 0}, n=3, copies=copies)


def _join_d2d(shard, geo):
    def copies(ins, outs):
        x, y, c, _ = _place()
        mine = outs[0].at[pl.ds(pl.multiple_of(c * geo.hr, 8), geo.hr), :]
        other = outs[0].at[pl.ds(pl.multiple_of((1 - c) * geo.hr, 8), geo.hr), :]
        return [(mine, mine, other, (x, y, 1 - c))]

    return dict(ins=[shard], outs=[_same(shard)], aliases={0: 0}, n=1, copies=copies)


def _add_pair(grad, got, geo, place, name):
    tr = _pick(geo.hr, ROW_TILES)
    nb = geo.hr // tr

    def body(place_ref, a_ref, b_ref, o_ref):
        o_ref[0] = (a_ref[...].astype(F32) + b_ref[0].astype(F32)).astype(BF16)

    if geo.kind == "col":
        own_map = lambda j, i, place_ref: (place_ref[1] * nb + i, j)
    else:
        own_map = lambda j, i, place_ref: ((2 * j + place_ref[1]) * nb + i, 0)
    spec = pl.BlockSpec((1, tr, geo.hc), lambda j, i, place_ref: (j, i, 0))
    return pl.pallas_call(
        body, name=name, out_shape=jax.ShapeDtypeStruct((N_CHIPS, geo.hr, geo.hc), BF16),
        grid_spec=pltpu.PrefetchScalarGridSpec(
            num_scalar_prefetch=1, grid=(N_CHIPS, nb),
            in_specs=[pl.BlockSpec((tr, geo.hc), own_map), spec], out_specs=spec),
        compiler_params=_params(2),
    )(place, grad, got)


def _add_four(pair, recv, geo, place, name):
    tr = _pick(geo.hr, ROW_TILES)
    nb = geo.hr // tr

    def body(place_ref, p_ref, r_ref, o_ref):
        o_ref[...] = ((p_ref[0].astype(F32) + r_ref[0].astype(F32)) + r_ref[1].astype(F32)) + r_ref[2].astype(F32)

    return pl.pallas_call(
        body, name=name, out_shape=jax.ShapeDtypeStruct((2 * geo.hr, geo.hc), F32),
        grid_spec=pltpu.PrefetchScalarGridSpec(
            num_scalar_prefetch=1, grid=(nb,),
            in_specs=[pl.BlockSpec((1, tr, geo.hc), lambda i, place_ref: (place_ref[0], i, 0)),
                      pl.BlockSpec((3, tr, geo.hc), lambda i, place_ref: (0, i, 0))],
            out_specs=pl.BlockSpec((tr, geo.hc), lambda i, place_ref: (place_ref[1] * nb + i, 0))),
        compiler_params=_params(1),
    )(place, pair, recv)


PIECES = (4, 1, 16, 4) + (1,) * IN_CHUNKS
SCHEDULE = {
    "proj_fwd": (("gather_ici", W_OUT, 0, 1), ("gather_ici", W_UP, 0, 3)),
    "hgrn_fwd": (("gather_ici", W_UP, 3, 8), ("gather_d2d", W_OUT)),
    "attn_fwd_d1": (("gather_ici", W_UP, 8, 11),),
    "attn_fwd_d4": (("gather_ici", W_UP, 11, 14),),
    "attn_fwd_d16": (("gather_ici", W_UP, 14, 16),),
    "mix_fwd": (("gather_d2d", W_UP),),
    "up_fwd": (("gather_ici", W_DOWN, 0, 4),),
    "conv_gate_fwd": (("gather_d2d", W_DOWN),),
    "down_bwd_x": (("swap", W_DOWN),),
    "conv_gate_bwd": (("scatter", W_DOWN, 0, 2),),
    "up_bwd_w": (("scatter", W_DOWN, 2, 4),),
    "up_bwd_x": (("swap", W_UP),),
    "norm2_bwd": (("scatter", W_UP, 0, 2),),
    "mix_bwd_w": (("scatter", W_UP, 2, 3),),
    "mix_bwd_x": (("swap", W_OUT), ("scatter", W_UP, 3, 5)),
    "hgrn_bwd": (("scatter", W_UP, 5, 10), ("join", W_DOWN)),
    "attn_bwd_d1": (("scatter", W_UP, 10, 14),),
    "attn_bwd_d4": (("scatter", W_UP, 14, 16), ("scatter", W_OUT, 0, 1)),
    "attn_bwd_d16": (("join", W_UP), ("join", W_OUT)),
    "proj_bwd_w_1": (("swap", W_INQ),),
    "proj_bwd_w_2": (("swap", W_INQ + 1),),
    "proj_bwd_w_3": (("swap", W_INQ + 2), ("scatter", W_INQ, 0, 1)),
    "proj_bwd_x": (("swap", W_INQ + 3), ("scatter", W_INQ + 1, 0, 1), ("scatter", W_INQ + 2, 0, 1)),
    "norm1_bwd": (("scatter", W_INQ + 3, 0, 1), ("join", W_INQ), ("join", W_INQ + 1), ("join", W_INQ + 2)),
    "grad_in_join": (("join", W_INQ + 3),),
}


class _Net:
    def __init__(self, shards, place):
        self.place = place
        self.geos = [_Geo(k, s.shape) for k, s in zip(WEIGHT_KINDS, shards)]
        self.full = [_cast_into_full(s, g, place, f"cast_shard_{w}") for w, (s, g) in enumerate(zip(shards, self.geos))]
        self.full[W_IN] = _gather_weights([self.full[W_IN]], [self.geos[W_IN]], "gather_w_in")[0]
        rows, cols = shards[W_IN].shape
        self.geos += [_Geo("col", (rows // IN_CHUNKS, cols))] * IN_CHUNKS
        n = NW + IN_CHUNKS
        self.grad, self.pair, self.recv, self.shard = [None] * n, [None] * n, [None] * n, [None] * n
        self.slots = []

    def host(self, name):
        phase = _Phase()
        self.slots = []
        for item in SCHEDULE.get(name, ()):
            kind, w = item[0], item[1]
            geo = self.geos[w]
            if kind == "gather_ici":
                spec = _gather_ici(self.full[w], geo, item[2], item[3], PIECES[w])
            elif kind == "gather_d2d":
                spec = _gather_d2d(self.full[w], geo)
            elif kind == "swap":
                spec = _swap_d2d(self.grad[w], geo)
            elif kind == "scatter":
                spec = _scatter_ici(self.pair[w], self.recv[w], geo, item[2], item[3], PIECES[w])
            else:
                spec = _join_d2d(self.shard[w], geo)
            self.slots.append((item, phase.add(**spec)))
        return phase

    def done(self, name, comm):
        for item, sl in self.slots:
            kind, w = item[0], item[1]
            out = comm[sl][0]
            if kind in ("gather_ici", "gather_d2d"):
                self.full[w] = out
            elif kind == "swap":
                self.pair[w] = _add_pair(self.grad[w], out, self.geos[w], self.place, f"grad_pair_sum_{w}")
            elif kind == "scatter":
                self.recv[w] = out
                if item[3] == PIECES[w]:
                    self.shard[w] = _add_four(self.pair[w], out, self.geos[w], self.place, f"grad_chip_sum_{w}")
            else:
                self.shard[w] = out

    def alone(self, name):
        self.done(name, _comm_only(name, self.host(name)))


CONVW_SHARD = 3 * DFF // N_CHIPS
COND_PAD = 8 * 896
SMALL_SIZES = (("norm1_w", D), ("lb", HW), ("hg_norm_w", DH), ("q_norm_w", DH), ("k_norm_w", DH),
               ("norm2_w", D), ("conv_b", DFF), ("conv_w", 3 * DFF), ("loss", 128))
SMALL_TOTAL = sum(n for _, n in SMALL_SIZES)
STATS_PAD = 8 * 5120


def kernel(x, c, w_ada, b_ada, norm1_w, w_in, lb_logits, hg_norm_w, q_norm_w, k_norm_w, w_out, norm2_w, w_up, conv_w, conv_b, w_down, loss_target, m_w_ada, m_b_ada, m_norm1_w, m_w_in, m_lb_logits, m_hg_norm_w, m_q_norm_w, m_k_norm_w, m_w_out, m_norm2_w, m_w_up, m_conv_w, m_conv_b, m_w_down, v_w_ada, v_b_ada, v_norm1_w, v_w_in, v_lb_logits, v_hg_norm_w, v_q_norm_w, v_k_norm_w, v_w_out, v_norm2_w, v_w_up, v_conv_w, v_conv_b, v_w_down):
    chip = 2 * lax.axis_index("x") + lax.axis_index("y")
    dev = 2 * chip + lax.axis_index("c")

    cond = jnp.concatenate([c, conv_w[0].reshape(1, CONVW_SHARD), jnp.zeros((1, COND_PAD - D - CONVW_SHARD), F32)], axis=1)
    cond_all = _all_gather_rows(cond.reshape(8, COND_PAD // 8), "gather_cond").reshape(N_DEV, COND_PAD)
    c_all = cond_all[:, :D]
    conv_w_full = jnp.concatenate(
        [cond_all[2 * j, D:D + CONVW_SHARD].reshape(3, DFF // N_CHIPS) for j in range(N_CHIPS)], axis=1)

    b_shard = lax.dynamic_slice_in_dim(b_ada, chip * ADA_COLS, ADA_COLS, axis=1)
    mod_cols = _all_gather_rows(_ada_fwd(c_all, w_ada[0], b_shard), "gather_mod")
    mod_all = jnp.concatenate([mod_cols[16 * j:16 * j + 8] for j in range(N_CHIPS)], axis=1)
    mod = lax.dynamic_slice_in_dim(mod_all, dev, 1, axis=0)

    place = jnp.stack([chip, lax.axis_index("c")]).astype(jnp.int32)
    net = _Net([w_in[0], w_out[0], w_up[0], w_down[0]], place)
    loss_row, grad_x, dmod, small = _sequence_step(
        x[0], loss_target[0], mod, norm1_w, lb_logits, hg_norm_w, q_norm_w, k_norm_w, norm2_w, conv_w_full, conv_b, net)
    net.alone("grad_in_join")
    g_out, g_up, g_down = net.shard[W_OUT], net.shard[W_UP], net.shard[W_DOWN]
    g_in = _join_row_chunks(net.shard[W_INQ:W_INQ + IN_CHUNKS])

    small["conv_w"] = small["conv_w"].reshape(1, 3 * DFF)
    small["loss"] = loss_row
    stats = jnp.concatenate([dmod] + [small[k] for k, _ in SMALL_SIZES]
                            + [jnp.zeros((1, STATS_PAD - 6 * D - SMALL_TOTAL), F32)], axis=1)
    stats_all = _all_gather_rows(stats.reshape(8, STATS_PAD // 8), "gather_stats").reshape(N_DEV, STATS_PAD)
    dmod_all = stats_all[:, :6 * D]
    sums = _sum_rows(stats_all[:, 6 * D:6 * D + SMALL_TOTAL], "small_grad_sum")
    g_small, off = {}, 0
    for k, n in SMALL_SIZES:
        g_small[k] = sums[:, off:off + n]
        off += n
    loss = g_small["loss"][0, 0]
    g_b_ada = _sum_rows(dmod_all, "b_ada_grad_sum")
    g_w_ada = _ada_bwd(c_all, lax.dynamic_slice_in_dim(dmod_all, chip * ADA_COLS, ADA_COLS, axis=1))
    g_lb = _lb_grad(lb_logits, g_small["lb"])
    g_conv_w = lax.dynamic_slice_in_dim(g_small["conv_w"].reshape(3, DFF), chip * (DFF // N_CHIPS), DFF // N_CHIPS, axis=1)

    names = ["w_ada", "b_ada", "norm1_w", "w_in", "lb_logits", "hg_norm_w", "q_norm_w", "k_norm_w", "w_out",
             "norm2_w", "w_up", "conv_w", "conv_b", "w_down"]
    lead = {"w_ada", "w_in", "w_out", "w_up", "conv_w", "w_down"}
    w = dict(w_ada=w_ada, b_ada=b_ada, norm1_w=norm1_w, w_in=w_in, lb_logits=lb_logits, hg_norm_w=hg_norm_w,
             q_norm_w=q_norm_w, k_norm_w=k_norm_w, w_out=w_out, norm2_w=norm2_w, w_up=w_up, conv_w=conv_w,
             conv_b=conv_b, w_down=w_down)
    m = dict(w_ada=m_w_ada, b_ada=m_b_ada, norm1_w=m_norm1_w, w_in=m_w_in, lb_logits=m_lb_logits,
             hg_norm_w=m_hg_norm_w, q_norm_w=m_q_norm_w, k_norm_w=m_k_norm_w, w_out=m_w_out, norm2_w=m_norm2_w,
             w_up=m_w_up, conv_w=m_conv_w, conv_b=m_conv_b, w_down=m_w_down)
    v = dict(w_ada=v_w_ada, b_ada=v_b_ada, norm1_w=v_norm1_w, w_in=v_w_in, lb_logits=v_lb_logits,
             hg_norm_w=v_hg_norm_w, q_norm_w=v_q_norm_w, k_norm_w=v_k_norm_w, w_out=v_w_out, norm2_w=v_norm2_w,
             w_up=v_w_up, conv_w=v_conv_w, conv_b=v_conv_b, w_down=v_w_down)
    g = dict(w_ada=g_w_ada, b_ada=g_b_ada, norm1_w=g_small["norm1_w"], w_in=g_in, lb_logits=g_lb,
             hg_norm_w=g_small["hg_norm_w"], q_norm_w=g_small["q_norm_w"], k_norm_w=g_small["k_norm_w"], w_out=g_out,
             norm2_w=g_small["norm2_w"], w_up=g_up, conv_w=g_conv_w, conv_b=g_small["conv_b"], w_down=g_down)

    grads, deltas, new_m, new_v = [], [], [], []
    for n in names:
        strip = (lambda t: t[0]) if n in lead else (lambda t: t)
        wrap = (lambda t: t[None]) if n in lead else (lambda t: t)
        d_n, m_n, v_n = _adamw(strip(w[n]), g[n], strip(m[n]), strip(v[n]), f"adamw_{n}")
        grads.append(wrap(g[n]))
        deltas.append(wrap(d_n))
        new_m.append(wrap(m_n))
        new_v.append(wrap(v_n))
    return (loss, grad_x[None], *grads, *deltas, *new_m, *new_v)
```

```python
import functools
import math

import jax
import jax.numpy as jnp
from jax import lax
from jax.experimental import pallas as pl
from jax.experimental.pallas import tpu as pltpu

F32 = jnp.float32
BF16 = jnp.bfloat16

S = 2048
D = 2048
H = 8
DH = 128
HW = H * DH
CH = 64
NCH = S // CH
DFF = 5632
INC = 7 * HW
BLK = 128
DILS = (1, 4, 16)
EPS = 1e-6
NEG = -1e30
N_CHIPS = 4
N_DEV = 8

ADAM_LR = 0.001
ADAM_B1 = 0.9
ADAM_B2 = 0.999
ADAM_EPS = 1e-08
ADAM_WD = 0.01
ADAM_STEP = 10
ADAM_BLOCK_BYTES = 1 << 20

V7X_VMEM_BYTES = 64 * 1024 * 1024
VMEM_LIMIT = (V7X_VMEM_BYTES * 3) // 4
MESH = pl.DeviceIdType.MESH
HBM_SPEC = pl.BlockSpec(memory_space=pltpu.HBM)
VMEM_SPEC = pl.BlockSpec(memory_space=pltpu.VMEM)

NN = (((1,), (0,)), ((), ()))
NT = (((1,), (1,)), ((), ()))
TN = (((0,), (0,)), ((), ()))


def _params(n_grid):
    return pltpu.CompilerParams(dimension_semantics=("arbitrary",) * n_grid, vmem_limit_bytes=VMEM_LIMIT)


def _dot(a, b, dims=NN):
    return lax.dot_general(a.astype(BF16), b.astype(BF16), dims, preferred_element_type=F32)


def _dot_f32(a, b):
    return lax.dot_general(a, b, NN, precision=lax.Precision.HIGHEST, preferred_element_type=F32)


def _sigmoid(x):
    return 1.0 / (1.0 + jnp.exp(-x))


def _pick(n, cands):
    for t in cands:
        if n % t == 0:
            return t
    raise ValueError(n)


def _matmul(a, b, mode, out_dtype, name, phase=None, m_blocks=None):
    if mode == "nn":
        (m, k), (_, n) = a.shape, b.shape
    elif mode == "nt":
        (m, k), (n, _) = a.shape, b.shape
    else:
        (k, m), (_, n) = a.shape, b.shape
    tm = _pick(m, (1024, 512))
    a_block = lambda i: i
    if m_blocks is not None:
        tm, blocks = m_blocks
        m = tm * len(blocks)
        step = blocks[1] - blocks[0] if len(blocks) > 1 else 0
        assert all(blk == blocks[0] + step * i for i, blk in enumerate(blocks))
        a_block = lambda i: blocks[0] + step * i
    tn = _pick(n, (1024, 512))
    tk = _pick(k, (2048, 2816, 1792, 1024, 512))
    nk = k // tk
    dims = {"nn": NN, "nt": NT, "tn": TN}[mode]

    def body(a_ref, b_ref, o_ref, *acc):
        part = lax.dot_general(a_ref[...], b_ref[...], dims, preferred_element_type=F32)
        if nk == 1:
            o_ref[...] = part.astype(o_ref.dtype)
            return
        acc_ref, kk = acc[0], pl.program_id(2)

        @pl.when(kk == 0)
        def _():
            acc_ref[...] = part

        @pl.when(jnp.logical_and(kk > 0, kk < nk - 1))
        def _():
            acc_ref[...] += part

        @pl.when(kk == nk - 1)
        def _():
            o_ref[...] = (acc_ref[...] + part).astype(o_ref.dtype)

    if mode == "tn":
        a_spec = pl.BlockSpec((tk, tm), lambda i, j, kk: (kk, a_block(i)))
    else:
        a_spec = pl.BlockSpec((tm, tk), lambda i, j, kk: (i, kk))
    if mode == "nt":
        b_spec = pl.BlockSpec((tn, tk), lambda i, j, kk: (j, kk))
    else:
        b_spec = pl.BlockSpec((tk, tn), lambda i, j, kk: (kk, j))
    return _pcall(
        body, [a, b], phase, name=name,
        out_shape=jax.ShapeDtypeStruct((m, n), out_dtype),
        grid=(m // tm, n // tn, nk),
        in_specs=[a_spec, b_spec],
        out_specs=pl.BlockSpec((tm, tn), lambda i, j, kk: (i, j)),
        scratch_shapes=[pltpu.VMEM((tm, tn), F32)] if nk > 1 else [],
        compiler_params=_params(3),
    )


TR = 256


def _row_spec(cols=D):
    return pl.BlockSpec((TR, cols), lambda i: (i, 0))


def _vec_spec(cols=D):
    return pl.BlockSpec((1, cols), lambda i: (0, 0))


def _norm_mod(x, nw, scale, shift, name):
    def body(x_ref, nw_ref, sc_ref, sh_ref, h_ref):
        xv = x_ref[...]
        r = lax.rsqrt(jnp.mean(xv * xv, axis=-1, keepdims=True) + EPS)
        h_ref[...] = ((xv * r) * nw_ref[...] * (1.0 + sc_ref[...]) + sh_ref[...]).astype(BF16)

    return pl.pallas_call(
        body, name=name, out_shape=jax.ShapeDtypeStruct((S, D), BF16), grid=(S // TR,),
        in_specs=[_row_spec(), _vec_spec(), _vec_spec(), _vec_spec()], out_specs=_row_spec(),
        compiler_params=_params(1),
    )(x, nw, scale, shift)


def _resid_norm_mod(x, mix, gate, nw, scale, shift, name):
    def body(x_ref, mix_ref, g_ref, nw_ref, sc_ref, sh_ref, x1_ref, h_ref):
        xv = x_ref[...] + g_ref[...] * mix_ref[...]
        x1_ref[...] = xv
        r = lax.rsqrt(jnp.mean(xv * xv, axis=-1, keepdims=True) + EPS)
        h_ref[...] = ((xv * r) * nw_ref[...] * (1.0 + sc_ref[...]) + sh_ref[...]).astype(BF16)

    return pl.pallas_call(
        body, name=name,
        out_shape=(jax.ShapeDtypeStruct((S, D), F32), jax.ShapeDtypeStruct((S, D), BF16)), grid=(S // TR,),
        in_specs=[_row_spec(), _row_spec(), _vec_spec(), _vec_spec(), _vec_spec(), _vec_spec()],
        out_specs=(_row_spec(), _row_spec()),
        compiler_params=_params(1),
    )(x, mix, gate, nw, scale, shift)


def _norm_mod_bwd(dh, xin, nw, scale, dres, name, mix=None, gate=None, phase=None):
    with_gate = mix is not None

    def body(*refs):
        if with_gate:
            dh_ref, x_ref, nw_ref, sc_ref, dres_ref, mix_ref, g_ref, dx_ref, dsh_ref, dsc_ref, dnw_ref, dg_ref, dmix_ref = refs
        else:
            dh_ref, x_ref, nw_ref, sc_ref, dres_ref, dx_ref, dsh_ref, dsc_ref, dnw_ref = refs
        i = pl.program_id(0)
        dhv = dh_ref[...]
        xv = x_ref[...]
        r = lax.rsqrt(jnp.mean(xv * xv, axis=-1, keepdims=True) + EPS)
        xn = xv * r
        nwv = nw_ref[...]
        one_sc = 1.0 + sc_ref[...]
        dxn = dhv * nwv * one_sc
        dx = dres_ref[...] + r * (dxn - xn * jnp.mean(dxn * xn, axis=-1, keepdims=True))
        dx_ref[...] = dx

        @pl.when(i == 0)
        def _():
            dsh_ref[...] = jnp.zeros_like(dsh_ref)
            dsc_ref[...] = jnp.zeros_like(dsc_ref)
            dnw_ref[...] = jnp.zeros_like(dnw_ref)
            if with_gate:
                dg_ref[...] = jnp.zeros_like(dg_ref)

        dsh_ref[...] += jnp.sum(dhv, axis=0, keepdims=True)
        dsc_ref[...] += jnp.sum(dhv * xn * nwv, axis=0, keepdims=True)
        dnw_ref[...] += jnp.sum(dhv * xn * one_sc, axis=0, keepdims=True)
        if with_gate:
            dg_ref[...] += jnp.sum(dx * mix_ref[...], axis=0, keepdims=True)
            dmix_ref[...] = (dx * g_ref[...]).astype(BF16)

    vec = jax.ShapeDtypeStruct((1, D), F32)
    ins = [dh, xin, nw, scale, dres]
    in_specs = [_row_spec(), _row_spec(), _vec_spec(), _vec_spec(), _row_spec()]
    outs = [jax.ShapeDtypeStruct((S, D), F32), vec, vec, vec]
    out_specs = [_row_spec(), _vec_spec(), _vec_spec(), _vec_spec()]
    if with_gate:
        ins += [mix, gate]
        in_specs += [_row_spec(), _vec_spec()]
        outs += [vec, jax.ShapeDtypeStruct((S, D), BF16)]
        out_specs += [_vec_spec(), _row_spec()]
    return _pcall(
        body, ins, phase, name=name, out_shape=tuple(outs), grid=(S // TR,), in_specs=in_specs,
        out_specs=tuple(out_specs), compiler_params=_params(1),
    )


def _tri(lower):
    row = lax.broadcasted_iota(jnp.int32, (CH, CH), 0)
    col = lax.broadcasted_iota(jnp.int32, (CH, CH), 1)
    return (row >= col) if lower else (col >= row)


def _hgrn_gates(hq, hf, lb):
    sig = _sigmoid(hf)
    f = lb + (1.0 - lb) * sig
    g = jnp.log(f)
    sq = _sigmoid(hq)
    return sig, f, g, 1.0 - f, hq * sq, sq


HG_HP = 2
HG_UNROLL = 4


def _proj_col_spec(group):
    return pl.BlockSpec((S, HG_HP * DH), lambda h: (0, group * (H // HG_HP) + h))


def _hgrn_fwd(proj, lb_logits, norm_w, phase=None):
    def body(hq_ref, hf_ref, hi_ref, hg_ref, lbl_ref, nw_ref, out_ref, oraw_ref, st_ref, s_ref):
        nw = nw_ref[...]
        lower = _tri(True)
        ltri = lower.astype(F32)
        s_ref[...] = jnp.zeros_like(s_ref)

        def chunk(n, carry):
            rows = pl.ds(pl.multiple_of(n * CH, CH), CH)
            for j in range(HG_HP):
                cols = slice(j * DH, (j + 1) * DH)
                lb = 1.0 / (1.0 + jnp.exp(lbl_ref[1:2, cols] - lbl_ref[0:1, cols]))
                hq, hf, v, hg = hq_ref[rows, cols], hf_ref[rows, cols], hi_ref[rows, cols], hg_ref[rows, cols]
                _, _, g, kk, q, _ = _hgrn_gates(hq, hf, lb)
                gc = _dot_f32(ltri, g)
                gl = jnp.sum(g, axis=0, keepdims=True)
                qe = q * jnp.exp(gc)
                ke = kk * jnp.exp(gl - gc)
                qh = q * jnp.exp(gc - 0.5 * gl)
                kh = kk * jnp.exp(0.5 * gl - gc)
                st = s_ref[j]
                st_ref[j, pl.ds(n, 1)] = st[None]
                a = jnp.where(lower, _dot(qh, kh, NT), 0.0)
                o = _dot(qe, st, NT) + _dot(a, v)
                s_ref[j] = st * jnp.exp(gl) + _dot(v, ke, TN)
                oraw_ref[rows, cols] = o
                rs = lax.rsqrt(jnp.mean(o * o, axis=-1, keepdims=True) + EPS)
                out_ref[rows, cols] = (o * rs * nw * (hg * _sigmoid(hg))).astype(BF16)
            return carry

        lax.fori_loop(0, NCH, chunk, 0, unroll=HG_UNROLL)

    head_spec = pl.BlockSpec((S, HG_HP * DH), lambda h: (0, h))
    return _pcall(
        body, [proj, proj, proj, proj, lb_logits, norm_w], phase, name="hgrn_fwd",
        out_shape=(jax.ShapeDtypeStruct((S, 2 * HW), BF16), jax.ShapeDtypeStruct((S, HW), F32),
                   jax.ShapeDtypeStruct((H, NCH, DH, DH), F32)),
        grid=(H // HG_HP,),
        in_specs=[_proj_col_spec(0), _proj_col_spec(1), _proj_col_spec(2), _proj_col_spec(3),
                  pl.BlockSpec((2, HG_HP * DH), lambda h: (0, h)), pl.BlockSpec((1, DH), lambda h: (0, 0))],
        out_specs=(head_spec, head_spec, pl.BlockSpec((HG_HP, NCH, DH, DH), lambda h: (h, 0, 0, 0))),
        scratch_shapes=[pltpu.VMEM((HG_HP, DH, DH), F32)],
        compiler_params=_params(1),
    )


def _hgrn_bwd(proj, lb_logits, norm_w, oraw, states, dout, phase=None):
    def body(hq_ref, hf_ref, hi_ref, hg_ref, lbl_ref, nw_ref, oraw_ref, st_ref, do_ref,
             dhq_ref, dhf_ref, dhi_ref, dhg_ref, dlb_ref, dnw_ref, ds_ref, acc_ref):
        h = pl.program_id(0)
        nw = nw_ref[...]
        lower = _tri(True)
        ltri = lower.astype(F32)
        utri = _tri(False).astype(F32)
        last = lax.broadcasted_iota(jnp.int32, (CH, DH), 0) == CH - 1
        ds_ref[...] = jnp.zeros_like(ds_ref)
        acc_ref[...] = jnp.zeros_like(acc_ref)

        @pl.when(h == 0)
        def _():
            dnw_ref[...] = jnp.zeros_like(dnw_ref)

        def chunk(i, carry):
            n = NCH - 1 - i
            rows = pl.ds(pl.multiple_of(n * CH, CH), CH)
            for j in range(HG_HP):
                cols = slice(j * DH, (j + 1) * DH)
                lb = 1.0 / (1.0 + jnp.exp(lbl_ref[1:2, cols] - lbl_ref[0:1, cols]))
                hq, hf, v, hg = hq_ref[rows, cols], hf_ref[rows, cols], hi_ref[rows, cols], hg_ref[rows, cols]
                sig, f, g, kk, q, sq = _hgrn_gates(hq, hf, lb)
                gc = _dot_f32(ltri, g)
                gl = jnp.sum(g, axis=0, keepdims=True)
                eg = jnp.exp(gc)
                ek = jnp.exp(gl - gc)
                gam = jnp.exp(gl)
                ph = jnp.exp(gc - 0.5 * gl)
                pk = jnp.exp(0.5 * gl - gc)
                qe, ke = q * eg, kk * ek
                qh, kh = q * ph, kk * pk
                st = st_ref[j, pl.ds(n, 1)][0]
                dst = ds_ref[j]
                a = jnp.where(lower, _dot(qh, kh, NT), 0.0)
                o = oraw_ref[rows, cols]
                rs = lax.rsqrt(jnp.mean(o * o, axis=-1, keepdims=True) + EPS)
                xh = o * rs
                sg = _sigmoid(hg)
                dgo = do_ref[rows, cols]
                d_on = dgo * (hg * sg)
                dhg_ref[rows, cols] = (dgo * xh * nw * (sg * (1.0 + hg * (1.0 - sg)))).astype(BF16)
                acc_ref[j, 1:2, :] += jnp.sum(d_on * xh, axis=0, keepdims=True)
                dy = d_on * nw
                do = rs * (dy - xh * jnp.mean(dy * xh, axis=-1, keepdims=True))
                dqe = _dot(do, st)
                da = jnp.where(lower, _dot(do, v, NT), 0.0)
                dv = _dot(a, do, TN) + _dot(ke, dst, NT)
                dke = _dot(v, dst)
                dgam = jnp.sum(dst * st, axis=0, keepdims=True)
                dqh = _dot(da, kh)
                dkh = _dot(da, qh, TN)
                dq = dqh * ph + dqe * eg
                dk = dkh * pk + dke * ek
                dgl = jnp.sum(dke * ke, axis=0, keepdims=True) + dgam * gam
                dgc = dqh * qh - dkh * kh + dqe * qe - dke * ke + jnp.where(last, dgl, 0.0)
                dg = _dot_f32(utri, dgc)
                df = dg / f - dk
                dhf_ref[rows, cols] = (df * (1.0 - lb) * sig * (1.0 - sig)).astype(BF16)
                acc_ref[j, 0:1, :] += jnp.sum(df * (1.0 - sig), axis=0, keepdims=True)
                dhq_ref[rows, cols] = (dq * (sq * (1.0 + hq * (1.0 - sq)))).astype(BF16)
                dhi_ref[rows, cols] = dv.astype(BF16)
                ds_ref[j] = dst * gam + _dot(do, qe, TN)
            return carry

        lax.fori_loop(0, NCH, chunk, 0, unroll=HG_UNROLL)
        for j in range(HG_HP):
            dlb_ref[:, j * DH:(j + 1) * DH] = acc_ref[j, 0:1, :]
            dnw_ref[...] += acc_ref[j, 1:2, :]

    head_spec = pl.BlockSpec((S, HG_HP * DH), lambda h: (0, h))
    head_out = jax.ShapeDtypeStruct((S, HW), BF16)
    return _pcall(
        body, [proj, proj, proj, proj, lb_logits, norm_w, oraw, states, dout], phase, name="hgrn_bwd",
        out_shape=(head_out, head_out, head_out, head_out,
                   jax.ShapeDtypeStruct((1, HW), F32), jax.ShapeDtypeStruct((1, DH), F32)),
        grid=(H // HG_HP,),
        in_specs=[_proj_col_spec(0), _proj_col_spec(1), _proj_col_spec(2), _proj_col_spec(3),
                  pl.BlockSpec((2, HG_HP * DH), lambda h: (0, h)), pl.BlockSpec((1, DH), lambda h: (0, 0)),
                  head_spec, pl.BlockSpec((HG_HP, NCH, DH, DH), lambda h: (h, 0, 0, 0)), head_spec],
        out_specs=(head_spec, head_spec, head_spec, head_spec,
                   pl.BlockSpec((1, HG_HP * DH), lambda h: (0, h)), pl.BlockSpec((1, DH), lambda h: (0, 0))),
        scratch_shapes=[pltpu.VMEM((HG_HP, DH, DH), F32), pltpu.VMEM((HG_HP, 8, DH), F32)],
        compiler_params=_params(1),
    )


ATT_SCALE = DH ** -0.5
HEADS_PER_STEP = {1: 8, 4: 1, 16: 1}


def _sub_rows(ref, r, dil, cols):
    if dil == 1:
        return ref[:, cols]
    return ref[pl.ds(r, BLK, stride=dil), cols]


def _set_sub_rows(ref, r, dil, cols, val):
    if dil == 1:
        ref[:, cols] = val
    else:
        ref[pl.ds(r, BLK, stride=dil), cols] = val


def _slopes(dil):
    hp = HEADS_PER_STEP[dil]
    s = jnp.asarray([dil * 2.0 ** (-(h + 1)) for h in range(H)], F32)
    return jnp.broadcast_to(s[:, None], (H, DH)).reshape(H // hp, hp, DH)


def _rms_rows(x, w):
    rs = lax.rsqrt(jnp.mean(x * x, axis=-1, keepdims=True) + EPS)
    return x * rs, rs


def _att_masks():
    qi = lax.broadcasted_iota(jnp.int32, (BLK, BLK), 0)
    kj = lax.broadcasted_iota(jnp.int32, (BLK, BLK), 1)
    steps_c = qi - kj
    steps_p = qi - kj + BLK
    return steps_c, steps_p, steps_c >= 0, steps_p <= BLK


def _attn_fwd(proj, q_w, k_w, dil, phase=None):
    hp = HEADS_PER_STEP[dil]
    rows, ng, nb = BLK * dil, H // hp, S // (BLK * dil)

    def body(q_ref, kc_ref, kp_ref, vc_ref, vp_ref, qw_ref, kw_ref, sl_ref, o_ref, lse_ref):
        n = pl.program_id(0)
        steps_c, steps_p, ok_c, ok_p = _att_masks()
        ok_p = jnp.logical_and(ok_p, n > 0)
        fc, fp = steps_c.astype(F32), steps_p.astype(F32)
        qw, kw = qw_ref[...], kw_ref[...]
        for hh in range(hp):
            cols = slice(hh * DH, (hh + 1) * DH)
            sl = sl_ref[0, hh:hh + 1, :]
            for r in range(dil):
                qn = _rms_rows(_sub_rows(q_ref, r, dil, cols), None)[0] * qw
                kcn = _rms_rows(_sub_rows(kc_ref, r, dil, cols), None)[0] * kw
                kpn = _rms_rows(_sub_rows(kp_ref, r, dil, cols), None)[0] * kw
                sc = jnp.where(ok_c, _dot(qn, kcn, NT) * ATT_SCALE - sl * fc, NEG)
                sp = jnp.where(ok_p, _dot(qn, kpn, NT) * ATT_SCALE - sl * fp, NEG)
                m = jnp.maximum(jnp.max(sc, axis=-1, keepdims=True), jnp.max(sp, axis=-1, keepdims=True))
                pc, pp = jnp.exp(sc - m), jnp.exp(sp - m)
                den = jnp.sum(pc, axis=-1, keepdims=True) + jnp.sum(pp, axis=-1, keepdims=True)
                o = (_dot(pc, _sub_rows(vc_ref, r, dil, cols)) + _dot(pp, _sub_rows(vp_ref, r, dil, cols))) / den
                _set_sub_rows(o_ref, r, dil, cols, o)
                _set_sub_rows(lse_ref, r, dil, cols, jnp.broadcast_to(m + jnp.log(den), (BLK, DH)))

    def spec(group, prev):
        if prev:
            return pl.BlockSpec((rows, hp * DH), lambda n, g: (jnp.maximum(n - 1, 0), group * ng + g))
        return pl.BlockSpec((rows, hp * DH), lambda n, g: (n, group * ng + g))

    out = jax.ShapeDtypeStruct((S, HW), F32)
    out_spec = pl.BlockSpec((rows, hp * DH), lambda n, g: (n, g))
    wspec = pl.BlockSpec((1, DH), lambda n, g: (0, 0))
    return _pcall(
        body, [proj, proj, proj, proj, proj, q_w, k_w, _slopes(dil)], phase,
        name=f"attn_fwd_d{dil}", out_shape=(out, out), grid=(nb, ng),
        in_specs=[spec(4, False), spec(5, False), spec(5, True), spec(6, False), spec(6, True), wspec, wspec,
                  pl.BlockSpec((1, hp, DH), lambda n, g: (g, 0, 0))],
        out_specs=(out_spec, out_spec),
        compiler_params=_params(2),
    )


def _attn_merge(outs, lses, cat):
    def body(o1, o2, o3, l1, l2, l3, cat_ref, ob_ref, of_ref, lse_ref):
        a, b, c = l1[...], l2[...], l3[...]
        m = jnp.maximum(jnp.maximum(a, b), c)
        ea, eb, ec = jnp.exp(a - m), jnp.exp(b - m), jnp.exp(c - m)
        den = ea + eb + ec
        o = (ea * o1[...] + eb * o2[...] + ec * o3[...]) / den
        ob_ref[...] = o.astype(BF16)
        of_ref[...] = o
        lse_ref[...] = m + jnp.log(den)

    f = jax.ShapeDtypeStruct((S, HW), F32)
    return pl.pallas_call(
        body, name="attn_merge", out_shape=(jax.ShapeDtypeStruct((S, 2 * HW), BF16), f, f), grid=(S // TR,),
        in_specs=[_row_spec(HW)] * 6 + [pl.BlockSpec(memory_space=pl.ANY)],
        out_specs=(pl.BlockSpec((TR, HW), lambda i: (i, 1)), _row_spec(HW), _row_spec(HW)),
        input_output_aliases={6: 0},
        compiler_params=_params(1),
    )(*outs, *lses, cat)


def _attn_bwd(proj, q_w, k_w, o, lse, do, dil, acc, phase=None):
    hp = HEADS_PER_STEP[dil]
    rows, ng, nb = BLK * dil, H // hp, S // (BLK * dil)
    has_acc = acc is not None

    def body(*refs):
        (q_ref, qn_ref, kp_ref, kc_ref, vp_ref, vc_ref, o_ref, on_ref, l_ref, ln_ref, do_ref, don_ref,
         qw_ref, kw_ref, sl_ref) = refs[:15]
        refs = refs[15:]
        if has_acc:
            aq_ref, ak_ref, av_ref, aqw_ref, akw_ref = refs[:5]
            refs = refs[5:]
        dq_ref, dk_ref, dv_ref, dqw_ref, dkw_ref = refs
        m, g = pl.program_id(0), pl.program_id(1)
        steps_c, steps_p, ok_c, ok_p = _att_masks()
        ok_prev = jnp.logical_and(ok_p, m > 0)
        ok_next = jnp.logical_and(ok_p, m < nb - 1)
        fc, fp = steps_c.astype(F32), steps_p.astype(F32)
        qw, kw = qw_ref[...], kw_ref[...]

        @pl.when(jnp.logical_and(m == 0, g == 0))
        def _():
            if has_acc:
                dqw_ref[...] = aqw_ref[...]
                dkw_ref[...] = akw_ref[...]
            else:
                dqw_ref[...] = jnp.zeros_like(dqw_ref)
                dkw_ref[...] = jnp.zeros_like(dkw_ref)

        for hh in range(hp):
            cols = slice(hh * DH, (hh + 1) * DH)
            sl = sl_ref[0, hh:hh + 1, :]
            for r in range(dil):
                sub = lambda ref: _sub_rows(ref, r, dil, cols)
                qx, qrs = _rms_rows(sub(q_ref), None)
                kx, krs = _rms_rows(sub(kc_ref), None)
                qn, kcn = qx * qw, kx * kw
                qnn = _rms_rows(sub(qn_ref), None)[0] * qw
                kpn = _rms_rows(sub(kp_ref), None)[0] * kw
                vc, vp = sub(vc_ref), sub(vp_ref)
                dov, donv = sub(do_ref), sub(don_ref)
                lse_m, lse_n = sub(l_ref)[:, 0:1], sub(ln_ref)[:, 0:1]
                delta_m = jnp.sum(dov * sub(o_ref), axis=-1, keepdims=True)
                delta_n = jnp.sum(donv * sub(on_ref), axis=-1, keepdims=True)
                p_c = jnp.where(ok_c, jnp.exp(_dot(qn, kcn, NT) * ATT_SCALE - sl * fc - lse_m), 0.0)
                p_p = jnp.where(ok_prev, jnp.exp(_dot(qn, kpn, NT) * ATT_SCALE - sl * fp - lse_m), 0.0)
                p_n = jnp.where(ok_next, jnp.exp(_dot(qnn, kcn, NT) * ATT_SCALE - sl * fp - lse_n), 0.0)
                ds_c = p_c * (_dot(dov, vc, NT) - delta_m)
                ds_p = p_p * (_dot(dov, vp, NT) - delta_m)
                ds_n = p_n * (_dot(donv, vc, NT) - delta_n)
                dqn = (_dot(ds_c, kcn) + _dot(ds_p, kpn)) * ATT_SCALE
                dkn = (_dot(ds_c, qn, TN) + _dot(ds_n, qnn, TN)) * ATT_SCALE
                dv = _dot(p_c, dov, TN) + _dot(p_n, donv, TN)
                dqw_ref[...] += jnp.sum(dqn * qx, axis=0, keepdims=True)
                dkw_ref[...] += jnp.sum(dkn * kx, axis=0, keepdims=True)
                dyq, dyk = dqn * qw, dkn * kw
                daq = qrs * (dyq - qx * jnp.mean(dyq * qx, axis=-1, keepdims=True))
                dak = krs * (dyk - kx * jnp.mean(dyk * kx, axis=-1, keepdims=True))
                if has_acc:
                    daq, dak, dv = daq + sub(aq_ref), dak + sub(ak_ref), dv + sub(av_ref)
                _set_sub_rows(dq_ref, r, dil, cols, daq)
                _set_sub_rows(dk_ref, r, dil, cols, dak)
                _set_sub_rows(dv_ref, r, dil, cols, dv)

    def shifted(shift, m):
        if shift < 0:
            return jnp.maximum(m - 1, 0)
        if shift > 0:
            return jnp.minimum(m + 1, nb - 1)
        return m

    def pspec(group, shift):
        return pl.BlockSpec((rows, hp * DH), lambda m, g: (shifted(shift, m), group * ng + g))

    cur = pl.BlockSpec((rows, hp * DH), lambda m, g: (m, g))
    nxt = pl.BlockSpec((rows, hp * DH), lambda m, g: (shifted(1, m), g))
    wspec = pl.BlockSpec((1, DH), lambda m, g: (0, 0))
    ins = [proj, proj, proj, proj, proj, proj, o, o, lse, lse, do, do, q_w, k_w, _slopes(dil)]
    in_specs = [pspec(4, 0), pspec(4, 1), pspec(5, -1), pspec(5, 0), pspec(6, -1), pspec(6, 0),
                cur, nxt, cur, nxt, pspec(1, 0), pspec(1, 1), wspec, wspec,
                pl.BlockSpec((1, hp, DH), lambda m, g: (g, 0, 0))]
    if has_acc:
        ins += list(acc)
        in_specs += [cur, cur, cur, wspec, wspec]
    big = jax.ShapeDtypeStruct((S, HW), F32)
    small = jax.ShapeDtypeStruct((1, DH), F32)
    return _pcall(
        body, ins, phase, name=f"attn_bwd_d{dil}", out_shape=(big, big, big, small, small), grid=(nb, ng),
        in_specs=in_specs, out_specs=(cur, cur, cur, wspec, wspec),
        compiler_params=_params(2),
    )


TC = 512
NCT = DFF // TC


def _shift_rows(a, k):
    rows = lax.broadcasted_iota(jnp.int32, a.shape, 0)
    rolled = pltpu.roll(a, k % S, 0)
    if k > 0:
        return jnp.where(rows >= k, rolled, 0.0)
    return jnp.where(rows < S + k, rolled, 0.0)


def _conv_pre(a, w_ref, b_ref):
    return b_ref[...] + w_ref[0:1, :] * _shift_rows(a, 2) + w_ref[1:2, :] * _shift_rows(a, 1) + w_ref[2:3, :] * a


def _conv_gate_fwd(u, conv_w, conv_b, phase=None):
    def body(a_ref, g_ref, w_ref, b_ref, y_ref):
        pre = _conv_pre(a_ref[...].astype(F32), w_ref, b_ref)
        y_ref[...] = (pre * _sigmoid(pre) * g_ref[...].astype(F32)).astype(BF16)

    return _pcall(
        body, [u, u, conv_w, conv_b], phase,
        name="conv_gate_fwd", out_shape=jax.ShapeDtypeStruct((S, DFF), BF16), grid=(NCT,),
        in_specs=[pl.BlockSpec((S, TC), lambda i: (0, i)), pl.BlockSpec((S, TC), lambda i: (0, NCT + i)),
                  pl.BlockSpec((3, TC), lambda i: (0, i)), pl.BlockSpec((1, TC), lambda i: (0, i))],
        out_specs=pl.BlockSpec((S, TC), lambda i: (0, i)),
        compiler_params=_params(1),
    )


def _conv_gate_bwd(dy, u, conv_w, conv_b, phase=None):
    def body(dy_ref, a_ref, g_ref, w_ref, b_ref, da_ref, dg_ref, db_ref, dw_ref):
        a = a_ref[...].astype(F32)
        dyv = dy_ref[...].astype(F32)
        pre = _conv_pre(a, w_ref, b_ref)
        sg = _sigmoid(pre)
        dg_ref[...] = (dyv * pre * sg).astype(BF16)
        dpre = dyv * g_ref[...].astype(F32) * (sg * (1.0 + pre * (1.0 - sg)))
        da = w_ref[2:3, :] * dpre + w_ref[1:2, :] * _shift_rows(dpre, -1) + w_ref[0:1, :] * _shift_rows(dpre, -2)
        da_ref[...] = da.astype(BF16)
        db_ref[...] = jnp.sum(dpre, axis=0, keepdims=True)
        dw_ref[0:1, :] = jnp.sum(dpre * _shift_rows(a, 2), axis=0, keepdims=True)
        dw_ref[1:2, :] = jnp.sum(dpre * _shift_rows(a, 1), axis=0, keepdims=True)
        dw_ref[2:3, :] = jnp.sum(dpre * a, axis=0, keepdims=True)

    col = pl.BlockSpec((S, TC), lambda i: (0, i))
    half = jax.ShapeDtypeStruct((S, DFF), BF16)
    return _pcall(
        body, [dy, u, u, conv_w, conv_b], phase, name="conv_gate_bwd",
        out_shape=(half, half, jax.ShapeDtypeStruct((1, DFF), F32), jax.ShapeDtypeStruct((3, DFF), F32)),
        grid=(NCT,),
        in_specs=[col, col, pl.BlockSpec((S, TC), lambda i: (0, NCT + i)),
                  pl.BlockSpec((3, TC), lambda i: (0, i)), pl.BlockSpec((1, TC), lambda i: (0, i))],
        out_specs=(col, col, pl.BlockSpec((1, TC), lambda i: (0, i)), pl.BlockSpec((3, TC), lambda i: (0, i))),
        compiler_params=_params(1),
    )


def _out_loss(z, x1, target, gate):
    def body(z_ref, x_ref, t_ref, g_ref, do_ref, dz_ref, dg_ref, loss_ref):
        i = pl.program_id(0)
        zv = z_ref[...]
        gv = g_ref[...]
        err = x_ref[...] + gv * zv - t_ref[...]
        dout = err * (1.0 / D)
        do_ref[...] = dout
        dz_ref[...] = (dout * gv).astype(BF16)

        @pl.when(i == 0)
        def _():
            dg_ref[...] = jnp.zeros_like(dg_ref)
            loss_ref[...] = jnp.zeros_like(loss_ref)

        dg_ref[...] += jnp.sum(dout * zv, axis=0, keepdims=True)
        part = jnp.sum(jnp.sum(err * err, axis=0, keepdims=True), axis=-1, keepdims=True) * (0.5 / D)
        lane = lax.broadcasted_iota(jnp.int32, (1, 128), 1)
        loss_ref[...] += jnp.where(lane == 0, part, 0.0)

    return pl.pallas_call(
        body, name="out_loss",
        out_shape=(jax.ShapeDtypeStruct((S, D), F32), jax.ShapeDtypeStruct((S, D), BF16),
                   jax.ShapeDtypeStruct((1, D), F32), jax.ShapeDtypeStruct((1, 128), F32)),
        grid=(S // TR,),
        in_specs=[_row_spec(), _row_spec(), _row_spec(), _vec_spec()],
        out_specs=(_row_spec(), _row_spec(), _vec_spec(), _vec_spec(128)),
        compiler_params=_params(1),
    )(z, x1, target, gate)


ADA_COLS = 6 * D // N_CHIPS
TA = 512


def _ada_fwd(c_all, w_shard, b_shard):
    def body(c_ref, w_ref, b_ref, o_ref):
        cv = c_ref[...]
        o_ref[...] = _dot_f32(cv * _sigmoid(cv), w_ref[...]) + b_ref[...]

    return pl.pallas_call(
        body, name="ada_fwd", out_shape=jax.ShapeDtypeStruct((N_DEV, ADA_COLS), F32), grid=(ADA_COLS // TA,),
        in_specs=[pl.BlockSpec((N_DEV, D), lambda j: (0, 0)), pl.BlockSpec((D, TA), lambda j: (0, j)),
                  pl.BlockSpec((1, TA), lambda j: (0, j))],
        out_specs=pl.BlockSpec((N_DEV, TA), lambda j: (0, j)),
        compiler_params=_params(1),
    )(c_all, w_shard, b_shard)


def _ada_bwd(c_all, dmod_shard):
    def body(c_ref, d_ref, o_ref):
        cv = c_ref[...]
        o_ref[...] = lax.dot_general(cv * _sigmoid(cv), d_ref[...], TN, precision=lax.Precision.HIGHEST,
                                     preferred_element_type=F32)

    return pl.pallas_call(
        body, name="ada_bwd", out_shape=jax.ShapeDtypeStruct((D, ADA_COLS), F32), grid=(ADA_COLS // TA,),
        in_specs=[pl.BlockSpec((N_DEV, D), lambda j: (0, 0)), pl.BlockSpec((N_DEV, TA), lambda j: (0, j))],
        out_specs=pl.BlockSpec((D, TA), lambda j: (0, j)),
        compiler_params=_params(1),
    )(c_all, dmod_shard)


def _sum_rows(g, name):
    rows, n = g.shape

    def body(g_ref, o_ref):
        acc = g_ref[0:1, :]
        for i in range(1, rows):
            acc = acc + g_ref[i:i + 1, :]
        o_ref[...] = acc

    return pl.pallas_call(
        body, name=name, out_shape=jax.ShapeDtypeStruct((1, n), F32),
        in_specs=[VMEM_SPEC], out_specs=VMEM_SPEC,
    )(g)


def _lb_grad(lb_logits, dlb):
    def body(l_ref, d_ref, o_ref):
        p = 1.0 / (1.0 + jnp.exp(l_ref[1:2, :] - l_ref[0:1, :]))
        t = d_ref[...] * p * (1.0 - p)
        o_ref[0:1, :] = t
        o_ref[1:2, :] = -t

    return pl.pallas_call(
        body, name="lb_grad", out_shape=jax.ShapeDtypeStruct((2, HW), F32),
        in_specs=[VMEM_SPEC, VMEM_SPEC], out_specs=VMEM_SPEC,
    )(lb_logits, dlb)


def _adamw(w, g, m, v, name):
    rows, cols = w.shape
    tr = rows
    if rows * cols * 4 > ADAM_BLOCK_BYTES:
        tr = _pick(rows, [t for t in (256, 128, 64, 32, 16, 8) if t * cols * 4 <= ADAM_BLOCK_BYTES])

    def body(w_ref, g_ref, m_ref, v_ref, d_ref, mo_ref, vo_ref):
        gv = g_ref[...]
        m2 = ADAM_B1 * m_ref[...] + (1.0 - ADAM_B1) * gv
        v2 = ADAM_B2 * v_ref[...] + (1.0 - ADAM_B2) * (gv * gv)
        m_hat = m2 / (1.0 - ADAM_B1 ** ADAM_STEP)
        v_hat = v2 / (1.0 - ADAM_B2 ** ADAM_STEP)
        d_ref[...] = -ADAM_LR * (m_hat / (jnp.sqrt(v_hat) + ADAM_EPS) + ADAM_WD * w_ref[...])
        mo_ref[...] = m2
        vo_ref[...] = v2

    spec = pl.BlockSpec((tr, cols), lambda i: (i, 0))
    out = jax.ShapeDtypeStruct((rows, cols), F32)
    return pl.pallas_call(
        body, name=name, out_shape=(out, out, out), grid=(rows // tr,),
        in_specs=[spec] * 4, out_specs=(spec,) * 3,
        compiler_params=_params(1),
    )(w, g, m, v)


W_IN, W_OUT, W_UP, W_DOWN = range(4)
IN_CHUNKS = 4
W_INQ = 4


def _join_row_chunks(chunks):
    return jnp.concatenate(chunks, axis=0)


def _sequence_step(x, target, mod, norm1_w, lb_logits, hg_norm_w, q_norm_w, k_norm_w, norm2_w, conv_w, conv_b, net):
    shift1, scale1, gate1, shift2, scale2, gate2 = [mod[:, i * D:(i + 1) * D] for i in range(6)]

    def run(name, fn, *args, **kw):
        phase = net.host(name)
        if phase is None:
            return fn(*args, **kw)
        res, comm = fn(*args, phase=phase, **kw)
        net.done(name, comm)
        return res

    h = _norm_mod(x, norm1_w, scale1, shift1, "norm1_fwd")
    proj = run("proj_fwd", _matmul, h, net.full[W_IN], "nn", F32, "proj_fwd")
    cat, o_raw, states = run("hgrn_fwd", _hgrn_fwd, proj, lb_logits, hg_norm_w)
    att = [run(f"attn_fwd_d{dil}", _attn_fwd, proj, q_norm_w, k_norm_w, dil) for dil in DILS]
    cat, b_out_f32, lse = _attn_merge([t[0] for t in att], [t[1] for t in att], cat)
    mix = run("mix_fwd", _matmul, cat, net.full[W_OUT], "nn", F32, "mix_fwd")
    x1, h2 = _resid_norm_mod(x, mix, gate1, norm2_w, scale2, shift2, "norm2_fwd")
    u = run("up_fwd", _matmul, h2, net.full[W_UP], "nn", BF16, "up_fwd")
    yact = run("conv_gate_fwd", _conv_gate_fwd, u, conv_w, conv_b)
    net.alone("before_down_fwd")
    z = _matmul(yact, net.full[W_DOWN], "nn", F32, "down_fwd")
    dout, dz, dgate2, loss_row = _out_loss(z, x1, target, gate2)

    net.grad[W_DOWN] = _matmul(yact, dz, "tn", BF16, "down_bwd_w")
    dyact = run("down_bwd_x", _matmul, dz, net.full[W_DOWN], "nt", BF16, "down_bwd_x")
    da, dg, dconv_b, dconv_w = run("conv_gate_bwd", _conv_gate_bwd, dyact, u, conv_w, conv_b)
    du = jnp.concatenate([da, dg], axis=1)
    net.grad[W_UP] = run("up_bwd_w", _matmul, h2, du, "tn", BF16, "up_bwd_w")
    dh2 = run("up_bwd_x", _matmul, du, net.full[W_UP], "nt", F32, "up_bwd_x")
    dx1, dshift2, dscale2, dnorm2, dgate1, dmix = run(
        "norm2_bwd", _norm_mod_bwd, dh2, x1, norm2_w, scale2, dout, "norm2_bwd", mix=mix, gate=gate1)
    net.grad[W_OUT] = run("mix_bwd_w", _matmul, cat, dmix, "tn", BF16, "mix_bwd_w")
    dcat = run("mix_bwd_x", _matmul, dmix, net.full[W_OUT], "nt", F32, "mix_bwd_x")
    dhq, dhf, dhi, dhg, dlb, dhg_norm = run("hgrn_bwd", _hgrn_bwd, proj, lb_logits, hg_norm_w, o_raw, states, dcat)
    acc = None
    for dil in DILS:
        acc = run(f"attn_bwd_d{dil}", _attn_bwd, proj, q_norm_w, k_norm_w, b_out_f32, lse, dcat, dil, acc)
    daq, dak, dav, dq_norm, dk_norm = acc
    dproj = jnp.concatenate([dhq, dhf, dhi, dhg, daq.astype(BF16), dak.astype(BF16), dav.astype(BF16)], axis=1)
    for q in range(IN_CHUNKS):
        net.grad[W_INQ + q] = run(f"proj_bwd_w_{q}", _matmul, h, dproj, "tn", BF16, f"proj_bwd_w_{q}",
                                  m_blocks=(D // IN_CHUNKS, [q]))
    dh = run("proj_bwd_x", _matmul, dproj, net.full[W_IN], "nt", F32, "proj_bwd_x")
    grad_x, dshift1, dscale1, dnorm1 = run("norm1_bwd", _norm_mod_bwd, dh, x, norm1_w, scale1, dx1, "norm1_bwd")

    dmod = jnp.concatenate([dshift1, dscale1, dgate1, dshift2, dscale2, dgate2], axis=1)
    small = dict(norm1_w=dnorm1, lb=dlb, hg_norm_w=dhg_norm, q_norm_w=dq_norm, k_norm_w=dk_norm,
                 norm2_w=dnorm2, conv_b=dconv_b, conv_w=dconv_w)
    return loss_row, grad_x, dmod, small


def _place():
    x, y, c = lax.axis_index("x"), lax.axis_index("y"), lax.axis_index("c")
    others = [(1 - x, y), (x, 1 - y), (1 - x, 1 - y)]
    return x, y, c, others


def _remote(src, dst, send_sems, recv_sems, k, to):
    return pltpu.make_async_remote_copy(src_ref=src, dst_ref=dst, send_sem=send_sems.at[k], recv_sem=recv_sems.at[k],
                                        device_id=to, device_id_type=MESH)


class _Phase:
    def __init__(self):
        self.ins, self.outs, self.aliases, self.groups, self.n = [], [], {}, [], 0

    def add(self, ins, outs, aliases, n, copies):
        i0, o0 = len(self.ins), len(self.outs)
        self.ins += list(ins)
        self.outs += list(outs)
        self.aliases.update({i0 + i: o0 + o for i, o in aliases.items()})
        self.groups.append((slice(i0, i0 + len(ins)), slice(o0, o0 + len(outs)), copies))
        self.n += n
        return slice(o0, o0 + len(outs))

    def build(self, cin, cout, send_sems, recv_sems, landing):
        res = []
        for si, so, copies in self.groups:
            for src, dst, land, peer in copies(cin[si], cout[so]):
                k = len(res)
                res.append(_remote(land, land, send_sems, recv_sems, k, peer) if landing
                           else _remote(src, dst, send_sems, recv_sems, k, peer))
        assert len(res) == self.n
        return res


def _pcall(body, args, phase=None, **kw):
    if phase is None or not phase.groups:
        res = pl.pallas_call(body, **kw)(*args)
        return res if phase is None else (res, ())
    grid = tuple(kw.pop("grid", ()))
    out_shape, out_specs = kw.pop("out_shape"), kw.pop("out_specs")
    single = not isinstance(out_shape, (tuple, list))
    out_shape = [out_shape] if single else list(out_shape)
    out_specs = [out_specs] if single else list(out_specs)
    scratch = list(kw.pop("scratch_shapes", ()))
    n_in, n_out, n_ci, n_co = len(args), len(out_shape), len(phase.ins), len(phase.outs)

    def hosted(*refs):
        ins, cin = refs[:n_in], refs[n_in:n_in + n_ci]
        outs = refs[n_in + n_ci:n_in + n_ci + n_out]
        cout = refs[n_in + n_ci + n_out:n_in + n_ci + n_out + n_co]
        scr, send_sems, recv_sems = refs[n_in + n_ci + n_out + n_co:-2], refs[-2], refs[-1]
        ids = [pl.program_id(a) for a in range(len(grid))]

        def start():
            for cp in phase.build(cin, cout, send_sems, recv_sems, False):
                cp.start()

        def finish():
            for cp in phase.build(cin, cout, send_sems, recv_sems, False):
                cp.wait_send()
            for cp in phase.build(cin, cout, send_sems, recv_sems, True):
                cp.wait_recv()

        if grid:
            first = functools.reduce(jnp.logical_and, [i == 0 for i in ids])
            last = functools.reduce(jnp.logical_and, [i == g - 1 for i, g in zip(ids, grid)])
            pl.when(first)(start)
            body(*ins, *outs, *scr)
            pl.when(last)(finish)
        else:
            start()
            body(*ins, *outs, *scr)
            finish()

    res = pl.pallas_call(
        hosted, out_shape=tuple(out_shape + phase.outs), grid=grid,
        in_specs=list(kw.pop("in_specs")) + [HBM_SPEC] * n_ci, out_specs=tuple(out_specs + [HBM_SPEC] * n_co),
        input_output_aliases={n_in + i: n_out + o for i, o in phase.aliases.items()},
        scratch_shapes=scratch + [pltpu.SemaphoreType.DMA((phase.n,)), pltpu.SemaphoreType.DMA((phase.n,))],
        **kw,
    )(*args, *phase.ins)
    outs = res[:n_out]
    return (outs[0] if single else tuple(outs)), tuple(res[n_out:])


def _comm_only(name, phase):
    return _pcall(lambda: None, [], phase, name=name, out_shape=(), in_specs=[], out_specs=())[1]


def _all_gather_rows(block, name):
    m_per, n = block.shape

    def body(x_ref, out_ref, send_sems, recv_sems, local_sem):
        x, y, c, chips = _place()
        me, sibling = (x, y, c), (x, y, 1 - c)

        def rows(px, py, pc):
            return out_ref.at[pl.ds((4 * px + 2 * py + pc) * m_per, m_per), :]

        def copy(k, blk, to, src=None):
            return _remote(rows(*blk) if src is None else src, rows(*blk), send_sems, recv_sems, k, to)

        mine = pltpu.make_async_copy(x_ref, rows(*me), local_sem)
        mine.start()
        first = [copy(0, me, sibling, src=x_ref)]
        first += [copy(1 + j, me, (*chip, c), src=x_ref) for j, chip in enumerate(chips)]
        for cp in first:
            cp.start()
        passed = [copy(4 + j, (*chip, c), sibling) for j, chip in enumerate(chips)]
        for j, chip in enumerate(chips):
            copy(1 + j, (*chip, c), me).wait_recv()
            passed[j].start()
        copy(0, sibling, me).wait_recv()
        for j, chip in enumerate(chips):
            copy(4 + j, (*chip, 1 - c), me).wait_recv()
        for cp in first + passed:
            cp.wait_send()
        mine.wait()

    return pl.pallas_call(
        body, name=name, out_shape=jax.ShapeDtypeStruct((N_DEV * m_per, n), block.dtype),
        in_specs=[VMEM_SPEC], out_specs=VMEM_SPEC,
        scratch_shapes=[pltpu.SemaphoreType.DMA((7,)), pltpu.SemaphoreType.DMA((7,)), pltpu.SemaphoreType.DMA],
    )(block)


class _Geo:
    def __init__(self, kind, shard_shape):
        self.kind = kind
        self.shard_shape = tuple(shard_shape)
        self.hr, self.hc = shard_shape[0] // 2, shard_shape[1]
        if kind == "col":
            self.full_shape = (shard_shape[0], N_CHIPS * shard_shape[1])
        else:
            self.full_shape = (N_CHIPS * shard_shape[0], shard_shape[1])

    def full_shard(self, ref, j):
        if self.kind == "col":
            return ref.at[:, pl.ds(pl.multiple_of(j * self.hc, 128), self.hc)]
        return ref.at[pl.ds(pl.multiple_of(j * 2 * self.hr, 16), 2 * self.hr), :]

    def full_half(self, ref, j, c):
        if self.kind == "col":
            return ref.at[pl.ds(pl.multiple_of(c * self.hr, 16), self.hr),
                          pl.ds(pl.multiple_of(j * self.hc, 128), self.hc)]
        return ref.at[pl.ds(pl.multiple_of((2 * j + c) * self.hr, 16), self.hr), :]

    def piece(self, ref, j, c, p0, p1, pieces, part=None):
        pr = self.hr // pieces
        off, n = p0 * pr, (p1 - p0) * pr
        if part is not None:
            top = (n // 32) * 16
            off, n = (off, top) if part == 0 else (off + top, n - top)
        if self.kind == "col":
            return ref.at[pl.ds(pl.multiple_of(c * self.hr + off, 16), n),
                          pl.ds(pl.multiple_of(j * self.hc, 128), self.hc)]
        return ref.at[pl.ds(pl.multiple_of((2 * j + c) * self.hr + off, 16), n), :]


WEIGHT_KINDS = ("col", "row", "col", "row")
NW = len(WEIGHT_KINDS)


def _comm_call(body, name, ins, outs, n_remote, aliases=None):
    return pl.pallas_call(
        body, name=name, out_shape=tuple(outs),
        in_specs=[HBM_SPEC] * len(ins), out_specs=tuple([HBM_SPEC] * len(outs)),
        input_output_aliases=aliases or {},
        scratch_shapes=[pltpu.SemaphoreType.DMA((n_remote,)), pltpu.SemaphoreType.DMA((n_remote,))],
    )(*ins)


ROW_TILES = (256, 176, 128, 64, 32, 16)


def _cast_into_full(shard, geo, place, name):
    rs, cs = shard.shape
    tr = _pick(rs, ROW_TILES)
    nb = rs // tr

    def body(place_ref, s_ref, o_ref):
        o_ref[...] = s_ref[...].astype(BF16)

    if geo.kind == "col":
        out_map = lambda i, place_ref: (i, place_ref[0])
    else:
        out_map = lambda i, place_ref: (place_ref[0] * nb + i, 0)
    return pl.pallas_call(
        body, name=name, out_shape=jax.ShapeDtypeStruct(geo.full_shape, BF16),
        grid_spec=pltpu.PrefetchScalarGridSpec(
            num_scalar_prefetch=1, grid=(nb,),
            in_specs=[pl.BlockSpec((tr, cs), lambda i, place_ref: (i, 0))],
            out_specs=pl.BlockSpec((tr, cs), out_map)),
        compiler_params=_params(1),
    )(place, shard)


def _gather_weight(full, geo, pieces, name):
    per = 8

    def body(in_ref, ref, send_sems, recv_sems):
        x, y, c, _ = _place()
        j, jx, jy, jo = 2 * x + y, 2 * (1 - x) + y, 2 * x + (1 - y), 2 * (1 - x) + (1 - y)
        nx, ny, sib = (1 - x, y, c), (x, 1 - y, c), (x, y, 1 - c)

        def cp(region, k, to):
            return _remote(region, region, send_sems, recv_sems, k, to)

        def reg(chip, p, part=None, core=c):
            return geo.piece(ref, chip, core, p, p + 1, pieces, part)

        sent = []
        for p in range(pieces):
            sent += [cp(reg(j, p), per * p, nx), cp(reg(j, p), per * p + 1, ny)]
        for d in sent:
            d.start()
        for p in range(pieces):
            cp(reg(jx, p), per * p, nx).wait_recv()
            new = [cp(reg(jx, p, 0), per * p + 2, ny), cp(reg(jx, p), per * p + 4, sib)]
            cp(reg(jy, p), per * p + 1, ny).wait_recv()
            new += [cp(reg(jy, p, 1), per * p + 3, nx), cp(reg(jy, p), per * p + 5, sib)]
            for d in new:
                d.start()
            sent += new
        for p in range(pieces):
            cp(reg(jo, p, 0), per * p + 2, ny).wait_recv()
            cp(reg(jo, p, 1), per * p + 3, nx).wait_recv()
            new = [cp(reg(jo, p, 0), per * p + 6, sib), cp(reg(jo, p, 1), per * p + 7, sib)]
            for d in new:
                d.start()
            sent += new
        for p in range(pieces):
            cp(reg(jx, p, None, 1 - c), per * p + 4, sib).wait_recv()
            cp(reg(jy, p, None, 1 - c), per * p + 5, sib).wait_recv()
            cp(reg(jo, p, 0, 1 - c), per * p + 6, sib).wait_recv()
            cp(reg(jo, p, 1, 1 - c), per * p + 7, sib).wait_recv()
        for d in sent:
            d.wait_send()

    return _comm_call(body, name, [full], [_same(full)], per * pieces, aliases={0: 0})[0]


def _same(a):
    return jax.ShapeDtypeStruct(a.shape, a.dtype)


def _gather_ici(full, geo, p0, p1, pieces):
    def copies(ins, outs):
        x, y, c, _ = _place()
        mine = geo.piece(outs[0], 2 * x + y, c, p0, p1, pieces)
        return [(mine, mine, geo.piece(outs[0], 2 * ox + oy, c, p0, p1, pieces), (ox, oy, c))
                for ox, oy in ((1 - x, y), (x, 1 - y))]

    return dict(ins=[full], outs=[_same(full)], aliases={0: 0}, n=2, copies=copies)


def _gather_relay(full, geo, p0, p1, pieces):
    def copies(ins, outs):
        x, y, c, _ = _place()
        jx, jy, jo = 2 * (1 - x) + y, 2 * x + (1 - y), 2 * (1 - x) + (1 - y)
        reg = lambda chip, part: geo.piece(outs[0], chip, c, p0, p1, pieces, part)
        return [(reg(jx, 0), reg(jx, 0), reg(jo, 0), (x, 1 - y, c)), (reg(jy, 1), reg(jy, 1), reg(jo, 1), (1 - x, y, c))]

    return dict(ins=[full], outs=[_same(full)], aliases={0: 0}, n=2, copies=copies)


def _gather_d2d(full, geo):
    def copies(ins, outs):
        x, y, c, chips = _place()
        return [(geo.full_half(outs[0], 2 * ox + oy, c), geo.full_half(outs[0], 2 * ox + oy, c),
                 geo.full_half(outs[0], 2 * ox + oy, 1 - c), (x, y, 1 - c)) for ox, oy in chips]

    return dict(ins=[full], outs=[_same(full)], aliases={0: 0}, n=3, copies=copies)


def _swap_d2d(grad, geo):
    def copies(ins, outs):
        x, y, c, _ = _place()
        return [(geo.full_half(ins[0], j, 1 - c), outs[0].at[j], outs[0].at[j], (x, y, 1 - c)) for j in range(N_CHIPS)]

    return dict(ins=[grad], outs=[jax.ShapeDtypeStruct((N_CHIPS, geo.hr, geo.hc), BF16)], aliases={}, n=N_CHIPS,
                copies=copies)


def _scatter_ici(pair, recv, geo, p0, p1, pieces):
    pr = geo.hr // pieces
    rows = pl.ds(p0 * pr, (p1 - p0) * pr)

    def copies(ins, outs):
        x, y, c, chips = _place()
        return [(ins[0].at[2 * ox + oy, rows, :], outs[0].at[k, rows, :], outs[0].at[k, rows, :], (ox, oy, c))
                for k, (ox, oy) in enumerate(chips)]

    out = jax.ShapeDtypeStruct((3, geo.hr, geo.hc), BF16)
    if recv is None:
        return dict(ins=[pair], outs=[out], aliases={}, n=3, copies=copies)
    return dict(ins=[pair, recv], outs=[out], aliases={1: 0}, n=3, copies=copies)


def _join_d2d(shard, geo):
    def copies(ins, outs):
        x, y, c, _ = _place()
        mine = outs[0].at[pl.ds(pl.multiple_of(c * geo.hr, 8), geo.hr), :]
        other = outs[0].at[pl.ds(pl.multiple_of((1 - c) * geo.hr, 8), geo.hr), :]
        return [(mine, mine, other, (x, y, 1 - c))]

    return dict(ins=[shard], outs=[_same(shard)], aliases={0: 0}, n=1, copies=copies)


def _add_pair(grad, got, geo, place, name):
    tr = _pick(geo.hr, ROW_TILES)
    nb = geo.hr // tr

    def body(place_ref, a_ref, b_ref, o_ref):
        o_ref[0] = (a_ref[...].astype(F32) + b_ref[0].astype(F32)).astype(BF16)

    if geo.kind == "col":
        own_map = lambda j, i, place_ref: (place_ref[1] * nb + i, j)
    else:
        own_map = lambda j, i, place_ref: ((2 * j + place_ref[1]) * nb + i, 0)
    spec = pl.BlockSpec((1, tr, geo.hc), lambda j, i, place_ref: (j, i, 0))
    return pl.pallas_call(
        body, name=name, out_shape=jax.ShapeDtypeStruct((N_CHIPS, geo.hr, geo.hc), BF16),
        grid_spec=pltpu.PrefetchScalarGridSpec(
            num_scalar_prefetch=1, grid=(N_CHIPS, nb),
            in_specs=[pl.BlockSpec((tr, geo.hc), own_map), spec], out_specs=spec),
        compiler_params=_params(2),
    )(place, grad, got)


def _add_four(pair, recv, geo, place, name):
    tr = _pick(geo.hr, ROW_TILES)
    nb = geo.hr // tr

    def body(place_ref, p_ref, r_ref, o_ref):
        o_ref[...] = ((p_ref[0].astype(F32) + r_ref[0].astype(F32)) + r_ref[1].astype(F32)) + r_ref[2].astype(F32)

    return pl.pallas_call(
        body, name=name, out_shape=jax.ShapeDtypeStruct((2 * geo.hr, geo.hc), F32),
        grid_spec=pltpu.PrefetchScalarGridSpec(
            num_scalar_prefetch=1, grid=(nb,),
            in_specs=[pl.BlockSpec((1, tr, geo.hc), lambda i, place_ref: (place_ref[0], i, 0)),
                      pl.BlockSpec((3, tr, geo.hc), lambda i, place_ref: (0, i, 0))],
            out_specs=pl.BlockSpec((tr, geo.hc), lambda i, place_ref: (place_ref[1] * nb + i, 0))),
        compiler_params=_params(1),
    )(place, pair, recv)


PIECES = (4, 1, 16, 8) + (1,) * IN_CHUNKS
SCHEDULE = {
    "proj_fwd": (("gather_ici", W_OUT, 0, 1), ("gather_ici", W_UP, 0, 6)),
    "hgrn_fwd": (("gather_relay", W_OUT, 0, 1), ("gather_relay", W_UP, 0, 6), ("gather_ici", W_UP, 6, 10)),
    "attn_fwd_d1": (("gather_ici", W_UP, 10, 12), ("gather_relay", W_UP, 6, 10)),
    "attn_fwd_d4": (("gather_ici", W_UP, 12, 16), ("gather_relay", W_UP, 10, 12), ("gather_d2d", W_OUT)),
    "attn_fwd_d16": (("gather_relay", W_UP, 12, 16), ("gather_ici", W_DOWN, 0, 2)),
    "mix_fwd": (("gather_d2d", W_UP), ("gather_ici", W_DOWN, 2, 4)),
    "up_fwd": (("gather_ici", W_DOWN, 4, 8), ("gather_relay", W_DOWN, 0, 4)),
    "conv_gate_fwd": (("gather_relay", W_DOWN, 4, 8),),
    "before_down_fwd": (("gather_d2d", W_DOWN),),
    "down_bwd_x": (("swap", W_DOWN),),
    "conv_gate_bwd": (("scatter", W_DOWN, 0, 4),),
    "up_bwd_w": (("scatter", W_DOWN, 4, 8),),
    "up_bwd_x": (("swap", W_UP),),
    "norm2_bwd": (("scatter", W_UP, 0, 2),),
    "mix_bwd_w": (("scatter", W_UP, 2, 3),),
    "mix_bwd_x": (("swap", W_OUT), ("scatter", W_UP, 3, 5)),
    "hgrn_bwd": (("scatter", W_UP, 5, 10), ("join", W_DOWN)),
    "attn_bwd_d1": (("scatter", W_UP, 10, 14),),
    "attn_bwd_d4": (("scatter", W_UP, 14, 16), ("scatter", W_OUT, 0, 1)),
    "attn_bwd_d16": (("join", W_UP), ("join", W_OUT)),
    "proj_bwd_w_1": (("swap", W_INQ),),
    "proj_bwd_w_2": (("swap", W_INQ + 1),),
    "proj_bwd_w_3": (("swap", W_INQ + 2), ("scatter", W_INQ, 0, 1)),
    "proj_bwd_x": (("swap", W_INQ + 3), ("scatter", W_INQ + 1, 0, 1), ("scatter", W_INQ + 2, 0, 1)),
    "norm1_bwd": (("scatter", W_INQ + 3, 0, 1), ("join", W_INQ), ("join", W_INQ + 1), ("join", W_INQ + 2)),
    "grad_in_join": (("join", W_INQ + 3),),
}


class _Net:
    def __init__(self, shards, place):
        self.place = place
        self.geos = [_Geo(k, s.shape) for k, s in zip(WEIGHT_KINDS, shards)]
        self.full = [_cast_into_full(s, g, place, f"cast_shard_{w}") for w, (s, g) in enumerate(zip(shards, self.geos))]
        self.full[W_IN] = _gather_weight(self.full[W_IN], self.geos[W_IN], PIECES[W_IN], "gather_w_in")
        rows, cols = shards[W_IN].shape
        self.geos += [_Geo("col", (rows // IN_CHUNKS, cols))] * IN_CHUNKS
        n = NW + IN_CHUNKS
        self.grad, self.pair, self.recv, self.shard = [None] * n, [None] * n, [None] * n, [None] * n
        self.slots = []

    def host(self, name):
        phase = _Phase()
        self.slots = []
        gathers = {}
        for item in SCHEDULE.get(name, ()):
            kind, w = item[0], item[1]
            geo = self.geos[w]
            if kind == "gather_ici":
                spec = _gather_ici(self.full[w], geo, item[2], item[3], PIECES[w])
            elif kind == "gather_relay":
                spec = _gather_relay(self.full[w], geo, item[2], item[3], PIECES[w])
            elif kind == "gather_d2d":
                spec = _gather_d2d(self.full[w], geo)
            elif kind == "swap":
                spec = _swap_d2d(self.grad[w], geo)
            elif kind == "scatter":
                spec = _scatter_ici(self.pair[w], self.recv[w], geo, item[2], item[3], PIECES[w])
            else:
                spec = _join_d2d(self.shard[w], geo)
            if kind.startswith("gather"):
                gathers.setdefault(w, []).append((item, spec))
            else:
                self.slots.append((item, phase.add(**spec)))
        for w, group in gathers.items():
            specs = [s for _, s in group]
            merged = dict(specs[0], n=sum(s["n"] for s in specs),
                          copies=lambda ins, outs, specs=specs: [t for s in specs for t in s["copies"](ins, outs)])
            self.slots.append((group[-1][0], phase.add(**merged)))
        return phase

    def done(self, name, comm):
        for item, sl in self.slots:
            kind, w = item[0], item[1]
            out = comm[sl][0]
            if kind in ("gather_ici", "gather_relay", "gather_d2d"):
                self.full[w] = out
            elif kind == "swap":
                self.pair[w] = _add_pair(self.grad[w], out, self.geos[w], self.place, f"grad_pair_sum_{w}")
            elif kind == "scatter":
                self.recv[w] = out
                if item[3] == PIECES[w]:
                    self.shard[w] = _add_four(self.pair[w], out, self.geos[w], self.place, f"grad_chip_sum_{w}")
            else:
                self.shard[w] = out

    def alone(self, name):
        self.done(name, _comm_only(name, self.host(name)))


CONVW_SHARD = 3 * DFF // N_CHIPS
COND_PAD = 8 * 896
SMALL_SIZES = (("norm1_w", D), ("lb", HW), ("hg_norm_w", DH), ("q_norm_w", DH), ("k_norm_w", DH),
               ("norm2_w", D), ("conv_b", DFF), ("conv_w", 3 * DFF), ("loss", 128))
SMALL_TOTAL = sum(n for _, n in SMALL_SIZES)
STATS_PAD = 8 * 5120


def kernel(x, c, w_ada, b_ada, norm1_w, w_in, lb_logits, hg_norm_w, q_norm_w, k_norm_w, w_out, norm2_w, w_up, conv_w, conv_b, w_down, loss_target, m_w_ada, m_b_ada, m_norm1_w, m_w_in, m_lb_logits, m_hg_norm_w, m_q_norm_w, m_k_norm_w, m_w_out, m_norm2_w, m_w_up, m_conv_w, m_conv_b, m_w_down, v_w_ada, v_b_ada, v_norm1_w, v_w_in, v_lb_logits, v_hg_norm_w, v_q_norm_w, v_k_norm_w, v_w_out, v_norm2_w, v_w_up, v_conv_w, v_conv_b, v_w_down):
    chip = 2 * lax.axis_index("x") + lax.axis_index("y")
    dev = 2 * chip + lax.axis_index("c")

    cond = jnp.concatenate([c, conv_w[0].reshape(1, CONVW_SHARD), jnp.zeros((1, COND_PAD - D - CONVW_SHARD), F32)], axis=1)
    cond_all = _all_gather_rows(cond.reshape(8, COND_PAD // 8), "gather_cond").reshape(N_DEV, COND_PAD)
    c_all = cond_all[:, :D]
    conv_w_full = jnp.concatenate(
        [cond_all[2 * j, D:D + CONVW_SHARD].reshape(3, DFF // N_CHIPS) for j in range(N_CHIPS)], axis=1)

    b_shard = lax.dynamic_slice_in_dim(b_ada, chip * ADA_COLS, ADA_COLS, axis=1)
    mod_cols = _all_gather_rows(_ada_fwd(c_all, w_ada[0], b_shard), "gather_mod")
    mod_all = jnp.concatenate([mod_cols[16 * j:16 * j + 8] for j in range(N_CHIPS)], axis=1)
    mod = lax.dynamic_slice_in_dim(mod_all, dev, 1, axis=0)

    place = jnp.stack([chip, lax.axis_index("c")]).astype(jnp.int32)
    net = _Net([w_in[0], w_out[0], w_up[0], w_down[0]], place)
    loss_row, grad_x, dmod, small = _sequence_step(
        x[0], loss_target[0], mod, norm1_w, lb_logits, hg_norm_w, q_norm_w, k_norm_w, norm2_w, conv_w_full, conv_b, net)
    net.alone("grad_in_join")
    g_out, g_up, g_down = net.shard[W_OUT], net.shard[W_UP], net.shard[W_DOWN]
    g_in = _join_row_chunks(net.shard[W_INQ:W_INQ + IN_CHUNKS])

    small["conv_w"] = small["conv_w"].reshape(1, 3 * DFF)
    small["loss"] = loss_row
    stats = jnp.concatenate([dmod] + [small[k] for k, _ in SMALL_SIZES]
                            + [jnp.zeros((1, STATS_PAD - 6 * D - SMALL_TOTAL), F32)], axis=1)
    stats_all = _all_gather_rows(stats.reshape(8, STATS_PAD // 8), "gather_stats").reshape(N_DEV, STATS_PAD)
    dmod_all = stats_all[:, :6 * D]
    sums = _sum_rows(stats_all[:, 6 * D:6 * D + SMALL_TOTAL], "small_grad_sum")
    g_small, off = {}, 0
    for k, n in SMALL_SIZES:
        g_small[k] = sums[:, off:off + n]
        off += n
    loss = g_small["loss"][0, 0]
    g_b_ada = _sum_rows(dmod_all, "b_ada_grad_sum")
    g_w_ada = _ada_bwd(c_all, lax.dynamic_slice_in_dim(dmod_all, chip * ADA_COLS, ADA_COLS, axis=1))
    g_lb = _lb_grad(lb_logits, g_small["lb"])
    g_conv_w = lax.dynamic_slice_in_dim(g_small["conv_w"].reshape(3, DFF), chip * (DFF // N_CHIPS), DFF // N_CHIPS, axis=1)

    names = ["w_ada", "b_ada", "norm1_w", "w_in", "lb_logits", "hg_norm_w", "q_norm_w", "k_norm_w", "w_out",
             "norm2_w", "w_up", "conv_w", "conv_b", "w_down"]
    lead = {"w_ada", "w_in", "w_out", "w_up", "conv_w", "w_down"}
    w = dict(w_ada=w_ada, b_ada=b_ada, norm1_w=norm1_w, w_in=w_in, lb_logits=lb_logits, hg_norm_w=hg_norm_w,
             q_norm_w=q_norm_w, k_norm_w=k_norm_w, w_out=w_out, norm2_w=norm2_w, w_up=w_up, conv_w=conv_w,
             conv_b=conv_b, w_down=w_down)
    m = dict(w_ada=m_w_ada, b_ada=m_b_ada, norm1_w=m_norm1_w, w_in=m_w_in, lb_logits=m_lb_logits,
             hg_norm_w=m_hg_norm_w, q_norm_w=m_q_norm_w, k_norm_w=m_k_norm_w, w_out=m_w_out, norm2_w=m_norm2_w,
             w_up=m_w_up, conv_w=m_conv_w, conv_b=m_conv_b, w_down=m_w_down)
    v = dict(w_ada=v_w_ada, b_ada=v_b_ada, norm1_w=v_norm1_w, w_in=v_w_in, lb_logits=v_lb_logits,
             hg_norm_w=v_hg_norm_w, q_norm_w=v_q_norm_w, k_norm_w=v_k_norm_w, w_out=v_w_out, norm2_w=v_norm2_w,
             w_up=v_w_up, conv_w=v_conv_w, conv_b=v_conv_b, w_down=v_w_down)
    g = dict(w_ada=g_w_ada, b_ada=g_b_ada, norm1_w=g_small["norm1_w"], w_in=g_in, lb_logits=g_lb,
             hg_norm_w=g_small["hg_norm_w"], q_norm_w=g_small["q_norm_w"], k_norm_w=g_small["k_norm_w"], w_out=g_out,
             norm2_w=g_small["norm2_w"], w_up=g_up, conv_w=g_conv_w, conv_b=g_small["conv_b"], w_down=g_down)

    grads, deltas, new_m, new_v = [], [], [], []
    for n in names:
        strip = (lambda t: t[0]) if n in lead else (lambda t: t)
        wrap = (lambda t: t[None]) if n in lead else (lambda t: t)
        d_n, m_n, v_n = _adamw(strip(w[n]), g[n], strip(m[n]), strip(v[n]), f"adamw_{n}")
        grads.append(wrap(g[n]))
        deltas.append(wrap(d_n))
        new_m.append(wrap(m_n))
        new_v.append(wrap(v_n))
    return (loss, grad_x[None], *grads, *deltas, *new_m, *new_v)
```

```python
import functools
import math

import jax
import jax.numpy as jnp
from jax import lax
from jax.experimental import pallas as pl
from jax.experimental.pallas import tpu as pltpu

F32 = jnp.float32
BF16 = jnp.bfloat16

S = 2048
D = 2048
H = 8
DH = 128
HW = H * DH
CH = 64
NCH = S // CH
DFF = 5632
INC = 7 * HW
BLK = 128
DILS = (1, 4, 16)
EPS = 1e-6
NEG = -1e30
N_CHIPS = 4
N_DEV = 8

ADAM_LR = 0.001
ADAM_B1 = 0.9
ADAM_B2 = 0.999
ADAM_EPS = 1e-08
ADAM_WD = 0.01
ADAM_STEP = 10
ADAM_BLOCK_BYTES = 1 << 20

V7X_VMEM_BYTES = 64 * 1024 * 1024
VMEM_LIMIT = (V7X_VMEM_BYTES * 3) // 4
MESH = pl.DeviceIdType.MESH
HBM_SPEC = pl.BlockSpec(memory_space=pltpu.HBM)
VMEM_SPEC = pl.BlockSpec(memory_space=pltpu.VMEM)

NN = (((1,), (0,)), ((), ()))
NT = (((1,), (1,)), ((), ()))
TN = (((0,), (0,)), ((), ()))


def _params(n_grid):
    return pltpu.CompilerParams(dimension_semantics=("arbitrary",) * n_grid, vmem_limit_bytes=VMEM_LIMIT)


def _dot(a, b, dims=NN):
    return lax.dot_general(a.astype(BF16), b.astype(BF16), dims, preferred_element_type=F32)


def _dot_f32(a, b):
    return lax.dot_general(a, b, NN, precision=lax.Precision.HIGHEST, preferred_element_type=F32)


def _sigmoid(x):
    return 1.0 / (1.0 + jnp.exp(-x))


def _pick(n, cands):
    for t in cands:
        if n % t == 0:
            return t
    raise ValueError(n)


def _matmul(a, b, mode, out_dtype, name, phase=None, m_blocks=None):
    if mode == "nn":
        (m, k), (_, n) = a.shape, b.shape
    elif mode == "nt":
        (m, k), (n, _) = a.shape, b.shape
    else:
        (k, m), (_, n) = a.shape, b.shape
    tm = _pick(m, (1024, 512))
    a_block = lambda i: i
    if m_blocks is not None:
        tm, blocks = m_blocks
        m = tm * len(blocks)
        step = blocks[1] - blocks[0] if len(blocks) > 1 else 0
        assert all(blk == blocks[0] + step * i for i, blk in enumerate(blocks))
        a_block = lambda i: blocks[0] + step * i
    tn = _pick(n, (1024, 512))
    tk = _pick(k, (2048, 2816, 1792, 1024, 512))
    nk = k // tk
    dims = {"nn": NN, "nt": NT, "tn": TN}[mode]

    def body(a_ref, b_ref, o_ref, *acc):
        part = lax.dot_general(a_ref[...], b_ref[...], dims, preferred_element_type=F32)
        if nk == 1:
            o_ref[...] = part.astype(o_ref.dtype)
            return
        acc_ref, kk = acc[0], pl.program_id(2)

        @pl.when(kk == 0)
        def _():
            acc_ref[...] = part

        @pl.when(jnp.logical_and(kk > 0, kk < nk - 1))
        def _():
            acc_ref[...] += part

        @pl.when(kk == nk - 1)
        def _():
            o_ref[...] = (acc_ref[...] + part).astype(o_ref.dtype)

    if mode == "tn":
        a_spec = pl.BlockSpec((tk, tm), lambda i, j, kk: (kk, a_block(i)))
    else:
        a_spec = pl.BlockSpec((tm, tk), lambda i, j, kk: (i, kk))
    if mode == "nt":
        b_spec = pl.BlockSpec((tn, tk), lambda i, j, kk: (j, kk))
    else:
        b_spec = pl.BlockSpec((tk, tn), lambda i, j, kk: (kk, j))
    return _pcall(
        body, [a, b], phase, name=name,
        out_shape=jax.ShapeDtypeStruct((m, n), out_dtype),
        grid=(m // tm, n // tn, nk),
        in_specs=[a_spec, b_spec],
        out_specs=pl.BlockSpec((tm, tn), lambda i, j, kk: (i, j)),
        scratch_shapes=[pltpu.VMEM((tm, tn), F32)] if nk > 1 else [],
        compiler_params=_params(3),
    )


TR = 256


def _row_spec(cols=D):
    return pl.BlockSpec((TR, cols), lambda i: (i, 0))


def _vec_spec(cols=D):
    return pl.BlockSpec((1, cols), lambda i: (0, 0))


def _norm_mod(x, nw, scale, shift, name):
    def body(x_ref, nw_ref, sc_ref, sh_ref, h_ref):
        xv = x_ref[...]
        r = lax.rsqrt(jnp.mean(xv * xv, axis=-1, keepdims=True) + EPS)
        h_ref[...] = ((xv * r) * nw_ref[...] * (1.0 + sc_ref[...]) + sh_ref[...]).astype(BF16)

    return pl.pallas_call(
        body, name=name, out_shape=jax.ShapeDtypeStruct((S, D), BF16), grid=(S // TR,),
        in_specs=[_row_spec(), _vec_spec(), _vec_spec(), _vec_spec()], out_specs=_row_spec(),
        compiler_params=_params(1),
    )(x, nw, scale, shift)


def _resid_norm_mod(x, mix, gate, nw, scale, shift, name):
    def body(x_ref, mix_ref, g_ref, nw_ref, sc_ref, sh_ref, x1_ref, h_ref):
        xv = x_ref[...] + g_ref[...] * mix_ref[...]
        x1_ref[...] = xv
        r = lax.rsqrt(jnp.mean(xv * xv, axis=-1, keepdims=True) + EPS)
        h_ref[...] = ((xv * r) * nw_ref[...] * (1.0 + sc_ref[...]) + sh_ref[...]).astype(BF16)

    return pl.pallas_call(
        body, name=name,
        out_shape=(jax.ShapeDtypeStruct((S, D), F32), jax.ShapeDtypeStruct((S, D), BF16)), grid=(S // TR,),
        in_specs=[_row_spec(), _row_spec(), _vec_spec(), _vec_spec(), _vec_spec(), _vec_spec()],
        out_specs=(_row_spec(), _row_spec()),
        compiler_params=_params(1),
    )(x, mix, gate, nw, scale, shift)


def _norm_mod_bwd(dh, xin, nw, scale, dres, name, mix=None, gate=None, phase=None):
    with_gate = mix is not None

    def body(*refs):
        if with_gate:
            dh_ref, x_ref, nw_ref, sc_ref, dres_ref, mix_ref, g_ref, dx_ref, dsh_ref, dsc_ref, dnw_ref, dg_ref, dmix_ref = refs
        else:
            dh_ref, x_ref, nw_ref, sc_ref, dres_ref, dx_ref, dsh_ref, dsc_ref, dnw_ref = refs
        i = pl.program_id(0)
        dhv = dh_ref[...]
        xv = x_ref[...]
        r = lax.rsqrt(jnp.mean(xv * xv, axis=-1, keepdims=True) + EPS)
        xn = xv * r
        nwv = nw_ref[...]
        one_sc = 1.0 + sc_ref[...]
        dxn = dhv * nwv * one_sc
        dx = dres_ref[...] + r * (dxn - xn * jnp.mean(dxn * xn, axis=-1, keepdims=True))
        dx_ref[...] = dx

        @pl.when(i == 0)
        def _():
            dsh_ref[...] = jnp.zeros_like(dsh_ref)
            dsc_ref[...] = jnp.zeros_like(dsc_ref)
            dnw_ref[...] = jnp.zeros_like(dnw_ref)
            if with_gate:
                dg_ref[...] = jnp.zeros_like(dg_ref)

        dsh_ref[...] += jnp.sum(dhv, axis=0, keepdims=True)
        dsc_ref[...] += jnp.sum(dhv * xn * nwv, axis=0, keepdims=True)
        dnw_ref[...] += jnp.sum(dhv * xn * one_sc, axis=0, keepdims=True)
        if with_gate:
            dg_ref[...] += jnp.sum(dx * mix_ref[...], axis=0, keepdims=True)
            dmix_ref[...] = (dx * g_ref[...]).astype(BF16)

    vec = jax.ShapeDtypeStruct((1, D), F32)
    ins = [dh, xin, nw, scale, dres]
    in_specs = [_row_spec(), _row_spec(), _vec_spec(), _vec_spec(), _row_spec()]
    outs = [jax.ShapeDtypeStruct((S, D), F32), vec, vec, vec]
    out_specs = [_row_spec(), _vec_spec(), _vec_spec(), _vec_spec()]
    if with_gate:
        ins += [mix, gate]
        in_specs += [_row_spec(), _vec_spec()]
        outs += [vec, jax.ShapeDtypeStruct((S, D), BF16)]
        out_specs += [_vec_spec(), _row_spec()]
    return _pcall(
        body, ins, phase, name=name, out_shape=tuple(outs), grid=(S // TR,), in_specs=in_specs,
        out_specs=tuple(out_specs), compiler_params=_params(1),
    )


def _tri(lower):
    row = lax.broadcasted_iota(jnp.int32, (CH, CH), 0)
    col = lax.broadcasted_iota(jnp.int32, (CH, CH), 1)
    return (row >= col) if lower else (col >= row)


def _hgrn_gates(hq, hf, lb):
    sig = _sigmoid(hf)
    f = lb + (1.0 - lb) * sig
    g = jnp.log(f)
    sq = _sigmoid(hq)
    return sig, f, g, 1.0 - f, hq * sq, sq


HG_HP = 2
HG_UNROLL = 4


def _proj_col_spec(group):
    return pl.BlockSpec((S, HG_HP * DH), lambda h: (0, group * (H // HG_HP) + h))


def _hgrn_fwd(proj, lb_logits, norm_w, phase=None):
    def body(hq_ref, hf_ref, hi_ref, hg_ref, lbl_ref, nw_ref, out_ref, oraw_ref, st_ref, s_ref):
        nw = nw_ref[...]
        lower = _tri(True)
        ltri = lower.astype(F32)
        s_ref[...] = jnp.zeros_like(s_ref)

        def chunk(n, carry):
            rows = pl.ds(pl.multiple_of(n * CH, CH), CH)
            for j in range(HG_HP):
                cols = slice(j * DH, (j + 1) * DH)
                lb = 1.0 / (1.0 + jnp.exp(lbl_ref[1:2, cols] - lbl_ref[0:1, cols]))
                hq, hf, v, hg = hq_ref[rows, cols], hf_ref[rows, cols], hi_ref[rows, cols], hg_ref[rows, cols]
                _, _, g, kk, q, _ = _hgrn_gates(hq, hf, lb)
                gc = _dot_f32(ltri, g)
                gl = jnp.sum(g, axis=0, keepdims=True)
                qe = q * jnp.exp(gc)
                ke = kk * jnp.exp(gl - gc)
                qh = q * jnp.exp(gc - 0.5 * gl)
                kh = kk * jnp.exp(0.5 * gl - gc)
                st = s_ref[j]
                st_ref[j, pl.ds(n, 1)] = st[None]
                a = jnp.where(lower, _dot(qh, kh, NT), 0.0)
                o = _dot(qe, st, NT) + _dot(a, v)
                s_ref[j] = st * jnp.exp(gl) + _dot(v, ke, TN)
                oraw_ref[rows, cols] = o
                rs = lax.rsqrt(jnp.mean(o * o, axis=-1, keepdims=True) + EPS)
                out_ref[rows, cols] = (o * rs * nw * (hg * _sigmoid(hg))).astype(BF16)
            return carry

        lax.fori_loop(0, NCH, chunk, 0, unroll=HG_UNROLL)

    head_spec = pl.BlockSpec((S, HG_HP * DH), lambda h: (0, h))
    return _pcall(
        body, [proj, proj, proj, proj, lb_logits, norm_w], phase, name="hgrn_fwd",
        out_shape=(jax.ShapeDtypeStruct((S, 2 * HW), BF16), jax.ShapeDtypeStruct((S, HW), F32),
                   jax.ShapeDtypeStruct((H, NCH, DH, DH), F32)),
        grid=(H // HG_HP,),
        in_specs=[_proj_col_spec(0), _proj_col_spec(1), _proj_col_spec(2), _proj_col_spec(3),
                  pl.BlockSpec((2, HG_HP * DH), lambda h: (0, h)), pl.BlockSpec((1, DH), lambda h: (0, 0))],
        out_specs=(head_spec, head_spec, pl.BlockSpec((HG_HP, NCH, DH, DH), lambda h: (h, 0, 0, 0))),
        scratch_shapes=[pltpu.VMEM((HG_HP, DH, DH), F32)],
        compiler_params=_params(1),
    )


def _hgrn_bwd(proj, lb_logits, norm_w, oraw, states, dout, phase=None):
    def body(hq_ref, hf_ref, hi_ref, hg_ref, lbl_ref, nw_ref, oraw_ref, st_ref, do_ref,
             dhq_ref, dhf_ref, dhi_ref, dhg_ref, dlb_ref, dnw_ref, ds_ref, acc_ref):
        h = pl.program_id(0)
        nw = nw_ref[...]
        lower = _tri(True)
        ltri = lower.astype(F32)
        utri = _tri(False).astype(F32)
        last = lax.broadcasted_iota(jnp.int32, (CH, DH), 0) == CH - 1
        ds_ref[...] = jnp.zeros_like(ds_ref)
        acc_ref[...] = jnp.zeros_like(acc_ref)

        @pl.when(h == 0)
        def _():
            dnw_ref[...] = jnp.zeros_like(dnw_ref)

        def chunk(i, carry):
            n = NCH - 1 - i
            rows = pl.ds(pl.multiple_of(n * CH, CH), CH)
            for j in range(HG_HP):
                cols = slice(j * DH, (j + 1) * DH)
                lb = 1.0 / (1.0 + jnp.exp(lbl_ref[1:2, cols] - lbl_ref[0:1, cols]))
                hq, hf, v, hg = hq_ref[rows, cols], hf_ref[rows, cols], hi_ref[rows, cols], hg_ref[rows, cols]
                sig, f, g, kk, q, sq = _hgrn_gates(hq, hf, lb)
                gc = _dot_f32(ltri, g)
                gl = jnp.sum(g, axis=0, keepdims=True)
                eg = jnp.exp(gc)
                ek = jnp.exp(gl - gc)
                gam = jnp.exp(gl)
                ph = jnp.exp(gc - 0.5 * gl)
                pk = jnp.exp(0.5 * gl - gc)
                qe, ke = q * eg, kk * ek
                qh, kh = q * ph, kk * pk
                st = st_ref[j, pl.ds(n, 1)][0]
                dst = ds_ref[j]
                a = jnp.where(lower, _dot(qh, kh, NT), 0.0)
                o = oraw_ref[rows, cols]
                rs = lax.rsqrt(jnp.mean(o * o, axis=-1, keepdims=True) + EPS)
                xh = o * rs
                sg = _sigmoid(hg)
                dgo = do_ref[rows, cols]
                d_on = dgo * (hg * sg)
                dhg_ref[rows, cols] = (dgo * xh * nw * (sg * (1.0 + hg * (1.0 - sg)))).astype(BF16)
                acc_ref[j, 1:2, :] += jnp.sum(d_on * xh, axis=0, keepdims=True)
                dy = d_on * nw
                do = rs * (dy - xh * jnp.mean(dy * xh, axis=-1, keepdims=True))
                dqe = _dot(do, st)
                da = jnp.where(lower, _dot(do, v, NT), 0.0)
                dv = _dot(a, do, TN) + _dot(ke, dst, NT)
                dke = _dot(v, dst)
                dgam = jnp.sum(dst * st, axis=0, keepdims=True)
                dqh = _dot(da, kh)
                dkh = _dot(da, qh, TN)
                dq = dqh * ph + dqe * eg
                dk = dkh * pk + dke * ek
                dgl = jnp.sum(dke * ke, axis=0, keepdims=True) + dgam * gam
                dgc = dqh * qh - dkh * kh + dqe * qe - dke * ke + jnp.where(last, dgl, 0.0)
                dg = _dot_f32(utri, dgc)
                df = dg / f - dk
                dhf_ref[rows, cols] = (df * (1.0 - lb) * sig * (1.0 - sig)).astype(BF16)
                acc_ref[j, 0:1, :] += jnp.sum(df * (1.0 - sig), axis=0, keepdims=True)
                dhq_ref[rows, cols] = (dq * (sq * (1.0 + hq * (1.0 - sq)))).astype(BF16)
                dhi_ref[rows, cols] = dv.astype(BF16)
                ds_ref[j] = dst * gam + _dot(do, qe, TN)
            return carry

        lax.fori_loop(0, NCH, chunk, 0, unroll=HG_UNROLL)
        for j in range(HG_HP):
            dlb_ref[:, j * DH:(j + 1) * DH] = acc_ref[j, 0:1, :]
            dnw_ref[...] += acc_ref[j, 1:2, :]

    head_spec = pl.BlockSpec((S, HG_HP * DH), lambda h: (0, h))
    head_out = jax.ShapeDtypeStruct((S, HW), BF16)
    return _pcall(
        body, [proj, proj, proj, proj, lb_logits, norm_w, oraw, states, dout], phase, name="hgrn_bwd",
        out_shape=(head_out, head_out, head_out, head_out,
                   jax.ShapeDtypeStruct((1, HW), F32), jax.ShapeDtypeStruct((1, DH), F32)),
        grid=(H // HG_HP,),
        in_specs=[_proj_col_spec(0), _proj_col_spec(1), _proj_col_spec(2), _proj_col_spec(3),
                  pl.BlockSpec((2, HG_HP * DH), lambda h: (0, h)), pl.BlockSpec((1, DH), lambda h: (0, 0)),
                  head_spec, pl.BlockSpec((HG_HP, NCH, DH, DH), lambda h: (h, 0, 0, 0)), head_spec],
        out_specs=(head_spec, head_spec, head_spec, head_spec,
                   pl.BlockSpec((1, HG_HP * DH), lambda h: (0, h)), pl.BlockSpec((1, DH), lambda h: (0, 0))),
        scratch_shapes=[pltpu.VMEM((HG_HP, DH, DH), F32), pltpu.VMEM((HG_HP, 8, DH), F32)],
        compiler_params=_params(1),
    )


ATT_SCALE = DH ** -0.5
HEADS_PER_STEP = {1: 8, 4: 1, 16: 1}


def _sub_rows(ref, r, dil, cols):
    if dil == 1:
        return ref[:, cols]
    return ref[pl.ds(r, BLK, stride=dil), cols]


def _set_sub_rows(ref, r, dil, cols, val):
    if dil == 1:
        ref[:, cols] = val
    else:
        ref[pl.ds(r, BLK, stride=dil), cols] = val


def _slopes(dil):
    hp = HEADS_PER_STEP[dil]
    s = jnp.asarray([dil * 2.0 ** (-(h + 1)) for h in range(H)], F32)
    return jnp.broadcast_to(s[:, None], (H, DH)).reshape(H // hp, hp, DH)


def _rms_rows(x, w):
    rs = lax.rsqrt(jnp.mean(x * x, axis=-1, keepdims=True) + EPS)
    return x * rs, rs


def _att_masks():
    qi = lax.broadcasted_iota(jnp.int32, (BLK, BLK), 0)
    kj = lax.broadcasted_iota(jnp.int32, (BLK, BLK), 1)
    steps_c = qi - kj
    steps_p = qi - kj + BLK
    return steps_c, steps_p, steps_c >= 0, steps_p <= BLK


def _attn_fwd(proj, q_w, k_w, dil, phase=None):
    hp = HEADS_PER_STEP[dil]
    rows, ng, nb = BLK * dil, H // hp, S // (BLK * dil)

    def body(q_ref, kc_ref, kp_ref, vc_ref, vp_ref, qw_ref, kw_ref, sl_ref, o_ref, lse_ref):
        n = pl.program_id(0)
        steps_c, steps_p, ok_c, ok_p = _att_masks()
        ok_p = jnp.logical_and(ok_p, n > 0)
        fc, fp = steps_c.astype(F32), steps_p.astype(F32)
        qw, kw = qw_ref[...], kw_ref[...]
        for hh in range(hp):
            cols = slice(hh * DH, (hh + 1) * DH)
            sl = sl_ref[0, hh:hh + 1, :]
            for r in range(dil):
                qn = _rms_rows(_sub_rows(q_ref, r, dil, cols), None)[0] * qw
                kcn = _rms_rows(_sub_rows(kc_ref, r, dil, cols), None)[0] * kw
                kpn = _rms_rows(_sub_rows(kp_ref, r, dil, cols), None)[0] * kw
                sc = jnp.where(ok_c, _dot(qn, kcn, NT) * ATT_SCALE - sl * fc, NEG)
                sp = jnp.where(ok_p, _dot(qn, kpn, NT) * ATT_SCALE - sl * fp, NEG)
                m = jnp.maximum(jnp.max(sc, axis=-1, keepdims=True), jnp.max(sp, axis=-1, keepdims=True))
                pc, pp = jnp.exp(sc - m), jnp.exp(sp - m)
                den = jnp.sum(pc, axis=-1, keepdims=True) + jnp.sum(pp, axis=-1, keepdims=True)
                o = (_dot(pc, _sub_rows(vc_ref, r, dil, cols)) + _dot(pp, _sub_rows(vp_ref, r, dil, cols))) / den
                _set_sub_rows(o_ref, r, dil, cols, o)
                _set_sub_rows(lse_ref, r, dil, cols, jnp.broadcast_to(m + jnp.log(den), (BLK, DH)))

    def spec(group, prev):
        if prev:
            return pl.BlockSpec((rows, hp * DH), lambda n, g: (jnp.maximum(n - 1, 0), group * ng + g))
        return pl.BlockSpec((rows, hp * DH), lambda n, g: (n, group * ng + g))

    out = jax.ShapeDtypeStruct((S, HW), F32)
    out_spec = pl.BlockSpec((rows, hp * DH), lambda n, g: (n, g))
    wspec = pl.BlockSpec((1, DH), lambda n, g: (0, 0))
    return _pcall(
        body, [proj, proj, proj, proj, proj, q_w, k_w, _slopes(dil)], phase,
        name=f"attn_fwd_d{dil}", out_shape=(out, out), grid=(nb, ng),
        in_specs=[spec(4, False), spec(5, False), spec(5, True), spec(6, False), spec(6, True), wspec, wspec,
                  pl.BlockSpec((1, hp, DH), lambda n, g: (g, 0, 0))],
        out_specs=(out_spec, out_spec),
        compiler_params=_params(2),
    )


def _attn_merge(outs, lses, cat):
    def body(o1, o2, o3, l1, l2, l3, cat_ref, ob_ref, of_ref, lse_ref):
        a, b, c = l1[...], l2[...], l3[...]
        m = jnp.maximum(jnp.maximum(a, b), c)
        ea, eb, ec = jnp.exp(a - m), jnp.exp(b - m), jnp.exp(c - m)
        den = ea + eb + ec
        o = (ea * o1[...] + eb * o2[...] + ec * o3[...]) / den
        ob_ref[...] = o.astype(BF16)
        of_ref[...] = o
        lse_ref[...] = m + jnp.log(den)

    f = jax.ShapeDtypeStruct((S, HW), F32)
    return pl.pallas_call(
        body, name="attn_merge", out_shape=(jax.ShapeDtypeStruct((S, 2 * HW), BF16), f, f), grid=(S // TR,),
        in_specs=[_row_spec(HW)] * 6 + [pl.BlockSpec(memory_space=pl.ANY)],
        out_specs=(pl.BlockSpec((TR, HW), lambda i: (i, 1)), _row_spec(HW), _row_spec(HW)),
        input_output_aliases={6: 0},
        compiler_params=_params(1),
    )(*outs, *lses, cat)


def _attn_bwd(proj, q_w, k_w, o, lse, do, dil, acc, phase=None):
    hp = HEADS_PER_STEP[dil]
    rows, ng, nb = BLK * dil, H // hp, S // (BLK * dil)
    has_acc = acc is not None

    def body(*refs):
        (q_ref, qn_ref, kp_ref, kc_ref, vp_ref, vc_ref, o_ref, on_ref, l_ref, ln_ref, do_ref, don_ref,
         qw_ref, kw_ref, sl_ref) = refs[:15]
        refs = refs[15:]
        if has_acc:
            aq_ref, ak_ref, av_ref, aqw_ref, akw_ref = refs[:5]
            refs = refs[5:]
        dq_ref, dk_ref, dv_ref, dqw_ref, dkw_ref = refs
        m, g = pl.program_id(0), pl.program_id(1)
        steps_c, steps_p, ok_c, ok_p = _att_masks()
        ok_prev = jnp.logical_and(ok_p, m > 0)
        ok_next = jnp.logical_and(ok_p, m < nb - 1)
        fc, fp = steps_c.astype(F32), steps_p.astype(F32)
        qw, kw = qw_ref[...], kw_ref[...]

        @pl.when(jnp.logical_and(m == 0, g == 0))
        def _():
            if has_acc:
                dqw_ref[...] = aqw_ref[...]
                dkw_ref[...] = akw_ref[...]
            else:
                dqw_ref[...] = jnp.zeros_like(dqw_ref)
                dkw_ref[...] = jnp.zeros_like(dkw_ref)

        for hh in range(hp):
            cols = slice(hh * DH, (hh + 1) * DH)
            sl = sl_ref[0, hh:hh + 1, :]
            for r in range(dil):
                sub = lambda ref: _sub_rows(ref, r, dil, cols)
                qx, qrs = _rms_rows(sub(q_ref), None)
                kx, krs = _rms_rows(sub(kc_ref), None)
                qn, kcn = qx * qw, kx * kw
                qnn = _rms_rows(sub(qn_ref), None)[0] * qw
                kpn = _rms_rows(sub(kp_ref), None)[0] * kw
                vc, vp = sub(vc_ref), sub(vp_ref)
                dov, donv = sub(do_ref), sub(don_ref)
                lse_m, lse_n = sub(l_ref)[:, 0:1], sub(ln_ref)[:, 0:1]
                delta_m = jnp.sum(dov * sub(o_ref), axis=-1, keepdims=True)
                delta_n = jnp.sum(donv * sub(on_ref), axis=-1, keepdims=True)
                p_c = jnp.where(ok_c, jnp.exp(_dot(qn, kcn, NT) * ATT_SCALE - sl * fc - lse_m), 0.0)
                p_p = jnp.where(ok_prev, jnp.exp(_dot(qn, kpn, NT) * ATT_SCALE - sl * fp - lse_m), 0.0)
                p_n = jnp.where(ok_next, jnp.exp(_dot(qnn, kcn, NT) * ATT_SCALE - sl * fp - lse_n), 0.0)
                ds_c = p_c * (_dot(dov, vc, NT) - delta_m)
                ds_p = p_p * (_dot(dov, vp, NT) - delta_m)
                ds_n = p_n * (_dot(donv, vc, NT) - delta_n)
                dqn = (_dot(ds_c, kcn) + _dot(ds_p, kpn)) * ATT_SCALE
                dkn = (_dot(ds_c, qn, TN) + _dot(ds_n, qnn, TN)) * ATT_SCALE
                dv = _dot(p_c, dov, TN) + _dot(p_n, donv, TN)
                dqw_ref[...] += jnp.sum(dqn * qx, axis=0, keepdims=True)
                dkw_ref[...] += jnp.sum(dkn * kx, axis=0, keepdims=True)
                dyq, dyk = dqn * qw, dkn * kw
                daq = qrs * (dyq - qx * jnp.mean(dyq * qx, axis=-1, keepdims=True))
                dak = krs * (dyk - kx * jnp.mean(dyk * kx, axis=-1, keepdims=True))
                if has_acc:
                    daq, dak, dv = daq + sub(aq_ref), dak + sub(ak_ref), dv + sub(av_ref)
                _set_sub_rows(dq_ref, r, dil, cols, daq)
                _set_sub_rows(dk_ref, r, dil, cols, dak)
                _set_sub_rows(dv_ref, r, dil, cols, dv)

    def shifted(shift, m):
        if shift < 0:
            return jnp.maximum(m - 1, 0)
        if shift > 0:
            return jnp.minimum(m + 1, nb - 1)
        return m

    def pspec(group, shift):
        return pl.BlockSpec((rows, hp * DH), lambda m, g: (shifted(shift, m), group * ng + g))

    cur = pl.BlockSpec((rows, hp * DH), lambda m, g: (m, g))
    nxt = pl.BlockSpec((rows, hp * DH), lambda m, g: (shifted(1, m), g))
    wspec = pl.BlockSpec((1, DH), lambda m, g: (0, 0))
    ins = [proj, proj, proj, proj, proj, proj, o, o, lse, lse, do, do, q_w, k_w, _slopes(dil)]
    in_specs = [pspec(4, 0), pspec(4, 1), pspec(5, -1), pspec(5, 0), pspec(6, -1), pspec(6, 0),
                cur, nxt, cur, nxt, pspec(1, 0), pspec(1, 1), wspec, wspec,
                pl.BlockSpec((1, hp, DH), lambda m, g: (g, 0, 0))]
    if has_acc:
        ins += list(acc)
        in_specs += [cur, cur, cur, wspec, wspec]
    big = jax.ShapeDtypeStruct((S, HW), F32)
    small = jax.ShapeDtypeStruct((1, DH), F32)
    return _pcall(
        body, ins, phase, name=f"attn_bwd_d{dil}", out_shape=(big, big, big, small, small), grid=(nb, ng),
        in_specs=in_specs, out_specs=(cur, cur, cur, wspec, wspec),
        compiler_params=_params(2),
    )


TC = 512
NCT = DFF // TC


def _shift_rows(a, k):
    rows = lax.broadcasted_iota(jnp.int32, a.shape, 0)
    rolled = pltpu.roll(a, k % S, 0)
    if k > 0:
        return jnp.where(rows >= k, rolled, 0.0)
    return jnp.where(rows < S + k, rolled, 0.0)


def _conv_pre(a, w_ref, b_ref):
    return b_ref[...] + w_ref[0:1, :] * _shift_rows(a, 2) + w_ref[1:2, :] * _shift_rows(a, 1) + w_ref[2:3, :] * a


def _conv_gate_fwd(u, conv_w, conv_b, phase=None):
    def body(a_ref, g_ref, w_ref, b_ref, y_ref):
        pre = _conv_pre(a_ref[...].astype(F32), w_ref, b_ref)
        y_ref[...] = (pre * _sigmoid(pre) * g_ref[...].astype(F32)).astype(BF16)

    return _pcall(
        body, [u, u, conv_w, conv_b], phase,
        name="conv_gate_fwd", out_shape=jax.ShapeDtypeStruct((S, DFF), BF16), grid=(NCT,),
        in_specs=[pl.BlockSpec((S, TC), lambda i: (0, i)), pl.BlockSpec((S, TC), lambda i: (0, NCT + i)),
                  pl.BlockSpec((3, TC), lambda i: (0, i)), pl.BlockSpec((1, TC), lambda i: (0, i))],
        out_specs=pl.BlockSpec((S, TC), lambda i: (0, i)),
        compiler_params=_params(1),
    )


def _conv_gate_bwd(dy, u, conv_w, conv_b, phase=None):
    def body(dy_ref, a_ref, g_ref, w_ref, b_ref, da_ref, dg_ref, db_ref, dw_ref):
        a = a_ref[...].astype(F32)
        dyv = dy_ref[...].astype(F32)
        pre = _conv_pre(a, w_ref, b_ref)
        sg = _sigmoid(pre)
        dg_ref[...] = (dyv * pre * sg).astype(BF16)
        dpre = dyv * g_ref[...].astype(F32) * (sg * (1.0 + pre * (1.0 - sg)))
        da = w_ref[2:3, :] * dpre + w_ref[1:2, :] * _shift_rows(dpre, -1) + w_ref[0:1, :] * _shift_rows(dpre, -2)
        da_ref[...] = da.astype(BF16)
        db_ref[...] = jnp.sum(dpre, axis=0, keepdims=True)
        dw_ref[0:1, :] = jnp.sum(dpre * _shift_rows(a, 2), axis=0, keepdims=True)
        dw_ref[1:2, :] = jnp.sum(dpre * _shift_rows(a, 1), axis=0, keepdims=True)
        dw_ref[2:3, :] = jnp.sum(dpre * a, axis=0, keepdims=True)

    col = pl.BlockSpec((S, TC), lambda i: (0, i))
    half = jax.ShapeDtypeStruct((S, DFF), BF16)
    return _pcall(
        body, [dy, u, u, conv_w, conv_b], phase, name="conv_gate_bwd",
        out_shape=(half, half, jax.ShapeDtypeStruct((1, DFF), F32), jax.ShapeDtypeStruct((3, DFF), F32)),
        grid=(NCT,),
        in_specs=[col, col, pl.BlockSpec((S, TC), lambda i: (0, NCT + i)),
                  pl.BlockSpec((3, TC), lambda i: (0, i)), pl.BlockSpec((1, TC), lambda i: (0, i))],
        out_specs=(col, col, pl.BlockSpec((1, TC), lambda i: (0, i)), pl.BlockSpec((3, TC), lambda i: (0, i))),
        compiler_params=_params(1),
    )


def _out_loss(z, x1, target, gate):
    def body(z_ref, x_ref, t_ref, g_ref, do_ref, dz_ref, dg_ref, loss_ref):
        i = pl.program_id(0)
        zv = z_ref[...]
        gv = g_ref[...]
        err = x_ref[...] + gv * zv - t_ref[...]
        dout = err * (1.0 / D)
        do_ref[...] = dout
        dz_ref[...] = (dout * gv).astype(BF16)

        @pl.when(i == 0)
        def _():
            dg_ref[...] = jnp.zeros_like(dg_ref)
            loss_ref[...] = jnp.zeros_like(loss_ref)

        dg_ref[...] += jnp.sum(dout * zv, axis=0, keepdims=True)
        part = jnp.sum(jnp.sum(err * err, axis=0, keepdims=True), axis=-1, keepdims=True) * (0.5 / D)
        lane = lax.broadcasted_iota(jnp.int32, (1, 128), 1)
        loss_ref[...] += jnp.where(lane == 0, part, 0.0)

    return pl.pallas_call(
        body, name="out_loss",
        out_shape=(jax.ShapeDtypeStruct((S, D), F32), jax.ShapeDtypeStruct((S, D), BF16),
                   jax.ShapeDtypeStruct((1, D), F32), jax.ShapeDtypeStruct((1, 128), F32)),
        grid=(S // TR,),
        in_specs=[_row_spec(), _row_spec(), _row_spec(), _vec_spec()],
        out_specs=(_row_spec(), _row_spec(), _vec_spec(), _vec_spec(128)),
        compiler_params=_params(1),
    )(z, x1, target, gate)


ADA_COLS = 6 * D // N_CHIPS
TA = 512


def _ada_fwd(c_all, w_shard, b_shard):
    def body(c_ref, w_ref, b_ref, o_ref):
        cv = c_ref[...]
        o_ref[...] = _dot_f32(cv * _sigmoid(cv), w_ref[...]) + b_ref[...]

    return pl.pallas_call(
        body, name="ada_fwd", out_shape=jax.ShapeDtypeStruct((N_DEV, ADA_COLS), F32), grid=(ADA_COLS // TA,),
        in_specs=[pl.BlockSpec((N_DEV, D), lambda j: (0, 0)), pl.BlockSpec((D, TA), lambda j: (0, j)),
                  pl.BlockSpec((1, TA), lambda j: (0, j))],
        out_specs=pl.BlockSpec((N_DEV, TA), lambda j: (0, j)),
        compiler_params=_params(1),
    )(c_all, w_shard, b_shard)


def _ada_bwd(c_all, dmod_shard):
    def body(c_ref, d_ref, o_ref):
        cv = c_ref[...]
        o_ref[...] = lax.dot_general(cv * _sigmoid(cv), d_ref[...], TN, precision=lax.Precision.HIGHEST,
                                     preferred_element_type=F32)

    return pl.pallas_call(
        body, name="ada_bwd", out_shape=jax.ShapeDtypeStruct((D, ADA_COLS), F32), grid=(ADA_COLS // TA,),
        in_specs=[pl.BlockSpec((N_DEV, D), lambda j: (0, 0)), pl.BlockSpec((N_DEV, TA), lambda j: (0, j))],
        out_specs=pl.BlockSpec((D, TA), lambda j: (0, j)),
        compiler_params=_params(1),
    )(c_all, dmod_shard)


def _sum_rows(g, name):
    rows, n = g.shape

    def body(g_ref, o_ref):
        acc = g_ref[0:1, :]
        for i in range(1, rows):
            acc = acc + g_ref[i:i + 1, :]
        o_ref[...] = acc

    return pl.pallas_call(
        body, name=name, out_shape=jax.ShapeDtypeStruct((1, n), F32),
        in_specs=[VMEM_SPEC], out_specs=VMEM_SPEC,
    )(g)


def _lb_grad(lb_logits, dlb):
    def body(l_ref, d_ref, o_ref):
        p = 1.0 / (1.0 + jnp.exp(l_ref[1:2, :] - l_ref[0:1, :]))
        t = d_ref[...] * p * (1.0 - p)
        o_ref[0:1, :] = t
        o_ref[1:2, :] = -t

    return pl.pallas_call(
        body, name="lb_grad", out_shape=jax.ShapeDtypeStruct((2, HW), F32),
        in_specs=[VMEM_SPEC, VMEM_SPEC], out_specs=VMEM_SPEC,
    )(lb_logits, dlb)


def _adamw(w, g, m, v, name):
    rows, cols = w.shape
    tr = rows
    if rows * cols * 4 > ADAM_BLOCK_BYTES:
        tr = _pick(rows, [t for t in (256, 128, 64, 32, 16, 8) if t * cols * 4 <= ADAM_BLOCK_BYTES])

    def body(w_ref, g_ref, m_ref, v_ref, d_ref, mo_ref, vo_ref):
        gv = g_ref[...]
        m2 = ADAM_B1 * m_ref[...] + (1.0 - ADAM_B1) * gv
        v2 = ADAM_B2 * v_ref[...] + (1.0 - ADAM_B2) * (gv * gv)
        m_hat = m2 / (1.0 - ADAM_B1 ** ADAM_STEP)
        v_hat = v2 / (1.0 - ADAM_B2 ** ADAM_STEP)
        d_ref[...] = -ADAM_LR * (m_hat / (jnp.sqrt(v_hat) + ADAM_EPS) + ADAM_WD * w_ref[...])
        mo_ref[...] = m2
        vo_ref[...] = v2

    spec = pl.BlockSpec((tr, cols), lambda i: (i, 0))
    out = jax.ShapeDtypeStruct((rows, cols), F32)
    return pl.pallas_call(
        body, name=name, out_shape=(out, out, out), grid=(rows // tr,),
        in_specs=[spec] * 4, out_specs=(spec,) * 3,
        compiler_params=_params(1),
    )(w, g, m, v)


W_IN, W_OUT, W_UP, W_DOWN = range(4)
IN_CHUNKS = 4
W_INQ = 4


def _join_row_chunks(chunks):
    return jnp.concatenate(chunks, axis=0)


def _sequence_step(x, target, mod, norm1_w, lb_logits, hg_norm_w, q_norm_w, k_norm_w, norm2_w, conv_w, conv_b, net):
    shift1, scale1, gate1, shift2, scale2, gate2 = [mod[:, i * D:(i + 1) * D] for i in range(6)]

    def run(name, fn, *args, **kw):
        phase = net.host(name)
        if phase is None:
            return fn(*args, **kw)
        res, comm = fn(*args, phase=phase, **kw)
        net.done(name, comm)
        return res

    h = _norm_mod(x, norm1_w, scale1, shift1, "norm1_fwd")
    proj = run("proj_fwd", _matmul, h, net.full[W_IN], "nn", F32, "proj_fwd")
    cat, o_raw, states = run("hgrn_fwd", _hgrn_fwd, proj, lb_logits, hg_norm_w)
    att = [run(f"attn_fwd_d{dil}", _attn_fwd, proj, q_norm_w, k_norm_w, dil) for dil in DILS]
    cat, b_out_f32, lse = _attn_merge([t[0] for t in att], [t[1] for t in att], cat)
    mix = run("mix_fwd", _matmul, cat, net.full[W_OUT], "nn", F32, "mix_fwd")
    x1, h2 = _resid_norm_mod(x, mix, gate1, norm2_w, scale2, shift2, "norm2_fwd")
    u = run("up_fwd", _matmul, h2, net.full[W_UP], "nn", BF16, "up_fwd")
    yact = run("conv_gate_fwd", _conv_gate_fwd, u, conv_w, conv_b)
    net.alone("before_down_fwd")
    z = _matmul(yact, net.full[W_DOWN], "nn", F32, "down_fwd")
    dout, dz, dgate2, loss_row = _out_loss(z, x1, target, gate2)

    net.grad[W_DOWN] = _matmul(yact, dz, "tn", BF16, "down_bwd_w")
    dyact = run("down_bwd_x", _matmul, dz, net.full[W_DOWN], "nt", BF16, "down_bwd_x")
    da, dg, dconv_b, dconv_w = run("conv_gate_bwd", _conv_gate_bwd, dyact, u, conv_w, conv_b)
    du = jnp.concatenate([da, dg], axis=1)
    net.grad[W_UP] = run("up_bwd_w", _matmul, h2, du, "tn", BF16, "up_bwd_w")
    dh2 = run("up_bwd_x", _matmul, du, net.full[W_UP], "nt", F32, "up_bwd_x")
    dx1, dshift2, dscale2, dnorm2, dgate1, dmix = run(
        "norm2_bwd", _norm_mod_bwd, dh2, x1, norm2_w, scale2, dout, "norm2_bwd", mix=mix, gate=gate1)
    net.grad[W_OUT] = run("mix_bwd_w", _matmul, cat, dmix, "tn", BF16, "mix_bwd_w")
    dcat = run("mix_bwd_x", _matmul, dmix, net.full[W_OUT], "nt", F32, "mix_bwd_x")
    dhq, dhf, dhi, dhg, dlb, dhg_norm = run("hgrn_bwd", _hgrn_bwd, proj, lb_logits, hg_norm_w, o_raw, states, dcat)
    acc = None
    for dil in DILS:
        acc = run(f"attn_bwd_d{dil}", _attn_bwd, proj, q_norm_w, k_norm_w, b_out_f32, lse, dcat, dil, acc)
    daq, dak, dav, dq_norm, dk_norm = acc
    dproj = jnp.concatenate([dhq, dhf, dhi, dhg, daq.astype(BF16), dak.astype(BF16), dav.astype(BF16)], axis=1)
    for q in range(IN_CHUNKS):
        net.grad[W_INQ + q] = run(f"proj_bwd_w_{q}", _matmul, h, dproj, "tn", BF16, f"proj_bwd_w_{q}",
                                  m_blocks=(D // IN_CHUNKS, [q]))
    dh = run("proj_bwd_x", _matmul, dproj, net.full[W_IN], "nt", F32, "proj_bwd_x")
    grad_x, dshift1, dscale1, dnorm1 = run("norm1_bwd", _norm_mod_bwd, dh, x, norm1_w, scale1, dx1, "norm1_bwd")

    dmod = jnp.concatenate([dshift1, dscale1, dgate1, dshift2, dscale2, dgate2], axis=1)
    small = dict(norm1_w=dnorm1, lb=dlb, hg_norm_w=dhg_norm, q_norm_w=dq_norm, k_norm_w=dk_norm,
                 norm2_w=dnorm2, conv_b=dconv_b, conv_w=dconv_w)
    return loss_row, grad_x, dmod, small


def _place():
    x, y, c = lax.axis_index("x"), lax.axis_index("y"), lax.axis_index("c")
    others = [(1 - x, y), (x, 1 - y), (1 - x, 1 - y)]
    return x, y, c, others


def _remote(src, dst, send_sems, recv_sems, k, to):
    return pltpu.make_async_remote_copy(src_ref=src, dst_ref=dst, send_sem=send_sems.at[k], recv_sem=recv_sems.at[k],
                                        device_id=to, device_id_type=MESH)


class _Phase:
    def __init__(self):
        self.ins, self.outs, self.aliases, self.groups, self.n = [], [], {}, [], 0

    def add(self, ins, outs, aliases, n, copies):
        i0, o0 = len(self.ins), len(self.outs)
        self.ins += list(ins)
        self.outs += list(outs)
        self.aliases.update({i0 + i: o0 + o for i, o in aliases.items()})
        self.groups.append((slice(i0, i0 + len(ins)), slice(o0, o0 + len(outs)), copies))
        self.n += n
        return slice(o0, o0 + len(outs))

    def build(self, cin, cout, send_sems, recv_sems, landing):
        res = []
        for si, so, copies in self.groups:
            for src, dst, land, peer in copies(cin[si], cout[so]):
                k = len(res)
                res.append(_remote(land, land, send_sems, recv_sems, k, peer) if landing
                           else _remote(src, dst, send_sems, recv_sems, k, peer))
        assert len(res) == self.n
        return res


def _pcall(body, args, phase=None, **kw):
    if phase is None or not phase.groups:
        res = pl.pallas_call(body, **kw)(*args)
        return res if phase is None else (res, ())
    grid = tuple(kw.pop("grid", ()))
    out_shape, out_specs = kw.pop("out_shape"), kw.pop("out_specs")
    single = not isinstance(out_shape, (tuple, list))
    out_shape = [out_shape] if single else list(out_shape)
    out_specs = [out_specs] if single else list(out_specs)
    scratch = list(kw.pop("scratch_shapes", ()))
    n_in, n_out, n_ci, n_co = len(args), len(out_shape), len(phase.ins), len(phase.outs)

    def hosted(*refs):
        ins, cin = refs[:n_in], refs[n_in:n_in + n_ci]
        outs = refs[n_in + n_ci:n_in + n_ci + n_out]
        cout = refs[n_in + n_ci + n_out:n_in + n_ci + n_out + n_co]
        scr, send_sems, recv_sems = refs[n_in + n_ci + n_out + n_co:-2], refs[-2], refs[-1]
        ids = [pl.program_id(a) for a in range(len(grid))]

        def start():
            for cp in phase.build(cin, cout, send_sems, recv_sems, False):
                cp.start()

        def finish():
            for cp in phase.build(cin, cout, send_sems, recv_sems, False):
                cp.wait_send()
            for cp in phase.build(cin, cout, send_sems, recv_sems, True):
                cp.wait_recv()

        if grid:
            first = functools.reduce(jnp.logical_and, [i == 0 for i in ids])
            last = functools.reduce(jnp.logical_and, [i == g - 1 for i, g in zip(ids, grid)])
            pl.when(first)(start)
            body(*ins, *outs, *scr)
            pl.when(last)(finish)
        else:
            start()
            body(*ins, *outs, *scr)
            finish()

    res = pl.pallas_call(
        hosted, out_shape=tuple(out_shape + phase.outs), grid=grid,
        in_specs=list(kw.pop("in_specs")) + [HBM_SPEC] * n_ci, out_specs=tuple(out_specs + [HBM_SPEC] * n_co),
        input_output_aliases={n_in + i: n_out + o for i, o in phase.aliases.items()},
        scratch_shapes=scratch + [pltpu.SemaphoreType.DMA((phase.n,)), pltpu.SemaphoreType.DMA((phase.n,))],
        **kw,
    )(*args, *phase.ins)
    outs = res[:n_out]
    return (outs[0] if single else tuple(outs)), tuple(res[n_out:])


def _comm_only(name, phase):
    return _pcall(lambda: None, [], phase, name=name, out_shape=(), in_specs=[], out_specs=())[1]


def _all_gather_rows(block, name):
    m_per, n = block.shape

    def body(x_ref, out_ref, send_sems, recv_sems, local_sem):
        x, y, c, chips = _place()
        me, sibling = (x, y, c), (x, y, 1 - c)

        def rows(px, py, pc):
            return out_ref.at[pl.ds((4 * px + 2 * py + pc) * m_per, m_per), :]

        def copy(k, blk, to, src=None):
            return _remote(rows(*blk) if src is None else src, rows(*blk), send_sems, recv_sems, k, to)

        mine = pltpu.make_async_copy(x_ref, rows(*me), local_sem)
        mine.start()
        first = [copy(0, me, sibling, src=x_ref)]
        first += [copy(1 + j, me, (*chip, c), src=x_ref) for j, chip in enumerate(chips)]
        for cp in first:
            cp.start()
        passed = [copy(4 + j, (*chip, c), sibling) for j, chip in enumerate(chips)]
        for j, chip in enumerate(chips):
            copy(1 + j, (*chip, c), me).wait_recv()
            passed[j].start()
        copy(0, sibling, me).wait_recv()
        for j, chip in enumerate(chips):
            copy(4 + j, (*chip, 1 - c), me).wait_recv()
        for cp in first + passed:
            cp.wait_send()
        mine.wait()

    return pl.pallas_call(
        body, name=name, out_shape=jax.ShapeDtypeStruct((N_DEV * m_per, n), block.dtype),
        in_specs=[VMEM_SPEC], out_specs=VMEM_SPEC,
        scratch_shapes=[pltpu.SemaphoreType.DMA((7,)), pltpu.SemaphoreType.DMA((7,)), pltpu.SemaphoreType.DMA],
    )(block)


class _Geo:
    def __init__(self, kind, shard_shape):
        self.kind = kind
        self.shard_shape = tuple(shard_shape)
        self.hr, self.hc = shard_shape[0] // 2, shard_shape[1]
        if kind == "col":
            self.full_shape = (shard_shape[0], N_CHIPS * shard_shape[1])
        else:
            self.full_shape = (N_CHIPS * shard_shape[0], shard_shape[1])

    def full_shard(self, ref, j):
        if self.kind == "col":
            return ref.at[:, pl.ds(pl.multiple_of(j * self.hc, 128), self.hc)]
        return ref.at[pl.ds(pl.multiple_of(j * 2 * self.hr, 16), 2 * self.hr), :]

    def full_half(self, ref, j, c):
        if self.kind == "col":
            return ref.at[pl.ds(pl.multiple_of(c * self.hr, 16), self.hr),
                          pl.ds(pl.multiple_of(j * self.hc, 128), self.hc)]
        return ref.at[pl.ds(pl.multiple_of((2 * j + c) * self.hr, 16), self.hr), :]

    def piece(self, ref, j, c, p0, p1, pieces, part=None):
        pr = self.hr // pieces
        off, n = p0 * pr, (p1 - p0) * pr
        if part is not None:
            top = (n // 32) * 16
            off, n = (off, top) if part == 0 else (off + top, n - top)
        if self.kind == "col":
            return ref.at[pl.ds(pl.multiple_of(c * self.hr + off, 16), n),
                          pl.ds(pl.multiple_of(j * self.hc, 128), self.hc)]
        return ref.at[pl.ds(pl.multiple_of((2 * j + c) * self.hr + off, 16), n), :]


WEIGHT_KINDS = ("col", "row", "col", "row")
NW = len(WEIGHT_KINDS)


def _comm_call(body, name, ins, outs, n_remote, aliases=None):
    return pl.pallas_call(
        body, name=name, out_shape=tuple(outs),
        in_specs=[HBM_SPEC] * len(ins), out_specs=tuple([HBM_SPEC] * len(outs)),
        input_output_aliases=aliases or {},
        scratch_shapes=[pltpu.SemaphoreType.DMA((n_remote,)), pltpu.SemaphoreType.DMA((n_remote,))],
    )(*ins)


ROW_TILES = (256, 176, 128, 64, 32, 16)


def _cast_into_full(shard, geo, place, name):
    rs, cs = shard.shape
    tr = _pick(rs, ROW_TILES)
    nb = rs // tr

    def body(place_ref, s_ref, o_ref):
        o_ref[...] = s_ref[...].astype(BF16)

    if geo.kind == "col":
        out_map = lambda i, place_ref: (i, place_ref[0])
    else:
        out_map = lambda i, place_ref: (place_ref[0] * nb + i, 0)
    return pl.pallas_call(
        body, name=name, out_shape=jax.ShapeDtypeStruct(geo.full_shape, BF16),
        grid_spec=pltpu.PrefetchScalarGridSpec(
            num_scalar_prefetch=1, grid=(nb,),
            in_specs=[pl.BlockSpec((tr, cs), lambda i, place_ref: (i, 0))],
            out_specs=pl.BlockSpec((tr, cs), out_map)),
        compiler_params=_params(1),
    )(place, shard)


def _gather_weight(full, geo, pieces, name):
    per = 8

    def body(in_ref, ref, send_sems, recv_sems):
        x, y, c, _ = _place()
        j, jx, jy, jo = 2 * x + y, 2 * (1 - x) + y, 2 * x + (1 - y), 2 * (1 - x) + (1 - y)
        nx, ny, sib = (1 - x, y, c), (x, 1 - y, c), (x, y, 1 - c)

        def cp(region, k, to):
            return _remote(region, region, send_sems, recv_sems, k, to)

        def reg(chip, p, part=None, core=c):
            return geo.piece(ref, chip, core, p, p + 1, pieces, part)

        sent = []
        for p in range(pieces):
            sent += [cp(reg(j, p), per * p, nx), cp(reg(j, p), per * p + 1, ny)]
        for d in sent:
            d.start()
        for p in range(pieces):
            cp(reg(jx, p), per * p, nx).wait_recv()
            new = [cp(reg(jx, p, 0), per * p + 2, ny), cp(reg(jx, p), per * p + 4, sib)]
            cp(reg(jy, p), per * p + 1, ny).wait_recv()
            new += [cp(reg(jy, p, 1), per * p + 3, nx), cp(reg(jy, p), per * p + 5, sib)]
            for d in new:
                d.start()
            sent += new
        for p in range(pieces):
            cp(reg(jo, p, 0), per * p + 2, ny).wait_recv()
            cp(reg(jo, p, 1), per * p + 3, nx).wait_recv()
            new = [cp(reg(jo, p, 0), per * p + 6, sib), cp(reg(jo, p, 1), per * p + 7, sib)]
            for d in new:
                d.start()
            sent += new
        for p in range(pieces):
            cp(reg(jx, p, None, 1 - c), per * p + 4, sib).wait_recv()
            cp(reg(jy, p, None, 1 - c), per * p + 5, sib).wait_recv()
            cp(reg(jo, p, 0, 1 - c), per * p + 6, sib).wait_recv()
            cp(reg(jo, p, 1, 1 - c), per * p + 7, sib).wait_recv()
        for d in sent:
            d.wait_send()

    return _comm_call(body, name, [full], [_same(full)], per * pieces, aliases={0: 0})[0]


def _same(a):
    return jax.ShapeDtypeStruct(a.shape, a.dtype)


def _gather_ici(full, geo, p0, p1, pieces):
    def copies(ins, outs):
        x, y, c, _ = _place()
        mine = geo.piece(outs[0], 2 * x + y, c, p0, p1, pieces)
        return [(mine, mine, geo.piece(outs[0], 2 * ox + oy, c, p0, p1, pieces), (ox, oy, c))
                for ox, oy in ((1 - x, y), (x, 1 - y))]

    return dict(ins=[full], outs=[_same(full)], aliases={0: 0}, n=2, copies=copies)


def _gather_relay(full, geo, p0, p1, pieces):
    def copies(ins, outs):
        x, y, c, _ = _place()
        jx, jy, jo = 2 * (1 - x) + y, 2 * x + (1 - y), 2 * (1 - x) + (1 - y)
        reg = lambda chip, part: geo.piece(outs[0], chip, c, p0, p1, pieces, part)
        return [(reg(jx, 0), reg(jx, 0), reg(jo, 0), (x, 1 - y, c)), (reg(jy, 1), reg(jy, 1), reg(jo, 1), (1 - x, y, c))]

    return dict(ins=[full], outs=[_same(full)], aliases={0: 0}, n=2, copies=copies)


def _gather_d2d(full, geo):
    def copies(ins, outs):
        x, y, c, chips = _place()
        return [(geo.full_half(outs[0], 2 * ox + oy, c), geo.full_half(outs[0], 2 * ox + oy, c),
                 geo.full_half(outs[0], 2 * ox + oy, 1 - c), (x, y, 1 - c)) for ox, oy in chips]

    return dict(ins=[full], outs=[_same(full)], aliases={0: 0}, n=3, copies=copies)


def _swap_d2d(grad, geo):
    def copies(ins, outs):
        x, y, c, _ = _place()
        return [(geo.full_half(ins[0], j, 1 - c), outs[0].at[j], outs[0].at[j], (x, y, 1 - c)) for j in range(N_CHIPS)]

    return dict(ins=[grad], outs=[jax.ShapeDtypeStruct((N_CHIPS, geo.hr, geo.hc), BF16)], aliases={}, n=N_CHIPS,
                copies=copies)


def _scatter_ici(pair, recv, geo, p0, p1, pieces):
    pr = geo.hr // pieces
    rows = pl.ds(p0 * pr, (p1 - p0) * pr)

    def copies(ins, outs):
        x, y, c, chips = _place()
        return [(ins[0].at[2 * ox + oy, rows, :], outs[0].at[k, rows, :], outs[0].at[k, rows, :], (ox, oy, c))
                for k, (ox, oy) in enumerate(chips)]

    out = jax.ShapeDtypeStruct((3, geo.hr, geo.hc), BF16)
    if recv is None:
        return dict(ins=[pair], outs=[out], aliases={}, n=3, copies=copies)
    return dict(ins=[pair, recv], outs=[out], aliases={1: 0}, n=3, copies=copies)


def _join_d2d(shard, geo, row0=0):
    def copies(ins, outs):
        x, y, c, _ = _place()
        mine = outs[0].at[pl.ds(pl.multiple_of(row0 + c * geo.hr, 8), geo.hr), :]
        other = outs[0].at[pl.ds(pl.multiple_of(row0 + (1 - c) * geo.hr, 8), geo.hr), :]
        return [(mine, mine, other, (x, y, 1 - c))]

    return dict(ins=[shard], outs=[_same(shard)], aliases={0: 0}, n=1, copies=copies)


def _add_pair(grad, got, geo, place, name):
    tr = _pick(geo.hr, ROW_TILES)
    nb = geo.hr // tr

    def body(place_ref, a_ref, b_ref, o_ref):
        o_ref[0] = (a_ref[...].astype(F32) + b_ref[0].astype(F32)).astype(BF16)

    if geo.kind == "col":
        own_map = lambda j, i, place_ref: (place_ref[1] * nb + i, j)
    else:
        own_map = lambda j, i, place_ref: ((2 * j + place_ref[1]) * nb + i, 0)
    spec = pl.BlockSpec((1, tr, geo.hc), lambda j, i, place_ref: (j, i, 0))
    return pl.pallas_call(
        body, name=name, out_shape=jax.ShapeDtypeStruct((N_CHIPS, geo.hr, geo.hc), BF16),
        grid_spec=pltpu.PrefetchScalarGridSpec(
            num_scalar_prefetch=1, grid=(N_CHIPS, nb),
            in_specs=[pl.BlockSpec((tr, geo.hc), own_map), spec], out_specs=spec),
        compiler_params=_params(2),
    )(place, grad, got)


def _add_four(pair, recv, geo, place, name, rows=None, row0=0, into=None):
    tr = _pick(geo.hr, ROW_TILES)
    nb = geo.hr // tr
    rows = 2 * geo.hr if rows is None else rows

    def body(place_ref, p_ref, r_ref, *rest):
        rest[-1][...] = ((p_ref[0].astype(F32) + r_ref[0].astype(F32)) + r_ref[1].astype(F32)) + r_ref[2].astype(F32)

    in_specs = [pl.BlockSpec((1, tr, geo.hc), lambda i, place_ref: (place_ref[0], i, 0)),
                pl.BlockSpec((3, tr, geo.hc), lambda i, place_ref: (0, i, 0))]
    args = [place, pair, recv]
    if into is not None:
        in_specs.append(pl.BlockSpec(memory_space=pl.ANY))
        args.append(into)
    return pl.pallas_call(
        body, name=name, out_shape=jax.ShapeDtypeStruct((rows, geo.hc), F32),
        grid_spec=pltpu.PrefetchScalarGridSpec(
            num_scalar_prefetch=1, grid=(nb,), in_specs=in_specs,
            out_specs=pl.BlockSpec((tr, geo.hc), lambda i, place_ref: (row0 // tr + place_ref[1] * nb + i, 0))),
        input_output_aliases={3: 0} if into is not None else {},
        compiler_params=_params(1),
    )(*args)


PIECES = (4, 1, 16, 8) + (1,) * IN_CHUNKS
SCHEDULE = {
    "proj_fwd": (("gather_ici", W_OUT, 0, 1), ("gather_ici", W_UP, 0, 6)),
    "hgrn_fwd": (("gather_relay", W_OUT, 0, 1), ("gather_relay", W_UP, 0, 6), ("gather_ici", W_UP, 6, 10)),
    "attn_fwd_d1": (("gather_ici", W_UP, 10, 12), ("gather_relay", W_UP, 6, 10)),
    "attn_fwd_d4": (("gather_ici", W_UP, 12, 16), ("gather_relay", W_UP, 10, 12), ("gather_d2d", W_OUT)),
    "attn_fwd_d16": (("gather_relay", W_UP, 12, 16), ("gather_ici", W_DOWN, 0, 2)),
    "mix_fwd": (("gather_d2d", W_UP), ("gather_ici", W_DOWN, 2, 4)),
    "up_fwd": (("gather_ici", W_DOWN, 4, 8), ("gather_relay", W_DOWN, 0, 4)),
    "conv_gate_fwd": (("gather_relay", W_DOWN, 4, 8),),
    "before_down_fwd": (("gather_d2d", W_DOWN),),
    "down_bwd_x": (("swap", W_DOWN),),
    "conv_gate_bwd": (("scatter", W_DOWN, 0, 4),),
    "up_bwd_w": (("scatter", W_DOWN, 4, 8),),
    "up_bwd_x": (("swap", W_UP),),
    "norm2_bwd": (("scatter", W_UP, 0, 1),),
    "mix_bwd_w": (("scatter", W_UP, 1, 2),),
    "mix_bwd_x": (("swap", W_OUT), ("scatter", W_UP, 2, 3)),
    "hgrn_bwd": (("scatter", W_UP, 3, 9), ("join", W_DOWN)),
    "attn_bwd_d1": (("scatter", W_UP, 9, 12),),
    "attn_bwd_d4": (("scatter", W_UP, 12, 13), ("scatter", W_OUT, 0, 1)),
    "attn_bwd_d16": (("scatter", W_UP, 13, 16),),
    "proj_bwd_w_0": (("join", W_UP), ("join", W_OUT)),
    "proj_bwd_w_1": (("swap", W_INQ),),
    "proj_bwd_w_2": (("swap", W_INQ + 1),),
    "proj_bwd_w_3": (("swap", W_INQ + 2), ("scatter", W_INQ, 0, 1)),
    "proj_bwd_x": (("swap", W_INQ + 3), ("scatter", W_INQ + 1, 0, 1), ("scatter", W_INQ + 2, 0, 1)),
    "norm1_bwd": (("scatter", W_INQ + 3, 0, 1), ("join", W_INQ), ("join", W_INQ + 1), ("join", W_INQ + 2)),
    "grad_in_join": (("join", W_INQ + 3),),
}


class _Net:
    def __init__(self, shards, place):
        self.place = place
        self.geos = [_Geo(k, s.shape) for k, s in zip(WEIGHT_KINDS, shards)]
        self.full = [_cast_into_full(s, g, place, f"cast_shard_{w}") for w, (s, g) in enumerate(zip(shards, self.geos))]
        self.full[W_IN] = _gather_weight(self.full[W_IN], self.geos[W_IN], PIECES[W_IN], "gather_w_in")
        rows, cols = shards[W_IN].shape
        self.geos += [_Geo("col", (rows // IN_CHUNKS, cols))] * IN_CHUNKS
        n = NW + IN_CHUNKS
        self.grad, self.pair, self.recv, self.shard = [None] * n, [None] * n, [None] * n, [None] * n
        self.in_rows, self.in_shard = rows, None
        self.slots = []

    def _row0(self, w):
        return (w - W_INQ) * 2 * self.geos[w].hr

    def host(self, name):
        phase = _Phase()
        self.slots = []
        groups = {}
        for item in SCHEDULE.get(name, ()):
            kind, w = item[0], item[1]
            geo = self.geos[w]
            key = len(groups)
            if kind == "gather_ici":
                spec, key = _gather_ici(self.full[w], geo, item[2], item[3], PIECES[w]), ("full", w)
            elif kind == "gather_relay":
                spec, key = _gather_relay(self.full[w], geo, item[2], item[3], PIECES[w]), ("full", w)
            elif kind == "gather_d2d":
                spec, key = _gather_d2d(self.full[w], geo), ("full", w)
            elif kind == "swap":
                spec = _swap_d2d(self.grad[w], geo)
            elif kind == "scatter":
                spec, key = _scatter_ici(self.pair[w], self.recv[w], geo, item[2], item[3], PIECES[w]), ("recv", w)
            elif w >= W_INQ:
                spec, key = _join_d2d(self.in_shard, geo, self._row0(w)), "in_shard"
            else:
                spec = _join_d2d(self.shard[w], geo)
            groups.setdefault(key, []).append((item, spec))
        for group in groups.values():
            specs = [s for _, s in group]
            merged = dict(specs[0], n=sum(s["n"] for s in specs),
                          copies=lambda ins, outs, specs=specs: [t for s in specs for t in s["copies"](ins, outs)])
            self.slots.append(([item for item, _ in group], phase.add(**merged)))
        return phase

    def done(self, name, comm):
        for items, sl in sorted(self.slots, key=lambda s: s[0][0][0] == "scatter"):
            out = comm[sl][0]
            for item in items:
                kind, w = item[0], item[1]
                if kind in ("gather_ici", "gather_relay", "gather_d2d"):
                    self.full[w] = out
                elif kind == "swap":
                    self.pair[w] = _add_pair(self.grad[w], out, self.geos[w], self.place, f"grad_pair_sum_{w}")
                elif kind == "scatter":
                    self.recv[w] = out
                    if item[3] == PIECES[w] and w >= W_INQ:
                        self.in_shard = _add_four(self.pair[w], out, self.geos[w], self.place, f"grad_chip_sum_{w}",
                                                  rows=self.in_rows, row0=self._row0(w), into=self.in_shard)
                    elif item[3] == PIECES[w]:
                        self.shard[w] = _add_four(self.pair[w], out, self.geos[w], self.place, f"grad_chip_sum_{w}")
                elif w >= W_INQ:
                    self.in_shard = out
                else:
                    self.shard[w] = out

    def alone(self, name):
        self.done(name, _comm_only(name, self.host(name)))


CONVW_SHARD = 3 * DFF // N_CHIPS
COND_PAD = 8 * 896
SMALL_SIZES = (("norm1_w", D), ("lb", HW), ("hg_norm_w", DH), ("q_norm_w", DH), ("k_norm_w", DH),
               ("norm2_w", D), ("conv_b", DFF), ("conv_w", 3 * DFF), ("loss", 128))
SMALL_TOTAL = sum(n for _, n in SMALL_SIZES)
STATS_PAD = 8 * 5120


def kernel(x, c, w_ada, b_ada, norm1_w, w_in, lb_logits, hg_norm_w, q_norm_w, k_norm_w, w_out, norm2_w, w_up, conv_w, conv_b, w_down, loss_target, m_w_ada, m_b_ada, m_norm1_w, m_w_in, m_lb_logits, m_hg_norm_w, m_q_norm_w, m_k_norm_w, m_w_out, m_norm2_w, m_w_up, m_conv_w, m_conv_b, m_w_down, v_w_ada, v_b_ada, v_norm1_w, v_w_in, v_lb_logits, v_hg_norm_w, v_q_norm_w, v_k_norm_w, v_w_out, v_norm2_w, v_w_up, v_conv_w, v_conv_b, v_w_down):
    chip = 2 * lax.axis_index("x") + lax.axis_index("y")
    dev = 2 * chip + lax.axis_index("c")

    cond = jnp.concatenate([c, conv_w[0].reshape(1, CONVW_SHARD), jnp.zeros((1, COND_PAD - D - CONVW_SHARD), F32)], axis=1)
    cond_all = _all_gather_rows(cond.reshape(8, COND_PAD // 8), "gather_cond").reshape(N_DEV, COND_PAD)
    c_all = cond_all[:, :D]
    conv_w_full = jnp.concatenate(
        [cond_all[2 * j, D:D + CONVW_SHARD].reshape(3, DFF // N_CHIPS) for j in range(N_CHIPS)], axis=1)

    b_shard = lax.dynamic_slice_in_dim(b_ada, chip * ADA_COLS, ADA_COLS, axis=1)
    mod_cols = _all_gather_rows(_ada_fwd(c_all, w_ada[0], b_shard), "gather_mod")
    mod_all = jnp.concatenate([mod_cols[16 * j:16 * j + 8] for j in range(N_CHIPS)], axis=1)
    mod = lax.dynamic_slice_in_dim(mod_all, dev, 1, axis=0)

    place = jnp.stack([chip, lax.axis_index("c")]).astype(jnp.int32)
    net = _Net([w_in[0], w_out[0], w_up[0], w_down[0]], place)
    loss_row, grad_x, dmod, small = _sequence_step(
        x[0], loss_target[0], mod, norm1_w, lb_logits, hg_norm_w, q_norm_w, k_norm_w, norm2_w, conv_w_full, conv_b, net)
    net.alone("grad_in_join")
    g_out, g_up, g_down = net.shard[W_OUT], net.shard[W_UP], net.shard[W_DOWN]
    g_in = net.in_shard

    small["conv_w"] = small["conv_w"].reshape(1, 3 * DFF)
    small["loss"] = loss_row
    stats = jnp.concatenate([dmod] + [small[k] for k, _ in SMALL_SIZES]
                            + [jnp.zeros((1, STATS_PAD - 6 * D - SMALL_TOTAL), F32)], axis=1)
    stats_all = _all_gather_rows(stats.reshape(8, STATS_PAD // 8), "gather_stats").reshape(N_DEV, STATS_PAD)
    dmod_all = stats_all[:, :6 * D]
    sums = _sum_rows(stats_all[:, 6 * D:6 * D + SMALL_TOTAL], "small_grad_sum")
    g_small, off = {}, 0
    for k, n in SMALL_SIZES:
        g_small[k] = sums[:, off:off + n]
        off += n
    loss = g_small["loss"][0, 0]
    g_b_ada = _sum_rows(dmod_all, "b_ada_grad_sum")
    g_w_ada = _ada_bwd(c_all, lax.dynamic_slice_in_dim(dmod_all, chip * ADA_COLS, ADA_COLS, axis=1))
    g_lb = _lb_grad(lb_logits, g_small["lb"])
    g_conv_w = lax.dynamic_slice_in_dim(g_small["conv_w"].reshape(3, DFF), chip * (DFF // N_CHIPS), DFF // N_CHIPS, axis=1)

    names = ["w_ada", "b_ada", "norm1_w", "w_in", "lb_logits", "hg_norm_w", "q_norm_w", "k_norm_w", "w_out",
             "norm2_w", "w_up", "conv_w", "conv_b", "w_down"]
    lead = {"w_ada", "w_in", "w_out", "w_up", "conv_w", "w_down"}
    w = dict(w_ada=w_ada, b_ada=b_ada, norm1_w=norm1_w, w_in=w_in, lb_logits=lb_logits, hg_norm_w=hg_norm_w,
             q_norm_w=q_norm_w, k_norm_w=k_norm_w, w_out=w_out, norm2_w=norm2_w, w_up=w_up, conv_w=conv_w,
             conv_b=conv_b, w_down=w_down)
    m = dict(w_ada=m_w_ada, b_ada=m_b_ada, norm1_w=m_norm1_w, w_in=m_w_in, lb_logits=m_lb_logits,
             hg_norm_w=m_hg_norm_w, q_norm_w=m_q_norm_w, k_norm_w=m_k_norm_w, w_out=m_w_out, norm2_w=m_norm2_w,
             w_up=m_w_up, conv_w=m_conv_w, conv_b=m_conv_b, w_down=m_w_down)
    v = dict(w_ada=v_w_ada, b_ada=v_b_ada, norm1_w=v_norm1_w, w_in=v_w_in, lb_logits=v_lb_logits,
             hg_norm_w=v_hg_norm_w, q_norm_w=v_q_norm_w, k_norm_w=v_k_norm_w, w_out=v_w_out, norm2_w=v_norm2_w,
             w_up=v_w_up, conv_w=v_conv_w, conv_b=v_conv_b, w_down=v_w_down)
    g = dict(w_ada=g_w_ada, b_ada=g_b_ada, norm1_w=g_small["norm1_w"], w_in=g_in, lb_logits=g_lb,
             hg_norm_w=g_small["hg_norm_w"], q_norm_w=g_small["q_norm_w"], k_norm_w=g_small["k_norm_w"], w_out=g_out,
             norm2_w=g_small["norm2_w"], w_up=g_up, conv_w=g_conv_w, conv_b=g_small["conv_b"], w_down=g_down)

    grads, deltas, new_m, new_v = [], [], [], []
    for n in names:
        strip = (lambda t: t[0]) if n in lead else (lambda t: t)
        wrap = (lambda t: t[None]) if n in lead else (lambda t: t)
        d_n, m_n, v_n = _adamw(strip(w[n]), g[n], strip(m[n]), strip(v[n]), f"adamw_{n}")
        grads.append(wrap(g[n]))
        deltas.append(wrap(d_n))
        new_m.append(wrap(m_n))
        new_v.append(wrap(v_n))
    return (loss, grad_x[None], *grads, *deltas, *new_m, *new_v)
```

```python
import functools
import math

import jax
import jax.numpy as jnp
from jax import lax
from jax.experimental import pallas as pl
from jax.experimental.pallas import tpu as pltpu

F32 = jnp.float32
BF16 = jnp.bfloat16

S = 2048
D = 2048
H = 8
DH = 128
HW = H * DH
CH = 64
NCH = S // CH
DFF = 5632
INC = 7 * HW
BLK = 128
DILS = (1, 4, 16)
EPS = 1e-6
NEG = -1e30
N_CHIPS = 4
N_DEV = 8

ADAM_LR = 0.001
ADAM_B1 = 0.9
ADAM_B2 = 0.999
ADAM_EPS = 1e-08
ADAM_WD = 0.01
ADAM_STEP = 10
ADAM_BLOCK_BYTES = 1 << 20

V7X_VMEM_BYTES = 64 * 1024 * 1024
VMEM_LIMIT = (V7X_VMEM_BYTES * 3) // 4
MESH = pl.DeviceIdType.MESH
HBM_SPEC = pl.BlockSpec(memory_space=pltpu.HBM)
VMEM_SPEC = pl.BlockSpec(memory_space=pltpu.VMEM)

NN = (((1,), (0,)), ((), ()))
NT = (((1,), (1,)), ((), ()))
TN = (((0,), (0,)), ((), ()))


def _params(n_grid):
    return pltpu.CompilerParams(dimension_semantics=("arbitrary",) * n_grid, vmem_limit_bytes=VMEM_LIMIT)


def _dot(a, b, dims=NN):
    return lax.dot_general(a.astype(BF16), b.astype(BF16), dims, preferred_element_type=F32)


def _dot_f32(a, b):
    return lax.dot_general(a, b, NN, precision=lax.Precision.HIGHEST, preferred_element_type=F32)


def _sigmoid(x):
    return 1.0 / (1.0 + jnp.exp(-x))


def _pick(n, cands):
    for t in cands:
        if n % t == 0:
            return t
    raise ValueError(n)


def _matmul(a, b, mode, out_dtype, name, phase=None, m_blocks=None):
    if mode == "nn":
        (m, k), (_, n) = a.shape, b.shape
    elif mode == "nt":
        (m, k), (n, _) = a.shape, b.shape
    else:
        (k, m), (_, n) = a.shape, b.shape
    tm = _pick(m, (1024, 1408, 512))
    a_block = lambda i: i
    if m_blocks is not None:
        tm, blocks = m_blocks
        m = tm * len(blocks)
        step = blocks[1] - blocks[0] if len(blocks) > 1 else 0
        assert all(blk == blocks[0] + step * i for i, blk in enumerate(blocks))
        a_block = lambda i: blocks[0] + step * i
    tn = _pick(n, (1024, 1408, 512))
    tk = _pick(k, (2048, 2816, 1792, 1024, 512))
    nk = k // tk
    dims = {"nn": NN, "nt": NT, "tn": TN}[mode]

    def body(a_ref, b_ref, o_ref, *acc):
        part = lax.dot_general(a_ref[...], b_ref[...], dims, preferred_element_type=F32)
        if nk == 1:
            o_ref[...] = part.astype(o_ref.dtype)
            return
        acc_ref, kk = acc[0], pl.program_id(2)

        @pl.when(kk == 0)
        def _():
            acc_ref[...] = part

        @pl.when(jnp.logical_and(kk > 0, kk < nk - 1))
        def _():
            acc_ref[...] += part

        @pl.when(kk == nk - 1)
        def _():
            o_ref[...] = (acc_ref[...] + part).astype(o_ref.dtype)

    if mode == "tn":
        a_spec = pl.BlockSpec((tk, tm), lambda i, j, kk: (kk, a_block(i)))
    else:
        a_spec = pl.BlockSpec((tm, tk), lambda i, j, kk: (i, kk))
    if mode == "nt":
        b_spec = pl.BlockSpec((tn, tk), lambda i, j, kk: (j, kk))
    else:
        b_spec = pl.BlockSpec((tk, tn), lambda i, j, kk: (kk, j))
    return _pcall(
        body, [a, b], phase, name=name,
        out_shape=jax.ShapeDtypeStruct((m, n), out_dtype),
        grid=(m // tm, n // tn, nk),
        in_specs=[a_spec, b_spec],
        out_specs=pl.BlockSpec((tm, tn), lambda i, j, kk: (i, j)),
        scratch_shapes=[pltpu.VMEM((tm, tn), F32)] if nk > 1 else [],
        compiler_params=_params(3),
    )


TR = 256


def _row_spec(cols=D):
    return pl.BlockSpec((TR, cols), lambda i: (i, 0))


def _vec_spec(cols=D):
    return pl.BlockSpec((1, cols), lambda i: (0, 0))


def _norm_mod(x, nw, scale, shift, name):
    def body(x_ref, nw_ref, sc_ref, sh_ref, h_ref):
        xv = x_ref[...]
        r = lax.rsqrt(jnp.mean(xv * xv, axis=-1, keepdims=True) + EPS)
        h_ref[...] = ((xv * r) * nw_ref[...] * (1.0 + sc_ref[...]) + sh_ref[...]).astype(BF16)

    return pl.pallas_call(
        body, name=name, out_shape=jax.ShapeDtypeStruct((S, D), BF16), grid=(S // TR,),
        in_specs=[_row_spec(), _vec_spec(), _vec_spec(), _vec_spec()], out_specs=_row_spec(),
        compiler_params=_params(1),
    )(x, nw, scale, shift)


def _resid_norm_mod(x, mix, gate, nw, scale, shift, name):
    def body(x_ref, mix_ref, g_ref, nw_ref, sc_ref, sh_ref, x1_ref, h_ref):
        xv = x_ref[...] + g_ref[...] * mix_ref[...]
        x1_ref[...] = xv
        r = lax.rsqrt(jnp.mean(xv * xv, axis=-1, keepdims=True) + EPS)
        h_ref[...] = ((xv * r) * nw_ref[...] * (1.0 + sc_ref[...]) + sh_ref[...]).astype(BF16)

    return pl.pallas_call(
        body, name=name,
        out_shape=(jax.ShapeDtypeStruct((S, D), F32), jax.ShapeDtypeStruct((S, D), BF16)), grid=(S // TR,),
        in_specs=[_row_spec(), _row_spec(), _vec_spec(), _vec_spec(), _vec_spec(), _vec_spec()],
        out_specs=(_row_spec(), _row_spec()),
        compiler_params=_params(1),
    )(x, mix, gate, nw, scale, shift)


def _norm_mod_bwd(dh, xin, nw, scale, dres, name, mix=None, gate=None, phase=None):
    with_gate = mix is not None

    def body(*refs):
        if with_gate:
            dh_ref, x_ref, nw_ref, sc_ref, dres_ref, mix_ref, g_ref, dx_ref, dsh_ref, dsc_ref, dnw_ref, dg_ref, dmix_ref = refs
        else:
            dh_ref, x_ref, nw_ref, sc_ref, dres_ref, dx_ref, dsh_ref, dsc_ref, dnw_ref = refs
        i = pl.program_id(0)
        dhv = dh_ref[...]
        xv = x_ref[...]
        r = lax.rsqrt(jnp.mean(xv * xv, axis=-1, keepdims=True) + EPS)
        xn = xv * r
        nwv = nw_ref[...]
        one_sc = 1.0 + sc_ref[...]
        dxn = dhv * nwv * one_sc
        dx = dres_ref[...] + r * (dxn - xn * jnp.mean(dxn * xn, axis=-1, keepdims=True))
        dx_ref[...] = dx

        @pl.when(i == 0)
        def _():
            dsh_ref[...] = jnp.zeros_like(dsh_ref)
            dsc_ref[...] = jnp.zeros_like(dsc_ref)
            dnw_ref[...] = jnp.zeros_like(dnw_ref)
            if with_gate:
                dg_ref[...] = jnp.zeros_like(dg_ref)

        dsh_ref[...] += jnp.sum(dhv, axis=0, keepdims=True)
        dsc_ref[...] += jnp.sum(dhv * xn * nwv, axis=0, keepdims=True)
        dnw_ref[...] += jnp.sum(dhv * xn * one_sc, axis=0, keepdims=True)
        if with_gate:
            dg_ref[...] += jnp.sum(dx * mix_ref[...], axis=0, keepdims=True)
            dmix_ref[...] = (dx * g_ref[...]).astype(BF16)

    vec = jax.ShapeDtypeStruct((1, D), F32)
    ins = [dh, xin, nw, scale, dres]
    in_specs = [_row_spec(), _row_spec(), _vec_spec(), _vec_spec(), _row_spec()]
    outs = [jax.ShapeDtypeStruct((S, D), F32), vec, vec, vec]
    out_specs = [_row_spec(), _vec_spec(), _vec_spec(), _vec_spec()]
    if with_gate:
        ins += [mix, gate]
        in_specs += [_row_spec(), _vec_spec()]
        outs += [vec, jax.ShapeDtypeStruct((S, D), BF16)]
        out_specs += [_vec_spec(), _row_spec()]
    return _pcall(
        body, ins, phase, name=name, out_shape=tuple(outs), grid=(S // TR,), in_specs=in_specs,
        out_specs=tuple(out_specs), compiler_params=_params(1),
    )


def _tri(lower):
    row = lax.broadcasted_iota(jnp.int32, (CH, CH), 0)
    col = lax.broadcasted_iota(jnp.int32, (CH, CH), 1)
    return (row >= col) if lower else (col >= row)


def _hgrn_gates(hq, hf, lb):
    sig = _sigmoid(hf)
    f = lb + (1.0 - lb) * sig
    g = jnp.log(f)
    sq = _sigmoid(hq)
    return sig, f, g, 1.0 - f, hq * sq, sq


HG_HP = 2
HG_UNROLL = 4


def _proj_col_spec(group):
    return pl.BlockSpec((S, HG_HP * DH), lambda h: (0, group * (H // HG_HP) + h))


def _hgrn_fwd(proj, lb_logits, norm_w, phase=None):
    def body(hq_ref, hf_ref, hi_ref, hg_ref, lbl_ref, nw_ref, out_ref, oraw_ref, st_ref, s_ref):
        nw = nw_ref[...]
        lower = _tri(True)
        ltri = lower.astype(F32)
        s_ref[...] = jnp.zeros_like(s_ref)

        def chunk(n, carry):
            rows = pl.ds(pl.multiple_of(n * CH, CH), CH)
            for j in range(HG_HP):
                cols = slice(j * DH, (j + 1) * DH)
                lb = 1.0 / (1.0 + jnp.exp(lbl_ref[1:2, cols] - lbl_ref[0:1, cols]))
                hq, hf, v, hg = hq_ref[rows, cols], hf_ref[rows, cols], hi_ref[rows, cols], hg_ref[rows, cols]
                _, _, g, kk, q, _ = _hgrn_gates(hq, hf, lb)
                gc = _dot_f32(ltri, g)
                gl = jnp.sum(g, axis=0, keepdims=True)
                qe = q * jnp.exp(gc)
                ke = kk * jnp.exp(gl - gc)
                qh = q * jnp.exp(gc - 0.5 * gl)
                kh = kk * jnp.exp(0.5 * gl - gc)
                st = s_ref[j]
                st_ref[j, pl.ds(n, 1)] = st[None]
                a = jnp.where(lower, _dot(qh, kh, NT), 0.0)
                o = _dot(qe, st, NT) + _dot(a, v)
                s_ref[j] = st * jnp.exp(gl) + _dot(v, ke, TN)
                oraw_ref[rows, cols] = o
                rs = lax.rsqrt(jnp.mean(o * o, axis=-1, keepdims=True) + EPS)
                out_ref[rows, cols] = (o * rs * nw * (hg * _sigmoid(hg))).astype(BF16)
            return carry

        lax.fori_loop(0, NCH, chunk, 0, unroll=HG_UNROLL)

    head_spec = pl.BlockSpec((S, HG_HP * DH), lambda h: (0, h))
    return _pcall(
        body, [proj, proj, proj, proj, lb_logits, norm_w], phase, name="hgrn_fwd",
        out_shape=(jax.ShapeDtypeStruct((S, 2 * HW), BF16), jax.ShapeDtypeStruct((S, HW), F32),
                   jax.ShapeDtypeStruct((H, NCH, DH, DH), F32)),
        grid=(H // HG_HP,),
        in_specs=[_proj_col_spec(0), _proj_col_spec(1), _proj_col_spec(2), _proj_col_spec(3),
                  pl.BlockSpec((2, HG_HP * DH), lambda h: (0, h)), pl.BlockSpec((1, DH), lambda h: (0, 0))],
        out_specs=(head_spec, head_spec, pl.BlockSpec((HG_HP, NCH, DH, DH), lambda h: (h, 0, 0, 0))),
        scratch_shapes=[pltpu.VMEM((HG_HP, DH, DH), F32)],
        compiler_params=_params(1),
    )


def _hgrn_bwd(proj, lb_logits, norm_w, oraw, states, dout, phase=None):
    def body(hq_ref, hf_ref, hi_ref, hg_ref, lbl_ref, nw_ref, oraw_ref, st_ref, do_ref,
             dhq_ref, dhf_ref, dhi_ref, dhg_ref, dlb_ref, dnw_ref, ds_ref, acc_ref):
        h = pl.program_id(0)
        nw = nw_ref[...]
        lower = _tri(True)
        ltri = lower.astype(F32)
        utri = _tri(False).astype(F32)
        last = lax.broadcasted_iota(jnp.int32, (CH, DH), 0) == CH - 1
        ds_ref[...] = jnp.zeros_like(ds_ref)
        acc_ref[...] = jnp.zeros_like(acc_ref)

        @pl.when(h == 0)
        def _():
            dnw_ref[...] = jnp.zeros_like(dnw_ref)

        def chunk(i, carry):
            n = NCH - 1 - i
            rows = pl.ds(pl.multiple_of(n * CH, CH), CH)
            for j in range(HG_HP):
                cols = slice(j * DH, (j + 1) * DH)
                lb = 1.0 / (1.0 + jnp.exp(lbl_ref[1:2, cols] - lbl_ref[0:1, cols]))
                hq, hf, v, hg = hq_ref[rows, cols], hf_ref[rows, cols], hi_ref[rows, cols], hg_ref[rows, cols]
                sig, f, g, kk, q, sq = _hgrn_gates(hq, hf, lb)
                gc = _dot_f32(ltri, g)
                gl = jnp.sum(g, axis=0, keepdims=True)
                eg = jnp.exp(gc)
                ek = jnp.exp(gl - gc)
                gam = jnp.exp(gl)
                ph = jnp.exp(gc - 0.5 * gl)
                pk = jnp.exp(0.5 * gl - gc)
                qe, ke = q * eg, kk * ek
                qh, kh = q * ph, kk * pk
                st = st_ref[j, pl.ds(n, 1)][0]
                dst = ds_ref[j]
                a = jnp.where(lower, _dot(qh, kh, NT), 0.0)
                o = oraw_ref[rows, cols]
                rs = lax.rsqrt(jnp.mean(o * o, axis=-1, keepdims=True) + EPS)
                xh = o * rs
                sg = _sigmoid(hg)
                dgo = do_ref[rows, cols]
                d_on = dgo * (hg * sg)
                dhg_ref[rows, cols] = (dgo * xh * nw * (sg * (1.0 + hg * (1.0 - sg)))).astype(BF16)
                acc_ref[j, 1:2, :] += jnp.sum(d_on * xh, axis=0, keepdims=True)
                dy = d_on * nw
                do = rs * (dy - xh * jnp.mean(dy * xh, axis=-1, keepdims=True))
                dqe = _dot(do, st)
                da = jnp.where(lower, _dot(do, v, NT), 0.0)
                dv = _dot(a, do, TN) + _dot(ke, dst, NT)
                dke = _dot(v, dst)
                dgam = jnp.sum(dst * st, axis=0, keepdims=True)
                dqh = _dot(da, kh)
                dkh = _dot(da, qh, TN)
                dq = dqh * ph + dqe * eg
                dk = dkh * pk + dke * ek
                dgl = jnp.sum(dke * ke, axis=0, keepdims=True) + dgam * gam
                dgc = dqh * qh - dkh * kh + dqe * qe - dke * ke + jnp.where(last, dgl, 0.0)
                dg = _dot_f32(utri, dgc)
                df = dg / f - dk
                dhf_ref[rows, cols] = (df * (1.0 - lb) * sig * (1.0 - sig)).astype(BF16)
                acc_ref[j, 0:1, :] += jnp.sum(df * (1.0 - sig), axis=0, keepdims=True)
                dhq_ref[rows, cols] = (dq * (sq * (1.0 + hq * (1.0 - sq)))).astype(BF16)
                dhi_ref[rows, cols] = dv.astype(BF16)
                ds_ref[j] = dst * gam + _dot(do, qe, TN)
            return carry

        lax.fori_loop(0, NCH, chunk, 0, unroll=HG_UNROLL)
        for j in range(HG_HP):
            dlb_ref[:, j * DH:(j + 1) * DH] = acc_ref[j, 0:1, :]
            dnw_ref[...] += acc_ref[j, 1:2, :]

    head_spec = pl.BlockSpec((S, HG_HP * DH), lambda h: (0, h))
    head_out = jax.ShapeDtypeStruct((S, HW), BF16)
    return _pcall(
        body, [proj, proj, proj, proj, lb_logits, norm_w, oraw, states, dout], phase, name="hgrn_bwd",
        out_shape=(head_out, head_out, head_out, head_out,
                   jax.ShapeDtypeStruct((1, HW), F32), jax.ShapeDtypeStruct((1, DH), F32)),
        grid=(H // HG_HP,),
        in_specs=[_proj_col_spec(0), _proj_col_spec(1), _proj_col_spec(2), _proj_col_spec(3),
                  pl.BlockSpec((2, HG_HP * DH), lambda h: (0, h)), pl.BlockSpec((1, DH), lambda h: (0, 0)),
                  head_spec, pl.BlockSpec((HG_HP, NCH, DH, DH), lambda h: (h, 0, 0, 0)), head_spec],
        out_specs=(head_spec, head_spec, head_spec, head_spec,
                   pl.BlockSpec((1, HG_HP * DH), lambda h: (0, h)), pl.BlockSpec((1, DH), lambda h: (0, 0))),
        scratch_shapes=[pltpu.VMEM((HG_HP, DH, DH), F32), pltpu.VMEM((HG_HP, 8, DH), F32)],
        compiler_params=_params(1),
    )


ATT_SCALE = DH ** -0.5
HEADS_PER_STEP = {1: 8, 4: 1, 16: 1}


def _sub_rows(ref, r, dil, cols):
    if dil == 1:
        return ref[:, cols]
    return ref[pl.ds(r, BLK, stride=dil), cols]


def _set_sub_rows(ref, r, dil, cols, val):
    if dil == 1:
        ref[:, cols] = val
    else:
        ref[pl.ds(r, BLK, stride=dil), cols] = val


def _slopes(dil):
    hp = HEADS_PER_STEP[dil]
    s = jnp.asarray([dil * 2.0 ** (-(h + 1)) for h in range(H)], F32)
    return jnp.broadcast_to(s[:, None], (H, DH)).reshape(H // hp, hp, DH)


def _rms_rows(x, w):
    rs = lax.rsqrt(jnp.mean(x * x, axis=-1, keepdims=True) + EPS)
    return x * rs, rs


def _att_masks():
    qi = lax.broadcasted_iota(jnp.int32, (BLK, BLK), 0)
    kj = lax.broadcasted_iota(jnp.int32, (BLK, BLK), 1)
    steps_c = qi - kj
    steps_p = qi - kj + BLK
    return steps_c, steps_p, steps_c >= 0, steps_p <= BLK


def _qk_prep(proj, q_w, k_w):
    def body(q_ref, k_ref, qw_ref, kw_ref, qn_ref, kn_ref):
        for h in range(H):
            cols = slice(h * DH, (h + 1) * DH)
            qn_ref[:, cols] = _rms_rows(q_ref[:, cols], None)[0] * qw_ref[...]
            kn_ref[:, cols] = _rms_rows(k_ref[:, cols], None)[0] * kw_ref[...]

    out = jax.ShapeDtypeStruct((S, HW), F32)
    wspec = pl.BlockSpec((1, DH), lambda i: (0, 0))
    return pl.pallas_call(
        body, name="qk_prep", out_shape=(out, out), grid=(S // TR,),
        in_specs=[pl.BlockSpec((TR, HW), lambda i: (i, 4)), pl.BlockSpec((TR, HW), lambda i: (i, 5)), wspec, wspec],
        out_specs=(_row_spec(HW), _row_spec(HW)),
        compiler_params=_params(1),
    )(proj, proj, q_w, k_w)


def _qk_norm_bwd(proj, dqn, dkn, dv, q_w, k_w):
    def body(q_ref, k_ref, dqn_ref, dkn_ref, dv_ref, qw_ref, kw_ref, dq_ref, dk_ref, dvo_ref, dqw_ref, dkw_ref):
        i = pl.program_id(0)

        @pl.when(i == 0)
        def _():
            dqw_ref[...] = jnp.zeros_like(dqw_ref)
            dkw_ref[...] = jnp.zeros_like(dkw_ref)

        dvo_ref[...] = dv_ref[...].astype(BF16)
        for x_ref, d_ref, w_ref, o_ref, dw_ref in ((q_ref, dqn_ref, qw_ref, dq_ref, dqw_ref),
                                                   (k_ref, dkn_ref, kw_ref, dk_ref, dkw_ref)):
            for h in range(H):
                cols = slice(h * DH, (h + 1) * DH)
                xh, rs = _rms_rows(x_ref[:, cols], None)
                d = d_ref[:, cols]
                dw_ref[...] += jnp.sum(d * xh, axis=0, keepdims=True)
                dy = d * w_ref[...]
                o_ref[:, cols] = (rs * (dy - xh * jnp.mean(dy * xh, axis=-1, keepdims=True))).astype(BF16)

    big = jax.ShapeDtypeStruct((S, HW), BF16)
    small = jax.ShapeDtypeStruct((1, DH), F32)
    wspec = pl.BlockSpec((1, DH), lambda i: (0, 0))
    return pl.pallas_call(
        body, name="qk_norm_bwd", out_shape=(big, big, big, small, small), grid=(S // TR,),
        in_specs=[pl.BlockSpec((TR, HW), lambda i: (i, 4)), pl.BlockSpec((TR, HW), lambda i: (i, 5)),
                  _row_spec(HW), _row_spec(HW), _row_spec(HW), wspec, wspec],
        out_specs=(_row_spec(HW), _row_spec(HW), _row_spec(HW), wspec, wspec),
        compiler_params=_params(1),
    )(proj, proj, dqn, dkn, dv, q_w, k_w)


def _attn_delta(do, o):
    def body(do_ref, o_ref, d_ref):
        for h in range(H):
            cols = slice(h * DH, (h + 1) * DH)
            d_ref[:, cols] = jnp.broadcast_to(jnp.sum(do_ref[:, cols] * o_ref[:, cols], axis=-1, keepdims=True), (TR, DH))

    return pl.pallas_call(
        body, name="attn_delta", out_shape=jax.ShapeDtypeStruct((S, HW), F32), grid=(S // TR,),
        in_specs=[pl.BlockSpec((TR, HW), lambda i: (i, 1)), _row_spec(HW)], out_specs=_row_spec(HW),
        compiler_params=_params(1),
    )(do, o)


def _attn_fwd(proj, qn, kn, dil, phase=None):
    hp = HEADS_PER_STEP[dil]
    rows, ng, nb = BLK * dil, H // hp, S // (BLK * dil)

    def body(q_ref, kc_ref, kp_ref, vc_ref, vp_ref, sl_ref, o_ref, lse_ref):
        n = pl.program_id(0)
        steps_c, steps_p, ok_c, ok_p = _att_masks()
        ok_p = jnp.logical_and(ok_p, n > 0)
        fc, fp = steps_c.astype(F32), steps_p.astype(F32)
        for hh in range(hp):
            cols = slice(hh * DH, (hh + 1) * DH)
            sl = sl_ref[0, hh:hh + 1, :]
            for r in range(dil):
                sub = lambda ref: _sub_rows(ref, r, dil, cols)
                qv = sub(q_ref)
                sc = jnp.where(ok_c, _dot(qv, sub(kc_ref), NT) * ATT_SCALE - sl * fc, NEG)
                sp = jnp.where(ok_p, _dot(qv, sub(kp_ref), NT) * ATT_SCALE - sl * fp, NEG)
                m = jnp.maximum(jnp.max(sc, axis=-1, keepdims=True), jnp.max(sp, axis=-1, keepdims=True))
                pc, pp = jnp.exp(sc - m), jnp.exp(sp - m)
                den = jnp.sum(pc, axis=-1, keepdims=True) + jnp.sum(pp, axis=-1, keepdims=True)
                o = (_dot(pc, sub(vc_ref)) + _dot(pp, sub(vp_ref))) / den
                _set_sub_rows(o_ref, r, dil, cols, o)
                _set_sub_rows(lse_ref, r, dil, cols, jnp.broadcast_to(m + jnp.log(den), (BLK, DH)))

    cur = pl.BlockSpec((rows, hp * DH), lambda n, g: (n, g))
    prev = pl.BlockSpec((rows, hp * DH), lambda n, g: (jnp.maximum(n - 1, 0), g))
    vcur = pl.BlockSpec((rows, hp * DH), lambda n, g: (n, 6 * ng + g))
    vprev = pl.BlockSpec((rows, hp * DH), lambda n, g: (jnp.maximum(n - 1, 0), 6 * ng + g))
    out = jax.ShapeDtypeStruct((S, HW), F32)
    return _pcall(
        body, [qn, kn, kn, proj, proj, _slopes(dil)], phase,
        name=f"attn_fwd_d{dil}", out_shape=(out, out), grid=(nb, ng),
        in_specs=[cur, cur, prev, vcur, vprev, pl.BlockSpec((1, hp, DH), lambda n, g: (g, 0, 0))],
        out_specs=(cur, cur),
        compiler_params=_params(2),
    )


def _attn_merge(outs, lses, cat):
    def body(o1, o2, o3, l1, l2, l3, cat_ref, ob_ref, of_ref, lse_ref):
        a, b, c = l1[...], l2[...], l3[...]
        m = jnp.maximum(jnp.maximum(a, b), c)
        ea, eb, ec = jnp.exp(a - m), jnp.exp(b - m), jnp.exp(c - m)
        den = ea + eb + ec
        o = (ea * o1[...] + eb * o2[...] + ec * o3[...]) / den
        ob_ref[...] = o.astype(BF16)
        of_ref[...] = o
        lse_ref[...] = m + jnp.log(den)

    f = jax.ShapeDtypeStruct((S, HW), F32)
    return pl.pallas_call(
        body, name="attn_merge", out_shape=(jax.ShapeDtypeStruct((S, 2 * HW), BF16), f, f), grid=(S // TR,),
        in_specs=[_row_spec(HW)] * 6 + [pl.BlockSpec(memory_space=pl.ANY)],
        out_specs=(pl.BlockSpec((TR, HW), lambda i: (i, 1)), _row_spec(HW), _row_spec(HW)),
        input_output_aliases={6: 0},
        compiler_params=_params(1),
    )(*outs, *lses, cat)


def _attn_bwd(proj, qn, kn, lse, delta, do, dil, acc, phase=None):
    hp = HEADS_PER_STEP[dil]
    rows, ng, nb = BLK * dil, H // hp, S // (BLK * dil)
    has_acc = acc is not None

    def body(*refs):
        q_ref, qn_ref, kp_ref, kc_ref, vp_ref, vc_ref, l_ref, ln_ref, d_ref, dn_ref, do_ref, don_ref, sl_ref = refs[:13]
        refs = refs[13:]
        if has_acc:
            aq_ref, ak_ref, av_ref = refs[:3]
            refs = refs[3:]
        dq_ref, dk_ref, dv_ref = refs
        m = pl.program_id(0)
        steps_c, steps_p, ok_c, ok_p = _att_masks()
        ok_prev = jnp.logical_and(ok_p, m > 0)
        ok_next = jnp.logical_and(ok_p, m < nb - 1)
        fc, fp = steps_c.astype(F32), steps_p.astype(F32)
        for hh in range(hp):
            cols = slice(hh * DH, (hh + 1) * DH)
            sl = sl_ref[0, hh:hh + 1, :]
            for r in range(dil):
                sub = lambda ref: _sub_rows(ref, r, dil, cols)
                qv, qnv, kcv, kpv = sub(q_ref), sub(qn_ref), sub(kc_ref), sub(kp_ref)
                vc, vp = sub(vc_ref), sub(vp_ref)
                dov, donv = sub(do_ref), sub(don_ref)
                lse_m, lse_n = sub(l_ref)[:, 0:1], sub(ln_ref)[:, 0:1]
                delta_m, delta_n = sub(d_ref)[:, 0:1], sub(dn_ref)[:, 0:1]
                p_c = jnp.where(ok_c, jnp.exp(_dot(qv, kcv, NT) * ATT_SCALE - sl * fc - lse_m), 0.0)
                p_p = jnp.where(ok_prev, jnp.exp(_dot(qv, kpv, NT) * ATT_SCALE - sl * fp - lse_m), 0.0)
                p_n = jnp.where(ok_next, jnp.exp(_dot(qnv, kcv, NT) * ATT_SCALE - sl * fp - lse_n), 0.0)
                ds_c = p_c * (_dot(dov, vc, NT) - delta_m)
                ds_p = p_p * (_dot(dov, vp, NT) - delta_m)
                ds_n = p_n * (_dot(donv, vc, NT) - delta_n)
                dqn = (_dot(ds_c, kcv) + _dot(ds_p, kpv)) * ATT_SCALE
                dkn = (_dot(ds_c, qv, TN) + _dot(ds_n, qnv, TN)) * ATT_SCALE
                dv = _dot(p_c, dov, TN) + _dot(p_n, donv, TN)
                if has_acc:
                    dqn, dkn, dv = dqn + sub(aq_ref), dkn + sub(ak_ref), dv + sub(av_ref)
                _set_sub_rows(dq_ref, r, dil, cols, dqn)
                _set_sub_rows(dk_ref, r, dil, cols, dkn)
                _set_sub_rows(dv_ref, r, dil, cols, dv)

    def shifted(shift, m):
        if shift < 0:
            return jnp.maximum(m - 1, 0)
        if shift > 0:
            return jnp.minimum(m + 1, nb - 1)
        return m

    def spec(shift, group=0):
        return pl.BlockSpec((rows, hp * DH), lambda m, g: (shifted(shift, m), group * ng + g))

    ins = [qn, qn, kn, kn, proj, proj, lse, lse, delta, delta, do, do, _slopes(dil)]
    in_specs = [spec(0), spec(1), spec(-1), spec(0), spec(-1, 6), spec(0, 6), spec(0), spec(1), spec(0), spec(1),
                spec(0, 1), spec(1, 1), pl.BlockSpec((1, hp, DH), lambda m, g: (g, 0, 0))]
    if has_acc:
        ins += list(acc)
        in_specs += [spec(0)] * 3
    big = jax.ShapeDtypeStruct((S, HW), F32)
    return _pcall(
        body, ins, phase, name=f"attn_bwd_d{dil}", out_shape=(big, big, big), grid=(nb, ng),
        in_specs=in_specs, out_specs=(spec(0),) * 3,
        compiler_params=_params(2),
    )


TC = 512
NCT = DFF // TC


def _shift_rows(a, k):
    rows = lax.broadcasted_iota(jnp.int32, a.shape, 0)
    rolled = pltpu.roll(a, k % S, 0)
    if k > 0:
        return jnp.where(rows >= k, rolled, 0.0)
    return jnp.where(rows < S + k, rolled, 0.0)


def _conv_pre(a, w_ref, b_ref):
    return b_ref[...] + w_ref[0:1, :] * _shift_rows(a, 2) + w_ref[1:2, :] * _shift_rows(a, 1) + w_ref[2:3, :] * a


def _conv_gate_fwd(u, conv_w, conv_b, phase=None):
    def body(a_ref, g_ref, w_ref, b_ref, y_ref):
        pre = _conv_pre(a_ref[...].astype(F32), w_ref, b_ref)
        y_ref[...] = (pre * _sigmoid(pre) * g_ref[...].astype(F32)).astype(BF16)

    return _pcall(
        body, [u, u, conv_w, conv_b], phase,
        name="conv_gate_fwd", out_shape=jax.ShapeDtypeStruct((S, DFF), BF16), grid=(NCT,),
        in_specs=[pl.BlockSpec((S, TC), lambda i: (0, i)), pl.BlockSpec((S, TC), lambda i: (0, NCT + i)),
                  pl.BlockSpec((3, TC), lambda i: (0, i)), pl.BlockSpec((1, TC), lambda i: (0, i))],
        out_specs=pl.BlockSpec((S, TC), lambda i: (0, i)),
        compiler_params=_params(1),
    )


def _conv_gate_bwd(dy, u, conv_w, conv_b, phase=None):
    def body(dy_ref, a_ref, g_ref, w_ref, b_ref, da_ref, dg_ref, db_ref, dw_ref):
        a = a_ref[...].astype(F32)
        dyv = dy_ref[...].astype(F32)
        pre = _conv_pre(a, w_ref, b_ref)
        sg = _sigmoid(pre)
        dg_ref[...] = (dyv * pre * sg).astype(BF16)
        dpre = dyv * g_ref[...].astype(F32) * (sg * (1.0 + pre * (1.0 - sg)))
        da = w_ref[2:3, :] * dpre + w_ref[1:2, :] * _shift_rows(dpre, -1) + w_ref[0:1, :] * _shift_rows(dpre, -2)
        da_ref[...] = da.astype(BF16)
        db_ref[...] = jnp.sum(dpre, axis=0, keepdims=True)
        dw_ref[0:1, :] = jnp.sum(dpre * _shift_rows(a, 2), axis=0, keepdims=True)
        dw_ref[1:2, :] = jnp.sum(dpre * _shift_rows(a, 1), axis=0, keepdims=True)
        dw_ref[2:3, :] = jnp.sum(dpre * a, axis=0, keepdims=True)

    col = pl.BlockSpec((S, TC), lambda i: (0, i))
    half = jax.ShapeDtypeStruct((S, DFF), BF16)
    return _pcall(
        body, [dy, u, u, conv_w, conv_b], phase, name="conv_gate_bwd",
        out_shape=(half, half, jax.ShapeDtypeStruct((1, DFF), F32), jax.ShapeDtypeStruct((3, DFF), F32)),
        grid=(NCT,),
        in_specs=[col, col, pl.BlockSpec((S, TC), lambda i: (0, NCT + i)),
                  pl.BlockSpec((3, TC), lambda i: (0, i)), pl.BlockSpec((1, TC), lambda i: (0, i))],
        out_specs=(col, col, pl.BlockSpec((1, TC), lambda i: (0, i)), pl.BlockSpec((3, TC), lambda i: (0, i))),
        compiler_params=_params(1),
    )


def _out_loss(z, x1, target, gate):
    def body(z_ref, x_ref, t_ref, g_ref, do_ref, dz_ref, dg_ref, loss_ref):
        i = pl.program_id(0)
        zv = z_ref[...]
        gv = g_ref[...]
        err = x_ref[...] + gv * zv - t_ref[...]
        dout = err * (1.0 / D)
        do_ref[...] = dout
        dz_ref[...] = (dout * gv).astype(BF16)

        @pl.when(i == 0)
        def _():
            dg_ref[...] = jnp.zeros_like(dg_ref)
            loss_ref[...] = jnp.zeros_like(loss_ref)

        dg_ref[...] += jnp.sum(dout * zv, axis=0, keepdims=True)
        part = jnp.sum(jnp.sum(err * err, axis=0, keepdims=True), axis=-1, keepdims=True) * (0.5 / D)
        lane = lax.broadcasted_iota(jnp.int32, (1, 128), 1)
        loss_ref[...] += jnp.where(lane == 0, part, 0.0)

    return pl.pallas_call(
        body, name="out_loss",
        out_shape=(jax.ShapeDtypeStruct((S, D), F32), jax.ShapeDtypeStruct((S, D), BF16),
                   jax.ShapeDtypeStruct((1, D), F32), jax.ShapeDtypeStruct((1, 128), F32)),
        grid=(S // TR,),
        in_specs=[_row_spec(), _row_spec(), _row_spec(), _vec_spec()],
        out_specs=(_row_spec(), _row_spec(), _vec_spec(), _vec_spec(128)),
        compiler_params=_params(1),
    )(z, x1, target, gate)


ADA_COLS = 6 * D // N_CHIPS
TA = 512


def _ada_fwd(c_all, w_shard, b_shard):
    def body(c_ref, w_ref, b_ref, o_ref):
        cv = c_ref[...]
        o_ref[...] = _dot_f32(cv * _sigmoid(cv), w_ref[...]) + b_ref[...]

    return pl.pallas_call(
        body, name="ada_fwd", out_shape=jax.ShapeDtypeStruct((N_DEV, ADA_COLS), F32), grid=(ADA_COLS // TA,),
        in_specs=[pl.BlockSpec((N_DEV, D), lambda j: (0, 0)), pl.BlockSpec((D, TA), lambda j: (0, j)),
                  pl.BlockSpec((1, TA), lambda j: (0, j))],
        out_specs=pl.BlockSpec((N_DEV, TA), lambda j: (0, j)),
        compiler_params=_params(1),
    )(c_all, w_shard, b_shard)


def _ada_bwd(c_all, dmod_shard):
    def body(c_ref, d_ref, o_ref):
        cv = c_ref[...]
        o_ref[...] = lax.dot_general(cv * _sigmoid(cv), d_ref[...], TN, precision=lax.Precision.HIGHEST,
                                     preferred_element_type=F32)

    return pl.pallas_call(
        body, name="ada_bwd", out_shape=jax.ShapeDtypeStruct((D, ADA_COLS), F32), grid=(ADA_COLS // TA,),
        in_specs=[pl.BlockSpec((N_DEV, D), lambda j: (0, 0)), pl.BlockSpec((N_DEV, TA), lambda j: (0, j))],
        out_specs=pl.BlockSpec((D, TA), lambda j: (0, j)),
        compiler_params=_params(1),
    )(c_all, dmod_shard)


def _sum_rows(g, name):
    rows, n = g.shape

    def body(g_ref, o_ref):
        acc = g_ref[0:1, :]
        for i in range(1, rows):
            acc = acc + g_ref[i:i + 1, :]
        o_ref[...] = acc

    return pl.pallas_call(
        body, name=name, out_shape=jax.ShapeDtypeStruct((1, n), F32),
        in_specs=[VMEM_SPEC], out_specs=VMEM_SPEC,
    )(g)


def _lb_grad(lb_logits, dlb):
    def body(l_ref, d_ref, o_ref):
        p = 1.0 / (1.0 + jnp.exp(l_ref[1:2, :] - l_ref[0:1, :]))
        t = d_ref[...] * p * (1.0 - p)
        o_ref[0:1, :] = t
        o_ref[1:2, :] = -t

    return pl.pallas_call(
        body, name="lb_grad", out_shape=jax.ShapeDtypeStruct((2, HW), F32),
        in_specs=[VMEM_SPEC, VMEM_SPEC], out_specs=VMEM_SPEC,
    )(lb_logits, dlb)


def _adamw(w, g, m, v, name):
    rows, cols = w.shape
    tr = rows
    if rows * cols * 4 > ADAM_BLOCK_BYTES:
        tr = _pick(rows, [t for t in (256, 128, 64, 32, 16, 8) if t * cols * 4 <= ADAM_BLOCK_BYTES])

    def body(w_ref, g_ref, m_ref, v_ref, d_ref, mo_ref, vo_ref):
        gv = g_ref[...]
        m2 = ADAM_B1 * m_ref[...] + (1.0 - ADAM_B1) * gv
        v2 = ADAM_B2 * v_ref[...] + (1.0 - ADAM_B2) * (gv * gv)
        m_hat = m2 / (1.0 - ADAM_B1 ** ADAM_STEP)
        v_hat = v2 / (1.0 - ADAM_B2 ** ADAM_STEP)
        d_ref[...] = -ADAM_LR * (m_hat / (jnp.sqrt(v_hat) + ADAM_EPS) + ADAM_WD * w_ref[...])
        mo_ref[...] = m2
        vo_ref[...] = v2

    spec = pl.BlockSpec((tr, cols), lambda i: (i, 0))
    out = jax.ShapeDtypeStruct((rows, cols), F32)
    return pl.pallas_call(
        body, name=name, out_shape=(out, out, out), grid=(rows // tr,),
        in_specs=[spec] * 4, out_specs=(spec,) * 3,
        compiler_params=_params(1),
    )(w, g, m, v)


W_IN, W_OUT, W_UP, W_DOWN = range(4)
IN_CHUNKS = 4
W_INQ = 4


def _join_row_chunks(chunks):
    return jnp.concatenate(chunks, axis=0)


def _sequence_step(x, target, mod, norm1_w, lb_logits, hg_norm_w, q_norm_w, k_norm_w, norm2_w, conv_w, conv_b, net):
    shift1, scale1, gate1, shift2, scale2, gate2 = [mod[:, i * D:(i + 1) * D] for i in range(6)]

    def run(name, fn, *args, **kw):
        phase = net.host(name)
        if phase is None:
            return fn(*args, **kw)
        res, comm = fn(*args, phase=phase, **kw)
        net.done(name, comm)
        return res

    h = _norm_mod(x, norm1_w, scale1, shift1, "norm1_fwd")
    proj = run("proj_fwd", _matmul, h, net.full[W_IN], "nn", F32, "proj_fwd")
    cat, o_raw, states = run("hgrn_fwd", _hgrn_fwd, proj, lb_logits, hg_norm_w)
    qn, kn = _qk_prep(proj, q_norm_w, k_norm_w)
    att = [run(f"attn_fwd_d{dil}", _attn_fwd, proj, qn, kn, dil) for dil in DILS]
    cat, b_out_f32, lse = _attn_merge([t[0] for t in att], [t[1] for t in att], cat)
    mix = run("mix_fwd", _matmul, cat, net.full[W_OUT], "nn", F32, "mix_fwd")
    x1, h2 = _resid_norm_mod(x, mix, gate1, norm2_w, scale2, shift2, "norm2_fwd")
    u = run("up_fwd", _matmul, h2, net.full[W_UP], "nn", BF16, "up_fwd")
    yact = run("conv_gate_fwd", _conv_gate_fwd, u, conv_w, conv_b)
    net.alone("before_down_fwd")
    z = _matmul(yact, net.full[W_DOWN], "nn", F32, "down_fwd")
    dout, dz, dgate2, loss_row = _out_loss(z, x1, target, gate2)

    net.grad[W_DOWN] = _matmul(yact, dz, "tn", BF16, "down_bwd_w")
    dyact = run("down_bwd_x", _matmul, dz, net.full[W_DOWN], "nt", BF16, "down_bwd_x")
    da, dg, dconv_b, dconv_w = run("conv_gate_bwd", _conv_gate_bwd, dyact, u, conv_w, conv_b)
    du = jnp.concatenate([da, dg], axis=1)
    net.grad[W_UP] = run("up_bwd_w", _matmul, h2, du, "tn", BF16, "up_bwd_w")
    dh2 = run("up_bwd_x", _matmul, du, net.full[W_UP], "nt", F32, "up_bwd_x")
    dx1, dshift2, dscale2, dnorm2, dgate1, dmix = run(
        "norm2_bwd", _norm_mod_bwd, dh2, x1, norm2_w, scale2, dout, "norm2_bwd", mix=mix, gate=gate1)
    net.grad[W_OUT] = run("mix_bwd_w", _matmul, cat, dmix, "tn", BF16, "mix_bwd_w")
    dcat = run("mix_bwd_x", _matmul, dmix, net.full[W_OUT], "nt", F32, "mix_bwd_x")
    dhq, dhf, dhi, dhg, dlb, dhg_norm = run("hgrn_bwd", _hgrn_bwd, proj, lb_logits, hg_norm_w, o_raw, states, dcat)
    delta = _attn_delta(dcat, b_out_f32)
    acc = None
    for dil in DILS:
        acc = run(f"attn_bwd_d{dil}", _attn_bwd, proj, qn, kn, lse, delta, dcat, dil, acc)
    daq, dak, dav, dq_norm, dk_norm = _qk_norm_bwd(proj, *acc, q_norm_w, k_norm_w)
    dproj = jnp.concatenate([dhq, dhf, dhi, dhg, daq, dak, dav], axis=1)
    for q in range(IN_CHUNKS):
        net.grad[W_INQ + q] = run(f"proj_bwd_w_{q}", _matmul, h, dproj, "tn", BF16, f"proj_bwd_w_{q}",
                                  m_blocks=(D // IN_CHUNKS, [q]))
    dh = run("proj_bwd_x", _matmul, dproj, net.full[W_IN], "nt", F32, "proj_bwd_x")
    grad_x, dshift1, dscale1, dnorm1 = run("norm1_bwd", _norm_mod_bwd, dh, x, norm1_w, scale1, dx1, "norm1_bwd")

    dmod = jnp.concatenate([dshift1, dscale1, dgate1, dshift2, dscale2, dgate2], axis=1)
    small = dict(norm1_w=dnorm1, lb=dlb, hg_norm_w=dhg_norm, q_norm_w=dq_norm, k_norm_w=dk_norm,
                 norm2_w=dnorm2, conv_b=dconv_b, conv_w=dconv_w)
    return loss_row, grad_x, dmod, small


def _place():
    x, y, c = lax.axis_index("x"), lax.axis_index("y"), lax.axis_index("c")
    others = [(1 - x, y), (x, 1 - y), (1 - x, 1 - y)]
    return x, y, c, others


def _remote(src, dst, send_sems, recv_sems, k, to):
    return pltpu.make_async_remote_copy(src_ref=src, dst_ref=dst, send_sem=send_sems.at[k], recv_sem=recv_sems.at[k],
                                        device_id=to, device_id_type=MESH)


class _Phase:
    def __init__(self):
        self.ins, self.outs, self.aliases, self.groups, self.n = [], [], {}, [], 0

    def add(self, ins, outs, aliases, n, copies):
        i0, o0 = len(self.ins), len(self.outs)
        self.ins += list(ins)
        self.outs += list(outs)
        self.aliases.update({i0 + i: o0 + o for i, o in aliases.items()})
        self.groups.append((slice(i0, i0 + len(ins)), slice(o0, o0 + len(outs)), copies))
        self.n += n
        return slice(o0, o0 + len(outs))

    def build(self, cin, cout, send_sems, recv_sems, landing):
        res = []
        for si, so, copies in self.groups:
            for src, dst, land, peer in copies(cin[si], cout[so]):
                k = len(res)
                res.append(_remote(land, land, send_sems, recv_sems, k, peer) if landing
                           else _remote(src, dst, send_sems, recv_sems, k, peer))
        assert len(res) == self.n
        return res


def _pcall(body, args, phase=None, **kw):
    if phase is None or not phase.groups:
        res = pl.pallas_call(body, **kw)(*args)
        return res if phase is None else (res, ())
    grid = tuple(kw.pop("grid", ()))
    out_shape, out_specs = kw.pop("out_shape"), kw.pop("out_specs")
    single = not isinstance(out_shape, (tuple, list))
    out_shape = [out_shape] if single else list(out_shape)
    out_specs = [out_specs] if single else list(out_specs)
    scratch = list(kw.pop("scratch_shapes", ()))
    n_in, n_out, n_ci, n_co = len(args), len(out_shape), len(phase.ins), len(phase.outs)

    def hosted(*refs):
        ins, cin = refs[:n_in], refs[n_in:n_in + n_ci]
        outs = refs[n_in + n_ci:n_in + n_ci + n_out]
        cout = refs[n_in + n_ci + n_out:n_in + n_ci + n_out + n_co]
        scr, send_sems, recv_sems = refs[n_in + n_ci + n_out + n_co:-2], refs[-2], refs[-1]
        ids = [pl.program_id(a) for a in range(len(grid))]

        def start():
            for cp in phase.build(cin, cout, send_sems, recv_sems, False):
                cp.start()

        def finish():
            for cp in phase.build(cin, cout, send_sems, recv_sems, False):
                cp.wait_send()
            for cp in phase.build(cin, cout, send_sems, recv_sems, True):
                cp.wait_recv()

        if grid:
            first = functools.reduce(jnp.logical_and, [i == 0 for i in ids])
            last = functools.reduce(jnp.logical_and, [i == g - 1 for i, g in zip(ids, grid)])
            pl.when(first)(start)
            body(*ins, *outs, *scr)
            pl.when(last)(finish)
        else:
            start()
            body(*ins, *outs, *scr)
            finish()

    res = pl.pallas_call(
        hosted, out_shape=tuple(out_shape + phase.outs), grid=grid,
        in_specs=list(kw.pop("in_specs")) + [HBM_SPEC] * n_ci, out_specs=tuple(out_specs + [HBM_SPEC] * n_co),
        input_output_aliases={n_in + i: n_out + o for i, o in phase.aliases.items()},
        scratch_shapes=scratch + [pltpu.SemaphoreType.DMA((phase.n,)), pltpu.SemaphoreType.DMA((phase.n,))],
        **kw,
    )(*args, *phase.ins)
    outs = res[:n_out]
    return (outs[0] if single else tuple(outs)), tuple(res[n_out:])


def _comm_only(name, phase):
    return _pcall(lambda: None, [], phase, name=name, out_shape=(), in_specs=[], out_specs=())[1]


def _all_gather_rows(block, name):
    m_per, n = block.shape

    def body(x_ref, out_ref, send_sems, recv_sems, local_sem):
        x, y, c, chips = _place()
        me, sibling = (x, y, c), (x, y, 1 - c)

        def rows(px, py, pc):
            return out_ref.at[pl.ds((4 * px + 2 * py + pc) * m_per, m_per), :]

        def copy(k, blk, to, src=None):
            return _remote(rows(*blk) if src is None else src, rows(*blk), send_sems, recv_sems, k, to)

        mine = pltpu.make_async_copy(x_ref, rows(*me), local_sem)
        mine.start()
        first = [copy(0, me, sibling, src=x_ref)]
        first += [copy(1 + j, me, (*chip, c), src=x_ref) for j, chip in enumerate(chips)]
        for cp in first:
            cp.start()
        passed = [copy(4 + j, (*chip, c), sibling) for j, chip in enumerate(chips)]
        for j, chip in enumerate(chips):
            copy(1 + j, (*chip, c), me).wait_recv()
            passed[j].start()
        copy(0, sibling, me).wait_recv()
        for j, chip in enumerate(chips):
            copy(4 + j, (*chip, 1 - c), me).wait_recv()
        for cp in first + passed:
            cp.wait_send()
        mine.wait()

    return pl.pallas_call(
        body, name=name, out_shape=jax.ShapeDtypeStruct((N_DEV * m_per, n), block.dtype),
        in_specs=[VMEM_SPEC], out_specs=VMEM_SPEC,
        scratch_shapes=[pltpu.SemaphoreType.DMA((7,)), pltpu.SemaphoreType.DMA((7,)), pltpu.SemaphoreType.DMA],
    )(block)


class _Geo:
    def __init__(self, kind, shard_shape):
        self.kind = kind
        self.shard_shape = tuple(shard_shape)
        self.hr, self.hc = shard_shape[0] // 2, shard_shape[1]
        if kind == "col":
            self.full_shape = (shard_shape[0], N_CHIPS * shard_shape[1])
        else:
            self.full_shape = (N_CHIPS * shard_shape[0], shard_shape[1])

    def full_shard(self, ref, j):
        if self.kind == "col":
            return ref.at[:, pl.ds(pl.multiple_of(j * self.hc, 128), self.hc)]
        return ref.at[pl.ds(pl.multiple_of(j * 2 * self.hr, 16), 2 * self.hr), :]

    def full_half(self, ref, j, c):
        if self.kind == "col":
            return ref.at[pl.ds(pl.multiple_of(c * self.hr, 16), self.hr),
                          pl.ds(pl.multiple_of(j * self.hc, 128), self.hc)]
        return ref.at[pl.ds(pl.multiple_of((2 * j + c) * self.hr, 16), self.hr), :]

    def piece(self, ref, j, c, p0, p1, pieces, part=None):
        pr = self.hr // pieces
        off, n = p0 * pr, (p1 - p0) * pr
        if part is not None:
            top = (n // 32) * 16
            off, n = (off, top) if part == 0 else (off + top, n - top)
        if self.kind == "col":
            return ref.at[pl.ds(pl.multiple_of(c * self.hr + off, 16), n),
                          pl.ds(pl.multiple_of(j * self.hc, 128), self.hc)]
        return ref.at[pl.ds(pl.multiple_of((2 * j + c) * self.hr + off, 16), n), :]


WEIGHT_KINDS = ("col", "row", "col", "row")
NW = len(WEIGHT_KINDS)


def _comm_call(body, name, ins, outs, n_remote, aliases=None):
    return pl.pallas_call(
        body, name=name, out_shape=tuple(outs),
        in_specs=[HBM_SPEC] * len(ins), out_specs=tuple([HBM_SPEC] * len(outs)),
        input_output_aliases=aliases or {},
        scratch_shapes=[pltpu.SemaphoreType.DMA((n_remote,)), pltpu.SemaphoreType.DMA((n_remote,))],
    )(*ins)


ROW_TILES = (256, 176, 128, 64, 32, 16)


def _cast_into_full(shard, geo, place, name):
    rs, cs = shard.shape
    tr = _pick(rs, ROW_TILES)
    nb = rs // tr

    def body(place_ref, s_ref, o_ref):
        o_ref[...] = s_ref[...].astype(BF16)

    if geo.kind == "col":
        out_map = lambda i, place_ref: (i, place_ref[0])
    else:
        out_map = lambda i, place_ref: (place_ref[0] * nb + i, 0)
    return pl.pallas_call(
        body, name=name, out_shape=jax.ShapeDtypeStruct(geo.full_shape, BF16),
        grid_spec=pltpu.PrefetchScalarGridSpec(
            num_scalar_prefetch=1, grid=(nb,),
            in_specs=[pl.BlockSpec((tr, cs), lambda i, place_ref: (i, 0))],
            out_specs=pl.BlockSpec((tr, cs), out_map)),
        compiler_params=_params(1),
    )(place, shard)


def _gather_weight(full, geo, pieces, name):
    per = 8

    def body(in_ref, ref, send_sems, recv_sems):
        x, y, c, _ = _place()
        j, jx, jy, jo = 2 * x + y, 2 * (1 - x) + y, 2 * x + (1 - y), 2 * (1 - x) + (1 - y)
        nx, ny, sib = (1 - x, y, c), (x, 1 - y, c), (x, y, 1 - c)

        def cp(region, k, to):
            return _remote(region, region, send_sems, recv_sems, k, to)

        def reg(chip, p, part=None, core=c):
            return geo.piece(ref, chip, core, p, p + 1, pieces, part)

        sent = []
        for p in range(pieces):
            sent += [cp(reg(j, p), per * p, nx), cp(reg(j, p), per * p + 1, ny)]
        for d in sent:
            d.start()
        for p in range(pieces):
            cp(reg(jx, p), per * p, nx).wait_recv()
            new = [cp(reg(jx, p, 0), per * p + 2, ny), cp(reg(jx, p), per * p + 4, sib)]
            cp(reg(jy, p), per * p + 1, ny).wait_recv()
            new += [cp(reg(jy, p, 1), per * p + 3, nx), cp(reg(jy, p), per * p + 5, sib)]
            for d in new:
                d.start()
            sent += new
        for p in range(pieces):
            cp(reg(jo, p, 0), per * p + 2, ny).wait_recv()
            cp(reg(jo, p, 1), per * p + 3, nx).wait_recv()
            new = [cp(reg(jo, p, 0), per * p + 6, sib), cp(reg(jo, p, 1), per * p + 7, sib)]
            for d in new:
                d.start()
            sent += new
        for p in range(pieces):
            cp(reg(jx, p, None, 1 - c), per * p + 4, sib).wait_recv()
            cp(reg(jy, p, None, 1 - c), per * p + 5, sib).wait_recv()
            cp(reg(jo, p, 0, 1 - c), per * p + 6, sib).wait_recv()
            cp(reg(jo, p, 1, 1 - c), per * p + 7, sib).wait_recv()
        for d in sent:
            d.wait_send()

    return _comm_call(body, name, [full], [_same(full)], per * pieces, aliases={0: 0})[0]


def _same(a):
    return jax.ShapeDtypeStruct(a.shape, a.dtype)


def _gather_ici(full, geo, p0, p1, pieces):
    def copies(ins, outs):
        x, y, c, _ = _place()
        mine = geo.piece(outs[0], 2 * x + y, c, p0, p1, pieces)
        return [(mine, mine, geo.piece(outs[0], 2 * ox + oy, c, p0, p1, pieces), (ox, oy, c))
                for ox, oy in ((1 - x, y), (x, 1 - y))]

    return dict(ins=[full], outs=[_same(full)], aliases={0: 0}, n=2, copies=copies)


def _gather_relay(full, geo, p0, p1, pieces):
    def copies(ins, outs):
        x, y, c, _ = _place()
        jx, jy, jo = 2 * (1 - x) + y, 2 * x + (1 - y), 2 * (1 - x) + (1 - y)
        reg = lambda chip, part: geo.piece(outs[0], chip, c, p0, p1, pieces, part)
        return [(reg(jx, 0), reg(jx, 0), reg(jo, 0), (x, 1 - y, c)), (reg(jy, 1), reg(jy, 1), reg(jo, 1), (1 - x, y, c))]

    return dict(ins=[full], outs=[_same(full)], aliases={0: 0}, n=2, copies=copies)


def _gather_d2d(full, geo):
    def copies(ins, outs):
        x, y, c, chips = _place()
        return [(geo.full_half(outs[0], 2 * ox + oy, c), geo.full_half(outs[0], 2 * ox + oy, c),
                 geo.full_half(outs[0], 2 * ox + oy, 1 - c), (x, y, 1 - c)) for ox, oy in chips]

    return dict(ins=[full], outs=[_same(full)], aliases={0: 0}, n=3, copies=copies)


def _swap_d2d(grad, geo):
    def copies(ins, outs):
        x, y, c, _ = _place()
        return [(geo.full_half(ins[0], j, 1 - c), outs[0].at[j], outs[0].at[j], (x, y, 1 - c)) for j in range(N_CHIPS)]

    return dict(ins=[grad], outs=[jax.ShapeDtypeStruct((N_CHIPS, geo.hr, geo.hc), BF16)], aliases={}, n=N_CHIPS,
                copies=copies)


def _scatter_ici(pair, recv, geo, p0, p1, pieces):
    pr = geo.hr // pieces
    rows = pl.ds(p0 * pr, (p1 - p0) * pr)

    def copies(ins, outs):
        x, y, c, chips = _place()
        return [(ins[0].at[2 * ox + oy, rows, :], outs[0].at[k, rows, :], outs[0].at[k, rows, :], (ox, oy, c))
                for k, (ox, oy) in enumerate(chips)]

    out = jax.ShapeDtypeStruct((3, geo.hr, geo.hc), BF16)
    if recv is None:
        return dict(ins=[pair], outs=[out], aliases={}, n=3, copies=copies)
    return dict(ins=[pair, recv], outs=[out], aliases={1: 0}, n=3, copies=copies)


def _join_d2d(shard, geo, row0=0):
    def copies(ins, outs):
        x, y, c, _ = _place()
        mine = outs[0].at[pl.ds(pl.multiple_of(row0 + c * geo.hr, 8), geo.hr), :]
        other = outs[0].at[pl.ds(pl.multiple_of(row0 + (1 - c) * geo.hr, 8), geo.hr), :]
        return [(mine, mine, other, (x, y, 1 - c))]

    return dict(ins=[shard], outs=[_same(shard)], aliases={0: 0}, n=1, copies=copies)


def _add_pair(grad, got, geo, place, name):
    tr = _pick(geo.hr, ROW_TILES)
    nb = geo.hr // tr

    def body(place_ref, a_ref, b_ref, o_ref):
        o_ref[0] = (a_ref[...].astype(F32) + b_ref[0].astype(F32)).astype(BF16)

    if geo.kind == "col":
        own_map = lambda j, i, place_ref: (place_ref[1] * nb + i, j)
    else:
        own_map = lambda j, i, place_ref: ((2 * j + place_ref[1]) * nb + i, 0)
    spec = pl.BlockSpec((1, tr, geo.hc), lambda j, i, place_ref: (j, i, 0))
    return pl.pallas_call(
        body, name=name, out_shape=jax.ShapeDtypeStruct((N_CHIPS, geo.hr, geo.hc), BF16),
        grid_spec=pltpu.PrefetchScalarGridSpec(
            num_scalar_prefetch=1, grid=(N_CHIPS, nb),
            in_specs=[pl.BlockSpec((tr, geo.hc), own_map), spec], out_specs=spec),
        compiler_params=_params(2),
    )(place, grad, got)


def _add_four(pair, recv, geo, place, name, rows=None, row0=0, into=None):
    tr = _pick(geo.hr, ROW_TILES)
    nb = geo.hr // tr
    rows = 2 * geo.hr if rows is None else rows

    def body(place_ref, p_ref, r_ref, *rest):
        rest[-1][...] = ((p_ref[0].astype(F32) + r_ref[0].astype(F32)) + r_ref[1].astype(F32)) + r_ref[2].astype(F32)

    in_specs = [pl.BlockSpec((1, tr, geo.hc), lambda i, place_ref: (place_ref[0], i, 0)),
                pl.BlockSpec((3, tr, geo.hc), lambda i, place_ref: (0, i, 0))]
    args = [place, pair, recv]
    if into is not None:
        in_specs.append(pl.BlockSpec(memory_space=pl.ANY))
        args.append(into)
    return pl.pallas_call(
        body, name=name, out_shape=jax.ShapeDtypeStruct((rows, geo.hc), F32),
        grid_spec=pltpu.PrefetchScalarGridSpec(
            num_scalar_prefetch=1, grid=(nb,), in_specs=in_specs,
            out_specs=pl.BlockSpec((tr, geo.hc), lambda i, place_ref: (row0 // tr + place_ref[1] * nb + i, 0))),
        input_output_aliases={3: 0} if into is not None else {},
        compiler_params=_params(1),
    )(*args)


PIECES = (4, 1, 16, 8) + (1,) * IN_CHUNKS
SCHEDULE = {
    "proj_fwd": (("gather_ici", W_OUT, 0, 1), ("gather_ici", W_UP, 0, 6)),
    "hgrn_fwd": (("gather_relay", W_OUT, 0, 1), ("gather_relay", W_UP, 0, 6), ("gather_ici", W_UP, 6, 10)),
    "attn_fwd_d1": (("gather_ici", W_UP, 10, 12), ("gather_relay", W_UP, 6, 10)),
    "attn_fwd_d4": (("gather_ici", W_UP, 12, 16), ("gather_relay", W_UP, 10, 12), ("gather_d2d", W_OUT)),
    "attn_fwd_d16": (("gather_relay", W_UP, 12, 16), ("gather_ici", W_DOWN, 0, 2)),
    "mix_fwd": (("gather_d2d", W_UP), ("gather_ici", W_DOWN, 2, 4)),
    "up_fwd": (("gather_ici", W_DOWN, 4, 8), ("gather_relay", W_DOWN, 0, 4)),
    "conv_gate_fwd": (("gather_relay", W_DOWN, 4, 8),),
    "before_down_fwd": (("gather_d2d", W_DOWN),),
    "down_bwd_x": (("swap", W_DOWN),),
    "conv_gate_bwd": (("scatter", W_DOWN, 0, 4),),
    "up_bwd_w": (("scatter", W_DOWN, 4, 8),),
    "up_bwd_x": (("swap", W_UP),),
    "norm2_bwd": (("scatter", W_UP, 0, 1),),
    "mix_bwd_w": (("scatter", W_UP, 1, 2),),
    "mix_bwd_x": (("swap", W_OUT), ("scatter", W_UP, 2, 3)),
    "hgrn_bwd": (("scatter", W_UP, 3, 9), ("join", W_DOWN)),
    "attn_bwd_d1": (("scatter", W_UP, 9, 12),),
    "attn_bwd_d4": (("scatter", W_UP, 12, 13), ("scatter", W_OUT, 0, 1)),
    "attn_bwd_d16": (("scatter", W_UP, 13, 16),),
    "proj_bwd_w_0": (("join", W_UP), ("join", W_OUT)),
    "proj_bwd_w_1": (("swap", W_INQ),),
    "proj_bwd_w_2": (("swap", W_INQ + 1),),
    "proj_bwd_w_3": (("swap", W_INQ + 2), ("scatter", W_INQ, 0, 1)),
    "proj_bwd_x": (("swap", W_INQ + 3), ("scatter", W_INQ + 1, 0, 1), ("scatter", W_INQ + 2, 0, 1)),
    "norm1_bwd": (("scatter", W_INQ + 3, 0, 1), ("join", W_INQ), ("join", W_INQ + 1), ("join", W_INQ + 2)),
    "grad_in_join": (("join", W_INQ + 3),),
}


class _Net:
    def __init__(self, shards, place):
        self.place = place
        self.geos = [_Geo(k, s.shape) for k, s in zip(WEIGHT_KINDS, shards)]
        self.full = [_cast_into_full(s, g, place, f"cast_shard_{w}") for w, (s, g) in enumerate(zip(shards, self.geos))]
        self.full[W_IN] = _gather_weight(self.full[W_IN], self.geos[W_IN], PIECES[W_IN], "gather_w_in")
        rows, cols = shards[W_IN].shape
        self.geos += [_Geo("col", (rows // IN_CHUNKS, cols))] * IN_CHUNKS
        n = NW + IN_CHUNKS
        self.grad, self.pair, self.recv, self.shard = [None] * n, [None] * n, [None] * n, [None] * n
        self.in_rows, self.in_shard = rows, None
        self.slots = []

    def _row0(self, w):
        return (w - W_INQ) * 2 * self.geos[w].hr

    def host(self, name):
        phase = _Phase()
        self.slots = []
        groups = {}
        for item in SCHEDULE.get(name, ()):
            kind, w = item[0], item[1]
            geo = self.geos[w]
            key = len(groups)
            if kind == "gather_ici":
                spec, key = _gather_ici(self.full[w], geo, item[2], item[3], PIECES[w]), ("full", w)
            elif kind == "gather_relay":
                spec, key = _gather_relay(self.full[w], geo, item[2], item[3], PIECES[w]), ("full", w)
            elif kind == "gather_d2d":
                spec, key = _gather_d2d(self.full[w], geo), ("full", w)
            elif kind == "swap":
                spec = _swap_d2d(self.grad[w], geo)
            elif kind == "scatter":
                spec, key = _scatter_ici(self.pair[w], self.recv[w], geo, item[2], item[3], PIECES[w]), ("recv", w)
            elif w >= W_INQ:
                spec, key = _join_d2d(self.in_shard, geo, self._row0(w)), "in_shard"
            else:
                spec = _join_d2d(self.shard[w], geo)
            groups.setdefault(key, []).append((item, spec))
        for group in groups.values():
            specs = [s for _, s in group]
            merged = dict(specs[0], n=sum(s["n"] for s in specs),
                          copies=lambda ins, outs, specs=specs: [t for s in specs for t in s["copies"](ins, outs)])
            self.slots.append(([item for item, _ in group], phase.add(**merged)))
        return phase

    def done(self, name, comm):
        for items, sl in sorted(self.slots, key=lambda s: s[0][0][0] == "scatter"):
            out = comm[sl][0]
            for item in items:
                kind, w = item[0], item[1]
                if kind in ("gather_ici", "gather_relay", "gather_d2d"):
                    self.full[w] = out
                elif kind == "swap":
                    self.pair[w] = _add_pair(self.grad[w], out, self.geos[w], self.place, f"grad_pair_sum_{w}")
                elif kind == "scatter":
                    self.recv[w] = out
                    if item[3] == PIECES[w] and w >= W_INQ:
                        self.in_shard = _add_four(self.pair[w], out, self.geos[w], self.place, f"grad_chip_sum_{w}",
                                                  rows=self.in_rows, row0=self._row0(w), into=self.in_shard)
                    elif item[3] == PIECES[w]:
                        self.shard[w] = _add_four(self.pair[w], out, self.geos[w], self.place, f"grad_chip_sum_{w}")
                elif w >= W_INQ:
                    self.in_shard = out
                else:
                    self.shard[w] = out

    def alone(self, name):
        self.done(name, _comm_only(name, self.host(name)))


CONVW_SHARD = 3 * DFF // N_CHIPS
COND_PAD = 8 * 896
SMALL_SIZES = (("norm1_w", D), ("lb", HW), ("hg_norm_w", DH), ("q_norm_w", DH), ("k_norm_w", DH),
               ("norm2_w", D), ("conv_b", DFF), ("conv_w", 3 * DFF), ("loss", 128))
SMALL_TOTAL = sum(n for _, n in SMALL_SIZES)
STATS_PAD = 8 * 5120


def kernel(x, c, w_ada, b_ada, norm1_w, w_in, lb_logits, hg_norm_w, q_norm_w, k_norm_w, w_out, norm2_w, w_up, conv_w, conv_b, w_down, loss_target, m_w_ada, m_b_ada, m_norm1_w, m_w_in, m_lb_logits, m_hg_norm_w, m_q_norm_w, m_k_norm_w, m_w_out, m_norm2_w, m_w_up, m_conv_w, m_conv_b, m_w_down, v_w_ada, v_b_ada, v_norm1_w, v_w_in, v_lb_logits, v_hg_norm_w, v_q_norm_w, v_k_norm_w, v_w_out, v_norm2_w, v_w_up, v_conv_w, v_conv_b, v_w_down):
    chip = 2 * lax.axis_index("x") + lax.axis_index("y")
    dev = 2 * chip + lax.axis_index("c")

    cond = jnp.concatenate([c, conv_w[0].reshape(1, CONVW_SHARD), jnp.zeros((1, COND_PAD - D - CONVW_SHARD), F32)], axis=1)
    cond_all = _all_gather_rows(cond.reshape(8, COND_PAD // 8), "gather_cond").reshape(N_DEV, COND_PAD)
    c_all = cond_all[:, :D]
    conv_w_full = jnp.concatenate(
        [cond_all[2 * j, D:D + CONVW_SHARD].reshape(3, DFF // N_CHIPS) for j in range(N_CHIPS)], axis=1)

    b_shard = lax.dynamic_slice_in_dim(b_ada, chip * ADA_COLS, ADA_COLS, axis=1)
    mod_cols = _all_gather_rows(_ada_fwd(c_all, w_ada[0], b_shard), "gather_mod")
    mod_all = jnp.concatenate([mod_cols[16 * j:16 * j + 8] for j in range(N_CHIPS)], axis=1)
    mod = lax.dynamic_slice_in_dim(mod_all, dev, 1, axis=0)

    place = jnp.stack([chip, lax.axis_index("c")]).astype(jnp.int32)
    net = _Net([w_in[0], w_out[0], w_up[0], w_down[0]], place)
    loss_row, grad_x, dmod, small = _sequence_step(
        x[0], loss_target[0], mod, norm1_w, lb_logits, hg_norm_w, q_norm_w, k_norm_w, norm2_w, conv_w_full, conv_b, net)
    net.alone("grad_in_join")
    g_out, g_up, g_down = net.shard[W_OUT], net.shard[W_UP], net.shard[W_DOWN]
    g_in = net.in_shard

    small["conv_w"] = small["conv_w"].reshape(1, 3 * DFF)
    small["loss"] = loss_row
    stats = jnp.concatenate([dmod] + [small[k] for k, _ in SMALL_SIZES]
                            + [jnp.zeros((1, STATS_PAD - 6 * D - SMALL_TOTAL), F32)], axis=1)
    stats_all = _all_gather_rows(stats.reshape(8, STATS_PAD // 8), "gather_stats").reshape(N_DEV, STATS_PAD)
    dmod_all = stats_all[:, :6 * D]
    sums = _sum_rows(stats_all[:, 6 * D:6 * D + SMALL_TOTAL], "small_grad_sum")
    g_small, off = {}, 0
    for k, n in SMALL_SIZES:
        g_small[k] = sums[:, off:off + n]
        off += n
    loss = g_small["loss"][0, 0]
    g_b_ada = _sum_rows(dmod_all, "b_ada_grad_sum")
    g_w_ada = _ada_bwd(c_all, lax.dynamic_slice_in_dim(dmod_all, chip * ADA_COLS, ADA_COLS, axis=1))
    g_lb = _lb_grad(lb_logits, g_small["lb"])
    g_conv_w = lax.dynamic_slice_in_dim(g_small["conv_w"].reshape(3, DFF), chip * (DFF // N_CHIPS), DFF // N_CHIPS, axis=1)

    names = ["w_ada", "b_ada", "norm1_w", "w_in", "lb_logits", "hg_norm_w", "q_norm_w", "k_norm_w", "w_out",
             "norm2_w", "w_up", "conv_w", "conv_b", "w_down"]
    lead = {"w_ada", "w_in", "w_out", "w_up", "conv_w", "w_down"}
    w = dict(w_ada=w_ada, b_ada=b_ada, norm1_w=norm1_w, w_in=w_in, lb_logits=lb_logits, hg_norm_w=hg_norm_w,
             q_norm_w=q_norm_w, k_norm_w=k_norm_w, w_out=w_out, norm2_w=norm2_w, w_up=w_up, conv_w=conv_w,
             conv_b=conv_b, w_down=w_down)
    m = dict(w_ada=m_w_ada, b_ada=m_b_ada, norm1_w=m_norm1_w, w_in=m_w_in, lb_logits=m_lb_logits,
             hg_norm_w=m_hg_norm_w, q_norm_w=m_q_norm_w, k_norm_w=m_k_norm_w, w_out=m_w_out, norm2_w=m_norm2_w,
             w_up=m_w_up, conv_w=m_conv_w, conv_b=m_conv_b, w_down=m_w_down)
    v = dict(w_ada=v_w_ada, b_ada=v_b_ada, norm1_w=v_norm1_w, w_in=v_w_in, lb_logits=v_lb_logits,
             hg_norm_w=v_hg_norm_w, q_norm_w=v_q_norm_w, k_norm_w=v_k_norm_w, w_out=v_w_out, norm2_w=v_norm2_w,
             w_up=v_w_up, conv_w=v_conv_w, conv_b=v_conv_b, w_down=v_w_down)
    g = dict(w_ada=g_w_ada, b_ada=g_b_ada, norm1_w=g_small["norm1_w"], w_in=g_in, lb_logits=g_lb,
             hg_norm_w=g_small["hg_norm_w"], q_norm_w=g_small["q_norm_w"], k_norm_w=g_small["k_norm_w"], w_out=g_out,
             norm2_w=g_small["norm2_w"], w_up=g_up, conv_w=g_conv_w, conv_b=g_small["conv_b"], w_down=g_down)

    grads, deltas, new_m, new_v = [], [], [], []
    for n in names:
        strip = (lambda t: t[0]) if n in lead else (lambda t: t)
        wrap = (lambda t: t[None]) if n in lead else (lambda t: t)
        d_n, m_n, v_n = _adamw(strip(w[n]), g[n], strip(m[n]), strip(v[n]), f"adamw_{n}")
        grads.append(wrap(g[n]))
        deltas.append(wrap(d_n))
        new_m.append(wrap(m_n))
        new_v.append(wrap(v_n))
    return (loss, grad_x[None], *grads, *deltas, *new_m, *new_v)
```

```python
import functools
import math

import jax
import jax.numpy as jnp
from jax import lax
from jax.experimental import pallas as pl
from jax.experimental.pallas import tpu as pltpu
from jax.experimental.pallas import tpu_sc as plsc

F32 = jnp.float32
BF16 = jnp.bfloat16

S = 2048
D = 2048
H = 8
DH = 128
HW = H * DH
CH = 64
NCH = S // CH
DFF = 5632
INC = 7 * HW
BLK = 128
DILS = (1, 4, 16)
EPS = 1e-6
NEG = -1e30
N_CHIPS = 4
N_DEV = 8

ADAM_LR = 0.001
ADAM_B1 = 0.9
ADAM_B2 = 0.999
ADAM_EPS = 1e-08
ADAM_WD = 0.01
ADAM_STEP = 10
ADAM_BLOCK_BYTES = 1 << 20

V7X_VMEM_BYTES = 64 * 1024 * 1024
VMEM_LIMIT = (V7X_VMEM_BYTES * 3) // 4
MESH = pl.DeviceIdType.MESH
HBM_SPEC = pl.BlockSpec(memory_space=pltpu.HBM)
VMEM_SPEC = pl.BlockSpec(memory_space=pltpu.VMEM)

NN = (((1,), (0,)), ((), ()))
NT = (((1,), (1,)), ((), ()))
TN = (((0,), (0,)), ((), ()))


def _params(n_grid):
    return pltpu.CompilerParams(dimension_semantics=("arbitrary",) * n_grid, vmem_limit_bytes=VMEM_LIMIT)


def _dot(a, b, dims=NN):
    return lax.dot_general(a.astype(BF16), b.astype(BF16), dims, preferred_element_type=F32)


def _dot_f32(a, b):
    return lax.dot_general(a, b, NN, precision=lax.Precision.HIGHEST, preferred_element_type=F32)


def _sigmoid(x):
    return 1.0 / (1.0 + jnp.exp(-x))


def _pick(n, cands):
    for t in cands:
        if n % t == 0:
            return t
    raise ValueError(n)


def _matmul(a, b, mode, out_dtype, name, phase=None, m_blocks=None):
    if mode == "nn":
        (m, k), (_, n) = a.shape, b.shape
    elif mode == "nt":
        (m, k), (n, _) = a.shape, b.shape
    else:
        (k, m), (_, n) = a.shape, b.shape
    tm = _pick(m, (1024, 1408, 512))
    a_block = lambda i: i
    if m_blocks is not None:
        tm, blocks = m_blocks
        m = tm * len(blocks)
        step = blocks[1] - blocks[0] if len(blocks) > 1 else 0
        assert all(blk == blocks[0] + step * i for i, blk in enumerate(blocks))
        a_block = lambda i: blocks[0] + step * i
    tn = _pick(n, (1024, 1408, 512))
    tk = _pick(k, (2048, 2816, 1792, 1024, 512))
    nk = k // tk
    dims = {"nn": NN, "nt": NT, "tn": TN}[mode]

    def body(a_ref, b_ref, o_ref, *acc):
        part = lax.dot_general(a_ref[...], b_ref[...], dims, preferred_element_type=F32)
        if nk == 1:
            o_ref[...] = part.astype(o_ref.dtype)
            return
        acc_ref, kk = acc[0], pl.program_id(2)

        @pl.when(kk == 0)
        def _():
            acc_ref[...] = part

        @pl.when(jnp.logical_and(kk > 0, kk < nk - 1))
        def _():
            acc_ref[...] += part

        @pl.when(kk == nk - 1)
        def _():
            o_ref[...] = (acc_ref[...] + part).astype(o_ref.dtype)

    if mode == "tn":
        a_spec = pl.BlockSpec((tk, tm), lambda i, j, kk: (kk, a_block(i)))
    else:
        a_spec = pl.BlockSpec((tm, tk), lambda i, j, kk: (i, kk))
    if mode == "nt":
        b_spec = pl.BlockSpec((tn, tk), lambda i, j, kk: (j, kk))
    else:
        b_spec = pl.BlockSpec((tk, tn), lambda i, j, kk: (kk, j))
    return _pcall(
        body, [a, b], phase, name=name,
        out_shape=jax.ShapeDtypeStruct((m, n), out_dtype),
        grid=(m // tm, n // tn, nk),
        in_specs=[a_spec, b_spec],
        out_specs=pl.BlockSpec((tm, tn), lambda i, j, kk: (i, j)),
        scratch_shapes=[pltpu.VMEM((tm, tn), F32)] if nk > 1 else [],
        compiler_params=_params(3),
    )


TR = 256


def _row_spec(cols=D):
    return pl.BlockSpec((TR, cols), lambda i: (i, 0))


def _vec_spec(cols=D):
    return pl.BlockSpec((1, cols), lambda i: (0, 0))


def _norm_mod(x, nw, scale, shift, name):
    def body(x_ref, nw_ref, sc_ref, sh_ref, h_ref):
        xv = x_ref[...]
        r = lax.rsqrt(jnp.mean(xv * xv, axis=-1, keepdims=True) + EPS)
        h_ref[...] = ((xv * r) * nw_ref[...] * (1.0 + sc_ref[...]) + sh_ref[...]).astype(BF16)

    return pl.pallas_call(
        body, name=name, out_shape=jax.ShapeDtypeStruct((S, D), BF16), grid=(S // TR,),
        in_specs=[_row_spec(), _vec_spec(), _vec_spec(), _vec_spec()], out_specs=_row_spec(),
        compiler_params=_params(1),
    )(x, nw, scale, shift)


def _resid_norm_mod(x, mix, gate, nw, scale, shift, name):
    def body(x_ref, mix_ref, g_ref, nw_ref, sc_ref, sh_ref, x1_ref, h_ref):
        xv = x_ref[...] + g_ref[...] * mix_ref[...]
        x1_ref[...] = xv
        r = lax.rsqrt(jnp.mean(xv * xv, axis=-1, keepdims=True) + EPS)
        h_ref[...] = ((xv * r) * nw_ref[...] * (1.0 + sc_ref[...]) + sh_ref[...]).astype(BF16)

    return pl.pallas_call(
        body, name=name,
        out_shape=(jax.ShapeDtypeStruct((S, D), F32), jax.ShapeDtypeStruct((S, D), BF16)), grid=(S // TR,),
        in_specs=[_row_spec(), _row_spec(), _vec_spec(), _vec_spec(), _vec_spec(), _vec_spec()],
        out_specs=(_row_spec(), _row_spec()),
        compiler_params=_params(1),
    )(x, mix, gate, nw, scale, shift)


def _norm_mod_bwd(dh, xin, nw, scale, dres, name, mix=None, gate=None, phase=None):
    with_gate = mix is not None

    def body(*refs):
        if with_gate:
            dh_ref, x_ref, nw_ref, sc_ref, dres_ref, mix_ref, g_ref, dx_ref, dsh_ref, dsc_ref, dnw_ref, dg_ref, dmix_ref = refs
        else:
            dh_ref, x_ref, nw_ref, sc_ref, dres_ref, dx_ref, dsh_ref, dsc_ref, dnw_ref = refs
        i = pl.program_id(0)
        dhv = dh_ref[...]
        xv = x_ref[...]
        r = lax.rsqrt(jnp.mean(xv * xv, axis=-1, keepdims=True) + EPS)
        xn = xv * r
        nwv = nw_ref[...]
        one_sc = 1.0 + sc_ref[...]
        dxn = dhv * nwv * one_sc
        dx = dres_ref[...] + r * (dxn - xn * jnp.mean(dxn * xn, axis=-1, keepdims=True))
        dx_ref[...] = dx

        @pl.when(i == 0)
        def _():
            dsh_ref[...] = jnp.zeros_like(dsh_ref)
            dsc_ref[...] = jnp.zeros_like(dsc_ref)
            dnw_ref[...] = jnp.zeros_like(dnw_ref)
            if with_gate:
                dg_ref[...] = jnp.zeros_like(dg_ref)

        dsh_ref[...] += jnp.sum(dhv, axis=0, keepdims=True)
        dsc_ref[...] += jnp.sum(dhv * xn * nwv, axis=0, keepdims=True)
        dnw_ref[...] += jnp.sum(dhv * xn * one_sc, axis=0, keepdims=True)
        if with_gate:
            dg_ref[...] += jnp.sum(dx * mix_ref[...], axis=0, keepdims=True)
            dmix_ref[...] = (dx * g_ref[...]).astype(BF16)

    vec = jax.ShapeDtypeStruct((1, D), F32)
    ins = [dh, xin, nw, scale, dres]
    in_specs = [_row_spec(), _row_spec(), _vec_spec(), _vec_spec(), _row_spec()]
    outs = [jax.ShapeDtypeStruct((S, D), F32), vec, vec, vec]
    out_specs = [_row_spec(), _vec_spec(), _vec_spec(), _vec_spec()]
    if with_gate:
        ins += [mix, gate]
        in_specs += [_row_spec(), _vec_spec()]
        outs += [vec, jax.ShapeDtypeStruct((S, D), BF16)]
        out_specs += [_vec_spec(), _row_spec()]
    return _pcall(
        body, ins, phase, name=name, out_shape=tuple(outs), grid=(S // TR,), in_specs=in_specs,
        out_specs=tuple(out_specs), compiler_params=_params(1),
    )


def _tri(lower):
    row = lax.broadcasted_iota(jnp.int32, (CH, CH), 0)
    col = lax.broadcasted_iota(jnp.int32, (CH, CH), 1)
    return (row >= col) if lower else (col >= row)


def _hgrn_gates(hq, hf, lb):
    sig = _sigmoid(hf)
    f = lb + (1.0 - lb) * sig
    g = jnp.log(f)
    sq = _sigmoid(hq)
    return sig, f, g, 1.0 - f, hq * sq, sq


HG_HP = 2
HG_UNROLL = 4


def _proj_col_spec(group):
    return pl.BlockSpec((S, HG_HP * DH), lambda h: (0, group * (H // HG_HP) + h))


def _hgrn_fwd(proj, lb_logits, norm_w, phase=None):
    def body(hq_ref, hf_ref, hi_ref, hg_ref, lbl_ref, nw_ref, out_ref, oraw_ref, st_ref, s_ref):
        nw = nw_ref[...]
        lower = _tri(True)
        ltri = lower.astype(F32)
        s_ref[...] = jnp.zeros_like(s_ref)

        def chunk(n, carry):
            rows = pl.ds(pl.multiple_of(n * CH, CH), CH)
            for j in range(HG_HP):
                cols = slice(j * DH, (j + 1) * DH)
                lb = 1.0 / (1.0 + jnp.exp(lbl_ref[1:2, cols] - lbl_ref[0:1, cols]))
                hq, hf, v, hg = hq_ref[rows, cols], hf_ref[rows, cols], hi_ref[rows, cols], hg_ref[rows, cols]
                _, _, g, kk, q, _ = _hgrn_gates(hq, hf, lb)
                gc = _dot_f32(ltri, g)
                gl = jnp.sum(g, axis=0, keepdims=True)
                qe = q * jnp.exp(gc)
                ke = kk * jnp.exp(gl - gc)
                qh = q * jnp.exp(gc - 0.5 * gl)
                kh = kk * jnp.exp(0.5 * gl - gc)
                st = s_ref[j]
                st_ref[j, pl.ds(n, 1)] = st[None]
                a = jnp.where(lower, _dot(qh, kh, NT), 0.0)
                o = _dot(qe, st, NT) + _dot(a, v)
                s_ref[j] = st * jnp.exp(gl) + _dot(v, ke, TN)
                oraw_ref[rows, cols] = o
                rs = lax.rsqrt(jnp.mean(o * o, axis=-1, keepdims=True) + EPS)
                out_ref[rows, cols] = (o * rs * nw * (hg * _sigmoid(hg))).astype(BF16)
            return carry

        lax.fori_loop(0, NCH, chunk, 0, unroll=HG_UNROLL)

    head_spec = pl.BlockSpec((S, HG_HP * DH), lambda h: (0, h))
    return _pcall(
        body, [proj, proj, proj, proj, lb_logits, norm_w], phase, name="hgrn_fwd",
        out_shape=(jax.ShapeDtypeStruct((S, 2 * HW), BF16), jax.ShapeDtypeStruct((S, HW), F32),
                   jax.ShapeDtypeStruct((H, NCH, DH, DH), F32)),
        grid=(H // HG_HP,),
        in_specs=[_proj_col_spec(0), _proj_col_spec(1), _proj_col_spec(2), _proj_col_spec(3),
                  pl.BlockSpec((2, HG_HP * DH), lambda h: (0, h)), pl.BlockSpec((1, DH), lambda h: (0, 0))],
        out_specs=(head_spec, head_spec, pl.BlockSpec((HG_HP, NCH, DH, DH), lambda h: (h, 0, 0, 0))),
        scratch_shapes=[pltpu.VMEM((HG_HP, DH, DH), F32)],
        compiler_params=_params(1),
    )


def _hgrn_bwd(proj, lb_logits, norm_w, oraw, states, dout, phase=None):
    def body(hq_ref, hf_ref, hi_ref, hg_ref, lbl_ref, nw_ref, oraw_ref, st_ref, do_ref,
             dhq_ref, dhf_ref, dhi_ref, dhg_ref, dlb_ref, dnw_ref, ds_ref, acc_ref):
        h = pl.program_id(0)
        nw = nw_ref[...]
        lower = _tri(True)
        ltri = lower.astype(F32)
        utri = _tri(False).astype(F32)
        last = lax.broadcasted_iota(jnp.int32, (CH, DH), 0) == CH - 1
        ds_ref[...] = jnp.zeros_like(ds_ref)
        acc_ref[...] = jnp.zeros_like(acc_ref)

        @pl.when(h == 0)
        def _():
            dnw_ref[...] = jnp.zeros_like(dnw_ref)

        def chunk(i, carry):
            n = NCH - 1 - i
            rows = pl.ds(pl.multiple_of(n * CH, CH), CH)
            for j in range(HG_HP):
                cols = slice(j * DH, (j + 1) * DH)
                lb = 1.0 / (1.0 + jnp.exp(lbl_ref[1:2, cols] - lbl_ref[0:1, cols]))
                hq, hf, v, hg = hq_ref[rows, cols], hf_ref[rows, cols], hi_ref[rows, cols], hg_ref[rows, cols]
                sig, f, g, kk, q, sq = _hgrn_gates(hq, hf, lb)
                gc = _dot_f32(ltri, g)
                gl = jnp.sum(g, axis=0, keepdims=True)
                eg = jnp.exp(gc)
                ek = jnp.exp(gl - gc)
                gam = jnp.exp(gl)
                ph = jnp.exp(gc - 0.5 * gl)
                pk = jnp.exp(0.5 * gl - gc)
                qe, ke = q * eg, kk * ek
                qh, kh = q * ph, kk * pk
                st = st_ref[j, pl.ds(n, 1)][0]
                dst = ds_ref[j]
                a = jnp.where(lower, _dot(qh, kh, NT), 0.0)
                o = oraw_ref[rows, cols]
                rs = lax.rsqrt(jnp.mean(o * o, axis=-1, keepdims=True) + EPS)
                xh = o * rs
                sg = _sigmoid(hg)
                dgo = do_ref[rows, cols]
                d_on = dgo * (hg * sg)
                dhg_ref[rows, cols] = (dgo * xh * nw * (sg * (1.0 + hg * (1.0 - sg)))).astype(BF16)
                acc_ref[j, 1:2, :] += jnp.sum(d_on * xh, axis=0, keepdims=True)
                dy = d_on * nw
                do = rs * (dy - xh * jnp.mean(dy * xh, axis=-1, keepdims=True))
                dqe = _dot(do, st)
                da = jnp.where(lower, _dot(do, v, NT), 0.0)
                dv = _dot(a, do, TN) + _dot(ke, dst, NT)
                dke = _dot(v, dst)
                dgam = jnp.sum(dst * st, axis=0, keepdims=True)
                dqh = _dot(da, kh)
                dkh = _dot(da, qh, TN)
                dq = dqh * ph + dqe * eg
                dk = dkh * pk + dke * ek
                dgl = jnp.sum(dke * ke, axis=0, keepdims=True) + dgam * gam
                dgc = dqh * qh - dkh * kh + dqe * qe - dke * ke + jnp.where(last, dgl, 0.0)
                dg = _dot_f32(utri, dgc)
                df = dg / f - dk
                dhf_ref[rows, cols] = (df * (1.0 - lb) * sig * (1.0 - sig)).astype(BF16)
                acc_ref[j, 0:1, :] += jnp.sum(df * (1.0 - sig), axis=0, keepdims=True)
                dhq_ref[rows, cols] = (dq * (sq * (1.0 + hq * (1.0 - sq)))).astype(BF16)
                dhi_ref[rows, cols] = dv.astype(BF16)
                ds_ref[j] = dst * gam + _dot(do, qe, TN)
            return carry

        lax.fori_loop(0, NCH, chunk, 0, unroll=HG_UNROLL)
        for j in range(HG_HP):
            dlb_ref[:, j * DH:(j + 1) * DH] = acc_ref[j, 0:1, :]
            dnw_ref[...] += acc_ref[j, 1:2, :]

    head_spec = pl.BlockSpec((S, HG_HP * DH), lambda h: (0, h))
    head_out = jax.ShapeDtypeStruct((S, HW), BF16)
    return _pcall(
        body, [proj, proj, proj, proj, lb_logits, norm_w, oraw, states, dout], phase, name="hgrn_bwd",
        out_shape=(head_out, head_out, head_out, head_out,
                   jax.ShapeDtypeStruct((1, HW), F32), jax.ShapeDtypeStruct((1, DH), F32)),
        grid=(H // HG_HP,),
        in_specs=[_proj_col_spec(0), _proj_col_spec(1), _proj_col_spec(2), _proj_col_spec(3),
                  pl.BlockSpec((2, HG_HP * DH), lambda h: (0, h)), pl.BlockSpec((1, DH), lambda h: (0, 0)),
                  head_spec, pl.BlockSpec((HG_HP, NCH, DH, DH), lambda h: (h, 0, 0, 0)), head_spec],
        out_specs=(head_spec, head_spec, head_spec, head_spec,
                   pl.BlockSpec((1, HG_HP * DH), lambda h: (0, h)), pl.BlockSpec((1, DH), lambda h: (0, 0))),
        scratch_shapes=[pltpu.VMEM((HG_HP, DH, DH), F32), pltpu.VMEM((HG_HP, 8, DH), F32)],
        compiler_params=_params(1),
    )


ATT_SCALE = DH ** -0.5
HEADS_PER_STEP = {1: 8, 4: 1, 16: 1}


def _sub_rows(ref, r, dil, cols):
    if dil == 1:
        return ref[:, cols]
    return ref[pl.ds(r, BLK, stride=dil), cols]


def _set_sub_rows(ref, r, dil, cols, val):
    if dil == 1:
        ref[:, cols] = val
    else:
        ref[pl.ds(r, BLK, stride=dil), cols] = val


def _slopes(dil):
    hp = HEADS_PER_STEP[dil]
    s = jnp.asarray([dil * 2.0 ** (-(h + 1)) for h in range(H)], F32)
    return jnp.broadcast_to(s[:, None], (H, DH)).reshape(H // hp, hp, DH)


def _rms_rows(x, w):
    rs = lax.rsqrt(jnp.mean(x * x, axis=-1, keepdims=True) + EPS)
    return x * rs, rs


def _att_masks():
    qi = lax.broadcasted_iota(jnp.int32, (BLK, BLK), 0)
    kj = lax.broadcasted_iota(jnp.int32, (BLK, BLK), 1)
    steps_c = qi - kj
    steps_p = qi - kj + BLK
    return steps_c, steps_p, steps_c >= 0, steps_p <= BLK


def _qk_prep(proj, q_w, k_w):
    def body(q_ref, k_ref, qw_ref, kw_ref, qn_ref, kn_ref):
        for h in range(H):
            cols = slice(h * DH, (h + 1) * DH)
            qn_ref[:, cols] = _rms_rows(q_ref[:, cols], None)[0] * qw_ref[...]
            kn_ref[:, cols] = _rms_rows(k_ref[:, cols], None)[0] * kw_ref[...]

    out = jax.ShapeDtypeStruct((S, HW), F32)
    wspec = pl.BlockSpec((1, DH), lambda i: (0, 0))
    return pl.pallas_call(
        body, name="qk_prep", out_shape=(out, out), grid=(S // TR,),
        in_specs=[pl.BlockSpec((TR, HW), lambda i: (i, 4)), pl.BlockSpec((TR, HW), lambda i: (i, 5)), wspec, wspec],
        out_specs=(_row_spec(HW), _row_spec(HW)),
        compiler_params=_params(1),
    )(proj, proj, q_w, k_w)


def _qk_norm_bwd(proj, dqn, dkn, dv, q_w, k_w):
    def body(q_ref, k_ref, dqn_ref, dkn_ref, dv_ref, qw_ref, kw_ref, dq_ref, dk_ref, dvo_ref, dqw_ref, dkw_ref):
        i = pl.program_id(0)

        @pl.when(i == 0)
        def _():
            dqw_ref[...] = jnp.zeros_like(dqw_ref)
            dkw_ref[...] = jnp.zeros_like(dkw_ref)

        dvo_ref[...] = dv_ref[...].astype(BF16)
        for x_ref, d_ref, w_ref, o_ref, dw_ref in ((q_ref, dqn_ref, qw_ref, dq_ref, dqw_ref),
                                                   (k_ref, dkn_ref, kw_ref, dk_ref, dkw_ref)):
            for h in range(H):
                cols = slice(h * DH, (h + 1) * DH)
                xh, rs = _rms_rows(x_ref[:, cols], None)
                d = d_ref[:, cols]
                dw_ref[...] += jnp.sum(d * xh, axis=0, keepdims=True)
                dy = d * w_ref[...]
                o_ref[:, cols] = (rs * (dy - xh * jnp.mean(dy * xh, axis=-1, keepdims=True))).astype(BF16)

    big = jax.ShapeDtypeStruct((S, HW), BF16)
    small = jax.ShapeDtypeStruct((1, DH), F32)
    wspec = pl.BlockSpec((1, DH), lambda i: (0, 0))
    return pl.pallas_call(
        body, name="qk_norm_bwd", out_shape=(big, big, big, small, small), grid=(S // TR,),
        in_specs=[pl.BlockSpec((TR, HW), lambda i: (i, 4)), pl.BlockSpec((TR, HW), lambda i: (i, 5)),
                  _row_spec(HW), _row_spec(HW), _row_spec(HW), wspec, wspec],
        out_specs=(_row_spec(HW), _row_spec(HW), _row_spec(HW), wspec, wspec),
        compiler_params=_params(1),
    )(proj, proj, dqn, dkn, dv, q_w, k_w)


def _attn_delta(do, o):
    def body(do_ref, o_ref, d_ref):
        for h in range(H):
            cols = slice(h * DH, (h + 1) * DH)
            d_ref[:, cols] = jnp.broadcast_to(jnp.sum(do_ref[:, cols] * o_ref[:, cols], axis=-1, keepdims=True), (TR, DH))

    return pl.pallas_call(
        body, name="attn_delta", out_shape=jax.ShapeDtypeStruct((S, HW), F32), grid=(S // TR,),
        in_specs=[pl.BlockSpec((TR, HW), lambda i: (i, 1)), _row_spec(HW)], out_specs=_row_spec(HW),
        compiler_params=_params(1),
    )(do, o)


def _attn_fwd(proj, qn, kn, dil, phase=None):
    hp = HEADS_PER_STEP[dil]
    rows, ng, nb = BLK * dil, H // hp, S // (BLK * dil)

    def body(q_ref, kc_ref, kp_ref, vc_ref, vp_ref, sl_ref, o_ref, lse_ref):
        n = pl.program_id(0)
        steps_c, steps_p, ok_c, ok_p = _att_masks()
        ok_p = jnp.logical_and(ok_p, n > 0)
        fc, fp = steps_c.astype(F32), steps_p.astype(F32)
        for hh in range(hp):
            cols = slice(hh * DH, (hh + 1) * DH)
            sl = sl_ref[0, hh:hh + 1, :]
            for r in range(dil):
                sub = lambda ref: _sub_rows(ref, r, dil, cols)
                qv = sub(q_ref)
                sc = jnp.where(ok_c, _dot(qv, sub(kc_ref), NT) * ATT_SCALE - sl * fc, NEG)
                sp = jnp.where(ok_p, _dot(qv, sub(kp_ref), NT) * ATT_SCALE - sl * fp, NEG)
                m = jnp.maximum(jnp.max(sc, axis=-1, keepdims=True), jnp.max(sp, axis=-1, keepdims=True))
                pc, pp = jnp.exp(sc - m), jnp.exp(sp - m)
                den = jnp.sum(pc, axis=-1, keepdims=True) + jnp.sum(pp, axis=-1, keepdims=True)
                o = (_dot(pc, sub(vc_ref)) + _dot(pp, sub(vp_ref))) / den
                _set_sub_rows(o_ref, r, dil, cols, o)
                _set_sub_rows(lse_ref, r, dil, cols, jnp.broadcast_to(m + jnp.log(den), (BLK, DH)))

    cur = pl.BlockSpec((rows, hp * DH), lambda n, g: (n, g))
    prev = pl.BlockSpec((rows, hp * DH), lambda n, g: (jnp.maximum(n - 1, 0), g))
    vcur = pl.BlockSpec((rows, hp * DH), lambda n, g: (n, 6 * ng + g))
    vprev = pl.BlockSpec((rows, hp * DH), lambda n, g: (jnp.maximum(n - 1, 0), 6 * ng + g))
    out = jax.ShapeDtypeStruct((S, HW), F32)
    return _pcall(
        body, [qn, kn, kn, proj, proj, _slopes(dil)], phase,
        name=f"attn_fwd_d{dil}", out_shape=(out, out), grid=(nb, ng),
        in_specs=[cur, cur, prev, vcur, vprev, pl.BlockSpec((1, hp, DH), lambda n, g: (g, 0, 0))],
        out_specs=(cur, cur),
        compiler_params=_params(2),
    )


def _attn_merge(outs, lses, cat):
    def body(o1, o2, o3, l1, l2, l3, cat_ref, ob_ref, of_ref, lse_ref):
        a, b, c = l1[...], l2[...], l3[...]
        m = jnp.maximum(jnp.maximum(a, b), c)
        ea, eb, ec = jnp.exp(a - m), jnp.exp(b - m), jnp.exp(c - m)
        den = ea + eb + ec
        o = (ea * o1[...] + eb * o2[...] + ec * o3[...]) / den
        ob_ref[...] = o.astype(BF16)
        of_ref[...] = o
        lse_ref[...] = m + jnp.log(den)

    f = jax.ShapeDtypeStruct((S, HW), F32)
    return pl.pallas_call(
        body, name="attn_merge", out_shape=(jax.ShapeDtypeStruct((S, 2 * HW), BF16), f, f), grid=(S // TR,),
        in_specs=[_row_spec(HW)] * 6 + [pl.BlockSpec(memory_space=pl.ANY)],
        out_specs=(pl.BlockSpec((TR, HW), lambda i: (i, 1)), _row_spec(HW), _row_spec(HW)),
        input_output_aliases={6: 0},
        compiler_params=_params(1),
    )(*outs, *lses, cat)


def _attn_bwd(proj, qn, kn, lse, delta, do, dil, acc, phase=None):
    hp = HEADS_PER_STEP[dil]
    rows, ng, nb = BLK * dil, H // hp, S // (BLK * dil)
    has_acc = acc is not None

    def body(*refs):
        q_ref, qn_ref, kp_ref, kc_ref, vp_ref, vc_ref, l_ref, ln_ref, d_ref, dn_ref, do_ref, don_ref, sl_ref = refs[:13]
        refs = refs[13:]
        if has_acc:
            aq_ref, ak_ref, av_ref = refs[:3]
            refs = refs[3:]
        dq_ref, dk_ref, dv_ref = refs
        m = pl.program_id(0)
        steps_c, steps_p, ok_c, ok_p = _att_masks()
        ok_prev = jnp.logical_and(ok_p, m > 0)
        ok_next = jnp.logical_and(ok_p, m < nb - 1)
        fc, fp = steps_c.astype(F32), steps_p.astype(F32)
        for hh in range(hp):
            cols = slice(hh * DH, (hh + 1) * DH)
            sl = sl_ref[0, hh:hh + 1, :]
            for r in range(dil):
                sub = lambda ref: _sub_rows(ref, r, dil, cols)
                qv, qnv, kcv, kpv = sub(q_ref), sub(qn_ref), sub(kc_ref), sub(kp_ref)
                vc, vp = sub(vc_ref), sub(vp_ref)
                dov, donv = sub(do_ref), sub(don_ref)
                lse_m, lse_n = sub(l_ref)[:, 0:1], sub(ln_ref)[:, 0:1]
                delta_m, delta_n = sub(d_ref)[:, 0:1], sub(dn_ref)[:, 0:1]
                p_c = jnp.where(ok_c, jnp.exp(_dot(qv, kcv, NT) * ATT_SCALE - sl * fc - lse_m), 0.0)
                p_p = jnp.where(ok_prev, jnp.exp(_dot(qv, kpv, NT) * ATT_SCALE - sl * fp - lse_m), 0.0)
                p_n = jnp.where(ok_next, jnp.exp(_dot(qnv, kcv, NT) * ATT_SCALE - sl * fp - lse_n), 0.0)
                ds_c = p_c * (_dot(dov, vc, NT) - delta_m)
                ds_p = p_p * (_dot(dov, vp, NT) - delta_m)
                ds_n = p_n * (_dot(donv, vc, NT) - delta_n)
                dqn = (_dot(ds_c, kcv) + _dot(ds_p, kpv)) * ATT_SCALE
                dkn = (_dot(ds_c, qv, TN) + _dot(ds_n, qnv, TN)) * ATT_SCALE
                dv = _dot(p_c, dov, TN) + _dot(p_n, donv, TN)
                if has_acc:
                    dqn, dkn, dv = dqn + sub(aq_ref), dkn + sub(ak_ref), dv + sub(av_ref)
                _set_sub_rows(dq_ref, r, dil, cols, dqn)
                _set_sub_rows(dk_ref, r, dil, cols, dkn)
                _set_sub_rows(dv_ref, r, dil, cols, dv)

    def shifted(shift, m):
        if shift < 0:
            return jnp.maximum(m - 1, 0)
        if shift > 0:
            return jnp.minimum(m + 1, nb - 1)
        return m

    def spec(shift, group=0):
        return pl.BlockSpec((rows, hp * DH), lambda m, g: (shifted(shift, m), group * ng + g))

    ins = [qn, qn, kn, kn, proj, proj, lse, lse, delta, delta, do, do, _slopes(dil)]
    in_specs = [spec(0), spec(1), spec(-1), spec(0), spec(-1, 6), spec(0, 6), spec(0), spec(1), spec(0), spec(1),
                spec(0, 1), spec(1, 1), pl.BlockSpec((1, hp, DH), lambda m, g: (g, 0, 0))]
    if has_acc:
        ins += list(acc)
        in_specs += [spec(0)] * 3
    big = jax.ShapeDtypeStruct((S, HW), F32)
    return _pcall(
        body, ins, phase, name=f"attn_bwd_d{dil}", out_shape=(big, big, big), grid=(nb, ng),
        in_specs=in_specs, out_specs=(spec(0),) * 3,
        compiler_params=_params(2),
    )


TC = 512
NCT = DFF // TC


def _shift_rows(a, k):
    rows = lax.broadcasted_iota(jnp.int32, a.shape, 0)
    rolled = pltpu.roll(a, k % S, 0)
    if k > 0:
        return jnp.where(rows >= k, rolled, 0.0)
    return jnp.where(rows < S + k, rolled, 0.0)


def _conv_pre(a, w_ref, b_ref):
    return b_ref[...] + w_ref[0:1, :] * _shift_rows(a, 2) + w_ref[1:2, :] * _shift_rows(a, 1) + w_ref[2:3, :] * a


def _conv_gate_fwd(u, conv_w, conv_b, phase=None):
    def body(a_ref, g_ref, w_ref, b_ref, y_ref):
        pre = _conv_pre(a_ref[...].astype(F32), w_ref, b_ref)
        y_ref[...] = (pre * _sigmoid(pre) * g_ref[...].astype(F32)).astype(BF16)

    return _pcall(
        body, [u, u, conv_w, conv_b], phase,
        name="conv_gate_fwd", out_shape=jax.ShapeDtypeStruct((S, DFF), BF16), grid=(NCT,),
        in_specs=[pl.BlockSpec((S, TC), lambda i: (0, i)), pl.BlockSpec((S, TC), lambda i: (0, NCT + i)),
                  pl.BlockSpec((3, TC), lambda i: (0, i)), pl.BlockSpec((1, TC), lambda i: (0, i))],
        out_specs=pl.BlockSpec((S, TC), lambda i: (0, i)),
        compiler_params=_params(1),
    )


def _conv_gate_bwd(dy, u, conv_w, conv_b, phase=None):
    def body(dy_ref, a_ref, g_ref, w_ref, b_ref, da_ref, dg_ref, db_ref, dw_ref):
        a = a_ref[...].astype(F32)
        dyv = dy_ref[...].astype(F32)
        pre = _conv_pre(a, w_ref, b_ref)
        sg = _sigmoid(pre)
        dg_ref[...] = (dyv * pre * sg).astype(BF16)
        dpre = dyv * g_ref[...].astype(F32) * (sg * (1.0 + pre * (1.0 - sg)))
        da = w_ref[2:3, :] * dpre + w_ref[1:2, :] * _shift_rows(dpre, -1) + w_ref[0:1, :] * _shift_rows(dpre, -2)
        da_ref[...] = da.astype(BF16)
        db_ref[...] = jnp.sum(dpre, axis=0, keepdims=True)
        dw_ref[0:1, :] = jnp.sum(dpre * _shift_rows(a, 2), axis=0, keepdims=True)
        dw_ref[1:2, :] = jnp.sum(dpre * _shift_rows(a, 1), axis=0, keepdims=True)
        dw_ref[2:3, :] = jnp.sum(dpre * a, axis=0, keepdims=True)

    col = pl.BlockSpec((S, TC), lambda i: (0, i))
    half = jax.ShapeDtypeStruct((S, DFF), BF16)
    return _pcall(
        body, [dy, u, u, conv_w, conv_b], phase, name="conv_gate_bwd",
        out_shape=(half, half, jax.ShapeDtypeStruct((1, DFF), F32), jax.ShapeDtypeStruct((3, DFF), F32)),
        grid=(NCT,),
        in_specs=[col, col, pl.BlockSpec((S, TC), lambda i: (0, NCT + i)),
                  pl.BlockSpec((3, TC), lambda i: (0, i)), pl.BlockSpec((1, TC), lambda i: (0, i))],
        out_specs=(col, col, pl.BlockSpec((1, TC), lambda i: (0, i)), pl.BlockSpec((3, TC), lambda i: (0, i))),
        compiler_params=_params(1),
    )


def _out_loss(z, x1, target, gate):
    def body(z_ref, x_ref, t_ref, g_ref, do_ref, dz_ref, dg_ref, loss_ref):
        i = pl.program_id(0)
        zv = z_ref[...]
        gv = g_ref[...]
        err = x_ref[...] + gv * zv - t_ref[...]
        dout = err * (1.0 / D)
        do_ref[...] = dout
        dz_ref[...] = (dout * gv).astype(BF16)

        @pl.when(i == 0)
        def _():
            dg_ref[...] = jnp.zeros_like(dg_ref)
            loss_ref[...] = jnp.zeros_like(loss_ref)

        dg_ref[...] += jnp.sum(dout * zv, axis=0, keepdims=True)
        part = jnp.sum(jnp.sum(err * err, axis=0, keepdims=True), axis=-1, keepdims=True) * (0.5 / D)
        lane = lax.broadcasted_iota(jnp.int32, (1, 128), 1)
        loss_ref[...] += jnp.where(lane == 0, part, 0.0)

    return pl.pallas_call(
        body, name="out_loss",
        out_shape=(jax.ShapeDtypeStruct((S, D), F32), jax.ShapeDtypeStruct((S, D), BF16),
                   jax.ShapeDtypeStruct((1, D), F32), jax.ShapeDtypeStruct((1, 128), F32)),
        grid=(S // TR,),
        in_specs=[_row_spec(), _row_spec(), _row_spec(), _vec_spec()],
        out_specs=(_row_spec(), _row_spec(), _vec_spec(), _vec_spec(128)),
        compiler_params=_params(1),
    )(z, x1, target, gate)


ADA_COLS = 6 * D // N_CHIPS
TA = 512


def _ada_fwd(c_all, w_shard, b_shard):
    def body(c_ref, w_ref, b_ref, o_ref):
        cv = c_ref[...]
        o_ref[...] = _dot_f32(cv * _sigmoid(cv), w_ref[...]) + b_ref[...]

    return pl.pallas_call(
        body, name="ada_fwd", out_shape=jax.ShapeDtypeStruct((N_DEV, ADA_COLS), F32), grid=(ADA_COLS // TA,),
        in_specs=[pl.BlockSpec((N_DEV, D), lambda j: (0, 0)), pl.BlockSpec((D, TA), lambda j: (0, j)),
                  pl.BlockSpec((1, TA), lambda j: (0, j))],
        out_specs=pl.BlockSpec((N_DEV, TA), lambda j: (0, j)),
        compiler_params=_params(1),
    )(c_all, w_shard, b_shard)


def _ada_bwd(c_all, dmod_shard):
    def body(c_ref, d_ref, o_ref):
        cv = c_ref[...]
        o_ref[...] = lax.dot_general(cv * _sigmoid(cv), d_ref[...], TN, precision=lax.Precision.HIGHEST,
                                     preferred_element_type=F32)

    return pl.pallas_call(
        body, name="ada_bwd", out_shape=jax.ShapeDtypeStruct((D, ADA_COLS), F32), grid=(ADA_COLS // TA,),
        in_specs=[pl.BlockSpec((N_DEV, D), lambda j: (0, 0)), pl.BlockSpec((N_DEV, TA), lambda j: (0, j))],
        out_specs=pl.BlockSpec((D, TA), lambda j: (0, j)),
        compiler_params=_params(1),
    )(c_all, dmod_shard)


def _sum_rows(g, name):
    rows, n = g.shape

    def body(g_ref, o_ref):
        acc = g_ref[0:1, :]
        for i in range(1, rows):
            acc = acc + g_ref[i:i + 1, :]
        o_ref[...] = acc

    return pl.pallas_call(
        body, name=name, out_shape=jax.ShapeDtypeStruct((1, n), F32),
        in_specs=[VMEM_SPEC], out_specs=VMEM_SPEC,
    )(g)


def _lb_grad(lb_logits, dlb):
    def body(l_ref, d_ref, o_ref):
        p = 1.0 / (1.0 + jnp.exp(l_ref[1:2, :] - l_ref[0:1, :]))
        t = d_ref[...] * p * (1.0 - p)
        o_ref[0:1, :] = t
        o_ref[1:2, :] = -t

    return pl.pallas_call(
        body, name="lb_grad", out_shape=jax.ShapeDtypeStruct((2, HW), F32),
        in_specs=[VMEM_SPEC, VMEM_SPEC], out_specs=VMEM_SPEC,
    )(lb_logits, dlb)


def _adamw(w, g, m, v, name):
    rows, cols = w.shape
    tr = rows
    if rows * cols * 4 > ADAM_BLOCK_BYTES:
        tr = _pick(rows, [t for t in (256, 128, 64, 32, 16, 8) if t * cols * 4 <= ADAM_BLOCK_BYTES])

    def body(w_ref, g_ref, m_ref, v_ref, d_ref, mo_ref, vo_ref):
        gv = g_ref[...]
        m2 = ADAM_B1 * m_ref[...] + (1.0 - ADAM_B1) * gv
        v2 = ADAM_B2 * v_ref[...] + (1.0 - ADAM_B2) * (gv * gv)
        m_hat = m2 / (1.0 - ADAM_B1 ** ADAM_STEP)
        v_hat = v2 / (1.0 - ADAM_B2 ** ADAM_STEP)
        d_ref[...] = -ADAM_LR * (m_hat / (jnp.sqrt(v_hat) + ADAM_EPS) + ADAM_WD * w_ref[...])
        mo_ref[...] = m2
        vo_ref[...] = v2

    spec = pl.BlockSpec((tr, cols), lambda i: (i, 0))
    out = jax.ShapeDtypeStruct((rows, cols), F32)
    return pl.pallas_call(
        body, name=name, out_shape=(out, out, out), grid=(rows // tr,),
        in_specs=[spec] * 4, out_specs=(spec,) * 3,
        compiler_params=_params(1),
    )(w, g, m, v)


SC_TILES = 32
SC_LANES = 16
SC_COL_PARTS = 2
SC_CHUNK_ROWS = 8


def _adamw_sc(w, g, m, v, name):
    rows, cols = w.shape
    band, part = rows // (SC_TILES // SC_COL_PARTS), cols // SC_COL_PARTS
    assert band % SC_CHUNK_ROWS == 0 and part % SC_LANES == 0, (rows, cols)

    def body(w_hbm, g_hbm, m_hbm, v_hbm, d_hbm, mo_hbm, vo_hbm, wb, gb, mb, vb, db):
        tile = lax.axis_index("sc_subcore") * 2 + lax.axis_index("sc_core")
        row0 = (tile // SC_COL_PARTS) * band
        col0 = (tile % SC_COL_PARTS) * part

        @pl.loop(0, band, step=SC_CHUNK_ROWS)
        def _(r):
            at = lambda ref: ref.at[pl.ds(row0 + r, SC_CHUNK_ROWS), pl.ds(col0, part)]
            pltpu.sync_copy(at(w_hbm), wb)
            pltpu.sync_copy(at(g_hbm), gb)
            pltpu.sync_copy(at(m_hbm), mb)
            pltpu.sync_copy(at(v_hbm), vb)

            @pl.loop(0, SC_CHUNK_ROWS)
            def _(i):
                @pl.loop(0, part, step=SC_LANES)
                def _(j):
                    sl = (i, pl.ds(j, SC_LANES))
                    gv = gb[sl]
                    m2 = ADAM_B1 * mb[sl] + (1.0 - ADAM_B1) * gv
                    v2 = ADAM_B2 * vb[sl] + (1.0 - ADAM_B2) * (gv * gv)
                    m_hat = m2 / (1.0 - ADAM_B1 ** ADAM_STEP)
                    v_hat = v2 / (1.0 - ADAM_B2 ** ADAM_STEP)
                    db[sl] = -ADAM_LR * (m_hat / (jnp.sqrt(v_hat) + ADAM_EPS) + ADAM_WD * wb[sl])
                    mb[sl] = m2
                    vb[sl] = v2

            pltpu.sync_copy(db, at(d_hbm))
            pltpu.sync_copy(mb, at(mo_hbm))
            pltpu.sync_copy(vb, at(vo_hbm))

    out = jax.ShapeDtypeStruct((rows, cols), F32)
    buf = pltpu.VMEM((SC_CHUNK_ROWS, part), F32)
    return pl.kernel(
        body, name=name, out_type=(out, out, out),
        mesh=plsc.VectorSubcoreMesh(core_axis_name="sc_core", subcore_axis_name="sc_subcore"),
        scratch_types=[buf] * 5,
    )(w, g, m, v)


W_IN, W_OUT, W_UP, W_DOWN = range(4)
IN_CHUNKS = 4
W_INQ = 4


def _join_row_chunks(chunks):
    return jnp.concatenate(chunks, axis=0)


def _sequence_step(x, target, mod, norm1_w, lb_logits, hg_norm_w, q_norm_w, k_norm_w, norm2_w, conv_w, conv_b, net):
    shift1, scale1, gate1, shift2, scale2, gate2 = [mod[:, i * D:(i + 1) * D] for i in range(6)]

    def run(name, fn, *args, **kw):
        phase = net.host(name)
        if phase is None:
            return fn(*args, **kw)
        res, comm = fn(*args, phase=phase, **kw)
        net.done(name, comm)
        return res

    h = _norm_mod(x, norm1_w, scale1, shift1, "norm1_fwd")
    proj = run("proj_fwd", _matmul, h, net.full[W_IN], "nn", F32, "proj_fwd")
    cat, o_raw, states = run("hgrn_fwd", _hgrn_fwd, proj, lb_logits, hg_norm_w)
    qn, kn = _qk_prep(proj, q_norm_w, k_norm_w)
    att = [run(f"attn_fwd_d{dil}", _attn_fwd, proj, qn, kn, dil) for dil in DILS]
    cat, b_out_f32, lse = _attn_merge([t[0] for t in att], [t[1] for t in att], cat)
    mix = run("mix_fwd", _matmul, cat, net.full[W_OUT], "nn", F32, "mix_fwd")
    x1, h2 = _resid_norm_mod(x, mix, gate1, norm2_w, scale2, shift2, "norm2_fwd")
    u = run("up_fwd", _matmul, h2, net.full[W_UP], "nn", BF16, "up_fwd")
    yact = run("conv_gate_fwd", _conv_gate_fwd, u, conv_w, conv_b)
    net.alone("before_down_fwd")
    z = _matmul(yact, net.full[W_DOWN], "nn", F32, "down_fwd")
    dout, dz, dgate2, loss_row = _out_loss(z, x1, target, gate2)

    net.grad[W_DOWN] = _matmul(yact, dz, "tn", BF16, "down_bwd_w")
    dyact = run("down_bwd_x", _matmul, dz, net.full[W_DOWN], "nt", BF16, "down_bwd_x")
    da, dg, dconv_b, dconv_w = run("conv_gate_bwd", _conv_gate_bwd, dyact, u, conv_w, conv_b)
    du = jnp.concatenate([da, dg], axis=1)
    net.grad[W_UP] = run("up_bwd_w", _matmul, h2, du, "tn", BF16, "up_bwd_w")
    dh2 = run("up_bwd_x", _matmul, du, net.full[W_UP], "nt", F32, "up_bwd_x")
    dx1, dshift2, dscale2, dnorm2, dgate1, dmix = run(
        "norm2_bwd", _norm_mod_bwd, dh2, x1, norm2_w, scale2, dout, "norm2_bwd", mix=mix, gate=gate1)
    net.grad[W_OUT] = run("mix_bwd_w", _matmul, cat, dmix, "tn", BF16, "mix_bwd_w")
    dcat = run("mix_bwd_x", _matmul, dmix, net.full[W_OUT], "nt", F32, "mix_bwd_x")
    dhq, dhf, dhi, dhg, dlb, dhg_norm = run("hgrn_bwd", _hgrn_bwd, proj, lb_logits, hg_norm_w, o_raw, states, dcat)
    delta = _attn_delta(dcat, b_out_f32)
    acc = None
    for dil in DILS:
        acc = run(f"attn_bwd_d{dil}", _attn_bwd, proj, qn, kn, lse, delta, dcat, dil, acc)
    daq, dak, dav, dq_norm, dk_norm = _qk_norm_bwd(proj, *acc, q_norm_w, k_norm_w)
    dproj = jnp.concatenate([dhq, dhf, dhi, dhg, daq, dak, dav], axis=1)
    for q in range(IN_CHUNKS):
        net.grad[W_INQ + q] = run(f"proj_bwd_w_{q}", _matmul, h, dproj, "tn", BF16, f"proj_bwd_w_{q}",
                                  m_blocks=(D // IN_CHUNKS, [q]))
    dh = run("proj_bwd_x", _matmul, dproj, net.full[W_IN], "nt", F32, "proj_bwd_x")
    grad_x, dshift1, dscale1, dnorm1 = run("norm1_bwd", _norm_mod_bwd, dh, x, norm1_w, scale1, dx1, "norm1_bwd")

    dmod = jnp.concatenate([dshift1, dscale1, dgate1, dshift2, dscale2, dgate2], axis=1)
    small = dict(norm1_w=dnorm1, lb=dlb, hg_norm_w=dhg_norm, q_norm_w=dq_norm, k_norm_w=dk_norm,
                 norm2_w=dnorm2, conv_b=dconv_b, conv_w=dconv_w)
    return loss_row, grad_x, dmod, small


def _place():
    x, y, c = lax.axis_index("x"), lax.axis_index("y"), lax.axis_index("c")
    others = [(1 - x, y), (x, 1 - y), (1 - x, 1 - y)]
    return x, y, c, others


def _remote(src, dst, send_sems, recv_sems, k, to):
    return pltpu.make_async_remote_copy(src_ref=src, dst_ref=dst, send_sem=send_sems.at[k], recv_sem=recv_sems.at[k],
                                        device_id=to, device_id_type=MESH)


class _Phase:
    def __init__(self):
        self.ins, self.outs, self.aliases, self.groups, self.n = [], [], {}, [], 0

    def add(self, ins, outs, aliases, n, copies):
        i0, o0 = len(self.ins), len(self.outs)
        self.ins += list(ins)
        self.outs += list(outs)
        self.aliases.update({i0 + i: o0 + o for i, o in aliases.items()})
        self.groups.append((slice(i0, i0 + len(ins)), slice(o0, o0 + len(outs)), copies))
        self.n += n
        return slice(o0, o0 + len(outs))

    def build(self, cin, cout, send_sems, recv_sems, landing):
        res = []
        for si, so, copies in self.groups:
            for src, dst, land, peer in copies(cin[si], cout[so]):
                k = len(res)
                res.append(_remote(land, land, send_sems, recv_sems, k, peer) if landing
                           else _remote(src, dst, send_sems, recv_sems, k, peer))
        assert len(res) == self.n
        return res


def _pcall(body, args, phase=None, **kw):
    if phase is None or not phase.groups:
        res = pl.pallas_call(body, **kw)(*args)
        return res if phase is None else (res, ())
    grid = tuple(kw.pop("grid", ()))
    out_shape, out_specs = kw.pop("out_shape"), kw.pop("out_specs")
    single = not isinstance(out_shape, (tuple, list))
    out_shape = [out_shape] if single else list(out_shape)
    out_specs = [out_specs] if single else list(out_specs)
    scratch = list(kw.pop("scratch_shapes", ()))
    n_in, n_out, n_ci, n_co = len(args), len(out_shape), len(phase.ins), len(phase.outs)

    def hosted(*refs):
        ins, cin = refs[:n_in], refs[n_in:n_in + n_ci]
        outs = refs[n_in + n_ci:n_in + n_ci + n_out]
        cout = refs[n_in + n_ci + n_out:n_in + n_ci + n_out + n_co]
        scr, send_sems, recv_sems = refs[n_in + n_ci + n_out + n_co:-2], refs[-2], refs[-1]
        ids = [pl.program_id(a) for a in range(len(grid))]

        def start():
            for cp in phase.build(cin, cout, send_sems, recv_sems, False):
                cp.start()

        def finish():
            for cp in phase.build(cin, cout, send_sems, recv_sems, False):
                cp.wait_send()
            for cp in phase.build(cin, cout, send_sems, recv_sems, True):
                cp.wait_recv()

        if grid:
            first = functools.reduce(jnp.logical_and, [i == 0 for i in ids])
            last = functools.reduce(jnp.logical_and, [i == g - 1 for i, g in zip(ids, grid)])
            pl.when(first)(start)
            body(*ins, *outs, *scr)
            pl.when(last)(finish)
        else:
            start()
            body(*ins, *outs, *scr)
            finish()

    res = pl.pallas_call(
        hosted, out_shape=tuple(out_shape + phase.outs), grid=grid,
        in_specs=list(kw.pop("in_specs")) + [HBM_SPEC] * n_ci, out_specs=tuple(out_specs + [HBM_SPEC] * n_co),
        input_output_aliases={n_in + i: n_out + o for i, o in phase.aliases.items()},
        scratch_shapes=scratch + [pltpu.SemaphoreType.DMA((phase.n,)), pltpu.SemaphoreType.DMA((phase.n,))],
        **kw,
    )(*args, *phase.ins)
    outs = res[:n_out]
    return (outs[0] if single else tuple(outs)), tuple(res[n_out:])


def _comm_only(name, phase):
    return _pcall(lambda: None, [], phase, name=name, out_shape=(), in_specs=[], out_specs=())[1]


def _all_gather_rows(block, name):
    m_per, n = block.shape

    def body(x_ref, out_ref, send_sems, recv_sems, local_sem):
        x, y, c, chips = _place()
        me, sibling = (x, y, c), (x, y, 1 - c)

        def rows(px, py, pc):
            return out_ref.at[pl.ds((4 * px + 2 * py + pc) * m_per, m_per), :]

        def copy(k, blk, to, src=None):
            return _remote(rows(*blk) if src is None else src, rows(*blk), send_sems, recv_sems, k, to)

        mine = pltpu.make_async_copy(x_ref, rows(*me), local_sem)
        mine.start()
        first = [copy(0, me, sibling, src=x_ref)]
        first += [copy(1 + j, me, (*chip, c), src=x_ref) for j, chip in enumerate(chips)]
        for cp in first:
            cp.start()
        passed = [copy(4 + j, (*chip, c), sibling) for j, chip in enumerate(chips)]
        for j, chip in enumerate(chips):
            copy(1 + j, (*chip, c), me).wait_recv()
            passed[j].start()
        copy(0, sibling, me).wait_recv()
        for j, chip in enumerate(chips):
            copy(4 + j, (*chip, 1 - c), me).wait_recv()
        for cp in first + passed:
            cp.wait_send()
        mine.wait()

    return pl.pallas_call(
        body, name=name, out_shape=jax.ShapeDtypeStruct((N_DEV * m_per, n), block.dtype),
        in_specs=[VMEM_SPEC], out_specs=VMEM_SPEC,
        scratch_shapes=[pltpu.SemaphoreType.DMA((7,)), pltpu.SemaphoreType.DMA((7,)), pltpu.SemaphoreType.DMA],
    )(block)


class _Geo:
    def __init__(self, kind, shard_shape):
        self.kind = kind
        self.shard_shape = tuple(shard_shape)
        self.hr, self.hc = shard_shape[0] // 2, shard_shape[1]
        if kind == "col":
            self.full_shape = (shard_shape[0], N_CHIPS * shard_shape[1])
        else:
            self.full_shape = (N_CHIPS * shard_shape[0], shard_shape[1])

    def full_shard(self, ref, j):
        if self.kind == "col":
            return ref.at[:, pl.ds(pl.multiple_of(j * self.hc, 128), self.hc)]
        return ref.at[pl.ds(pl.multiple_of(j * 2 * self.hr, 16), 2 * self.hr), :]

    def full_half(self, ref, j, c):
        if self.kind == "col":
            return ref.at[pl.ds(pl.multiple_of(c * self.hr, 16), self.hr),
                          pl.ds(pl.multiple_of(j * self.hc, 128), self.hc)]
        return ref.at[pl.ds(pl.multiple_of((2 * j + c) * self.hr, 16), self.hr), :]

    def piece(self, ref, j, c, p0, p1, pieces, part=None):
        pr = self.hr // pieces
        off, n = p0 * pr, (p1 - p0) * pr
        if part is not None:
            top = (n // 32) * 16
            off, n = (off, top) if part == 0 else (off + top, n - top)
        if self.kind == "col":
            return ref.at[pl.ds(pl.multiple_of(c * self.hr + off, 16), n),
                          pl.ds(pl.multiple_of(j * self.hc, 128), self.hc)]
        return ref.at[pl.ds(pl.multiple_of((2 * j + c) * self.hr + off, 16), n), :]


WEIGHT_KINDS = ("col", "row", "col", "row")
NW = len(WEIGHT_KINDS)


def _comm_call(body, name, ins, outs, n_remote, aliases=None):
    return pl.pallas_call(
        body, name=name, out_shape=tuple(outs),
        in_specs=[HBM_SPEC] * len(ins), out_specs=tuple([HBM_SPEC] * len(outs)),
        input_output_aliases=aliases or {},
        scratch_shapes=[pltpu.SemaphoreType.DMA((n_remote,)), pltpu.SemaphoreType.DMA((n_remote,))],
    )(*ins)


ROW_TILES = (256, 176, 128, 64, 32, 16)


def _cast_into_full(shard, geo, place, name):
    rs, cs = shard.shape
    tr = _pick(rs, ROW_TILES)
    nb = rs // tr

    def body(place_ref, s_ref, o_ref):
        o_ref[...] = s_ref[...].astype(BF16)

    if geo.kind == "col":
        out_map = lambda i, place_ref: (i, place_ref[0])
    else:
        out_map = lambda i, place_ref: (place_ref[0] * nb + i, 0)
    return pl.pallas_call(
        body, name=name, out_shape=jax.ShapeDtypeStruct(geo.full_shape, BF16),
        grid_spec=pltpu.PrefetchScalarGridSpec(
            num_scalar_prefetch=1, grid=(nb,),
            in_specs=[pl.BlockSpec((tr, cs), lambda i, place_ref: (i, 0))],
            out_specs=pl.BlockSpec((tr, cs), out_map)),
        compiler_params=_params(1),
    )(place, shard)


def _gather_weight(full, geo, pieces, name):
    per = 8

    def body(in_ref, ref, send_sems, recv_sems):
        x, y, c, _ = _place()
        j, jx, jy, jo = 2 * x + y, 2 * (1 - x) + y, 2 * x + (1 - y), 2 * (1 - x) + (1 - y)
        nx, ny, sib = (1 - x, y, c), (x, 1 - y, c), (x, y, 1 - c)

        def cp(region, k, to):
            return _remote(region, region, send_sems, recv_sems, k, to)

        def reg(chip, p, part=None, core=c):
            return geo.piece(ref, chip, core, p, p + 1, pieces, part)

        sent = []
        for p in range(pieces):
            sent += [cp(reg(j, p), per * p, nx), cp(reg(j, p), per * p + 1, ny)]
        for d in sent:
            d.start()
        for p in range(pieces):
            cp(reg(jx, p), per * p, nx).wait_recv()
            new = [cp(reg(jx, p, 0), per * p + 2, ny), cp(reg(jx, p), per * p + 4, sib)]
            cp(reg(jy, p), per * p + 1, ny).wait_recv()
            new += [cp(reg(jy, p, 1), per * p + 3, nx), cp(reg(jy, p), per * p + 5, sib)]
            for d in new:
                d.start()
            sent += new
        for p in range(pieces):
            cp(reg(jo, p, 0), per * p + 2, ny).wait_recv()
            cp(reg(jo, p, 1), per * p + 3, nx).wait_recv()
            new = [cp(reg(jo, p, 0), per * p + 6, sib), cp(reg(jo, p, 1), per * p + 7, sib)]
            for d in new:
                d.start()
            sent += new
        for p in range(pieces):
            cp(reg(jx, p, None, 1 - c), per * p + 4, sib).wait_recv()
            cp(reg(jy, p, None, 1 - c), per * p + 5, sib).wait_recv()
            cp(reg(jo, p, 0, 1 - c), per * p + 6, sib).wait_recv()
            cp(reg(jo, p, 1, 1 - c), per * p + 7, sib).wait_recv()
        for d in sent:
            d.wait_send()

    return _comm_call(body, name, [full], [_same(full)], per * pieces, aliases={0: 0})[0]


def _same(a):
    return jax.ShapeDtypeStruct(a.shape, a.dtype)


def _gather_ici(full, geo, p0, p1, pieces):
    def copies(ins, outs):
        x, y, c, _ = _place()
        mine = geo.piece(outs[0], 2 * x + y, c, p0, p1, pieces)
        return [(mine, mine, geo.piece(outs[0], 2 * ox + oy, c, p0, p1, pieces), (ox, oy, c))
                for ox, oy in ((1 - x, y), (x, 1 - y))]

    return dict(ins=[full], outs=[_same(full)], aliases={0: 0}, n=2, copies=copies)


def _gather_relay(full, geo, p0, p1, pieces):
    def copies(ins, outs):
        x, y, c, _ = _place()
        jx, jy, jo = 2 * (1 - x) + y, 2 * x + (1 - y), 2 * (1 - x) + (1 - y)
        reg = lambda chip, part: geo.piece(outs[0], chip, c, p0, p1, pieces, part)
        return [(reg(jx, 0), reg(jx, 0), reg(jo, 0), (x, 1 - y, c)), (reg(jy, 1), reg(jy, 1), reg(jo, 1), (1 - x, y, c))]

    return dict(ins=[full], outs=[_same(full)], aliases={0: 0}, n=2, copies=copies)


def _gather_d2d(full, geo):
    def copies(ins, outs):
        x, y, c, chips = _place()
        return [(geo.full_half(outs[0], 2 * ox + oy, c), geo.full_half(outs[0], 2 * ox + oy, c),
                 geo.full_half(outs[0], 2 * ox + oy, 1 - c), (x, y, 1 - c)) for ox, oy in chips]

    return dict(ins=[full], outs=[_same(full)], aliases={0: 0}, n=3, copies=copies)


def _swap_d2d(grad, geo):
    def copies(ins, outs):
        x, y, c, _ = _place()
        return [(geo.full_half(ins[0], j, 1 - c), outs[0].at[j], outs[0].at[j], (x, y, 1 - c)) for j in range(N_CHIPS)]

    return dict(ins=[grad], outs=[jax.ShapeDtypeStruct((N_CHIPS, geo.hr, geo.hc), BF16)], aliases={}, n=N_CHIPS,
                copies=copies)


def _scatter_ici(pair, recv, geo, p0, p1, pieces):
    pr = geo.hr // pieces
    rows = pl.ds(p0 * pr, (p1 - p0) * pr)

    def copies(ins, outs):
        x, y, c, chips = _place()
        return [(ins[0].at[2 * ox + oy, rows, :], outs[0].at[k, rows, :], outs[0].at[k, rows, :], (ox, oy, c))
                for k, (ox, oy) in enumerate(chips)]

    out = jax.ShapeDtypeStruct((3, geo.hr, geo.hc), BF16)
    if recv is None:
        return dict(ins=[pair], outs=[out], aliases={}, n=3, copies=copies)
    return dict(ins=[pair, recv], outs=[out], aliases={1: 0}, n=3, copies=copies)


def _join_d2d(shard, geo, row0=0):
    def copies(ins, outs):
        x, y, c, _ = _place()
        mine = outs[0].at[pl.ds(pl.multiple_of(row0 + c * geo.hr, 8), geo.hr), :]
        other = outs[0].at[pl.ds(pl.multiple_of(row0 + (1 - c) * geo.hr, 8), geo.hr), :]
        return [(mine, mine, other, (x, y, 1 - c))]

    return dict(ins=[shard], outs=[_same(shard)], aliases={0: 0}, n=1, copies=copies)


def _add_pair(grad, got, geo, place, name):
    tr = _pick(geo.hr, ROW_TILES)
    nb = geo.hr // tr

    def body(place_ref, a_ref, b_ref, o_ref):
        o_ref[0] = (a_ref[...].astype(F32) + b_ref[0].astype(F32)).astype(BF16)

    if geo.kind == "col":
        own_map = lambda j, i, place_ref: (place_ref[1] * nb + i, j)
    else:
        own_map = lambda j, i, place_ref: ((2 * j + place_ref[1]) * nb + i, 0)
    spec = pl.BlockSpec((1, tr, geo.hc), lambda j, i, place_ref: (j, i, 0))
    return pl.pallas_call(
        body, name=name, out_shape=jax.ShapeDtypeStruct((N_CHIPS, geo.hr, geo.hc), BF16),
        grid_spec=pltpu.PrefetchScalarGridSpec(
            num_scalar_prefetch=1, grid=(N_CHIPS, nb),
            in_specs=[pl.BlockSpec((tr, geo.hc), own_map), spec], out_specs=spec),
        compiler_params=_params(2),
    )(place, grad, got)


def _add_four(pair, recv, geo, place, name, rows=None, row0=0, into=None):
    tr = _pick(geo.hr, ROW_TILES)
    nb = geo.hr // tr
    rows = 2 * geo.hr if rows is None else rows

    def body(place_ref, p_ref, r_ref, *rest):
        rest[-1][...] = ((p_ref[0].astype(F32) + r_ref[0].astype(F32)) + r_ref[1].astype(F32)) + r_ref[2].astype(F32)

    in_specs = [pl.BlockSpec((1, tr, geo.hc), lambda i, place_ref: (place_ref[0], i, 0)),
                pl.BlockSpec((3, tr, geo.hc), lambda i, place_ref: (0, i, 0))]
    args = [place, pair, recv]
    if into is not None:
        in_specs.append(pl.BlockSpec(memory_space=pl.ANY))
        args.append(into)
    return pl.pallas_call(
        body, name=name, out_shape=jax.ShapeDtypeStruct((rows, geo.hc), F32),
        grid_spec=pltpu.PrefetchScalarGridSpec(
            num_scalar_prefetch=1, grid=(nb,), in_specs=in_specs,
            out_specs=pl.BlockSpec((tr, geo.hc), lambda i, place_ref: (row0 // tr + place_ref[1] * nb + i, 0))),
        input_output_aliases={3: 0} if into is not None else {},
        compiler_params=_params(1),
    )(*args)


PIECES = (4, 1, 16, 8) + (1,) * IN_CHUNKS
SCHEDULE = {
    "proj_fwd": (("gather_ici", W_OUT, 0, 1), ("gather_ici", W_UP, 0, 6)),
    "hgrn_fwd": (("gather_relay", W_OUT, 0, 1), ("gather_relay", W_UP, 0, 6), ("gather_ici", W_UP, 6, 10)),
    "attn_fwd_d1": (("gather_ici", W_UP, 10, 12), ("gather_relay", W_UP, 6, 10)),
    "attn_fwd_d4": (("gather_ici", W_UP, 12, 16), ("gather_relay", W_UP, 10, 12), ("gather_d2d", W_OUT)),
    "attn_fwd_d16": (("gather_relay", W_UP, 12, 16), ("gather_ici", W_DOWN, 0, 2)),
    "mix_fwd": (("gather_d2d", W_UP), ("gather_ici", W_DOWN, 2, 4)),
    "up_fwd": (("gather_ici", W_DOWN, 4, 8), ("gather_relay", W_DOWN, 0, 4)),
    "conv_gate_fwd": (("gather_relay", W_DOWN, 4, 8),),
    "before_down_fwd": (("gather_d2d", W_DOWN),),
    "down_bwd_x": (("swap", W_DOWN),),
    "conv_gate_bwd": (("scatter", W_DOWN, 0, 4),),
    "up_bwd_w": (("scatter", W_DOWN, 4, 8),),
    "up_bwd_x": (("swap", W_UP),),
    "norm2_bwd": (("scatter", W_UP, 0, 1),),
    "mix_bwd_w": (("scatter", W_UP, 1, 2),),
    "mix_bwd_x": (("swap", W_OUT), ("scatter", W_UP, 2, 3)),
    "hgrn_bwd": (("scatter", W_UP, 3, 9), ("join", W_DOWN)),
    "attn_bwd_d1": (("scatter", W_UP, 9, 12),),
    "attn_bwd_d4": (("scatter", W_UP, 12, 13), ("scatter", W_OUT, 0, 1)),
    "attn_bwd_d16": (("scatter", W_UP, 13, 16),),
    "proj_bwd_w_0": (("join", W_UP), ("join", W_OUT)),
    "proj_bwd_w_1": (("swap", W_INQ),),
    "proj_bwd_w_2": (("swap", W_INQ + 1),),
    "proj_bwd_w_3": (("swap", W_INQ + 2), ("scatter", W_INQ, 0, 1)),
    "proj_bwd_x": (("swap", W_INQ + 3), ("scatter", W_INQ + 1, 0, 1), ("scatter", W_INQ + 2, 0, 1)),
    "norm1_bwd": (("scatter", W_INQ + 3, 0, 1), ("join", W_INQ), ("join", W_INQ + 1), ("join", W_INQ + 2)),
    "grad_in_join": (("join", W_INQ + 3),),
}


class _Net:
    def __init__(self, shards, place):
        self.place = place
        self.geos = [_Geo(k, s.shape) for k, s in zip(WEIGHT_KINDS, shards)]
        self.full = [_cast_into_full(s, g, place, f"cast_shard_{w}") for w, (s, g) in enumerate(zip(shards, self.geos))]
        self.full[W_IN] = _gather_weight(self.full[W_IN], self.geos[W_IN], PIECES[W_IN], "gather_w_in")
        rows, cols = shards[W_IN].shape
        self.geos += [_Geo("col", (rows // IN_CHUNKS, cols))] * IN_CHUNKS
        n = NW + IN_CHUNKS
        self.grad, self.pair, self.recv, self.shard = [None] * n, [None] * n, [None] * n, [None] * n
        self.in_rows, self.in_shard = rows, None
        self.when_joined, self.joined = {}, {}
        self.slots = []

    def _row0(self, w):
        return (w - W_INQ) * 2 * self.geos[w].hr

    def host(self, name):
        phase = _Phase()
        self.slots = []
        groups = {}
        for item in SCHEDULE.get(name, ()):
            kind, w = item[0], item[1]
            geo = self.geos[w]
            key = len(groups)
            if kind == "gather_ici":
                spec, key = _gather_ici(self.full[w], geo, item[2], item[3], PIECES[w]), ("full", w)
            elif kind == "gather_relay":
                spec, key = _gather_relay(self.full[w], geo, item[2], item[3], PIECES[w]), ("full", w)
            elif kind == "gather_d2d":
                spec, key = _gather_d2d(self.full[w], geo), ("full", w)
            elif kind == "swap":
                spec = _swap_d2d(self.grad[w], geo)
            elif kind == "scatter":
                spec, key = _scatter_ici(self.pair[w], self.recv[w], geo, item[2], item[3], PIECES[w]), ("recv", w)
            elif w >= W_INQ:
                spec, key = _join_d2d(self.in_shard, geo, self._row0(w)), "in_shard"
            else:
                spec = _join_d2d(self.shard[w], geo)
            groups.setdefault(key, []).append((item, spec))
        for group in groups.values():
            specs = [s for _, s in group]
            merged = dict(specs[0], n=sum(s["n"] for s in specs),
                          copies=lambda ins, outs, specs=specs: [t for s in specs for t in s["copies"](ins, outs)])
            self.slots.append(([item for item, _ in group], phase.add(**merged)))
        return phase

    def done(self, name, comm):
        for items, sl in sorted(self.slots, key=lambda s: s[0][0][0] == "scatter"):
            out = comm[sl][0]
            for item in items:
                kind, w = item[0], item[1]
                if kind in ("gather_ici", "gather_relay", "gather_d2d"):
                    self.full[w] = out
                elif kind == "swap":
                    self.pair[w] = _add_pair(self.grad[w], out, self.geos[w], self.place, f"grad_pair_sum_{w}")
                elif kind == "scatter":
                    self.recv[w] = out
                    if item[3] == PIECES[w] and w >= W_INQ:
                        self.in_shard = _add_four(self.pair[w], out, self.geos[w], self.place, f"grad_chip_sum_{w}",
                                                  rows=self.in_rows, row0=self._row0(w), into=self.in_shard)
                    elif item[3] == PIECES[w]:
                        self.shard[w] = _add_four(self.pair[w], out, self.geos[w], self.place, f"grad_chip_sum_{w}")
                elif w >= W_INQ:
                    self.in_shard = out
                else:
                    self.shard[w] = out
                    if w in self.when_joined:
                        self.joined[w] = self.when_joined[w](out)

    def alone(self, name):
        self.done(name, _comm_only(name, self.host(name)))


CONVW_SHARD = 3 * DFF // N_CHIPS
COND_PAD = 8 * 896
SMALL_SIZES = (("norm1_w", D), ("lb", HW), ("hg_norm_w", DH), ("q_norm_w", DH), ("k_norm_w", DH),
               ("norm2_w", D), ("conv_b", DFF), ("conv_w", 3 * DFF), ("loss", 128))
SMALL_TOTAL = sum(n for _, n in SMALL_SIZES)
STATS_PAD = 8 * 5120


def kernel(x, c, w_ada, b_ada, norm1_w, w_in, lb_logits, hg_norm_w, q_norm_w, k_norm_w, w_out, norm2_w, w_up, conv_w, conv_b, w_down, loss_target, m_w_ada, m_b_ada, m_norm1_w, m_w_in, m_lb_logits, m_hg_norm_w, m_q_norm_w, m_k_norm_w, m_w_out, m_norm2_w, m_w_up, m_conv_w, m_conv_b, m_w_down, v_w_ada, v_b_ada, v_norm1_w, v_w_in, v_lb_logits, v_hg_norm_w, v_q_norm_w, v_k_norm_w, v_w_out, v_norm2_w, v_w_up, v_conv_w, v_conv_b, v_w_down):
    chip = 2 * lax.axis_index("x") + lax.axis_index("y")
    dev = 2 * chip + lax.axis_index("c")

    cond = jnp.concatenate([c, conv_w[0].reshape(1, CONVW_SHARD), jnp.zeros((1, COND_PAD - D - CONVW_SHARD), F32)], axis=1)
    cond_all = _all_gather_rows(cond.reshape(8, COND_PAD // 8), "gather_cond").reshape(N_DEV, COND_PAD)
    c_all = cond_all[:, :D]
    conv_w_full = jnp.concatenate(
        [cond_all[2 * j, D:D + CONVW_SHARD].reshape(3, DFF // N_CHIPS) for j in range(N_CHIPS)], axis=1)

    b_shard = lax.dynamic_slice_in_dim(b_ada, chip * ADA_COLS, ADA_COLS, axis=1)
    mod_cols = _all_gather_rows(_ada_fwd(c_all, w_ada[0], b_shard), "gather_mod")
    mod_all = jnp.concatenate([mod_cols[16 * j:16 * j + 8] for j in range(N_CHIPS)], axis=1)
    mod = lax.dynamic_slice_in_dim(mod_all, dev, 1, axis=0)

    place = jnp.stack([chip, lax.axis_index("c")]).astype(jnp.int32)
    net = _Net([w_in[0], w_out[0], w_up[0], w_down[0]], place)
    net.when_joined = {
        W_OUT: lambda grad: _adamw_sc(w_out[0], grad, m_w_out[0], v_w_out[0], "adamw_sc_w_out"),
        W_DOWN: lambda grad: _adamw_sc(w_down[0], grad, m_w_down[0], v_w_down[0], "adamw_sc_w_down"),
    }
    early = {W_OUT: "w_out", W_DOWN: "w_down"}
    loss_row, grad_x, dmod, small = _sequence_step(
        x[0], loss_target[0], mod, norm1_w, lb_logits, hg_norm_w, q_norm_w, k_norm_w, norm2_w, conv_w_full, conv_b, net)
    net.alone("grad_in_join")
    g_out, g_up, g_down = net.shard[W_OUT], net.shard[W_UP], net.shard[W_DOWN]
    g_in = net.in_shard

    small["conv_w"] = small["conv_w"].reshape(1, 3 * DFF)
    small["loss"] = loss_row
    stats = jnp.concatenate([dmod] + [small[k] for k, _ in SMALL_SIZES]
                            + [jnp.zeros((1, STATS_PAD - 6 * D - SMALL_TOTAL), F32)], axis=1)
    stats_all = _all_gather_rows(stats.reshape(8, STATS_PAD // 8), "gather_stats").reshape(N_DEV, STATS_PAD)
    dmod_all = stats_all[:, :6 * D]
    sums = _sum_rows(stats_all[:, 6 * D:6 * D + SMALL_TOTAL], "small_grad_sum")
    g_small, off = {}, 0
    for k, n in SMALL_SIZES:
        g_small[k] = sums[:, off:off + n]
        off += n
    loss = g_small["loss"][0, 0]
    g_b_ada = _sum_rows(dmod_all, "b_ada_grad_sum")
    g_w_ada = _ada_bwd(c_all, lax.dynamic_slice_in_dim(dmod_all, chip * ADA_COLS, ADA_COLS, axis=1))
    g_lb = _lb_grad(lb_logits, g_small["lb"])
    g_conv_w = lax.dynamic_slice_in_dim(g_small["conv_w"].reshape(3, DFF), chip * (DFF // N_CHIPS), DFF // N_CHIPS, axis=1)

    names = ["w_ada", "b_ada", "norm1_w", "w_in", "lb_logits", "hg_norm_w", "q_norm_w", "k_norm_w", "w_out",
             "norm2_w", "w_up", "conv_w", "conv_b", "w_down"]
    lead = {"w_ada", "w_in", "w_out", "w_up", "conv_w", "w_down"}
    w = dict(w_ada=w_ada, b_ada=b_ada, norm1_w=norm1_w, w_in=w_in, lb_logits=lb_logits, hg_norm_w=hg_norm_w,
             q_norm_w=q_norm_w, k_norm_w=k_norm_w, w_out=w_out, norm2_w=norm2_w, w_up=w_up, conv_w=conv_w,
             conv_b=conv_b, w_down=w_down)
    m = dict(w_ada=m_w_ada, b_ada=m_b_ada, norm1_w=m_norm1_w, w_in=m_w_in, lb_logits=m_lb_logits,
             hg_norm_w=m_hg_norm_w, q_norm_w=m_q_norm_w, k_norm_w=m_k_norm_w, w_out=m_w_out, norm2_w=m_norm2_w,
             w_up=m_w_up, conv_w=m_conv_w, conv_b=m_conv_b, w_down=m_w_down)
    v = dict(w_ada=v_w_ada, b_ada=v_b_ada, norm1_w=v_norm1_w, w_in=v_w_in, lb_logits=v_lb_logits,
             hg_norm_w=v_hg_norm_w, q_norm_w=v_q_norm_w, k_norm_w=v_k_norm_w, w_out=v_w_out, norm2_w=v_norm2_w,
             w_up=v_w_up, conv_w=v_conv_w, conv_b=v_conv_b, w_down=v_w_down)
    g = dict(w_ada=g_w_ada, b_ada=g_b_ada, norm1_w=g_small["norm1_w"], w_in=g_in, lb_logits=g_lb,
             hg_norm_w=g_small["hg_norm_w"], q_norm_w=g_small["q_norm_w"], k_norm_w=g_small["k_norm_w"], w_out=g_out,
             norm2_w=g_small["norm2_w"], w_up=g_up, conv_w=g_conv_w, conv_b=g_small["conv_b"], w_down=g_down)

    updated = {early[i]: res for i, res in net.joined.items()}
    grads, deltas, new_m, new_v = [], [], [], []
    for n in names:
        strip = (lambda t: t[0]) if n in lead else (lambda t: t)
        wrap = (lambda t: t[None]) if n in lead else (lambda t: t)
        if n in updated:
            d_n, m_n, v_n = updated[n]
        else:
            d_n, m_n, v_n = _adamw(strip(w[n]), g[n], strip(m[n]), strip(v[n]), f"adamw_{n}")
        grads.append(wrap(g[n]))
        deltas.append(wrap(d_n))
        new_m.append(wrap(m_n))
        new_v.append(wrap(v_n))
    return (loss, grad_x[None], *grads, *deltas, *new_m, *new_v)
```

```python
import functools
import math

import jax
import jax.numpy as jnp
from jax import lax
from jax.experimental import pallas as pl
from jax.experimental.pallas import tpu as pltpu
from jax.experimental.pallas import tpu_sc as plsc

F32 = jnp.float32
BF16 = jnp.bfloat16

S = 2048
D = 2048
H = 8
DH = 128
HW = H * DH
CH = 64
NCH = S // CH
DFF = 5632
INC = 7 * HW
BLK = 128
DILS = (1, 4, 16)
EPS = 1e-6
NEG = -1e30
N_CHIPS = 4
N_DEV = 8

ADAM_LR = 0.001
ADAM_B1 = 0.9
ADAM_B2 = 0.999
ADAM_EPS = 1e-08
ADAM_WD = 0.01
ADAM_STEP = 10
ADAM_BLOCK_BYTES = 1 << 20

V7X_VMEM_BYTES = 64 * 1024 * 1024
VMEM_LIMIT = (V7X_VMEM_BYTES * 3) // 4
MESH = pl.DeviceIdType.MESH
HBM_SPEC = pl.BlockSpec(memory_space=pltpu.HBM)
VMEM_SPEC = pl.BlockSpec(memory_space=pltpu.VMEM)

NN = (((1,), (0,)), ((), ()))
NT = (((1,), (1,)), ((), ()))
TN = (((0,), (0,)), ((), ()))


def _params(n_grid):
    return pltpu.CompilerParams(dimension_semantics=("arbitrary",) * n_grid, vmem_limit_bytes=VMEM_LIMIT)


def _dot(a, b, dims=NN):
    return lax.dot_general(a.astype(BF16), b.astype(BF16), dims, preferred_element_type=F32)


def _dot_f32(a, b):
    return lax.dot_general(a, b, NN, precision=lax.Precision.HIGHEST, preferred_element_type=F32)


def _sigmoid(x):
    return 1.0 / (1.0 + jnp.exp(-x))


def _pick(n, cands):
    for t in cands:
        if n % t == 0:
            return t
    raise ValueError(n)


def _matmul(a, b, mode, out_dtype, name, phase=None, m_blocks=None):
    if mode == "nn":
        (m, k), (_, n) = a.shape, b.shape
    elif mode == "nt":
        (m, k), (n, _) = a.shape, b.shape
    else:
        (k, m), (_, n) = a.shape, b.shape
    tm = _pick(m, (1024, 1408, 512))
    a_block = lambda i: i
    if m_blocks is not None:
        tm, blocks = m_blocks
        m = tm * len(blocks)
        step = blocks[1] - blocks[0] if len(blocks) > 1 else 0
        assert all(blk == blocks[0] + step * i for i, blk in enumerate(blocks))
        a_block = lambda i: blocks[0] + step * i
    tn = _pick(n, (1024, 1408, 512))
    tk = _pick(k, (2048, 2816, 1792, 1024, 512))
    nk = k // tk
    dims = {"nn": NN, "nt": NT, "tn": TN}[mode]

    def body(a_ref, b_ref, o_ref, *acc):
        part = lax.dot_general(a_ref[...], b_ref[...], dims, preferred_element_type=F32)
        if nk == 1:
            o_ref[...] = part.astype(o_ref.dtype)
            return
        acc_ref, kk = acc[0], pl.program_id(2)

        @pl.when(kk == 0)
        def _():
            acc_ref[...] = part

        @pl.when(jnp.logical_and(kk > 0, kk < nk - 1))
        def _():
            acc_ref[...] += part

        @pl.when(kk == nk - 1)
        def _():
            o_ref[...] = (acc_ref[...] + part).astype(o_ref.dtype)

    if mode == "tn":
        a_spec = pl.BlockSpec((tk, tm), lambda i, j, kk: (kk, a_block(i)))
    else:
        a_spec = pl.BlockSpec((tm, tk), lambda i, j, kk: (i, kk))
    if mode == "nt":
        b_spec = pl.BlockSpec((tn, tk), lambda i, j, kk: (j, kk))
    else:
        b_spec = pl.BlockSpec((tk, tn), lambda i, j, kk: (kk, j))
    return _pcall(
        body, [a, b], phase, name=name,
        out_shape=jax.ShapeDtypeStruct((m, n), out_dtype),
        grid=(m // tm, n // tn, nk),
        in_specs=[a_spec, b_spec],
        out_specs=pl.BlockSpec((tm, tn), lambda i, j, kk: (i, j)),
        scratch_shapes=[pltpu.VMEM((tm, tn), F32)] if nk > 1 else [],
        compiler_params=_params(3),
    )


TR = 256


def _row_spec(cols=D):
    return pl.BlockSpec((TR, cols), lambda i: (i, 0))


def _vec_spec(cols=D):
    return pl.BlockSpec((1, cols), lambda i: (0, 0))


def _norm_mod(x, nw, scale, shift, name):
    def body(x_ref, nw_ref, sc_ref, sh_ref, h_ref):
        xv = x_ref[...]
        r = lax.rsqrt(jnp.mean(xv * xv, axis=-1, keepdims=True) + EPS)
        h_ref[...] = ((xv * r) * nw_ref[...] * (1.0 + sc_ref[...]) + sh_ref[...]).astype(BF16)

    return pl.pallas_call(
        body, name=name, out_shape=jax.ShapeDtypeStruct((S, D), BF16), grid=(S // TR,),
        in_specs=[_row_spec(), _vec_spec(), _vec_spec(), _vec_spec()], out_specs=_row_spec(),
        compiler_params=_params(1),
    )(x, nw, scale, shift)


def _resid_norm_mod(x, mix, gate, nw, scale, shift, name):
    def body(x_ref, mix_ref, g_ref, nw_ref, sc_ref, sh_ref, x1_ref, h_ref):
        xv = x_ref[...] + g_ref[...] * mix_ref[...]
        x1_ref[...] = xv
        r = lax.rsqrt(jnp.mean(xv * xv, axis=-1, keepdims=True) + EPS)
        h_ref[...] = ((xv * r) * nw_ref[...] * (1.0 + sc_ref[...]) + sh_ref[...]).astype(BF16)

    return pl.pallas_call(
        body, name=name,
        out_shape=(jax.ShapeDtypeStruct((S, D), F32), jax.ShapeDtypeStruct((S, D), BF16)), grid=(S // TR,),
        in_specs=[_row_spec(), _row_spec(), _vec_spec(), _vec_spec(), _vec_spec(), _vec_spec()],
        out_specs=(_row_spec(), _row_spec()),
        compiler_params=_params(1),
    )(x, mix, gate, nw, scale, shift)


def _norm_mod_bwd(dh, xin, nw, scale, dres, name, mix=None, gate=None, phase=None):
    with_gate = mix is not None

    def body(*refs):
        if with_gate:
            dh_ref, x_ref, nw_ref, sc_ref, dres_ref, mix_ref, g_ref, dx_ref, dsh_ref, dsc_ref, dnw_ref, dg_ref, dmix_ref = refs
        else:
            dh_ref, x_ref, nw_ref, sc_ref, dres_ref, dx_ref, dsh_ref, dsc_ref, dnw_ref = refs
        i = pl.program_id(0)
        dhv = dh_ref[...]
        xv = x_ref[...]
        r = lax.rsqrt(jnp.mean(xv * xv, axis=-1, keepdims=True) + EPS)
        xn = xv * r
        nwv = nw_ref[...]
        one_sc = 1.0 + sc_ref[...]
        dxn = dhv * nwv * one_sc
        dx = dres_ref[...] + r * (dxn - xn * jnp.mean(dxn * xn, axis=-1, keepdims=True))
        dx_ref[...] = dx

        @pl.when(i == 0)
        def _():
            dsh_ref[...] = jnp.zeros_like(dsh_ref)
            dsc_ref[...] = jnp.zeros_like(dsc_ref)
            dnw_ref[...] = jnp.zeros_like(dnw_ref)
            if with_gate:
                dg_ref[...] = jnp.zeros_like(dg_ref)

        dsh_ref[...] += jnp.sum(dhv, axis=0, keepdims=True)
        dsc_ref[...] += jnp.sum(dhv * xn * nwv, axis=0, keepdims=True)
        dnw_ref[...] += jnp.sum(dhv * xn * one_sc, axis=0, keepdims=True)
        if with_gate:
            dg_ref[...] += jnp.sum(dx * mix_ref[...], axis=0, keepdims=True)
            dmix_ref[...] = (dx * g_ref[...]).astype(BF16)

    vec = jax.ShapeDtypeStruct((1, D), F32)
    ins = [dh, xin, nw, scale, dres]
    in_specs = [_row_spec(), _row_spec(), _vec_spec(), _vec_spec(), _row_spec()]
    outs = [jax.ShapeDtypeStruct((S, D), F32), vec, vec, vec]
    out_specs = [_row_spec(), _vec_spec(), _vec_spec(), _vec_spec()]
    if with_gate:
        ins += [mix, gate]
        in_specs += [_row_spec(), _vec_spec()]
        outs += [vec, jax.ShapeDtypeStruct((S, D), BF16)]
        out_specs += [_vec_spec(), _row_spec()]
    return _pcall(
        body, ins, phase, name=name, out_shape=tuple(outs), grid=(S // TR,), in_specs=in_specs,
        out_specs=tuple(out_specs), compiler_params=_params(1),
    )


def _tri(lower):
    row = lax.broadcasted_iota(jnp.int32, (CH, CH), 0)
    col = lax.broadcasted_iota(jnp.int32, (CH, CH), 1)
    return (row >= col) if lower else (col >= row)


def _hgrn_gates(hq, hf, lb):
    sig = _sigmoid(hf)
    f = lb + (1.0 - lb) * sig
    g = jnp.log(f)
    sq = _sigmoid(hq)
    return sig, f, g, 1.0 - f, hq * sq, sq


HG_HP = 2
HG_UNROLL = 4


def _proj_col_spec(group):
    return pl.BlockSpec((S, HG_HP * DH), lambda h: (0, group * (H // HG_HP) + h))


def _hgrn_fwd(proj, lb_logits, norm_w, phase=None):
    def body(hq_ref, hf_ref, hi_ref, hg_ref, lbl_ref, nw_ref, out_ref, oraw_ref, st_ref, s_ref):
        nw = nw_ref[...]
        lower = _tri(True)
        ltri = lower.astype(F32)
        s_ref[...] = jnp.zeros_like(s_ref)

        def chunk(n, carry):
            rows = pl.ds(pl.multiple_of(n * CH, CH), CH)
            for j in range(HG_HP):
                cols = slice(j * DH, (j + 1) * DH)
                lb = 1.0 / (1.0 + jnp.exp(lbl_ref[1:2, cols] - lbl_ref[0:1, cols]))
                hq, hf, v, hg = hq_ref[rows, cols], hf_ref[rows, cols], hi_ref[rows, cols], hg_ref[rows, cols]
                _, _, g, kk, q, _ = _hgrn_gates(hq, hf, lb)
                gc = _dot_f32(ltri, g)
                gl = jnp.sum(g, axis=0, keepdims=True)
                qe = q * jnp.exp(gc)
                ke = kk * jnp.exp(gl - gc)
                qh = q * jnp.exp(gc - 0.5 * gl)
                kh = kk * jnp.exp(0.5 * gl - gc)
                st = s_ref[j]
                st_ref[j, pl.ds(n, 1)] = st[None]
                a = jnp.where(lower, _dot(qh, kh, NT), 0.0)
                o = _dot(qe, st, NT) + _dot(a, v)
                s_ref[j] = st * jnp.exp(gl) + _dot(v, ke, TN)
                oraw_ref[rows, cols] = o
                rs = lax.rsqrt(jnp.mean(o * o, axis=-1, keepdims=True) + EPS)
                out_ref[rows, cols] = (o * rs * nw * (hg * _sigmoid(hg))).astype(BF16)
            return carry

        lax.fori_loop(0, NCH, chunk, 0, unroll=HG_UNROLL)

    head_spec = pl.BlockSpec((S, HG_HP * DH), lambda h: (0, h))
    return _pcall(
        body, [proj, proj, proj, proj, lb_logits, norm_w], phase, name="hgrn_fwd",
        out_shape=(jax.ShapeDtypeStruct((S, 2 * HW), BF16), jax.ShapeDtypeStruct((S, HW), F32),
                   jax.ShapeDtypeStruct((H, NCH, DH, DH), F32)),
        grid=(H // HG_HP,),
        in_specs=[_proj_col_spec(0), _proj_col_spec(1), _proj_col_spec(2), _proj_col_spec(3),
                  pl.BlockSpec((2, HG_HP * DH), lambda h: (0, h)), pl.BlockSpec((1, DH), lambda h: (0, 0))],
        out_specs=(head_spec, head_spec, pl.BlockSpec((HG_HP, NCH, DH, DH), lambda h: (h, 0, 0, 0))),
        scratch_shapes=[pltpu.VMEM((HG_HP, DH, DH), F32)],
        compiler_params=_params(1),
    )


def _hgrn_bwd(proj, lb_logits, norm_w, oraw, states, dout, phase=None):
    def body(hq_ref, hf_ref, hi_ref, hg_ref, lbl_ref, nw_ref, oraw_ref, st_ref, do_ref,
             dhq_ref, dhf_ref, dhi_ref, dhg_ref, dlb_ref, dnw_ref, ds_ref, acc_ref):
        h = pl.program_id(0)
        nw = nw_ref[...]
        lower = _tri(True)
        ltri = lower.astype(F32)
        utri = _tri(False).astype(F32)
        last = lax.broadcasted_iota(jnp.int32, (CH, DH), 0) == CH - 1
        ds_ref[...] = jnp.zeros_like(ds_ref)
        acc_ref[...] = jnp.zeros_like(acc_ref)

        @pl.when(h == 0)
        def _():
            dnw_ref[...] = jnp.zeros_like(dnw_ref)

        def chunk(i, carry):
            n = NCH - 1 - i
            rows = pl.ds(pl.multiple_of(n * CH, CH), CH)
            for j in range(HG_HP):
                cols = slice(j * DH, (j + 1) * DH)
                lb = 1.0 / (1.0 + jnp.exp(lbl_ref[1:2, cols] - lbl_ref[0:1, cols]))
                hq, hf, v, hg = hq_ref[rows, cols], hf_ref[rows, cols], hi_ref[rows, cols], hg_ref[rows, cols]
                sig, f, g, kk, q, sq = _hgrn_gates(hq, hf, lb)
                gc = _dot_f32(ltri, g)
                gl = jnp.sum(g, axis=0, keepdims=True)
                eg = jnp.exp(gc)
                ek = jnp.exp(gl - gc)
                gam = jnp.exp(gl)
                ph = jnp.exp(gc - 0.5 * gl)
                pk = jnp.exp(0.5 * gl - gc)
                qe, ke = q * eg, kk * ek
                qh, kh = q * ph, kk * pk
                st = st_ref[j, pl.ds(n, 1)][0]
                dst = ds_ref[j]
                a = jnp.where(lower, _dot(qh, kh, NT), 0.0)
                o = oraw_ref[rows, cols]
                rs = lax.rsqrt(jnp.mean(o * o, axis=-1, keepdims=True) + EPS)
                xh = o * rs
                sg = _sigmoid(hg)
                dgo = do_ref[rows, cols]
                d_on = dgo * (hg * sg)
                dhg_ref[rows, cols] = (dgo * xh * nw * (sg * (1.0 + hg * (1.0 - sg)))).astype(BF16)
                acc_ref[j, 1:2, :] += jnp.sum(d_on * xh, axis=0, keepdims=True)
                dy = d_on * nw
                do = rs * (dy - xh * jnp.mean(dy * xh, axis=-1, keepdims=True))
                dqe = _dot(do, st)
                da = jnp.where(lower, _dot(do, v, NT), 0.0)
                dv = _dot(a, do, TN) + _dot(ke, dst, NT)
                dke = _dot(v, dst)
                dgam = jnp.sum(dst * st, axis=0, keepdims=True)
                dqh = _dot(da, kh)
                dkh = _dot(da, qh, TN)
                dq = dqh * ph + dqe * eg
                dk = dkh * pk + dke * ek
                dgl = jnp.sum(dke * ke, axis=0, keepdims=True) + dgam * gam
                dgc = dqh * qh - dkh * kh + dqe * qe - dke * ke + jnp.where(last, dgl, 0.0)
                dg = _dot_f32(utri, dgc)
                df = dg / f - dk
                dhf_ref[rows, cols] = (df * (1.0 - lb) * sig * (1.0 - sig)).astype(BF16)
                acc_ref[j, 0:1, :] += jnp.sum(df * (1.0 - sig), axis=0, keepdims=True)
                dhq_ref[rows, cols] = (dq * (sq * (1.0 + hq * (1.0 - sq)))).astype(BF16)
                dhi_ref[rows, cols] = dv.astype(BF16)
                ds_ref[j] = dst * gam + _dot(do, qe, TN)
            return carry

        lax.fori_loop(0, NCH, chunk, 0, unroll=HG_UNROLL)
        for j in range(HG_HP):
            dlb_ref[:, j * DH:(j + 1) * DH] = acc_ref[j, 0:1, :]
            dnw_ref[...] += acc_ref[j, 1:2, :]

    head_spec = pl.BlockSpec((S, HG_HP * DH), lambda h: (0, h))
    head_out = jax.ShapeDtypeStruct((S, HW), BF16)
    return _pcall(
        body, [proj, proj, proj, proj, lb_logits, norm_w, oraw, states, dout], phase, name="hgrn_bwd",
        out_shape=(head_out, head_out, head_out, head_out,
                   jax.ShapeDtypeStruct((1, HW), F32), jax.ShapeDtypeStruct((1, DH), F32)),
        grid=(H // HG_HP,),
        in_specs=[_proj_col_spec(0), _proj_col_spec(1), _proj_col_spec(2), _proj_col_spec(3),
                  pl.BlockSpec((2, HG_HP * DH), lambda h: (0, h)), pl.BlockSpec((1, DH), lambda h: (0, 0)),
                  head_spec, pl.BlockSpec((HG_HP, NCH, DH, DH), lambda h: (h, 0, 0, 0)), head_spec],
        out_specs=(head_spec, head_spec, head_spec, head_spec,
                   pl.BlockSpec((1, HG_HP * DH), lambda h: (0, h)), pl.BlockSpec((1, DH), lambda h: (0, 0))),
        scratch_shapes=[pltpu.VMEM((HG_HP, DH, DH), F32), pltpu.VMEM((HG_HP, 8, DH), F32)],
        compiler_params=_params(1),
    )


ATT_SCALE = DH ** -0.5
HEADS_PER_STEP = {1: 8, 4: 1, 16: 1}


def _sub_rows(ref, r, dil, cols):
    if dil == 1:
        return ref[:, cols]
    return ref[pl.ds(r, BLK, stride=dil), cols]


def _set_sub_rows(ref, r, dil, cols, val):
    if dil == 1:
        ref[:, cols] = val
    else:
        ref[pl.ds(r, BLK, stride=dil), cols] = val


def _slopes(dil):
    hp = HEADS_PER_STEP[dil]
    s = jnp.asarray([dil * 2.0 ** (-(h + 1)) for h in range(H)], F32)
    return jnp.broadcast_to(s[:, None], (H, DH)).reshape(H // hp, hp, DH)


def _rms_rows(x, w):
    rs = lax.rsqrt(jnp.mean(x * x, axis=-1, keepdims=True) + EPS)
    return x * rs, rs


def _att_masks():
    qi = lax.broadcasted_iota(jnp.int32, (BLK, BLK), 0)
    kj = lax.broadcasted_iota(jnp.int32, (BLK, BLK), 1)
    steps_c = qi - kj
    steps_p = qi - kj + BLK
    return steps_c, steps_p, steps_c >= 0, steps_p <= BLK


def _qk_prep(proj, q_w, k_w):
    def body(q_ref, k_ref, qw_ref, kw_ref, qn_ref, kn_ref):
        for h in range(H):
            cols = slice(h * DH, (h + 1) * DH)
            qn_ref[:, cols] = _rms_rows(q_ref[:, cols], None)[0] * qw_ref[...]
            kn_ref[:, cols] = _rms_rows(k_ref[:, cols], None)[0] * kw_ref[...]

    out = jax.ShapeDtypeStruct((S, HW), F32)
    wspec = pl.BlockSpec((1, DH), lambda i: (0, 0))
    return pl.pallas_call(
        body, name="qk_prep", out_shape=(out, out), grid=(S // TR,),
        in_specs=[pl.BlockSpec((TR, HW), lambda i: (i, 4)), pl.BlockSpec((TR, HW), lambda i: (i, 5)), wspec, wspec],
        out_specs=(_row_spec(HW), _row_spec(HW)),
        compiler_params=_params(1),
    )(proj, proj, q_w, k_w)


def _qk_norm_bwd(proj, dqn, dkn, dv, q_w, k_w):
    def body(q_ref, k_ref, dqn_ref, dkn_ref, dv_ref, qw_ref, kw_ref, dq_ref, dk_ref, dvo_ref, dqw_ref, dkw_ref):
        i = pl.program_id(0)

        @pl.when(i == 0)
        def _():
            dqw_ref[...] = jnp.zeros_like(dqw_ref)
            dkw_ref[...] = jnp.zeros_like(dkw_ref)

        dvo_ref[...] = dv_ref[...].astype(BF16)
        for x_ref, d_ref, w_ref, o_ref, dw_ref in ((q_ref, dqn_ref, qw_ref, dq_ref, dqw_ref),
                                                   (k_ref, dkn_ref, kw_ref, dk_ref, dkw_ref)):
            for h in range(H):
                cols = slice(h * DH, (h + 1) * DH)
                xh, rs = _rms_rows(x_ref[:, cols], None)
                d = d_ref[:, cols]
                dw_ref[...] += jnp.sum(d * xh, axis=0, keepdims=True)
                dy = d * w_ref[...]
                o_ref[:, cols] = (rs * (dy - xh * jnp.mean(dy * xh, axis=-1, keepdims=True))).astype(BF16)

    big = jax.ShapeDtypeStruct((S, HW), BF16)
    small = jax.ShapeDtypeStruct((1, DH), F32)
    wspec = pl.BlockSpec((1, DH), lambda i: (0, 0))
    return pl.pallas_call(
        body, name="qk_norm_bwd", out_shape=(big, big, big, small, small), grid=(S // TR,),
        in_specs=[pl.BlockSpec((TR, HW), lambda i: (i, 4)), pl.BlockSpec((TR, HW), lambda i: (i, 5)),
                  _row_spec(HW), _row_spec(HW), _row_spec(HW), wspec, wspec],
        out_specs=(_row_spec(HW), _row_spec(HW), _row_spec(HW), wspec, wspec),
        compiler_params=_params(1),
    )(proj, proj, dqn, dkn, dv, q_w, k_w)


def _attn_delta(do, o):
    def body(do_ref, o_ref, d_ref):
        for h in range(H):
            cols = slice(h * DH, (h + 1) * DH)
            d_ref[:, cols] = jnp.broadcast_to(jnp.sum(do_ref[:, cols] * o_ref[:, cols], axis=-1, keepdims=True), (TR, DH))

    return pl.pallas_call(
        body, name="attn_delta", out_shape=jax.ShapeDtypeStruct((S, HW), F32), grid=(S // TR,),
        in_specs=[pl.BlockSpec((TR, HW), lambda i: (i, 1)), _row_spec(HW)], out_specs=_row_spec(HW),
        compiler_params=_params(1),
    )(do, o)


def _attn_fwd(proj, qn, kn, dil, phase=None):
    hp = HEADS_PER_STEP[dil]
    rows, ng, nb = BLK * dil, H // hp, S // (BLK * dil)

    def body(q_ref, kc_ref, kp_ref, vc_ref, vp_ref, sl_ref, o_ref, lse_ref):
        n = pl.program_id(0)
        steps_c, steps_p, ok_c, ok_p = _att_masks()
        ok_p = jnp.logical_and(ok_p, n > 0)
        fc, fp = steps_c.astype(F32), steps_p.astype(F32)
        for hh in range(hp):
            cols = slice(hh * DH, (hh + 1) * DH)
            sl = sl_ref[0, hh:hh + 1, :]
            for r in range(dil):
                sub = lambda ref: _sub_rows(ref, r, dil, cols)
                qv = sub(q_ref)
                sc = jnp.where(ok_c, _dot(qv, sub(kc_ref), NT) * ATT_SCALE - sl * fc, NEG)
                sp = jnp.where(ok_p, _dot(qv, sub(kp_ref), NT) * ATT_SCALE - sl * fp, NEG)
                m = jnp.maximum(jnp.max(sc, axis=-1, keepdims=True), jnp.max(sp, axis=-1, keepdims=True))
                pc, pp = jnp.exp(sc - m), jnp.exp(sp - m)
                den = jnp.sum(pc, axis=-1, keepdims=True) + jnp.sum(pp, axis=-1, keepdims=True)
                o = (_dot(pc, sub(vc_ref)) + _dot(pp, sub(vp_ref))) / den
                _set_sub_rows(o_ref, r, dil, cols, o)
                _set_sub_rows(lse_ref, r, dil, cols, jnp.broadcast_to(m + jnp.log(den), (BLK, DH)))

    cur = pl.BlockSpec((rows, hp * DH), lambda n, g: (n, g))
    prev = pl.BlockSpec((rows, hp * DH), lambda n, g: (jnp.maximum(n - 1, 0), g))
    vcur = pl.BlockSpec((rows, hp * DH), lambda n, g: (n, 6 * ng + g))
    vprev = pl.BlockSpec((rows, hp * DH), lambda n, g: (jnp.maximum(n - 1, 0), 6 * ng + g))
    out = jax.ShapeDtypeStruct((S, HW), F32)
    return _pcall(
        body, [qn, kn, kn, proj, proj, _slopes(dil)], phase,
        name=f"attn_fwd_d{dil}", out_shape=(out, out), grid=(nb, ng),
        in_specs=[cur, cur, prev, vcur, vprev, pl.BlockSpec((1, hp, DH), lambda n, g: (g, 0, 0))],
        out_specs=(cur, cur),
        compiler_params=_params(2),
    )


def _attn_merge(outs, lses, cat):
    def body(o1, o2, o3, l1, l2, l3, cat_ref, ob_ref, of_ref, lse_ref):
        a, b, c = l1[...], l2[...], l3[...]
        m = jnp.maximum(jnp.maximum(a, b), c)
        ea, eb, ec = jnp.exp(a - m), jnp.exp(b - m), jnp.exp(c - m)
        den = ea + eb + ec
        o = (ea * o1[...] + eb * o2[...] + ec * o3[...]) / den
        ob_ref[...] = o.astype(BF16)
        of_ref[...] = o
        lse_ref[...] = m + jnp.log(den)

    f = jax.ShapeDtypeStruct((S, HW), F32)
    return pl.pallas_call(
        body, name="attn_merge", out_shape=(jax.ShapeDtypeStruct((S, 2 * HW), BF16), f, f), grid=(S // TR,),
        in_specs=[_row_spec(HW)] * 6 + [pl.BlockSpec(memory_space=pl.ANY)],
        out_specs=(pl.BlockSpec((TR, HW), lambda i: (i, 1)), _row_spec(HW), _row_spec(HW)),
        input_output_aliases={6: 0},
        compiler_params=_params(1),
    )(*outs, *lses, cat)


def _attn_bwd(proj, qn, kn, lse, delta, do, dil, acc, phase=None):
    hp = HEADS_PER_STEP[dil]
    rows, ng, nb = BLK * dil, H // hp, S // (BLK * dil)
    has_acc = acc is not None

    def body(*refs):
        q_ref, qn_ref, kp_ref, kc_ref, vp_ref, vc_ref, l_ref, ln_ref, d_ref, dn_ref, do_ref, don_ref, sl_ref = refs[:13]
        refs = refs[13:]
        if has_acc:
            aq_ref, ak_ref, av_ref = refs[:3]
            refs = refs[3:]
        dq_ref, dk_ref, dv_ref = refs
        m = pl.program_id(0)
        steps_c, steps_p, ok_c, ok_p = _att_masks()
        ok_prev = jnp.logical_and(ok_p, m > 0)
        ok_next = jnp.logical_and(ok_p, m < nb - 1)
        fc, fp = steps_c.astype(F32), steps_p.astype(F32)
        for hh in range(hp):
            cols = slice(hh * DH, (hh + 1) * DH)
            sl = sl_ref[0, hh:hh + 1, :]
            for r in range(dil):
                sub = lambda ref: _sub_rows(ref, r, dil, cols)
                qv, qnv, kcv, kpv = sub(q_ref), sub(qn_ref), sub(kc_ref), sub(kp_ref)
                vc, vp = sub(vc_ref), sub(vp_ref)
                dov, donv = sub(do_ref), sub(don_ref)
                lse_m, lse_n = sub(l_ref)[:, 0:1], sub(ln_ref)[:, 0:1]
                delta_m, delta_n = sub(d_ref)[:, 0:1], sub(dn_ref)[:, 0:1]
                p_c = jnp.where(ok_c, jnp.exp(_dot(qv, kcv, NT) * ATT_SCALE - sl * fc - lse_m), 0.0)
                p_p = jnp.where(ok_prev, jnp.exp(_dot(qv, kpv, NT) * ATT_SCALE - sl * fp - lse_m), 0.0)
                p_n = jnp.where(ok_next, jnp.exp(_dot(qnv, kcv, NT) * ATT_SCALE - sl * fp - lse_n), 0.0)
                ds_c = p_c * (_dot(dov, vc, NT) - delta_m)
                ds_p = p_p * (_dot(dov, vp, NT) - delta_m)
                ds_n = p_n * (_dot(donv, vc, NT) - delta_n)
                dqn = (_dot(ds_c, kcv) + _dot(ds_p, kpv)) * ATT_SCALE
                dkn = (_dot(ds_c, qv, TN) + _dot(ds_n, qnv, TN)) * ATT_SCALE
                dv = _dot(p_c, dov, TN) + _dot(p_n, donv, TN)
                if has_acc:
                    dqn, dkn, dv = dqn + sub(aq_ref), dkn + sub(ak_ref), dv + sub(av_ref)
                _set_sub_rows(dq_ref, r, dil, cols, dqn)
                _set_sub_rows(dk_ref, r, dil, cols, dkn)
                _set_sub_rows(dv_ref, r, dil, cols, dv)

    def shifted(shift, m):
        if shift < 0:
            return jnp.maximum(m - 1, 0)
        if shift > 0:
            return jnp.minimum(m + 1, nb - 1)
        return m

    def spec(shift, group=0):
        return pl.BlockSpec((rows, hp * DH), lambda m, g: (shifted(shift, m), group * ng + g))

    ins = [qn, qn, kn, kn, proj, proj, lse, lse, delta, delta, do, do, _slopes(dil)]
    in_specs = [spec(0), spec(1), spec(-1), spec(0), spec(-1, 6), spec(0, 6), spec(0), spec(1), spec(0), spec(1),
                spec(0, 1), spec(1, 1), pl.BlockSpec((1, hp, DH), lambda m, g: (g, 0, 0))]
    if has_acc:
        ins += list(acc)
        in_specs += [spec(0)] * 3
    big = jax.ShapeDtypeStruct((S, HW), F32)
    return _pcall(
        body, ins, phase, name=f"attn_bwd_d{dil}", out_shape=(big, big, big), grid=(nb, ng),
        in_specs=in_specs, out_specs=(spec(0),) * 3,
        compiler_params=_params(2),
    )


TC = 512
NCT = DFF // TC


def _shift_rows(a, k):
    rows = lax.broadcasted_iota(jnp.int32, a.shape, 0)
    rolled = pltpu.roll(a, k % S, 0)
    if k > 0:
        return jnp.where(rows >= k, rolled, 0.0)
    return jnp.where(rows < S + k, rolled, 0.0)


def _conv_pre(a, w_ref, b_ref):
    return b_ref[...] + w_ref[0:1, :] * _shift_rows(a, 2) + w_ref[1:2, :] * _shift_rows(a, 1) + w_ref[2:3, :] * a


def _conv_gate_fwd(u, conv_w, conv_b, phase=None):
    def body(a_ref, g_ref, w_ref, b_ref, y_ref):
        pre = _conv_pre(a_ref[...].astype(F32), w_ref, b_ref)
        y_ref[...] = (pre * _sigmoid(pre) * g_ref[...].astype(F32)).astype(BF16)

    return _pcall(
        body, [u, u, conv_w, conv_b], phase,
        name="conv_gate_fwd", out_shape=jax.ShapeDtypeStruct((S, DFF), BF16), grid=(NCT,),
        in_specs=[pl.BlockSpec((S, TC), lambda i: (0, i)), pl.BlockSpec((S, TC), lambda i: (0, NCT + i)),
                  pl.BlockSpec((3, TC), lambda i: (0, i)), pl.BlockSpec((1, TC), lambda i: (0, i))],
        out_specs=pl.BlockSpec((S, TC), lambda i: (0, i)),
        compiler_params=_params(1),
    )


def _conv_gate_bwd(dy, u, conv_w, conv_b, phase=None):
    def body(dy_ref, a_ref, g_ref, w_ref, b_ref, da_ref, dg_ref, db_ref, dw_ref):
        a = a_ref[...].astype(F32)
        dyv = dy_ref[...].astype(F32)
        pre = _conv_pre(a, w_ref, b_ref)
        sg = _sigmoid(pre)
        dg_ref[...] = (dyv * pre * sg).astype(BF16)
        dpre = dyv * g_ref[...].astype(F32) * (sg * (1.0 + pre * (1.0 - sg)))
        da = w_ref[2:3, :] * dpre + w_ref[1:2, :] * _shift_rows(dpre, -1) + w_ref[0:1, :] * _shift_rows(dpre, -2)
        da_ref[...] = da.astype(BF16)
        db_ref[...] = jnp.sum(dpre, axis=0, keepdims=True)
        dw_ref[0:1, :] = jnp.sum(dpre * _shift_rows(a, 2), axis=0, keepdims=True)
        dw_ref[1:2, :] = jnp.sum(dpre * _shift_rows(a, 1), axis=0, keepdims=True)
        dw_ref[2:3, :] = jnp.sum(dpre * a, axis=0, keepdims=True)

    col = pl.BlockSpec((S, TC), lambda i: (0, i))
    half = jax.ShapeDtypeStruct((S, DFF), BF16)
    return _pcall(
        body, [dy, u, u, conv_w, conv_b], phase, name="conv_gate_bwd",
        out_shape=(half, half, jax.ShapeDtypeStruct((1, DFF), F32), jax.ShapeDtypeStruct((3, DFF), F32)),
        grid=(NCT,),
        in_specs=[col, col, pl.BlockSpec((S, TC), lambda i: (0, NCT + i)),
                  pl.BlockSpec((3, TC), lambda i: (0, i)), pl.BlockSpec((1, TC), lambda i: (0, i))],
        out_specs=(col, col, pl.BlockSpec((1, TC), lambda i: (0, i)), pl.BlockSpec((3, TC), lambda i: (0, i))),
        compiler_params=_params(1),
    )


def _out_loss(z, x1, target, gate):
    def body(z_ref, x_ref, t_ref, g_ref, do_ref, dz_ref, dg_ref, loss_ref):
        i = pl.program_id(0)
        zv = z_ref[...]
        gv = g_ref[...]
        err = x_ref[...] + gv * zv - t_ref[...]
        dout = err * (1.0 / D)
        do_ref[...] = dout
        dz_ref[...] = (dout * gv).astype(BF16)

        @pl.when(i == 0)
        def _():
            dg_ref[...] = jnp.zeros_like(dg_ref)
            loss_ref[...] = jnp.zeros_like(loss_ref)

        dg_ref[...] += jnp.sum(dout * zv, axis=0, keepdims=True)
        part = jnp.sum(jnp.sum(err * err, axis=0, keepdims=True), axis=-1, keepdims=True) * (0.5 / D)
        lane = lax.broadcasted_iota(jnp.int32, (1, 128), 1)
        loss_ref[...] += jnp.where(lane == 0, part, 0.0)

    return pl.pallas_call(
        body, name="out_loss",
        out_shape=(jax.ShapeDtypeStruct((S, D), F32), jax.ShapeDtypeStruct((S, D), BF16),
                   jax.ShapeDtypeStruct((1, D), F32), jax.ShapeDtypeStruct((1, 128), F32)),
        grid=(S // TR,),
        in_specs=[_row_spec(), _row_spec(), _row_spec(), _vec_spec()],
        out_specs=(_row_spec(), _row_spec(), _vec_spec(), _vec_spec(128)),
        compiler_params=_params(1),
    )(z, x1, target, gate)


ADA_COLS = 6 * D // N_CHIPS
TA = 512


def _ada_fwd(c_all, w_shard, b_shard):
    def body(c_ref, w_ref, b_ref, o_ref):
        cv = c_ref[...]
        o_ref[...] = _dot_f32(cv * _sigmoid(cv), w_ref[...]) + b_ref[...]

    return pl.pallas_call(
        body, name="ada_fwd", out_shape=jax.ShapeDtypeStruct((N_DEV, ADA_COLS), F32), grid=(ADA_COLS // TA,),
        in_specs=[pl.BlockSpec((N_DEV, D), lambda j: (0, 0)), pl.BlockSpec((D, TA), lambda j: (0, j)),
                  pl.BlockSpec((1, TA), lambda j: (0, j))],
        out_specs=pl.BlockSpec((N_DEV, TA), lambda j: (0, j)),
        compiler_params=_params(1),
    )(c_all, w_shard, b_shard)


def _ada_bwd(c_all, dmod_shard):
    def body(c_ref, d_ref, o_ref):
        cv = c_ref[...]
        o_ref[...] = lax.dot_general(cv * _sigmoid(cv), d_ref[...], TN, precision=lax.Precision.HIGHEST,
                                     preferred_element_type=F32)

    return pl.pallas_call(
        body, name="ada_bwd", out_shape=jax.ShapeDtypeStruct((D, ADA_COLS), F32), grid=(ADA_COLS // TA,),
        in_specs=[pl.BlockSpec((N_DEV, D), lambda j: (0, 0)), pl.BlockSpec((N_DEV, TA), lambda j: (0, j))],
        out_specs=pl.BlockSpec((D, TA), lambda j: (0, j)),
        compiler_params=_params(1),
    )(c_all, dmod_shard)


def _sum_rows(g, name):
    rows, n = g.shape

    def body(g_ref, o_ref):
        acc = g_ref[0:1, :]
        for i in range(1, rows):
            acc = acc + g_ref[i:i + 1, :]
        o_ref[...] = acc

    return pl.pallas_call(
        body, name=name, out_shape=jax.ShapeDtypeStruct((1, n), F32),
        in_specs=[VMEM_SPEC], out_specs=VMEM_SPEC,
    )(g)


def _lb_grad(lb_logits, dlb):
    def body(l_ref, d_ref, o_ref):
        p = 1.0 / (1.0 + jnp.exp(l_ref[1:2, :] - l_ref[0:1, :]))
        t = d_ref[...] * p * (1.0 - p)
        o_ref[0:1, :] = t
        o_ref[1:2, :] = -t

    return pl.pallas_call(
        body, name="lb_grad", out_shape=jax.ShapeDtypeStruct((2, HW), F32),
        in_specs=[VMEM_SPEC, VMEM_SPEC], out_specs=VMEM_SPEC,
    )(lb_logits, dlb)


def _adamw(w, g, m, v, name):
    rows, cols = w.shape
    tr = rows
    if rows * cols * 4 > ADAM_BLOCK_BYTES:
        tr = _pick(rows, [t for t in (256, 128, 64, 32, 16, 8) if t * cols * 4 <= ADAM_BLOCK_BYTES])

    def body(w_ref, g_ref, m_ref, v_ref, d_ref, mo_ref, vo_ref):
        gv = g_ref[...]
        m2 = ADAM_B1 * m_ref[...] + (1.0 - ADAM_B1) * gv
        v2 = ADAM_B2 * v_ref[...] + (1.0 - ADAM_B2) * (gv * gv)
        m_hat = m2 / (1.0 - ADAM_B1 ** ADAM_STEP)
        v_hat = v2 / (1.0 - ADAM_B2 ** ADAM_STEP)
        d_ref[...] = -ADAM_LR * (m_hat / (jnp.sqrt(v_hat) + ADAM_EPS) + ADAM_WD * w_ref[...])
        mo_ref[...] = m2
        vo_ref[...] = v2

    spec = pl.BlockSpec((tr, cols), lambda i: (i, 0))
    out = jax.ShapeDtypeStruct((rows, cols), F32)
    return pl.pallas_call(
        body, name=name, out_shape=(out, out, out), grid=(rows // tr,),
        in_specs=[spec] * 4, out_specs=(spec,) * 3,
        compiler_params=_params(1),
    )(w, g, m, v)


SC_TILES = 32
SC_LANES = 16
SC_COL_PARTS = 2
SC_CHUNK_ROWS = 8


def _adamw_sc(w, g, m, v, name):
    rows, cols = w.shape
    band, part = rows // (SC_TILES // SC_COL_PARTS), cols // SC_COL_PARTS
    assert band % SC_CHUNK_ROWS == 0 and part % SC_LANES == 0, (rows, cols)

    def body(w_hbm, g_hbm, m_hbm, v_hbm, d_hbm, mo_hbm, vo_hbm, wb, gb, mb, vb, db):
        tile = lax.axis_index("sc_subcore") * 2 + lax.axis_index("sc_core")
        row0 = (tile // SC_COL_PARTS) * band
        col0 = (tile % SC_COL_PARTS) * part

        @pl.loop(0, band, step=SC_CHUNK_ROWS)
        def _(r):
            at = lambda ref: ref.at[pl.ds(row0 + r, SC_CHUNK_ROWS), pl.ds(col0, part)]
            pltpu.sync_copy(at(w_hbm), wb)
            pltpu.sync_copy(at(g_hbm), gb)
            pltpu.sync_copy(at(m_hbm), mb)
            pltpu.sync_copy(at(v_hbm), vb)

            @pl.loop(0, SC_CHUNK_ROWS)
            def _(i):
                @pl.loop(0, part, step=SC_LANES)
                def _(j):
                    sl = (i, pl.ds(j, SC_LANES))
                    gv = gb[sl]
                    m2 = ADAM_B1 * mb[sl] + (1.0 - ADAM_B1) * gv
                    v2 = ADAM_B2 * vb[sl] + (1.0 - ADAM_B2) * (gv * gv)
                    m_hat = m2 / (1.0 - ADAM_B1 ** ADAM_STEP)
                    v_hat = v2 / (1.0 - ADAM_B2 ** ADAM_STEP)
                    db[sl] = -ADAM_LR * (m_hat / (jnp.sqrt(v_hat) + ADAM_EPS) + ADAM_WD * wb[sl])
                    mb[sl] = m2
                    vb[sl] = v2

            pltpu.sync_copy(db, at(d_hbm))
            pltpu.sync_copy(mb, at(mo_hbm))
            pltpu.sync_copy(vb, at(vo_hbm))

    out = jax.ShapeDtypeStruct((rows, cols), F32)
    buf = pltpu.VMEM((SC_CHUNK_ROWS, part), F32)
    return pl.kernel(
        body, name=name, out_type=(out, out, out),
        mesh=plsc.VectorSubcoreMesh(core_axis_name="sc_core", subcore_axis_name="sc_subcore"),
        scratch_types=[buf] * 5,
    )(w, g, m, v)


W_IN, W_OUT, W_UP, W_DOWN = range(4)
IN_CHUNKS = 4
W_INQ = 4


def _join_row_chunks(chunks):
    return jnp.concatenate(chunks, axis=0)


def _sequence_step(x, target, mod, norm1_w, lb_logits, hg_norm_w, q_norm_w, k_norm_w, norm2_w, conv_w, conv_b, net):
    shift1, scale1, gate1, shift2, scale2, gate2 = [mod[:, i * D:(i + 1) * D] for i in range(6)]

    def run(name, fn, *args, **kw):
        phase = net.host(name)
        if phase is None:
            return fn(*args, **kw)
        res, comm = fn(*args, phase=phase, **kw)
        net.done(name, comm)
        return res

    h = _norm_mod(x, norm1_w, scale1, shift1, "norm1_fwd")
    proj = run("proj_fwd", _matmul, h, net.full[W_IN], "nn", F32, "proj_fwd")
    cat, o_raw, states = run("hgrn_fwd", _hgrn_fwd, proj, lb_logits, hg_norm_w)
    qn, kn = _qk_prep(proj, q_norm_w, k_norm_w)
    att = [run(f"attn_fwd_d{dil}", _attn_fwd, proj, qn, kn, dil) for dil in DILS]
    cat, b_out_f32, lse = _attn_merge([t[0] for t in att], [t[1] for t in att], cat)
    mix = run("mix_fwd", _matmul, cat, net.full[W_OUT], "nn", F32, "mix_fwd")
    x1, h2 = _resid_norm_mod(x, mix, gate1, norm2_w, scale2, shift2, "norm2_fwd")
    u = run("up_fwd", _matmul, h2, net.full[W_UP], "nn", BF16, "up_fwd")
    yact = run("conv_gate_fwd", _conv_gate_fwd, u, conv_w, conv_b)
    net.alone("before_down_fwd")
    z = _matmul(yact, net.full[W_DOWN], "nn", F32, "down_fwd")
    dout, dz, dgate2, loss_row = _out_loss(z, x1, target, gate2)

    net.grad[W_DOWN] = _matmul(yact, dz, "tn", BF16, "down_bwd_w")
    dyact = run("down_bwd_x", _matmul, dz, net.full[W_DOWN], "nt", BF16, "down_bwd_x")
    da, dg, dconv_b, dconv_w = run("conv_gate_bwd", _conv_gate_bwd, dyact, u, conv_w, conv_b)
    du = jnp.concatenate([da, dg], axis=1)
    net.grad[W_UP] = run("up_bwd_w", _matmul, h2, du, "tn", BF16, "up_bwd_w")
    dh2 = run("up_bwd_x", _matmul, du, net.full[W_UP], "nt", F32, "up_bwd_x")
    dx1, dshift2, dscale2, dnorm2, dgate1, dmix = run(
        "norm2_bwd", _norm_mod_bwd, dh2, x1, norm2_w, scale2, dout, "norm2_bwd", mix=mix, gate=gate1)
    net.grad[W_OUT] = run("mix_bwd_w", _matmul, cat, dmix, "tn", BF16, "mix_bwd_w")
    dcat = run("mix_bwd_x", _matmul, dmix, net.full[W_OUT], "nt", F32, "mix_bwd_x")
    dhq, dhf, dhi, dhg, dlb, dhg_norm = run("hgrn_bwd", _hgrn_bwd, proj, lb_logits, hg_norm_w, o_raw, states, dcat)
    delta = _attn_delta(dcat, b_out_f32)
    acc = None
    for dil in DILS:
        acc = run(f"attn_bwd_d{dil}", _attn_bwd, proj, qn, kn, lse, delta, dcat, dil, acc)
    daq, dak, dav, dq_norm, dk_norm = _qk_norm_bwd(proj, *acc, q_norm_w, k_norm_w)
    dproj = jnp.concatenate([dhq, dhf, dhi, dhg, daq, dak, dav], axis=1)
    for q in range(IN_CHUNKS):
        net.grad[W_INQ + q] = run(f"proj_bwd_w_{q}", _matmul, h, dproj, "tn", BF16, f"proj_bwd_w_{q}",
                                  m_blocks=(D // IN_CHUNKS, [q]))
    dh = run("proj_bwd_x", _matmul, dproj, net.full[W_IN], "nt", F32, "proj_bwd_x")
    grad_x, dshift1, dscale1, dnorm1 = run("norm1_bwd", _norm_mod_bwd, dh, x, norm1_w, scale1, dx1, "norm1_bwd")

    dmod = jnp.concatenate([dshift1, dscale1, dgate1, dshift2, dscale2, dgate2], axis=1)
    small = dict(norm1_w=dnorm1, lb=dlb, hg_norm_w=dhg_norm, q_norm_w=dq_norm, k_norm_w=dk_norm,
                 norm2_w=dnorm2, conv_b=dconv_b, conv_w=dconv_w)
    return loss_row, grad_x, dmod, small


def _place():
    x, y, c = lax.axis_index("x"), lax.axis_index("y"), lax.axis_index("c")
    others = [(1 - x, y), (x, 1 - y), (1 - x, 1 - y)]
    return x, y, c, others


def _remote(src, dst, send_sems, recv_sems, k, to):
    return pltpu.make_async_remote_copy(src_ref=src, dst_ref=dst, send_sem=send_sems.at[k], recv_sem=recv_sems.at[k],
                                        device_id=to, device_id_type=MESH)


class _Phase:
    def __init__(self):
        self.ins, self.outs, self.aliases, self.groups, self.n = [], [], {}, [], 0

    def add(self, ins, outs, aliases, n, copies):
        i0, o0 = len(self.ins), len(self.outs)
        self.ins += list(ins)
        self.outs += list(outs)
        self.aliases.update({i0 + i: o0 + o for i, o in aliases.items()})
        self.groups.append((slice(i0, i0 + len(ins)), slice(o0, o0 + len(outs)), copies))
        self.n += n
        return slice(o0, o0 + len(outs))

    def build(self, cin, cout, send_sems, recv_sems, landing):
        res = []
        for si, so, copies in self.groups:
            for src, dst, land, peer in copies(cin[si], cout[so]):
                k = len(res)
                res.append(_remote(land, land, send_sems, recv_sems, k, peer) if landing
                           else _remote(src, dst, send_sems, recv_sems, k, peer))
        assert len(res) == self.n
        return res


def _pcall(body, args, phase=None, **kw):
    if phase is None or not phase.groups:
        res = pl.pallas_call(body, **kw)(*args)
        return res if phase is None else (res, ())
    grid = tuple(kw.pop("grid", ()))
    out_shape, out_specs = kw.pop("out_shape"), kw.pop("out_specs")
    single = not isinstance(out_shape, (tuple, list))
    out_shape = [out_shape] if single else list(out_shape)
    out_specs = [out_specs] if single else list(out_specs)
    scratch = list(kw.pop("scratch_shapes", ()))
    n_in, n_out, n_ci, n_co = len(args), len(out_shape), len(phase.ins), len(phase.outs)

    def hosted(*refs):
        ins, cin = refs[:n_in], refs[n_in:n_in + n_ci]
        outs = refs[n_in + n_ci:n_in + n_ci + n_out]
        cout = refs[n_in + n_ci + n_out:n_in + n_ci + n_out + n_co]
        scr, send_sems, recv_sems = refs[n_in + n_ci + n_out + n_co:-2], refs[-2], refs[-1]
        ids = [pl.program_id(a) for a in range(len(grid))]

        def start():
            for cp in phase.build(cin, cout, send_sems, recv_sems, False):
                cp.start()

        def finish():
            for cp in phase.build(cin, cout, send_sems, recv_sems, False):
                cp.wait_send()
            for cp in phase.build(cin, cout, send_sems, recv_sems, True):
                cp.wait_recv()

        if grid:
            first = functools.reduce(jnp.logical_and, [i == 0 for i in ids])
            last = functools.reduce(jnp.logical_and, [i == g - 1 for i, g in zip(ids, grid)])
            pl.when(first)(start)
            body(*ins, *outs, *scr)
            pl.when(last)(finish)
        else:
            start()
            body(*ins, *outs, *scr)
            finish()

    res = pl.pallas_call(
        hosted, out_shape=tuple(out_shape + phase.outs), grid=grid,
        in_specs=list(kw.pop("in_specs")) + [HBM_SPEC] * n_ci, out_specs=tuple(out_specs + [HBM_SPEC] * n_co),
        input_output_aliases={n_in + i: n_out + o for i, o in phase.aliases.items()},
        scratch_shapes=scratch + [pltpu.SemaphoreType.DMA((phase.n,)), pltpu.SemaphoreType.DMA((phase.n,))],
        **kw,
    )(*args, *phase.ins)
    outs = res[:n_out]
    return (outs[0] if single else tuple(outs)), tuple(res[n_out:])


def _comm_only(name, phase):
    return _pcall(lambda: None, [], phase, name=name, out_shape=(), in_specs=[], out_specs=())[1]


def _all_gather_rows(block, name):
    m_per, n = block.shape

    def body(x_ref, out_ref, send_sems, recv_sems, local_sem):
        x, y, c, chips = _place()
        me, sibling = (x, y, c), (x, y, 1 - c)

        def rows(px, py, pc):
            return out_ref.at[pl.ds((4 * px + 2 * py + pc) * m_per, m_per), :]

        def copy(k, blk, to, src=None):
            return _remote(rows(*blk) if src is None else src, rows(*blk), send_sems, recv_sems, k, to)

        mine = pltpu.make_async_copy(x_ref, rows(*me), local_sem)
        mine.start()
        first = [copy(0, me, sibling, src=x_ref)]
        first += [copy(1 + j, me, (*chip, c), src=x_ref) for j, chip in enumerate(chips)]
        for cp in first:
            cp.start()
        passed = [copy(4 + j, (*chip, c), sibling) for j, chip in enumerate(chips)]
        for j, chip in enumerate(chips):
            copy(1 + j, (*chip, c), me).wait_recv()
            passed[j].start()
        copy(0, sibling, me).wait_recv()
        for j, chip in enumerate(chips):
            copy(4 + j, (*chip, 1 - c), me).wait_recv()
        for cp in first + passed:
            cp.wait_send()
        mine.wait()

    return pl.pallas_call(
        body, name=name, out_shape=jax.ShapeDtypeStruct((N_DEV * m_per, n), block.dtype),
        in_specs=[VMEM_SPEC], out_specs=VMEM_SPEC,
        scratch_shapes=[pltpu.SemaphoreType.DMA((7,)), pltpu.SemaphoreType.DMA((7,)), pltpu.SemaphoreType.DMA],
    )(block)


class _Geo:
    def __init__(self, kind, shard_shape):
        self.kind = kind
        self.shard_shape = tuple(shard_shape)
        self.hr, self.hc = shard_shape[0] // 2, shard_shape[1]
        if kind == "col":
            self.full_shape = (shard_shape[0], N_CHIPS * shard_shape[1])
        else:
            self.full_shape = (N_CHIPS * shard_shape[0], shard_shape[1])

    def full_shard(self, ref, j):
        if self.kind == "col":
            return ref.at[:, pl.ds(pl.multiple_of(j * self.hc, 128), self.hc)]
        return ref.at[pl.ds(pl.multiple_of(j * 2 * self.hr, 16), 2 * self.hr), :]

    def full_half(self, ref, j, c):
        if self.kind == "col":
            return ref.at[pl.ds(pl.multiple_of(c * self.hr, 16), self.hr),
                          pl.ds(pl.multiple_of(j * self.hc, 128), self.hc)]
        return ref.at[pl.ds(pl.multiple_of((2 * j + c) * self.hr, 16), self.hr), :]

    def piece(self, ref, j, c, p0, p1, pieces, part=None):
        pr = self.hr // pieces
        off, n = p0 * pr, (p1 - p0) * pr
        if part is not None:
            top = (n // 32) * 16
            off, n = (off, top) if part == 0 else (off + top, n - top)
        if self.kind == "col":
            return ref.at[pl.ds(pl.multiple_of(c * self.hr + off, 16), n),
                          pl.ds(pl.multiple_of(j * self.hc, 128), self.hc)]
        return ref.at[pl.ds(pl.multiple_of((2 * j + c) * self.hr + off, 16), n), :]


WEIGHT_KINDS = ("col", "row", "col", "row")
NW = len(WEIGHT_KINDS)


def _comm_call(body, name, ins, outs, n_remote, aliases=None):
    return pl.pallas_call(
        body, name=name, out_shape=tuple(outs),
        in_specs=[HBM_SPEC] * len(ins), out_specs=tuple([HBM_SPEC] * len(outs)),
        input_output_aliases=aliases or {},
        scratch_shapes=[pltpu.SemaphoreType.DMA((n_remote,)), pltpu.SemaphoreType.DMA((n_remote,))],
    )(*ins)


ROW_TILES = (256, 176, 128, 64, 32, 16)


def _cast_into_full(shard, geo, place, name):
    rs, cs = shard.shape
    tr = _pick(rs, ROW_TILES)
    nb = rs // tr

    def body(place_ref, s_ref, o_ref):
        o_ref[...] = s_ref[...].astype(BF16)

    if geo.kind == "col":
        out_map = lambda i, place_ref: (i, place_ref[0])
    else:
        out_map = lambda i, place_ref: (place_ref[0] * nb + i, 0)
    return pl.pallas_call(
        body, name=name, out_shape=jax.ShapeDtypeStruct(geo.full_shape, BF16),
        grid_spec=pltpu.PrefetchScalarGridSpec(
            num_scalar_prefetch=1, grid=(nb,),
            in_specs=[pl.BlockSpec((tr, cs), lambda i, place_ref: (i, 0))],
            out_specs=pl.BlockSpec((tr, cs), out_map)),
        compiler_params=_params(1),
    )(place, shard)


def _gather_weight(full, geo, pieces, name):
    per = 8

    def body(in_ref, ref, send_sems, recv_sems):
        x, y, c, _ = _place()
        j, jx, jy, jo = 2 * x + y, 2 * (1 - x) + y, 2 * x + (1 - y), 2 * (1 - x) + (1 - y)
        nx, ny, sib = (1 - x, y, c), (x, 1 - y, c), (x, y, 1 - c)

        def cp(region, k, to):
            return _remote(region, region, send_sems, recv_sems, k, to)

        def reg(chip, p, part=None, core=c):
            return geo.piece(ref, chip, core, p, p + 1, pieces, part)

        sent = []
        for p in range(pieces):
            sent += [cp(reg(j, p), per * p, nx), cp(reg(j, p), per * p + 1, ny)]
        for d in sent:
            d.start()
        for p in range(pieces):
            cp(reg(jx, p), per * p, nx).wait_recv()
            new = [cp(reg(jx, p, 0), per * p + 2, ny), cp(reg(jx, p), per * p + 4, sib)]
            cp(reg(jy, p), per * p + 1, ny).wait_recv()
            new += [cp(reg(jy, p, 1), per * p + 3, nx), cp(reg(jy, p), per * p + 5, sib)]
            for d in new:
                d.start()
            sent += new
        for p in range(pieces):
            cp(reg(jo, p, 0), per * p + 2, ny).wait_recv()
            cp(reg(jo, p, 1), per * p + 3, nx).wait_recv()
            new = [cp(reg(jo, p, 0), per * p + 6, sib), cp(reg(jo, p, 1), per * p + 7, sib)]
            for d in new:
                d.start()
            sent += new
        for p in range(pieces):
            cp(reg(jx, p, None, 1 - c), per * p + 4, sib).wait_recv()
            cp(reg(jy, p, None, 1 - c), per * p + 5, sib).wait_recv()
            cp(reg(jo, p, 0, 1 - c), per * p + 6, sib).wait_recv()
            cp(reg(jo, p, 1, 1 - c), per * p + 7, sib).wait_recv()
        for d in sent:
            d.wait_send()

    return _comm_call(body, name, [full], [_same(full)], per * pieces, aliases={0: 0})[0]


def _same(a):
    return jax.ShapeDtypeStruct(a.shape, a.dtype)


def _gather_ici(full, geo, p0, p1, pieces):
    def copies(ins, outs):
        x, y, c, _ = _place()
        mine = geo.piece(outs[0], 2 * x + y, c, p0, p1, pieces)
        return [(mine, mine, geo.piece(outs[0], 2 * ox + oy, c, p0, p1, pieces), (ox, oy, c))
                for ox, oy in ((1 - x, y), (x, 1 - y))]

    return dict(ins=[full], outs=[_same(full)], aliases={0: 0}, n=2, copies=copies)


def _gather_relay(full, geo, p0, p1, pieces):
    def copies(ins, outs):
        x, y, c, _ = _place()
        jx, jy, jo = 2 * (1 - x) + y, 2 * x + (1 - y), 2 * (1 - x) + (1 - y)
        reg = lambda chip, part: geo.piece(outs[0], chip, c, p0, p1, pieces, part)
        return [(reg(jx, 0), reg(jx, 0), reg(jo, 0), (x, 1 - y, c)), (reg(jy, 1), reg(jy, 1), reg(jo, 1), (1 - x, y, c))]

    return dict(ins=[full], outs=[_same(full)], aliases={0: 0}, n=2, copies=copies)


def _gather_d2d(full, geo):
    def copies(ins, outs):
        x, y, c, chips = _place()
        return [(geo.full_half(outs[0], 2 * ox + oy, c), geo.full_half(outs[0], 2 * ox + oy, c),
                 geo.full_half(outs[0], 2 * ox + oy, 1 - c), (x, y, 1 - c)) for ox, oy in chips]

    return dict(ins=[full], outs=[_same(full)], aliases={0: 0}, n=3, copies=copies)


def _swap_d2d(grad, geo):
    def copies(ins, outs):
        x, y, c, _ = _place()
        return [(geo.full_half(ins[0], j, 1 - c), outs[0].at[j], outs[0].at[j], (x, y, 1 - c)) for j in range(N_CHIPS)]

    return dict(ins=[grad], outs=[jax.ShapeDtypeStruct((N_CHIPS, geo.hr, geo.hc), BF16)], aliases={}, n=N_CHIPS,
                copies=copies)


def _scatter_ici(pair, recv, geo, p0, p1, pieces):
    pr = geo.hr // pieces
    rows = pl.ds(p0 * pr, (p1 - p0) * pr)

    def copies(ins, outs):
        x, y, c, chips = _place()
        return [(ins[0].at[2 * ox + oy, rows, :], outs[0].at[k, rows, :], outs[0].at[k, rows, :], (ox, oy, c))
                for k, (ox, oy) in enumerate(chips)]

    out = jax.ShapeDtypeStruct((3, geo.hr, geo.hc), BF16)
    if recv is None:
        return dict(ins=[pair], outs=[out], aliases={}, n=3, copies=copies)
    return dict(ins=[pair, recv], outs=[out], aliases={1: 0}, n=3, copies=copies)


def _join_d2d(shard, geo, row0=0):
    def copies(ins, outs):
        x, y, c, _ = _place()
        mine = outs[0].at[pl.ds(pl.multiple_of(row0 + c * geo.hr, 8), geo.hr), :]
        other = outs[0].at[pl.ds(pl.multiple_of(row0 + (1 - c) * geo.hr, 8), geo.hr), :]
        return [(mine, mine, other, (x, y, 1 - c))]

    return dict(ins=[shard], outs=[_same(shard)], aliases={0: 0}, n=1, copies=copies)


def _add_pair(grad, got, geo, place, name):
    tr = _pick(geo.hr, ROW_TILES)
    nb = geo.hr // tr

    def body(place_ref, a_ref, b_ref, o_ref):
        o_ref[0] = (a_ref[...].astype(F32) + b_ref[0].astype(F32)).astype(BF16)

    if geo.kind == "col":
        own_map = lambda j, i, place_ref: (place_ref[1] * nb + i, j)
    else:
        own_map = lambda j, i, place_ref: ((2 * j + place_ref[1]) * nb + i, 0)
    spec = pl.BlockSpec((1, tr, geo.hc), lambda j, i, place_ref: (j, i, 0))
    return pl.pallas_call(
        body, name=name, out_shape=jax.ShapeDtypeStruct((N_CHIPS, geo.hr, geo.hc), BF16),
        grid_spec=pltpu.PrefetchScalarGridSpec(
            num_scalar_prefetch=1, grid=(N_CHIPS, nb),
            in_specs=[pl.BlockSpec((tr, geo.hc), own_map), spec], out_specs=spec),
        compiler_params=_params(2),
    )(place, grad, got)


def _add_four(pair, recv, geo, place, name, rows=None, row0=0, into=None):
    tr = _pick(geo.hr, ROW_TILES)
    nb = geo.hr // tr
    rows = 2 * geo.hr if rows is None else rows

    def body(place_ref, p_ref, r_ref, *rest):
        rest[-1][...] = ((p_ref[0].astype(F32) + r_ref[0].astype(F32)) + r_ref[1].astype(F32)) + r_ref[2].astype(F32)

    in_specs = [pl.BlockSpec((1, tr, geo.hc), lambda i, place_ref: (place_ref[0], i, 0)),
                pl.BlockSpec((3, tr, geo.hc), lambda i, place_ref: (0, i, 0))]
    args = [place, pair, recv]
    if into is not None:
        in_specs.append(pl.BlockSpec(memory_space=pl.ANY))
        args.append(into)
    return pl.pallas_call(
        body, name=name, out_shape=jax.ShapeDtypeStruct((rows, geo.hc), F32),
        grid_spec=pltpu.PrefetchScalarGridSpec(
            num_scalar_prefetch=1, grid=(nb,), in_specs=in_specs,
            out_specs=pl.BlockSpec((tr, geo.hc), lambda i, place_ref: (row0 // tr + place_ref[1] * nb + i, 0))),
        input_output_aliases={3: 0} if into is not None else {},
        compiler_params=_params(1),
    )(*args)


PIECES = (4, 1, 16, 8) + (1,) * IN_CHUNKS
SCHEDULE = {
    "proj_fwd": (("gather_ici", W_OUT, 0, 1), ("gather_ici", W_UP, 0, 6)),
    "hgrn_fwd": (("gather_relay", W_OUT, 0, 1), ("gather_relay", W_UP, 0, 6), ("gather_ici", W_UP, 6, 10)),
    "attn_fwd_d1": (("gather_ici", W_UP, 10, 12), ("gather_relay", W_UP, 6, 10)),
    "attn_fwd_d4": (("gather_ici", W_UP, 12, 16), ("gather_relay", W_UP, 10, 12), ("gather_d2d", W_OUT)),
    "attn_fwd_d16": (("gather_relay", W_UP, 12, 16), ("gather_ici", W_DOWN, 0, 2)),
    "mix_fwd": (("gather_d2d", W_UP), ("gather_ici", W_DOWN, 2, 4)),
    "up_fwd": (("gather_ici", W_DOWN, 4, 8), ("gather_relay", W_DOWN, 0, 4)),
    "conv_gate_fwd": (("gather_relay", W_DOWN, 4, 8),),
    "before_down_fwd": (("gather_d2d", W_DOWN),),
    "down_bwd_x": (("swap", W_DOWN),),
    "conv_gate_bwd": (("scatter", W_DOWN, 0, 4),),
    "up_bwd_w": (("scatter", W_DOWN, 4, 8),),
    "up_bwd_x": (("swap", W_UP),),
    "norm2_bwd": (("scatter", W_UP, 0, 1),),
    "mix_bwd_w": (("scatter", W_UP, 1, 2),),
    "mix_bwd_x": (("swap", W_OUT), ("scatter", W_UP, 2, 3)),
    "hgrn_bwd": (("scatter", W_UP, 3, 9), ("join", W_DOWN)),
    "attn_bwd_d1": (("scatter", W_UP, 9, 12),),
    "attn_bwd_d4": (("scatter", W_UP, 12, 13), ("scatter", W_OUT, 0, 1)),
    "attn_bwd_d16": (("scatter", W_UP, 13, 16),),
    "proj_bwd_w_0": (("join", W_UP), ("join", W_OUT)),
    "proj_bwd_w_1": (("swap", W_INQ),),
    "proj_bwd_w_2": (("swap", W_INQ + 1),),
    "proj_bwd_w_3": (("swap", W_INQ + 2), ("scatter", W_INQ, 0, 1)),
    "proj_bwd_x": (("swap", W_INQ + 3), ("scatter", W_INQ + 1, 0, 1), ("scatter", W_INQ + 2, 0, 1)),
    "norm1_bwd": (("scatter", W_INQ + 3, 0, 1), ("join", W_INQ), ("join", W_INQ + 1), ("join", W_INQ + 2)),
    "grad_in_join": (("join", W_INQ + 3),),
}


class _Net:
    def __init__(self, shards, place):
        self.place = place
        self.geos = [_Geo(k, s.shape) for k, s in zip(WEIGHT_KINDS, shards)]
        self.full = [_cast_into_full(s, g, place, f"cast_shard_{w}") for w, (s, g) in enumerate(zip(shards, self.geos))]
        self.full[W_IN] = _gather_weight(self.full[W_IN], self.geos[W_IN], PIECES[W_IN], "gather_w_in")
        rows, cols = shards[W_IN].shape
        self.geos += [_Geo("col", (rows // IN_CHUNKS, cols))] * IN_CHUNKS
        n = NW + IN_CHUNKS
        self.grad, self.pair, self.recv, self.shard = [None] * n, [None] * n, [None] * n, [None] * n
        self.in_rows, self.in_shard = rows, None
        self.when_joined, self.joined = {}, {}
        self.slots = []

    def _row0(self, w):
        return (w - W_INQ) * 2 * self.geos[w].hr

    def host(self, name):
        phase = _Phase()
        self.slots = []
        groups = {}
        for item in SCHEDULE.get(name, ()):
            kind, w = item[0], item[1]
            geo = self.geos[w]
            key = len(groups)
            if kind == "gather_ici":
                spec, key = _gather_ici(self.full[w], geo, item[2], item[3], PIECES[w]), ("full", w)
            elif kind == "gather_relay":
                spec, key = _gather_relay(self.full[w], geo, item[2], item[3], PIECES[w]), ("full", w)
            elif kind == "gather_d2d":
                spec, key = _gather_d2d(self.full[w], geo), ("full", w)
            elif kind == "swap":
                spec = _swap_d2d(self.grad[w], geo)
            elif kind == "scatter":
                spec, key = _scatter_ici(self.pair[w], self.recv[w], geo, item[2], item[3], PIECES[w]), ("recv", w)
            elif w >= W_INQ:
                spec, key = _join_d2d(self.in_shard, geo, self._row0(w)), "in_shard"
            else:
                spec = _join_d2d(self.shard[w], geo)
            groups.setdefault(key, []).append((item, spec))
        for group in groups.values():
            specs = [s for _, s in group]
            merged = dict(specs[0], n=sum(s["n"] for s in specs),
                          copies=lambda ins, outs, specs=specs: [t for s in specs for t in s["copies"](ins, outs)])
            self.slots.append(([item for item, _ in group], phase.add(**merged)))
        return phase

    def done(self, name, comm):
        for items, sl in sorted(self.slots, key=lambda s: s[0][0][0] == "scatter"):
            out = comm[sl][0]
            for item in items:
                kind, w = item[0], item[1]
                if kind in ("gather_ici", "gather_relay", "gather_d2d"):
                    self.full[w] = out
                elif kind == "swap":
                    self.pair[w] = _add_pair(self.grad[w], out, self.geos[w], self.place, f"grad_pair_sum_{w}")
                elif kind == "scatter":
                    self.recv[w] = out
                    if item[3] == PIECES[w] and w >= W_INQ:
                        self.in_shard = _add_four(self.pair[w], out, self.geos[w], self.place, f"grad_chip_sum_{w}",
                                                  rows=self.in_rows, row0=self._row0(w), into=self.in_shard)
                    elif item[3] == PIECES[w]:
                        self.shard[w] = _add_four(self.pair[w], out, self.geos[w], self.place, f"grad_chip_sum_{w}")
                elif w >= W_INQ:
                    self.in_shard = out
                else:
                    self.shard[w] = out
                    if w in self.when_joined:
                        self.joined[w] = self.when_joined[w](out)

    def alone(self, name):
        self.done(name, _comm_only(name, self.host(name)))


CONVW_SHARD = 3 * DFF // N_CHIPS
COND_PAD = 8 * 896
SMALL_SIZES = (("norm1_w", D), ("lb", HW), ("hg_norm_w", DH), ("q_norm_w", DH), ("k_norm_w", DH),
               ("norm2_w", D), ("conv_b", DFF), ("conv_w", 3 * DFF), ("loss", 128))
SMALL_TOTAL = sum(n for _, n in SMALL_SIZES)
STATS_PAD = 8 * 5120


def kernel(x, c, w_ada, b_ada, norm1_w, w_in, lb_logits, hg_norm_w, q_norm_w, k_norm_w, w_out, norm2_w, w_up, conv_w, conv_b, w_down, loss_target, m_w_ada, m_b_ada, m_norm1_w, m_w_in, m_lb_logits, m_hg_norm_w, m_q_norm_w, m_k_norm_w, m_w_out, m_norm2_w, m_w_up, m_conv_w, m_conv_b, m_w_down, v_w_ada, v_b_ada, v_norm1_w, v_w_in, v_lb_logits, v_hg_norm_w, v_q_norm_w, v_k_norm_w, v_w_out, v_norm2_w, v_w_up, v_conv_w, v_conv_b, v_w_down):
    chip = 2 * lax.axis_index("x") + lax.axis_index("y")
    dev = 2 * chip + lax.axis_index("c")

    cond = jnp.concatenate([c, conv_w[0].reshape(1, CONVW_SHARD), jnp.zeros((1, COND_PAD - D - CONVW_SHARD), F32)], axis=1)
    cond_all = _all_gather_rows(cond.reshape(8, COND_PAD // 8), "gather_cond").reshape(N_DEV, COND_PAD)
    c_all = cond_all[:, :D]
    conv_w_full = jnp.concatenate(
        [cond_all[2 * j, D:D + CONVW_SHARD].reshape(3, DFF // N_CHIPS) for j in range(N_CHIPS)], axis=1)

    b_shard = lax.dynamic_slice_in_dim(b_ada, chip * ADA_COLS, ADA_COLS, axis=1)
    mod_cols = _all_gather_rows(_ada_fwd(c_all, w_ada[0], b_shard), "gather_mod")
    mod_all = jnp.concatenate([mod_cols[16 * j:16 * j + 8] for j in range(N_CHIPS)], axis=1)
    mod = lax.dynamic_slice_in_dim(mod_all, dev, 1, axis=0)

    place = jnp.stack([chip, lax.axis_index("c")]).astype(jnp.int32)
    net = _Net([w_in[0], w_out[0], w_up[0], w_down[0]], place)
    net.when_joined = {
        W_OUT: lambda grad: _adamw_sc(w_out[0], grad, m_w_out[0], v_w_out[0], "adamw_sc_w_out"),
        W_DOWN: lambda grad: _adamw_sc(w_down[0], grad, m_w_down[0], v_w_down[0], "adamw_sc_w_down"),
        W_UP: lambda grad: _adamw_sc(w_up[0], grad, m_w_up[0], v_w_up[0], "adamw_sc_w_up"),
    }
    early = {W_OUT: "w_out", W_DOWN: "w_down", W_UP: "w_up"}
    loss_row, grad_x, dmod, small = _sequence_step(
        x[0], loss_target[0], mod, norm1_w, lb_logits, hg_norm_w, q_norm_w, k_norm_w, norm2_w, conv_w_full, conv_b, net)
    net.alone("grad_in_join")
    g_out, g_up, g_down = net.shard[W_OUT], net.shard[W_UP], net.shard[W_DOWN]
    g_in = net.in_shard

    small["conv_w"] = small["conv_w"].reshape(1, 3 * DFF)
    small["loss"] = loss_row
    stats = jnp.concatenate([dmod] + [small[k] for k, _ in SMALL_SIZES]
                            + [jnp.zeros((1, STATS_PAD - 6 * D - SMALL_TOTAL), F32)], axis=1)
    stats_all = _all_gather_rows(stats.reshape(8, STATS_PAD // 8), "gather_stats").reshape(N_DEV, STATS_PAD)
    dmod_all = stats_all[:, :6 * D]
    sums = _sum_rows(stats_all[:, 6 * D:6 * D + SMALL_TOTAL], "small_grad_sum")
    g_small, off = {}, 0
    for k, n in SMALL_SIZES:
        g_small[k] = sums[:, off:off + n]
        off += n
    loss = g_small["loss"][0, 0]
    g_b_ada = _sum_rows(dmod_all, "b_ada_grad_sum")
    g_w_ada = _ada_bwd(c_all, lax.dynamic_slice_in_dim(dmod_all, chip * ADA_COLS, ADA_COLS, axis=1))
    g_lb = _lb_grad(lb_logits, g_small["lb"])
    g_conv_w = lax.dynamic_slice_in_dim(g_small["conv_w"].reshape(3, DFF), chip * (DFF // N_CHIPS), DFF // N_CHIPS, axis=1)

    names = ["w_ada", "b_ada", "norm1_w", "w_in", "lb_logits", "hg_norm_w", "q_norm_w", "k_norm_w", "w_out",
             "norm2_w", "w_up", "conv_w", "conv_b", "w_down"]
    lead = {"w_ada", "w_in", "w_out", "w_up", "conv_w", "w_down"}
    w = dict(w_ada=w_ada, b_ada=b_ada, norm1_w=norm1_w, w_in=w_in, lb_logits=lb_logits, hg_norm_w=hg_norm_w,
             q_norm_w=q_norm_w, k_norm_w=k_norm_w, w_out=w_out, norm2_w=norm2_w, w_up=w_up, conv_w=conv_w,
             conv_b=conv_b, w_down=w_down)
    m = dict(w_ada=m_w_ada, b_ada=m_b_ada, norm1_w=m_norm1_w, w_in=m_w_in, lb_logits=m_lb_logits,
             hg_norm_w=m_hg_norm_w, q_norm_w=m_q_norm_w, k_norm_w=m_k_norm_w, w_out=m_w_out, norm2_w=m_norm2_w,
             w_up=m_w_up, conv_w=m_conv_w, conv_b=m_conv_b, w_down=m_w_down)
    v = dict(w_ada=v_w_ada, b_ada=v_b_ada, norm1_w=v_norm1_w, w_in=v_w_in, lb_logits=v_lb_logits,
             hg_norm_w=v_hg_norm_w, q_norm_w=v_q_norm_w, k_norm_w=v_k_norm_w, w_out=v_w_out, norm2_w=v_norm2_w,
             w_up=v_w_up, conv_w=v_conv_w, conv_b=v_conv_b, w_down=v_w_down)
    g = dict(w_ada=g_w_ada, b_ada=g_b_ada, norm1_w=g_small["norm1_w"], w_in=g_in, lb_logits=g_lb,
             hg_norm_w=g_small["hg_norm_w"], q_norm_w=g_small["q_norm_w"], k_norm_w=g_small["k_norm_w"], w_out=g_out,
             norm2_w=g_small["norm2_w"], w_up=g_up, conv_w=g_conv_w, conv_b=g_small["conv_b"], w_down=g_down)

    updated = {early[i]: res for i, res in net.joined.items()}
    grads, deltas, new_m, new_v = [], [], [], []
    for n in names:
        strip = (lambda t: t[0]) if n in lead else (lambda t: t)
        wrap = (lambda t: t[None]) if n in lead else (lambda t: t)
        if n in updated:
            d_n, m_n, v_n = updated[n]
        else:
            d_n, m_n, v_n = _adamw(strip(w[n]), g[n], strip(m[n]), strip(v[n]), f"adamw_{n}")
        grads.append(wrap(g[n]))
        deltas.append(wrap(d_n))
        new_m.append(wrap(m_n))
        new_v.append(wrap(v_n))
    return (loss, grad_x[None], *grads, *deltas, *new_m, *new_v)
```

```python
import functools
import math

import jax
import jax.numpy as jnp
from jax import lax
from jax.experimental import pallas as pl
from jax.experimental.pallas import tpu as pltpu
from jax.experimental.pallas import tpu_sc as plsc

F32 = jnp.float32
BF16 = jnp.bfloat16

S = 2048
D = 2048
H = 8
DH = 128
HW = H * DH
CH = 64
NCH = S // CH
DFF = 5632
INC = 7 * HW
BLK = 128
DILS = (1, 4, 16)
EPS = 1e-6
NEG = -1e30
N_CHIPS = 4
N_DEV = 8

ADAM_LR = 0.001
ADAM_B1 = 0.9
ADAM_B2 = 0.999
ADAM_EPS = 1e-08
ADAM_WD = 0.01
ADAM_STEP = 10
ADAM_BLOCK_BYTES = 1 << 20

V7X_VMEM_BYTES = 64 * 1024 * 1024
VMEM_LIMIT = (V7X_VMEM_BYTES * 3) // 4
MESH = pl.DeviceIdType.MESH
HBM_SPEC = pl.BlockSpec(memory_space=pltpu.HBM)
VMEM_SPEC = pl.BlockSpec(memory_space=pltpu.VMEM)

NN = (((1,), (0,)), ((), ()))
NT = (((1,), (1,)), ((), ()))
TN = (((0,), (0,)), ((), ()))


def _params(n_grid):
    return pltpu.CompilerParams(dimension_semantics=("arbitrary",) * n_grid, vmem_limit_bytes=VMEM_LIMIT)


def _dot(a, b, dims=NN):
    return lax.dot_general(a.astype(BF16), b.astype(BF16), dims, preferred_element_type=F32)


def _dot_f32(a, b):
    return lax.dot_general(a, b, NN, precision=lax.Precision.HIGHEST, preferred_element_type=F32)


def _sigmoid(x):
    return 1.0 / (1.0 + jnp.exp(-x))


def _pick(n, cands):
    for t in cands:
        if n % t == 0:
            return t
    raise ValueError(n)


def _matmul(a, b, mode, out_dtype, name, phase=None, m_blocks=None):
    if mode == "nn":
        (m, k), (_, n) = a.shape, b.shape
    elif mode == "nt":
        (m, k), (n, _) = a.shape, b.shape
    else:
        (k, m), (_, n) = a.shape, b.shape
    tm = _pick(m, (1024, 1408, 512))
    a_block = lambda i: i
    if m_blocks is not None:
        tm, blocks = m_blocks
        m = tm * len(blocks)
        step = blocks[1] - blocks[0] if len(blocks) > 1 else 0
        assert all(blk == blocks[0] + step * i for i, blk in enumerate(blocks))
        a_block = lambda i: blocks[0] + step * i
    tn = _pick(n, (1024, 1408, 512))
    tk = _pick(k, (2048, 2816, 1792, 1024, 512))
    nk = k // tk
    dims = {"nn": NN, "nt": NT, "tn": TN}[mode]

    def body(a_ref, b_ref, o_ref, *acc):
        part = lax.dot_general(a_ref[...], b_ref[...], dims, preferred_element_type=F32)
        if nk == 1:
            o_ref[...] = part.astype(o_ref.dtype)
            return
        acc_ref, kk = acc[0], pl.program_id(2)

        @pl.when(kk == 0)
        def _():
            acc_ref[...] = part

        @pl.when(jnp.logical_and(kk > 0, kk < nk - 1))
        def _():
            acc_ref[...] += part

        @pl.when(kk == nk - 1)
        def _():
            o_ref[...] = (acc_ref[...] + part).astype(o_ref.dtype)

    if mode == "tn":
        a_spec = pl.BlockSpec((tk, tm), lambda i, j, kk: (kk, a_block(i)))
    else:
        a_spec = pl.BlockSpec((tm, tk), lambda i, j, kk: (i, kk))
    if mode == "nt":
        b_spec = pl.BlockSpec((tn, tk), lambda i, j, kk: (j, kk))
    else:
        b_spec = pl.BlockSpec((tk, tn), lambda i, j, kk: (kk, j))
    return _pcall(
        body, [a, b], phase, name=name,
        out_shape=jax.ShapeDtypeStruct((m, n), out_dtype),
        grid=(m // tm, n // tn, nk),
        in_specs=[a_spec, b_spec],
        out_specs=pl.BlockSpec((tm, tn), lambda i, j, kk: (i, j)),
        scratch_shapes=[pltpu.VMEM((tm, tn), F32)] if nk > 1 else [],
        compiler_params=_params(3),
    )


TR = 256


def _row_spec(cols=D):
    return pl.BlockSpec((TR, cols), lambda i: (i, 0))


def _vec_spec(cols=D):
    return pl.BlockSpec((1, cols), lambda i: (0, 0))


def _norm_mod(x, nw, scale, shift, name):
    def body(x_ref, nw_ref, sc_ref, sh_ref, h_ref):
        xv = x_ref[...]
        r = lax.rsqrt(jnp.mean(xv * xv, axis=-1, keepdims=True) + EPS)
        h_ref[...] = ((xv * r) * nw_ref[...] * (1.0 + sc_ref[...]) + sh_ref[...]).astype(BF16)

    return pl.pallas_call(
        body, name=name, out_shape=jax.ShapeDtypeStruct((S, D), BF16), grid=(S // TR,),
        in_specs=[_row_spec(), _vec_spec(), _vec_spec(), _vec_spec()], out_specs=_row_spec(),
        compiler_params=_params(1),
    )(x, nw, scale, shift)


def _resid_norm_mod(x, mix, gate, nw, scale, shift, name):
    def body(x_ref, mix_ref, g_ref, nw_ref, sc_ref, sh_ref, x1_ref, h_ref):
        xv = x_ref[...] + g_ref[...] * mix_ref[...]
        x1_ref[...] = xv
        r = lax.rsqrt(jnp.mean(xv * xv, axis=-1, keepdims=True) + EPS)
        h_ref[...] = ((xv * r) * nw_ref[...] * (1.0 + sc_ref[...]) + sh_ref[...]).astype(BF16)

    return pl.pallas_call(
        body, name=name,
        out_shape=(jax.ShapeDtypeStruct((S, D), F32), jax.ShapeDtypeStruct((S, D), BF16)), grid=(S // TR,),
        in_specs=[_row_spec(), _row_spec(), _vec_spec(), _vec_spec(), _vec_spec(), _vec_spec()],
        out_specs=(_row_spec(), _row_spec()),
        compiler_params=_params(1),
    )(x, mix, gate, nw, scale, shift)


def _norm_mod_bwd(dh, xin, nw, scale, dres, name, mix=None, gate=None, phase=None):
    with_gate = mix is not None

    def body(*refs):
        if with_gate:
            dh_ref, x_ref, nw_ref, sc_ref, dres_ref, mix_ref, g_ref, dx_ref, dsh_ref, dsc_ref, dnw_ref, dg_ref, dmix_ref = refs
        else:
            dh_ref, x_ref, nw_ref, sc_ref, dres_ref, dx_ref, dsh_ref, dsc_ref, dnw_ref = refs
        i = pl.program_id(0)
        dhv = dh_ref[...]
        xv = x_ref[...]
        r = lax.rsqrt(jnp.mean(xv * xv, axis=-1, keepdims=True) + EPS)
        xn = xv * r
        nwv = nw_ref[...]
        one_sc = 1.0 + sc_ref[...]
        dxn = dhv * nwv * one_sc
        dx = dres_ref[...] + r * (dxn - xn * jnp.mean(dxn * xn, axis=-1, keepdims=True))
        dx_ref[...] = dx

        @pl.when(i == 0)
        def _():
            dsh_ref[...] = jnp.zeros_like(dsh_ref)
            dsc_ref[...] = jnp.zeros_like(dsc_ref)
            dnw_ref[...] = jnp.zeros_like(dnw_ref)
            if with_gate:
                dg_ref[...] = jnp.zeros_like(dg_ref)

        dsh_ref[...] += jnp.sum(dhv, axis=0, keepdims=True)
        dsc_ref[...] += jnp.sum(dhv * xn * nwv, axis=0, keepdims=True)
        dnw_ref[...] += jnp.sum(dhv * xn * one_sc, axis=0, keepdims=True)
        if with_gate:
            dg_ref[...] += jnp.sum(dx * mix_ref[...], axis=0, keepdims=True)
            dmix_ref[...] = (dx * g_ref[...]).astype(BF16)

    vec = jax.ShapeDtypeStruct((1, D), F32)
    ins = [dh, xin, nw, scale, dres]
    in_specs = [_row_spec(), _row_spec(), _vec_spec(), _vec_spec(), _row_spec()]
    outs = [jax.ShapeDtypeStruct((S, D), F32), vec, vec, vec]
    out_specs = [_row_spec(), _vec_spec(), _vec_spec(), _vec_spec()]
    if with_gate:
        ins += [mix, gate]
        in_specs += [_row_spec(), _vec_spec()]
        outs += [vec, jax.ShapeDtypeStruct((S, D), BF16)]
        out_specs += [_vec_spec(), _row_spec()]
    return _pcall(
        body, ins, phase, name=name, out_shape=tuple(outs), grid=(S // TR,), in_specs=in_specs,
        out_specs=tuple(out_specs), compiler_params=_params(1),
    )


def _tri(lower):
    row = lax.broadcasted_iota(jnp.int32, (CH, CH), 0)
    col = lax.broadcasted_iota(jnp.int32, (CH, CH), 1)
    return (row >= col) if lower else (col >= row)


def _hgrn_gates(hq, hf, lb):
    sig = _sigmoid(hf)
    f = lb + (1.0 - lb) * sig
    g = jnp.log(f)
    sq = _sigmoid(hq)
    return sig, f, g, 1.0 - f, hq * sq, sq


HG_HP = 2
HG_UNROLL = 4


def _proj_col_spec(group):
    return pl.BlockSpec((S, HG_HP * DH), lambda h: (0, group * (H // HG_HP) + h))


def _hgrn_fwd(proj, lb_logits, norm_w, phase=None):
    def body(hq_ref, hf_ref, hi_ref, hg_ref, lbl_ref, nw_ref, out_ref, oraw_ref, st_ref, s_ref):
        nw = nw_ref[...]
        lower = _tri(True)
        ltri = lower.astype(F32)
        s_ref[...] = jnp.zeros_like(s_ref)

        def chunk(n, carry):
            rows = pl.ds(pl.multiple_of(n * CH, CH), CH)
            for j in range(HG_HP):
                cols = slice(j * DH, (j + 1) * DH)
                lb = 1.0 / (1.0 + jnp.exp(lbl_ref[1:2, cols] - lbl_ref[0:1, cols]))
                hq, hf, v, hg = hq_ref[rows, cols], hf_ref[rows, cols], hi_ref[rows, cols], hg_ref[rows, cols]
                _, _, g, kk, q, _ = _hgrn_gates(hq, hf, lb)
                gc = _dot_f32(ltri, g)
                gl = jnp.sum(g, axis=0, keepdims=True)
                qe = q * jnp.exp(gc)
                ke = kk * jnp.exp(gl - gc)
                qh = q * jnp.exp(gc - 0.5 * gl)
                kh = kk * jnp.exp(0.5 * gl - gc)
                st = s_ref[j]
                st_ref[j, pl.ds(n, 1)] = st[None]
                a = jnp.where(lower, _dot(qh, kh, NT), 0.0)
                o = _dot(qe, st, NT) + _dot(a, v)
                s_ref[j] = st * jnp.exp(gl) + _dot(v, ke, TN)
                oraw_ref[rows, cols] = o
                rs = lax.rsqrt(jnp.mean(o * o, axis=-1, keepdims=True) + EPS)
                out_ref[rows, cols] = (o * rs * nw * (hg * _sigmoid(hg))).astype(BF16)
            return carry

        lax.fori_loop(0, NCH, chunk, 0, unroll=HG_UNROLL)

    head_spec = pl.BlockSpec((S, HG_HP * DH), lambda h: (0, h))
    return _pcall(
        body, [proj, proj, proj, proj, lb_logits, norm_w], phase, name="hgrn_fwd",
        out_shape=(jax.ShapeDtypeStruct((S, 2 * HW), BF16), jax.ShapeDtypeStruct((S, HW), F32),
                   jax.ShapeDtypeStruct((H, NCH, DH, DH), F32)),
        grid=(H // HG_HP,),
        in_specs=[_proj_col_spec(0), _proj_col_spec(1), _proj_col_spec(2), _proj_col_spec(3),
                  pl.BlockSpec((2, HG_HP * DH), lambda h: (0, h)), pl.BlockSpec((1, DH), lambda h: (0, 0))],
        out_specs=(head_spec, head_spec, pl.BlockSpec((HG_HP, NCH, DH, DH), lambda h: (h, 0, 0, 0))),
        scratch_shapes=[pltpu.VMEM((HG_HP, DH, DH), F32)],
        compiler_params=_params(1),
    )


def _hgrn_bwd(proj, lb_logits, norm_w, oraw, states, dout, phase=None):
    def body(hq_ref, hf_ref, hi_ref, hg_ref, lbl_ref, nw_ref, oraw_ref, st_ref, do_ref,
             dhq_ref, dhf_ref, dhi_ref, dhg_ref, dlb_ref, dnw_ref, ds_ref, acc_ref):
        h = pl.program_id(0)
        nw = nw_ref[...]
        lower = _tri(True)
        ltri = lower.astype(F32)
        utri = _tri(False).astype(F32)
        last = lax.broadcasted_iota(jnp.int32, (CH, DH), 0) == CH - 1
        ds_ref[...] = jnp.zeros_like(ds_ref)
        acc_ref[...] = jnp.zeros_like(acc_ref)

        @pl.when(h == 0)
        def _():
            dnw_ref[...] = jnp.zeros_like(dnw_ref)

        def chunk(i, carry):
            n = NCH - 1 - i
            rows = pl.ds(pl.multiple_of(n * CH, CH), CH)
            for j in range(HG_HP):
                cols = slice(j * DH, (j + 1) * DH)
                lb = 1.0 / (1.0 + jnp.exp(lbl_ref[1:2, cols] - lbl_ref[0:1, cols]))
                hq, hf, v, hg = hq_ref[rows, cols], hf_ref[rows, cols], hi_ref[rows, cols], hg_ref[rows, cols]
                sig, f, g, kk, q, sq = _hgrn_gates(hq, hf, lb)
                gc = _dot_f32(ltri, g)
                gl = jnp.sum(g, axis=0, keepdims=True)
                eg = jnp.exp(gc)
                ek = jnp.exp(gl - gc)
                gam = jnp.exp(gl)
                ph = jnp.exp(gc - 0.5 * gl)
                pk = jnp.exp(0.5 * gl - gc)
                qe, ke = q * eg, kk * ek
                qh, kh = q * ph, kk * pk
                st = st_ref[j, pl.ds(n, 1)][0]
                dst = ds_ref[j]
                a = jnp.where(lower, _dot(qh, kh, NT), 0.0)
                o = oraw_ref[rows, cols]
                rs = lax.rsqrt(jnp.mean(o * o, axis=-1, keepdims=True) + EPS)
                xh = o * rs
                sg = _sigmoid(hg)
                dgo = do_ref[rows, cols]
                d_on = dgo * (hg * sg)
                dhg_ref[rows, cols] = (dgo * xh * nw * (sg * (1.0 + hg * (1.0 - sg)))).astype(BF16)
                acc_ref[j, 1:2, :] += jnp.sum(d_on * xh, axis=0, keepdims=True)
                dy = d_on * nw
                do = rs * (dy - xh * jnp.mean(dy * xh, axis=-1, keepdims=True))
                dqe = _dot(do, st)
                da = jnp.where(lower, _dot(do, v, NT), 0.0)
                dv = _dot(a, do, TN) + _dot(ke, dst, NT)
                dke = _dot(v, dst)
                dgam = jnp.sum(dst * st, axis=0, keepdims=True)
                dqh = _dot(da, kh)
                dkh = _dot(da, qh, TN)
                dq = dqh * ph + dqe * eg
                dk = dkh * pk + dke * ek
                dgl = jnp.sum(dke * ke, axis=0, keepdims=True) + dgam * gam
                dgc = dqh * qh - dkh * kh + dqe * qe - dke * ke + jnp.where(last, dgl, 0.0)
                dg = _dot_f32(utri, dgc)
                df = dg / f - dk
                dhf_ref[rows, cols] = (df * (1.0 - lb) * sig * (1.0 - sig)).astype(BF16)
                acc_ref[j, 0:1, :] += jnp.sum(df * (1.0 - sig), axis=0, keepdims=True)
                dhq_ref[rows, cols] = (dq * (sq * (1.0 + hq * (1.0 - sq)))).astype(BF16)
                dhi_ref[rows, cols] = dv.astype(BF16)
                ds_ref[j] = dst * gam + _dot(do, qe, TN)
            return carry

        lax.fori_loop(0, NCH, chunk, 0, unroll=HG_UNROLL)
        for j in range(HG_HP):
            dlb_ref[:, j * DH:(j + 1) * DH] = acc_ref[j, 0:1, :]
            dnw_ref[...] += acc_ref[j, 1:2, :]

    head_spec = pl.BlockSpec((S, HG_HP * DH), lambda h: (0, h))
    head_out = jax.ShapeDtypeStruct((S, HW), BF16)
    return _pcall(
        body, [proj, proj, proj, proj, lb_logits, norm_w, oraw, states, dout], phase, name="hgrn_bwd",
        out_shape=(head_out, head_out, head_out, head_out,
                   jax.ShapeDtypeStruct((1, HW), F32), jax.ShapeDtypeStruct((1, DH), F32)),
        grid=(H // HG_HP,),
        in_specs=[_proj_col_spec(0), _proj_col_spec(1), _proj_col_spec(2), _proj_col_spec(3),
                  pl.BlockSpec((2, HG_HP * DH), lambda h: (0, h)), pl.BlockSpec((1, DH), lambda h: (0, 0)),
                  head_spec, pl.BlockSpec((HG_HP, NCH, DH, DH), lambda h: (h, 0, 0, 0)), head_spec],
        out_specs=(head_spec, head_spec, head_spec, head_spec,
                   pl.BlockSpec((1, HG_HP * DH), lambda h: (0, h)), pl.BlockSpec((1, DH), lambda h: (0, 0))),
        scratch_shapes=[pltpu.VMEM((HG_HP, DH, DH), F32), pltpu.VMEM((HG_HP, 8, DH), F32)],
        compiler_params=_params(1),
    )


ATT_SCALE = DH ** -0.5
HEADS_PER_STEP = {1: 8, 4: 1, 16: 1}


def _sub_rows(ref, r, dil, cols):
    if dil == 1:
        return ref[:, cols]
    return ref[pl.ds(r, BLK, stride=dil), cols]


def _set_sub_rows(ref, r, dil, cols, val):
    if dil == 1:
        ref[:, cols] = val
    else:
        ref[pl.ds(r, BLK, stride=dil), cols] = val


def _slopes(dil):
    hp = HEADS_PER_STEP[dil]
    s = jnp.asarray([dil * 2.0 ** (-(h + 1)) for h in range(H)], F32)
    return jnp.broadcast_to(s[:, None], (H, DH)).reshape(H // hp, hp, DH)


def _rms_rows(x, w):
    rs = lax.rsqrt(jnp.mean(x * x, axis=-1, keepdims=True) + EPS)
    return x * rs, rs


def _att_masks():
    qi = lax.broadcasted_iota(jnp.int32, (BLK, BLK), 0)
    kj = lax.broadcasted_iota(jnp.int32, (BLK, BLK), 1)
    steps_c = qi - kj
    steps_p = qi - kj + BLK
    return steps_c, steps_p, steps_c >= 0, steps_p <= BLK


def _qk_prep(proj, q_w, k_w):
    def body(q_ref, k_ref, qw_ref, kw_ref, qn_ref, kn_ref):
        for h in range(H):
            cols = slice(h * DH, (h + 1) * DH)
            qn_ref[:, cols] = _rms_rows(q_ref[:, cols], None)[0] * qw_ref[...]
            kn_ref[:, cols] = _rms_rows(k_ref[:, cols], None)[0] * kw_ref[...]

    out = jax.ShapeDtypeStruct((S, HW), F32)
    wspec = pl.BlockSpec((1, DH), lambda i: (0, 0))
    return pl.pallas_call(
        body, name="qk_prep", out_shape=(out, out), grid=(S // TR,),
        in_specs=[pl.BlockSpec((TR, HW), lambda i: (i, 4)), pl.BlockSpec((TR, HW), lambda i: (i, 5)), wspec, wspec],
        out_specs=(_row_spec(HW), _row_spec(HW)),
        compiler_params=_params(1),
    )(proj, proj, q_w, k_w)


def _qk_norm_bwd(proj, dqn, dkn, dv, q_w, k_w):
    def body(q_ref, k_ref, dqn_ref, dkn_ref, dv_ref, qw_ref, kw_ref, dq_ref, dk_ref, dvo_ref, dqw_ref, dkw_ref):
        i = pl.program_id(0)

        @pl.when(i == 0)
        def _():
            dqw_ref[...] = jnp.zeros_like(dqw_ref)
            dkw_ref[...] = jnp.zeros_like(dkw_ref)

        dvo_ref[...] = dv_ref[...].astype(BF16)
        for x_ref, d_ref, w_ref, o_ref, dw_ref in ((q_ref, dqn_ref, qw_ref, dq_ref, dqw_ref),
                                                   (k_ref, dkn_ref, kw_ref, dk_ref, dkw_ref)):
            for h in range(H):
                cols = slice(h * DH, (h + 1) * DH)
                xh, rs = _rms_rows(x_ref[:, cols], None)
                d = d_ref[:, cols]
                dw_ref[...] += jnp.sum(d * xh, axis=0, keepdims=True)
                dy = d * w_ref[...]
                o_ref[:, cols] = (rs * (dy - xh * jnp.mean(dy * xh, axis=-1, keepdims=True))).astype(BF16)

    big = jax.ShapeDtypeStruct((S, HW), BF16)
    small = jax.ShapeDtypeStruct((1, DH), F32)
    wspec = pl.BlockSpec((1, DH), lambda i: (0, 0))
    return pl.pallas_call(
        body, name="qk_norm_bwd", out_shape=(big, big, big, small, small), grid=(S // TR,),
        in_specs=[pl.BlockSpec((TR, HW), lambda i: (i, 4)), pl.BlockSpec((TR, HW), lambda i: (i, 5)),
                  _row_spec(HW), _row_spec(HW), _row_spec(HW), wspec, wspec],
        out_specs=(_row_spec(HW), _row_spec(HW), _row_spec(HW), wspec, wspec),
        compiler_params=_params(1),
    )(proj, proj, dqn, dkn, dv, q_w, k_w)


def _attn_delta(do, o):
    def body(do_ref, o_ref, d_ref):
        for h in range(H):
            cols = slice(h * DH, (h + 1) * DH)
            d_ref[:, cols] = jnp.broadcast_to(jnp.sum(do_ref[:, cols] * o_ref[:, cols], axis=-1, keepdims=True), (TR, DH))

    return pl.pallas_call(
        body, name="attn_delta", out_shape=jax.ShapeDtypeStruct((S, HW), F32), grid=(S // TR,),
        in_specs=[pl.BlockSpec((TR, HW), lambda i: (i, 1)), _row_spec(HW)], out_specs=_row_spec(HW),
        compiler_params=_params(1),
    )(do, o)


def _attn_fwd(proj, qn, kn, dil, phase=None):
    hp = HEADS_PER_STEP[dil]
    rows, ng, nb = BLK * dil, H // hp, S // (BLK * dil)

    def body(q_ref, kc_ref, kp_ref, vc_ref, vp_ref, sl_ref, o_ref, lse_ref):
        n = pl.program_id(0)
        steps_c, steps_p, ok_c, ok_p = _att_masks()
        ok_p = jnp.logical_and(ok_p, n > 0)
        fc, fp = steps_c.astype(F32), steps_p.astype(F32)
        for hh in range(hp):
            cols = slice(hh * DH, (hh + 1) * DH)
            sl = sl_ref[0, hh:hh + 1, :]
            for r in range(dil):
                sub = lambda ref: _sub_rows(ref, r, dil, cols)
                qv = sub(q_ref)
                sc = jnp.where(ok_c, _dot(qv, sub(kc_ref), NT) * ATT_SCALE - sl * fc, NEG)
                sp = jnp.where(ok_p, _dot(qv, sub(kp_ref), NT) * ATT_SCALE - sl * fp, NEG)
                m = jnp.maximum(jnp.max(sc, axis=-1, keepdims=True), jnp.max(sp, axis=-1, keepdims=True))
                pc, pp = jnp.exp(sc - m), jnp.exp(sp - m)
                den = jnp.sum(pc, axis=-1, keepdims=True) + jnp.sum(pp, axis=-1, keepdims=True)
                o = (_dot(pc, sub(vc_ref)) + _dot(pp, sub(vp_ref))) / den
                _set_sub_rows(o_ref, r, dil, cols, o)
                _set_sub_rows(lse_ref, r, dil, cols, jnp.broadcast_to(m + jnp.log(den), (BLK, DH)))

    cur = pl.BlockSpec((rows, hp * DH), lambda n, g: (n, g))
    prev = pl.BlockSpec((rows, hp * DH), lambda n, g: (jnp.maximum(n - 1, 0), g))
    vcur = pl.BlockSpec((rows, hp * DH), lambda n, g: (n, 6 * ng + g))
    vprev = pl.BlockSpec((rows, hp * DH), lambda n, g: (jnp.maximum(n - 1, 0), 6 * ng + g))
    out = jax.ShapeDtypeStruct((S, HW), F32)
    return _pcall(
        body, [qn, kn, kn, proj, proj, _slopes(dil)], phase,
        name=f"attn_fwd_d{dil}", out_shape=(out, out), grid=(nb, ng),
        in_specs=[cur, cur, prev, vcur, vprev, pl.BlockSpec((1, hp, DH), lambda n, g: (g, 0, 0))],
        out_specs=(cur, cur),
        compiler_params=_params(2),
    )


def _attn_merge(outs, lses, cat):
    def body(o1, o2, o3, l1, l2, l3, cat_ref, ob_ref, of_ref, lse_ref):
        a, b, c = l1[...], l2[...], l3[...]
        m = jnp.maximum(jnp.maximum(a, b), c)
        ea, eb, ec = jnp.exp(a - m), jnp.exp(b - m), jnp.exp(c - m)
        den = ea + eb + ec
        o = (ea * o1[...] + eb * o2[...] + ec * o3[...]) / den
        ob_ref[...] = o.astype(BF16)
        of_ref[...] = o
        lse_ref[...] = m + jnp.log(den)

    f = jax.ShapeDtypeStruct((S, HW), F32)
    return pl.pallas_call(
        body, name="attn_merge", out_shape=(jax.ShapeDtypeStruct((S, 2 * HW), BF16), f, f), grid=(S // TR,),
        in_specs=[_row_spec(HW)] * 6 + [pl.BlockSpec(memory_space=pl.ANY)],
        out_specs=(pl.BlockSpec((TR, HW), lambda i: (i, 1)), _row_spec(HW), _row_spec(HW)),
        input_output_aliases={6: 0},
        compiler_params=_params(1),
    )(*outs, *lses, cat)


def _attn_bwd(proj, qn, kn, lse, delta, do, dil, acc, phase=None):
    hp = HEADS_PER_STEP[dil]
    rows, ng, nb = BLK * dil, H // hp, S // (BLK * dil)
    has_acc = acc is not None

    def body(*refs):
        q_ref, qn_ref, kp_ref, kc_ref, vp_ref, vc_ref, l_ref, ln_ref, d_ref, dn_ref, do_ref, don_ref, sl_ref = refs[:13]
        refs = refs[13:]
        if has_acc:
            aq_ref, ak_ref, av_ref = refs[:3]
            refs = refs[3:]
        dq_ref, dk_ref, dv_ref = refs
        m = pl.program_id(0)
        steps_c, steps_p, ok_c, ok_p = _att_masks()
        ok_prev = jnp.logical_and(ok_p, m > 0)
        ok_next = jnp.logical_and(ok_p, m < nb - 1)
        fc, fp = steps_c.astype(F32), steps_p.astype(F32)
        for hh in range(hp):
            cols = slice(hh * DH, (hh + 1) * DH)
            sl = sl_ref[0, hh:hh + 1, :]
            for r in range(dil):
                sub = lambda ref: _sub_rows(ref, r, dil, cols)
                qv, qnv, kcv, kpv = sub(q_ref), sub(qn_ref), sub(kc_ref), sub(kp_ref)
                vc, vp = sub(vc_ref), sub(vp_ref)
                dov, donv = sub(do_ref), sub(don_ref)
                lse_m, lse_n = sub(l_ref)[:, 0:1], sub(ln_ref)[:, 0:1]
                delta_m, delta_n = sub(d_ref)[:, 0:1], sub(dn_ref)[:, 0:1]
                p_c = jnp.where(ok_c, jnp.exp(_dot(qv, kcv, NT) * ATT_SCALE - sl * fc - lse_m), 0.0)
                p_p = jnp.where(ok_prev, jnp.exp(_dot(qv, kpv, NT) * ATT_SCALE - sl * fp - lse_m), 0.0)
                p_n = jnp.where(ok_next, jnp.exp(_dot(qnv, kcv, NT) * ATT_SCALE - sl * fp - lse_n), 0.0)
                ds_c = p_c * (_dot(dov, vc, NT) - delta_m)
                ds_p = p_p * (_dot(dov, vp, NT) - delta_m)
                ds_n = p_n * (_dot(donv, vc, NT) - delta_n)
                dqn = (_dot(ds_c, kcv) + _dot(ds_p, kpv)) * ATT_SCALE
                dkn = (_dot(ds_c, qv, TN) + _dot(ds_n, qnv, TN)) * ATT_SCALE
                dv = _dot(p_c, dov, TN) + _dot(p_n, donv, TN)
                if has_acc:
                    dqn, dkn, dv = dqn + sub(aq_ref), dkn + sub(ak_ref), dv + sub(av_ref)
                _set_sub_rows(dq_ref, r, dil, cols, dqn)
                _set_sub_rows(dk_ref, r, dil, cols, dkn)
                _set_sub_rows(dv_ref, r, dil, cols, dv)

    def shifted(shift, m):
        if shift < 0:
            return jnp.maximum(m - 1, 0)
        if shift > 0:
            return jnp.minimum(m + 1, nb - 1)
        return m

    def spec(shift, group=0):
        return pl.BlockSpec((rows, hp * DH), lambda m, g: (shifted(shift, m), group * ng + g))

    ins = [qn, qn, kn, kn, proj, proj, lse, lse, delta, delta, do, do, _slopes(dil)]
    in_specs = [spec(0), spec(1), spec(-1), spec(0), spec(-1, 6), spec(0, 6), spec(0), spec(1), spec(0), spec(1),
                spec(0, 1), spec(1, 1), pl.BlockSpec((1, hp, DH), lambda m, g: (g, 0, 0))]
    if has_acc:
        ins += list(acc)
        in_specs += [spec(0)] * 3
    big = jax.ShapeDtypeStruct((S, HW), F32)
    return _pcall(
        body, ins, phase, name=f"attn_bwd_d{dil}", out_shape=(big, big, big), grid=(nb, ng),
        in_specs=in_specs, out_specs=(spec(0),) * 3,
        compiler_params=_params(2),
    )


TC = 512
NCT = DFF // TC


def _shift_rows(a, k):
    rows = lax.broadcasted_iota(jnp.int32, a.shape, 0)
    rolled = pltpu.roll(a, k % S, 0)
    if k > 0:
        return jnp.where(rows >= k, rolled, 0.0)
    return jnp.where(rows < S + k, rolled, 0.0)


def _conv_pre(a, w_ref, b_ref):
    return b_ref[...] + w_ref[0:1, :] * _shift_rows(a, 2) + w_ref[1:2, :] * _shift_rows(a, 1) + w_ref[2:3, :] * a


def _conv_gate_fwd(u, conv_w, conv_b, phase=None):
    def body(a_ref, g_ref, w_ref, b_ref, y_ref):
        pre = _conv_pre(a_ref[...].astype(F32), w_ref, b_ref)
        y_ref[...] = (pre * _sigmoid(pre) * g_ref[...].astype(F32)).astype(BF16)

    return _pcall(
        body, [u, u, conv_w, conv_b], phase,
        name="conv_gate_fwd", out_shape=jax.ShapeDtypeStruct((S, DFF), BF16), grid=(NCT,),
        in_specs=[pl.BlockSpec((S, TC), lambda i: (0, i)), pl.BlockSpec((S, TC), lambda i: (0, NCT + i)),
                  pl.BlockSpec((3, TC), lambda i: (0, i)), pl.BlockSpec((1, TC), lambda i: (0, i))],
        out_specs=pl.BlockSpec((S, TC), lambda i: (0, i)),
        compiler_params=_params(1),
    )


def _conv_gate_bwd(dy, u, conv_w, conv_b, phase=None):
    def body(dy_ref, a_ref, g_ref, w_ref, b_ref, du_ref, db_ref, dw_ref, da_buf, dg_buf, sems):
        i = pl.program_id(0)
        slot = i % 2

        def writes(step, s):
            col = pl.multiple_of(step * TC, TC)
            return (pltpu.make_async_copy(da_buf.at[s], du_ref.at[:, pl.ds(col, TC)], sems.at[0, s]),
                    pltpu.make_async_copy(dg_buf.at[s], du_ref.at[:, pl.ds(DFF + col, TC)], sems.at[1, s]))

        @pl.when(i >= 2)
        def _():
            for cp in writes(i - 2, slot):
                cp.wait()

        a = a_ref[...].astype(F32)
        dyv = dy_ref[...].astype(F32)
        pre = _conv_pre(a, w_ref, b_ref)
        sg = _sigmoid(pre)
        dg_buf[slot] = (dyv * pre * sg).astype(BF16)
        dpre = dyv * g_ref[...].astype(F32) * (sg * (1.0 + pre * (1.0 - sg)))
        da = w_ref[2:3, :] * dpre + w_ref[1:2, :] * _shift_rows(dpre, -1) + w_ref[0:1, :] * _shift_rows(dpre, -2)
        da_buf[slot] = da.astype(BF16)
        for cp in writes(i, slot):
            cp.start()
        db_ref[...] = jnp.sum(dpre, axis=0, keepdims=True)
        dw_ref[0:1, :] = jnp.sum(dpre * _shift_rows(a, 2), axis=0, keepdims=True)
        dw_ref[1:2, :] = jnp.sum(dpre * _shift_rows(a, 1), axis=0, keepdims=True)
        dw_ref[2:3, :] = jnp.sum(dpre * a, axis=0, keepdims=True)

        @pl.when(i == NCT - 1)
        def _():
            for cp in writes(i - 1, 1 - slot) + writes(i, slot):
                cp.wait()

    col = pl.BlockSpec((S, TC), lambda i: (0, i))
    return _pcall(
        body, [dy, u, u, conv_w, conv_b], phase, name="conv_gate_bwd",
        out_shape=(jax.ShapeDtypeStruct((S, 2 * DFF), BF16), jax.ShapeDtypeStruct((1, DFF), F32),
                   jax.ShapeDtypeStruct((3, DFF), F32)),
        grid=(NCT,),
        in_specs=[col, col, pl.BlockSpec((S, TC), lambda i: (0, NCT + i)),
                  pl.BlockSpec((3, TC), lambda i: (0, i)), pl.BlockSpec((1, TC), lambda i: (0, i))],
        out_specs=(pl.BlockSpec(memory_space=pl.ANY), pl.BlockSpec((1, TC), lambda i: (0, i)),
                   pl.BlockSpec((3, TC), lambda i: (0, i))),
        scratch_shapes=[pltpu.VMEM((2, S, TC), BF16), pltpu.VMEM((2, S, TC), BF16), pltpu.SemaphoreType.DMA((2, 2))],
        compiler_params=_params(1),
    )


def _out_loss(z, x1, target, gate):
    def body(z_ref, x_ref, t_ref, g_ref, do_ref, dz_ref, dg_ref, loss_ref):
        i = pl.program_id(0)
        zv = z_ref[...]
        gv = g_ref[...]
        err = x_ref[...] + gv * zv - t_ref[...]
        dout = err * (1.0 / D)
        do_ref[...] = dout
        dz_ref[...] = (dout * gv).astype(BF16)

        @pl.when(i == 0)
        def _():
            dg_ref[...] = jnp.zeros_like(dg_ref)
            loss_ref[...] = jnp.zeros_like(loss_ref)

        dg_ref[...] += jnp.sum(dout * zv, axis=0, keepdims=True)
        part = jnp.sum(jnp.sum(err * err, axis=0, keepdims=True), axis=-1, keepdims=True) * (0.5 / D)
        lane = lax.broadcasted_iota(jnp.int32, (1, 128), 1)
        loss_ref[...] += jnp.where(lane == 0, part, 0.0)

    return pl.pallas_call(
        body, name="out_loss",
        out_shape=(jax.ShapeDtypeStruct((S, D), F32), jax.ShapeDtypeStruct((S, D), BF16),
                   jax.ShapeDtypeStruct((1, D), F32), jax.ShapeDtypeStruct((1, 128), F32)),
        grid=(S // TR,),
        in_specs=[_row_spec(), _row_spec(), _row_spec(), _vec_spec()],
        out_specs=(_row_spec(), _row_spec(), _vec_spec(), _vec_spec(128)),
        compiler_params=_params(1),
    )(z, x1, target, gate)


ADA_COLS = 6 * D // N_CHIPS
TA = 512


def _ada_fwd(c_all, w_shard, b_shard):
    def body(c_ref, w_ref, b_ref, o_ref):
        cv = c_ref[...]
        o_ref[...] = _dot_f32(cv * _sigmoid(cv), w_ref[...]) + b_ref[...]

    return pl.pallas_call(
        body, name="ada_fwd", out_shape=jax.ShapeDtypeStruct((N_DEV, ADA_COLS), F32), grid=(ADA_COLS // TA,),
        in_specs=[pl.BlockSpec((N_DEV, D), lambda j: (0, 0)), pl.BlockSpec((D, TA), lambda j: (0, j)),
                  pl.BlockSpec((1, TA), lambda j: (0, j))],
        out_specs=pl.BlockSpec((N_DEV, TA), lambda j: (0, j)),
        compiler_params=_params(1),
    )(c_all, w_shard, b_shard)


def _ada_bwd(c_all, dmod_shard):
    def body(c_ref, d_ref, o_ref):
        cv = c_ref[...]
        o_ref[...] = lax.dot_general(cv * _sigmoid(cv), d_ref[...], TN, precision=lax.Precision.HIGHEST,
                                     preferred_element_type=F32)

    return pl.pallas_call(
        body, name="ada_bwd", out_shape=jax.ShapeDtypeStruct((D, ADA_COLS), F32), grid=(ADA_COLS // TA,),
        in_specs=[pl.BlockSpec((N_DEV, D), lambda j: (0, 0)), pl.BlockSpec((N_DEV, TA), lambda j: (0, j))],
        out_specs=pl.BlockSpec((D, TA), lambda j: (0, j)),
        compiler_params=_params(1),
    )(c_all, dmod_shard)


def _sum_rows(g, name):
    rows, n = g.shape

    def body(g_ref, o_ref):
        acc = g_ref[0:1, :]
        for i in range(1, rows):
            acc = acc + g_ref[i:i + 1, :]
        o_ref[...] = acc

    return pl.pallas_call(
        body, name=name, out_shape=jax.ShapeDtypeStruct((1, n), F32),
        in_specs=[VMEM_SPEC], out_specs=VMEM_SPEC,
    )(g)


def _lb_grad(lb_logits, dlb):
    def body(l_ref, d_ref, o_ref):
        p = 1.0 / (1.0 + jnp.exp(l_ref[1:2, :] - l_ref[0:1, :]))
        t = d_ref[...] * p * (1.0 - p)
        o_ref[0:1, :] = t
        o_ref[1:2, :] = -t

    return pl.pallas_call(
        body, name="lb_grad", out_shape=jax.ShapeDtypeStruct((2, HW), F32),
        in_specs=[VMEM_SPEC, VMEM_SPEC], out_specs=VMEM_SPEC,
    )(lb_logits, dlb)


def _adamw(w, g, m, v, name, copy_grad=False):
    rows, cols = w.shape
    tr = rows
    if rows * cols * 4 > ADAM_BLOCK_BYTES:
        tr = _pick(rows, [t for t in (256, 128, 64, 32, 16, 8) if t * cols * 4 <= ADAM_BLOCK_BYTES])

    def body(w_ref, g_ref, m_ref, v_ref, d_ref, mo_ref, vo_ref, *go_ref):
        gv = g_ref[...]
        if copy_grad:
            go_ref[0][...] = gv
        m2 = ADAM_B1 * m_ref[...] + (1.0 - ADAM_B1) * gv
        v2 = ADAM_B2 * v_ref[...] + (1.0 - ADAM_B2) * (gv * gv)
        m_hat = m2 / (1.0 - ADAM_B1 ** ADAM_STEP)
        v_hat = v2 / (1.0 - ADAM_B2 ** ADAM_STEP)
        d_ref[...] = -ADAM_LR * (m_hat / (jnp.sqrt(v_hat) + ADAM_EPS) + ADAM_WD * w_ref[...])
        mo_ref[...] = m2
        vo_ref[...] = v2

    spec = pl.BlockSpec((tr, cols), lambda i: (i, 0))
    out = jax.ShapeDtypeStruct((rows, cols), F32)
    n_out = 4 if copy_grad else 3
    return pl.pallas_call(
        body, name=name, out_shape=(out,) * n_out, grid=(rows // tr,),
        in_specs=[spec] * 4, out_specs=(spec,) * n_out,
        compiler_params=_params(1),
    )(w, g, m, v)


SC_TILES = 32
SC_LANES = 16
SC_COL_PARTS = 2
SC_CHUNK_ROWS = 8


def _adamw_sc(w, g, m, v, name):
    rows, cols = w.shape
    band, part = rows // (SC_TILES // SC_COL_PARTS), cols // SC_COL_PARTS
    assert band % SC_CHUNK_ROWS == 0 and part % SC_LANES == 0, (rows, cols)

    def body(w_hbm, g_hbm, m_hbm, v_hbm, d_hbm, mo_hbm, vo_hbm, go_hbm, wb, gb, mb, vb, db):
        tile = lax.axis_index("sc_subcore") * 2 + lax.axis_index("sc_core")
        row0 = (tile // SC_COL_PARTS) * band
        col0 = (tile % SC_COL_PARTS) * part

        @pl.loop(0, band, step=SC_CHUNK_ROWS)
        def _(r):
            at = lambda ref: ref.at[pl.ds(row0 + r, SC_CHUNK_ROWS), pl.ds(col0, part)]
            pltpu.sync_copy(at(w_hbm), wb)
            pltpu.sync_copy(at(g_hbm), gb)
            pltpu.sync_copy(gb, at(go_hbm))
            pltpu.sync_copy(at(m_hbm), mb)
            pltpu.sync_copy(at(v_hbm), vb)

            @pl.loop(0, SC_CHUNK_ROWS)
            def _(i):
                @pl.loop(0, part, step=SC_LANES)
                def _(j):
                    sl = (i, pl.ds(j, SC_LANES))
                    gv = gb[sl]
                    m2 = ADAM_B1 * mb[sl] + (1.0 - ADAM_B1) * gv
                    v2 = ADAM_B2 * vb[sl] + (1.0 - ADAM_B2) * (gv * gv)
                    m_hat = m2 / (1.0 - ADAM_B1 ** ADAM_STEP)
                    v_hat = v2 / (1.0 - ADAM_B2 ** ADAM_STEP)
                    db[sl] = -ADAM_LR * (m_hat / (jnp.sqrt(v_hat) + ADAM_EPS) + ADAM_WD * wb[sl])
                    mb[sl] = m2
                    vb[sl] = v2

            pltpu.sync_copy(db, at(d_hbm))
            pltpu.sync_copy(mb, at(mo_hbm))
            pltpu.sync_copy(vb, at(vo_hbm))

    out = jax.ShapeDtypeStruct((rows, cols), F32)
    buf = pltpu.VMEM((SC_CHUNK_ROWS, part), F32)
    return pl.kernel(
        body, name=name, out_type=(out, out, out, out),
        mesh=plsc.VectorSubcoreMesh(core_axis_name="sc_core", subcore_axis_name="sc_subcore"),
        scratch_types=[buf] * 5,
    )(w, g, m, v)


W_IN, W_OUT, W_UP, W_DOWN = range(4)
IN_CHUNKS = 4
W_INQ = 4


def _join_row_chunks(chunks):
    return jnp.concatenate(chunks, axis=0)


def _sequence_step(x, target, mod, norm1_w, lb_logits, hg_norm_w, q_norm_w, k_norm_w, norm2_w, conv_w, conv_b, net):
    shift1, scale1, gate1, shift2, scale2, gate2 = [mod[:, i * D:(i + 1) * D] for i in range(6)]

    def run(name, fn, *args, **kw):
        phase = net.host(name)
        if phase is None:
            return fn(*args, **kw)
        res, comm = fn(*args, phase=phase, **kw)
        net.done(name, comm)
        return res

    h = _norm_mod(x, norm1_w, scale1, shift1, "norm1_fwd")
    proj = run("proj_fwd", _matmul, h, net.full[W_IN], "nn", F32, "proj_fwd")
    cat, o_raw, states = run("hgrn_fwd", _hgrn_fwd, proj, lb_logits, hg_norm_w)
    qn, kn = _qk_prep(proj, q_norm_w, k_norm_w)
    att = [run(f"attn_fwd_d{dil}", _attn_fwd, proj, qn, kn, dil) for dil in DILS]
    cat, b_out_f32, lse = _attn_merge([t[0] for t in att], [t[1] for t in att], cat)
    mix = run("mix_fwd", _matmul, cat, net.full[W_OUT], "nn", F32, "mix_fwd")
    x1, h2 = _resid_norm_mod(x, mix, gate1, norm2_w, scale2, shift2, "norm2_fwd")
    u = run("up_fwd", _matmul, h2, net.full[W_UP], "nn", BF16, "up_fwd")
    yact = run("conv_gate_fwd", _conv_gate_fwd, u, conv_w, conv_b)
    net.alone("before_down_fwd")
    z = _matmul(yact, net.full[W_DOWN], "nn", F32, "down_fwd")
    dout, dz, dgate2, loss_row = _out_loss(z, x1, target, gate2)

    net.grad[W_DOWN] = _matmul(yact, dz, "tn", BF16, "down_bwd_w")
    dyact = run("down_bwd_x", _matmul, dz, net.full[W_DOWN], "nt", BF16, "down_bwd_x")
    du, dconv_b, dconv_w = run("conv_gate_bwd", _conv_gate_bwd, dyact, u, conv_w, conv_b)
    net.grad[W_UP] = run("up_bwd_w", _matmul, h2, du, "tn", BF16, "up_bwd_w")
    dh2 = run("up_bwd_x", _matmul, du, net.full[W_UP], "nt", F32, "up_bwd_x")
    dx1, dshift2, dscale2, dnorm2, dgate1, dmix = run(
        "norm2_bwd", _norm_mod_bwd, dh2, x1, norm2_w, scale2, dout, "norm2_bwd", mix=mix, gate=gate1)
    net.grad[W_OUT] = run("mix_bwd_w", _matmul, cat, dmix, "tn", BF16, "mix_bwd_w")
    dcat = run("mix_bwd_x", _matmul, dmix, net.full[W_OUT], "nt", F32, "mix_bwd_x")
    dhq, dhf, dhi, dhg, dlb, dhg_norm = run("hgrn_bwd", _hgrn_bwd, proj, lb_logits, hg_norm_w, o_raw, states, dcat)
    delta = _attn_delta(dcat, b_out_f32)
    acc = None
    for dil in DILS:
        acc = run(f"attn_bwd_d{dil}", _attn_bwd, proj, qn, kn, lse, delta, dcat, dil, acc)
    daq, dak, dav, dq_norm, dk_norm = _qk_norm_bwd(proj, *acc, q_norm_w, k_norm_w)
    dproj = jnp.concatenate([dhq, dhf, dhi, dhg, daq, dak, dav], axis=1)
    for q in range(IN_CHUNKS):
        net.grad[W_INQ + q] = run(f"proj_bwd_w_{q}", _matmul, h, dproj, "tn", BF16, f"proj_bwd_w_{q}",
                                  m_blocks=(D // IN_CHUNKS, [q]))
    dh = run("proj_bwd_x", _matmul, dproj, net.full[W_IN], "nt", F32, "proj_bwd_x")
    grad_x, dshift1, dscale1, dnorm1 = run("norm1_bwd", _norm_mod_bwd, dh, x, norm1_w, scale1, dx1, "norm1_bwd")

    dmod = jnp.concatenate([dshift1, dscale1, dgate1, dshift2, dscale2, dgate2], axis=1)
    small = dict(norm1_w=dnorm1, lb=dlb, hg_norm_w=dhg_norm, q_norm_w=dq_norm, k_norm_w=dk_norm,
                 norm2_w=dnorm2, conv_b=dconv_b, conv_w=dconv_w)
    return loss_row, grad_x, dmod, small


def _place():
    x, y, c = lax.axis_index("x"), lax.axis_index("y"), lax.axis_index("c")
    others = [(1 - x, y), (x, 1 - y), (1 - x, 1 - y)]
    return x, y, c, others


def _remote(src, dst, send_sems, recv_sems, k, to):
    return pltpu.make_async_remote_copy(src_ref=src, dst_ref=dst, send_sem=send_sems.at[k], recv_sem=recv_sems.at[k],
                                        device_id=to, device_id_type=MESH)


class _Phase:
    def __init__(self):
        self.ins, self.outs, self.aliases, self.groups, self.n = [], [], {}, [], 0

    def add(self, ins, outs, aliases, n, copies):
        i0, o0 = len(self.ins), len(self.outs)
        self.ins += list(ins)
        self.outs += list(outs)
        self.aliases.update({i0 + i: o0 + o for i, o in aliases.items()})
        self.groups.append((slice(i0, i0 + len(ins)), slice(o0, o0 + len(outs)), copies))
        self.n += n
        return slice(o0, o0 + len(outs))

    def build(self, cin, cout, send_sems, recv_sems, landing):
        res = []
        for si, so, copies in self.groups:
            for src, dst, land, peer in copies(cin[si], cout[so]):
                k = len(res)
                res.append(_remote(land, land, send_sems, recv_sems, k, peer) if landing
                           else _remote(src, dst, send_sems, recv_sems, k, peer))
        assert len(res) == self.n
        return res


def _pcall(body, args, phase=None, **kw):
    if phase is None or not phase.groups:
        res = pl.pallas_call(body, **kw)(*args)
        return res if phase is None else (res, ())
    grid = tuple(kw.pop("grid", ()))
    out_shape, out_specs = kw.pop("out_shape"), kw.pop("out_specs")
    single = not isinstance(out_shape, (tuple, list))
    out_shape = [out_shape] if single else list(out_shape)
    out_specs = [out_specs] if single else list(out_specs)
    scratch = list(kw.pop("scratch_shapes", ()))
    n_in, n_out, n_ci, n_co = len(args), len(out_shape), len(phase.ins), len(phase.outs)

    def hosted(*refs):
        ins, cin = refs[:n_in], refs[n_in:n_in + n_ci]
        outs = refs[n_in + n_ci:n_in + n_ci + n_out]
        cout = refs[n_in + n_ci + n_out:n_in + n_ci + n_out + n_co]
        scr, send_sems, recv_sems = refs[n_in + n_ci + n_out + n_co:-2], refs[-2], refs[-1]
        ids = [pl.program_id(a) for a in range(len(grid))]

        def start():
            for cp in phase.build(cin, cout, send_sems, recv_sems, False):
                cp.start()

        def finish():
            for cp in phase.build(cin, cout, send_sems, recv_sems, False):
                cp.wait_send()
            for cp in phase.build(cin, cout, send_sems, recv_sems, True):
                cp.wait_recv()

        if grid:
            first = functools.reduce(jnp.logical_and, [i == 0 for i in ids])
            last = functools.reduce(jnp.logical_and, [i == g - 1 for i, g in zip(ids, grid)])
            pl.when(first)(start)
            body(*ins, *outs, *scr)
            pl.when(last)(finish)
        else:
            start()
            body(*ins, *outs, *scr)
            finish()

    res = pl.pallas_call(
        hosted, out_shape=tuple(out_shape + phase.outs), grid=grid,
        in_specs=list(kw.pop("in_specs")) + [HBM_SPEC] * n_ci, out_specs=tuple(out_specs + [HBM_SPEC] * n_co),
        input_output_aliases={n_in + i: n_out + o for i, o in phase.aliases.items()},
        scratch_shapes=scratch + [pltpu.SemaphoreType.DMA((phase.n,)), pltpu.SemaphoreType.DMA((phase.n,))],
        **kw,
    )(*args, *phase.ins)
    outs = res[:n_out]
    return (outs[0] if single else tuple(outs)), tuple(res[n_out:])


def _comm_only(name, phase):
    return _pcall(lambda: None, [], phase, name=name, out_shape=(), in_specs=[], out_specs=())[1]


def _all_gather_rows(block, name, phase=None):
    m_per, n = block.shape

    def body(x_ref, out_ref, send_sems, recv_sems, local_sem):
        x, y, c, chips = _place()
        me, sibling = (x, y, c), (x, y, 1 - c)

        def rows(px, py, pc):
            return out_ref.at[pl.ds((4 * px + 2 * py + pc) * m_per, m_per), :]

        def copy(k, blk, to, src=None):
            return _remote(rows(*blk) if src is None else src, rows(*blk), send_sems, recv_sems, k, to)

        mine = pltpu.make_async_copy(x_ref, rows(*me), local_sem)
        mine.start()
        first = [copy(0, me, sibling, src=x_ref)]
        first += [copy(1 + j, me, (*chip, c), src=x_ref) for j, chip in enumerate(chips)]
        for cp in first:
            cp.start()
        passed = [copy(4 + j, (*chip, c), sibling) for j, chip in enumerate(chips)]
        for j, chip in enumerate(chips):
            copy(1 + j, (*chip, c), me).wait_recv()
            passed[j].start()
        copy(0, sibling, me).wait_recv()
        for j, chip in enumerate(chips):
            copy(4 + j, (*chip, 1 - c), me).wait_recv()
        for cp in first + passed:
            cp.wait_send()
        mine.wait()

    return _pcall(
        body, [block], phase, name=name, out_shape=jax.ShapeDtypeStruct((N_DEV * m_per, n), block.dtype),
        in_specs=[VMEM_SPEC], out_specs=VMEM_SPEC,
        scratch_shapes=[pltpu.SemaphoreType.DMA((7,)), pltpu.SemaphoreType.DMA((7,)), pltpu.SemaphoreType.DMA],
    )


class _Geo:
    def __init__(self, kind, shard_shape):
        self.kind = kind
        self.shard_shape = tuple(shard_shape)
        self.hr, self.hc = shard_shape[0] // 2, shard_shape[1]
        if kind == "col":
            self.full_shape = (shard_shape[0], N_CHIPS * shard_shape[1])
        else:
            self.full_shape = (N_CHIPS * shard_shape[0], shard_shape[1])

    def full_shard(self, ref, j):
        if self.kind == "col":
            return ref.at[:, pl.ds(pl.multiple_of(j * self.hc, 128), self.hc)]
        return ref.at[pl.ds(pl.multiple_of(j * 2 * self.hr, 16), 2 * self.hr), :]

    def full_half(self, ref, j, c):
        if self.kind == "col":
            return ref.at[pl.ds(pl.multiple_of(c * self.hr, 16), self.hr),
                          pl.ds(pl.multiple_of(j * self.hc, 128), self.hc)]
        return ref.at[pl.ds(pl.multiple_of((2 * j + c) * self.hr, 16), self.hr), :]

    def piece(self, ref, j, c, p0, p1, pieces, part=None):
        pr = self.hr // pieces
        off, n = p0 * pr, (p1 - p0) * pr
        if part is not None:
            top = (n // 32) * 16
            off, n = (off, top) if part == 0 else (off + top, n - top)
        if self.kind == "col":
            return ref.at[pl.ds(pl.multiple_of(c * self.hr + off, 16), n),
                          pl.ds(pl.multiple_of(j * self.hc, 128), self.hc)]
        return ref.at[pl.ds(pl.multiple_of((2 * j + c) * self.hr + off, 16), n), :]


WEIGHT_KINDS = ("col", "row", "col", "row")
NW = len(WEIGHT_KINDS)


def _comm_call(body, name, ins, outs, n_remote, aliases=None):
    return pl.pallas_call(
        body, name=name, out_shape=tuple(outs),
        in_specs=[HBM_SPEC] * len(ins), out_specs=tuple([HBM_SPEC] * len(outs)),
        input_output_aliases=aliases or {},
        scratch_shapes=[pltpu.SemaphoreType.DMA((n_remote,)), pltpu.SemaphoreType.DMA((n_remote,))],
    )(*ins)


ROW_TILES = (256, 176, 128, 64, 32, 16)


def _cast_into_full(shard, geo, place, name):
    rs, cs = shard.shape
    tr = _pick(rs, ROW_TILES)
    nb = rs // tr

    def body(place_ref, s_ref, o_ref):
        o_ref[...] = s_ref[...].astype(BF16)

    if geo.kind == "col":
        out_map = lambda i, place_ref: (i, place_ref[0])
    else:
        out_map = lambda i, place_ref: (place_ref[0] * nb + i, 0)
    return pl.pallas_call(
        body, name=name, out_shape=jax.ShapeDtypeStruct(geo.full_shape, BF16),
        grid_spec=pltpu.PrefetchScalarGridSpec(
            num_scalar_prefetch=1, grid=(nb,),
            in_specs=[pl.BlockSpec((tr, cs), lambda i, place_ref: (i, 0))],
            out_specs=pl.BlockSpec((tr, cs), out_map)),
        compiler_params=_params(1),
    )(place, shard)


def _gather_weight(full, geo, pieces, name):
    per = 8

    def body(in_ref, ref, send_sems, recv_sems):
        x, y, c, _ = _place()
        j, jx, jy, jo = 2 * x + y, 2 * (1 - x) + y, 2 * x + (1 - y), 2 * (1 - x) + (1 - y)
        nx, ny, sib = (1 - x, y, c), (x, 1 - y, c), (x, y, 1 - c)

        def cp(region, k, to):
            return _remote(region, region, send_sems, recv_sems, k, to)

        def reg(chip, p, part=None, core=c):
            return geo.piece(ref, chip, core, p, p + 1, pieces, part)

        sent = []
        for p in range(pieces):
            sent += [cp(reg(j, p), per * p, nx), cp(reg(j, p), per * p + 1, ny)]
        for d in sent:
            d.start()
        for p in range(pieces):
            cp(reg(jx, p), per * p, nx).wait_recv()
            new = [cp(reg(jx, p, 0), per * p + 2, ny), cp(reg(jx, p), per * p + 4, sib)]
            cp(reg(jy, p), per * p + 1, ny).wait_recv()
            new += [cp(reg(jy, p, 1), per * p + 3, nx), cp(reg(jy, p), per * p + 5, sib)]
            for d in new:
                d.start()
            sent += new
        for p in range(pieces):
            cp(reg(jo, p, 0), per * p + 2, ny).wait_recv()
            cp(reg(jo, p, 1), per * p + 3, nx).wait_recv()
            new = [cp(reg(jo, p, 0), per * p + 6, sib), cp(reg(jo, p, 1), per * p + 7, sib)]
            for d in new:
                d.start()
            sent += new
        for p in range(pieces):
            cp(reg(jx, p, None, 1 - c), per * p + 4, sib).wait_recv()
            cp(reg(jy, p, None, 1 - c), per * p + 5, sib).wait_recv()
            cp(reg(jo, p, 0, 1 - c), per * p + 6, sib).wait_recv()
            cp(reg(jo, p, 1, 1 - c), per * p + 7, sib).wait_recv()
        for d in sent:
            d.wait_send()

    return _comm_call(body, name, [full], [_same(full)], per * pieces, aliases={0: 0})[0]


def _same(a):
    return jax.ShapeDtypeStruct(a.shape, a.dtype)


def _gather_ici(full, geo, p0, p1, pieces):
    def copies(ins, outs):
        x, y, c, _ = _place()
        mine = geo.piece(outs[0], 2 * x + y, c, p0, p1, pieces)
        return [(mine, mine, geo.piece(outs[0], 2 * ox + oy, c, p0, p1, pieces), (ox, oy, c))
                for ox, oy in ((1 - x, y), (x, 1 - y))]

    return dict(ins=[full], outs=[_same(full)], aliases={0: 0}, n=2, copies=copies)


def _gather_relay(full, geo, p0, p1, pieces):
    def copies(ins, outs):
        x, y, c, _ = _place()
        jx, jy, jo = 2 * (1 - x) + y, 2 * x + (1 - y), 2 * (1 - x) + (1 - y)
        reg = lambda chip, part: geo.piece(outs[0], chip, c, p0, p1, pieces, part)
        return [(reg(jx, 0), reg(jx, 0), reg(jo, 0), (x, 1 - y, c)), (reg(jy, 1), reg(jy, 1), reg(jo, 1), (1 - x, y, c))]

    return dict(ins=[full], outs=[_same(full)], aliases={0: 0}, n=2, copies=copies)


def _gather_d2d(full, geo):
    def copies(ins, outs):
        x, y, c, chips = _place()
        return [(geo.full_half(outs[0], 2 * ox + oy, c), geo.full_half(outs[0], 2 * ox + oy, c),
                 geo.full_half(outs[0], 2 * ox + oy, 1 - c), (x, y, 1 - c)) for ox, oy in chips]

    return dict(ins=[full], outs=[_same(full)], aliases={0: 0}, n=3, copies=copies)


def _swap_d2d(grad, geo):
    def copies(ins, outs):
        x, y, c, _ = _place()
        return [(geo.full_half(ins[0], j, 1 - c), outs[0].at[j], outs[0].at[j], (x, y, 1 - c)) for j in range(N_CHIPS)]

    return dict(ins=[grad], outs=[jax.ShapeDtypeStruct((N_CHIPS, geo.hr, geo.hc), BF16)], aliases={}, n=N_CHIPS,
                copies=copies)


def _scatter_ici(pair, recv, geo, p0, p1, pieces):
    pr = geo.hr // pieces
    rows = pl.ds(p0 * pr, (p1 - p0) * pr)

    def copies(ins, outs):
        x, y, c, chips = _place()
        return [(ins[0].at[2 * ox + oy, rows, :], outs[0].at[k, rows, :], outs[0].at[k, rows, :], (ox, oy, c))
                for k, (ox, oy) in enumerate(chips)]

    out = jax.ShapeDtypeStruct((3, geo.hr, geo.hc), BF16)
    if recv is None:
        return dict(ins=[pair], outs=[out], aliases={}, n=3, copies=copies)
    return dict(ins=[pair, recv], outs=[out], aliases={1: 0}, n=3, copies=copies)


def _join_d2d(shard, geo, row0=0):
    def copies(ins, outs):
        x, y, c, _ = _place()
        mine = outs[0].at[pl.ds(pl.multiple_of(row0 + c * geo.hr, 8), geo.hr), :]
        other = outs[0].at[pl.ds(pl.multiple_of(row0 + (1 - c) * geo.hr, 8), geo.hr), :]
        return [(mine, mine, other, (x, y, 1 - c))]

    return dict(ins=[shard], outs=[_same(shard)], aliases={0: 0}, n=1, copies=copies)


def _add_pair(grad, got, geo, place, name):
    tr = _pick(geo.hr, ROW_TILES)
    nb = geo.hr // tr

    def body(place_ref, a_ref, b_ref, o_ref):
        o_ref[0] = (a_ref[...].astype(F32) + b_ref[0].astype(F32)).astype(BF16)

    if geo.kind == "col":
        own_map = lambda j, i, place_ref: (place_ref[1] * nb + i, j)
    else:
        own_map = lambda j, i, place_ref: ((2 * j + place_ref[1]) * nb + i, 0)
    spec = pl.BlockSpec((1, tr, geo.hc), lambda j, i, place_ref: (j, i, 0))
    return pl.pallas_call(
        body, name=name, out_shape=jax.ShapeDtypeStruct((N_CHIPS, geo.hr, geo.hc), BF16),
        grid_spec=pltpu.PrefetchScalarGridSpec(
            num_scalar_prefetch=1, grid=(N_CHIPS, nb),
            in_specs=[pl.BlockSpec((tr, geo.hc), own_map), spec], out_specs=spec),
        compiler_params=_params(2),
    )(place, grad, got)


def _add_four(pair, recv, geo, place, name, rows=None, row0=0, into=None):
    tr = _pick(geo.hr, ROW_TILES)
    nb = geo.hr // tr
    rows = 2 * geo.hr if rows is None else rows

    def body(place_ref, p_ref, r_ref, *rest):
        rest[-1][...] = ((p_ref[0].astype(F32) + r_ref[0].astype(F32)) + r_ref[1].astype(F32)) + r_ref[2].astype(F32)

    in_specs = [pl.BlockSpec((1, tr, geo.hc), lambda i, place_ref: (place_ref[0], i, 0)),
                pl.BlockSpec((3, tr, geo.hc), lambda i, place_ref: (0, i, 0))]
    args = [place, pair, recv]
    if into is not None:
        in_specs.append(pl.BlockSpec(memory_space=pl.ANY))
        args.append(into)
    return pl.pallas_call(
        body, name=name, out_shape=jax.ShapeDtypeStruct((rows, geo.hc), F32),
        grid_spec=pltpu.PrefetchScalarGridSpec(
            num_scalar_prefetch=1, grid=(nb,), in_specs=in_specs,
            out_specs=pl.BlockSpec((tr, geo.hc), lambda i, place_ref: (row0 // tr + place_ref[1] * nb + i, 0))),
        input_output_aliases={3: 0} if into is not None else {},
        compiler_params=_params(1),
    )(*args)


PIECES = (4, 1, 16, 8) + (1,) * IN_CHUNKS
SCHEDULE = {
    "proj_fwd": (("gather_ici", W_OUT, 0, 1), ("gather_ici", W_UP, 0, 6)),
    "hgrn_fwd": (("gather_relay", W_OUT, 0, 1), ("gather_relay", W_UP, 0, 6), ("gather_ici", W_UP, 6, 10)),
    "attn_fwd_d1": (("gather_ici", W_UP, 10, 12), ("gather_relay", W_UP, 6, 10)),
    "attn_fwd_d4": (("gather_ici", W_UP, 12, 16), ("gather_relay", W_UP, 10, 12), ("gather_d2d", W_OUT)),
    "attn_fwd_d16": (("gather_relay", W_UP, 12, 16), ("gather_ici", W_DOWN, 0, 2)),
    "mix_fwd": (("gather_d2d", W_UP), ("gather_ici", W_DOWN, 2, 4)),
    "up_fwd": (("gather_ici", W_DOWN, 4, 8), ("gather_relay", W_DOWN, 0, 4)),
    "conv_gate_fwd": (("gather_relay", W_DOWN, 4, 8),),
    "before_down_fwd": (("gather_d2d", W_DOWN),),
    "down_bwd_x": (("swap", W_DOWN),),
    "conv_gate_bwd": (("scatter", W_DOWN, 0, 4),),
    "up_bwd_w": (("scatter", W_DOWN, 4, 8),),
    "up_bwd_x": (("swap", W_UP),),
    "norm2_bwd": (("scatter", W_UP, 0, 1),),
    "mix_bwd_w": (("scatter", W_UP, 1, 2),),
    "mix_bwd_x": (("swap", W_OUT), ("scatter", W_UP, 2, 3)),
    "hgrn_bwd": (("scatter", W_UP, 3, 9), ("join", W_DOWN)),
    "attn_bwd_d1": (("scatter", W_UP, 9, 12),),
    "attn_bwd_d4": (("scatter", W_UP, 12, 13), ("scatter", W_OUT, 0, 1)),
    "attn_bwd_d16": (("scatter", W_UP, 13, 16),),
    "proj_bwd_w_0": (("join", W_UP), ("join", W_OUT)),
    "proj_bwd_w_1": (("swap", W_INQ),),
    "proj_bwd_w_2": (("swap", W_INQ + 1),),
    "proj_bwd_w_3": (("swap", W_INQ + 2), ("scatter", W_INQ, 0, 1)),
    "proj_bwd_x": (("swap", W_INQ + 3), ("scatter", W_INQ + 1, 0, 1), ("scatter", W_INQ + 2, 0, 1)),
    "gather_stats": (("scatter", W_INQ + 3, 0, 1), ("join", W_INQ), ("join", W_INQ + 1), ("join", W_INQ + 2)),
    "grad_in_join": (("join", W_INQ + 3),),
}


class _Net:
    def __init__(self, shards, place):
        self.place = place
        self.geos = [_Geo(k, s.shape) for k, s in zip(WEIGHT_KINDS, shards)]
        self.full = [_cast_into_full(s, g, place, f"cast_shard_{w}") for w, (s, g) in enumerate(zip(shards, self.geos))]
        self.full[W_IN] = _gather_weight(self.full[W_IN], self.geos[W_IN], PIECES[W_IN], "gather_w_in")
        rows, cols = shards[W_IN].shape
        self.geos += [_Geo("col", (rows // IN_CHUNKS, cols))] * IN_CHUNKS
        n = NW + IN_CHUNKS
        self.grad, self.pair, self.recv, self.shard = [None] * n, [None] * n, [None] * n, [None] * n
        self.in_rows, self.in_shard = rows, None
        self.when_joined, self.joined = {}, {}
        self.slots = []

    def _row0(self, w):
        return (w - W_INQ) * 2 * self.geos[w].hr

    def host(self, name):
        phase = _Phase()
        self.slots = []
        groups = {}
        for item in SCHEDULE.get(name, ()):
            kind, w = item[0], item[1]
            geo = self.geos[w]
            key = len(groups)
            if kind == "gather_ici":
                spec, key = _gather_ici(self.full[w], geo, item[2], item[3], PIECES[w]), ("full", w)
            elif kind == "gather_relay":
                spec, key = _gather_relay(self.full[w], geo, item[2], item[3], PIECES[w]), ("full", w)
            elif kind == "gather_d2d":
                spec, key = _gather_d2d(self.full[w], geo), ("full", w)
            elif kind == "swap":
                spec = _swap_d2d(self.grad[w], geo)
            elif kind == "scatter":
                spec, key = _scatter_ici(self.pair[w], self.recv[w], geo, item[2], item[3], PIECES[w]), ("recv", w)
            elif w >= W_INQ:
                spec, key = _join_d2d(self.in_shard, geo, self._row0(w)), "in_shard"
            else:
                spec = _join_d2d(self.shard[w], geo)
            groups.setdefault(key, []).append((item, spec))
        for group in groups.values():
            specs = [s for _, s in group]
            merged = dict(specs[0], n=sum(s["n"] for s in specs),
                          copies=lambda ins, outs, specs=specs: [t for s in specs for t in s["copies"](ins, outs)])
            self.slots.append(([item for item, _ in group], phase.add(**merged)))
        return phase

    def done(self, name, comm):
        for items, sl in sorted(self.slots, key=lambda s: s[0][0][0] == "scatter"):
            out = comm[sl][0]
            for item in items:
                kind, w = item[0], item[1]
                if kind in ("gather_ici", "gather_relay", "gather_d2d"):
                    self.full[w] = out
                elif kind == "swap":
                    self.pair[w] = _add_pair(self.grad[w], out, self.geos[w], self.place, f"grad_pair_sum_{w}")
                elif kind == "scatter":
                    self.recv[w] = out
                    if item[3] == PIECES[w] and w >= W_INQ:
                        self.in_shard = _add_four(self.pair[w], out, self.geos[w], self.place, f"grad_chip_sum_{w}",
                                                  rows=self.in_rows, row0=self._row0(w), into=self.in_shard)
                    elif item[3] == PIECES[w]:
                        self.shard[w] = _add_four(self.pair[w], out, self.geos[w], self.place, f"grad_chip_sum_{w}")
                elif w >= W_INQ:
                    self.in_shard = out
                else:
                    self.shard[w] = out
                    if w in self.when_joined:
                        self.joined[w] = self.when_joined[w](out)

    def alone(self, name):
        self.done(name, _comm_only(name, self.host(name)))


CONVW_SHARD = 3 * DFF // N_CHIPS
COND_PAD = 8 * 896
SMALL_SIZES = (("norm1_w", D), ("lb", HW), ("hg_norm_w", DH), ("q_norm_w", DH), ("k_norm_w", DH),
               ("norm2_w", D), ("conv_b", DFF), ("conv_w", 3 * DFF), ("loss", 128))
SMALL_TOTAL = sum(n for _, n in SMALL_SIZES)
STATS_PAD = 8 * 5120


def kernel(x, c, w_ada, b_ada, norm1_w, w_in, lb_logits, hg_norm_w, q_norm_w, k_norm_w, w_out, norm2_w, w_up, conv_w, conv_b, w_down, loss_target, m_w_ada, m_b_ada, m_norm1_w, m_w_in, m_lb_logits, m_hg_norm_w, m_q_norm_w, m_k_norm_w, m_w_out, m_norm2_w, m_w_up, m_conv_w, m_conv_b, m_w_down, v_w_ada, v_b_ada, v_norm1_w, v_w_in, v_lb_logits, v_hg_norm_w, v_q_norm_w, v_k_norm_w, v_w_out, v_norm2_w, v_w_up, v_conv_w, v_conv_b, v_w_down):
    chip = 2 * lax.axis_index("x") + lax.axis_index("y")
    dev = 2 * chip + lax.axis_index("c")

    cond = jnp.concatenate([c, conv_w[0].reshape(1, CONVW_SHARD), jnp.zeros((1, COND_PAD - D - CONVW_SHARD), F32)], axis=1)
    cond_all = _all_gather_rows(cond.reshape(8, COND_PAD // 8), "gather_cond").reshape(N_DEV, COND_PAD)
    c_all = cond_all[:, :D]
    conv_w_full = jnp.concatenate(
        [cond_all[2 * j, D:D + CONVW_SHARD].reshape(3, DFF // N_CHIPS) for j in range(N_CHIPS)], axis=1)

    b_shard = lax.dynamic_slice_in_dim(b_ada, chip * ADA_COLS, ADA_COLS, axis=1)
    mod_cols = _all_gather_rows(_ada_fwd(c_all, w_ada[0], b_shard), "gather_mod")
    mod_all = jnp.concatenate([mod_cols[16 * j:16 * j + 8] for j in range(N_CHIPS)], axis=1)
    mod = lax.dynamic_slice_in_dim(mod_all, dev, 1, axis=0)

    place = jnp.stack([chip, lax.axis_index("c")]).astype(jnp.int32)
    net = _Net([w_in[0], w_out[0], w_up[0], w_down[0]], place)
    net.when_joined = {
        W_OUT: lambda grad: _adamw_sc(w_out[0], grad, m_w_out[0], v_w_out[0], "adamw_sc_w_out"),
        W_DOWN: lambda grad: _adamw_sc(w_down[0], grad, m_w_down[0], v_w_down[0], "adamw_sc_w_down"),
        W_UP: lambda grad: _adamw_sc(w_up[0], grad, m_w_up[0], v_w_up[0], "adamw_sc_w_up"),
    }
    early = {W_OUT: "w_out", W_DOWN: "w_down", W_UP: "w_up"}
    loss_row, grad_x, dmod, small = _sequence_step(
        x[0], loss_target[0], mod, norm1_w, lb_logits, hg_norm_w, q_norm_w, k_norm_w, norm2_w, conv_w_full, conv_b, net)
    small["conv_w"] = small["conv_w"].reshape(1, 3 * DFF)
    small["loss"] = loss_row
    stats = jnp.concatenate([dmod] + [small[k] for k, _ in SMALL_SIZES]
                            + [jnp.zeros((1, STATS_PAD - 6 * D - SMALL_TOTAL), F32)], axis=1)
    stats_all, comm = _all_gather_rows(stats.reshape(8, STATS_PAD // 8), "gather_stats", net.host("gather_stats"))
    net.done("gather_stats", comm)
    stats_all = stats_all.reshape(N_DEV, STATS_PAD)
    net.alone("grad_in_join")
    g_out, g_up, g_down = net.shard[W_OUT], net.shard[W_UP], net.shard[W_DOWN]
    g_in = net.in_shard
    dmod_all = stats_all[:, :6 * D]
    sums = _sum_rows(stats_all[:, 6 * D:6 * D + SMALL_TOTAL], "small_grad_sum")
    g_small, off = {}, 0
    for k, n in SMALL_SIZES:
        g_small[k] = sums[:, off:off + n]
        off += n
    loss = g_small["loss"][0, 0]
    g_b_ada = _sum_rows(dmod_all, "b_ada_grad_sum")
    g_w_ada = _ada_bwd(c_all, lax.dynamic_slice_in_dim(dmod_all, chip * ADA_COLS, ADA_COLS, axis=1))
    g_lb = _lb_grad(lb_logits, g_small["lb"])
    g_conv_w = lax.dynamic_slice_in_dim(g_small["conv_w"].reshape(3, DFF), chip * (DFF // N_CHIPS), DFF // N_CHIPS, axis=1)

    names = ["w_ada", "b_ada", "norm1_w", "w_in", "lb_logits", "hg_norm_w", "q_norm_w", "k_norm_w", "w_out",
             "norm2_w", "w_up", "conv_w", "conv_b", "w_down"]
    lead = {"w_ada", "w_in", "w_out", "w_up", "conv_w", "w_down"}
    w = dict(w_ada=w_ada, b_ada=b_ada, norm1_w=norm1_w, w_in=w_in, lb_logits=lb_logits, hg_norm_w=hg_norm_w,
             q_norm_w=q_norm_w, k_norm_w=k_norm_w, w_out=w_out, norm2_w=norm2_w, w_up=w_up, conv_w=conv_w,
             conv_b=conv_b, w_down=w_down)
    m = dict(w_ada=m_w_ada, b_ada=m_b_ada, norm1_w=m_norm1_w, w_in=m_w_in, lb_logits=m_lb_logits,
             hg_norm_w=m_hg_norm_w, q_norm_w=m_q_norm_w, k_norm_w=m_k_norm_w, w_out=m_w_out, norm2_w=m_norm2_w,
             w_up=m_w_up, conv_w=m_conv_w, conv_b=m_conv_b, w_down=m_w_down)
    v = dict(w_ada=v_w_ada, b_ada=v_b_ada, norm1_w=v_norm1_w, w_in=v_w_in, lb_logits=v_lb_logits,
             hg_norm_w=v_hg_norm_w, q_norm_w=v_q_norm_w, k_norm_w=v_k_norm_w, w_out=v_w_out, norm2_w=v_norm2_w,
             w_up=v_w_up, conv_w=v_conv_w, conv_b=v_conv_b, w_down=v_w_down)
    g = dict(w_ada=g_w_ada, b_ada=g_b_ada, norm1_w=g_small["norm1_w"], w_in=g_in, lb_logits=g_lb,
             hg_norm_w=g_small["hg_norm_w"], q_norm_w=g_small["q_norm_w"], k_norm_w=g_small["k_norm_w"], w_out=g_out,
             norm2_w=g_small["norm2_w"], w_up=g_up, conv_w=g_conv_w, conv_b=g_small["conv_b"], w_down=g_down)

    updated = {early[i]: res for i, res in net.joined.items()}
    grads, deltas, new_m, new_v = [], [], [], []
    for n in names:
        strip = (lambda t: t[0]) if n in lead else (lambda t: t)
        wrap = (lambda t: t[None]) if n in lead else (lambda t: t)
        g_n = g[n]
        if n in updated:
            d_n, m_n, v_n, g_n = updated[n]
        elif n == "w_in":
            d_n, m_n, v_n, g_n = _adamw(strip(w[n]), g_n, strip(m[n]), strip(v[n]), f"adamw_{n}", copy_grad=True)
        else:
            d_n, m_n, v_n = _adamw(strip(w[n]), g_n, strip(m[n]), strip(v[n]), f"adamw_{n}")
        grads.append(wrap(g_n))
        deltas.append(wrap(d_n))
        new_m.append(wrap(m_n))
        new_v.append(wrap(v_n))
    return (loss, grad_x[None], *grads, *deltas, *new_m, *new_v)
```

```python
import functools
import math

import jax
import jax.numpy as jnp
from jax import lax
from jax.experimental import pallas as pl
from jax.experimental.pallas import tpu as pltpu
from jax.experimental.pallas import tpu_sc as plsc

F32 = jnp.float32
BF16 = jnp.bfloat16

S = 2048
D = 2048
H = 8
DH = 128
HW = H * DH
CH = 64
NCH = S // CH
DFF = 5632
INC = 7 * HW
BLK = 128
DILS = (1, 4, 16)
EPS = 1e-6
NEG = -1e30
N_CHIPS = 4
N_DEV = 8

ADAM_LR = 0.001
ADAM_B1 = 0.9
ADAM_B2 = 0.999
ADAM_EPS = 1e-08
ADAM_WD = 0.01
ADAM_STEP = 10
ADAM_BLOCK_BYTES = 1 << 21

V7X_VMEM_BYTES = 64 * 1024 * 1024
VMEM_LIMIT = (V7X_VMEM_BYTES * 3) // 4
MESH = pl.DeviceIdType.MESH
HBM_SPEC = pl.BlockSpec(memory_space=pltpu.HBM)
VMEM_SPEC = pl.BlockSpec(memory_space=pltpu.VMEM)

NN = (((1,), (0,)), ((), ()))
NT = (((1,), (1,)), ((), ()))
TN = (((0,), (0,)), ((), ()))


def _params(n_grid):
    return pltpu.CompilerParams(dimension_semantics=("arbitrary",) * n_grid, vmem_limit_bytes=VMEM_LIMIT)


def _dot(a, b, dims=NN):
    return lax.dot_general(a.astype(BF16), b.astype(BF16), dims, preferred_element_type=F32)


def _dot_f32(a, b):
    return lax.dot_general(a, b, NN, precision=lax.Precision.HIGHEST, preferred_element_type=F32)


def _sigmoid(x):
    return 1.0 / (1.0 + jnp.exp(-x))


def _pick(n, cands):
    for t in cands:
        if n % t == 0:
            return t
    raise ValueError(n)


def _matmul(a, b, mode, out_dtype, name, phase=None, m_blocks=None):
    if mode == "nn":
        (m, k), (_, n) = a.shape, b.shape
    elif mode == "nt":
        (m, k), (n, _) = a.shape, b.shape
    else:
        (k, m), (_, n) = a.shape, b.shape
    tm = _pick(m, (1024, 1408, 512))
    a_block = lambda i: i
    if m_blocks is not None:
        tm, blocks = m_blocks
        m = tm * len(blocks)
        step = blocks[1] - blocks[0] if len(blocks) > 1 else 0
        assert all(blk == blocks[0] + step * i for i, blk in enumerate(blocks))
        a_block = lambda i: blocks[0] + step * i
    tn = _pick(n, (1024, 1408, 512))
    tk = _pick(k, (2048, 2816, 1792, 1024, 512))
    nk = k // tk
    dims = {"nn": NN, "nt": NT, "tn": TN}[mode]

    def body(a_ref, b_ref, o_ref, *acc):
        part = lax.dot_general(a_ref[...], b_ref[...], dims, preferred_element_type=F32)
        if nk == 1:
            o_ref[...] = part.astype(o_ref.dtype)
            return
        acc_ref, kk = acc[0], pl.program_id(2)

        @pl.when(kk == 0)
        def _():
            acc_ref[...] = part

        @pl.when(jnp.logical_and(kk > 0, kk < nk - 1))
        def _():
            acc_ref[...] += part

        @pl.when(kk == nk - 1)
        def _():
            o_ref[...] = (acc_ref[...] + part).astype(o_ref.dtype)

    if mode == "tn":
        a_spec = pl.BlockSpec((tk, tm), lambda i, j, kk: (kk, a_block(i)))
    else:
        a_spec = pl.BlockSpec((tm, tk), lambda i, j, kk: (i, kk))
    if mode == "nt":
        b_spec = pl.BlockSpec((tn, tk), lambda i, j, kk: (j, kk))
    else:
        b_spec = pl.BlockSpec((tk, tn), lambda i, j, kk: (kk, j))
    return _pcall(
        body, [a, b], phase, name=name,
        out_shape=jax.ShapeDtypeStruct((m, n), out_dtype),
        grid=(m // tm, n // tn, nk),
        in_specs=[a_spec, b_spec],
        out_specs=pl.BlockSpec((tm, tn), lambda i, j, kk: (i, j)),
        scratch_shapes=[pltpu.VMEM((tm, tn), F32)] if nk > 1 else [],
        compiler_params=_params(3),
    )


TR = 256


def _row_spec(cols=D):
    return pl.BlockSpec((TR, cols), lambda i: (i, 0))


def _vec_spec(cols=D):
    return pl.BlockSpec((1, cols), lambda i: (0, 0))


def _norm_mod(x, nw, scale, shift, name):
    def body(x_ref, nw_ref, sc_ref, sh_ref, h_ref):
        xv = x_ref[...]
        r = lax.rsqrt(jnp.mean(xv * xv, axis=-1, keepdims=True) + EPS)
        h_ref[...] = ((xv * r) * nw_ref[...] * (1.0 + sc_ref[...]) + sh_ref[...]).astype(BF16)

    return pl.pallas_call(
        body, name=name, out_shape=jax.ShapeDtypeStruct((S, D), BF16), grid=(S // TR,),
        in_specs=[_row_spec(), _vec_spec(), _vec_spec(), _vec_spec()], out_specs=_row_spec(),
        compiler_params=_params(1),
    )(x, nw, scale, shift)


def _resid_norm_mod(x, mix, gate, nw, scale, shift, name):
    def body(x_ref, mix_ref, g_ref, nw_ref, sc_ref, sh_ref, x1_ref, h_ref):
        xv = x_ref[...] + g_ref[...] * mix_ref[...]
        x1_ref[...] = xv
        r = lax.rsqrt(jnp.mean(xv * xv, axis=-1, keepdims=True) + EPS)
        h_ref[...] = ((xv * r) * nw_ref[...] * (1.0 + sc_ref[...]) + sh_ref[...]).astype(BF16)

    return pl.pallas_call(
        body, name=name,
        out_shape=(jax.ShapeDtypeStruct((S, D), F32), jax.ShapeDtypeStruct((S, D), BF16)), grid=(S // TR,),
        in_specs=[_row_spec(), _row_spec(), _vec_spec(), _vec_spec(), _vec_spec(), _vec_spec()],
        out_specs=(_row_spec(), _row_spec()),
        compiler_params=_params(1),
    )(x, mix, gate, nw, scale, shift)


def _norm_mod_bwd(dh, xin, nw, scale, dres, name, mix=None, gate=None, phase=None):
    with_gate = mix is not None

    def body(*refs):
        if with_gate:
            dh_ref, x_ref, nw_ref, sc_ref, dres_ref, mix_ref, g_ref, dx_ref, dsh_ref, dsc_ref, dnw_ref, dg_ref, dmix_ref = refs
        else:
            dh_ref, x_ref, nw_ref, sc_ref, dres_ref, dx_ref, dsh_ref, dsc_ref, dnw_ref = refs
        i = pl.program_id(0)
        dhv = dh_ref[...]
        xv = x_ref[...]
        r = lax.rsqrt(jnp.mean(xv * xv, axis=-1, keepdims=True) + EPS)
        xn = xv * r
        nwv = nw_ref[...]
        one_sc = 1.0 + sc_ref[...]
        dxn = dhv * nwv * one_sc
        dx = dres_ref[...] + r * (dxn - xn * jnp.mean(dxn * xn, axis=-1, keepdims=True))
        dx_ref[...] = dx

        @pl.when(i == 0)
        def _():
            dsh_ref[...] = jnp.zeros_like(dsh_ref)
            dsc_ref[...] = jnp.zeros_like(dsc_ref)
            dnw_ref[...] = jnp.zeros_like(dnw_ref)
            if with_gate:
                dg_ref[...] = jnp.zeros_like(dg_ref)

        dsh_ref[...] += jnp.sum(dhv, axis=0, keepdims=True)
        dsc_ref[...] += jnp.sum(dhv * xn * nwv, axis=0, keepdims=True)
        dnw_ref[...] += jnp.sum(dhv * xn * one_sc, axis=0, keepdims=True)
        if with_gate:
            dg_ref[...] += jnp.sum(dx * mix_ref[...], axis=0, keepdims=True)
            dmix_ref[...] = (dx * g_ref[...]).astype(BF16)

    vec = jax.ShapeDtypeStruct((1, D), F32)
    ins = [dh, xin, nw, scale, dres]
    in_specs = [_row_spec(), _row_spec(), _vec_spec(), _vec_spec(), _row_spec()]
    outs = [jax.ShapeDtypeStruct((S, D), F32), vec, vec, vec]
    out_specs = [_row_spec(), _vec_spec(), _vec_spec(), _vec_spec()]
    if with_gate:
        ins += [mix, gate]
        in_specs += [_row_spec(), _vec_spec()]
        outs += [vec, jax.ShapeDtypeStruct((S, D), BF16)]
        out_specs += [_vec_spec(), _row_spec()]
    return _pcall(
        body, ins, phase, name=name, out_shape=tuple(outs), grid=(S // TR,), in_specs=in_specs,
        out_specs=tuple(out_specs), compiler_params=_params(1),
    )


def _tri(lower):
    row = lax.broadcasted_iota(jnp.int32, (CH, CH), 0)
    col = lax.broadcasted_iota(jnp.int32, (CH, CH), 1)
    return (row >= col) if lower else (col >= row)


def _hgrn_gates(hq, hf, lb):
    sig = _sigmoid(hf)
    f = lb + (1.0 - lb) * sig
    g = jnp.log(f)
    sq = _sigmoid(hq)
    return sig, f, g, 1.0 - f, hq * sq, sq


HG_HP = 2
HG_UNROLL = 4


def _proj_col_spec(group):
    return pl.BlockSpec((S, HG_HP * DH), lambda h: (0, group * (H // HG_HP) + h))


def _hgrn_fwd(proj, lb_logits, norm_w, phase=None):
    def body(hq_ref, hf_ref, hi_ref, hg_ref, lbl_ref, nw_ref, out_ref, oraw_ref, st_ref, s_ref):
        nw = nw_ref[...]
        lower = _tri(True)
        ltri = lower.astype(F32)
        s_ref[...] = jnp.zeros_like(s_ref)

        def chunk(n, carry):
            rows = pl.ds(pl.multiple_of(n * CH, CH), CH)
            for j in range(HG_HP):
                cols = slice(j * DH, (j + 1) * DH)
                lb = 1.0 / (1.0 + jnp.exp(lbl_ref[1:2, cols] - lbl_ref[0:1, cols]))
                hq, hf, v, hg = hq_ref[rows, cols], hf_ref[rows, cols], hi_ref[rows, cols], hg_ref[rows, cols]
                _, _, g, kk, q, _ = _hgrn_gates(hq, hf, lb)
                gc = _dot_f32(ltri, g)
                gl = jnp.sum(g, axis=0, keepdims=True)
                qe = q * jnp.exp(gc)
                ke = kk * jnp.exp(gl - gc)
                qh = q * jnp.exp(gc - 0.5 * gl)
                kh = kk * jnp.exp(0.5 * gl - gc)
                st = s_ref[j]
                st_ref[j, pl.ds(n, 1)] = st[None]
                a = jnp.where(lower, _dot(qh, kh, NT), 0.0)
                o = _dot(qe, st, NT) + _dot(a, v)
                s_ref[j] = st * jnp.exp(gl) + _dot(v, ke, TN)
                oraw_ref[rows, cols] = o
                rs = lax.rsqrt(jnp.mean(o * o, axis=-1, keepdims=True) + EPS)
                out_ref[rows, cols] = (o * rs * nw * (hg * _sigmoid(hg))).astype(BF16)
            return carry

        lax.fori_loop(0, NCH, chunk, 0, unroll=HG_UNROLL)

    head_spec = pl.BlockSpec((S, HG_HP * DH), lambda h: (0, h))
    return _pcall(
        body, [proj, proj, proj, proj, lb_logits, norm_w], phase, name="hgrn_fwd",
        out_shape=(jax.ShapeDtypeStruct((S, 2 * HW), BF16), jax.ShapeDtypeStruct((S, HW), F32),
                   jax.ShapeDtypeStruct((H, NCH, DH, DH), F32)),
        grid=(H // HG_HP,),
        in_specs=[_proj_col_spec(0), _proj_col_spec(1), _proj_col_spec(2), _proj_col_spec(3),
                  pl.BlockSpec((2, HG_HP * DH), lambda h: (0, h)), pl.BlockSpec((1, DH), lambda h: (0, 0))],
        out_specs=(head_spec, head_spec, pl.BlockSpec((HG_HP, NCH, DH, DH), lambda h: (h, 0, 0, 0))),
        scratch_shapes=[pltpu.VMEM((HG_HP, DH, DH), F32)],
        compiler_params=_params(1),
    )


def _hgrn_bwd(proj, lb_logits, norm_w, oraw, states, dout, phase=None):
    def body(hq_ref, hf_ref, hi_ref, hg_ref, lbl_ref, nw_ref, oraw_ref, st_ref, do_ref,
             dhq_ref, dhf_ref, dhi_ref, dhg_ref, dlb_ref, dnw_ref, ds_ref, acc_ref):
        h = pl.program_id(0)
        nw = nw_ref[...]
        lower = _tri(True)
        ltri = lower.astype(F32)
        utri = _tri(False).astype(F32)
        last = lax.broadcasted_iota(jnp.int32, (CH, DH), 0) == CH - 1
        ds_ref[...] = jnp.zeros_like(ds_ref)
        acc_ref[...] = jnp.zeros_like(acc_ref)

        @pl.when(h == 0)
        def _():
            dnw_ref[...] = jnp.zeros_like(dnw_ref)

        def chunk(i, carry):
            n = NCH - 1 - i
            rows = pl.ds(pl.multiple_of(n * CH, CH), CH)
            for j in range(HG_HP):
                cols = slice(j * DH, (j + 1) * DH)
                lb = 1.0 / (1.0 + jnp.exp(lbl_ref[1:2, cols] - lbl_ref[0:1, cols]))
                hq, hf, v, hg = hq_ref[rows, cols], hf_ref[rows, cols], hi_ref[rows, cols], hg_ref[rows, cols]
                sig, f, g, kk, q, sq = _hgrn_gates(hq, hf, lb)
                gc = _dot_f32(ltri, g)
                gl = jnp.sum(g, axis=0, keepdims=True)
                eg = jnp.exp(gc)
                ek = jnp.exp(gl - gc)
                gam = jnp.exp(gl)
                ph = jnp.exp(gc - 0.5 * gl)
                pk = jnp.exp(0.5 * gl - gc)
                qe, ke = q * eg, kk * ek
                qh, kh = q * ph, kk * pk
                st = st_ref[j, pl.ds(n, 1)][0]
                dst = ds_ref[j]
                a = jnp.where(lower, _dot(qh, kh, NT), 0.0)
                o = oraw_ref[rows, cols]
                rs = lax.rsqrt(jnp.mean(o * o, axis=-1, keepdims=True) + EPS)
                xh = o * rs
                sg = _sigmoid(hg)
                dgo = do_ref[rows, cols]
                d_on = dgo * (hg * sg)
                dhg_ref[rows, cols] = (dgo * xh * nw * (sg * (1.0 + hg * (1.0 - sg)))).astype(BF16)
                acc_ref[j, 1:2, :] += jnp.sum(d_on * xh, axis=0, keepdims=True)
                dy = d_on * nw
                do = rs * (dy - xh * jnp.mean(dy * xh, axis=-1, keepdims=True))
                dqe = _dot(do, st)
                da = jnp.where(lower, _dot(do, v, NT), 0.0)
                dv = _dot(a, do, TN) + _dot(ke, dst, NT)
                dke = _dot(v, dst)
                dgam = jnp.sum(dst * st, axis=0, keepdims=True)
                dqh = _dot(da, kh)
                dkh = _dot(da, qh, TN)
                dq = dqh * ph + dqe * eg
                dk = dkh * pk + dke * ek
                dgl = jnp.sum(dke * ke, axis=0, keepdims=True) + dgam * gam
                dgc = dqh * qh - dkh * kh + dqe * qe - dke * ke + jnp.where(last, dgl, 0.0)
                dg = _dot_f32(utri, dgc)
                df = dg / f - dk
                dhf_ref[rows, cols] = (df * (1.0 - lb) * sig * (1.0 - sig)).astype(BF16)
                acc_ref[j, 0:1, :] += jnp.sum(df * (1.0 - sig), axis=0, keepdims=True)
                dhq_ref[rows, cols] = (dq * (sq * (1.0 + hq * (1.0 - sq)))).astype(BF16)
                dhi_ref[rows, cols] = dv.astype(BF16)
                ds_ref[j] = dst * gam + _dot(do, qe, TN)
            return carry

        lax.fori_loop(0, NCH, chunk, 0, unroll=HG_UNROLL)
        for j in range(HG_HP):
            dlb_ref[:, j * DH:(j + 1) * DH] = acc_ref[j, 0:1, :]
            dnw_ref[...] += acc_ref[j, 1:2, :]

    head_spec = pl.BlockSpec((S, HG_HP * DH), lambda h: (0, h))
    head_out = jax.ShapeDtypeStruct((S, HW), BF16)
    return _pcall(
        body, [proj, proj, proj, proj, lb_logits, norm_w, oraw, states, dout], phase, name="hgrn_bwd",
        out_shape=(head_out, head_out, head_out, head_out,
                   jax.ShapeDtypeStruct((1, HW), F32), jax.ShapeDtypeStruct((1, DH), F32)),
        grid=(H // HG_HP,),
        in_specs=[_proj_col_spec(0), _proj_col_spec(1), _proj_col_spec(2), _proj_col_spec(3),
                  pl.BlockSpec((2, HG_HP * DH), lambda h: (0, h)), pl.BlockSpec((1, DH), lambda h: (0, 0)),
                  head_spec, pl.BlockSpec((HG_HP, NCH, DH, DH), lambda h: (h, 0, 0, 0)), head_spec],
        out_specs=(head_spec, head_spec, head_spec, head_spec,
                   pl.BlockSpec((1, HG_HP * DH), lambda h: (0, h)), pl.BlockSpec((1, DH), lambda h: (0, 0))),
        scratch_shapes=[pltpu.VMEM((HG_HP, DH, DH), F32), pltpu.VMEM((HG_HP, 8, DH), F32)],
        compiler_params=_params(1),
    )


ATT_SCALE = DH ** -0.5
HEADS_PER_STEP = {1: 8, 4: 1, 16: 1}


def _sub_rows(ref, r, dil, cols):
    if dil == 1:
        return ref[:, cols]
    return ref[pl.ds(r, BLK, stride=dil), cols]


def _set_sub_rows(ref, r, dil, cols, val):
    if dil == 1:
        ref[:, cols] = val
    else:
        ref[pl.ds(r, BLK, stride=dil), cols] = val


def _slopes(dil):
    hp = HEADS_PER_STEP[dil]
    s = jnp.asarray([dil * 2.0 ** (-(h + 1)) for h in range(H)], F32)
    return jnp.broadcast_to(s[:, None], (H, DH)).reshape(H // hp, hp, DH)


def _rms_rows(x, w):
    rs = lax.rsqrt(jnp.mean(x * x, axis=-1, keepdims=True) + EPS)
    return x * rs, rs


def _att_masks():
    qi = lax.broadcasted_iota(jnp.int32, (BLK, BLK), 0)
    kj = lax.broadcasted_iota(jnp.int32, (BLK, BLK), 1)
    steps_c = qi - kj
    steps_p = qi - kj + BLK
    return steps_c, steps_p, steps_c >= 0, steps_p <= BLK


def _qk_prep(proj, q_w, k_w):
    def body(q_ref, k_ref, qw_ref, kw_ref, qn_ref, kn_ref):
        for h in range(H):
            cols = slice(h * DH, (h + 1) * DH)
            qn_ref[:, cols] = _rms_rows(q_ref[:, cols], None)[0] * qw_ref[...]
            kn_ref[:, cols] = _rms_rows(k_ref[:, cols], None)[0] * kw_ref[...]

    out = jax.ShapeDtypeStruct((S, HW), F32)
    wspec = pl.BlockSpec((1, DH), lambda i: (0, 0))
    return pl.pallas_call(
        body, name="qk_prep", out_shape=(out, out), grid=(S // TR,),
        in_specs=[pl.BlockSpec((TR, HW), lambda i: (i, 4)), pl.BlockSpec((TR, HW), lambda i: (i, 5)), wspec, wspec],
        out_specs=(_row_spec(HW), _row_spec(HW)),
        compiler_params=_params(1),
    )(proj, proj, q_w, k_w)


def _qk_norm_bwd(proj, dqn, dkn, dv, q_w, k_w):
    def body(q_ref, k_ref, dqn_ref, dkn_ref, dv_ref, qw_ref, kw_ref, dq_ref, dk_ref, dvo_ref, dqw_ref, dkw_ref):
        i = pl.program_id(0)

        @pl.when(i == 0)
        def _():
            dqw_ref[...] = jnp.zeros_like(dqw_ref)
            dkw_ref[...] = jnp.zeros_like(dkw_ref)

        dvo_ref[...] = dv_ref[...].astype(BF16)
        for x_ref, d_ref, w_ref, o_ref, dw_ref in ((q_ref, dqn_ref, qw_ref, dq_ref, dqw_ref),
                                                   (k_ref, dkn_ref, kw_ref, dk_ref, dkw_ref)):
            for h in range(H):
                cols = slice(h * DH, (h + 1) * DH)
                xh, rs = _rms_rows(x_ref[:, cols], None)
                d = d_ref[:, cols]
                dw_ref[...] += jnp.sum(d * xh, axis=0, keepdims=True)
                dy = d * w_ref[...]
                o_ref[:, cols] = (rs * (dy - xh * jnp.mean(dy * xh, axis=-1, keepdims=True))).astype(BF16)

    big = jax.ShapeDtypeStruct((S, HW), BF16)
    small = jax.ShapeDtypeStruct((1, DH), F32)
    wspec = pl.BlockSpec((1, DH), lambda i: (0, 0))
    return pl.pallas_call(
        body, name="qk_norm_bwd", out_shape=(big, big, big, small, small), grid=(S // TR,),
        in_specs=[pl.BlockSpec((TR, HW), lambda i: (i, 4)), pl.BlockSpec((TR, HW), lambda i: (i, 5)),
                  _row_spec(HW), _row_spec(HW), _row_spec(HW), wspec, wspec],
        out_specs=(_row_spec(HW), _row_spec(HW), _row_spec(HW), wspec, wspec),
        compiler_params=_params(1),
    )(proj, proj, dqn, dkn, dv, q_w, k_w)


def _attn_delta(do, o):
    def body(do_ref, o_ref, d_ref):
        for h in range(H):
            cols = slice(h * DH, (h + 1) * DH)
            d_ref[:, cols] = jnp.broadcast_to(jnp.sum(do_ref[:, cols] * o_ref[:, cols], axis=-1, keepdims=True), (TR, DH))

    return pl.pallas_call(
        body, name="attn_delta", out_shape=jax.ShapeDtypeStruct((S, HW), F32), grid=(S // TR,),
        in_specs=[pl.BlockSpec((TR, HW), lambda i: (i, 1)), _row_spec(HW)], out_specs=_row_spec(HW),
        compiler_params=_params(1),
    )(do, o)


def _attn_fwd(proj, qn, kn, dil, phase=None):
    hp = HEADS_PER_STEP[dil]
    rows, ng, nb = BLK * dil, H // hp, S // (BLK * dil)

    def body(q_ref, kc_ref, kp_ref, vc_ref, vp_ref, sl_ref, o_ref, lse_ref):
        n = pl.program_id(0)
        steps_c, steps_p, ok_c, ok_p = _att_masks()
        ok_p = jnp.logical_and(ok_p, n > 0)
        fc, fp = steps_c.astype(F32), steps_p.astype(F32)
        for hh in range(hp):
            cols = slice(hh * DH, (hh + 1) * DH)
            sl = sl_ref[0, hh:hh + 1, :]
            for r in range(dil):
                sub = lambda ref: _sub_rows(ref, r, dil, cols)
                qv = sub(q_ref)
                sc = jnp.where(ok_c, _dot(qv, sub(kc_ref), NT) * ATT_SCALE - sl * fc, NEG)
                sp = jnp.where(ok_p, _dot(qv, sub(kp_ref), NT) * ATT_SCALE - sl * fp, NEG)
                m = jnp.maximum(jnp.max(sc, axis=-1, keepdims=True), jnp.max(sp, axis=-1, keepdims=True))
                pc, pp = jnp.exp(sc - m), jnp.exp(sp - m)
                den = jnp.sum(pc, axis=-1, keepdims=True) + jnp.sum(pp, axis=-1, keepdims=True)
                o = (_dot(pc, sub(vc_ref)) + _dot(pp, sub(vp_ref))) / den
                _set_sub_rows(o_ref, r, dil, cols, o)
                _set_sub_rows(lse_ref, r, dil, cols, jnp.broadcast_to(m + jnp.log(den), (BLK, DH)))

    cur = pl.BlockSpec((rows, hp * DH), lambda n, g: (n, g))
    prev = pl.BlockSpec((rows, hp * DH), lambda n, g: (jnp.maximum(n - 1, 0), g))
    vcur = pl.BlockSpec((rows, hp * DH), lambda n, g: (n, 6 * ng + g))
    vprev = pl.BlockSpec((rows, hp * DH), lambda n, g: (jnp.maximum(n - 1, 0), 6 * ng + g))
    out = jax.ShapeDtypeStruct((S, HW), F32)
    return _pcall(
        body, [qn, kn, kn, proj, proj, _slopes(dil)], phase,
        name=f"attn_fwd_d{dil}", out_shape=(out, out), grid=(nb, ng),
        in_specs=[cur, cur, prev, vcur, vprev, pl.BlockSpec((1, hp, DH), lambda n, g: (g, 0, 0))],
        out_specs=(cur, cur),
        compiler_params=_params(2),
    )


def _attn_merge(outs, lses, cat):
    def body(o1, o2, o3, l1, l2, l3, cat_ref, ob_ref, of_ref, lse_ref):
        a, b, c = l1[...], l2[...], l3[...]
        m = jnp.maximum(jnp.maximum(a, b), c)
        ea, eb, ec = jnp.exp(a - m), jnp.exp(b - m), jnp.exp(c - m)
        den = ea + eb + ec
        o = (ea * o1[...] + eb * o2[...] + ec * o3[...]) / den
        ob_ref[...] = o.astype(BF16)
        of_ref[...] = o
        lse_ref[...] = m + jnp.log(den)

    f = jax.ShapeDtypeStruct((S, HW), F32)
    return pl.pallas_call(
        body, name="attn_merge", out_shape=(jax.ShapeDtypeStruct((S, 2 * HW), BF16), f, f), grid=(S // TR,),
        in_specs=[_row_spec(HW)] * 6 + [pl.BlockSpec(memory_space=pl.ANY)],
        out_specs=(pl.BlockSpec((TR, HW), lambda i: (i, 1)), _row_spec(HW), _row_spec(HW)),
        input_output_aliases={6: 0},
        compiler_params=_params(1),
    )(*outs, *lses, cat)


def _attn_bwd(proj, qn, kn, lse, delta, do, dil, acc, phase=None):
    hp = HEADS_PER_STEP[dil]
    rows, ng, nb = BLK * dil, H // hp, S // (BLK * dil)
    has_acc = acc is not None

    def body(*refs):
        q_ref, qn_ref, kp_ref, kc_ref, vp_ref, vc_ref, l_ref, ln_ref, d_ref, dn_ref, do_ref, don_ref, sl_ref = refs[:13]
        refs = refs[13:]
        if has_acc:
            aq_ref, ak_ref, av_ref = refs[:3]
            refs = refs[3:]
        dq_ref, dk_ref, dv_ref = refs
        m = pl.program_id(0)
        steps_c, steps_p, ok_c, ok_p = _att_masks()
        ok_prev = jnp.logical_and(ok_p, m > 0)
        ok_next = jnp.logical_and(ok_p, m < nb - 1)
        fc, fp = steps_c.astype(F32), steps_p.astype(F32)
        for hh in range(hp):
            cols = slice(hh * DH, (hh + 1) * DH)
            sl = sl_ref[0, hh:hh + 1, :]
            for r in range(dil):
                sub = lambda ref: _sub_rows(ref, r, dil, cols)
                qv, qnv, kcv, kpv = sub(q_ref), sub(qn_ref), sub(kc_ref), sub(kp_ref)
                vc, vp = sub(vc_ref), sub(vp_ref)
                dov, donv = sub(do_ref), sub(don_ref)
                lse_m, lse_n = sub(l_ref)[:, 0:1], sub(ln_ref)[:, 0:1]
                delta_m, delta_n = sub(d_ref)[:, 0:1], sub(dn_ref)[:, 0:1]
                p_c = jnp.where(ok_c, jnp.exp(_dot(qv, kcv, NT) * ATT_SCALE - sl * fc - lse_m), 0.0)
                p_p = jnp.where(ok_prev, jnp.exp(_dot(qv, kpv, NT) * ATT_SCALE - sl * fp - lse_m), 0.0)
                p_n = jnp.where(ok_next, jnp.exp(_dot(qnv, kcv, NT) * ATT_SCALE - sl * fp - lse_n), 0.0)
                ds_c = p_c * (_dot(dov, vc, NT) - delta_m)
                ds_p = p_p * (_dot(dov, vp, NT) - delta_m)
                ds_n = p_n * (_dot(donv, vc, NT) - delta_n)
                dqn = (_dot(ds_c, kcv) + _dot(ds_p, kpv)) * ATT_SCALE
                dkn = (_dot(ds_c, qv, TN) + _dot(ds_n, qnv, TN)) * ATT_SCALE
                dv = _dot(p_c, dov, TN) + _dot(p_n, donv, TN)
                if has_acc:
                    dqn, dkn, dv = dqn + sub(aq_ref), dkn + sub(ak_ref), dv + sub(av_ref)
                _set_sub_rows(dq_ref, r, dil, cols, dqn)
                _set_sub_rows(dk_ref, r, dil, cols, dkn)
                _set_sub_rows(dv_ref, r, dil, cols, dv)

    def shifted(shift, m):
        if shift < 0:
            return jnp.maximum(m - 1, 0)
        if shift > 0:
            return jnp.minimum(m + 1, nb - 1)
        return m

    def spec(shift, group=0):
        return pl.BlockSpec((rows, hp * DH), lambda m, g: (shifted(shift, m), group * ng + g))

    ins = [qn, qn, kn, kn, proj, proj, lse, lse, delta, delta, do, do, _slopes(dil)]
    in_specs = [spec(0), spec(1), spec(-1), spec(0), spec(-1, 6), spec(0, 6), spec(0), spec(1), spec(0), spec(1),
                spec(0, 1), spec(1, 1), pl.BlockSpec((1, hp, DH), lambda m, g: (g, 0, 0))]
    if has_acc:
        ins += list(acc)
        in_specs += [spec(0)] * 3
    big = jax.ShapeDtypeStruct((S, HW), F32)
    return _pcall(
        body, ins, phase, name=f"attn_bwd_d{dil}", out_shape=(big, big, big), grid=(nb, ng),
        in_specs=in_specs, out_specs=(spec(0),) * 3,
        compiler_params=_params(2),
    )


TC = 512
NCT = DFF // TC


def _shift_rows(a, k):
    rows = lax.broadcasted_iota(jnp.int32, a.shape, 0)
    rolled = pltpu.roll(a, k % S, 0)
    if k > 0:
        return jnp.where(rows >= k, rolled, 0.0)
    return jnp.where(rows < S + k, rolled, 0.0)


def _conv_pre(a, w_ref, b_ref):
    return b_ref[...] + w_ref[0:1, :] * _shift_rows(a, 2) + w_ref[1:2, :] * _shift_rows(a, 1) + w_ref[2:3, :] * a


def _conv_gate_fwd(u, conv_w, conv_b, phase=None):
    def body(a_ref, g_ref, w_ref, b_ref, y_ref):
        pre = _conv_pre(a_ref[...].astype(F32), w_ref, b_ref)
        y_ref[...] = (pre * _sigmoid(pre) * g_ref[...].astype(F32)).astype(BF16)

    return _pcall(
        body, [u, u, conv_w, conv_b], phase,
        name="conv_gate_fwd", out_shape=jax.ShapeDtypeStruct((S, DFF), BF16), grid=(NCT,),
        in_specs=[pl.BlockSpec((S, TC), lambda i: (0, i)), pl.BlockSpec((S, TC), lambda i: (0, NCT + i)),
                  pl.BlockSpec((3, TC), lambda i: (0, i)), pl.BlockSpec((1, TC), lambda i: (0, i))],
        out_specs=pl.BlockSpec((S, TC), lambda i: (0, i)),
        compiler_params=_params(1),
    )


def _conv_gate_bwd(dy, u, conv_w, conv_b, phase=None):
    def body(dy_ref, a_ref, g_ref, w_ref, b_ref, du_ref, db_ref, dw_ref, da_buf, dg_buf, sems):
        i = pl.program_id(0)
        slot = i % 2

        def writes(step, s):
            col = pl.multiple_of(step * TC, TC)
            return (pltpu.make_async_copy(da_buf.at[s], du_ref.at[:, pl.ds(col, TC)], sems.at[0, s]),
                    pltpu.make_async_copy(dg_buf.at[s], du_ref.at[:, pl.ds(DFF + col, TC)], sems.at[1, s]))

        @pl.when(i >= 2)
        def _():
            for cp in writes(i - 2, slot):
                cp.wait()

        a = a_ref[...].astype(F32)
        dyv = dy_ref[...].astype(F32)
        pre = _conv_pre(a, w_ref, b_ref)
        sg = _sigmoid(pre)
        dg_buf[slot] = (dyv * pre * sg).astype(BF16)
        dpre = dyv * g_ref[...].astype(F32) * (sg * (1.0 + pre * (1.0 - sg)))
        da = w_ref[2:3, :] * dpre + w_ref[1:2, :] * _shift_rows(dpre, -1) + w_ref[0:1, :] * _shift_rows(dpre, -2)
        da_buf[slot] = da.astype(BF16)
        for cp in writes(i, slot):
            cp.start()
        db_ref[...] = jnp.sum(dpre, axis=0, keepdims=True)
        dw_ref[0:1, :] = jnp.sum(dpre * _shift_rows(a, 2), axis=0, keepdims=True)
        dw_ref[1:2, :] = jnp.sum(dpre * _shift_rows(a, 1), axis=0, keepdims=True)
        dw_ref[2:3, :] = jnp.sum(dpre * a, axis=0, keepdims=True)

        @pl.when(i == NCT - 1)
        def _():
            for cp in writes(i - 1, 1 - slot) + writes(i, slot):
                cp.wait()

    col = pl.BlockSpec((S, TC), lambda i: (0, i))
    return _pcall(
        body, [dy, u, u, conv_w, conv_b], phase, name="conv_gate_bwd",
        out_shape=(jax.ShapeDtypeStruct((S, 2 * DFF), BF16), jax.ShapeDtypeStruct((1, DFF), F32),
                   jax.ShapeDtypeStruct((3, DFF), F32)),
        grid=(NCT,),
        in_specs=[col, col, pl.BlockSpec((S, TC), lambda i: (0, NCT + i)),
                  pl.BlockSpec((3, TC), lambda i: (0, i)), pl.BlockSpec((1, TC), lambda i: (0, i))],
        out_specs=(pl.BlockSpec(memory_space=pl.ANY), pl.BlockSpec((1, TC), lambda i: (0, i)),
                   pl.BlockSpec((3, TC), lambda i: (0, i))),
        scratch_shapes=[pltpu.VMEM((2, S, TC), BF16), pltpu.VMEM((2, S, TC), BF16), pltpu.SemaphoreType.DMA((2, 2))],
        compiler_params=_params(1),
    )


def _out_loss(z, x1, target, gate):
    def body(z_ref, x_ref, t_ref, g_ref, do_ref, dz_ref, dg_ref, loss_ref):
        i = pl.program_id(0)
        zv = z_ref[...]
        gv = g_ref[...]
        err = x_ref[...] + gv * zv - t_ref[...]
        dout = err * (1.0 / D)
        do_ref[...] = dout
        dz_ref[...] = (dout * gv).astype(BF16)

        @pl.when(i == 0)
        def _():
            dg_ref[...] = jnp.zeros_like(dg_ref)
            loss_ref[...] = jnp.zeros_like(loss_ref)

        dg_ref[...] += jnp.sum(dout * zv, axis=0, keepdims=True)
        part = jnp.sum(jnp.sum(err * err, axis=0, keepdims=True), axis=-1, keepdims=True) * (0.5 / D)
        lane = lax.broadcasted_iota(jnp.int32, (1, 128), 1)
        loss_ref[...] += jnp.where(lane == 0, part, 0.0)

    return pl.pallas_call(
        body, name="out_loss",
        out_shape=(jax.ShapeDtypeStruct((S, D), F32), jax.ShapeDtypeStruct((S, D), BF16),
                   jax.ShapeDtypeStruct((1, D), F32), jax.ShapeDtypeStruct((1, 128), F32)),
        grid=(S // TR,),
        in_specs=[_row_spec(), _row_spec(), _row_spec(), _vec_spec()],
        out_specs=(_row_spec(), _row_spec(), _vec_spec(), _vec_spec(128)),
        compiler_params=_params(1),
    )(z, x1, target, gate)


ADA_COLS = 6 * D // N_CHIPS
TA = 512


def _ada_fwd(c_all, w_shard, b_shard):
    def body(c_ref, w_ref, b_ref, o_ref):
        cv = c_ref[...]
        o_ref[...] = _dot_f32(cv * _sigmoid(cv), w_ref[...]) + b_ref[...]

    return pl.pallas_call(
        body, name="ada_fwd", out_shape=jax.ShapeDtypeStruct((N_DEV, ADA_COLS), F32), grid=(ADA_COLS // TA,),
        in_specs=[pl.BlockSpec((N_DEV, D), lambda j: (0, 0)), pl.BlockSpec((D, TA), lambda j: (0, j)),
                  pl.BlockSpec((1, TA), lambda j: (0, j))],
        out_specs=pl.BlockSpec((N_DEV, TA), lambda j: (0, j)),
        compiler_params=_params(1),
    )(c_all, w_shard, b_shard)


def _ada_bwd(c_all, dmod_shard):
    def body(c_ref, d_ref, o_ref):
        cv = c_ref[...]
        o_ref[...] = lax.dot_general(cv * _sigmoid(cv), d_ref[...], TN, precision=lax.Precision.HIGHEST,
                                     preferred_element_type=F32)

    return pl.pallas_call(
        body, name="ada_bwd", out_shape=jax.ShapeDtypeStruct((D, ADA_COLS), F32), grid=(ADA_COLS // TA,),
        in_specs=[pl.BlockSpec((N_DEV, D), lambda j: (0, 0)), pl.BlockSpec((N_DEV, TA), lambda j: (0, j))],
        out_specs=pl.BlockSpec((D, TA), lambda j: (0, j)),
        compiler_params=_params(1),
    )(c_all, dmod_shard)


def _sum_rows(g, name):
    rows, n = g.shape

    def body(g_ref, o_ref):
        acc = g_ref[0:1, :]
        for i in range(1, rows):
            acc = acc + g_ref[i:i + 1, :]
        o_ref[...] = acc

    return pl.pallas_call(
        body, name=name, out_shape=jax.ShapeDtypeStruct((1, n), F32),
        in_specs=[VMEM_SPEC], out_specs=VMEM_SPEC,
    )(g)


def _lb_grad(lb_logits, dlb):
    def body(l_ref, d_ref, o_ref):
        p = 1.0 / (1.0 + jnp.exp(l_ref[1:2, :] - l_ref[0:1, :]))
        t = d_ref[...] * p * (1.0 - p)
        o_ref[0:1, :] = t
        o_ref[1:2, :] = -t

    return pl.pallas_call(
        body, name="lb_grad", out_shape=jax.ShapeDtypeStruct((2, HW), F32),
        in_specs=[VMEM_SPEC, VMEM_SPEC], out_specs=VMEM_SPEC,
    )(lb_logits, dlb)


def _adamw(w, g, m, v, name, copy_grad=False):
    rows, cols = w.shape
    tr = rows
    if rows * cols * 4 > ADAM_BLOCK_BYTES:
        tr = _pick(rows, [t for t in (256, 128, 64, 32, 16, 8) if t * cols * 4 <= ADAM_BLOCK_BYTES])

    def body(w_ref, g_ref, m_ref, v_ref, d_ref, mo_ref, vo_ref, *go_ref):
        gv = g_ref[...]
        if copy_grad:
            go_ref[0][...] = gv
        m2 = ADAM_B1 * m_ref[...] + (1.0 - ADAM_B1) * gv
        v2 = ADAM_B2 * v_ref[...] + (1.0 - ADAM_B2) * (gv * gv)
        m_hat = m2 / (1.0 - ADAM_B1 ** ADAM_STEP)
        v_hat = v2 / (1.0 - ADAM_B2 ** ADAM_STEP)
        d_ref[...] = -ADAM_LR * (m_hat / (jnp.sqrt(v_hat) + ADAM_EPS) + ADAM_WD * w_ref[...])
        mo_ref[...] = m2
        vo_ref[...] = v2

    spec = pl.BlockSpec((tr, cols), lambda i: (i, 0))
    out = jax.ShapeDtypeStruct((rows, cols), F32)
    n_out = 4 if copy_grad else 3
    return pl.pallas_call(
        body, name=name, out_shape=(out,) * n_out, grid=(rows // tr,),
        in_specs=[spec] * 4, out_specs=(spec,) * n_out,
        compiler_params=_params(1),
    )(w, g, m, v)


SC_TILES = 32
SC_LANES = 16
SC_COL_PARTS = 2
SC_CHUNK_ROWS = 8


def _adamw_sc(w, g, m, v, name):
    rows, cols = w.shape
    band, part = rows // (SC_TILES // SC_COL_PARTS), cols // SC_COL_PARTS
    assert band % SC_CHUNK_ROWS == 0 and part % SC_LANES == 0, (rows, cols)

    def body(w_hbm, g_hbm, m_hbm, v_hbm, d_hbm, mo_hbm, vo_hbm, go_hbm, wb, gb, mb, vb, db):
        tile = lax.axis_index("sc_subcore") * 2 + lax.axis_index("sc_core")
        row0 = (tile // SC_COL_PARTS) * band
        col0 = (tile % SC_COL_PARTS) * part

        @pl.loop(0, band, step=SC_CHUNK_ROWS)
        def _(r):
            at = lambda ref: ref.at[pl.ds(row0 + r, SC_CHUNK_ROWS), pl.ds(col0, part)]
            pltpu.sync_copy(at(w_hbm), wb)
            pltpu.sync_copy(at(g_hbm), gb)
            pltpu.sync_copy(gb, at(go_hbm))
            pltpu.sync_copy(at(m_hbm), mb)
            pltpu.sync_copy(at(v_hbm), vb)

            @pl.loop(0, SC_CHUNK_ROWS)
            def _(i):
                @pl.loop(0, part, step=SC_LANES)
                def _(j):
                    sl = (i, pl.ds(j, SC_LANES))
                    gv = gb[sl]
                    m2 = ADAM_B1 * mb[sl] + (1.0 - ADAM_B1) * gv
                    v2 = ADAM_B2 * vb[sl] + (1.0 - ADAM_B2) * (gv * gv)
                    m_hat = m2 / (1.0 - ADAM_B1 ** ADAM_STEP)
                    v_hat = v2 / (1.0 - ADAM_B2 ** ADAM_STEP)
                    db[sl] = -ADAM_LR * (m_hat / (jnp.sqrt(v_hat) + ADAM_EPS) + ADAM_WD * wb[sl])
                    mb[sl] = m2
                    vb[sl] = v2

            pltpu.sync_copy(db, at(d_hbm))
            pltpu.sync_copy(mb, at(mo_hbm))
            pltpu.sync_copy(vb, at(vo_hbm))

    out = jax.ShapeDtypeStruct((rows, cols), F32)
    buf = pltpu.VMEM((SC_CHUNK_ROWS, part), F32)
    return pl.kernel(
        body, name=name, out_type=(out, out, out, out),
        mesh=plsc.VectorSubcoreMesh(core_axis_name="sc_core", subcore_axis_name="sc_subcore"),
        scratch_types=[buf] * 5,
    )(w, g, m, v)


W_IN, W_OUT, W_UP, W_DOWN = range(4)
IN_CHUNKS = 4
W_INQ = 4


def _join_row_chunks(chunks):
    return jnp.concatenate(chunks, axis=0)


def _sequence_step(x, target, mod, norm1_w, lb_logits, hg_norm_w, q_norm_w, k_norm_w, norm2_w, conv_w, conv_b, net):
    shift1, scale1, gate1, shift2, scale2, gate2 = [mod[:, i * D:(i + 1) * D] for i in range(6)]

    def run(name, fn, *args, **kw):
        phase = net.host(name)
        if phase is None:
            return fn(*args, **kw)
        res, comm = fn(*args, phase=phase, **kw)
        net.done(name, comm)
        return res

    h = _norm_mod(x, norm1_w, scale1, shift1, "norm1_fwd")
    proj = run("proj_fwd", _matmul, h, net.full[W_IN], "nn", F32, "proj_fwd")
    cat, o_raw, states = run("hgrn_fwd", _hgrn_fwd, proj, lb_logits, hg_norm_w)
    qn, kn = _qk_prep(proj, q_norm_w, k_norm_w)
    att = [run(f"attn_fwd_d{dil}", _attn_fwd, proj, qn, kn, dil) for dil in DILS]
    cat, b_out_f32, lse = _attn_merge([t[0] for t in att], [t[1] for t in att], cat)
    mix = run("mix_fwd", _matmul, cat, net.full[W_OUT], "nn", F32, "mix_fwd")
    x1, h2 = _resid_norm_mod(x, mix, gate1, norm2_w, scale2, shift2, "norm2_fwd")
    u = run("up_fwd", _matmul, h2, net.full[W_UP], "nn", BF16, "up_fwd")
    yact = run("conv_gate_fwd", _conv_gate_fwd, u, conv_w, conv_b)
    net.alone("before_down_fwd")
    z = _matmul(yact, net.full[W_DOWN], "nn", F32, "down_fwd")
    dout, dz, dgate2, loss_row = _out_loss(z, x1, target, gate2)

    net.grad[W_DOWN] = _matmul(yact, dz, "tn", BF16, "down_bwd_w")
    dyact = run("down_bwd_x", _matmul, dz, net.full[W_DOWN], "nt", BF16, "down_bwd_x")
    du, dconv_b, dconv_w = run("conv_gate_bwd", _conv_gate_bwd, dyact, u, conv_w, conv_b)
    net.grad[W_UP] = run("up_bwd_w", _matmul, h2, du, "tn", BF16, "up_bwd_w")
    dh2 = run("up_bwd_x", _matmul, du, net.full[W_UP], "nt", F32, "up_bwd_x")
    dx1, dshift2, dscale2, dnorm2, dgate1, dmix = run(
        "norm2_bwd", _norm_mod_bwd, dh2, x1, norm2_w, scale2, dout, "norm2_bwd", mix=mix, gate=gate1)
    net.grad[W_OUT] = run("mix_bwd_w", _matmul, cat, dmix, "tn", BF16, "mix_bwd_w")
    dcat = run("mix_bwd_x", _matmul, dmix, net.full[W_OUT], "nt", F32, "mix_bwd_x")
    dhq, dhf, dhi, dhg, dlb, dhg_norm = run("hgrn_bwd", _hgrn_bwd, proj, lb_logits, hg_norm_w, o_raw, states, dcat)
    delta = _attn_delta(dcat, b_out_f32)
    acc = None
    for dil in DILS:
        acc = run(f"attn_bwd_d{dil}", _attn_bwd, proj, qn, kn, lse, delta, dcat, dil, acc)
    daq, dak, dav, dq_norm, dk_norm = _qk_norm_bwd(proj, *acc, q_norm_w, k_norm_w)
    dproj = jnp.concatenate([dhq, dhf, dhi, dhg, daq, dak, dav], axis=1)
    for q in range(IN_CHUNKS):
        net.grad[W_INQ + q] = run(f"proj_bwd_w_{q}", _matmul, h, dproj, "tn", BF16, f"proj_bwd_w_{q}",
                                  m_blocks=(D // IN_CHUNKS, [q]))
    dh = run("proj_bwd_x", _matmul, dproj, net.full[W_IN], "nt", F32, "proj_bwd_x")
    grad_x, dshift1, dscale1, dnorm1 = run("norm1_bwd", _norm_mod_bwd, dh, x, norm1_w, scale1, dx1, "norm1_bwd")

    dmod = jnp.concatenate([dshift1, dscale1, dgate1, dshift2, dscale2, dgate2], axis=1)
    small = dict(norm1_w=dnorm1, lb=dlb, hg_norm_w=dhg_norm, q_norm_w=dq_norm, k_norm_w=dk_norm,
                 norm2_w=dnorm2, conv_b=dconv_b, conv_w=dconv_w)
    return loss_row, grad_x, dmod, small


def _place():
    x, y, c = lax.axis_index("x"), lax.axis_index("y"), lax.axis_index("c")
    others = [(1 - x, y), (x, 1 - y), (1 - x, 1 - y)]
    return x, y, c, others


def _remote(src, dst, send_sems, recv_sems, k, to):
    return pltpu.make_async_remote_copy(src_ref=src, dst_ref=dst, send_sem=send_sems.at[k], recv_sem=recv_sems.at[k],
                                        device_id=to, device_id_type=MESH)


class _Phase:
    def __init__(self):
        self.ins, self.outs, self.aliases, self.groups, self.n = [], [], {}, [], 0

    def add(self, ins, outs, aliases, n, copies):
        i0, o0 = len(self.ins), len(self.outs)
        self.ins += list(ins)
        self.outs += list(outs)
        self.aliases.update({i0 + i: o0 + o for i, o in aliases.items()})
        self.groups.append((slice(i0, i0 + len(ins)), slice(o0, o0 + len(outs)), copies))
        self.n += n
        return slice(o0, o0 + len(outs))

    def build(self, cin, cout, send_sems, recv_sems, landing):
        res = []
        for si, so, copies in self.groups:
            for src, dst, land, peer in copies(cin[si], cout[so]):
                k = len(res)
                res.append(_remote(land, land, send_sems, recv_sems, k, peer) if landing
                           else _remote(src, dst, send_sems, recv_sems, k, peer))
        assert len(res) == self.n
        return res


def _pcall(body, args, phase=None, **kw):
    if phase is None or not phase.groups:
        res = pl.pallas_call(body, **kw)(*args)
        return res if phase is None else (res, ())
    grid = tuple(kw.pop("grid", ()))
    out_shape, out_specs = kw.pop("out_shape"), kw.pop("out_specs")
    single = not isinstance(out_shape, (tuple, list))
    out_shape = [out_shape] if single else list(out_shape)
    out_specs = [out_specs] if single else list(out_specs)
    scratch = list(kw.pop("scratch_shapes", ()))
    n_in, n_out, n_ci, n_co = len(args), len(out_shape), len(phase.ins), len(phase.outs)

    def hosted(*refs):
        ins, cin = refs[:n_in], refs[n_in:n_in + n_ci]
        outs = refs[n_in + n_ci:n_in + n_ci + n_out]
        cout = refs[n_in + n_ci + n_out:n_in + n_ci + n_out + n_co]
        scr, send_sems, recv_sems = refs[n_in + n_ci + n_out + n_co:-2], refs[-2], refs[-1]
        ids = [pl.program_id(a) for a in range(len(grid))]

        def start():
            for cp in phase.build(cin, cout, send_sems, recv_sems, False):
                cp.start()

        def finish():
            for cp in phase.build(cin, cout, send_sems, recv_sems, False):
                cp.wait_send()
            for cp in phase.build(cin, cout, send_sems, recv_sems, True):
                cp.wait_recv()

        if grid:
            first = functools.reduce(jnp.logical_and, [i == 0 for i in ids])
            last = functools.reduce(jnp.logical_and, [i == g - 1 for i, g in zip(ids, grid)])
            pl.when(first)(start)
            body(*ins, *outs, *scr)
            pl.when(last)(finish)
        else:
            start()
            body(*ins, *outs, *scr)
            finish()

    res = pl.pallas_call(
        hosted, out_shape=tuple(out_shape + phase.outs), grid=grid,
        in_specs=list(kw.pop("in_specs")) + [HBM_SPEC] * n_ci, out_specs=tuple(out_specs + [HBM_SPEC] * n_co),
        input_output_aliases={n_in + i: n_out + o for i, o in phase.aliases.items()},
        scratch_shapes=scratch + [pltpu.SemaphoreType.DMA((phase.n,)), pltpu.SemaphoreType.DMA((phase.n,))],
        **kw,
    )(*args, *phase.ins)
    outs = res[:n_out]
    return (outs[0] if single else tuple(outs)), tuple(res[n_out:])


def _comm_only(name, phase):
    return _pcall(lambda: None, [], phase, name=name, out_shape=(), in_specs=[], out_specs=())[1]


def _all_gather_rows(block, name, phase=None):
    m_per, n = block.shape

    def body(x_ref, out_ref, send_sems, recv_sems, local_sem):
        x, y, c, chips = _place()
        me, sibling = (x, y, c), (x, y, 1 - c)

        def rows(px, py, pc):
            return out_ref.at[pl.ds((4 * px + 2 * py + pc) * m_per, m_per), :]

        def copy(k, blk, to, src=None):
            return _remote(rows(*blk) if src is None else src, rows(*blk), send_sems, recv_sems, k, to)

        mine = pltpu.make_async_copy(x_ref, rows(*me), local_sem)
        mine.start()
        first = [copy(0, me, sibling, src=x_ref)]
        first += [copy(1 + j, me, (*chip, c), src=x_ref) for j, chip in enumerate(chips)]
        for cp in first:
            cp.start()
        passed = [copy(4 + j, (*chip, c), sibling) for j, chip in enumerate(chips)]
        for j, chip in enumerate(chips):
            copy(1 + j, (*chip, c), me).wait_recv()
            passed[j].start()
        copy(0, sibling, me).wait_recv()
        for j, chip in enumerate(chips):
            copy(4 + j, (*chip, 1 - c), me).wait_recv()
        for cp in first + passed:
            cp.wait_send()
        mine.wait()

    return _pcall(
        body, [block], phase, name=name, out_shape=jax.ShapeDtypeStruct((N_DEV * m_per, n), block.dtype),
        in_specs=[VMEM_SPEC], out_specs=VMEM_SPEC,
        scratch_shapes=[pltpu.SemaphoreType.DMA((7,)), pltpu.SemaphoreType.DMA((7,)), pltpu.SemaphoreType.DMA],
    )


class _Geo:
    def __init__(self, kind, shard_shape):
        self.kind = kind
        self.shard_shape = tuple(shard_shape)
        self.hr, self.hc = shard_shape[0] // 2, shard_shape[1]
        if kind == "col":
            self.full_shape = (shard_shape[0], N_CHIPS * shard_shape[1])
        else:
            self.full_shape = (N_CHIPS * shard_shape[0], shard_shape[1])

    def full_shard(self, ref, j):
        if self.kind == "col":
            return ref.at[:, pl.ds(pl.multiple_of(j * self.hc, 128), self.hc)]
        return ref.at[pl.ds(pl.multiple_of(j * 2 * self.hr, 16), 2 * self.hr), :]

    def full_half(self, ref, j, c):
        if self.kind == "col":
            return ref.at[pl.ds(pl.multiple_of(c * self.hr, 16), self.hr),
                          pl.ds(pl.multiple_of(j * self.hc, 128), self.hc)]
        return ref.at[pl.ds(pl.multiple_of((2 * j + c) * self.hr, 16), self.hr), :]

    def piece(self, ref, j, c, p0, p1, pieces, part=None):
        pr = self.hr // pieces
        off, n = p0 * pr, (p1 - p0) * pr
        if part is not None:
            top = (n // 32) * 16
            off, n = (off, top) if part == 0 else (off + top, n - top)
        if self.kind == "col":
            return ref.at[pl.ds(pl.multiple_of(c * self.hr + off, 16), n),
                          pl.ds(pl.multiple_of(j * self.hc, 128), self.hc)]
        return ref.at[pl.ds(pl.multiple_of((2 * j + c) * self.hr + off, 16), n), :]


WEIGHT_KINDS = ("col", "row", "col", "row")
NW = len(WEIGHT_KINDS)


def _comm_call(body, name, ins, outs, n_remote, aliases=None):
    return pl.pallas_call(
        body, name=name, out_shape=tuple(outs),
        in_specs=[HBM_SPEC] * len(ins), out_specs=tuple([HBM_SPEC] * len(outs)),
        input_output_aliases=aliases or {},
        scratch_shapes=[pltpu.SemaphoreType.DMA((n_remote,)), pltpu.SemaphoreType.DMA((n_remote,))],
    )(*ins)


ROW_TILES = (256, 176, 128, 64, 32, 16)


def _cast_into_full(shard, geo, place, name):
    rs, cs = shard.shape
    tr = _pick(rs, ROW_TILES)
    nb = rs // tr

    def body(place_ref, s_ref, o_ref):
        o_ref[...] = s_ref[...].astype(BF16)

    if geo.kind == "col":
        out_map = lambda i, place_ref: (i, place_ref[0])
    else:
        out_map = lambda i, place_ref: (place_ref[0] * nb + i, 0)
    return pl.pallas_call(
        body, name=name, out_shape=jax.ShapeDtypeStruct(geo.full_shape, BF16),
        grid_spec=pltpu.PrefetchScalarGridSpec(
            num_scalar_prefetch=1, grid=(nb,),
            in_specs=[pl.BlockSpec((tr, cs), lambda i, place_ref: (i, 0))],
            out_specs=pl.BlockSpec((tr, cs), out_map)),
        compiler_params=_params(1),
    )(place, shard)


def _gather_weight(full, geo, pieces, name):
    per = 8

    def body(in_ref, ref, send_sems, recv_sems):
        x, y, c, _ = _place()
        j, jx, jy, jo = 2 * x + y, 2 * (1 - x) + y, 2 * x + (1 - y), 2 * (1 - x) + (1 - y)
        nx, ny, sib = (1 - x, y, c), (x, 1 - y, c), (x, y, 1 - c)

        def cp(region, k, to):
            return _remote(region, region, send_sems, recv_sems, k, to)

        def reg(chip, p, part=None, core=c):
            return geo.piece(ref, chip, core, p, p + 1, pieces, part)

        sent = []
        for p in range(pieces):
            sent += [cp(reg(j, p), per * p, nx), cp(reg(j, p), per * p + 1, ny)]
        for d in sent:
            d.start()
        for p in range(pieces):
            cp(reg(jx, p), per * p, nx).wait_recv()
            new = [cp(reg(jx, p, 0), per * p + 2, ny), cp(reg(jx, p), per * p + 4, sib)]
            cp(reg(jy, p), per * p + 1, ny).wait_recv()
            new += [cp(reg(jy, p, 1), per * p + 3, nx), cp(reg(jy, p), per * p + 5, sib)]
            for d in new:
                d.start()
            sent += new
        for p in range(pieces):
            cp(reg(jo, p, 0), per * p + 2, ny).wait_recv()
            cp(reg(jo, p, 1), per * p + 3, nx).wait_recv()
            new = [cp(reg(jo, p, 0), per * p + 6, sib), cp(reg(jo, p, 1), per * p + 7, sib)]
            for d in new:
                d.start()
            sent += new
        for p in range(pieces):
            cp(reg(jx, p, None, 1 - c), per * p + 4, sib).wait_recv()
            cp(reg(jy, p, None, 1 - c), per * p + 5, sib).wait_recv()
            cp(reg(jo, p, 0, 1 - c), per * p + 6, sib).wait_recv()
            cp(reg(jo, p, 1, 1 - c), per * p + 7, sib).wait_recv()
        for d in sent:
            d.wait_send()

    return _comm_call(body, name, [full], [_same(full)], per * pieces, aliases={0: 0})[0]


def _same(a):
    return jax.ShapeDtypeStruct(a.shape, a.dtype)


def _gather_ici(full, geo, p0, p1, pieces):
    def copies(ins, outs):
        x, y, c, _ = _place()
        mine = geo.piece(outs[0], 2 * x + y, c, p0, p1, pieces)
        return [(mine, mine, geo.piece(outs[0], 2 * ox + oy, c, p0, p1, pieces), (ox, oy, c))
                for ox, oy in ((1 - x, y), (x, 1 - y))]

    return dict(ins=[full], outs=[_same(full)], aliases={0: 0}, n=2, copies=copies)


def _gather_relay(full, geo, p0, p1, pieces):
    def copies(ins, outs):
        x, y, c, _ = _place()
        jx, jy, jo = 2 * (1 - x) + y, 2 * x + (1 - y), 2 * (1 - x) + (1 - y)
        reg = lambda chip, part: geo.piece(outs[0], chip, c, p0, p1, pieces, part)
        return [(reg(jx, 0), reg(jx, 0), reg(jo, 0), (x, 1 - y, c)), (reg(jy, 1), reg(jy, 1), reg(jo, 1), (1 - x, y, c))]

    return dict(ins=[full], outs=[_same(full)], aliases={0: 0}, n=2, copies=copies)


def _gather_d2d(full, geo):
    def copies(ins, outs):
        x, y, c, chips = _place()
        return [(geo.full_half(outs[0], 2 * ox + oy, c), geo.full_half(outs[0], 2 * ox + oy, c),
                 geo.full_half(outs[0], 2 * ox + oy, 1 - c), (x, y, 1 - c)) for ox, oy in chips]

    return dict(ins=[full], outs=[_same(full)], aliases={0: 0}, n=3, copies=copies)


def _swap_d2d(grad, geo):
    def copies(ins, outs):
        x, y, c, _ = _place()
        return [(geo.full_half(ins[0], j, 1 - c), outs[0].at[j], outs[0].at[j], (x, y, 1 - c)) for j in range(N_CHIPS)]

    return dict(ins=[grad], outs=[jax.ShapeDtypeStruct((N_CHIPS, geo.hr, geo.hc), BF16)], aliases={}, n=N_CHIPS,
                copies=copies)


def _scatter_ici(pair, recv, geo, p0, p1, pieces):
    pr = geo.hr // pieces
    rows = pl.ds(p0 * pr, (p1 - p0) * pr)

    def copies(ins, outs):
        x, y, c, chips = _place()
        return [(ins[0].at[2 * ox + oy, rows, :], outs[0].at[k, rows, :], outs[0].at[k, rows, :], (ox, oy, c))
                for k, (ox, oy) in enumerate(chips)]

    out = jax.ShapeDtypeStruct((3, geo.hr, geo.hc), BF16)
    if recv is None:
        return dict(ins=[pair], outs=[out], aliases={}, n=3, copies=copies)
    return dict(ins=[pair, recv], outs=[out], aliases={1: 0}, n=3, copies=copies)


def _join_d2d(shard, geo, row0=0):
    def copies(ins, outs):
        x, y, c, _ = _place()
        mine = outs[0].at[pl.ds(pl.multiple_of(row0 + c * geo.hr, 8), geo.hr), :]
        other = outs[0].at[pl.ds(pl.multiple_of(row0 + (1 - c) * geo.hr, 8), geo.hr), :]
        return [(mine, mine, other, (x, y, 1 - c))]

    return dict(ins=[shard], outs=[_same(shard)], aliases={0: 0}, n=1, copies=copies)


def _add_pair(grad, got, geo, place, name):
    tr = _pick(geo.hr, ROW_TILES)
    nb = geo.hr // tr

    def body(place_ref, a_ref, b_ref, o_ref):
        o_ref[0] = (a_ref[...].astype(F32) + b_ref[0].astype(F32)).astype(BF16)

    if geo.kind == "col":
        own_map = lambda j, i, place_ref: (place_ref[1] * nb + i, j)
    else:
        own_map = lambda j, i, place_ref: ((2 * j + place_ref[1]) * nb + i, 0)
    spec = pl.BlockSpec((1, tr, geo.hc), lambda j, i, place_ref: (j, i, 0))
    return pl.pallas_call(
        body, name=name, out_shape=jax.ShapeDtypeStruct((N_CHIPS, geo.hr, geo.hc), BF16),
        grid_spec=pltpu.PrefetchScalarGridSpec(
            num_scalar_prefetch=1, grid=(N_CHIPS, nb),
            in_specs=[pl.BlockSpec((tr, geo.hc), own_map), spec], out_specs=spec),
        compiler_params=_params(2),
    )(place, grad, got)


def _add_four(pair, recv, geo, place, name, rows=None, row0=0, into=None):
    tr = _pick(geo.hr, ROW_TILES)
    nb = geo.hr // tr
    rows = 2 * geo.hr if rows is None else rows

    def body(place_ref, p_ref, r_ref, *rest):
        rest[-1][...] = ((p_ref[0].astype(F32) + r_ref[0].astype(F32)) + r_ref[1].astype(F32)) + r_ref[2].astype(F32)

    in_specs = [pl.BlockSpec((1, tr, geo.hc), lambda i, place_ref: (place_ref[0], i, 0)),
                pl.BlockSpec((3, tr, geo.hc), lambda i, place_ref: (0, i, 0))]
    args = [place, pair, recv]
    if into is not None:
        in_specs.append(pl.BlockSpec(memory_space=pl.ANY))
        args.append(into)
    return pl.pallas_call(
        body, name=name, out_shape=jax.ShapeDtypeStruct((rows, geo.hc), F32),
        grid_spec=pltpu.PrefetchScalarGridSpec(
            num_scalar_prefetch=1, grid=(nb,), in_specs=in_specs,
            out_specs=pl.BlockSpec((tr, geo.hc), lambda i, place_ref: (row0 // tr + place_ref[1] * nb + i, 0))),
        input_output_aliases={3: 0} if into is not None else {},
        compiler_params=_params(1),
    )(*args)


PIECES = (4, 1, 16, 8) + (1,) * IN_CHUNKS
SCHEDULE = {
    "proj_fwd": (("gather_ici", W_OUT, 0, 1), ("gather_ici", W_UP, 0, 6)),
    "hgrn_fwd": (("gather_relay", W_OUT, 0, 1), ("gather_relay", W_UP, 0, 6), ("gather_ici", W_UP, 6, 12)),
    "attn_fwd_d1": (("gather_relay", W_UP, 6, 12),),
    "attn_fwd_d4": (("gather_ici", W_UP, 12, 16), ("gather_d2d", W_OUT)),
    "attn_fwd_d16": (("gather_relay", W_UP, 12, 16), ("gather_ici", W_DOWN, 0, 2)),
    "mix_fwd": (("gather_d2d", W_UP), ("gather_ici", W_DOWN, 2, 4)),
    "up_fwd": (("gather_ici", W_DOWN, 4, 8), ("gather_relay", W_DOWN, 0, 4)),
    "conv_gate_fwd": (("gather_relay", W_DOWN, 4, 8),),
    "before_down_fwd": (("gather_d2d", W_DOWN),),
    "down_bwd_x": (("swap", W_DOWN),),
    "conv_gate_bwd": (("scatter", W_DOWN, 0, 4),),
    "up_bwd_w": (("scatter", W_DOWN, 4, 8),),
    "up_bwd_x": (("swap", W_UP),),
    "norm2_bwd": (("scatter", W_UP, 0, 1),),
    "mix_bwd_w": (("scatter", W_UP, 1, 2),),
    "mix_bwd_x": (("swap", W_OUT), ("scatter", W_UP, 2, 3)),
    "hgrn_bwd": (("scatter", W_UP, 3, 9), ("join", W_DOWN)),
    "attn_bwd_d1": (("scatter", W_UP, 9, 12),),
    "attn_bwd_d4": (("scatter", W_OUT, 0, 1),),
    "attn_bwd_d16": (("scatter", W_UP, 12, 16),),
    "proj_bwd_w_0": (("join", W_UP), ("join", W_OUT)),
    "proj_bwd_w_1": (("swap", W_INQ),),
    "proj_bwd_w_2": (("swap", W_INQ + 1),),
    "proj_bwd_w_3": (("swap", W_INQ + 2), ("scatter", W_INQ, 0, 1)),
    "proj_bwd_x": (("swap", W_INQ + 3), ("scatter", W_INQ + 1, 0, 1), ("scatter", W_INQ + 2, 0, 1)),
    "gather_stats": (("scatter", W_INQ + 3, 0, 1), ("join", W_INQ), ("join", W_INQ + 1), ("join", W_INQ + 2)),
    "grad_in_join": (("join", W_INQ + 3),),
}


class _Net:
    def __init__(self, shards, place):
        self.place = place
        self.geos = [_Geo(k, s.shape) for k, s in zip(WEIGHT_KINDS, shards)]
        self.full = [_cast_into_full(s, g, place, f"cast_shard_{w}") for w, (s, g) in enumerate(zip(shards, self.geos))]
        self.full[W_IN] = _gather_weight(self.full[W_IN], self.geos[W_IN], PIECES[W_IN], "gather_w_in")
        rows, cols = shards[W_IN].shape
        self.geos += [_Geo("col", (rows // IN_CHUNKS, cols))] * IN_CHUNKS
        n = NW + IN_CHUNKS
        self.grad, self.pair, self.recv, self.shard = [None] * n, [None] * n, [None] * n, [None] * n
        self.in_rows, self.in_shard = rows, None
        self.when_joined, self.joined = {}, {}
        self.slots = []

    def _row0(self, w):
        return (w - W_INQ) * 2 * self.geos[w].hr

    def host(self, name):
        phase = _Phase()
        self.slots = []
        groups = {}
        for item in SCHEDULE.get(name, ()):
            kind, w = item[0], item[1]
            geo = self.geos[w]
            key = len(groups)
            if kind == "gather_ici":
                spec, key = _gather_ici(self.full[w], geo, item[2], item[3], PIECES[w]), ("full", w)
            elif kind == "gather_relay":
                spec, key = _gather_relay(self.full[w], geo, item[2], item[3], PIECES[w]), ("full", w)
            elif kind == "gather_d2d":
                spec, key = _gather_d2d(self.full[w], geo), ("full", w)
            elif kind == "swap":
                spec = _swap_d2d(self.grad[w], geo)
            elif kind == "scatter":
                spec, key = _scatter_ici(self.pair[w], self.recv[w], geo, item[2], item[3], PIECES[w]), ("recv", w)
            elif w >= W_INQ:
                spec, key = _join_d2d(self.in_shard, geo, self._row0(w)), "in_shard"
            else:
                spec = _join_d2d(self.shard[w], geo)
            groups.setdefault(key, []).append((item, spec))
        for group in groups.values():
            specs = [s for _, s in group]
            merged = dict(specs[0], n=sum(s["n"] for s in specs),
                          copies=lambda ins, outs, specs=specs: [t for s in specs for t in s["copies"](ins, outs)])
            self.slots.append(([item for item, _ in group], phase.add(**merged)))
        return phase

    def done(self, name, comm):
        for items, sl in sorted(self.slots, key=lambda s: s[0][0][0] == "scatter"):
            out = comm[sl][0]
            for item in items:
                kind, w = item[0], item[1]
                if kind in ("gather_ici", "gather_relay", "gather_d2d"):
                    self.full[w] = out
                elif kind == "swap":
                    self.pair[w] = _add_pair(self.grad[w], out, self.geos[w], self.place, f"grad_pair_sum_{w}")
                elif kind == "scatter":
                    self.recv[w] = out
                    if item[3] == PIECES[w] and w >= W_INQ:
                        self.in_shard = _add_four(self.pair[w], out, self.geos[w], self.place, f"grad_chip_sum_{w}",
                                                  rows=self.in_rows, row0=self._row0(w), into=self.in_shard)
                    elif item[3] == PIECES[w]:
                        self.shard[w] = _add_four(self.pair[w], out, self.geos[w], self.place, f"grad_chip_sum_{w}")
                elif w >= W_INQ:
                    self.in_shard = out
                else:
                    self.shard[w] = out
                    if w in self.when_joined:
                        self.joined[w] = self.when_joined[w](out)

    def alone(self, name):
        self.done(name, _comm_only(name, self.host(name)))


CONVW_SHARD = 3 * DFF // N_CHIPS
COND_PAD = 8 * 896
SMALL_SIZES = (("norm1_w", D), ("lb", HW), ("hg_norm_w", DH), ("q_norm_w", DH), ("k_norm_w", DH),
               ("norm2_w", D), ("conv_b", DFF), ("conv_w", 3 * DFF), ("loss", 128))
SMALL_TOTAL = sum(n for _, n in SMALL_SIZES)
STATS_PAD = 8 * 5120


def kernel(x, c, w_ada, b_ada, norm1_w, w_in, lb_logits, hg_norm_w, q_norm_w, k_norm_w, w_out, norm2_w, w_up, conv_w, conv_b, w_down, loss_target, m_w_ada, m_b_ada, m_norm1_w, m_w_in, m_lb_logits, m_hg_norm_w, m_q_norm_w, m_k_norm_w, m_w_out, m_norm2_w, m_w_up, m_conv_w, m_conv_b, m_w_down, v_w_ada, v_b_ada, v_norm1_w, v_w_in, v_lb_logits, v_hg_norm_w, v_q_norm_w, v_k_norm_w, v_w_out, v_norm2_w, v_w_up, v_conv_w, v_conv_b, v_w_down):
    chip = 2 * lax.axis_index("x") + lax.axis_index("y")
    dev = 2 * chip + lax.axis_index("c")

    cond = jnp.concatenate([c, conv_w[0].reshape(1, CONVW_SHARD), jnp.zeros((1, COND_PAD - D - CONVW_SHARD), F32)], axis=1)
    cond_all = _all_gather_rows(cond.reshape(8, COND_PAD // 8), "gather_cond").reshape(N_DEV, COND_PAD)
    c_all = cond_all[:, :D]
    conv_w_full = jnp.concatenate(
        [cond_all[2 * j, D:D + CONVW_SHARD].reshape(3, DFF // N_CHIPS) for j in range(N_CHIPS)], axis=1)

    b_shard = lax.dynamic_slice_in_dim(b_ada, chip * ADA_COLS, ADA_COLS, axis=1)
    mod_cols = _all_gather_rows(_ada_fwd(c_all, w_ada[0], b_shard), "gather_mod")
    mod_all = jnp.concatenate([mod_cols[16 * j:16 * j + 8] for j in range(N_CHIPS)], axis=1)
    mod = lax.dynamic_slice_in_dim(mod_all, dev, 1, axis=0)

    place = jnp.stack([chip, lax.axis_index("c")]).astype(jnp.int32)
    net = _Net([w_in[0], w_out[0], w_up[0], w_down[0]], place)
    net.when_joined = {
        W_OUT: lambda grad: _adamw_sc(w_out[0], grad, m_w_out[0], v_w_out[0], "adamw_sc_w_out"),
        W_DOWN: lambda grad: _adamw_sc(w_down[0], grad, m_w_down[0], v_w_down[0], "adamw_sc_w_down"),
        W_UP: lambda grad: _adamw_sc(w_up[0], grad, m_w_up[0], v_w_up[0], "adamw_sc_w_up"),
    }
    early = {W_OUT: "w_out", W_DOWN: "w_down", W_UP: "w_up"}
    loss_row, grad_x, dmod, small = _sequence_step(
        x[0], loss_target[0], mod, norm1_w, lb_logits, hg_norm_w, q_norm_w, k_norm_w, norm2_w, conv_w_full, conv_b, net)
    small["conv_w"] = small["conv_w"].reshape(1, 3 * DFF)
    small["loss"] = loss_row
    stats = jnp.concatenate([dmod] + [small[k] for k, _ in SMALL_SIZES]
                            + [jnp.zeros((1, STATS_PAD - 6 * D - SMALL_TOTAL), F32)], axis=1)
    stats_all, comm = _all_gather_rows(stats.reshape(8, STATS_PAD // 8), "gather_stats", net.host("gather_stats"))
    net.done("gather_stats", comm)
    stats_all = stats_all.reshape(N_DEV, STATS_PAD)
    net.alone("grad_in_join")
    g_out, g_up, g_down = net.shard[W_OUT], net.shard[W_UP], net.shard[W_DOWN]
    g_in = net.in_shard
    dmod_all = stats_all[:, :6 * D]
    sums = _sum_rows(stats_all[:, 6 * D:6 * D + SMALL_TOTAL], "small_grad_sum")
    g_small, off = {}, 0
    for k, n in SMALL_SIZES:
        g_small[k] = sums[:, off:off + n]
        off += n
    loss = g_small["loss"][0, 0]
    g_b_ada = _sum_rows(dmod_all, "b_ada_grad_sum")
    g_w_ada = _ada_bwd(c_all, lax.dynamic_slice_in_dim(dmod_all, chip * ADA_COLS, ADA_COLS, axis=1))
    g_lb = _lb_grad(lb_logits, g_small["lb"])
    g_conv_w = lax.dynamic_slice_in_dim(g_small["conv_w"].reshape(3, DFF), chip * (DFF // N_CHIPS), DFF // N_CHIPS, axis=1)

    names = ["w_ada", "b_ada", "norm1_w", "w_in", "lb_logits", "hg_norm_w", "q_norm_w", "k_norm_w", "w_out",
             "norm2_w", "w_up", "conv_w", "conv_b", "w_down"]
    lead = {"w_ada", "w_in", "w_out", "w_up", "conv_w", "w_down"}
    w = dict(w_ada=w_ada, b_ada=b_ada, norm1_w=norm1_w, w_in=w_in, lb_logits=lb_logits, hg_norm_w=hg_norm_w,
             q_norm_w=q_norm_w, k_norm_w=k_norm_w, w_out=w_out, norm2_w=norm2_w, w_up=w_up, conv_w=conv_w,
             conv_b=conv_b, w_down=w_down)
    m = dict(w_ada=m_w_ada, b_ada=m_b_ada, norm1_w=m_norm1_w, w_in=m_w_in, lb_logits=m_lb_logits,
             hg_norm_w=m_hg_norm_w, q_norm_w=m_q_norm_w, k_norm_w=m_k_norm_w, w_out=m_w_out, norm2_w=m_norm2_w,
             w_up=m_w_up, conv_w=m_conv_w, conv_b=m_conv_b, w_down=m_w_down)
    v = dict(w_ada=v_w_ada, b_ada=v_b_ada, norm1_w=v_norm1_w, w_in=v_w_in, lb_logits=v_lb_logits,
             hg_norm_w=v_hg_norm_w, q_norm_w=v_q_norm_w, k_norm_w=v_k_norm_w, w_out=v_w_out, norm2_w=v_norm2_w,
             w_up=v_w_up, conv_w=v_conv_w, conv_b=v_conv_b, w_down=v_w_down)
    g = dict(w_ada=g_w_ada, b_ada=g_b_ada, norm1_w=g_small["norm1_w"], w_in=g_in, lb_logits=g_lb,
             hg_norm_w=g_small["hg_norm_w"], q_norm_w=g_small["q_norm_w"], k_norm_w=g_small["k_norm_w"], w_out=g_out,
             norm2_w=g_small["norm2_w"], w_up=g_up, conv_w=g_conv_w, conv_b=g_small["conv_b"], w_down=g_down)

    updated = {early[i]: res for i, res in net.joined.items()}
    grads, deltas, new_m, new_v = [], [], [], []
    for n in names:
        strip = (lambda t: t[0]) if n in lead else (lambda t: t)
        wrap = (lambda t: t[None]) if n in lead else (lambda t: t)
        g_n = g[n]
        if n in updated:
            d_n, m_n, v_n, g_n = updated[n]
        elif n == "w_in":
            d_n, m_n, v_n, g_n = _adamw(strip(w[n]), g_n, strip(m[n]), strip(v[n]), f"adamw_{n}", copy_grad=True)
        else:
            d_n, m_n, v_n = _adamw(strip(w[n]), g_n, strip(m[n]), strip(v[n]), f"adamw_{n}")
        grads.append(wrap(g_n))
        deltas.append(wrap(d_n))
        new_m.append(wrap(m_n))
        new_v.append(wrap(v_n))
    return (loss, grad_x[None], *grads, *deltas, *new_m, *new_v)
```

```python
import functools
import math

import jax
import jax.numpy as jnp
from jax import lax
from jax.experimental import pallas as pl
from jax.experimental.pallas import tpu as pltpu
from jax.experimental.pallas import tpu_sc as plsc

F32 = jnp.float32
BF16 = jnp.bfloat16

S = 2048
D = 2048
H = 8
DH = 128
HW = H * DH
CH = 64
NCH = S // CH
DFF = 5632
INC = 7 * HW
BLK = 128
DILS = (1, 4, 16)
EPS = 1e-6
NEG = -1e30
N_CHIPS = 4
N_DEV = 8

ADAM_LR = 0.001
ADAM_B1 = 0.9
ADAM_B2 = 0.999
ADAM_EPS = 1e-08
ADAM_WD = 0.01
ADAM_STEP = 10
ADAM_BLOCK_BYTES = 1 << 21

V7X_VMEM_BYTES = 64 * 1024 * 1024
VMEM_LIMIT = (V7X_VMEM_BYTES * 3) // 4
MESH = pl.DeviceIdType.MESH
HBM_SPEC = pl.BlockSpec(memory_space=pltpu.HBM)
VMEM_SPEC = pl.BlockSpec(memory_space=pltpu.VMEM)

NN = (((1,), (0,)), ((), ()))
NT = (((1,), (1,)), ((), ()))
TN = (((0,), (0,)), ((), ()))


def _params(n_grid):
    return pltpu.CompilerParams(dimension_semantics=("arbitrary",) * n_grid, vmem_limit_bytes=VMEM_LIMIT)


def _dot(a, b, dims=NN):
    return lax.dot_general(a.astype(BF16), b.astype(BF16), dims, preferred_element_type=F32)


def _dot_f32(a, b):
    return lax.dot_general(a, b, NN, precision=lax.Precision.HIGHEST, preferred_element_type=F32)


def _sigmoid(x):
    return 1.0 / (1.0 + jnp.exp(-x))


def _pick(n, cands):
    for t in cands:
        if n % t == 0:
            return t
    raise ValueError(n)


def _matmul(a, b, mode, out_dtype, name, phase=None, m_blocks=None):
    if mode == "nn":
        (m, k), (_, n) = a.shape, b.shape
    elif mode == "nt":
        (m, k), (n, _) = a.shape, b.shape
    else:
        (k, m), (_, n) = a.shape, b.shape
    tm = _pick(m, (1024, 1408, 512))
    a_block = lambda i: i
    if m_blocks is not None:
        tm, blocks = m_blocks
        m = tm * len(blocks)
        step = blocks[1] - blocks[0] if len(blocks) > 1 else 0
        assert all(blk == blocks[0] + step * i for i, blk in enumerate(blocks))
        a_block = lambda i: blocks[0] + step * i
    tn = _pick(n, (1024, 1408, 512))
    tk = _pick(k, (2048, 2816, 1792, 1024, 512))
    nk = k // tk
    dims = {"nn": NN, "nt": NT, "tn": TN}[mode]

    def body(a_ref, b_ref, o_ref, *acc):
        part = lax.dot_general(a_ref[...], b_ref[...], dims, preferred_element_type=F32)
        if nk == 1:
            o_ref[...] = part.astype(o_ref.dtype)
            return
        acc_ref, kk = acc[0], pl.program_id(2)

        @pl.when(kk == 0)
        def _():
            acc_ref[...] = part

        @pl.when(jnp.logical_and(kk > 0, kk < nk - 1))
        def _():
            acc_ref[...] += part

        @pl.when(kk == nk - 1)
        def _():
            o_ref[...] = (acc_ref[...] + part).astype(o_ref.dtype)

    if mode == "tn":
        a_spec = pl.BlockSpec((tk, tm), lambda i, j, kk: (kk, a_block(i)))
    else:
        a_spec = pl.BlockSpec((tm, tk), lambda i, j, kk: (i, kk))
    if mode == "nt":
        b_spec = pl.BlockSpec((tn, tk), lambda i, j, kk: (j, kk))
    else:
        b_spec = pl.BlockSpec((tk, tn), lambda i, j, kk: (kk, j))
    return _pcall(
        body, [a, b], phase, name=name,
        out_shape=jax.ShapeDtypeStruct((m, n), out_dtype),
        grid=(m // tm, n // tn, nk),
        in_specs=[a_spec, b_spec],
        out_specs=pl.BlockSpec((tm, tn), lambda i, j, kk: (i, j)),
        scratch_shapes=[pltpu.VMEM((tm, tn), F32)] if nk > 1 else [],
        compiler_params=_params(3),
    )


TR = 256


def _row_spec(cols=D):
    return pl.BlockSpec((TR, cols), lambda i: (i, 0))


def _vec_spec(cols=D):
    return pl.BlockSpec((1, cols), lambda i: (0, 0))


def _norm_mod(x, nw, scale, shift, name):
    def body(x_ref, nw_ref, sc_ref, sh_ref, h_ref):
        xv = x_ref[...]
        r = lax.rsqrt(jnp.mean(xv * xv, axis=-1, keepdims=True) + EPS)
        h_ref[...] = ((xv * r) * nw_ref[...] * (1.0 + sc_ref[...]) + sh_ref[...]).astype(BF16)

    return pl.pallas_call(
        body, name=name, out_shape=jax.ShapeDtypeStruct((S, D), BF16), grid=(S // TR,),
        in_specs=[_row_spec(), _vec_spec(), _vec_spec(), _vec_spec()], out_specs=_row_spec(),
        compiler_params=_params(1),
    )(x, nw, scale, shift)


def _resid_norm_mod(x, mix, gate, nw, scale, shift, name):
    def body(x_ref, mix_ref, g_ref, nw_ref, sc_ref, sh_ref, x1_ref, h_ref):
        xv = x_ref[...] + g_ref[...] * mix_ref[...]
        x1_ref[...] = xv
        r = lax.rsqrt(jnp.mean(xv * xv, axis=-1, keepdims=True) + EPS)
        h_ref[...] = ((xv * r) * nw_ref[...] * (1.0 + sc_ref[...]) + sh_ref[...]).astype(BF16)

    return pl.pallas_call(
        body, name=name,
        out_shape=(jax.ShapeDtypeStruct((S, D), F32), jax.ShapeDtypeStruct((S, D), BF16)), grid=(S // TR,),
        in_specs=[_row_spec(), _row_spec(), _vec_spec(), _vec_spec(), _vec_spec(), _vec_spec()],
        out_specs=(_row_spec(), _row_spec()),
        compiler_params=_params(1),
    )(x, mix, gate, nw, scale, shift)


def _norm_mod_bwd(dh, xin, nw, scale, dres, name, mix=None, gate=None, phase=None):
    with_gate = mix is not None

    def body(*refs):
        if with_gate:
            dh_ref, x_ref, nw_ref, sc_ref, dres_ref, mix_ref, g_ref, dx_ref, dsh_ref, dsc_ref, dnw_ref, dg_ref, dmix_ref = refs
        else:
            dh_ref, x_ref, nw_ref, sc_ref, dres_ref, dx_ref, dsh_ref, dsc_ref, dnw_ref = refs
        i = pl.program_id(0)
        dhv = dh_ref[...]
        xv = x_ref[...]
        r = lax.rsqrt(jnp.mean(xv * xv, axis=-1, keepdims=True) + EPS)
        xn = xv * r
        nwv = nw_ref[...]
        one_sc = 1.0 + sc_ref[...]
        dxn = dhv * nwv * one_sc
        dx = dres_ref[...] + r * (dxn - xn * jnp.mean(dxn * xn, axis=-1, keepdims=True))
        dx_ref[...] = dx

        @pl.when(i == 0)
        def _():
            dsh_ref[...] = jnp.zeros_like(dsh_ref)
            dsc_ref[...] = jnp.zeros_like(dsc_ref)
            dnw_ref[...] = jnp.zeros_like(dnw_ref)
            if with_gate:
                dg_ref[...] = jnp.zeros_like(dg_ref)

        dsh_ref[...] += jnp.sum(dhv, axis=0, keepdims=True)
        dsc_ref[...] += jnp.sum(dhv * xn * nwv, axis=0, keepdims=True)
        dnw_ref[...] += jnp.sum(dhv * xn * one_sc, axis=0, keepdims=True)
        if with_gate:
            dg_ref[...] += jnp.sum(dx * mix_ref[...], axis=0, keepdims=True)
            dmix_ref[...] = (dx * g_ref[...]).astype(BF16)

    vec = jax.ShapeDtypeStruct((1, D), F32)
    ins = [dh, xin, nw, scale, dres]
    in_specs = [_row_spec(), _row_spec(), _vec_spec(), _vec_spec(), _row_spec()]
    outs = [jax.ShapeDtypeStruct((S, D), F32), vec, vec, vec]
    out_specs = [_row_spec(), _vec_spec(), _vec_spec(), _vec_spec()]
    if with_gate:
        ins += [mix, gate]
        in_specs += [_row_spec(), _vec_spec()]
        outs += [vec, jax.ShapeDtypeStruct((S, D), BF16)]
        out_specs += [_vec_spec(), _row_spec()]
    return _pcall(
        body, ins, phase, name=name, out_shape=tuple(outs), grid=(S // TR,), in_specs=in_specs,
        out_specs=tuple(out_specs), compiler_params=_params(1),
    )


def _tri(lower):
    row = lax.broadcasted_iota(jnp.int32, (CH, CH), 0)
    col = lax.broadcasted_iota(jnp.int32, (CH, CH), 1)
    return (row >= col) if lower else (col >= row)


def _hgrn_gates(hq, hf, lb):
    sig = _sigmoid(hf)
    f = lb + (1.0 - lb) * sig
    g = jnp.log(f)
    sq = _sigmoid(hq)
    return sig, f, g, 1.0 - f, hq * sq, sq


HG_HP = 2
HG_UNROLL = 8


def _proj_col_spec(group):
    return pl.BlockSpec((S, HG_HP * DH), lambda h: (0, group * (H // HG_HP) + h))


def _hgrn_fwd(proj, lb_logits, norm_w, phase=None):
    def body(hq_ref, hf_ref, hi_ref, hg_ref, lbl_ref, nw_ref, out_ref, oraw_ref, st_ref, s_ref):
        nw = nw_ref[...]
        lower = _tri(True)
        ltri = lower.astype(F32)
        s_ref[...] = jnp.zeros_like(s_ref)

        def chunk(n, carry):
            rows = pl.ds(pl.multiple_of(n * CH, CH), CH)
            for j in range(HG_HP):
                cols = slice(j * DH, (j + 1) * DH)
                lb = 1.0 / (1.0 + jnp.exp(lbl_ref[1:2, cols] - lbl_ref[0:1, cols]))
                hq, hf, v, hg = hq_ref[rows, cols], hf_ref[rows, cols], hi_ref[rows, cols], hg_ref[rows, cols]
                _, _, g, kk, q, _ = _hgrn_gates(hq, hf, lb)
                gc = _dot_f32(ltri, g)
                gl = jnp.sum(g, axis=0, keepdims=True)
                qe = q * jnp.exp(gc)
                ke = kk * jnp.exp(gl - gc)
                qh = q * jnp.exp(gc - 0.5 * gl)
                kh = kk * jnp.exp(0.5 * gl - gc)
                st = s_ref[j]
                st_ref[j, pl.ds(n, 1)] = st[None]
                a = jnp.where(lower, _dot(qh, kh, NT), 0.0)
                o = _dot(qe, st, NT) + _dot(a, v)
                s_ref[j] = st * jnp.exp(gl) + _dot(v, ke, TN)
                oraw_ref[rows, cols] = o
                rs = lax.rsqrt(jnp.mean(o * o, axis=-1, keepdims=True) + EPS)
                out_ref[rows, cols] = (o * rs * nw * (hg * _sigmoid(hg))).astype(BF16)
            return carry

        lax.fori_loop(0, NCH, chunk, 0, unroll=HG_UNROLL)

    head_spec = pl.BlockSpec((S, HG_HP * DH), lambda h: (0, h))
    return _pcall(
        body, [proj, proj, proj, proj, lb_logits, norm_w], phase, name="hgrn_fwd",
        out_shape=(jax.ShapeDtypeStruct((S, 2 * HW), BF16), jax.ShapeDtypeStruct((S, HW), F32),
                   jax.ShapeDtypeStruct((H, NCH, DH, DH), F32)),
        grid=(H // HG_HP,),
        in_specs=[_proj_col_spec(0), _proj_col_spec(1), _proj_col_spec(2), _proj_col_spec(3),
                  pl.BlockSpec((2, HG_HP * DH), lambda h: (0, h)), pl.BlockSpec((1, DH), lambda h: (0, 0))],
        out_specs=(head_spec, head_spec, pl.BlockSpec((HG_HP, NCH, DH, DH), lambda h: (h, 0, 0, 0))),
        scratch_shapes=[pltpu.VMEM((HG_HP, DH, DH), F32)],
        compiler_params=_params(1),
    )


def _hgrn_bwd(proj, lb_logits, norm_w, oraw, states, dout, phase=None):
    def body(hq_ref, hf_ref, hi_ref, hg_ref, lbl_ref, nw_ref, oraw_ref, st_ref, do_ref,
             dhq_ref, dhf_ref, dhi_ref, dhg_ref, dlb_ref, dnw_ref, ds_ref, acc_ref):
        h = pl.program_id(0)
        nw = nw_ref[...]
        lower = _tri(True)
        ltri = lower.astype(F32)
        utri = _tri(False).astype(F32)
        last = lax.broadcasted_iota(jnp.int32, (CH, DH), 0) == CH - 1
        ds_ref[...] = jnp.zeros_like(ds_ref)
        acc_ref[...] = jnp.zeros_like(acc_ref)

        @pl.when(h == 0)
        def _():
            dnw_ref[...] = jnp.zeros_like(dnw_ref)

        def chunk(i, carry):
            n = NCH - 1 - i
            rows = pl.ds(pl.multiple_of(n * CH, CH), CH)
            for j in range(HG_HP):
                cols = slice(j * DH, (j + 1) * DH)
                lb = 1.0 / (1.0 + jnp.exp(lbl_ref[1:2, cols] - lbl_ref[0:1, cols]))
                hq, hf, v, hg = hq_ref[rows, cols], hf_ref[rows, cols], hi_ref[rows, cols], hg_ref[rows, cols]
                sig, f, g, kk, q, sq = _hgrn_gates(hq, hf, lb)
                gc = _dot_f32(ltri, g)
                gl = jnp.sum(g, axis=0, keepdims=True)
                eg = jnp.exp(gc)
                ek = jnp.exp(gl - gc)
                gam = jnp.exp(gl)
                ph = jnp.exp(gc - 0.5 * gl)
                pk = jnp.exp(0.5 * gl - gc)
                qe, ke = q * eg, kk * ek
                qh, kh = q * ph, kk * pk
                st = st_ref[j, pl.ds(n, 1)][0]
                dst = ds_ref[j]
                a = jnp.where(lower, _dot(qh, kh, NT), 0.0)
                o = oraw_ref[rows, cols]
                rs = lax.rsqrt(jnp.mean(o * o, axis=-1, keepdims=True) + EPS)
                xh = o * rs
                sg = _sigmoid(hg)
                dgo = do_ref[rows, cols]
                d_on = dgo * (hg * sg)
                dhg_ref[rows, cols] = (dgo * xh * nw * (sg * (1.0 + hg * (1.0 - sg)))).astype(BF16)
                acc_ref[j, 1:2, :] += jnp.sum(d_on * xh, axis=0, keepdims=True)
                dy = d_on * nw
                do = rs * (dy - xh * jnp.mean(dy * xh, axis=-1, keepdims=True))
                dqe = _dot(do, st)
                da = jnp.where(lower, _dot(do, v, NT), 0.0)
                dv = _dot(a, do, TN) + _dot(ke, dst, NT)
                dke = _dot(v, dst)
                dgam = jnp.sum(dst * st, axis=0, keepdims=True)
                dqh = _dot(da, kh)
                dkh = _dot(da, qh, TN)
                dq = dqh * ph + dqe * eg
                dk = dkh * pk + dke * ek
                dgl = jnp.sum(dke * ke, axis=0, keepdims=True) + dgam * gam
                dgc = dqh * qh - dkh * kh + dqe * qe - dke * ke + jnp.where(last, dgl, 0.0)
                dg = _dot_f32(utri, dgc)
                df = dg / f - dk
                dhf_ref[rows, cols] = (df * (1.0 - lb) * sig * (1.0 - sig)).astype(BF16)
                acc_ref[j, 0:1, :] += jnp.sum(df * (1.0 - sig), axis=0, keepdims=True)
                dhq_ref[rows, cols] = (dq * (sq * (1.0 + hq * (1.0 - sq)))).astype(BF16)
                dhi_ref[rows, cols] = dv.astype(BF16)
                ds_ref[j] = dst * gam + _dot(do, qe, TN)
            return carry

        lax.fori_loop(0, NCH, chunk, 0, unroll=HG_UNROLL)
        for j in range(HG_HP):
            dlb_ref[:, j * DH:(j + 1) * DH] = acc_ref[j, 0:1, :]
            dnw_ref[...] += acc_ref[j, 1:2, :]

    head_spec = pl.BlockSpec((S, HG_HP * DH), lambda h: (0, h))
    head_out = jax.ShapeDtypeStruct((S, HW), BF16)
    return _pcall(
        body, [proj, proj, proj, proj, lb_logits, norm_w, oraw, states, dout], phase, name="hgrn_bwd",
        out_shape=(head_out, head_out, head_out, head_out,
                   jax.ShapeDtypeStruct((1, HW), F32), jax.ShapeDtypeStruct((1, DH), F32)),
        grid=(H // HG_HP,),
        in_specs=[_proj_col_spec(0), _proj_col_spec(1), _proj_col_spec(2), _proj_col_spec(3),
                  pl.BlockSpec((2, HG_HP * DH), lambda h: (0, h)), pl.BlockSpec((1, DH), lambda h: (0, 0)),
                  head_spec, pl.BlockSpec((HG_HP, NCH, DH, DH), lambda h: (h, 0, 0, 0)), head_spec],
        out_specs=(head_spec, head_spec, head_spec, head_spec,
                   pl.BlockSpec((1, HG_HP * DH), lambda h: (0, h)), pl.BlockSpec((1, DH), lambda h: (0, 0))),
        scratch_shapes=[pltpu.VMEM((HG_HP, DH, DH), F32), pltpu.VMEM((HG_HP, 8, DH), F32)],
        compiler_params=_params(1),
    )


ATT_SCALE = DH ** -0.5
HEADS_PER_STEP = {1: 8, 4: 1, 16: 1}


def _sub_rows(ref, r, dil, cols):
    if dil == 1:
        return ref[:, cols]
    return ref[pl.ds(r, BLK, stride=dil), cols]


def _set_sub_rows(ref, r, dil, cols, val):
    if dil == 1:
        ref[:, cols] = val
    else:
        ref[pl.ds(r, BLK, stride=dil), cols] = val


def _slopes(dil):
    hp = HEADS_PER_STEP[dil]
    s = jnp.asarray([dil * 2.0 ** (-(h + 1)) for h in range(H)], F32)
    return jnp.broadcast_to(s[:, None], (H, DH)).reshape(H // hp, hp, DH)


def _rms_rows(x, w):
    rs = lax.rsqrt(jnp.mean(x * x, axis=-1, keepdims=True) + EPS)
    return x * rs, rs


def _att_masks():
    qi = lax.broadcasted_iota(jnp.int32, (BLK, BLK), 0)
    kj = lax.broadcasted_iota(jnp.int32, (BLK, BLK), 1)
    steps_c = qi - kj
    steps_p = qi - kj + BLK
    return steps_c, steps_p, steps_c >= 0, steps_p <= BLK


def _qk_prep(proj, q_w, k_w):
    def body(q_ref, k_ref, qw_ref, kw_ref, qn_ref, kn_ref):
        for h in range(H):
            cols = slice(h * DH, (h + 1) * DH)
            qn_ref[:, cols] = _rms_rows(q_ref[:, cols], None)[0] * qw_ref[...]
            kn_ref[:, cols] = _rms_rows(k_ref[:, cols], None)[0] * kw_ref[...]

    out = jax.ShapeDtypeStruct((S, HW), F32)
    wspec = pl.BlockSpec((1, DH), lambda i: (0, 0))
    return pl.pallas_call(
        body, name="qk_prep", out_shape=(out, out), grid=(S // TR,),
        in_specs=[pl.BlockSpec((TR, HW), lambda i: (i, 4)), pl.BlockSpec((TR, HW), lambda i: (i, 5)), wspec, wspec],
        out_specs=(_row_spec(HW), _row_spec(HW)),
        compiler_params=_params(1),
    )(proj, proj, q_w, k_w)


def _qk_norm_bwd(proj, dqn, dkn, dv, q_w, k_w):
    def body(q_ref, k_ref, dqn_ref, dkn_ref, dv_ref, qw_ref, kw_ref, dq_ref, dk_ref, dvo_ref, dqw_ref, dkw_ref):
        i = pl.program_id(0)

        @pl.when(i == 0)
        def _():
            dqw_ref[...] = jnp.zeros_like(dqw_ref)
            dkw_ref[...] = jnp.zeros_like(dkw_ref)

        dvo_ref[...] = dv_ref[...].astype(BF16)
        for x_ref, d_ref, w_ref, o_ref, dw_ref in ((q_ref, dqn_ref, qw_ref, dq_ref, dqw_ref),
                                                   (k_ref, dkn_ref, kw_ref, dk_ref, dkw_ref)):
            for h in range(H):
                cols = slice(h * DH, (h + 1) * DH)
                xh, rs = _rms_rows(x_ref[:, cols], None)
                d = d_ref[:, cols]
                dw_ref[...] += jnp.sum(d * xh, axis=0, keepdims=True)
                dy = d * w_ref[...]
                o_ref[:, cols] = (rs * (dy - xh * jnp.mean(dy * xh, axis=-1, keepdims=True))).astype(BF16)

    big = jax.ShapeDtypeStruct((S, HW), BF16)
    small = jax.ShapeDtypeStruct((1, DH), F32)
    wspec = pl.BlockSpec((1, DH), lambda i: (0, 0))
    return pl.pallas_call(
        body, name="qk_norm_bwd", out_shape=(big, big, big, small, small), grid=(S // TR,),
        in_specs=[pl.BlockSpec((TR, HW), lambda i: (i, 4)), pl.BlockSpec((TR, HW), lambda i: (i, 5)),
                  _row_spec(HW), _row_spec(HW), _row_spec(HW), wspec, wspec],
        out_specs=(_row_spec(HW), _row_spec(HW), _row_spec(HW), wspec, wspec),
        compiler_params=_params(1),
    )(proj, proj, dqn, dkn, dv, q_w, k_w)


def _attn_delta(do, o):
    def body(do_ref, o_ref, d_ref):
        for h in range(H):
            cols = slice(h * DH, (h + 1) * DH)
            d_ref[:, cols] = jnp.broadcast_to(jnp.sum(do_ref[:, cols] * o_ref[:, cols], axis=-1, keepdims=True), (TR, DH))

    return pl.pallas_call(
        body, name="attn_delta", out_shape=jax.ShapeDtypeStruct((S, HW), F32), grid=(S // TR,),
        in_specs=[pl.BlockSpec((TR, HW), lambda i: (i, 1)), _row_spec(HW)], out_specs=_row_spec(HW),
        compiler_params=_params(1),
    )(do, o)


def _attn_fwd(proj, qn, kn, dil, phase=None):
    hp = HEADS_PER_STEP[dil]
    rows, ng, nb = BLK * dil, H // hp, S // (BLK * dil)

    def body(q_ref, kc_ref, kp_ref, vc_ref, vp_ref, sl_ref, o_ref, lse_ref):
        n = pl.program_id(0)
        steps_c, steps_p, ok_c, ok_p = _att_masks()
        ok_p = jnp.logical_and(ok_p, n > 0)
        fc, fp = steps_c.astype(F32), steps_p.astype(F32)
        for hh in range(hp):
            cols = slice(hh * DH, (hh + 1) * DH)
            sl = sl_ref[0, hh:hh + 1, :]
            for r in range(dil):
                sub = lambda ref: _sub_rows(ref, r, dil, cols)
                qv = sub(q_ref)
                sc = jnp.where(ok_c, _dot(qv, sub(kc_ref), NT) * ATT_SCALE - sl * fc, NEG)
                sp = jnp.where(ok_p, _dot(qv, sub(kp_ref), NT) * ATT_SCALE - sl * fp, NEG)
                m = jnp.maximum(jnp.max(sc, axis=-1, keepdims=True), jnp.max(sp, axis=-1, keepdims=True))
                pc, pp = jnp.exp(sc - m), jnp.exp(sp - m)
                den = jnp.sum(pc, axis=-1, keepdims=True) + jnp.sum(pp, axis=-1, keepdims=True)
                o = (_dot(pc, sub(vc_ref)) + _dot(pp, sub(vp_ref))) / den
                _set_sub_rows(o_ref, r, dil, cols, o)
                _set_sub_rows(lse_ref, r, dil, cols, jnp.broadcast_to(m + jnp.log(den), (BLK, DH)))

    cur = pl.BlockSpec((rows, hp * DH), lambda n, g: (n, g))
    prev = pl.BlockSpec((rows, hp * DH), lambda n, g: (jnp.maximum(n - 1, 0), g))
    vcur = pl.BlockSpec((rows, hp * DH), lambda n, g: (n, 6 * ng + g))
    vprev = pl.BlockSpec((rows, hp * DH), lambda n, g: (jnp.maximum(n - 1, 0), 6 * ng + g))
    out = jax.ShapeDtypeStruct((S, HW), F32)
    return _pcall(
        body, [qn, kn, kn, proj, proj, _slopes(dil)], phase,
        name=f"attn_fwd_d{dil}", out_shape=(out, out), grid=(nb, ng),
        in_specs=[cur, cur, prev, vcur, vprev, pl.BlockSpec((1, hp, DH), lambda n, g: (g, 0, 0))],
        out_specs=(cur, cur),
        compiler_params=_params(2),
    )


def _attn_merge(outs, lses, cat):
    def body(o1, o2, o3, l1, l2, l3, cat_ref, ob_ref, of_ref, lse_ref):
        a, b, c = l1[...], l2[...], l3[...]
        m = jnp.maximum(jnp.maximum(a, b), c)
        ea, eb, ec = jnp.exp(a - m), jnp.exp(b - m), jnp.exp(c - m)
        den = ea + eb + ec
        o = (ea * o1[...] + eb * o2[...] + ec * o3[...]) / den
        ob_ref[...] = o.astype(BF16)
        of_ref[...] = o
        lse_ref[...] = m + jnp.log(den)

    f = jax.ShapeDtypeStruct((S, HW), F32)
    return pl.pallas_call(
        body, name="attn_merge", out_shape=(jax.ShapeDtypeStruct((S, 2 * HW), BF16), f, f), grid=(S // TR,),
        in_specs=[_row_spec(HW)] * 6 + [pl.BlockSpec(memory_space=pl.ANY)],
        out_specs=(pl.BlockSpec((TR, HW), lambda i: (i, 1)), _row_spec(HW), _row_spec(HW)),
        input_output_aliases={6: 0},
        compiler_params=_params(1),
    )(*outs, *lses, cat)


def _attn_bwd(proj, qn, kn, lse, delta, do, dil, acc, phase=None):
    hp = HEADS_PER_STEP[dil]
    rows, ng, nb = BLK * dil, H // hp, S // (BLK * dil)
    has_acc = acc is not None

    def body(*refs):
        q_ref, qn_ref, kp_ref, kc_ref, vp_ref, vc_ref, l_ref, ln_ref, d_ref, dn_ref, do_ref, don_ref, sl_ref = refs[:13]
        refs = refs[13:]
        if has_acc:
            aq_ref, ak_ref, av_ref = refs[:3]
            refs = refs[3:]
        dq_ref, dk_ref, dv_ref = refs
        m = pl.program_id(0)
        steps_c, steps_p, ok_c, ok_p = _att_masks()
        ok_prev = jnp.logical_and(ok_p, m > 0)
        ok_next = jnp.logical_and(ok_p, m < nb - 1)
        fc, fp = steps_c.astype(F32), steps_p.astype(F32)
        for hh in range(hp):
            cols = slice(hh * DH, (hh + 1) * DH)
            sl = sl_ref[0, hh:hh + 1, :]
            for r in range(dil):
                sub = lambda ref: _sub_rows(ref, r, dil, cols)
                qv, qnv, kcv, kpv = sub(q_ref), sub(qn_ref), sub(kc_ref), sub(kp_ref)
                vc, vp = sub(vc_ref), sub(vp_ref)
                dov, donv = sub(do_ref), sub(don_ref)
                lse_m, lse_n = sub(l_ref)[:, 0:1], sub(ln_ref)[:, 0:1]
                delta_m, delta_n = sub(d_ref)[:, 0:1], sub(dn_ref)[:, 0:1]
                p_c = jnp.where(ok_c, jnp.exp(_dot(qv, kcv, NT) * ATT_SCALE - sl * fc - lse_m), 0.0)
                p_p = jnp.where(ok_prev, jnp.exp(_dot(qv, kpv, NT) * ATT_SCALE - sl * fp - lse_m), 0.0)
                p_n = jnp.where(ok_next, jnp.exp(_dot(qnv, kcv, NT) * ATT_SCALE - sl * fp - lse_n), 0.0)
                ds_c = p_c * (_dot(dov, vc, NT) - delta_m)
                ds_p = p_p * (_dot(dov, vp, NT) - delta_m)
                ds_n = p_n * (_dot(donv, vc, NT) - delta_n)
                dqn = (_dot(ds_c, kcv) + _dot(ds_p, kpv)) * ATT_SCALE
                dkn = (_dot(ds_c, qv, TN) + _dot(ds_n, qnv, TN)) * ATT_SCALE
                dv = _dot(p_c, dov, TN) + _dot(p_n, donv, TN)
                if has_acc:
                    dqn, dkn, dv = dqn + sub(aq_ref), dkn + sub(ak_ref), dv + sub(av_ref)
                _set_sub_rows(dq_ref, r, dil, cols, dqn)
                _set_sub_rows(dk_ref, r, dil, cols, dkn)
                _set_sub_rows(dv_ref, r, dil, cols, dv)

    def shifted(shift, m):
        if shift < 0:
            return jnp.maximum(m - 1, 0)
        if shift > 0:
            return jnp.minimum(m + 1, nb - 1)
        return m

    def spec(shift, group=0):
        return pl.BlockSpec((rows, hp * DH), lambda m, g: (shifted(shift, m), group * ng + g))

    ins = [qn, qn, kn, kn, proj, proj, lse, lse, delta, delta, do, do, _slopes(dil)]
    in_specs = [spec(0), spec(1), spec(-1), spec(0), spec(-1, 6), spec(0, 6), spec(0), spec(1), spec(0), spec(1),
                spec(0, 1), spec(1, 1), pl.BlockSpec((1, hp, DH), lambda m, g: (g, 0, 0))]
    if has_acc:
        ins += list(acc)
        in_specs += [spec(0)] * 3
    big = jax.ShapeDtypeStruct((S, HW), F32)
    return _pcall(
        body, ins, phase, name=f"attn_bwd_d{dil}", out_shape=(big, big, big), grid=(nb, ng),
        in_specs=in_specs, out_specs=(spec(0),) * 3,
        compiler_params=_params(2),
    )


TC = 512
NCT = DFF // TC


def _shift_rows(a, k):
    rows = lax.broadcasted_iota(jnp.int32, a.shape, 0)
    rolled = pltpu.roll(a, k % S, 0)
    if k > 0:
        return jnp.where(rows >= k, rolled, 0.0)
    return jnp.where(rows < S + k, rolled, 0.0)


def _conv_pre(a, w_ref, b_ref):
    return b_ref[...] + w_ref[0:1, :] * _shift_rows(a, 2) + w_ref[1:2, :] * _shift_rows(a, 1) + w_ref[2:3, :] * a


def _conv_gate_fwd(u, conv_w, conv_b, phase=None):
    def body(a_ref, g_ref, w_ref, b_ref, y_ref):
        pre = _conv_pre(a_ref[...].astype(F32), w_ref, b_ref)
        y_ref[...] = (pre * _sigmoid(pre) * g_ref[...].astype(F32)).astype(BF16)

    return _pcall(
        body, [u, u, conv_w, conv_b], phase,
        name="conv_gate_fwd", out_shape=jax.ShapeDtypeStruct((S, DFF), BF16), grid=(NCT,),
        in_specs=[pl.BlockSpec((S, TC), lambda i: (0, i)), pl.BlockSpec((S, TC), lambda i: (0, NCT + i)),
                  pl.BlockSpec((3, TC), lambda i: (0, i)), pl.BlockSpec((1, TC), lambda i: (0, i))],
        out_specs=pl.BlockSpec((S, TC), lambda i: (0, i)),
        compiler_params=_params(1),
    )


def _conv_gate_bwd(dy, u, conv_w, conv_b, phase=None):
    def body(dy_ref, a_ref, g_ref, w_ref, b_ref, du_ref, db_ref, dw_ref, da_buf, dg_buf, sems):
        i = pl.program_id(0)
        slot = i % 2

        def writes(step, s):
            col = pl.multiple_of(step * TC, TC)
            return (pltpu.make_async_copy(da_buf.at[s], du_ref.at[:, pl.ds(col, TC)], sems.at[0, s]),
                    pltpu.make_async_copy(dg_buf.at[s], du_ref.at[:, pl.ds(DFF + col, TC)], sems.at[1, s]))

        @pl.when(i >= 2)
        def _():
            for cp in writes(i - 2, slot):
                cp.wait()

        a = a_ref[...].astype(F32)
        dyv = dy_ref[...].astype(F32)
        pre = _conv_pre(a, w_ref, b_ref)
        sg = _sigmoid(pre)
        dg_buf[slot] = (dyv * pre * sg).astype(BF16)
        dpre = dyv * g_ref[...].astype(F32) * (sg * (1.0 + pre * (1.0 - sg)))
        da = w_ref[2:3, :] * dpre + w_ref[1:2, :] * _shift_rows(dpre, -1) + w_ref[0:1, :] * _shift_rows(dpre, -2)
        da_buf[slot] = da.astype(BF16)
        for cp in writes(i, slot):
            cp.start()
        db_ref[...] = jnp.sum(dpre, axis=0, keepdims=True)
        dw_ref[0:1, :] = jnp.sum(dpre * _shift_rows(a, 2), axis=0, keepdims=True)
        dw_ref[1:2, :] = jnp.sum(dpre * _shift_rows(a, 1), axis=0, keepdims=True)
        dw_ref[2:3, :] = jnp.sum(dpre * a, axis=0, keepdims=True)

        @pl.when(i == NCT - 1)
        def _():
            for cp in writes(i - 1, 1 - slot) + writes(i, slot):
                cp.wait()

    col = pl.BlockSpec((S, TC), lambda i: (0, i))
    return _pcall(
        body, [dy, u, u, conv_w, conv_b], phase, name="conv_gate_bwd",
        out_shape=(jax.ShapeDtypeStruct((S, 2 * DFF), BF16), jax.ShapeDtypeStruct((1, DFF), F32),
                   jax.ShapeDtypeStruct((3, DFF), F32)),
        grid=(NCT,),
        in_specs=[col, col, pl.BlockSpec((S, TC), lambda i: (0, NCT + i)),
                  pl.BlockSpec((3, TC), lambda i: (0, i)), pl.BlockSpec((1, TC), lambda i: (0, i))],
        out_specs=(pl.BlockSpec(memory_space=pl.ANY), pl.BlockSpec((1, TC), lambda i: (0, i)),
                   pl.BlockSpec((3, TC), lambda i: (0, i))),
        scratch_shapes=[pltpu.VMEM((2, S, TC), BF16), pltpu.VMEM((2, S, TC), BF16), pltpu.SemaphoreType.DMA((2, 2))],
        compiler_params=_params(1),
    )


def _out_loss(z, x1, target, gate):
    def body(z_ref, x_ref, t_ref, g_ref, do_ref, dz_ref, dg_ref, loss_ref):
        i = pl.program_id(0)
        zv = z_ref[...]
        gv = g_ref[...]
        err = x_ref[...] + gv * zv - t_ref[...]
        dout = err * (1.0 / D)
        do_ref[...] = dout
        dz_ref[...] = (dout * gv).astype(BF16)

        @pl.when(i == 0)
        def _():
            dg_ref[...] = jnp.zeros_like(dg_ref)
            loss_ref[...] = jnp.zeros_like(loss_ref)

        dg_ref[...] += jnp.sum(dout * zv, axis=0, keepdims=True)
        part = jnp.sum(jnp.sum(err * err, axis=0, keepdims=True), axis=-1, keepdims=True) * (0.5 / D)
        lane = lax.broadcasted_iota(jnp.int32, (1, 128), 1)
        loss_ref[...] += jnp.where(lane == 0, part, 0.0)

    return pl.pallas_call(
        body, name="out_loss",
        out_shape=(jax.ShapeDtypeStruct((S, D), F32), jax.ShapeDtypeStruct((S, D), BF16),
                   jax.ShapeDtypeStruct((1, D), F32), jax.ShapeDtypeStruct((1, 128), F32)),
        grid=(S // TR,),
        in_specs=[_row_spec(), _row_spec(), _row_spec(), _vec_spec()],
        out_specs=(_row_spec(), _row_spec(), _vec_spec(), _vec_spec(128)),
        compiler_params=_params(1),
    )(z, x1, target, gate)


ADA_COLS = 6 * D // N_CHIPS
TA = 512


def _ada_fwd(c_all, w_shard, b_shard):
    def body(c_ref, w_ref, b_ref, o_ref):
        cv = c_ref[...]
        o_ref[...] = _dot_f32(cv * _sigmoid(cv), w_ref[...]) + b_ref[...]

    return pl.pallas_call(
        body, name="ada_fwd", out_shape=jax.ShapeDtypeStruct((N_DEV, ADA_COLS), F32), grid=(ADA_COLS // TA,),
        in_specs=[pl.BlockSpec((N_DEV, D), lambda j: (0, 0)), pl.BlockSpec((D, TA), lambda j: (0, j)),
                  pl.BlockSpec((1, TA), lambda j: (0, j))],
        out_specs=pl.BlockSpec((N_DEV, TA), lambda j: (0, j)),
        compiler_params=_params(1),
    )(c_all, w_shard, b_shard)


ADA_ROWS = 128


def _ada_bwd_adamw(c_all, dmod_shard, w, m, v):
    def body(c_ref, d_ref, w_ref, m_ref, v_ref, g_ref, dl_ref, mo_ref, vo_ref):
        cv = c_ref[...]
        gv = lax.dot_general(cv * _sigmoid(cv), d_ref[...], TN, precision=lax.Precision.HIGHEST,
                             preferred_element_type=F32)
        g_ref[...] = gv
        m2 = ADAM_B1 * m_ref[...] + (1.0 - ADAM_B1) * gv
        v2 = ADAM_B2 * v_ref[...] + (1.0 - ADAM_B2) * (gv * gv)
        m_hat = m2 / (1.0 - ADAM_B1 ** ADAM_STEP)
        v_hat = v2 / (1.0 - ADAM_B2 ** ADAM_STEP)
        dl_ref[...] = -ADAM_LR * (m_hat / (jnp.sqrt(v_hat) + ADAM_EPS) + ADAM_WD * w_ref[...])
        mo_ref[...] = m2
        vo_ref[...] = v2

    spec = pl.BlockSpec((ADA_ROWS, ADA_COLS), lambda i: (i, 0))
    out = jax.ShapeDtypeStruct((D, ADA_COLS), F32)
    return pl.pallas_call(
        body, name="ada_bwd_adamw", out_shape=(out,) * 4, grid=(D // ADA_ROWS,),
        in_specs=[pl.BlockSpec((N_DEV, ADA_ROWS), lambda i: (0, i)), pl.BlockSpec((N_DEV, ADA_COLS), lambda i: (0, 0)),
                  spec, spec, spec],
        out_specs=(spec,) * 4,
        compiler_params=_params(1),
    )(c_all, dmod_shard, w, m, v)


def _sum_rows(g, name):
    rows, n = g.shape

    def body(g_ref, o_ref):
        acc = g_ref[0:1, :]
        for i in range(1, rows):
            acc = acc + g_ref[i:i + 1, :]
        o_ref[...] = acc

    return pl.pallas_call(
        body, name=name, out_shape=jax.ShapeDtypeStruct((1, n), F32),
        in_specs=[VMEM_SPEC], out_specs=VMEM_SPEC,
    )(g)


def _lb_grad(lb_logits, dlb):
    def body(l_ref, d_ref, o_ref):
        p = 1.0 / (1.0 + jnp.exp(l_ref[1:2, :] - l_ref[0:1, :]))
        t = d_ref[...] * p * (1.0 - p)
        o_ref[0:1, :] = t
        o_ref[1:2, :] = -t

    return pl.pallas_call(
        body, name="lb_grad", out_shape=jax.ShapeDtypeStruct((2, HW), F32),
        in_specs=[VMEM_SPEC, VMEM_SPEC], out_specs=VMEM_SPEC,
    )(lb_logits, dlb)


def _adamw(w, g, m, v, name, copy_grad=False):
    rows, cols = w.shape
    tr = rows
    if rows * cols * 4 > ADAM_BLOCK_BYTES:
        tr = _pick(rows, [t for t in (256, 128, 64, 32, 16, 8) if t * cols * 4 <= ADAM_BLOCK_BYTES])

    def body(w_ref, g_ref, m_ref, v_ref, d_ref, mo_ref, vo_ref, *go_ref):
        gv = g_ref[...]
        if copy_grad:
            go_ref[0][...] = gv
        m2 = ADAM_B1 * m_ref[...] + (1.0 - ADAM_B1) * gv
        v2 = ADAM_B2 * v_ref[...] + (1.0 - ADAM_B2) * (gv * gv)
        m_hat = m2 / (1.0 - ADAM_B1 ** ADAM_STEP)
        v_hat = v2 / (1.0 - ADAM_B2 ** ADAM_STEP)
        d_ref[...] = -ADAM_LR * (m_hat / (jnp.sqrt(v_hat) + ADAM_EPS) + ADAM_WD * w_ref[...])
        mo_ref[...] = m2
        vo_ref[...] = v2

    spec = pl.BlockSpec((tr, cols), lambda i: (i, 0))
    out = jax.ShapeDtypeStruct((rows, cols), F32)
    n_out = 4 if copy_grad else 3
    return pl.pallas_call(
        body, name=name, out_shape=(out,) * n_out, grid=(rows // tr,),
        in_specs=[spec] * 4, out_specs=(spec,) * n_out,
        compiler_params=_params(1),
    )(w, g, m, v)


SC_TILES = 32
SC_LANES = 16
SC_COL_PARTS = 2
SC_CHUNK_ROWS = 8


def _adamw_sc(w, g, m, v, name):
    rows, cols = w.shape
    band, part = rows // (SC_TILES // SC_COL_PARTS), cols // SC_COL_PARTS
    assert band % SC_CHUNK_ROWS == 0 and part % SC_LANES == 0, (rows, cols)

    def body(w_hbm, g_hbm, m_hbm, v_hbm, d_hbm, mo_hbm, vo_hbm, go_hbm, wb, gb, mb, vb, db):
        tile = lax.axis_index("sc_subcore") * 2 + lax.axis_index("sc_core")
        row0 = (tile // SC_COL_PARTS) * band
        col0 = (tile % SC_COL_PARTS) * part

        @pl.loop(0, band, step=SC_CHUNK_ROWS)
        def _(r):
            at = lambda ref: ref.at[pl.ds(row0 + r, SC_CHUNK_ROWS), pl.ds(col0, part)]
            pltpu.sync_copy(at(w_hbm), wb)
            pltpu.sync_copy(at(g_hbm), gb)
            pltpu.sync_copy(gb, at(go_hbm))
            pltpu.sync_copy(at(m_hbm), mb)
            pltpu.sync_copy(at(v_hbm), vb)

            @pl.loop(0, SC_CHUNK_ROWS)
            def _(i):
                @pl.loop(0, part, step=SC_LANES)
                def _(j):
                    sl = (i, pl.ds(j, SC_LANES))
                    gv = gb[sl]
                    m2 = ADAM_B1 * mb[sl] + (1.0 - ADAM_B1) * gv
                    v2 = ADAM_B2 * vb[sl] + (1.0 - ADAM_B2) * (gv * gv)
                    m_hat = m2 / (1.0 - ADAM_B1 ** ADAM_STEP)
                    v_hat = v2 / (1.0 - ADAM_B2 ** ADAM_STEP)
                    db[sl] = -ADAM_LR * (m_hat / (jnp.sqrt(v_hat) + ADAM_EPS) + ADAM_WD * wb[sl])
                    mb[sl] = m2
                    vb[sl] = v2

            pltpu.sync_copy(db, at(d_hbm))
            pltpu.sync_copy(mb, at(mo_hbm))
            pltpu.sync_copy(vb, at(vo_hbm))

    out = jax.ShapeDtypeStruct((rows, cols), F32)
    buf = pltpu.VMEM((SC_CHUNK_ROWS, part), F32)
    return pl.kernel(
        body, name=name, out_type=(out, out, out, out),
        mesh=plsc.VectorSubcoreMesh(core_axis_name="sc_core", subcore_axis_name="sc_subcore"),
        scratch_types=[buf] * 5,
    )(w, g, m, v)


W_IN, W_OUT, W_UP, W_DOWN = range(4)
IN_CHUNKS = 4
W_INQ = 4


def _join_row_chunks(chunks):
    return jnp.concatenate(chunks, axis=0)


def _sequence_step(x, target, mod, norm1_w, lb_logits, hg_norm_w, q_norm_w, k_norm_w, norm2_w, conv_w, conv_b, net):
    shift1, scale1, gate1, shift2, scale2, gate2 = [mod[:, i * D:(i + 1) * D] for i in range(6)]

    def run(name, fn, *args, **kw):
        phase = net.host(name)
        if phase is None:
            return fn(*args, **kw)
        res, comm = fn(*args, phase=phase, **kw)
        net.done(name, comm)
        return res

    h = _norm_mod(x, norm1_w, scale1, shift1, "norm1_fwd")
    proj = run("proj_fwd", _matmul, h, net.full[W_IN], "nn", F32, "proj_fwd")
    cat, o_raw, states = run("hgrn_fwd", _hgrn_fwd, proj, lb_logits, hg_norm_w)
    qn, kn = _qk_prep(proj, q_norm_w, k_norm_w)
    att = [run(f"attn_fwd_d{dil}", _attn_fwd, proj, qn, kn, dil) for dil in DILS]
    cat, b_out_f32, lse = _attn_merge([t[0] for t in att], [t[1] for t in att], cat)
    mix = run("mix_fwd", _matmul, cat, net.full[W_OUT], "nn", F32, "mix_fwd")
    x1, h2 = _resid_norm_mod(x, mix, gate1, norm2_w, scale2, shift2, "norm2_fwd")
    u = run("up_fwd", _matmul, h2, net.full[W_UP], "nn", BF16, "up_fwd")
    yact = run("conv_gate_fwd", _conv_gate_fwd, u, conv_w, conv_b)
    net.alone("before_down_fwd")
    z = _matmul(yact, net.full[W_DOWN], "nn", F32, "down_fwd")
    dout, dz, dgate2, loss_row = _out_loss(z, x1, target, gate2)

    net.grad[W_DOWN] = _matmul(yact, dz, "tn", BF16, "down_bwd_w")
    dyact = run("down_bwd_x", _matmul, dz, net.full[W_DOWN], "nt", BF16, "down_bwd_x")
    du, dconv_b, dconv_w = run("conv_gate_bwd", _conv_gate_bwd, dyact, u, conv_w, conv_b)
    net.grad[W_UP] = run("up_bwd_w", _matmul, h2, du, "tn", BF16, "up_bwd_w")
    dh2 = run("up_bwd_x", _matmul, du, net.full[W_UP], "nt", F32, "up_bwd_x")
    dx1, dshift2, dscale2, dnorm2, dgate1, dmix = run(
        "norm2_bwd", _norm_mod_bwd, dh2, x1, norm2_w, scale2, dout, "norm2_bwd", mix=mix, gate=gate1)
    net.grad[W_OUT] = run("mix_bwd_w", _matmul, cat, dmix, "tn", BF16, "mix_bwd_w")
    dcat = run("mix_bwd_x", _matmul, dmix, net.full[W_OUT], "nt", F32, "mix_bwd_x")
    dhq, dhf, dhi, dhg, dlb, dhg_norm = run("hgrn_bwd", _hgrn_bwd, proj, lb_logits, hg_norm_w, o_raw, states, dcat)
    delta = _attn_delta(dcat, b_out_f32)
    acc = None
    for dil in DILS:
        acc = run(f"attn_bwd_d{dil}", _attn_bwd, proj, qn, kn, lse, delta, dcat, dil, acc)
    daq, dak, dav, dq_norm, dk_norm = _qk_norm_bwd(proj, *acc, q_norm_w, k_norm_w)
    dproj = jnp.concatenate([dhq, dhf, dhi, dhg, daq, dak, dav], axis=1)
    for q in range(IN_CHUNKS):
        net.grad[W_INQ + q] = run(f"proj_bwd_w_{q}", _matmul, h, dproj, "tn", BF16, f"proj_bwd_w_{q}",
                                  m_blocks=(D // IN_CHUNKS, [q]))
    dh = run("proj_bwd_x", _matmul, dproj, net.full[W_IN], "nt", F32, "proj_bwd_x")
    grad_x, dshift1, dscale1, dnorm1 = run("norm1_bwd", _norm_mod_bwd, dh, x, norm1_w, scale1, dx1, "norm1_bwd")

    dmod = jnp.concatenate([dshift1, dscale1, dgate1, dshift2, dscale2, dgate2], axis=1)
    small = dict(norm1_w=dnorm1, lb=dlb, hg_norm_w=dhg_norm, q_norm_w=dq_norm, k_norm_w=dk_norm,
                 norm2_w=dnorm2, conv_b=dconv_b, conv_w=dconv_w)
    return loss_row, grad_x, dmod, small


def _place():
    x, y, c = lax.axis_index("x"), lax.axis_index("y"), lax.axis_index("c")
    others = [(1 - x, y), (x, 1 - y), (1 - x, 1 - y)]
    return x, y, c, others


def _remote(src, dst, send_sems, recv_sems, k, to):
    return pltpu.make_async_remote_copy(src_ref=src, dst_ref=dst, send_sem=send_sems.at[k], recv_sem=recv_sems.at[k],
                                        device_id=to, device_id_type=MESH)


class _Phase:
    def __init__(self):
        self.ins, self.outs, self.aliases, self.groups, self.n = [], [], {}, [], 0

    def add(self, ins, outs, aliases, n, copies):
        i0, o0 = len(self.ins), len(self.outs)
        self.ins += list(ins)
        self.outs += list(outs)
        self.aliases.update({i0 + i: o0 + o for i, o in aliases.items()})
        self.groups.append((slice(i0, i0 + len(ins)), slice(o0, o0 + len(outs)), copies))
        self.n += n
        return slice(o0, o0 + len(outs))

    def build(self, cin, cout, send_sems, recv_sems, landing):
        res = []
        for si, so, copies in self.groups:
            for src, dst, land, peer in copies(cin[si], cout[so]):
                k = len(res)
                res.append(_remote(land, land, send_sems, recv_sems, k, peer) if landing
                           else _remote(src, dst, send_sems, recv_sems, k, peer))
        assert len(res) == self.n
        return res


def _pcall(body, args, phase=None, **kw):
    if phase is None or not phase.groups:
        res = pl.pallas_call(body, **kw)(*args)
        return res if phase is None else (res, ())
    grid = tuple(kw.pop("grid", ()))
    out_shape, out_specs = kw.pop("out_shape"), kw.pop("out_specs")
    single = not isinstance(out_shape, (tuple, list))
    out_shape = [out_shape] if single else list(out_shape)
    out_specs = [out_specs] if single else list(out_specs)
    scratch = list(kw.pop("scratch_shapes", ()))
    n_in, n_out, n_ci, n_co = len(args), len(out_shape), len(phase.ins), len(phase.outs)

    def hosted(*refs):
        ins, cin = refs[:n_in], refs[n_in:n_in + n_ci]
        outs = refs[n_in + n_ci:n_in + n_ci + n_out]
        cout = refs[n_in + n_ci + n_out:n_in + n_ci + n_out + n_co]
        scr, send_sems, recv_sems = refs[n_in + n_ci + n_out + n_co:-2], refs[-2], refs[-1]
        ids = [pl.program_id(a) for a in range(len(grid))]

        def start():
            for cp in phase.build(cin, cout, send_sems, recv_sems, False):
                cp.start()

        def finish():
            for cp in phase.build(cin, cout, send_sems, recv_sems, False):
                cp.wait_send()
            for cp in phase.build(cin, cout, send_sems, recv_sems, True):
                cp.wait_recv()

        if grid:
            first = functools.reduce(jnp.logical_and, [i == 0 for i in ids])
            last = functools.reduce(jnp.logical_and, [i == g - 1 for i, g in zip(ids, grid)])
            pl.when(first)(start)
            body(*ins, *outs, *scr)
            pl.when(last)(finish)
        else:
            start()
            body(*ins, *outs, *scr)
            finish()

    res = pl.pallas_call(
        hosted, out_shape=tuple(out_shape + phase.outs), grid=grid,
        in_specs=list(kw.pop("in_specs")) + [HBM_SPEC] * n_ci, out_specs=tuple(out_specs + [HBM_SPEC] * n_co),
        input_output_aliases={n_in + i: n_out + o for i, o in phase.aliases.items()},
        scratch_shapes=scratch + [pltpu.SemaphoreType.DMA((phase.n,)), pltpu.SemaphoreType.DMA((phase.n,))],
        **kw,
    )(*args, *phase.ins)
    outs = res[:n_out]
    return (outs[0] if single else tuple(outs)), tuple(res[n_out:])


def _comm_only(name, phase):
    return _pcall(lambda: None, [], phase, name=name, out_shape=(), in_specs=[], out_specs=())[1]


def _all_gather_rows(block, name, phase=None):
    m_per, n = block.shape

    def body(x_ref, out_ref, send_sems, recv_sems, local_sem):
        x, y, c, chips = _place()
        me, sibling = (x, y, c), (x, y, 1 - c)

        def rows(px, py, pc):
            return out_ref.at[pl.ds((4 * px + 2 * py + pc) * m_per, m_per), :]

        def copy(k, blk, to, src=None):
            return _remote(rows(*blk) if src is None else src, rows(*blk), send_sems, recv_sems, k, to)

        mine = pltpu.make_async_copy(x_ref, rows(*me), local_sem)
        mine.start()
        first = [copy(0, me, sibling, src=x_ref)]
        first += [copy(1 + j, me, (*chip, c), src=x_ref) for j, chip in enumerate(chips)]
        for cp in first:
            cp.start()
        passed = [copy(4 + j, (*chip, c), sibling) for j, chip in enumerate(chips)]
        for j, chip in enumerate(chips):
            copy(1 + j, (*chip, c), me).wait_recv()
            passed[j].start()
        copy(0, sibling, me).wait_recv()
        for j, chip in enumerate(chips):
            copy(4 + j, (*chip, 1 - c), me).wait_recv()
        for cp in first + passed:
            cp.wait_send()
        mine.wait()

    return _pcall(
        body, [block], phase, name=name, out_shape=jax.ShapeDtypeStruct((N_DEV * m_per, n), block.dtype),
        in_specs=[VMEM_SPEC], out_specs=VMEM_SPEC,
        scratch_shapes=[pltpu.SemaphoreType.DMA((7,)), pltpu.SemaphoreType.DMA((7,)), pltpu.SemaphoreType.DMA],
    )


class _Geo:
    def __init__(self, kind, shard_shape):
        self.kind = kind
        self.shard_shape = tuple(shard_shape)
        self.hr, self.hc = shard_shape[0] // 2, shard_shape[1]
        if kind == "col":
            self.full_shape = (shard_shape[0], N_CHIPS * shard_shape[1])
        else:
            self.full_shape = (N_CHIPS * shard_shape[0], shard_shape[1])

    def full_shard(self, ref, j):
        if self.kind == "col":
            return ref.at[:, pl.ds(pl.multiple_of(j * self.hc, 128), self.hc)]
        return ref.at[pl.ds(pl.multiple_of(j * 2 * self.hr, 16), 2 * self.hr), :]

    def full_half(self, ref, j, c):
        if self.kind == "col":
            return ref.at[pl.ds(pl.multiple_of(c * self.hr, 16), self.hr),
                          pl.ds(pl.multiple_of(j * self.hc, 128), self.hc)]
        return ref.at[pl.ds(pl.multiple_of((2 * j + c) * self.hr, 16), self.hr), :]

    def piece(self, ref, j, c, p0, p1, pieces, part=None):
        pr = self.hr // pieces
        off, n = p0 * pr, (p1 - p0) * pr
        if part is not None:
            top = (n // 32) * 16
            off, n = (off, top) if part == 0 else (off + top, n - top)
        if self.kind == "col":
            return ref.at[pl.ds(pl.multiple_of(c * self.hr + off, 16), n),
                          pl.ds(pl.multiple_of(j * self.hc, 128), self.hc)]
        return ref.at[pl.ds(pl.multiple_of((2 * j + c) * self.hr + off, 16), n), :]


WEIGHT_KINDS = ("col", "row", "col", "row")
NW = len(WEIGHT_KINDS)


def _comm_call(body, name, ins, outs, n_remote, aliases=None):
    return pl.pallas_call(
        body, name=name, out_shape=tuple(outs),
        in_specs=[HBM_SPEC] * len(ins), out_specs=tuple([HBM_SPEC] * len(outs)),
        input_output_aliases=aliases or {},
        scratch_shapes=[pltpu.SemaphoreType.DMA((n_remote,)), pltpu.SemaphoreType.DMA((n_remote,))],
    )(*ins)


ROW_TILES = (256, 176, 128, 64, 32, 16)


def _cast_into_full(shard, geo, place, name):
    rs, cs = shard.shape
    tr = _pick(rs, ROW_TILES)
    nb = rs // tr

    def body(place_ref, s_ref, o_ref):
        o_ref[...] = s_ref[...].astype(BF16)

    if geo.kind == "col":
        out_map = lambda i, place_ref: (i, place_ref[0])
    else:
        out_map = lambda i, place_ref: (place_ref[0] * nb + i, 0)
    return pl.pallas_call(
        body, name=name, out_shape=jax.ShapeDtypeStruct(geo.full_shape, BF16),
        grid_spec=pltpu.PrefetchScalarGridSpec(
            num_scalar_prefetch=1, grid=(nb,),
            in_specs=[pl.BlockSpec((tr, cs), lambda i, place_ref: (i, 0))],
            out_specs=pl.BlockSpec((tr, cs), out_map)),
        compiler_params=_params(1),
    )(place, shard)


def _gather_weight(full, geo, pieces, name):
    per = 8

    def body(in_ref, ref, send_sems, recv_sems):
        x, y, c, _ = _place()
        j, jx, jy, jo = 2 * x + y, 2 * (1 - x) + y, 2 * x + (1 - y), 2 * (1 - x) + (1 - y)
        nx, ny, sib = (1 - x, y, c), (x, 1 - y, c), (x, y, 1 - c)

        def cp(region, k, to):
            return _remote(region, region, send_sems, recv_sems, k, to)

        def reg(chip, p, part=None, core=c):
            return geo.piece(ref, chip, core, p, p + 1, pieces, part)

        sent = []
        for p in range(pieces):
            sent += [cp(reg(j, p), per * p, nx), cp(reg(j, p), per * p + 1, ny)]
        for d in sent:
            d.start()
        for p in range(pieces):
            cp(reg(jx, p), per * p, nx).wait_recv()
            new = [cp(reg(jx, p, 0), per * p + 2, ny), cp(reg(jx, p), per * p + 4, sib)]
            cp(reg(jy, p), per * p + 1, ny).wait_recv()
            new += [cp(reg(jy, p, 1), per * p + 3, nx), cp(reg(jy, p), per * p + 5, sib)]
            for d in new:
                d.start()
            sent += new
        for p in range(pieces):
            cp(reg(jo, p, 0), per * p + 2, ny).wait_recv()
            cp(reg(jo, p, 1), per * p + 3, nx).wait_recv()
            new = [cp(reg(jo, p, 0), per * p + 6, sib), cp(reg(jo, p, 1), per * p + 7, sib)]
            for d in new:
                d.start()
            sent += new
        for p in range(pieces):
            cp(reg(jx, p, None, 1 - c), per * p + 4, sib).wait_recv()
            cp(reg(jy, p, None, 1 - c), per * p + 5, sib).wait_recv()
            cp(reg(jo, p, 0, 1 - c), per * p + 6, sib).wait_recv()
            cp(reg(jo, p, 1, 1 - c), per * p + 7, sib).wait_recv()
        for d in sent:
            d.wait_send()

    return _comm_call(body, name, [full], [_same(full)], per * pieces, aliases={0: 0})[0]


def _same(a):
    return jax.ShapeDtypeStruct(a.shape, a.dtype)


def _gather_ici(full, geo, p0, p1, pieces):
    def copies(ins, outs):
        x, y, c, _ = _place()
        mine = geo.piece(outs[0], 2 * x + y, c, p0, p1, pieces)
        return [(mine, mine, geo.piece(outs[0], 2 * ox + oy, c, p0, p1, pieces), (ox, oy, c))
                for ox, oy in ((1 - x, y), (x, 1 - y))]

    return dict(ins=[full], outs=[_same(full)], aliases={0: 0}, n=2, copies=copies)


def _gather_relay(full, geo, p0, p1, pieces):
    def copies(ins, outs):
        x, y, c, _ = _place()
        jx, jy, jo = 2 * (1 - x) + y, 2 * x + (1 - y), 2 * (1 - x) + (1 - y)
        reg = lambda chip, part: geo.piece(outs[0], chip, c, p0, p1, pieces, part)
        return [(reg(jx, 0), reg(jx, 0), reg(jo, 0), (x, 1 - y, c)), (reg(jy, 1), reg(jy, 1), reg(jo, 1), (1 - x, y, c))]

    return dict(ins=[full], outs=[_same(full)], aliases={0: 0}, n=2, copies=copies)


def _gather_d2d(full, geo):
    def copies(ins, outs):
        x, y, c, chips = _place()
        return [(geo.full_half(outs[0], 2 * ox + oy, c), geo.full_half(outs[0], 2 * ox + oy, c),
                 geo.full_half(outs[0], 2 * ox + oy, 1 - c), (x, y, 1 - c)) for ox, oy in chips]

    return dict(ins=[full], outs=[_same(full)], aliases={0: 0}, n=3, copies=copies)


def _swap_d2d(grad, geo):
    def copies(ins, outs):
        x, y, c, _ = _place()
        return [(geo.full_half(ins[0], j, 1 - c), outs[0].at[j], outs[0].at[j], (x, y, 1 - c)) for j in range(N_CHIPS)]

    return dict(ins=[grad], outs=[jax.ShapeDtypeStruct((N_CHIPS, geo.hr, geo.hc), BF16)], aliases={}, n=N_CHIPS,
                copies=copies)


def _scatter_ici(pair, recv, geo, p0, p1, pieces):
    pr = geo.hr // pieces
    rows = pl.ds(p0 * pr, (p1 - p0) * pr)

    def copies(ins, outs):
        x, y, c, chips = _place()
        return [(ins[0].at[2 * ox + oy, rows, :], outs[0].at[k, rows, :], outs[0].at[k, rows, :], (ox, oy, c))
                for k, (ox, oy) in enumerate(chips)]

    out = jax.ShapeDtypeStruct((3, geo.hr, geo.hc), BF16)
    if recv is None:
        return dict(ins=[pair], outs=[out], aliases={}, n=3, copies=copies)
    return dict(ins=[pair, recv], outs=[out], aliases={1: 0}, n=3, copies=copies)


def _join_d2d(shard, geo, row0=0):
    def copies(ins, outs):
        x, y, c, _ = _place()
        mine = outs[0].at[pl.ds(pl.multiple_of(row0 + c * geo.hr, 8), geo.hr), :]
        other = outs[0].at[pl.ds(pl.multiple_of(row0 + (1 - c) * geo.hr, 8), geo.hr), :]
        return [(mine, mine, other, (x, y, 1 - c))]

    return dict(ins=[shard], outs=[_same(shard)], aliases={0: 0}, n=1, copies=copies)


def _add_pair(grad, got, geo, place, name):
    tr = _pick(geo.hr, ROW_TILES)
    nb = geo.hr // tr

    def body(place_ref, a_ref, b_ref, o_ref):
        o_ref[0] = (a_ref[...].astype(F32) + b_ref[0].astype(F32)).astype(BF16)

    if geo.kind == "col":
        own_map = lambda j, i, place_ref: (place_ref[1] * nb + i, j)
    else:
        own_map = lambda j, i, place_ref: ((2 * j + place_ref[1]) * nb + i, 0)
    spec = pl.BlockSpec((1, tr, geo.hc), lambda j, i, place_ref: (j, i, 0))
    return pl.pallas_call(
        body, name=name, out_shape=jax.ShapeDtypeStruct((N_CHIPS, geo.hr, geo.hc), BF16),
        grid_spec=pltpu.PrefetchScalarGridSpec(
            num_scalar_prefetch=1, grid=(N_CHIPS, nb),
            in_specs=[pl.BlockSpec((tr, geo.hc), own_map), spec], out_specs=spec),
        compiler_params=_params(2),
    )(place, grad, got)


def _add_four(pair, recv, geo, place, name, rows=None, row0=0, into=None):
    tr = _pick(geo.hr, ROW_TILES)
    nb = geo.hr // tr
    rows = 2 * geo.hr if rows is None else rows

    def body(place_ref, p_ref, r_ref, *rest):
        rest[-1][...] = ((p_ref[0].astype(F32) + r_ref[0].astype(F32)) + r_ref[1].astype(F32)) + r_ref[2].astype(F32)

    in_specs = [pl.BlockSpec((1, tr, geo.hc), lambda i, place_ref: (place_ref[0], i, 0)),
                pl.BlockSpec((3, tr, geo.hc), lambda i, place_ref: (0, i, 0))]
    args = [place, pair, recv]
    if into is not None:
        in_specs.append(pl.BlockSpec(memory_space=pl.ANY))
        args.append(into)
    return pl.pallas_call(
        body, name=name, out_shape=jax.ShapeDtypeStruct((rows, geo.hc), F32),
        grid_spec=pltpu.PrefetchScalarGridSpec(
            num_scalar_prefetch=1, grid=(nb,), in_specs=in_specs,
            out_specs=pl.BlockSpec((tr, geo.hc), lambda i, place_ref: (row0 // tr + place_ref[1] * nb + i, 0))),
        input_output_aliases={3: 0} if into is not None else {},
        compiler_params=_params(1),
    )(*args)


PIECES = (4, 1, 16, 8) + (1,) * IN_CHUNKS
SCHEDULE = {
    "proj_fwd": (("gather_ici", W_OUT, 0, 1), ("gather_ici", W_UP, 0, 6)),
    "hgrn_fwd": (("gather_relay", W_OUT, 0, 1), ("gather_relay", W_UP, 0, 6), ("gather_ici", W_UP, 6, 12)),
    "attn_fwd_d1": (("gather_relay", W_UP, 6, 12),),
    "attn_fwd_d4": (("gather_ici", W_UP, 12, 16), ("gather_d2d", W_OUT)),
    "attn_fwd_d16": (("gather_relay", W_UP, 12, 16), ("gather_ici", W_DOWN, 0, 2)),
    "mix_fwd": (("gather_d2d", W_UP), ("gather_ici", W_DOWN, 2, 4)),
    "up_fwd": (("gather_ici", W_DOWN, 4, 8), ("gather_relay", W_DOWN, 0, 4)),
    "conv_gate_fwd": (("gather_relay", W_DOWN, 4, 8),),
    "before_down_fwd": (("gather_d2d", W_DOWN),),
    "down_bwd_x": (("swap", W_DOWN),),
    "conv_gate_bwd": (("scatter", W_DOWN, 0, 4),),
    "up_bwd_w": (("scatter", W_DOWN, 4, 8),),
    "up_bwd_x": (("swap", W_UP),),
    "norm2_bwd": (("scatter", W_UP, 0, 1),),
    "mix_bwd_w": (("scatter", W_UP, 1, 2),),
    "mix_bwd_x": (("swap", W_OUT), ("scatter", W_UP, 2, 3)),
    "hgrn_bwd": (("scatter", W_UP, 3, 9), ("join", W_DOWN)),
    "attn_bwd_d1": (("scatter", W_UP, 9, 12),),
    "attn_bwd_d4": (("scatter", W_OUT, 0, 1),),
    "attn_bwd_d16": (("scatter", W_UP, 12, 16),),
    "proj_bwd_w_0": (("join", W_UP), ("join", W_OUT)),
    "proj_bwd_w_1": (("swap", W_INQ),),
    "proj_bwd_w_2": (("swap", W_INQ + 1),),
    "proj_bwd_w_3": (("swap", W_INQ + 2), ("scatter", W_INQ, 0, 1)),
    "proj_bwd_x": (("swap", W_INQ + 3), ("scatter", W_INQ + 1, 0, 1), ("scatter", W_INQ + 2, 0, 1)),
    "gather_stats": (("scatter", W_INQ + 3, 0, 1), ("join", W_INQ), ("join", W_INQ + 1), ("join", W_INQ + 2)),
    "grad_in_join": (("join", W_INQ + 3),),
}


class _Net:
    def __init__(self, shards, place):
        self.place = place
        self.geos = [_Geo(k, s.shape) for k, s in zip(WEIGHT_KINDS, shards)]
        self.full = [_cast_into_full(s, g, place, f"cast_shard_{w}") for w, (s, g) in enumerate(zip(shards, self.geos))]
        self.full[W_IN] = _gather_weight(self.full[W_IN], self.geos[W_IN], PIECES[W_IN], "gather_w_in")
        rows, cols = shards[W_IN].shape
        self.geos += [_Geo("col", (rows // IN_CHUNKS, cols))] * IN_CHUNKS
        n = NW + IN_CHUNKS
        self.grad, self.pair, self.recv, self.shard = [None] * n, [None] * n, [None] * n, [None] * n
        self.in_rows, self.in_shard = rows, None
        self.when_joined, self.joined = {}, {}
        self.slots = []

    def _row0(self, w):
        return (w - W_INQ) * 2 * self.geos[w].hr

    def host(self, name):
        phase = _Phase()
        self.slots = []
        groups = {}
        for item in SCHEDULE.get(name, ()):
            kind, w = item[0], item[1]
            geo = self.geos[w]
            key = len(groups)
            if kind == "gather_ici":
                spec, key = _gather_ici(self.full[w], geo, item[2], item[3], PIECES[w]), ("full", w)
            elif kind == "gather_relay":
                spec, key = _gather_relay(self.full[w], geo, item[2], item[3], PIECES[w]), ("full", w)
            elif kind == "gather_d2d":
                spec, key = _gather_d2d(self.full[w], geo), ("full", w)
            elif kind == "swap":
                spec = _swap_d2d(self.grad[w], geo)
            elif kind == "scatter":
                spec, key = _scatter_ici(self.pair[w], self.recv[w], geo, item[2], item[3], PIECES[w]), ("recv", w)
            elif w >= W_INQ:
                spec, key = _join_d2d(self.in_shard, geo, self._row0(w)), "in_shard"
            else:
                spec = _join_d2d(self.shard[w], geo)
            groups.setdefault(key, []).append((item, spec))
        for group in groups.values():
            specs = [s for _, s in group]
            merged = dict(specs[0], n=sum(s["n"] for s in specs),
                          copies=lambda ins, outs, specs=specs: [t for s in specs for t in s["copies"](ins, outs)])
            self.slots.append(([item for item, _ in group], phase.add(**merged)))
        return phase

    def done(self, name, comm):
        for items, sl in sorted(self.slots, key=lambda s: s[0][0][0] == "scatter"):
            out = comm[sl][0]
            for item in items:
                kind, w = item[0], item[1]
                if kind in ("gather_ici", "gather_relay", "gather_d2d"):
                    self.full[w] = out
                elif kind == "swap":
                    self.pair[w] = _add_pair(self.grad[w], out, self.geos[w], self.place, f"grad_pair_sum_{w}")
                elif kind == "scatter":
                    self.recv[w] = out
                    if item[3] == PIECES[w] and w >= W_INQ:
                        self.in_shard = _add_four(self.pair[w], out, self.geos[w], self.place, f"grad_chip_sum_{w}",
                                                  rows=self.in_rows, row0=self._row0(w), into=self.in_shard)
                    elif item[3] == PIECES[w]:
                        self.shard[w] = _add_four(self.pair[w], out, self.geos[w], self.place, f"grad_chip_sum_{w}")
                elif w >= W_INQ:
                    self.in_shard = out
                else:
                    self.shard[w] = out
                    if w in self.when_joined:
                        self.joined[w] = self.when_joined[w](out)

    def alone(self, name):
        self.done(name, _comm_only(name, self.host(name)))


CONVW_SHARD = 3 * DFF // N_CHIPS
COND_PAD = 8 * 896
SMALL_SIZES = (("norm1_w", D), ("lb", HW), ("hg_norm_w", DH), ("q_norm_w", DH), ("k_norm_w", DH),
               ("norm2_w", D), ("conv_b", DFF), ("conv_w", 3 * DFF), ("loss", 128))
SMALL_TOTAL = sum(n for _, n in SMALL_SIZES)
STATS_PAD = 8 * 5120


def kernel(x, c, w_ada, b_ada, norm1_w, w_in, lb_logits, hg_norm_w, q_norm_w, k_norm_w, w_out, norm2_w, w_up, conv_w, conv_b, w_down, loss_target, m_w_ada, m_b_ada, m_norm1_w, m_w_in, m_lb_logits, m_hg_norm_w, m_q_norm_w, m_k_norm_w, m_w_out, m_norm2_w, m_w_up, m_conv_w, m_conv_b, m_w_down, v_w_ada, v_b_ada, v_norm1_w, v_w_in, v_lb_logits, v_hg_norm_w, v_q_norm_w, v_k_norm_w, v_w_out, v_norm2_w, v_w_up, v_conv_w, v_conv_b, v_w_down):
    chip = 2 * lax.axis_index("x") + lax.axis_index("y")
    dev = 2 * chip + lax.axis_index("c")

    cond = jnp.concatenate([c, conv_w[0].reshape(1, CONVW_SHARD), jnp.zeros((1, COND_PAD - D - CONVW_SHARD), F32)], axis=1)
    cond_all = _all_gather_rows(cond.reshape(8, COND_PAD // 8), "gather_cond").reshape(N_DEV, COND_PAD)
    c_all = cond_all[:, :D]
    conv_w_full = jnp.concatenate(
        [cond_all[2 * j, D:D + CONVW_SHARD].reshape(3, DFF // N_CHIPS) for j in range(N_CHIPS)], axis=1)

    b_shard = lax.dynamic_slice_in_dim(b_ada, chip * ADA_COLS, ADA_COLS, axis=1)
    mod_cols = _all_gather_rows(_ada_fwd(c_all, w_ada[0], b_shard), "gather_mod")
    mod_all = jnp.concatenate([mod_cols[16 * j:16 * j + 8] for j in range(N_CHIPS)], axis=1)
    mod = lax.dynamic_slice_in_dim(mod_all, dev, 1, axis=0)

    place = jnp.stack([chip, lax.axis_index("c")]).astype(jnp.int32)
    net = _Net([w_in[0], w_out[0], w_up[0], w_down[0]], place)
    net.when_joined = {
        W_OUT: lambda grad: _adamw_sc(w_out[0], grad, m_w_out[0], v_w_out[0], "adamw_sc_w_out"),
        W_DOWN: lambda grad: _adamw_sc(w_down[0], grad, m_w_down[0], v_w_down[0], "adamw_sc_w_down"),
        W_UP: lambda grad: _adamw_sc(w_up[0], grad, m_w_up[0], v_w_up[0], "adamw_sc_w_up"),
    }
    early = {W_OUT: "w_out", W_DOWN: "w_down", W_UP: "w_up"}
    loss_row, grad_x, dmod, small = _sequence_step(
        x[0], loss_target[0], mod, norm1_w, lb_logits, hg_norm_w, q_norm_w, k_norm_w, norm2_w, conv_w_full, conv_b, net)
    small["conv_w"] = small["conv_w"].reshape(1, 3 * DFF)
    small["loss"] = loss_row
    stats = jnp.concatenate([dmod] + [small[k] for k, _ in SMALL_SIZES]
                            + [jnp.zeros((1, STATS_PAD - 6 * D - SMALL_TOTAL), F32)], axis=1)
    stats_all, comm = _all_gather_rows(stats.reshape(8, STATS_PAD // 8), "gather_stats", net.host("gather_stats"))
    net.done("gather_stats", comm)
    stats_all = stats_all.reshape(N_DEV, STATS_PAD)
    net.alone("grad_in_join")
    g_out, g_up, g_down = net.shard[W_OUT], net.shard[W_UP], net.shard[W_DOWN]
    g_in = net.in_shard
    dmod_all = stats_all[:, :6 * D]
    sums = _sum_rows(stats_all[:, 6 * D:6 * D + SMALL_TOTAL], "small_grad_sum")
    g_small, off = {}, 0
    for k, n in SMALL_SIZES:
        g_small[k] = sums[:, off:off + n]
        off += n
    loss = g_small["loss"][0, 0]
    g_b_ada = _sum_rows(dmod_all, "b_ada_grad_sum")
    ada = _ada_bwd_adamw(c_all, lax.dynamic_slice_in_dim(dmod_all, chip * ADA_COLS, ADA_COLS, axis=1),
                         w_ada[0], m_w_ada[0], v_w_ada[0])
    g_w_ada = ada[0]
    g_lb = _lb_grad(lb_logits, g_small["lb"])
    g_conv_w = lax.dynamic_slice_in_dim(g_small["conv_w"].reshape(3, DFF), chip * (DFF // N_CHIPS), DFF // N_CHIPS, axis=1)

    names = ["w_ada", "b_ada", "norm1_w", "w_in", "lb_logits", "hg_norm_w", "q_norm_w", "k_norm_w", "w_out",
             "norm2_w", "w_up", "conv_w", "conv_b", "w_down"]
    lead = {"w_ada", "w_in", "w_out", "w_up", "conv_w", "w_down"}
    w = dict(w_ada=w_ada, b_ada=b_ada, norm1_w=norm1_w, w_in=w_in, lb_logits=lb_logits, hg_norm_w=hg_norm_w,
             q_norm_w=q_norm_w, k_norm_w=k_norm_w, w_out=w_out, norm2_w=norm2_w, w_up=w_up, conv_w=conv_w,
             conv_b=conv_b, w_down=w_down)
    m = dict(w_ada=m_w_ada, b_ada=m_b_ada, norm1_w=m_norm1_w, w_in=m_w_in, lb_logits=m_lb_logits,
             hg_norm_w=m_hg_norm_w, q_norm_w=m_q_norm_w, k_norm_w=m_k_norm_w, w_out=m_w_out, norm2_w=m_norm2_w,
             w_up=m_w_up, conv_w=m_conv_w, conv_b=m_conv_b, w_down=m_w_down)
    v = dict(w_ada=v_w_ada, b_ada=v_b_ada, norm1_w=v_norm1_w, w_in=v_w_in, lb_logits=v_lb_logits,
             hg_norm_w=v_hg_norm_w, q_norm_w=v_q_norm_w, k_norm_w=v_k_norm_w, w_out=v_w_out, norm2_w=v_norm2_w,
             w_up=v_w_up, conv_w=v_conv_w, conv_b=v_conv_b, w_down=v_w_down)
    g = dict(w_ada=g_w_ada, b_ada=g_b_ada, norm1_w=g_small["norm1_w"], w_in=g_in, lb_logits=g_lb,
             hg_norm_w=g_small["hg_norm_w"], q_norm_w=g_small["q_norm_w"], k_norm_w=g_small["k_norm_w"], w_out=g_out,
             norm2_w=g_small["norm2_w"], w_up=g_up, conv_w=g_conv_w, conv_b=g_small["conv_b"], w_down=g_down)

    updated = {early[i]: res for i, res in net.joined.items()}
    updated["w_ada"] = (ada[1], ada[2], ada[3], ada[0])
    grads, deltas, new_m, new_v = [], [], [], []
    for n in names:
        strip = (lambda t: t[0]) if n in lead else (lambda t: t)
        wrap = (lambda t: t[None]) if n in lead else (lambda t: t)
        g_n = g[n]
        if n in updated:
            d_n, m_n, v_n, g_n = updated[n]
        elif n == "w_in":
            d_n, m_n, v_n, g_n = _adamw(strip(w[n]), g_n, strip(m[n]), strip(v[n]), f"adamw_{n}", copy_grad=True)
        else:
            d_n, m_n, v_n = _adamw(strip(w[n]), g_n, strip(m[n]), strip(v[n]), f"adamw_{n}")
        grads.append(wrap(g_n))
        deltas.append(wrap(d_n))
        new_m.append(wrap(m_n))
        new_v.append(wrap(v_n))
    return (loss, grad_x[None], *grads, *deltas, *new_m, *new_v)
```

```python
import functools
import math

import jax
import jax.numpy as jnp
from jax import lax
from jax.experimental import pallas as pl
from jax.experimental.pallas import tpu as pltpu
from jax.experimental.pallas import tpu_sc as plsc

F32 = jnp.float32
BF16 = jnp.bfloat16

S = 2048
D = 2048
H = 8
DH = 128
HW = H * DH
CH = 64
NCH = S // CH
DFF = 5632
INC = 7 * HW
BLK = 128
DILS = (1, 4, 16)
EPS = 1e-6
NEG = -1e30
N_CHIPS = 4
N_DEV = 8

ADAM_LR = 0.001
ADAM_B1 = 0.9
ADAM_B2 = 0.999
ADAM_EPS = 1e-08
ADAM_WD = 0.01
ADAM_STEP = 10
ADAM_BLOCK_BYTES = 1 << 21

V7X_VMEM_BYTES = 64 * 1024 * 1024
VMEM_LIMIT = (V7X_VMEM_BYTES * 3) // 4
MESH = pl.DeviceIdType.MESH
HBM_SPEC = pl.BlockSpec(memory_space=pltpu.HBM)
VMEM_SPEC = pl.BlockSpec(memory_space=pltpu.VMEM)

NN = (((1,), (0,)), ((), ()))
NT = (((1,), (1,)), ((), ()))
TN = (((0,), (0,)), ((), ()))


def _params(n_grid):
    return pltpu.CompilerParams(dimension_semantics=("arbitrary",) * n_grid, vmem_limit_bytes=VMEM_LIMIT)


def _dot(a, b, dims=NN):
    return lax.dot_general(a.astype(BF16), b.astype(BF16), dims, preferred_element_type=F32)


def _dot_f32(a, b):
    return lax.dot_general(a, b, NN, precision=lax.Precision.HIGHEST, preferred_element_type=F32)


def _sigmoid(x):
    return 1.0 / (1.0 + jnp.exp(-x))


def _pick(n, cands):
    for t in cands:
        if n % t == 0:
            return t
    raise ValueError(n)


def _matmul(a, b, mode, out_dtype, name, phase=None, m_blocks=None):
    if mode == "nn":
        (m, k), (_, n) = a.shape, b.shape
    elif mode == "nt":
        (m, k), (n, _) = a.shape, b.shape
    else:
        (k, m), (_, n) = a.shape, b.shape
    tm = _pick(m, (1024, 1408, 512))
    a_block = lambda i: i
    if m_blocks is not None:
        tm, blocks = m_blocks
        m = tm * len(blocks)
        step = blocks[1] - blocks[0] if len(blocks) > 1 else 0
        assert all(blk == blocks[0] + step * i for i, blk in enumerate(blocks))
        a_block = lambda i: blocks[0] + step * i
    tn = _pick(n, (1024, 1408, 512))
    tk = _pick(k, (2048, 2816, 1792, 1024, 512))
    nk = k // tk
    dims = {"nn": NN, "nt": NT, "tn": TN}[mode]

    def body(a_ref, b_ref, o_ref, *acc):
        part = lax.dot_general(a_ref[...], b_ref[...], dims, preferred_element_type=F32)
        if nk == 1:
            o_ref[...] = part.astype(o_ref.dtype)
            return
        acc_ref, kk = acc[0], pl.program_id(2)

        @pl.when(kk == 0)
        def _():
            acc_ref[...] = part

        @pl.when(jnp.logical_and(kk > 0, kk < nk - 1))
        def _():
            acc_ref[...] += part

        @pl.when(kk == nk - 1)
        def _():
            o_ref[...] = (acc_ref[...] + part).astype(o_ref.dtype)

    if mode == "tn":
        a_spec = pl.BlockSpec((tk, tm), lambda i, j, kk: (kk, a_block(i)))
    else:
        a_spec = pl.BlockSpec((tm, tk), lambda i, j, kk: (i, kk))
    if mode == "nt":
        b_spec = pl.BlockSpec((tn, tk), lambda i, j, kk: (j, kk))
    else:
        b_spec = pl.BlockSpec((tk, tn), lambda i, j, kk: (kk, j))
    return _pcall(
        body, [a, b], phase, name=name,
        out_shape=jax.ShapeDtypeStruct((m, n), out_dtype),
        grid=(m // tm, n // tn, nk),
        in_specs=[a_spec, b_spec],
        out_specs=pl.BlockSpec((tm, tn), lambda i, j, kk: (i, j)),
        scratch_shapes=[pltpu.VMEM((tm, tn), F32)] if nk > 1 else [],
        compiler_params=_params(3),
    )


TR = 256


def _row_spec(cols=D):
    return pl.BlockSpec((TR, cols), lambda i: (i, 0))


def _vec_spec(cols=D):
    return pl.BlockSpec((1, cols), lambda i: (0, 0))


def _norm_mod(x, nw, scale, shift, name):
    def body(x_ref, nw_ref, sc_ref, sh_ref, h_ref):
        xv = x_ref[...]
        r = lax.rsqrt(jnp.mean(xv * xv, axis=-1, keepdims=True) + EPS)
        h_ref[...] = ((xv * r) * nw_ref[...] * (1.0 + sc_ref[...]) + sh_ref[...]).astype(BF16)

    return pl.pallas_call(
        body, name=name, out_shape=jax.ShapeDtypeStruct((S, D), BF16), grid=(S // TR,),
        in_specs=[_row_spec(), _vec_spec(), _vec_spec(), _vec_spec()], out_specs=_row_spec(),
        compiler_params=_params(1),
    )(x, nw, scale, shift)


def _resid_norm_mod(x, mix, gate, nw, scale, shift, name):
    def body(x_ref, mix_ref, g_ref, nw_ref, sc_ref, sh_ref, x1_ref, h_ref):
        xv = x_ref[...] + g_ref[...] * mix_ref[...]
        x1_ref[...] = xv
        r = lax.rsqrt(jnp.mean(xv * xv, axis=-1, keepdims=True) + EPS)
        h_ref[...] = ((xv * r) * nw_ref[...] * (1.0 + sc_ref[...]) + sh_ref[...]).astype(BF16)

    return pl.pallas_call(
        body, name=name,
        out_shape=(jax.ShapeDtypeStruct((S, D), F32), jax.ShapeDtypeStruct((S, D), BF16)), grid=(S // TR,),
        in_specs=[_row_spec(), _row_spec(), _vec_spec(), _vec_spec(), _vec_spec(), _vec_spec()],
        out_specs=(_row_spec(), _row_spec()),
        compiler_params=_params(1),
    )(x, mix, gate, nw, scale, shift)


def _norm_mod_bwd(dh, xin, nw, scale, dres, name, mix=None, gate=None, phase=None):
    with_gate = mix is not None

    def body(*refs):
        if with_gate:
            dh_ref, x_ref, nw_ref, sc_ref, dres_ref, mix_ref, g_ref, dx_ref, dsh_ref, dsc_ref, dnw_ref, dg_ref, dmix_ref = refs
        else:
            dh_ref, x_ref, nw_ref, sc_ref, dres_ref, dx_ref, dsh_ref, dsc_ref, dnw_ref = refs
        i = pl.program_id(0)
        dhv = dh_ref[...]
        xv = x_ref[...]
        r = lax.rsqrt(jnp.mean(xv * xv, axis=-1, keepdims=True) + EPS)
        xn = xv * r
        nwv = nw_ref[...]
        one_sc = 1.0 + sc_ref[...]
        dxn = dhv * nwv * one_sc
        dx = dres_ref[...] + r * (dxn - xn * jnp.mean(dxn * xn, axis=-1, keepdims=True))
        dx_ref[...] = dx

        @pl.when(i == 0)
        def _():
            dsh_ref[...] = jnp.zeros_like(dsh_ref)
            dsc_ref[...] = jnp.zeros_like(dsc_ref)
            dnw_ref[...] = jnp.zeros_like(dnw_ref)
            if with_gate:
                dg_ref[...] = jnp.zeros_like(dg_ref)

        dsh_ref[...] += jnp.sum(dhv, axis=0, keepdims=True)
        dsc_ref[...] += jnp.sum(dhv * xn * nwv, axis=0, keepdims=True)
        dnw_ref[...] += jnp.sum(dhv * xn * one_sc, axis=0, keepdims=True)
        if with_gate:
            dg_ref[...] += jnp.sum(dx * mix_ref[...], axis=0, keepdims=True)
            dmix_ref[...] = (dx * g_ref[...]).astype(BF16)

    vec = jax.ShapeDtypeStruct((1, D), F32)
    ins = [dh, xin, nw, scale, dres]
    in_specs = [_row_spec(), _row_spec(), _vec_spec(), _vec_spec(), _row_spec()]
    outs = [jax.ShapeDtypeStruct((S, D), F32), vec, vec, vec]
    out_specs = [_row_spec(), _vec_spec(), _vec_spec(), _vec_spec()]
    if with_gate:
        ins += [mix, gate]
        in_specs += [_row_spec(), _vec_spec()]
        outs += [vec, jax.ShapeDtypeStruct((S, D), BF16)]
        out_specs += [_vec_spec(), _row_spec()]
    return _pcall(
        body, ins, phase, name=name, out_shape=tuple(outs), grid=(S // TR,), in_specs=in_specs,
        out_specs=tuple(out_specs), compiler_params=_params(1),
    )


def _tri(lower):
    row = lax.broadcasted_iota(jnp.int32, (CH, CH), 0)
    col = lax.broadcasted_iota(jnp.int32, (CH, CH), 1)
    return (row >= col) if lower else (col >= row)


def _hgrn_gates(hq, hf, lb):
    sig = _sigmoid(hf)
    f = lb + (1.0 - lb) * sig
    g = jnp.log(f)
    sq = _sigmoid(hq)
    return sig, f, g, 1.0 - f, hq * sq, sq


HG_HP = 2
HG_UNROLL = 8


def _proj_col_spec(group):
    return pl.BlockSpec((S, HG_HP * DH), lambda h: (0, group * (H // HG_HP) + h))


def _hgrn_fwd(proj, lb_logits, norm_w, phase=None):
    def body(hq_ref, hf_ref, hi_ref, hg_ref, lbl_ref, nw_ref, out_ref, oraw_ref, st_ref, s_ref):
        nw = nw_ref[...]
        lower = _tri(True)
        ltri = lower.astype(F32)
        s_ref[...] = jnp.zeros_like(s_ref)

        def chunk(n, carry):
            rows = pl.ds(pl.multiple_of(n * CH, CH), CH)
            for j in range(HG_HP):
                cols = slice(j * DH, (j + 1) * DH)
                lb = 1.0 / (1.0 + jnp.exp(lbl_ref[1:2, cols] - lbl_ref[0:1, cols]))
                hq, hf, v, hg = hq_ref[rows, cols], hf_ref[rows, cols], hi_ref[rows, cols], hg_ref[rows, cols]
                _, _, g, kk, q, _ = _hgrn_gates(hq, hf, lb)
                gc = _dot_f32(ltri, g)
                gl = jnp.sum(g, axis=0, keepdims=True)
                qe = q * jnp.exp(gc)
                ke = kk * jnp.exp(gl - gc)
                qh = q * jnp.exp(gc - 0.5 * gl)
                kh = kk * jnp.exp(0.5 * gl - gc)
                st = s_ref[j]
                st_ref[j, pl.ds(n, 1)] = st[None]
                a = jnp.where(lower, _dot(qh, kh, NT), 0.0)
                o = _dot(qe, st, NT) + _dot(a, v)
                s_ref[j] = st * jnp.exp(gl) + _dot(v, ke, TN)
                oraw_ref[rows, cols] = o
                rs = lax.rsqrt(jnp.mean(o * o, axis=-1, keepdims=True) + EPS)
                out_ref[rows, cols] = (o * rs * nw * (hg * _sigmoid(hg))).astype(BF16)
            return carry

        lax.fori_loop(0, NCH, chunk, 0, unroll=HG_UNROLL)

    head_spec = pl.BlockSpec((S, HG_HP * DH), lambda h: (0, h))
    return _pcall(
        body, [proj, proj, proj, proj, lb_logits, norm_w], phase, name="hgrn_fwd",
        out_shape=(jax.ShapeDtypeStruct((S, 2 * HW), BF16), jax.ShapeDtypeStruct((S, HW), F32),
                   jax.ShapeDtypeStruct((H, NCH, DH, DH), F32)),
        grid=(H // HG_HP,),
        in_specs=[_proj_col_spec(0), _proj_col_spec(1), _proj_col_spec(2), _proj_col_spec(3),
                  pl.BlockSpec((2, HG_HP * DH), lambda h: (0, h)), pl.BlockSpec((1, DH), lambda h: (0, 0))],
        out_specs=(head_spec, head_spec, pl.BlockSpec((HG_HP, NCH, DH, DH), lambda h: (h, 0, 0, 0))),
        scratch_shapes=[pltpu.VMEM((HG_HP, DH, DH), F32)],
        compiler_params=_params(1),
    )


def _hgrn_bwd(proj, lb_logits, norm_w, oraw, states, dout, phase=None):
    def body(hq_ref, hf_ref, hi_ref, hg_ref, lbl_ref, nw_ref, oraw_ref, st_ref, do_ref,
             dhq_ref, dhf_ref, dhi_ref, dhg_ref, dlb_ref, dnw_ref, ds_ref, acc_ref):
        h = pl.program_id(0)
        nw = nw_ref[...]
        lower = _tri(True)
        ltri = lower.astype(F32)
        utri = _tri(False).astype(F32)
        last = lax.broadcasted_iota(jnp.int32, (CH, DH), 0) == CH - 1
        ds_ref[...] = jnp.zeros_like(ds_ref)
        acc_ref[...] = jnp.zeros_like(acc_ref)

        @pl.when(h == 0)
        def _():
            dnw_ref[...] = jnp.zeros_like(dnw_ref)

        def chunk(i, carry):
            n = NCH - 1 - i
            rows = pl.ds(pl.multiple_of(n * CH, CH), CH)
            for j in range(HG_HP):
                cols = slice(j * DH, (j + 1) * DH)
                lb = 1.0 / (1.0 + jnp.exp(lbl_ref[1:2, cols] - lbl_ref[0:1, cols]))
                hq, hf, v, hg = hq_ref[rows, cols], hf_ref[rows, cols], hi_ref[rows, cols], hg_ref[rows, cols]
                sig, f, g, kk, q, sq = _hgrn_gates(hq, hf, lb)
                gc = _dot_f32(ltri, g)
                gl = jnp.sum(g, axis=0, keepdims=True)
                eg = jnp.exp(gc)
                ek = jnp.exp(gl - gc)
                gam = jnp.exp(gl)
                ph = jnp.exp(gc - 0.5 * gl)
                pk = jnp.exp(0.5 * gl - gc)
                qe, ke = q * eg, kk * ek
                qh, kh = q * ph, kk * pk
                st = st_ref[j, pl.ds(n, 1)][0]
                dst = ds_ref[j]
                a = jnp.where(lower, _dot(qh, kh, NT), 0.0)
                o = oraw_ref[rows, cols]
                rs = lax.rsqrt(jnp.mean(o * o, axis=-1, keepdims=True) + EPS)
                xh = o * rs
                sg = _sigmoid(hg)
                dgo = do_ref[rows, cols]
                d_on = dgo * (hg * sg)
                dhg_ref[rows, cols] = (dgo * xh * nw * (sg * (1.0 + hg * (1.0 - sg)))).astype(BF16)
                acc_ref[j, 1:2, :] += jnp.sum(d_on * xh, axis=0, keepdims=True)
                dy = d_on * nw
                do = rs * (dy - xh * jnp.mean(dy * xh, axis=-1, keepdims=True))
                dqe = _dot(do, st)
                da = jnp.where(lower, _dot(do, v, NT), 0.0)
                dv = _dot(a, do, TN) + _dot(ke, dst, NT)
                dke = _dot(v, dst)
                dgam = jnp.sum(dst * st, axis=0, keepdims=True)
                dqh = lax.dot_general(da, kh, NN, precision=lax.Precision.HIGH, preferred_element_type=F32)
                dkh = lax.dot_general(da, qh, TN, precision=lax.Precision.HIGH, preferred_element_type=F32)
                dq = dqh * ph + dqe * eg
                dk = dkh * pk + dke * ek
                dgl = jnp.sum(dke * ke, axis=0, keepdims=True) + dgam * gam
                dgc = dqh * qh - dkh * kh + dqe * qe - dke * ke + jnp.where(last, dgl, 0.0)
                dg = _dot_f32(utri, dgc)
                df = dg / f - dk
                dhf_ref[rows, cols] = (df * (1.0 - lb) * sig * (1.0 - sig)).astype(BF16)
                acc_ref[j, 0:1, :] += jnp.sum(df * (1.0 - sig), axis=0, keepdims=True)
                dhq_ref[rows, cols] = (dq * (sq * (1.0 + hq * (1.0 - sq)))).astype(BF16)
                dhi_ref[rows, cols] = dv.astype(BF16)
                ds_ref[j] = dst * gam + _dot(do, qe, TN)
            return carry

        lax.fori_loop(0, NCH, chunk, 0, unroll=HG_UNROLL)
        for j in range(HG_HP):
            dlb_ref[:, j * DH:(j + 1) * DH] = acc_ref[j, 0:1, :]
            dnw_ref[...] += acc_ref[j, 1:2, :]

    head_spec = pl.BlockSpec((S, HG_HP * DH), lambda h: (0, h))
    head_out = jax.ShapeDtypeStruct((S, HW), BF16)
    return _pcall(
        body, [proj, proj, proj, proj, lb_logits, norm_w, oraw, states, dout], phase, name="hgrn_bwd",
        out_shape=(head_out, head_out, head_out, head_out,
                   jax.ShapeDtypeStruct((1, HW), F32), jax.ShapeDtypeStruct((1, DH), F32)),
        grid=(H // HG_HP,),
        in_specs=[_proj_col_spec(0), _proj_col_spec(1), _proj_col_spec(2), _proj_col_spec(3),
                  pl.BlockSpec((2, HG_HP * DH), lambda h: (0, h)), pl.BlockSpec((1, DH), lambda h: (0, 0)),
                  head_spec, pl.BlockSpec((HG_HP, NCH, DH, DH), lambda h: (h, 0, 0, 0)), head_spec],
        out_specs=(head_spec, head_spec, head_spec, head_spec,
                   pl.BlockSpec((1, HG_HP * DH), lambda h: (0, h)), pl.BlockSpec((1, DH), lambda h: (0, 0))),
        scratch_shapes=[pltpu.VMEM((HG_HP, DH, DH), F32), pltpu.VMEM((HG_HP, 8, DH), F32)],
        compiler_params=_params(1),
    )


ATT_SCALE = DH ** -0.5
HEADS_PER_STEP = {1: 8, 4: 1, 16: 1}


def _sub_rows(ref, r, dil, cols):
    if dil == 1:
        return ref[:, cols]
    return ref[pl.ds(r, BLK, stride=dil), cols]


def _set_sub_rows(ref, r, dil, cols, val):
    if dil == 1:
        ref[:, cols] = val
    else:
        ref[pl.ds(r, BLK, stride=dil), cols] = val


def _slopes(dil):
    hp = HEADS_PER_STEP[dil]
    s = jnp.asarray([dil * 2.0 ** (-(h + 1)) for h in range(H)], F32)
    return jnp.broadcast_to(s[:, None], (H, DH)).reshape(H // hp, hp, DH)


def _rms_rows(x, w):
    rs = lax.rsqrt(jnp.mean(x * x, axis=-1, keepdims=True) + EPS)
    return x * rs, rs


def _att_masks():
    qi = lax.broadcasted_iota(jnp.int32, (BLK, BLK), 0)
    kj = lax.broadcasted_iota(jnp.int32, (BLK, BLK), 1)
    steps_c = qi - kj
    steps_p = qi - kj + BLK
    return steps_c, steps_p, steps_c >= 0, steps_p <= BLK


def _qk_prep(proj, q_w, k_w):
    def body(q_ref, k_ref, qw_ref, kw_ref, qn_ref, kn_ref):
        for h in range(H):
            cols = slice(h * DH, (h + 1) * DH)
            qn_ref[:, cols] = _rms_rows(q_ref[:, cols], None)[0] * qw_ref[...]
            kn_ref[:, cols] = _rms_rows(k_ref[:, cols], None)[0] * kw_ref[...]

    out = jax.ShapeDtypeStruct((S, HW), F32)
    wspec = pl.BlockSpec((1, DH), lambda i: (0, 0))
    return pl.pallas_call(
        body, name="qk_prep", out_shape=(out, out), grid=(S // TR,),
        in_specs=[pl.BlockSpec((TR, HW), lambda i: (i, 4)), pl.BlockSpec((TR, HW), lambda i: (i, 5)), wspec, wspec],
        out_specs=(_row_spec(HW), _row_spec(HW)),
        compiler_params=_params(1),
    )(proj, proj, q_w, k_w)


def _qk_norm_bwd(proj, dqn, dkn, dv, q_w, k_w):
    def body(q_ref, k_ref, dqn_ref, dkn_ref, dv_ref, qw_ref, kw_ref, dq_ref, dk_ref, dvo_ref, dqw_ref, dkw_ref):
        i = pl.program_id(0)

        @pl.when(i == 0)
        def _():
            dqw_ref[...] = jnp.zeros_like(dqw_ref)
            dkw_ref[...] = jnp.zeros_like(dkw_ref)

        dvo_ref[...] = dv_ref[...].astype(BF16)
        for x_ref, d_ref, w_ref, o_ref, dw_ref in ((q_ref, dqn_ref, qw_ref, dq_ref, dqw_ref),
                                                   (k_ref, dkn_ref, kw_ref, dk_ref, dkw_ref)):
            for h in range(H):
                cols = slice(h * DH, (h + 1) * DH)
                xh, rs = _rms_rows(x_ref[:, cols], None)
                d = d_ref[:, cols]
                dw_ref[...] += jnp.sum(d * xh, axis=0, keepdims=True)
                dy = d * w_ref[...]
                o_ref[:, cols] = (rs * (dy - xh * jnp.mean(dy * xh, axis=-1, keepdims=True))).astype(BF16)

    big = jax.ShapeDtypeStruct((S, HW), BF16)
    small = jax.ShapeDtypeStruct((1, DH), F32)
    wspec = pl.BlockSpec((1, DH), lambda i: (0, 0))
    return pl.pallas_call(
        body, name="qk_norm_bwd", out_shape=(big, big, big, small, small), grid=(S // TR,),
        in_specs=[pl.BlockSpec((TR, HW), lambda i: (i, 4)), pl.BlockSpec((TR, HW), lambda i: (i, 5)),
                  _row_spec(HW), _row_spec(HW), _row_spec(HW), wspec, wspec],
        out_specs=(_row_spec(HW), _row_spec(HW), _row_spec(HW), wspec, wspec),
        compiler_params=_params(1),
    )(proj, proj, dqn, dkn, dv, q_w, k_w)


def _attn_delta(do, o):
    def body(do_ref, o_ref, d_ref):
        for h in range(H):
            cols = slice(h * DH, (h + 1) * DH)
            d_ref[:, cols] = jnp.broadcast_to(jnp.sum(do_ref[:, cols] * o_ref[:, cols], axis=-1, keepdims=True), (TR, DH))

    return pl.pallas_call(
        body, name="attn_delta", out_shape=jax.ShapeDtypeStruct((S, HW), F32), grid=(S // TR,),
        in_specs=[pl.BlockSpec((TR, HW), lambda i: (i, 1)), _row_spec(HW)], out_specs=_row_spec(HW),
        compiler_params=_params(1),
    )(do, o)


def _attn_fwd(proj, qn, kn, dil, phase=None):
    hp = HEADS_PER_STEP[dil]
    rows, ng, nb = BLK * dil, H // hp, S // (BLK * dil)

    def body(q_ref, kc_ref, kp_ref, vc_ref, vp_ref, sl_ref, o_ref, lse_ref):
        n = pl.program_id(0)
        steps_c, steps_p, ok_c, ok_p = _att_masks()
        ok_p = jnp.logical_and(ok_p, n > 0)
        fc, fp = steps_c.astype(F32), steps_p.astype(F32)
        for hh in range(hp):
            cols = slice(hh * DH, (hh + 1) * DH)
            sl = sl_ref[0, hh:hh + 1, :]
            for r in range(dil):
                sub = lambda ref: _sub_rows(ref, r, dil, cols)
                qv = sub(q_ref)
                sc = jnp.where(ok_c, _dot(qv, sub(kc_ref), NT) * ATT_SCALE - sl * fc, NEG)
                sp = jnp.where(ok_p, _dot(qv, sub(kp_ref), NT) * ATT_SCALE - sl * fp, NEG)
                m = jnp.maximum(jnp.max(sc, axis=-1, keepdims=True), jnp.max(sp, axis=-1, keepdims=True))
                pc, pp = jnp.exp(sc - m), jnp.exp(sp - m)
                den = jnp.sum(pc, axis=-1, keepdims=True) + jnp.sum(pp, axis=-1, keepdims=True)
                o = (_dot(pc, sub(vc_ref)) + _dot(pp, sub(vp_ref))) / den
                _set_sub_rows(o_ref, r, dil, cols, o)
                _set_sub_rows(lse_ref, r, dil, cols, jnp.broadcast_to(m + jnp.log(den), (BLK, DH)))

    cur = pl.BlockSpec((rows, hp * DH), lambda n, g: (n, g))
    prev = pl.BlockSpec((rows, hp * DH), lambda n, g: (jnp.maximum(n - 1, 0), g))
    vcur = pl.BlockSpec((rows, hp * DH), lambda n, g: (n, 6 * ng + g))
    vprev = pl.BlockSpec((rows, hp * DH), lambda n, g: (jnp.maximum(n - 1, 0), 6 * ng + g))
    out = jax.ShapeDtypeStruct((S, HW), F32)
    return _pcall(
        body, [qn, kn, kn, proj, proj, _slopes(dil)], phase,
        name=f"attn_fwd_d{dil}", out_shape=(out, out), grid=(nb, ng),
        in_specs=[cur, cur, prev, vcur, vprev, pl.BlockSpec((1, hp, DH), lambda n, g: (g, 0, 0))],
        out_specs=(cur, cur),
        compiler_params=_params(2),
    )


def _attn_merge(outs, lses, cat):
    def body(o1, o2, o3, l1, l2, l3, cat_ref, ob_ref, of_ref, lse_ref):
        a, b, c = l1[...], l2[...], l3[...]
        m = jnp.maximum(jnp.maximum(a, b), c)
        ea, eb, ec = jnp.exp(a - m), jnp.exp(b - m), jnp.exp(c - m)
        den = ea + eb + ec
        o = (ea * o1[...] + eb * o2[...] + ec * o3[...]) / den
        ob_ref[...] = o.astype(BF16)
        of_ref[...] = o
        lse_ref[...] = m + jnp.log(den)

    f = jax.ShapeDtypeStruct((S, HW), F32)
    return pl.pallas_call(
        body, name="attn_merge", out_shape=(jax.ShapeDtypeStruct((S, 2 * HW), BF16), f, f), grid=(S // TR,),
        in_specs=[_row_spec(HW)] * 6 + [pl.BlockSpec(memory_space=pl.ANY)],
        out_specs=(pl.BlockSpec((TR, HW), lambda i: (i, 1)), _row_spec(HW), _row_spec(HW)),
        input_output_aliases={6: 0},
        compiler_params=_params(1),
    )(*outs, *lses, cat)


def _attn_bwd(proj, qn, kn, lse, delta, do, dil, acc, phase=None):
    hp = HEADS_PER_STEP[dil]
    rows, ng, nb = BLK * dil, H // hp, S // (BLK * dil)
    has_acc = acc is not None

    def body(*refs):
        q_ref, qn_ref, kp_ref, kc_ref, vp_ref, vc_ref, l_ref, ln_ref, d_ref, dn_ref, do_ref, don_ref, sl_ref = refs[:13]
        refs = refs[13:]
        if has_acc:
            aq_ref, ak_ref, av_ref = refs[:3]
            refs = refs[3:]
        dq_ref, dk_ref, dv_ref = refs
        m = pl.program_id(0)
        steps_c, steps_p, ok_c, ok_p = _att_masks()
        ok_prev = jnp.logical_and(ok_p, m > 0)
        ok_next = jnp.logical_and(ok_p, m < nb - 1)
        fc, fp = steps_c.astype(F32), steps_p.astype(F32)
        for hh in range(hp):
            cols = slice(hh * DH, (hh + 1) * DH)
            sl = sl_ref[0, hh:hh + 1, :]
            for r in range(dil):
                sub = lambda ref: _sub_rows(ref, r, dil, cols)
                qv, qnv, kcv, kpv = sub(q_ref), sub(qn_ref), sub(kc_ref), sub(kp_ref)
                vc, vp = sub(vc_ref), sub(vp_ref)
                dov, donv = sub(do_ref), sub(don_ref)
                lse_m, lse_n = sub(l_ref)[:, 0:1], sub(ln_ref)[:, 0:1]
                delta_m, delta_n = sub(d_ref)[:, 0:1], sub(dn_ref)[:, 0:1]
                p_c = jnp.where(ok_c, jnp.exp(_dot(qv, kcv, NT) * ATT_SCALE - sl * fc - lse_m), 0.0)
                p_p = jnp.where(ok_prev, jnp.exp(_dot(qv, kpv, NT) * ATT_SCALE - sl * fp - lse_m), 0.0)
                p_n = jnp.where(ok_next, jnp.exp(_dot(qnv, kcv, NT) * ATT_SCALE - sl * fp - lse_n), 0.0)
                ds_c = p_c * (_dot(dov, vc, NT) - delta_m)
                ds_p = p_p * (_dot(dov, vp, NT) - delta_m)
                ds_n = p_n * (_dot(donv, vc, NT) - delta_n)
                dqn = (_dot(ds_c, kcv) + _dot(ds_p, kpv)) * ATT_SCALE
                dkn = (_dot(ds_c, qv, TN) + _dot(ds_n, qnv, TN)) * ATT_SCALE
                dv = _dot(p_c, dov, TN) + _dot(p_n, donv, TN)
                if has_acc:
                    dqn, dkn, dv = dqn + sub(aq_ref), dkn + sub(ak_ref), dv + sub(av_ref)
                _set_sub_rows(dq_ref, r, dil, cols, dqn)
                _set_sub_rows(dk_ref, r, dil, cols, dkn)
                _set_sub_rows(dv_ref, r, dil, cols, dv)

    def shifted(shift, m):
        if shift < 0:
            return jnp.maximum(m - 1, 0)
        if shift > 0:
            return jnp.minimum(m + 1, nb - 1)
        return m

    def spec(shift, group=0):
        return pl.BlockSpec((rows, hp * DH), lambda m, g: (shifted(shift, m), group * ng + g))

    ins = [qn, qn, kn, kn, proj, proj, lse, lse, delta, delta, do, do, _slopes(dil)]
    in_specs = [spec(0), spec(1), spec(-1), spec(0), spec(-1, 6), spec(0, 6), spec(0), spec(1), spec(0), spec(1),
                spec(0, 1), spec(1, 1), pl.BlockSpec((1, hp, DH), lambda m, g: (g, 0, 0))]
    if has_acc:
        ins += list(acc)
        in_specs += [spec(0)] * 3
    big = jax.ShapeDtypeStruct((S, HW), F32)
    return _pcall(
        body, ins, phase, name=f"attn_bwd_d{dil}", out_shape=(big, big, big), grid=(nb, ng),
        in_specs=in_specs, out_specs=(spec(0),) * 3,
        compiler_params=_params(2),
    )


TC = 512
NCT = DFF // TC


def _shift_rows(a, k):
    rows = lax.broadcasted_iota(jnp.int32, a.shape, 0)
    rolled = pltpu.roll(a, k % S, 0)
    if k > 0:
        return jnp.where(rows >= k, rolled, 0.0)
    return jnp.where(rows < S + k, rolled, 0.0)


def _conv_pre(a, w_ref, b_ref):
    return b_ref[...] + w_ref[0:1, :] * _shift_rows(a, 2) + w_ref[1:2, :] * _shift_rows(a, 1) + w_ref[2:3, :] * a


def _conv_gate_fwd(u, conv_w, conv_b, phase=None):
    def body(a_ref, g_ref, w_ref, b_ref, y_ref):
        pre = _conv_pre(a_ref[...].astype(F32), w_ref, b_ref)
        y_ref[...] = (pre * _sigmoid(pre) * g_ref[...].astype(F32)).astype(BF16)

    return _pcall(
        body, [u, u, conv_w, conv_b], phase,
        name="conv_gate_fwd", out_shape=jax.ShapeDtypeStruct((S, DFF), BF16), grid=(NCT,),
        in_specs=[pl.BlockSpec((S, TC), lambda i: (0, i)), pl.BlockSpec((S, TC), lambda i: (0, NCT + i)),
                  pl.BlockSpec((3, TC), lambda i: (0, i)), pl.BlockSpec((1, TC), lambda i: (0, i))],
        out_specs=pl.BlockSpec((S, TC), lambda i: (0, i)),
        compiler_params=_params(1),
    )


def _conv_gate_bwd(dy, u, conv_w, conv_b, phase=None):
    def body(dy_ref, a_ref, g_ref, w_ref, b_ref, du_ref, db_ref, dw_ref, da_buf, dg_buf, sems):
        i = pl.program_id(0)
        slot = i % 2

        def writes(step, s):
            col = pl.multiple_of(step * TC, TC)
            return (pltpu.make_async_copy(da_buf.at[s], du_ref.at[:, pl.ds(col, TC)], sems.at[0, s]),
                    pltpu.make_async_copy(dg_buf.at[s], du_ref.at[:, pl.ds(DFF + col, TC)], sems.at[1, s]))

        @pl.when(i >= 2)
        def _():
            for cp in writes(i - 2, slot):
                cp.wait()

        a = a_ref[...].astype(F32)
        dyv = dy_ref[...].astype(F32)
        pre = _conv_pre(a, w_ref, b_ref)
        sg = _sigmoid(pre)
        dg_buf[slot] = (dyv * pre * sg).astype(BF16)
        dpre = dyv * g_ref[...].astype(F32) * (sg * (1.0 + pre * (1.0 - sg)))
        da = w_ref[2:3, :] * dpre + w_ref[1:2, :] * _shift_rows(dpre, -1) + w_ref[0:1, :] * _shift_rows(dpre, -2)
        da_buf[slot] = da.astype(BF16)
        for cp in writes(i, slot):
            cp.start()
        db_ref[...] = jnp.sum(dpre, axis=0, keepdims=True)
        dw_ref[0:1, :] = jnp.sum(dpre * _shift_rows(a, 2), axis=0, keepdims=True)
        dw_ref[1:2, :] = jnp.sum(dpre * _shift_rows(a, 1), axis=0, keepdims=True)
        dw_ref[2:3, :] = jnp.sum(dpre * a, axis=0, keepdims=True)

        @pl.when(i == NCT - 1)
        def _():
            for cp in writes(i - 1, 1 - slot) + writes(i, slot):
                cp.wait()

    col = pl.BlockSpec((S, TC), lambda i: (0, i))
    return _pcall(
        body, [dy, u, u, conv_w, conv_b], phase, name="conv_gate_bwd",
        out_shape=(jax.ShapeDtypeStruct((S, 2 * DFF), BF16), jax.ShapeDtypeStruct((1, DFF), F32),
                   jax.ShapeDtypeStruct((3, DFF), F32)),
        grid=(NCT,),
        in_specs=[col, col, pl.BlockSpec((S, TC), lambda i: (0, NCT + i)),
                  pl.BlockSpec((3, TC), lambda i: (0, i)), pl.BlockSpec((1, TC), lambda i: (0, i))],
        out_specs=(pl.BlockSpec(memory_space=pl.ANY), pl.BlockSpec((1, TC), lambda i: (0, i)),
                   pl.BlockSpec((3, TC), lambda i: (0, i))),
        scratch_shapes=[pltpu.VMEM((2, S, TC), BF16), pltpu.VMEM((2, S, TC), BF16), pltpu.SemaphoreType.DMA((2, 2))],
        compiler_params=_params(1),
    )


def _out_loss(z, x1, target, gate):
    def body(z_ref, x_ref, t_ref, g_ref, do_ref, dz_ref, dg_ref, loss_ref):
        i = pl.program_id(0)
        zv = z_ref[...]
        gv = g_ref[...]
        err = x_ref[...] + gv * zv - t_ref[...]
        dout = err * (1.0 / D)
        do_ref[...] = dout
        dz_ref[...] = (dout * gv).astype(BF16)

        @pl.when(i == 0)
        def _():
            dg_ref[...] = jnp.zeros_like(dg_ref)
            loss_ref[...] = jnp.zeros_like(loss_ref)

        dg_ref[...] += jnp.sum(dout * zv, axis=0, keepdims=True)
        part = jnp.sum(jnp.sum(err * err, axis=0, keepdims=True), axis=-1, keepdims=True) * (0.5 / D)
        lane = lax.broadcasted_iota(jnp.int32, (1, 128), 1)
        loss_ref[...] += jnp.where(lane == 0, part, 0.0)

    return pl.pallas_call(
        body, name="out_loss",
        out_shape=(jax.ShapeDtypeStruct((S, D), F32), jax.ShapeDtypeStruct((S, D), BF16),
                   jax.ShapeDtypeStruct((1, D), F32), jax.ShapeDtypeStruct((1, 128), F32)),
        grid=(S // TR,),
        in_specs=[_row_spec(), _row_spec(), _row_spec(), _vec_spec()],
        out_specs=(_row_spec(), _row_spec(), _vec_spec(), _vec_spec(128)),
        compiler_params=_params(1),
    )(z, x1, target, gate)


ADA_COLS = 6 * D // N_CHIPS
TA = 512


def _ada_fwd(c_all, w_shard, b_shard):
    def body(c_ref, w_ref, b_ref, o_ref):
        cv = c_ref[...]
        o_ref[...] = _dot_f32(cv * _sigmoid(cv), w_ref[...]) + b_ref[...]

    return pl.pallas_call(
        body, name="ada_fwd", out_shape=jax.ShapeDtypeStruct((N_DEV, ADA_COLS), F32), grid=(ADA_COLS // TA,),
        in_specs=[pl.BlockSpec((N_DEV, D), lambda j: (0, 0)), pl.BlockSpec((D, TA), lambda j: (0, j)),
                  pl.BlockSpec((1, TA), lambda j: (0, j))],
        out_specs=pl.BlockSpec((N_DEV, TA), lambda j: (0, j)),
        compiler_params=_params(1),
    )(c_all, w_shard, b_shard)


ADA_ROWS = 128


def _ada_bwd_adamw(c_all, dmod_shard, w, m, v):
    def body(c_ref, d_ref, w_ref, m_ref, v_ref, g_ref, dl_ref, mo_ref, vo_ref):
        cv = c_ref[...]
        gv = lax.dot_general(cv * _sigmoid(cv), d_ref[...], TN, precision=lax.Precision.HIGHEST,
                             preferred_element_type=F32)
        g_ref[...] = gv
        m2 = ADAM_B1 * m_ref[...] + (1.0 - ADAM_B1) * gv
        v2 = ADAM_B2 * v_ref[...] + (1.0 - ADAM_B2) * (gv * gv)
        m_hat = m2 / (1.0 - ADAM_B1 ** ADAM_STEP)
        v_hat = v2 / (1.0 - ADAM_B2 ** ADAM_STEP)
        dl_ref[...] = -ADAM_LR * (m_hat / (jnp.sqrt(v_hat) + ADAM_EPS) + ADAM_WD * w_ref[...])
        mo_ref[...] = m2
        vo_ref[...] = v2

    spec = pl.BlockSpec((ADA_ROWS, ADA_COLS), lambda i: (i, 0))
    out = jax.ShapeDtypeStruct((D, ADA_COLS), F32)
    return pl.pallas_call(
        body, name="ada_bwd_adamw", out_shape=(out,) * 4, grid=(D // ADA_ROWS,),
        in_specs=[pl.BlockSpec((N_DEV, ADA_ROWS), lambda i: (0, i)), pl.BlockSpec((N_DEV, ADA_COLS), lambda i: (0, 0)),
                  spec, spec, spec],
        out_specs=(spec,) * 4,
        compiler_params=_params(1),
    )(c_all, dmod_shard, w, m, v)


def _sum_rows(g, name):
    rows, n = g.shape

    def body(g_ref, o_ref):
        acc = g_ref[0:1, :]
        for i in range(1, rows):
            acc = acc + g_ref[i:i + 1, :]
        o_ref[...] = acc

    return pl.pallas_call(
        body, name=name, out_shape=jax.ShapeDtypeStruct((1, n), F32),
        in_specs=[VMEM_SPEC], out_specs=VMEM_SPEC,
    )(g)


def _lb_grad(lb_logits, dlb):
    def body(l_ref, d_ref, o_ref):
        p = 1.0 / (1.0 + jnp.exp(l_ref[1:2, :] - l_ref[0:1, :]))
        t = d_ref[...] * p * (1.0 - p)
        o_ref[0:1, :] = t
        o_ref[1:2, :] = -t

    return pl.pallas_call(
        body, name="lb_grad", out_shape=jax.ShapeDtypeStruct((2, HW), F32),
        in_specs=[VMEM_SPEC, VMEM_SPEC], out_specs=VMEM_SPEC,
    )(lb_logits, dlb)


def _adamw(w, g, m, v, name, copy_grad=False):
    rows, cols = w.shape
    tr = rows
    if rows * cols * 4 > ADAM_BLOCK_BYTES:
        tr = _pick(rows, [t for t in (256, 128, 64, 32, 16, 8) if t * cols * 4 <= ADAM_BLOCK_BYTES])

    def body(w_ref, g_ref, m_ref, v_ref, d_ref, mo_ref, vo_ref, *go_ref):
        gv = g_ref[...]
        if copy_grad:
            go_ref[0][...] = gv
        m2 = ADAM_B1 * m_ref[...] + (1.0 - ADAM_B1) * gv
        v2 = ADAM_B2 * v_ref[...] + (1.0 - ADAM_B2) * (gv * gv)
        m_hat = m2 / (1.0 - ADAM_B1 ** ADAM_STEP)
        v_hat = v2 / (1.0 - ADAM_B2 ** ADAM_STEP)
        d_ref[...] = -ADAM_LR * (m_hat / (jnp.sqrt(v_hat) + ADAM_EPS) + ADAM_WD * w_ref[...])
        mo_ref[...] = m2
        vo_ref[...] = v2

    spec = pl.BlockSpec((tr, cols), lambda i: (i, 0))
    out = jax.ShapeDtypeStruct((rows, cols), F32)
    n_out = 4 if copy_grad else 3
    return pl.pallas_call(
        body, name=name, out_shape=(out,) * n_out, grid=(rows // tr,),
        in_specs=[spec] * 4, out_specs=(spec,) * n_out,
        compiler_params=_params(1),
    )(w, g, m, v)


SC_TILES = 32
SC_LANES = 16
SC_COL_PARTS = 2
SC_CHUNK_ROWS = 8


def _adamw_sc(w, g, m, v, name):
    rows, cols = w.shape
    band, part = rows // (SC_TILES // SC_COL_PARTS), cols // SC_COL_PARTS
    assert band % SC_CHUNK_ROWS == 0 and part % SC_LANES == 0, (rows, cols)

    def body(w_hbm, g_hbm, m_hbm, v_hbm, d_hbm, mo_hbm, vo_hbm, go_hbm, wb, gb, mb, vb, db):
        tile = lax.axis_index("sc_subcore") * 2 + lax.axis_index("sc_core")
        row0 = (tile // SC_COL_PARTS) * band
        col0 = (tile % SC_COL_PARTS) * part

        @pl.loop(0, band, step=SC_CHUNK_ROWS)
        def _(r):
            at = lambda ref: ref.at[pl.ds(row0 + r, SC_CHUNK_ROWS), pl.ds(col0, part)]
            pltpu.sync_copy(at(w_hbm), wb)
            pltpu.sync_copy(at(g_hbm), gb)
            pltpu.sync_copy(gb, at(go_hbm))
            pltpu.sync_copy(at(m_hbm), mb)
            pltpu.sync_copy(at(v_hbm), vb)

            @pl.loop(0, SC_CHUNK_ROWS)
            def _(i):
                @pl.loop(0, part, step=SC_LANES)
                def _(j):
                    sl = (i, pl.ds(j, SC_LANES))
                    gv = gb[sl]
                    m2 = ADAM_B1 * mb[sl] + (1.0 - ADAM_B1) * gv
                    v2 = ADAM_B2 * vb[sl] + (1.0 - ADAM_B2) * (gv * gv)
                    m_hat = m2 / (1.0 - ADAM_B1 ** ADAM_STEP)
                    v_hat = v2 / (1.0 - ADAM_B2 ** ADAM_STEP)
                    db[sl] = -ADAM_LR * (m_hat / (jnp.sqrt(v_hat) + ADAM_EPS) + ADAM_WD * wb[sl])
                    mb[sl] = m2
                    vb[sl] = v2

            pltpu.sync_copy(db, at(d_hbm))
            pltpu.sync_copy(mb, at(mo_hbm))
            pltpu.sync_copy(vb, at(vo_hbm))

    out = jax.ShapeDtypeStruct((rows, cols), F32)
    buf = pltpu.VMEM((SC_CHUNK_ROWS, part), F32)
    return pl.kernel(
        body, name=name, out_type=(out, out, out, out),
        mesh=plsc.VectorSubcoreMesh(core_axis_name="sc_core", subcore_axis_name="sc_subcore"),
        scratch_types=[buf] * 5,
    )(w, g, m, v)


W_IN, W_OUT, W_UP, W_DOWN = range(4)
IN_CHUNKS = 4
W_INQ = 4


def _join_row_chunks(chunks):
    return jnp.concatenate(chunks, axis=0)


def _sequence_step(x, target, mod, norm1_w, lb_logits, hg_norm_w, q_norm_w, k_norm_w, norm2_w, conv_w, conv_b, net):
    shift1, scale1, gate1, shift2, scale2, gate2 = [mod[:, i * D:(i + 1) * D] for i in range(6)]

    def run(name, fn, *args, **kw):
        phase = net.host(name)
        if phase is None:
            return fn(*args, **kw)
        res, comm = fn(*args, phase=phase, **kw)
        net.done(name, comm)
        return res

    h = _norm_mod(x, norm1_w, scale1, shift1, "norm1_fwd")
    proj = run("proj_fwd", _matmul, h, net.full[W_IN], "nn", F32, "proj_fwd")
    cat, o_raw, states = run("hgrn_fwd", _hgrn_fwd, proj, lb_logits, hg_norm_w)
    qn, kn = _qk_prep(proj, q_norm_w, k_norm_w)
    att = [run(f"attn_fwd_d{dil}", _attn_fwd, proj, qn, kn, dil) for dil in DILS]
    cat, b_out_f32, lse = _attn_merge([t[0] for t in att], [t[1] for t in att], cat)
    mix = run("mix_fwd", _matmul, cat, net.full[W_OUT], "nn", F32, "mix_fwd")
    x1, h2 = _resid_norm_mod(x, mix, gate1, norm2_w, scale2, shift2, "norm2_fwd")
    u = run("up_fwd", _matmul, h2, net.full[W_UP], "nn", BF16, "up_fwd")
    yact = run("conv_gate_fwd", _conv_gate_fwd, u, conv_w, conv_b)
    net.alone("before_down_fwd")
    z = _matmul(yact, net.full[W_DOWN], "nn", F32, "down_fwd")
    dout, dz, dgate2, loss_row = _out_loss(z, x1, target, gate2)

    net.grad[W_DOWN] = _matmul(yact, dz, "tn", BF16, "down_bwd_w")
    dyact = run("down_bwd_x", _matmul, dz, net.full[W_DOWN], "nt", BF16, "down_bwd_x")
    du, dconv_b, dconv_w = run("conv_gate_bwd", _conv_gate_bwd, dyact, u, conv_w, conv_b)
    net.grad[W_UP] = run("up_bwd_w", _matmul, h2, du, "tn", BF16, "up_bwd_w")
    dh2 = run("up_bwd_x", _matmul, du, net.full[W_UP], "nt", F32, "up_bwd_x")
    dx1, dshift2, dscale2, dnorm2, dgate1, dmix = run(
        "norm2_bwd", _norm_mod_bwd, dh2, x1, norm2_w, scale2, dout, "norm2_bwd", mix=mix, gate=gate1)
    net.grad[W_OUT] = run("mix_bwd_w", _matmul, cat, dmix, "tn", BF16, "mix_bwd_w")
    dcat = run("mix_bwd_x", _matmul, dmix, net.full[W_OUT], "nt", F32, "mix_bwd_x")
    dhq, dhf, dhi, dhg, dlb, dhg_norm = run("hgrn_bwd", _hgrn_bwd, proj, lb_logits, hg_norm_w, o_raw, states, dcat)
    delta = _attn_delta(dcat, b_out_f32)
    acc = None
    for dil in DILS:
        acc = run(f"attn_bwd_d{dil}", _attn_bwd, proj, qn, kn, lse, delta, dcat, dil, acc)
    daq, dak, dav, dq_norm, dk_norm = _qk_norm_bwd(proj, *acc, q_norm_w, k_norm_w)
    dproj = jnp.concatenate([dhq, dhf, dhi, dhg, daq, dak, dav], axis=1)
    for q in range(IN_CHUNKS):
        net.grad[W_INQ + q] = run(f"proj_bwd_w_{q}", _matmul, h, dproj, "tn", BF16, f"proj_bwd_w_{q}",
                                  m_blocks=(D // IN_CHUNKS, [q]))
    dh = run("proj_bwd_x", _matmul, dproj, net.full[W_IN], "nt", F32, "proj_bwd_x")
    grad_x, dshift1, dscale1, dnorm1 = run("norm1_bwd", _norm_mod_bwd, dh, x, norm1_w, scale1, dx1, "norm1_bwd")

    dmod = jnp.concatenate([dshift1, dscale1, dgate1, dshift2, dscale2, dgate2], axis=1)
    small = dict(norm1_w=dnorm1, lb=dlb, hg_norm_w=dhg_norm, q_norm_w=dq_norm, k_norm_w=dk_norm,
                 norm2_w=dnorm2, conv_b=dconv_b, conv_w=dconv_w)
    return loss_row, grad_x, dmod, small


def _place():
    x, y, c = lax.axis_index("x"), lax.axis_index("y"), lax.axis_index("c")
    others = [(1 - x, y), (x, 1 - y), (1 - x, 1 - y)]
    return x, y, c, others


def _remote(src, dst, send_sems, recv_sems, k, to):
    return pltpu.make_async_remote_copy(src_ref=src, dst_ref=dst, send_sem=send_sems.at[k], recv_sem=recv_sems.at[k],
                                        device_id=to, device_id_type=MESH)


class _Phase:
    def __init__(self):
        self.ins, self.outs, self.aliases, self.groups, self.n = [], [], {}, [], 0

    def add(self, ins, outs, aliases, n, copies):
        i0, o0 = len(self.ins), len(self.outs)
        self.ins += list(ins)
        self.outs += list(outs)
        self.aliases.update({i0 + i: o0 + o for i, o in aliases.items()})
        self.groups.append((slice(i0, i0 + len(ins)), slice(o0, o0 + len(outs)), copies))
        self.n += n
        return slice(o0, o0 + len(outs))

    def build(self, cin, cout, send_sems, recv_sems, landing):
        res = []
        for si, so, copies in self.groups:
            for src, dst, land, peer in copies(cin[si], cout[so]):
                k = len(res)
                res.append(_remote(land, land, send_sems, recv_sems, k, peer) if landing
                           else _remote(src, dst, send_sems, recv_sems, k, peer))
        assert len(res) == self.n
        return res


def _pcall(body, args, phase=None, **kw):
    if phase is None or not phase.groups:
        res = pl.pallas_call(body, **kw)(*args)
        return res if phase is None else (res, ())
    grid = tuple(kw.pop("grid", ()))
    out_shape, out_specs = kw.pop("out_shape"), kw.pop("out_specs")
    single = not isinstance(out_shape, (tuple, list))
    out_shape = [out_shape] if single else list(out_shape)
    out_specs = [out_specs] if single else list(out_specs)
    scratch = list(kw.pop("scratch_shapes", ()))
    n_in, n_out, n_ci, n_co = len(args), len(out_shape), len(phase.ins), len(phase.outs)

    def hosted(*refs):
        ins, cin = refs[:n_in], refs[n_in:n_in + n_ci]
        outs = refs[n_in + n_ci:n_in + n_ci + n_out]
        cout = refs[n_in + n_ci + n_out:n_in + n_ci + n_out + n_co]
        scr, send_sems, recv_sems = refs[n_in + n_ci + n_out + n_co:-2], refs[-2], refs[-1]
        ids = [pl.program_id(a) for a in range(len(grid))]

        def start():
            for cp in phase.build(cin, cout, send_sems, recv_sems, False):
                cp.start()

        def finish():
            for cp in phase.build(cin, cout, send_sems, recv_sems, False):
                cp.wait_send()
            for cp in phase.build(cin, cout, send_sems, recv_sems, True):
                cp.wait_recv()

        if grid:
            first = functools.reduce(jnp.logical_and, [i == 0 for i in ids])
            last = functools.reduce(jnp.logical_and, [i == g - 1 for i, g in zip(ids, grid)])
            pl.when(first)(start)
            body(*ins, *outs, *scr)
            pl.when(last)(finish)
        else:
            start()
            body(*ins, *outs, *scr)
            finish()

    res = pl.pallas_call(
        hosted, out_shape=tuple(out_shape + phase.outs), grid=grid,
        in_specs=list(kw.pop("in_specs")) + [HBM_SPEC] * n_ci, out_specs=tuple(out_specs + [HBM_SPEC] * n_co),
        input_output_aliases={n_in + i: n_out + o for i, o in phase.aliases.items()},
        scratch_shapes=scratch + [pltpu.SemaphoreType.DMA((phase.n,)), pltpu.SemaphoreType.DMA((phase.n,))],
        **kw,
    )(*args, *phase.ins)
    outs = res[:n_out]
    return (outs[0] if single else tuple(outs)), tuple(res[n_out:])


def _comm_only(name, phase):
    return _pcall(lambda: None, [], phase, name=name, out_shape=(), in_specs=[], out_specs=())[1]


def _all_gather_rows(block, name, phase=None):
    m_per, n = block.shape

    def body(x_ref, out_ref, send_sems, recv_sems, local_sem):
        x, y, c, chips = _place()
        me, sibling = (x, y, c), (x, y, 1 - c)

        def rows(px, py, pc):
            return out_ref.at[pl.ds((4 * px + 2 * py + pc) * m_per, m_per), :]

        def copy(k, blk, to, src=None):
            return _remote(rows(*blk) if src is None else src, rows(*blk), send_sems, recv_sems, k, to)

        mine = pltpu.make_async_copy(x_ref, rows(*me), local_sem)
        mine.start()
        first = [copy(0, me, sibling, src=x_ref)]
        first += [copy(1 + j, me, (*chip, c), src=x_ref) for j, chip in enumerate(chips)]
        for cp in first:
            cp.start()
        passed = [copy(4 + j, (*chip, c), sibling) for j, chip in enumerate(chips)]
        for j, chip in enumerate(chips):
            copy(1 + j, (*chip, c), me).wait_recv()
            passed[j].start()
        copy(0, sibling, me).wait_recv()
        for j, chip in enumerate(chips):
            copy(4 + j, (*chip, 1 - c), me).wait_recv()
        for cp in first + passed:
            cp.wait_send()
        mine.wait()

    return _pcall(
        body, [block], phase, name=name, out_shape=jax.ShapeDtypeStruct((N_DEV * m_per, n), block.dtype),
        in_specs=[VMEM_SPEC], out_specs=VMEM_SPEC,
        scratch_shapes=[pltpu.SemaphoreType.DMA((7,)), pltpu.SemaphoreType.DMA((7,)), pltpu.SemaphoreType.DMA],
    )


class _Geo:
    def __init__(self, kind, shard_shape):
        self.kind = kind
        self.shard_shape = tuple(shard_shape)
        self.hr, self.hc = shard_shape[0] // 2, shard_shape[1]
        if kind == "col":
            self.full_shape = (shard_shape[0], N_CHIPS * shard_shape[1])
        else:
            self.full_shape = (N_CHIPS * shard_shape[0], shard_shape[1])

    def full_shard(self, ref, j):
        if self.kind == "col":
            return ref.at[:, pl.ds(pl.multiple_of(j * self.hc, 128), self.hc)]
        return ref.at[pl.ds(pl.multiple_of(j * 2 * self.hr, 16), 2 * self.hr), :]

    def full_half(self, ref, j, c):
        if self.kind == "col":
            return ref.at[pl.ds(pl.multiple_of(c * self.hr, 16), self.hr),
                          pl.ds(pl.multiple_of(j * self.hc, 128), self.hc)]
        return ref.at[pl.ds(pl.multiple_of((2 * j + c) * self.hr, 16), self.hr), :]

    def piece(self, ref, j, c, p0, p1, pieces, part=None):
        pr = self.hr // pieces
        off, n = p0 * pr, (p1 - p0) * pr
        if part is not None:
            top = (n // 32) * 16
            off, n = (off, top) if part == 0 else (off + top, n - top)
        if self.kind == "col":
            return ref.at[pl.ds(pl.multiple_of(c * self.hr + off, 16), n),
                          pl.ds(pl.multiple_of(j * self.hc, 128), self.hc)]
        return ref.at[pl.ds(pl.multiple_of((2 * j + c) * self.hr + off, 16), n), :]


WEIGHT_KINDS = ("col", "row", "col", "row")
NW = len(WEIGHT_KINDS)


def _comm_call(body, name, ins, outs, n_remote, aliases=None):
    return pl.pallas_call(
        body, name=name, out_shape=tuple(outs),
        in_specs=[HBM_SPEC] * len(ins), out_specs=tuple([HBM_SPEC] * len(outs)),
        input_output_aliases=aliases or {},
        scratch_shapes=[pltpu.SemaphoreType.DMA((n_remote,)), pltpu.SemaphoreType.DMA((n_remote,))],
    )(*ins)


ROW_TILES = (256, 176, 128, 64, 32, 16)


def _cast_into_full(shard, geo, place, name):
    rs, cs = shard.shape
    tr = _pick(rs, ROW_TILES)
    nb = rs // tr

    def body(place_ref, s_ref, o_ref):
        o_ref[...] = s_ref[...].astype(BF16)

    if geo.kind == "col":
        out_map = lambda i, place_ref: (i, place_ref[0])
    else:
        out_map = lambda i, place_ref: (place_ref[0] * nb + i, 0)
    return pl.pallas_call(
        body, name=name, out_shape=jax.ShapeDtypeStruct(geo.full_shape, BF16),
        grid_spec=pltpu.PrefetchScalarGridSpec(
            num_scalar_prefetch=1, grid=(nb,),
            in_specs=[pl.BlockSpec((tr, cs), lambda i, place_ref: (i, 0))],
            out_specs=pl.BlockSpec((tr, cs), out_map)),
        compiler_params=_params(1),
    )(place, shard)


def _gather_weight(full, geo, pieces, name):
    per = 8

    def body(in_ref, ref, send_sems, recv_sems):
        x, y, c, _ = _place()
        j, jx, jy, jo = 2 * x + y, 2 * (1 - x) + y, 2 * x + (1 - y), 2 * (1 - x) + (1 - y)
        nx, ny, sib = (1 - x, y, c), (x, 1 - y, c), (x, y, 1 - c)

        def cp(region, k, to):
            return _remote(region, region, send_sems, recv_sems, k, to)

        def reg(chip, p, part=None, core=c):
            return geo.piece(ref, chip, core, p, p + 1, pieces, part)

        sent = []
        for p in range(pieces):
            sent += [cp(reg(j, p), per * p, nx), cp(reg(j, p), per * p + 1, ny)]
        for d in sent:
            d.start()
        for p in range(pieces):
            cp(reg(jx, p), per * p, nx).wait_recv()
            new = [cp(reg(jx, p, 0), per * p + 2, ny), cp(reg(jx, p), per * p + 4, sib)]
            cp(reg(jy, p), per * p + 1, ny).wait_recv()
            new += [cp(reg(jy, p, 1), per * p + 3, nx), cp(reg(jy, p), per * p + 5, sib)]
            for d in new:
                d.start()
            sent += new
        for p in range(pieces):
            cp(reg(jo, p, 0), per * p + 2, ny).wait_recv()
            cp(reg(jo, p, 1), per * p + 3, nx).wait_recv()
            new = [cp(reg(jo, p, 0), per * p + 6, sib), cp(reg(jo, p, 1), per * p + 7, sib)]
            for d in new:
                d.start()
            sent += new
        for p in range(pieces):
            cp(reg(jx, p, None, 1 - c), per * p + 4, sib).wait_recv()
            cp(reg(jy, p, None, 1 - c), per * p + 5, sib).wait_recv()
            cp(reg(jo, p, 0, 1 - c), per * p + 6, sib).wait_recv()
            cp(reg(jo, p, 1, 1 - c), per * p + 7, sib).wait_recv()
        for d in sent:
            d.wait_send()

    return _comm_call(body, name, [full], [_same(full)], per * pieces, aliases={0: 0})[0]


def _same(a):
    return jax.ShapeDtypeStruct(a.shape, a.dtype)


def _gather_ici(full, geo, p0, p1, pieces):
    def copies(ins, outs):
        x, y, c, _ = _place()
        mine = geo.piece(outs[0], 2 * x + y, c, p0, p1, pieces)
        return [(mine, mine, geo.piece(outs[0], 2 * ox + oy, c, p0, p1, pieces), (ox, oy, c))
                for ox, oy in ((1 - x, y), (x, 1 - y))]

    return dict(ins=[full], outs=[_same(full)], aliases={0: 0}, n=2, copies=copies)


def _gather_relay(full, geo, p0, p1, pieces):
    def copies(ins, outs):
        x, y, c, _ = _place()
        jx, jy, jo = 2 * (1 - x) + y, 2 * x + (1 - y), 2 * (1 - x) + (1 - y)
        reg = lambda chip, part: geo.piece(outs[0], chip, c, p0, p1, pieces, part)
        return [(reg(jx, 0), reg(jx, 0), reg(jo, 0), (x, 1 - y, c)), (reg(jy, 1), reg(jy, 1), reg(jo, 1), (1 - x, y, c))]

    return dict(ins=[full], outs=[_same(full)], aliases={0: 0}, n=2, copies=copies)


def _gather_d2d(full, geo):
    def copies(ins, outs):
        x, y, c, chips = _place()
        return [(geo.full_half(outs[0], 2 * ox + oy, c), geo.full_half(outs[0], 2 * ox + oy, c),
                 geo.full_half(outs[0], 2 * ox + oy, 1 - c), (x, y, 1 - c)) for ox, oy in chips]

    return dict(ins=[full], outs=[_same(full)], aliases={0: 0}, n=3, copies=copies)


def _swap_d2d(grad, geo):
    def copies(ins, outs):
        x, y, c, _ = _place()
        return [(geo.full_half(ins[0], j, 1 - c), outs[0].at[j], outs[0].at[j], (x, y, 1 - c)) for j in range(N_CHIPS)]

    return dict(ins=[grad], outs=[jax.ShapeDtypeStruct((N_CHIPS, geo.hr, geo.hc), BF16)], aliases={}, n=N_CHIPS,
                copies=copies)


def _scatter_ici(pair, recv, geo, p0, p1, pieces):
    pr = geo.hr // pieces
    rows = pl.ds(p0 * pr, (p1 - p0) * pr)

    def copies(ins, outs):
        x, y, c, chips = _place()
        return [(ins[0].at[2 * ox + oy, rows, :], outs[0].at[k, rows, :], outs[0].at[k, rows, :], (ox, oy, c))
                for k, (ox, oy) in enumerate(chips)]

    out = jax.ShapeDtypeStruct((3, geo.hr, geo.hc), BF16)
    if recv is None:
        return dict(ins=[pair], outs=[out], aliases={}, n=3, copies=copies)
    return dict(ins=[pair, recv], outs=[out], aliases={1: 0}, n=3, copies=copies)


def _join_d2d(shard, geo, row0=0):
    def copies(ins, outs):
        x, y, c, _ = _place()
        mine = outs[0].at[pl.ds(pl.multiple_of(row0 + c * geo.hr, 8), geo.hr), :]
        other = outs[0].at[pl.ds(pl.multiple_of(row0 + (1 - c) * geo.hr, 8), geo.hr), :]
        return [(mine, mine, other, (x, y, 1 - c))]

    return dict(ins=[shard], outs=[_same(shard)], aliases={0: 0}, n=1, copies=copies)


def _add_pair(grad, got, geo, place, name):
    tr = _pick(geo.hr, ROW_TILES)
    nb = geo.hr // tr

    def body(place_ref, a_ref, b_ref, o_ref):
        o_ref[0] = (a_ref[...].astype(F32) + b_ref[0].astype(F32)).astype(BF16)

    if geo.kind == "col":
        own_map = lambda j, i, place_ref: (place_ref[1] * nb + i, j)
    else:
        own_map = lambda j, i, place_ref: ((2 * j + place_ref[1]) * nb + i, 0)
    spec = pl.BlockSpec((1, tr, geo.hc), lambda j, i, place_ref: (j, i, 0))
    return pl.pallas_call(
        body, name=name, out_shape=jax.ShapeDtypeStruct((N_CHIPS, geo.hr, geo.hc), BF16),
        grid_spec=pltpu.PrefetchScalarGridSpec(
            num_scalar_prefetch=1, grid=(N_CHIPS, nb),
            in_specs=[pl.BlockSpec((tr, geo.hc), own_map), spec], out_specs=spec),
        compiler_params=_params(2),
    )(place, grad, got)


def _add_four(pair, recv, geo, place, name, rows=None, row0=0, into=None):
    tr = _pick(geo.hr, ROW_TILES)
    nb = geo.hr // tr
    rows = 2 * geo.hr if rows is None else rows

    def body(place_ref, p_ref, r_ref, *rest):
        rest[-1][...] = ((p_ref[0].astype(F32) + r_ref[0].astype(F32)) + r_ref[1].astype(F32)) + r_ref[2].astype(F32)

    in_specs = [pl.BlockSpec((1, tr, geo.hc), lambda i, place_ref: (place_ref[0], i, 0)),
                pl.BlockSpec((3, tr, geo.hc), lambda i, place_ref: (0, i, 0))]
    args = [place, pair, recv]
    if into is not None:
        in_specs.append(pl.BlockSpec(memory_space=pl.ANY))
        args.append(into)
    return pl.pallas_call(
        body, name=name, out_shape=jax.ShapeDtypeStruct((rows, geo.hc), F32),
        grid_spec=pltpu.PrefetchScalarGridSpec(
            num_scalar_prefetch=1, grid=(nb,), in_specs=in_specs,
            out_specs=pl.BlockSpec((tr, geo.hc), lambda i, place_ref: (row0 // tr + place_ref[1] * nb + i, 0))),
        input_output_aliases={3: 0} if into is not None else {},
        compiler_params=_params(1),
    )(*args)


PIECES = (4, 1, 16, 8) + (1,) * IN_CHUNKS
SCHEDULE = {
    "proj_fwd": (("gather_ici", W_OUT, 0, 1), ("gather_ici", W_UP, 0, 6)),
    "hgrn_fwd": (("gather_relay", W_OUT, 0, 1), ("gather_relay", W_UP, 0, 6), ("gather_ici", W_UP, 6, 12)),
    "attn_fwd_d1": (("gather_relay", W_UP, 6, 12),),
    "attn_fwd_d4": (("gather_ici", W_UP, 12, 16), ("gather_d2d", W_OUT)),
    "attn_fwd_d16": (("gather_relay", W_UP, 12, 16), ("gather_ici", W_DOWN, 0, 2)),
    "mix_fwd": (("gather_d2d", W_UP), ("gather_ici", W_DOWN, 2, 4)),
    "up_fwd": (("gather_ici", W_DOWN, 4, 8), ("gather_relay", W_DOWN, 0, 4)),
    "conv_gate_fwd": (("gather_relay", W_DOWN, 4, 8),),
    "before_down_fwd": (("gather_d2d", W_DOWN),),
    "down_bwd_x": (("swap", W_DOWN),),
    "conv_gate_bwd": (("scatter", W_DOWN, 0, 4),),
    "up_bwd_w": (("scatter", W_DOWN, 4, 8),),
    "up_bwd_x": (("swap", W_UP),),
    "norm2_bwd": (("scatter", W_UP, 0, 1),),
    "mix_bwd_w": (("scatter", W_UP, 1, 2),),
    "mix_bwd_x": (("swap", W_OUT), ("scatter", W_UP, 2, 3)),
    "hgrn_bwd": (("scatter", W_UP, 3, 9), ("join", W_DOWN)),
    "attn_bwd_d1": (("scatter", W_UP, 9, 12),),
    "attn_bwd_d4": (("scatter", W_OUT, 0, 1),),
    "attn_bwd_d16": (("scatter", W_UP, 12, 16),),
    "proj_bwd_w_0": (("join", W_UP), ("join", W_OUT)),
    "proj_bwd_w_1": (("swap", W_INQ),),
    "proj_bwd_w_2": (("swap", W_INQ + 1),),
    "proj_bwd_w_3": (("swap", W_INQ + 2), ("scatter", W_INQ, 0, 1)),
    "proj_bwd_x": (("swap", W_INQ + 3), ("scatter", W_INQ + 1, 0, 1), ("scatter", W_INQ + 2, 0, 1)),
    "gather_stats": (("scatter", W_INQ + 3, 0, 1), ("join", W_INQ), ("join", W_INQ + 1), ("join", W_INQ + 2)),
    "grad_in_join": (("join", W_INQ + 3),),
}


class _Net:
    def __init__(self, shards, place):
        self.place = place
        self.geos = [_Geo(k, s.shape) for k, s in zip(WEIGHT_KINDS, shards)]
        self.full = [_cast_into_full(s, g, place, f"cast_shard_{w}") for w, (s, g) in enumerate(zip(shards, self.geos))]
        self.full[W_IN] = _gather_weight(self.full[W_IN], self.geos[W_IN], PIECES[W_IN], "gather_w_in")
        rows, cols = shards[W_IN].shape
        self.geos += [_Geo("col", (rows // IN_CHUNKS, cols))] * IN_CHUNKS
        n = NW + IN_CHUNKS
        self.grad, self.pair, self.recv, self.shard = [None] * n, [None] * n, [None] * n, [None] * n
        self.in_rows, self.in_shard = rows, None
        self.when_joined, self.joined = {}, {}
        self.slots = []

    def _row0(self, w):
        return (w - W_INQ) * 2 * self.geos[w].hr

    def host(self, name):
        phase = _Phase()
        self.slots = []
        groups = {}
        for item in SCHEDULE.get(name, ()):
            kind, w = item[0], item[1]
            geo = self.geos[w]
            key = len(groups)
            if kind == "gather_ici":
                spec, key = _gather_ici(self.full[w], geo, item[2], item[3], PIECES[w]), ("full", w)
            elif kind == "gather_relay":
                spec, key = _gather_relay(self.full[w], geo, item[2], item[3], PIECES[w]), ("full", w)
            elif kind == "gather_d2d":
                spec, key = _gather_d2d(self.full[w], geo), ("full", w)
            elif kind == "swap":
                spec = _swap_d2d(self.grad[w], geo)
            elif kind == "scatter":
                spec, key = _scatter_ici(self.pair[w], self.recv[w], geo, item[2], item[3], PIECES[w]), ("recv", w)
            elif w >= W_INQ:
                spec, key = _join_d2d(self.in_shard, geo, self._row0(w)), "in_shard"
            else:
                spec = _join_d2d(self.shard[w], geo)
            groups.setdefault(key, []).append((item, spec))
        for group in groups.values():
            specs = [s for _, s in group]
            merged = dict(specs[0], n=sum(s["n"] for s in specs),
                          copies=lambda ins, outs, specs=specs: [t for s in specs for t in s["copies"](ins, outs)])
            self.slots.append(([item for item, _ in group], phase.add(**merged)))
        return phase

    def done(self, name, comm):
        for items, sl in sorted(self.slots, key=lambda s: s[0][0][0] == "scatter"):
            out = comm[sl][0]
            for item in items:
                kind, w = item[0], item[1]
                if kind in ("gather_ici", "gather_relay", "gather_d2d"):
                    self.full[w] = out
                elif kind == "swap":
                    self.pair[w] = _add_pair(self.grad[w], out, self.geos[w], self.place, f"grad_pair_sum_{w}")
                elif kind == "scatter":
                    self.recv[w] = out
                    if item[3] == PIECES[w] and w >= W_INQ:
                        self.in_shard = _add_four(self.pair[w], out, self.geos[w], self.place, f"grad_chip_sum_{w}",
                                                  rows=self.in_rows, row0=self._row0(w), into=self.in_shard)
                    elif item[3] == PIECES[w]:
                        self.shard[w] = _add_four(self.pair[w], out, self.geos[w], self.place, f"grad_chip_sum_{w}")
                elif w >= W_INQ:
                    self.in_shard = out
                else:
                    self.shard[w] = out
                    if w in self.when_joined:
                        self.joined[w] = self.when_joined[w](out)

    def alone(self, name):
        self.done(name, _comm_only(name, self.host(name)))


CONVW_SHARD = 3 * DFF // N_CHIPS
COND_PAD = 8 * 896
SMALL_SIZES = (("norm1_w", D), ("lb", HW), ("hg_norm_w", DH), ("q_norm_w", DH), ("k_norm_w", DH),
               ("norm2_w", D), ("conv_b", DFF), ("conv_w", 3 * DFF), ("loss", 128))
SMALL_TOTAL = sum(n for _, n in SMALL_SIZES)
STATS_PAD = 8 * 5120


def kernel(x, c, w_ada, b_ada, norm1_w, w_in, lb_logits, hg_norm_w, q_norm_w, k_norm_w, w_out, norm2_w, w_up, conv_w, conv_b, w_down, loss_target, m_w_ada, m_b_ada, m_norm1_w, m_w_in, m_lb_logits, m_hg_norm_w, m_q_norm_w, m_k_norm_w, m_w_out, m_norm2_w, m_w_up, m_conv_w, m_conv_b, m_w_down, v_w_ada, v_b_ada, v_norm1_w, v_w_in, v_lb_logits, v_hg_norm_w, v_q_norm_w, v_k_norm_w, v_w_out, v_norm2_w, v_w_up, v_conv_w, v_conv_b, v_w_down):
    chip = 2 * lax.axis_index("x") + lax.axis_index("y")
    dev = 2 * chip + lax.axis_index("c")

    cond = jnp.concatenate([c, conv_w[0].reshape(1, CONVW_SHARD), jnp.zeros((1, COND_PAD - D - CONVW_SHARD), F32)], axis=1)
    cond_all = _all_gather_rows(cond.reshape(8, COND_PAD // 8), "gather_cond").reshape(N_DEV, COND_PAD)
    c_all = cond_all[:, :D]
    conv_w_full = jnp.concatenate(
        [cond_all[2 * j, D:D + CONVW_SHARD].reshape(3, DFF // N_CHIPS) for j in range(N_CHIPS)], axis=1)

    b_shard = lax.dynamic_slice_in_dim(b_ada, chip * ADA_COLS, ADA_COLS, axis=1)
    mod_cols = _all_gather_rows(_ada_fwd(c_all, w_ada[0], b_shard), "gather_mod")
    mod_all = jnp.concatenate([mod_cols[16 * j:16 * j + 8] for j in range(N_CHIPS)], axis=1)
    mod = lax.dynamic_slice_in_dim(mod_all, dev, 1, axis=0)

    place = jnp.stack([chip, lax.axis_index("c")]).astype(jnp.int32)
    net = _Net([w_in[0], w_out[0], w_up[0], w_down[0]], place)
    net.when_joined = {
        W_OUT: lambda grad: _adamw_sc(w_out[0], grad, m_w_out[0], v_w_out[0], "adamw_sc_w_out"),
        W_DOWN: lambda grad: _adamw_sc(w_down[0], grad, m_w_down[0], v_w_down[0], "adamw_sc_w_down"),
        W_UP: lambda grad: _adamw_sc(w_up[0], grad, m_w_up[0], v_w_up[0], "adamw_sc_w_up"),
    }
    early = {W_OUT: "w_out", W_DOWN: "w_down", W_UP: "w_up"}
    loss_row, grad_x, dmod, small = _sequence_step(
        x[0], loss_target[0], mod, norm1_w, lb_logits, hg_norm_w, q_norm_w, k_norm_w, norm2_w, conv_w_full, conv_b, net)
    small["conv_w"] = small["conv_w"].reshape(1, 3 * DFF)
    small["loss"] = loss_row
    stats = jnp.concatenate([dmod] + [small[k] for k, _ in SMALL_SIZES]
                            + [jnp.zeros((1, STATS_PAD - 6 * D - SMALL_TOTAL), F32)], axis=1)
    stats_all, comm = _all_gather_rows(stats.reshape(8, STATS_PAD // 8), "gather_stats", net.host("gather_stats"))
    net.done("gather_stats", comm)
    stats_all = stats_all.reshape(N_DEV, STATS_PAD)
    net.alone("grad_in_join")
    g_out, g_up, g_down = net.shard[W_OUT], net.shard[W_UP], net.shard[W_DOWN]
    g_in = net.in_shard
    dmod_all = stats_all[:, :6 * D]
    sums = _sum_rows(stats_all[:, 6 * D:6 * D + SMALL_TOTAL], "small_grad_sum")
    g_small, off = {}, 0
    for k, n in SMALL_SIZES:
        g_small[k] = sums[:, off:off + n]
        off += n
    loss = g_small["loss"][0, 0]
    g_b_ada = _sum_rows(dmod_all, "b_ada_grad_sum")
    ada = _ada_bwd_adamw(c_all, lax.dynamic_slice_in_dim(dmod_all, chip * ADA_COLS, ADA_COLS, axis=1),
                         w_ada[0], m_w_ada[0], v_w_ada[0])
    g_w_ada = ada[0]
    g_lb = _lb_grad(lb_logits, g_small["lb"])
    g_conv_w = lax.dynamic_slice_in_dim(g_small["conv_w"].reshape(3, DFF), chip * (DFF // N_CHIPS), DFF // N_CHIPS, axis=1)

    names = ["w_ada", "b_ada", "norm1_w", "w_in", "lb_logits", "hg_norm_w", "q_norm_w", "k_norm_w", "w_out",
             "norm2_w", "w_up", "conv_w", "conv_b", "w_down"]
    lead = {"w_ada", "w_in", "w_out", "w_up", "conv_w", "w_down"}
    w = dict(w_ada=w_ada, b_ada=b_ada, norm1_w=norm1_w, w_in=w_in, lb_logits=lb_logits, hg_norm_w=hg_norm_w,
             q_norm_w=q_norm_w, k_norm_w=k_norm_w, w_out=w_out, norm2_w=norm2_w, w_up=w_up, conv_w=conv_w,
             conv_b=conv_b, w_down=w_down)
    m = dict(w_ada=m_w_ada, b_ada=m_b_ada, norm1_w=m_norm1_w, w_in=m_w_in, lb_logits=m_lb_logits,
             hg_norm_w=m_hg_norm_w, q_norm_w=m_q_norm_w, k_norm_w=m_k_norm_w, w_out=m_w_out, norm2_w=m_norm2_w,
             w_up=m_w_up, conv_w=m_conv_w, conv_b=m_conv_b, w_down=m_w_down)
    v = dict(w_ada=v_w_ada, b_ada=v_b_ada, norm1_w=v_norm1_w, w_in=v_w_in, lb_logits=v_lb_logits,
             hg_norm_w=v_hg_norm_w, q_norm_w=v_q_norm_w, k_norm_w=v_k_norm_w, w_out=v_w_out, norm2_w=v_norm2_w,
             w_up=v_w_up, conv_w=v_conv_w, conv_b=v_conv_b, w_down=v_w_down)
    g = dict(w_ada=g_w_ada, b_ada=g_b_ada, norm1_w=g_small["norm1_w"], w_in=g_in, lb_logits=g_lb,
             hg_norm_w=g_small["hg_norm_w"], q_norm_w=g_small["q_norm_w"], k_norm_w=g_small["k_norm_w"], w_out=g_out,
             norm2_w=g_small["norm2_w"], w_up=g_up, conv_w=g_conv_w, conv_b=g_small["conv_b"], w_down=g_down)

    updated = {early[i]: res for i, res in net.joined.items()}
    updated["w_ada"] = (ada[1], ada[2], ada[3], ada[0])
    grads, deltas, new_m, new_v = [], [], [], []
    for n in names:
        strip = (lambda t: t[0]) if n in lead else (lambda t: t)
        wrap = (lambda t: t[None]) if n in lead else (lambda t: t)
        g_n = g[n]
        if n in updated:
            d_n, m_n, v_n, g_n = updated[n]
        elif n == "w_in":
            d_n, m_n, v_n, g_n = _adamw(strip(w[n]), g_n, strip(m[n]), strip(v[n]), f"adamw_{n}", copy_grad=True)
        else:
            d_n, m_n, v_n = _adamw(strip(w[n]), g_n, strip(m[n]), strip(v[n]), f"adamw_{n}")
        grads.append(wrap(g_n))
        deltas.append(wrap(d_n))
        new_m.append(wrap(m_n))
        new_v.append(wrap(v_n))
    return (loss, grad_x[None], *grads, *deltas, *new_m, *new_v)
```

```python
import functools

import jax
import jax.numpy as jnp
from jax import lax
from jax.experimental import pallas as pl
from jax.experimental.pallas import tpu as pltpu
from jax.experimental.pallas import tpu_sc as plsc

F32 = jnp.float32
BF16 = jnp.bfloat16

S = 2048
D = 2048
H = 8
DH = 128
HW = H * DH
CH = 64
NCH = S // CH
DFF = 5632
INC = 7 * HW
BLK = 128
DILS = (1, 4, 16)
EPS = 1e-6
NEG = -1e30
N_CHIPS = 4
N_DEV = 8

ADAM_LR = 0.001
ADAM_B1 = 0.9
ADAM_B2 = 0.999
ADAM_EPS = 1e-08
ADAM_WD = 0.01
ADAM_STEP = 10
ADAM_BLOCK_BYTES = 1 << 21

V7X_VMEM_BYTES = 64 * 1024 * 1024
VMEM_LIMIT = (V7X_VMEM_BYTES * 3) // 4
MESH = pl.DeviceIdType.MESH
HBM_SPEC = pl.BlockSpec(memory_space=pltpu.HBM)
VMEM_SPEC = pl.BlockSpec(memory_space=pltpu.VMEM)

NN = (((1,), (0,)), ((), ()))
NT = (((1,), (1,)), ((), ()))
TN = (((0,), (0,)), ((), ()))


def _params(n_grid):
    return pltpu.CompilerParams(dimension_semantics=("arbitrary",) * n_grid, vmem_limit_bytes=VMEM_LIMIT)


def _dot(a, b, dims=NN):
    return lax.dot_general(a.astype(BF16), b.astype(BF16), dims, preferred_element_type=F32)


def _dot_f32(a, b):
    return lax.dot_general(a, b, NN, precision=lax.Precision.HIGHEST, preferred_element_type=F32)


def _sigmoid(x):
    return 1.0 / (1.0 + jnp.exp(-x))


def _pick(n, cands):
    for t in cands:
        if n % t == 0:
            return t
    raise ValueError(n)


def _matmul(a, b, mode, out_dtype, name, phase=None, m_blocks=None):
    if mode == "nn":
        (m, k), (_, n) = a.shape, b.shape
    elif mode == "nt":
        (m, k), (n, _) = a.shape, b.shape
    else:
        (k, m), (_, n) = a.shape, b.shape
    tm = _pick(m, (1024, 1408, 512))
    a_block = lambda i: i
    if m_blocks is not None:
        tm, blocks = m_blocks
        m = tm * len(blocks)
        step = blocks[1] - blocks[0] if len(blocks) > 1 else 0
        assert all(blk == blocks[0] + step * i for i, blk in enumerate(blocks))
        a_block = lambda i: blocks[0] + step * i
    tn = _pick(n, (1024, 1408, 512))
    tk = _pick(k, (2048, 2816, 1792, 1024, 512))
    nk = k // tk
    dims = {"nn": NN, "nt": NT, "tn": TN}[mode]

    def body(a_ref, b_ref, o_ref, *acc):
        part = lax.dot_general(a_ref[...], b_ref[...], dims, preferred_element_type=F32)
        if nk == 1:
            o_ref[...] = part.astype(o_ref.dtype)
            return
        acc_ref, kk = acc[0], pl.program_id(2)

        @pl.when(kk == 0)
        def _():
            acc_ref[...] = part

        @pl.when(jnp.logical_and(kk > 0, kk < nk - 1))
        def _():
            acc_ref[...] += part

        @pl.when(kk == nk - 1)
        def _():
            o_ref[...] = (acc_ref[...] + part).astype(o_ref.dtype)

    if mode == "tn":
        a_spec = pl.BlockSpec((tk, tm), lambda i, j, kk: (kk, a_block(i)))
    else:
        a_spec = pl.BlockSpec((tm, tk), lambda i, j, kk: (i, kk))
    if mode == "nt":
        b_spec = pl.BlockSpec((tn, tk), lambda i, j, kk: (j, kk))
    else:
        b_spec = pl.BlockSpec((tk, tn), lambda i, j, kk: (kk, j))
    return _pcall(
        body, [a, b], phase, name=name,
        out_shape=jax.ShapeDtypeStruct((m, n), out_dtype),
        grid=(m // tm, n // tn, nk),
        in_specs=[a_spec, b_spec],
        out_specs=pl.BlockSpec((tm, tn), lambda i, j, kk: (i, j)),
        scratch_shapes=[pltpu.VMEM((tm, tn), F32)] if nk > 1 else [],
        compiler_params=_params(3),
    )


TR = 256


def _row_spec(cols=D):
    return pl.BlockSpec((TR, cols), lambda i: (i, 0))


def _vec_spec(cols=D):
    return pl.BlockSpec((1, cols), lambda i: (0, 0))


def _norm_mod(x, nw, scale, shift, name):
    def body(x_ref, nw_ref, sc_ref, sh_ref, h_ref):
        xv = x_ref[...]
        r = lax.rsqrt(jnp.mean(xv * xv, axis=-1, keepdims=True) + EPS)
        h_ref[...] = ((xv * r) * nw_ref[...] * (1.0 + sc_ref[...]) + sh_ref[...]).astype(BF16)

    return pl.pallas_call(
        body, name=name, out_shape=jax.ShapeDtypeStruct((S, D), BF16), grid=(S // TR,),
        in_specs=[_row_spec(), _vec_spec(), _vec_spec(), _vec_spec()], out_specs=_row_spec(),
        compiler_params=_params(1),
    )(x, nw, scale, shift)


def _resid_norm_mod(x, mix, gate, nw, scale, shift, name):
    def body(x_ref, mix_ref, g_ref, nw_ref, sc_ref, sh_ref, x1_ref, h_ref):
        xv = x_ref[...] + g_ref[...] * mix_ref[...]
        x1_ref[...] = xv
        r = lax.rsqrt(jnp.mean(xv * xv, axis=-1, keepdims=True) + EPS)
        h_ref[...] = ((xv * r) * nw_ref[...] * (1.0 + sc_ref[...]) + sh_ref[...]).astype(BF16)

    return pl.pallas_call(
        body, name=name,
        out_shape=(jax.ShapeDtypeStruct((S, D), F32), jax.ShapeDtypeStruct((S, D), BF16)), grid=(S // TR,),
        in_specs=[_row_spec(), _row_spec(), _vec_spec(), _vec_spec(), _vec_spec(), _vec_spec()],
        out_specs=(_row_spec(), _row_spec()),
        compiler_params=_params(1),
    )(x, mix, gate, nw, scale, shift)


def _norm_mod_bwd(dh, xin, nw, scale, dres, name, mix=None, gate=None, phase=None):
    with_gate = mix is not None

    def body(*refs):
        if with_gate:
            dh_ref, x_ref, nw_ref, sc_ref, dres_ref, mix_ref, g_ref, dx_ref, dsh_ref, dsc_ref, dnw_ref, dg_ref, dmix_ref = refs
        else:
            dh_ref, x_ref, nw_ref, sc_ref, dres_ref, dx_ref, dsh_ref, dsc_ref, dnw_ref = refs
        i = pl.program_id(0)
        dhv = dh_ref[...]
        xv = x_ref[...]
        r = lax.rsqrt(jnp.mean(xv * xv, axis=-1, keepdims=True) + EPS)
        xn = xv * r
        nwv = nw_ref[...]
        one_sc = 1.0 + sc_ref[...]
        dxn = dhv * nwv * one_sc
        dx = dres_ref[...] + r * (dxn - xn * jnp.mean(dxn * xn, axis=-1, keepdims=True))
        dx_ref[...] = dx

        @pl.when(i == 0)
        def _():
            dsh_ref[...] = jnp.zeros_like(dsh_ref)
            dsc_ref[...] = jnp.zeros_like(dsc_ref)
            dnw_ref[...] = jnp.zeros_like(dnw_ref)
            if with_gate:
                dg_ref[...] = jnp.zeros_like(dg_ref)

        dsh_ref[...] += jnp.sum(dhv, axis=0, keepdims=True)
        dsc_ref[...] += jnp.sum(dhv * xn * nwv, axis=0, keepdims=True)
        dnw_ref[...] += jnp.sum(dhv * xn * one_sc, axis=0, keepdims=True)
        if with_gate:
            dg_ref[...] += jnp.sum(dx * mix_ref[...], axis=0, keepdims=True)
            dmix_ref[...] = (dx * g_ref[...]).astype(BF16)

    vec = jax.ShapeDtypeStruct((1, D), F32)
    ins = [dh, xin, nw, scale, dres]
    in_specs = [_row_spec(), _row_spec(), _vec_spec(), _vec_spec(), _row_spec()]
    outs = [jax.ShapeDtypeStruct((S, D), F32), vec, vec, vec]
    out_specs = [_row_spec(), _vec_spec(), _vec_spec(), _vec_spec()]
    if with_gate:
        ins += [mix, gate]
        in_specs += [_row_spec(), _vec_spec()]
        outs += [vec, jax.ShapeDtypeStruct((S, D), BF16)]
        out_specs += [_vec_spec(), _row_spec()]
    return _pcall(
        body, ins, phase, name=name, out_shape=tuple(outs), grid=(S // TR,), in_specs=in_specs,
        out_specs=tuple(out_specs), compiler_params=_params(1),
    )


def _tri(lower):
    row = lax.broadcasted_iota(jnp.int32, (CH, CH), 0)
    col = lax.broadcasted_iota(jnp.int32, (CH, CH), 1)
    return (row >= col) if lower else (col >= row)


def _hgrn_gates(hq, hf, lb):
    sig = _sigmoid(hf)
    f = lb + (1.0 - lb) * sig
    g = jnp.log(f)
    sq = _sigmoid(hq)
    return sig, f, g, 1.0 - f, hq * sq, sq


HG_HP = 2
HG_UNROLL = 8


def _proj_col_spec(group):
    return pl.BlockSpec((S, HG_HP * DH), lambda h: (0, group * (H // HG_HP) + h))


def _hgrn_fwd(proj, lb_logits, norm_w, phase=None):
    def body(hq_ref, hf_ref, hi_ref, hg_ref, lbl_ref, nw_ref, out_ref, oraw_ref, st_ref, s_ref):
        nw = nw_ref[...]
        lower = _tri(True)
        ltri = lower.astype(F32)
        s_ref[...] = jnp.zeros_like(s_ref)

        def chunk(n, carry):
            rows = pl.ds(pl.multiple_of(n * CH, CH), CH)
            for j in range(HG_HP):
                cols = slice(j * DH, (j + 1) * DH)
                lb = 1.0 / (1.0 + jnp.exp(lbl_ref[1:2, cols] - lbl_ref[0:1, cols]))
                hq, hf, v, hg = hq_ref[rows, cols], hf_ref[rows, cols], hi_ref[rows, cols], hg_ref[rows, cols]
                _, _, g, kk, q, _ = _hgrn_gates(hq, hf, lb)
                gc = _dot_f32(ltri, g)
                gl = jnp.sum(g, axis=0, keepdims=True)
                qe = q * jnp.exp(gc)
                ke = kk * jnp.exp(gl - gc)
                qh = q * jnp.exp(gc - 0.5 * gl)
                kh = kk * jnp.exp(0.5 * gl - gc)
                st = s_ref[j]
                st_ref[j, pl.ds(n, 1)] = st[None]
                a = jnp.where(lower, _dot(qh, kh, NT), 0.0)
                o = _dot(qe, st, NT) + _dot(a, v)
                s_ref[j] = st * jnp.exp(gl) + _dot(v, ke, TN)
                oraw_ref[rows, cols] = o
                rs = lax.rsqrt(jnp.mean(o * o, axis=-1, keepdims=True) + EPS)
                out_ref[rows, cols] = (o * rs * nw * (hg * _sigmoid(hg))).astype(BF16)
            return carry

        lax.fori_loop(0, NCH, chunk, 0, unroll=HG_UNROLL)

    head_spec = pl.BlockSpec((S, HG_HP * DH), lambda h: (0, h))
    return _pcall(
        body, [proj, proj, proj, proj, lb_logits, norm_w], phase, name="hgrn_fwd",
        out_shape=(jax.ShapeDtypeStruct((S, 2 * HW), BF16), jax.ShapeDtypeStruct((S, HW), F32),
                   jax.ShapeDtypeStruct((H, NCH, DH, DH), F32)),
        grid=(H // HG_HP,),
        in_specs=[_proj_col_spec(0), _proj_col_spec(1), _proj_col_spec(2), _proj_col_spec(3),
                  pl.BlockSpec((2, HG_HP * DH), lambda h: (0, h)), pl.BlockSpec((1, DH), lambda h: (0, 0))],
        out_specs=(head_spec, head_spec, pl.BlockSpec((HG_HP, NCH, DH, DH), lambda h: (h, 0, 0, 0))),
        scratch_shapes=[pltpu.VMEM((HG_HP, DH, DH), F32)],
        compiler_params=_params(1),
    )


def _hgrn_bwd(proj, lb_logits, norm_w, oraw, states, dout, phase=None):
    def body(hq_ref, hf_ref, hi_ref, hg_ref, lbl_ref, nw_ref, oraw_ref, st_ref, do_ref,
             dhq_ref, dhf_ref, dhi_ref, dhg_ref, dlb_ref, dnw_ref, ds_ref, acc_ref):
        h = pl.program_id(0)
        nw = nw_ref[...]
        lower = _tri(True)
        ltri = lower.astype(F32)
        utri = _tri(False).astype(F32)
        last = lax.broadcasted_iota(jnp.int32, (CH, DH), 0) == CH - 1
        ds_ref[...] = jnp.zeros_like(ds_ref)
        acc_ref[...] = jnp.zeros_like(acc_ref)

        @pl.when(h == 0)
        def _():
            dnw_ref[...] = jnp.zeros_like(dnw_ref)

        def chunk(i, carry):
            n = NCH - 1 - i
            rows = pl.ds(pl.multiple_of(n * CH, CH), CH)
            for j in range(HG_HP):
                cols = slice(j * DH, (j + 1) * DH)
                lb = 1.0 / (1.0 + jnp.exp(lbl_ref[1:2, cols] - lbl_ref[0:1, cols]))
                hq, hf, v, hg = hq_ref[rows, cols], hf_ref[rows, cols], hi_ref[rows, cols], hg_ref[rows, cols]
                sig, f, g, kk, q, sq = _hgrn_gates(hq, hf, lb)
                gc = _dot_f32(ltri, g)
                gl = jnp.sum(g, axis=0, keepdims=True)
                eg = jnp.exp(gc)
                ek = jnp.exp(gl - gc)
                gam = jnp.exp(gl)
                ph = jnp.exp(gc - 0.5 * gl)
                pk = jnp.exp(0.5 * gl - gc)
                qe, ke = q * eg, kk * ek
                qh, kh = q * ph, kk * pk
                st = st_ref[j, pl.ds(n, 1)][0]
                dst = ds_ref[j]
                a = jnp.where(lower, _dot(qh, kh, NT), 0.0)
                o = oraw_ref[rows, cols]
                rs = lax.rsqrt(jnp.mean(o * o, axis=-1, keepdims=True) + EPS)
                xh = o * rs
                sg = _sigmoid(hg)
                dgo = do_ref[rows, cols]
                d_on = dgo * (hg * sg)
                dhg_ref[rows, cols] = (dgo * xh * nw * (sg * (1.0 + hg * (1.0 - sg)))).astype(BF16)
                acc_ref[j, 1:2, :] += jnp.sum(d_on * xh, axis=0, keepdims=True)
                dy = d_on * nw
                do = rs * (dy - xh * jnp.mean(dy * xh, axis=-1, keepdims=True))
                dqe = _dot(do, st)
                da = jnp.where(lower, _dot(do, v, NT), 0.0)
                dv = _dot(a, do, TN) + _dot(ke, dst, NT)
                dke = _dot(v, dst)
                dgam = jnp.sum(dst * st, axis=0, keepdims=True)
                dqh = lax.dot_general(da, kh, NN, precision=lax.Precision.HIGH, preferred_element_type=F32)
                dkh = lax.dot_general(da, qh, TN, precision=lax.Precision.HIGH, preferred_element_type=F32)
                dq = dqh * ph + dqe * eg
                dk = dkh * pk + dke * ek
                dgl = jnp.sum(dke * ke, axis=0, keepdims=True) + dgam * gam
                dgc = dqh * qh - dkh * kh + dqe * qe - dke * ke + jnp.where(last, dgl, 0.0)
                dg = _dot_f32(utri, dgc)
                df = dg / f - dk
                dhf_ref[rows, cols] = (df * (1.0 - lb) * sig * (1.0 - sig)).astype(BF16)
                acc_ref[j, 0:1, :] += jnp.sum(df * (1.0 - sig), axis=0, keepdims=True)
                dhq_ref[rows, cols] = (dq * (sq * (1.0 + hq * (1.0 - sq)))).astype(BF16)
                dhi_ref[rows, cols] = dv.astype(BF16)
                ds_ref[j] = dst * gam + _dot(do, qe, TN)
            return carry

        lax.fori_loop(0, NCH, chunk, 0, unroll=HG_UNROLL)
        for j in range(HG_HP):
            dlb_ref[:, j * DH:(j + 1) * DH] = acc_ref[j, 0:1, :]
            dnw_ref[...] += acc_ref[j, 1:2, :]

    head_spec = pl.BlockSpec((S, HG_HP * DH), lambda h: (0, h))
    head_out = jax.ShapeDtypeStruct((S, HW), BF16)
    return _pcall(
        body, [proj, proj, proj, proj, lb_logits, norm_w, oraw, states, dout], phase, name="hgrn_bwd",
        out_shape=(head_out, head_out, head_out, head_out,
                   jax.ShapeDtypeStruct((1, HW), F32), jax.ShapeDtypeStruct((1, DH), F32)),
        grid=(H // HG_HP,),
        in_specs=[_proj_col_spec(0), _proj_col_spec(1), _proj_col_spec(2), _proj_col_spec(3),
                  pl.BlockSpec((2, HG_HP * DH), lambda h: (0, h)), pl.BlockSpec((1, DH), lambda h: (0, 0)),
                  head_spec, pl.BlockSpec((HG_HP, NCH, DH, DH), lambda h: (h, 0, 0, 0)), head_spec],
        out_specs=(head_spec, head_spec, head_spec, head_spec,
                   pl.BlockSpec((1, HG_HP * DH), lambda h: (0, h)), pl.BlockSpec((1, DH), lambda h: (0, 0))),
        scratch_shapes=[pltpu.VMEM((HG_HP, DH, DH), F32), pltpu.VMEM((HG_HP, 8, DH), F32)],
        compiler_params=_params(1),
    )


ATT_SCALE = DH ** -0.5
HEADS_PER_STEP = {1: 8, 4: 1, 16: 1}


def _sub_rows(ref, r, dil, cols):
    if dil == 1:
        return ref[:, cols]
    return ref[pl.ds(r, BLK, stride=dil), cols]


def _set_sub_rows(ref, r, dil, cols, val):
    if dil == 1:
        ref[:, cols] = val
    else:
        ref[pl.ds(r, BLK, stride=dil), cols] = val


def _slopes(dil):
    hp = HEADS_PER_STEP[dil]
    s = jnp.asarray([dil * 2.0 ** (-(h + 1)) for h in range(H)], F32)
    return jnp.broadcast_to(s[:, None], (H, DH)).reshape(H // hp, hp, DH)


def _rms_rows(x):
    rs = lax.rsqrt(jnp.mean(x * x, axis=-1, keepdims=True) + EPS)
    return x * rs, rs


def _att_masks():
    qi = lax.broadcasted_iota(jnp.int32, (BLK, BLK), 0)
    kj = lax.broadcasted_iota(jnp.int32, (BLK, BLK), 1)
    steps_c = qi - kj
    steps_p = qi - kj + BLK
    return steps_c, steps_p, steps_c >= 0, steps_p <= BLK


def _qk_prep(proj, q_w, k_w):
    def body(q_ref, k_ref, qw_ref, kw_ref, qn_ref, kn_ref):
        for h in range(H):
            cols = slice(h * DH, (h + 1) * DH)
            qn_ref[:, cols] = _rms_rows(q_ref[:, cols])[0] * qw_ref[...]
            kn_ref[:, cols] = _rms_rows(k_ref[:, cols])[0] * kw_ref[...]

    out = jax.ShapeDtypeStruct((S, HW), F32)
    wspec = pl.BlockSpec((1, DH), lambda i: (0, 0))
    return pl.pallas_call(
        body, name="qk_prep", out_shape=(out, out), grid=(S // TR,),
        in_specs=[pl.BlockSpec((TR, HW), lambda i: (i, 4)), pl.BlockSpec((TR, HW), lambda i: (i, 5)), wspec, wspec],
        out_specs=(_row_spec(HW), _row_spec(HW)),
        compiler_params=_params(1),
    )(proj, proj, q_w, k_w)


def _qk_norm_bwd(proj, dqn, dkn, dv, q_w, k_w):
    def body(q_ref, k_ref, dqn_ref, dkn_ref, dv_ref, qw_ref, kw_ref, dq_ref, dk_ref, dvo_ref, dqw_ref, dkw_ref):
        i = pl.program_id(0)

        @pl.when(i == 0)
        def _():
            dqw_ref[...] = jnp.zeros_like(dqw_ref)
            dkw_ref[...] = jnp.zeros_like(dkw_ref)

        dvo_ref[...] = dv_ref[...].astype(BF16)
        for x_ref, d_ref, w_ref, o_ref, dw_ref in ((q_ref, dqn_ref, qw_ref, dq_ref, dqw_ref),
                                                   (k_ref, dkn_ref, kw_ref, dk_ref, dkw_ref)):
            for h in range(H):
                cols = slice(h * DH, (h + 1) * DH)
                xh, rs = _rms_rows(x_ref[:, cols])
                d = d_ref[:, cols]
                dw_ref[...] += jnp.sum(d * xh, axis=0, keepdims=True)
                dy = d * w_ref[...]
                o_ref[:, cols] = (rs * (dy - xh * jnp.mean(dy * xh, axis=-1, keepdims=True))).astype(BF16)

    big = jax.ShapeDtypeStruct((S, HW), BF16)
    small = jax.ShapeDtypeStruct((1, DH), F32)
    wspec = pl.BlockSpec((1, DH), lambda i: (0, 0))
    return pl.pallas_call(
        body, name="qk_norm_bwd", out_shape=(big, big, big, small, small), grid=(S // TR,),
        in_specs=[pl.BlockSpec((TR, HW), lambda i: (i, 4)), pl.BlockSpec((TR, HW), lambda i: (i, 5)),
                  _row_spec(HW), _row_spec(HW), _row_spec(HW), wspec, wspec],
        out_specs=(_row_spec(HW), _row_spec(HW), _row_spec(HW), wspec, wspec),
        compiler_params=_params(1),
    )(proj, proj, dqn, dkn, dv, q_w, k_w)


def _attn_delta(do, o):
    def body(do_ref, o_ref, d_ref):
        for h in range(H):
            cols = slice(h * DH, (h + 1) * DH)
            d_ref[:, cols] = jnp.broadcast_to(jnp.sum(do_ref[:, cols] * o_ref[:, cols], axis=-1, keepdims=True), (TR, DH))

    return pl.pallas_call(
        body, name="attn_delta", out_shape=jax.ShapeDtypeStruct((S, HW), F32), grid=(S // TR,),
        in_specs=[pl.BlockSpec((TR, HW), lambda i: (i, 1)), _row_spec(HW)], out_specs=_row_spec(HW),
        compiler_params=_params(1),
    )(do, o)


def _attn_fwd(proj, qn, kn, dil, phase=None):
    hp = HEADS_PER_STEP[dil]
    rows, ng, nb = BLK * dil, H // hp, S // (BLK * dil)

    def body(q_ref, kc_ref, kp_ref, vc_ref, vp_ref, sl_ref, o_ref, lse_ref):
        n = pl.program_id(0)
        steps_c, steps_p, ok_c, ok_p = _att_masks()
        ok_p = jnp.logical_and(ok_p, n > 0)
        fc, fp = steps_c.astype(F32), steps_p.astype(F32)
        for hh in range(hp):
            cols = slice(hh * DH, (hh + 1) * DH)
            sl = sl_ref[0, hh:hh + 1, :]
            for r in range(dil):
                sub = lambda ref: _sub_rows(ref, r, dil, cols)
                qv = sub(q_ref)
                sc = jnp.where(ok_c, _dot(qv, sub(kc_ref), NT) * ATT_SCALE - sl * fc, NEG)
                sp = jnp.where(ok_p, _dot(qv, sub(kp_ref), NT) * ATT_SCALE - sl * fp, NEG)
                m = jnp.maximum(jnp.max(sc, axis=-1, keepdims=True), jnp.max(sp, axis=-1, keepdims=True))
                pc, pp = jnp.exp(sc - m), jnp.exp(sp - m)
                den = jnp.sum(pc, axis=-1, keepdims=True) + jnp.sum(pp, axis=-1, keepdims=True)
                o = (_dot(pc, sub(vc_ref)) + _dot(pp, sub(vp_ref))) / den
                _set_sub_rows(o_ref, r, dil, cols, o)
                _set_sub_rows(lse_ref, r, dil, cols, jnp.broadcast_to(m + jnp.log(den), (BLK, DH)))

    cur = pl.BlockSpec((rows, hp * DH), lambda n, g: (n, g))
    prev = pl.BlockSpec((rows, hp * DH), lambda n, g: (jnp.maximum(n - 1, 0), g))
    vcur = pl.BlockSpec((rows, hp * DH), lambda n, g: (n, 6 * ng + g))
    vprev = pl.BlockSpec((rows, hp * DH), lambda n, g: (jnp.maximum(n - 1, 0), 6 * ng + g))
    out = jax.ShapeDtypeStruct((S, HW), F32)
    return _pcall(
        body, [qn, kn, kn, proj, proj, _slopes(dil)], phase,
        name=f"attn_fwd_d{dil}", out_shape=(out, out), grid=(nb, ng),
        in_specs=[cur, cur, prev, vcur, vprev, pl.BlockSpec((1, hp, DH), lambda n, g: (g, 0, 0))],
        out_specs=(cur, cur),
        compiler_params=_params(2),
    )


def _attn_merge(outs, lses, cat):
    def body(o1, o2, o3, l1, l2, l3, cat_ref, ob_ref, of_ref, lse_ref):
        a, b, c = l1[...], l2[...], l3[...]
        m = jnp.maximum(jnp.maximum(a, b), c)
        ea, eb, ec = jnp.exp(a - m), jnp.exp(b - m), jnp.exp(c - m)
        den = ea + eb + ec
        o = (ea * o1[...] + eb * o2[...] + ec * o3[...]) / den
        ob_ref[...] = o.astype(BF16)
        of_ref[...] = o
        lse_ref[...] = m + jnp.log(den)

    f = jax.ShapeDtypeStruct((S, HW), F32)
    return pl.pallas_call(
        body, name="attn_merge", out_shape=(jax.ShapeDtypeStruct((S, 2 * HW), BF16), f, f), grid=(S // TR,),
        in_specs=[_row_spec(HW)] * 6 + [pl.BlockSpec(memory_space=pl.ANY)],
        out_specs=(pl.BlockSpec((TR, HW), lambda i: (i, 1)), _row_spec(HW), _row_spec(HW)),
        input_output_aliases={6: 0},
        compiler_params=_params(1),
    )(*outs, *lses, cat)


def _attn_bwd(proj, qn, kn, lse, delta, do, dil, acc, phase=None):
    hp = HEADS_PER_STEP[dil]
    rows, ng, nb = BLK * dil, H // hp, S // (BLK * dil)
    has_acc = acc is not None

    def body(*refs):
        q_ref, qn_ref, kp_ref, kc_ref, vp_ref, vc_ref, l_ref, ln_ref, d_ref, dn_ref, do_ref, don_ref, sl_ref = refs[:13]
        refs = refs[13:]
        if has_acc:
            aq_ref, ak_ref, av_ref = refs[:3]
            refs = refs[3:]
        dq_ref, dk_ref, dv_ref = refs
        m = pl.program_id(0)
        steps_c, steps_p, ok_c, ok_p = _att_masks()
        ok_prev = jnp.logical_and(ok_p, m > 0)
        ok_next = jnp.logical_and(ok_p, m < nb - 1)
        fc, fp = steps_c.astype(F32), steps_p.astype(F32)
        for hh in range(hp):
            cols = slice(hh * DH, (hh + 1) * DH)
            sl = sl_ref[0, hh:hh + 1, :]
            for r in range(dil):
                sub = lambda ref: _sub_rows(ref, r, dil, cols)
                qv, qnv, kcv, kpv = sub(q_ref), sub(qn_ref), sub(kc_ref), sub(kp_ref)
                vc, vp = sub(vc_ref), sub(vp_ref)
                dov, donv = sub(do_ref), sub(don_ref)
                lse_m, lse_n = sub(l_ref)[:, 0:1], sub(ln_ref)[:, 0:1]
                delta_m, delta_n = sub(d_ref)[:, 0:1], sub(dn_ref)[:, 0:1]
                p_c = jnp.where(ok_c, jnp.exp(_dot(qv, kcv, NT) * ATT_SCALE - sl * fc - lse_m), 0.0)
                p_p = jnp.where(ok_prev, jnp.exp(_dot(qv, kpv, NT) * ATT_SCALE - sl * fp - lse_m), 0.0)
                p_n = jnp.where(ok_next, jnp.exp(_dot(qnv, kcv, NT) * ATT_SCALE - sl * fp - lse_n), 0.0)
                ds_c = p_c * (_dot(dov, vc, NT) - delta_m)
                ds_p = p_p * (_dot(dov, vp, NT) - delta_m)
                ds_n = p_n * (_dot(donv, vc, NT) - delta_n)
                dqn = (_dot(ds_c, kcv) + _dot(ds_p, kpv)) * ATT_SCALE
                dkn = (_dot(ds_c, qv, TN) + _dot(ds_n, qnv, TN)) * ATT_SCALE
                dv = _dot(p_c, dov, TN) + _dot(p_n, donv, TN)
                if has_acc:
                    dqn, dkn, dv = dqn + sub(aq_ref), dkn + sub(ak_ref), dv + sub(av_ref)
                _set_sub_rows(dq_ref, r, dil, cols, dqn)
                _set_sub_rows(dk_ref, r, dil, cols, dkn)
                _set_sub_rows(dv_ref, r, dil, cols, dv)

    def shifted(shift, m):
        if shift < 0:
            return jnp.maximum(m - 1, 0)
        if shift > 0:
            return jnp.minimum(m + 1, nb - 1)
        return m

    def spec(shift, group=0):
        return pl.BlockSpec((rows, hp * DH), lambda m, g: (shifted(shift, m), group * ng + g))

    ins = [qn, qn, kn, kn, proj, proj, lse, lse, delta, delta, do, do, _slopes(dil)]
    in_specs = [spec(0), spec(1), spec(-1), spec(0), spec(-1, 6), spec(0, 6), spec(0), spec(1), spec(0), spec(1),
                spec(0, 1), spec(1, 1), pl.BlockSpec((1, hp, DH), lambda m, g: (g, 0, 0))]
    if has_acc:
        ins += list(acc)
        in_specs += [spec(0)] * 3
    big = jax.ShapeDtypeStruct((S, HW), F32)
    return _pcall(
        body, ins, phase, name=f"attn_bwd_d{dil}", out_shape=(big, big, big), grid=(nb, ng),
        in_specs=in_specs, out_specs=(spec(0),) * 3,
        compiler_params=_params(2),
    )


TC = 512
NCT = DFF // TC


def _shift_rows(a, k):
    rows = lax.broadcasted_iota(jnp.int32, a.shape, 0)
    rolled = pltpu.roll(a, k % S, 0)
    if k > 0:
        return jnp.where(rows >= k, rolled, 0.0)
    return jnp.where(rows < S + k, rolled, 0.0)


def _conv_pre(a, w_ref, b_ref):
    return b_ref[...] + w_ref[0:1, :] * _shift_rows(a, 2) + w_ref[1:2, :] * _shift_rows(a, 1) + w_ref[2:3, :] * a


def _conv_gate_fwd(u, conv_w, conv_b, phase=None):
    def body(a_ref, g_ref, w_ref, b_ref, y_ref):
        pre = _conv_pre(a_ref[...].astype(F32), w_ref, b_ref)
        y_ref[...] = (pre * _sigmoid(pre) * g_ref[...].astype(F32)).astype(BF16)

    return _pcall(
        body, [u, u, conv_w, conv_b], phase,
        name="conv_gate_fwd", out_shape=jax.ShapeDtypeStruct((S, DFF), BF16), grid=(NCT,),
        in_specs=[pl.BlockSpec((S, TC), lambda i: (0, i)), pl.BlockSpec((S, TC), lambda i: (0, NCT + i)),
                  pl.BlockSpec((3, TC), lambda i: (0, i)), pl.BlockSpec((1, TC), lambda i: (0, i))],
        out_specs=pl.BlockSpec((S, TC), lambda i: (0, i)),
        compiler_params=_params(1),
    )


def _conv_gate_bwd(dy, u, conv_w, conv_b, phase=None):
    def body(dy_ref, a_ref, g_ref, w_ref, b_ref, du_ref, db_ref, dw_ref, da_buf, dg_buf, sems):
        i = pl.program_id(0)
        slot = i % 2

        def writes(step, s):
            col = pl.multiple_of(step * TC, TC)
            return (pltpu.make_async_copy(da_buf.at[s], du_ref.at[:, pl.ds(col, TC)], sems.at[0, s]),
                    pltpu.make_async_copy(dg_buf.at[s], du_ref.at[:, pl.ds(DFF + col, TC)], sems.at[1, s]))

        @pl.when(i >= 2)
        def _():
            for cp in writes(i - 2, slot):
                cp.wait()

        a = a_ref[...].astype(F32)
        dyv = dy_ref[...].astype(F32)
        pre = _conv_pre(a, w_ref, b_ref)
        sg = _sigmoid(pre)
        dg_buf[slot] = (dyv * pre * sg).astype(BF16)
        dpre = dyv * g_ref[...].astype(F32) * (sg * (1.0 + pre * (1.0 - sg)))
        da = w_ref[2:3, :] * dpre + w_ref[1:2, :] * _shift_rows(dpre, -1) + w_ref[0:1, :] * _shift_rows(dpre, -2)
        da_buf[slot] = da.astype(BF16)
        for cp in writes(i, slot):
            cp.start()
        db_ref[...] = jnp.sum(dpre, axis=0, keepdims=True)
        dw_ref[0:1, :] = jnp.sum(dpre * _shift_rows(a, 2), axis=0, keepdims=True)
        dw_ref[1:2, :] = jnp.sum(dpre * _shift_rows(a, 1), axis=0, keepdims=True)
        dw_ref[2:3, :] = jnp.sum(dpre * a, axis=0, keepdims=True)

        @pl.when(i == NCT - 1)
        def _():
            for cp in writes(i - 1, 1 - slot) + writes(i, slot):
                cp.wait()

    col = pl.BlockSpec((S, TC), lambda i: (0, i))
    return _pcall(
        body, [dy, u, u, conv_w, conv_b], phase, name="conv_gate_bwd",
        out_shape=(jax.ShapeDtypeStruct((S, 2 * DFF), BF16), jax.ShapeDtypeStruct((1, DFF), F32),
                   jax.ShapeDtypeStruct((3, DFF), F32)),
        grid=(NCT,),
        in_specs=[col, col, pl.BlockSpec((S, TC), lambda i: (0, NCT + i)),
                  pl.BlockSpec((3, TC), lambda i: (0, i)), pl.BlockSpec((1, TC), lambda i: (0, i))],
        out_specs=(pl.BlockSpec(memory_space=pl.ANY), pl.BlockSpec((1, TC), lambda i: (0, i)),
                   pl.BlockSpec((3, TC), lambda i: (0, i))),
        scratch_shapes=[pltpu.VMEM((2, S, TC), BF16), pltpu.VMEM((2, S, TC), BF16), pltpu.SemaphoreType.DMA((2, 2))],
        compiler_params=_params(1),
    )


def _out_loss(z, x1, target, gate):
    def body(z_ref, x_ref, t_ref, g_ref, do_ref, dz_ref, dg_ref, loss_ref):
        i = pl.program_id(0)
        zv = z_ref[...]
        gv = g_ref[...]
        err = x_ref[...] + gv * zv - t_ref[...]
        dout = err * (1.0 / D)
        do_ref[...] = dout
        dz_ref[...] = (dout * gv).astype(BF16)

        @pl.when(i == 0)
        def _():
            dg_ref[...] = jnp.zeros_like(dg_ref)
            loss_ref[...] = jnp.zeros_like(loss_ref)

        dg_ref[...] += jnp.sum(dout * zv, axis=0, keepdims=True)
        part = jnp.sum(jnp.sum(err * err, axis=0, keepdims=True), axis=-1, keepdims=True) * (0.5 / D)
        lane = lax.broadcasted_iota(jnp.int32, (1, 128), 1)
        loss_ref[...] += jnp.where(lane == 0, part, 0.0)

    return pl.pallas_call(
        body, name="out_loss",
        out_shape=(jax.ShapeDtypeStruct((S, D), F32), jax.ShapeDtypeStruct((S, D), BF16),
                   jax.ShapeDtypeStruct((1, D), F32), jax.ShapeDtypeStruct((1, 128), F32)),
        grid=(S // TR,),
        in_specs=[_row_spec(), _row_spec(), _row_spec(), _vec_spec()],
        out_specs=(_row_spec(), _row_spec(), _vec_spec(), _vec_spec(128)),
        compiler_params=_params(1),
    )(z, x1, target, gate)


ADA_COLS = 6 * D // N_CHIPS
TA = 512


def _ada_fwd(c_all, w_shard, b_shard):
    def body(c_ref, w_ref, b_ref, o_ref):
        cv = c_ref[...]
        o_ref[...] = _dot_f32(cv * _sigmoid(cv), w_ref[...]) + b_ref[...]

    return pl.pallas_call(
        body, name="ada_fwd", out_shape=jax.ShapeDtypeStruct((N_DEV, ADA_COLS), F32), grid=(ADA_COLS // TA,),
        in_specs=[pl.BlockSpec((N_DEV, D), lambda j: (0, 0)), pl.BlockSpec((D, TA), lambda j: (0, j)),
                  pl.BlockSpec((1, TA), lambda j: (0, j))],
        out_specs=pl.BlockSpec((N_DEV, TA), lambda j: (0, j)),
        compiler_params=_params(1),
    )(c_all, w_shard, b_shard)


ADA_ROWS = 128


def _ada_bwd_adamw(c_all, dmod_shard, w, m, v):
    def body(c_ref, d_ref, w_ref, m_ref, v_ref, g_ref, dl_ref, mo_ref, vo_ref):
        cv = c_ref[...]
        gv = lax.dot_general(cv * _sigmoid(cv), d_ref[...], TN, precision=lax.Precision.HIGHEST,
                             preferred_element_type=F32)
        g_ref[...] = gv
        m2 = ADAM_B1 * m_ref[...] + (1.0 - ADAM_B1) * gv
        v2 = ADAM_B2 * v_ref[...] + (1.0 - ADAM_B2) * (gv * gv)
        m_hat = m2 / (1.0 - ADAM_B1 ** ADAM_STEP)
        v_hat = v2 / (1.0 - ADAM_B2 ** ADAM_STEP)
        dl_ref[...] = -ADAM_LR * (m_hat / (jnp.sqrt(v_hat) + ADAM_EPS) + ADAM_WD * w_ref[...])
        mo_ref[...] = m2
        vo_ref[...] = v2

    spec = pl.BlockSpec((ADA_ROWS, ADA_COLS), lambda i: (i, 0))
    out = jax.ShapeDtypeStruct((D, ADA_COLS), F32)
    return pl.pallas_call(
        body, name="ada_bwd_adamw", out_shape=(out,) * 4, grid=(D // ADA_ROWS,),
        in_specs=[pl.BlockSpec((N_DEV, ADA_ROWS), lambda i: (0, i)), pl.BlockSpec((N_DEV, ADA_COLS), lambda i: (0, 0)),
                  spec, spec, spec],
        out_specs=(spec,) * 4,
        compiler_params=_params(1),
    )(c_all, dmod_shard, w, m, v)


def _sum_rows(g, name):
    rows, n = g.shape

    def body(g_ref, o_ref):
        acc = g_ref[0:1, :]
        for i in range(1, rows):
            acc = acc + g_ref[i:i + 1, :]
        o_ref[...] = acc

    return pl.pallas_call(
        body, name=name, out_shape=jax.ShapeDtypeStruct((1, n), F32),
        in_specs=[VMEM_SPEC], out_specs=VMEM_SPEC,
    )(g)


def _lb_grad(lb_logits, dlb):
    def body(l_ref, d_ref, o_ref):
        p = 1.0 / (1.0 + jnp.exp(l_ref[1:2, :] - l_ref[0:1, :]))
        t = d_ref[...] * p * (1.0 - p)
        o_ref[0:1, :] = t
        o_ref[1:2, :] = -t

    return pl.pallas_call(
        body, name="lb_grad", out_shape=jax.ShapeDtypeStruct((2, HW), F32),
        in_specs=[VMEM_SPEC, VMEM_SPEC], out_specs=VMEM_SPEC,
    )(lb_logits, dlb)


def _adamw(w, g, m, v, name, copy_grad=False):
    rows, cols = w.shape
    tr = rows
    if rows * cols * 4 > ADAM_BLOCK_BYTES:
        tr = _pick(rows, [t for t in (256, 128, 64, 32, 16, 8) if t * cols * 4 <= ADAM_BLOCK_BYTES])

    def body(w_ref, g_ref, m_ref, v_ref, d_ref, mo_ref, vo_ref, *go_ref):
        gv = g_ref[...]
        if copy_grad:
            go_ref[0][...] = gv
        m2 = ADAM_B1 * m_ref[...] + (1.0 - ADAM_B1) * gv
        v2 = ADAM_B2 * v_ref[...] + (1.0 - ADAM_B2) * (gv * gv)
        m_hat = m2 / (1.0 - ADAM_B1 ** ADAM_STEP)
        v_hat = v2 / (1.0 - ADAM_B2 ** ADAM_STEP)
        d_ref[...] = -ADAM_LR * (m_hat / (jnp.sqrt(v_hat) + ADAM_EPS) + ADAM_WD * w_ref[...])
        mo_ref[...] = m2
        vo_ref[...] = v2

    spec = pl.BlockSpec((tr, cols), lambda i: (i, 0))
    out = jax.ShapeDtypeStruct((rows, cols), F32)
    n_out = 4 if copy_grad else 3
    return pl.pallas_call(
        body, name=name, out_shape=(out,) * n_out, grid=(rows // tr,),
        in_specs=[spec] * 4, out_specs=(spec,) * n_out,
        compiler_params=_params(1),
    )(w, g, m, v)


SC_TILES = 32
SC_LANES = 16
SC_COL_PARTS = 2
SC_CHUNK_ROWS = 8


def _adamw_sc(w, g, m, v, name):
    rows, cols = w.shape
    band, part = rows // (SC_TILES // SC_COL_PARTS), cols // SC_COL_PARTS
    assert band % SC_CHUNK_ROWS == 0 and part % SC_LANES == 0, (rows, cols)

    def body(w_hbm, g_hbm, m_hbm, v_hbm, d_hbm, mo_hbm, vo_hbm, go_hbm, wb, gb, mb, vb, db):
        tile = lax.axis_index("sc_subcore") * 2 + lax.axis_index("sc_core")
        row0 = (tile // SC_COL_PARTS) * band
        col0 = (tile % SC_COL_PARTS) * part

        @pl.loop(0, band, step=SC_CHUNK_ROWS)
        def _(r):
            at = lambda ref: ref.at[pl.ds(row0 + r, SC_CHUNK_ROWS), pl.ds(col0, part)]
            pltpu.sync_copy(at(w_hbm), wb)
            pltpu.sync_copy(at(g_hbm), gb)
            pltpu.sync_copy(gb, at(go_hbm))
            pltpu.sync_copy(at(m_hbm), mb)
            pltpu.sync_copy(at(v_hbm), vb)

            @pl.loop(0, SC_CHUNK_ROWS)
            def _(i):
                @pl.loop(0, part, step=SC_LANES)
                def _(j):
                    sl = (i, pl.ds(j, SC_LANES))
                    gv = gb[sl]
                    m2 = ADAM_B1 * mb[sl] + (1.0 - ADAM_B1) * gv
                    v2 = ADAM_B2 * vb[sl] + (1.0 - ADAM_B2) * (gv * gv)
                    m_hat = m2 / (1.0 - ADAM_B1 ** ADAM_STEP)
                    v_hat = v2 / (1.0 - ADAM_B2 ** ADAM_STEP)
                    db[sl] = -ADAM_LR * (m_hat / (jnp.sqrt(v_hat) + ADAM_EPS) + ADAM_WD * wb[sl])
                    mb[sl] = m2
                    vb[sl] = v2

            pltpu.sync_copy(db, at(d_hbm))
            pltpu.sync_copy(mb, at(mo_hbm))
            pltpu.sync_copy(vb, at(vo_hbm))

    out = jax.ShapeDtypeStruct((rows, cols), F32)
    buf = pltpu.VMEM((SC_CHUNK_ROWS, part), F32)
    return pl.kernel(
        body, name=name, out_type=(out, out, out, out),
        mesh=plsc.VectorSubcoreMesh(core_axis_name="sc_core", subcore_axis_name="sc_subcore"),
        scratch_types=[buf] * 5,
    )(w, g, m, v)


W_IN, W_OUT, W_UP, W_DOWN = range(4)
IN_CHUNKS = 4
W_INQ = 4


def _sequence_step(x, target, mod, norm1_w, lb_logits, hg_norm_w, q_norm_w, k_norm_w, norm2_w, conv_w, conv_b, net):
    shift1, scale1, gate1, shift2, scale2, gate2 = [mod[:, i * D:(i + 1) * D] for i in range(6)]

    def run(name, fn, *args, **kw):
        phase = net.host(name)
        if phase is None:
            return fn(*args, **kw)
        res, comm = fn(*args, phase=phase, **kw)
        net.done(name, comm)
        return res

    h = _norm_mod(x, norm1_w, scale1, shift1, "norm1_fwd")
    proj = run("proj_fwd", _matmul, h, net.full[W_IN], "nn", F32, "proj_fwd")
    cat, o_raw, states = run("hgrn_fwd", _hgrn_fwd, proj, lb_logits, hg_norm_w)
    qn, kn = _qk_prep(proj, q_norm_w, k_norm_w)
    att = [run(f"attn_fwd_d{dil}", _attn_fwd, proj, qn, kn, dil) for dil in DILS]
    cat, b_out_f32, lse = _attn_merge([t[0] for t in att], [t[1] for t in att], cat)
    mix = run("mix_fwd", _matmul, cat, net.full[W_OUT], "nn", F32, "mix_fwd")
    x1, h2 = _resid_norm_mod(x, mix, gate1, norm2_w, scale2, shift2, "norm2_fwd")
    u = run("up_fwd", _matmul, h2, net.full[W_UP], "nn", BF16, "up_fwd")
    yact = run("conv_gate_fwd", _conv_gate_fwd, u, conv_w, conv_b)
    net.alone("before_down_fwd")
    z = _matmul(yact, net.full[W_DOWN], "nn", F32, "down_fwd")
    dout, dz, dgate2, loss_row = _out_loss(z, x1, target, gate2)

    net.grad[W_DOWN] = _matmul(yact, dz, "tn", BF16, "down_bwd_w")
    dyact = run("down_bwd_x", _matmul, dz, net.full[W_DOWN], "nt", BF16, "down_bwd_x")
    du, dconv_b, dconv_w = run("conv_gate_bwd", _conv_gate_bwd, dyact, u, conv_w, conv_b)
    net.grad[W_UP] = run("up_bwd_w", _matmul, h2, du, "tn", BF16, "up_bwd_w")
    dh2 = run("up_bwd_x", _matmul, du, net.full[W_UP], "nt", F32, "up_bwd_x")
    dx1, dshift2, dscale2, dnorm2, dgate1, dmix = run(
        "norm2_bwd", _norm_mod_bwd, dh2, x1, norm2_w, scale2, dout, "norm2_bwd", mix=mix, gate=gate1)
    net.grad[W_OUT] = run("mix_bwd_w", _matmul, cat, dmix, "tn", BF16, "mix_bwd_w")
    dcat = run("mix_bwd_x", _matmul, dmix, net.full[W_OUT], "nt", F32, "mix_bwd_x")
    dhq, dhf, dhi, dhg, dlb, dhg_norm = run("hgrn_bwd", _hgrn_bwd, proj, lb_logits, hg_norm_w, o_raw, states, dcat)
    delta = _attn_delta(dcat, b_out_f32)
    acc = None
    for dil in DILS:
        acc = run(f"attn_bwd_d{dil}", _attn_bwd, proj, qn, kn, lse, delta, dcat, dil, acc)
    daq, dak, dav, dq_norm, dk_norm = _qk_norm_bwd(proj, *acc, q_norm_w, k_norm_w)
    dproj = jnp.concatenate([dhq, dhf, dhi, dhg, daq, dak, dav], axis=1)
    for q in range(IN_CHUNKS):
        net.grad[W_INQ + q] = run(f"proj_bwd_w_{q}", _matmul, h, dproj, "tn", BF16, f"proj_bwd_w_{q}",
                                  m_blocks=(D // IN_CHUNKS, [q]))
    dh = run("proj_bwd_x", _matmul, dproj, net.full[W_IN], "nt", F32, "proj_bwd_x")
    grad_x, dshift1, dscale1, dnorm1 = run("norm1_bwd", _norm_mod_bwd, dh, x, norm1_w, scale1, dx1, "norm1_bwd")

    dmod = jnp.concatenate([dshift1, dscale1, dgate1, dshift2, dscale2, dgate2], axis=1)
    small = dict(norm1_w=dnorm1, lb=dlb, hg_norm_w=dhg_norm, q_norm_w=dq_norm, k_norm_w=dk_norm,
                 norm2_w=dnorm2, conv_b=dconv_b, conv_w=dconv_w)
    return loss_row, grad_x, dmod, small


def _place():
    x, y, c = lax.axis_index("x"), lax.axis_index("y"), lax.axis_index("c")
    others = [(1 - x, y), (x, 1 - y), (1 - x, 1 - y)]
    return x, y, c, others


def _remote(src, dst, send_sems, recv_sems, k, to):
    return pltpu.make_async_remote_copy(src_ref=src, dst_ref=dst, send_sem=send_sems.at[k], recv_sem=recv_sems.at[k],
                                        device_id=to, device_id_type=MESH)


class _Phase:
    def __init__(self):
        self.ins, self.outs, self.aliases, self.groups, self.n = [], [], {}, [], 0

    def add(self, ins, outs, aliases, n, copies):
        i0, o0 = len(self.ins), len(self.outs)
        self.ins += list(ins)
        self.outs += list(outs)
        self.aliases.update({i0 + i: o0 + o for i, o in aliases.items()})
        self.groups.append((slice(i0, i0 + len(ins)), slice(o0, o0 + len(outs)), copies))
        self.n += n
        return slice(o0, o0 + len(outs))

    def build(self, cin, cout, send_sems, recv_sems, landing):
        res = []
        for si, so, copies in self.groups:
            for src, dst, land, peer in copies(cin[si], cout[so]):
                k = len(res)
                res.append(_remote(land, land, send_sems, recv_sems, k, peer) if landing
                           else _remote(src, dst, send_sems, recv_sems, k, peer))
        assert len(res) == self.n
        return res


def _pcall(body, args, phase=None, **kw):
    if phase is None or not phase.groups:
        res = pl.pallas_call(body, **kw)(*args)
        return res if phase is None else (res, ())
    grid = tuple(kw.pop("grid", ()))
    out_shape, out_specs = kw.pop("out_shape"), kw.pop("out_specs")
    single = not isinstance(out_shape, (tuple, list))
    out_shape = [out_shape] if single else list(out_shape)
    out_specs = [out_specs] if single else list(out_specs)
    scratch = list(kw.pop("scratch_shapes", ()))
    n_in, n_out, n_ci, n_co = len(args), len(out_shape), len(phase.ins), len(phase.outs)

    def hosted(*refs):
        ins, cin = refs[:n_in], refs[n_in:n_in + n_ci]
        outs = refs[n_in + n_ci:n_in + n_ci + n_out]
        cout = refs[n_in + n_ci + n_out:n_in + n_ci + n_out + n_co]
        scr, send_sems, recv_sems = refs[n_in + n_ci + n_out + n_co:-2], refs[-2], refs[-1]
        ids = [pl.program_id(a) for a in range(len(grid))]

        def start():
            for cp in phase.build(cin, cout, send_sems, recv_sems, False):
                cp.start()

        def finish():
            for cp in phase.build(cin, cout, send_sems, recv_sems, False):
                cp.wait_send()
            for cp in phase.build(cin, cout, send_sems, recv_sems, True):
                cp.wait_recv()

        if grid:
            first = functools.reduce(jnp.logical_and, [i == 0 for i in ids])
            last = functools.reduce(jnp.logical_and, [i == g - 1 for i, g in zip(ids, grid)])
            pl.when(first)(start)
            body(*ins, *outs, *scr)
            pl.when(last)(finish)
        else:
            start()
            body(*ins, *outs, *scr)
            finish()

    res = pl.pallas_call(
        hosted, out_shape=tuple(out_shape + phase.outs), grid=grid,
        in_specs=list(kw.pop("in_specs")) + [HBM_SPEC] * n_ci, out_specs=tuple(out_specs + [HBM_SPEC] * n_co),
        input_output_aliases={n_in + i: n_out + o for i, o in phase.aliases.items()},
        scratch_shapes=scratch + [pltpu.SemaphoreType.DMA((phase.n,)), pltpu.SemaphoreType.DMA((phase.n,))],
        **kw,
    )(*args, *phase.ins)
    outs = res[:n_out]
    return (outs[0] if single else tuple(outs)), tuple(res[n_out:])


def _comm_only(name, phase):
    return _pcall(lambda: None, [], phase, name=name, out_shape=(), in_specs=[], out_specs=())[1]


def _all_gather_rows(block, name, phase=None):
    m_per, n = block.shape

    def body(x_ref, out_ref, send_sems, recv_sems, local_sem):
        x, y, c, chips = _place()
        me, sibling = (x, y, c), (x, y, 1 - c)

        def rows(px, py, pc):
            return out_ref.at[pl.ds((4 * px + 2 * py + pc) * m_per, m_per), :]

        def copy(k, blk, to, src=None):
            return _remote(rows(*blk) if src is None else src, rows(*blk), send_sems, recv_sems, k, to)

        mine = pltpu.make_async_copy(x_ref, rows(*me), local_sem)
        mine.start()
        first = [copy(0, me, sibling, src=x_ref)]
        first += [copy(1 + j, me, (*chip, c), src=x_ref) for j, chip in enumerate(chips)]
        for cp in first:
            cp.start()
        passed = [copy(4 + j, (*chip, c), sibling) for j, chip in enumerate(chips)]
        for j, chip in enumerate(chips):
            copy(1 + j, (*chip, c), me).wait_recv()
            passed[j].start()
        copy(0, sibling, me).wait_recv()
        for j, chip in enumerate(chips):
            copy(4 + j, (*chip, 1 - c), me).wait_recv()
        for cp in first + passed:
            cp.wait_send()
        mine.wait()

    return _pcall(
        body, [block], phase, name=name, out_shape=jax.ShapeDtypeStruct((N_DEV * m_per, n), block.dtype),
        in_specs=[VMEM_SPEC], out_specs=VMEM_SPEC,
        scratch_shapes=[pltpu.SemaphoreType.DMA((7,)), pltpu.SemaphoreType.DMA((7,)), pltpu.SemaphoreType.DMA],
    )


class _Geo:
    def __init__(self, kind, shard_shape):
        self.kind = kind
        self.shard_shape = tuple(shard_shape)
        self.hr, self.hc = shard_shape[0] // 2, shard_shape[1]
        if kind == "col":
            self.full_shape = (shard_shape[0], N_CHIPS * shard_shape[1])
        else:
            self.full_shape = (N_CHIPS * shard_shape[0], shard_shape[1])

    def full_half(self, ref, j, c):
        return self.piece(ref, j, c, 0, 1, 1)

    def piece(self, ref, j, c, p0, p1, pieces, part=None):
        pr = self.hr // pieces
        off, n = p0 * pr, (p1 - p0) * pr
        if part is not None:
            top = (n // 32) * 16
            off, n = (off, top) if part == 0 else (off + top, n - top)
        if self.kind == "col":
            return ref.at[pl.ds(pl.multiple_of(c * self.hr + off, 16), n),
                          pl.ds(pl.multiple_of(j * self.hc, 128), self.hc)]
        return ref.at[pl.ds(pl.multiple_of((2 * j + c) * self.hr + off, 16), n), :]


WEIGHT_KINDS = ("col", "row", "col", "row")
NW = len(WEIGHT_KINDS)


def _comm_call(body, name, ins, outs, n_remote, aliases=None):
    return pl.pallas_call(
        body, name=name, out_shape=tuple(outs),
        in_specs=[HBM_SPEC] * len(ins), out_specs=tuple([HBM_SPEC] * len(outs)),
        input_output_aliases=aliases or {},
        scratch_shapes=[pltpu.SemaphoreType.DMA((n_remote,)), pltpu.SemaphoreType.DMA((n_remote,))],
    )(*ins)


ROW_TILES = (256, 176, 128, 64, 32, 16)


def _cast_into_full(shard, geo, place, name):
    rs, cs = shard.shape
    tr = _pick(rs, ROW_TILES)
    nb = rs // tr

    def body(place_ref, s_ref, o_ref):
        o_ref[...] = s_ref[...].astype(BF16)

    if geo.kind == "col":
        out_map = lambda i, place_ref: (i, place_ref[0])
    else:
        out_map = lambda i, place_ref: (place_ref[0] * nb + i, 0)
    return pl.pallas_call(
        body, name=name, out_shape=jax.ShapeDtypeStruct(geo.full_shape, BF16),
        grid_spec=pltpu.PrefetchScalarGridSpec(
            num_scalar_prefetch=1, grid=(nb,),
            in_specs=[pl.BlockSpec((tr, cs), lambda i, place_ref: (i, 0))],
            out_specs=pl.BlockSpec((tr, cs), out_map)),
        compiler_params=_params(1),
    )(place, shard)


def _gather_weight(full, geo, pieces, name):
    per = 8

    def body(in_ref, ref, send_sems, recv_sems):
        x, y, c, _ = _place()
        j, jx, jy, jo = 2 * x + y, 2 * (1 - x) + y, 2 * x + (1 - y), 2 * (1 - x) + (1 - y)
        nx, ny, sib = (1 - x, y, c), (x, 1 - y, c), (x, y, 1 - c)

        def cp(region, k, to):
            return _remote(region, region, send_sems, recv_sems, k, to)

        def reg(chip, p, part=None, core=c):
            return geo.piece(ref, chip, core, p, p + 1, pieces, part)

        sent = []
        for p in range(pieces):
            sent += [cp(reg(j, p), per * p, nx), cp(reg(j, p), per * p + 1, ny)]
        for d in sent:
            d.start()
        for p in range(pieces):
            cp(reg(jx, p), per * p, nx).wait_recv()
            new = [cp(reg(jx, p, 0), per * p + 2, ny), cp(reg(jx, p), per * p + 4, sib)]
            cp(reg(jy, p), per * p + 1, ny).wait_recv()
            new += [cp(reg(jy, p, 1), per * p + 3, nx), cp(reg(jy, p), per * p + 5, sib)]
            for d in new:
                d.start()
            sent += new
        for p in range(pieces):
            cp(reg(jo, p, 0), per * p + 2, ny).wait_recv()
            cp(reg(jo, p, 1), per * p + 3, nx).wait_recv()
            new = [cp(reg(jo, p, 0), per * p + 6, sib), cp(reg(jo, p, 1), per * p + 7, sib)]
            for d in new:
                d.start()
            sent += new
        for p in range(pieces):
            cp(reg(jx, p, None, 1 - c), per * p + 4, sib).wait_recv()
            cp(reg(jy, p, None, 1 - c), per * p + 5, sib).wait_recv()
            cp(reg(jo, p, 0, 1 - c), per * p + 6, sib).wait_recv()
            cp(reg(jo, p, 1, 1 - c), per * p + 7, sib).wait_recv()
        for d in sent:
            d.wait_send()

    return _comm_call(body, name, [full], [_same(full)], per * pieces, aliases={0: 0})[0]


def _same(a):
    return jax.ShapeDtypeStruct(a.shape, a.dtype)


def _gather_ici(full, geo, p0, p1, pieces):
    def copies(ins, outs):
        x, y, c, _ = _place()
        mine = geo.piece(outs[0], 2 * x + y, c, p0, p1, pieces)
        return [(mine, mine, geo.piece(outs[0], 2 * ox + oy, c, p0, p1, pieces), (ox, oy, c))
                for ox, oy in ((1 - x, y), (x, 1 - y))]

    return dict(ins=[full], outs=[_same(full)], aliases={0: 0}, n=2, copies=copies)


def _gather_relay(full, geo, p0, p1, pieces):
    def copies(ins, outs):
        x, y, c, _ = _place()
        jx, jy, jo = 2 * (1 - x) + y, 2 * x + (1 - y), 2 * (1 - x) + (1 - y)
        reg = lambda chip, part: geo.piece(outs[0], chip, c, p0, p1, pieces, part)
        return [(reg(jx, 0), reg(jx, 0), reg(jo, 0), (x, 1 - y, c)), (reg(jy, 1), reg(jy, 1), reg(jo, 1), (1 - x, y, c))]

    return dict(ins=[full], outs=[_same(full)], aliases={0: 0}, n=2, copies=copies)


def _gather_d2d(full, geo):
    def copies(ins, outs):
        x, y, c, chips = _place()
        return [(geo.full_half(outs[0], 2 * ox + oy, c), geo.full_half(outs[0], 2 * ox + oy, c),
                 geo.full_half(outs[0], 2 * ox + oy, 1 - c), (x, y, 1 - c)) for ox, oy in chips]

    return dict(ins=[full], outs=[_same(full)], aliases={0: 0}, n=3, copies=copies)


def _swap_d2d(grad, geo):
    def copies(ins, outs):
        x, y, c, _ = _place()
        return [(geo.full_half(ins[0], j, 1 - c), outs[0].at[j], outs[0].at[j], (x, y, 1 - c)) for j in range(N_CHIPS)]

    return dict(ins=[grad], outs=[jax.ShapeDtypeStruct((N_CHIPS, geo.hr, geo.hc), BF16)], aliases={}, n=N_CHIPS,
                copies=copies)


def _scatter_ici(pair, recv, geo, p0, p1, pieces):
    pr = geo.hr // pieces
    rows = pl.ds(p0 * pr, (p1 - p0) * pr)

    def copies(ins, outs):
        x, y, c, chips = _place()
        return [(ins[0].at[2 * ox + oy, rows, :], outs[0].at[k, rows, :], outs[0].at[k, rows, :], (ox, oy, c))
                for k, (ox, oy) in enumerate(chips)]

    out = jax.ShapeDtypeStruct((3, geo.hr, geo.hc), BF16)
    if recv is None:
        return dict(ins=[pair], outs=[out], aliases={}, n=3, copies=copies)
    return dict(ins=[pair, recv], outs=[out], aliases={1: 0}, n=3, copies=copies)


def _join_d2d(shard, geo, row0=0):
    def copies(ins, outs):
        x, y, c, _ = _place()
        mine = outs[0].at[pl.ds(pl.multiple_of(row0 + c * geo.hr, 8), geo.hr), :]
        other = outs[0].at[pl.ds(pl.multiple_of(row0 + (1 - c) * geo.hr, 8), geo.hr), :]
        return [(mine, mine, other, (x, y, 1 - c))]

    return dict(ins=[shard], outs=[_same(shard)], aliases={0: 0}, n=1, copies=copies)


def _add_pair(grad, got, geo, place, name):
    tr = _pick(geo.hr, ROW_TILES)
    nb = geo.hr // tr

    def body(place_ref, a_ref, b_ref, o_ref):
        o_ref[0] = (a_ref[...].astype(F32) + b_ref[0].astype(F32)).astype(BF16)

    if geo.kind == "col":
        own_map = lambda j, i, place_ref: (place_ref[1] * nb + i, j)
    else:
        own_map = lambda j, i, place_ref: ((2 * j + place_ref[1]) * nb + i, 0)
    spec = pl.BlockSpec((1, tr, geo.hc), lambda j, i, place_ref: (j, i, 0))
    return pl.pallas_call(
        body, name=name, out_shape=jax.ShapeDtypeStruct((N_CHIPS, geo.hr, geo.hc), BF16),
        grid_spec=pltpu.PrefetchScalarGridSpec(
            num_scalar_prefetch=1, grid=(N_CHIPS, nb),
            in_specs=[pl.BlockSpec((tr, geo.hc), own_map), spec], out_specs=spec),
        compiler_params=_params(2),
    )(place, grad, got)


def _add_four(pair, recv, geo, place, name, rows=None, row0=0, into=None):
    tr = _pick(geo.hr, ROW_TILES)
    nb = geo.hr // tr
    rows = 2 * geo.hr if rows is None else rows

    def body(place_ref, p_ref, r_ref, *rest):
        rest[-1][...] = ((p_ref[0].astype(F32) + r_ref[0].astype(F32)) + r_ref[1].astype(F32)) + r_ref[2].astype(F32)

    in_specs = [pl.BlockSpec((1, tr, geo.hc), lambda i, place_ref: (place_ref[0], i, 0)),
                pl.BlockSpec((3, tr, geo.hc), lambda i, place_ref: (0, i, 0))]
    args = [place, pair, recv]
    if into is not None:
        in_specs.append(pl.BlockSpec(memory_space=pl.ANY))
        args.append(into)
    return pl.pallas_call(
        body, name=name, out_shape=jax.ShapeDtypeStruct((rows, geo.hc), F32),
        grid_spec=pltpu.PrefetchScalarGridSpec(
            num_scalar_prefetch=1, grid=(nb,), in_specs=in_specs,
            out_specs=pl.BlockSpec((tr, geo.hc), lambda i, place_ref: (row0 // tr + place_ref[1] * nb + i, 0))),
        input_output_aliases={3: 0} if into is not None else {},
        compiler_params=_params(1),
    )(*args)


PIECES = (4, 1, 16, 8) + (1,) * IN_CHUNKS
SCHEDULE = {
    "proj_fwd": (("gather_ici", W_OUT, 0, 1), ("gather_ici", W_UP, 0, 6)),
    "hgrn_fwd": (("gather_relay", W_OUT, 0, 1), ("gather_relay", W_UP, 0, 6), ("gather_ici", W_UP, 6, 12)),
    "attn_fwd_d1": (("gather_relay", W_UP, 6, 12),),
    "attn_fwd_d4": (("gather_ici", W_UP, 12, 16), ("gather_d2d", W_OUT)),
    "attn_fwd_d16": (("gather_relay", W_UP, 12, 16), ("gather_ici", W_DOWN, 0, 2)),
    "mix_fwd": (("gather_d2d", W_UP), ("gather_ici", W_DOWN, 2, 4)),
    "up_fwd": (("gather_ici", W_DOWN, 4, 8), ("gather_relay", W_DOWN, 0, 4)),
    "conv_gate_fwd": (("gather_relay", W_DOWN, 4, 8),),
    "before_down_fwd": (("gather_d2d", W_DOWN),),
    "down_bwd_x": (("swap", W_DOWN),),
    "conv_gate_bwd": (("scatter", W_DOWN, 0, 4),),
    "up_bwd_w": (("scatter", W_DOWN, 4, 8),),
    "up_bwd_x": (("swap", W_UP),),
    "norm2_bwd": (("scatter", W_UP, 0, 1),),
    "mix_bwd_w": (("scatter", W_UP, 1, 2),),
    "mix_bwd_x": (("swap", W_OUT), ("scatter", W_UP, 2, 3)),
    "hgrn_bwd": (("scatter", W_UP, 3, 9), ("join", W_DOWN)),
    "attn_bwd_d1": (("scatter", W_UP, 9, 12),),
    "attn_bwd_d4": (("scatter", W_OUT, 0, 1),),
    "attn_bwd_d16": (("scatter", W_UP, 12, 16),),
    "proj_bwd_w_0": (("join", W_UP), ("join", W_OUT)),
    "proj_bwd_w_1": (("swap", W_INQ),),
    "proj_bwd_w_2": (("swap", W_INQ + 1),),
    "proj_bwd_w_3": (("swap", W_INQ + 2), ("scatter", W_INQ, 0, 1)),
    "proj_bwd_x": (("swap", W_INQ + 3), ("scatter", W_INQ + 1, 0, 1), ("scatter", W_INQ + 2, 0, 1)),
    "gather_stats": (("scatter", W_INQ + 3, 0, 1), ("join", W_INQ), ("join", W_INQ + 1), ("join", W_INQ + 2)),
    "grad_in_join": (("join", W_INQ + 3),),
}


class _Net:
    def __init__(self, shards, place):
        self.place = place
        self.geos = [_Geo(k, s.shape) for k, s in zip(WEIGHT_KINDS, shards)]
        self.full = [_cast_into_full(s, g, place, f"cast_shard_{w}") for w, (s, g) in enumerate(zip(shards, self.geos))]
        self.full[W_IN] = _gather_weight(self.full[W_IN], self.geos[W_IN], PIECES[W_IN], "gather_w_in")
        rows, cols = shards[W_IN].shape
        self.geos += [_Geo("col", (rows // IN_CHUNKS, cols))] * IN_CHUNKS
        n = NW + IN_CHUNKS
        self.grad, self.pair, self.recv, self.shard = [None] * n, [None] * n, [None] * n, [None] * n
        self.in_rows, self.in_shard = rows, None
        self.when_joined, self.joined = {}, {}
        self.slots = []

    def _row0(self, w):
        return (w - W_INQ) * 2 * self.geos[w].hr

    def host(self, name):
        phase = _Phase()
        self.slots = []
        groups = {}
        for item in SCHEDULE.get(name, ()):
            kind, w = item[0], item[1]
            geo = self.geos[w]
            key = len(groups)
            if kind == "gather_ici":
                spec, key = _gather_ici(self.full[w], geo, item[2], item[3], PIECES[w]), ("full", w)
            elif kind == "gather_relay":
                spec, key = _gather_relay(self.full[w], geo, item[2], item[3], PIECES[w]), ("full", w)
            elif kind == "gather_d2d":
                spec, key = _gather_d2d(self.full[w], geo), ("full", w)
            elif kind == "swap":
                spec = _swap_d2d(self.grad[w], geo)
            elif kind == "scatter":
                spec, key = _scatter_ici(self.pair[w], self.recv[w], geo, item[2], item[3], PIECES[w]), ("recv", w)
            elif w >= W_INQ:
                spec, key = _join_d2d(self.in_shard, geo, self._row0(w)), "in_shard"
            else:
                spec = _join_d2d(self.shard[w], geo)
            groups.setdefault(key, []).append((item, spec))
        for group in groups.values():
            specs = [s for _, s in group]
            merged = dict(specs[0], n=sum(s["n"] for s in specs),
                          copies=lambda ins, outs, specs=specs: [t for s in specs for t in s["copies"](ins, outs)])
            self.slots.append(([item for item, _ in group], phase.add(**merged)))
        return phase

    def done(self, name, comm):
        for items, sl in sorted(self.slots, key=lambda s: s[0][0][0] == "scatter"):
            out = comm[sl][0]
            for item in items:
                kind, w = item[0], item[1]
                if kind in ("gather_ici", "gather_relay", "gather_d2d"):
                    self.full[w] = out
                elif kind == "swap":
                    self.pair[w] = _add_pair(self.grad[w], out, self.geos[w], self.place, f"grad_pair_sum_{w}")
                elif kind == "scatter":
                    self.recv[w] = out
                    if item[3] == PIECES[w] and w >= W_INQ:
                        self.in_shard = _add_four(self.pair[w], out, self.geos[w], self.place, f"grad_chip_sum_{w}",
                                                  rows=self.in_rows, row0=self._row0(w), into=self.in_shard)
                    elif item[3] == PIECES[w]:
                        self.shard[w] = _add_four(self.pair[w], out, self.geos[w], self.place, f"grad_chip_sum_{w}")
                elif w >= W_INQ:
                    self.in_shard = out
                else:
                    self.shard[w] = out
                    if w in self.when_joined:
                        self.joined[w] = self.when_joined[w](out)

    def alone(self, name):
        self.done(name, _comm_only(name, self.host(name)))


CONVW_SHARD = 3 * DFF // N_CHIPS
COND_PAD = 8 * 896
SMALL_SIZES = (("norm1_w", D), ("lb", HW), ("hg_norm_w", DH), ("q_norm_w", DH), ("k_norm_w", DH),
               ("norm2_w", D), ("conv_b", DFF), ("conv_w", 3 * DFF), ("loss", 128))
SMALL_TOTAL = sum(n for _, n in SMALL_SIZES)
STATS_PAD = 8 * 5120


def kernel(x, c, w_ada, b_ada, norm1_w, w_in, lb_logits, hg_norm_w, q_norm_w, k_norm_w, w_out, norm2_w, w_up, conv_w, conv_b, w_down, loss_target, m_w_ada, m_b_ada, m_norm1_w, m_w_in, m_lb_logits, m_hg_norm_w, m_q_norm_w, m_k_norm_w, m_w_out, m_norm2_w, m_w_up, m_conv_w, m_conv_b, m_w_down, v_w_ada, v_b_ada, v_norm1_w, v_w_in, v_lb_logits, v_hg_norm_w, v_q_norm_w, v_k_norm_w, v_w_out, v_norm2_w, v_w_up, v_conv_w, v_conv_b, v_w_down):
    chip = 2 * lax.axis_index("x") + lax.axis_index("y")
    dev = 2 * chip + lax.axis_index("c")

    cond = jnp.concatenate([c, conv_w[0].reshape(1, CONVW_SHARD), jnp.zeros((1, COND_PAD - D - CONVW_SHARD), F32)], axis=1)
    cond_all = _all_gather_rows(cond.reshape(8, COND_PAD // 8), "gather_cond").reshape(N_DEV, COND_PAD)
    c_all = cond_all[:, :D]
    conv_w_full = jnp.concatenate(
        [cond_all[2 * j, D:D + CONVW_SHARD].reshape(3, DFF // N_CHIPS) for j in range(N_CHIPS)], axis=1)

    b_shard = lax.dynamic_slice_in_dim(b_ada, chip * ADA_COLS, ADA_COLS, axis=1)
    mod_cols = _all_gather_rows(_ada_fwd(c_all, w_ada[0], b_shard), "gather_mod")
    mod_all = jnp.concatenate([mod_cols[16 * j:16 * j + 8] for j in range(N_CHIPS)], axis=1)
    mod = lax.dynamic_slice_in_dim(mod_all, dev, 1, axis=0)

    place = jnp.stack([chip, lax.axis_index("c")]).astype(jnp.int32)
    net = _Net([w_in[0], w_out[0], w_up[0], w_down[0]], place)
    net.when_joined = {
        W_OUT: lambda grad: _adamw_sc(w_out[0], grad, m_w_out[0], v_w_out[0], "adamw_sc_w_out"),
        W_DOWN: lambda grad: _adamw_sc(w_down[0], grad, m_w_down[0], v_w_down[0], "adamw_sc_w_down"),
        W_UP: lambda grad: _adamw_sc(w_up[0], grad, m_w_up[0], v_w_up[0], "adamw_sc_w_up"),
    }
    early = {W_OUT: "w_out", W_DOWN: "w_down", W_UP: "w_up"}
    loss_row, grad_x, dmod, small = _sequence_step(
        x[0], loss_target[0], mod, norm1_w, lb_logits, hg_norm_w, q_norm_w, k_norm_w, norm2_w, conv_w_full, conv_b, net)
    small["conv_w"] = small["conv_w"].reshape(1, 3 * DFF)
    small["loss"] = loss_row
    stats = jnp.concatenate([dmod] + [small[k] for k, _ in SMALL_SIZES]
                            + [jnp.zeros((1, STATS_PAD - 6 * D - SMALL_TOTAL), F32)], axis=1)
    stats_all, comm = _all_gather_rows(stats.reshape(8, STATS_PAD // 8), "gather_stats", net.host("gather_stats"))
    net.done("gather_stats", comm)
    stats_all = stats_all.reshape(N_DEV, STATS_PAD)
    net.alone("grad_in_join")
    g_out, g_up, g_down = net.shard[W_OUT], net.shard[W_UP], net.shard[W_DOWN]
    g_in = net.in_shard
    dmod_all = stats_all[:, :6 * D]
    sums = _sum_rows(stats_all[:, 6 * D:6 * D + SMALL_TOTAL], "small_grad_sum")
    g_small, off = {}, 0
    for k, n in SMALL_SIZES:
        g_small[k] = sums[:, off:off + n]
        off += n
    loss = g_small["loss"][0, 0]
    g_b_ada = _sum_rows(dmod_all, "b_ada_grad_sum")
    ada = _ada_bwd_adamw(c_all, lax.dynamic_slice_in_dim(dmod_all, chip * ADA_COLS, ADA_COLS, axis=1),
                         w_ada[0], m_w_ada[0], v_w_ada[0])
    g_w_ada = ada[0]
    g_lb = _lb_grad(lb_logits, g_small["lb"])
    g_conv_w = lax.dynamic_slice_in_dim(g_small["conv_w"].reshape(3, DFF), chip * (DFF // N_CHIPS), DFF // N_CHIPS, axis=1)

    names = ["w_ada", "b_ada", "norm1_w", "w_in", "lb_logits", "hg_norm_w", "q_norm_w", "k_norm_w", "w_out",
             "norm2_w", "w_up", "conv_w", "conv_b", "w_down"]
    lead = {"w_ada", "w_in", "w_out", "w_up", "conv_w", "w_down"}
    w = dict(w_ada=w_ada, b_ada=b_ada, norm1_w=norm1_w, w_in=w_in, lb_logits=lb_logits, hg_norm_w=hg_norm_w,
             q_norm_w=q_norm_w, k_norm_w=k_norm_w, w_out=w_out, norm2_w=norm2_w, w_up=w_up, conv_w=conv_w,
             conv_b=conv_b, w_down=w_down)
    m = dict(w_ada=m_w_ada, b_ada=m_b_ada, norm1_w=m_norm1_w, w_in=m_w_in, lb_logits=m_lb_logits,
             hg_norm_w=m_hg_norm_w, q_norm_w=m_q_norm_w, k_norm_w=m_k_norm_w, w_out=m_w_out, norm2_w=m_norm2_w,
             w_up=m_w_up, conv_w=m_conv_w, conv_b=m_conv_b, w_down=m_w_down)
    v = dict(w_ada=v_w_ada, b_ada=v_b_ada, norm1_w=v_norm1_w, w_in=v_w_in, lb_logits=v_lb_logits,
             hg_norm_w=v_hg_norm_w, q_norm_w=v_q_norm_w, k_norm_w=v_k_norm_w, w_out=v_w_out, norm2_w=v_norm2_w,
             w_up=v_w_up, conv_w=v_conv_w, conv_b=v_conv_b, w_down=v_w_down)
    g = dict(w_ada=g_w_ada, b_ada=g_b_ada, norm1_w=g_small["norm1_w"], w_in=g_in, lb_logits=g_lb,
             hg_norm_w=g_small["hg_norm_w"], q_norm_w=g_small["q_norm_w"], k_norm_w=g_small["k_norm_w"], w_out=g_out,
             norm2_w=g_small["norm2_w"], w_up=g_up, conv_w=g_conv_w, conv_b=g_small["conv_b"], w_down=g_down)

    updated = {early[i]: res for i, res in net.joined.items()}
    updated["w_ada"] = (ada[1], ada[2], ada[3], ada[0])
    grads, deltas, new_m, new_v = [], [], [], []
    for n in names:
        strip = (lambda t: t[0]) if n in lead else (lambda t: t)
        wrap = (lambda t: t[None]) if n in lead else (lambda t: t)
        g_n = g[n]
        if n in updated:
            d_n, m_n, v_n, g_n = updated[n]
        elif n == "w_in":
            d_n, m_n, v_n, g_n = _adamw(strip(w[n]), g_n, strip(m[n]), strip(v[n]), f"adamw_{n}", copy_grad=True)
        else:
            d_n, m_n, v_n = _adamw(strip(w[n]), g_n, strip(m[n]), strip(v[n]), f"adamw_{n}")
        grads.append(wrap(g_n))
        deltas.append(wrap(d_n))
        new_m.append(wrap(m_n))
        new_v.append(wrap(v_n))
    return (loss, grad_x[None], *grads, *deltas, *new_m, *new_v)
```

```python
import functools

import jax
import jax.numpy as jnp
from jax import lax
from jax.experimental import pallas as pl
from jax.experimental.pallas import tpu as pltpu
from jax.experimental.pallas import tpu_sc as plsc

F32 = jnp.float32
BF16 = jnp.bfloat16

S = 2048
D = 2048
H = 8
DH = 128
HW = H * DH
CH = 64
NCH = S // CH
DFF = 5632
INC = 7 * HW
BLK = 128
DILS = (1, 4, 16)
EPS = 1e-6
NEG = -1e30
N_CHIPS = 4
N_DEV = 8

ADAM_LR = 0.001
ADAM_B1 = 0.9
ADAM_B2 = 0.999
ADAM_EPS = 1e-08
ADAM_WD = 0.01
ADAM_STEP = 10
ADAM_BLOCK_BYTES = 1 << 21

V7X_VMEM_BYTES = 64 * 1024 * 1024
VMEM_LIMIT = (V7X_VMEM_BYTES * 3) // 4
MESH = pl.DeviceIdType.MESH
HBM_SPEC = pl.BlockSpec(memory_space=pltpu.HBM)
VMEM_SPEC = pl.BlockSpec(memory_space=pltpu.VMEM)

NN = (((1,), (0,)), ((), ()))
NT = (((1,), (1,)), ((), ()))
TN = (((0,), (0,)), ((), ()))


def _params(n_grid):
    return pltpu.CompilerParams(dimension_semantics=("arbitrary",) * n_grid, vmem_limit_bytes=VMEM_LIMIT)


def _dot(a, b, dims=NN):
    return lax.dot_general(a.astype(BF16), b.astype(BF16), dims, preferred_element_type=F32)


def _dot_f32(a, b):
    return lax.dot_general(a, b, NN, precision=lax.Precision.HIGHEST, preferred_element_type=F32)


def _sigmoid(x):
    return 1.0 / (1.0 + jnp.exp(-x))


def _pick(n, cands):
    for t in cands:
        if n % t == 0:
            return t
    raise ValueError(n)


def _matmul(a, b, mode, out_dtype, name, phase=None, m_blocks=None):
    if mode == "nn":
        (m, k), (_, n) = a.shape, b.shape
    elif mode == "nt":
        (m, k), (n, _) = a.shape, b.shape
    else:
        (k, m), (_, n) = a.shape, b.shape
    tm = _pick(m, (1024, 1408, 512))
    a_block = lambda i: i
    if m_blocks is not None:
        tm, blocks = m_blocks
        m = tm * len(blocks)
        step = blocks[1] - blocks[0] if len(blocks) > 1 else 0
        assert all(blk == blocks[0] + step * i for i, blk in enumerate(blocks))
        a_block = lambda i: blocks[0] + step * i
    tn = _pick(n, (1024, 1408, 512))
    tk = _pick(k, (2048, 2816, 1792, 1024, 512))
    nk = k // tk
    dims = {"nn": NN, "nt": NT, "tn": TN}[mode]

    def body(a_ref, b_ref, o_ref, *acc):
        part = lax.dot_general(a_ref[...], b_ref[...], dims, preferred_element_type=F32)
        if nk == 1:
            o_ref[...] = part.astype(o_ref.dtype)
            return
        acc_ref, kk = acc[0], pl.program_id(2)

        @pl.when(kk == 0)
        def _():
            acc_ref[...] = part

        @pl.when(jnp.logical_and(kk > 0, kk < nk - 1))
        def _():
            acc_ref[...] += part

        @pl.when(kk == nk - 1)
        def _():
            o_ref[...] = (acc_ref[...] + part).astype(o_ref.dtype)

    if mode == "tn":
        a_spec = pl.BlockSpec((tk, tm), lambda i, j, kk: (kk, a_block(i)))
    else:
        a_spec = pl.BlockSpec((tm, tk), lambda i, j, kk: (i, kk))
    if mode == "nt":
        b_spec = pl.BlockSpec((tn, tk), lambda i, j, kk: (j, kk))
    else:
        b_spec = pl.BlockSpec((tk, tn), lambda i, j, kk: (kk, j))
    return _pcall(
        body, [a, b], phase, name=name,
        out_shape=jax.ShapeDtypeStruct((m, n), out_dtype),
        grid=(m // tm, n // tn, nk),
        in_specs=[a_spec, b_spec],
        out_specs=pl.BlockSpec((tm, tn), lambda i, j, kk: (i, j)),
        scratch_shapes=[pltpu.VMEM((tm, tn), F32)] if nk > 1 else [],
        compiler_params=_params(3),
    )


TR = 256


def _row_spec(cols=D):
    return pl.BlockSpec((TR, cols), lambda i: (i, 0))


def _vec_spec(cols=D):
    return pl.BlockSpec((1, cols), lambda i: (0, 0))


def _norm_mod(x, nw, scale, shift, name):
    def body(x_ref, nw_ref, sc_ref, sh_ref, h_ref):
        xv = x_ref[...]
        r = lax.rsqrt(jnp.mean(xv * xv, axis=-1, keepdims=True) + EPS)
        h_ref[...] = ((xv * r) * nw_ref[...] * (1.0 + sc_ref[...]) + sh_ref[...]).astype(BF16)

    return pl.pallas_call(
        body, name=name, out_shape=jax.ShapeDtypeStruct((S, D), BF16), grid=(S // TR,),
        in_specs=[_row_spec(), _vec_spec(), _vec_spec(), _vec_spec()], out_specs=_row_spec(),
        compiler_params=_params(1),
    )(x, nw, scale, shift)


def _resid_norm_mod(x, mix, gate, nw, scale, shift, name):
    def body(x_ref, mix_ref, g_ref, nw_ref, sc_ref, sh_ref, x1_ref, h_ref):
        xv = x_ref[...] + g_ref[...] * mix_ref[...]
        x1_ref[...] = xv
        r = lax.rsqrt(jnp.mean(xv * xv, axis=-1, keepdims=True) + EPS)
        h_ref[...] = ((xv * r) * nw_ref[...] * (1.0 + sc_ref[...]) + sh_ref[...]).astype(BF16)

    return pl.pallas_call(
        body, name=name,
        out_shape=(jax.ShapeDtypeStruct((S, D), F32), jax.ShapeDtypeStruct((S, D), BF16)), grid=(S // TR,),
        in_specs=[_row_spec(), _row_spec(), _vec_spec(), _vec_spec(), _vec_spec(), _vec_spec()],
        out_specs=(_row_spec(), _row_spec()),
        compiler_params=_params(1),
    )(x, mix, gate, nw, scale, shift)


def _norm_mod_bwd(dh, xin, nw, scale, dres, name, mix=None, gate=None, phase=None):
    with_gate = mix is not None

    def body(*refs):
        if with_gate:
            dh_ref, x_ref, nw_ref, sc_ref, dres_ref, mix_ref, g_ref, dx_ref, dsh_ref, dsc_ref, dnw_ref, dg_ref, dmix_ref = refs
        else:
            dh_ref, x_ref, nw_ref, sc_ref, dres_ref, dx_ref, dsh_ref, dsc_ref, dnw_ref = refs
        i = pl.program_id(0)
        dhv = dh_ref[...]
        xv = x_ref[...]
        r = lax.rsqrt(jnp.mean(xv * xv, axis=-1, keepdims=True) + EPS)
        xn = xv * r
        nwv = nw_ref[...]
        one_sc = 1.0 + sc_ref[...]
        dxn = dhv * nwv * one_sc
        dx = dres_ref[...] + r * (dxn - xn * jnp.mean(dxn * xn, axis=-1, keepdims=True))
        dx_ref[...] = dx

        @pl.when(i == 0)
        def _():
            dsh_ref[...] = jnp.zeros_like(dsh_ref)
            dsc_ref[...] = jnp.zeros_like(dsc_ref)
            dnw_ref[...] = jnp.zeros_like(dnw_ref)
            if with_gate:
                dg_ref[...] = jnp.zeros_like(dg_ref)

        dsh_ref[...] += jnp.sum(dhv, axis=0, keepdims=True)
        dsc_ref[...] += jnp.sum(dhv * xn * nwv, axis=0, keepdims=True)
        dnw_ref[...] += jnp.sum(dhv * xn * one_sc, axis=0, keepdims=True)
        if with_gate:
            dg_ref[...] += jnp.sum(dx * mix_ref[...], axis=0, keepdims=True)
            dmix_ref[...] = (dx * g_ref[...]).astype(BF16)

    vec = jax.ShapeDtypeStruct((1, D), F32)
    ins = [dh, xin, nw, scale, dres]
    in_specs = [_row_spec(), _row_spec(), _vec_spec(), _vec_spec(), _row_spec()]
    outs = [jax.ShapeDtypeStruct((S, D), F32), vec, vec, vec]
    out_specs = [_row_spec(), _vec_spec(), _vec_spec(), _vec_spec()]
    if with_gate:
        ins += [mix, gate]
        in_specs += [_row_spec(), _vec_spec()]
        outs += [vec, jax.ShapeDtypeStruct((S, D), BF16)]
        out_specs += [_vec_spec(), _row_spec()]
    return _pcall(
        body, ins, phase, name=name, out_shape=tuple(outs), grid=(S // TR,), in_specs=in_specs,
        out_specs=tuple(out_specs), compiler_params=_params(1),
    )


def _tri(lower):
    row = lax.broadcasted_iota(jnp.int32, (CH, CH), 0)
    col = lax.broadcasted_iota(jnp.int32, (CH, CH), 1)
    return (row >= col) if lower else (col >= row)


def _hgrn_gates(hq, hf, lb):
    sig = _sigmoid(hf)
    f = lb + (1.0 - lb) * sig
    g = jnp.log(f)
    sq = _sigmoid(hq)
    return sig, f, g, 1.0 - f, hq * sq, sq


HG_HP = 2
HG_UNROLL = 8


def _proj_col_spec(group):
    return pl.BlockSpec((S, HG_HP * DH), lambda h: (0, group * (H // HG_HP) + h))


def _hgrn_fwd(proj, lb_logits, norm_w, phase=None):
    def body(hq_ref, hf_ref, hi_ref, hg_ref, lbl_ref, nw_ref, out_ref, oraw_ref, st_ref, s_ref):
        nw = nw_ref[...]
        lower = _tri(True)
        ltri = lower.astype(F32)
        s_ref[...] = jnp.zeros_like(s_ref)

        def chunk(n, carry):
            rows = pl.ds(pl.multiple_of(n * CH, CH), CH)
            for j in range(HG_HP):
                cols = slice(j * DH, (j + 1) * DH)
                lb = 1.0 / (1.0 + jnp.exp(lbl_ref[1:2, cols] - lbl_ref[0:1, cols]))
                hq, hf, v, hg = hq_ref[rows, cols], hf_ref[rows, cols], hi_ref[rows, cols], hg_ref[rows, cols]
                _, _, g, kk, q, _ = _hgrn_gates(hq, hf, lb)
                gc = _dot_f32(ltri, g)
                gl = jnp.sum(g, axis=0, keepdims=True)
                qe = q * jnp.exp(gc)
                ke = kk * jnp.exp(gl - gc)
                qh = q * jnp.exp(gc - 0.5 * gl)
                kh = kk * jnp.exp(0.5 * gl - gc)
                st = s_ref[j]
                st_ref[j, pl.ds(n, 1)] = st[None]
                a = jnp.where(lower, _dot(qh, kh, NT), 0.0)
                o = _dot(qe, st, NT) + _dot(a, v)
                s_ref[j] = st * jnp.exp(gl) + _dot(v, ke, TN)
                oraw_ref[rows, cols] = o
                rs = lax.rsqrt(jnp.mean(o * o, axis=-1, keepdims=True) + EPS)
                out_ref[rows, cols] = (o * rs * nw * (hg * _sigmoid(hg))).astype(BF16)
            return carry

        lax.fori_loop(0, NCH, chunk, 0, unroll=HG_UNROLL)

    head_spec = pl.BlockSpec((S, HG_HP * DH), lambda h: (0, h))
    return _pcall(
        body, [proj, proj, proj, proj, lb_logits, norm_w], phase, name="hgrn_fwd",
        out_shape=(jax.ShapeDtypeStruct((S, 2 * HW), BF16), jax.ShapeDtypeStruct((S, HW), F32),
                   jax.ShapeDtypeStruct((H, NCH, DH, DH), F32)),
        grid=(H // HG_HP,),
        in_specs=[_proj_col_spec(0), _proj_col_spec(1), _proj_col_spec(2), _proj_col_spec(3),
                  pl.BlockSpec((2, HG_HP * DH), lambda h: (0, h)), pl.BlockSpec((1, DH), lambda h: (0, 0))],
        out_specs=(head_spec, head_spec, pl.BlockSpec((HG_HP, NCH, DH, DH), lambda h: (h, 0, 0, 0))),
        scratch_shapes=[pltpu.VMEM((HG_HP, DH, DH), F32)],
        compiler_params=_params(1),
    )


def _hgrn_bwd(proj, lb_logits, norm_w, oraw, states, dout, phase=None):
    def body(hq_ref, hf_ref, hi_ref, hg_ref, lbl_ref, nw_ref, oraw_ref, st_ref, do_ref,
             dhq_ref, dhf_ref, dhi_ref, dhg_ref, dlb_ref, dnw_ref, ds_ref, acc_ref):
        h = pl.program_id(0)
        nw = nw_ref[...]
        lower = _tri(True)
        ltri = lower.astype(F32)
        utri = _tri(False).astype(F32)
        last = lax.broadcasted_iota(jnp.int32, (CH, DH), 0) == CH - 1
        ds_ref[...] = jnp.zeros_like(ds_ref)
        acc_ref[...] = jnp.zeros_like(acc_ref)

        @pl.when(h == 0)
        def _():
            dnw_ref[...] = jnp.zeros_like(dnw_ref)

        def chunk(i, carry):
            n = NCH - 1 - i
            rows = pl.ds(pl.multiple_of(n * CH, CH), CH)
            for j in range(HG_HP):
                cols = slice(j * DH, (j + 1) * DH)
                lb = 1.0 / (1.0 + jnp.exp(lbl_ref[1:2, cols] - lbl_ref[0:1, cols]))
                hq, hf, v, hg = hq_ref[rows, cols], hf_ref[rows, cols], hi_ref[rows, cols], hg_ref[rows, cols]
                sig, f, g, kk, q, sq = _hgrn_gates(hq, hf, lb)
                gc = _dot_f32(ltri, g)
                gl = jnp.sum(g, axis=0, keepdims=True)
                eg = jnp.exp(gc)
                ek = jnp.exp(gl - gc)
                gam = jnp.exp(gl)
                ph = jnp.exp(gc - 0.5 * gl)
                pk = jnp.exp(0.5 * gl - gc)
                qe, ke = q * eg, kk * ek
                qh, kh = q * ph, kk * pk
                st = st_ref[j, pl.ds(n, 1)][0]
                dst = ds_ref[j]
                a = jnp.where(lower, _dot(qh, kh, NT), 0.0)
                o = oraw_ref[rows, cols]
                rs = lax.rsqrt(jnp.mean(o * o, axis=-1, keepdims=True) + EPS)
                xh = o * rs
                sg = _sigmoid(hg)
                dgo = do_ref[rows, cols]
                d_on = dgo * (hg * sg)
                dhg_ref[rows, cols] = (dgo * xh * nw * (sg * (1.0 + hg * (1.0 - sg)))).astype(BF16)
                acc_ref[j, 1:2, :] += jnp.sum(d_on * xh, axis=0, keepdims=True)
                dy = d_on * nw
                do = rs * (dy - xh * jnp.mean(dy * xh, axis=-1, keepdims=True))
                dqe = _dot(do, st)
                da = jnp.where(lower, _dot(do, v, NT), 0.0)
                dv = _dot(a, do, TN) + _dot(ke, dst, NT)
                dke = _dot(v, dst)
                dgam = jnp.sum(dst * st, axis=0, keepdims=True)
                dqh = lax.dot_general(da, kh, NN, precision=lax.Precision.HIGH, preferred_element_type=F32)
                dkh = lax.dot_general(da, qh, TN, precision=lax.Precision.HIGH, preferred_element_type=F32)
                dq = dqh * ph + dqe * eg
                dk = dkh * pk + dke * ek
                dgl = jnp.sum(dke * ke, axis=0, keepdims=True) + dgam * gam
                dgc = dqh * qh - dkh * kh + dqe * qe - dke * ke + jnp.where(last, dgl, 0.0)
                dg = _dot_f32(utri, dgc)
                df = dg / f - dk
                dhf_ref[rows, cols] = (df * (1.0 - lb) * sig * (1.0 - sig)).astype(BF16)
                acc_ref[j, 0:1, :] += jnp.sum(df * (1.0 - sig), axis=0, keepdims=True)
                dhq_ref[rows, cols] = (dq * (sq * (1.0 + hq * (1.0 - sq)))).astype(BF16)
                dhi_ref[rows, cols] = dv.astype(BF16)
                ds_ref[j] = dst * gam + _dot(do, qe, TN)
            return carry

        lax.fori_loop(0, NCH, chunk, 0, unroll=HG_UNROLL)
        for j in range(HG_HP):
            dlb_ref[:, j * DH:(j + 1) * DH] = acc_ref[j, 0:1, :]
            dnw_ref[...] += acc_ref[j, 1:2, :]

    head_spec = pl.BlockSpec((S, HG_HP * DH), lambda h: (0, h))
    head_out = jax.ShapeDtypeStruct((S, HW), BF16)
    return _pcall(
        body, [proj, proj, proj, proj, lb_logits, norm_w, oraw, states, dout], phase, name="hgrn_bwd",
        out_shape=(head_out, head_out, head_out, head_out,
                   jax.ShapeDtypeStruct((1, HW), F32), jax.ShapeDtypeStruct((1, DH), F32)),
        grid=(H // HG_HP,),
        in_specs=[_proj_col_spec(0), _proj_col_spec(1), _proj_col_spec(2), _proj_col_spec(3),
                  pl.BlockSpec((2, HG_HP * DH), lambda h: (0, h)), pl.BlockSpec((1, DH), lambda h: (0, 0)),
                  head_spec, pl.BlockSpec((HG_HP, NCH, DH, DH), lambda h: (h, 0, 0, 0)), head_spec],
        out_specs=(head_spec, head_spec, head_spec, head_spec,
                   pl.BlockSpec((1, HG_HP * DH), lambda h: (0, h)), pl.BlockSpec((1, DH), lambda h: (0, 0))),
        scratch_shapes=[pltpu.VMEM((HG_HP, DH, DH), F32), pltpu.VMEM((HG_HP, 8, DH), F32)],
        compiler_params=_params(1),
    )


ATT_SCALE = DH ** -0.5
HEADS_PER_STEP = {1: 8, 4: 1, 16: 1}


def _sub_rows(ref, r, dil, cols):
    if dil == 1:
        return ref[:, cols]
    return ref[pl.ds(r, BLK, stride=dil), cols]


def _set_sub_rows(ref, r, dil, cols, val):
    if dil == 1:
        ref[:, cols] = val
    else:
        ref[pl.ds(r, BLK, stride=dil), cols] = val


def _slopes(dil):
    hp = HEADS_PER_STEP[dil]
    s = jnp.asarray([dil * 2.0 ** (-(h + 1)) for h in range(H)], F32)
    return jnp.broadcast_to(s[:, None], (H, DH)).reshape(H // hp, hp, DH)


def _rms_rows(x):
    rs = lax.rsqrt(jnp.mean(x * x, axis=-1, keepdims=True) + EPS)
    return x * rs, rs


def _att_masks():
    qi = lax.broadcasted_iota(jnp.int32, (BLK, BLK), 0)
    kj = lax.broadcasted_iota(jnp.int32, (BLK, BLK), 1)
    steps_c = qi - kj
    steps_p = qi - kj + BLK
    return steps_c, steps_p, steps_c >= 0, steps_p <= BLK


def _qk_prep(proj, q_w, k_w):
    def body(q_ref, k_ref, qw_ref, kw_ref, qn_ref, kn_ref):
        for h in range(H):
            cols = slice(h * DH, (h + 1) * DH)
            qn_ref[:, cols] = _rms_rows(q_ref[:, cols])[0] * qw_ref[...]
            kn_ref[:, cols] = _rms_rows(k_ref[:, cols])[0] * kw_ref[...]

    out = jax.ShapeDtypeStruct((S, HW), F32)
    wspec = pl.BlockSpec((1, DH), lambda i: (0, 0))
    return pl.pallas_call(
        body, name="qk_prep", out_shape=(out, out), grid=(S // TR,),
        in_specs=[pl.BlockSpec((TR, HW), lambda i: (i, 4)), pl.BlockSpec((TR, HW), lambda i: (i, 5)), wspec, wspec],
        out_specs=(_row_spec(HW), _row_spec(HW)),
        compiler_params=_params(1),
    )(proj, proj, q_w, k_w)


def _qk_norm_bwd(proj, dqn, dkn, dv, q_w, k_w):
    def body(q_ref, k_ref, dqn_ref, dkn_ref, dv_ref, qw_ref, kw_ref, dq_ref, dk_ref, dvo_ref, dqw_ref, dkw_ref):
        i = pl.program_id(0)

        @pl.when(i == 0)
        def _():
            dqw_ref[...] = jnp.zeros_like(dqw_ref)
            dkw_ref[...] = jnp.zeros_like(dkw_ref)

        dvo_ref[...] = dv_ref[...].astype(BF16)
        for x_ref, d_ref, w_ref, o_ref, dw_ref in ((q_ref, dqn_ref, qw_ref, dq_ref, dqw_ref),
                                                   (k_ref, dkn_ref, kw_ref, dk_ref, dkw_ref)):
            for h in range(H):
                cols = slice(h * DH, (h + 1) * DH)
                xh, rs = _rms_rows(x_ref[:, cols])
                d = d_ref[:, cols]
                dw_ref[...] += jnp.sum(d * xh, axis=0, keepdims=True)
                dy = d * w_ref[...]
                o_ref[:, cols] = (rs * (dy - xh * jnp.mean(dy * xh, axis=-1, keepdims=True))).astype(BF16)

    big = jax.ShapeDtypeStruct((S, HW), BF16)
    small = jax.ShapeDtypeStruct((1, DH), F32)
    wspec = pl.BlockSpec((1, DH), lambda i: (0, 0))
    return pl.pallas_call(
        body, name="qk_norm_bwd", out_shape=(big, big, big, small, small), grid=(S // TR,),
        in_specs=[pl.BlockSpec((TR, HW), lambda i: (i, 4)), pl.BlockSpec((TR, HW), lambda i: (i, 5)),
                  _row_spec(HW), _row_spec(HW), _row_spec(HW), wspec, wspec],
        out_specs=(_row_spec(HW), _row_spec(HW), _row_spec(HW), wspec, wspec),
        compiler_params=_params(1),
    )(proj, proj, dqn, dkn, dv, q_w, k_w)


def _attn_delta(do, o):
    def body(do_ref, o_ref, d_ref):
        for h in range(H):
            cols = slice(h * DH, (h + 1) * DH)
            d_ref[:, cols] = jnp.broadcast_to(jnp.sum(do_ref[:, cols] * o_ref[:, cols], axis=-1, keepdims=True), (TR, DH))

    return pl.pallas_call(
        body, name="attn_delta", out_shape=jax.ShapeDtypeStruct((S, HW), F32), grid=(S // TR,),
        in_specs=[pl.BlockSpec((TR, HW), lambda i: (i, 1)), _row_spec(HW)], out_specs=_row_spec(HW),
        compiler_params=_params(1),
    )(do, o)


def _attn_fwd(proj, qn, kn, dil, phase=None):
    hp = HEADS_PER_STEP[dil]
    rows, ng, nb = BLK * dil, H // hp, S // (BLK * dil)

    def body(q_ref, kc_ref, kp_ref, vc_ref, vp_ref, sl_ref, o_ref, lse_ref):
        n = pl.program_id(0)
        steps_c, steps_p, ok_c, ok_p = _att_masks()
        ok_p = jnp.logical_and(ok_p, n > 0)
        fc, fp = steps_c.astype(F32), steps_p.astype(F32)
        for hh in range(hp):
            cols = slice(hh * DH, (hh + 1) * DH)
            sl = sl_ref[0, hh:hh + 1, :]
            for r in range(dil):
                sub = lambda ref: _sub_rows(ref, r, dil, cols)
                qv = sub(q_ref)
                sc = jnp.where(ok_c, _dot(qv, sub(kc_ref), NT) * ATT_SCALE - sl * fc, NEG)
                sp = jnp.where(ok_p, _dot(qv, sub(kp_ref), NT) * ATT_SCALE - sl * fp, NEG)
                m = jnp.maximum(jnp.max(sc, axis=-1, keepdims=True), jnp.max(sp, axis=-1, keepdims=True))
                pc, pp = jnp.exp(sc - m), jnp.exp(sp - m)
                den = jnp.sum(pc, axis=-1, keepdims=True) + jnp.sum(pp, axis=-1, keepdims=True)
                o = (_dot(pc, sub(vc_ref)) + _dot(pp, sub(vp_ref))) / den
                _set_sub_rows(o_ref, r, dil, cols, o)
                _set_sub_rows(lse_ref, r, dil, cols, jnp.broadcast_to(m + jnp.log(den), (BLK, DH)))

    cur = pl.BlockSpec((rows, hp * DH), lambda n, g: (n, g))
    prev = pl.BlockSpec((rows, hp * DH), lambda n, g: (jnp.maximum(n - 1, 0), g))
    vcur = pl.BlockSpec((rows, hp * DH), lambda n, g: (n, 6 * ng + g))
    vprev = pl.BlockSpec((rows, hp * DH), lambda n, g: (jnp.maximum(n - 1, 0), 6 * ng + g))
    out = jax.ShapeDtypeStruct((S, HW), F32)
    return _pcall(
        body, [qn, kn, kn, proj, proj, _slopes(dil)], phase,
        name=f"attn_fwd_d{dil}", out_shape=(out, out), grid=(nb, ng),
        in_specs=[cur, cur, prev, vcur, vprev, pl.BlockSpec((1, hp, DH), lambda n, g: (g, 0, 0))],
        out_specs=(cur, cur),
        compiler_params=_params(2),
    )


def _attn_merge(outs, lses, cat):
    def body(o1, o2, o3, l1, l2, l3, cat_ref, ob_ref, of_ref, lse_ref):
        a, b, c = l1[...], l2[...], l3[...]
        m = jnp.maximum(jnp.maximum(a, b), c)
        ea, eb, ec = jnp.exp(a - m), jnp.exp(b - m), jnp.exp(c - m)
        den = ea + eb + ec
        o = (ea * o1[...] + eb * o2[...] + ec * o3[...]) / den
        ob_ref[...] = o.astype(BF16)
        of_ref[...] = o
        lse_ref[...] = m + jnp.log(den)

    f = jax.ShapeDtypeStruct((S, HW), F32)
    return pl.pallas_call(
        body, name="attn_merge", out_shape=(jax.ShapeDtypeStruct((S, 2 * HW), BF16), f, f), grid=(S // TR,),
        in_specs=[_row_spec(HW)] * 6 + [pl.BlockSpec(memory_space=pl.ANY)],
        out_specs=(pl.BlockSpec((TR, HW), lambda i: (i, 1)), _row_spec(HW), _row_spec(HW)),
        input_output_aliases={6: 0},
        compiler_params=_params(1),
    )(*outs, *lses, cat)


def _attn_bwd(proj, qn, kn, lse, delta, do, dil, acc, phase=None):
    hp = HEADS_PER_STEP[dil]
    rows, ng, nb = BLK * dil, H // hp, S // (BLK * dil)
    has_acc = acc is not None

    def body(*refs):
        q_ref, qn_ref, kp_ref, kc_ref, vp_ref, vc_ref, l_ref, ln_ref, d_ref, dn_ref, do_ref, don_ref, sl_ref = refs[:13]
        refs = refs[13:]
        if has_acc:
            aq_ref, ak_ref, av_ref = refs[:3]
            refs = refs[3:]
        dq_ref, dk_ref, dv_ref = refs
        m = pl.program_id(0)
        steps_c, steps_p, ok_c, ok_p = _att_masks()
        ok_prev = jnp.logical_and(ok_p, m > 0)
        ok_next = jnp.logical_and(ok_p, m < nb - 1)
        fc, fp = steps_c.astype(F32), steps_p.astype(F32)
        for hh in range(hp):
            cols = slice(hh * DH, (hh + 1) * DH)
            sl = sl_ref[0, hh:hh + 1, :]
            for r in range(dil):
                sub = lambda ref: _sub_rows(ref, r, dil, cols)
                qv, qnv, kcv, kpv = sub(q_ref), sub(qn_ref), sub(kc_ref), sub(kp_ref)
                vc, vp = sub(vc_ref), sub(vp_ref)
                dov, donv = sub(do_ref), sub(don_ref)
                lse_m, lse_n = sub(l_ref)[:, 0:1], sub(ln_ref)[:, 0:1]
                delta_m, delta_n = sub(d_ref)[:, 0:1], sub(dn_ref)[:, 0:1]
                p_c = jnp.where(ok_c, jnp.exp(_dot(qv, kcv, NT) * ATT_SCALE - sl * fc - lse_m), 0.0)
                p_p = jnp.where(ok_prev, jnp.exp(_dot(qv, kpv, NT) * ATT_SCALE - sl * fp - lse_m), 0.0)
                p_n = jnp.where(ok_next, jnp.exp(_dot(qnv, kcv, NT) * ATT_SCALE - sl * fp - lse_n), 0.0)
                ds_c = p_c * (_dot(dov, vc, NT) - delta_m)
                ds_p = p_p * (_dot(dov, vp, NT) - delta_m)
                ds_n = p_n * (_dot(donv, vc, NT) - delta_n)
                dqn = (_dot(ds_c, kcv) + _dot(ds_p, kpv)) * ATT_SCALE
                dkn = (_dot(ds_c, qv, TN) + _dot(ds_n, qnv, TN)) * ATT_SCALE
                dv = _dot(p_c, dov, TN) + _dot(p_n, donv, TN)
                if has_acc:
                    dqn, dkn, dv = dqn + sub(aq_ref), dkn + sub(ak_ref), dv + sub(av_ref)
                _set_sub_rows(dq_ref, r, dil, cols, dqn)
                _set_sub_rows(dk_ref, r, dil, cols, dkn)
                _set_sub_rows(dv_ref, r, dil, cols, dv)

    def shifted(shift, m):
        if shift < 0:
            return jnp.maximum(m - 1, 0)
        if shift > 0:
            return jnp.minimum(m + 1, nb - 1)
        return m

    def spec(shift, group=0):
        return pl.BlockSpec((rows, hp * DH), lambda m, g: (shifted(shift, m), group * ng + g))

    ins = [qn, qn, kn, kn, proj, proj, lse, lse, delta, delta, do, do, _slopes(dil)]
    in_specs = [spec(0), spec(1), spec(-1), spec(0), spec(-1, 6), spec(0, 6), spec(0), spec(1), spec(0), spec(1),
                spec(0, 1), spec(1, 1), pl.BlockSpec((1, hp, DH), lambda m, g: (g, 0, 0))]
    if has_acc:
        ins += list(acc)
        in_specs += [spec(0)] * 3
    big = jax.ShapeDtypeStruct((S, HW), F32)
    return _pcall(
        body, ins, phase, name=f"attn_bwd_d{dil}", out_shape=(big, big, big), grid=(nb, ng),
        in_specs=in_specs, out_specs=(spec(0),) * 3,
        compiler_params=_params(2),
    )


TC = 512
NCT = DFF // TC


def _shift_rows(a, k):
    rows = lax.broadcasted_iota(jnp.int32, a.shape, 0)
    rolled = pltpu.roll(a, k % S, 0)
    if k > 0:
        return jnp.where(rows >= k, rolled, 0.0)
    return jnp.where(rows < S + k, rolled, 0.0)


def _conv_pre(a, w_ref, b_ref):
    return b_ref[...] + w_ref[0:1, :] * _shift_rows(a, 2) + w_ref[1:2, :] * _shift_rows(a, 1) + w_ref[2:3, :] * a


def _conv_gate_fwd(u, conv_w, conv_b, name, tiles=(0, NCT), into=None, phase=None):
    t0, t1 = tiles

    def body(a_ref, g_ref, w_ref, b_ref, *rest):
        pre = _conv_pre(a_ref[...].astype(F32), w_ref, b_ref)
        rest[-1][...] = (pre * _sigmoid(pre) * g_ref[...].astype(F32)).astype(BF16)

    args = [u, u, conv_w, conv_b]
    in_specs = [pl.BlockSpec((S, TC), lambda i: (0, t0 + i)), pl.BlockSpec((S, TC), lambda i: (0, NCT + t0 + i)),
                pl.BlockSpec((3, TC), lambda i: (0, t0 + i)), pl.BlockSpec((1, TC), lambda i: (0, t0 + i))]
    kw = {}
    if into is not None:
        args.append(into)
        in_specs.append(pl.BlockSpec(memory_space=pl.ANY))
        kw["input_output_aliases"] = {4: 0}
    return _pcall(
        body, args, phase, name=name, out_shape=jax.ShapeDtypeStruct((S, DFF), BF16), grid=(t1 - t0,),
        in_specs=in_specs, out_specs=pl.BlockSpec((S, TC), lambda i: (0, t0 + i)),
        compiler_params=_params(1), **kw,
    )


def _conv_gate_bwd(dy, u, conv_w, conv_b, phase=None):
    def body(dy_ref, a_ref, g_ref, w_ref, b_ref, du_ref, db_ref, dw_ref, da_buf, dg_buf, sems):
        i = pl.program_id(0)
        slot = i % 2

        def writes(step, s):
            col = pl.multiple_of(step * TC, TC)
            return (pltpu.make_async_copy(da_buf.at[s], du_ref.at[:, pl.ds(col, TC)], sems.at[0, s]),
                    pltpu.make_async_copy(dg_buf.at[s], du_ref.at[:, pl.ds(DFF + col, TC)], sems.at[1, s]))

        @pl.when(i >= 2)
        def _():
            for cp in writes(i - 2, slot):
                cp.wait()

        a = a_ref[...].astype(F32)
        dyv = dy_ref[...].astype(F32)
        pre = _conv_pre(a, w_ref, b_ref)
        sg = _sigmoid(pre)
        dg_buf[slot] = (dyv * pre * sg).astype(BF16)
        dpre = dyv * g_ref[...].astype(F32) * (sg * (1.0 + pre * (1.0 - sg)))
        da = w_ref[2:3, :] * dpre + w_ref[1:2, :] * _shift_rows(dpre, -1) + w_ref[0:1, :] * _shift_rows(dpre, -2)
        da_buf[slot] = da.astype(BF16)
        for cp in writes(i, slot):
            cp.start()
        db_ref[...] = jnp.sum(dpre, axis=0, keepdims=True)
        dw_ref[0:1, :] = jnp.sum(dpre * _shift_rows(a, 2), axis=0, keepdims=True)
        dw_ref[1:2, :] = jnp.sum(dpre * _shift_rows(a, 1), axis=0, keepdims=True)
        dw_ref[2:3, :] = jnp.sum(dpre * a, axis=0, keepdims=True)

        @pl.when(i == NCT - 1)
        def _():
            for cp in writes(i - 1, 1 - slot) + writes(i, slot):
                cp.wait()

    col = pl.BlockSpec((S, TC), lambda i: (0, i))
    return _pcall(
        body, [dy, u, u, conv_w, conv_b], phase, name="conv_gate_bwd",
        out_shape=(jax.ShapeDtypeStruct((S, 2 * DFF), BF16), jax.ShapeDtypeStruct((1, DFF), F32),
                   jax.ShapeDtypeStruct((3, DFF), F32)),
        grid=(NCT,),
        in_specs=[col, col, pl.BlockSpec((S, TC), lambda i: (0, NCT + i)),
                  pl.BlockSpec((3, TC), lambda i: (0, i)), pl.BlockSpec((1, TC), lambda i: (0, i))],
        out_specs=(pl.BlockSpec(memory_space=pl.ANY), pl.BlockSpec((1, TC), lambda i: (0, i)),
                   pl.BlockSpec((3, TC), lambda i: (0, i))),
        scratch_shapes=[pltpu.VMEM((2, S, TC), BF16), pltpu.VMEM((2, S, TC), BF16), pltpu.SemaphoreType.DMA((2, 2))],
        compiler_params=_params(1),
    )


def _out_loss(z, x1, target, gate):
    def body(z_ref, x_ref, t_ref, g_ref, do_ref, dz_ref, dg_ref, loss_ref):
        i = pl.program_id(0)
        zv = z_ref[...]
        gv = g_ref[...]
        err = x_ref[...] + gv * zv - t_ref[...]
        dout = err * (1.0 / D)
        do_ref[...] = dout
        dz_ref[...] = (dout * gv).astype(BF16)

        @pl.when(i == 0)
        def _():
            dg_ref[...] = jnp.zeros_like(dg_ref)
            loss_ref[...] = jnp.zeros_like(loss_ref)

        dg_ref[...] += jnp.sum(dout * zv, axis=0, keepdims=True)
        part = jnp.sum(jnp.sum(err * err, axis=0, keepdims=True), axis=-1, keepdims=True) * (0.5 / D)
        lane = lax.broadcasted_iota(jnp.int32, (1, 128), 1)
        loss_ref[...] += jnp.where(lane == 0, part, 0.0)

    return pl.pallas_call(
        body, name="out_loss",
        out_shape=(jax.ShapeDtypeStruct((S, D), F32), jax.ShapeDtypeStruct((S, D), BF16),
                   jax.ShapeDtypeStruct((1, D), F32), jax.ShapeDtypeStruct((1, 128), F32)),
        grid=(S // TR,),
        in_specs=[_row_spec(), _row_spec(), _row_spec(), _vec_spec()],
        out_specs=(_row_spec(), _row_spec(), _vec_spec(), _vec_spec(128)),
        compiler_params=_params(1),
    )(z, x1, target, gate)


ADA_COLS = 6 * D // N_CHIPS
TA = 512


def _ada_fwd(c_all, w_shard, b_shard):
    def body(c_ref, w_ref, b_ref, o_ref):
        cv = c_ref[...]
        o_ref[...] = _dot_f32(cv * _sigmoid(cv), w_ref[...]) + b_ref[...]

    return pl.pallas_call(
        body, name="ada_fwd", out_shape=jax.ShapeDtypeStruct((N_DEV, ADA_COLS), F32), grid=(ADA_COLS // TA,),
        in_specs=[pl.BlockSpec((N_DEV, D), lambda j: (0, 0)), pl.BlockSpec((D, TA), lambda j: (0, j)),
                  pl.BlockSpec((1, TA), lambda j: (0, j))],
        out_specs=pl.BlockSpec((N_DEV, TA), lambda j: (0, j)),
        compiler_params=_params(1),
    )(c_all, w_shard, b_shard)


ADA_ROWS = 128


def _ada_bwd_adamw(c_all, dmod_shard, w, m, v):
    def body(c_ref, d_ref, w_ref, m_ref, v_ref, g_ref, dl_ref, mo_ref, vo_ref):
        cv = c_ref[...]
        gv = lax.dot_general(cv * _sigmoid(cv), d_ref[...], TN, precision=lax.Precision.HIGHEST,
                             preferred_element_type=F32)
        g_ref[...] = gv
        m2 = ADAM_B1 * m_ref[...] + (1.0 - ADAM_B1) * gv
        v2 = ADAM_B2 * v_ref[...] + (1.0 - ADAM_B2) * (gv * gv)
        m_hat = m2 / (1.0 - ADAM_B1 ** ADAM_STEP)
        v_hat = v2 / (1.0 - ADAM_B2 ** ADAM_STEP)
        dl_ref[...] = -ADAM_LR * (m_hat / (jnp.sqrt(v_hat) + ADAM_EPS) + ADAM_WD * w_ref[...])
        mo_ref[...] = m2
        vo_ref[...] = v2

    spec = pl.BlockSpec((ADA_ROWS, ADA_COLS), lambda i: (i, 0))
    out = jax.ShapeDtypeStruct((D, ADA_COLS), F32)
    return pl.pallas_call(
        body, name="ada_bwd_adamw", out_shape=(out,) * 4, grid=(D // ADA_ROWS,),
        in_specs=[pl.BlockSpec((N_DEV, ADA_ROWS), lambda i: (0, i)), pl.BlockSpec((N_DEV, ADA_COLS), lambda i: (0, 0)),
                  spec, spec, spec],
        out_specs=(spec,) * 4,
        compiler_params=_params(1),
    )(c_all, dmod_shard, w, m, v)


def _sum_rows(g, name):
    rows, n = g.shape

    def body(g_ref, o_ref):
        acc = g_ref[0:1, :]
        for i in range(1, rows):
            acc = acc + g_ref[i:i + 1, :]
        o_ref[...] = acc

    return pl.pallas_call(
        body, name=name, out_shape=jax.ShapeDtypeStruct((1, n), F32),
        in_specs=[VMEM_SPEC], out_specs=VMEM_SPEC,
    )(g)


def _lb_grad(lb_logits, dlb):
    def body(l_ref, d_ref, o_ref):
        p = 1.0 / (1.0 + jnp.exp(l_ref[1:2, :] - l_ref[0:1, :]))
        t = d_ref[...] * p * (1.0 - p)
        o_ref[0:1, :] = t
        o_ref[1:2, :] = -t

    return pl.pallas_call(
        body, name="lb_grad", out_shape=jax.ShapeDtypeStruct((2, HW), F32),
        in_specs=[VMEM_SPEC, VMEM_SPEC], out_specs=VMEM_SPEC,
    )(lb_logits, dlb)


def _adamw(w, g, m, v, name, copy_grad=False):
    rows, cols = w.shape
    tr = rows
    if rows * cols * 4 > ADAM_BLOCK_BYTES:
        tr = _pick(rows, [t for t in (256, 128, 64, 32, 16, 8) if t * cols * 4 <= ADAM_BLOCK_BYTES])

    def body(w_ref, g_ref, m_ref, v_ref, d_ref, mo_ref, vo_ref, *go_ref):
        gv = g_ref[...]
        if copy_grad:
            go_ref[0][...] = gv
        m2 = ADAM_B1 * m_ref[...] + (1.0 - ADAM_B1) * gv
        v2 = ADAM_B2 * v_ref[...] + (1.0 - ADAM_B2) * (gv * gv)
        m_hat = m2 / (1.0 - ADAM_B1 ** ADAM_STEP)
        v_hat = v2 / (1.0 - ADAM_B2 ** ADAM_STEP)
        d_ref[...] = -ADAM_LR * (m_hat / (jnp.sqrt(v_hat) + ADAM_EPS) + ADAM_WD * w_ref[...])
        mo_ref[...] = m2
        vo_ref[...] = v2

    spec = pl.BlockSpec((tr, cols), lambda i: (i, 0))
    out = jax.ShapeDtypeStruct((rows, cols), F32)
    n_out = 4 if copy_grad else 3
    return pl.pallas_call(
        body, name=name, out_shape=(out,) * n_out, grid=(rows // tr,),
        in_specs=[spec] * 4, out_specs=(spec,) * n_out,
        compiler_params=_params(1),
    )(w, g, m, v)


SC_TILES = 32
SC_LANES = 16
SC_COL_PARTS = 2
SC_CHUNK_ROWS = 8


def _adamw_sc(w, g, m, v, name):
    rows, cols = w.shape
    band, part = rows // (SC_TILES // SC_COL_PARTS), cols // SC_COL_PARTS
    assert band % SC_CHUNK_ROWS == 0 and part % SC_LANES == 0, (rows, cols)

    def body(w_hbm, g_hbm, m_hbm, v_hbm, d_hbm, mo_hbm, vo_hbm, go_hbm, wb, gb, mb, vb, db):
        tile = lax.axis_index("sc_subcore") * 2 + lax.axis_index("sc_core")
        row0 = (tile // SC_COL_PARTS) * band
        col0 = (tile % SC_COL_PARTS) * part

        @pl.loop(0, band, step=SC_CHUNK_ROWS)
        def _(r):
            at = lambda ref: ref.at[pl.ds(row0 + r, SC_CHUNK_ROWS), pl.ds(col0, part)]
            pltpu.sync_copy(at(w_hbm), wb)
            pltpu.sync_copy(at(g_hbm), gb)
            pltpu.sync_copy(gb, at(go_hbm))
            pltpu.sync_copy(at(m_hbm), mb)
            pltpu.sync_copy(at(v_hbm), vb)

            @pl.loop(0, SC_CHUNK_ROWS)
            def _(i):
                @pl.loop(0, part, step=SC_LANES)
                def _(j):
                    sl = (i, pl.ds(j, SC_LANES))
                    gv = gb[sl]
                    m2 = ADAM_B1 * mb[sl] + (1.0 - ADAM_B1) * gv
                    v2 = ADAM_B2 * vb[sl] + (1.0 - ADAM_B2) * (gv * gv)
                    m_hat = m2 / (1.0 - ADAM_B1 ** ADAM_STEP)
                    v_hat = v2 / (1.0 - ADAM_B2 ** ADAM_STEP)
                    db[sl] = -ADAM_LR * (m_hat / (jnp.sqrt(v_hat) + ADAM_EPS) + ADAM_WD * wb[sl])
                    mb[sl] = m2
                    vb[sl] = v2

            pltpu.sync_copy(db, at(d_hbm))
            pltpu.sync_copy(mb, at(mo_hbm))
            pltpu.sync_copy(vb, at(vo_hbm))

    out = jax.ShapeDtypeStruct((rows, cols), F32)
    buf = pltpu.VMEM((SC_CHUNK_ROWS, part), F32)
    return pl.kernel(
        body, name=name, out_type=(out, out, out, out),
        mesh=plsc.VectorSubcoreMesh(core_axis_name="sc_core", subcore_axis_name="sc_subcore"),
        scratch_types=[buf] * 5,
    )(w, g, m, v)


W_IN, W_OUT, W_UP, W_DOWN = range(4)
IN_CHUNKS = 4
W_INQ = 4


def _sequence_step(x, target, mod, norm1_w, lb_logits, hg_norm_w, q_norm_w, k_norm_w, norm2_w, conv_w, conv_b, net):
    shift1, scale1, gate1, shift2, scale2, gate2 = [mod[:, i * D:(i + 1) * D] for i in range(6)]

    def run(name, fn, *args, **kw):
        phase = net.host(name)
        if phase is None:
            return fn(*args, **kw)
        res, comm = fn(*args, phase=phase, **kw)
        net.done(name, comm)
        return res

    h = _norm_mod(x, norm1_w, scale1, shift1, "norm1_fwd")
    proj = run("proj_fwd", _matmul, h, net.full[W_IN], "nn", F32, "proj_fwd")
    cat, o_raw, states = run("hgrn_fwd", _hgrn_fwd, proj, lb_logits, hg_norm_w)
    qn, kn = _qk_prep(proj, q_norm_w, k_norm_w)
    att = [run(f"attn_fwd_d{dil}", _attn_fwd, proj, qn, kn, dil) for dil in DILS]
    cat, b_out_f32, lse = _attn_merge([t[0] for t in att], [t[1] for t in att], cat)
    mix = run("mix_fwd", _matmul, cat, net.full[W_OUT], "nn", F32, "mix_fwd")
    x1, h2 = _resid_norm_mod(x, mix, gate1, norm2_w, scale2, shift2, "norm2_fwd")
    u = run("up_fwd", _matmul, h2, net.full[W_UP], "nn", BF16, "up_fwd")
    yact = run("conv_gate_fwd_a", _conv_gate_fwd, u, conv_w, conv_b, "conv_gate_fwd_a", tiles=(0, NCT // 2 + 1))
    yact = run("conv_gate_fwd_b", _conv_gate_fwd, u, conv_w, conv_b, "conv_gate_fwd_b", tiles=(NCT // 2 + 1, NCT),
               into=yact)
    z =_matmul(yact, net.full[W_DOWN], "nn", F32, "down_fwd")
    dout, dz, dgate2, loss_row = _out_loss(z, x1, target, gate2)

    net.grad[W_DOWN] = _matmul(yact, dz, "tn", BF16, "down_bwd_w")
    dyact = run("down_bwd_x", _matmul, dz, net.full[W_DOWN], "nt", BF16, "down_bwd_x")
    du, dconv_b, dconv_w = run("conv_gate_bwd", _conv_gate_bwd, dyact, u, conv_w, conv_b)
    net.grad[W_UP] = run("up_bwd_w", _matmul, h2, du, "tn", BF16, "up_bwd_w")
    dh2 = run("up_bwd_x", _matmul, du, net.full[W_UP], "nt", F32, "up_bwd_x")
    dx1, dshift2, dscale2, dnorm2, dgate1, dmix = run(
        "norm2_bwd", _norm_mod_bwd, dh2, x1, norm2_w, scale2, dout, "norm2_bwd", mix=mix, gate=gate1)
    net.grad[W_OUT] = run("mix_bwd_w", _matmul, cat, dmix, "tn", BF16, "mix_bwd_w")
    dcat = run("mix_bwd_x", _matmul, dmix, net.full[W_OUT], "nt", F32, "mix_bwd_x")
    dhq, dhf, dhi, dhg, dlb, dhg_norm = run("hgrn_bwd", _hgrn_bwd, proj, lb_logits, hg_norm_w, o_raw, states, dcat)
    delta = _attn_delta(dcat, b_out_f32)
    acc = None
    for dil in DILS:
        acc = run(f"attn_bwd_d{dil}", _attn_bwd, proj, qn, kn, lse, delta, dcat, dil, acc)
    daq, dak, dav, dq_norm, dk_norm = _qk_norm_bwd(proj, *acc, q_norm_w, k_norm_w)
    dproj = jnp.concatenate([dhq, dhf, dhi, dhg, daq, dak, dav], axis=1)
    for q in range(IN_CHUNKS):
        net.grad[W_INQ + q] = run(f"proj_bwd_w_{q}", _matmul, h, dproj, "tn", BF16, f"proj_bwd_w_{q}",
                                  m_blocks=(D // IN_CHUNKS, [q]))
    dh = run("proj_bwd_x", _matmul, dproj, net.full[W_IN], "nt", F32, "proj_bwd_x")
    grad_x, dshift1, dscale1, dnorm1 = run("norm1_bwd", _norm_mod_bwd, dh, x, norm1_w, scale1, dx1, "norm1_bwd")

    dmod = jnp.concatenate([dshift1, dscale1, dgate1, dshift2, dscale2, dgate2], axis=1)
    small = dict(norm1_w=dnorm1, lb=dlb, hg_norm_w=dhg_norm, q_norm_w=dq_norm, k_norm_w=dk_norm,
                 norm2_w=dnorm2, conv_b=dconv_b, conv_w=dconv_w)
    return loss_row, grad_x, dmod, small


def _place():
    x, y, c = lax.axis_index("x"), lax.axis_index("y"), lax.axis_index("c")
    others = [(1 - x, y), (x, 1 - y), (1 - x, 1 - y)]
    return x, y, c, others


def _remote(src, dst, send_sems, recv_sems, k, to):
    return pltpu.make_async_remote_copy(src_ref=src, dst_ref=dst, send_sem=send_sems.at[k], recv_sem=recv_sems.at[k],
                                        device_id=to, device_id_type=MESH)


class _Phase:
    def __init__(self):
        self.ins, self.outs, self.aliases, self.groups, self.n = [], [], {}, [], 0

    def add(self, ins, outs, aliases, n, copies):
        i0, o0 = len(self.ins), len(self.outs)
        self.ins += list(ins)
        self.outs += list(outs)
        self.aliases.update({i0 + i: o0 + o for i, o in aliases.items()})
        self.groups.append((slice(i0, i0 + len(ins)), slice(o0, o0 + len(outs)), copies))
        self.n += n
        return slice(o0, o0 + len(outs))

    def build(self, cin, cout, send_sems, recv_sems, landing):
        res = []
        for si, so, copies in self.groups:
            for src, dst, land, peer in copies(cin[si], cout[so]):
                k = len(res)
                res.append(_remote(land, land, send_sems, recv_sems, k, peer) if landing
                           else _remote(src, dst, send_sems, recv_sems, k, peer))
        assert len(res) == self.n
        return res


def _pcall(body, args, phase=None, **kw):
    if phase is None or not phase.groups:
        res = pl.pallas_call(body, **kw)(*args)
        return res if phase is None else (res, ())
    grid = tuple(kw.pop("grid", ()))
    out_shape, out_specs = kw.pop("out_shape"), kw.pop("out_specs")
    single = not isinstance(out_shape, (tuple, list))
    out_shape = [out_shape] if single else list(out_shape)
    out_specs = [out_specs] if single else list(out_specs)
    scratch = list(kw.pop("scratch_shapes", ()))
    aliases = dict(kw.pop("input_output_aliases", {}))
    n_in, n_out, n_ci, n_co = len(args), len(out_shape), len(phase.ins), len(phase.outs)
    aliases.update({n_in + i: n_out + o for i, o in phase.aliases.items()})

    def hosted(*refs):
        ins, cin = refs[:n_in], refs[n_in:n_in + n_ci]
        outs = refs[n_in + n_ci:n_in + n_ci + n_out]
        cout = refs[n_in + n_ci + n_out:n_in + n_ci + n_out + n_co]
        scr, send_sems, recv_sems = refs[n_in + n_ci + n_out + n_co:-2], refs[-2], refs[-1]
        ids = [pl.program_id(a) for a in range(len(grid))]

        def start():
            for cp in phase.build(cin, cout, send_sems, recv_sems, False):
                cp.start()

        def finish():
            for cp in phase.build(cin, cout, send_sems, recv_sems, False):
                cp.wait_send()
            for cp in phase.build(cin, cout, send_sems, recv_sems, True):
                cp.wait_recv()

        if grid:
            first = functools.reduce(jnp.logical_and, [i == 0 for i in ids])
            last = functools.reduce(jnp.logical_and, [i == g - 1 for i, g in zip(ids, grid)])
            pl.when(first)(start)
            body(*ins, *outs, *scr)
            pl.when(last)(finish)
        else:
            start()
            body(*ins, *outs, *scr)
            finish()

    res = pl.pallas_call(
        hosted, out_shape=tuple(out_shape + phase.outs), grid=grid,
        in_specs=list(kw.pop("in_specs")) + [HBM_SPEC] * n_ci, out_specs=tuple(out_specs + [HBM_SPEC] * n_co),
        input_output_aliases=aliases,
        scratch_shapes=scratch + [pltpu.SemaphoreType.DMA((phase.n,)), pltpu.SemaphoreType.DMA((phase.n,))],
        **kw,
    )(*args, *phase.ins)
    outs = res[:n_out]
    return (outs[0] if single else tuple(outs)), tuple(res[n_out:])


def _comm_only(name, phase):
    return _pcall(lambda: None, [], phase, name=name, out_shape=(), in_specs=[], out_specs=())[1]


def _all_gather_rows(block, name, phase=None):
    m_per, n = block.shape

    def body(x_ref, out_ref, send_sems, recv_sems, local_sem):
        x, y, c, chips = _place()
        me, sibling = (x, y, c), (x, y, 1 - c)

        def rows(px, py, pc):
            return out_ref.at[pl.ds((4 * px + 2 * py + pc) * m_per, m_per), :]

        def copy(k, blk, to, src=None):
            return _remote(rows(*blk) if src is None else src, rows(*blk), send_sems, recv_sems, k, to)

        mine = pltpu.make_async_copy(x_ref, rows(*me), local_sem)
        mine.start()
        first = [copy(0, me, sibling, src=x_ref)]
        first += [copy(1 + j, me, (*chip, c), src=x_ref) for j, chip in enumerate(chips)]
        for cp in first:
            cp.start()
        passed = [copy(4 + j, (*chip, c), sibling) for j, chip in enumerate(chips)]
        for j, chip in enumerate(chips):
            copy(1 + j, (*chip, c), me).wait_recv()
            passed[j].start()
        copy(0, sibling, me).wait_recv()
        for j, chip in enumerate(chips):
            copy(4 + j, (*chip, 1 - c), me).wait_recv()
        for cp in first + passed:
            cp.wait_send()
        mine.wait()

    return _pcall(
        body, [block], phase, name=name, out_shape=jax.ShapeDtypeStruct((N_DEV * m_per, n), block.dtype),
        in_specs=[VMEM_SPEC], out_specs=VMEM_SPEC,
        scratch_shapes=[pltpu.SemaphoreType.DMA((7,)), pltpu.SemaphoreType.DMA((7,)), pltpu.SemaphoreType.DMA],
    )


class _Geo:
    def __init__(self, kind, shard_shape):
        self.kind = kind
        self.shard_shape = tuple(shard_shape)
        self.hr, self.hc = shard_shape[0] // 2, shard_shape[1]
        if kind == "col":
            self.full_shape = (shard_shape[0], N_CHIPS * shard_shape[1])
        else:
            self.full_shape = (N_CHIPS * shard_shape[0], shard_shape[1])

    def full_half(self, ref, j, c):
        return self.piece(ref, j, c, 0, 1, 1)

    def piece(self, ref, j, c, p0, p1, pieces, part=None):
        pr = self.hr // pieces
        off, n = p0 * pr, (p1 - p0) * pr
        if part is not None:
            top = (n // 32) * 16
            off, n = (off, top) if part == 0 else (off + top, n - top)
        if self.kind == "col":
            return ref.at[pl.ds(pl.multiple_of(c * self.hr + off, 16), n),
                          pl.ds(pl.multiple_of(j * self.hc, 128), self.hc)]
        return ref.at[pl.ds(pl.multiple_of((2 * j + c) * self.hr + off, 16), n), :]


WEIGHT_KINDS = ("col", "row", "col", "row")
NW = len(WEIGHT_KINDS)


def _comm_call(body, name, ins, outs, n_remote, aliases=None):
    return pl.pallas_call(
        body, name=name, out_shape=tuple(outs),
        in_specs=[HBM_SPEC] * len(ins), out_specs=tuple([HBM_SPEC] * len(outs)),
        input_output_aliases=aliases or {},
        scratch_shapes=[pltpu.SemaphoreType.DMA((n_remote,)), pltpu.SemaphoreType.DMA((n_remote,))],
    )(*ins)


ROW_TILES = (256, 176, 128, 64, 32, 16)


def _cast_into_full(shard, geo, place, name):
    rs, cs = shard.shape
    tr = _pick(rs, ROW_TILES)
    nb = rs // tr

    def body(place_ref, s_ref, o_ref):
        o_ref[...] = s_ref[...].astype(BF16)

    if geo.kind == "col":
        out_map = lambda i, place_ref: (i, place_ref[0])
    else:
        out_map = lambda i, place_ref: (place_ref[0] * nb + i, 0)
    return pl.pallas_call(
        body, name=name, out_shape=jax.ShapeDtypeStruct(geo.full_shape, BF16),
        grid_spec=pltpu.PrefetchScalarGridSpec(
            num_scalar_prefetch=1, grid=(nb,),
            in_specs=[pl.BlockSpec((tr, cs), lambda i, place_ref: (i, 0))],
            out_specs=pl.BlockSpec((tr, cs), out_map)),
        compiler_params=_params(1),
    )(place, shard)


def _gather_weight(full, geo, pieces, name):
    per = 8

    def body(in_ref, ref, send_sems, recv_sems):
        x, y, c, _ = _place()
        j, jx, jy, jo = 2 * x + y, 2 * (1 - x) + y, 2 * x + (1 - y), 2 * (1 - x) + (1 - y)
        nx, ny, sib = (1 - x, y, c), (x, 1 - y, c), (x, y, 1 - c)

        def cp(region, k, to):
            return _remote(region, region, send_sems, recv_sems, k, to)

        def reg(chip, p, part=None, core=c):
            return geo.piece(ref, chip, core, p, p + 1, pieces, part)

        sent = []
        for p in range(pieces):
            sent += [cp(reg(j, p), per * p, nx), cp(reg(j, p), per * p + 1, ny)]
        for d in sent:
            d.start()
        for p in range(pieces):
            cp(reg(jx, p), per * p, nx).wait_recv()
            new = [cp(reg(jx, p, 0), per * p + 2, ny), cp(reg(jx, p), per * p + 4, sib)]
            cp(reg(jy, p), per * p + 1, ny).wait_recv()
            new += [cp(reg(jy, p, 1), per * p + 3, nx), cp(reg(jy, p), per * p + 5, sib)]
            for d in new:
                d.start()
            sent += new
        for p in range(pieces):
            cp(reg(jo, p, 0), per * p + 2, ny).wait_recv()
            cp(reg(jo, p, 1), per * p + 3, nx).wait_recv()
            new = [cp(reg(jo, p, 0), per * p + 6, sib), cp(reg(jo, p, 1), per * p + 7, sib)]
            for d in new:
                d.start()
            sent += new
        for p in range(pieces):
            cp(reg(jx, p, None, 1 - c), per * p + 4, sib).wait_recv()
            cp(reg(jy, p, None, 1 - c), per * p + 5, sib).wait_recv()
            cp(reg(jo, p, 0, 1 - c), per * p + 6, sib).wait_recv()
            cp(reg(jo, p, 1, 1 - c), per * p + 7, sib).wait_recv()
        for d in sent:
            d.wait_send()

    return _comm_call(body, name, [full], [_same(full)], per * pieces, aliases={0: 0})[0]


def _same(a):
    return jax.ShapeDtypeStruct(a.shape, a.dtype)


def _gather_ici(full, geo, p0, p1, pieces):
    def copies(ins, outs):
        x, y, c, _ = _place()
        mine = geo.piece(outs[0], 2 * x + y, c, p0, p1, pieces)
        return [(mine, mine, geo.piece(outs[0], 2 * ox + oy, c, p0, p1, pieces), (ox, oy, c))
                for ox, oy in ((1 - x, y), (x, 1 - y))]

    return dict(ins=[full], outs=[_same(full)], aliases={0: 0}, n=2, copies=copies)


def _gather_relay(full, geo, p0, p1, pieces):
    def copies(ins, outs):
        x, y, c, _ = _place()
        jx, jy, jo = 2 * (1 - x) + y, 2 * x + (1 - y), 2 * (1 - x) + (1 - y)
        reg = lambda chip, part: geo.piece(outs[0], chip, c, p0, p1, pieces, part)
        return [(reg(jx, 0), reg(jx, 0), reg(jo, 0), (x, 1 - y, c)), (reg(jy, 1), reg(jy, 1), reg(jo, 1), (1 - x, y, c))]

    return dict(ins=[full], outs=[_same(full)], aliases={0: 0}, n=2, copies=copies)


def _gather_d2d(full, geo):
    def copies(ins, outs):
        x, y, c, chips = _place()
        return [(geo.full_half(outs[0], 2 * ox + oy, c), geo.full_half(outs[0], 2 * ox + oy, c),
                 geo.full_half(outs[0], 2 * ox + oy, 1 - c), (x, y, 1 - c)) for ox, oy in chips]

    return dict(ins=[full], outs=[_same(full)], aliases={0: 0}, n=3, copies=copies)


def _swap_d2d(grad, geo):
    def copies(ins, outs):
        x, y, c, _ = _place()
        return [(geo.full_half(ins[0], j, 1 - c), outs[0].at[j], outs[0].at[j], (x, y, 1 - c)) for j in range(N_CHIPS)]

    return dict(ins=[grad], outs=[jax.ShapeDtypeStruct((N_CHIPS, geo.hr, geo.hc), BF16)], aliases={}, n=N_CHIPS,
                copies=copies)


def _scatter_ici(pair, recv, geo, p0, p1, pieces):
    pr = geo.hr // pieces
    rows = pl.ds(p0 * pr, (p1 - p0) * pr)

    def copies(ins, outs):
        x, y, c, chips = _place()
        return [(ins[0].at[2 * ox + oy, rows, :], outs[0].at[k, rows, :], outs[0].at[k, rows, :], (ox, oy, c))
                for k, (ox, oy) in enumerate(chips)]

    out = jax.ShapeDtypeStruct((3, geo.hr, geo.hc), BF16)
    if recv is None:
        return dict(ins=[pair], outs=[out], aliases={}, n=3, copies=copies)
    return dict(ins=[pair, recv], outs=[out], aliases={1: 0}, n=3, copies=copies)


def _join_d2d(shard, geo, row0=0):
    def copies(ins, outs):
        x, y, c, _ = _place()
        mine = outs[0].at[pl.ds(pl.multiple_of(row0 + c * geo.hr, 8), geo.hr), :]
        other = outs[0].at[pl.ds(pl.multiple_of(row0 + (1 - c) * geo.hr, 8), geo.hr), :]
        return [(mine, mine, other, (x, y, 1 - c))]

    return dict(ins=[shard], outs=[_same(shard)], aliases={0: 0}, n=1, copies=copies)


def _add_pair(grad, got, geo, place, name):
    tr = _pick(geo.hr, ROW_TILES)
    nb = geo.hr // tr

    def body(place_ref, a_ref, b_ref, o_ref):
        o_ref[0] = (a_ref[...].astype(F32) + b_ref[0].astype(F32)).astype(BF16)

    if geo.kind == "col":
        own_map = lambda j, i, place_ref: (place_ref[1] * nb + i, j)
    else:
        own_map = lambda j, i, place_ref: ((2 * j + place_ref[1]) * nb + i, 0)
    spec = pl.BlockSpec((1, tr, geo.hc), lambda j, i, place_ref: (j, i, 0))
    return pl.pallas_call(
        body, name=name, out_shape=jax.ShapeDtypeStruct((N_CHIPS, geo.hr, geo.hc), BF16),
        grid_spec=pltpu.PrefetchScalarGridSpec(
            num_scalar_prefetch=1, grid=(N_CHIPS, nb),
            in_specs=[pl.BlockSpec((tr, geo.hc), own_map), spec], out_specs=spec),
        compiler_params=_params(2),
    )(place, grad, got)


def _add_four(pair, recv, geo, place, name, rows=None, row0=0, into=None):
    tr = _pick(geo.hr, ROW_TILES)
    nb = geo.hr // tr
    rows = 2 * geo.hr if rows is None else rows

    def body(place_ref, p_ref, r_ref, *rest):
        rest[-1][...] = ((p_ref[0].astype(F32) + r_ref[0].astype(F32)) + r_ref[1].astype(F32)) + r_ref[2].astype(F32)

    in_specs = [pl.BlockSpec((1, tr, geo.hc), lambda i, place_ref: (place_ref[0], i, 0)),
                pl.BlockSpec((3, tr, geo.hc), lambda i, place_ref: (0, i, 0))]
    args = [place, pair, recv]
    if into is not None:
        in_specs.append(pl.BlockSpec(memory_space=pl.ANY))
        args.append(into)
    return pl.pallas_call(
        body, name=name, out_shape=jax.ShapeDtypeStruct((rows, geo.hc), F32),
        grid_spec=pltpu.PrefetchScalarGridSpec(
            num_scalar_prefetch=1, grid=(nb,), in_specs=in_specs,
            out_specs=pl.BlockSpec((tr, geo.hc), lambda i, place_ref: (row0 // tr + place_ref[1] * nb + i, 0))),
        input_output_aliases={3: 0} if into is not None else {},
        compiler_params=_params(1),
    )(*args)


PIECES = (4, 1, 16, 8) + (1,) * IN_CHUNKS
SCHEDULE = {
    "proj_fwd": (("gather_ici", W_OUT, 0, 1), ("gather_ici", W_UP, 0, 6)),
    "hgrn_fwd": (("gather_relay", W_OUT, 0, 1), ("gather_relay", W_UP, 0, 6), ("gather_ici", W_UP, 6, 12)),
    "attn_fwd_d1": (("gather_relay", W_UP, 6, 12),),
    "attn_fwd_d4": (("gather_ici", W_UP, 12, 16), ("gather_d2d", W_OUT)),
    "attn_fwd_d16": (("gather_relay", W_UP, 12, 16), ("gather_ici", W_DOWN, 0, 2)),
    "mix_fwd": (("gather_d2d", W_UP), ("gather_ici", W_DOWN, 2, 4)),
    "up_fwd": (("gather_ici", W_DOWN, 4, 8), ("gather_relay", W_DOWN, 0, 4)),
    "conv_gate_fwd_a": (("gather_relay", W_DOWN, 4, 8),),
    "conv_gate_fwd_b": (("gather_d2d", W_DOWN),),
    "down_bwd_x": (("swap", W_DOWN),),
    "conv_gate_bwd": (("scatter", W_DOWN, 0, 4),),
    "up_bwd_w": (("scatter", W_DOWN, 4, 8),),
    "up_bwd_x": (("swap", W_UP),),
    "norm2_bwd": (("scatter", W_UP, 0, 1),),
    "mix_bwd_w": (("scatter", W_UP, 1, 2),),
    "mix_bwd_x": (("swap", W_OUT), ("scatter", W_UP, 2, 3)),
    "hgrn_bwd": (("scatter", W_UP, 3, 9), ("join", W_DOWN)),
    "attn_bwd_d1": (("scatter", W_UP, 9, 12),),
    "attn_bwd_d4": (("scatter", W_OUT, 0, 1),),
    "attn_bwd_d16": (("scatter", W_UP, 12, 16),),
    "proj_bwd_w_0": (("join", W_UP), ("join", W_OUT)),
    "proj_bwd_w_1": (("swap", W_INQ),),
    "proj_bwd_w_2": (("swap", W_INQ + 1),),
    "proj_bwd_w_3": (("swap", W_INQ + 2), ("scatter", W_INQ, 0, 1)),
    "proj_bwd_x": (("swap", W_INQ + 3), ("scatter", W_INQ + 1, 0, 1), ("scatter", W_INQ + 2, 0, 1)),
    "gather_stats": (("scatter", W_INQ + 3, 0, 1), ("join", W_INQ), ("join", W_INQ + 1), ("join", W_INQ + 2)),
    "grad_in_join": (("join", W_INQ + 3),),
}


class _Net:
    def __init__(self, shards, place):
        self.place = place
        self.geos = [_Geo(k, s.shape) for k, s in zip(WEIGHT_KINDS, shards)]
        self.full = [_cast_into_full(s, g, place, f"cast_shard_{w}") for w, (s, g) in enumerate(zip(shards, self.geos))]
        self.full[W_IN] = _gather_weight(self.full[W_IN], self.geos[W_IN], PIECES[W_IN], "gather_w_in")
        rows, cols = shards[W_IN].shape
        self.geos += [_Geo("col", (rows // IN_CHUNKS, cols))] * IN_CHUNKS
        n = NW + IN_CHUNKS
        self.grad, self.pair, self.recv, self.shard = [None] * n, [None] * n, [None] * n, [None] * n
        self.in_rows, self.in_shard = rows, None
        self.when_joined, self.joined = {}, {}
        self.slots = []

    def _row0(self, w):
        return (w - W_INQ) * 2 * self.geos[w].hr

    def host(self, name):
        phase = _Phase()
        self.slots = []
        groups = {}
        for item in SCHEDULE.get(name, ()):
            kind, w = item[0], item[1]
            geo = self.geos[w]
            key = len(groups)
            if kind == "gather_ici":
                spec, key = _gather_ici(self.full[w], geo, item[2], item[3], PIECES[w]), ("full", w)
            elif kind == "gather_relay":
                spec, key = _gather_relay(self.full[w], geo, item[2], item[3], PIECES[w]), ("full", w)
            elif kind == "gather_d2d":
                spec, key = _gather_d2d(self.full[w], geo), ("full", w)
            elif kind == "swap":
                spec = _swap_d2d(self.grad[w], geo)
            elif kind == "scatter":
                spec, key = _scatter_ici(self.pair[w], self.recv[w], geo, item[2], item[3], PIECES[w]), ("recv", w)
            elif w >= W_INQ:
                spec, key = _join_d2d(self.in_shard, geo, self._row0(w)), "in_shard"
            else:
                spec = _join_d2d(self.shard[w], geo)
            groups.setdefault(key, []).append((item, spec))
        for group in groups.values():
            specs = [s for _, s in group]
            merged = dict(specs[0], n=sum(s["n"] for s in specs),
                          copies=lambda ins, outs, specs=specs: [t for s in specs for t in s["copies"](ins, outs)])
            self.slots.append(([item for item, _ in group], phase.add(**merged)))
        return phase

    def done(self, name, comm):
        for items, sl in sorted(self.slots, key=lambda s: s[0][0][0] == "scatter"):
            out = comm[sl][0]
            for item in items:
                kind, w = item[0], item[1]
                if kind in ("gather_ici", "gather_relay", "gather_d2d"):
                    self.full[w] = out
                elif kind == "swap":
                    self.pair[w] = _add_pair(self.grad[w], out, self.geos[w], self.place, f"grad_pair_sum_{w}")
                elif kind == "scatter":
                    self.recv[w] = out
                    if item[3] == PIECES[w] and w >= W_INQ:
                        self.in_shard = _add_four(self.pair[w], out, self.geos[w], self.place, f"grad_chip_sum_{w}",
                                                  rows=self.in_rows, row0=self._row0(w), into=self.in_shard)
                    elif item[3] == PIECES[w]:
                        self.shard[w] = _add_four(self.pair[w], out, self.geos[w], self.place, f"grad_chip_sum_{w}")
                elif w >= W_INQ:
                    self.in_shard = out
                else:
                    self.shard[w] = out
                    if w in self.when_joined:
                        self.joined[w] = self.when_joined[w](out)

    def alone(self, name):
        self.done(name, _comm_only(name, self.host(name)))


CONVW_SHARD = 3 * DFF // N_CHIPS
COND_PAD = 8 * 896
SMALL_SIZES = (("norm1_w", D), ("lb", HW), ("hg_norm_w", DH), ("q_norm_w", DH), ("k_norm_w", DH),
               ("norm2_w", D), ("conv_b", DFF), ("conv_w", 3 * DFF), ("loss", 128))
SMALL_TOTAL = sum(n for _, n in SMALL_SIZES)
STATS_PAD = 8 * 5120


def kernel(x, c, w_ada, b_ada, norm1_w, w_in, lb_logits, hg_norm_w, q_norm_w, k_norm_w, w_out, norm2_w, w_up, conv_w, conv_b, w_down, loss_target, m_w_ada, m_b_ada, m_norm1_w, m_w_in, m_lb_logits, m_hg_norm_w, m_q_norm_w, m_k_norm_w, m_w_out, m_norm2_w, m_w_up, m_conv_w, m_conv_b, m_w_down, v_w_ada, v_b_ada, v_norm1_w, v_w_in, v_lb_logits, v_hg_norm_w, v_q_norm_w, v_k_norm_w, v_w_out, v_norm2_w, v_w_up, v_conv_w, v_conv_b, v_w_down):
    chip = 2 * lax.axis_index("x") + lax.axis_index("y")
    dev = 2 * chip + lax.axis_index("c")

    cond = jnp.concatenate([c, conv_w[0].reshape(1, CONVW_SHARD), jnp.zeros((1, COND_PAD - D - CONVW_SHARD), F32)], axis=1)
    cond_all = _all_gather_rows(cond.reshape(8, COND_PAD // 8), "gather_cond").reshape(N_DEV, COND_PAD)
    c_all = cond_all[:, :D]
    conv_w_full = jnp.concatenate(
        [cond_all[2 * j, D:D + CONVW_SHARD].reshape(3, DFF // N_CHIPS) for j in range(N_CHIPS)], axis=1)

    b_shard = lax.dynamic_slice_in_dim(b_ada, chip * ADA_COLS, ADA_COLS, axis=1)
    mod_cols = _all_gather_rows(_ada_fwd(c_all, w_ada[0], b_shard), "gather_mod")
    mod_all = jnp.concatenate([mod_cols[16 * j:16 * j + 8] for j in range(N_CHIPS)], axis=1)
    mod = lax.dynamic_slice_in_dim(mod_all, dev, 1, axis=0)

    place = jnp.stack([chip, lax.axis_index("c")]).astype(jnp.int32)
    net = _Net([w_in[0], w_out[0], w_up[0], w_down[0]], place)
    net.when_joined = {
        W_OUT: lambda grad: _adamw_sc(w_out[0], grad, m_w_out[0], v_w_out[0], "adamw_sc_w_out"),
        W_DOWN: lambda grad: _adamw_sc(w_down[0], grad, m_w_down[0], v_w_down[0], "adamw_sc_w_down"),
        W_UP: lambda grad: _adamw_sc(w_up[0], grad, m_w_up[0], v_w_up[0], "adamw_sc_w_up"),
    }
    early = {W_OUT: "w_out", W_DOWN: "w_down", W_UP: "w_up"}
    loss_row, grad_x, dmod, small = _sequence_step(
        x[0], loss_target[0], mod, norm1_w, lb_logits, hg_norm_w, q_norm_w, k_norm_w, norm2_w, conv_w_full, conv_b, net)
    small["conv_w"] = small["conv_w"].reshape(1, 3 * DFF)
    small["loss"] = loss_row
    stats = jnp.concatenate([dmod] + [small[k] for k, _ in SMALL_SIZES]
                            + [jnp.zeros((1, STATS_PAD - 6 * D - SMALL_TOTAL), F32)], axis=1)
    stats_all, comm = _all_gather_rows(stats.reshape(8, STATS_PAD // 8), "gather_stats", net.host("gather_stats"))
    net.done("gather_stats", comm)
    stats_all = stats_all.reshape(N_DEV, STATS_PAD)
    net.alone("grad_in_join")
    g_out, g_up, g_down = net.shard[W_OUT], net.shard[W_UP], net.shard[W_DOWN]
    g_in = net.in_shard
    dmod_all = stats_all[:, :6 * D]
    sums = _sum_rows(stats_all[:, 6 * D:6 * D + SMALL_TOTAL], "small_grad_sum")
    g_small, off = {}, 0
    for k, n in SMALL_SIZES:
        g_small[k] = sums[:, off:off + n]
        off += n
    loss = g_small["loss"][0, 0]
    g_b_ada = _sum_rows(dmod_all, "b_ada_grad_sum")
    ada = _ada_bwd_adamw(c_all, lax.dynamic_slice_in_dim(dmod_all, chip * ADA_COLS, ADA_COLS, axis=1),
                         w_ada[0], m_w_ada[0], v_w_ada[0])
    g_w_ada = ada[0]
    g_lb = _lb_grad(lb_logits, g_small["lb"])
    g_conv_w = lax.dynamic_slice_in_dim(g_small["conv_w"].reshape(3, DFF), chip * (DFF // N_CHIPS), DFF // N_CHIPS, axis=1)

    names = ["w_ada", "b_ada", "norm1_w", "w_in", "lb_logits", "hg_norm_w", "q_norm_w", "k_norm_w", "w_out",
             "norm2_w", "w_up", "conv_w", "conv_b", "w_down"]
    lead = {"w_ada", "w_in", "w_out", "w_up", "conv_w", "w_down"}
    w = dict(w_ada=w_ada, b_ada=b_ada, norm1_w=norm1_w, w_in=w_in, lb_logits=lb_logits, hg_norm_w=hg_norm_w,
             q_norm_w=q_norm_w, k_norm_w=k_norm_w, w_out=w_out, norm2_w=norm2_w, w_up=w_up, conv_w=conv_w,
             conv_b=conv_b, w_down=w_down)
    m = dict(w_ada=m_w_ada, b_ada=m_b_ada, norm1_w=m_norm1_w, w_in=m_w_in, lb_logits=m_lb_logits,
             hg_norm_w=m_hg_norm_w, q_norm_w=m_q_norm_w, k_norm_w=m_k_norm_w, w_out=m_w_out, norm2_w=m_norm2_w,
             w_up=m_w_up, conv_w=m_conv_w, conv_b=m_conv_b, w_down=m_w_down)
    v = dict(w_ada=v_w_ada, b_ada=v_b_ada, norm1_w=v_norm1_w, w_in=v_w_in, lb_logits=v_lb_logits,
             hg_norm_w=v_hg_norm_w, q_norm_w=v_q_norm_w, k_norm_w=v_k_norm_w, w_out=v_w_out, norm2_w=v_norm2_w,
             w_up=v_w_up, conv_w=v_conv_w, conv_b=v_conv_b, w_down=v_w_down)
    g = dict(w_ada=g_w_ada, b_ada=g_b_ada, norm1_w=g_small["norm1_w"], w_in=g_in, lb_logits=g_lb,
             hg_norm_w=g_small["hg_norm_w"], q_norm_w=g_small["q_norm_w"], k_norm_w=g_small["k_norm_w"], w_out=g_out,
             norm2_w=g_small["norm2_w"], w_up=g_up, conv_w=g_conv_w, conv_b=g_small["conv_b"], w_down=g_down)

    updated = {early[i]: res for i, res in net.joined.items()}
    updated["w_ada"] = (ada[1], ada[2], ada[3], ada[0])
    grads, deltas, new_m, new_v = [], [], [], []
    for n in names:
        strip = (lambda t: t[0]) if n in lead else (lambda t: t)
        wrap = (lambda t: t[None]) if n in lead else (lambda t: t)
        g_n = g[n]
        if n in updated:
            d_n, m_n, v_n, g_n = updated[n]
        elif n == "w_in":
            d_n, m_n, v_n, g_n = _adamw(strip(w[n]), g_n, strip(m[n]), strip(v[n]), f"adamw_{n}", copy_grad=True)
        else:
            d_n, m_n, v_n = _adamw(strip(w[n]), g_n, strip(m[n]), strip(v[n]), f"adamw_{n}")
        grads.append(wrap(g_n))
        deltas.append(wrap(d_n))
        new_m.append(wrap(m_n))
        new_v.append(wrap(v_n))
    return (loss, grad_x[None], *grads, *deltas, *new_m, *new_v)
```

```python
import functools

import jax
import jax.numpy as jnp
from jax import lax
from jax.experimental import pallas as pl
from jax.experimental.pallas import tpu as pltpu
from jax.experimental.pallas import tpu_sc as plsc

F32 = jnp.float32
BF16 = jnp.bfloat16

S = 2048
D = 2048
H = 8
DH = 128
HW = H * DH
CH = 64
NCH = S // CH
DFF = 5632
INC = 7 * HW
BLK = 128
DILS = (1, 4, 16)
EPS = 1e-6
NEG = -1e30
N_CHIPS = 4
N_DEV = 8

ADAM_LR = 0.001
ADAM_B1 = 0.9
ADAM_B2 = 0.999
ADAM_EPS = 1e-08
ADAM_WD = 0.01
ADAM_STEP = 10
ADAM_BLOCK_BYTES = 1 << 21

V7X_VMEM_BYTES = 64 * 1024 * 1024
VMEM_LIMIT = (V7X_VMEM_BYTES * 3) // 4
MESH = pl.DeviceIdType.MESH
HBM_SPEC = pl.BlockSpec(memory_space=pltpu.HBM)
VMEM_SPEC = pl.BlockSpec(memory_space=pltpu.VMEM)

NN = (((1,), (0,)), ((), ()))
NT = (((1,), (1,)), ((), ()))
TN = (((0,), (0,)), ((), ()))


def _params(n_grid):
    return pltpu.CompilerParams(dimension_semantics=("arbitrary",) * n_grid, vmem_limit_bytes=VMEM_LIMIT)


def _dot(a, b, dims=NN):
    return lax.dot_general(a.astype(BF16), b.astype(BF16), dims, preferred_element_type=F32)


def _dot_f32(a, b):
    return lax.dot_general(a, b, NN, precision=lax.Precision.HIGHEST, preferred_element_type=F32)


def _sigmoid(x):
    return 1.0 / (1.0 + jnp.exp(-x))


def _pick(n, cands):
    for t in cands:
        if n % t == 0:
            return t
    raise ValueError(n)


def _matmul(a, b, mode, out_dtype, name, phase=None, m_blocks=None):
    if mode == "nn":
        (m, k), (_, n) = a.shape, b.shape
    elif mode == "nt":
        (m, k), (n, _) = a.shape, b.shape
    else:
        (k, m), (_, n) = a.shape, b.shape
    tm = _pick(m, (1024, 1408, 512))
    a_block = lambda i: i
    if m_blocks is not None:
        tm, blocks = m_blocks
        m = tm * len(blocks)
        step = blocks[1] - blocks[0] if len(blocks) > 1 else 0
        assert all(blk == blocks[0] + step * i for i, blk in enumerate(blocks))
        a_block = lambda i: blocks[0] + step * i
    tn = _pick(n, (1024, 1408, 512))
    tk = _pick(k, (2048, 2816, 1792, 1024, 512))
    nk = k // tk
    dims = {"nn": NN, "nt": NT, "tn": TN}[mode]

    def body(a_ref, b_ref, o_ref, *acc):
        part = lax.dot_general(a_ref[...], b_ref[...], dims, preferred_element_type=F32)
        if nk == 1:
            o_ref[...] = part.astype(o_ref.dtype)
            return
        acc_ref, kk = acc[0], pl.program_id(2)

        @pl.when(kk == 0)
        def _():
            acc_ref[...] = part

        @pl.when(jnp.logical_and(kk > 0, kk < nk - 1))
        def _():
            acc_ref[...] += part

        @pl.when(kk == nk - 1)
        def _():
            o_ref[...] = (acc_ref[...] + part).astype(o_ref.dtype)

    if mode == "tn":
        a_spec = pl.BlockSpec((tk, tm), lambda i, j, kk: (kk, a_block(i)))
    else:
        a_spec = pl.BlockSpec((tm, tk), lambda i, j, kk: (i, kk))
    if mode == "nt":
        b_spec = pl.BlockSpec((tn, tk), lambda i, j, kk: (j, kk))
    else:
        b_spec = pl.BlockSpec((tk, tn), lambda i, j, kk: (kk, j))
    return _pcall(
        body, [a, b], phase, name=name,
        out_shape=jax.ShapeDtypeStruct((m, n), out_dtype),
        grid=(m // tm, n // tn, nk),
        in_specs=[a_spec, b_spec],
        out_specs=pl.BlockSpec((tm, tn), lambda i, j, kk: (i, j)),
        scratch_shapes=[pltpu.VMEM((tm, tn), F32)] if nk > 1 else [],
        compiler_params=_params(3),
    )


TR = 256


def _row_spec(cols=D):
    return pl.BlockSpec((TR, cols), lambda i: (i, 0))


def _vec_spec(cols=D):
    return pl.BlockSpec((1, cols), lambda i: (0, 0))


def _norm_mod(x, nw, scale, shift, name):
    def body(x_ref, nw_ref, sc_ref, sh_ref, h_ref):
        xv = x_ref[...]
        r = lax.rsqrt(jnp.mean(xv * xv, axis=-1, keepdims=True) + EPS)
        h_ref[...] = ((xv * r) * nw_ref[...] * (1.0 + sc_ref[...]) + sh_ref[...]).astype(BF16)

    return pl.pallas_call(
        body, name=name, out_shape=jax.ShapeDtypeStruct((S, D), BF16), grid=(S // TR,),
        in_specs=[_row_spec(), _vec_spec(), _vec_spec(), _vec_spec()], out_specs=_row_spec(),
        compiler_params=_params(1),
    )(x, nw, scale, shift)


def _resid_norm_mod(x, mix, gate, nw, scale, shift, name):
    def body(x_ref, mix_ref, g_ref, nw_ref, sc_ref, sh_ref, x1_ref, h_ref):
        xv = x_ref[...] + g_ref[...] * mix_ref[...]
        x1_ref[...] = xv
        r = lax.rsqrt(jnp.mean(xv * xv, axis=-1, keepdims=True) + EPS)
        h_ref[...] = ((xv * r) * nw_ref[...] * (1.0 + sc_ref[...]) + sh_ref[...]).astype(BF16)

    return pl.pallas_call(
        body, name=name,
        out_shape=(jax.ShapeDtypeStruct((S, D), F32), jax.ShapeDtypeStruct((S, D), BF16)), grid=(S // TR,),
        in_specs=[_row_spec(), _row_spec(), _vec_spec(), _vec_spec(), _vec_spec(), _vec_spec()],
        out_specs=(_row_spec(), _row_spec()),
        compiler_params=_params(1),
    )(x, mix, gate, nw, scale, shift)


def _norm_mod_bwd(dh, xin, nw, scale, dres, name, mix=None, gate=None, phase=None):
    with_gate = mix is not None

    def body(*refs):
        if with_gate:
            dh_ref, x_ref, nw_ref, sc_ref, dres_ref, mix_ref, g_ref, dx_ref, dsh_ref, dsc_ref, dnw_ref, dg_ref, dmix_ref = refs
        else:
            dh_ref, x_ref, nw_ref, sc_ref, dres_ref, dx_ref, dsh_ref, dsc_ref, dnw_ref = refs
        i = pl.program_id(0)
        dhv = dh_ref[...]
        xv = x_ref[...]
        r = lax.rsqrt(jnp.mean(xv * xv, axis=-1, keepdims=True) + EPS)
        xn = xv * r
        nwv = nw_ref[...]
        one_sc = 1.0 + sc_ref[...]
        dxn = dhv * nwv * one_sc
        dx = dres_ref[...] + r * (dxn - xn * jnp.mean(dxn * xn, axis=-1, keepdims=True))
        dx_ref[...] = dx

        @pl.when(i == 0)
        def _():
            dsh_ref[...] = jnp.zeros_like(dsh_ref)
            dsc_ref[...] = jnp.zeros_like(dsc_ref)
            dnw_ref[...] = jnp.zeros_like(dnw_ref)
            if with_gate:
                dg_ref[...] = jnp.zeros_like(dg_ref)

        dsh_ref[...] += jnp.sum(dhv, axis=0, keepdims=True)
        dsc_ref[...] += jnp.sum(dhv * xn * nwv, axis=0, keepdims=True)
        dnw_ref[...] += jnp.sum(dhv * xn * one_sc, axis=0, keepdims=True)
        if with_gate:
            dg_ref[...] += jnp.sum(dx * mix_ref[...], axis=0, keepdims=True)
            dmix_ref[...] = (dx * g_ref[...]).astype(BF16)

    vec = jax.ShapeDtypeStruct((1, D), F32)
    ins = [dh, xin, nw, scale, dres]
    in_specs = [_row_spec(), _row_spec(), _vec_spec(), _vec_spec(), _row_spec()]
    outs = [jax.ShapeDtypeStruct((S, D), F32), vec, vec, vec]
    out_specs = [_row_spec(), _vec_spec(), _vec_spec(), _vec_spec()]
    if with_gate:
        ins += [mix, gate]
        in_specs += [_row_spec(), _vec_spec()]
        outs += [vec, jax.ShapeDtypeStruct((S, D), BF16)]
        out_specs += [_vec_spec(), _row_spec()]
    return _pcall(
        body, ins, phase, name=name, out_shape=tuple(outs), grid=(S // TR,), in_specs=in_specs,
        out_specs=tuple(out_specs), compiler_params=_params(1),
    )


def _tri(lower):
    row = lax.broadcasted_iota(jnp.int32, (CH, CH), 0)
    col = lax.broadcasted_iota(jnp.int32, (CH, CH), 1)
    return (row >= col) if lower else (col >= row)


def _hgrn_gates(hq, hf, lb):
    sig = _sigmoid(hf)
    f = lb + (1.0 - lb) * sig
    g = jnp.log(f)
    sq = _sigmoid(hq)
    return sig, f, g, 1.0 - f, hq * sq, sq


HG_HP = 2
HG_UNROLL = 8


def _proj_col_spec(group):
    return pl.BlockSpec((S, HG_HP * DH), lambda h: (0, group * (H // HG_HP) + h))


def _hgrn_fwd(proj, lb_logits, norm_w, phase=None):
    def body(hq_ref, hf_ref, hi_ref, hg_ref, lbl_ref, nw_ref, out_ref, oraw_ref, st_ref, s_ref):
        nw = nw_ref[...]
        lower = _tri(True)
        ltri = lower.astype(F32)
        s_ref[...] = jnp.zeros_like(s_ref)

        def chunk(n, carry):
            rows = pl.ds(pl.multiple_of(n * CH, CH), CH)
            for j in range(HG_HP):
                cols = slice(j * DH, (j + 1) * DH)
                lb = 1.0 / (1.0 + jnp.exp(lbl_ref[1:2, cols] - lbl_ref[0:1, cols]))
                hq, hf, v, hg = hq_ref[rows, cols], hf_ref[rows, cols], hi_ref[rows, cols], hg_ref[rows, cols]
                _, _, g, kk, q, _ = _hgrn_gates(hq, hf, lb)
                gc = _dot_f32(ltri, g)
                gl = jnp.sum(g, axis=0, keepdims=True)
                qe = q * jnp.exp(gc)
                ke = kk * jnp.exp(gl - gc)
                qh = q * jnp.exp(gc - 0.5 * gl)
                kh = kk * jnp.exp(0.5 * gl - gc)
                st = s_ref[j]
                st_ref[j, pl.ds(n, 1)] = st[None]
                a = jnp.where(lower, _dot(qh, kh, NT), 0.0)
                o = _dot(qe, st, NT) + _dot(a, v)
                s_ref[j] = st * jnp.exp(gl) + _dot(v, ke, TN)
                oraw_ref[rows, cols] = o
                rs = lax.rsqrt(jnp.mean(o * o, axis=-1, keepdims=True) + EPS)
                out_ref[rows, cols] = (o * rs * nw * (hg * _sigmoid(hg))).astype(BF16)
            return carry

        lax.fori_loop(0, NCH, chunk, 0, unroll=HG_UNROLL)

    head_spec = pl.BlockSpec((S, HG_HP * DH), lambda h: (0, h))
    return _pcall(
        body, [proj, proj, proj, proj, lb_logits, norm_w], phase, name="hgrn_fwd",
        out_shape=(jax.ShapeDtypeStruct((S, 2 * HW), BF16), jax.ShapeDtypeStruct((S, HW), F32),
                   jax.ShapeDtypeStruct((H, NCH, DH, DH), F32)),
        grid=(H // HG_HP,),
        in_specs=[_proj_col_spec(0), _proj_col_spec(1), _proj_col_spec(2), _proj_col_spec(3),
                  pl.BlockSpec((2, HG_HP * DH), lambda h: (0, h)), pl.BlockSpec((1, DH), lambda h: (0, 0))],
        out_specs=(head_spec, head_spec, pl.BlockSpec((HG_HP, NCH, DH, DH), lambda h: (h, 0, 0, 0))),
        scratch_shapes=[pltpu.VMEM((HG_HP, DH, DH), F32)],
        compiler_params=_params(1),
    )


def _hgrn_bwd(proj, lb_logits, norm_w, oraw, states, dout, phase=None):
    def body(hq_ref, hf_ref, hi_ref, hg_ref, lbl_ref, nw_ref, oraw_ref, st_ref, do_ref,
             dhq_ref, dhf_ref, dhi_ref, dhg_ref, dlb_ref, dnw_ref, ds_ref, acc_ref):
        h = pl.program_id(0)
        nw = nw_ref[...]
        lower = _tri(True)
        ltri = lower.astype(F32)
        utri = _tri(False).astype(F32)
        last = lax.broadcasted_iota(jnp.int32, (CH, DH), 0) == CH - 1
        ds_ref[...] = jnp.zeros_like(ds_ref)
        acc_ref[...] = jnp.zeros_like(acc_ref)

        @pl.when(h == 0)
        def _():
            dnw_ref[...] = jnp.zeros_like(dnw_ref)

        def chunk(i, carry):
            n = NCH - 1 - i
            rows = pl.ds(pl.multiple_of(n * CH, CH), CH)
            for j in range(HG_HP):
                cols = slice(j * DH, (j + 1) * DH)
                lb = 1.0 / (1.0 + jnp.exp(lbl_ref[1:2, cols] - lbl_ref[0:1, cols]))
                hq, hf, v, hg = hq_ref[rows, cols], hf_ref[rows, cols], hi_ref[rows, cols], hg_ref[rows, cols]
                sig, f, g, kk, q, sq = _hgrn_gates(hq, hf, lb)
                gc = _dot_f32(ltri, g)
                gl = jnp.sum(g, axis=0, keepdims=True)
                eg = jnp.exp(gc)
                ek = jnp.exp(gl - gc)
                gam = jnp.exp(gl)
                ph = jnp.exp(gc - 0.5 * gl)
                pk = jnp.exp(0.5 * gl - gc)
                qe, ke = q * eg, kk * ek
                qh, kh = q * ph, kk * pk
                st = st_ref[j, pl.ds(n, 1)][0]
                dst = ds_ref[j]
                a = jnp.where(lower, _dot(qh, kh, NT), 0.0)
                o = oraw_ref[rows, cols]
                rs = lax.rsqrt(jnp.mean(o * o, axis=-1, keepdims=True) + EPS)
                xh = o * rs
                sg = _sigmoid(hg)
                dgo = do_ref[rows, cols]
                d_on = dgo * (hg * sg)
                dhg_ref[rows, cols] = (dgo * xh * nw * (sg * (1.0 + hg * (1.0 - sg)))).astype(BF16)
                acc_ref[j, 1:2, :] += jnp.sum(d_on * xh, axis=0, keepdims=True)
                dy = d_on * nw
                do = rs * (dy - xh * jnp.mean(dy * xh, axis=-1, keepdims=True))
                dqe = _dot(do, st)
                da = jnp.where(lower, _dot(do, v, NT), 0.0)
                dv = _dot(a, do, TN) + _dot(ke, dst, NT)
                dke = _dot(v, dst)
                dgam = jnp.sum(dst * st, axis=0, keepdims=True)
                dqh = lax.dot_general(da, kh, NN, precision=lax.Precision.HIGH, preferred_element_type=F32)
                dkh = lax.dot_general(da, qh, TN, precision=lax.Precision.HIGH, preferred_element_type=F32)
                dq = dqh * ph + dqe * eg
                dk = dkh * pk + dke * ek
                dgl = jnp.sum(dke * ke, axis=0, keepdims=True) + dgam * gam
                dgc = dqh * qh - dkh * kh + dqe * qe - dke * ke + jnp.where(last, dgl, 0.0)
                dg = _dot_f32(utri, dgc)
                df = dg / f - dk
                dhf_ref[rows, cols] = (df * (1.0 - lb) * sig * (1.0 - sig)).astype(BF16)
                acc_ref[j, 0:1, :] += jnp.sum(df * (1.0 - sig), axis=0, keepdims=True)
                dhq_ref[rows, cols] = (dq * (sq * (1.0 + hq * (1.0 - sq)))).astype(BF16)
                dhi_ref[rows, cols] = dv.astype(BF16)
                ds_ref[j] = dst * gam + _dot(do, qe, TN)
            return carry

        lax.fori_loop(0, NCH, chunk, 0, unroll=HG_UNROLL)
        for j in range(HG_HP):
            dlb_ref[:, j * DH:(j + 1) * DH] = acc_ref[j, 0:1, :]
            dnw_ref[...] += acc_ref[j, 1:2, :]

    head_spec = pl.BlockSpec((S, HG_HP * DH), lambda h: (0, h))
    head_out = jax.ShapeDtypeStruct((S, HW), BF16)
    return _pcall(
        body, [proj, proj, proj, proj, lb_logits, norm_w, oraw, states, dout], phase, name="hgrn_bwd",
        out_shape=(head_out, head_out, head_out, head_out,
                   jax.ShapeDtypeStruct((1, HW), F32), jax.ShapeDtypeStruct((1, DH), F32)),
        grid=(H // HG_HP,),
        in_specs=[_proj_col_spec(0), _proj_col_spec(1), _proj_col_spec(2), _proj_col_spec(3),
                  pl.BlockSpec((2, HG_HP * DH), lambda h: (0, h)), pl.BlockSpec((1, DH), lambda h: (0, 0)),
                  head_spec, pl.BlockSpec((HG_HP, NCH, DH, DH), lambda h: (h, 0, 0, 0)), head_spec],
        out_specs=(head_spec, head_spec, head_spec, head_spec,
                   pl.BlockSpec((1, HG_HP * DH), lambda h: (0, h)), pl.BlockSpec((1, DH), lambda h: (0, 0))),
        scratch_shapes=[pltpu.VMEM((HG_HP, DH, DH), F32), pltpu.VMEM((HG_HP, 8, DH), F32)],
        compiler_params=_params(1),
    )


ATT_SCALE = DH ** -0.5
HEADS_PER_STEP = {1: 8, 4: 1, 16: 1}


def _sub_rows(ref, r, dil, cols):
    if dil == 1:
        return ref[:, cols]
    return ref[pl.ds(r, BLK, stride=dil), cols]


def _set_sub_rows(ref, r, dil, cols, val):
    if dil == 1:
        ref[:, cols] = val
    else:
        ref[pl.ds(r, BLK, stride=dil), cols] = val


def _slopes(dil):
    hp = HEADS_PER_STEP[dil]
    s = jnp.asarray([dil * 2.0 ** (-(h + 1)) for h in range(H)], F32)
    return jnp.broadcast_to(s[:, None], (H, DH)).reshape(H // hp, hp, DH)


def _rms_rows(x):
    rs = lax.rsqrt(jnp.mean(x * x, axis=-1, keepdims=True) + EPS)
    return x * rs, rs


def _att_masks():
    qi = lax.broadcasted_iota(jnp.int32, (BLK, BLK), 0)
    kj = lax.broadcasted_iota(jnp.int32, (BLK, BLK), 1)
    steps_c = qi - kj
    steps_p = qi - kj + BLK
    return steps_c, steps_p, steps_c >= 0, steps_p <= BLK


def _qk_prep(proj, q_w, k_w):
    def body(q_ref, k_ref, qw_ref, kw_ref, qn_ref, kn_ref):
        for h in range(H):
            cols = slice(h * DH, (h + 1) * DH)
            qn_ref[:, cols] = _rms_rows(q_ref[:, cols])[0] * qw_ref[...]
            kn_ref[:, cols] = _rms_rows(k_ref[:, cols])[0] * kw_ref[...]

    out = jax.ShapeDtypeStruct((S, HW), F32)
    wspec = pl.BlockSpec((1, DH), lambda i: (0, 0))
    return pl.pallas_call(
        body, name="qk_prep", out_shape=(out, out), grid=(S // TR,),
        in_specs=[pl.BlockSpec((TR, HW), lambda i: (i, 4)), pl.BlockSpec((TR, HW), lambda i: (i, 5)), wspec, wspec],
        out_specs=(_row_spec(HW), _row_spec(HW)),
        compiler_params=_params(1),
    )(proj, proj, q_w, k_w)


def _qk_norm_bwd(proj, dqn, dkn, dv, q_w, k_w):
    def body(q_ref, k_ref, dqn_ref, dkn_ref, dv_ref, qw_ref, kw_ref, dq_ref, dk_ref, dvo_ref, dqw_ref, dkw_ref):
        i = pl.program_id(0)

        @pl.when(i == 0)
        def _():
            dqw_ref[...] = jnp.zeros_like(dqw_ref)
            dkw_ref[...] = jnp.zeros_like(dkw_ref)

        dvo_ref[...] = dv_ref[...].astype(BF16)
        for x_ref, d_ref, w_ref, o_ref, dw_ref in ((q_ref, dqn_ref, qw_ref, dq_ref, dqw_ref),
                                                   (k_ref, dkn_ref, kw_ref, dk_ref, dkw_ref)):
            for h in range(H):
                cols = slice(h * DH, (h + 1) * DH)
                xh, rs = _rms_rows(x_ref[:, cols])
                d = d_ref[:, cols]
                dw_ref[...] += jnp.sum(d * xh, axis=0, keepdims=True)
                dy = d * w_ref[...]
                o_ref[:, cols] = (rs * (dy - xh * jnp.mean(dy * xh, axis=-1, keepdims=True))).astype(BF16)

    big = jax.ShapeDtypeStruct((S, HW), BF16)
    small = jax.ShapeDtypeStruct((1, DH), F32)
    wspec = pl.BlockSpec((1, DH), lambda i: (0, 0))
    return pl.pallas_call(
        body, name="qk_norm_bwd", out_shape=(big, big, big, small, small), grid=(S // TR,),
        in_specs=[pl.BlockSpec((TR, HW), lambda i: (i, 4)), pl.BlockSpec((TR, HW), lambda i: (i, 5)),
                  _row_spec(HW), _row_spec(HW), _row_spec(HW), wspec, wspec],
        out_specs=(_row_spec(HW), _row_spec(HW), _row_spec(HW), wspec, wspec),
        compiler_params=_params(1),
    )(proj, proj, dqn, dkn, dv, q_w, k_w)


def _attn_delta(do, o):
    def body(do_ref, o_ref, d_ref):
        for h in range(H):
            cols = slice(h * DH, (h + 1) * DH)
            d_ref[:, cols] = jnp.broadcast_to(jnp.sum(do_ref[:, cols] * o_ref[:, cols], axis=-1, keepdims=True), (TR, DH))

    return pl.pallas_call(
        body, name="attn_delta", out_shape=jax.ShapeDtypeStruct((S, HW), F32), grid=(S // TR,),
        in_specs=[pl.BlockSpec((TR, HW), lambda i: (i, 1)), _row_spec(HW)], out_specs=_row_spec(HW),
        compiler_params=_params(1),
    )(do, o)


def _attn_fwd(proj, qn, kn, dil, phase=None):
    hp = HEADS_PER_STEP[dil]
    rows, ng, nb = BLK * dil, H // hp, S // (BLK * dil)

    def body(q_ref, kc_ref, kp_ref, vc_ref, vp_ref, sl_ref, o_ref, lse_ref):
        n = pl.program_id(0)
        steps_c, steps_p, ok_c, ok_p = _att_masks()
        ok_p = jnp.logical_and(ok_p, n > 0)
        fc, fp = steps_c.astype(F32), steps_p.astype(F32)
        for hh in range(hp):
            cols = slice(hh * DH, (hh + 1) * DH)
            sl = sl_ref[0, hh:hh + 1, :]
            for r in range(dil):
                sub = lambda ref: _sub_rows(ref, r, dil, cols)
                qv = sub(q_ref)
                sc = jnp.where(ok_c, _dot(qv, sub(kc_ref), NT) * ATT_SCALE - sl * fc, NEG)
                sp = jnp.where(ok_p, _dot(qv, sub(kp_ref), NT) * ATT_SCALE - sl * fp, NEG)
                m = jnp.maximum(jnp.max(sc, axis=-1, keepdims=True), jnp.max(sp, axis=-1, keepdims=True))
                pc, pp = jnp.exp(sc - m), jnp.exp(sp - m)
                den = jnp.sum(pc, axis=-1, keepdims=True) + jnp.sum(pp, axis=-1, keepdims=True)
                o = (_dot(pc, sub(vc_ref)) + _dot(pp, sub(vp_ref))) / den
                _set_sub_rows(o_ref, r, dil, cols, o)
                _set_sub_rows(lse_ref, r, dil, cols, jnp.broadcast_to(m + jnp.log(den), (BLK, DH)))

    cur = pl.BlockSpec((rows, hp * DH), lambda n, g: (n, g))
    prev = pl.BlockSpec((rows, hp * DH), lambda n, g: (jnp.maximum(n - 1, 0), g))
    vcur = pl.BlockSpec((rows, hp * DH), lambda n, g: (n, 6 * ng + g))
    vprev = pl.BlockSpec((rows, hp * DH), lambda n, g: (jnp.maximum(n - 1, 0), 6 * ng + g))
    out = jax.ShapeDtypeStruct((S, HW), F32)
    return _pcall(
        body, [qn, kn, kn, proj, proj, _slopes(dil)], phase,
        name=f"attn_fwd_d{dil}", out_shape=(out, out), grid=(nb, ng),
        in_specs=[cur, cur, prev, vcur, vprev, pl.BlockSpec((1, hp, DH), lambda n, g: (g, 0, 0))],
        out_specs=(cur, cur),
        compiler_params=_params(2),
    )


def _attn_merge(outs, lses, cat):
    def body(o1, o2, o3, l1, l2, l3, cat_ref, ob_ref, of_ref, lse_ref):
        a, b, c = l1[...], l2[...], l3[...]
        m = jnp.maximum(jnp.maximum(a, b), c)
        ea, eb, ec = jnp.exp(a - m), jnp.exp(b - m), jnp.exp(c - m)
        den = ea + eb + ec
        o = (ea * o1[...] + eb * o2[...] + ec * o3[...]) / den
        ob_ref[...] = o.astype(BF16)
        of_ref[...] = o
        lse_ref[...] = m + jnp.log(den)

    f = jax.ShapeDtypeStruct((S, HW), F32)
    return pl.pallas_call(
        body, name="attn_merge", out_shape=(jax.ShapeDtypeStruct((S, 2 * HW), BF16), f, f), grid=(S // TR,),
        in_specs=[_row_spec(HW)] * 6 + [pl.BlockSpec(memory_space=pl.ANY)],
        out_specs=(pl.BlockSpec((TR, HW), lambda i: (i, 1)), _row_spec(HW), _row_spec(HW)),
        input_output_aliases={6: 0},
        compiler_params=_params(1),
    )(*outs, *lses, cat)


def _attn_bwd(proj, qn, kn, lse, delta, do, dil, acc, phase=None):
    hp = HEADS_PER_STEP[dil]
    rows, ng, nb = BLK * dil, H // hp, S // (BLK * dil)
    has_acc = acc is not None

    def body(*refs):
        q_ref, qn_ref, kp_ref, kc_ref, vp_ref, vc_ref, l_ref, ln_ref, d_ref, dn_ref, do_ref, don_ref, sl_ref = refs[:13]
        refs = refs[13:]
        if has_acc:
            aq_ref, ak_ref, av_ref = refs[:3]
            refs = refs[3:]
        dq_ref, dk_ref, dv_ref = refs
        m = pl.program_id(0)
        steps_c, steps_p, ok_c, ok_p = _att_masks()
        ok_prev = jnp.logical_and(ok_p, m > 0)
        ok_next = jnp.logical_and(ok_p, m < nb - 1)
        fc, fp = steps_c.astype(F32), steps_p.astype(F32)
        for hh in range(hp):
            cols = slice(hh * DH, (hh + 1) * DH)
            sl = sl_ref[0, hh:hh + 1, :]
            for r in range(dil):
                sub = lambda ref: _sub_rows(ref, r, dil, cols)
                qv, qnv, kcv, kpv = sub(q_ref), sub(qn_ref), sub(kc_ref), sub(kp_ref)
                vc, vp = sub(vc_ref), sub(vp_ref)
                dov, donv = sub(do_ref), sub(don_ref)
                lse_m, lse_n = sub(l_ref)[:, 0:1], sub(ln_ref)[:, 0:1]
                delta_m, delta_n = sub(d_ref)[:, 0:1], sub(dn_ref)[:, 0:1]
                p_c = jnp.where(ok_c, jnp.exp(_dot(qv, kcv, NT) * ATT_SCALE - sl * fc - lse_m), 0.0)
                p_p = jnp.where(ok_prev, jnp.exp(_dot(qv, kpv, NT) * ATT_SCALE - sl * fp - lse_m), 0.0)
                p_n = jnp.where(ok_next, jnp.exp(_dot(qnv, kcv, NT) * ATT_SCALE - sl * fp - lse_n), 0.0)
                ds_c = p_c * (_dot(dov, vc, NT) - delta_m)
                ds_p = p_p * (_dot(dov, vp, NT) - delta_m)
                ds_n = p_n * (_dot(donv, vc, NT) - delta_n)
                dqn = (_dot(ds_c, kcv) + _dot(ds_p, kpv)) * ATT_SCALE
                dkn = (_dot(ds_c, qv, TN) + _dot(ds_n, qnv, TN)) * ATT_SCALE
                dv = _dot(p_c, dov, TN) + _dot(p_n, donv, TN)
                if has_acc:
                    dqn, dkn, dv = dqn + sub(aq_ref), dkn + sub(ak_ref), dv + sub(av_ref)
                _set_sub_rows(dq_ref, r, dil, cols, dqn)
                _set_sub_rows(dk_ref, r, dil, cols, dkn)
                _set_sub_rows(dv_ref, r, dil, cols, dv)

    def shifted(shift, m):
        if shift < 0:
            return jnp.maximum(m - 1, 0)
        if shift > 0:
            return jnp.minimum(m + 1, nb - 1)
        return m

    def spec(shift, group=0):
        return pl.BlockSpec((rows, hp * DH), lambda m, g: (shifted(shift, m), group * ng + g))

    ins = [qn, qn, kn, kn, proj, proj, lse, lse, delta, delta, do, do, _slopes(dil)]
    in_specs = [spec(0), spec(1), spec(-1), spec(0), spec(-1, 6), spec(0, 6), spec(0), spec(1), spec(0), spec(1),
                spec(0, 1), spec(1, 1), pl.BlockSpec((1, hp, DH), lambda m, g: (g, 0, 0))]
    if has_acc:
        ins += list(acc)
        in_specs += [spec(0)] * 3
    big = jax.ShapeDtypeStruct((S, HW), F32)
    return _pcall(
        body, ins, phase, name=f"attn_bwd_d{dil}", out_shape=(big, big, big), grid=(nb, ng),
        in_specs=in_specs, out_specs=(spec(0),) * 3,
        compiler_params=_params(2),
    )


TC = 512
NCT = DFF // TC


def _shift_rows(a, k):
    rows = lax.broadcasted_iota(jnp.int32, a.shape, 0)
    rolled = pltpu.roll(a, k % S, 0)
    if k > 0:
        return jnp.where(rows >= k, rolled, 0.0)
    return jnp.where(rows < S + k, rolled, 0.0)


def _conv_pre(a, w_ref, b_ref):
    return b_ref[...] + w_ref[0:1, :] * _shift_rows(a, 2) + w_ref[1:2, :] * _shift_rows(a, 1) + w_ref[2:3, :] * a


def _conv_gate_fwd(u, conv_w, conv_b, name, tiles=(0, NCT), into=None, phase=None):
    t0, t1 = tiles

    def body(a_ref, g_ref, w_ref, b_ref, *rest):
        pre = _conv_pre(a_ref[...].astype(F32), w_ref, b_ref)
        rest[-1][...] = (pre * _sigmoid(pre) * g_ref[...].astype(F32)).astype(BF16)

    args = [u, u, conv_w, conv_b]
    in_specs = [pl.BlockSpec((S, TC), lambda i: (0, t0 + i)), pl.BlockSpec((S, TC), lambda i: (0, NCT + t0 + i)),
                pl.BlockSpec((3, TC), lambda i: (0, t0 + i)), pl.BlockSpec((1, TC), lambda i: (0, t0 + i))]
    kw = {}
    if into is not None:
        args.append(into)
        in_specs.append(pl.BlockSpec(memory_space=pl.ANY))
        kw["input_output_aliases"] = {4: 0}
    return _pcall(
        body, args, phase, name=name, out_shape=jax.ShapeDtypeStruct((S, DFF), BF16), grid=(t1 - t0,),
        in_specs=in_specs, out_specs=pl.BlockSpec((S, TC), lambda i: (0, t0 + i)),
        compiler_params=_params(1), **kw,
    )


def _conv_gate_bwd(dy, u, conv_w, conv_b, phase=None):
    def body(dy_ref, a_ref, g_ref, w_ref, b_ref, du_ref, db_ref, dw_ref, da_buf, dg_buf, sems):
        i = pl.program_id(0)
        slot = i % 2

        def writes(step, s):
            col = pl.multiple_of(step * TC, TC)
            return (pltpu.make_async_copy(da_buf.at[s], du_ref.at[:, pl.ds(col, TC)], sems.at[0, s]),
                    pltpu.make_async_copy(dg_buf.at[s], du_ref.at[:, pl.ds(DFF + col, TC)], sems.at[1, s]))

        @pl.when(i >= 2)
        def _():
            for cp in writes(i - 2, slot):
                cp.wait()

        a = a_ref[...].astype(F32)
        dyv = dy_ref[...].astype(F32)
        pre = _conv_pre(a, w_ref, b_ref)
        sg = _sigmoid(pre)
        dg_buf[slot] = (dyv * pre * sg).astype(BF16)
        dpre = dyv * g_ref[...].astype(F32) * (sg * (1.0 + pre * (1.0 - sg)))
        da = w_ref[2:3, :] * dpre + w_ref[1:2, :] * _shift_rows(dpre, -1) + w_ref[0:1, :] * _shift_rows(dpre, -2)
        da_buf[slot] = da.astype(BF16)
        for cp in writes(i, slot):
            cp.start()
        db_ref[...] = jnp.sum(dpre, axis=0, keepdims=True)
        dw_ref[0:1, :] = jnp.sum(dpre * _shift_rows(a, 2), axis=0, keepdims=True)
        dw_ref[1:2, :] = jnp.sum(dpre * _shift_rows(a, 1), axis=0, keepdims=True)
        dw_ref[2:3, :] = jnp.sum(dpre * a, axis=0, keepdims=True)

        @pl.when(i == NCT - 1)
        def _():
            for cp in writes(i - 1, 1 - slot) + writes(i, slot):
                cp.wait()

    col = pl.BlockSpec((S, TC), lambda i: (0, i))
    return _pcall(
        body, [dy, u, u, conv_w, conv_b], phase, name="conv_gate_bwd",
        out_shape=(jax.ShapeDtypeStruct((S, 2 * DFF), BF16), jax.ShapeDtypeStruct((1, DFF), F32),
                   jax.ShapeDtypeStruct((3, DFF), F32)),
        grid=(NCT,),
        in_specs=[col, col, pl.BlockSpec((S, TC), lambda i: (0, NCT + i)),
                  pl.BlockSpec((3, TC), lambda i: (0, i)), pl.BlockSpec((1, TC), lambda i: (0, i))],
        out_specs=(pl.BlockSpec(memory_space=pl.ANY), pl.BlockSpec((1, TC), lambda i: (0, i)),
                   pl.BlockSpec((3, TC), lambda i: (0, i))),
        scratch_shapes=[pltpu.VMEM((2, S, TC), BF16), pltpu.VMEM((2, S, TC), BF16), pltpu.SemaphoreType.DMA((2, 2))],
        compiler_params=_params(1),
    )


def _out_loss(z, x1, target, gate):
    def body(z_ref, x_ref, t_ref, g_ref, do_ref, dz_ref, dg_ref, loss_ref):
        i = pl.program_id(0)
        zv = z_ref[...]
        gv = g_ref[...]
        err = x_ref[...] + gv * zv - t_ref[...]
        dout = err * (1.0 / D)
        do_ref[...] = dout
        dz_ref[...] = (dout * gv).astype(BF16)

        @pl.when(i == 0)
        def _():
            dg_ref[...] = jnp.zeros_like(dg_ref)
            loss_ref[...] = jnp.zeros_like(loss_ref)

        dg_ref[...] += jnp.sum(dout * zv, axis=0, keepdims=True)
        part = jnp.sum(jnp.sum(err * err, axis=0, keepdims=True), axis=-1, keepdims=True) * (0.5 / D)
        lane = lax.broadcasted_iota(jnp.int32, (1, 128), 1)
        loss_ref[...] += jnp.where(lane == 0, part, 0.0)

    return pl.pallas_call(
        body, name="out_loss",
        out_shape=(jax.ShapeDtypeStruct((S, D), F32), jax.ShapeDtypeStruct((S, D), BF16),
                   jax.ShapeDtypeStruct((1, D), F32), jax.ShapeDtypeStruct((1, 128), F32)),
        grid=(S // TR,),
        in_specs=[_row_spec(), _row_spec(), _row_spec(), _vec_spec()],
        out_specs=(_row_spec(), _row_spec(), _vec_spec(), _vec_spec(128)),
        compiler_params=_params(1),
    )(z, x1, target, gate)


ADA_COLS = 6 * D // N_CHIPS
TA = 512


def _ada_fwd(c_all, w_shard, b_shard):
    def body(c_ref, w_ref, b_ref, o_ref):
        cv = c_ref[...]
        o_ref[...] = _dot_f32(cv * _sigmoid(cv), w_ref[...]) + b_ref[...]

    return pl.pallas_call(
        body, name="ada_fwd", out_shape=jax.ShapeDtypeStruct((N_DEV, ADA_COLS), F32), grid=(ADA_COLS // TA,),
        in_specs=[pl.BlockSpec((N_DEV, D), lambda j: (0, 0)), pl.BlockSpec((D, TA), lambda j: (0, j)),
                  pl.BlockSpec((1, TA), lambda j: (0, j))],
        out_specs=pl.BlockSpec((N_DEV, TA), lambda j: (0, j)),
        compiler_params=_params(1),
    )(c_all, w_shard, b_shard)


ADA_ROWS = 128


def _ada_bwd_adamw(c_all, dmod_shard, w, m, v):
    def body(c_ref, d_ref, w_ref, m_ref, v_ref, g_ref, dl_ref, mo_ref, vo_ref):
        cv = c_ref[...]
        gv = lax.dot_general(cv * _sigmoid(cv), d_ref[...], TN, precision=lax.Precision.HIGHEST,
                             preferred_element_type=F32)
        g_ref[...] = gv
        m2 = ADAM_B1 * m_ref[...] + (1.0 - ADAM_B1) * gv
        v2 = ADAM_B2 * v_ref[...] + (1.0 - ADAM_B2) * (gv * gv)
        m_hat = m2 / (1.0 - ADAM_B1 ** ADAM_STEP)
        v_hat = v2 / (1.0 - ADAM_B2 ** ADAM_STEP)
        dl_ref[...] = -ADAM_LR * (m_hat / (jnp.sqrt(v_hat) + ADAM_EPS) + ADAM_WD * w_ref[...])
        mo_ref[...] = m2
        vo_ref[...] = v2

    spec = pl.BlockSpec((ADA_ROWS, ADA_COLS), lambda i: (i, 0))
    out = jax.ShapeDtypeStruct((D, ADA_COLS), F32)
    return pl.pallas_call(
        body, name="ada_bwd_adamw", out_shape=(out,) * 4, grid=(D // ADA_ROWS,),
        in_specs=[pl.BlockSpec((N_DEV, ADA_ROWS), lambda i: (0, i)), pl.BlockSpec((N_DEV, ADA_COLS), lambda i: (0, 0)),
                  spec, spec, spec],
        out_specs=(spec,) * 4,
        compiler_params=_params(1),
    )(c_all, dmod_shard, w, m, v)


def _sum_rows(g, name):
    rows, n = g.shape

    def body(g_ref, o_ref):
        acc = g_ref[0:1, :]
        for i in range(1, rows):
            acc = acc + g_ref[i:i + 1, :]
        o_ref[...] = acc

    return pl.pallas_call(
        body, name=name, out_shape=jax.ShapeDtypeStruct((1, n), F32),
        in_specs=[VMEM_SPEC], out_specs=VMEM_SPEC,
    )(g)


def _lb_grad(lb_logits, dlb):
    def body(l_ref, d_ref, o_ref):
        p = 1.0 / (1.0 + jnp.exp(l_ref[1:2, :] - l_ref[0:1, :]))
        t = d_ref[...] * p * (1.0 - p)
        o_ref[0:1, :] = t
        o_ref[1:2, :] = -t

    return pl.pallas_call(
        body, name="lb_grad", out_shape=jax.ShapeDtypeStruct((2, HW), F32),
        in_specs=[VMEM_SPEC, VMEM_SPEC], out_specs=VMEM_SPEC,
    )(lb_logits, dlb)


def _adamw(w, g, m, v, name, copy_grad=False):
    rows, cols = w.shape
    tr = rows
    if rows * cols * 4 > ADAM_BLOCK_BYTES:
        tr = _pick(rows, [t for t in (256, 128, 64, 32, 16, 8) if t * cols * 4 <= ADAM_BLOCK_BYTES])

    def body(w_ref, g_ref, m_ref, v_ref, d_ref, mo_ref, vo_ref, *go_ref):
        gv = g_ref[...]
        if copy_grad:
            go_ref[0][...] = gv
        m2 = ADAM_B1 * m_ref[...] + (1.0 - ADAM_B1) * gv
        v2 = ADAM_B2 * v_ref[...] + (1.0 - ADAM_B2) * (gv * gv)
        m_hat = m2 / (1.0 - ADAM_B1 ** ADAM_STEP)
        v_hat = v2 / (1.0 - ADAM_B2 ** ADAM_STEP)
        d_ref[...] = -ADAM_LR * (m_hat / (jnp.sqrt(v_hat) + ADAM_EPS) + ADAM_WD * w_ref[...])
        mo_ref[...] = m2
        vo_ref[...] = v2

    spec = pl.BlockSpec((tr, cols), lambda i: (i, 0))
    out = jax.ShapeDtypeStruct((rows, cols), F32)
    n_out = 4 if copy_grad else 3
    return pl.pallas_call(
        body, name=name, out_shape=(out,) * n_out, grid=(rows // tr,),
        in_specs=[spec] * 4, out_specs=(spec,) * n_out,
        compiler_params=_params(1),
    )(w, g, m, v)


SC_TILES = 32
SC_LANES = 16
SC_COL_PARTS = 2
SC_CHUNK_ROWS = 8


def _adamw_sc(w, g, m, v, name):
    rows, cols = w.shape
    band, part = rows // (SC_TILES // SC_COL_PARTS), cols // SC_COL_PARTS
    assert band % SC_CHUNK_ROWS == 0 and part % SC_LANES == 0, (rows, cols)

    def body(w_hbm, g_hbm, m_hbm, v_hbm, d_hbm, mo_hbm, vo_hbm, go_hbm, wb, gb, mb, vb, db):
        tile = lax.axis_index("sc_subcore") * 2 + lax.axis_index("sc_core")
        row0 = (tile // SC_COL_PARTS) * band
        col0 = (tile % SC_COL_PARTS) * part

        @pl.loop(0, band, step=SC_CHUNK_ROWS)
        def _(r):
            at = lambda ref: ref.at[pl.ds(row0 + r, SC_CHUNK_ROWS), pl.ds(col0, part)]
            pltpu.sync_copy(at(w_hbm), wb)
            pltpu.sync_copy(at(g_hbm), gb)
            pltpu.sync_copy(gb, at(go_hbm))
            pltpu.sync_copy(at(m_hbm), mb)
            pltpu.sync_copy(at(v_hbm), vb)

            @pl.loop(0, SC_CHUNK_ROWS)
            def _(i):
                @pl.loop(0, part, step=SC_LANES)
                def _(j):
                    sl = (i, pl.ds(j, SC_LANES))
                    gv = gb[sl]
                    m2 = ADAM_B1 * mb[sl] + (1.0 - ADAM_B1) * gv
                    v2 = ADAM_B2 * vb[sl] + (1.0 - ADAM_B2) * (gv * gv)
                    m_hat = m2 / (1.0 - ADAM_B1 ** ADAM_STEP)
                    v_hat = v2 / (1.0 - ADAM_B2 ** ADAM_STEP)
                    db[sl] = -ADAM_LR * (m_hat / (jnp.sqrt(v_hat) + ADAM_EPS) + ADAM_WD * wb[sl])
                    mb[sl] = m2
                    vb[sl] = v2

            pltpu.sync_copy(db, at(d_hbm))
            pltpu.sync_copy(mb, at(mo_hbm))
            pltpu.sync_copy(vb, at(vo_hbm))

    out = jax.ShapeDtypeStruct((rows, cols), F32)
    buf = pltpu.VMEM((SC_CHUNK_ROWS, part), F32)
    return pl.kernel(
        body, name=name, out_type=(out, out, out, out),
        mesh=plsc.VectorSubcoreMesh(core_axis_name="sc_core", subcore_axis_name="sc_subcore"),
        scratch_types=[buf] * 5,
    )(w, g, m, v)


W_IN, W_OUT, W_UP, W_DOWN = range(4)
IN_CHUNKS = 4
W_INQ = 4


def _sequence_step(x, target, mod, norm1_w, lb_logits, hg_norm_w, q_norm_w, k_norm_w, norm2_w, conv_w, conv_b, net):
    shift1, scale1, gate1, shift2, scale2, gate2 = [mod[:, i * D:(i + 1) * D] for i in range(6)]

    def run(name, fn, *args, **kw):
        phase = net.host(name)
        if phase is None:
            return fn(*args, **kw)
        res, comm = fn(*args, phase=phase, **kw)
        net.done(name, comm)
        return res

    h = _norm_mod(x, norm1_w, scale1, shift1, "norm1_fwd")
    proj = run("proj_fwd", _matmul, h, net.full[W_IN], "nn", F32, "proj_fwd")
    cat, o_raw, states = run("hgrn_fwd", _hgrn_fwd, proj, lb_logits, hg_norm_w)
    qn, kn = _qk_prep(proj, q_norm_w, k_norm_w)
    att = [run(f"attn_fwd_d{dil}", _attn_fwd, proj, qn, kn, dil) for dil in DILS]
    cat, b_out_f32, lse = _attn_merge([t[0] for t in att], [t[1] for t in att], cat)
    mix = run("mix_fwd", _matmul, cat, net.full[W_OUT], "nn", F32, "mix_fwd")
    x1, h2 = _resid_norm_mod(x, mix, gate1, norm2_w, scale2, shift2, "norm2_fwd")
    u = run("up_fwd", _matmul, h2, net.full[W_UP], "nn", BF16, "up_fwd")
    yact = run("conv_gate_fwd_a", _conv_gate_fwd, u, conv_w, conv_b, "conv_gate_fwd_a", tiles=(0, NCT // 2 + 1))
    yact = run("conv_gate_fwd_b", _conv_gate_fwd, u, conv_w, conv_b, "conv_gate_fwd_b", tiles=(NCT // 2 + 1, NCT),
               into=yact)
    z =_matmul(yact, net.full[W_DOWN], "nn", F32, "down_fwd")
    dout, dz, dgate2, loss_row = _out_loss(z, x1, target, gate2)

    net.grad[W_DOWN] = _matmul(yact, dz, "tn", BF16, "down_bwd_w")
    dyact = run("down_bwd_x", _matmul, dz, net.full[W_DOWN], "nt", BF16, "down_bwd_x")
    du, dconv_b, dconv_w = run("conv_gate_bwd", _conv_gate_bwd, dyact, u, conv_w, conv_b)
    net.grad[W_UP] = run("up_bwd_w", _matmul, h2, du, "tn", BF16, "up_bwd_w")
    dh2 = run("up_bwd_x", _matmul, du, net.full[W_UP], "nt", F32, "up_bwd_x")
    dx1, dshift2, dscale2, dnorm2, dgate1, dmix = run(
        "norm2_bwd", _norm_mod_bwd, dh2, x1, norm2_w, scale2, dout, "norm2_bwd", mix=mix, gate=gate1)
    net.grad[W_OUT] = run("mix_bwd_w", _matmul, cat, dmix, "tn", BF16, "mix_bwd_w")
    dcat = run("mix_bwd_x", _matmul, dmix, net.full[W_OUT], "nt", F32, "mix_bwd_x")
    dhq, dhf, dhi, dhg, dlb, dhg_norm = run("hgrn_bwd", _hgrn_bwd, proj, lb_logits, hg_norm_w, o_raw, states, dcat)
    delta = _attn_delta(dcat, b_out_f32)
    acc = None
    for dil in DILS:
        acc = run(f"attn_bwd_d{dil}", _attn_bwd, proj, qn, kn, lse, delta, dcat, dil, acc)
    daq, dak, dav, dq_norm, dk_norm = _qk_norm_bwd(proj, *acc, q_norm_w, k_norm_w)
    dproj = jnp.concatenate([dhq, dhf, dhi, dhg, daq, dak, dav], axis=1)
    for q in range(IN_CHUNKS):
        net.grad[W_INQ + q] = run(f"proj_bwd_w_{q}", _matmul, h, dproj, "tn", BF16, f"proj_bwd_w_{q}",
                                  m_blocks=(D // IN_CHUNKS, [q]))
    dh = run("proj_bwd_x", _matmul, dproj, net.full[W_IN], "nt", F32, "proj_bwd_x")
    grad_x, dshift1, dscale1, dnorm1 = run("norm1_bwd", _norm_mod_bwd, dh, x, norm1_w, scale1, dx1, "norm1_bwd")

    dmod = jnp.concatenate([dshift1, dscale1, dgate1, dshift2, dscale2, dgate2], axis=1)
    small = dict(norm1_w=dnorm1, lb=dlb, hg_norm_w=dhg_norm, q_norm_w=dq_norm, k_norm_w=dk_norm,
                 norm2_w=dnorm2, conv_b=dconv_b, conv_w=dconv_w)
    return loss_row, grad_x, dmod, small


def _place():
    x, y, c = lax.axis_index("x"), lax.axis_index("y"), lax.axis_index("c")
    others = [(1 - x, y), (x, 1 - y), (1 - x, 1 - y)]
    return x, y, c, others


def _remote(src, dst, send_sems, recv_sems, k, to):
    return pltpu.make_async_remote_copy(src_ref=src, dst_ref=dst, send_sem=send_sems.at[k], recv_sem=recv_sems.at[k],
                                        device_id=to, device_id_type=MESH)


class _Phase:
    def __init__(self):
        self.ins, self.outs, self.aliases, self.groups, self.n = [], [], {}, [], 0

    def add(self, ins, outs, aliases, n, copies):
        i0, o0 = len(self.ins), len(self.outs)
        self.ins += list(ins)
        self.outs += list(outs)
        self.aliases.update({i0 + i: o0 + o for i, o in aliases.items()})
        self.groups.append((slice(i0, i0 + len(ins)), slice(o0, o0 + len(outs)), copies))
        self.n += n
        return slice(o0, o0 + len(outs))

    def build(self, cin, cout, send_sems, recv_sems, landing):
        res = []
        for si, so, copies in self.groups:
            for src, dst, land, peer in copies(cin[si], cout[so]):
                k = len(res)
                res.append(_remote(land, land, send_sems, recv_sems, k, peer) if landing
                           else _remote(src, dst, send_sems, recv_sems, k, peer))
        assert len(res) == self.n
        return res


def _pcall(body, args, phase=None, **kw):
    if phase is None or not phase.groups:
        res = pl.pallas_call(body, **kw)(*args)
        return res if phase is None else (res, ())
    grid = tuple(kw.pop("grid", ()))
    out_shape, out_specs = kw.pop("out_shape"), kw.pop("out_specs")
    single = not isinstance(out_shape, (tuple, list))
    out_shape = [out_shape] if single else list(out_shape)
    out_specs = [out_specs] if single else list(out_specs)
    scratch = list(kw.pop("scratch_shapes", ()))
    aliases = dict(kw.pop("input_output_aliases", {}))
    n_in, n_out, n_ci, n_co = len(args), len(out_shape), len(phase.ins), len(phase.outs)
    aliases.update({n_in + i: n_out + o for i, o in phase.aliases.items()})

    def hosted(*refs):
        ins, cin = refs[:n_in], refs[n_in:n_in + n_ci]
        outs = refs[n_in + n_ci:n_in + n_ci + n_out]
        cout = refs[n_in + n_ci + n_out:n_in + n_ci + n_out + n_co]
        scr, send_sems, recv_sems = refs[n_in + n_ci + n_out + n_co:-2], refs[-2], refs[-1]
        ids = [pl.program_id(a) for a in range(len(grid))]

        def start():
            for cp in phase.build(cin, cout, send_sems, recv_sems, False):
                cp.start()

        def finish():
            for cp in phase.build(cin, cout, send_sems, recv_sems, False):
                cp.wait_send()
            for cp in phase.build(cin, cout, send_sems, recv_sems, True):
                cp.wait_recv()

        if grid:
            first = functools.reduce(jnp.logical_and, [i == 0 for i in ids])
            last = functools.reduce(jnp.logical_and, [i == g - 1 for i, g in zip(ids, grid)])
            pl.when(first)(start)
            body(*ins, *outs, *scr)
            pl.when(last)(finish)
        else:
            start()
            body(*ins, *outs, *scr)
            finish()

    res = pl.pallas_call(
        hosted, out_shape=tuple(out_shape + phase.outs), grid=grid,
        in_specs=list(kw.pop("in_specs")) + [HBM_SPEC] * n_ci, out_specs=tuple(out_specs + [HBM_SPEC] * n_co),
        input_output_aliases=aliases,
        scratch_shapes=scratch + [pltpu.SemaphoreType.DMA((phase.n,)), pltpu.SemaphoreType.DMA((phase.n,))],
        **kw,
    )(*args, *phase.ins)
    outs = res[:n_out]
    return (outs[0] if single else tuple(outs)), tuple(res[n_out:])


def _comm_only(name, phase):
    return _pcall(lambda: None, [], phase, name=name, out_shape=(), in_specs=[], out_specs=())[1]


def _all_gather_rows(block, name, phase=None):
    m_per, n = block.shape

    def body(x_ref, out_ref, send_sems, recv_sems, local_sem):
        x, y, c, chips = _place()
        me, sibling = (x, y, c), (x, y, 1 - c)

        def rows(px, py, pc):
            return out_ref.at[pl.ds((4 * px + 2 * py + pc) * m_per, m_per), :]

        def copy(k, blk, to, src=None):
            return _remote(rows(*blk) if src is None else src, rows(*blk), send_sems, recv_sems, k, to)

        mine = pltpu.make_async_copy(x_ref, rows(*me), local_sem)
        mine.start()
        first = [copy(0, me, sibling, src=x_ref)]
        first += [copy(1 + j, me, (*chip, c), src=x_ref) for j, chip in enumerate(chips)]
        for cp in first:
            cp.start()
        passed = [copy(4 + j, (*chip, c), sibling) for j, chip in enumerate(chips)]
        for j, chip in enumerate(chips):
            copy(1 + j, (*chip, c), me).wait_recv()
            passed[j].start()
        copy(0, sibling, me).wait_recv()
        for j, chip in enumerate(chips):
            copy(4 + j, (*chip, 1 - c), me).wait_recv()
        for cp in first + passed:
            cp.wait_send()
        mine.wait()

    return _pcall(
        body, [block], phase, name=name, out_shape=jax.ShapeDtypeStruct((N_DEV * m_per, n), block.dtype),
        in_specs=[VMEM_SPEC], out_specs=VMEM_SPEC,
        scratch_shapes=[pltpu.SemaphoreType.DMA((7,)), pltpu.SemaphoreType.DMA((7,)), pltpu.SemaphoreType.DMA],
    )


class _Geo:
    def __init__(self, kind, shard_shape):
        self.kind = kind
        self.shard_shape = tuple(shard_shape)
        self.hr, self.hc = shard_shape[0] // 2, shard_shape[1]
        if kind == "col":
            self.full_shape = (shard_shape[0], N_CHIPS * shard_shape[1])
        else:
            self.full_shape = (N_CHIPS * shard_shape[0], shard_shape[1])

    def full_half(self, ref, j, c):
        return self.piece(ref, j, c, 0, 1, 1)

    def piece(self, ref, j, c, p0, p1, pieces, part=None):
        pr = self.hr // pieces
        off, n = p0 * pr, (p1 - p0) * pr
        if part is not None:
            top = (n // 32) * 16
            off, n = (off, top) if part == 0 else (off + top, n - top)
        if self.kind == "col":
            return ref.at[pl.ds(pl.multiple_of(c * self.hr + off, 16), n),
                          pl.ds(pl.multiple_of(j * self.hc, 128), self.hc)]
        return ref.at[pl.ds(pl.multiple_of((2 * j + c) * self.hr + off, 16), n), :]


WEIGHT_KINDS = ("col", "row", "col", "row")
NW = len(WEIGHT_KINDS)


def _comm_call(body, name, ins, outs, n_remote, aliases=None):
    return pl.pallas_call(
        body, name=name, out_shape=tuple(outs),
        in_specs=[HBM_SPEC] * len(ins), out_specs=tuple([HBM_SPEC] * len(outs)),
        input_output_aliases=aliases or {},
        scratch_shapes=[pltpu.SemaphoreType.DMA((n_remote,)), pltpu.SemaphoreType.DMA((n_remote,))],
    )(*ins)


ROW_TILES = (256, 176, 128, 64, 32, 16)


def _cast_into_full(shard, geo, place, name):
    rs, cs = shard.shape
    tr = _pick(rs, ROW_TILES)
    nb = rs // tr

    def body(place_ref, s_ref, o_ref):
        o_ref[...] = s_ref[...].astype(BF16)

    if geo.kind == "col":
        out_map = lambda i, place_ref: (i, place_ref[0])
    else:
        out_map = lambda i, place_ref: (place_ref[0] * nb + i, 0)
    return pl.pallas_call(
        body, name=name, out_shape=jax.ShapeDtypeStruct(geo.full_shape, BF16),
        grid_spec=pltpu.PrefetchScalarGridSpec(
            num_scalar_prefetch=1, grid=(nb,),
            in_specs=[pl.BlockSpec((tr, cs), lambda i, place_ref: (i, 0))],
            out_specs=pl.BlockSpec((tr, cs), out_map)),
        compiler_params=_params(1),
    )(place, shard)


def _gather_weight(full, geo, pieces, name):
    per = 8

    def body(in_ref, ref, send_sems, recv_sems):
        x, y, c, _ = _place()
        j, jx, jy, jo = 2 * x + y, 2 * (1 - x) + y, 2 * x + (1 - y), 2 * (1 - x) + (1 - y)
        nx, ny, sib = (1 - x, y, c), (x, 1 - y, c), (x, y, 1 - c)

        def cp(region, k, to):
            return _remote(region, region, send_sems, recv_sems, k, to)

        def reg(chip, p, part=None, core=c):
            return geo.piece(ref, chip, core, p, p + 1, pieces, part)

        sent = []
        for p in range(pieces):
            sent += [cp(reg(j, p), per * p, nx), cp(reg(j, p), per * p + 1, ny)]
        for d in sent:
            d.start()
        for p in range(pieces):
            cp(reg(jx, p), per * p, nx).wait_recv()
            new = [cp(reg(jx, p, 0), per * p + 2, ny), cp(reg(jx, p), per * p + 4, sib)]
            cp(reg(jy, p), per * p + 1, ny).wait_recv()
            new += [cp(reg(jy, p, 1), per * p + 3, nx), cp(reg(jy, p), per * p + 5, sib)]
            for d in new:
                d.start()
            sent += new
        for p in range(pieces):
            cp(reg(jo, p, 0), per * p + 2, ny).wait_recv()
            cp(reg(jo, p, 1), per * p + 3, nx).wait_recv()
            new = [cp(reg(jo, p, 0), per * p + 6, sib), cp(reg(jo, p, 1), per * p + 7, sib)]
            for d in new:
                d.start()
            sent += new
        for p in range(pieces):
            cp(reg(jx, p, None, 1 - c), per * p + 4, sib).wait_recv()
            cp(reg(jy, p, None, 1 - c), per * p + 5, sib).wait_recv()
            cp(reg(jo, p, 0, 1 - c), per * p + 6, sib).wait_recv()
            cp(reg(jo, p, 1, 1 - c), per * p + 7, sib).wait_recv()
        for d in sent:
            d.wait_send()

    return _comm_call(body, name, [full], [_same(full)], per * pieces, aliases={0: 0})[0]


def _same(a):
    return jax.ShapeDtypeStruct(a.shape, a.dtype)


def _gather_ici(full, geo, p0, p1, pieces):
    def copies(ins, outs):
        x, y, c, _ = _place()
        mine = geo.piece(outs[0], 2 * x + y, c, p0, p1, pieces)
        return [(mine, mine, geo.piece(outs[0], 2 * ox + oy, c, p0, p1, pieces), (ox, oy, c))
                for ox, oy in ((1 - x, y), (x, 1 - y))]

    return dict(ins=[full], outs=[_same(full)], aliases={0: 0}, n=2, copies=copies)


def _gather_relay(full, geo, p0, p1, pieces):
    def copies(ins, outs):
        x, y, c, _ = _place()
        jx, jy, jo = 2 * (1 - x) + y, 2 * x + (1 - y), 2 * (1 - x) + (1 - y)
        reg = lambda chip, part: geo.piece(outs[0], chip, c, p0, p1, pieces, part)
        return [(reg(jx, 0), reg(jx, 0), reg(jo, 0), (x, 1 - y, c)), (reg(jy, 1), reg(jy, 1), reg(jo, 1), (1 - x, y, c))]

    return dict(ins=[full], outs=[_same(full)], aliases={0: 0}, n=2, copies=copies)


def _gather_d2d(full, geo):
    def copies(ins, outs):
        x, y, c, chips = _place()
        return [(geo.full_half(outs[0], 2 * ox + oy, c), geo.full_half(outs[0], 2 * ox + oy, c),
                 geo.full_half(outs[0], 2 * ox + oy, 1 - c), (x, y, 1 - c)) for ox, oy in chips]

    return dict(ins=[full], outs=[_same(full)], aliases={0: 0}, n=3, copies=copies)


def _swap_d2d(grad, geo):
    def copies(ins, outs):
        x, y, c, _ = _place()
        return [(geo.full_half(ins[0], j, 1 - c), outs[0].at[j], outs[0].at[j], (x, y, 1 - c)) for j in range(N_CHIPS)]

    return dict(ins=[grad], outs=[jax.ShapeDtypeStruct((N_CHIPS, geo.hr, geo.hc), BF16)], aliases={}, n=N_CHIPS,
                copies=copies)


SEM_SPEC = pl.BlockSpec(memory_space=pltpu.SEMAPHORE)
SIDE_EFFECT = pltpu.SideEffectType.DATAFLOW_SIDE_EFFECTING


def _scatter_start(pair, geo, name):
    def body(pair_ref, land_ref, *rest):
        sems, token = rest[:6], rest[-1]
        x, y, c, chips = _place()
        for k, (ox, oy) in enumerate(chips):
            pltpu.make_async_remote_copy(src_ref=pair_ref.at[2 * ox + oy], dst_ref=land_ref.at[k], send_sem=sems[k],
                                         recv_sem=sems[3 + k], device_id=(ox, oy, c), device_id_type=MESH).start()
        token[...] = jnp.zeros_like(token)

    land = jax.ShapeDtypeStruct((3, geo.hr, geo.hc), BF16)
    return pl.pallas_call(
        body, name=name,
        out_shape=(pltpu.SemaphoreType.DMA(()),) * 6 + (pltpu.HBM(pair.shape, pair.dtype), pltpu.HBM(land.shape, land.dtype),
                                                       jax.ShapeDtypeStruct((8, 128), F32)),
        in_specs=(HBM_SPEC, HBM_SPEC), out_specs=(SEM_SPEC,) * 6 + (HBM_SPEC, HBM_SPEC, VMEM_SPEC),
        input_output_aliases={0: 6, 1: 7},
        compiler_params=pltpu.CompilerParams(has_side_effects=SIDE_EFFECT),
    )(pltpu.with_memory_space_constraint(pair, pltpu.HBM),
      pltpu.with_memory_space_constraint(lax.empty(land.shape, land.dtype), pltpu.HBM))


def _scatter_wait(started, after, name):
    sems, pair_thru, land_thru = started[:6], started[6], started[7]

    def body(pair_ref, land_ref, *rest):
        sems = rest[:6]
        x, y, c, chips = _place()
        for k, (ox, oy) in enumerate(chips):
            copy = pltpu.make_async_remote_copy(src_ref=pair_ref.at[2 * ox + oy], dst_ref=land_ref.at[k], send_sem=sems[k],
                                                recv_sem=sems[3 + k], device_id=(ox, oy, c), device_id_type=MESH)
            copy.wait_send()
            copy.wait_recv()

    return pl.pallas_call(
        body, name=name,
        out_shape=(pltpu.HBM(pair_thru.shape, pair_thru.dtype), pltpu.HBM(land_thru.shape, land_thru.dtype)),
        in_specs=(HBM_SPEC, HBM_SPEC) + (SEM_SPEC,) * 6 + (pl.BlockSpec(memory_space=pl.ANY),),
        out_specs=(HBM_SPEC, HBM_SPEC), input_output_aliases={0: 0, 1: 1},
        compiler_params=pltpu.CompilerParams(has_side_effects=SIDE_EFFECT),
    )(pair_thru, land_thru, *sems, after)


def _scatter_ici(pair, recv, geo, p0, p1, pieces):
    pr = geo.hr // pieces
    rows = pl.ds(p0 * pr, (p1 - p0) * pr)

    def copies(ins, outs):
        x, y, c, chips = _place()
        return [(ins[0].at[2 * ox + oy, rows, :], outs[0].at[k, rows, :], outs[0].at[k, rows, :], (ox, oy, c))
                for k, (ox, oy) in enumerate(chips)]

    out = jax.ShapeDtypeStruct((3, geo.hr, geo.hc), BF16)
    if recv is None:
        return dict(ins=[pair], outs=[out], aliases={}, n=3, copies=copies)
    return dict(ins=[pair, recv], outs=[out], aliases={1: 0}, n=3, copies=copies)


def _join_d2d(shard, geo, row0=0):
    def copies(ins, outs):
        x, y, c, _ = _place()
        mine = outs[0].at[pl.ds(pl.multiple_of(row0 + c * geo.hr, 8), geo.hr), :]
        other = outs[0].at[pl.ds(pl.multiple_of(row0 + (1 - c) * geo.hr, 8), geo.hr), :]
        return [(mine, mine, other, (x, y, 1 - c))]

    return dict(ins=[shard], outs=[_same(shard)], aliases={0: 0}, n=1, copies=copies)


def _add_pair(grad, got, geo, place, name):
    tr = _pick(geo.hr, ROW_TILES)
    nb = geo.hr // tr

    def body(place_ref, a_ref, b_ref, o_ref):
        o_ref[0] = (a_ref[...].astype(F32) + b_ref[0].astype(F32)).astype(BF16)

    if geo.kind == "col":
        own_map = lambda j, i, place_ref: (place_ref[1] * nb + i, j)
    else:
        own_map = lambda j, i, place_ref: ((2 * j + place_ref[1]) * nb + i, 0)
    spec = pl.BlockSpec((1, tr, geo.hc), lambda j, i, place_ref: (j, i, 0))
    return pl.pallas_call(
        body, name=name, out_shape=jax.ShapeDtypeStruct((N_CHIPS, geo.hr, geo.hc), BF16),
        grid_spec=pltpu.PrefetchScalarGridSpec(
            num_scalar_prefetch=1, grid=(N_CHIPS, nb),
            in_specs=[pl.BlockSpec((tr, geo.hc), own_map), spec], out_specs=spec),
        compiler_params=_params(2),
    )(place, grad, got)


def _add_four(pair, recv, geo, place, name, rows=None, row0=0, into=None):
    tr = _pick(geo.hr, ROW_TILES)
    nb = geo.hr // tr
    rows = 2 * geo.hr if rows is None else rows

    def body(place_ref, p_ref, r_ref, *rest):
        rest[-1][...] = ((p_ref[0].astype(F32) + r_ref[0].astype(F32)) + r_ref[1].astype(F32)) + r_ref[2].astype(F32)

    in_specs = [pl.BlockSpec((1, tr, geo.hc), lambda i, place_ref: (place_ref[0], i, 0)),
                pl.BlockSpec((3, tr, geo.hc), lambda i, place_ref: (0, i, 0))]
    args = [place, pair, recv]
    if into is not None:
        in_specs.append(pl.BlockSpec(memory_space=pl.ANY))
        args.append(into)
    return pl.pallas_call(
        body, name=name, out_shape=jax.ShapeDtypeStruct((rows, geo.hc), F32),
        grid_spec=pltpu.PrefetchScalarGridSpec(
            num_scalar_prefetch=1, grid=(nb,), in_specs=in_specs,
            out_specs=pl.BlockSpec((tr, geo.hc), lambda i, place_ref: (row0 // tr + place_ref[1] * nb + i, 0))),
        input_output_aliases={3: 0} if into is not None else {},
        compiler_params=_params(1),
    )(*args)


PIECES = (4, 1, 16, 8) + (1,) * IN_CHUNKS
SCHEDULE = {
    "proj_fwd": (("gather_ici", W_OUT, 0, 1), ("gather_ici", W_UP, 0, 6)),
    "hgrn_fwd": (("gather_relay", W_OUT, 0, 1), ("gather_relay", W_UP, 0, 6), ("gather_ici", W_UP, 6, 12)),
    "attn_fwd_d1": (("gather_relay", W_UP, 6, 12),),
    "attn_fwd_d4": (("gather_ici", W_UP, 12, 16), ("gather_d2d", W_OUT)),
    "attn_fwd_d16": (("gather_relay", W_UP, 12, 16), ("gather_ici", W_DOWN, 0, 2)),
    "mix_fwd": (("gather_d2d", W_UP), ("gather_ici", W_DOWN, 2, 4)),
    "up_fwd": (("gather_ici", W_DOWN, 4, 8), ("gather_relay", W_DOWN, 0, 4)),
    "conv_gate_fwd_a": (("gather_relay", W_DOWN, 4, 8),),
    "conv_gate_fwd_b": (("gather_d2d", W_DOWN),),
    "down_bwd_x": (("swap", W_DOWN),),
    "conv_gate_bwd": (("scatter", W_DOWN, 0, 4),),
    "up_bwd_w": (("scatter", W_DOWN, 4, 8),),
    "up_bwd_x": (("swap", W_UP),),
    "norm2_bwd": (("scatter", W_UP, 0, 1),),
    "mix_bwd_w": (("scatter", W_UP, 1, 2),),
    "mix_bwd_x": (("swap", W_OUT), ("scatter", W_UP, 2, 3)),
    "hgrn_bwd": (("scatter", W_UP, 3, 9), ("join", W_DOWN)),
    "attn_bwd_d1": (("scatter", W_UP, 9, 12),),
    "attn_bwd_d4": (("scatter", W_OUT, 0, 1),),
    "attn_bwd_d16": (("scatter", W_UP, 12, 16),),
    "proj_bwd_w_0": (("join", W_UP), ("join", W_OUT)),
    "proj_bwd_w_1": (("swap", W_INQ),),
    "proj_bwd_w_2": (("swap", W_INQ + 1),),
    "proj_bwd_w_3": (("swap", W_INQ + 2), ("scatter", W_INQ, 0, 1)),
    "proj_bwd_x": (("swap", W_INQ + 3), ("scatter", W_INQ + 1, 0, 1), ("scatter", W_INQ + 2, 0, 1)),
    "gather_stats": (("join", W_INQ), ("join", W_INQ + 1), ("join", W_INQ + 2)),
    "grad_in_join": (("join", W_INQ + 3),),
}


class _Net:
    def __init__(self, shards, place):
        self.place = place
        self.geos = [_Geo(k, s.shape) for k, s in zip(WEIGHT_KINDS, shards)]
        self.full = [_cast_into_full(s, g, place, f"cast_shard_{w}") for w, (s, g) in enumerate(zip(shards, self.geos))]
        self.full[W_IN] = _gather_weight(self.full[W_IN], self.geos[W_IN], PIECES[W_IN], "gather_w_in")
        rows, cols = shards[W_IN].shape
        self.geos += [_Geo("col", (rows // IN_CHUNKS, cols))] * IN_CHUNKS
        n = NW + IN_CHUNKS
        self.grad, self.pair, self.recv, self.shard = [None] * n, [None] * n, [None] * n, [None] * n
        self.in_rows, self.in_shard = rows, None
        self.when_joined, self.joined = {}, {}
        self.slots = []

    def _row0(self, w):
        return (w - W_INQ) * 2 * self.geos[w].hr

    def host(self, name):
        phase = _Phase()
        self.slots = []
        groups = {}
        for item in SCHEDULE.get(name, ()):
            kind, w = item[0], item[1]
            geo = self.geos[w]
            key = len(groups)
            if kind == "gather_ici":
                spec, key = _gather_ici(self.full[w], geo, item[2], item[3], PIECES[w]), ("full", w)
            elif kind == "gather_relay":
                spec, key = _gather_relay(self.full[w], geo, item[2], item[3], PIECES[w]), ("full", w)
            elif kind == "gather_d2d":
                spec, key = _gather_d2d(self.full[w], geo), ("full", w)
            elif kind == "swap":
                spec = _swap_d2d(self.grad[w], geo)
            elif kind == "scatter":
                spec, key = _scatter_ici(self.pair[w], self.recv[w], geo, item[2], item[3], PIECES[w]), ("recv", w)
            elif w >= W_INQ:
                spec, key = _join_d2d(self.in_shard, geo, self._row0(w)), "in_shard"
            else:
                spec = _join_d2d(self.shard[w], geo)
            groups.setdefault(key, []).append((item, spec))
        for group in groups.values():
            specs = [s for _, s in group]
            merged = dict(specs[0], n=sum(s["n"] for s in specs),
                          copies=lambda ins, outs, specs=specs: [t for s in specs for t in s["copies"](ins, outs)])
            self.slots.append(([item for item, _ in group], phase.add(**merged)))
        return phase

    def done(self, name, comm):
        for items, sl in sorted(self.slots, key=lambda s: s[0][0][0] == "scatter"):
            out = comm[sl][0]
            for item in items:
                kind, w = item[0], item[1]
                if kind in ("gather_ici", "gather_relay", "gather_d2d"):
                    self.full[w] = out
                elif kind == "swap":
                    self.pair[w] = _add_pair(self.grad[w], out, self.geos[w], self.place, f"grad_pair_sum_{w}")
                elif kind == "scatter":
                    self.recv[w] = out
                    if item[3] == PIECES[w] and w >= W_INQ:
                        self.in_shard = _add_four(self.pair[w], out, self.geos[w], self.place, f"grad_chip_sum_{w}",
                                                  rows=self.in_rows, row0=self._row0(w), into=self.in_shard)
                    elif item[3] == PIECES[w]:
                        self.shard[w] = _add_four(self.pair[w], out, self.geos[w], self.place, f"grad_chip_sum_{w}")
                elif w >= W_INQ:
                    self.in_shard = out
                else:
                    self.shard[w] = out
                    if w in self.when_joined:
                        self.joined[w] = self.when_joined[w](out)

    def alone(self, name):
        self.done(name, _comm_only(name, self.host(name)))


CONVW_SHARD = 3 * DFF // N_CHIPS
COND_PAD = 8 * 896
SMALL_SIZES = (("norm1_w", D), ("lb", HW), ("hg_norm_w", DH), ("q_norm_w", DH), ("k_norm_w", DH),
               ("norm2_w", D), ("conv_b", DFF), ("conv_w", 3 * DFF), ("loss", 128))
SMALL_TOTAL = sum(n for _, n in SMALL_SIZES)
STATS_PAD = 8 * 5120


def kernel(x, c, w_ada, b_ada, norm1_w, w_in, lb_logits, hg_norm_w, q_norm_w, k_norm_w, w_out, norm2_w, w_up, conv_w, conv_b, w_down, loss_target, m_w_ada, m_b_ada, m_norm1_w, m_w_in, m_lb_logits, m_hg_norm_w, m_q_norm_w, m_k_norm_w, m_w_out, m_norm2_w, m_w_up, m_conv_w, m_conv_b, m_w_down, v_w_ada, v_b_ada, v_norm1_w, v_w_in, v_lb_logits, v_hg_norm_w, v_q_norm_w, v_k_norm_w, v_w_out, v_norm2_w, v_w_up, v_conv_w, v_conv_b, v_w_down):
    chip = 2 * lax.axis_index("x") + lax.axis_index("y")
    dev = 2 * chip + lax.axis_index("c")

    cond = jnp.concatenate([c, conv_w[0].reshape(1, CONVW_SHARD), jnp.zeros((1, COND_PAD - D - CONVW_SHARD), F32)], axis=1)
    cond_all = _all_gather_rows(cond.reshape(8, COND_PAD // 8), "gather_cond").reshape(N_DEV, COND_PAD)
    c_all = cond_all[:, :D]
    conv_w_full = jnp.concatenate(
        [cond_all[2 * j, D:D + CONVW_SHARD].reshape(3, DFF // N_CHIPS) for j in range(N_CHIPS)], axis=1)

    b_shard = lax.dynamic_slice_in_dim(b_ada, chip * ADA_COLS, ADA_COLS, axis=1)
    mod_cols = _all_gather_rows(_ada_fwd(c_all, w_ada[0], b_shard), "gather_mod")
    mod_all = jnp.concatenate([mod_cols[16 * j:16 * j + 8] for j in range(N_CHIPS)], axis=1)
    mod = lax.dynamic_slice_in_dim(mod_all, dev, 1, axis=0)

    place = jnp.stack([chip, lax.axis_index("c")]).astype(jnp.int32)
    net = _Net([w_in[0], w_out[0], w_up[0], w_down[0]], place)
    net.when_joined = {
        W_OUT: lambda grad: _adamw_sc(w_out[0], grad, m_w_out[0], v_w_out[0], "adamw_sc_w_out"),
        W_DOWN: lambda grad: _adamw_sc(w_down[0], grad, m_w_down[0], v_w_down[0], "adamw_sc_w_down"),
        W_UP: lambda grad: _adamw_sc(w_up[0], grad, m_w_up[0], v_w_up[0], "adamw_sc_w_up"),
    }
    early = {W_OUT: "w_out", W_DOWN: "w_down", W_UP: "w_up"}
    loss_row, grad_x, dmod, small = _sequence_step(
        x[0], loss_target[0], mod, norm1_w, lb_logits, hg_norm_w, q_norm_w, k_norm_w, norm2_w, conv_w_full, conv_b, net)
    small["conv_w"] = small["conv_w"].reshape(1, 3 * DFF)
    small["loss"] = loss_row
    stats = jnp.concatenate([dmod] + [small[k] for k, _ in SMALL_SIZES]
                            + [jnp.zeros((1, STATS_PAD - 6 * D - SMALL_TOTAL), F32)], axis=1)
    stats_all, comm = _all_gather_rows(stats.reshape(8, STATS_PAD // 8), "gather_stats", net.host("gather_stats"))
    net.done("gather_stats", comm)
    stats_all = stats_all.reshape(N_DEV, STATS_PAD)
    last = W_INQ + IN_CHUNKS - 1
    started = _scatter_start(net.pair[last], net.geos[last], "grad_in_scatter_start")
    dmod_all = stats_all[:, :6 * D] + started[8][0, 0]
    sums = _sum_rows(stats_all[:, 6 * D:6 * D + SMALL_TOTAL], "small_grad_sum")
    g_small, off = {}, 0
    for k, n in SMALL_SIZES:
        g_small[k] = sums[:, off:off + n]
        off += n
    loss = g_small["loss"][0, 0]
    g_b_ada = _sum_rows(dmod_all, "b_ada_grad_sum")
    ada = _ada_bwd_adamw(c_all, lax.dynamic_slice_in_dim(dmod_all, chip * ADA_COLS, ADA_COLS, axis=1),
                         w_ada[0], m_w_ada[0], v_w_ada[0])
    g_w_ada = ada[0]
    pair_last, recv_last = _scatter_wait(started, ada[3], "grad_in_scatter_wait")
    net.in_shard = _add_four(pair_last, recv_last, net.geos[last], place, f"grad_chip_sum_{last}",
                             rows=net.in_rows, row0=net._row0(last), into=net.in_shard)
    net.alone("grad_in_join")
    g_out, g_up, g_down = net.shard[W_OUT], net.shard[W_UP], net.shard[W_DOWN]
    g_in = net.in_shard
    g_lb = _lb_grad(lb_logits, g_small["lb"])
    g_conv_w = lax.dynamic_slice_in_dim(g_small["conv_w"].reshape(3, DFF), chip * (DFF // N_CHIPS), DFF // N_CHIPS, axis=1)

    names = ["w_ada", "b_ada", "norm1_w", "w_in", "lb_logits", "hg_norm_w", "q_norm_w", "k_norm_w", "w_out",
             "norm2_w", "w_up", "conv_w", "conv_b", "w_down"]
    lead = {"w_ada", "w_in", "w_out", "w_up", "conv_w", "w_down"}
    w = dict(w_ada=w_ada, b_ada=b_ada, norm1_w=norm1_w, w_in=w_in, lb_logits=lb_logits, hg_norm_w=hg_norm_w,
             q_norm_w=q_norm_w, k_norm_w=k_norm_w, w_out=w_out, norm2_w=norm2_w, w_up=w_up, conv_w=conv_w,
             conv_b=conv_b, w_down=w_down)
    m = dict(w_ada=m_w_ada, b_ada=m_b_ada, norm1_w=m_norm1_w, w_in=m_w_in, lb_logits=m_lb_logits,
             hg_norm_w=m_hg_norm_w, q_norm_w=m_q_norm_w, k_norm_w=m_k_norm_w, w_out=m_w_out, norm2_w=m_norm2_w,
             w_up=m_w_up, conv_w=m_conv_w, conv_b=m_conv_b, w_down=m_w_down)
    v = dict(w_ada=v_w_ada, b_ada=v_b_ada, norm1_w=v_norm1_w, w_in=v_w_in, lb_logits=v_lb_logits,
             hg_norm_w=v_hg_norm_w, q_norm_w=v_q_norm_w, k_norm_w=v_k_norm_w, w_out=v_w_out, norm2_w=v_norm2_w,
             w_up=v_w_up, conv_w=v_conv_w, conv_b=v_conv_b, w_down=v_w_down)
    g = dict(w_ada=g_w_ada, b_ada=g_b_ada, norm1_w=g_small["norm1_w"], w_in=g_in, lb_logits=g_lb,
             hg_norm_w=g_small["hg_norm_w"], q_norm_w=g_small["q_norm_w"], k_norm_w=g_small["k_norm_w"], w_out=g_out,
             norm2_w=g_small["norm2_w"], w_up=g_up, conv_w=g_conv_w, conv_b=g_small["conv_b"], w_down=g_down)

    updated = {early[i]: res for i, res in net.joined.items()}
    updated["w_ada"] = (ada[1], ada[2], ada[3], ada[0])
    grads, deltas, new_m, new_v = [], [], [], []
    for n in names:
        strip = (lambda t: t[0]) if n in lead else (lambda t: t)
        wrap = (lambda t: t[None]) if n in lead else (lambda t: t)
        g_n = g[n]
        if n in updated:
            d_n, m_n, v_n, g_n = updated[n]
        elif n == "w_in":
            d_n, m_n, v_n, g_n = _adamw(strip(w[n]), g_n, strip(m[n]), strip(v[n]), f"adamw_{n}", copy_grad=True)
        else:
            d_n, m_n, v_n = _adamw(strip(w[n]), g_n, strip(m[n]), strip(v[n]), f"adamw_{n}")
        grads.append(wrap(g_n))
        deltas.append(wrap(d_n))
        new_m.append(wrap(m_n))
        new_v.append(wrap(v_n))
    return (loss, grad_x[None], *grads, *deltas, *new_m, *new_v)
```

```python
import functools

import jax
import jax.numpy as jnp
from jax import lax
from jax.experimental import pallas as pl
from jax.experimental.pallas import tpu as pltpu
from jax.experimental.pallas import tpu_sc as plsc

F32 = jnp.float32
BF16 = jnp.bfloat16

S = 2048
D = 2048
H = 8
DH = 128
HW = H * DH
CH = 64
NCH = S // CH
DFF = 5632
INC = 7 * HW
BLK = 128
DILS = (1, 4, 16)
EPS = 1e-6
NEG = -1e30
N_CHIPS = 4
N_DEV = 8

ADAM_LR = 0.001
ADAM_B1 = 0.9
ADAM_B2 = 0.999
ADAM_EPS = 1e-08
ADAM_WD = 0.01
ADAM_STEP = 10
ADAM_BLOCK_BYTES = 1 << 21

V7X_VMEM_BYTES = 64 * 1024 * 1024
VMEM_LIMIT = (V7X_VMEM_BYTES * 3) // 4
MESH = pl.DeviceIdType.MESH
HBM_SPEC = pl.BlockSpec(memory_space=pltpu.HBM)
VMEM_SPEC = pl.BlockSpec(memory_space=pltpu.VMEM)

NN = (((1,), (0,)), ((), ()))
NT = (((1,), (1,)), ((), ()))
TN = (((0,), (0,)), ((), ()))


def _params(n_grid):
    return pltpu.CompilerParams(dimension_semantics=("arbitrary",) * n_grid, vmem_limit_bytes=VMEM_LIMIT)


def _dot(a, b, dims=NN):
    return lax.dot_general(a.astype(BF16), b.astype(BF16), dims, preferred_element_type=F32)


def _dot_f32(a, b):
    return lax.dot_general(a, b, NN, precision=lax.Precision.HIGHEST, preferred_element_type=F32)


def _sigmoid(x):
    return 1.0 / (1.0 + jnp.exp(-x))


def _pick(n, cands):
    for t in cands:
        if n % t == 0:
            return t
    raise ValueError(n)


def _matmul(a, b, mode, out_dtype, name, phase=None, m_blocks=None):
    if mode == "nn":
        (m, k), (_, n) = a.shape, b.shape
    elif mode == "nt":
        (m, k), (n, _) = a.shape, b.shape
    else:
        (k, m), (_, n) = a.shape, b.shape
    tm = _pick(m, (1024, 1408, 512))
    a_block = lambda i: i
    if m_blocks is not None:
        tm, blocks = m_blocks
        m = tm * len(blocks)
        step = blocks[1] - blocks[0] if len(blocks) > 1 else 0
        assert all(blk == blocks[0] + step * i for i, blk in enumerate(blocks))
        a_block = lambda i: blocks[0] + step * i
    tn = _pick(n, (1024, 1408, 512))
    tk = _pick(k, (2048, 2816, 1792, 1024, 512))
    nk = k // tk
    dims = {"nn": NN, "nt": NT, "tn": TN}[mode]

    def body(a_ref, b_ref, o_ref, *acc):
        part = lax.dot_general(a_ref[...], b_ref[...], dims, preferred_element_type=F32)
        if nk == 1:
            o_ref[...] = part.astype(o_ref.dtype)
            return
        acc_ref, kk = acc[0], pl.program_id(2)

        @pl.when(kk == 0)
        def _():
            acc_ref[...] = part

        @pl.when(jnp.logical_and(kk > 0, kk < nk - 1))
        def _():
            acc_ref[...] += part

        @pl.when(kk == nk - 1)
        def _():
            o_ref[...] = (acc_ref[...] + part).astype(o_ref.dtype)

    if mode == "tn":
        a_spec = pl.BlockSpec((tk, tm), lambda i, j, kk: (kk, a_block(i)))
    else:
        a_spec = pl.BlockSpec((tm, tk), lambda i, j, kk: (i, kk))
    if mode == "nt":
        b_spec = pl.BlockSpec((tn, tk), lambda i, j, kk: (j, kk))
    else:
        b_spec = pl.BlockSpec((tk, tn), lambda i, j, kk: (kk, j))
    return _pcall(
        body, [a, b], phase, name=name,
        out_shape=jax.ShapeDtypeStruct((m, n), out_dtype),
        grid=(m // tm, n // tn, nk),
        in_specs=[a_spec, b_spec],
        out_specs=pl.BlockSpec((tm, tn), lambda i, j, kk: (i, j)),
        scratch_shapes=[pltpu.VMEM((tm, tn), F32)] if nk > 1 else [],
        compiler_params=_params(3),
    )


TR = 256


def _row_spec(cols=D):
    return pl.BlockSpec((TR, cols), lambda i: (i, 0))


def _vec_spec(cols=D):
    return pl.BlockSpec((1, cols), lambda i: (0, 0))


def _norm_mod(x, nw, scale, shift, name):
    def body(x_ref, nw_ref, sc_ref, sh_ref, h_ref):
        xv = x_ref[...]
        r = lax.rsqrt(jnp.mean(xv * xv, axis=-1, keepdims=True) + EPS)
        h_ref[...] = ((xv * r) * nw_ref[...] * (1.0 + sc_ref[...]) + sh_ref[...]).astype(BF16)

    return pl.pallas_call(
        body, name=name, out_shape=jax.ShapeDtypeStruct((S, D), BF16), grid=(S // TR,),
        in_specs=[_row_spec(), _vec_spec(), _vec_spec(), _vec_spec()], out_specs=_row_spec(),
        compiler_params=_params(1),
    )(x, nw, scale, shift)


def _resid_norm_mod(x, mix, gate, nw, scale, shift, name):
    def body(x_ref, mix_ref, g_ref, nw_ref, sc_ref, sh_ref, x1_ref, h_ref):
        xv = x_ref[...] + g_ref[...] * mix_ref[...]
        x1_ref[...] = xv
        r = lax.rsqrt(jnp.mean(xv * xv, axis=-1, keepdims=True) + EPS)
        h_ref[...] = ((xv * r) * nw_ref[...] * (1.0 + sc_ref[...]) + sh_ref[...]).astype(BF16)

    return pl.pallas_call(
        body, name=name,
        out_shape=(jax.ShapeDtypeStruct((S, D), F32), jax.ShapeDtypeStruct((S, D), BF16)), grid=(S // TR,),
        in_specs=[_row_spec(), _row_spec(), _vec_spec(), _vec_spec(), _vec_spec(), _vec_spec()],
        out_specs=(_row_spec(), _row_spec()),
        compiler_params=_params(1),
    )(x, mix, gate, nw, scale, shift)


def _norm_mod_bwd(dh, xin, nw, scale, dres, name, mix=None, gate=None, phase=None):
    with_gate = mix is not None

    def body(*refs):
        if with_gate:
            dh_ref, x_ref, nw_ref, sc_ref, dres_ref, mix_ref, g_ref, dx_ref, dsh_ref, dsc_ref, dnw_ref, dg_ref, dmix_ref = refs
        else:
            dh_ref, x_ref, nw_ref, sc_ref, dres_ref, dx_ref, dsh_ref, dsc_ref, dnw_ref = refs
        i = pl.program_id(0)
        dhv = dh_ref[...]
        xv = x_ref[...]
        r = lax.rsqrt(jnp.mean(xv * xv, axis=-1, keepdims=True) + EPS)
        xn = xv * r
        nwv = nw_ref[...]
        one_sc = 1.0 + sc_ref[...]
        dxn = dhv * nwv * one_sc
        dx = dres_ref[...] + r * (dxn - xn * jnp.mean(dxn * xn, axis=-1, keepdims=True))
        dx_ref[...] = dx

        @pl.when(i == 0)
        def _():
            dsh_ref[...] = jnp.zeros_like(dsh_ref)
            dsc_ref[...] = jnp.zeros_like(dsc_ref)
            dnw_ref[...] = jnp.zeros_like(dnw_ref)
            if with_gate:
                dg_ref[...] = jnp.zeros_like(dg_ref)

        dsh_ref[...] += jnp.sum(dhv, axis=0, keepdims=True)
        dsc_ref[...] += jnp.sum(dhv * xn * nwv, axis=0, keepdims=True)
        dnw_ref[...] += jnp.sum(dhv * xn * one_sc, axis=0, keepdims=True)
        if with_gate:
            dg_ref[...] += jnp.sum(dx * mix_ref[...], axis=0, keepdims=True)
            dmix_ref[...] = (dx * g_ref[...]).astype(BF16)

    vec = jax.ShapeDtypeStruct((1, D), F32)
    ins = [dh, xin, nw, scale, dres]
    in_specs = [_row_spec(), _row_spec(), _vec_spec(), _vec_spec(), _row_spec()]
    outs = [jax.ShapeDtypeStruct((S, D), F32), vec, vec, vec]
    out_specs = [_row_spec(), _vec_spec(), _vec_spec(), _vec_spec()]
    if with_gate:
        ins += [mix, gate]
        in_specs += [_row_spec(), _vec_spec()]
        outs += [vec, jax.ShapeDtypeStruct((S, D), BF16)]
        out_specs += [_vec_spec(), _row_spec()]
    return _pcall(
        body, ins, phase, name=name, out_shape=tuple(outs), grid=(S // TR,), in_specs=in_specs,
        out_specs=tuple(out_specs), compiler_params=_params(1),
    )


def _tri(lower):
    row = lax.broadcasted_iota(jnp.int32, (CH, CH), 0)
    col = lax.broadcasted_iota(jnp.int32, (CH, CH), 1)
    return (row >= col) if lower else (col >= row)


def _hgrn_gates(hq, hf, lb):
    sig = _sigmoid(hf)
    f = lb + (1.0 - lb) * sig
    g = jnp.log(f)
    sq = _sigmoid(hq)
    return sig, f, g, 1.0 - f, hq * sq, sq


HG_HP = 2
HG_UNROLL = 8


def _proj_col_spec(group):
    return pl.BlockSpec((S, HG_HP * DH), lambda h: (0, group * (H // HG_HP) + h))


def _hgrn_fwd(proj, lb_logits, norm_w, phase=None):
    def body(hq_ref, hf_ref, hi_ref, hg_ref, lbl_ref, nw_ref, out_ref, oraw_ref, st_ref, s_ref):
        nw = nw_ref[...]
        lower = _tri(True)
        ltri = lower.astype(F32)
        s_ref[...] = jnp.zeros_like(s_ref)

        def chunk(n, carry):
            rows = pl.ds(pl.multiple_of(n * CH, CH), CH)
            for j in range(HG_HP):
                cols = slice(j * DH, (j + 1) * DH)
                lb = 1.0 / (1.0 + jnp.exp(lbl_ref[1:2, cols] - lbl_ref[0:1, cols]))
                hq, hf, v, hg = hq_ref[rows, cols], hf_ref[rows, cols], hi_ref[rows, cols], hg_ref[rows, cols]
                _, _, g, kk, q, _ = _hgrn_gates(hq, hf, lb)
                gc = _dot_f32(ltri, g)
                gl = jnp.sum(g, axis=0, keepdims=True)
                qe = q * jnp.exp(gc)
                ke = kk * jnp.exp(gl - gc)
                qh = q * jnp.exp(gc - 0.5 * gl)
                kh = kk * jnp.exp(0.5 * gl - gc)
                st = s_ref[j]
                st_ref[j, pl.ds(n, 1)] = st[None]
                a = jnp.where(lower, _dot(qh, kh, NT), 0.0)
                o = _dot(qe, st, NT) + _dot(a, v)
                s_ref[j] = st * jnp.exp(gl) + _dot(v, ke, TN)
                oraw_ref[rows, cols] = o
                rs = lax.rsqrt(jnp.mean(o * o, axis=-1, keepdims=True) + EPS)
                out_ref[rows, cols] = (o * rs * nw * (hg * _sigmoid(hg))).astype(BF16)
            return carry

        lax.fori_loop(0, NCH, chunk, 0, unroll=HG_UNROLL)

    head_spec = pl.BlockSpec((S, HG_HP * DH), lambda h: (0, h))
    return _pcall(
        body, [proj, proj, proj, proj, lb_logits, norm_w], phase, name="hgrn_fwd",
        out_shape=(jax.ShapeDtypeStruct((S, 2 * HW), BF16), jax.ShapeDtypeStruct((S, HW), F32),
                   jax.ShapeDtypeStruct((H, NCH, DH, DH), F32)),
        grid=(H // HG_HP,),
        in_specs=[_proj_col_spec(0), _proj_col_spec(1), _proj_col_spec(2), _proj_col_spec(3),
                  pl.BlockSpec((2, HG_HP * DH), lambda h: (0, h)), pl.BlockSpec((1, DH), lambda h: (0, 0))],
        out_specs=(head_spec, head_spec, pl.BlockSpec((HG_HP, NCH, DH, DH), lambda h: (h, 0, 0, 0))),
        scratch_shapes=[pltpu.VMEM((HG_HP, DH, DH), F32)],
        compiler_params=_params(1),
    )


def _hgrn_bwd(proj, lb_logits, norm_w, oraw, states, dout, phase=None):
    def body(hq_ref, hf_ref, hi_ref, hg_ref, lbl_ref, nw_ref, oraw_ref, st_ref, do_ref,
             dhq_ref, dhf_ref, dhi_ref, dhg_ref, dlb_ref, dnw_ref, ds_ref, acc_ref):
        h = pl.program_id(0)
        nw = nw_ref[...]
        lower = _tri(True)
        ltri = lower.astype(F32)
        utri = _tri(False).astype(F32)
        last = lax.broadcasted_iota(jnp.int32, (CH, DH), 0) == CH - 1
        ds_ref[...] = jnp.zeros_like(ds_ref)
        acc_ref[...] = jnp.zeros_like(acc_ref)

        @pl.when(h == 0)
        def _():
            dnw_ref[...] = jnp.zeros_like(dnw_ref)

        def chunk(i, carry):
            n = NCH - 1 - i
            rows = pl.ds(pl.multiple_of(n * CH, CH), CH)
            for j in range(HG_HP):
                cols = slice(j * DH, (j + 1) * DH)
                lb = 1.0 / (1.0 + jnp.exp(lbl_ref[1:2, cols] - lbl_ref[0:1, cols]))
                hq, hf, v, hg = hq_ref[rows, cols], hf_ref[rows, cols], hi_ref[rows, cols], hg_ref[rows, cols]
                sig, f, g, kk, q, sq = _hgrn_gates(hq, hf, lb)
                gc = _dot_f32(ltri, g)
                gl = jnp.sum(g, axis=0, keepdims=True)
                eg = jnp.exp(gc)
                ek = jnp.exp(gl - gc)
                gam = jnp.exp(gl)
                ph = jnp.exp(gc - 0.5 * gl)
                pk = jnp.exp(0.5 * gl - gc)
                qe, ke = q * eg, kk * ek
                qh, kh = q * ph, kk * pk
                st = st_ref[j, pl.ds(n, 1)][0]
                dst = ds_ref[j]
                a = jnp.where(lower, _dot(qh, kh, NT), 0.0)
                o = oraw_ref[rows, cols]
                rs = lax.rsqrt(jnp.mean(o * o, axis=-1, keepdims=True) + EPS)
                xh = o * rs
                sg = _sigmoid(hg)
                dgo = do_ref[rows, cols]
                d_on = dgo * (hg * sg)
                dhg_ref[rows, cols] = (dgo * xh * nw * (sg * (1.0 + hg * (1.0 - sg)))).astype(BF16)
                acc_ref[j, 1:2, :] += jnp.sum(d_on * xh, axis=0, keepdims=True)
                dy = d_on * nw
                do = rs * (dy - xh * jnp.mean(dy * xh, axis=-1, keepdims=True))
                dqe = _dot(do, st)
                da = jnp.where(lower, _dot(do, v, NT), 0.0)
                dv = _dot(a, do, TN) + _dot(ke, dst, NT)
                dke = _dot(v, dst)
                dgam = jnp.sum(dst * st, axis=0, keepdims=True)
                dqh = lax.dot_general(da, kh, NN, precision=lax.Precision.HIGH, preferred_element_type=F32)
                dkh = lax.dot_general(da, qh, TN, precision=lax.Precision.HIGH, preferred_element_type=F32)
                dq = dqh * ph + dqe * eg
                dk = dkh * pk + dke * ek
                dgl = jnp.sum(dke * ke, axis=0, keepdims=True) + dgam * gam
                dgc = dqh * qh - dkh * kh + dqe * qe - dke * ke + jnp.where(last, dgl, 0.0)
                dg = _dot_f32(utri, dgc)
                df = dg / f - dk
                dhf_ref[rows, cols] = (df * (1.0 - lb) * sig * (1.0 - sig)).astype(BF16)
                acc_ref[j, 0:1, :] += jnp.sum(df * (1.0 - sig), axis=0, keepdims=True)
                dhq_ref[rows, cols] = (dq * (sq * (1.0 + hq * (1.0 - sq)))).astype(BF16)
                dhi_ref[rows, cols] = dv.astype(BF16)
                ds_ref[j] = dst * gam + _dot(do, qe, TN)
            return carry

        lax.fori_loop(0, NCH, chunk, 0, unroll=HG_UNROLL)
        for j in range(HG_HP):
            dlb_ref[:, j * DH:(j + 1) * DH] = acc_ref[j, 0:1, :]
            dnw_ref[...] += acc_ref[j, 1:2, :]

    head_spec = pl.BlockSpec((S, HG_HP * DH), lambda h: (0, h))
    head_out = jax.ShapeDtypeStruct((S, HW), BF16)
    return _pcall(
        body, [proj, proj, proj, proj, lb_logits, norm_w, oraw, states, dout], phase, name="hgrn_bwd",
        out_shape=(head_out, head_out, head_out, head_out,
                   jax.ShapeDtypeStruct((1, HW), F32), jax.ShapeDtypeStruct((1, DH), F32)),
        grid=(H // HG_HP,),
        in_specs=[_proj_col_spec(0), _proj_col_spec(1), _proj_col_spec(2), _proj_col_spec(3),
                  pl.BlockSpec((2, HG_HP * DH), lambda h: (0, h)), pl.BlockSpec((1, DH), lambda h: (0, 0)),
                  head_spec, pl.BlockSpec((HG_HP, NCH, DH, DH), lambda h: (h, 0, 0, 0)), head_spec],
        out_specs=(head_spec, head_spec, head_spec, head_spec,
                   pl.BlockSpec((1, HG_HP * DH), lambda h: (0, h)), pl.BlockSpec((1, DH), lambda h: (0, 0))),
        scratch_shapes=[pltpu.VMEM((HG_HP, DH, DH), F32), pltpu.VMEM((HG_HP, 8, DH), F32)],
        compiler_params=_params(1),
    )


ATT_SCALE = DH ** -0.5
HEADS_PER_STEP = {1: 8, 4: 1, 16: 1}


def _sub_rows(ref, r, dil, cols):
    if dil == 1:
        return ref[:, cols]
    return ref[pl.ds(r, BLK, stride=dil), cols]


def _set_sub_rows(ref, r, dil, cols, val):
    if dil == 1:
        ref[:, cols] = val
    else:
        ref[pl.ds(r, BLK, stride=dil), cols] = val


def _slopes(dil):
    hp = HEADS_PER_STEP[dil]
    s = jnp.asarray([dil * 2.0 ** (-(h + 1)) for h in range(H)], F32)
    return jnp.broadcast_to(s[:, None], (H, DH)).reshape(H // hp, hp, DH)


def _rms_rows(x):
    rs = lax.rsqrt(jnp.mean(x * x, axis=-1, keepdims=True) + EPS)
    return x * rs, rs


def _att_masks():
    qi = lax.broadcasted_iota(jnp.int32, (BLK, BLK), 0)
    kj = lax.broadcasted_iota(jnp.int32, (BLK, BLK), 1)
    steps_c = qi - kj
    steps_p = qi - kj + BLK
    return steps_c, steps_p, steps_c >= 0, steps_p <= BLK


def _qk_prep(proj, q_w, k_w):
    def body(q_ref, k_ref, qw_ref, kw_ref, qn_ref, kn_ref):
        for h in range(H):
            cols = slice(h * DH, (h + 1) * DH)
            qn_ref[:, cols] = _rms_rows(q_ref[:, cols])[0] * qw_ref[...]
            kn_ref[:, cols] = _rms_rows(k_ref[:, cols])[0] * kw_ref[...]

    out = jax.ShapeDtypeStruct((S, HW), F32)
    wspec = pl.BlockSpec((1, DH), lambda i: (0, 0))
    return pl.pallas_call(
        body, name="qk_prep", out_shape=(out, out), grid=(S // TR,),
        in_specs=[pl.BlockSpec((TR, HW), lambda i: (i, 4)), pl.BlockSpec((TR, HW), lambda i: (i, 5)), wspec, wspec],
        out_specs=(_row_spec(HW), _row_spec(HW)),
        compiler_params=_params(1),
    )(proj, proj, q_w, k_w)


def _qk_norm_bwd(proj, dqn, dkn, dv, q_w, k_w):
    def body(q_ref, k_ref, dqn_ref, dkn_ref, dv_ref, qw_ref, kw_ref, dq_ref, dk_ref, dvo_ref, dqw_ref, dkw_ref):
        i = pl.program_id(0)

        @pl.when(i == 0)
        def _():
            dqw_ref[...] = jnp.zeros_like(dqw_ref)
            dkw_ref[...] = jnp.zeros_like(dkw_ref)

        dvo_ref[...] = dv_ref[...].astype(BF16)
        for x_ref, d_ref, w_ref, o_ref, dw_ref in ((q_ref, dqn_ref, qw_ref, dq_ref, dqw_ref),
                                                   (k_ref, dkn_ref, kw_ref, dk_ref, dkw_ref)):
            for h in range(H):
                cols = slice(h * DH, (h + 1) * DH)
                xh, rs = _rms_rows(x_ref[:, cols])
                d = d_ref[:, cols]
                dw_ref[...] += jnp.sum(d * xh, axis=0, keepdims=True)
                dy = d * w_ref[...]
                o_ref[:, cols] = (rs * (dy - xh * jnp.mean(dy * xh, axis=-1, keepdims=True))).astype(BF16)

    big = jax.ShapeDtypeStruct((S, HW), BF16)
    small = jax.ShapeDtypeStruct((1, DH), F32)
    wspec = pl.BlockSpec((1, DH), lambda i: (0, 0))
    return pl.pallas_call(
        body, name="qk_norm_bwd", out_shape=(big, big, big, small, small), grid=(S // TR,),
        in_specs=[pl.BlockSpec((TR, HW), lambda i: (i, 4)), pl.BlockSpec((TR, HW), lambda i: (i, 5)),
                  _row_spec(HW), _row_spec(HW), _row_spec(HW), wspec, wspec],
        out_specs=(_row_spec(HW), _row_spec(HW), _row_spec(HW), wspec, wspec),
        compiler_params=_params(1),
    )(proj, proj, dqn, dkn, dv, q_w, k_w)


def _attn_delta(do, o):
    def body(do_ref, o_ref, d_ref):
        for h in range(H):
            cols = slice(h * DH, (h + 1) * DH)
            d_ref[:, cols] = jnp.broadcast_to(jnp.sum(do_ref[:, cols] * o_ref[:, cols], axis=-1, keepdims=True), (TR, DH))

    return pl.pallas_call(
        body, name="attn_delta", out_shape=jax.ShapeDtypeStruct((S, HW), F32), grid=(S // TR,),
        in_specs=[pl.BlockSpec((TR, HW), lambda i: (i, 1)), _row_spec(HW)], out_specs=_row_spec(HW),
        compiler_params=_params(1),
    )(do, o)


def _attn_fwd(proj, qn, kn, dil, phase=None):
    hp = HEADS_PER_STEP[dil]
    rows, ng, nb = BLK * dil, H // hp, S // (BLK * dil)

    def body(q_ref, kc_ref, kp_ref, vc_ref, vp_ref, sl_ref, o_ref, lse_ref):
        n = pl.program_id(0)
        steps_c, steps_p, ok_c, ok_p = _att_masks()
        ok_p = jnp.logical_and(ok_p, n > 0)
        fc, fp = steps_c.astype(F32), steps_p.astype(F32)
        for hh in range(hp):
            cols = slice(hh * DH, (hh + 1) * DH)
            sl = sl_ref[0, hh:hh + 1, :]
            for r in range(dil):
                sub = lambda ref: _sub_rows(ref, r, dil, cols)
                qv = sub(q_ref)
                sc = jnp.where(ok_c, _dot(qv, sub(kc_ref), NT) * ATT_SCALE - sl * fc, NEG)
                sp = jnp.where(ok_p, _dot(qv, sub(kp_ref), NT) * ATT_SCALE - sl * fp, NEG)
                m = jnp.maximum(jnp.max(sc, axis=-1, keepdims=True), jnp.max(sp, axis=-1, keepdims=True))
                pc, pp = jnp.exp(sc - m), jnp.exp(sp - m)
                den = jnp.sum(pc, axis=-1, keepdims=True) + jnp.sum(pp, axis=-1, keepdims=True)
                o = (_dot(pc, sub(vc_ref)) + _dot(pp, sub(vp_ref))) / den
                _set_sub_rows(o_ref, r, dil, cols, o)
                _set_sub_rows(lse_ref, r, dil, cols, jnp.broadcast_to(m + jnp.log(den), (BLK, DH)))

    cur = pl.BlockSpec((rows, hp * DH), lambda n, g: (n, g))
    prev = pl.BlockSpec((rows, hp * DH), lambda n, g: (jnp.maximum(n - 1, 0), g))
    vcur = pl.BlockSpec((rows, hp * DH), lambda n, g: (n, 6 * ng + g))
    vprev = pl.BlockSpec((rows, hp * DH), lambda n, g: (jnp.maximum(n - 1, 0), 6 * ng + g))
    out = jax.ShapeDtypeStruct((S, HW), F32)
    return _pcall(
        body, [qn, kn, kn, proj, proj, _slopes(dil)], phase,
        name=f"attn_fwd_d{dil}", out_shape=(out, out), grid=(nb, ng),
        in_specs=[cur, cur, prev, vcur, vprev, pl.BlockSpec((1, hp, DH), lambda n, g: (g, 0, 0))],
        out_specs=(cur, cur),
        compiler_params=_params(2),
    )


def _attn_merge(outs, lses, cat):
    def body(o1, o2, o3, l1, l2, l3, cat_ref, ob_ref, of_ref, lse_ref):
        a, b, c = l1[...], l2[...], l3[...]
        m = jnp.maximum(jnp.maximum(a, b), c)
        ea, eb, ec = jnp.exp(a - m), jnp.exp(b - m), jnp.exp(c - m)
        den = ea + eb + ec
        o = (ea * o1[...] + eb * o2[...] + ec * o3[...]) / den
        ob_ref[...] = o.astype(BF16)
        of_ref[...] = o
        lse_ref[...] = m + jnp.log(den)

    f = jax.ShapeDtypeStruct((S, HW), F32)
    return pl.pallas_call(
        body, name="attn_merge", out_shape=(jax.ShapeDtypeStruct((S, 2 * HW), BF16), f, f), grid=(S // TR,),
        in_specs=[_row_spec(HW)] * 6 + [pl.BlockSpec(memory_space=pl.ANY)],
        out_specs=(pl.BlockSpec((TR, HW), lambda i: (i, 1)), _row_spec(HW), _row_spec(HW)),
        input_output_aliases={6: 0},
        compiler_params=_params(1),
    )(*outs, *lses, cat)


def _attn_bwd(proj, qn, kn, lse, delta, do, dil, acc, phase=None):
    hp = HEADS_PER_STEP[dil]
    rows, ng, nb = BLK * dil, H // hp, S // (BLK * dil)
    has_acc = acc is not None

    def body(*refs):
        q_ref, qn_ref, kp_ref, kc_ref, vp_ref, vc_ref, l_ref, ln_ref, d_ref, dn_ref, do_ref, don_ref, sl_ref = refs[:13]
        refs = refs[13:]
        if has_acc:
            aq_ref, ak_ref, av_ref = refs[:3]
            refs = refs[3:]
        dq_ref, dk_ref, dv_ref = refs
        m = pl.program_id(0)
        steps_c, steps_p, ok_c, ok_p = _att_masks()
        ok_prev = jnp.logical_and(ok_p, m > 0)
        ok_next = jnp.logical_and(ok_p, m < nb - 1)
        fc, fp = steps_c.astype(F32), steps_p.astype(F32)
        for hh in range(hp):
            cols = slice(hh * DH, (hh + 1) * DH)
            sl = sl_ref[0, hh:hh + 1, :]
            for r in range(dil):
                sub = lambda ref: _sub_rows(ref, r, dil, cols)
                qv, qnv, kcv, kpv = sub(q_ref), sub(qn_ref), sub(kc_ref), sub(kp_ref)
                vc, vp = sub(vc_ref), sub(vp_ref)
                dov, donv = sub(do_ref), sub(don_ref)
                lse_m, lse_n = sub(l_ref)[:, 0:1], sub(ln_ref)[:, 0:1]
                delta_m, delta_n = sub(d_ref)[:, 0:1], sub(dn_ref)[:, 0:1]
                p_c = jnp.where(ok_c, jnp.exp(_dot(qv, kcv, NT) * ATT_SCALE - sl * fc - lse_m), 0.0)
                p_p = jnp.where(ok_prev, jnp.exp(_dot(qv, kpv, NT) * ATT_SCALE - sl * fp - lse_m), 0.0)
                p_n = jnp.where(ok_next, jnp.exp(_dot(qnv, kcv, NT) * ATT_SCALE - sl * fp - lse_n), 0.0)
                ds_c = p_c * (_dot(dov, vc, NT) - delta_m)
                ds_p = p_p * (_dot(dov, vp, NT) - delta_m)
                ds_n = p_n * (_dot(donv, vc, NT) - delta_n)
                dqn = (_dot(ds_c, kcv) + _dot(ds_p, kpv)) * ATT_SCALE
                dkn = (_dot(ds_c, qv, TN) + _dot(ds_n, qnv, TN)) * ATT_SCALE
                dv = _dot(p_c, dov, TN) + _dot(p_n, donv, TN)
                if has_acc:
                    dqn, dkn, dv = dqn + sub(aq_ref), dkn + sub(ak_ref), dv + sub(av_ref)
                _set_sub_rows(dq_ref, r, dil, cols, dqn)
                _set_sub_rows(dk_ref, r, dil, cols, dkn)
                _set_sub_rows(dv_ref, r, dil, cols, dv)

    def shifted(shift, m):
        if shift < 0:
            return jnp.maximum(m - 1, 0)
        if shift > 0:
            return jnp.minimum(m + 1, nb - 1)
        return m

    def spec(shift, group=0):
        return pl.BlockSpec((rows, hp * DH), lambda m, g: (shifted(shift, m), group * ng + g))

    ins = [qn, qn, kn, kn, proj, proj, lse, lse, delta, delta, do, do, _slopes(dil)]
    in_specs = [spec(0), spec(1), spec(-1), spec(0), spec(-1, 6), spec(0, 6), spec(0), spec(1), spec(0), spec(1),
                spec(0, 1), spec(1, 1), pl.BlockSpec((1, hp, DH), lambda m, g: (g, 0, 0))]
    if has_acc:
        ins += list(acc)
        in_specs += [spec(0)] * 3
    big = jax.ShapeDtypeStruct((S, HW), F32)
    return _pcall(
        body, ins, phase, name=f"attn_bwd_d{dil}", out_shape=(big, big, big), grid=(nb, ng),
        in_specs=in_specs, out_specs=(spec(0),) * 3,
        compiler_params=_params(2),
    )


TC = 512
NCT = DFF // TC


def _shift_rows(a, k):
    rows = lax.broadcasted_iota(jnp.int32, a.shape, 0)
    rolled = pltpu.roll(a, k % S, 0)
    if k > 0:
        return jnp.where(rows >= k, rolled, 0.0)
    return jnp.where(rows < S + k, rolled, 0.0)


def _conv_pre(a, w_ref, b_ref):
    return b_ref[...] + w_ref[0:1, :] * _shift_rows(a, 2) + w_ref[1:2, :] * _shift_rows(a, 1) + w_ref[2:3, :] * a


def _conv_gate_fwd(u, conv_w, conv_b, name, tiles=(0, NCT), into=None, phase=None):
    t0, t1 = tiles

    def body(a_ref, g_ref, w_ref, b_ref, *rest):
        pre = _conv_pre(a_ref[...].astype(F32), w_ref, b_ref)
        rest[-1][...] = (pre * _sigmoid(pre) * g_ref[...].astype(F32)).astype(BF16)

    args = [u, u, conv_w, conv_b]
    in_specs = [pl.BlockSpec((S, TC), lambda i: (0, t0 + i)), pl.BlockSpec((S, TC), lambda i: (0, NCT + t0 + i)),
                pl.BlockSpec((3, TC), lambda i: (0, t0 + i)), pl.BlockSpec((1, TC), lambda i: (0, t0 + i))]
    kw = {}
    if into is not None:
        args.append(into)
        in_specs.append(pl.BlockSpec(memory_space=pl.ANY))
        kw["input_output_aliases"] = {4: 0}
    return _pcall(
        body, args, phase, name=name, out_shape=jax.ShapeDtypeStruct((S, DFF), BF16), grid=(t1 - t0,),
        in_specs=in_specs, out_specs=pl.BlockSpec((S, TC), lambda i: (0, t0 + i)),
        compiler_params=_params(1), **kw,
    )


def _conv_gate_bwd(dy, u, conv_w, conv_b, phase=None):
    def body(dy_ref, a_ref, g_ref, w_ref, b_ref, du_ref, db_ref, dw_ref, da_buf, dg_buf, sems):
        i = pl.program_id(0)
        slot = i % 2

        def writes(step, s):
            col = pl.multiple_of(step * TC, TC)
            return (pltpu.make_async_copy(da_buf.at[s], du_ref.at[:, pl.ds(col, TC)], sems.at[0, s]),
                    pltpu.make_async_copy(dg_buf.at[s], du_ref.at[:, pl.ds(DFF + col, TC)], sems.at[1, s]))

        @pl.when(i >= 2)
        def _():
            for cp in writes(i - 2, slot):
                cp.wait()

        a = a_ref[...].astype(F32)
        dyv = dy_ref[...].astype(F32)
        pre = _conv_pre(a, w_ref, b_ref)
        sg = _sigmoid(pre)
        dg_buf[slot] = (dyv * pre * sg).astype(BF16)
        dpre = dyv * g_ref[...].astype(F32) * (sg * (1.0 + pre * (1.0 - sg)))
        da = w_ref[2:3, :] * dpre + w_ref[1:2, :] * _shift_rows(dpre, -1) + w_ref[0:1, :] * _shift_rows(dpre, -2)
        da_buf[slot] = da.astype(BF16)
        for cp in writes(i, slot):
            cp.start()
        db_ref[...] = jnp.sum(dpre, axis=0, keepdims=True)
        dw_ref[0:1, :] = jnp.sum(dpre * _shift_rows(a, 2), axis=0, keepdims=True)
        dw_ref[1:2, :] = jnp.sum(dpre * _shift_rows(a, 1), axis=0, keepdims=True)
        dw_ref[2:3, :] = jnp.sum(dpre * a, axis=0, keepdims=True)

        @pl.when(i == NCT - 1)
        def _():
            for cp in writes(i - 1, 1 - slot) + writes(i, slot):
                cp.wait()

    col = pl.BlockSpec((S, TC), lambda i: (0, i))
    return _pcall(
        body, [dy, u, u, conv_w, conv_b], phase, name="conv_gate_bwd",
        out_shape=(jax.ShapeDtypeStruct((S, 2 * DFF), BF16), jax.ShapeDtypeStruct((1, DFF), F32),
                   jax.ShapeDtypeStruct((3, DFF), F32)),
        grid=(NCT,),
        in_specs=[col, col, pl.BlockSpec((S, TC), lambda i: (0, NCT + i)),
                  pl.BlockSpec((3, TC), lambda i: (0, i)), pl.BlockSpec((1, TC), lambda i: (0, i))],
        out_specs=(pl.BlockSpec(memory_space=pl.ANY), pl.BlockSpec((1, TC), lambda i: (0, i)),
                   pl.BlockSpec((3, TC), lambda i: (0, i))),
        scratch_shapes=[pltpu.VMEM((2, S, TC), BF16), pltpu.VMEM((2, S, TC), BF16), pltpu.SemaphoreType.DMA((2, 2))],
        compiler_params=_params(1),
    )


def _out_loss(z, x1, target, gate):
    def body(z_ref, x_ref, t_ref, g_ref, do_ref, dz_ref, dg_ref, loss_ref):
        i = pl.program_id(0)
        zv = z_ref[...]
        gv = g_ref[...]
        err = x_ref[...] + gv * zv - t_ref[...]
        dout = err * (1.0 / D)
        do_ref[...] = dout
        dz_ref[...] = (dout * gv).astype(BF16)

        @pl.when(i == 0)
        def _():
            dg_ref[...] = jnp.zeros_like(dg_ref)
            loss_ref[...] = jnp.zeros_like(loss_ref)

        dg_ref[...] += jnp.sum(dout * zv, axis=0, keepdims=True)
        part = jnp.sum(jnp.sum(err * err, axis=0, keepdims=True), axis=-1, keepdims=True) * (0.5 / D)
        lane = lax.broadcasted_iota(jnp.int32, (1, 128), 1)
        loss_ref[...] += jnp.where(lane == 0, part, 0.0)

    return pl.pallas_call(
        body, name="out_loss",
        out_shape=(jax.ShapeDtypeStruct((S, D), F32), jax.ShapeDtypeStruct((S, D), BF16),
                   jax.ShapeDtypeStruct((1, D), F32), jax.ShapeDtypeStruct((1, 128), F32)),
        grid=(S // TR,),
        in_specs=[_row_spec(), _row_spec(), _row_spec(), _vec_spec()],
        out_specs=(_row_spec(), _row_spec(), _vec_spec(), _vec_spec(128)),
        compiler_params=_params(1),
    )(z, x1, target, gate)


ADA_COLS = 6 * D // N_CHIPS
TA = 512


def _ada_fwd(c_all, w_shard, b_shard):
    def body(c_ref, w_ref, b_ref, o_ref):
        cv = c_ref[...]
        o_ref[...] = _dot_f32(cv * _sigmoid(cv), w_ref[...]) + b_ref[...]

    return pl.pallas_call(
        body, name="ada_fwd", out_shape=jax.ShapeDtypeStruct((N_DEV, ADA_COLS), F32), grid=(ADA_COLS // TA,),
        in_specs=[pl.BlockSpec((N_DEV, D), lambda j: (0, 0)), pl.BlockSpec((D, TA), lambda j: (0, j)),
                  pl.BlockSpec((1, TA), lambda j: (0, j))],
        out_specs=pl.BlockSpec((N_DEV, TA), lambda j: (0, j)),
        compiler_params=_params(1),
    )(c_all, w_shard, b_shard)


ADA_ROWS = 128


def _ada_bwd_adamw(c_all, dmod_shard, w, m, v):
    def body(c_ref, d_ref, w_ref, m_ref, v_ref, g_ref, dl_ref, mo_ref, vo_ref):
        cv = c_ref[...]
        gv = lax.dot_general(cv * _sigmoid(cv), d_ref[...], TN, precision=lax.Precision.HIGHEST,
                             preferred_element_type=F32)
        g_ref[...] = gv
        m2 = ADAM_B1 * m_ref[...] + (1.0 - ADAM_B1) * gv
        v2 = ADAM_B2 * v_ref[...] + (1.0 - ADAM_B2) * (gv * gv)
        m_hat = m2 / (1.0 - ADAM_B1 ** ADAM_STEP)
        v_hat = v2 / (1.0 - ADAM_B2 ** ADAM_STEP)
        dl_ref[...] = -ADAM_LR * (m_hat / (jnp.sqrt(v_hat) + ADAM_EPS) + ADAM_WD * w_ref[...])
        mo_ref[...] = m2
        vo_ref[...] = v2

    spec = pl.BlockSpec((ADA_ROWS, ADA_COLS), lambda i: (i, 0))
    out = jax.ShapeDtypeStruct((D, ADA_COLS), F32)
    return pl.pallas_call(
        body, name="ada_bwd_adamw", out_shape=(out,) * 4, grid=(D // ADA_ROWS,),
        in_specs=[pl.BlockSpec((N_DEV, ADA_ROWS), lambda i: (0, i)), pl.BlockSpec((N_DEV, ADA_COLS), lambda i: (0, 0)),
                  spec, spec, spec],
        out_specs=(spec,) * 4,
        compiler_params=_params(1),
    )(c_all, dmod_shard, w, m, v)


def _sum_rows(g, name):
    rows, n = g.shape

    def body(g_ref, o_ref):
        acc = g_ref[0:1, :]
        for i in range(1, rows):
            acc = acc + g_ref[i:i + 1, :]
        o_ref[...] = acc

    return pl.pallas_call(
        body, name=name, out_shape=jax.ShapeDtypeStruct((1, n), F32),
        in_specs=[VMEM_SPEC], out_specs=VMEM_SPEC,
    )(g)


def _lb_grad(lb_logits, dlb):
    def body(l_ref, d_ref, o_ref):
        p = 1.0 / (1.0 + jnp.exp(l_ref[1:2, :] - l_ref[0:1, :]))
        t = d_ref[...] * p * (1.0 - p)
        o_ref[0:1, :] = t
        o_ref[1:2, :] = -t

    return pl.pallas_call(
        body, name="lb_grad", out_shape=jax.ShapeDtypeStruct((2, HW), F32),
        in_specs=[VMEM_SPEC, VMEM_SPEC], out_specs=VMEM_SPEC,
    )(lb_logits, dlb)


def _adamw(w, g, m, v, name, copy_grad=False):
    rows, cols = w.shape
    tr = rows
    if rows * cols * 4 > ADAM_BLOCK_BYTES:
        tr = _pick(rows, [t for t in (256, 128, 64, 32, 16, 8) if t * cols * 4 <= ADAM_BLOCK_BYTES])

    def body(w_ref, g_ref, m_ref, v_ref, d_ref, mo_ref, vo_ref, *go_ref):
        gv = g_ref[...]
        if copy_grad:
            go_ref[0][...] = gv
        m2 = ADAM_B1 * m_ref[...] + (1.0 - ADAM_B1) * gv
        v2 = ADAM_B2 * v_ref[...] + (1.0 - ADAM_B2) * (gv * gv)
        m_hat = m2 / (1.0 - ADAM_B1 ** ADAM_STEP)
        v_hat = v2 / (1.0 - ADAM_B2 ** ADAM_STEP)
        d_ref[...] = -ADAM_LR * (m_hat / (jnp.sqrt(v_hat) + ADAM_EPS) + ADAM_WD * w_ref[...])
        mo_ref[...] = m2
        vo_ref[...] = v2

    spec = pl.BlockSpec((tr, cols), lambda i: (i, 0))
    out = jax.ShapeDtypeStruct((rows, cols), F32)
    n_out = 4 if copy_grad else 3
    return pl.pallas_call(
        body, name=name, out_shape=(out,) * n_out, grid=(rows // tr,),
        in_specs=[spec] * 4, out_specs=(spec,) * n_out,
        compiler_params=_params(1),
    )(w, g, m, v)


SC_TILES = 32
SC_LANES = 16
SC_COL_PARTS = 2
SC_CHUNK_ROWS = 8


def _adamw_sc(w, g, m, v, name):
    rows, cols = w.shape
    band, part = rows // (SC_TILES // SC_COL_PARTS), cols // SC_COL_PARTS
    assert band % SC_CHUNK_ROWS == 0 and part % SC_LANES == 0, (rows, cols)

    def body(w_hbm, g_hbm, m_hbm, v_hbm, d_hbm, mo_hbm, vo_hbm, go_hbm, wb, gb, mb, vb, db):
        tile = lax.axis_index("sc_subcore") * 2 + lax.axis_index("sc_core")
        row0 = (tile // SC_COL_PARTS) * band
        col0 = (tile % SC_COL_PARTS) * part

        @pl.loop(0, band, step=SC_CHUNK_ROWS)
        def _(r):
            at = lambda ref: ref.at[pl.ds(row0 + r, SC_CHUNK_ROWS), pl.ds(col0, part)]
            pltpu.sync_copy(at(w_hbm), wb)
            pltpu.sync_copy(at(g_hbm), gb)
            pltpu.sync_copy(gb, at(go_hbm))
            pltpu.sync_copy(at(m_hbm), mb)
            pltpu.sync_copy(at(v_hbm), vb)

            @pl.loop(0, SC_CHUNK_ROWS)
            def _(i):
                @pl.loop(0, part, step=SC_LANES)
                def _(j):
                    sl = (i, pl.ds(j, SC_LANES))
                    gv = gb[sl]
                    m2 = ADAM_B1 * mb[sl] + (1.0 - ADAM_B1) * gv
                    v2 = ADAM_B2 * vb[sl] + (1.0 - ADAM_B2) * (gv * gv)
                    m_hat = m2 / (1.0 - ADAM_B1 ** ADAM_STEP)
                    v_hat = v2 / (1.0 - ADAM_B2 ** ADAM_STEP)
                    db[sl] = -ADAM_LR * (m_hat / (jnp.sqrt(v_hat) + ADAM_EPS) + ADAM_WD * wb[sl])
                    mb[sl] = m2
                    vb[sl] = v2

            pltpu.sync_copy(db, at(d_hbm))
            pltpu.sync_copy(mb, at(mo_hbm))
            pltpu.sync_copy(vb, at(vo_hbm))

    out = jax.ShapeDtypeStruct((rows, cols), F32)
    buf = pltpu.VMEM((SC_CHUNK_ROWS, part), F32)
    return pl.kernel(
        body, name=name, out_type=(out, out, out, out),
        mesh=plsc.VectorSubcoreMesh(core_axis_name="sc_core", subcore_axis_name="sc_subcore"),
        scratch_types=[buf] * 5,
    )(w, g, m, v)


W_IN, W_OUT, W_UP, W_DOWN = range(4)
IN_CHUNKS = 4
W_INQ = 4


def _sequence_step(x, target, mod, norm1_w, lb_logits, hg_norm_w, q_norm_w, k_norm_w, norm2_w, conv_w, conv_b, net):
    shift1, scale1, gate1, shift2, scale2, gate2 = [mod[:, i * D:(i + 1) * D] for i in range(6)]

    def run(name, fn, *args, **kw):
        phase = net.host(name)
        if phase is None:
            return fn(*args, **kw)
        res, comm = fn(*args, phase=phase, **kw)
        net.done(name, comm, res[0] if isinstance(res, tuple) else res)
        return res

    h = _norm_mod(x, norm1_w, scale1, shift1, "norm1_fwd")
    proj = run("proj_fwd", _matmul, h, net.full[W_IN], "nn", F32, "proj_fwd")
    cat, o_raw, states = run("hgrn_fwd", _hgrn_fwd, proj, lb_logits, hg_norm_w)
    qn, kn = _qk_prep(proj, q_norm_w, k_norm_w)
    att = [run(f"attn_fwd_d{dil}", _attn_fwd, proj, qn, kn, dil) for dil in DILS]
    cat, b_out_f32, lse = _attn_merge([t[0] for t in att], [t[1] for t in att], cat)
    mix = run("mix_fwd", _matmul, cat, net.full[W_OUT], "nn", F32, "mix_fwd")
    x1, h2 = _resid_norm_mod(x, mix, gate1, norm2_w, scale2, shift2, "norm2_fwd")
    u = run("up_fwd", _matmul, h2, net.full[W_UP], "nn", BF16, "up_fwd")
    yact = run("conv_gate_fwd_a", _conv_gate_fwd, u, conv_w, conv_b, "conv_gate_fwd_a", tiles=(0, NCT // 2 + 1))
    yact = run("conv_gate_fwd_b", _conv_gate_fwd, u, conv_w, conv_b, "conv_gate_fwd_b", tiles=(NCT // 2 + 1, NCT),
               into=yact)
    z =_matmul(yact, net.full[W_DOWN], "nn", F32, "down_fwd")
    dout, dz, dgate2, loss_row = _out_loss(z, x1, target, gate2)

    net.grad[W_DOWN] = _matmul(yact, dz, "tn", BF16, "down_bwd_w")
    dyact = run("down_bwd_x", _matmul, dz, net.full[W_DOWN], "nt", BF16, "down_bwd_x")
    du, dconv_b, dconv_w = run("conv_gate_bwd", _conv_gate_bwd, dyact, u, conv_w, conv_b)
    net.grad[W_UP] = run("up_bwd_w", _matmul, h2, du, "tn", BF16, "up_bwd_w")
    dh2 = run("up_bwd_x", _matmul, du, net.full[W_UP], "nt", F32, "up_bwd_x")
    dx1, dshift2, dscale2, dnorm2, dgate1, dmix = run(
        "norm2_bwd", _norm_mod_bwd, dh2, x1, norm2_w, scale2, dout, "norm2_bwd", mix=mix, gate=gate1)
    net.grad[W_OUT] = run("mix_bwd_w", _matmul, cat, dmix, "tn", BF16, "mix_bwd_w")
    dcat = run("mix_bwd_x", _matmul, dmix, net.full[W_OUT], "nt", F32, "mix_bwd_x")
    dhq, dhf, dhi, dhg, dlb, dhg_norm = run("hgrn_bwd", _hgrn_bwd, proj, lb_logits, hg_norm_w, o_raw, states, dcat)
    delta = _attn_delta(dcat, b_out_f32)
    acc = None
    for dil in DILS:
        acc = run(f"attn_bwd_d{dil}", _attn_bwd, proj, qn, kn, lse, delta, dcat, dil, acc)
    daq, dak, dav, dq_norm, dk_norm = _qk_norm_bwd(proj, *acc, q_norm_w, k_norm_w)
    dproj = jnp.concatenate([dhq, dhf, dhi, dhg, daq, dak, dav], axis=1)
    for q in range(IN_CHUNKS):
        net.grad[W_INQ + q] = run(f"proj_bwd_w_{q}", _matmul, h, dproj, "tn", BF16, f"proj_bwd_w_{q}",
                                  m_blocks=(D // IN_CHUNKS, [q]))
    dh = run("proj_bwd_x", _matmul, dproj, net.full[W_IN], "nt", F32, "proj_bwd_x")
    grad_x, dshift1, dscale1, dnorm1 = run("norm1_bwd", _norm_mod_bwd, dh, x, norm1_w, scale1, dx1, "norm1_bwd")

    dmod = jnp.concatenate([dshift1, dscale1, dgate1, dshift2, dscale2, dgate2], axis=1)
    small = dict(norm1_w=dnorm1, lb=dlb, hg_norm_w=dhg_norm, q_norm_w=dq_norm, k_norm_w=dk_norm,
                 norm2_w=dnorm2, conv_b=dconv_b, conv_w=dconv_w)
    return loss_row, grad_x, dmod, small


def _place():
    x, y, c = lax.axis_index("x"), lax.axis_index("y"), lax.axis_index("c")
    others = [(1 - x, y), (x, 1 - y), (1 - x, 1 - y)]
    return x, y, c, others


def _remote(src, dst, send_sems, recv_sems, k, to):
    return pltpu.make_async_remote_copy(src_ref=src, dst_ref=dst, send_sem=send_sems.at[k], recv_sem=recv_sems.at[k],
                                        device_id=to, device_id_type=MESH)


class _Phase:
    def __init__(self):
        self.ins, self.outs, self.aliases, self.groups, self.n = [], [], {}, [], 0

    def add(self, ins, outs, aliases, n, copies):
        i0, o0 = len(self.ins), len(self.outs)
        self.ins += list(ins)
        self.outs += list(outs)
        self.aliases.update({i0 + i: o0 + o for i, o in aliases.items()})
        self.groups.append((slice(i0, i0 + len(ins)), slice(o0, o0 + len(outs)), copies))
        self.n += n
        return slice(o0, o0 + len(outs))

    def build(self, cin, cout, send_sems, recv_sems, landing):
        res = []
        for si, so, copies in self.groups:
            for src, dst, land, peer in copies(cin[si], cout[so]):
                k = len(res)
                res.append(_remote(land, land, send_sems, recv_sems, k, peer) if landing
                           else _remote(src, dst, send_sems, recv_sems, k, peer))
        assert len(res) == self.n
        return res


def _pcall(body, args, phase=None, **kw):
    if phase is None or not phase.groups:
        res = pl.pallas_call(body, **kw)(*args)
        return res if phase is None else (res, ())
    grid = tuple(kw.pop("grid", ()))
    out_shape, out_specs = kw.pop("out_shape"), kw.pop("out_specs")
    single = not isinstance(out_shape, (tuple, list))
    out_shape = [out_shape] if single else list(out_shape)
    out_specs = [out_specs] if single else list(out_specs)
    scratch = list(kw.pop("scratch_shapes", ()))
    aliases = dict(kw.pop("input_output_aliases", {}))
    n_in, n_out, n_ci, n_co = len(args), len(out_shape), len(phase.ins), len(phase.outs)
    aliases.update({n_in + i: n_out + o for i, o in phase.aliases.items()})

    def hosted(*refs):
        ins, cin = refs[:n_in], refs[n_in:n_in + n_ci]
        outs = refs[n_in + n_ci:n_in + n_ci + n_out]
        cout = refs[n_in + n_ci + n_out:n_in + n_ci + n_out + n_co]
        scr, send_sems, recv_sems = refs[n_in + n_ci + n_out + n_co:-2], refs[-2], refs[-1]
        ids = [pl.program_id(a) for a in range(len(grid))]

        def start():
            for cp in phase.build(cin, cout, send_sems, recv_sems, False):
                cp.start()

        def finish():
            for cp in phase.build(cin, cout, send_sems, recv_sems, False):
                cp.wait_send()
            for cp in phase.build(cin, cout, send_sems, recv_sems, True):
                cp.wait_recv()

        if grid:
            first = functools.reduce(jnp.logical_and, [i == 0 for i in ids])
            last = functools.reduce(jnp.logical_and, [i == g - 1 for i, g in zip(ids, grid)])
            pl.when(first)(start)
            body(*ins, *outs, *scr)
            pl.when(last)(finish)
        else:
            start()
            body(*ins, *outs, *scr)
            finish()

    res = pl.pallas_call(
        hosted, out_shape=tuple(out_shape + phase.outs), grid=grid,
        in_specs=list(kw.pop("in_specs")) + [HBM_SPEC] * n_ci, out_specs=tuple(out_specs + [HBM_SPEC] * n_co),
        input_output_aliases=aliases,
        scratch_shapes=scratch + [pltpu.SemaphoreType.DMA((phase.n,)), pltpu.SemaphoreType.DMA((phase.n,))],
        **kw,
    )(*args, *phase.ins)
    outs = res[:n_out]
    return (outs[0] if single else tuple(outs)), tuple(res[n_out:])


def _comm_only(name, phase):
    return _pcall(lambda: None, [], phase, name=name, out_shape=(), in_specs=[], out_specs=())[1]


def _all_gather_rows(block, name, phase=None):
    m_per, n = block.shape

    def body(x_ref, out_ref, send_sems, recv_sems, local_sem):
        x, y, c, chips = _place()
        me, sibling = (x, y, c), (x, y, 1 - c)

        def rows(px, py, pc):
            return out_ref.at[pl.ds((4 * px + 2 * py + pc) * m_per, m_per), :]

        def copy(k, blk, to, src=None):
            return _remote(rows(*blk) if src is None else src, rows(*blk), send_sems, recv_sems, k, to)

        mine = pltpu.make_async_copy(x_ref, rows(*me), local_sem)
        mine.start()
        first = [copy(0, me, sibling, src=x_ref)]
        first += [copy(1 + j, me, (*chip, c), src=x_ref) for j, chip in enumerate(chips)]
        for cp in first:
            cp.start()
        passed = [copy(4 + j, (*chip, c), sibling) for j, chip in enumerate(chips)]
        for j, chip in enumerate(chips):
            copy(1 + j, (*chip, c), me).wait_recv()
            passed[j].start()
        copy(0, sibling, me).wait_recv()
        for j, chip in enumerate(chips):
            copy(4 + j, (*chip, 1 - c), me).wait_recv()
        for cp in first + passed:
            cp.wait_send()
        mine.wait()

    return _pcall(
        body, [block], phase, name=name, out_shape=jax.ShapeDtypeStruct((N_DEV * m_per, n), block.dtype),
        in_specs=[VMEM_SPEC], out_specs=VMEM_SPEC,
        scratch_shapes=[pltpu.SemaphoreType.DMA((7,)), pltpu.SemaphoreType.DMA((7,)), pltpu.SemaphoreType.DMA],
    )


class _Geo:
    def __init__(self, kind, shard_shape):
        self.kind = kind
        self.shard_shape = tuple(shard_shape)
        self.hr, self.hc = shard_shape[0] // 2, shard_shape[1]
        if kind == "col":
            self.full_shape = (shard_shape[0], N_CHIPS * shard_shape[1])
        else:
            self.full_shape = (N_CHIPS * shard_shape[0], shard_shape[1])

    def full_half(self, ref, j, c):
        return self.piece(ref, j, c, 0, 1, 1)

    def piece(self, ref, j, c, p0, p1, pieces, part=None):
        pr = self.hr // pieces
        off, n = p0 * pr, (p1 - p0) * pr
        if part is not None:
            top = (n // 32) * 16
            off, n = (off, top) if part == 0 else (off + top, n - top)
        if self.kind == "col":
            return ref.at[pl.ds(pl.multiple_of(c * self.hr + off, 16), n),
                          pl.ds(pl.multiple_of(j * self.hc, 128), self.hc)]
        return ref.at[pl.ds(pl.multiple_of((2 * j + c) * self.hr + off, 16), n), :]


WEIGHT_KINDS = ("col", "row", "col", "row")
NW = len(WEIGHT_KINDS)


def _comm_call(body, name, ins, outs, n_remote, aliases=None):
    return pl.pallas_call(
        body, name=name, out_shape=tuple(outs),
        in_specs=[HBM_SPEC] * len(ins), out_specs=tuple([HBM_SPEC] * len(outs)),
        input_output_aliases=aliases or {},
        scratch_shapes=[pltpu.SemaphoreType.DMA((n_remote,)), pltpu.SemaphoreType.DMA((n_remote,))],
    )(*ins)


ROW_TILES = (256, 176, 128, 64, 32, 16)


def _cast_into_full(shard, geo, place, name):
    rs, cs = shard.shape
    tr = _pick(rs, ROW_TILES)
    nb = rs // tr

    def body(place_ref, s_ref, o_ref):
        o_ref[...] = s_ref[...].astype(BF16)

    if geo.kind == "col":
        out_map = lambda i, place_ref: (i, place_ref[0])
    else:
        out_map = lambda i, place_ref: (place_ref[0] * nb + i, 0)
    return pl.pallas_call(
        body, name=name, out_shape=jax.ShapeDtypeStruct(geo.full_shape, BF16),
        grid_spec=pltpu.PrefetchScalarGridSpec(
            num_scalar_prefetch=1, grid=(nb,),
            in_specs=[pl.BlockSpec((tr, cs), lambda i, place_ref: (i, 0))],
            out_specs=pl.BlockSpec((tr, cs), out_map)),
        compiler_params=_params(1),
    )(place, shard)


def _gather_weight(full, geo, pieces, name):
    per = 8

    def body(in_ref, ref, send_sems, recv_sems):
        x, y, c, _ = _place()
        j, jx, jy, jo = 2 * x + y, 2 * (1 - x) + y, 2 * x + (1 - y), 2 * (1 - x) + (1 - y)
        nx, ny, sib = (1 - x, y, c), (x, 1 - y, c), (x, y, 1 - c)

        def cp(region, k, to):
            return _remote(region, region, send_sems, recv_sems, k, to)

        def reg(chip, p, part=None, core=c):
            return geo.piece(ref, chip, core, p, p + 1, pieces, part)

        sent = []
        for p in range(pieces):
            sent += [cp(reg(j, p), per * p, nx), cp(reg(j, p), per * p + 1, ny)]
        for d in sent:
            d.start()
        for p in range(pieces):
            cp(reg(jx, p), per * p, nx).wait_recv()
            new = [cp(reg(jx, p, 0), per * p + 2, ny), cp(reg(jx, p), per * p + 4, sib)]
            cp(reg(jy, p), per * p + 1, ny).wait_recv()
            new += [cp(reg(jy, p, 1), per * p + 3, nx), cp(reg(jy, p), per * p + 5, sib)]
            for d in new:
                d.start()
            sent += new
        for p in range(pieces):
            cp(reg(jo, p, 0), per * p + 2, ny).wait_recv()
            cp(reg(jo, p, 1), per * p + 3, nx).wait_recv()
            new = [cp(reg(jo, p, 0), per * p + 6, sib), cp(reg(jo, p, 1), per * p + 7, sib)]
            for d in new:
                d.start()
            sent += new
        for p in range(pieces):
            cp(reg(jx, p, None, 1 - c), per * p + 4, sib).wait_recv()
            cp(reg(jy, p, None, 1 - c), per * p + 5, sib).wait_recv()
            cp(reg(jo, p, 0, 1 - c), per * p + 6, sib).wait_recv()
            cp(reg(jo, p, 1, 1 - c), per * p + 7, sib).wait_recv()
        for d in sent:
            d.wait_send()

    return _comm_call(body, name, [full], [_same(full)], per * pieces, aliases={0: 0})[0]


def _same(a):
    return jax.ShapeDtypeStruct(a.shape, a.dtype)


def _gather_ici(full, geo, p0, p1, pieces):
    def copies(ins, outs):
        x, y, c, _ = _place()
        mine = geo.piece(outs[0], 2 * x + y, c, p0, p1, pieces)
        return [(mine, mine, geo.piece(outs[0], 2 * ox + oy, c, p0, p1, pieces), (ox, oy, c))
                for ox, oy in ((1 - x, y), (x, 1 - y))]

    return dict(ins=[full], outs=[_same(full)], aliases={0: 0}, n=2, copies=copies)


def _gather_relay(full, geo, p0, p1, pieces):
    def copies(ins, outs):
        x, y, c, _ = _place()
        jx, jy, jo = 2 * (1 - x) + y, 2 * x + (1 - y), 2 * (1 - x) + (1 - y)
        reg = lambda chip, part: geo.piece(outs[0], chip, c, p0, p1, pieces, part)
        return [(reg(jx, 0), reg(jx, 0), reg(jo, 0), (x, 1 - y, c)), (reg(jy, 1), reg(jy, 1), reg(jo, 1), (1 - x, y, c))]

    return dict(ins=[full], outs=[_same(full)], aliases={0: 0}, n=2, copies=copies)


def _gather_d2d(full, geo):
    def copies(ins, outs):
        x, y, c, chips = _place()
        return [(geo.full_half(outs[0], 2 * ox + oy, c), geo.full_half(outs[0], 2 * ox + oy, c),
                 geo.full_half(outs[0], 2 * ox + oy, 1 - c), (x, y, 1 - c)) for ox, oy in chips]

    return dict(ins=[full], outs=[_same(full)], aliases={0: 0}, n=3, copies=copies)


def _swap_d2d(grad, geo):
    def copies(ins, outs):
        x, y, c, _ = _place()
        return [(geo.full_half(ins[0], j, 1 - c), outs[0].at[j], outs[0].at[j], (x, y, 1 - c)) for j in range(N_CHIPS)]

    return dict(ins=[grad], outs=[jax.ShapeDtypeStruct((N_CHIPS, geo.hr, geo.hc), BF16)], aliases={}, n=N_CHIPS,
                copies=copies)


SEM_SPEC = pl.BlockSpec(memory_space=pltpu.SEMAPHORE)
SIDE_EFFECT = pltpu.SideEffectType.DATAFLOW_SIDE_EFFECTING


def _scatter_start(pair, geo, name):
    def body(pair_ref, land_ref, *rest):
        sems, token = rest[:6], rest[-1]
        x, y, c, chips = _place()
        for k, (ox, oy) in enumerate(chips):
            pltpu.make_async_remote_copy(src_ref=pair_ref.at[2 * ox + oy], dst_ref=land_ref.at[k], send_sem=sems[k],
                                         recv_sem=sems[3 + k], device_id=(ox, oy, c), device_id_type=MESH).start()
        token[...] = jnp.zeros_like(token)

    land = jax.ShapeDtypeStruct((3, geo.hr, geo.hc), BF16)
    return pl.pallas_call(
        body, name=name,
        out_shape=(pltpu.SemaphoreType.DMA(()),) * 6 + (pltpu.HBM(pair.shape, pair.dtype), pltpu.HBM(land.shape, land.dtype),
                                                       jax.ShapeDtypeStruct((8, 128), F32)),
        in_specs=(HBM_SPEC, HBM_SPEC), out_specs=(SEM_SPEC,) * 6 + (HBM_SPEC, HBM_SPEC, VMEM_SPEC),
        input_output_aliases={0: 6, 1: 7},
        compiler_params=pltpu.CompilerParams(has_side_effects=SIDE_EFFECT),
    )(pltpu.with_memory_space_constraint(pair, pltpu.HBM),
      pltpu.with_memory_space_constraint(lax.empty(land.shape, land.dtype), pltpu.HBM))


def _scatter_wait(started, after, name):
    sems, pair_thru, land_thru = started[:6], started[6], started[7]

    def body(pair_ref, land_ref, *rest):
        sems = rest[:6]
        x, y, c, chips = _place()
        for k, (ox, oy) in enumerate(chips):
            copy = pltpu.make_async_remote_copy(src_ref=pair_ref.at[2 * ox + oy], dst_ref=land_ref.at[k], send_sem=sems[k],
                                                recv_sem=sems[3 + k], device_id=(ox, oy, c), device_id_type=MESH)
            copy.wait_send()
            copy.wait_recv()

    return pl.pallas_call(
        body, name=name,
        out_shape=(pltpu.HBM(pair_thru.shape, pair_thru.dtype), pltpu.HBM(land_thru.shape, land_thru.dtype)),
        in_specs=(HBM_SPEC, HBM_SPEC) + (SEM_SPEC,) * 6 + (pl.BlockSpec(memory_space=pl.ANY),),
        out_specs=(HBM_SPEC, HBM_SPEC), input_output_aliases={0: 0, 1: 1},
        compiler_params=pltpu.CompilerParams(has_side_effects=SIDE_EFFECT),
    )(pair_thru, land_thru, *sems, after)


def _scatter_ici(pair, recv, geo, p0, p1, pieces):
    pr = geo.hr // pieces
    rows = pl.ds(p0 * pr, (p1 - p0) * pr)

    def copies(ins, outs):
        x, y, c, chips = _place()
        return [(ins[0].at[2 * ox + oy, rows, :], outs[0].at[k, rows, :], outs[0].at[k, rows, :], (ox, oy, c))
                for k, (ox, oy) in enumerate(chips)]

    out = jax.ShapeDtypeStruct((3, geo.hr, geo.hc), BF16)
    if recv is None:
        return dict(ins=[pair], outs=[out], aliases={}, n=3, copies=copies)
    return dict(ins=[pair, recv], outs=[out], aliases={1: 0}, n=3, copies=copies)


def _join_d2d(shard, geo, row0=0):
    def copies(ins, outs):
        x, y, c, _ = _place()
        mine = outs[0].at[pl.ds(pl.multiple_of(row0 + c * geo.hr, 8), geo.hr), :]
        other = outs[0].at[pl.ds(pl.multiple_of(row0 + (1 - c) * geo.hr, 8), geo.hr), :]
        return [(mine, mine, other, (x, y, 1 - c))]

    return dict(ins=[shard], outs=[_same(shard)], aliases={0: 0}, n=1, copies=copies)


def _add_pair(grad, got, geo, place, name):
    tr = _pick(geo.hr, ROW_TILES)
    nb = geo.hr // tr

    def body(place_ref, a_ref, b_ref, o_ref):
        o_ref[0] = (a_ref[...].astype(F32) + b_ref[0].astype(F32)).astype(BF16)

    if geo.kind == "col":
        own_map = lambda j, i, place_ref: (place_ref[1] * nb + i, j)
    else:
        own_map = lambda j, i, place_ref: ((2 * j + place_ref[1]) * nb + i, 0)
    spec = pl.BlockSpec((1, tr, geo.hc), lambda j, i, place_ref: (j, i, 0))
    return pl.pallas_call(
        body, name=name, out_shape=jax.ShapeDtypeStruct((N_CHIPS, geo.hr, geo.hc), BF16),
        grid_spec=pltpu.PrefetchScalarGridSpec(
            num_scalar_prefetch=1, grid=(N_CHIPS, nb),
            in_specs=[pl.BlockSpec((tr, geo.hc), own_map), spec], out_specs=spec),
        compiler_params=_params(2),
    )(place, grad, got)


def _add_four(pair, recv, geo, place, name, rows=None, row0=0, into=None):
    tr = _pick(geo.hr, ROW_TILES)
    nb = geo.hr // tr
    rows = 2 * geo.hr if rows is None else rows

    def body(place_ref, p_ref, r_ref, *rest):
        rest[-1][...] = ((p_ref[0].astype(F32) + r_ref[0].astype(F32)) + r_ref[1].astype(F32)) + r_ref[2].astype(F32)

    in_specs = [pl.BlockSpec((1, tr, geo.hc), lambda i, place_ref: (place_ref[0], i, 0)),
                pl.BlockSpec((3, tr, geo.hc), lambda i, place_ref: (0, i, 0))]
    args = [place, pair, recv]
    if into is not None:
        in_specs.append(pl.BlockSpec(memory_space=pl.ANY))
        args.append(into)
    return pl.pallas_call(
        body, name=name, out_shape=jax.ShapeDtypeStruct((rows, geo.hc), F32),
        grid_spec=pltpu.PrefetchScalarGridSpec(
            num_scalar_prefetch=1, grid=(nb,), in_specs=in_specs,
            out_specs=pl.BlockSpec((tr, geo.hc), lambda i, place_ref: (row0 // tr + place_ref[1] * nb + i, 0))),
        input_output_aliases={3: 0} if into is not None else {},
        compiler_params=_params(1),
    )(*args)


PIECES = (4, 1, 16, 8) + (1,) * IN_CHUNKS
SCHEDULE = {
    "proj_fwd": (("gather_ici", W_OUT, 0, 1), ("gather_ici", W_UP, 0, 6)),
    "hgrn_fwd": (("gather_relay", W_OUT, 0, 1), ("gather_relay", W_UP, 0, 6), ("gather_ici", W_UP, 6, 12)),
    "attn_fwd_d1": (("gather_relay", W_UP, 6, 12),),
    "attn_fwd_d4": (("gather_ici", W_UP, 12, 16), ("gather_d2d", W_OUT)),
    "attn_fwd_d16": (("gather_relay", W_UP, 12, 16), ("gather_ici", W_DOWN, 0, 2)),
    "mix_fwd": (("gather_d2d", W_UP), ("gather_ici", W_DOWN, 2, 4)),
    "up_fwd": (("gather_ici", W_DOWN, 4, 8), ("gather_relay", W_DOWN, 0, 4)),
    "conv_gate_fwd_a": (("gather_relay", W_DOWN, 4, 8),),
    "conv_gate_fwd_b": (("gather_d2d", W_DOWN),),
    "down_bwd_x": (("swap", W_DOWN),),
    "conv_gate_bwd": (("scatter", W_DOWN, 0, 4),),
    "up_bwd_w": (("scatter", W_DOWN, 4, 8),),
    "up_bwd_x": (("swap", W_UP),),
    "norm2_bwd": (("scatter", W_UP, 0, 1),),
    "mix_bwd_w": (("scatter", W_UP, 1, 2),),
    "mix_bwd_x": (("swap", W_OUT), ("scatter", W_UP, 2, 3)),
    "hgrn_bwd": (("scatter", W_UP, 3, 9), ("join", W_DOWN)),
    "attn_bwd_d1": (("scatter", W_UP, 9, 12),),
    "attn_bwd_d4": (("scatter", W_OUT, 0, 1),),
    "attn_bwd_d16": (("scatter", W_UP, 12, 16),),
    "proj_bwd_w_0": (("join", W_UP), ("join", W_OUT)),
    "proj_bwd_w_1": (("swap", W_INQ),),
    "proj_bwd_w_2": (("swap", W_INQ + 1),),
    "proj_bwd_w_3": (("swap", W_INQ + 2),),
    "proj_bwd_x": (("swap", W_INQ + 3), ("scatter", W_INQ + 1, 0, 1), ("scatter", W_INQ + 2, 0, 1)),
    "gather_stats": (("join", W_INQ), ("join", W_INQ + 1), ("join", W_INQ + 2)),
    "grad_in_join": (("join", W_INQ + 3),),
}


class _Net:
    def __init__(self, shards, place):
        self.place = place
        self.geos = [_Geo(k, s.shape) for k, s in zip(WEIGHT_KINDS, shards)]
        self.full = [_cast_into_full(s, g, place, f"cast_shard_{w}") for w, (s, g) in enumerate(zip(shards, self.geos))]
        self.full[W_IN] = _gather_weight(self.full[W_IN], self.geos[W_IN], PIECES[W_IN], "gather_w_in")
        rows, cols = shards[W_IN].shape
        self.geos += [_Geo("col", (rows // IN_CHUNKS, cols))] * IN_CHUNKS
        n = NW + IN_CHUNKS
        self.grad, self.pair, self.recv, self.shard = [None] * n, [None] * n, [None] * n, [None] * n
        self.in_rows, self.in_shard = rows, None
        self.when_joined, self.joined = {}, {}
        self.flying = None
        self.slots = []

    def _row0(self, w):
        return (w - W_INQ) * 2 * self.geos[w].hr

    def host(self, name):
        phase = _Phase()
        self.slots = []
        groups = {}
        for item in SCHEDULE.get(name, ()):
            kind, w = item[0], item[1]
            geo = self.geos[w]
            key = len(groups)
            if kind == "gather_ici":
                spec, key = _gather_ici(self.full[w], geo, item[2], item[3], PIECES[w]), ("full", w)
            elif kind == "gather_relay":
                spec, key = _gather_relay(self.full[w], geo, item[2], item[3], PIECES[w]), ("full", w)
            elif kind == "gather_d2d":
                spec, key = _gather_d2d(self.full[w], geo), ("full", w)
            elif kind == "swap":
                spec = _swap_d2d(self.grad[w], geo)
            elif kind == "scatter":
                spec, key = _scatter_ici(self.pair[w], self.recv[w], geo, item[2], item[3], PIECES[w]), ("recv", w)
            elif w >= W_INQ:
                spec, key = _join_d2d(self.in_shard, geo, self._row0(w)), "in_shard"
            else:
                spec = _join_d2d(self.shard[w], geo)
            groups.setdefault(key, []).append((item, spec))
        for group in groups.values():
            specs = [s for _, s in group]
            merged = dict(specs[0], n=sum(s["n"] for s in specs),
                          copies=lambda ins, outs, specs=specs: [t for s in specs for t in s["copies"](ins, outs)])
            self.slots.append(([item for item, _ in group], phase.add(**merged)))
        return phase

    def done(self, name, comm, result=None):
        for items, sl in sorted(self.slots, key=lambda s: s[0][0][0] == "scatter"):
            out = comm[sl][0]
            for item in items:
                kind, w = item[0], item[1]
                if kind in ("gather_ici", "gather_relay", "gather_d2d"):
                    self.full[w] = out
                elif kind == "swap":
                    self.pair[w] = _add_pair(self.grad[w], out, self.geos[w], self.place, f"grad_pair_sum_{w}")
                    if w == W_INQ:
                        self.flying = _scatter_start(self.pair[w], self.geos[w], "grad_in_scatter0_start")
                elif kind == "scatter":
                    self.recv[w] = out
                    if item[3] == PIECES[w] and w >= W_INQ:
                        self.in_shard = _add_four(self.pair[w], out, self.geos[w], self.place, f"grad_chip_sum_{w}",
                                                  rows=self.in_rows, row0=self._row0(w), into=self.in_shard)
                    elif item[3] == PIECES[w]:
                        self.shard[w] = _add_four(self.pair[w], out, self.geos[w], self.place, f"grad_chip_sum_{w}")
                elif w >= W_INQ:
                    self.in_shard = out
                else:
                    self.shard[w] = out
                    if w in self.when_joined:
                        self.joined[w] = self.when_joined[w](out)
        if name == f"proj_bwd_w_{IN_CHUNKS - 1}" and self.flying is not None:
            after = result if result is not None else self.grad[W_INQ + 1]
            pair, recv = _scatter_wait(self.flying, after, "grad_in_scatter0_wait")
            self.flying = None
            self.in_shard = _add_four(pair, recv, self.geos[W_INQ], self.place, f"grad_chip_sum_{W_INQ}",
                                      rows=self.in_rows, row0=0, into=self.in_shard)

    def alone(self, name):
        self.done(name, _comm_only(name, self.host(name)))


CONVW_SHARD = 3 * DFF // N_CHIPS
COND_PAD = 8 * 896
SMALL_SIZES = (("norm1_w", D), ("lb", HW), ("hg_norm_w", DH), ("q_norm_w", DH), ("k_norm_w", DH),
               ("norm2_w", D), ("conv_b", DFF), ("conv_w", 3 * DFF), ("loss", 128))
SMALL_TOTAL = sum(n for _, n in SMALL_SIZES)
STATS_PAD = 8 * 5120


def kernel(x, c, w_ada, b_ada, norm1_w, w_in, lb_logits, hg_norm_w, q_norm_w, k_norm_w, w_out, norm2_w, w_up, conv_w, conv_b, w_down, loss_target, m_w_ada, m_b_ada, m_norm1_w, m_w_in, m_lb_logits, m_hg_norm_w, m_q_norm_w, m_k_norm_w, m_w_out, m_norm2_w, m_w_up, m_conv_w, m_conv_b, m_w_down, v_w_ada, v_b_ada, v_norm1_w, v_w_in, v_lb_logits, v_hg_norm_w, v_q_norm_w, v_k_norm_w, v_w_out, v_norm2_w, v_w_up, v_conv_w, v_conv_b, v_w_down):
    chip = 2 * lax.axis_index("x") + lax.axis_index("y")
    dev = 2 * chip + lax.axis_index("c")

    cond = jnp.concatenate([c, conv_w[0].reshape(1, CONVW_SHARD), jnp.zeros((1, COND_PAD - D - CONVW_SHARD), F32)], axis=1)
    cond_all = _all_gather_rows(cond.reshape(8, COND_PAD // 8), "gather_cond").reshape(N_DEV, COND_PAD)
    c_all = cond_all[:, :D]
    conv_w_full = jnp.concatenate(
        [cond_all[2 * j, D:D + CONVW_SHARD].reshape(3, DFF // N_CHIPS) for j in range(N_CHIPS)], axis=1)

    b_shard = lax.dynamic_slice_in_dim(b_ada, chip * ADA_COLS, ADA_COLS, axis=1)
    mod_cols = _all_gather_rows(_ada_fwd(c_all, w_ada[0], b_shard), "gather_mod")
    mod_all = jnp.concatenate([mod_cols[16 * j:16 * j + 8] for j in range(N_CHIPS)], axis=1)
    mod = lax.dynamic_slice_in_dim(mod_all, dev, 1, axis=0)

    place = jnp.stack([chip, lax.axis_index("c")]).astype(jnp.int32)
    net = _Net([w_in[0], w_out[0], w_up[0], w_down[0]], place)
    net.when_joined = {
        W_OUT: lambda grad: _adamw_sc(w_out[0], grad, m_w_out[0], v_w_out[0], "adamw_sc_w_out"),
        W_DOWN: lambda grad: _adamw_sc(w_down[0], grad, m_w_down[0], v_w_down[0], "adamw_sc_w_down"),
        W_UP: lambda grad: _adamw_sc(w_up[0], grad, m_w_up[0], v_w_up[0], "adamw_sc_w_up"),
    }
    early = {W_OUT: "w_out", W_DOWN: "w_down", W_UP: "w_up"}
    loss_row, grad_x, dmod, small = _sequence_step(
        x[0], loss_target[0], mod, norm1_w, lb_logits, hg_norm_w, q_norm_w, k_norm_w, norm2_w, conv_w_full, conv_b, net)
    small["conv_w"] = small["conv_w"].reshape(1, 3 * DFF)
    small["loss"] = loss_row
    stats = jnp.concatenate([dmod] + [small[k] for k, _ in SMALL_SIZES]
                            + [jnp.zeros((1, STATS_PAD - 6 * D - SMALL_TOTAL), F32)], axis=1)
    stats_all, comm = _all_gather_rows(stats.reshape(8, STATS_PAD // 8), "gather_stats", net.host("gather_stats"))
    net.done("gather_stats", comm)
    stats_all = stats_all.reshape(N_DEV, STATS_PAD)
    last = W_INQ + IN_CHUNKS - 1
    started = _scatter_start(net.pair[last], net.geos[last], "grad_in_scatter_start")
    dmod_all = stats_all[:, :6 * D] + started[8][0, 0]
    sums = _sum_rows(stats_all[:, 6 * D:6 * D + SMALL_TOTAL], "small_grad_sum")
    g_small, off = {}, 0
    for k, n in SMALL_SIZES:
        g_small[k] = sums[:, off:off + n]
        off += n
    loss = g_small["loss"][0, 0]
    g_b_ada = _sum_rows(dmod_all, "b_ada_grad_sum")
    ada = _ada_bwd_adamw(c_all, lax.dynamic_slice_in_dim(dmod_all, chip * ADA_COLS, ADA_COLS, axis=1),
                         w_ada[0], m_w_ada[0], v_w_ada[0])
    g_w_ada = ada[0]
    pair_last, recv_last = _scatter_wait(started, ada[3], "grad_in_scatter_wait")
    net.in_shard = _add_four(pair_last, recv_last, net.geos[last], place, f"grad_chip_sum_{last}",
                             rows=net.in_rows, row0=net._row0(last), into=net.in_shard)
    net.alone("grad_in_join")
    g_out, g_up, g_down = net.shard[W_OUT], net.shard[W_UP], net.shard[W_DOWN]
    g_in = net.in_shard
    g_lb = _lb_grad(lb_logits, g_small["lb"])
    g_conv_w = lax.dynamic_slice_in_dim(g_small["conv_w"].reshape(3, DFF), chip * (DFF // N_CHIPS), DFF // N_CHIPS, axis=1)

    names = ["w_ada", "b_ada", "norm1_w", "w_in", "lb_logits", "hg_norm_w", "q_norm_w", "k_norm_w", "w_out",
             "norm2_w", "w_up", "conv_w", "conv_b", "w_down"]
    lead = {"w_ada", "w_in", "w_out", "w_up", "conv_w", "w_down"}
    w = dict(w_ada=w_ada, b_ada=b_ada, norm1_w=norm1_w, w_in=w_in, lb_logits=lb_logits, hg_norm_w=hg_norm_w,
             q_norm_w=q_norm_w, k_norm_w=k_norm_w, w_out=w_out, norm2_w=norm2_w, w_up=w_up, conv_w=conv_w,
             conv_b=conv_b, w_down=w_down)
    m = dict(w_ada=m_w_ada, b_ada=m_b_ada, norm1_w=m_norm1_w, w_in=m_w_in, lb_logits=m_lb_logits,
             hg_norm_w=m_hg_norm_w, q_norm_w=m_q_norm_w, k_norm_w=m_k_norm_w, w_out=m_w_out, norm2_w=m_norm2_w,
             w_up=m_w_up, conv_w=m_conv_w, conv_b=m_conv_b, w_down=m_w_down)
    v = dict(w_ada=v_w_ada, b_ada=v_b_ada, norm1_w=v_norm1_w, w_in=v_w_in, lb_logits=v_lb_logits,
             hg_norm_w=v_hg_norm_w, q_norm_w=v_q_norm_w, k_norm_w=v_k_norm_w, w_out=v_w_out, norm2_w=v_norm2_w,
             w_up=v_w_up, conv_w=v_conv_w, conv_b=v_conv_b, w_down=v_w_down)
    g = dict(w_ada=g_w_ada, b_ada=g_b_ada, norm1_w=g_small["norm1_w"], w_in=g_in, lb_logits=g_lb,
             hg_norm_w=g_small["hg_norm_w"], q_norm_w=g_small["q_norm_w"], k_norm_w=g_small["k_norm_w"], w_out=g_out,
             norm2_w=g_small["norm2_w"], w_up=g_up, conv_w=g_conv_w, conv_b=g_small["conv_b"], w_down=g_down)

    updated = {early[i]: res for i, res in net.joined.items()}
    updated["w_ada"] = (ada[1], ada[2], ada[3], ada[0])
    grads, deltas, new_m, new_v = [], [], [], []
    for n in names:
        strip = (lambda t: t[0]) if n in lead else (lambda t: t)
        wrap = (lambda t: t[None]) if n in lead else (lambda t: t)
        g_n = g[n]
        if n in updated:
            d_n, m_n, v_n, g_n = updated[n]
        elif n == "w_in":
            d_n, m_n, v_n, g_n = _adamw(strip(w[n]), g_n, strip(m[n]), strip(v[n]), f"adamw_{n}", copy_grad=True)
        else:
            d_n, m_n, v_n = _adamw(strip(w[n]), g_n, strip(m[n]), strip(v[n]), f"adamw_{n}")
        grads.append(wrap(g_n))
        deltas.append(wrap(d_n))
        new_m.append(wrap(m_n))
        new_v.append(wrap(v_n))
    return (loss, grad_x[None], *grads, *deltas, *new_m, *new_v)
```

```python
import functools

import jax
import jax.numpy as jnp
from jax import lax
from jax.experimental import pallas as pl
from jax.experimental.pallas import tpu as pltpu
from jax.experimental.pallas import tpu_sc as plsc

F32 = jnp.float32
BF16 = jnp.bfloat16

S = 2048
D = 2048
H = 8
DH = 128
HW = H * DH
CH = 64
NCH = S // CH
DFF = 5632
INC = 7 * HW
BLK = 128
DILS = (1, 4, 16)
EPS = 1e-6
NEG = -1e30
N_CHIPS = 4
N_DEV = 8

ADAM_LR = 0.001
ADAM_B1 = 0.9
ADAM_B2 = 0.999
ADAM_EPS = 1e-08
ADAM_WD = 0.01
ADAM_STEP = 10
ADAM_BLOCK_BYTES = 1 << 21

V7X_VMEM_BYTES = 64 * 1024 * 1024
VMEM_LIMIT = (V7X_VMEM_BYTES * 3) // 4
MESH = pl.DeviceIdType.MESH
HBM_SPEC = pl.BlockSpec(memory_space=pltpu.HBM)
VMEM_SPEC = pl.BlockSpec(memory_space=pltpu.VMEM)

NN = (((1,), (0,)), ((), ()))
NT = (((1,), (1,)), ((), ()))
TN = (((0,), (0,)), ((), ()))


def _params(n_grid):
    return pltpu.CompilerParams(dimension_semantics=("arbitrary",) * n_grid, vmem_limit_bytes=VMEM_LIMIT)


def _dot(a, b, dims=NN):
    return lax.dot_general(a.astype(BF16), b.astype(BF16), dims, preferred_element_type=F32)


def _dot_f32(a, b):
    return lax.dot_general(a, b, NN, precision=lax.Precision.HIGHEST, preferred_element_type=F32)


def _sigmoid(x):
    return 1.0 / (1.0 + jnp.exp(-x))


def _pick(n, cands):
    for t in cands:
        if n % t == 0:
            return t
    raise ValueError(n)


def _matmul(a, b, mode, out_dtype, name, phase=None, m_blocks=None):
    if mode == "nn":
        (m, k), (_, n) = a.shape, b.shape
    elif mode == "nt":
        (m, k), (n, _) = a.shape, b.shape
    else:
        (k, m), (_, n) = a.shape, b.shape
    tm = _pick(m, (1024, 1408, 512))
    a_block = lambda i: i
    if m_blocks is not None:
        tm, blocks = m_blocks
        m = tm * len(blocks)
        step = blocks[1] - blocks[0] if len(blocks) > 1 else 0
        assert all(blk == blocks[0] + step * i for i, blk in enumerate(blocks))
        a_block = lambda i: blocks[0] + step * i
    tn = _pick(n, (1024, 1408, 512))
    tk = _pick(k, (2048, 2816, 1792, 1024, 512))
    nk = k // tk
    dims = {"nn": NN, "nt": NT, "tn": TN}[mode]

    def body(a_ref, b_ref, o_ref, *acc):
        part = lax.dot_general(a_ref[...], b_ref[...], dims, preferred_element_type=F32)
        if nk == 1:
            o_ref[...] = part.astype(o_ref.dtype)
            return
        acc_ref, kk = acc[0], pl.program_id(2)

        @pl.when(kk == 0)
        def _():
            acc_ref[...] = part

        @pl.when(jnp.logical_and(kk > 0, kk < nk - 1))
        def _():
            acc_ref[...] += part

        @pl.when(kk == nk - 1)
        def _():
            o_ref[...] = (acc_ref[...] + part).astype(o_ref.dtype)

    if mode == "tn":
        a_spec = pl.BlockSpec((tk, tm), lambda i, j, kk: (kk, a_block(i)))
    else:
        a_spec = pl.BlockSpec((tm, tk), lambda i, j, kk: (i, kk))
    if mode == "nt":
        b_spec = pl.BlockSpec((tn, tk), lambda i, j, kk: (j, kk))
    else:
        b_spec = pl.BlockSpec((tk, tn), lambda i, j, kk: (kk, j))
    return _pcall(
        body, [a, b], phase, name=name,
        out_shape=jax.ShapeDtypeStruct((m, n), out_dtype),
        grid=(m // tm, n // tn, nk),
        in_specs=[a_spec, b_spec],
        out_specs=pl.BlockSpec((tm, tn), lambda i, j, kk: (i, j)),
        scratch_shapes=[pltpu.VMEM((tm, tn), F32)] if nk > 1 else [],
        compiler_params=_params(3),
    )


TR = 256


def _row_spec(cols=D):
    return pl.BlockSpec((TR, cols), lambda i: (i, 0))


def _vec_spec(cols=D):
    return pl.BlockSpec((1, cols), lambda i: (0, 0))


def _norm_mod(x, nw, scale, shift, name):
    def body(x_ref, nw_ref, sc_ref, sh_ref, h_ref):
        xv = x_ref[...]
        r = lax.rsqrt(jnp.mean(xv * xv, axis=-1, keepdims=True) + EPS)
        h_ref[...] = ((xv * r) * nw_ref[...] * (1.0 + sc_ref[...]) + sh_ref[...]).astype(BF16)

    return pl.pallas_call(
        body, name=name, out_shape=jax.ShapeDtypeStruct((S, D), BF16), grid=(S // TR,),
        in_specs=[_row_spec(), _vec_spec(), _vec_spec(), _vec_spec()], out_specs=_row_spec(),
        compiler_params=_params(1),
    )(x, nw, scale, shift)


def _resid_norm_mod(x, mix, gate, nw, scale, shift, name):
    def body(x_ref, mix_ref, g_ref, nw_ref, sc_ref, sh_ref, x1_ref, h_ref):
        xv = x_ref[...] + g_ref[...] * mix_ref[...]
        x1_ref[...] = xv
        r = lax.rsqrt(jnp.mean(xv * xv, axis=-1, keepdims=True) + EPS)
        h_ref[...] = ((xv * r) * nw_ref[...] * (1.0 + sc_ref[...]) + sh_ref[...]).astype(BF16)

    return pl.pallas_call(
        body, name=name,
        out_shape=(jax.ShapeDtypeStruct((S, D), F32), jax.ShapeDtypeStruct((S, D), BF16)), grid=(S // TR,),
        in_specs=[_row_spec(), _row_spec(), _vec_spec(), _vec_spec(), _vec_spec(), _vec_spec()],
        out_specs=(_row_spec(), _row_spec()),
        compiler_params=_params(1),
    )(x, mix, gate, nw, scale, shift)


def _norm_mod_bwd(dh, xin, nw, scale, dres, name, mix=None, gate=None, phase=None):
    with_gate = mix is not None

    def body(*refs):
        if with_gate:
            dh_ref, x_ref, nw_ref, sc_ref, dres_ref, mix_ref, g_ref, dx_ref, dsh_ref, dsc_ref, dnw_ref, dg_ref, dmix_ref = refs
        else:
            dh_ref, x_ref, nw_ref, sc_ref, dres_ref, dx_ref, dsh_ref, dsc_ref, dnw_ref = refs
        i = pl.program_id(0)
        dhv = dh_ref[...]
        xv = x_ref[...]
        r = lax.rsqrt(jnp.mean(xv * xv, axis=-1, keepdims=True) + EPS)
        xn = xv * r
        nwv = nw_ref[...]
        one_sc = 1.0 + sc_ref[...]
        dxn = dhv * nwv * one_sc
        dx = dres_ref[...] + r * (dxn - xn * jnp.mean(dxn * xn, axis=-1, keepdims=True))
        dx_ref[...] = dx

        @pl.when(i == 0)
        def _():
            dsh_ref[...] = jnp.zeros_like(dsh_ref)
            dsc_ref[...] = jnp.zeros_like(dsc_ref)
            dnw_ref[...] = jnp.zeros_like(dnw_ref)
            if with_gate:
                dg_ref[...] = jnp.zeros_like(dg_ref)

        dsh_ref[...] += jnp.sum(dhv, axis=0, keepdims=True)
        dsc_ref[...] += jnp.sum(dhv * xn * nwv, axis=0, keepdims=True)
        dnw_ref[...] += jnp.sum(dhv * xn * one_sc, axis=0, keepdims=True)
        if with_gate:
            dg_ref[...] += jnp.sum(dx * mix_ref[...], axis=0, keepdims=True)
            dmix_ref[...] = (dx * g_ref[...]).astype(BF16)

    vec = jax.ShapeDtypeStruct((1, D), F32)
    ins = [dh, xin, nw, scale, dres]
    in_specs = [_row_spec(), _row_spec(), _vec_spec(), _vec_spec(), _row_spec()]
    outs = [jax.ShapeDtypeStruct((S, D), F32), vec, vec, vec]
    out_specs = [_row_spec(), _vec_spec(), _vec_spec(), _vec_spec()]
    if with_gate:
        ins += [mix, gate]
        in_specs += [_row_spec(), _vec_spec()]
        outs += [vec, jax.ShapeDtypeStruct((S, D), BF16)]
        out_specs += [_vec_spec(), _row_spec()]
    return _pcall(
        body, ins, phase, name=name, out_shape=tuple(outs), grid=(S // TR,), in_specs=in_specs,
        out_specs=tuple(out_specs), compiler_params=_params(1),
    )


def _tri(lower):
    row = lax.broadcasted_iota(jnp.int32, (CH, CH), 0)
    col = lax.broadcasted_iota(jnp.int32, (CH, CH), 1)
    return (row >= col) if lower else (col >= row)


def _hgrn_gates(hq, hf, lb):
    sig = _sigmoid(hf)
    f = lb + (1.0 - lb) * sig
    g = jnp.log(f)
    sq = _sigmoid(hq)
    return sig, f, g, 1.0 - f, hq * sq, sq


HG_HP = 2
HG_UNROLL = 8


def _proj_col_spec(group):
    return pl.BlockSpec((S, HG_HP * DH), lambda h: (0, group * (H // HG_HP) + h))


def _hgrn_fwd(proj, lb_logits, norm_w, phase=None):
    def body(hq_ref, hf_ref, hi_ref, hg_ref, lbl_ref, nw_ref, out_ref, oraw_ref, st_ref, s_ref):
        nw = nw_ref[...]
        lower = _tri(True)
        ltri = lower.astype(F32)
        s_ref[...] = jnp.zeros_like(s_ref)

        def chunk(n, carry):
            rows = pl.ds(pl.multiple_of(n * CH, CH), CH)
            for j in range(HG_HP):
                cols = slice(j * DH, (j + 1) * DH)
                lb = 1.0 / (1.0 + jnp.exp(lbl_ref[1:2, cols] - lbl_ref[0:1, cols]))
                hq, hf, v, hg = hq_ref[rows, cols], hf_ref[rows, cols], hi_ref[rows, cols], hg_ref[rows, cols]
                _, _, g, kk, q, _ = _hgrn_gates(hq, hf, lb)
                gc = _dot_f32(ltri, g)
                gl = jnp.sum(g, axis=0, keepdims=True)
                qe = q * jnp.exp(gc)
                ke = kk * jnp.exp(gl - gc)
                qh = q * jnp.exp(gc - 0.5 * gl)
                kh = kk * jnp.exp(0.5 * gl - gc)
                st = s_ref[j]
                st_ref[j, pl.ds(n, 1)] = st[None]
                a = jnp.where(lower, _dot(qh, kh, NT), 0.0)
                o = _dot(qe, st, NT) + _dot(a, v)
                s_ref[j] = st * jnp.exp(gl) + _dot(v, ke, TN)
                oraw_ref[rows, cols] = o
                rs = lax.rsqrt(jnp.mean(o * o, axis=-1, keepdims=True) + EPS)
                out_ref[rows, cols] = (o * rs * nw * (hg * _sigmoid(hg))).astype(BF16)
            return carry

        lax.fori_loop(0, NCH, chunk, 0, unroll=HG_UNROLL)

    head_spec = pl.BlockSpec((S, HG_HP * DH), lambda h: (0, h))
    return _pcall(
        body, [proj, proj, proj, proj, lb_logits, norm_w], phase, name="hgrn_fwd",
        out_shape=(jax.ShapeDtypeStruct((S, 2 * HW), BF16), jax.ShapeDtypeStruct((S, HW), F32),
                   jax.ShapeDtypeStruct((H, NCH, DH, DH), F32)),
        grid=(H // HG_HP,),
        in_specs=[_proj_col_spec(0), _proj_col_spec(1), _proj_col_spec(2), _proj_col_spec(3),
                  pl.BlockSpec((2, HG_HP * DH), lambda h: (0, h)), pl.BlockSpec((1, DH), lambda h: (0, 0))],
        out_specs=(head_spec, head_spec, pl.BlockSpec((HG_HP, NCH, DH, DH), lambda h: (h, 0, 0, 0))),
        scratch_shapes=[pltpu.VMEM((HG_HP, DH, DH), F32)],
        compiler_params=_params(1),
    )


def _hgrn_bwd(proj, lb_logits, norm_w, oraw, states, dout, phase=None):
    def body(hq_ref, hf_ref, hi_ref, hg_ref, lbl_ref, nw_ref, oraw_ref, st_ref, do_ref,
             dhq_ref, dhf_ref, dhi_ref, dhg_ref, dlb_ref, dnw_ref, ds_ref, acc_ref):
        h = pl.program_id(0)
        nw = nw_ref[...]
        lower = _tri(True)
        ltri = lower.astype(F32)
        utri = _tri(False).astype(F32)
        last = lax.broadcasted_iota(jnp.int32, (CH, DH), 0) == CH - 1
        ds_ref[...] = jnp.zeros_like(ds_ref)
        acc_ref[...] = jnp.zeros_like(acc_ref)

        @pl.when(h == 0)
        def _():
            dnw_ref[...] = jnp.zeros_like(dnw_ref)

        def chunk(i, carry):
            n = NCH - 1 - i
            rows = pl.ds(pl.multiple_of(n * CH, CH), CH)
            for j in range(HG_HP):
                cols = slice(j * DH, (j + 1) * DH)
                lb = 1.0 / (1.0 + jnp.exp(lbl_ref[1:2, cols] - lbl_ref[0:1, cols]))
                hq, hf, v, hg = hq_ref[rows, cols], hf_ref[rows, cols], hi_ref[rows, cols], hg_ref[rows, cols]
                sig, f, g, kk, q, sq = _hgrn_gates(hq, hf, lb)
                gc = _dot_f32(ltri, g)
                gl = jnp.sum(g, axis=0, keepdims=True)
                eg = jnp.exp(gc)
                ek = jnp.exp(gl - gc)
                gam = jnp.exp(gl)
                ph = jnp.exp(gc - 0.5 * gl)
                pk = jnp.exp(0.5 * gl - gc)
                qe, ke = q * eg, kk * ek
                qh, kh = q * ph, kk * pk
                st = st_ref[j, pl.ds(n, 1)][0]
                dst = ds_ref[j]
                a = jnp.where(lower, _dot(qh, kh, NT), 0.0)
                o = oraw_ref[rows, cols]
                rs = lax.rsqrt(jnp.mean(o * o, axis=-1, keepdims=True) + EPS)
                xh = o * rs
                sg = _sigmoid(hg)
                dgo = do_ref[rows, cols]
                d_on = dgo * (hg * sg)
                dhg_ref[rows, cols] = (dgo * xh * nw * (sg * (1.0 + hg * (1.0 - sg)))).astype(BF16)
                acc_ref[j, 1:2, :] += jnp.sum(d_on * xh, axis=0, keepdims=True)
                dy = d_on * nw
                do = rs * (dy - xh * jnp.mean(dy * xh, axis=-1, keepdims=True))
                dqe = _dot(do, st)
                da = jnp.where(lower, _dot(do, v, NT), 0.0)
                dv = _dot(a, do, TN) + _dot(ke, dst, NT)
                dke = _dot(v, dst)
                dgam = jnp.sum(dst * st, axis=0, keepdims=True)
                dqh = lax.dot_general(da, kh, NN, precision=lax.Precision.HIGH, preferred_element_type=F32)
                dkh = lax.dot_general(da, qh, TN, precision=lax.Precision.HIGH, preferred_element_type=F32)
                dq = dqh * ph + dqe * eg
                dk = dkh * pk + dke * ek
                dgl = jnp.sum(dke * ke, axis=0, keepdims=True) + dgam * gam
                dgc = dqh * qh - dkh * kh + dqe * qe - dke * ke + jnp.where(last, dgl, 0.0)
                dg = _dot_f32(utri, dgc)
                df = dg / f - dk
                dhf_ref[rows, cols] = (df * (1.0 - lb) * sig * (1.0 - sig)).astype(BF16)
                acc_ref[j, 0:1, :] += jnp.sum(df * (1.0 - sig), axis=0, keepdims=True)
                dhq_ref[rows, cols] = (dq * (sq * (1.0 + hq * (1.0 - sq)))).astype(BF16)
                dhi_ref[rows, cols] = dv.astype(BF16)
                ds_ref[j] = dst * gam + _dot(do, qe, TN)
            return carry

        lax.fori_loop(0, NCH, chunk, 0, unroll=HG_UNROLL)
        for j in range(HG_HP):
            dlb_ref[:, j * DH:(j + 1) * DH] = acc_ref[j, 0:1, :]
            dnw_ref[...] += acc_ref[j, 1:2, :]

    head_spec = pl.BlockSpec((S, HG_HP * DH), lambda h: (0, h))
    head_out = jax.ShapeDtypeStruct((S, HW), BF16)
    return _pcall(
        body, [proj, proj, proj, proj, lb_logits, norm_w, oraw, states, dout], phase, name="hgrn_bwd",
        out_shape=(head_out, head_out, head_out, head_out,
                   jax.ShapeDtypeStruct((1, HW), F32), jax.ShapeDtypeStruct((1, DH), F32)),
        grid=(H // HG_HP,),
        in_specs=[_proj_col_spec(0), _proj_col_spec(1), _proj_col_spec(2), _proj_col_spec(3),
                  pl.BlockSpec((2, HG_HP * DH), lambda h: (0, h)), pl.BlockSpec((1, DH), lambda h: (0, 0)),
                  head_spec, pl.BlockSpec((HG_HP, NCH, DH, DH), lambda h: (h, 0, 0, 0)), head_spec],
        out_specs=(head_spec, head_spec, head_spec, head_spec,
                   pl.BlockSpec((1, HG_HP * DH), lambda h: (0, h)), pl.BlockSpec((1, DH), lambda h: (0, 0))),
        scratch_shapes=[pltpu.VMEM((HG_HP, DH, DH), F32), pltpu.VMEM((HG_HP, 8, DH), F32)],
        compiler_params=_params(1),
    )


ATT_SCALE = DH ** -0.5
HEADS_PER_STEP = {1: 8, 4: 1, 16: 1}


def _sub_rows(ref, r, dil, cols):
    if dil == 1:
        return ref[:, cols]
    return ref[pl.ds(r, BLK, stride=dil), cols]


def _set_sub_rows(ref, r, dil, cols, val):
    if dil == 1:
        ref[:, cols] = val
    else:
        ref[pl.ds(r, BLK, stride=dil), cols] = val


def _slopes(dil):
    hp = HEADS_PER_STEP[dil]
    s = jnp.asarray([dil * 2.0 ** (-(h + 1)) for h in range(H)], F32)
    return jnp.broadcast_to(s[:, None], (H, DH)).reshape(H // hp, hp, DH)


def _rms_rows(x):
    rs = lax.rsqrt(jnp.mean(x * x, axis=-1, keepdims=True) + EPS)
    return x * rs, rs


def _att_masks():
    qi = lax.broadcasted_iota(jnp.int32, (BLK, BLK), 0)
    kj = lax.broadcasted_iota(jnp.int32, (BLK, BLK), 1)
    steps_c = qi - kj
    steps_p = qi - kj + BLK
    return steps_c, steps_p, steps_c >= 0, steps_p <= BLK


def _qk_prep(proj, q_w, k_w):
    def body(q_ref, k_ref, qw_ref, kw_ref, qn_ref, kn_ref):
        for h in range(H):
            cols = slice(h * DH, (h + 1) * DH)
            qn_ref[:, cols] = _rms_rows(q_ref[:, cols])[0] * qw_ref[...]
            kn_ref[:, cols] = _rms_rows(k_ref[:, cols])[0] * kw_ref[...]

    out = jax.ShapeDtypeStruct((S, HW), F32)
    wspec = pl.BlockSpec((1, DH), lambda i: (0, 0))
    return pl.pallas_call(
        body, name="qk_prep", out_shape=(out, out), grid=(S // TR,),
        in_specs=[pl.BlockSpec((TR, HW), lambda i: (i, 4)), pl.BlockSpec((TR, HW), lambda i: (i, 5)), wspec, wspec],
        out_specs=(_row_spec(HW), _row_spec(HW)),
        compiler_params=_params(1),
    )(proj, proj, q_w, k_w)


def _qk_norm_bwd(proj, dqn, dkn, dv, q_w, k_w):
    def body(q_ref, k_ref, dqn_ref, dkn_ref, dv_ref, qw_ref, kw_ref, dq_ref, dk_ref, dvo_ref, dqw_ref, dkw_ref):
        i = pl.program_id(0)

        @pl.when(i == 0)
        def _():
            dqw_ref[...] = jnp.zeros_like(dqw_ref)
            dkw_ref[...] = jnp.zeros_like(dkw_ref)

        dvo_ref[...] = dv_ref[...].astype(BF16)
        for x_ref, d_ref, w_ref, o_ref, dw_ref in ((q_ref, dqn_ref, qw_ref, dq_ref, dqw_ref),
                                                   (k_ref, dkn_ref, kw_ref, dk_ref, dkw_ref)):
            for h in range(H):
                cols = slice(h * DH, (h + 1) * DH)
                xh, rs = _rms_rows(x_ref[:, cols])
                d = d_ref[:, cols]
                dw_ref[...] += jnp.sum(d * xh, axis=0, keepdims=True)
                dy = d * w_ref[...]
                o_ref[:, cols] = (rs * (dy - xh * jnp.mean(dy * xh, axis=-1, keepdims=True))).astype(BF16)

    big = jax.ShapeDtypeStruct((S, HW), BF16)
    small = jax.ShapeDtypeStruct((1, DH), F32)
    wspec = pl.BlockSpec((1, DH), lambda i: (0, 0))
    return pl.pallas_call(
        body, name="qk_norm_bwd", out_shape=(big, big, big, small, small), grid=(S // TR,),
        in_specs=[pl.BlockSpec((TR, HW), lambda i: (i, 4)), pl.BlockSpec((TR, HW), lambda i: (i, 5)),
                  _row_spec(HW), _row_spec(HW), _row_spec(HW), wspec, wspec],
        out_specs=(_row_spec(HW), _row_spec(HW), _row_spec(HW), wspec, wspec),
        compiler_params=_params(1),
    )(proj, proj, dqn, dkn, dv, q_w, k_w)


def _attn_delta(do, o):
    def body(do_ref, o_ref, d_ref):
        for h in range(H):
            cols = slice(h * DH, (h + 1) * DH)
            d_ref[:, cols] = jnp.broadcast_to(jnp.sum(do_ref[:, cols] * o_ref[:, cols], axis=-1, keepdims=True), (TR, DH))

    return pl.pallas_call(
        body, name="attn_delta", out_shape=jax.ShapeDtypeStruct((S, HW), F32), grid=(S // TR,),
        in_specs=[pl.BlockSpec((TR, HW), lambda i: (i, 1)), _row_spec(HW)], out_specs=_row_spec(HW),
        compiler_params=_params(1),
    )(do, o)


def _attn_fwd(proj, qn, kn, dil, phase=None):
    hp = HEADS_PER_STEP[dil]
    rows, ng, nb = BLK * dil, H // hp, S // (BLK * dil)

    def body(q_ref, kc_ref, kp_ref, vc_ref, vp_ref, sl_ref, o_ref, lse_ref):
        n = pl.program_id(0)
        steps_c, steps_p, ok_c, ok_p = _att_masks()
        ok_p = jnp.logical_and(ok_p, n > 0)
        fc, fp = steps_c.astype(F32), steps_p.astype(F32)
        for hh in range(hp):
            cols = slice(hh * DH, (hh + 1) * DH)
            sl = sl_ref[0, hh:hh + 1, :]
            for r in range(dil):
                sub = lambda ref: _sub_rows(ref, r, dil, cols)
                qv = sub(q_ref)
                sc = jnp.where(ok_c, _dot(qv, sub(kc_ref), NT) * ATT_SCALE - sl * fc, NEG)
                sp = jnp.where(ok_p, _dot(qv, sub(kp_ref), NT) * ATT_SCALE - sl * fp, NEG)
                m = jnp.maximum(jnp.max(sc, axis=-1, keepdims=True), jnp.max(sp, axis=-1, keepdims=True))
                pc, pp = jnp.exp(sc - m), jnp.exp(sp - m)
                den = jnp.sum(pc, axis=-1, keepdims=True) + jnp.sum(pp, axis=-1, keepdims=True)
                o = (_dot(pc, sub(vc_ref)) + _dot(pp, sub(vp_ref))) / den
                _set_sub_rows(o_ref, r, dil, cols, o)
                _set_sub_rows(lse_ref, r, dil, cols, jnp.broadcast_to(m + jnp.log(den), (BLK, DH)))

    cur = pl.BlockSpec((rows, hp * DH), lambda n, g: (n, g))
    prev = pl.BlockSpec((rows, hp * DH), lambda n, g: (jnp.maximum(n - 1, 0), g))
    vcur = pl.BlockSpec((rows, hp * DH), lambda n, g: (n, 6 * ng + g))
    vprev = pl.BlockSpec((rows, hp * DH), lambda n, g: (jnp.maximum(n - 1, 0), 6 * ng + g))
    out = jax.ShapeDtypeStruct((S, HW), F32)
    return _pcall(
        body, [qn, kn, kn, proj, proj, _slopes(dil)], phase,
        name=f"attn_fwd_d{dil}", out_shape=(out, out), grid=(nb, ng),
        in_specs=[cur, cur, prev, vcur, vprev, pl.BlockSpec((1, hp, DH), lambda n, g: (g, 0, 0))],
        out_specs=(cur, cur),
        compiler_params=_params(2),
    )


def _attn_merge(outs, lses, cat):
    def body(o1, o2, o3, l1, l2, l3, cat_ref, ob_ref, of_ref, lse_ref):
        a, b, c = l1[...], l2[...], l3[...]
        m = jnp.maximum(jnp.maximum(a, b), c)
        ea, eb, ec = jnp.exp(a - m), jnp.exp(b - m), jnp.exp(c - m)
        den = ea + eb + ec
        o = (ea * o1[...] + eb * o2[...] + ec * o3[...]) / den
        ob_ref[...] = o.astype(BF16)
        of_ref[...] = o
        lse_ref[...] = m + jnp.log(den)

    f = jax.ShapeDtypeStruct((S, HW), F32)
    return pl.pallas_call(
        body, name="attn_merge", out_shape=(jax.ShapeDtypeStruct((S, 2 * HW), BF16), f, f), grid=(S // TR,),
        in_specs=[_row_spec(HW)] * 6 + [pl.BlockSpec(memory_space=pl.ANY)],
        out_specs=(pl.BlockSpec((TR, HW), lambda i: (i, 1)), _row_spec(HW), _row_spec(HW)),
        input_output_aliases={6: 0},
        compiler_params=_params(1),
    )(*outs, *lses, cat)


def _attn_bwd(proj, qn, kn, lse, delta, do, dil, acc, phase=None):
    hp = HEADS_PER_STEP[dil]
    rows, ng, nb = BLK * dil, H // hp, S // (BLK * dil)
    has_acc = acc is not None

    def body(*refs):
        q_ref, qn_ref, kp_ref, kc_ref, vp_ref, vc_ref, l_ref, ln_ref, d_ref, dn_ref, do_ref, don_ref, sl_ref = refs[:13]
        refs = refs[13:]
        if has_acc:
            aq_ref, ak_ref, av_ref = refs[:3]
            refs = refs[3:]
        dq_ref, dk_ref, dv_ref = refs
        m = pl.program_id(0)
        steps_c, steps_p, ok_c, ok_p = _att_masks()
        ok_prev = jnp.logical_and(ok_p, m > 0)
        ok_next = jnp.logical_and(ok_p, m < nb - 1)
        fc, fp = steps_c.astype(F32), steps_p.astype(F32)
        for hh in range(hp):
            cols = slice(hh * DH, (hh + 1) * DH)
            sl = sl_ref[0, hh:hh + 1, :]
            for r in range(dil):
                sub = lambda ref: _sub_rows(ref, r, dil, cols)
                qv, qnv, kcv, kpv = sub(q_ref), sub(qn_ref), sub(kc_ref), sub(kp_ref)
                vc, vp = sub(vc_ref), sub(vp_ref)
                dov, donv = sub(do_ref), sub(don_ref)
                lse_m, lse_n = sub(l_ref)[:, 0:1], sub(ln_ref)[:, 0:1]
                delta_m, delta_n = sub(d_ref)[:, 0:1], sub(dn_ref)[:, 0:1]
                p_c = jnp.where(ok_c, jnp.exp(_dot(qv, kcv, NT) * ATT_SCALE - sl * fc - lse_m), 0.0)
                p_p = jnp.where(ok_prev, jnp.exp(_dot(qv, kpv, NT) * ATT_SCALE - sl * fp - lse_m), 0.0)
                p_n = jnp.where(ok_next, jnp.exp(_dot(qnv, kcv, NT) * ATT_SCALE - sl * fp - lse_n), 0.0)
                ds_c = p_c * (_dot(dov, vc, NT) - delta_m)
                ds_p = p_p * (_dot(dov, vp, NT) - delta_m)
                ds_n = p_n * (_dot(donv, vc, NT) - delta_n)
                dqn = (_dot(ds_c, kcv) + _dot(ds_p, kpv)) * ATT_SCALE
                dkn = (_dot(ds_c, qv, TN) + _dot(ds_n, qnv, TN)) * ATT_SCALE
                dv = _dot(p_c, dov, TN) + _dot(p_n, donv, TN)
                if has_acc:
                    dqn, dkn, dv = dqn + sub(aq_ref), dkn + sub(ak_ref), dv + sub(av_ref)
                _set_sub_rows(dq_ref, r, dil, cols, dqn)
                _set_sub_rows(dk_ref, r, dil, cols, dkn)
                _set_sub_rows(dv_ref, r, dil, cols, dv)

    def shifted(shift, m):
        if shift < 0:
            return jnp.maximum(m - 1, 0)
        if shift > 0:
            return jnp.minimum(m + 1, nb - 1)
        return m

    def spec(shift, group=0):
        return pl.BlockSpec((rows, hp * DH), lambda m, g: (shifted(shift, m), group * ng + g))

    ins = [qn, qn, kn, kn, proj, proj, lse, lse, delta, delta, do, do, _slopes(dil)]
    in_specs = [spec(0), spec(1), spec(-1), spec(0), spec(-1, 6), spec(0, 6), spec(0), spec(1), spec(0), spec(1),
                spec(0, 1), spec(1, 1), pl.BlockSpec((1, hp, DH), lambda m, g: (g, 0, 0))]
    if has_acc:
        ins += list(acc)
        in_specs += [spec(0)] * 3
    big = jax.ShapeDtypeStruct((S, HW), F32)
    return _pcall(
        body, ins, phase, name=f"attn_bwd_d{dil}", out_shape=(big, big, big), grid=(nb, ng),
        in_specs=in_specs, out_specs=(spec(0),) * 3,
        compiler_params=_params(2),
    )


TC = 512
NCT = DFF // TC


def _shift_rows(a, k):
    rows = lax.broadcasted_iota(jnp.int32, a.shape, 0)
    rolled = pltpu.roll(a, k % S, 0)
    if k > 0:
        return jnp.where(rows >= k, rolled, 0.0)
    return jnp.where(rows < S + k, rolled, 0.0)


def _conv_pre(a, w_ref, b_ref):
    return b_ref[...] + w_ref[0:1, :] * _shift_rows(a, 2) + w_ref[1:2, :] * _shift_rows(a, 1) + w_ref[2:3, :] * a


def _conv_gate_fwd(u, conv_w, conv_b, name, tiles=(0, NCT), into=None, phase=None):
    t0, t1 = tiles

    def body(a_ref, g_ref, w_ref, b_ref, *rest):
        pre = _conv_pre(a_ref[...].astype(F32), w_ref, b_ref)
        rest[-1][...] = (pre * _sigmoid(pre) * g_ref[...].astype(F32)).astype(BF16)

    args = [u, u, conv_w, conv_b]
    in_specs = [pl.BlockSpec((S, TC), lambda i: (0, t0 + i)), pl.BlockSpec((S, TC), lambda i: (0, NCT + t0 + i)),
                pl.BlockSpec((3, TC), lambda i: (0, t0 + i)), pl.BlockSpec((1, TC), lambda i: (0, t0 + i))]
    kw = {}
    if into is not None:
        args.append(into)
        in_specs.append(pl.BlockSpec(memory_space=pl.ANY))
        kw["input_output_aliases"] = {4: 0}
    return _pcall(
        body, args, phase, name=name, out_shape=jax.ShapeDtypeStruct((S, DFF), BF16), grid=(t1 - t0,),
        in_specs=in_specs, out_specs=pl.BlockSpec((S, TC), lambda i: (0, t0 + i)),
        compiler_params=_params(1), **kw,
    )


def _conv_gate_bwd(dy, u, conv_w, conv_b, phase=None):
    def body(dy_ref, a_ref, g_ref, w_ref, b_ref, du_ref, db_ref, dw_ref, da_buf, dg_buf, sems):
        i = pl.program_id(0)
        slot = i % 2

        def writes(step, s):
            col = pl.multiple_of(step * TC, TC)
            return (pltpu.make_async_copy(da_buf.at[s], du_ref.at[:, pl.ds(col, TC)], sems.at[0, s]),
                    pltpu.make_async_copy(dg_buf.at[s], du_ref.at[:, pl.ds(DFF + col, TC)], sems.at[1, s]))

        @pl.when(i >= 2)
        def _():
            for cp in writes(i - 2, slot):
                cp.wait()

        a = a_ref[...].astype(F32)
        dyv = dy_ref[...].astype(F32)
        pre = _conv_pre(a, w_ref, b_ref)
        sg = _sigmoid(pre)
        dg_buf[slot] = (dyv * pre * sg).astype(BF16)
        dpre = dyv * g_ref[...].astype(F32) * (sg * (1.0 + pre * (1.0 - sg)))
        da = w_ref[2:3, :] * dpre + w_ref[1:2, :] * _shift_rows(dpre, -1) + w_ref[0:1, :] * _shift_rows(dpre, -2)
        da_buf[slot] = da.astype(BF16)
        for cp in writes(i, slot):
            cp.start()
        db_ref[...] = jnp.sum(dpre, axis=0, keepdims=True)
        dw_ref[0:1, :] = jnp.sum(dpre * _shift_rows(a, 2), axis=0, keepdims=True)
        dw_ref[1:2, :] = jnp.sum(dpre * _shift_rows(a, 1), axis=0, keepdims=True)
        dw_ref[2:3, :] = jnp.sum(dpre * a, axis=0, keepdims=True)

        @pl.when(i == NCT - 1)
        def _():
            for cp in writes(i - 1, 1 - slot) + writes(i, slot):
                cp.wait()

    col = pl.BlockSpec((S, TC), lambda i: (0, i))
    return _pcall(
        body, [dy, u, u, conv_w, conv_b], phase, name="conv_gate_bwd",
        out_shape=(jax.ShapeDtypeStruct((S, 2 * DFF), BF16), jax.ShapeDtypeStruct((1, DFF), F32),
                   jax.ShapeDtypeStruct((3, DFF), F32)),
        grid=(NCT,),
        in_specs=[col, col, pl.BlockSpec((S, TC), lambda i: (0, NCT + i)),
                  pl.BlockSpec((3, TC), lambda i: (0, i)), pl.BlockSpec((1, TC), lambda i: (0, i))],
        out_specs=(pl.BlockSpec(memory_space=pl.ANY), pl.BlockSpec((1, TC), lambda i: (0, i)),
                   pl.BlockSpec((3, TC), lambda i: (0, i))),
        scratch_shapes=[pltpu.VMEM((2, S, TC), BF16), pltpu.VMEM((2, S, TC), BF16), pltpu.SemaphoreType.DMA((2, 2))],
        compiler_params=_params(1),
    )


def _out_loss(z, x1, target, gate):
    def body(z_ref, x_ref, t_ref, g_ref, do_ref, dz_ref, dg_ref, loss_ref):
        i = pl.program_id(0)
        zv = z_ref[...]
        gv = g_ref[...]
        err = x_ref[...] + gv * zv - t_ref[...]
        dout = err * (1.0 / D)
        do_ref[...] = dout
        dz_ref[...] = (dout * gv).astype(BF16)

        @pl.when(i == 0)
        def _():
            dg_ref[...] = jnp.zeros_like(dg_ref)
            loss_ref[...] = jnp.zeros_like(loss_ref)

        dg_ref[...] += jnp.sum(dout * zv, axis=0, keepdims=True)
        part = jnp.sum(jnp.sum(err * err, axis=0, keepdims=True), axis=-1, keepdims=True) * (0.5 / D)
        lane = lax.broadcasted_iota(jnp.int32, (1, 128), 1)
        loss_ref[...] += jnp.where(lane == 0, part, 0.0)

    return pl.pallas_call(
        body, name="out_loss",
        out_shape=(jax.ShapeDtypeStruct((S, D), F32), jax.ShapeDtypeStruct((S, D), BF16),
                   jax.ShapeDtypeStruct((1, D), F32), jax.ShapeDtypeStruct((1, 128), F32)),
        grid=(S // TR,),
        in_specs=[_row_spec(), _row_spec(), _row_spec(), _vec_spec()],
        out_specs=(_row_spec(), _row_spec(), _vec_spec(), _vec_spec(128)),
        compiler_params=_params(1),
    )(z, x1, target, gate)


ADA_COLS = 6 * D // N_CHIPS
TA = 512


def _ada_fwd(c_all, w_shard, b_shard):
    def body(c_ref, w_ref, b_ref, o_ref):
        cv = c_ref[...]
        o_ref[...] = _dot_f32(cv * _sigmoid(cv), w_ref[...]) + b_ref[...]

    return pl.pallas_call(
        body, name="ada_fwd", out_shape=jax.ShapeDtypeStruct((N_DEV, ADA_COLS), F32), grid=(ADA_COLS // TA,),
        in_specs=[pl.BlockSpec((N_DEV, D), lambda j: (0, 0)), pl.BlockSpec((D, TA), lambda j: (0, j)),
                  pl.BlockSpec((1, TA), lambda j: (0, j))],
        out_specs=pl.BlockSpec((N_DEV, TA), lambda j: (0, j)),
        compiler_params=_params(1),
    )(c_all, w_shard, b_shard)


ADA_ROWS = 128


def _ada_bwd_adamw(c_all, dmod_shard, w, m, v):
    def body(c_ref, d_ref, w_ref, m_ref, v_ref, g_ref, dl_ref, mo_ref, vo_ref):
        cv = c_ref[...]
        gv = lax.dot_general(cv * _sigmoid(cv), d_ref[...], TN, precision=lax.Precision.HIGHEST,
                             preferred_element_type=F32)
        g_ref[...] = gv
        m2 = ADAM_B1 * m_ref[...] + (1.0 - ADAM_B1) * gv
        v2 = ADAM_B2 * v_ref[...] + (1.0 - ADAM_B2) * (gv * gv)
        m_hat = m2 / (1.0 - ADAM_B1 ** ADAM_STEP)
        v_hat = v2 / (1.0 - ADAM_B2 ** ADAM_STEP)
        dl_ref[...] = -ADAM_LR * (m_hat / (jnp.sqrt(v_hat) + ADAM_EPS) + ADAM_WD * w_ref[...])
        mo_ref[...] = m2
        vo_ref[...] = v2

    spec = pl.BlockSpec((ADA_ROWS, ADA_COLS), lambda i: (i, 0))
    out = jax.ShapeDtypeStruct((D, ADA_COLS), F32)
    return pl.pallas_call(
        body, name="ada_bwd_adamw", out_shape=(out,) * 4, grid=(D // ADA_ROWS,),
        in_specs=[pl.BlockSpec((N_DEV, ADA_ROWS), lambda i: (0, i)), pl.BlockSpec((N_DEV, ADA_COLS), lambda i: (0, 0)),
                  spec, spec, spec],
        out_specs=(spec,) * 4,
        compiler_params=_params(1),
    )(c_all, dmod_shard, w, m, v)


def _sum_rows(g, name):
    rows, n = g.shape

    def body(g_ref, o_ref):
        acc = g_ref[0:1, :]
        for i in range(1, rows):
            acc = acc + g_ref[i:i + 1, :]
        o_ref[...] = acc

    return pl.pallas_call(
        body, name=name, out_shape=jax.ShapeDtypeStruct((1, n), F32),
        in_specs=[VMEM_SPEC], out_specs=VMEM_SPEC,
    )(g)


def _lb_grad(lb_logits, dlb):
    def body(l_ref, d_ref, o_ref):
        p = 1.0 / (1.0 + jnp.exp(l_ref[1:2, :] - l_ref[0:1, :]))
        t = d_ref[...] * p * (1.0 - p)
        o_ref[0:1, :] = t
        o_ref[1:2, :] = -t

    return pl.pallas_call(
        body, name="lb_grad", out_shape=jax.ShapeDtypeStruct((2, HW), F32),
        in_specs=[VMEM_SPEC, VMEM_SPEC], out_specs=VMEM_SPEC,
    )(lb_logits, dlb)


def _adamw(w, g, m, v, name, copy_grad=False):
    rows, cols = w.shape
    tr = rows
    if rows * cols * 4 > ADAM_BLOCK_BYTES:
        tr = _pick(rows, [t for t in (256, 128, 64, 32, 16, 8) if t * cols * 4 <= ADAM_BLOCK_BYTES])

    def body(w_ref, g_ref, m_ref, v_ref, d_ref, mo_ref, vo_ref, *go_ref):
        gv = g_ref[...]
        if copy_grad:
            go_ref[0][...] = gv
        m2 = ADAM_B1 * m_ref[...] + (1.0 - ADAM_B1) * gv
        v2 = ADAM_B2 * v_ref[...] + (1.0 - ADAM_B2) * (gv * gv)
        m_hat = m2 / (1.0 - ADAM_B1 ** ADAM_STEP)
        v_hat = v2 / (1.0 - ADAM_B2 ** ADAM_STEP)
        d_ref[...] = -ADAM_LR * (m_hat / (jnp.sqrt(v_hat) + ADAM_EPS) + ADAM_WD * w_ref[...])
        mo_ref[...] = m2
        vo_ref[...] = v2

    spec = pl.BlockSpec((tr, cols), lambda i: (i, 0))
    out = jax.ShapeDtypeStruct((rows, cols), F32)
    n_out = 4 if copy_grad else 3
    return pl.pallas_call(
        body, name=name, out_shape=(out,) * n_out, grid=(rows // tr,),
        in_specs=[spec] * 4, out_specs=(spec,) * n_out,
        compiler_params=_params(1),
    )(w, g, m, v)


SC_TILES = 32
SC_LANES = 16
SC_COL_PARTS = 2
SC_CHUNK_ROWS = 8


def _adamw_sc(w, g, m, v, name):
    rows, cols = w.shape
    band, part = rows // (SC_TILES // SC_COL_PARTS), cols // SC_COL_PARTS
    assert band % SC_CHUNK_ROWS == 0 and part % SC_LANES == 0, (rows, cols)

    def body(w_hbm, g_hbm, m_hbm, v_hbm, d_hbm, mo_hbm, vo_hbm, go_hbm, wb, gb, mb, vb, db):
        tile = lax.axis_index("sc_subcore") * 2 + lax.axis_index("sc_core")
        row0 = (tile // SC_COL_PARTS) * band
        col0 = (tile % SC_COL_PARTS) * part

        @pl.loop(0, band, step=SC_CHUNK_ROWS)
        def _(r):
            at = lambda ref: ref.at[pl.ds(row0 + r, SC_CHUNK_ROWS), pl.ds(col0, part)]
            pltpu.sync_copy(at(w_hbm), wb)
            pltpu.sync_copy(at(g_hbm), gb)
            pltpu.sync_copy(gb, at(go_hbm))
            pltpu.sync_copy(at(m_hbm), mb)
            pltpu.sync_copy(at(v_hbm), vb)

            @pl.loop(0, SC_CHUNK_ROWS)
            def _(i):
                @pl.loop(0, part, step=SC_LANES)
                def _(j):
                    sl = (i, pl.ds(j, SC_LANES))
                    gv = gb[sl]
                    m2 = ADAM_B1 * mb[sl] + (1.0 - ADAM_B1) * gv
                    v2 = ADAM_B2 * vb[sl] + (1.0 - ADAM_B2) * (gv * gv)
                    m_hat = m2 / (1.0 - ADAM_B1 ** ADAM_STEP)
                    v_hat = v2 / (1.0 - ADAM_B2 ** ADAM_STEP)
                    db[sl] = -ADAM_LR * (m_hat / (jnp.sqrt(v_hat) + ADAM_EPS) + ADAM_WD * wb[sl])
                    mb[sl] = m2
                    vb[sl] = v2

            pltpu.sync_copy(db, at(d_hbm))
            pltpu.sync_copy(mb, at(mo_hbm))
            pltpu.sync_copy(vb, at(vo_hbm))

    out = jax.ShapeDtypeStruct((rows, cols), F32)
    buf = pltpu.VMEM((SC_CHUNK_ROWS, part), F32)
    return pl.kernel(
        body, name=name, out_type=(out, out, out, out),
        mesh=plsc.VectorSubcoreMesh(core_axis_name="sc_core", subcore_axis_name="sc_subcore"),
        scratch_types=[buf] * 5,
    )(w, g, m, v)


W_IN, W_OUT, W_UP, W_DOWN = range(4)
IN_CHUNKS = 4
W_INQ = 4


def _sequence_step(x, target, mod, norm1_w, lb_logits, hg_norm_w, q_norm_w, k_norm_w, norm2_w, conv_w, conv_b, net):
    shift1, scale1, gate1, shift2, scale2, gate2 = [mod[:, i * D:(i + 1) * D] for i in range(6)]

    def run(name, fn, *args, **kw):
        phase = net.host(name)
        if phase is None:
            return fn(*args, **kw)
        res, comm = fn(*args, phase=phase, **kw)
        net.done(name, comm, res[0] if isinstance(res, tuple) else res)
        return res

    h = _norm_mod(x, norm1_w, scale1, shift1, "norm1_fwd")
    proj = run("proj_fwd", _matmul, h, net.full[W_IN], "nn", F32, "proj_fwd")
    cat, o_raw, states = run("hgrn_fwd", _hgrn_fwd, proj, lb_logits, hg_norm_w)
    qn, kn = _qk_prep(proj, q_norm_w, k_norm_w)
    att = [run(f"attn_fwd_d{dil}", _attn_fwd, proj, qn, kn, dil) for dil in DILS]
    cat, b_out_f32, lse = _attn_merge([t[0] for t in att], [t[1] for t in att], cat)
    mix = run("mix_fwd", _matmul, cat, net.full[W_OUT], "nn", F32, "mix_fwd")
    x1, h2 = _resid_norm_mod(x, mix, gate1, norm2_w, scale2, shift2, "norm2_fwd")
    u = run("up_fwd", _matmul, h2, net.full[W_UP], "nn", BF16, "up_fwd")
    yact = run("conv_gate_fwd_a", _conv_gate_fwd, u, conv_w, conv_b, "conv_gate_fwd_a", tiles=(0, NCT // 2 + 1))
    yact = run("conv_gate_fwd_b", _conv_gate_fwd, u, conv_w, conv_b, "conv_gate_fwd_b", tiles=(NCT // 2 + 1, NCT),
               into=yact)
    z =_matmul(yact, net.full[W_DOWN], "nn", F32, "down_fwd")
    dout, dz, dgate2, loss_row = _out_loss(z, x1, target, gate2)

    net.grad[W_DOWN] = _matmul(yact, dz, "tn", BF16, "down_bwd_w")
    dyact = run("down_bwd_x", _matmul, dz, net.full[W_DOWN], "nt", BF16, "down_bwd_x")
    du, dconv_b, dconv_w = run("conv_gate_bwd", _conv_gate_bwd, dyact, u, conv_w, conv_b)
    net.grad[W_UP] = run("up_bwd_w", _matmul, h2, du, "tn", BF16, "up_bwd_w")
    dh2 = run("up_bwd_x", _matmul, du, net.full[W_UP], "nt", F32, "up_bwd_x")
    dx1, dshift2, dscale2, dnorm2, dgate1, dmix = run(
        "norm2_bwd", _norm_mod_bwd, dh2, x1, norm2_w, scale2, dout, "norm2_bwd", mix=mix, gate=gate1)
    net.grad[W_OUT] = run("mix_bwd_w", _matmul, cat, dmix, "tn", BF16, "mix_bwd_w")
    dcat = run("mix_bwd_x", _matmul, dmix, net.full[W_OUT], "nt", F32, "mix_bwd_x")
    dhq, dhf, dhi, dhg, dlb, dhg_norm = run("hgrn_bwd", _hgrn_bwd, proj, lb_logits, hg_norm_w, o_raw, states, dcat)
    delta = _attn_delta(dcat, b_out_f32)
    acc = None
    for dil in DILS:
        acc = run(f"attn_bwd_d{dil}", _attn_bwd, proj, qn, kn, lse, delta, dcat, dil, acc)
    daq, dak, dav, dq_norm, dk_norm = _qk_norm_bwd(proj, *acc, q_norm_w, k_norm_w)
    dproj = jnp.concatenate([dhq, dhf, dhi, dhg, daq, dak, dav], axis=1)
    net.carry = dproj
    for q in range(IN_CHUNKS):
        net.grad[W_INQ + q] = run(f"proj_bwd_w_{q}", _matmul, h, net.carry, "tn", BF16, f"proj_bwd_w_{q}",
                                  m_blocks=(D // IN_CHUNKS, [q]))
    dh = run("proj_bwd_x", _matmul, net.carry, net.full[W_IN], "nt", F32, "proj_bwd_x")
    grad_x, dshift1, dscale1, dnorm1 = run("norm1_bwd", _norm_mod_bwd, dh, x, norm1_w, scale1, dx1, "norm1_bwd")

    dmod = jnp.concatenate([dshift1, dscale1, dgate1, dshift2, dscale2, dgate2], axis=1)
    small = dict(norm1_w=dnorm1, lb=dlb, hg_norm_w=dhg_norm, q_norm_w=dq_norm, k_norm_w=dk_norm,
                 norm2_w=dnorm2, conv_b=dconv_b, conv_w=dconv_w)
    return loss_row, grad_x, dmod, small


def _place():
    x, y, c = lax.axis_index("x"), lax.axis_index("y"), lax.axis_index("c")
    others = [(1 - x, y), (x, 1 - y), (1 - x, 1 - y)]
    return x, y, c, others


def _remote(src, dst, send_sems, recv_sems, k, to):
    return pltpu.make_async_remote_copy(src_ref=src, dst_ref=dst, send_sem=send_sems.at[k], recv_sem=recv_sems.at[k],
                                        device_id=to, device_id_type=MESH)


class _Phase:
    def __init__(self):
        self.ins, self.outs, self.aliases, self.groups, self.n = [], [], {}, [], 0

    def add(self, ins, outs, aliases, n, copies):
        i0, o0 = len(self.ins), len(self.outs)
        self.ins += list(ins)
        self.outs += list(outs)
        self.aliases.update({i0 + i: o0 + o for i, o in aliases.items()})
        self.groups.append((slice(i0, i0 + len(ins)), slice(o0, o0 + len(outs)), copies))
        self.n += n
        return slice(o0, o0 + len(outs))

    def build(self, cin, cout, send_sems, recv_sems, landing):
        res = []
        for si, so, copies in self.groups:
            for src, dst, land, peer in copies(cin[si], cout[so]):
                k = len(res)
                res.append(_remote(land, land, send_sems, recv_sems, k, peer) if landing
                           else _remote(src, dst, send_sems, recv_sems, k, peer))
        assert len(res) == self.n
        return res


def _pcall(body, args, phase=None, **kw):
    if phase is None or not phase.groups:
        res = pl.pallas_call(body, **kw)(*args)
        return res if phase is None else (res, ())
    grid = tuple(kw.pop("grid", ()))
    out_shape, out_specs = kw.pop("out_shape"), kw.pop("out_specs")
    single = not isinstance(out_shape, (tuple, list))
    out_shape = [out_shape] if single else list(out_shape)
    out_specs = [out_specs] if single else list(out_specs)
    scratch = list(kw.pop("scratch_shapes", ()))
    aliases = dict(kw.pop("input_output_aliases", {}))
    n_in, n_out, n_ci, n_co = len(args), len(out_shape), len(phase.ins), len(phase.outs)
    aliases.update({n_in + i: n_out + o for i, o in phase.aliases.items()})

    def hosted(*refs):
        ins, cin = refs[:n_in], refs[n_in:n_in + n_ci]
        outs = refs[n_in + n_ci:n_in + n_ci + n_out]
        cout = refs[n_in + n_ci + n_out:n_in + n_ci + n_out + n_co]
        scr, send_sems, recv_sems = refs[n_in + n_ci + n_out + n_co:-2], refs[-2], refs[-1]
        ids = [pl.program_id(a) for a in range(len(grid))]

        def start():
            for cp in phase.build(cin, cout, send_sems, recv_sems, False):
                cp.start()

        def finish():
            for cp in phase.build(cin, cout, send_sems, recv_sems, False):
                cp.wait_send()
            for cp in phase.build(cin, cout, send_sems, recv_sems, True):
                cp.wait_recv()

        if grid:
            first = functools.reduce(jnp.logical_and, [i == 0 for i in ids])
            last = functools.reduce(jnp.logical_and, [i == g - 1 for i, g in zip(ids, grid)])
            pl.when(first)(start)
            body(*ins, *outs, *scr)
            pl.when(last)(finish)
        else:
            start()
            body(*ins, *outs, *scr)
            finish()

    res = pl.pallas_call(
        hosted, out_shape=tuple(out_shape + phase.outs), grid=grid,
        in_specs=list(kw.pop("in_specs")) + [HBM_SPEC] * n_ci, out_specs=tuple(out_specs + [HBM_SPEC] * n_co),
        input_output_aliases=aliases,
        scratch_shapes=scratch + [pltpu.SemaphoreType.DMA((phase.n,)), pltpu.SemaphoreType.DMA((phase.n,))],
        **kw,
    )(*args, *phase.ins)
    outs = res[:n_out]
    return (outs[0] if single else tuple(outs)), tuple(res[n_out:])


def _comm_only(name, phase):
    return _pcall(lambda: None, [], phase, name=name, out_shape=(), in_specs=[], out_specs=())[1]


def _all_gather_rows(block, name, phase=None):
    m_per, n = block.shape

    def body(x_ref, out_ref, send_sems, recv_sems, local_sem):
        x, y, c, chips = _place()
        me, sibling = (x, y, c), (x, y, 1 - c)

        def rows(px, py, pc):
            return out_ref.at[pl.ds((4 * px + 2 * py + pc) * m_per, m_per), :]

        def copy(k, blk, to, src=None):
            return _remote(rows(*blk) if src is None else src, rows(*blk), send_sems, recv_sems, k, to)

        mine = pltpu.make_async_copy(x_ref, rows(*me), local_sem)
        mine.start()
        first = [copy(0, me, sibling, src=x_ref)]
        first += [copy(1 + j, me, (*chip, c), src=x_ref) for j, chip in enumerate(chips)]
        for cp in first:
            cp.start()
        passed = [copy(4 + j, (*chip, c), sibling) for j, chip in enumerate(chips)]
        for j, chip in enumerate(chips):
            copy(1 + j, (*chip, c), me).wait_recv()
            passed[j].start()
        copy(0, sibling, me).wait_recv()
        for j, chip in enumerate(chips):
            copy(4 + j, (*chip, 1 - c), me).wait_recv()
        for cp in first + passed:
            cp.wait_send()
        mine.wait()

    return _pcall(
        body, [block], phase, name=name, out_shape=jax.ShapeDtypeStruct((N_DEV * m_per, n), block.dtype),
        in_specs=[VMEM_SPEC], out_specs=VMEM_SPEC,
        scratch_shapes=[pltpu.SemaphoreType.DMA((7,)), pltpu.SemaphoreType.DMA((7,)), pltpu.SemaphoreType.DMA],
    )


class _Geo:
    def __init__(self, kind, shard_shape):
        self.kind = kind
        self.shard_shape = tuple(shard_shape)
        self.hr, self.hc = shard_shape[0] // 2, shard_shape[1]
        if kind == "col":
            self.full_shape = (shard_shape[0], N_CHIPS * shard_shape[1])
        else:
            self.full_shape = (N_CHIPS * shard_shape[0], shard_shape[1])

    def full_half(self, ref, j, c):
        return self.piece(ref, j, c, 0, 1, 1)

    def piece(self, ref, j, c, p0, p1, pieces, part=None):
        pr = self.hr // pieces
        off, n = p0 * pr, (p1 - p0) * pr
        if part is not None:
            top = (n // 32) * 16
            off, n = (off, top) if part == 0 else (off + top, n - top)
        if self.kind == "col":
            return ref.at[pl.ds(pl.multiple_of(c * self.hr + off, 16), n),
                          pl.ds(pl.multiple_of(j * self.hc, 128), self.hc)]
        return ref.at[pl.ds(pl.multiple_of((2 * j + c) * self.hr + off, 16), n), :]


WEIGHT_KINDS = ("col", "row", "col", "row")
NW = len(WEIGHT_KINDS)


def _comm_call(body, name, ins, outs, n_remote, aliases=None):
    return pl.pallas_call(
        body, name=name, out_shape=tuple(outs),
        in_specs=[HBM_SPEC] * len(ins), out_specs=tuple([HBM_SPEC] * len(outs)),
        input_output_aliases=aliases or {},
        scratch_shapes=[pltpu.SemaphoreType.DMA((n_remote,)), pltpu.SemaphoreType.DMA((n_remote,))],
    )(*ins)


ROW_TILES = (256, 176, 128, 64, 32, 16)


def _cast_into_full(shard, geo, place, name):
    rs, cs = shard.shape
    tr = _pick(rs, ROW_TILES)
    nb = rs // tr

    def body(place_ref, s_ref, o_ref):
        o_ref[...] = s_ref[...].astype(BF16)

    if geo.kind == "col":
        out_map = lambda i, place_ref: (i, place_ref[0])
    else:
        out_map = lambda i, place_ref: (place_ref[0] * nb + i, 0)
    return pl.pallas_call(
        body, name=name, out_shape=jax.ShapeDtypeStruct(geo.full_shape, BF16),
        grid_spec=pltpu.PrefetchScalarGridSpec(
            num_scalar_prefetch=1, grid=(nb,),
            in_specs=[pl.BlockSpec((tr, cs), lambda i, place_ref: (i, 0))],
            out_specs=pl.BlockSpec((tr, cs), out_map)),
        compiler_params=_params(1),
    )(place, shard)


def _gather_weight(full, geo, pieces, name):
    per = 8

    def body(in_ref, ref, send_sems, recv_sems):
        x, y, c, _ = _place()
        j, jx, jy, jo = 2 * x + y, 2 * (1 - x) + y, 2 * x + (1 - y), 2 * (1 - x) + (1 - y)
        nx, ny, sib = (1 - x, y, c), (x, 1 - y, c), (x, y, 1 - c)

        def cp(region, k, to):
            return _remote(region, region, send_sems, recv_sems, k, to)

        def reg(chip, p, part=None, core=c):
            return geo.piece(ref, chip, core, p, p + 1, pieces, part)

        sent = []
        for p in range(pieces):
            sent += [cp(reg(j, p), per * p, nx), cp(reg(j, p), per * p + 1, ny)]
        for d in sent:
            d.start()
        for p in range(pieces):
            cp(reg(jx, p), per * p, nx).wait_recv()
            new = [cp(reg(jx, p, 0), per * p + 2, ny), cp(reg(jx, p), per * p + 4, sib)]
            cp(reg(jy, p), per * p + 1, ny).wait_recv()
            new += [cp(reg(jy, p, 1), per * p + 3, nx), cp(reg(jy, p), per * p + 5, sib)]
            for d in new:
                d.start()
            sent += new
        for p in range(pieces):
            cp(reg(jo, p, 0), per * p + 2, ny).wait_recv()
            cp(reg(jo, p, 1), per * p + 3, nx).wait_recv()
            new = [cp(reg(jo, p, 0), per * p + 6, sib), cp(reg(jo, p, 1), per * p + 7, sib)]
            for d in new:
                d.start()
            sent += new
        for p in range(pieces):
            cp(reg(jx, p, None, 1 - c), per * p + 4, sib).wait_recv()
            cp(reg(jy, p, None, 1 - c), per * p + 5, sib).wait_recv()
            cp(reg(jo, p, 0, 1 - c), per * p + 6, sib).wait_recv()
            cp(reg(jo, p, 1, 1 - c), per * p + 7, sib).wait_recv()
        for d in sent:
            d.wait_send()

    return _comm_call(body, name, [full], [_same(full)], per * pieces, aliases={0: 0})[0]


def _same(a):
    return jax.ShapeDtypeStruct(a.shape, a.dtype)


def _gather_ici(full, geo, p0, p1, pieces):
    def copies(ins, outs):
        x, y, c, _ = _place()
        mine = geo.piece(outs[0], 2 * x + y, c, p0, p1, pieces)
        return [(mine, mine, geo.piece(outs[0], 2 * ox + oy, c, p0, p1, pieces), (ox, oy, c))
                for ox, oy in ((1 - x, y), (x, 1 - y))]

    return dict(ins=[full], outs=[_same(full)], aliases={0: 0}, n=2, copies=copies)


def _gather_relay(full, geo, p0, p1, pieces):
    def copies(ins, outs):
        x, y, c, _ = _place()
        jx, jy, jo = 2 * (1 - x) + y, 2 * x + (1 - y), 2 * (1 - x) + (1 - y)
        reg = lambda chip, part: geo.piece(outs[0], chip, c, p0, p1, pieces, part)
        return [(reg(jx, 0), reg(jx, 0), reg(jo, 0), (x, 1 - y, c)), (reg(jy, 1), reg(jy, 1), reg(jo, 1), (1 - x, y, c))]

    return dict(ins=[full], outs=[_same(full)], aliases={0: 0}, n=2, copies=copies)


def _gather_d2d(full, geo):
    def copies(ins, outs):
        x, y, c, chips = _place()
        return [(geo.full_half(outs[0], 2 * ox + oy, c), geo.full_half(outs[0], 2 * ox + oy, c),
                 geo.full_half(outs[0], 2 * ox + oy, 1 - c), (x, y, 1 - c)) for ox, oy in chips]

    return dict(ins=[full], outs=[_same(full)], aliases={0: 0}, n=3, copies=copies)


def _swap_d2d(grad, geo):
    def copies(ins, outs):
        x, y, c, _ = _place()
        return [(geo.full_half(ins[0], j, 1 - c), outs[0].at[j], outs[0].at[j], (x, y, 1 - c)) for j in range(N_CHIPS)]

    return dict(ins=[grad], outs=[jax.ShapeDtypeStruct((N_CHIPS, geo.hr, geo.hc), BF16)], aliases={}, n=N_CHIPS,
                copies=copies)


SEM_SPEC = pl.BlockSpec(memory_space=pltpu.SEMAPHORE)
SIDE_EFFECT = pltpu.SideEffectType.DATAFLOW_SIDE_EFFECTING


def _scatter_start(pair, geo, name, carry=None):
    n_carry = 0 if carry is None else 1

    def body(pair_ref, land_ref, *rest):
        sems, token = rest[n_carry:n_carry + 6], rest[n_carry + 8]
        x, y, c, chips = _place()
        for k, (ox, oy) in enumerate(chips):
            pltpu.make_async_remote_copy(src_ref=pair_ref.at[2 * ox + oy], dst_ref=land_ref.at[k], send_sem=sems[k],
                                         recv_sem=sems[3 + k], device_id=(ox, oy, c), device_id_type=MESH).start()
        token[...] = jnp.zeros_like(token)

    land = jax.ShapeDtypeStruct((3, geo.hr, geo.hc), BF16)
    out_shape = (pltpu.SemaphoreType.DMA(()),) * 6 + (pltpu.HBM(pair.shape, pair.dtype), pltpu.HBM(land.shape, land.dtype),
                                                     jax.ShapeDtypeStruct((8, 128), F32))
    out_specs = (SEM_SPEC,) * 6 + (HBM_SPEC, HBM_SPEC, VMEM_SPEC)
    args = [pltpu.with_memory_space_constraint(pair, pltpu.HBM),
            pltpu.with_memory_space_constraint(lax.empty(land.shape, land.dtype), pltpu.HBM)]
    aliases = {0: 6, 1: 7}
    if carry is not None:
        out_shape += (pltpu.HBM(carry.shape, carry.dtype),)
        out_specs += (HBM_SPEC,)
        args.append(pltpu.with_memory_space_constraint(carry, pltpu.HBM))
        aliases[2] = 9
    return pl.pallas_call(
        body, name=name, out_shape=out_shape, in_specs=(HBM_SPEC,) * len(args), out_specs=out_specs,
        input_output_aliases=aliases,
        compiler_params=pltpu.CompilerParams(has_side_effects=SIDE_EFFECT),
    )(*args)


def _scatter_wait(started, after, name):
    sems, pair_thru, land_thru = started[:6], started[6], started[7]

    def body(pair_ref, land_ref, *rest):
        sems = rest[:6]
        x, y, c, chips = _place()
        for k, (ox, oy) in enumerate(chips):
            copy = pltpu.make_async_remote_copy(src_ref=pair_ref.at[2 * ox + oy], dst_ref=land_ref.at[k], send_sem=sems[k],
                                                recv_sem=sems[3 + k], device_id=(ox, oy, c), device_id_type=MESH)
            copy.wait_send()
            copy.wait_recv()

    return pl.pallas_call(
        body, name=name,
        out_shape=(pltpu.HBM(pair_thru.shape, pair_thru.dtype), pltpu.HBM(land_thru.shape, land_thru.dtype)),
        in_specs=(HBM_SPEC, HBM_SPEC) + (SEM_SPEC,) * 6 + (pl.BlockSpec(memory_space=pl.ANY),),
        out_specs=(HBM_SPEC, HBM_SPEC), input_output_aliases={0: 0, 1: 1},
        compiler_params=pltpu.CompilerParams(has_side_effects=SIDE_EFFECT),
    )(pair_thru, land_thru, *sems, after)


def _scatter_ici(pair, recv, geo, p0, p1, pieces):
    pr = geo.hr // pieces
    rows = pl.ds(p0 * pr, (p1 - p0) * pr)

    def copies(ins, outs):
        x, y, c, chips = _place()
        return [(ins[0].at[2 * ox + oy, rows, :], outs[0].at[k, rows, :], outs[0].at[k, rows, :], (ox, oy, c))
                for k, (ox, oy) in enumerate(chips)]

    out = jax.ShapeDtypeStruct((3, geo.hr, geo.hc), BF16)
    if recv is None:
        return dict(ins=[pair], outs=[out], aliases={}, n=3, copies=copies)
    return dict(ins=[pair, recv], outs=[out], aliases={1: 0}, n=3, copies=copies)


def _join_d2d(shard, geo, row0=0):
    def copies(ins, outs):
        x, y, c, _ = _place()
        mine = outs[0].at[pl.ds(pl.multiple_of(row0 + c * geo.hr, 8), geo.hr), :]
        other = outs[0].at[pl.ds(pl.multiple_of(row0 + (1 - c) * geo.hr, 8), geo.hr), :]
        return [(mine, mine, other, (x, y, 1 - c))]

    return dict(ins=[shard], outs=[_same(shard)], aliases={0: 0}, n=1, copies=copies)


def _add_pair(grad, got, geo, place, name):
    tr = _pick(geo.hr, ROW_TILES)
    nb = geo.hr // tr

    def body(place_ref, a_ref, b_ref, o_ref):
        o_ref[0] = (a_ref[...].astype(F32) + b_ref[0].astype(F32)).astype(BF16)

    if geo.kind == "col":
        own_map = lambda j, i, place_ref: (place_ref[1] * nb + i, j)
    else:
        own_map = lambda j, i, place_ref: ((2 * j + place_ref[1]) * nb + i, 0)
    spec = pl.BlockSpec((1, tr, geo.hc), lambda j, i, place_ref: (j, i, 0))
    return pl.pallas_call(
        body, name=name, out_shape=jax.ShapeDtypeStruct((N_CHIPS, geo.hr, geo.hc), BF16),
        grid_spec=pltpu.PrefetchScalarGridSpec(
            num_scalar_prefetch=1, grid=(N_CHIPS, nb),
            in_specs=[pl.BlockSpec((tr, geo.hc), own_map), spec], out_specs=spec),
        compiler_params=_params(2),
    )(place, grad, got)


def _add_four(pair, recv, geo, place, name, rows=None, row0=0, into=None):
    tr = _pick(geo.hr, ROW_TILES)
    nb = geo.hr // tr
    rows = 2 * geo.hr if rows is None else rows

    def body(place_ref, p_ref, r_ref, *rest):
        rest[-1][...] = ((p_ref[0].astype(F32) + r_ref[0].astype(F32)) + r_ref[1].astype(F32)) + r_ref[2].astype(F32)

    in_specs = [pl.BlockSpec((1, tr, geo.hc), lambda i, place_ref: (place_ref[0], i, 0)),
                pl.BlockSpec((3, tr, geo.hc), lambda i, place_ref: (0, i, 0))]
    args = [place, pair, recv]
    if into is not None:
        in_specs.append(pl.BlockSpec(memory_space=pl.ANY))
        args.append(into)
    return pl.pallas_call(
        body, name=name, out_shape=jax.ShapeDtypeStruct((rows, geo.hc), F32),
        grid_spec=pltpu.PrefetchScalarGridSpec(
            num_scalar_prefetch=1, grid=(nb,), in_specs=in_specs,
            out_specs=pl.BlockSpec((tr, geo.hc), lambda i, place_ref: (row0 // tr + place_ref[1] * nb + i, 0))),
        input_output_aliases={3: 0} if into is not None else {},
        compiler_params=_params(1),
    )(*args)


PIECES = (4, 1, 16, 8) + (1,) * IN_CHUNKS
SCHEDULE = {
    "proj_fwd": (("gather_ici", W_OUT, 0, 1), ("gather_ici", W_UP, 0, 6)),
    "hgrn_fwd": (("gather_relay", W_OUT, 0, 1), ("gather_relay", W_UP, 0, 6), ("gather_ici", W_UP, 6, 12)),
    "attn_fwd_d1": (("gather_relay", W_UP, 6, 12),),
    "attn_fwd_d4": (("gather_ici", W_UP, 12, 16), ("gather_d2d", W_OUT)),
    "attn_fwd_d16": (("gather_relay", W_UP, 12, 16), ("gather_ici", W_DOWN, 0, 2)),
    "mix_fwd": (("gather_d2d", W_UP), ("gather_ici", W_DOWN, 2, 4)),
    "up_fwd": (("gather_ici", W_DOWN, 4, 8), ("gather_relay", W_DOWN, 0, 4)),
    "conv_gate_fwd_a": (("gather_relay", W_DOWN, 4, 8),),
    "conv_gate_fwd_b": (("gather_d2d", W_DOWN),),
    "down_bwd_x": (("swap", W_DOWN),),
    "conv_gate_bwd": (("scatter", W_DOWN, 0, 4),),
    "up_bwd_w": (("scatter", W_DOWN, 4, 8),),
    "up_bwd_x": (("swap", W_UP),),
    "norm2_bwd": (("scatter", W_UP, 0, 1),),
    "mix_bwd_w": (("scatter", W_UP, 1, 2),),
    "mix_bwd_x": (("swap", W_OUT), ("scatter", W_UP, 2, 3)),
    "hgrn_bwd": (("scatter", W_UP, 3, 9), ("join", W_DOWN)),
    "attn_bwd_d1": (("scatter", W_UP, 9, 12),),
    "attn_bwd_d4": (("scatter", W_OUT, 0, 1),),
    "attn_bwd_d16": (("scatter", W_UP, 12, 16),),
    "proj_bwd_w_0": (("join", W_UP), ("join", W_OUT)),
    "proj_bwd_w_1": (("swap", W_INQ),),
    "proj_bwd_w_2": (("swap", W_INQ + 1),),
    "proj_bwd_w_3": (("swap", W_INQ + 2),),
    "proj_bwd_x": (("swap", W_INQ + 3), ("scatter", W_INQ + 1, 0, 1), ("scatter", W_INQ + 2, 0, 1)),
    "gather_stats": (("join", W_INQ), ("join", W_INQ + 1), ("join", W_INQ + 2)),
    "grad_in_join": (("join", W_INQ + 3),),
}


class _Net:
    def __init__(self, shards, place):
        self.place = place
        self.geos = [_Geo(k, s.shape) for k, s in zip(WEIGHT_KINDS, shards)]
        self.full = [_cast_into_full(s, g, place, f"cast_shard_{w}") for w, (s, g) in enumerate(zip(shards, self.geos))]
        self.full[W_IN] = _gather_weight(self.full[W_IN], self.geos[W_IN], PIECES[W_IN], "gather_w_in")
        rows, cols = shards[W_IN].shape
        self.geos += [_Geo("col", (rows // IN_CHUNKS, cols))] * IN_CHUNKS
        n = NW + IN_CHUNKS
        self.grad, self.pair, self.recv, self.shard = [None] * n, [None] * n, [None] * n, [None] * n
        self.in_rows, self.in_shard = rows, None
        self.when_joined, self.joined = {}, {}
        self.flying = None
        self.carry = None
        self.slots = []

    def _row0(self, w):
        return (w - W_INQ) * 2 * self.geos[w].hr

    def host(self, name):
        phase = _Phase()
        self.slots = []
        groups = {}
        for item in SCHEDULE.get(name, ()):
            kind, w = item[0], item[1]
            geo = self.geos[w]
            key = len(groups)
            if kind == "gather_ici":
                spec, key = _gather_ici(self.full[w], geo, item[2], item[3], PIECES[w]), ("full", w)
            elif kind == "gather_relay":
                spec, key = _gather_relay(self.full[w], geo, item[2], item[3], PIECES[w]), ("full", w)
            elif kind == "gather_d2d":
                spec, key = _gather_d2d(self.full[w], geo), ("full", w)
            elif kind == "swap":
                spec = _swap_d2d(self.grad[w], geo)
            elif kind == "scatter":
                spec, key = _scatter_ici(self.pair[w], self.recv[w], geo, item[2], item[3], PIECES[w]), ("recv", w)
            elif w >= W_INQ:
                spec, key = _join_d2d(self.in_shard, geo, self._row0(w)), "in_shard"
            else:
                spec = _join_d2d(self.shard[w], geo)
            groups.setdefault(key, []).append((item, spec))
        for group in groups.values():
            specs = [s for _, s in group]
            merged = dict(specs[0], n=sum(s["n"] for s in specs),
                          copies=lambda ins, outs, specs=specs: [t for s in specs for t in s["copies"](ins, outs)])
            self.slots.append(([item for item, _ in group], phase.add(**merged)))
        return phase

    def done(self, name, comm, result=None):
        for items, sl in sorted(self.slots, key=lambda s: s[0][0][0] == "scatter"):
            out = comm[sl][0]
            for item in items:
                kind, w = item[0], item[1]
                if kind in ("gather_ici", "gather_relay", "gather_d2d"):
                    self.full[w] = out
                elif kind == "swap":
                    self.pair[w] = _add_pair(self.grad[w], out, self.geos[w], self.place, f"grad_pair_sum_{w}")
                    if w == W_INQ:
                        self.flying = _scatter_start(self.pair[w], self.geos[w], "grad_in_scatter0_start", self.carry)
                        self.carry = self.flying[9]
                elif kind == "scatter":
                    self.recv[w] = out
                    if item[3] == PIECES[w] and w >= W_INQ:
                        self.in_shard = _add_four(self.pair[w], out, self.geos[w], self.place, f"grad_chip_sum_{w}",
                                                  rows=self.in_rows, row0=self._row0(w), into=self.in_shard)
                    elif item[3] == PIECES[w]:
                        self.shard[w] = _add_four(self.pair[w], out, self.geos[w], self.place, f"grad_chip_sum_{w}")
                elif w >= W_INQ:
                    self.in_shard = out
                else:
                    self.shard[w] = out
                    if w in self.when_joined:
                        self.joined[w] = self.when_joined[w](out)
        if name == f"proj_bwd_w_{IN_CHUNKS - 1}" and self.flying is not None:
            after = result if result is not None else self.grad[W_INQ + 1]
            pair, recv = _scatter_wait(self.flying, after, "grad_in_scatter0_wait")
            self.flying = None
            self.in_shard = _add_four(pair, recv, self.geos[W_INQ], self.place, f"grad_chip_sum_{W_INQ}",
                                      rows=self.in_rows, row0=0, into=self.in_shard)

    def alone(self, name):
        self.done(name, _comm_only(name, self.host(name)))


CONVW_SHARD = 3 * DFF // N_CHIPS
COND_PAD = 8 * 896
SMALL_SIZES = (("norm1_w", D), ("lb", HW), ("hg_norm_w", DH), ("q_norm_w", DH), ("k_norm_w", DH),
               ("norm2_w", D), ("conv_b", DFF), ("conv_w", 3 * DFF), ("loss", 128))
SMALL_TOTAL = sum(n for _, n in SMALL_SIZES)
STATS_PAD = 8 * 5120


def kernel(x, c, w_ada, b_ada, norm1_w, w_in, lb_logits, hg_norm_w, q_norm_w, k_norm_w, w_out, norm2_w, w_up, conv_w, conv_b, w_down, loss_target, m_w_ada, m_b_ada, m_norm1_w, m_w_in, m_lb_logits, m_hg_norm_w, m_q_norm_w, m_k_norm_w, m_w_out, m_norm2_w, m_w_up, m_conv_w, m_conv_b, m_w_down, v_w_ada, v_b_ada, v_norm1_w, v_w_in, v_lb_logits, v_hg_norm_w, v_q_norm_w, v_k_norm_w, v_w_out, v_norm2_w, v_w_up, v_conv_w, v_conv_b, v_w_down):
    chip = 2 * lax.axis_index("x") + lax.axis_index("y")
    dev = 2 * chip + lax.axis_index("c")

    cond = jnp.concatenate([c, conv_w[0].reshape(1, CONVW_SHARD), jnp.zeros((1, COND_PAD - D - CONVW_SHARD), F32)], axis=1)
    cond_all = _all_gather_rows(cond.reshape(8, COND_PAD // 8), "gather_cond").reshape(N_DEV, COND_PAD)
    c_all = cond_all[:, :D]
    conv_w_full = jnp.concatenate(
        [cond_all[2 * j, D:D + CONVW_SHARD].reshape(3, DFF // N_CHIPS) for j in range(N_CHIPS)], axis=1)

    b_shard = lax.dynamic_slice_in_dim(b_ada, chip * ADA_COLS, ADA_COLS, axis=1)
    mod_cols = _all_gather_rows(_ada_fwd(c_all, w_ada[0], b_shard), "gather_mod")
    mod_all = jnp.concatenate([mod_cols[16 * j:16 * j + 8] for j in range(N_CHIPS)], axis=1)
    mod = lax.dynamic_slice_in_dim(mod_all, dev, 1, axis=0)

    place = jnp.stack([chip, lax.axis_index("c")]).astype(jnp.int32)
    net = _Net([w_in[0], w_out[0], w_up[0], w_down[0]], place)
    net.when_joined = {
        W_OUT: lambda grad: _adamw_sc(w_out[0], grad, m_w_out[0], v_w_out[0], "adamw_sc_w_out"),
        W_DOWN: lambda grad: _adamw_sc(w_down[0], grad, m_w_down[0], v_w_down[0], "adamw_sc_w_down"),
        W_UP: lambda grad: _adamw_sc(w_up[0], grad, m_w_up[0], v_w_up[0], "adamw_sc_w_up"),
    }
    early = {W_OUT: "w_out", W_DOWN: "w_down", W_UP: "w_up"}
    loss_row, grad_x, dmod, small = _sequence_step(
        x[0], loss_target[0], mod, norm1_w, lb_logits, hg_norm_w, q_norm_w, k_norm_w, norm2_w, conv_w_full, conv_b, net)
    small["conv_w"] = small["conv_w"].reshape(1, 3 * DFF)
    small["loss"] = loss_row
    stats = jnp.concatenate([dmod] + [small[k] for k, _ in SMALL_SIZES]
                            + [jnp.zeros((1, STATS_PAD - 6 * D - SMALL_TOTAL), F32)], axis=1)
    stats_all, comm = _all_gather_rows(stats.reshape(8, STATS_PAD // 8), "gather_stats", net.host("gather_stats"))
    net.done("gather_stats", comm)
    stats_all = stats_all.reshape(N_DEV, STATS_PAD)
    last = W_INQ + IN_CHUNKS - 1
    started = _scatter_start(net.pair[last], net.geos[last], "grad_in_scatter_start")
    dmod_all = stats_all[:, :6 * D] + started[8][0, 0]
    sums = _sum_rows(stats_all[:, 6 * D:6 * D + SMALL_TOTAL], "small_grad_sum")
    g_small, off = {}, 0
    for k, n in SMALL_SIZES:
        g_small[k] = sums[:, off:off + n]
        off += n
    loss = g_small["loss"][0, 0]
    g_b_ada = _sum_rows(dmod_all, "b_ada_grad_sum")
    ada = _ada_bwd_adamw(c_all, lax.dynamic_slice_in_dim(dmod_all, chip * ADA_COLS, ADA_COLS, axis=1),
                         w_ada[0], m_w_ada[0], v_w_ada[0])
    g_w_ada = ada[0]
    pair_last, recv_last = _scatter_wait(started, ada[3], "grad_in_scatter_wait")
    net.in_shard = _add_four(pair_last, recv_last, net.geos[last], place, f"grad_chip_sum_{last}",
                             rows=net.in_rows, row0=net._row0(last), into=net.in_shard)
    net.alone("grad_in_join")
    g_out, g_up, g_down = net.shard[W_OUT], net.shard[W_UP], net.shard[W_DOWN]
    g_in = net.in_shard
    g_lb = _lb_grad(lb_logits, g_small["lb"])
    g_conv_w = lax.dynamic_slice_in_dim(g_small["conv_w"].reshape(3, DFF), chip * (DFF // N_CHIPS), DFF // N_CHIPS, axis=1)

    names = ["w_ada", "b_ada", "norm1_w", "w_in", "lb_logits", "hg_norm_w", "q_norm_w", "k_norm_w", "w_out",
             "norm2_w", "w_up", "conv_w", "conv_b", "w_down"]
    lead = {"w_ada", "w_in", "w_out", "w_up", "conv_w", "w_down"}
    w = dict(w_ada=w_ada, b_ada=b_ada, norm1_w=norm1_w, w_in=w_in, lb_logits=lb_logits, hg_norm_w=hg_norm_w,
             q_norm_w=q_norm_w, k_norm_w=k_norm_w, w_out=w_out, norm2_w=norm2_w, w_up=w_up, conv_w=conv_w,
             conv_b=conv_b, w_down=w_down)
    m = dict(w_ada=m_w_ada, b_ada=m_b_ada, norm1_w=m_norm1_w, w_in=m_w_in, lb_logits=m_lb_logits,
             hg_norm_w=m_hg_norm_w, q_norm_w=m_q_norm_w, k_norm_w=m_k_norm_w, w_out=m_w_out, norm2_w=m_norm2_w,
             w_up=m_w_up, conv_w=m_conv_w, conv_b=m_conv_b, w_down=m_w_down)
    v = dict(w_ada=v_w_ada, b_ada=v_b_ada, norm1_w=v_norm1_w, w_in=v_w_in, lb_logits=v_lb_logits,
             hg_norm_w=v_hg_norm_w, q_norm_w=v_q_norm_w, k_norm_w=v_k_norm_w, w_out=v_w_out, norm2_w=v_norm2_w,
             w_up=v_w_up, conv_w=v_conv_w, conv_b=v_conv_b, w_down=v_w_down)
    g = dict(w_ada=g_w_ada, b_ada=g_b_ada, norm1_w=g_small["norm1_w"], w_in=g_in, lb_logits=g_lb,
             hg_norm_w=g_small["hg_norm_w"], q_norm_w=g_small["q_norm_w"], k_norm_w=g_small["k_norm_w"], w_out=g_out,
             norm2_w=g_small["norm2_w"], w_up=g_up, conv_w=g_conv_w, conv_b=g_small["conv_b"], w_down=g_down)

    updated = {early[i]: res for i, res in net.joined.items()}
    updated["w_ada"] = (ada[1], ada[2], ada[3], ada[0])
    grads, deltas, new_m, new_v = [], [], [], []
    for n in names:
        strip = (lambda t: t[0]) if n in lead else (lambda t: t)
        wrap = (lambda t: t[None]) if n in lead else (lambda t: t)
        g_n = g[n]
        if n in updated:
            d_n, m_n, v_n, g_n = updated[n]
        elif n == "w_in":
            d_n, m_n, v_n, g_n = _adamw(strip(w[n]), g_n, strip(m[n]), strip(v[n]), f"adamw_{n}", copy_grad=True)
        else:
            d_n, m_n, v_n = _adamw(strip(w[n]), g_n, strip(m[n]), strip(v[n]), f"adamw_{n}")
        grads.append(wrap(g_n))
        deltas.append(wrap(d_n))
        new_m.append(wrap(m_n))
        new_v.append(wrap(v_n))
    return (loss, grad_x[None], *grads, *deltas, *new_m, *new_v)
```

```python
import functools

import jax
import jax.numpy as jnp
from jax import lax
from jax.experimental import pallas as pl
from jax.experimental.pallas import tpu as pltpu
from jax.experimental.pallas import tpu_sc as plsc

F32 = jnp.float32
BF16 = jnp.bfloat16

S = 2048
D = 2048
H = 8
DH = 128
HW = H * DH
CH = 64
NCH = S // CH
DFF = 5632
INC = 7 * HW
BLK = 128
DILS = (1, 4, 16)
EPS = 1e-6
NEG = -1e30
N_CHIPS = 4
N_DEV = 8

ADAM_LR = 0.001
ADAM_B1 = 0.9
ADAM_B2 = 0.999
ADAM_EPS = 1e-08
ADAM_WD = 0.01
ADAM_STEP = 10
ADAM_BLOCK_BYTES = 1 << 21

V7X_VMEM_BYTES = 64 * 1024 * 1024
VMEM_LIMIT = (V7X_VMEM_BYTES * 3) // 4
MESH = pl.DeviceIdType.MESH
HBM_SPEC = pl.BlockSpec(memory_space=pltpu.HBM)
VMEM_SPEC = pl.BlockSpec(memory_space=pltpu.VMEM)

NN = (((1,), (0,)), ((), ()))
NT = (((1,), (1,)), ((), ()))
TN = (((0,), (0,)), ((), ()))


def _params(n_grid):
    return pltpu.CompilerParams(dimension_semantics=("arbitrary",) * n_grid, vmem_limit_bytes=VMEM_LIMIT)


def _dot(a, b, dims=NN):
    return lax.dot_general(a.astype(BF16), b.astype(BF16), dims, preferred_element_type=F32)


def _dot_f32(a, b):
    return lax.dot_general(a, b, NN, precision=lax.Precision.HIGHEST, preferred_element_type=F32)


def _sigmoid(x):
    return 1.0 / (1.0 + jnp.exp(-x))


def _pick(n, cands):
    for t in cands:
        if n % t == 0:
            return t
    raise ValueError(n)


def _matmul(a, b, mode, out_dtype, name, phase=None, m_blocks=None):
    if mode == "nn":
        (m, k), (_, n) = a.shape, b.shape
    elif mode == "nt":
        (m, k), (n, _) = a.shape, b.shape
    else:
        (k, m), (_, n) = a.shape, b.shape
    tm = _pick(m, (1024, 1408, 512))
    a_block = lambda i: i
    if m_blocks is not None:
        tm, blocks = m_blocks
        m = tm * len(blocks)
        step = blocks[1] - blocks[0] if len(blocks) > 1 else 0
        assert all(blk == blocks[0] + step * i for i, blk in enumerate(blocks))
        a_block = lambda i: blocks[0] + step * i
    tn = _pick(n, (1024, 1408, 512))
    tk = _pick(k, (2048, 2816, 1792, 1024, 512))
    nk = k // tk
    dims = {"nn": NN, "nt": NT, "tn": TN}[mode]

    def body(a_ref, b_ref, o_ref, *acc):
        part = lax.dot_general(a_ref[...], b_ref[...], dims, preferred_element_type=F32)
        if nk == 1:
            o_ref[...] = part.astype(o_ref.dtype)
            return
        acc_ref, kk = acc[0], pl.program_id(2)

        @pl.when(kk == 0)
        def _():
            acc_ref[...] = part

        @pl.when(jnp.logical_and(kk > 0, kk < nk - 1))
        def _():
            acc_ref[...] += part

        @pl.when(kk == nk - 1)
        def _():
            o_ref[...] = (acc_ref[...] + part).astype(o_ref.dtype)

    if mode == "tn":
        a_spec = pl.BlockSpec((tk, tm), lambda i, j, kk: (kk, a_block(i)))
    else:
        a_spec = pl.BlockSpec((tm, tk), lambda i, j, kk: (i, kk))
    if mode == "nt":
        b_spec = pl.BlockSpec((tn, tk), lambda i, j, kk: (j, kk))
    else:
        b_spec = pl.BlockSpec((tk, tn), lambda i, j, kk: (kk, j))
    return _pcall(
        body, [a, b], phase, name=name,
        out_shape=jax.ShapeDtypeStruct((m, n), out_dtype),
        grid=(m // tm, n // tn, nk),
        in_specs=[a_spec, b_spec],
        out_specs=pl.BlockSpec((tm, tn), lambda i, j, kk: (i, j)),
        scratch_shapes=[pltpu.VMEM((tm, tn), F32)] if nk > 1 else [],
        compiler_params=_params(3),
    )


TR = 256


def _row_spec(cols=D):
    return pl.BlockSpec((TR, cols), lambda i: (i, 0))


def _vec_spec(cols=D):
    return pl.BlockSpec((1, cols), lambda i: (0, 0))


def _norm_mod(x, nw, scale, shift, name):
    def body(x_ref, nw_ref, sc_ref, sh_ref, h_ref):
        xv = x_ref[...]
        r = lax.rsqrt(jnp.mean(xv * xv, axis=-1, keepdims=True) + EPS)
        h_ref[...] = ((xv * r) * nw_ref[...] * (1.0 + sc_ref[...]) + sh_ref[...]).astype(BF16)

    return pl.pallas_call(
        body, name=name, out_shape=jax.ShapeDtypeStruct((S, D), BF16), grid=(S // TR,),
        in_specs=[_row_spec(), _vec_spec(), _vec_spec(), _vec_spec()], out_specs=_row_spec(),
        compiler_params=_params(1),
    )(x, nw, scale, shift)


def _resid_norm_mod(x, mix, gate, nw, scale, shift, name):
    def body(x_ref, mix_ref, g_ref, nw_ref, sc_ref, sh_ref, x1_ref, h_ref):
        xv = x_ref[...] + g_ref[...] * mix_ref[...]
        x1_ref[...] = xv
        r = lax.rsqrt(jnp.mean(xv * xv, axis=-1, keepdims=True) + EPS)
        h_ref[...] = ((xv * r) * nw_ref[...] * (1.0 + sc_ref[...]) + sh_ref[...]).astype(BF16)

    return pl.pallas_call(
        body, name=name,
        out_shape=(jax.ShapeDtypeStruct((S, D), F32), jax.ShapeDtypeStruct((S, D), BF16)), grid=(S // TR,),
        in_specs=[_row_spec(), _row_spec(), _vec_spec(), _vec_spec(), _vec_spec(), _vec_spec()],
        out_specs=(_row_spec(), _row_spec()),
        compiler_params=_params(1),
    )(x, mix, gate, nw, scale, shift)


def _norm_mod_bwd(dh, xin, nw, scale, dres, name, mix=None, gate=None, phase=None):
    with_gate = mix is not None

    def body(*refs):
        if with_gate:
            dh_ref, x_ref, nw_ref, sc_ref, dres_ref, mix_ref, g_ref, dx_ref, dsh_ref, dsc_ref, dnw_ref, dg_ref, dmix_ref = refs
        else:
            dh_ref, x_ref, nw_ref, sc_ref, dres_ref, dx_ref, dsh_ref, dsc_ref, dnw_ref = refs
        i = pl.program_id(0)
        dhv = dh_ref[...]
        xv = x_ref[...]
        r = lax.rsqrt(jnp.mean(xv * xv, axis=-1, keepdims=True) + EPS)
        xn = xv * r
        nwv = nw_ref[...]
        one_sc = 1.0 + sc_ref[...]
        dxn = dhv * nwv * one_sc
        dx = dres_ref[...] + r * (dxn - xn * jnp.mean(dxn * xn, axis=-1, keepdims=True))
        dx_ref[...] = dx

        @pl.when(i == 0)
        def _():
            dsh_ref[...] = jnp.zeros_like(dsh_ref)
            dsc_ref[...] = jnp.zeros_like(dsc_ref)
            dnw_ref[...] = jnp.zeros_like(dnw_ref)
            if with_gate:
                dg_ref[...] = jnp.zeros_like(dg_ref)

        dsh_ref[...] += jnp.sum(dhv, axis=0, keepdims=True)
        dsc_ref[...] += jnp.sum(dhv * xn * nwv, axis=0, keepdims=True)
        dnw_ref[...] += jnp.sum(dhv * xn * one_sc, axis=0, keepdims=True)
        if with_gate:
            dg_ref[...] += jnp.sum(dx * mix_ref[...], axis=0, keepdims=True)
            dmix_ref[...] = (dx * g_ref[...]).astype(BF16)

    vec = jax.ShapeDtypeStruct((1, D), F32)
    ins = [dh, xin, nw, scale, dres]
    in_specs = [_row_spec(), _row_spec(), _vec_spec(), _vec_spec(), _row_spec()]
    outs = [jax.ShapeDtypeStruct((S, D), F32), vec, vec, vec]
    out_specs = [_row_spec(), _vec_spec(), _vec_spec(), _vec_spec()]
    if with_gate:
        ins += [mix, gate]
        in_specs += [_row_spec(), _vec_spec()]
        outs += [vec, jax.ShapeDtypeStruct((S, D), BF16)]
        out_specs += [_vec_spec(), _row_spec()]
    return _pcall(
        body, ins, phase, name=name, out_shape=tuple(outs), grid=(S // TR,), in_specs=in_specs,
        out_specs=tuple(out_specs), compiler_params=_params(1),
    )


def _tri(lower):
    row = lax.broadcasted_iota(jnp.int32, (CH, CH), 0)
    col = lax.broadcasted_iota(jnp.int32, (CH, CH), 1)
    return (row >= col) if lower else (col >= row)


def _hgrn_gates(hq, hf, lb):
    sig = _sigmoid(hf)
    f = lb + (1.0 - lb) * sig
    g = jnp.log(f)
    sq = _sigmoid(hq)
    return sig, f, g, 1.0 - f, hq * sq, sq


HG_HP = 2
HG_UNROLL = 8


def _proj_col_spec(group):
    return pl.BlockSpec((S, HG_HP * DH), lambda h: (0, group * (H // HG_HP) + h))


def _hgrn_fwd(proj, lb_logits, norm_w, phase=None):
    def body(hq_ref, hf_ref, hi_ref, hg_ref, lbl_ref, nw_ref, out_ref, oraw_ref, st_ref, s_ref):
        nw = nw_ref[...]
        lower = _tri(True)
        ltri = lower.astype(F32)
        s_ref[...] = jnp.zeros_like(s_ref)

        def chunk(n, carry):
            rows = pl.ds(pl.multiple_of(n * CH, CH), CH)
            for j in range(HG_HP):
                cols = slice(j * DH, (j + 1) * DH)
                lb = 1.0 / (1.0 + jnp.exp(lbl_ref[1:2, cols] - lbl_ref[0:1, cols]))
                hq, hf, v, hg = hq_ref[rows, cols], hf_ref[rows, cols], hi_ref[rows, cols], hg_ref[rows, cols]
                _, _, g, kk, q, _ = _hgrn_gates(hq, hf, lb)
                gc = _dot_f32(ltri, g)
                gl = jnp.sum(g, axis=0, keepdims=True)
                qe = q * jnp.exp(gc)
                ke = kk * jnp.exp(gl - gc)
                qh = q * jnp.exp(gc - 0.5 * gl)
                kh = kk * jnp.exp(0.5 * gl - gc)
                st = s_ref[j]
                st_ref[j, pl.ds(n, 1)] = st[None]
                a = jnp.where(lower, _dot(qh, kh, NT), 0.0)
                o = _dot(qe, st, NT) + _dot(a, v)
                s_ref[j] = st * jnp.exp(gl) + _dot(v, ke, TN)
                oraw_ref[rows, cols] = o
                rs = lax.rsqrt(jnp.mean(o * o, axis=-1, keepdims=True) + EPS)
                out_ref[rows, cols] = (o * rs * nw * (hg * _sigmoid(hg))).astype(BF16)
            return carry

        lax.fori_loop(0, NCH, chunk, 0, unroll=HG_UNROLL)

    head_spec = pl.BlockSpec((S, HG_HP * DH), lambda h: (0, h))
    return _pcall(
        body, [proj, proj, proj, proj, lb_logits, norm_w], phase, name="hgrn_fwd",
        out_shape=(jax.ShapeDtypeStruct((S, 2 * HW), BF16), jax.ShapeDtypeStruct((S, HW), F32),
                   jax.ShapeDtypeStruct((H, NCH, DH, DH), F32)),
        grid=(H // HG_HP,),
        in_specs=[_proj_col_spec(0), _proj_col_spec(1), _proj_col_spec(2), _proj_col_spec(3),
                  pl.BlockSpec((2, HG_HP * DH), lambda h: (0, h)), pl.BlockSpec((1, DH), lambda h: (0, 0))],
        out_specs=(head_spec, head_spec, pl.BlockSpec((HG_HP, NCH, DH, DH), lambda h: (h, 0, 0, 0))),
        scratch_shapes=[pltpu.VMEM((HG_HP, DH, DH), F32)],
        compiler_params=_params(1),
    )


def _hgrn_bwd(proj, lb_logits, norm_w, oraw, states, dout, phase=None):
    def body(hq_ref, hf_ref, hi_ref, hg_ref, lbl_ref, nw_ref, oraw_ref, st_ref, do_ref,
             dhq_ref, dhf_ref, dhi_ref, dhg_ref, dlb_ref, dnw_ref, ds_ref, acc_ref):
        h = pl.program_id(0)
        nw = nw_ref[...]
        lower = _tri(True)
        ltri = lower.astype(F32)
        utri = _tri(False).astype(F32)
        last = lax.broadcasted_iota(jnp.int32, (CH, DH), 0) == CH - 1
        ds_ref[...] = jnp.zeros_like(ds_ref)
        acc_ref[...] = jnp.zeros_like(acc_ref)

        @pl.when(h == 0)
        def _():
            dnw_ref[...] = jnp.zeros_like(dnw_ref)

        def chunk(i, carry):
            n = NCH - 1 - i
            rows = pl.ds(pl.multiple_of(n * CH, CH), CH)
            for j in range(HG_HP):
                cols = slice(j * DH, (j + 1) * DH)
                lb = 1.0 / (1.0 + jnp.exp(lbl_ref[1:2, cols] - lbl_ref[0:1, cols]))
                hq, hf, v, hg = hq_ref[rows, cols], hf_ref[rows, cols], hi_ref[rows, cols], hg_ref[rows, cols]
                sig, f, g, kk, q, sq = _hgrn_gates(hq, hf, lb)
                gc = _dot_f32(ltri, g)
                gl = jnp.sum(g, axis=0, keepdims=True)
                eg = jnp.exp(gc)
                ek = jnp.exp(gl - gc)
                gam = jnp.exp(gl)
                ph = jnp.exp(gc - 0.5 * gl)
                pk = jnp.exp(0.5 * gl - gc)
                qe, ke = q * eg, kk * ek
                qh, kh = q * ph, kk * pk
                st = st_ref[j, pl.ds(n, 1)][0]
                dst = ds_ref[j]
                a = jnp.where(lower, _dot(qh, kh, NT), 0.0)
                o = oraw_ref[rows, cols]
                rs = lax.rsqrt(jnp.mean(o * o, axis=-1, keepdims=True) + EPS)
                xh = o * rs
                sg = _sigmoid(hg)
                dgo = do_ref[rows, cols]
                d_on = dgo * (hg * sg)
                dhg_ref[rows, cols] = (dgo * xh * nw * (sg * (1.0 + hg * (1.0 - sg)))).astype(BF16)
                acc_ref[j, 1:2, :] += jnp.sum(d_on * xh, axis=0, keepdims=True)
                dy = d_on * nw
                do = rs * (dy - xh * jnp.mean(dy * xh, axis=-1, keepdims=True))
                dqe = _dot(do, st)
                da = jnp.where(lower, _dot(do, v, NT), 0.0)
                dv = _dot(a, do, TN) + _dot(ke, dst, NT)
                dke = _dot(v, dst)
                dgam = jnp.sum(dst * st, axis=0, keepdims=True)
                dqh = lax.dot_general(da, kh, NN, precision=lax.Precision.HIGH, preferred_element_type=F32)
                dkh = lax.dot_general(da, qh, TN, precision=lax.Precision.HIGH, preferred_element_type=F32)
                dq = dqh * ph + dqe * eg
                dk = dkh * pk + dke * ek
                dgl = jnp.sum(dke * ke, axis=0, keepdims=True) + dgam * gam
                dgc = dqh * qh - dkh * kh + dqe * qe - dke * ke + jnp.where(last, dgl, 0.0)
                dg = _dot_f32(utri, dgc)
                df = dg / f - dk
                dhf_ref[rows, cols] = (df * (1.0 - lb) * sig * (1.0 - sig)).astype(BF16)
                acc_ref[j, 0:1, :] += jnp.sum(df * (1.0 - sig), axis=0, keepdims=True)
                dhq_ref[rows, cols] = (dq * (sq * (1.0 + hq * (1.0 - sq)))).astype(BF16)
                dhi_ref[rows, cols] = dv.astype(BF16)
                ds_ref[j] = dst * gam + _dot(do, qe, TN)
            return carry

        lax.fori_loop(0, NCH, chunk, 0, unroll=HG_UNROLL)
        for j in range(HG_HP):
            dlb_ref[:, j * DH:(j + 1) * DH] = acc_ref[j, 0:1, :]
            dnw_ref[...] += acc_ref[j, 1:2, :]

    head_spec = pl.BlockSpec((S, HG_HP * DH), lambda h: (0, h))
    head_out = jax.ShapeDtypeStruct((S, HW), BF16)
    return _pcall(
        body, [proj, proj, proj, proj, lb_logits, norm_w, oraw, states, dout], phase, name="hgrn_bwd",
        out_shape=(head_out, head_out, head_out, head_out,
                   jax.ShapeDtypeStruct((1, HW), F32), jax.ShapeDtypeStruct((1, DH), F32)),
        grid=(H // HG_HP,),
        in_specs=[_proj_col_spec(0), _proj_col_spec(1), _proj_col_spec(2), _proj_col_spec(3),
                  pl.BlockSpec((2, HG_HP * DH), lambda h: (0, h)), pl.BlockSpec((1, DH), lambda h: (0, 0)),
                  head_spec, pl.BlockSpec((HG_HP, NCH, DH, DH), lambda h: (h, 0, 0, 0)), head_spec],
        out_specs=(head_spec, head_spec, head_spec, head_spec,
                   pl.BlockSpec((1, HG_HP * DH), lambda h: (0, h)), pl.BlockSpec((1, DH), lambda h: (0, 0))),
        scratch_shapes=[pltpu.VMEM((HG_HP, DH, DH), F32), pltpu.VMEM((HG_HP, 8, DH), F32)],
        compiler_params=_params(1),
    )


ATT_SCALE = DH ** -0.5
HEADS_PER_STEP = {1: 8, 4: 1, 16: 1}


def _sub_rows(ref, r, dil, cols):
    if dil == 1:
        return ref[:, cols]
    return ref[pl.ds(r, BLK, stride=dil), cols]


def _set_sub_rows(ref, r, dil, cols, val):
    if dil == 1:
        ref[:, cols] = val
    else:
        ref[pl.ds(r, BLK, stride=dil), cols] = val


def _slopes(dil):
    hp = HEADS_PER_STEP[dil]
    s = jnp.asarray([dil * 2.0 ** (-(h + 1)) for h in range(H)], F32)
    return jnp.broadcast_to(s[:, None], (H, DH)).reshape(H // hp, hp, DH)


def _rms_rows(x):
    rs = lax.rsqrt(jnp.mean(x * x, axis=-1, keepdims=True) + EPS)
    return x * rs, rs


def _att_masks():
    qi = lax.broadcasted_iota(jnp.int32, (BLK, BLK), 0)
    kj = lax.broadcasted_iota(jnp.int32, (BLK, BLK), 1)
    steps_c = qi - kj
    steps_p = qi - kj + BLK
    return steps_c, steps_p, steps_c >= 0, steps_p <= BLK


def _qk_prep(proj, q_w, k_w):
    def body(q_ref, k_ref, qw_ref, kw_ref, qn_ref, kn_ref):
        for h in range(H):
            cols = slice(h * DH, (h + 1) * DH)
            qn_ref[:, cols] = _rms_rows(q_ref[:, cols])[0] * qw_ref[...]
            kn_ref[:, cols] = _rms_rows(k_ref[:, cols])[0] * kw_ref[...]

    out = jax.ShapeDtypeStruct((S, HW), F32)
    wspec = pl.BlockSpec((1, DH), lambda i: (0, 0))
    return pl.pallas_call(
        body, name="qk_prep", out_shape=(out, out), grid=(S // TR,),
        in_specs=[pl.BlockSpec((TR, HW), lambda i: (i, 4)), pl.BlockSpec((TR, HW), lambda i: (i, 5)), wspec, wspec],
        out_specs=(_row_spec(HW), _row_spec(HW)),
        compiler_params=_params(1),
    )(proj, proj, q_w, k_w)


def _qk_norm_bwd(proj, dqn, dkn, dv, q_w, k_w):
    def body(q_ref, k_ref, dqn_ref, dkn_ref, dv_ref, qw_ref, kw_ref, dq_ref, dk_ref, dvo_ref, dqw_ref, dkw_ref):
        i = pl.program_id(0)

        @pl.when(i == 0)
        def _():
            dqw_ref[...] = jnp.zeros_like(dqw_ref)
            dkw_ref[...] = jnp.zeros_like(dkw_ref)

        dvo_ref[...] = dv_ref[...].astype(BF16)
        for x_ref, d_ref, w_ref, o_ref, dw_ref in ((q_ref, dqn_ref, qw_ref, dq_ref, dqw_ref),
                                                   (k_ref, dkn_ref, kw_ref, dk_ref, dkw_ref)):
            for h in range(H):
                cols = slice(h * DH, (h + 1) * DH)
                xh, rs = _rms_rows(x_ref[:, cols])
                d = d_ref[:, cols]
                dw_ref[...] += jnp.sum(d * xh, axis=0, keepdims=True)
                dy = d * w_ref[...]
                o_ref[:, cols] = (rs * (dy - xh * jnp.mean(dy * xh, axis=-1, keepdims=True))).astype(BF16)

    big = jax.ShapeDtypeStruct((S, HW), BF16)
    small = jax.ShapeDtypeStruct((1, DH), F32)
    wspec = pl.BlockSpec((1, DH), lambda i: (0, 0))
    return pl.pallas_call(
        body, name="qk_norm_bwd", out_shape=(big, big, big, small, small), grid=(S // TR,),
        in_specs=[pl.BlockSpec((TR, HW), lambda i: (i, 4)), pl.BlockSpec((TR, HW), lambda i: (i, 5)),
                  _row_spec(HW), _row_spec(HW), _row_spec(HW), wspec, wspec],
        out_specs=(_row_spec(HW), _row_spec(HW), _row_spec(HW), wspec, wspec),
        compiler_params=_params(1),
    )(proj, proj, dqn, dkn, dv, q_w, k_w)


def _attn_delta(do, o):
    def body(do_ref, o_ref, d_ref):
        for h in range(H):
            cols = slice(h * DH, (h + 1) * DH)
            d_ref[:, cols] = jnp.broadcast_to(jnp.sum(do_ref[:, cols] * o_ref[:, cols], axis=-1, keepdims=True), (TR, DH))

    return pl.pallas_call(
        body, name="attn_delta", out_shape=jax.ShapeDtypeStruct((S, HW), F32), grid=(S // TR,),
        in_specs=[pl.BlockSpec((TR, HW), lambda i: (i, 1)), _row_spec(HW)], out_specs=_row_spec(HW),
        compiler_params=_params(1),
    )(do, o)


def _attn_fwd(proj, qn, kn, dil, phase=None):
    hp = HEADS_PER_STEP[dil]
    rows, ng, nb = BLK * dil, H // hp, S // (BLK * dil)

    def body(q_ref, kc_ref, kp_ref, vc_ref, vp_ref, sl_ref, o_ref, lse_ref):
        n = pl.program_id(0)
        steps_c, steps_p, ok_c, ok_p = _att_masks()
        ok_p = jnp.logical_and(ok_p, n > 0)
        fc, fp = steps_c.astype(F32), steps_p.astype(F32)
        for hh in range(hp):
            cols = slice(hh * DH, (hh + 1) * DH)
            sl = sl_ref[0, hh:hh + 1, :]
            for r in range(dil):
                sub = lambda ref: _sub_rows(ref, r, dil, cols)
                qv = sub(q_ref)
                sc = jnp.where(ok_c, _dot(qv, sub(kc_ref), NT) * ATT_SCALE - sl * fc, NEG)
                sp = jnp.where(ok_p, _dot(qv, sub(kp_ref), NT) * ATT_SCALE - sl * fp, NEG)
                m = jnp.maximum(jnp.max(sc, axis=-1, keepdims=True), jnp.max(sp, axis=-1, keepdims=True))
                pc, pp = jnp.exp(sc - m), jnp.exp(sp - m)
                den = jnp.sum(pc, axis=-1, keepdims=True) + jnp.sum(pp, axis=-1, keepdims=True)
                o = (_dot(pc, sub(vc_ref)) + _dot(pp, sub(vp_ref))) / den
                _set_sub_rows(o_ref, r, dil, cols, o)
                _set_sub_rows(lse_ref, r, dil, cols, jnp.broadcast_to(m + jnp.log(den), (BLK, DH)))

    cur = pl.BlockSpec((rows, hp * DH), lambda n, g: (n, g))
    prev = pl.BlockSpec((rows, hp * DH), lambda n, g: (jnp.maximum(n - 1, 0), g))
    vcur = pl.BlockSpec((rows, hp * DH), lambda n, g: (n, 6 * ng + g))
    vprev = pl.BlockSpec((rows, hp * DH), lambda n, g: (jnp.maximum(n - 1, 0), 6 * ng + g))
    out = jax.ShapeDtypeStruct((S, HW), F32)
    return _pcall(
        body, [qn, kn, kn, proj, proj, _slopes(dil)], phase,
        name=f"attn_fwd_d{dil}", out_shape=(out, out), grid=(nb, ng),
        in_specs=[cur, cur, prev, vcur, vprev, pl.BlockSpec((1, hp, DH), lambda n, g: (g, 0, 0))],
        out_specs=(cur, cur),
        compiler_params=_params(2),
    )


def _attn_merge(outs, lses, cat):
    def body(o1, o2, o3, l1, l2, l3, cat_ref, ob_ref, of_ref, lse_ref):
        a, b, c = l1[...], l2[...], l3[...]
        m = jnp.maximum(jnp.maximum(a, b), c)
        ea, eb, ec = jnp.exp(a - m), jnp.exp(b - m), jnp.exp(c - m)
        den = ea + eb + ec
        o = (ea * o1[...] + eb * o2[...] + ec * o3[...]) / den
        ob_ref[...] = o.astype(BF16)
        of_ref[...] = o
        lse_ref[...] = m + jnp.log(den)

    f = jax.ShapeDtypeStruct((S, HW), F32)
    return pl.pallas_call(
        body, name="attn_merge", out_shape=(jax.ShapeDtypeStruct((S, 2 * HW), BF16), f, f), grid=(S // TR,),
        in_specs=[_row_spec(HW)] * 6 + [pl.BlockSpec(memory_space=pl.ANY)],
        out_specs=(pl.BlockSpec((TR, HW), lambda i: (i, 1)), _row_spec(HW), _row_spec(HW)),
        input_output_aliases={6: 0},
        compiler_params=_params(1),
    )(*outs, *lses, cat)


def _attn_bwd(proj, qn, kn, lse, delta, do, dil, acc, phase=None):
    hp = HEADS_PER_STEP[dil]
    rows, ng, nb = BLK * dil, H // hp, S // (BLK * dil)
    has_acc = acc is not None

    def body(*refs):
        q_ref, qn_ref, kp_ref, kc_ref, vp_ref, vc_ref, l_ref, ln_ref, d_ref, dn_ref, do_ref, don_ref, sl_ref = refs[:13]
        refs = refs[13:]
        if has_acc:
            aq_ref, ak_ref, av_ref = refs[:3]
            refs = refs[3:]
        dq_ref, dk_ref, dv_ref = refs
        m = pl.program_id(0)
        steps_c, steps_p, ok_c, ok_p = _att_masks()
        ok_prev = jnp.logical_and(ok_p, m > 0)
        ok_next = jnp.logical_and(ok_p, m < nb - 1)
        fc, fp = steps_c.astype(F32), steps_p.astype(F32)
        for hh in range(hp):
            cols = slice(hh * DH, (hh + 1) * DH)
            sl = sl_ref[0, hh:hh + 1, :]
            for r in range(dil):
                sub = lambda ref: _sub_rows(ref, r, dil, cols)
                qv, qnv, kcv, kpv = sub(q_ref), sub(qn_ref), sub(kc_ref), sub(kp_ref)
                vc, vp = sub(vc_ref), sub(vp_ref)
                dov, donv = sub(do_ref), sub(don_ref)
                lse_m, lse_n = sub(l_ref)[:, 0:1], sub(ln_ref)[:, 0:1]
                delta_m, delta_n = sub(d_ref)[:, 0:1], sub(dn_ref)[:, 0:1]
                p_c = jnp.where(ok_c, jnp.exp(_dot(qv, kcv, NT) * ATT_SCALE - sl * fc - lse_m), 0.0)
                p_p = jnp.where(ok_prev, jnp.exp(_dot(qv, kpv, NT) * ATT_SCALE - sl * fp - lse_m), 0.0)
                p_n = jnp.where(ok_next, jnp.exp(_dot(qnv, kcv, NT) * ATT_SCALE - sl * fp - lse_n), 0.0)
                ds_c = p_c * (_dot(dov, vc, NT) - delta_m)
                ds_p = p_p * (_dot(dov, vp, NT) - delta_m)
                ds_n = p_n * (_dot(donv, vc, NT) - delta_n)
                dqn = (_dot(ds_c, kcv) + _dot(ds_p, kpv)) * ATT_SCALE
                dkn = (_dot(ds_c, qv, TN) + _dot(ds_n, qnv, TN)) * ATT_SCALE
                dv = _dot(p_c, dov, TN) + _dot(p_n, donv, TN)
                if has_acc:
                    dqn, dkn, dv = dqn + sub(aq_ref), dkn + sub(ak_ref), dv + sub(av_ref)
                _set_sub_rows(dq_ref, r, dil, cols, dqn)
                _set_sub_rows(dk_ref, r, dil, cols, dkn)
                _set_sub_rows(dv_ref, r, dil, cols, dv)

    def shifted(shift, m):
        if shift < 0:
            return jnp.maximum(m - 1, 0)
        if shift > 0:
            return jnp.minimum(m + 1, nb - 1)
        return m

    def spec(shift, group=0):
        return pl.BlockSpec((rows, hp * DH), lambda m, g: (shifted(shift, m), group * ng + g))

    ins = [qn, qn, kn, kn, proj, proj, lse, lse, delta, delta, do, do, _slopes(dil)]
    in_specs = [spec(0), spec(1), spec(-1), spec(0), spec(-1, 6), spec(0, 6), spec(0), spec(1), spec(0), spec(1),
                spec(0, 1), spec(1, 1), pl.BlockSpec((1, hp, DH), lambda m, g: (g, 0, 0))]
    if has_acc:
        ins += list(acc)
        in_specs += [spec(0)] * 3
    big = jax.ShapeDtypeStruct((S, HW), F32)
    return _pcall(
        body, ins, phase, name=f"attn_bwd_d{dil}", out_shape=(big, big, big), grid=(nb, ng),
        in_specs=in_specs, out_specs=(spec(0),) * 3,
        compiler_params=_params(2),
    )


TC = 512
NCT = DFF // TC


def _shift_rows(a, k):
    rows = lax.broadcasted_iota(jnp.int32, a.shape, 0)
    rolled = pltpu.roll(a, k % S, 0)
    if k > 0:
        return jnp.where(rows >= k, rolled, 0.0)
    return jnp.where(rows < S + k, rolled, 0.0)


def _conv_pre(a, w_ref, b_ref):
    return b_ref[...] + w_ref[0:1, :] * _shift_rows(a, 2) + w_ref[1:2, :] * _shift_rows(a, 1) + w_ref[2:3, :] * a


def _conv_gate_fwd(u, conv_w, conv_b, name, tiles=(0, NCT), into=None, phase=None):
    t0, t1 = tiles

    def body(a_ref, g_ref, w_ref, b_ref, *rest):
        pre = _conv_pre(a_ref[...].astype(F32), w_ref, b_ref)
        rest[-1][...] = (pre * _sigmoid(pre) * g_ref[...].astype(F32)).astype(BF16)

    args = [u, u, conv_w, conv_b]
    in_specs = [pl.BlockSpec((S, TC), lambda i: (0, t0 + i)), pl.BlockSpec((S, TC), lambda i: (0, NCT + t0 + i)),
                pl.BlockSpec((3, TC), lambda i: (0, t0 + i)), pl.BlockSpec((1, TC), lambda i: (0, t0 + i))]
    kw = {}
    if into is not None:
        args.append(into)
        in_specs.append(pl.BlockSpec(memory_space=pl.ANY))
        kw["input_output_aliases"] = {4: 0}
    return _pcall(
        body, args, phase, name=name, out_shape=jax.ShapeDtypeStruct((S, DFF), BF16), grid=(t1 - t0,),
        in_specs=in_specs, out_specs=pl.BlockSpec((S, TC), lambda i: (0, t0 + i)),
        compiler_params=_params(1), **kw,
    )


def _conv_gate_bwd(dy, u, conv_w, conv_b, phase=None):
    def body(dy_ref, a_ref, g_ref, w_ref, b_ref, du_ref, db_ref, dw_ref, da_buf, dg_buf, sems):
        i = pl.program_id(0)
        slot = i % 2

        def writes(step, s):
            col = pl.multiple_of(step * TC, TC)
            return (pltpu.make_async_copy(da_buf.at[s], du_ref.at[:, pl.ds(col, TC)], sems.at[0, s]),
                    pltpu.make_async_copy(dg_buf.at[s], du_ref.at[:, pl.ds(DFF + col, TC)], sems.at[1, s]))

        @pl.when(i >= 2)
        def _():
            for cp in writes(i - 2, slot):
                cp.wait()

        a = a_ref[...].astype(F32)
        dyv = dy_ref[...].astype(F32)
        pre = _conv_pre(a, w_ref, b_ref)
        sg = _sigmoid(pre)
        dg_buf[slot] = (dyv * pre * sg).astype(BF16)
        dpre = dyv * g_ref[...].astype(F32) * (sg * (1.0 + pre * (1.0 - sg)))
        da = w_ref[2:3, :] * dpre + w_ref[1:2, :] * _shift_rows(dpre, -1) + w_ref[0:1, :] * _shift_rows(dpre, -2)
        da_buf[slot] = da.astype(BF16)
        for cp in writes(i, slot):
            cp.start()
        db_ref[...] = jnp.sum(dpre, axis=0, keepdims=True)
        dw_ref[0:1, :] = jnp.sum(dpre * _shift_rows(a, 2), axis=0, keepdims=True)
        dw_ref[1:2, :] = jnp.sum(dpre * _shift_rows(a, 1), axis=0, keepdims=True)
        dw_ref[2:3, :] = jnp.sum(dpre * a, axis=0, keepdims=True)

        @pl.when(i == NCT - 1)
        def _():
            for cp in writes(i - 1, 1 - slot) + writes(i, slot):
                cp.wait()

    col = pl.BlockSpec((S, TC), lambda i: (0, i))
    return _pcall(
        body, [dy, u, u, conv_w, conv_b], phase, name="conv_gate_bwd",
        out_shape=(jax.ShapeDtypeStruct((S, 2 * DFF), BF16), jax.ShapeDtypeStruct((1, DFF), F32),
                   jax.ShapeDtypeStruct((3, DFF), F32)),
        grid=(NCT,),
        in_specs=[col, col, pl.BlockSpec((S, TC), lambda i: (0, NCT + i)),
                  pl.BlockSpec((3, TC), lambda i: (0, i)), pl.BlockSpec((1, TC), lambda i: (0, i))],
        out_specs=(pl.BlockSpec(memory_space=pl.ANY), pl.BlockSpec((1, TC), lambda i: (0, i)),
                   pl.BlockSpec((3, TC), lambda i: (0, i))),
        scratch_shapes=[pltpu.VMEM((2, S, TC), BF16), pltpu.VMEM((2, S, TC), BF16), pltpu.SemaphoreType.DMA((2, 2))],
        compiler_params=_params(1),
    )


def _out_loss(z, x1, target, gate):
    def body(z_ref, x_ref, t_ref, g_ref, do_ref, dz_ref, dg_ref, loss_ref):
        i = pl.program_id(0)
        zv = z_ref[...]
        gv = g_ref[...]
        err = x_ref[...] + gv * zv - t_ref[...]
        dout = err * (1.0 / D)
        do_ref[...] = dout
        dz_ref[...] = (dout * gv).astype(BF16)

        @pl.when(i == 0)
        def _():
            dg_ref[...] = jnp.zeros_like(dg_ref)
            loss_ref[...] = jnp.zeros_like(loss_ref)

        dg_ref[...] += jnp.sum(dout * zv, axis=0, keepdims=True)
        part = jnp.sum(jnp.sum(err * err, axis=0, keepdims=True), axis=-1, keepdims=True) * (0.5 / D)
        lane = lax.broadcasted_iota(jnp.int32, (1, 128), 1)
        loss_ref[...] += jnp.where(lane == 0, part, 0.0)

    return pl.pallas_call(
        body, name="out_loss",
        out_shape=(jax.ShapeDtypeStruct((S, D), F32), jax.ShapeDtypeStruct((S, D), BF16),
                   jax.ShapeDtypeStruct((1, D), F32), jax.ShapeDtypeStruct((1, 128), F32)),
        grid=(S // TR,),
        in_specs=[_row_spec(), _row_spec(), _row_spec(), _vec_spec()],
        out_specs=(_row_spec(), _row_spec(), _vec_spec(), _vec_spec(128)),
        compiler_params=_params(1),
    )(z, x1, target, gate)


ADA_COLS = 6 * D // N_CHIPS
TA = 512


def _ada_fwd(c_all, w_shard, b_shard):
    def body(c_ref, w_ref, b_ref, o_ref):
        cv = c_ref[...]
        o_ref[...] = _dot_f32(cv * _sigmoid(cv), w_ref[...]) + b_ref[...]

    return pl.pallas_call(
        body, name="ada_fwd", out_shape=jax.ShapeDtypeStruct((N_DEV, ADA_COLS), F32), grid=(ADA_COLS // TA,),
        in_specs=[pl.BlockSpec((N_DEV, D), lambda j: (0, 0)), pl.BlockSpec((D, TA), lambda j: (0, j)),
                  pl.BlockSpec((1, TA), lambda j: (0, j))],
        out_specs=pl.BlockSpec((N_DEV, TA), lambda j: (0, j)),
        compiler_params=_params(1),
    )(c_all, w_shard, b_shard)


ADA_ROWS = 128


def _ada_bwd_adamw(c_all, dmod_shard, w, m, v):
    def body(c_ref, d_ref, w_ref, m_ref, v_ref, g_ref, dl_ref, mo_ref, vo_ref):
        cv = c_ref[...]
        gv = lax.dot_general(cv * _sigmoid(cv), d_ref[...], TN, precision=lax.Precision.HIGHEST,
                             preferred_element_type=F32)
        g_ref[...] = gv
        m2 = ADAM_B1 * m_ref[...] + (1.0 - ADAM_B1) * gv
        v2 = ADAM_B2 * v_ref[...] + (1.0 - ADAM_B2) * (gv * gv)
        m_hat = m2 / (1.0 - ADAM_B1 ** ADAM_STEP)
        v_hat = v2 / (1.0 - ADAM_B2 ** ADAM_STEP)
        dl_ref[...] = -ADAM_LR * (m_hat / (jnp.sqrt(v_hat) + ADAM_EPS) + ADAM_WD * w_ref[...])
        mo_ref[...] = m2
        vo_ref[...] = v2

    spec = pl.BlockSpec((ADA_ROWS, ADA_COLS), lambda i: (i, 0))
    out = jax.ShapeDtypeStruct((D, ADA_COLS), F32)
    return pl.pallas_call(
        body, name="ada_bwd_adamw", out_shape=(out,) * 4, grid=(D // ADA_ROWS,),
        in_specs=[pl.BlockSpec((N_DEV, ADA_ROWS), lambda i: (0, i)), pl.BlockSpec((N_DEV, ADA_COLS), lambda i: (0, 0)),
                  spec, spec, spec],
        out_specs=(spec,) * 4,
        compiler_params=_params(1),
    )(c_all, dmod_shard, w, m, v)


def _sum_rows(g, name):
    rows, n = g.shape

    def body(g_ref, o_ref):
        acc = g_ref[0:1, :]
        for i in range(1, rows):
            acc = acc + g_ref[i:i + 1, :]
        o_ref[...] = acc

    return pl.pallas_call(
        body, name=name, out_shape=jax.ShapeDtypeStruct((1, n), F32),
        in_specs=[VMEM_SPEC], out_specs=VMEM_SPEC,
    )(g)


def _lb_grad(lb_logits, dlb):
    def body(l_ref, d_ref, o_ref):
        p = 1.0 / (1.0 + jnp.exp(l_ref[1:2, :] - l_ref[0:1, :]))
        t = d_ref[...] * p * (1.0 - p)
        o_ref[0:1, :] = t
        o_ref[1:2, :] = -t

    return pl.pallas_call(
        body, name="lb_grad", out_shape=jax.ShapeDtypeStruct((2, HW), F32),
        in_specs=[VMEM_SPEC, VMEM_SPEC], out_specs=VMEM_SPEC,
    )(lb_logits, dlb)


def _adamw(w, g, m, v, name, copy_grad=False):
    rows, cols = w.shape
    tr = rows
    if rows * cols * 4 > ADAM_BLOCK_BYTES:
        tr = _pick(rows, [t for t in (256, 128, 64, 32, 16, 8) if t * cols * 4 <= ADAM_BLOCK_BYTES])

    def body(w_ref, g_ref, m_ref, v_ref, d_ref, mo_ref, vo_ref, *go_ref):
        gv = g_ref[...]
        if copy_grad:
            go_ref[0][...] = gv
        m2 = ADAM_B1 * m_ref[...] + (1.0 - ADAM_B1) * gv
        v2 = ADAM_B2 * v_ref[...] + (1.0 - ADAM_B2) * (gv * gv)
        m_hat = m2 / (1.0 - ADAM_B1 ** ADAM_STEP)
        v_hat = v2 / (1.0 - ADAM_B2 ** ADAM_STEP)
        d_ref[...] = -ADAM_LR * (m_hat / (jnp.sqrt(v_hat) + ADAM_EPS) + ADAM_WD * w_ref[...])
        mo_ref[...] = m2
        vo_ref[...] = v2

    spec = pl.BlockSpec((tr, cols), lambda i: (i, 0))
    out = jax.ShapeDtypeStruct((rows, cols), F32)
    n_out = 4 if copy_grad else 3
    return pl.pallas_call(
        body, name=name, out_shape=(out,) * n_out, grid=(rows // tr,),
        in_specs=[spec] * 4, out_specs=(spec,) * n_out,
        compiler_params=_params(1),
    )(w, g, m, v)


SC_TILES = 32
SC_LANES = 16
SC_COL_PARTS = 2
SC_CHUNK_ROWS = 8


def _adamw_sc(w, g, m, v, name):
    rows, cols = w.shape
    band, part = rows // (SC_TILES // SC_COL_PARTS), cols // SC_COL_PARTS
    assert band % SC_CHUNK_ROWS == 0 and part % SC_LANES == 0, (rows, cols)

    def body(w_hbm, g_hbm, m_hbm, v_hbm, d_hbm, mo_hbm, vo_hbm, go_hbm, wb, gb, mb, vb, db):
        tile = lax.axis_index("sc_subcore") * 2 + lax.axis_index("sc_core")
        row0 = (tile // SC_COL_PARTS) * band
        col0 = (tile % SC_COL_PARTS) * part

        @pl.loop(0, band, step=SC_CHUNK_ROWS)
        def _(r):
            at = lambda ref: ref.at[pl.ds(row0 + r, SC_CHUNK_ROWS), pl.ds(col0, part)]
            pltpu.sync_copy(at(w_hbm), wb)
            pltpu.sync_copy(at(g_hbm), gb)
            pltpu.sync_copy(gb, at(go_hbm))
            pltpu.sync_copy(at(m_hbm), mb)
            pltpu.sync_copy(at(v_hbm), vb)

            @pl.loop(0, SC_CHUNK_ROWS)
            def _(i):
                @pl.loop(0, part, step=SC_LANES)
                def _(j):
                    sl = (i, pl.ds(j, SC_LANES))
                    gv = gb[sl]
                    m2 = ADAM_B1 * mb[sl] + (1.0 - ADAM_B1) * gv
                    v2 = ADAM_B2 * vb[sl] + (1.0 - ADAM_B2) * (gv * gv)
                    m_hat = m2 / (1.0 - ADAM_B1 ** ADAM_STEP)
                    v_hat = v2 / (1.0 - ADAM_B2 ** ADAM_STEP)
                    db[sl] = -ADAM_LR * (m_hat / (jnp.sqrt(v_hat) + ADAM_EPS) + ADAM_WD * wb[sl])
                    mb[sl] = m2
                    vb[sl] = v2

            pltpu.sync_copy(db, at(d_hbm))
            pltpu.sync_copy(mb, at(mo_hbm))
            pltpu.sync_copy(vb, at(vo_hbm))

    out = jax.ShapeDtypeStruct((rows, cols), F32)
    buf = pltpu.VMEM((SC_CHUNK_ROWS, part), F32)
    return pl.kernel(
        body, name=name, out_type=(out, out, out, out),
        mesh=plsc.VectorSubcoreMesh(core_axis_name="sc_core", subcore_axis_name="sc_subcore"),
        scratch_types=[buf] * 5,
    )(w, g, m, v)


W_IN, W_OUT, W_UP, W_DOWN = range(4)
IN_CHUNKS = 4
W_INQ = 4


def _sequence_step(x, target, mod, norm1_w, lb_logits, hg_norm_w, q_norm_w, k_norm_w, norm2_w, conv_w, conv_b, net):
    shift1, scale1, gate1, shift2, scale2, gate2 = [mod[:, i * D:(i + 1) * D] for i in range(6)]

    def run(name, fn, *args, **kw):
        phase = net.host(name)
        if phase is None:
            return fn(*args, **kw)
        res, comm = fn(*args, phase=phase, **kw)
        net.done(name, comm, res[0] if isinstance(res, tuple) else res)
        return res

    h = _norm_mod(x, norm1_w, scale1, shift1, "norm1_fwd")
    proj = run("proj_fwd", _matmul, h, net.full[W_IN], "nn", F32, "proj_fwd")
    cat, o_raw, states = run("hgrn_fwd", _hgrn_fwd, proj, lb_logits, hg_norm_w)
    qn, kn = _qk_prep(proj, q_norm_w, k_norm_w)
    att = [run(f"attn_fwd_d{dil}", _attn_fwd, proj, qn, kn, dil) for dil in DILS]
    cat, b_out_f32, lse = _attn_merge([t[0] for t in att], [t[1] for t in att], cat)
    mix = run("mix_fwd", _matmul, cat, net.full[W_OUT], "nn", F32, "mix_fwd")
    x1, h2 = _resid_norm_mod(x, mix, gate1, norm2_w, scale2, shift2, "norm2_fwd")
    u = run("up_fwd", _matmul, h2, net.full[W_UP], "nn", BF16, "up_fwd")
    yact = run("conv_gate_fwd_a", _conv_gate_fwd, u, conv_w, conv_b, "conv_gate_fwd_a", tiles=(0, NCT // 2 + 1))
    yact = run("conv_gate_fwd_b", _conv_gate_fwd, u, conv_w, conv_b, "conv_gate_fwd_b", tiles=(NCT // 2 + 1, NCT),
               into=yact)
    z =_matmul(yact, net.full[W_DOWN], "nn", F32, "down_fwd")
    dout, dz, dgate2, loss_row = _out_loss(z, x1, target, gate2)

    net.grad[W_DOWN] = _matmul(yact, dz, "tn", BF16, "down_bwd_w")
    dyact = run("down_bwd_x", _matmul, dz, net.full[W_DOWN], "nt", BF16, "down_bwd_x")
    du, dconv_b, dconv_w = run("conv_gate_bwd", _conv_gate_bwd, dyact, u, conv_w, conv_b)
    net.grad[W_UP] = run("up_bwd_w", _matmul, h2, du, "tn", BF16, "up_bwd_w")
    dh2 = run("up_bwd_x", _matmul, du, net.full[W_UP], "nt", F32, "up_bwd_x")
    dx1, dshift2, dscale2, dnorm2, dgate1, dmix = run(
        "norm2_bwd", _norm_mod_bwd, dh2, x1, norm2_w, scale2, dout, "norm2_bwd", mix=mix, gate=gate1)
    net.grad[W_OUT] = run("mix_bwd_w", _matmul, cat, dmix, "tn", BF16, "mix_bwd_w")
    dcat = run("mix_bwd_x", _matmul, dmix, net.full[W_OUT], "nt", F32, "mix_bwd_x")
    dhq, dhf, dhi, dhg, dlb, dhg_norm = run("hgrn_bwd", _hgrn_bwd, proj, lb_logits, hg_norm_w, o_raw, states, dcat)
    delta = _attn_delta(dcat, b_out_f32)
    acc = None
    for dil in DILS:
        acc = run(f"attn_bwd_d{dil}", _attn_bwd, proj, qn, kn, lse, delta, dcat, dil, acc)
    daq, dak, dav, dq_norm, dk_norm = _qk_norm_bwd(proj, *acc, q_norm_w, k_norm_w)
    dproj = jnp.concatenate([dhq, dhf, dhi, dhg, daq, dak, dav], axis=1)
    net.carry = dproj
    for q in range(IN_CHUNKS):
        net.after_chunk(q, run(f"proj_bwd_w_{q}", _matmul, h, net.carry, "tn", BF16, f"proj_bwd_w_{q}",
                               m_blocks=(D // IN_CHUNKS, [q])))
    dh = run("proj_bwd_x", _matmul, net.carry, net.full[W_IN], "nt", F32, "proj_bwd_x")
    grad_x, dshift1, dscale1, dnorm1 = run("norm1_bwd", _norm_mod_bwd, dh, x, norm1_w, scale1, dx1, "norm1_bwd")

    dmod = jnp.concatenate([dshift1, dscale1, dgate1, dshift2, dscale2, dgate2], axis=1)
    small = dict(norm1_w=dnorm1, lb=dlb, hg_norm_w=dhg_norm, q_norm_w=dq_norm, k_norm_w=dk_norm,
                 norm2_w=dnorm2, conv_b=dconv_b, conv_w=dconv_w)
    return loss_row, grad_x, dmod, small


def _place():
    x, y, c = lax.axis_index("x"), lax.axis_index("y"), lax.axis_index("c")
    others = [(1 - x, y), (x, 1 - y), (1 - x, 1 - y)]
    return x, y, c, others


def _remote(src, dst, send_sems, recv_sems, k, to):
    return pltpu.make_async_remote_copy(src_ref=src, dst_ref=dst, send_sem=send_sems.at[k], recv_sem=recv_sems.at[k],
                                        device_id=to, device_id_type=MESH)


class _Phase:
    def __init__(self):
        self.ins, self.outs, self.aliases, self.groups, self.n = [], [], {}, [], 0

    def add(self, ins, outs, aliases, n, copies):
        i0, o0 = len(self.ins), len(self.outs)
        self.ins += list(ins)
        self.outs += list(outs)
        self.aliases.update({i0 + i: o0 + o for i, o in aliases.items()})
        self.groups.append((slice(i0, i0 + len(ins)), slice(o0, o0 + len(outs)), copies))
        self.n += n
        return slice(o0, o0 + len(outs))

    def build(self, cin, cout, send_sems, recv_sems, landing):
        res = []
        for si, so, copies in self.groups:
            for src, dst, land, peer in copies(cin[si], cout[so]):
                k = len(res)
                res.append(_remote(land, land, send_sems, recv_sems, k, peer) if landing
                           else _remote(src, dst, send_sems, recv_sems, k, peer))
        assert len(res) == self.n
        return res


def _pcall(body, args, phase=None, **kw):
    if phase is None or not phase.groups:
        res = pl.pallas_call(body, **kw)(*args)
        return res if phase is None else (res, ())
    grid = tuple(kw.pop("grid", ()))
    out_shape, out_specs = kw.pop("out_shape"), kw.pop("out_specs")
    single = not isinstance(out_shape, (tuple, list))
    out_shape = [out_shape] if single else list(out_shape)
    out_specs = [out_specs] if single else list(out_specs)
    scratch = list(kw.pop("scratch_shapes", ()))
    aliases = dict(kw.pop("input_output_aliases", {}))
    n_in, n_out, n_ci, n_co = len(args), len(out_shape), len(phase.ins), len(phase.outs)
    aliases.update({n_in + i: n_out + o for i, o in phase.aliases.items()})

    def hosted(*refs):
        ins, cin = refs[:n_in], refs[n_in:n_in + n_ci]
        outs = refs[n_in + n_ci:n_in + n_ci + n_out]
        cout = refs[n_in + n_ci + n_out:n_in + n_ci + n_out + n_co]
        scr, send_sems, recv_sems = refs[n_in + n_ci + n_out + n_co:-2], refs[-2], refs[-1]
        ids = [pl.program_id(a) for a in range(len(grid))]

        def start():
            for cp in phase.build(cin, cout, send_sems, recv_sems, False):
                cp.start()

        def finish():
            for cp in phase.build(cin, cout, send_sems, recv_sems, False):
                cp.wait_send()
            for cp in phase.build(cin, cout, send_sems, recv_sems, True):
                cp.wait_recv()

        if grid:
            first = functools.reduce(jnp.logical_and, [i == 0 for i in ids])
            last = functools.reduce(jnp.logical_and, [i == g - 1 for i, g in zip(ids, grid)])
            pl.when(first)(start)
            body(*ins, *outs, *scr)
            pl.when(last)(finish)
        else:
            start()
            body(*ins, *outs, *scr)
            finish()

    res = pl.pallas_call(
        hosted, out_shape=tuple(out_shape + phase.outs), grid=grid,
        in_specs=list(kw.pop("in_specs")) + [HBM_SPEC] * n_ci, out_specs=tuple(out_specs + [HBM_SPEC] * n_co),
        input_output_aliases=aliases,
        scratch_shapes=scratch + [pltpu.SemaphoreType.DMA((phase.n,)), pltpu.SemaphoreType.DMA((phase.n,))],
        **kw,
    )(*args, *phase.ins)
    outs = res[:n_out]
    return (outs[0] if single else tuple(outs)), tuple(res[n_out:])


def _comm_only(name, phase):
    return _pcall(lambda: None, [], phase, name=name, out_shape=(), in_specs=[], out_specs=())[1]


def _all_gather_rows(block, name, phase=None):
    m_per, n = block.shape

    def body(x_ref, out_ref, send_sems, recv_sems, local_sem):
        x, y, c, chips = _place()
        me, sibling = (x, y, c), (x, y, 1 - c)

        def rows(px, py, pc):
            return out_ref.at[pl.ds((4 * px + 2 * py + pc) * m_per, m_per), :]

        def copy(k, blk, to, src=None):
            return _remote(rows(*blk) if src is None else src, rows(*blk), send_sems, recv_sems, k, to)

        mine = pltpu.make_async_copy(x_ref, rows(*me), local_sem)
        mine.start()
        first = [copy(0, me, sibling, src=x_ref)]
        first += [copy(1 + j, me, (*chip, c), src=x_ref) for j, chip in enumerate(chips)]
        for cp in first:
            cp.start()
        passed = [copy(4 + j, (*chip, c), sibling) for j, chip in enumerate(chips)]
        for j, chip in enumerate(chips):
            copy(1 + j, (*chip, c), me).wait_recv()
            passed[j].start()
        copy(0, sibling, me).wait_recv()
        for j, chip in enumerate(chips):
            copy(4 + j, (*chip, 1 - c), me).wait_recv()
        for cp in first + passed:
            cp.wait_send()
        mine.wait()

    return _pcall(
        body, [block], phase, name=name, out_shape=jax.ShapeDtypeStruct((N_DEV * m_per, n), block.dtype),
        in_specs=[VMEM_SPEC], out_specs=VMEM_SPEC,
        scratch_shapes=[pltpu.SemaphoreType.DMA((7,)), pltpu.SemaphoreType.DMA((7,)), pltpu.SemaphoreType.DMA],
    )


class _Geo:
    def __init__(self, kind, shard_shape):
        self.kind = kind
        self.shard_shape = tuple(shard_shape)
        self.hr, self.hc = shard_shape[0] // 2, shard_shape[1]
        if kind == "col":
            self.full_shape = (shard_shape[0], N_CHIPS * shard_shape[1])
        else:
            self.full_shape = (N_CHIPS * shard_shape[0], shard_shape[1])

    def full_half(self, ref, j, c):
        return self.piece(ref, j, c, 0, 1, 1)

    def piece(self, ref, j, c, p0, p1, pieces, part=None):
        pr = self.hr // pieces
        off, n = p0 * pr, (p1 - p0) * pr
        if part is not None:
            top = (n // 32) * 16
            off, n = (off, top) if part == 0 else (off + top, n - top)
        if self.kind == "col":
            return ref.at[pl.ds(pl.multiple_of(c * self.hr + off, 16), n),
                          pl.ds(pl.multiple_of(j * self.hc, 128), self.hc)]
        return ref.at[pl.ds(pl.multiple_of((2 * j + c) * self.hr + off, 16), n), :]


WEIGHT_KINDS = ("col", "row", "col", "row")
NW = len(WEIGHT_KINDS)


def _comm_call(body, name, ins, outs, n_remote, aliases=None):
    return pl.pallas_call(
        body, name=name, out_shape=tuple(outs),
        in_specs=[HBM_SPEC] * len(ins), out_specs=tuple([HBM_SPEC] * len(outs)),
        input_output_aliases=aliases or {},
        scratch_shapes=[pltpu.SemaphoreType.DMA((n_remote,)), pltpu.SemaphoreType.DMA((n_remote,))],
    )(*ins)


ROW_TILES = (256, 176, 128, 64, 32, 16)


def _cast_into_full(shard, geo, place, name):
    rs, cs = shard.shape
    tr = _pick(rs, ROW_TILES)
    nb = rs // tr

    def body(place_ref, s_ref, o_ref):
        o_ref[...] = s_ref[...].astype(BF16)

    if geo.kind == "col":
        out_map = lambda i, place_ref: (i, place_ref[0])
    else:
        out_map = lambda i, place_ref: (place_ref[0] * nb + i, 0)
    return pl.pallas_call(
        body, name=name, out_shape=jax.ShapeDtypeStruct(geo.full_shape, BF16),
        grid_spec=pltpu.PrefetchScalarGridSpec(
            num_scalar_prefetch=1, grid=(nb,),
            in_specs=[pl.BlockSpec((tr, cs), lambda i, place_ref: (i, 0))],
            out_specs=pl.BlockSpec((tr, cs), out_map)),
        compiler_params=_params(1),
    )(place, shard)


def _gather_weight(full, geo, pieces, name):
    per = 8

    def body(in_ref, ref, send_sems, recv_sems):
        x, y, c, _ = _place()
        j, jx, jy, jo = 2 * x + y, 2 * (1 - x) + y, 2 * x + (1 - y), 2 * (1 - x) + (1 - y)
        nx, ny, sib = (1 - x, y, c), (x, 1 - y, c), (x, y, 1 - c)

        def cp(region, k, to):
            return _remote(region, region, send_sems, recv_sems, k, to)

        def reg(chip, p, part=None, core=c):
            return geo.piece(ref, chip, core, p, p + 1, pieces, part)

        sent = []
        for p in range(pieces):
            sent += [cp(reg(j, p), per * p, nx), cp(reg(j, p), per * p + 1, ny)]
        for d in sent:
            d.start()
        for p in range(pieces):
            cp(reg(jx, p), per * p, nx).wait_recv()
            new = [cp(reg(jx, p, 0), per * p + 2, ny), cp(reg(jx, p), per * p + 4, sib)]
            cp(reg(jy, p), per * p + 1, ny).wait_recv()
            new += [cp(reg(jy, p, 1), per * p + 3, nx), cp(reg(jy, p), per * p + 5, sib)]
            for d in new:
                d.start()
            sent += new
        for p in range(pieces):
            cp(reg(jo, p, 0), per * p + 2, ny).wait_recv()
            cp(reg(jo, p, 1), per * p + 3, nx).wait_recv()
            new = [cp(reg(jo, p, 0), per * p + 6, sib), cp(reg(jo, p, 1), per * p + 7, sib)]
            for d in new:
                d.start()
            sent += new
        for p in range(pieces):
            cp(reg(jx, p, None, 1 - c), per * p + 4, sib).wait_recv()
            cp(reg(jy, p, None, 1 - c), per * p + 5, sib).wait_recv()
            cp(reg(jo, p, 0, 1 - c), per * p + 6, sib).wait_recv()
            cp(reg(jo, p, 1, 1 - c), per * p + 7, sib).wait_recv()
        for d in sent:
            d.wait_send()

    return _comm_call(body, name, [full], [_same(full)], per * pieces, aliases={0: 0})[0]


def _same(a):
    return jax.ShapeDtypeStruct(a.shape, a.dtype)


def _gather_ici(full, geo, p0, p1, pieces):
    def copies(ins, outs):
        x, y, c, _ = _place()
        mine = geo.piece(outs[0], 2 * x + y, c, p0, p1, pieces)
        return [(mine, mine, geo.piece(outs[0], 2 * ox + oy, c, p0, p1, pieces), (ox, oy, c))
                for ox, oy in ((1 - x, y), (x, 1 - y))]

    return dict(ins=[full], outs=[_same(full)], aliases={0: 0}, n=2, copies=copies)


def _gather_relay(full, geo, p0, p1, pieces):
    def copies(ins, outs):
        x, y, c, _ = _place()
        jx, jy, jo = 2 * (1 - x) + y, 2 * x + (1 - y), 2 * (1 - x) + (1 - y)
        reg = lambda chip, part: geo.piece(outs[0], chip, c, p0, p1, pieces, part)
        return [(reg(jx, 0), reg(jx, 0), reg(jo, 0), (x, 1 - y, c)), (reg(jy, 1), reg(jy, 1), reg(jo, 1), (1 - x, y, c))]

    return dict(ins=[full], outs=[_same(full)], aliases={0: 0}, n=2, copies=copies)


def _gather_d2d(full, geo):
    def copies(ins, outs):
        x, y, c, chips = _place()
        return [(geo.full_half(outs[0], 2 * ox + oy, c), geo.full_half(outs[0], 2 * ox + oy, c),
                 geo.full_half(outs[0], 2 * ox + oy, 1 - c), (x, y, 1 - c)) for ox, oy in chips]

    return dict(ins=[full], outs=[_same(full)], aliases={0: 0}, n=3, copies=copies)


def _swap_d2d(grad, geo):
    def copies(ins, outs):
        x, y, c, _ = _place()
        return [(geo.full_half(ins[0], j, 1 - c), outs[0].at[j], outs[0].at[j], (x, y, 1 - c)) for j in range(N_CHIPS)]

    return dict(ins=[grad], outs=[jax.ShapeDtypeStruct((N_CHIPS, geo.hr, geo.hc), BF16)], aliases={}, n=N_CHIPS,
                copies=copies)


SEM_SPEC = pl.BlockSpec(memory_space=pltpu.SEMAPHORE)
SIDE_EFFECT = pltpu.SideEffectType.DATAFLOW_SIDE_EFFECTING


def _fly_copies(swap, src_ref, land_ref, geo, sems):
    x, y, c, chips = _place()
    if swap:
        ends = [(geo.full_half(src_ref, j, 1 - c), land_ref.at[j], (x, y, 1 - c)) for j in range(N_CHIPS)]
    else:
        ends = [(src_ref.at[2 * ox + oy], land_ref.at[k], (ox, oy, c)) for k, (ox, oy) in enumerate(chips)]
    n = len(ends)
    return [pltpu.make_async_remote_copy(src_ref=src, dst_ref=dst, send_sem=sems[k], recv_sem=sems[n + k],
                                         device_id=peer, device_id_type=MESH) for k, (src, dst, peer) in enumerate(ends)]


def _fly_start(swap, src, geo, name, carry=None):
    n = N_CHIPS if swap else 3
    n_carry = 0 if carry is None else 1

    def body(src_ref, land_ref, *rest):
        sems, token = rest[n_carry:n_carry + 2 * n], rest[n_carry + 2 * n + 2]
        for copy in _fly_copies(swap, src_ref, land_ref, geo, sems):
            copy.start()
        token[...] = jnp.zeros_like(token)

    land = jax.ShapeDtypeStruct((n, geo.hr, geo.hc), BF16)
    out_shape = (pltpu.SemaphoreType.DMA(()),) * (2 * n) + (pltpu.HBM(src.shape, src.dtype), pltpu.HBM(land.shape, land.dtype),
                                                             jax.ShapeDtypeStruct((8, 128), F32))
    out_specs = (SEM_SPEC,) * (2 * n) + (HBM_SPEC, HBM_SPEC, VMEM_SPEC)
    args = [pltpu.with_memory_space_constraint(src, pltpu.HBM),
            pltpu.with_memory_space_constraint(lax.empty(land.shape, land.dtype), pltpu.HBM)]
    aliases = {0: 2 * n, 1: 2 * n + 1}
    if carry is not None:
        out_shape += (pltpu.HBM(carry.shape, carry.dtype),)
        out_specs += (HBM_SPEC,)
        args.append(pltpu.with_memory_space_constraint(carry, pltpu.HBM))
        aliases[2] = 2 * n + 3
    return pl.pallas_call(
        body, name=name, out_shape=out_shape, in_specs=(HBM_SPEC,) * len(args), out_specs=out_specs,
        input_output_aliases=aliases,
        compiler_params=pltpu.CompilerParams(has_side_effects=SIDE_EFFECT),
    )(*args)


def _fly_wait(swap, started, geo, after, name):
    n = N_CHIPS if swap else 3
    sems, src_thru, land_thru = started[:2 * n], started[2 * n], started[2 * n + 1]

    def body(src_ref, land_ref, *rest):
        for copy in _fly_copies(swap, src_ref, land_ref, geo, rest[:2 * n]):
            copy.wait_send()
            copy.wait_recv()

    return pl.pallas_call(
        body, name=name,
        out_shape=(pltpu.HBM(src_thru.shape, src_thru.dtype), pltpu.HBM(land_thru.shape, land_thru.dtype)),
        in_specs=(HBM_SPEC, HBM_SPEC) + (SEM_SPEC,) * (2 * n) + (pl.BlockSpec(memory_space=pl.ANY),),
        out_specs=(HBM_SPEC, HBM_SPEC), input_output_aliases={0: 0, 1: 1},
        compiler_params=pltpu.CompilerParams(has_side_effects=SIDE_EFFECT),
    )(src_thru, land_thru, *sems, after)


def _scatter_ici(pair, recv, geo, p0, p1, pieces):
    pr = geo.hr // pieces
    rows = pl.ds(p0 * pr, (p1 - p0) * pr)

    def copies(ins, outs):
        x, y, c, chips = _place()
        return [(ins[0].at[2 * ox + oy, rows, :], outs[0].at[k, rows, :], outs[0].at[k, rows, :], (ox, oy, c))
                for k, (ox, oy) in enumerate(chips)]

    out = jax.ShapeDtypeStruct((3, geo.hr, geo.hc), BF16)
    if recv is None:
        return dict(ins=[pair], outs=[out], aliases={}, n=3, copies=copies)
    return dict(ins=[pair, recv], outs=[out], aliases={1: 0}, n=3, copies=copies)


def _join_d2d(shard, geo, row0=0):
    def copies(ins, outs):
        x, y, c, _ = _place()
        mine = outs[0].at[pl.ds(pl.multiple_of(row0 + c * geo.hr, 8), geo.hr), :]
        other = outs[0].at[pl.ds(pl.multiple_of(row0 + (1 - c) * geo.hr, 8), geo.hr), :]
        return [(mine, mine, other, (x, y, 1 - c))]

    return dict(ins=[shard], outs=[_same(shard)], aliases={0: 0}, n=1, copies=copies)


def _add_pair(grad, got, geo, place, name):
    tr = _pick(geo.hr, ROW_TILES)
    nb = geo.hr // tr

    def body(place_ref, a_ref, b_ref, o_ref):
        o_ref[0] = (a_ref[...].astype(F32) + b_ref[0].astype(F32)).astype(BF16)

    if geo.kind == "col":
        own_map = lambda j, i, place_ref: (place_ref[1] * nb + i, j)
    else:
        own_map = lambda j, i, place_ref: ((2 * j + place_ref[1]) * nb + i, 0)
    spec = pl.BlockSpec((1, tr, geo.hc), lambda j, i, place_ref: (j, i, 0))
    return pl.pallas_call(
        body, name=name, out_shape=jax.ShapeDtypeStruct((N_CHIPS, geo.hr, geo.hc), BF16),
        grid_spec=pltpu.PrefetchScalarGridSpec(
            num_scalar_prefetch=1, grid=(N_CHIPS, nb),
            in_specs=[pl.BlockSpec((tr, geo.hc), own_map), spec], out_specs=spec),
        compiler_params=_params(2),
    )(place, grad, got)


def _add_four(pair, recv, geo, place, name, rows=None, row0=0, into=None):
    tr = _pick(geo.hr, ROW_TILES)
    nb = geo.hr // tr
    rows = 2 * geo.hr if rows is None else rows

    def body(place_ref, p_ref, r_ref, *rest):
        rest[-1][...] = ((p_ref[0].astype(F32) + r_ref[0].astype(F32)) + r_ref[1].astype(F32)) + r_ref[2].astype(F32)

    in_specs = [pl.BlockSpec((1, tr, geo.hc), lambda i, place_ref: (place_ref[0], i, 0)),
                pl.BlockSpec((3, tr, geo.hc), lambda i, place_ref: (0, i, 0))]
    args = [place, pair, recv]
    if into is not None:
        in_specs.append(pl.BlockSpec(memory_space=pl.ANY))
        args.append(into)
    return pl.pallas_call(
        body, name=name, out_shape=jax.ShapeDtypeStruct((rows, geo.hc), F32),
        grid_spec=pltpu.PrefetchScalarGridSpec(
            num_scalar_prefetch=1, grid=(nb,), in_specs=in_specs,
            out_specs=pl.BlockSpec((tr, geo.hc), lambda i, place_ref: (row0 // tr + place_ref[1] * nb + i, 0))),
        input_output_aliases={3: 0} if into is not None else {},
        compiler_params=_params(1),
    )(*args)


PIECES = (4, 1, 16, 8) + (1,) * IN_CHUNKS
SCHEDULE = {
    "proj_fwd": (("gather_ici", W_OUT, 0, 1), ("gather_ici", W_UP, 0, 6)),
    "hgrn_fwd": (("gather_relay", W_OUT, 0, 1), ("gather_relay", W_UP, 0, 6), ("gather_ici", W_UP, 6, 12)),
    "attn_fwd_d1": (("gather_relay", W_UP, 6, 12),),
    "attn_fwd_d4": (("gather_ici", W_UP, 12, 16), ("gather_d2d", W_OUT)),
    "attn_fwd_d16": (("gather_relay", W_UP, 12, 16), ("gather_ici", W_DOWN, 0, 2)),
    "mix_fwd": (("gather_d2d", W_UP), ("gather_ici", W_DOWN, 2, 4)),
    "up_fwd": (("gather_ici", W_DOWN, 4, 8), ("gather_relay", W_DOWN, 0, 4)),
    "conv_gate_fwd_a": (("gather_relay", W_DOWN, 4, 8),),
    "conv_gate_fwd_b": (("gather_d2d", W_DOWN),),
    "down_bwd_x": (("swap", W_DOWN),),
    "conv_gate_bwd": (("scatter", W_DOWN, 0, 4),),
    "up_bwd_w": (("scatter", W_DOWN, 4, 8),),
    "up_bwd_x": (("swap", W_UP),),
    "norm2_bwd": (("scatter", W_UP, 0, 1),),
    "mix_bwd_w": (("scatter", W_UP, 1, 2),),
    "mix_bwd_x": (("swap", W_OUT), ("scatter", W_UP, 2, 3)),
    "hgrn_bwd": (("scatter", W_UP, 3, 9), ("join", W_DOWN)),
    "attn_bwd_d1": (("scatter", W_UP, 9, 12),),
    "attn_bwd_d4": (("scatter", W_OUT, 0, 1),),
    "attn_bwd_d16": (("scatter", W_UP, 12, 16),),
    "proj_bwd_w_0": (("join", W_UP), ("join", W_OUT)),
    "proj_bwd_x": (("scatter", W_INQ + 1, 0, 1), ("scatter", W_INQ + 2, 0, 1)),
    "gather_stats": (("join", W_INQ), ("join", W_INQ + 1), ("join", W_INQ + 2)),
    "grad_in_join": (("join", W_INQ + 3),),
}


class _Net:
    def __init__(self, shards, place):
        self.place = place
        self.geos = [_Geo(k, s.shape) for k, s in zip(WEIGHT_KINDS, shards)]
        self.full = [_cast_into_full(s, g, place, f"cast_shard_{w}") for w, (s, g) in enumerate(zip(shards, self.geos))]
        self.full[W_IN] = _gather_weight(self.full[W_IN], self.geos[W_IN], PIECES[W_IN], "gather_w_in")
        rows, cols = shards[W_IN].shape
        self.geos += [_Geo("col", (rows // IN_CHUNKS, cols))] * IN_CHUNKS
        n = NW + IN_CHUNKS
        self.grad, self.pair, self.recv, self.shard = [None] * n, [None] * n, [None] * n, [None] * n
        self.in_rows, self.in_shard = rows, None
        self.when_joined, self.joined = {}, {}
        self.flying = None
        self.carry = None
        self.swaps = [None] * IN_CHUNKS
        self.slots = []

    def _row0(self, w):
        return (w - W_INQ) * 2 * self.geos[w].hr

    def host(self, name):
        phase = _Phase()
        self.slots = []
        groups = {}
        for item in SCHEDULE.get(name, ()):
            kind, w = item[0], item[1]
            geo = self.geos[w]
            key = len(groups)
            if kind == "gather_ici":
                spec, key = _gather_ici(self.full[w], geo, item[2], item[3], PIECES[w]), ("full", w)
            elif kind == "gather_relay":
                spec, key = _gather_relay(self.full[w], geo, item[2], item[3], PIECES[w]), ("full", w)
            elif kind == "gather_d2d":
                spec, key = _gather_d2d(self.full[w], geo), ("full", w)
            elif kind == "swap":
                spec = _swap_d2d(self.grad[w], geo)
            elif kind == "scatter":
                spec, key = _scatter_ici(self.pair[w], self.recv[w], geo, item[2], item[3], PIECES[w]), ("recv", w)
            elif w >= W_INQ:
                spec, key = _join_d2d(self.in_shard, geo, self._row0(w)), "in_shard"
            else:
                spec = _join_d2d(self.shard[w], geo)
            groups.setdefault(key, []).append((item, spec))
        for group in groups.values():
            specs = [s for _, s in group]
            merged = dict(specs[0], n=sum(s["n"] for s in specs),
                          copies=lambda ins, outs, specs=specs: [t for s in specs for t in s["copies"](ins, outs)])
            self.slots.append(([item for item, _ in group], phase.add(**merged)))
        return phase

    def done(self, name, comm, result=None):
        for items, sl in sorted(self.slots, key=lambda s: s[0][0][0] == "scatter"):
            out = comm[sl][0]
            for item in items:
                kind, w = item[0], item[1]
                if kind in ("gather_ici", "gather_relay", "gather_d2d"):
                    self.full[w] = out
                elif kind == "swap":
                    self.pair[w] = _add_pair(self.grad[w], out, self.geos[w], self.place, f"grad_pair_sum_{w}")
                elif kind == "scatter":
                    self.recv[w] = out
                    if item[3] == PIECES[w] and w >= W_INQ:
                        self.in_shard = _add_four(self.pair[w], out, self.geos[w], self.place, f"grad_chip_sum_{w}",
                                                  rows=self.in_rows, row0=self._row0(w), into=self.in_shard)
                    elif item[3] == PIECES[w]:
                        self.shard[w] = _add_four(self.pair[w], out, self.geos[w], self.place, f"grad_chip_sum_{w}")
                elif w >= W_INQ:
                    self.in_shard = out
                else:
                    self.shard[w] = out
                    if w in self.when_joined:
                        self.joined[w] = self.when_joined[w](out)
        if name == "proj_bwd_x" and result is not None:
            self._land_swap(IN_CHUNKS - 1, result)

    def _land_swap(self, q, after):
        w = W_INQ + q
        grad, got = _fly_wait(True, self.swaps[q], self.geos[w], after, f"grad_in_swap{q}_wait")
        self.pair[w] = _add_pair(grad, got, self.geos[w], self.place, f"grad_pair_sum_{w}")

    def after_chunk(self, q, grad):
        w = W_INQ + q
        self.grad[w] = grad
        if q > 0:
            self._land_swap(q - 1, grad)
        if q == 1:
            self.flying = _fly_start(False, self.pair[W_INQ], self.geos[W_INQ], "grad_in_scatter0_start", self.carry)
            self.carry = self.flying[-1]
        self.swaps[q] = _fly_start(True, grad, self.geos[w], f"grad_in_swap{q}_start", self.carry)
        self.carry = self.swaps[q][-1]
        if q == IN_CHUNKS - 1:
            pair, recv = _fly_wait(False, self.flying, self.geos[W_INQ], grad, "grad_in_scatter0_wait")
            self.in_shard = _add_four(pair, recv, self.geos[W_INQ], self.place, f"grad_chip_sum_{W_INQ}",
                                      rows=self.in_rows, row0=0, into=self.in_shard)

    def alone(self, name):
        self.done(name, _comm_only(name, self.host(name)))


CONVW_SHARD = 3 * DFF // N_CHIPS
COND_PAD = 8 * 896
SMALL_SIZES = (("norm1_w", D), ("lb", HW), ("hg_norm_w", DH), ("q_norm_w", DH), ("k_norm_w", DH),
               ("norm2_w", D), ("conv_b", DFF), ("conv_w", 3 * DFF), ("loss", 128))
SMALL_TOTAL = sum(n for _, n in SMALL_SIZES)
STATS_PAD = 8 * 5120


def kernel(x, c, w_ada, b_ada, norm1_w, w_in, lb_logits, hg_norm_w, q_norm_w, k_norm_w, w_out, norm2_w, w_up, conv_w, conv_b, w_down, loss_target, m_w_ada, m_b_ada, m_norm1_w, m_w_in, m_lb_logits, m_hg_norm_w, m_q_norm_w, m_k_norm_w, m_w_out, m_norm2_w, m_w_up, m_conv_w, m_conv_b, m_w_down, v_w_ada, v_b_ada, v_norm1_w, v_w_in, v_lb_logits, v_hg_norm_w, v_q_norm_w, v_k_norm_w, v_w_out, v_norm2_w, v_w_up, v_conv_w, v_conv_b, v_w_down):
    chip = 2 * lax.axis_index("x") + lax.axis_index("y")
    dev = 2 * chip + lax.axis_index("c")

    cond = jnp.concatenate([c, conv_w[0].reshape(1, CONVW_SHARD), jnp.zeros((1, COND_PAD - D - CONVW_SHARD), F32)], axis=1)
    cond_all = _all_gather_rows(cond.reshape(8, COND_PAD // 8), "gather_cond").reshape(N_DEV, COND_PAD)
    c_all = cond_all[:, :D]
    conv_w_full = jnp.concatenate(
        [cond_all[2 * j, D:D + CONVW_SHARD].reshape(3, DFF // N_CHIPS) for j in range(N_CHIPS)], axis=1)

    b_shard = lax.dynamic_slice_in_dim(b_ada, chip * ADA_COLS, ADA_COLS, axis=1)
    mod_cols = _all_gather_rows(_ada_fwd(c_all, w_ada[0], b_shard), "gather_mod")
    mod_all = jnp.concatenate([mod_cols[16 * j:16 * j + 8] for j in range(N_CHIPS)], axis=1)
    mod = lax.dynamic_slice_in_dim(mod_all, dev, 1, axis=0)

    place = jnp.stack([chip, lax.axis_index("c")]).astype(jnp.int32)
    net = _Net([w_in[0], w_out[0], w_up[0], w_down[0]], place)
    net.when_joined = {
        W_OUT: lambda grad: _adamw_sc(w_out[0], grad, m_w_out[0], v_w_out[0], "adamw_sc_w_out"),
        W_DOWN: lambda grad: _adamw_sc(w_down[0], grad, m_w_down[0], v_w_down[0], "adamw_sc_w_down"),
        W_UP: lambda grad: _adamw_sc(w_up[0], grad, m_w_up[0], v_w_up[0], "adamw_sc_w_up"),
    }
    early = {W_OUT: "w_out", W_DOWN: "w_down", W_UP: "w_up"}
    loss_row, grad_x, dmod, small = _sequence_step(
        x[0], loss_target[0], mod, norm1_w, lb_logits, hg_norm_w, q_norm_w, k_norm_w, norm2_w, conv_w_full, conv_b, net)
    small["conv_w"] = small["conv_w"].reshape(1, 3 * DFF)
    small["loss"] = loss_row
    stats = jnp.concatenate([dmod] + [small[k] for k, _ in SMALL_SIZES]
                            + [jnp.zeros((1, STATS_PAD - 6 * D - SMALL_TOTAL), F32)], axis=1)
    stats_all, comm = _all_gather_rows(stats.reshape(8, STATS_PAD // 8), "gather_stats", net.host("gather_stats"))
    net.done("gather_stats", comm)
    stats_all = stats_all.reshape(N_DEV, STATS_PAD)
    last = W_INQ + IN_CHUNKS - 1
    started = _fly_start(False, net.pair[last], net.geos[last], "grad_in_scatter_start")
    dmod_all = stats_all[:, :6 * D] + started[8][0, 0]
    sums = _sum_rows(stats_all[:, 6 * D:6 * D + SMALL_TOTAL], "small_grad_sum")
    g_small, off = {}, 0
    for k, n in SMALL_SIZES:
        g_small[k] = sums[:, off:off + n]
        off += n
    loss = g_small["loss"][0, 0]
    g_b_ada = _sum_rows(dmod_all, "b_ada_grad_sum")
    ada = _ada_bwd_adamw(c_all, lax.dynamic_slice_in_dim(dmod_all, chip * ADA_COLS, ADA_COLS, axis=1),
                         w_ada[0], m_w_ada[0], v_w_ada[0])
    g_w_ada = ada[0]
    pair_last, recv_last = _fly_wait(False, started, net.geos[last], ada[3], "grad_in_scatter_wait")
    net.in_shard = _add_four(pair_last, recv_last, net.geos[last], place, f"grad_chip_sum_{last}",
                             rows=net.in_rows, row0=net._row0(last), into=net.in_shard)
    net.alone("grad_in_join")
    g_out, g_up, g_down = net.shard[W_OUT], net.shard[W_UP], net.shard[W_DOWN]
    g_in = net.in_shard
    g_lb = _lb_grad(lb_logits, g_small["lb"])
    g_conv_w = lax.dynamic_slice_in_dim(g_small["conv_w"].reshape(3, DFF), chip * (DFF // N_CHIPS), DFF // N_CHIPS, axis=1)

    names = ["w_ada", "b_ada", "norm1_w", "w_in", "lb_logits", "hg_norm_w", "q_norm_w", "k_norm_w", "w_out",
             "norm2_w", "w_up", "conv_w", "conv_b", "w_down"]
    lead = {"w_ada", "w_in", "w_out", "w_up", "conv_w", "w_down"}
    w = dict(w_ada=w_ada, b_ada=b_ada, norm1_w=norm1_w, w_in=w_in, lb_logits=lb_logits, hg_norm_w=hg_norm_w,
             q_norm_w=q_norm_w, k_norm_w=k_norm_w, w_out=w_out, norm2_w=norm2_w, w_up=w_up, conv_w=conv_w,
             conv_b=conv_b, w_down=w_down)
    m = dict(w_ada=m_w_ada, b_ada=m_b_ada, norm1_w=m_norm1_w, w_in=m_w_in, lb_logits=m_lb_logits,
             hg_norm_w=m_hg_norm_w, q_norm_w=m_q_norm_w, k_norm_w=m_k_norm_w, w_out=m_w_out, norm2_w=m_norm2_w,
             w_up=m_w_up, conv_w=m_conv_w, conv_b=m_conv_b, w_down=m_w_down)
    v = dict(w_ada=v_w_ada, b_ada=v_b_ada, norm1_w=v_norm1_w, w_in=v_w_in, lb_logits=v_lb_logits,
             hg_norm_w=v_hg_norm_w, q_norm_w=v_q_norm_w, k_norm_w=v_k_norm_w, w_out=v_w_out, norm2_w=v_norm2_w,
             w_up=v_w_up, conv_w=v_conv_w, conv_b=v_conv_b, w_down=v_w_down)
    g = dict(w_ada=g_w_ada, b_ada=g_b_ada, norm1_w=g_small["norm1_w"], w_in=g_in, lb_logits=g_lb,
             hg_norm_w=g_small["hg_norm_w"], q_norm_w=g_small["q_norm_w"], k_norm_w=g_small["k_norm_w"], w_out=g_out,
             norm2_w=g_small["norm2_w"], w_up=g_up, conv_w=g_conv_w, conv_b=g_small["conv_b"], w_down=g_down)

    updated = {early[i]: res for i, res in net.joined.items()}
    updated["w_ada"] = (ada[1], ada[2], ada[3], ada[0])
    grads, deltas, new_m, new_v = [], [], [], []
    for n in names:
        strip = (lambda t: t[0]) if n in lead else (lambda t: t)
        wrap = (lambda t: t[None]) if n in lead else (lambda t: t)
        g_n = g[n]
        if n in updated:
            d_n, m_n, v_n, g_n = updated[n]
        elif n == "w_in":
            d_n, m_n, v_n, g_n = _adamw(strip(w[n]), g_n, strip(m[n]), strip(v[n]), f"adamw_{n}", copy_grad=True)
        else:
            d_n, m_n, v_n = _adamw(strip(w[n]), g_n, strip(m[n]), strip(v[n]), f"adamw_{n}")
        grads.append(wrap(g_n))
        deltas.append(wrap(d_n))
        new_m.append(wrap(m_n))
        new_v.append(wrap(v_n))
    return (loss, grad_x[None], *grads, *deltas, *new_m, *new_v)
```

```python
import functools

import jax
import jax.numpy as jnp
from jax import lax
from jax.experimental import pallas as pl
from jax.experimental.pallas import tpu as pltpu
from jax.experimental.pallas import tpu_sc as plsc

F32 = jnp.float32
BF16 = jnp.bfloat16

S = 2048
D = 2048
H = 8
DH = 128
HW = H * DH
CH = 64
NCH = S // CH
DFF = 5632
INC = 7 * HW
BLK = 128
DILS = (1, 4, 16)
EPS = 1e-6
NEG = -1e30
N_CHIPS = 4
N_DEV = 8

ADAM_LR = 0.001
ADAM_B1 = 0.9
ADAM_B2 = 0.999
ADAM_EPS = 1e-08
ADAM_WD = 0.01
ADAM_STEP = 10
ADAM_BLOCK_BYTES = 1 << 21

V7X_VMEM_BYTES = 64 * 1024 * 1024
VMEM_LIMIT = (V7X_VMEM_BYTES * 3) // 4
MESH = pl.DeviceIdType.MESH
HBM_SPEC = pl.BlockSpec(memory_space=pltpu.HBM)
VMEM_SPEC = pl.BlockSpec(memory_space=pltpu.VMEM)

NN = (((1,), (0,)), ((), ()))
NT = (((1,), (1,)), ((), ()))
TN = (((0,), (0,)), ((), ()))


def _params(n_grid):
    return pltpu.CompilerParams(dimension_semantics=("arbitrary",) * n_grid, vmem_limit_bytes=VMEM_LIMIT)


def _dot(a, b, dims=NN):
    return lax.dot_general(a.astype(BF16), b.astype(BF16), dims, preferred_element_type=F32)


def _dot_f32(a, b):
    return lax.dot_general(a, b, NN, precision=lax.Precision.HIGHEST, preferred_element_type=F32)


def _sigmoid(x):
    return 1.0 / (1.0 + jnp.exp(-x))


def _pick(n, cands):
    for t in cands:
        if n % t == 0:
            return t
    raise ValueError(n)


def _matmul(a, b, mode, out_dtype, name, phase=None, m_blocks=None):
    if mode == "nn":
        (m, k), (_, n) = a.shape, b.shape
    elif mode == "nt":
        (m, k), (n, _) = a.shape, b.shape
    else:
        (k, m), (_, n) = a.shape, b.shape
    tm = _pick(m, (1024, 1408, 512))
    a_block = lambda i: i
    if m_blocks is not None:
        tm, blocks = m_blocks
        m = tm * len(blocks)
        step = blocks[1] - blocks[0] if len(blocks) > 1 else 0
        assert all(blk == blocks[0] + step * i for i, blk in enumerate(blocks))
        a_block = lambda i: blocks[0] + step * i
    tn = _pick(n, (1024, 1408, 512))
    tk = _pick(k, (2048, 2816, 1792, 1024, 512))
    nk = k // tk
    dims = {"nn": NN, "nt": NT, "tn": TN}[mode]

    def body(a_ref, b_ref, o_ref, *acc):
        part = lax.dot_general(a_ref[...], b_ref[...], dims, preferred_element_type=F32)
        if nk == 1:
            o_ref[...] = part.astype(o_ref.dtype)
            return
        acc_ref, kk = acc[0], pl.program_id(2)

        @pl.when(kk == 0)
        def _():
            acc_ref[...] = part

        @pl.when(jnp.logical_and(kk > 0, kk < nk - 1))
        def _():
            acc_ref[...] += part

        @pl.when(kk == nk - 1)
        def _():
            o_ref[...] = (acc_ref[...] + part).astype(o_ref.dtype)

    if mode == "tn":
        a_spec = pl.BlockSpec((tk, tm), lambda i, j, kk: (kk, a_block(i)))
    else:
        a_spec = pl.BlockSpec((tm, tk), lambda i, j, kk: (i, kk))
    if mode == "nt":
        b_spec = pl.BlockSpec((tn, tk), lambda i, j, kk: (j, kk))
    else:
        b_spec = pl.BlockSpec((tk, tn), lambda i, j, kk: (kk, j))
    return _pcall(
        body, [a, b], phase, name=name,
        out_shape=jax.ShapeDtypeStruct((m, n), out_dtype),
        grid=(m // tm, n // tn, nk),
        in_specs=[a_spec, b_spec],
        out_specs=pl.BlockSpec((tm, tn), lambda i, j, kk: (i, j)),
        scratch_shapes=[pltpu.VMEM((tm, tn), F32)] if nk > 1 else [],
        compiler_params=_params(3),
    )


TR = 256


def _row_spec(cols=D):
    return pl.BlockSpec((TR, cols), lambda i: (i, 0))


def _vec_spec(cols=D):
    return pl.BlockSpec((1, cols), lambda i: (0, 0))


def _norm_mod(x, nw, scale, shift, name):
    def body(x_ref, nw_ref, sc_ref, sh_ref, h_ref):
        xv = x_ref[...]
        r = lax.rsqrt(jnp.mean(xv * xv, axis=-1, keepdims=True) + EPS)
        h_ref[...] = ((xv * r) * nw_ref[...] * (1.0 + sc_ref[...]) + sh_ref[...]).astype(BF16)

    return pl.pallas_call(
        body, name=name, out_shape=jax.ShapeDtypeStruct((S, D), BF16), grid=(S // TR,),
        in_specs=[_row_spec(), _vec_spec(), _vec_spec(), _vec_spec()], out_specs=_row_spec(),
        compiler_params=_params(1),
    )(x, nw, scale, shift)


def _resid_norm_mod(x, mix, gate, nw, scale, shift, name):
    def body(x_ref, mix_ref, g_ref, nw_ref, sc_ref, sh_ref, x1_ref, h_ref):
        xv = x_ref[...] + g_ref[...] * mix_ref[...]
        x1_ref[...] = xv
        r = lax.rsqrt(jnp.mean(xv * xv, axis=-1, keepdims=True) + EPS)
        h_ref[...] = ((xv * r) * nw_ref[...] * (1.0 + sc_ref[...]) + sh_ref[...]).astype(BF16)

    return pl.pallas_call(
        body, name=name,
        out_shape=(jax.ShapeDtypeStruct((S, D), F32), jax.ShapeDtypeStruct((S, D), BF16)), grid=(S // TR,),
        in_specs=[_row_spec(), _row_spec(), _vec_spec(), _vec_spec(), _vec_spec(), _vec_spec()],
        out_specs=(_row_spec(), _row_spec()),
        compiler_params=_params(1),
    )(x, mix, gate, nw, scale, shift)


def _norm_mod_bwd(dh, xin, nw, scale, dres, name, mix=None, gate=None, phase=None):
    with_gate = mix is not None

    def body(*refs):
        if with_gate:
            dh_ref, x_ref, nw_ref, sc_ref, dres_ref, mix_ref, g_ref, dx_ref, dsh_ref, dsc_ref, dnw_ref, dg_ref, dmix_ref = refs
        else:
            dh_ref, x_ref, nw_ref, sc_ref, dres_ref, dx_ref, dsh_ref, dsc_ref, dnw_ref = refs
        i = pl.program_id(0)
        dhv = dh_ref[...]
        xv = x_ref[...]
        r = lax.rsqrt(jnp.mean(xv * xv, axis=-1, keepdims=True) + EPS)
        xn = xv * r
        nwv = nw_ref[...]
        one_sc = 1.0 + sc_ref[...]
        dxn = dhv * nwv * one_sc
        dx = dres_ref[...] + r * (dxn - xn * jnp.mean(dxn * xn, axis=-1, keepdims=True))
        dx_ref[...] = dx

        @pl.when(i == 0)
        def _():
            dsh_ref[...] = jnp.zeros_like(dsh_ref)
            dsc_ref[...] = jnp.zeros_like(dsc_ref)
            dnw_ref[...] = jnp.zeros_like(dnw_ref)
            if with_gate:
                dg_ref[...] = jnp.zeros_like(dg_ref)

        dsh_ref[...] += jnp.sum(dhv, axis=0, keepdims=True)
        dsc_ref[...] += jnp.sum(dhv * xn * nwv, axis=0, keepdims=True)
        dnw_ref[...] += jnp.sum(dhv * xn * one_sc, axis=0, keepdims=True)
        if with_gate:
            dg_ref[...] += jnp.sum(dx * mix_ref[...], axis=0, keepdims=True)
            dmix_ref[...] = (dx * g_ref[...]).astype(BF16)

    vec = jax.ShapeDtypeStruct((1, D), F32)
    ins = [dh, xin, nw, scale, dres]
    in_specs = [_row_spec(), _row_spec(), _vec_spec(), _vec_spec(), _row_spec()]
    outs = [jax.ShapeDtypeStruct((S, D), F32), vec, vec, vec]
    out_specs = [_row_spec(), _vec_spec(), _vec_spec(), _vec_spec()]
    if with_gate:
        ins += [mix, gate]
        in_specs += [_row_spec(), _vec_spec()]
        outs += [vec, jax.ShapeDtypeStruct((S, D), BF16)]
        out_specs += [_vec_spec(), _row_spec()]
    return _pcall(
        body, ins, phase, name=name, out_shape=tuple(outs), grid=(S // TR,), in_specs=in_specs,
        out_specs=tuple(out_specs), compiler_params=_params(1),
    )


def _tri(lower):
    row = lax.broadcasted_iota(jnp.int32, (CH, CH), 0)
    col = lax.broadcasted_iota(jnp.int32, (CH, CH), 1)
    return (row >= col) if lower else (col >= row)


def _hgrn_gates(hq, hf, lb):
    sig = _sigmoid(hf)
    f = lb + (1.0 - lb) * sig
    g = jnp.log(f)
    sq = _sigmoid(hq)
    return sig, f, g, 1.0 - f, hq * sq, sq


HG_HP = 2
HG_UNROLL = 8


def _proj_col_spec(group):
    return pl.BlockSpec((S, HG_HP * DH), lambda h: (0, group * (H // HG_HP) + h))


def _hgrn_fwd(proj, lb_logits, norm_w, phase=None):
    def body(hq_ref, hf_ref, hi_ref, hg_ref, lbl_ref, nw_ref, out_ref, oraw_ref, st_ref, s_ref):
        nw = nw_ref[...]
        lower = _tri(True)
        ltri = lower.astype(F32)
        s_ref[...] = jnp.zeros_like(s_ref)

        def chunk(n, carry):
            rows = pl.ds(pl.multiple_of(n * CH, CH), CH)
            for j in range(HG_HP):
                cols = slice(j * DH, (j + 1) * DH)
                lb = 1.0 / (1.0 + jnp.exp(lbl_ref[1:2, cols] - lbl_ref[0:1, cols]))
                hq, hf, v, hg = hq_ref[rows, cols], hf_ref[rows, cols], hi_ref[rows, cols], hg_ref[rows, cols]
                _, _, g, kk, q, _ = _hgrn_gates(hq, hf, lb)
                gc = _dot_f32(ltri, g)
                gl = jnp.sum(g, axis=0, keepdims=True)
                qe = q * jnp.exp(gc)
                ke = kk * jnp.exp(gl - gc)
                qh = q * jnp.exp(gc - 0.5 * gl)
                kh = kk * jnp.exp(0.5 * gl - gc)
                st = s_ref[j]
                st_ref[j, pl.ds(n, 1)] = st[None]
                a = jnp.where(lower, _dot(qh, kh, NT), 0.0)
                o = _dot(qe, st, NT) + _dot(a, v)
                s_ref[j] = st * jnp.exp(gl) + _dot(v, ke, TN)
                oraw_ref[rows, cols] = o
                rs = lax.rsqrt(jnp.mean(o * o, axis=-1, keepdims=True) + EPS)
                out_ref[rows, cols] = (o * rs * nw * (hg * _sigmoid(hg))).astype(BF16)
            return carry

        lax.fori_loop(0, NCH, chunk, 0, unroll=HG_UNROLL)

    head_spec = pl.BlockSpec((S, HG_HP * DH), lambda h: (0, h))
    return _pcall(
        body, [proj, proj, proj, proj, lb_logits, norm_w], phase, name="hgrn_fwd",
        out_shape=(jax.ShapeDtypeStruct((S, 2 * HW), BF16), jax.ShapeDtypeStruct((S, HW), F32),
                   jax.ShapeDtypeStruct((H, NCH, DH, DH), F32)),
        grid=(H // HG_HP,),
        in_specs=[_proj_col_spec(0), _proj_col_spec(1), _proj_col_spec(2), _proj_col_spec(3),
                  pl.BlockSpec((2, HG_HP * DH), lambda h: (0, h)), pl.BlockSpec((1, DH), lambda h: (0, 0))],
        out_specs=(head_spec, head_spec, pl.BlockSpec((HG_HP, NCH, DH, DH), lambda h: (h, 0, 0, 0))),
        scratch_shapes=[pltpu.VMEM((HG_HP, DH, DH), F32)],
        compiler_params=_params(1),
    )


def _hgrn_bwd(proj, lb_logits, norm_w, oraw, states, dout, phase=None):
    def body(hq_ref, hf_ref, hi_ref, hg_ref, lbl_ref, nw_ref, oraw_ref, st_ref, do_ref,
             dhq_ref, dhf_ref, dhi_ref, dhg_ref, dlb_ref, dnw_ref, ds_ref, acc_ref):
        h = pl.program_id(0)
        nw = nw_ref[...]
        lower = _tri(True)
        ltri = lower.astype(F32)
        utri = _tri(False).astype(F32)
        last = lax.broadcasted_iota(jnp.int32, (CH, DH), 0) == CH - 1
        ds_ref[...] = jnp.zeros_like(ds_ref)
        acc_ref[...] = jnp.zeros_like(acc_ref)

        @pl.when(h == 0)
        def _():
            dnw_ref[...] = jnp.zeros_like(dnw_ref)

        def chunk(i, carry):
            n = NCH - 1 - i
            rows = pl.ds(pl.multiple_of(n * CH, CH), CH)
            for j in range(HG_HP):
                cols = slice(j * DH, (j + 1) * DH)
                lb = 1.0 / (1.0 + jnp.exp(lbl_ref[1:2, cols] - lbl_ref[0:1, cols]))
                hq, hf, v, hg = hq_ref[rows, cols], hf_ref[rows, cols], hi_ref[rows, cols], hg_ref[rows, cols]
                sig, f, g, kk, q, sq = _hgrn_gates(hq, hf, lb)
                gc = _dot_f32(ltri, g)
                gl = jnp.sum(g, axis=0, keepdims=True)
                eg = jnp.exp(gc)
                ek = jnp.exp(gl - gc)
                gam = jnp.exp(gl)
                ph = jnp.exp(gc - 0.5 * gl)
                pk = jnp.exp(0.5 * gl - gc)
                qe, ke = q * eg, kk * ek
                qh, kh = q * ph, kk * pk
                st = st_ref[j, pl.ds(n, 1)][0]
                dst = ds_ref[j]
                a = jnp.where(lower, _dot(qh, kh, NT), 0.0)
                o = oraw_ref[rows, cols]
                rs = lax.rsqrt(jnp.mean(o * o, axis=-1, keepdims=True) + EPS)
                xh = o * rs
                sg = _sigmoid(hg)
                dgo = do_ref[rows, cols]
                d_on = dgo * (hg * sg)
                dhg_ref[rows, cols] = (dgo * xh * nw * (sg * (1.0 + hg * (1.0 - sg)))).astype(BF16)
                acc_ref[j, 1:2, :] += jnp.sum(d_on * xh, axis=0, keepdims=True)
                dy = d_on * nw
                do = rs * (dy - xh * jnp.mean(dy * xh, axis=-1, keepdims=True))
                dqe = _dot(do, st)
                da = jnp.where(lower, _dot(do, v, NT), 0.0)
                dv = _dot(a, do, TN) + _dot(ke, dst, NT)
                dke = _dot(v, dst)
                dgam = jnp.sum(dst * st, axis=0, keepdims=True)
                dqh = lax.dot_general(da, kh, NN, precision=lax.Precision.HIGH, preferred_element_type=F32)
                dkh = lax.dot_general(da, qh, TN, precision=lax.Precision.HIGH, preferred_element_type=F32)
                dq = dqh * ph + dqe * eg
                dk = dkh * pk + dke * ek
                dgl = jnp.sum(dke * ke, axis=0, keepdims=True) + dgam * gam
                dgc = dqh * qh - dkh * kh + dqe * qe - dke * ke + jnp.where(last, dgl, 0.0)
                dg = _dot_f32(utri, dgc)
                df = dg / f - dk
                dhf_ref[rows, cols] = (df * (1.0 - lb) * sig * (1.0 - sig)).astype(BF16)
                acc_ref[j, 0:1, :] += jnp.sum(df * (1.0 - sig), axis=0, keepdims=True)
                dhq_ref[rows, cols] = (dq * (sq * (1.0 + hq * (1.0 - sq)))).astype(BF16)
                dhi_ref[rows, cols] = dv.astype(BF16)
                ds_ref[j] = dst * gam + _dot(do, qe, TN)
            return carry

        lax.fori_loop(0, NCH, chunk, 0, unroll=HG_UNROLL)
        for j in range(HG_HP):
            dlb_ref[:, j * DH:(j + 1) * DH] = acc_ref[j, 0:1, :]
            dnw_ref[...] += acc_ref[j, 1:2, :]

    head_spec = pl.BlockSpec((S, HG_HP * DH), lambda h: (0, h))
    head_out = jax.ShapeDtypeStruct((S, HW), BF16)
    return _pcall(
        body, [proj, proj, proj, proj, lb_logits, norm_w, oraw, states, dout], phase, name="hgrn_bwd",
        out_shape=(head_out, head_out, head_out, head_out,
                   jax.ShapeDtypeStruct((1, HW), F32), jax.ShapeDtypeStruct((1, DH), F32)),
        grid=(H // HG_HP,),
        in_specs=[_proj_col_spec(0), _proj_col_spec(1), _proj_col_spec(2), _proj_col_spec(3),
                  pl.BlockSpec((2, HG_HP * DH), lambda h: (0, h)), pl.BlockSpec((1, DH), lambda h: (0, 0)),
                  head_spec, pl.BlockSpec((HG_HP, NCH, DH, DH), lambda h: (h, 0, 0, 0)), head_spec],
        out_specs=(head_spec, head_spec, head_spec, head_spec,
                   pl.BlockSpec((1, HG_HP * DH), lambda h: (0, h)), pl.BlockSpec((1, DH), lambda h: (0, 0))),
        scratch_shapes=[pltpu.VMEM((HG_HP, DH, DH), F32), pltpu.VMEM((HG_HP, 8, DH), F32)],
        compiler_params=_params(1),
    )


ATT_SCALE = DH ** -0.5
HEADS_PER_STEP = {1: 8, 4: 1, 16: 1}


def _sub_rows(ref, r, dil, cols):
    if dil == 1:
        return ref[:, cols]
    return ref[pl.ds(r, BLK, stride=dil), cols]


def _set_sub_rows(ref, r, dil, cols, val):
    if dil == 1:
        ref[:, cols] = val
    else:
        ref[pl.ds(r, BLK, stride=dil), cols] = val


def _slopes(dil):
    hp = HEADS_PER_STEP[dil]
    s = jnp.asarray([dil * 2.0 ** (-(h + 1)) for h in range(H)], F32)
    return jnp.broadcast_to(s[:, None], (H, DH)).reshape(H // hp, hp, DH)


def _rms_rows(x):
    rs = lax.rsqrt(jnp.mean(x * x, axis=-1, keepdims=True) + EPS)
    return x * rs, rs


def _att_masks():
    qi = lax.broadcasted_iota(jnp.int32, (BLK, BLK), 0)
    kj = lax.broadcasted_iota(jnp.int32, (BLK, BLK), 1)
    steps_c = qi - kj
    steps_p = qi - kj + BLK
    return steps_c, steps_p, steps_c >= 0, steps_p <= BLK


def _qk_prep(proj, q_w, k_w):
    def body(q_ref, k_ref, qw_ref, kw_ref, qn_ref, kn_ref):
        for h in range(H):
            cols = slice(h * DH, (h + 1) * DH)
            qn_ref[:, cols] = _rms_rows(q_ref[:, cols])[0] * qw_ref[...]
            kn_ref[:, cols] = _rms_rows(k_ref[:, cols])[0] * kw_ref[...]

    out = jax.ShapeDtypeStruct((S, HW), F32)
    wspec = pl.BlockSpec((1, DH), lambda i: (0, 0))
    return pl.pallas_call(
        body, name="qk_prep", out_shape=(out, out), grid=(S // TR,),
        in_specs=[pl.BlockSpec((TR, HW), lambda i: (i, 4)), pl.BlockSpec((TR, HW), lambda i: (i, 5)), wspec, wspec],
        out_specs=(_row_spec(HW), _row_spec(HW)),
        compiler_params=_params(1),
    )(proj, proj, q_w, k_w)


def _qk_norm_bwd(proj, dqn, dkn, dv, q_w, k_w):
    def body(q_ref, k_ref, dqn_ref, dkn_ref, dv_ref, qw_ref, kw_ref, dq_ref, dk_ref, dvo_ref, dqw_ref, dkw_ref):
        i = pl.program_id(0)

        @pl.when(i == 0)
        def _():
            dqw_ref[...] = jnp.zeros_like(dqw_ref)
            dkw_ref[...] = jnp.zeros_like(dkw_ref)

        dvo_ref[...] = dv_ref[...].astype(BF16)
        for x_ref, d_ref, w_ref, o_ref, dw_ref in ((q_ref, dqn_ref, qw_ref, dq_ref, dqw_ref),
                                                   (k_ref, dkn_ref, kw_ref, dk_ref, dkw_ref)):
            for h in range(H):
                cols = slice(h * DH, (h + 1) * DH)
                xh, rs = _rms_rows(x_ref[:, cols])
                d = d_ref[:, cols]
                dw_ref[...] += jnp.sum(d * xh, axis=0, keepdims=True)
                dy = d * w_ref[...]
                o_ref[:, cols] = (rs * (dy - xh * jnp.mean(dy * xh, axis=-1, keepdims=True))).astype(BF16)

    big = jax.ShapeDtypeStruct((S, HW), BF16)
    small = jax.ShapeDtypeStruct((1, DH), F32)
    wspec = pl.BlockSpec((1, DH), lambda i: (0, 0))
    return pl.pallas_call(
        body, name="qk_norm_bwd", out_shape=(big, big, big, small, small), grid=(S // TR,),
        in_specs=[pl.BlockSpec((TR, HW), lambda i: (i, 4)), pl.BlockSpec((TR, HW), lambda i: (i, 5)),
                  _row_spec(HW), _row_spec(HW), _row_spec(HW), wspec, wspec],
        out_specs=(_row_spec(HW), _row_spec(HW), _row_spec(HW), wspec, wspec),
        compiler_params=_params(1),
    )(proj, proj, dqn, dkn, dv, q_w, k_w)


def _attn_delta(do, o):
    def body(do_ref, o_ref, d_ref):
        for h in range(H):
            cols = slice(h * DH, (h + 1) * DH)
            d_ref[:, cols] = jnp.broadcast_to(jnp.sum(do_ref[:, cols] * o_ref[:, cols], axis=-1, keepdims=True), (TR, DH))

    return pl.pallas_call(
        body, name="attn_delta", out_shape=jax.ShapeDtypeStruct((S, HW), F32), grid=(S // TR,),
        in_specs=[pl.BlockSpec((TR, HW), lambda i: (i, 1)), _row_spec(HW)], out_specs=_row_spec(HW),
        compiler_params=_params(1),
    )(do, o)


def _attn_fwd(proj, qn, kn, dil, phase=None):
    hp = HEADS_PER_STEP[dil]
    rows, ng, nb = BLK * dil, H // hp, S // (BLK * dil)

    def body(q_ref, kc_ref, kp_ref, vc_ref, vp_ref, sl_ref, o_ref, lse_ref):
        n = pl.program_id(0)
        steps_c, steps_p, ok_c, ok_p = _att_masks()
        ok_p = jnp.logical_and(ok_p, n > 0)
        fc, fp = steps_c.astype(F32), steps_p.astype(F32)
        for hh in range(hp):
            cols = slice(hh * DH, (hh + 1) * DH)
            sl = sl_ref[0, hh:hh + 1, :]
            for r in range(dil):
                sub = lambda ref: _sub_rows(ref, r, dil, cols)
                qv = sub(q_ref)
                sc = jnp.where(ok_c, _dot(qv, sub(kc_ref), NT) * ATT_SCALE - sl * fc, NEG)
                sp = jnp.where(ok_p, _dot(qv, sub(kp_ref), NT) * ATT_SCALE - sl * fp, NEG)
                m = jnp.maximum(jnp.max(sc, axis=-1, keepdims=True), jnp.max(sp, axis=-1, keepdims=True))
                pc, pp = jnp.exp(sc - m), jnp.exp(sp - m)
                den = jnp.sum(pc, axis=-1, keepdims=True) + jnp.sum(pp, axis=-1, keepdims=True)
                o = (_dot(pc, sub(vc_ref)) + _dot(pp, sub(vp_ref))) / den
                _set_sub_rows(o_ref, r, dil, cols, o)
                _set_sub_rows(lse_ref, r, dil, cols, jnp.broadcast_to(m + jnp.log(den), (BLK, DH)))

    cur = pl.BlockSpec((rows, hp * DH), lambda n, g: (n, g))
    prev = pl.BlockSpec((rows, hp * DH), lambda n, g: (jnp.maximum(n - 1, 0), g))
    vcur = pl.BlockSpec((rows, hp * DH), lambda n, g: (n, 6 * ng + g))
    vprev = pl.BlockSpec((rows, hp * DH), lambda n, g: (jnp.maximum(n - 1, 0), 6 * ng + g))
    out = jax.ShapeDtypeStruct((S, HW), F32)
    return _pcall(
        body, [qn, kn, kn, proj, proj, _slopes(dil)], phase,
        name=f"attn_fwd_d{dil}", out_shape=(out, out), grid=(nb, ng),
        in_specs=[cur, cur, prev, vcur, vprev, pl.BlockSpec((1, hp, DH), lambda n, g: (g, 0, 0))],
        out_specs=(cur, cur),
        compiler_params=_params(2),
    )


def _attn_merge(outs, lses, cat):
    def body(o1, o2, o3, l1, l2, l3, cat_ref, ob_ref, of_ref, lse_ref):
        a, b, c = l1[...], l2[...], l3[...]
        m = jnp.maximum(jnp.maximum(a, b), c)
        ea, eb, ec = jnp.exp(a - m), jnp.exp(b - m), jnp.exp(c - m)
        den = ea + eb + ec
        o = (ea * o1[...] + eb * o2[...] + ec * o3[...]) / den
        ob_ref[...] = o.astype(BF16)
        of_ref[...] = o
        lse_ref[...] = m + jnp.log(den)

    f = jax.ShapeDtypeStruct((S, HW), F32)
    return pl.pallas_call(
        body, name="attn_merge", out_shape=(jax.ShapeDtypeStruct((S, 2 * HW), BF16), f, f), grid=(S // TR,),
        in_specs=[_row_spec(HW)] * 6 + [pl.BlockSpec(memory_space=pl.ANY)],
        out_specs=(pl.BlockSpec((TR, HW), lambda i: (i, 1)), _row_spec(HW), _row_spec(HW)),
        input_output_aliases={6: 0},
        compiler_params=_params(1),
    )(*outs, *lses, cat)


def _attn_bwd(proj, qn, kn, lse, delta, do, dil, acc, phase=None):
    hp = HEADS_PER_STEP[dil]
    rows, ng, nb = BLK * dil, H // hp, S // (BLK * dil)
    has_acc = acc is not None

    def body(*refs):
        q_ref, qn_ref, kp_ref, kc_ref, vp_ref, vc_ref, l_ref, ln_ref, d_ref, dn_ref, do_ref, don_ref, sl_ref = refs[:13]
        refs = refs[13:]
        if has_acc:
            aq_ref, ak_ref, av_ref = refs[:3]
            refs = refs[3:]
        dq_ref, dk_ref, dv_ref = refs
        m = pl.program_id(0)
        steps_c, steps_p, ok_c, ok_p = _att_masks()
        ok_prev = jnp.logical_and(ok_p, m > 0)
        ok_next = jnp.logical_and(ok_p, m < nb - 1)
        fc, fp = steps_c.astype(F32), steps_p.astype(F32)
        for hh in range(hp):
            cols = slice(hh * DH, (hh + 1) * DH)
            sl = sl_ref[0, hh:hh + 1, :]
            for r in range(dil):
                sub = lambda ref: _sub_rows(ref, r, dil, cols)
                qv, qnv, kcv, kpv = sub(q_ref), sub(qn_ref), sub(kc_ref), sub(kp_ref)
                vc, vp = sub(vc_ref), sub(vp_ref)
                dov, donv = sub(do_ref), sub(don_ref)
                lse_m, lse_n = sub(l_ref)[:, 0:1], sub(ln_ref)[:, 0:1]
                delta_m, delta_n = sub(d_ref)[:, 0:1], sub(dn_ref)[:, 0:1]
                p_c = jnp.where(ok_c, jnp.exp(_dot(qv, kcv, NT) * ATT_SCALE - sl * fc - lse_m), 0.0)
                p_p = jnp.where(ok_prev, jnp.exp(_dot(qv, kpv, NT) * ATT_SCALE - sl * fp - lse_m), 0.0)
                p_n = jnp.where(ok_next, jnp.exp(_dot(qnv, kcv, NT) * ATT_SCALE - sl * fp - lse_n), 0.0)
                ds_c = p_c * (_dot(dov, vc, NT) - delta_m)
                ds_p = p_p * (_dot(dov, vp, NT) - delta_m)
                ds_n = p_n * (_dot(donv, vc, NT) - delta_n)
                dqn = (_dot(ds_c, kcv) + _dot(ds_p, kpv)) * ATT_SCALE
                dkn = (_dot(ds_c, qv, TN) + _dot(ds_n, qnv, TN)) * ATT_SCALE
                dv = _dot(p_c, dov, TN) + _dot(p_n, donv, TN)
                if has_acc:
                    dqn, dkn, dv = dqn + sub(aq_ref), dkn + sub(ak_ref), dv + sub(av_ref)
                _set_sub_rows(dq_ref, r, dil, cols, dqn)
                _set_sub_rows(dk_ref, r, dil, cols, dkn)
                _set_sub_rows(dv_ref, r, dil, cols, dv)

    def shifted(shift, m):
        if shift < 0:
            return jnp.maximum(m - 1, 0)
        if shift > 0:
            return jnp.minimum(m + 1, nb - 1)
        return m

    def spec(shift, group=0):
        return pl.BlockSpec((rows, hp * DH), lambda m, g: (shifted(shift, m), group * ng + g))

    ins = [qn, qn, kn, kn, proj, proj, lse, lse, delta, delta, do, do, _slopes(dil)]
    in_specs = [spec(0), spec(1), spec(-1), spec(0), spec(-1, 6), spec(0, 6), spec(0), spec(1), spec(0), spec(1),
                spec(0, 1), spec(1, 1), pl.BlockSpec((1, hp, DH), lambda m, g: (g, 0, 0))]
    if has_acc:
        ins += list(acc)
        in_specs += [spec(0)] * 3
    big = jax.ShapeDtypeStruct((S, HW), F32)
    return _pcall(
        body, ins, phase, name=f"attn_bwd_d{dil}", out_shape=(big, big, big), grid=(nb, ng),
        in_specs=in_specs, out_specs=(spec(0),) * 3,
        compiler_params=_params(2),
    )


TC = 512
NCT = DFF // TC


def _shift_rows(a, k):
    rows = lax.broadcasted_iota(jnp.int32, a.shape, 0)
    rolled = pltpu.roll(a, k % S, 0)
    if k > 0:
        return jnp.where(rows >= k, rolled, 0.0)
    return jnp.where(rows < S + k, rolled, 0.0)


def _conv_pre(a, w_ref, b_ref):
    return b_ref[...] + w_ref[0:1, :] * _shift_rows(a, 2) + w_ref[1:2, :] * _shift_rows(a, 1) + w_ref[2:3, :] * a


def _conv_gate_fwd(u, conv_w, conv_b, name, tiles=(0, NCT), into=None, phase=None):
    t0, t1 = tiles

    def body(a_ref, g_ref, w_ref, b_ref, *rest):
        pre = _conv_pre(a_ref[...].astype(F32), w_ref, b_ref)
        rest[-1][...] = (pre * _sigmoid(pre) * g_ref[...].astype(F32)).astype(BF16)

    args = [u, u, conv_w, conv_b]
    in_specs = [pl.BlockSpec((S, TC), lambda i: (0, t0 + i)), pl.BlockSpec((S, TC), lambda i: (0, NCT + t0 + i)),
                pl.BlockSpec((3, TC), lambda i: (0, t0 + i)), pl.BlockSpec((1, TC), lambda i: (0, t0 + i))]
    kw = {}
    if into is not None:
        args.append(into)
        in_specs.append(pl.BlockSpec(memory_space=pl.ANY))
        kw["input_output_aliases"] = {4: 0}
    return _pcall(
        body, args, phase, name=name, out_shape=jax.ShapeDtypeStruct((S, DFF), BF16), grid=(t1 - t0,),
        in_specs=in_specs, out_specs=pl.BlockSpec((S, TC), lambda i: (0, t0 + i)),
        compiler_params=_params(1), **kw,
    )


def _conv_gate_bwd(dy, u, conv_w, conv_b, phase=None):
    def body(dy_ref, a_ref, g_ref, w_ref, b_ref, du_ref, db_ref, dw_ref, da_buf, dg_buf, sems):
        i = pl.program_id(0)
        slot = i % 2

        def writes(step, s):
            col = pl.multiple_of(step * TC, TC)
            return (pltpu.make_async_copy(da_buf.at[s], du_ref.at[:, pl.ds(col, TC)], sems.at[0, s]),
                    pltpu.make_async_copy(dg_buf.at[s], du_ref.at[:, pl.ds(DFF + col, TC)], sems.at[1, s]))

        @pl.when(i >= 2)
        def _():
            for cp in writes(i - 2, slot):
                cp.wait()

        a = a_ref[...].astype(F32)
        dyv = dy_ref[...].astype(F32)
        pre = _conv_pre(a, w_ref, b_ref)
        sg = _sigmoid(pre)
        dg_buf[slot] = (dyv * pre * sg).astype(BF16)
        dpre = dyv * g_ref[...].astype(F32) * (sg * (1.0 + pre * (1.0 - sg)))
        da = w_ref[2:3, :] * dpre + w_ref[1:2, :] * _shift_rows(dpre, -1) + w_ref[0:1, :] * _shift_rows(dpre, -2)
        da_buf[slot] = da.astype(BF16)
        for cp in writes(i, slot):
            cp.start()
        db_ref[...] = jnp.sum(dpre, axis=0, keepdims=True)
        dw_ref[0:1, :] = jnp.sum(dpre * _shift_rows(a, 2), axis=0, keepdims=True)
        dw_ref[1:2, :] = jnp.sum(dpre * _shift_rows(a, 1), axis=0, keepdims=True)
        dw_ref[2:3, :] = jnp.sum(dpre * a, axis=0, keepdims=True)

        @pl.when(i == NCT - 1)
        def _():
            for cp in writes(i - 1, 1 - slot) + writes(i, slot):
                cp.wait()

    col = pl.BlockSpec((S, TC), lambda i: (0, i))
    return _pcall(
        body, [dy, u, u, conv_w, conv_b], phase, name="conv_gate_bwd",
        out_shape=(jax.ShapeDtypeStruct((S, 2 * DFF), BF16), jax.ShapeDtypeStruct((1, DFF), F32),
                   jax.ShapeDtypeStruct((3, DFF), F32)),
        grid=(NCT,),
        in_specs=[col, col, pl.BlockSpec((S, TC), lambda i: (0, NCT + i)),
                  pl.BlockSpec((3, TC), lambda i: (0, i)), pl.BlockSpec((1, TC), lambda i: (0, i))],
        out_specs=(pl.BlockSpec(memory_space=pl.ANY), pl.BlockSpec((1, TC), lambda i: (0, i)),
                   pl.BlockSpec((3, TC), lambda i: (0, i))),
        scratch_shapes=[pltpu.VMEM((2, S, TC), BF16), pltpu.VMEM((2, S, TC), BF16), pltpu.SemaphoreType.DMA((2, 2))],
        compiler_params=_params(1),
    )


def _out_loss(z, x1, target, gate):
    def body(z_ref, x_ref, t_ref, g_ref, do_ref, dz_ref, dg_ref, loss_ref):
        i = pl.program_id(0)
        zv = z_ref[...]
        gv = g_ref[...]
        err = x_ref[...] + gv * zv - t_ref[...]
        dout = err * (1.0 / D)
        do_ref[...] = dout
        dz_ref[...] = (dout * gv).astype(BF16)

        @pl.when(i == 0)
        def _():
            dg_ref[...] = jnp.zeros_like(dg_ref)
            loss_ref[...] = jnp.zeros_like(loss_ref)

        dg_ref[...] += jnp.sum(dout * zv, axis=0, keepdims=True)
        part = jnp.sum(jnp.sum(err * err, axis=0, keepdims=True), axis=-1, keepdims=True) * (0.5 / D)
        lane = lax.broadcasted_iota(jnp.int32, (1, 128), 1)
        loss_ref[...] += jnp.where(lane == 0, part, 0.0)

    return pl.pallas_call(
        body, name="out_loss",
        out_shape=(jax.ShapeDtypeStruct((S, D), F32), jax.ShapeDtypeStruct((S, D), BF16),
                   jax.ShapeDtypeStruct((1, D), F32), jax.ShapeDtypeStruct((1, 128), F32)),
        grid=(S // TR,),
        in_specs=[_row_spec(), _row_spec(), _row_spec(), _vec_spec()],
        out_specs=(_row_spec(), _row_spec(), _vec_spec(), _vec_spec(128)),
        compiler_params=_params(1),
    )(z, x1, target, gate)


ADA_COLS = 6 * D // N_CHIPS
TA = 512


def _ada_fwd(c_all, w_shard, b_shard):
    def body(c_ref, w_ref, b_ref, o_ref):
        cv = c_ref[...]
        o_ref[...] = _dot_f32(cv * _sigmoid(cv), w_ref[...]) + b_ref[...]

    return pl.pallas_call(
        body, name="ada_fwd", out_shape=jax.ShapeDtypeStruct((N_DEV, ADA_COLS), F32), grid=(ADA_COLS // TA,),
        in_specs=[pl.BlockSpec((N_DEV, D), lambda j: (0, 0)), pl.BlockSpec((D, TA), lambda j: (0, j)),
                  pl.BlockSpec((1, TA), lambda j: (0, j))],
        out_specs=pl.BlockSpec((N_DEV, TA), lambda j: (0, j)),
        compiler_params=_params(1),
    )(c_all, w_shard, b_shard)


ADA_ROWS = 128


def _ada_bwd_adamw(c_all, dmod_shard, w, m, v):
    def body(c_ref, d_ref, w_ref, m_ref, v_ref, g_ref, dl_ref, mo_ref, vo_ref):
        cv = c_ref[...]
        gv = lax.dot_general(cv * _sigmoid(cv), d_ref[...], TN, precision=lax.Precision.HIGHEST,
                             preferred_element_type=F32)
        g_ref[...] = gv
        m2 = ADAM_B1 * m_ref[...] + (1.0 - ADAM_B1) * gv
        v2 = ADAM_B2 * v_ref[...] + (1.0 - ADAM_B2) * (gv * gv)
        m_hat = m2 / (1.0 - ADAM_B1 ** ADAM_STEP)
        v_hat = v2 / (1.0 - ADAM_B2 ** ADAM_STEP)
        dl_ref[...] = -ADAM_LR * (m_hat / (jnp.sqrt(v_hat) + ADAM_EPS) + ADAM_WD * w_ref[...])
        mo_ref[...] = m2
        vo_ref[...] = v2

    spec = pl.BlockSpec((ADA_ROWS, ADA_COLS), lambda i: (i, 0))
    out = jax.ShapeDtypeStruct((D, ADA_COLS), F32)
    return pl.pallas_call(
        body, name="ada_bwd_adamw", out_shape=(out,) * 4, grid=(D // ADA_ROWS,),
        in_specs=[pl.BlockSpec((N_DEV, ADA_ROWS), lambda i: (0, i)), pl.BlockSpec((N_DEV, ADA_COLS), lambda i: (0, 0)),
                  spec, spec, spec],
        out_specs=(spec,) * 4,
        compiler_params=_params(1),
    )(c_all, dmod_shard, w, m, v)


def _sum_rows(g, name):
    rows, n = g.shape

    def body(g_ref, o_ref):
        acc = g_ref[0:1, :]
        for i in range(1, rows):
            acc = acc + g_ref[i:i + 1, :]
        o_ref[...] = acc

    return pl.pallas_call(
        body, name=name, out_shape=jax.ShapeDtypeStruct((1, n), F32),
        in_specs=[VMEM_SPEC], out_specs=VMEM_SPEC,
    )(g)


def _lb_grad(lb_logits, dlb):
    def body(l_ref, d_ref, o_ref):
        p = 1.0 / (1.0 + jnp.exp(l_ref[1:2, :] - l_ref[0:1, :]))
        t = d_ref[...] * p * (1.0 - p)
        o_ref[0:1, :] = t
        o_ref[1:2, :] = -t

    return pl.pallas_call(
        body, name="lb_grad", out_shape=jax.ShapeDtypeStruct((2, HW), F32),
        in_specs=[VMEM_SPEC, VMEM_SPEC], out_specs=VMEM_SPEC,
    )(lb_logits, dlb)


def _adamw(w, g, m, v, name, copy_grad=False):
    rows, cols = w.shape
    tr = rows
    if rows * cols * 4 > ADAM_BLOCK_BYTES:
        tr = _pick(rows, [t for t in (256, 128, 64, 32, 16, 8) if t * cols * 4 <= ADAM_BLOCK_BYTES])

    def body(w_ref, g_ref, m_ref, v_ref, d_ref, mo_ref, vo_ref, *go_ref):
        gv = g_ref[...]
        if copy_grad:
            go_ref[0][...] = gv
        m2 = ADAM_B1 * m_ref[...] + (1.0 - ADAM_B1) * gv
        v2 = ADAM_B2 * v_ref[...] + (1.0 - ADAM_B2) * (gv * gv)
        m_hat = m2 / (1.0 - ADAM_B1 ** ADAM_STEP)
        v_hat = v2 / (1.0 - ADAM_B2 ** ADAM_STEP)
        d_ref[...] = -ADAM_LR * (m_hat / (jnp.sqrt(v_hat) + ADAM_EPS) + ADAM_WD * w_ref[...])
        mo_ref[...] = m2
        vo_ref[...] = v2

    spec = pl.BlockSpec((tr, cols), lambda i: (i, 0))
    out = jax.ShapeDtypeStruct((rows, cols), F32)
    n_out = 4 if copy_grad else 3
    return pl.pallas_call(
        body, name=name, out_shape=(out,) * n_out, grid=(rows // tr,),
        in_specs=[spec] * 4, out_specs=(spec,) * n_out,
        compiler_params=_params(1),
    )(w, g, m, v)


SC_TILES = 32
SC_LANES = 16
SC_COL_PARTS = 2
SC_CHUNK_ROWS = 8


def _adamw_sc(w, g, m, v, name):
    rows, cols = w.shape
    band, part = rows // (SC_TILES // SC_COL_PARTS), cols // SC_COL_PARTS
    assert band % SC_CHUNK_ROWS == 0 and part % SC_LANES == 0, (rows, cols)

    def body(w_hbm, g_hbm, m_hbm, v_hbm, d_hbm, mo_hbm, vo_hbm, go_hbm, wb, gb, mb, vb, db):
        tile = lax.axis_index("sc_subcore") * 2 + lax.axis_index("sc_core")
        row0 = (tile // SC_COL_PARTS) * band
        col0 = (tile % SC_COL_PARTS) * part

        @pl.loop(0, band, step=SC_CHUNK_ROWS)
        def _(r):
            at = lambda ref: ref.at[pl.ds(row0 + r, SC_CHUNK_ROWS), pl.ds(col0, part)]
            pltpu.sync_copy(at(w_hbm), wb)
            pltpu.sync_copy(at(g_hbm), gb)
            pltpu.sync_copy(gb, at(go_hbm))
            pltpu.sync_copy(at(m_hbm), mb)
            pltpu.sync_copy(at(v_hbm), vb)

            @pl.loop(0, SC_CHUNK_ROWS)
            def _(i):
                @pl.loop(0, part, step=SC_LANES)
                def _(j):
                    sl = (i, pl.ds(j, SC_LANES))
                    gv = gb[sl]
                    m2 = ADAM_B1 * mb[sl] + (1.0 - ADAM_B1) * gv
                    v2 = ADAM_B2 * vb[sl] + (1.0 - ADAM_B2) * (gv * gv)
                    m_hat = m2 / (1.0 - ADAM_B1 ** ADAM_STEP)
                    v_hat = v2 / (1.0 - ADAM_B2 ** ADAM_STEP)
                    db[sl] = -ADAM_LR * (m_hat / (jnp.sqrt(v_hat) + ADAM_EPS) + ADAM_WD * wb[sl])
                    mb[sl] = m2
                    vb[sl] = v2

            pltpu.sync_copy(db, at(d_hbm))
            pltpu.sync_copy(mb, at(mo_hbm))
            pltpu.sync_copy(vb, at(vo_hbm))

    out = jax.ShapeDtypeStruct((rows, cols), F32)
    buf = pltpu.VMEM((SC_CHUNK_ROWS, part), F32)
    return pl.kernel(
        body, name=name, out_type=(out, out, out, out),
        mesh=plsc.VectorSubcoreMesh(core_axis_name="sc_core", subcore_axis_name="sc_subcore"),
        scratch_types=[buf] * 5,
    )(w, g, m, v)


W_IN, W_OUT, W_UP, W_DOWN = range(4)
IN_CHUNKS = 2
W_INQ = 4


def _sequence_step(x, target, mod, norm1_w, lb_logits, hg_norm_w, q_norm_w, k_norm_w, norm2_w, conv_w, conv_b, net):
    shift1, scale1, gate1, shift2, scale2, gate2 = [mod[:, i * D:(i + 1) * D] for i in range(6)]

    def run(name, fn, *args, **kw):
        phase = net.host(name)
        if phase is None:
            return fn(*args, **kw)
        res, comm = fn(*args, phase=phase, **kw)
        net.done(name, comm, res[0] if isinstance(res, tuple) else res)
        return res

    h = _norm_mod(x, norm1_w, scale1, shift1, "norm1_fwd")
    proj = run("proj_fwd", _matmul, h, net.full[W_IN], "nn", F32, "proj_fwd")
    cat, o_raw, states = run("hgrn_fwd", _hgrn_fwd, proj, lb_logits, hg_norm_w)
    qn, kn = _qk_prep(proj, q_norm_w, k_norm_w)
    att = [run(f"attn_fwd_d{dil}", _attn_fwd, proj, qn, kn, dil) for dil in DILS]
    cat, b_out_f32, lse = _attn_merge([t[0] for t in att], [t[1] for t in att], cat)
    mix = run("mix_fwd", _matmul, cat, net.full[W_OUT], "nn", F32, "mix_fwd")
    x1, h2 = _resid_norm_mod(x, mix, gate1, norm2_w, scale2, shift2, "norm2_fwd")
    u = run("up_fwd", _matmul, h2, net.full[W_UP], "nn", BF16, "up_fwd")
    yact = run("conv_gate_fwd_a", _conv_gate_fwd, u, conv_w, conv_b, "conv_gate_fwd_a", tiles=(0, NCT // 2 + 1))
    yact = run("conv_gate_fwd_b", _conv_gate_fwd, u, conv_w, conv_b, "conv_gate_fwd_b", tiles=(NCT // 2 + 1, NCT),
               into=yact)
    z =_matmul(yact, net.full[W_DOWN], "nn", F32, "down_fwd")
    dout, dz, dgate2, loss_row = _out_loss(z, x1, target, gate2)

    net.grad[W_DOWN] = _matmul(yact, dz, "tn", BF16, "down_bwd_w")
    dyact = run("down_bwd_x", _matmul, dz, net.full[W_DOWN], "nt", BF16, "down_bwd_x")
    du, dconv_b, dconv_w = run("conv_gate_bwd", _conv_gate_bwd, dyact, u, conv_w, conv_b)
    net.grad[W_UP] = run("up_bwd_w", _matmul, h2, du, "tn", BF16, "up_bwd_w")
    dh2 = run("up_bwd_x", _matmul, du, net.full[W_UP], "nt", F32, "up_bwd_x")
    dx1, dshift2, dscale2, dnorm2, dgate1, dmix = run(
        "norm2_bwd", _norm_mod_bwd, dh2, x1, norm2_w, scale2, dout, "norm2_bwd", mix=mix, gate=gate1)
    net.grad[W_OUT] = run("mix_bwd_w", _matmul, cat, dmix, "tn", BF16, "mix_bwd_w")
    dcat = run("mix_bwd_x", _matmul, dmix, net.full[W_OUT], "nt", F32, "mix_bwd_x")
    dhq, dhf, dhi, dhg, dlb, dhg_norm = run("hgrn_bwd", _hgrn_bwd, proj, lb_logits, hg_norm_w, o_raw, states, dcat)
    delta = _attn_delta(dcat, b_out_f32)
    acc = None
    for dil in DILS:
        acc = run(f"attn_bwd_d{dil}", _attn_bwd, proj, qn, kn, lse, delta, dcat, dil, acc)
    daq, dak, dav, dq_norm, dk_norm = _qk_norm_bwd(proj, *acc, q_norm_w, k_norm_w)
    dproj = jnp.concatenate([dhq, dhf, dhi, dhg, daq, dak, dav], axis=1)
    net.carry = dproj
    for q in range(IN_CHUNKS):
        net.after_chunk(q, run(f"proj_bwd_w_{q}", _matmul, h, net.carry, "tn", BF16, f"proj_bwd_w_{q}",
                               m_blocks=(D // IN_CHUNKS, [q])))
    dh = run("proj_bwd_x", _matmul, net.carry, net.full[W_IN], "nt", F32, "proj_bwd_x")
    grad_x, dshift1, dscale1, dnorm1 = run("norm1_bwd", _norm_mod_bwd, dh, x, norm1_w, scale1, dx1, "norm1_bwd")

    dmod = jnp.concatenate([dshift1, dscale1, dgate1, dshift2, dscale2, dgate2], axis=1)
    small = dict(norm1_w=dnorm1, lb=dlb, hg_norm_w=dhg_norm, q_norm_w=dq_norm, k_norm_w=dk_norm,
                 norm2_w=dnorm2, conv_b=dconv_b, conv_w=dconv_w)
    return loss_row, grad_x, dmod, small


def _place():
    x, y, c = lax.axis_index("x"), lax.axis_index("y"), lax.axis_index("c")
    others = [(1 - x, y), (x, 1 - y), (1 - x, 1 - y)]
    return x, y, c, others


def _remote(src, dst, send_sems, recv_sems, k, to):
    return pltpu.make_async_remote_copy(src_ref=src, dst_ref=dst, send_sem=send_sems.at[k], recv_sem=recv_sems.at[k],
                                        device_id=to, device_id_type=MESH)


class _Phase:
    def __init__(self):
        self.ins, self.outs, self.aliases, self.groups, self.n = [], [], {}, [], 0

    def add(self, ins, outs, aliases, n, copies):
        i0, o0 = len(self.ins), len(self.outs)
        self.ins += list(ins)
        self.outs += list(outs)
        self.aliases.update({i0 + i: o0 + o for i, o in aliases.items()})
        self.groups.append((slice(i0, i0 + len(ins)), slice(o0, o0 + len(outs)), copies))
        self.n += n
        return slice(o0, o0 + len(outs))

    def build(self, cin, cout, send_sems, recv_sems, landing):
        res = []
        for si, so, copies in self.groups:
            for src, dst, land, peer in copies(cin[si], cout[so]):
                k = len(res)
                res.append(_remote(land, land, send_sems, recv_sems, k, peer) if landing
                           else _remote(src, dst, send_sems, recv_sems, k, peer))
        assert len(res) == self.n
        return res


def _pcall(body, args, phase=None, **kw):
    if phase is None or not phase.groups:
        res = pl.pallas_call(body, **kw)(*args)
        return res if phase is None else (res, ())
    grid = tuple(kw.pop("grid", ()))
    out_shape, out_specs = kw.pop("out_shape"), kw.pop("out_specs")
    single = not isinstance(out_shape, (tuple, list))
    out_shape = [out_shape] if single else list(out_shape)
    out_specs = [out_specs] if single else list(out_specs)
    scratch = list(kw.pop("scratch_shapes", ()))
    aliases = dict(kw.pop("input_output_aliases", {}))
    n_in, n_out, n_ci, n_co = len(args), len(out_shape), len(phase.ins), len(phase.outs)
    aliases.update({n_in + i: n_out + o for i, o in phase.aliases.items()})

    def hosted(*refs):
        ins, cin = refs[:n_in], refs[n_in:n_in + n_ci]
        outs = refs[n_in + n_ci:n_in + n_ci + n_out]
        cout = refs[n_in + n_ci + n_out:n_in + n_ci + n_out + n_co]
        scr, send_sems, recv_sems = refs[n_in + n_ci + n_out + n_co:-2], refs[-2], refs[-1]
        ids = [pl.program_id(a) for a in range(len(grid))]

        def start():
            for cp in phase.build(cin, cout, send_sems, recv_sems, False):
                cp.start()

        def finish():
            for cp in phase.build(cin, cout, send_sems, recv_sems, False):
                cp.wait_send()
            for cp in phase.build(cin, cout, send_sems, recv_sems, True):
                cp.wait_recv()

        if grid:
            first = functools.reduce(jnp.logical_and, [i == 0 for i in ids])
            last = functools.reduce(jnp.logical_and, [i == g - 1 for i, g in zip(ids, grid)])
            pl.when(first)(start)
            body(*ins, *outs, *scr)
            pl.when(last)(finish)
        else:
            start()
            body(*ins, *outs, *scr)
            finish()

    res = pl.pallas_call(
        hosted, out_shape=tuple(out_shape + phase.outs), grid=grid,
        in_specs=list(kw.pop("in_specs")) + [HBM_SPEC] * n_ci, out_specs=tuple(out_specs + [HBM_SPEC] * n_co),
        input_output_aliases=aliases,
        scratch_shapes=scratch + [pltpu.SemaphoreType.DMA((phase.n,)), pltpu.SemaphoreType.DMA((phase.n,))],
        **kw,
    )(*args, *phase.ins)
    outs = res[:n_out]
    return (outs[0] if single else tuple(outs)), tuple(res[n_out:])


def _comm_only(name, phase):
    return _pcall(lambda: None, [], phase, name=name, out_shape=(), in_specs=[], out_specs=())[1]


def _all_gather_rows(block, name, phase=None):
    m_per, n = block.shape

    def body(x_ref, out_ref, send_sems, recv_sems, local_sem):
        x, y, c, chips = _place()
        me, sibling = (x, y, c), (x, y, 1 - c)

        def rows(px, py, pc):
            return out_ref.at[pl.ds((4 * px + 2 * py + pc) * m_per, m_per), :]

        def copy(k, blk, to, src=None):
            return _remote(rows(*blk) if src is None else src, rows(*blk), send_sems, recv_sems, k, to)

        mine = pltpu.make_async_copy(x_ref, rows(*me), local_sem)
        mine.start()
        first = [copy(0, me, sibling, src=x_ref)]
        first += [copy(1 + j, me, (*chip, c), src=x_ref) for j, chip in enumerate(chips)]
        for cp in first:
            cp.start()
        passed = [copy(4 + j, (*chip, c), sibling) for j, chip in enumerate(chips)]
        for j, chip in enumerate(chips):
            copy(1 + j, (*chip, c), me).wait_recv()
            passed[j].start()
        copy(0, sibling, me).wait_recv()
        for j, chip in enumerate(chips):
            copy(4 + j, (*chip, 1 - c), me).wait_recv()
        for cp in first + passed:
            cp.wait_send()
        mine.wait()

    return _pcall(
        body, [block], phase, name=name, out_shape=jax.ShapeDtypeStruct((N_DEV * m_per, n), block.dtype),
        in_specs=[VMEM_SPEC], out_specs=VMEM_SPEC,
        scratch_shapes=[pltpu.SemaphoreType.DMA((7,)), pltpu.SemaphoreType.DMA((7,)), pltpu.SemaphoreType.DMA],
    )


class _Geo:
    def __init__(self, kind, shard_shape):
        self.kind = kind
        self.shard_shape = tuple(shard_shape)
        self.hr, self.hc = shard_shape[0] // 2, shard_shape[1]
        if kind == "col":
            self.full_shape = (shard_shape[0], N_CHIPS * shard_shape[1])
        else:
            self.full_shape = (N_CHIPS * shard_shape[0], shard_shape[1])

    def full_half(self, ref, j, c):
        return self.piece(ref, j, c, 0, 1, 1)

    def piece(self, ref, j, c, p0, p1, pieces, part=None):
        pr = self.hr // pieces
        off, n = p0 * pr, (p1 - p0) * pr
        if part is not None:
            top = (n // 32) * 16
            off, n = (off, top) if part == 0 else (off + top, n - top)
        if self.kind == "col":
            return ref.at[pl.ds(pl.multiple_of(c * self.hr + off, 16), n),
                          pl.ds(pl.multiple_of(j * self.hc, 128), self.hc)]
        return ref.at[pl.ds(pl.multiple_of((2 * j + c) * self.hr + off, 16), n), :]


WEIGHT_KINDS = ("col", "row", "col", "row")
NW = len(WEIGHT_KINDS)


def _comm_call(body, name, ins, outs, n_remote, aliases=None):
    return pl.pallas_call(
        body, name=name, out_shape=tuple(outs),
        in_specs=[HBM_SPEC] * len(ins), out_specs=tuple([HBM_SPEC] * len(outs)),
        input_output_aliases=aliases or {},
        scratch_shapes=[pltpu.SemaphoreType.DMA((n_remote,)), pltpu.SemaphoreType.DMA((n_remote,))],
    )(*ins)


ROW_TILES = (256, 176, 128, 64, 32, 16)


def _cast_into_full(shard, geo, place, name):
    rs, cs = shard.shape
    tr = _pick(rs, ROW_TILES)
    nb = rs // tr

    def body(place_ref, s_ref, o_ref):
        o_ref[...] = s_ref[...].astype(BF16)

    if geo.kind == "col":
        out_map = lambda i, place_ref: (i, place_ref[0])
    else:
        out_map = lambda i, place_ref: (place_ref[0] * nb + i, 0)
    return pl.pallas_call(
        body, name=name, out_shape=jax.ShapeDtypeStruct(geo.full_shape, BF16),
        grid_spec=pltpu.PrefetchScalarGridSpec(
            num_scalar_prefetch=1, grid=(nb,),
            in_specs=[pl.BlockSpec((tr, cs), lambda i, place_ref: (i, 0))],
            out_specs=pl.BlockSpec((tr, cs), out_map)),
        compiler_params=_params(1),
    )(place, shard)


def _gather_weight(full, geo, pieces, name):
    per = 8

    def body(in_ref, ref, send_sems, recv_sems):
        x, y, c, _ = _place()
        j, jx, jy, jo = 2 * x + y, 2 * (1 - x) + y, 2 * x + (1 - y), 2 * (1 - x) + (1 - y)
        nx, ny, sib = (1 - x, y, c), (x, 1 - y, c), (x, y, 1 - c)

        def cp(region, k, to):
            return _remote(region, region, send_sems, recv_sems, k, to)

        def reg(chip, p, part=None, core=c):
            return geo.piece(ref, chip, core, p, p + 1, pieces, part)

        sent = []
        for p in range(pieces):
            sent += [cp(reg(j, p), per * p, nx), cp(reg(j, p), per * p + 1, ny)]
        for d in sent:
            d.start()
        for p in range(pieces):
            cp(reg(jx, p), per * p, nx).wait_recv()
            new = [cp(reg(jx, p, 0), per * p + 2, ny), cp(reg(jx, p), per * p + 4, sib)]
            cp(reg(jy, p), per * p + 1, ny).wait_recv()
            new += [cp(reg(jy, p, 1), per * p + 3, nx), cp(reg(jy, p), per * p + 5, sib)]
            for d in new:
                d.start()
            sent += new
        for p in range(pieces):
            cp(reg(jo, p, 0), per * p + 2, ny).wait_recv()
            cp(reg(jo, p, 1), per * p + 3, nx).wait_recv()
            new = [cp(reg(jo, p, 0), per * p + 6, sib), cp(reg(jo, p, 1), per * p + 7, sib)]
            for d in new:
                d.start()
            sent += new
        for p in range(pieces):
            cp(reg(jx, p, None, 1 - c), per * p + 4, sib).wait_recv()
            cp(reg(jy, p, None, 1 - c), per * p + 5, sib).wait_recv()
            cp(reg(jo, p, 0, 1 - c), per * p + 6, sib).wait_recv()
            cp(reg(jo, p, 1, 1 - c), per * p + 7, sib).wait_recv()
        for d in sent:
            d.wait_send()

    return _comm_call(body, name, [full], [_same(full)], per * pieces, aliases={0: 0})[0]


def _same(a):
    return jax.ShapeDtypeStruct(a.shape, a.dtype)


def _gather_ici(full, geo, p0, p1, pieces):
    def copies(ins, outs):
        x, y, c, _ = _place()
        mine = geo.piece(outs[0], 2 * x + y, c, p0, p1, pieces)
        return [(mine, mine, geo.piece(outs[0], 2 * ox + oy, c, p0, p1, pieces), (ox, oy, c))
                for ox, oy in ((1 - x, y), (x, 1 - y))]

    return dict(ins=[full], outs=[_same(full)], aliases={0: 0}, n=2, copies=copies)


def _gather_relay(full, geo, p0, p1, pieces):
    def copies(ins, outs):
        x, y, c, _ = _place()
        jx, jy, jo = 2 * (1 - x) + y, 2 * x + (1 - y), 2 * (1 - x) + (1 - y)
        reg = lambda chip, part: geo.piece(outs[0], chip, c, p0, p1, pieces, part)
        return [(reg(jx, 0), reg(jx, 0), reg(jo, 0), (x, 1 - y, c)), (reg(jy, 1), reg(jy, 1), reg(jo, 1), (1 - x, y, c))]

    return dict(ins=[full], outs=[_same(full)], aliases={0: 0}, n=2, copies=copies)


def _gather_d2d(full, geo):
    def copies(ins, outs):
        x, y, c, chips = _place()
        return [(geo.full_half(outs[0], 2 * ox + oy, c), geo.full_half(outs[0], 2 * ox + oy, c),
                 geo.full_half(outs[0], 2 * ox + oy, 1 - c), (x, y, 1 - c)) for ox, oy in chips]

    return dict(ins=[full], outs=[_same(full)], aliases={0: 0}, n=3, copies=copies)


def _swap_d2d(grad, geo):
    def copies(ins, outs):
        x, y, c, _ = _place()
        return [(geo.full_half(ins[0], j, 1 - c), outs[0].at[j], outs[0].at[j], (x, y, 1 - c)) for j in range(N_CHIPS)]

    return dict(ins=[grad], outs=[jax.ShapeDtypeStruct((N_CHIPS, geo.hr, geo.hc), BF16)], aliases={}, n=N_CHIPS,
                copies=copies)


SEM_SPEC = pl.BlockSpec(memory_space=pltpu.SEMAPHORE)
SIDE_EFFECT = pltpu.SideEffectType.DATAFLOW_SIDE_EFFECTING


def _fly_copies(swap, src_ref, land_ref, geo, sems):
    x, y, c, chips = _place()
    if swap:
        ends = [(geo.full_half(src_ref, j, 1 - c), land_ref.at[j], (x, y, 1 - c)) for j in range(N_CHIPS)]
    else:
        ends = [(src_ref.at[2 * ox + oy], land_ref.at[k], (ox, oy, c)) for k, (ox, oy) in enumerate(chips)]
    n = len(ends)
    return [pltpu.make_async_remote_copy(src_ref=src, dst_ref=dst, send_sem=sems[k], recv_sem=sems[n + k],
                                         device_id=peer, device_id_type=MESH) for k, (src, dst, peer) in enumerate(ends)]


def _fly_start(swap, src, geo, name, carry=None):
    n = N_CHIPS if swap else 3
    n_carry = 0 if carry is None else 1

    def body(src_ref, land_ref, *rest):
        sems, token = rest[n_carry:n_carry + 2 * n], rest[n_carry + 2 * n + 2]
        for copy in _fly_copies(swap, src_ref, land_ref, geo, sems):
            copy.start()
        token[...] = jnp.zeros_like(token)

    land = jax.ShapeDtypeStruct((n, geo.hr, geo.hc), BF16)
    out_shape = (pltpu.SemaphoreType.DMA(()),) * (2 * n) + (pltpu.HBM(src.shape, src.dtype), pltpu.HBM(land.shape, land.dtype),
                                                             jax.ShapeDtypeStruct((8, 128), F32))
    out_specs = (SEM_SPEC,) * (2 * n) + (HBM_SPEC, HBM_SPEC, VMEM_SPEC)
    args = [pltpu.with_memory_space_constraint(src, pltpu.HBM),
            pltpu.with_memory_space_constraint(lax.empty(land.shape, land.dtype), pltpu.HBM)]
    aliases = {0: 2 * n, 1: 2 * n + 1}
    if carry is not None:
        out_shape += (pltpu.HBM(carry.shape, carry.dtype),)
        out_specs += (HBM_SPEC,)
        args.append(pltpu.with_memory_space_constraint(carry, pltpu.HBM))
        aliases[2] = 2 * n + 3
    return pl.pallas_call(
        body, name=name, out_shape=out_shape, in_specs=(HBM_SPEC,) * len(args), out_specs=out_specs,
        input_output_aliases=aliases,
        compiler_params=pltpu.CompilerParams(has_side_effects=SIDE_EFFECT),
    )(*args)


def _fly_wait(swap, started, geo, after, name):
    n = N_CHIPS if swap else 3
    sems, src_thru, land_thru = started[:2 * n], started[2 * n], started[2 * n + 1]

    def body(src_ref, land_ref, *rest):
        for copy in _fly_copies(swap, src_ref, land_ref, geo, rest[:2 * n]):
            copy.wait_send()
            copy.wait_recv()

    return pl.pallas_call(
        body, name=name,
        out_shape=(pltpu.HBM(src_thru.shape, src_thru.dtype), pltpu.HBM(land_thru.shape, land_thru.dtype)),
        in_specs=(HBM_SPEC, HBM_SPEC) + (SEM_SPEC,) * (2 * n) + (pl.BlockSpec(memory_space=pl.ANY),),
        out_specs=(HBM_SPEC, HBM_SPEC), input_output_aliases={0: 0, 1: 1},
        compiler_params=pltpu.CompilerParams(has_side_effects=SIDE_EFFECT),
    )(src_thru, land_thru, *sems, after)


def _scatter_ici(pair, recv, geo, p0, p1, pieces):
    pr = geo.hr // pieces
    rows = pl.ds(p0 * pr, (p1 - p0) * pr)

    def copies(ins, outs):
        x, y, c, chips = _place()
        return [(ins[0].at[2 * ox + oy, rows, :], outs[0].at[k, rows, :], outs[0].at[k, rows, :], (ox, oy, c))
                for k, (ox, oy) in enumerate(chips)]

    out = jax.ShapeDtypeStruct((3, geo.hr, geo.hc), BF16)
    if recv is None:
        return dict(ins=[pair], outs=[out], aliases={}, n=3, copies=copies)
    return dict(ins=[pair, recv], outs=[out], aliases={1: 0}, n=3, copies=copies)


def _join_d2d(shard, geo, row0=0):
    def copies(ins, outs):
        x, y, c, _ = _place()
        mine = outs[0].at[pl.ds(pl.multiple_of(row0 + c * geo.hr, 8), geo.hr), :]
        other = outs[0].at[pl.ds(pl.multiple_of(row0 + (1 - c) * geo.hr, 8), geo.hr), :]
        return [(mine, mine, other, (x, y, 1 - c))]

    return dict(ins=[shard], outs=[_same(shard)], aliases={0: 0}, n=1, copies=copies)


def _add_pair(grad, got, geo, place, name):
    tr = _pick(geo.hr, ROW_TILES)
    nb = geo.hr // tr

    def body(place_ref, a_ref, b_ref, o_ref):
        o_ref[0] = (a_ref[...].astype(F32) + b_ref[0].astype(F32)).astype(BF16)

    if geo.kind == "col":
        own_map = lambda j, i, place_ref: (place_ref[1] * nb + i, j)
    else:
        own_map = lambda j, i, place_ref: ((2 * j + place_ref[1]) * nb + i, 0)
    spec = pl.BlockSpec((1, tr, geo.hc), lambda j, i, place_ref: (j, i, 0))
    return pl.pallas_call(
        body, name=name, out_shape=jax.ShapeDtypeStruct((N_CHIPS, geo.hr, geo.hc), BF16),
        grid_spec=pltpu.PrefetchScalarGridSpec(
            num_scalar_prefetch=1, grid=(N_CHIPS, nb),
            in_specs=[pl.BlockSpec((tr, geo.hc), own_map), spec], out_specs=spec),
        compiler_params=_params(2),
    )(place, grad, got)


def _add_four(pair, recv, geo, place, name, rows=None, row0=0, into=None):
    tr = _pick(geo.hr, ROW_TILES)
    nb = geo.hr // tr
    rows = 2 * geo.hr if rows is None else rows

    def body(place_ref, p_ref, r_ref, *rest):
        rest[-1][...] = ((p_ref[0].astype(F32) + r_ref[0].astype(F32)) + r_ref[1].astype(F32)) + r_ref[2].astype(F32)

    in_specs = [pl.BlockSpec((1, tr, geo.hc), lambda i, place_ref: (place_ref[0], i, 0)),
                pl.BlockSpec((3, tr, geo.hc), lambda i, place_ref: (0, i, 0))]
    args = [place, pair, recv]
    if into is not None:
        in_specs.append(pl.BlockSpec(memory_space=pl.ANY))
        args.append(into)
    return pl.pallas_call(
        body, name=name, out_shape=jax.ShapeDtypeStruct((rows, geo.hc), F32),
        grid_spec=pltpu.PrefetchScalarGridSpec(
            num_scalar_prefetch=1, grid=(nb,), in_specs=in_specs,
            out_specs=pl.BlockSpec((tr, geo.hc), lambda i, place_ref: (row0 // tr + place_ref[1] * nb + i, 0))),
        input_output_aliases={3: 0} if into is not None else {},
        compiler_params=_params(1),
    )(*args)


PIECES = (4, 1, 16, 8) + (1,) * IN_CHUNKS
SCHEDULE = {
    "proj_fwd": (("gather_ici", W_OUT, 0, 1), ("gather_ici", W_UP, 0, 6)),
    "hgrn_fwd": (("gather_relay", W_OUT, 0, 1), ("gather_relay", W_UP, 0, 6), ("gather_ici", W_UP, 6, 12)),
    "attn_fwd_d1": (("gather_relay", W_UP, 6, 12),),
    "attn_fwd_d4": (("gather_ici", W_UP, 12, 16), ("gather_d2d", W_OUT)),
    "attn_fwd_d16": (("gather_relay", W_UP, 12, 16), ("gather_ici", W_DOWN, 0, 2)),
    "mix_fwd": (("gather_d2d", W_UP), ("gather_ici", W_DOWN, 2, 4)),
    "up_fwd": (("gather_ici", W_DOWN, 4, 8), ("gather_relay", W_DOWN, 0, 4)),
    "conv_gate_fwd_a": (("gather_relay", W_DOWN, 4, 8),),
    "conv_gate_fwd_b": (("gather_d2d", W_DOWN),),
    "down_bwd_x": (("swap", W_DOWN),),
    "conv_gate_bwd": (("scatter", W_DOWN, 0, 4),),
    "up_bwd_w": (("scatter", W_DOWN, 4, 8),),
    "up_bwd_x": (("swap", W_UP),),
    "norm2_bwd": (("scatter", W_UP, 0, 1),),
    "mix_bwd_w": (("scatter", W_UP, 1, 2),),
    "mix_bwd_x": (("swap", W_OUT), ("scatter", W_UP, 2, 3)),
    "hgrn_bwd": (("scatter", W_UP, 3, 9), ("join", W_DOWN)),
    "attn_bwd_d1": (("scatter", W_UP, 9, 12),),
    "attn_bwd_d4": (("scatter", W_OUT, 0, 1),),
    "attn_bwd_d16": (("scatter", W_UP, 12, 16),),
    "proj_bwd_w_0": (("join", W_UP), ("join", W_OUT)),
    "gather_stats": (("join", W_INQ),),
    "grad_in_join": (("join", W_INQ + 1),),
}


class _Net:
    def __init__(self, shards, place):
        self.place = place
        self.geos = [_Geo(k, s.shape) for k, s in zip(WEIGHT_KINDS, shards)]
        self.full = [_cast_into_full(s, g, place, f"cast_shard_{w}") for w, (s, g) in enumerate(zip(shards, self.geos))]
        self.full[W_IN] = _gather_weight(self.full[W_IN], self.geos[W_IN], PIECES[W_IN], "gather_w_in")
        rows, cols = shards[W_IN].shape
        self.geos += [_Geo("col", (rows // IN_CHUNKS, cols))] * IN_CHUNKS
        n = NW + IN_CHUNKS
        self.grad, self.pair, self.recv, self.shard = [None] * n, [None] * n, [None] * n, [None] * n
        self.in_rows, self.in_shard = rows, None
        self.when_joined, self.joined = {}, {}
        self.flying = None
        self.carry = None
        self.swaps = [None] * IN_CHUNKS
        self.slots = []

    def _row0(self, w):
        return (w - W_INQ) * 2 * self.geos[w].hr

    def host(self, name):
        phase = _Phase()
        self.slots = []
        groups = {}
        for item in SCHEDULE.get(name, ()):
            kind, w = item[0], item[1]
            geo = self.geos[w]
            key = len(groups)
            if kind == "gather_ici":
                spec, key = _gather_ici(self.full[w], geo, item[2], item[3], PIECES[w]), ("full", w)
            elif kind == "gather_relay":
                spec, key = _gather_relay(self.full[w], geo, item[2], item[3], PIECES[w]), ("full", w)
            elif kind == "gather_d2d":
                spec, key = _gather_d2d(self.full[w], geo), ("full", w)
            elif kind == "swap":
                spec = _swap_d2d(self.grad[w], geo)
            elif kind == "scatter":
                spec, key = _scatter_ici(self.pair[w], self.recv[w], geo, item[2], item[3], PIECES[w]), ("recv", w)
            elif w >= W_INQ:
                spec, key = _join_d2d(self.in_shard, geo, self._row0(w)), "in_shard"
            else:
                spec = _join_d2d(self.shard[w], geo)
            groups.setdefault(key, []).append((item, spec))
        for group in groups.values():
            specs = [s for _, s in group]
            merged = dict(specs[0], n=sum(s["n"] for s in specs),
                          copies=lambda ins, outs, specs=specs: [t for s in specs for t in s["copies"](ins, outs)])
            self.slots.append(([item for item, _ in group], phase.add(**merged)))
        return phase

    def done(self, name, comm, result=None):
        for items, sl in sorted(self.slots, key=lambda s: s[0][0][0] == "scatter"):
            out = comm[sl][0]
            for item in items:
                kind, w = item[0], item[1]
                if kind in ("gather_ici", "gather_relay", "gather_d2d"):
                    self.full[w] = out
                elif kind == "swap":
                    self.pair[w] = _add_pair(self.grad[w], out, self.geos[w], self.place, f"grad_pair_sum_{w}")
                elif kind == "scatter":
                    self.recv[w] = out
                    if item[3] == PIECES[w] and w >= W_INQ:
                        self.in_shard = _add_four(self.pair[w], out, self.geos[w], self.place, f"grad_chip_sum_{w}",
                                                  rows=self.in_rows, row0=self._row0(w), into=self.in_shard)
                    elif item[3] == PIECES[w]:
                        self.shard[w] = _add_four(self.pair[w], out, self.geos[w], self.place, f"grad_chip_sum_{w}")
                elif w >= W_INQ:
                    self.in_shard = out
                else:
                    self.shard[w] = out
                    if w in self.when_joined:
                        self.joined[w] = self.when_joined[w](out)
        if name == "proj_bwd_x" and result is not None:
            self._land_swap(IN_CHUNKS - 1, result)
            pair, recv = _fly_wait(False, self.flying, self.geos[W_INQ], result, "grad_in_scatter0_wait")
            self.in_shard = _add_four(pair, recv, self.geos[W_INQ], self.place, f"grad_chip_sum_{W_INQ}",
                                      rows=self.in_rows, row0=0, into=self.in_shard)

    def _land_swap(self, q, after):
        w = W_INQ + q
        grad, got = _fly_wait(True, self.swaps[q], self.geos[w], after, f"grad_in_swap{q}_wait")
        self.pair[w] = _add_pair(grad, got, self.geos[w], self.place, f"grad_pair_sum_{w}")

    def after_chunk(self, q, grad):
        w = W_INQ + q
        self.grad[w] = grad
        if q > 0:
            self._land_swap(q - 1, grad)
        if q == 1:
            self.flying = _fly_start(False, self.pair[W_INQ], self.geos[W_INQ], "grad_in_scatter0_start", self.carry)
            self.carry = self.flying[-1]
        self.swaps[q] = _fly_start(True, grad, self.geos[w], f"grad_in_swap{q}_start", self.carry)
        self.carry = self.swaps[q][-1]

    def alone(self, name):
        self.done(name, _comm_only(name, self.host(name)))


CONVW_SHARD = 3 * DFF // N_CHIPS
COND_PAD = 8 * 896
SMALL_SIZES = (("norm1_w", D), ("lb", HW), ("hg_norm_w", DH), ("q_norm_w", DH), ("k_norm_w", DH),
               ("norm2_w", D), ("conv_b", DFF), ("conv_w", 3 * DFF), ("loss", 128))
SMALL_TOTAL = sum(n for _, n in SMALL_SIZES)
STATS_PAD = 8 * 5120


def kernel(x, c, w_ada, b_ada, norm1_w, w_in, lb_logits, hg_norm_w, q_norm_w, k_norm_w, w_out, norm2_w, w_up, conv_w, conv_b, w_down, loss_target, m_w_ada, m_b_ada, m_norm1_w, m_w_in, m_lb_logits, m_hg_norm_w, m_q_norm_w, m_k_norm_w, m_w_out, m_norm2_w, m_w_up, m_conv_w, m_conv_b, m_w_down, v_w_ada, v_b_ada, v_norm1_w, v_w_in, v_lb_logits, v_hg_norm_w, v_q_norm_w, v_k_norm_w, v_w_out, v_norm2_w, v_w_up, v_conv_w, v_conv_b, v_w_down):
    chip = 2 * lax.axis_index("x") + lax.axis_index("y")
    dev = 2 * chip + lax.axis_index("c")

    cond = jnp.concatenate([c, conv_w[0].reshape(1, CONVW_SHARD), jnp.zeros((1, COND_PAD - D - CONVW_SHARD), F32)], axis=1)
    cond_all = _all_gather_rows(cond.reshape(8, COND_PAD // 8), "gather_cond").reshape(N_DEV, COND_PAD)
    c_all = cond_all[:, :D]
    conv_w_full = jnp.concatenate(
        [cond_all[2 * j, D:D + CONVW_SHARD].reshape(3, DFF // N_CHIPS) for j in range(N_CHIPS)], axis=1)

    b_shard = lax.dynamic_slice_in_dim(b_ada, chip * ADA_COLS, ADA_COLS, axis=1)
    mod_cols = _all_gather_rows(_ada_fwd(c_all, w_ada[0], b_shard), "gather_mod")
    mod_all = jnp.concatenate([mod_cols[16 * j:16 * j + 8] for j in range(N_CHIPS)], axis=1)
    mod = lax.dynamic_slice_in_dim(mod_all, dev, 1, axis=0)

    place = jnp.stack([chip, lax.axis_index("c")]).astype(jnp.int32)
    net = _Net([w_in[0], w_out[0], w_up[0], w_down[0]], place)
    net.when_joined = {
        W_OUT: lambda grad: _adamw_sc(w_out[0], grad, m_w_out[0], v_w_out[0], "adamw_sc_w_out"),
        W_DOWN: lambda grad: _adamw_sc(w_down[0], grad, m_w_down[0], v_w_down[0], "adamw_sc_w_down"),
        W_UP: lambda grad: _adamw_sc(w_up[0], grad, m_w_up[0], v_w_up[0], "adamw_sc_w_up"),
    }
    early = {W_OUT: "w_out", W_DOWN: "w_down", W_UP: "w_up"}
    loss_row, grad_x, dmod, small = _sequence_step(
        x[0], loss_target[0], mod, norm1_w, lb_logits, hg_norm_w, q_norm_w, k_norm_w, norm2_w, conv_w_full, conv_b, net)
    small["conv_w"] = small["conv_w"].reshape(1, 3 * DFF)
    small["loss"] = loss_row
    stats = jnp.concatenate([dmod] + [small[k] for k, _ in SMALL_SIZES]
                            + [jnp.zeros((1, STATS_PAD - 6 * D - SMALL_TOTAL), F32)], axis=1)
    stats_all, comm = _all_gather_rows(stats.reshape(8, STATS_PAD // 8), "gather_stats", net.host("gather_stats"))
    net.done("gather_stats", comm)
    stats_all = stats_all.reshape(N_DEV, STATS_PAD)
    last = W_INQ + IN_CHUNKS - 1
    started = _fly_start(False, net.pair[last], net.geos[last], "grad_in_scatter_start")
    dmod_all = stats_all[:, :6 * D] + started[8][0, 0]
    sums = _sum_rows(stats_all[:, 6 * D:6 * D + SMALL_TOTAL], "small_grad_sum")
    g_small, off = {}, 0
    for k, n in SMALL_SIZES:
        g_small[k] = sums[:, off:off + n]
        off += n
    loss = g_small["loss"][0, 0]
    g_b_ada = _sum_rows(dmod_all, "b_ada_grad_sum")
    ada = _ada_bwd_adamw(c_all, lax.dynamic_slice_in_dim(dmod_all, chip * ADA_COLS, ADA_COLS, axis=1),
                         w_ada[0], m_w_ada[0], v_w_ada[0])
    g_w_ada = ada[0]
    pair_last, recv_last = _fly_wait(False, started, net.geos[last], ada[3], "grad_in_scatter_wait")
    net.in_shard = _add_four(pair_last, recv_last, net.geos[last], place, f"grad_chip_sum_{last}",
                             rows=net.in_rows, row0=net._row0(last), into=net.in_shard)
    net.alone("grad_in_join")
    g_out, g_up, g_down = net.shard[W_OUT], net.shard[W_UP], net.shard[W_DOWN]
    g_in = net.in_shard
    g_lb = _lb_grad(lb_logits, g_small["lb"])
    g_conv_w = lax.dynamic_slice_in_dim(g_small["conv_w"].reshape(3, DFF), chip * (DFF // N_CHIPS), DFF // N_CHIPS, axis=1)

    names = ["w_ada", "b_ada", "norm1_w", "w_in", "lb_logits", "hg_norm_w", "q_norm_w", "k_norm_w", "w_out",
             "norm2_w", "w_up", "conv_w", "conv_b", "w_down"]
    lead = {"w_ada", "w_in", "w_out", "w_up", "conv_w", "w_down"}
    w = dict(w_ada=w_ada, b_ada=b_ada, norm1_w=norm1_w, w_in=w_in, lb_logits=lb_logits, hg_norm_w=hg_norm_w,
             q_norm_w=q_norm_w, k_norm_w=k_norm_w, w_out=w_out, norm2_w=norm2_w, w_up=w_up, conv_w=conv_w,
             conv_b=conv_b, w_down=w_down)
    m = dict(w_ada=m_w_ada, b_ada=m_b_ada, norm1_w=m_norm1_w, w_in=m_w_in, lb_logits=m_lb_logits,
             hg_norm_w=m_hg_norm_w, q_norm_w=m_q_norm_w, k_norm_w=m_k_norm_w, w_out=m_w_out, norm2_w=m_norm2_w,
             w_up=m_w_up, conv_w=m_conv_w, conv_b=m_conv_b, w_down=m_w_down)
    v = dict(w_ada=v_w_ada, b_ada=v_b_ada, norm1_w=v_norm1_w, w_in=v_w_in, lb_logits=v_lb_logits,
             hg_norm_w=v_hg_norm_w, q_norm_w=v_q_norm_w, k_norm_w=v_k_norm_w, w_out=v_w_out, norm2_w=v_norm2_w,
             w_up=v_w_up, conv_w=v_conv_w, conv_b=v_conv_b, w_down=v_w_down)
    g = dict(w_ada=g_w_ada, b_ada=g_b_ada, norm1_w=g_small["norm1_w"], w_in=g_in, lb_logits=g_lb,
             hg_norm_w=g_small["hg_norm_w"], q_norm_w=g_small["q_norm_w"], k_norm_w=g_small["k_norm_w"], w_out=g_out,
             norm2_w=g_small["norm2_w"], w_up=g_up, conv_w=g_conv_w, conv_b=g_small["conv_b"], w_down=g_down)

    updated = {early[i]: res for i, res in net.joined.items()}
    updated["w_ada"] = (ada[1], ada[2], ada[3], ada[0])
    grads, deltas, new_m, new_v = [], [], [], []
    for n in names:
        strip = (lambda t: t[0]) if n in lead else (lambda t: t)
        wrap = (lambda t: t[None]) if n in lead else (lambda t: t)
        g_n = g[n]
        if n in updated:
            d_n, m_n, v_n, g_n = updated[n]
        elif n == "w_in":
            d_n, m_n, v_n, g_n = _adamw(strip(w[n]), g_n, strip(m[n]), strip(v[n]), f"adamw_{n}", copy_grad=True)
        else:
            d_n, m_n, v_n = _adamw(strip(w[n]), g_n, strip(m[n]), strip(v[n]), f"adamw_{n}")
        grads.append(wrap(g_n))
        deltas.append(wrap(d_n))
        new_m.append(wrap(m_n))
        new_v.append(wrap(v_n))
    return (loss, grad_x[None], *grads, *deltas, *new_m, *new_v)
```

```python
import functools

import jax
import jax.numpy as jnp
from jax import lax
from jax.experimental import pallas as pl
from jax.experimental.pallas import tpu as pltpu
from jax.experimental.pallas import tpu_sc as plsc

F32 = jnp.float32
BF16 = jnp.bfloat16

S = 2048
D = 2048
H = 8
DH = 128
HW = H * DH
CH = 64
NCH = S // CH
DFF = 5632
INC = 7 * HW
BLK = 128
DILS = (1, 4, 16)
EPS = 1e-6
NEG = -1e30
N_CHIPS = 4
N_DEV = 8

ADAM_LR = 0.001
ADAM_B1 = 0.9
ADAM_B2 = 0.999
ADAM_EPS = 1e-08
ADAM_WD = 0.01
ADAM_STEP = 10
ADAM_BLOCK_BYTES = 1 << 21

V7X_VMEM_BYTES = 64 * 1024 * 1024
VMEM_LIMIT = (V7X_VMEM_BYTES * 3) // 4
MESH = pl.DeviceIdType.MESH
HBM_SPEC = pl.BlockSpec(memory_space=pltpu.HBM)
VMEM_SPEC = pl.BlockSpec(memory_space=pltpu.VMEM)

NN = (((1,), (0,)), ((), ()))
NT = (((1,), (1,)), ((), ()))
TN = (((0,), (0,)), ((), ()))


def _params(n_grid):
    return pltpu.CompilerParams(dimension_semantics=("arbitrary",) * n_grid, vmem_limit_bytes=VMEM_LIMIT)


def _dot(a, b, dims=NN):
    return lax.dot_general(a.astype(BF16), b.astype(BF16), dims, preferred_element_type=F32)


def _dot_f32(a, b):
    return lax.dot_general(a, b, NN, precision=lax.Precision.HIGHEST, preferred_element_type=F32)


def _sigmoid(x):
    return 1.0 / (1.0 + jnp.exp(-x))


def _pick(n, cands):
    for t in cands:
        if n % t == 0:
            return t
    raise ValueError(n)


def _matmul(a, b, mode, out_dtype, name, phase=None, m_blocks=None):
    if mode == "nn":
        (m, k), (_, n) = a.shape, b.shape
    elif mode == "nt":
        (m, k), (n, _) = a.shape, b.shape
    else:
        (k, m), (_, n) = a.shape, b.shape
    tm = _pick(m, (1024, 1408, 512))
    a_block = lambda i: i
    if m_blocks is not None:
        tm, blocks = m_blocks
        m = tm * len(blocks)
        step = blocks[1] - blocks[0] if len(blocks) > 1 else 0
        assert all(blk == blocks[0] + step * i for i, blk in enumerate(blocks))
        a_block = lambda i: blocks[0] + step * i
    tn = _pick(n, (1024, 1408, 512))
    tk = _pick(k, (2048, 2816, 1792, 1024, 512))
    nk = k // tk
    dims = {"nn": NN, "nt": NT, "tn": TN}[mode]

    def body(a_ref, b_ref, o_ref, *acc):
        part = lax.dot_general(a_ref[...], b_ref[...], dims, preferred_element_type=F32)
        if nk == 1:
            o_ref[...] = part.astype(o_ref.dtype)
            return
        acc_ref, kk = acc[0], pl.program_id(2)

        @pl.when(kk == 0)
        def _():
            acc_ref[...] = part

        @pl.when(jnp.logical_and(kk > 0, kk < nk - 1))
        def _():
            acc_ref[...] += part

        @pl.when(kk == nk - 1)
        def _():
            o_ref[...] = (acc_ref[...] + part).astype(o_ref.dtype)

    if mode == "tn":
        a_spec = pl.BlockSpec((tk, tm), lambda i, j, kk: (kk, a_block(i)))
    else:
        a_spec = pl.BlockSpec((tm, tk), lambda i, j, kk: (i, kk))
    if mode == "nt":
        b_spec = pl.BlockSpec((tn, tk), lambda i, j, kk: (j, kk))
    else:
        b_spec = pl.BlockSpec((tk, tn), lambda i, j, kk: (kk, j))
    return _pcall(
        body, [a, b], phase, name=name,
        out_shape=jax.ShapeDtypeStruct((m, n), out_dtype),
        grid=(m // tm, n // tn, nk),
        in_specs=[a_spec, b_spec],
        out_specs=pl.BlockSpec((tm, tn), lambda i, j, kk: (i, j)),
        scratch_shapes=[pltpu.VMEM((tm, tn), F32)] if nk > 1 else [],
        compiler_params=_params(3),
    )


TR = 256


def _row_spec(cols=D):
    return pl.BlockSpec((TR, cols), lambda i: (i, 0))


def _vec_spec(cols=D):
    return pl.BlockSpec((1, cols), lambda i: (0, 0))


def _norm_mod(x, nw, scale, shift, name):
    def body(x_ref, nw_ref, sc_ref, sh_ref, h_ref):
        xv = x_ref[...]
        r = lax.rsqrt(jnp.mean(xv * xv, axis=-1, keepdims=True) + EPS)
        h_ref[...] = ((xv * r) * nw_ref[...] * (1.0 + sc_ref[...]) + sh_ref[...]).astype(BF16)

    return pl.pallas_call(
        body, name=name, out_shape=jax.ShapeDtypeStruct((S, D), BF16), grid=(S // TR,),
        in_specs=[_row_spec(), _vec_spec(), _vec_spec(), _vec_spec()], out_specs=_row_spec(),
        compiler_params=_params(1),
    )(x, nw, scale, shift)


def _resid_norm_mod(x, mix, gate, nw, scale, shift, name):
    def body(x_ref, mix_ref, g_ref, nw_ref, sc_ref, sh_ref, x1_ref, h_ref):
        xv = x_ref[...] + g_ref[...] * mix_ref[...]
        x1_ref[...] = xv
        r = lax.rsqrt(jnp.mean(xv * xv, axis=-1, keepdims=True) + EPS)
        h_ref[...] = ((xv * r) * nw_ref[...] * (1.0 + sc_ref[...]) + sh_ref[...]).astype(BF16)

    return pl.pallas_call(
        body, name=name,
        out_shape=(jax.ShapeDtypeStruct((S, D), F32), jax.ShapeDtypeStruct((S, D), BF16)), grid=(S // TR,),
        in_specs=[_row_spec(), _row_spec(), _vec_spec(), _vec_spec(), _vec_spec(), _vec_spec()],
        out_specs=(_row_spec(), _row_spec()),
        compiler_params=_params(1),
    )(x, mix, gate, nw, scale, shift)


def _norm_mod_bwd(dh, xin, nw, scale, dres, name, mix=None, gate=None, phase=None):
    with_gate = mix is not None

    def body(*refs):
        if with_gate:
            dh_ref, x_ref, nw_ref, sc_ref, dres_ref, mix_ref, g_ref, dx_ref, dsh_ref, dsc_ref, dnw_ref, dg_ref, dmix_ref = refs
        else:
            dh_ref, x_ref, nw_ref, sc_ref, dres_ref, dx_ref, dsh_ref, dsc_ref, dnw_ref = refs
        i = pl.program_id(0)
        dhv = dh_ref[...]
        xv = x_ref[...]
        r = lax.rsqrt(jnp.mean(xv * xv, axis=-1, keepdims=True) + EPS)
        xn = xv * r
        nwv = nw_ref[...]
        one_sc = 1.0 + sc_ref[...]
        dxn = dhv * nwv * one_sc
        dx = dres_ref[...] + r * (dxn - xn * jnp.mean(dxn * xn, axis=-1, keepdims=True))
        dx_ref[...] = dx

        @pl.when(i == 0)
        def _():
            dsh_ref[...] = jnp.zeros_like(dsh_ref)
            dsc_ref[...] = jnp.zeros_like(dsc_ref)
            dnw_ref[...] = jnp.zeros_like(dnw_ref)
            if with_gate:
                dg_ref[...] = jnp.zeros_like(dg_ref)

        dsh_ref[...] += jnp.sum(dhv, axis=0, keepdims=True)
        dsc_ref[...] += jnp.sum(dhv * xn * nwv, axis=0, keepdims=True)
        dnw_ref[...] += jnp.sum(dhv * xn * one_sc, axis=0, keepdims=True)
        if with_gate:
            dg_ref[...] += jnp.sum(dx * mix_ref[...], axis=0, keepdims=True)
            dmix_ref[...] = (dx * g_ref[...]).astype(BF16)

    vec = jax.ShapeDtypeStruct((1, D), F32)
    ins = [dh, xin, nw, scale, dres]
    in_specs = [_row_spec(), _row_spec(), _vec_spec(), _vec_spec(), _row_spec()]
    outs = [jax.ShapeDtypeStruct((S, D), F32), vec, vec, vec]
    out_specs = [_row_spec(), _vec_spec(), _vec_spec(), _vec_spec()]
    if with_gate:
        ins += [mix, gate]
        in_specs += [_row_spec(), _vec_spec()]
        outs += [vec, jax.ShapeDtypeStruct((S, D), BF16)]
        out_specs += [_vec_spec(), _row_spec()]
    return _pcall(
        body, ins, phase, name=name, out_shape=tuple(outs), grid=(S // TR,), in_specs=in_specs,
        out_specs=tuple(out_specs), compiler_params=_params(1),
    )


def _tri(lower):
    row = lax.broadcasted_iota(jnp.int32, (CH, CH), 0)
    col = lax.broadcasted_iota(jnp.int32, (CH, CH), 1)
    return (row >= col) if lower else (col >= row)


def _hgrn_gates(hq, hf, lb):
    sig = _sigmoid(hf)
    f = lb + (1.0 - lb) * sig
    g = jnp.log(f)
    sq = _sigmoid(hq)
    return sig, f, g, 1.0 - f, hq * sq, sq


HG_HP = 2
HG_UNROLL = 8


def _proj_col_spec(group):
    return pl.BlockSpec((S, HG_HP * DH), lambda h: (0, group * (H // HG_HP) + h))


def _hgrn_fwd(proj, lb_logits, norm_w, phase=None):
    def body(hq_ref, hf_ref, hi_ref, hg_ref, lbl_ref, nw_ref, out_ref, oraw_ref, st_ref, s_ref):
        nw = nw_ref[...]
        lower = _tri(True)
        ltri = lower.astype(F32)
        s_ref[...] = jnp.zeros_like(s_ref)

        def chunk(n, carry):
            rows = pl.ds(pl.multiple_of(n * CH, CH), CH)
            for j in range(HG_HP):
                cols = slice(j * DH, (j + 1) * DH)
                lb = 1.0 / (1.0 + jnp.exp(lbl_ref[1:2, cols] - lbl_ref[0:1, cols]))
                hq, hf, v, hg = hq_ref[rows, cols], hf_ref[rows, cols], hi_ref[rows, cols], hg_ref[rows, cols]
                _, _, g, kk, q, _ = _hgrn_gates(hq, hf, lb)
                gc = _dot_f32(ltri, g)
                gl = jnp.sum(g, axis=0, keepdims=True)
                qe = q * jnp.exp(gc)
                ke = kk * jnp.exp(gl - gc)
                qh = q * jnp.exp(gc - 0.5 * gl)
                kh = kk * jnp.exp(0.5 * gl - gc)
                st = s_ref[j]
                st_ref[j, pl.ds(n, 1)] = st[None]
                a = jnp.where(lower, _dot(qh, kh, NT), 0.0)
                o = _dot(qe, st, NT) + _dot(a, v)
                s_ref[j] = st * jnp.exp(gl) + _dot(v, ke, TN)
                oraw_ref[rows, cols] = o
                rs = lax.rsqrt(jnp.mean(o * o, axis=-1, keepdims=True) + EPS)
                out_ref[rows, cols] = (o * rs * nw * (hg * _sigmoid(hg))).astype(BF16)
            return carry

        lax.fori_loop(0, NCH, chunk, 0, unroll=HG_UNROLL)

    head_spec = pl.BlockSpec((S, HG_HP * DH), lambda h: (0, h))
    return _pcall(
        body, [proj, proj, proj, proj, lb_logits, norm_w], phase, name="hgrn_fwd",
        out_shape=(jax.ShapeDtypeStruct((S, 2 * HW), BF16), jax.ShapeDtypeStruct((S, HW), F32),
                   jax.ShapeDtypeStruct((H, NCH, DH, DH), F32)),
        grid=(H // HG_HP,),
        in_specs=[_proj_col_spec(0), _proj_col_spec(1), _proj_col_spec(2), _proj_col_spec(3),
                  pl.BlockSpec((2, HG_HP * DH), lambda h: (0, h)), pl.BlockSpec((1, DH), lambda h: (0, 0))],
        out_specs=(head_spec, head_spec, pl.BlockSpec((HG_HP, NCH, DH, DH), lambda h: (h, 0, 0, 0))),
        scratch_shapes=[pltpu.VMEM((HG_HP, DH, DH), F32)],
        compiler_params=_params(1),
    )


def _hgrn_bwd(proj, lb_logits, norm_w, oraw, states, dout, phase=None):
    def body(hq_ref, hf_ref, hi_ref, hg_ref, lbl_ref, nw_ref, oraw_ref, st_ref, do_ref,
             dhq_ref, dhf_ref, dhi_ref, dhg_ref, dlb_ref, dnw_ref, ds_ref, acc_ref):
        h = pl.program_id(0)
        nw = nw_ref[...]
        lower = _tri(True)
        ltri = lower.astype(F32)
        utri = _tri(False).astype(F32)
        last = lax.broadcasted_iota(jnp.int32, (CH, DH), 0) == CH - 1
        ds_ref[...] = jnp.zeros_like(ds_ref)
        acc_ref[...] = jnp.zeros_like(acc_ref)

        @pl.when(h == 0)
        def _():
            dnw_ref[...] = jnp.zeros_like(dnw_ref)

        def chunk(i, carry):
            n = NCH - 1 - i
            rows = pl.ds(pl.multiple_of(n * CH, CH), CH)
            for j in range(HG_HP):
                cols = slice(j * DH, (j + 1) * DH)
                lb = 1.0 / (1.0 + jnp.exp(lbl_ref[1:2, cols] - lbl_ref[0:1, cols]))
                hq, hf, v, hg = hq_ref[rows, cols], hf_ref[rows, cols], hi_ref[rows, cols], hg_ref[rows, cols]
                sig, f, g, kk, q, sq = _hgrn_gates(hq, hf, lb)
                gc = _dot_f32(ltri, g)
                gl = jnp.sum(g, axis=0, keepdims=True)
                eg = jnp.exp(gc)
                ek = jnp.exp(gl - gc)
                gam = jnp.exp(gl)
                ph = jnp.exp(gc - 0.5 * gl)
                pk = jnp.exp(0.5 * gl - gc)
                qe, ke = q * eg, kk * ek
                qh, kh = q * ph, kk * pk
                st = st_ref[j, pl.ds(n, 1)][0]
                dst = ds_ref[j]
                a = jnp.where(lower, _dot(qh, kh, NT), 0.0)
                o = oraw_ref[rows, cols]
                rs = lax.rsqrt(jnp.mean(o * o, axis=-1, keepdims=True) + EPS)
                xh = o * rs
                sg = _sigmoid(hg)
                dgo = do_ref[rows, cols]
                d_on = dgo * (hg * sg)
                dhg_ref[rows, cols] = (dgo * xh * nw * (sg * (1.0 + hg * (1.0 - sg)))).astype(BF16)
                acc_ref[j, 1:2, :] += jnp.sum(d_on * xh, axis=0, keepdims=True)
                dy = d_on * nw
                do = rs * (dy - xh * jnp.mean(dy * xh, axis=-1, keepdims=True))
                dqe = _dot(do, st)
                da = jnp.where(lower, _dot(do, v, NT), 0.0)
                dv = _dot(a, do, TN) + _dot(ke, dst, NT)
                dke = _dot(v, dst)
                dgam = jnp.sum(dst * st, axis=0, keepdims=True)
                dqh = lax.dot_general(da, kh, NN, precision=lax.Precision.HIGH, preferred_element_type=F32)
                dkh = lax.dot_general(da, qh, TN, precision=lax.Precision.HIGH, preferred_element_type=F32)
                dq = dqh * ph + dqe * eg
                dk = dkh * pk + dke * ek
                dgl = jnp.sum(dke * ke, axis=0, keepdims=True) + dgam * gam
                dgc = dqh * qh - dkh * kh + dqe * qe - dke * ke + jnp.where(last, dgl, 0.0)
                dg = _dot_f32(utri, dgc)
                df = dg / f - dk
                dhf_ref[rows, cols] = (df * (1.0 - lb) * sig * (1.0 - sig)).astype(BF16)
                acc_ref[j, 0:1, :] += jnp.sum(df * (1.0 - sig), axis=0, keepdims=True)
                dhq_ref[rows, cols] = (dq * (sq * (1.0 + hq * (1.0 - sq)))).astype(BF16)
                dhi_ref[rows, cols] = dv.astype(BF16)
                ds_ref[j] = dst * gam + _dot(do, qe, TN)
            return carry

        lax.fori_loop(0, NCH, chunk, 0, unroll=HG_UNROLL)
        for j in range(HG_HP):
            dlb_ref[:, j * DH:(j + 1) * DH] = acc_ref[j, 0:1, :]
            dnw_ref[...] += acc_ref[j, 1:2, :]

    head_spec = pl.BlockSpec((S, HG_HP * DH), lambda h: (0, h))
    head_out = jax.ShapeDtypeStruct((S, HW), BF16)
    return _pcall(
        body, [proj, proj, proj, proj, lb_logits, norm_w, oraw, states, dout], phase, name="hgrn_bwd",
        out_shape=(head_out, head_out, head_out, head_out,
                   jax.ShapeDtypeStruct((1, HW), F32), jax.ShapeDtypeStruct((1, DH), F32)),
        grid=(H // HG_HP,),
        in_specs=[_proj_col_spec(0), _proj_col_spec(1), _proj_col_spec(2), _proj_col_spec(3),
                  pl.BlockSpec((2, HG_HP * DH), lambda h: (0, h)), pl.BlockSpec((1, DH), lambda h: (0, 0)),
                  head_spec, pl.BlockSpec((HG_HP, NCH, DH, DH), lambda h: (h, 0, 0, 0)), head_spec],
        out_specs=(head_spec, head_spec, head_spec, head_spec,
                   pl.BlockSpec((1, HG_HP * DH), lambda h: (0, h)), pl.BlockSpec((1, DH), lambda h: (0, 0))),
        scratch_shapes=[pltpu.VMEM((HG_HP, DH, DH), F32), pltpu.VMEM((HG_HP, 8, DH), F32)],
        compiler_params=_params(1),
    )


ATT_SCALE = DH ** -0.5
HEADS_PER_STEP = {1: 8, 4: 1, 16: 1}


def _sub_rows(ref, r, dil, cols):
    if dil == 1:
        return ref[:, cols]
    return ref[pl.ds(r, BLK, stride=dil), cols]


def _set_sub_rows(ref, r, dil, cols, val):
    if dil == 1:
        ref[:, cols] = val
    else:
        ref[pl.ds(r, BLK, stride=dil), cols] = val


def _slopes(dil):
    hp = HEADS_PER_STEP[dil]
    s = jnp.asarray([dil * 2.0 ** (-(h + 1)) for h in range(H)], F32)
    return jnp.broadcast_to(s[:, None], (H, DH)).reshape(H // hp, hp, DH)


def _rms_rows(x):
    rs = lax.rsqrt(jnp.mean(x * x, axis=-1, keepdims=True) + EPS)
    return x * rs, rs


def _att_masks():
    qi = lax.broadcasted_iota(jnp.int32, (BLK, BLK), 0)
    kj = lax.broadcasted_iota(jnp.int32, (BLK, BLK), 1)
    steps_c = qi - kj
    steps_p = qi - kj + BLK
    return steps_c, steps_p, steps_c >= 0, steps_p <= BLK


def _qk_prep(proj, q_w, k_w):
    def body(q_ref, k_ref, qw_ref, kw_ref, qn_ref, kn_ref):
        for h in range(H):
            cols = slice(h * DH, (h + 1) * DH)
            qn_ref[:, cols] = _rms_rows(q_ref[:, cols])[0] * qw_ref[...]
            kn_ref[:, cols] = _rms_rows(k_ref[:, cols])[0] * kw_ref[...]

    out = jax.ShapeDtypeStruct((S, HW), F32)
    wspec = pl.BlockSpec((1, DH), lambda i: (0, 0))
    return pl.pallas_call(
        body, name="qk_prep", out_shape=(out, out), grid=(S // TR,),
        in_specs=[pl.BlockSpec((TR, HW), lambda i: (i, 4)), pl.BlockSpec((TR, HW), lambda i: (i, 5)), wspec, wspec],
        out_specs=(_row_spec(HW), _row_spec(HW)),
        compiler_params=_params(1),
    )(proj, proj, q_w, k_w)


def _qk_norm_bwd(proj, dqn, dkn, dv, q_w, k_w):
    def body(q_ref, k_ref, dqn_ref, dkn_ref, dv_ref, qw_ref, kw_ref, dq_ref, dk_ref, dvo_ref, dqw_ref, dkw_ref):
        i = pl.program_id(0)

        @pl.when(i == 0)
        def _():
            dqw_ref[...] = jnp.zeros_like(dqw_ref)
            dkw_ref[...] = jnp.zeros_like(dkw_ref)

        dvo_ref[...] = dv_ref[...].astype(BF16)
        for x_ref, d_ref, w_ref, o_ref, dw_ref in ((q_ref, dqn_ref, qw_ref, dq_ref, dqw_ref),
                                                   (k_ref, dkn_ref, kw_ref, dk_ref, dkw_ref)):
            for h in range(H):
                cols = slice(h * DH, (h + 1) * DH)
                xh, rs = _rms_rows(x_ref[:, cols])
                d = d_ref[:, cols]
                dw_ref[...] += jnp.sum(d * xh, axis=0, keepdims=True)
                dy = d * w_ref[...]
                o_ref[:, cols] = (rs * (dy - xh * jnp.mean(dy * xh, axis=-1, keepdims=True))).astype(BF16)

    big = jax.ShapeDtypeStruct((S, HW), BF16)
    small = jax.ShapeDtypeStruct((1, DH), F32)
    wspec = pl.BlockSpec((1, DH), lambda i: (0, 0))
    return pl.pallas_call(
        body, name="qk_norm_bwd", out_shape=(big, big, big, small, small), grid=(S // TR,),
        in_specs=[pl.BlockSpec((TR, HW), lambda i: (i, 4)), pl.BlockSpec((TR, HW), lambda i: (i, 5)),
                  _row_spec(HW), _row_spec(HW), _row_spec(HW), wspec, wspec],
        out_specs=(_row_spec(HW), _row_spec(HW), _row_spec(HW), wspec, wspec),
        compiler_params=_params(1),
    )(proj, proj, dqn, dkn, dv, q_w, k_w)


def _attn_delta(do, o):
    def body(do_ref, o_ref, d_ref):
        for h in range(H):
            cols = slice(h * DH, (h + 1) * DH)
            d_ref[:, cols] = jnp.broadcast_to(jnp.sum(do_ref[:, cols] * o_ref[:, cols], axis=-1, keepdims=True), (TR, DH))

    return pl.pallas_call(
        body, name="attn_delta", out_shape=jax.ShapeDtypeStruct((S, HW), F32), grid=(S // TR,),
        in_specs=[pl.BlockSpec((TR, HW), lambda i: (i, 1)), _row_spec(HW)], out_specs=_row_spec(HW),
        compiler_params=_params(1),
    )(do, o)


def _attn_fwd(proj, qn, kn, dil, phase=None):
    hp = HEADS_PER_STEP[dil]
    rows, ng, nb = BLK * dil, H // hp, S // (BLK * dil)

    def body(q_ref, kc_ref, kp_ref, vc_ref, vp_ref, sl_ref, o_ref, lse_ref):
        n = pl.program_id(0)
        steps_c, steps_p, ok_c, ok_p = _att_masks()
        ok_p = jnp.logical_and(ok_p, n > 0)
        fc, fp = steps_c.astype(F32), steps_p.astype(F32)
        for hh in range(hp):
            cols = slice(hh * DH, (hh + 1) * DH)
            sl = sl_ref[0, hh:hh + 1, :]
            for r in range(dil):
                sub = lambda ref: _sub_rows(ref, r, dil, cols)
                qv = sub(q_ref)
                sc = jnp.where(ok_c, _dot(qv, sub(kc_ref), NT) * ATT_SCALE - sl * fc, NEG)
                sp = jnp.where(ok_p, _dot(qv, sub(kp_ref), NT) * ATT_SCALE - sl * fp, NEG)
                m = jnp.maximum(jnp.max(sc, axis=-1, keepdims=True), jnp.max(sp, axis=-1, keepdims=True))
                pc, pp = jnp.exp(sc - m), jnp.exp(sp - m)
                den = jnp.sum(pc, axis=-1, keepdims=True) + jnp.sum(pp, axis=-1, keepdims=True)
                o = (_dot(pc, sub(vc_ref)) + _dot(pp, sub(vp_ref))) / den
                _set_sub_rows(o_ref, r, dil, cols, o)
                _set_sub_rows(lse_ref, r, dil, cols, jnp.broadcast_to(m + jnp.log(den), (BLK, DH)))

    cur = pl.BlockSpec((rows, hp * DH), lambda n, g: (n, g))
    prev = pl.BlockSpec((rows, hp * DH), lambda n, g: (jnp.maximum(n - 1, 0), g))
    vcur = pl.BlockSpec((rows, hp * DH), lambda n, g: (n, 6 * ng + g))
    vprev = pl.BlockSpec((rows, hp * DH), lambda n, g: (jnp.maximum(n - 1, 0), 6 * ng + g))
    out = jax.ShapeDtypeStruct((S, HW), F32)
    return _pcall(
        body, [qn, kn, kn, proj, proj, _slopes(dil)], phase,
        name=f"attn_fwd_d{dil}", out_shape=(out, out), grid=(nb, ng),
        in_specs=[cur, cur, prev, vcur, vprev, pl.BlockSpec((1, hp, DH), lambda n, g: (g, 0, 0))],
        out_specs=(cur, cur),
        compiler_params=_params(2),
    )


def _attn_merge(outs, lses, cat):
    def body(o1, o2, o3, l1, l2, l3, cat_ref, ob_ref, of_ref, lse_ref):
        a, b, c = l1[...], l2[...], l3[...]
        m = jnp.maximum(jnp.maximum(a, b), c)
        ea, eb, ec = jnp.exp(a - m), jnp.exp(b - m), jnp.exp(c - m)
        den = ea + eb + ec
        o = (ea * o1[...] + eb * o2[...] + ec * o3[...]) / den
        ob_ref[...] = o.astype(BF16)
        of_ref[...] = o
        lse_ref[...] = m + jnp.log(den)

    f = jax.ShapeDtypeStruct((S, HW), F32)
    return pl.pallas_call(
        body, name="attn_merge", out_shape=(jax.ShapeDtypeStruct((S, 2 * HW), BF16), f, f), grid=(S // TR,),
        in_specs=[_row_spec(HW)] * 6 + [pl.BlockSpec(memory_space=pl.ANY)],
        out_specs=(pl.BlockSpec((TR, HW), lambda i: (i, 1)), _row_spec(HW), _row_spec(HW)),
        input_output_aliases={6: 0},
        compiler_params=_params(1),
    )(*outs, *lses, cat)


def _attn_bwd(proj, qn, kn, lse, delta, do, dil, acc, phase=None):
    hp = HEADS_PER_STEP[dil]
    rows, ng, nb = BLK * dil, H // hp, S // (BLK * dil)
    has_acc = acc is not None

    def body(*refs):
        q_ref, qn_ref, kp_ref, kc_ref, vp_ref, vc_ref, l_ref, ln_ref, d_ref, dn_ref, do_ref, don_ref, sl_ref = refs[:13]
        refs = refs[13:]
        if has_acc:
            aq_ref, ak_ref, av_ref = refs[:3]
            refs = refs[3:]
        dq_ref, dk_ref, dv_ref = refs
        m = pl.program_id(0)
        steps_c, steps_p, ok_c, ok_p = _att_masks()
        ok_prev = jnp.logical_and(ok_p, m > 0)
        ok_next = jnp.logical_and(ok_p, m < nb - 1)
        fc, fp = steps_c.astype(F32), steps_p.astype(F32)
        for hh in range(hp):
            cols = slice(hh * DH, (hh + 1) * DH)
            sl = sl_ref[0, hh:hh + 1, :]
            for r in range(dil):
                sub = lambda ref: _sub_rows(ref, r, dil, cols)
                qv, qnv, kcv, kpv = sub(q_ref), sub(qn_ref), sub(kc_ref), sub(kp_ref)
                vc, vp = sub(vc_ref), sub(vp_ref)
                dov, donv = sub(do_ref), sub(don_ref)
                lse_m, lse_n = sub(l_ref)[:, 0:1], sub(ln_ref)[:, 0:1]
                delta_m, delta_n = sub(d_ref)[:, 0:1], sub(dn_ref)[:, 0:1]
                p_c = jnp.where(ok_c, jnp.exp(_dot(qv, kcv, NT) * ATT_SCALE - sl * fc - lse_m), 0.0)
                p_p = jnp.where(ok_prev, jnp.exp(_dot(qv, kpv, NT) * ATT_SCALE - sl * fp - lse_m), 0.0)
                p_n = jnp.where(ok_next, jnp.exp(_dot(qnv, kcv, NT) * ATT_SCALE - sl * fp - lse_n), 0.0)
                ds_c = p_c * (_dot(dov, vc, NT) - delta_m)
                ds_p = p_p * (_dot(dov, vp, NT) - delta_m)
                ds_n = p_n * (_dot(donv, vc, NT) - delta_n)
                dqn = (_dot(ds_c, kcv) + _dot(ds_p, kpv)) * ATT_SCALE
                dkn = (_dot(ds_c, qv, TN) + _dot(ds_n, qnv, TN)) * ATT_SCALE
                dv = _dot(p_c, dov, TN) + _dot(p_n, donv, TN)
                if has_acc:
                    dqn, dkn, dv = dqn + sub(aq_ref), dkn + sub(ak_ref), dv + sub(av_ref)
                _set_sub_rows(dq_ref, r, dil, cols, dqn)
                _set_sub_rows(dk_ref, r, dil, cols, dkn)
                _set_sub_rows(dv_ref, r, dil, cols, dv)

    def shifted(shift, m):
        if shift < 0:
            return jnp.maximum(m - 1, 0)
        if shift > 0:
            return jnp.minimum(m + 1, nb - 1)
        return m

    def spec(shift, group=0):
        return pl.BlockSpec((rows, hp * DH), lambda m, g: (shifted(shift, m), group * ng + g))

    ins = [qn, qn, kn, kn, proj, proj, lse, lse, delta, delta, do, do, _slopes(dil)]
    in_specs = [spec(0), spec(1), spec(-1), spec(0), spec(-1, 6), spec(0, 6), spec(0), spec(1), spec(0), spec(1),
                spec(0, 1), spec(1, 1), pl.BlockSpec((1, hp, DH), lambda m, g: (g, 0, 0))]
    if has_acc:
        ins += list(acc)
        in_specs += [spec(0)] * 3
    big = jax.ShapeDtypeStruct((S, HW), F32)
    return _pcall(
        body, ins, phase, name=f"attn_bwd_d{dil}", out_shape=(big, big, big), grid=(nb, ng),
        in_specs=in_specs, out_specs=(spec(0),) * 3,
        compiler_params=_params(2),
    )


TC = 512
NCT = DFF // TC


def _shift_rows(a, k):
    rows = lax.broadcasted_iota(jnp.int32, a.shape, 0)
    rolled = pltpu.roll(a, k % S, 0)
    if k > 0:
        return jnp.where(rows >= k, rolled, 0.0)
    return jnp.where(rows < S + k, rolled, 0.0)


def _conv_pre(a, w_ref, b_ref):
    return b_ref[...] + w_ref[0:1, :] * _shift_rows(a, 2) + w_ref[1:2, :] * _shift_rows(a, 1) + w_ref[2:3, :] * a


def _conv_gate_fwd(u, conv_w, conv_b, name, tiles=(0, NCT), into=None, phase=None):
    t0, t1 = tiles

    def body(a_ref, g_ref, w_ref, b_ref, *rest):
        pre = _conv_pre(a_ref[...].astype(F32), w_ref, b_ref)
        rest[-1][...] = (pre * _sigmoid(pre) * g_ref[...].astype(F32)).astype(BF16)

    args = [u, u, conv_w, conv_b]
    in_specs = [pl.BlockSpec((S, TC), lambda i: (0, t0 + i)), pl.BlockSpec((S, TC), lambda i: (0, NCT + t0 + i)),
                pl.BlockSpec((3, TC), lambda i: (0, t0 + i)), pl.BlockSpec((1, TC), lambda i: (0, t0 + i))]
    kw = {}
    if into is not None:
        args.append(into)
        in_specs.append(pl.BlockSpec(memory_space=pl.ANY))
        kw["input_output_aliases"] = {4: 0}
    return _pcall(
        body, args, phase, name=name, out_shape=jax.ShapeDtypeStruct((S, DFF), BF16), grid=(t1 - t0,),
        in_specs=in_specs, out_specs=pl.BlockSpec((S, TC), lambda i: (0, t0 + i)),
        compiler_params=_params(1), **kw,
    )


def _conv_gate_bwd(dy, u, conv_w, conv_b, phase=None):
    def body(dy_ref, a_ref, g_ref, w_ref, b_ref, du_ref, db_ref, dw_ref, da_buf, dg_buf, sems):
        i = pl.program_id(0)
        slot = i % 2

        def writes(step, s):
            col = pl.multiple_of(step * TC, TC)
            return (pltpu.make_async_copy(da_buf.at[s], du_ref.at[:, pl.ds(col, TC)], sems.at[0, s]),
                    pltpu.make_async_copy(dg_buf.at[s], du_ref.at[:, pl.ds(DFF + col, TC)], sems.at[1, s]))

        @pl.when(i >= 2)
        def _():
            for cp in writes(i - 2, slot):
                cp.wait()

        a = a_ref[...].astype(F32)
        dyv = dy_ref[...].astype(F32)
        pre = _conv_pre(a, w_ref, b_ref)
        sg = _sigmoid(pre)
        dg_buf[slot] = (dyv * pre * sg).astype(BF16)
        dpre = dyv * g_ref[...].astype(F32) * (sg * (1.0 + pre * (1.0 - sg)))
        da = w_ref[2:3, :] * dpre + w_ref[1:2, :] * _shift_rows(dpre, -1) + w_ref[0:1, :] * _shift_rows(dpre, -2)
        da_buf[slot] = da.astype(BF16)
        for cp in writes(i, slot):
            cp.start()
        db_ref[...] = jnp.sum(dpre, axis=0, keepdims=True)
        dw_ref[0:1, :] = jnp.sum(dpre * _shift_rows(a, 2), axis=0, keepdims=True)
        dw_ref[1:2, :] = jnp.sum(dpre * _shift_rows(a, 1), axis=0, keepdims=True)
        dw_ref[2:3, :] = jnp.sum(dpre * a, axis=0, keepdims=True)

        @pl.when(i == NCT - 1)
        def _():
            for cp in writes(i - 1, 1 - slot) + writes(i, slot):
                cp.wait()

    col = pl.BlockSpec((S, TC), lambda i: (0, i))
    return _pcall(
        body, [dy, u, u, conv_w, conv_b], phase, name="conv_gate_bwd",
        out_shape=(jax.ShapeDtypeStruct((S, 2 * DFF), BF16), jax.ShapeDtypeStruct((1, DFF), F32),
                   jax.ShapeDtypeStruct((3, DFF), F32)),
        grid=(NCT,),
        in_specs=[col, col, pl.BlockSpec((S, TC), lambda i: (0, NCT + i)),
                  pl.BlockSpec((3, TC), lambda i: (0, i)), pl.BlockSpec((1, TC), lambda i: (0, i))],
        out_specs=(pl.BlockSpec(memory_space=pl.ANY), pl.BlockSpec((1, TC), lambda i: (0, i)),
                   pl.BlockSpec((3, TC), lambda i: (0, i))),
        scratch_shapes=[pltpu.VMEM((2, S, TC), BF16), pltpu.VMEM((2, S, TC), BF16), pltpu.SemaphoreType.DMA((2, 2))],
        compiler_params=_params(1),
    )


def _out_loss(z, x1, target, gate):
    def body(z_ref, x_ref, t_ref, g_ref, do_ref, dz_ref, dg_ref, loss_ref):
        i = pl.program_id(0)
        zv = z_ref[...]
        gv = g_ref[...]
        err = x_ref[...] + gv * zv - t_ref[...]
        dout = err * (1.0 / D)
        do_ref[...] = dout
        dz_ref[...] = (dout * gv).astype(BF16)

        @pl.when(i == 0)
        def _():
            dg_ref[...] = jnp.zeros_like(dg_ref)
            loss_ref[...] = jnp.zeros_like(loss_ref)

        dg_ref[...] += jnp.sum(dout * zv, axis=0, keepdims=True)
        part = jnp.sum(jnp.sum(err * err, axis=0, keepdims=True), axis=-1, keepdims=True) * (0.5 / D)
        lane = lax.broadcasted_iota(jnp.int32, (1, 128), 1)
        loss_ref[...] += jnp.where(lane == 0, part, 0.0)

    return pl.pallas_call(
        body, name="out_loss",
        out_shape=(jax.ShapeDtypeStruct((S, D), F32), jax.ShapeDtypeStruct((S, D), BF16),
                   jax.ShapeDtypeStruct((1, D), F32), jax.ShapeDtypeStruct((1, 128), F32)),
        grid=(S // TR,),
        in_specs=[_row_spec(), _row_spec(), _row_spec(), _vec_spec()],
        out_specs=(_row_spec(), _row_spec(), _vec_spec(), _vec_spec(128)),
        compiler_params=_params(1),
    )(z, x1, target, gate)


ADA_COLS = 6 * D // N_CHIPS
TA = 512


def _ada_fwd(c_all, w_shard, b_shard):
    def body(c_ref, w_ref, b_ref, o_ref):
        cv = c_ref[...]
        o_ref[...] = _dot_f32(cv * _sigmoid(cv), w_ref[...]) + b_ref[...]

    return pl.pallas_call(
        body, name="ada_fwd", out_shape=jax.ShapeDtypeStruct((N_DEV, ADA_COLS), F32), grid=(ADA_COLS // TA,),
        in_specs=[pl.BlockSpec((N_DEV, D), lambda j: (0, 0)), pl.BlockSpec((D, TA), lambda j: (0, j)),
                  pl.BlockSpec((1, TA), lambda j: (0, j))],
        out_specs=pl.BlockSpec((N_DEV, TA), lambda j: (0, j)),
        compiler_params=_params(1),
    )(c_all, w_shard, b_shard)


ADA_ROWS = 128


def _ada_bwd_adamw(c_all, dmod_shard, w, m, v):
    def body(c_ref, d_ref, w_ref, m_ref, v_ref, g_ref, dl_ref, mo_ref, vo_ref):
        cv = c_ref[...]
        gv = lax.dot_general(cv * _sigmoid(cv), d_ref[...], TN, precision=lax.Precision.HIGHEST,
                             preferred_element_type=F32)
        g_ref[...] = gv
        m2 = ADAM_B1 * m_ref[...] + (1.0 - ADAM_B1) * gv
        v2 = ADAM_B2 * v_ref[...] + (1.0 - ADAM_B2) * (gv * gv)
        m_hat = m2 / (1.0 - ADAM_B1 ** ADAM_STEP)
        v_hat = v2 / (1.0 - ADAM_B2 ** ADAM_STEP)
        dl_ref[...] = -ADAM_LR * (m_hat / (jnp.sqrt(v_hat) + ADAM_EPS) + ADAM_WD * w_ref[...])
        mo_ref[...] = m2
        vo_ref[...] = v2

    spec = pl.BlockSpec((ADA_ROWS, ADA_COLS), lambda i: (i, 0))
    out = jax.ShapeDtypeStruct((D, ADA_COLS), F32)
    return pl.pallas_call(
        body, name="ada_bwd_adamw", out_shape=(out,) * 4, grid=(D // ADA_ROWS,),
        in_specs=[pl.BlockSpec((N_DEV, ADA_ROWS), lambda i: (0, i)), pl.BlockSpec((N_DEV, ADA_COLS), lambda i: (0, 0)),
                  spec, spec, spec],
        out_specs=(spec,) * 4,
        compiler_params=_params(1),
    )(c_all, dmod_shard, w, m, v)


def _sum_rows(g, name):
    rows, n = g.shape

    def body(g_ref, o_ref):
        acc = g_ref[0:1, :]
        for i in range(1, rows):
            acc = acc + g_ref[i:i + 1, :]
        o_ref[...] = acc

    return pl.pallas_call(
        body, name=name, out_shape=jax.ShapeDtypeStruct((1, n), F32),
        in_specs=[VMEM_SPEC], out_specs=VMEM_SPEC,
    )(g)


def _lb_grad(lb_logits, dlb):
    def body(l_ref, d_ref, o_ref):
        p = 1.0 / (1.0 + jnp.exp(l_ref[1:2, :] - l_ref[0:1, :]))
        t = d_ref[...] * p * (1.0 - p)
        o_ref[0:1, :] = t
        o_ref[1:2, :] = -t

    return pl.pallas_call(
        body, name="lb_grad", out_shape=jax.ShapeDtypeStruct((2, HW), F32),
        in_specs=[VMEM_SPEC, VMEM_SPEC], out_specs=VMEM_SPEC,
    )(lb_logits, dlb)


def _adamw(w, g, m, v, name, copy_grad=False):
    rows, cols = w.shape
    tr = rows
    if rows * cols * 4 > ADAM_BLOCK_BYTES:
        tr = _pick(rows, [t for t in (256, 128, 64, 32, 16, 8) if t * cols * 4 <= ADAM_BLOCK_BYTES])

    def body(w_ref, g_ref, m_ref, v_ref, d_ref, mo_ref, vo_ref, *go_ref):
        gv = g_ref[...]
        if copy_grad:
            go_ref[0][...] = gv
        m2 = ADAM_B1 * m_ref[...] + (1.0 - ADAM_B1) * gv
        v2 = ADAM_B2 * v_ref[...] + (1.0 - ADAM_B2) * (gv * gv)
        m_hat = m2 / (1.0 - ADAM_B1 ** ADAM_STEP)
        v_hat = v2 / (1.0 - ADAM_B2 ** ADAM_STEP)
        d_ref[...] = -ADAM_LR * (m_hat / (jnp.sqrt(v_hat) + ADAM_EPS) + ADAM_WD * w_ref[...])
        mo_ref[...] = m2
        vo_ref[...] = v2

    spec = pl.BlockSpec((tr, cols), lambda i: (i, 0))
    out = jax.ShapeDtypeStruct((rows, cols), F32)
    n_out = 4 if copy_grad else 3
    return pl.pallas_call(
        body, name=name, out_shape=(out,) * n_out, grid=(rows // tr,),
        in_specs=[spec] * 4, out_specs=(spec,) * n_out,
        compiler_params=_params(1),
    )(w, g, m, v)


SC_TILES = 32
SC_LANES = 16
SC_COL_PARTS = 2
SC_CHUNK_ROWS = 8


def _adamw_sc(w, g, m, v, name):
    rows, cols = w.shape
    band, part = rows // (SC_TILES // SC_COL_PARTS), cols // SC_COL_PARTS
    assert band % SC_CHUNK_ROWS == 0 and part % SC_LANES == 0, (rows, cols)

    def body(w_hbm, g_hbm, m_hbm, v_hbm, d_hbm, mo_hbm, vo_hbm, go_hbm, wb, gb, mb, vb, db):
        tile = lax.axis_index("sc_subcore") * 2 + lax.axis_index("sc_core")
        row0 = (tile // SC_COL_PARTS) * band
        col0 = (tile % SC_COL_PARTS) * part

        @pl.loop(0, band, step=SC_CHUNK_ROWS)
        def _(r):
            at = lambda ref: ref.at[pl.ds(row0 + r, SC_CHUNK_ROWS), pl.ds(col0, part)]
            pltpu.sync_copy(at(w_hbm), wb)
            pltpu.sync_copy(at(g_hbm), gb)
            pltpu.sync_copy(gb, at(go_hbm))
            pltpu.sync_copy(at(m_hbm), mb)
            pltpu.sync_copy(at(v_hbm), vb)

            @pl.loop(0, SC_CHUNK_ROWS)
            def _(i):
                @pl.loop(0, part, step=SC_LANES)
                def _(j):
                    sl = (i, pl.ds(j, SC_LANES))
                    gv = gb[sl]
                    m2 = ADAM_B1 * mb[sl] + (1.0 - ADAM_B1) * gv
                    v2 = ADAM_B2 * vb[sl] + (1.0 - ADAM_B2) * (gv * gv)
                    m_hat = m2 / (1.0 - ADAM_B1 ** ADAM_STEP)
                    v_hat = v2 / (1.0 - ADAM_B2 ** ADAM_STEP)
                    db[sl] = -ADAM_LR * (m_hat / (jnp.sqrt(v_hat) + ADAM_EPS) + ADAM_WD * wb[sl])
                    mb[sl] = m2
                    vb[sl] = v2

            pltpu.sync_copy(db, at(d_hbm))
            pltpu.sync_copy(mb, at(mo_hbm))
            pltpu.sync_copy(vb, at(vo_hbm))

    out = jax.ShapeDtypeStruct((rows, cols), F32)
    buf = pltpu.VMEM((SC_CHUNK_ROWS, part), F32)
    return pl.kernel(
        body, name=name, out_type=(out, out, out, out),
        mesh=plsc.VectorSubcoreMesh(core_axis_name="sc_core", subcore_axis_name="sc_subcore"),
        scratch_types=[buf] * 5,
    )(w, g, m, v)


W_IN, W_OUT, W_UP, W_DOWN = range(4)
IN_CHUNKS = 2
W_INQ = 4


def _sequence_step(x, target, mod, norm1_w, lb_logits, hg_norm_w, q_norm_w, k_norm_w, norm2_w, conv_w, conv_b, net):
    shift1, scale1, gate1, shift2, scale2, gate2 = [mod[:, i * D:(i + 1) * D] for i in range(6)]

    def run(name, fn, *args, **kw):
        phase = net.host(name)
        if phase is None:
            return fn(*args, **kw)
        res, comm = fn(*args, phase=phase, **kw)
        net.done(name, comm, res[0] if isinstance(res, tuple) else res)
        return res

    h = _norm_mod(x, norm1_w, scale1, shift1, "norm1_fwd")
    proj = run("proj_fwd", _matmul, h, net.full[W_IN], "nn", F32, "proj_fwd")
    cat, o_raw, states = run("hgrn_fwd", _hgrn_fwd, proj, lb_logits, hg_norm_w)
    qn, kn = _qk_prep(proj, q_norm_w, k_norm_w)
    att = [run(f"attn_fwd_d{dil}", _attn_fwd, proj, qn, kn, dil) for dil in DILS]
    cat, b_out_f32, lse = _attn_merge([t[0] for t in att], [t[1] for t in att], cat)
    mix = run("mix_fwd", _matmul, cat, net.full[W_OUT], "nn", F32, "mix_fwd")
    x1, h2 = _resid_norm_mod(x, mix, gate1, norm2_w, scale2, shift2, "norm2_fwd")
    u = run("up_fwd", _matmul, h2, net.full[W_UP], "nn", BF16, "up_fwd")
    yact = run("conv_gate_fwd_a", _conv_gate_fwd, u, conv_w, conv_b, "conv_gate_fwd_a", tiles=(0, NCT // 2 + 1))
    yact = run("conv_gate_fwd_b", _conv_gate_fwd, u, conv_w, conv_b, "conv_gate_fwd_b", tiles=(NCT // 2 + 1, NCT),
               into=yact)
    z =_matmul(yact, net.full[W_DOWN], "nn", F32, "down_fwd")
    dout, dz, dgate2, loss_row = _out_loss(z, x1, target, gate2)

    net.grad[W_DOWN] = _matmul(yact, dz, "tn", BF16, "down_bwd_w")
    dyact = run("down_bwd_x", _matmul, dz, net.full[W_DOWN], "nt", BF16, "down_bwd_x")
    du, dconv_b, dconv_w = run("conv_gate_bwd", _conv_gate_bwd, dyact, u, conv_w, conv_b)
    net.grad[W_UP] = run("up_bwd_w", _matmul, h2, du, "tn", BF16, "up_bwd_w")
    dh2 = run("up_bwd_x", _matmul, du, net.full[W_UP], "nt", F32, "up_bwd_x")
    dx1, dshift2, dscale2, dnorm2, dgate1, dmix = run(
        "norm2_bwd", _norm_mod_bwd, dh2, x1, norm2_w, scale2, dout, "norm2_bwd", mix=mix, gate=gate1)
    net.grad[W_OUT] = run("mix_bwd_w", _matmul, cat, dmix, "tn", BF16, "mix_bwd_w")
    dcat = run("mix_bwd_x", _matmul, dmix, net.full[W_OUT], "nt", F32, "mix_bwd_x")
    dhq, dhf, dhi, dhg, dlb, dhg_norm = run("hgrn_bwd", _hgrn_bwd, proj, lb_logits, hg_norm_w, o_raw, states, dcat)
    delta = _attn_delta(dcat, b_out_f32)
    acc = None
    for dil in DILS:
        acc = run(f"attn_bwd_d{dil}", _attn_bwd, proj, qn, kn, lse, delta, dcat, dil, acc)
    daq, dak, dav, dq_norm, dk_norm = _qk_norm_bwd(proj, *acc, q_norm_w, k_norm_w)
    dproj = jnp.concatenate([dhq, dhf, dhi, dhg, daq, dak, dav], axis=1)
    net.carry = dproj
    for q in range(IN_CHUNKS):
        net.after_chunk(q, run(f"proj_bwd_w_{q}", _matmul, h, net.carry, "tn", BF16, f"proj_bwd_w_{q}",
                               m_blocks=(D // IN_CHUNKS, [q])))
    net.handed = None
    dh = run("proj_bwd_x", _matmul, net.carry, net.full[W_IN], "nt", F32, "proj_bwd_x")
    dh = dh if net.handed is None else net.handed
    grad_x, dshift1, dscale1, dnorm1 = run("norm1_bwd", _norm_mod_bwd, dh, x, norm1_w, scale1, dx1, "norm1_bwd")

    dmod = jnp.concatenate([dshift1, dscale1, dgate1, dshift2, dscale2, dgate2], axis=1)
    small = dict(norm1_w=dnorm1, lb=dlb, hg_norm_w=dhg_norm, q_norm_w=dq_norm, k_norm_w=dk_norm,
                 norm2_w=dnorm2, conv_b=dconv_b, conv_w=dconv_w)
    return loss_row, grad_x, dmod, small


def _place():
    x, y, c = lax.axis_index("x"), lax.axis_index("y"), lax.axis_index("c")
    others = [(1 - x, y), (x, 1 - y), (1 - x, 1 - y)]
    return x, y, c, others


def _remote(src, dst, send_sems, recv_sems, k, to):
    return pltpu.make_async_remote_copy(src_ref=src, dst_ref=dst, send_sem=send_sems.at[k], recv_sem=recv_sems.at[k],
                                        device_id=to, device_id_type=MESH)


class _Phase:
    def __init__(self):
        self.ins, self.outs, self.aliases, self.groups, self.n = [], [], {}, [], 0

    def add(self, ins, outs, aliases, n, copies):
        i0, o0 = len(self.ins), len(self.outs)
        self.ins += list(ins)
        self.outs += list(outs)
        self.aliases.update({i0 + i: o0 + o for i, o in aliases.items()})
        self.groups.append((slice(i0, i0 + len(ins)), slice(o0, o0 + len(outs)), copies))
        self.n += n
        return slice(o0, o0 + len(outs))

    def build(self, cin, cout, send_sems, recv_sems, landing):
        res = []
        for si, so, copies in self.groups:
            for src, dst, land, peer in copies(cin[si], cout[so]):
                k = len(res)
                res.append(_remote(land, land, send_sems, recv_sems, k, peer) if landing
                           else _remote(src, dst, send_sems, recv_sems, k, peer))
        assert len(res) == self.n
        return res


def _pcall(body, args, phase=None, **kw):
    if phase is None or not phase.groups:
        res = pl.pallas_call(body, **kw)(*args)
        return res if phase is None else (res, ())
    grid = tuple(kw.pop("grid", ()))
    out_shape, out_specs = kw.pop("out_shape"), kw.pop("out_specs")
    single = not isinstance(out_shape, (tuple, list))
    out_shape = [out_shape] if single else list(out_shape)
    out_specs = [out_specs] if single else list(out_specs)
    scratch = list(kw.pop("scratch_shapes", ()))
    aliases = dict(kw.pop("input_output_aliases", {}))
    n_in, n_out, n_ci, n_co = len(args), len(out_shape), len(phase.ins), len(phase.outs)
    aliases.update({n_in + i: n_out + o for i, o in phase.aliases.items()})

    def hosted(*refs):
        ins, cin = refs[:n_in], refs[n_in:n_in + n_ci]
        outs = refs[n_in + n_ci:n_in + n_ci + n_out]
        cout = refs[n_in + n_ci + n_out:n_in + n_ci + n_out + n_co]
        scr, send_sems, recv_sems = refs[n_in + n_ci + n_out + n_co:-2], refs[-2], refs[-1]
        ids = [pl.program_id(a) for a in range(len(grid))]

        def start():
            for cp in phase.build(cin, cout, send_sems, recv_sems, False):
                cp.start()

        def finish():
            for cp in phase.build(cin, cout, send_sems, recv_sems, False):
                cp.wait_send()
            for cp in phase.build(cin, cout, send_sems, recv_sems, True):
                cp.wait_recv()

        if grid:
            first = functools.reduce(jnp.logical_and, [i == 0 for i in ids])
            last = functools.reduce(jnp.logical_and, [i == g - 1 for i, g in zip(ids, grid)])
            pl.when(first)(start)
            body(*ins, *outs, *scr)
            pl.when(last)(finish)
        else:
            start()
            body(*ins, *outs, *scr)
            finish()

    res = pl.pallas_call(
        hosted, out_shape=tuple(out_shape + phase.outs), grid=grid,
        in_specs=list(kw.pop("in_specs")) + [HBM_SPEC] * n_ci, out_specs=tuple(out_specs + [HBM_SPEC] * n_co),
        input_output_aliases=aliases,
        scratch_shapes=scratch + [pltpu.SemaphoreType.DMA((phase.n,)), pltpu.SemaphoreType.DMA((phase.n,))],
        **kw,
    )(*args, *phase.ins)
    outs = res[:n_out]
    return (outs[0] if single else tuple(outs)), tuple(res[n_out:])


def _comm_only(name, phase):
    return _pcall(lambda: None, [], phase, name=name, out_shape=(), in_specs=[], out_specs=())[1]


def _all_gather_rows(block, name, phase=None):
    m_per, n = block.shape

    def body(x_ref, out_ref, send_sems, recv_sems, local_sem):
        x, y, c, chips = _place()
        me, sibling = (x, y, c), (x, y, 1 - c)

        def rows(px, py, pc):
            return out_ref.at[pl.ds((4 * px + 2 * py + pc) * m_per, m_per), :]

        def copy(k, blk, to, src=None):
            return _remote(rows(*blk) if src is None else src, rows(*blk), send_sems, recv_sems, k, to)

        mine = pltpu.make_async_copy(x_ref, rows(*me), local_sem)
        mine.start()
        first = [copy(0, me, sibling, src=x_ref)]
        first += [copy(1 + j, me, (*chip, c), src=x_ref) for j, chip in enumerate(chips)]
        for cp in first:
            cp.start()
        passed = [copy(4 + j, (*chip, c), sibling) for j, chip in enumerate(chips)]
        for j, chip in enumerate(chips):
            copy(1 + j, (*chip, c), me).wait_recv()
            passed[j].start()
        copy(0, sibling, me).wait_recv()
        for j, chip in enumerate(chips):
            copy(4 + j, (*chip, 1 - c), me).wait_recv()
        for cp in first + passed:
            cp.wait_send()
        mine.wait()

    return _pcall(
        body, [block], phase, name=name, out_shape=jax.ShapeDtypeStruct((N_DEV * m_per, n), block.dtype),
        in_specs=[VMEM_SPEC], out_specs=VMEM_SPEC,
        scratch_shapes=[pltpu.SemaphoreType.DMA((7,)), pltpu.SemaphoreType.DMA((7,)), pltpu.SemaphoreType.DMA],
    )


class _Geo:
    def __init__(self, kind, shard_shape):
        self.kind = kind
        self.shard_shape = tuple(shard_shape)
        self.hr, self.hc = shard_shape[0] // 2, shard_shape[1]
        if kind == "col":
            self.full_shape = (shard_shape[0], N_CHIPS * shard_shape[1])
        else:
            self.full_shape = (N_CHIPS * shard_shape[0], shard_shape[1])

    def full_half(self, ref, j, c):
        return self.piece(ref, j, c, 0, 1, 1)

    def piece(self, ref, j, c, p0, p1, pieces, part=None):
        pr = self.hr // pieces
        off, n = p0 * pr, (p1 - p0) * pr
        if part is not None:
            top = (n // 32) * 16
            off, n = (off, top) if part == 0 else (off + top, n - top)
        if self.kind == "col":
            return ref.at[pl.ds(pl.multiple_of(c * self.hr + off, 16), n),
                          pl.ds(pl.multiple_of(j * self.hc, 128), self.hc)]
        return ref.at[pl.ds(pl.multiple_of((2 * j + c) * self.hr + off, 16), n), :]


WEIGHT_KINDS = ("col", "row", "col", "row")
NW = len(WEIGHT_KINDS)


def _comm_call(body, name, ins, outs, n_remote, aliases=None):
    return pl.pallas_call(
        body, name=name, out_shape=tuple(outs),
        in_specs=[HBM_SPEC] * len(ins), out_specs=tuple([HBM_SPEC] * len(outs)),
        input_output_aliases=aliases or {},
        scratch_shapes=[pltpu.SemaphoreType.DMA((n_remote,)), pltpu.SemaphoreType.DMA((n_remote,))],
    )(*ins)


ROW_TILES = (256, 176, 128, 64, 32, 16)


def _cast_into_full(shard, geo, place, name):
    rs, cs = shard.shape
    tr = _pick(rs, ROW_TILES)
    nb = rs // tr

    def body(place_ref, s_ref, o_ref):
        o_ref[...] = s_ref[...].astype(BF16)

    if geo.kind == "col":
        out_map = lambda i, place_ref: (i, place_ref[0])
    else:
        out_map = lambda i, place_ref: (place_ref[0] * nb + i, 0)
    return pl.pallas_call(
        body, name=name, out_shape=jax.ShapeDtypeStruct(geo.full_shape, BF16),
        grid_spec=pltpu.PrefetchScalarGridSpec(
            num_scalar_prefetch=1, grid=(nb,),
            in_specs=[pl.BlockSpec((tr, cs), lambda i, place_ref: (i, 0))],
            out_specs=pl.BlockSpec((tr, cs), out_map)),
        compiler_params=_params(1),
    )(place, shard)


def _gather_weight(full, geo, pieces, name):
    per = 8

    def body(in_ref, ref, send_sems, recv_sems):
        x, y, c, _ = _place()
        j, jx, jy, jo = 2 * x + y, 2 * (1 - x) + y, 2 * x + (1 - y), 2 * (1 - x) + (1 - y)
        nx, ny, sib = (1 - x, y, c), (x, 1 - y, c), (x, y, 1 - c)

        def cp(region, k, to):
            return _remote(region, region, send_sems, recv_sems, k, to)

        def reg(chip, p, part=None, core=c):
            return geo.piece(ref, chip, core, p, p + 1, pieces, part)

        sent = []
        for p in range(pieces):
            sent += [cp(reg(j, p), per * p, nx), cp(reg(j, p), per * p + 1, ny)]
        for d in sent:
            d.start()
        for p in range(pieces):
            cp(reg(jx, p), per * p, nx).wait_recv()
            new = [cp(reg(jx, p, 0), per * p + 2, ny), cp(reg(jx, p), per * p + 4, sib)]
            cp(reg(jy, p), per * p + 1, ny).wait_recv()
            new += [cp(reg(jy, p, 1), per * p + 3, nx), cp(reg(jy, p), per * p + 5, sib)]
            for d in new:
                d.start()
            sent += new
        for p in range(pieces):
            cp(reg(jo, p, 0), per * p + 2, ny).wait_recv()
            cp(reg(jo, p, 1), per * p + 3, nx).wait_recv()
            new = [cp(reg(jo, p, 0), per * p + 6, sib), cp(reg(jo, p, 1), per * p + 7, sib)]
            for d in new:
                d.start()
            sent += new
        for p in range(pieces):
            cp(reg(jx, p, None, 1 - c), per * p + 4, sib).wait_recv()
            cp(reg(jy, p, None, 1 - c), per * p + 5, sib).wait_recv()
            cp(reg(jo, p, 0, 1 - c), per * p + 6, sib).wait_recv()
            cp(reg(jo, p, 1, 1 - c), per * p + 7, sib).wait_recv()
        for d in sent:
            d.wait_send()

    return _comm_call(body, name, [full], [_same(full)], per * pieces, aliases={0: 0})[0]


def _same(a):
    return jax.ShapeDtypeStruct(a.shape, a.dtype)


def _gather_ici(full, geo, p0, p1, pieces):
    def copies(ins, outs):
        x, y, c, _ = _place()
        mine = geo.piece(outs[0], 2 * x + y, c, p0, p1, pieces)
        return [(mine, mine, geo.piece(outs[0], 2 * ox + oy, c, p0, p1, pieces), (ox, oy, c))
                for ox, oy in ((1 - x, y), (x, 1 - y))]

    return dict(ins=[full], outs=[_same(full)], aliases={0: 0}, n=2, copies=copies)


def _gather_relay(full, geo, p0, p1, pieces):
    def copies(ins, outs):
        x, y, c, _ = _place()
        jx, jy, jo = 2 * (1 - x) + y, 2 * x + (1 - y), 2 * (1 - x) + (1 - y)
        reg = lambda chip, part: geo.piece(outs[0], chip, c, p0, p1, pieces, part)
        return [(reg(jx, 0), reg(jx, 0), reg(jo, 0), (x, 1 - y, c)), (reg(jy, 1), reg(jy, 1), reg(jo, 1), (1 - x, y, c))]

    return dict(ins=[full], outs=[_same(full)], aliases={0: 0}, n=2, copies=copies)


def _gather_d2d(full, geo):
    def copies(ins, outs):
        x, y, c, chips = _place()
        return [(geo.full_half(outs[0], 2 * ox + oy, c), geo.full_half(outs[0], 2 * ox + oy, c),
                 geo.full_half(outs[0], 2 * ox + oy, 1 - c), (x, y, 1 - c)) for ox, oy in chips]

    return dict(ins=[full], outs=[_same(full)], aliases={0: 0}, n=3, copies=copies)


def _swap_d2d(grad, geo):
    def copies(ins, outs):
        x, y, c, _ = _place()
        return [(geo.full_half(ins[0], j, 1 - c), outs[0].at[j], outs[0].at[j], (x, y, 1 - c)) for j in range(N_CHIPS)]

    return dict(ins=[grad], outs=[jax.ShapeDtypeStruct((N_CHIPS, geo.hr, geo.hc), BF16)], aliases={}, n=N_CHIPS,
                copies=copies)


SEM_SPEC = pl.BlockSpec(memory_space=pltpu.SEMAPHORE)
SIDE_EFFECT = pltpu.SideEffectType.DATAFLOW_SIDE_EFFECTING


def _fly_copies(swap, src_ref, land_ref, geo, sems):
    x, y, c, chips = _place()
    if swap:
        ends = [(geo.full_half(src_ref, j, 1 - c), land_ref.at[j], (x, y, 1 - c)) for j in range(N_CHIPS)]
    else:
        ends = [(src_ref.at[2 * ox + oy], land_ref.at[k], (ox, oy, c)) for k, (ox, oy) in enumerate(chips)]
    n = len(ends)
    return [pltpu.make_async_remote_copy(src_ref=src, dst_ref=dst, send_sem=sems[k], recv_sem=sems[n + k],
                                         device_id=peer, device_id_type=MESH) for k, (src, dst, peer) in enumerate(ends)]


def _fly_start(swap, src, geo, name, carry=None):
    n = N_CHIPS if swap else 3
    n_carry = 0 if carry is None else 1

    def body(src_ref, land_ref, *rest):
        sems, token = rest[n_carry:n_carry + 2 * n], rest[n_carry + 2 * n + 2]
        for copy in _fly_copies(swap, src_ref, land_ref, geo, sems):
            copy.start()
        token[...] = jnp.zeros_like(token)

    land = jax.ShapeDtypeStruct((n, geo.hr, geo.hc), BF16)
    out_shape = (pltpu.SemaphoreType.DMA(()),) * (2 * n) + (pltpu.HBM(src.shape, src.dtype), pltpu.HBM(land.shape, land.dtype),
                                                             jax.ShapeDtypeStruct((8, 128), F32))
    out_specs = (SEM_SPEC,) * (2 * n) + (HBM_SPEC, HBM_SPEC, VMEM_SPEC)
    args = [pltpu.with_memory_space_constraint(src, pltpu.HBM),
            pltpu.with_memory_space_constraint(lax.empty(land.shape, land.dtype), pltpu.HBM)]
    aliases = {0: 2 * n, 1: 2 * n + 1}
    if carry is not None:
        out_shape += (pltpu.HBM(carry.shape, carry.dtype),)
        out_specs += (HBM_SPEC,)
        args.append(pltpu.with_memory_space_constraint(carry, pltpu.HBM))
        aliases[2] = 2 * n + 3
    return pl.pallas_call(
        body, name=name, out_shape=out_shape, in_specs=(HBM_SPEC,) * len(args), out_specs=out_specs,
        input_output_aliases=aliases,
        compiler_params=pltpu.CompilerParams(has_side_effects=SIDE_EFFECT),
    )(*args)


def _fly_wait(swap, started, geo, after, name):
    n = N_CHIPS if swap else 3
    sems, src_thru, land_thru = started[:2 * n], started[2 * n], started[2 * n + 1]

    def body(src_ref, land_ref, *rest):
        for copy in _fly_copies(swap, src_ref, land_ref, geo, rest[:2 * n]):
            copy.wait_send()
            copy.wait_recv()

    return pl.pallas_call(
        body, name=name,
        out_shape=(pltpu.HBM(src_thru.shape, src_thru.dtype), pltpu.HBM(land_thru.shape, land_thru.dtype)),
        in_specs=(HBM_SPEC, HBM_SPEC) + (SEM_SPEC,) * (2 * n) + (pl.BlockSpec(memory_space=pl.ANY),),
        out_specs=(HBM_SPEC, HBM_SPEC), input_output_aliases={0: 0, 1: 1},
        compiler_params=pltpu.CompilerParams(has_side_effects=SIDE_EFFECT),
    )(src_thru, land_thru, *sems, after)


def _scatter_ici(pair, recv, geo, p0, p1, pieces):
    pr = geo.hr // pieces
    rows = pl.ds(p0 * pr, (p1 - p0) * pr)

    def copies(ins, outs):
        x, y, c, chips = _place()
        return [(ins[0].at[2 * ox + oy, rows, :], outs[0].at[k, rows, :], outs[0].at[k, rows, :], (ox, oy, c))
                for k, (ox, oy) in enumerate(chips)]

    out = jax.ShapeDtypeStruct((3, geo.hr, geo.hc), BF16)
    if recv is None:
        return dict(ins=[pair], outs=[out], aliases={}, n=3, copies=copies)
    return dict(ins=[pair, recv], outs=[out], aliases={1: 0}, n=3, copies=copies)


def _join_d2d(shard, geo, row0=0):
    def copies(ins, outs):
        x, y, c, _ = _place()
        mine = outs[0].at[pl.ds(pl.multiple_of(row0 + c * geo.hr, 8), geo.hr), :]
        other = outs[0].at[pl.ds(pl.multiple_of(row0 + (1 - c) * geo.hr, 8), geo.hr), :]
        return [(mine, mine, other, (x, y, 1 - c))]

    return dict(ins=[shard], outs=[_same(shard)], aliases={0: 0}, n=1, copies=copies)


def _add_pair(grad, got, geo, place, name):
    tr = _pick(geo.hr, ROW_TILES)
    nb = geo.hr // tr

    def body(place_ref, a_ref, b_ref, o_ref):
        o_ref[0] = (a_ref[...].astype(F32) + b_ref[0].astype(F32)).astype(BF16)

    if geo.kind == "col":
        own_map = lambda j, i, place_ref: (place_ref[1] * nb + i, j)
    else:
        own_map = lambda j, i, place_ref: ((2 * j + place_ref[1]) * nb + i, 0)
    spec = pl.BlockSpec((1, tr, geo.hc), lambda j, i, place_ref: (j, i, 0))
    return pl.pallas_call(
        body, name=name, out_shape=jax.ShapeDtypeStruct((N_CHIPS, geo.hr, geo.hc), BF16),
        grid_spec=pltpu.PrefetchScalarGridSpec(
            num_scalar_prefetch=1, grid=(N_CHIPS, nb),
            in_specs=[pl.BlockSpec((tr, geo.hc), own_map), spec], out_specs=spec),
        compiler_params=_params(2),
    )(place, grad, got)


def _add_four(pair, recv, geo, place, name, rows=None, row0=0, into=None):
    tr = _pick(geo.hr, ROW_TILES)
    nb = geo.hr // tr
    rows = 2 * geo.hr if rows is None else rows

    def body(place_ref, p_ref, r_ref, *rest):
        rest[-1][...] = ((p_ref[0].astype(F32) + r_ref[0].astype(F32)) + r_ref[1].astype(F32)) + r_ref[2].astype(F32)

    in_specs = [pl.BlockSpec((1, tr, geo.hc), lambda i, place_ref: (place_ref[0], i, 0)),
                pl.BlockSpec((3, tr, geo.hc), lambda i, place_ref: (0, i, 0))]
    args = [place, pair, recv]
    if into is not None:
        in_specs.append(pl.BlockSpec(memory_space=pl.ANY))
        args.append(into)
    return pl.pallas_call(
        body, name=name, out_shape=jax.ShapeDtypeStruct((rows, geo.hc), F32),
        grid_spec=pltpu.PrefetchScalarGridSpec(
            num_scalar_prefetch=1, grid=(nb,), in_specs=in_specs,
            out_specs=pl.BlockSpec((tr, geo.hc), lambda i, place_ref: (row0 // tr + place_ref[1] * nb + i, 0))),
        input_output_aliases={3: 0} if into is not None else {},
        compiler_params=_params(1),
    )(*args)


PIECES = (4, 1, 16, 8) + (1,) * IN_CHUNKS
SCHEDULE = {
    "proj_fwd": (("gather_ici", W_OUT, 0, 1), ("gather_ici", W_UP, 0, 6)),
    "hgrn_fwd": (("gather_relay", W_OUT, 0, 1), ("gather_relay", W_UP, 0, 6), ("gather_ici", W_UP, 6, 12)),
    "attn_fwd_d1": (("gather_relay", W_UP, 6, 12),),
    "attn_fwd_d4": (("gather_ici", W_UP, 12, 16), ("gather_d2d", W_OUT)),
    "attn_fwd_d16": (("gather_relay", W_UP, 12, 16), ("gather_ici", W_DOWN, 0, 2)),
    "mix_fwd": (("gather_d2d", W_UP), ("gather_ici", W_DOWN, 2, 4)),
    "up_fwd": (("gather_ici", W_DOWN, 4, 8), ("gather_relay", W_DOWN, 0, 4)),
    "conv_gate_fwd_a": (("gather_relay", W_DOWN, 4, 8),),
    "conv_gate_fwd_b": (("gather_d2d", W_DOWN),),
    "down_bwd_x": (("swap", W_DOWN),),
    "conv_gate_bwd": (("scatter", W_DOWN, 0, 4),),
    "up_bwd_w": (("scatter", W_DOWN, 4, 8),),
    "up_bwd_x": (("swap", W_UP),),
    "norm2_bwd": (("scatter", W_UP, 0, 1),),
    "mix_bwd_w": (("scatter", W_UP, 1, 2),),
    "mix_bwd_x": (("swap", W_OUT), ("scatter", W_UP, 2, 3)),
    "hgrn_bwd": (("scatter", W_UP, 3, 9), ("join", W_DOWN)),
    "attn_bwd_d1": (("scatter", W_UP, 9, 12),),
    "attn_bwd_d4": (("scatter", W_OUT, 0, 1),),
    "attn_bwd_d16": (("scatter", W_UP, 12, 16),),
    "proj_bwd_w_0": (("join", W_UP), ("join", W_OUT)),
    "gather_stats": (("join", W_INQ),),
    "grad_in_join": (("join", W_INQ + 1),),
}


class _Net:
    def __init__(self, shards, place):
        self.place = place
        self.geos = [_Geo(k, s.shape) for k, s in zip(WEIGHT_KINDS, shards)]
        self.full = [_cast_into_full(s, g, place, f"cast_shard_{w}") for w, (s, g) in enumerate(zip(shards, self.geos))]
        self.full[W_IN] = _gather_weight(self.full[W_IN], self.geos[W_IN], PIECES[W_IN], "gather_w_in")
        rows, cols = shards[W_IN].shape
        self.geos += [_Geo("col", (rows // IN_CHUNKS, cols))] * IN_CHUNKS
        n = NW + IN_CHUNKS
        self.grad, self.pair, self.recv, self.shard = [None] * n, [None] * n, [None] * n, [None] * n
        self.in_rows, self.in_shard = rows, None
        self.when_joined, self.joined = {}, {}
        self.flying = None
        self.carry = None
        self.swaps = [None] * IN_CHUNKS
        self.slots = []

    def _row0(self, w):
        return (w - W_INQ) * 2 * self.geos[w].hr

    def host(self, name):
        phase = _Phase()
        self.slots = []
        groups = {}
        for item in SCHEDULE.get(name, ()):
            kind, w = item[0], item[1]
            geo = self.geos[w]
            key = len(groups)
            if kind == "gather_ici":
                spec, key = _gather_ici(self.full[w], geo, item[2], item[3], PIECES[w]), ("full", w)
            elif kind == "gather_relay":
                spec, key = _gather_relay(self.full[w], geo, item[2], item[3], PIECES[w]), ("full", w)
            elif kind == "gather_d2d":
                spec, key = _gather_d2d(self.full[w], geo), ("full", w)
            elif kind == "swap":
                spec = _swap_d2d(self.grad[w], geo)
            elif kind == "scatter":
                spec, key = _scatter_ici(self.pair[w], self.recv[w], geo, item[2], item[3], PIECES[w]), ("recv", w)
            elif w >= W_INQ:
                spec, key = _join_d2d(self.in_shard, geo, self._row0(w)), "in_shard"
            else:
                spec = _join_d2d(self.shard[w], geo)
            groups.setdefault(key, []).append((item, spec))
        for group in groups.values():
            specs = [s for _, s in group]
            merged = dict(specs[0], n=sum(s["n"] for s in specs),
                          copies=lambda ins, outs, specs=specs: [t for s in specs for t in s["copies"](ins, outs)])
            self.slots.append(([item for item, _ in group], phase.add(**merged)))
        return phase

    def done(self, name, comm, result=None):
        for items, sl in sorted(self.slots, key=lambda s: s[0][0][0] == "scatter"):
            out = comm[sl][0]
            for item in items:
                kind, w = item[0], item[1]
                if kind in ("gather_ici", "gather_relay", "gather_d2d"):
                    self.full[w] = out
                elif kind == "swap":
                    self.pair[w] = _add_pair(self.grad[w], out, self.geos[w], self.place, f"grad_pair_sum_{w}")
                elif kind == "scatter":
                    self.recv[w] = out
                    if item[3] == PIECES[w] and w >= W_INQ:
                        self.in_shard = _add_four(self.pair[w], out, self.geos[w], self.place, f"grad_chip_sum_{w}",
                                                  rows=self.in_rows, row0=self._row0(w), into=self.in_shard)
                    elif item[3] == PIECES[w]:
                        self.shard[w] = _add_four(self.pair[w], out, self.geos[w], self.place, f"grad_chip_sum_{w}")
                elif w >= W_INQ:
                    self.in_shard = out
                else:
                    self.shard[w] = out
                    if w in self.when_joined:
                        self.joined[w] = self.when_joined[w](out)
        if name == "proj_bwd_x" and result is not None:
            self._land_swap(IN_CHUNKS - 1, result)
            pair, recv = _fly_wait(False, self.flying, self.geos[W_INQ], result, "grad_in_scatter0_wait")
            self.in_shard = _add_four(pair, recv, self.geos[W_INQ], self.place, f"grad_chip_sum_{W_INQ}",
                                      rows=self.in_rows, row0=0, into=self.in_shard)
            last = W_INQ + IN_CHUNKS - 1
            self.last_flying = _fly_start(False, self.pair[last], self.geos[last], "grad_in_scatter_start", result)
            self.handed = self.last_flying[-1]

    def _land_swap(self, q, after):
        w = W_INQ + q
        grad, got = _fly_wait(True, self.swaps[q], self.geos[w], after, f"grad_in_swap{q}_wait")
        self.pair[w] = _add_pair(grad, got, self.geos[w], self.place, f"grad_pair_sum_{w}")

    def after_chunk(self, q, grad):
        w = W_INQ + q
        self.grad[w] = grad
        if q > 0:
            self._land_swap(q - 1, grad)
        if q == 1:
            self.flying = _fly_start(False, self.pair[W_INQ], self.geos[W_INQ], "grad_in_scatter0_start", self.carry)
            self.carry = self.flying[-1]
        self.swaps[q] = _fly_start(True, grad, self.geos[w], f"grad_in_swap{q}_start", self.carry)
        self.carry = self.swaps[q][-1]

    def alone(self, name):
        self.done(name, _comm_only(name, self.host(name)))


CONVW_SHARD = 3 * DFF // N_CHIPS
COND_PAD = 8 * 896
SMALL_SIZES = (("norm1_w", D), ("lb", HW), ("hg_norm_w", DH), ("q_norm_w", DH), ("k_norm_w", DH),
               ("norm2_w", D), ("conv_b", DFF), ("conv_w", 3 * DFF), ("loss", 128))
SMALL_TOTAL = sum(n for _, n in SMALL_SIZES)
STATS_PAD = 8 * 5120


def kernel(x, c, w_ada, b_ada, norm1_w, w_in, lb_logits, hg_norm_w, q_norm_w, k_norm_w, w_out, norm2_w, w_up, conv_w, conv_b, w_down, loss_target, m_w_ada, m_b_ada, m_norm1_w, m_w_in, m_lb_logits, m_hg_norm_w, m_q_norm_w, m_k_norm_w, m_w_out, m_norm2_w, m_w_up, m_conv_w, m_conv_b, m_w_down, v_w_ada, v_b_ada, v_norm1_w, v_w_in, v_lb_logits, v_hg_norm_w, v_q_norm_w, v_k_norm_w, v_w_out, v_norm2_w, v_w_up, v_conv_w, v_conv_b, v_w_down):
    chip = 2 * lax.axis_index("x") + lax.axis_index("y")
    dev = 2 * chip + lax.axis_index("c")

    cond = jnp.concatenate([c, conv_w[0].reshape(1, CONVW_SHARD), jnp.zeros((1, COND_PAD - D - CONVW_SHARD), F32)], axis=1)
    cond_all = _all_gather_rows(cond.reshape(8, COND_PAD // 8), "gather_cond").reshape(N_DEV, COND_PAD)
    c_all = cond_all[:, :D]
    conv_w_full = jnp.concatenate(
        [cond_all[2 * j, D:D + CONVW_SHARD].reshape(3, DFF // N_CHIPS) for j in range(N_CHIPS)], axis=1)

    b_shard = lax.dynamic_slice_in_dim(b_ada, chip * ADA_COLS, ADA_COLS, axis=1)
    mod_cols = _all_gather_rows(_ada_fwd(c_all, w_ada[0], b_shard), "gather_mod")
    mod_all = jnp.concatenate([mod_cols[16 * j:16 * j + 8] for j in range(N_CHIPS)], axis=1)
    mod = lax.dynamic_slice_in_dim(mod_all, dev, 1, axis=0)

    place = jnp.stack([chip, lax.axis_index("c")]).astype(jnp.int32)
    net = _Net([w_in[0], w_out[0], w_up[0], w_down[0]], place)
    net.when_joined = {
        W_OUT: lambda grad: _adamw_sc(w_out[0], grad, m_w_out[0], v_w_out[0], "adamw_sc_w_out"),
        W_DOWN: lambda grad: _adamw_sc(w_down[0], grad, m_w_down[0], v_w_down[0], "adamw_sc_w_down"),
        W_UP: lambda grad: _adamw_sc(w_up[0], grad, m_w_up[0], v_w_up[0], "adamw_sc_w_up"),
    }
    early = {W_OUT: "w_out", W_DOWN: "w_down", W_UP: "w_up"}
    loss_row, grad_x, dmod, small = _sequence_step(
        x[0], loss_target[0], mod, norm1_w, lb_logits, hg_norm_w, q_norm_w, k_norm_w, norm2_w, conv_w_full, conv_b, net)
    small["conv_w"] = small["conv_w"].reshape(1, 3 * DFF)
    small["loss"] = loss_row
    stats = jnp.concatenate([dmod] + [small[k] for k, _ in SMALL_SIZES]
                            + [jnp.zeros((1, STATS_PAD - 6 * D - SMALL_TOTAL), F32)], axis=1)
    stats_all, comm = _all_gather_rows(stats.reshape(8, STATS_PAD // 8), "gather_stats", net.host("gather_stats"))
    net.done("gather_stats", comm)
    stats_all = stats_all.reshape(N_DEV, STATS_PAD)
    last = W_INQ + IN_CHUNKS - 1
    started = net.last_flying
    dmod_all = stats_all[:, :6 * D]
    sums = _sum_rows(stats_all[:, 6 * D:6 * D + SMALL_TOTAL], "small_grad_sum")
    g_small, off = {}, 0
    for k, n in SMALL_SIZES:
        g_small[k] = sums[:, off:off + n]
        off += n
    loss = g_small["loss"][0, 0]
    g_b_ada = _sum_rows(dmod_all, "b_ada_grad_sum")
    ada = _ada_bwd_adamw(c_all, lax.dynamic_slice_in_dim(dmod_all, chip * ADA_COLS, ADA_COLS, axis=1),
                         w_ada[0], m_w_ada[0], v_w_ada[0])
    g_w_ada = ada[0]
    pair_last, recv_last = _fly_wait(False, started, net.geos[last], ada[3], "grad_in_scatter_wait")
    net.in_shard = _add_four(pair_last, recv_last, net.geos[last], place, f"grad_chip_sum_{last}",
                             rows=net.in_rows, row0=net._row0(last), into=net.in_shard)
    net.alone("grad_in_join")
    g_out, g_up, g_down = net.shard[W_OUT], net.shard[W_UP], net.shard[W_DOWN]
    g_in = net.in_shard
    g_lb = _lb_grad(lb_logits, g_small["lb"])
    g_conv_w = lax.dynamic_slice_in_dim(g_small["conv_w"].reshape(3, DFF), chip * (DFF // N_CHIPS), DFF // N_CHIPS, axis=1)

    names = ["w_ada", "b_ada", "norm1_w", "w_in", "lb_logits", "hg_norm_w", "q_norm_w", "k_norm_w", "w_out",
             "norm2_w", "w_up", "conv_w", "conv_b", "w_down"]
    lead = {"w_ada", "w_in", "w_out", "w_up", "conv_w", "w_down"}
    w = dict(w_ada=w_ada, b_ada=b_ada, norm1_w=norm1_w, w_in=w_in, lb_logits=lb_logits, hg_norm_w=hg_norm_w,
             q_norm_w=q_norm_w, k_norm_w=k_norm_w, w_out=w_out, norm2_w=norm2_w, w_up=w_up, conv_w=conv_w,
             conv_b=conv_b, w_down=w_down)
    m = dict(w_ada=m_w_ada, b_ada=m_b_ada, norm1_w=m_norm1_w, w_in=m_w_in, lb_logits=m_lb_logits,
             hg_norm_w=m_hg_norm_w, q_norm_w=m_q_norm_w, k_norm_w=m_k_norm_w, w_out=m_w_out, norm2_w=m_norm2_w,
             w_up=m_w_up, conv_w=m_conv_w, conv_b=m_conv_b, w_down=m_w_down)
    v = dict(w_ada=v_w_ada, b_ada=v_b_ada, norm1_w=v_norm1_w, w_in=v_w_in, lb_logits=v_lb_logits,
             hg_norm_w=v_hg_norm_w, q_norm_w=v_q_norm_w, k_norm_w=v_k_norm_w, w_out=v_w_out, norm2_w=v_norm2_w,
             w_up=v_w_up, conv_w=v_conv_w, conv_b=v_conv_b, w_down=v_w_down)
    g = dict(w_ada=g_w_ada, b_ada=g_b_ada, norm1_w=g_small["norm1_w"], w_in=g_in, lb_logits=g_lb,
             hg_norm_w=g_small["hg_norm_w"], q_norm_w=g_small["q_norm_w"], k_norm_w=g_small["k_norm_w"], w_out=g_out,
             norm2_w=g_small["norm2_w"], w_up=g_up, conv_w=g_conv_w, conv_b=g_small["conv_b"], w_down=g_down)

    updated = {early[i]: res for i, res in net.joined.items()}
    updated["w_ada"] = (ada[1], ada[2], ada[3], ada[0])
    grads, deltas, new_m, new_v = [], [], [], []
    for n in names:
        strip = (lambda t: t[0]) if n in lead else (lambda t: t)
        wrap = (lambda t: t[None]) if n in lead else (lambda t: t)
        g_n = g[n]
        if n in updated:
            d_n, m_n, v_n, g_n = updated[n]
        elif n == "w_in":
            d_n, m_n, v_n, g_n = _adamw(strip(w[n]), g_n, strip(m[n]), strip(v[n]), f"adamw_{n}", copy_grad=True)
        else:
            d_n, m_n, v_n = _adamw(strip(w[n]), g_n, strip(m[n]), strip(v[n]), f"adamw_{n}")
        grads.append(wrap(g_n))
        deltas.append(wrap(d_n))
        new_m.append(wrap(m_n))
        new_v.append(wrap(v_n))
    return (loss, grad_x[None], *grads, *deltas, *new_m, *new_v)
```

```python
import functools

import jax
import jax.numpy as jnp
from jax import lax
from jax.experimental import pallas as pl
from jax.experimental.pallas import tpu as pltpu
from jax.experimental.pallas import tpu_sc as plsc

F32 = jnp.float32
BF16 = jnp.bfloat16

S = 2048
D = 2048
H = 8
DH = 128
HW = H * DH
CH = 64
NCH = S // CH
DFF = 5632
INC = 7 * HW
BLK = 128
DILS = (1, 4, 16)
EPS = 1e-6
NEG = -1e30
N_CHIPS = 4
N_DEV = 8

ADAM_LR = 0.001
ADAM_B1 = 0.9
ADAM_B2 = 0.999
ADAM_EPS = 1e-08
ADAM_WD = 0.01
ADAM_STEP = 10
ADAM_BLOCK_BYTES = 1 << 21

V7X_VMEM_BYTES = 64 * 1024 * 1024
VMEM_LIMIT = (V7X_VMEM_BYTES * 3) // 4
MESH = pl.DeviceIdType.MESH
HBM_SPEC = pl.BlockSpec(memory_space=pltpu.HBM)
VMEM_SPEC = pl.BlockSpec(memory_space=pltpu.VMEM)

NN = (((1,), (0,)), ((), ()))
NT = (((1,), (1,)), ((), ()))
TN = (((0,), (0,)), ((), ()))


def _params(n_grid):
    return pltpu.CompilerParams(dimension_semantics=("arbitrary",) * n_grid, vmem_limit_bytes=VMEM_LIMIT)


def _dot(a, b, dims=NN):
    return lax.dot_general(a.astype(BF16), b.astype(BF16), dims, preferred_element_type=F32)


def _dot_f32(a, b):
    return lax.dot_general(a, b, NN, precision=lax.Precision.HIGHEST, preferred_element_type=F32)


def _sigmoid(x):
    return 1.0 / (1.0 + jnp.exp(-x))


def _pick(n, cands):
    for t in cands:
        if n % t == 0:
            return t
    raise ValueError(n)


def _matmul(a, b, mode, out_dtype, name, phase=None, m_blocks=None):
    if mode == "nn":
        (m, k), (_, n) = a.shape, b.shape
    elif mode == "nt":
        (m, k), (n, _) = a.shape, b.shape
    else:
        (k, m), (_, n) = a.shape, b.shape
    tm = _pick(m, (1024, 1408, 512))
    a_block = lambda i: i
    if m_blocks is not None:
        tm, blocks = m_blocks
        m = tm * len(blocks)
        step = blocks[1] - blocks[0] if len(blocks) > 1 else 0
        assert all(blk == blocks[0] + step * i for i, blk in enumerate(blocks))
        a_block = lambda i: blocks[0] + step * i
    tn = _pick(n, (1024, 1408, 512))
    tk = _pick(k, (2048, 2816, 1792, 1024, 512))
    nk = k // tk
    dims = {"nn": NN, "nt": NT, "tn": TN}[mode]

    def body(a_ref, b_ref, o_ref, *acc):
        part = lax.dot_general(a_ref[...], b_ref[...], dims, preferred_element_type=F32)
        if nk == 1:
            o_ref[...] = part.astype(o_ref.dtype)
            return
        acc_ref, kk = acc[0], pl.program_id(2)

        @pl.when(kk == 0)
        def _():
            acc_ref[...] = part

        @pl.when(jnp.logical_and(kk > 0, kk < nk - 1))
        def _():
            acc_ref[...] += part

        @pl.when(kk == nk - 1)
        def _():
            o_ref[...] = (acc_ref[...] + part).astype(o_ref.dtype)

    if mode == "tn":
        a_spec = pl.BlockSpec((tk, tm), lambda i, j, kk: (kk, a_block(i)))
    else:
        a_spec = pl.BlockSpec((tm, tk), lambda i, j, kk: (i, kk))
    if mode == "nt":
        b_spec = pl.BlockSpec((tn, tk), lambda i, j, kk: (j, kk))
    else:
        b_spec = pl.BlockSpec((tk, tn), lambda i, j, kk: (kk, j))
    return _pcall(
        body, [a, b], phase, name=name,
        out_shape=jax.ShapeDtypeStruct((m, n), out_dtype),
        grid=(m // tm, n // tn, nk),
        in_specs=[a_spec, b_spec],
        out_specs=pl.BlockSpec((tm, tn), lambda i, j, kk: (i, j)),
        scratch_shapes=[pltpu.VMEM((tm, tn), F32)] if nk > 1 else [],
        compiler_params=_params(3),
    )


TR = 256


def _row_spec(cols=D):
    return pl.BlockSpec((TR, cols), lambda i: (i, 0))


def _vec_spec(cols=D):
    return pl.BlockSpec((1, cols), lambda i: (0, 0))


def _norm_mod(x, nw, scale, shift, name):
    def body(x_ref, nw_ref, sc_ref, sh_ref, h_ref):
        xv = x_ref[...]
        r = lax.rsqrt(jnp.mean(xv * xv, axis=-1, keepdims=True) + EPS)
        h_ref[...] = ((xv * r) * nw_ref[...] * (1.0 + sc_ref[...]) + sh_ref[...]).astype(BF16)

    return pl.pallas_call(
        body, name=name, out_shape=jax.ShapeDtypeStruct((S, D), BF16), grid=(S // TR,),
        in_specs=[_row_spec(), _vec_spec(), _vec_spec(), _vec_spec()], out_specs=_row_spec(),
        compiler_params=_params(1),
    )(x, nw, scale, shift)


def _resid_norm_mod(x, mix, gate, nw, scale, shift, name):
    def body(x_ref, mix_ref, g_ref, nw_ref, sc_ref, sh_ref, x1_ref, h_ref):
        xv = x_ref[...] + g_ref[...] * mix_ref[...]
        x1_ref[...] = xv
        r = lax.rsqrt(jnp.mean(xv * xv, axis=-1, keepdims=True) + EPS)
        h_ref[...] = ((xv * r) * nw_ref[...] * (1.0 + sc_ref[...]) + sh_ref[...]).astype(BF16)

    return pl.pallas_call(
        body, name=name,
        out_shape=(jax.ShapeDtypeStruct((S, D), F32), jax.ShapeDtypeStruct((S, D), BF16)), grid=(S // TR,),
        in_specs=[_row_spec(), _row_spec(), _vec_spec(), _vec_spec(), _vec_spec(), _vec_spec()],
        out_specs=(_row_spec(), _row_spec()),
        compiler_params=_params(1),
    )(x, mix, gate, nw, scale, shift)


def _norm_mod_bwd(dh, xin, nw, scale, dres, name, mix=None, gate=None, phase=None):
    with_gate = mix is not None

    def body(*refs):
        if with_gate:
            dh_ref, x_ref, nw_ref, sc_ref, dres_ref, mix_ref, g_ref, dx_ref, dsh_ref, dsc_ref, dnw_ref, dg_ref, dmix_ref = refs
        else:
            dh_ref, x_ref, nw_ref, sc_ref, dres_ref, dx_ref, dsh_ref, dsc_ref, dnw_ref = refs
        i = pl.program_id(0)
        dhv = dh_ref[...]
        xv = x_ref[...]
        r = lax.rsqrt(jnp.mean(xv * xv, axis=-1, keepdims=True) + EPS)
        xn = xv * r
        nwv = nw_ref[...]
        one_sc = 1.0 + sc_ref[...]
        dxn = dhv * nwv * one_sc
        dx = dres_ref[...] + r * (dxn - xn * jnp.mean(dxn * xn, axis=-1, keepdims=True))
        dx_ref[...] = dx

        @pl.when(i == 0)
        def _():
            dsh_ref[...] = jnp.zeros_like(dsh_ref)
            dsc_ref[...] = jnp.zeros_like(dsc_ref)
            dnw_ref[...] = jnp.zeros_like(dnw_ref)
            if with_gate:
                dg_ref[...] = jnp.zeros_like(dg_ref)

        dsh_ref[...] += jnp.sum(dhv, axis=0, keepdims=True)
        dsc_ref[...] += jnp.sum(dhv * xn * nwv, axis=0, keepdims=True)
        dnw_ref[...] += jnp.sum(dhv * xn * one_sc, axis=0, keepdims=True)
        if with_gate:
            dg_ref[...] += jnp.sum(dx * mix_ref[...], axis=0, keepdims=True)
            dmix_ref[...] = (dx * g_ref[...]).astype(BF16)

    vec = jax.ShapeDtypeStruct((1, D), F32)
    ins = [dh, xin, nw, scale, dres]
    in_specs = [_row_spec(), _row_spec(), _vec_spec(), _vec_spec(), _row_spec()]
    outs = [jax.ShapeDtypeStruct((S, D), F32), vec, vec, vec]
    out_specs = [_row_spec(), _vec_spec(), _vec_spec(), _vec_spec()]
    if with_gate:
        ins += [mix, gate]
        in_specs += [_row_spec(), _vec_spec()]
        outs += [vec, jax.ShapeDtypeStruct((S, D), BF16)]
        out_specs += [_vec_spec(), _row_spec()]
    return _pcall(
        body, ins, phase, name=name, out_shape=tuple(outs), grid=(S // TR,), in_specs=in_specs,
        out_specs=tuple(out_specs), compiler_params=_params(1),
    )


def _tri(lower):
    row = lax.broadcasted_iota(jnp.int32, (CH, CH), 0)
    col = lax.broadcasted_iota(jnp.int32, (CH, CH), 1)
    return (row >= col) if lower else (col >= row)


def _hgrn_gates(hq, hf, lb):
    sig = _sigmoid(hf)
    f = lb + (1.0 - lb) * sig
    g = jnp.log(f)
    sq = _sigmoid(hq)
    return sig, f, g, 1.0 - f, hq * sq, sq


HG_HP = 2
HG_UNROLL = 8


def _proj_col_spec(group):
    return pl.BlockSpec((S, HG_HP * DH), lambda h: (0, group * (H // HG_HP) + h))


def _hgrn_fwd(proj, lb_logits, norm_w, phase=None):
    def body(hq_ref, hf_ref, hi_ref, hg_ref, lbl_ref, nw_ref, out_ref, oraw_ref, st_ref, s_ref):
        nw = nw_ref[...]
        lower = _tri(True)
        ltri = lower.astype(F32)
        s_ref[...] = jnp.zeros_like(s_ref)

        def chunk(n, carry):
            rows = pl.ds(pl.multiple_of(n * CH, CH), CH)
            for j in range(HG_HP):
                cols = slice(j * DH, (j + 1) * DH)
                lb = 1.0 / (1.0 + jnp.exp(lbl_ref[1:2, cols] - lbl_ref[0:1, cols]))
                hq, hf, v, hg = hq_ref[rows, cols], hf_ref[rows, cols], hi_ref[rows, cols], hg_ref[rows, cols]
                _, _, g, kk, q, _ = _hgrn_gates(hq, hf, lb)
                gc = _dot_f32(ltri, g)
                gl = jnp.sum(g, axis=0, keepdims=True)
                qe = q * jnp.exp(gc)
                ke = kk * jnp.exp(gl - gc)
                qh = q * jnp.exp(gc - 0.5 * gl)
                kh = kk * jnp.exp(0.5 * gl - gc)
                st = s_ref[j]
                st_ref[j, pl.ds(n, 1)] = st[None]
                a = jnp.where(lower, _dot(qh, kh, NT), 0.0)
                o = _dot(qe, st, NT) + _dot(a, v)
                s_ref[j] = st * jnp.exp(gl) + _dot(v, ke, TN)
                oraw_ref[rows, cols] = o
                rs = lax.rsqrt(jnp.mean(o * o, axis=-1, keepdims=True) + EPS)
                out_ref[rows, cols] = (o * rs * nw * (hg * _sigmoid(hg))).astype(BF16)
            return carry

        lax.fori_loop(0, NCH, chunk, 0, unroll=HG_UNROLL)

    head_spec = pl.BlockSpec((S, HG_HP * DH), lambda h: (0, h))
    return _pcall(
        body, [proj, proj, proj, proj, lb_logits, norm_w], phase, name="hgrn_fwd",
        out_shape=(jax.ShapeDtypeStruct((S, 2 * HW), BF16), jax.ShapeDtypeStruct((S, HW), F32),
                   jax.ShapeDtypeStruct((H, NCH, DH, DH), F32)),
        grid=(H // HG_HP,),
        in_specs=[_proj_col_spec(0), _proj_col_spec(1), _proj_col_spec(2), _proj_col_spec(3),
                  pl.BlockSpec((2, HG_HP * DH), lambda h: (0, h)), pl.BlockSpec((1, DH), lambda h: (0, 0))],
        out_specs=(head_spec, head_spec, pl.BlockSpec((HG_HP, NCH, DH, DH), lambda h: (h, 0, 0, 0))),
        scratch_shapes=[pltpu.VMEM((HG_HP, DH, DH), F32)],
        compiler_params=_params(1),
    )


def _hgrn_bwd(proj, lb_logits, norm_w, oraw, states, dout, phase=None):
    def body(hq_ref, hf_ref, hi_ref, hg_ref, lbl_ref, nw_ref, oraw_ref, st_ref, do_ref,
             dhq_ref, dhf_ref, dhi_ref, dhg_ref, dlb_ref, dnw_ref, ds_ref, acc_ref):
        h = pl.program_id(0)
        nw = nw_ref[...]
        lower = _tri(True)
        ltri = lower.astype(F32)
        utri = _tri(False).astype(F32)
        last = lax.broadcasted_iota(jnp.int32, (CH, DH), 0) == CH - 1
        ds_ref[...] = jnp.zeros_like(ds_ref)
        acc_ref[...] = jnp.zeros_like(acc_ref)

        @pl.when(h == 0)
        def _():
            dnw_ref[...] = jnp.zeros_like(dnw_ref)

        def chunk(i, carry):
            n = NCH - 1 - i
            rows = pl.ds(pl.multiple_of(n * CH, CH), CH)
            for j in range(HG_HP):
                cols = slice(j * DH, (j + 1) * DH)
                lb = 1.0 / (1.0 + jnp.exp(lbl_ref[1:2, cols] - lbl_ref[0:1, cols]))
                hq, hf, v, hg = hq_ref[rows, cols], hf_ref[rows, cols], hi_ref[rows, cols], hg_ref[rows, cols]
                sig, f, g, kk, q, sq = _hgrn_gates(hq, hf, lb)
                gc = _dot_f32(ltri, g)
                gl = jnp.sum(g, axis=0, keepdims=True)
                eg = jnp.exp(gc)
                ek = jnp.exp(gl - gc)
                gam = jnp.exp(gl)
                ph = jnp.exp(gc - 0.5 * gl)
                pk = jnp.exp(0.5 * gl - gc)
                qe, ke = q * eg, kk * ek
                qh, kh = q * ph, kk * pk
                st = st_ref[j, pl.ds(n, 1)][0]
                dst = ds_ref[j]
                a = jnp.where(lower, _dot(qh, kh, NT), 0.0)
                o = oraw_ref[rows, cols]
                rs = lax.rsqrt(jnp.mean(o * o, axis=-1, keepdims=True) + EPS)
                xh = o * rs
                sg = _sigmoid(hg)
                dgo = do_ref[rows, cols]
                d_on = dgo * (hg * sg)
                dhg_ref[rows, cols] = (dgo * xh * nw * (sg * (1.0 + hg * (1.0 - sg)))).astype(BF16)
                acc_ref[j, 1:2, :] += jnp.sum(d_on * xh, axis=0, keepdims=True)
                dy = d_on * nw
                do = rs * (dy - xh * jnp.mean(dy * xh, axis=-1, keepdims=True))
                dqe = _dot(do, st)
                da = jnp.where(lower, _dot(do, v, NT), 0.0)
                dv = _dot(a, do, TN) + _dot(ke, dst, NT)
                dke = _dot(v, dst)
                dgam = jnp.sum(dst * st, axis=0, keepdims=True)
                dqh = lax.dot_general(da, kh, NN, precision=lax.Precision.HIGH, preferred_element_type=F32)
                dkh = lax.dot_general(da, qh, TN, precision=lax.Precision.HIGH, preferred_element_type=F32)
                dq = dqh * ph + dqe * eg
                dk = dkh * pk + dke * ek
                dgl = jnp.sum(dke * ke, axis=0, keepdims=True) + dgam * gam
                dgc = dqh * qh - dkh * kh + dqe * qe - dke * ke + jnp.where(last, dgl, 0.0)
                dg = _dot_f32(utri, dgc)
                df = dg / f - dk
                dhf_ref[rows, cols] = (df * (1.0 - lb) * sig * (1.0 - sig)).astype(BF16)
                acc_ref[j, 0:1, :] += jnp.sum(df * (1.0 - sig), axis=0, keepdims=True)
                dhq_ref[rows, cols] = (dq * (sq * (1.0 + hq * (1.0 - sq)))).astype(BF16)
                dhi_ref[rows, cols] = dv.astype(BF16)
                ds_ref[j] = dst * gam + _dot(do, qe, TN)
            return carry

        lax.fori_loop(0, NCH, chunk, 0, unroll=HG_UNROLL)
        for j in range(HG_HP):
            dlb_ref[:, j * DH:(j + 1) * DH] = acc_ref[j, 0:1, :]
            dnw_ref[...] += acc_ref[j, 1:2, :]

    head_spec = pl.BlockSpec((S, HG_HP * DH), lambda h: (0, h))
    head_out = jax.ShapeDtypeStruct((S, HW), BF16)
    return _pcall(
        body, [proj, proj, proj, proj, lb_logits, norm_w, oraw, states, dout], phase, name="hgrn_bwd",
        out_shape=(head_out, head_out, head_out, head_out,
                   jax.ShapeDtypeStruct((1, HW), F32), jax.ShapeDtypeStruct((1, DH), F32)),
        grid=(H // HG_HP,),
        in_specs=[_proj_col_spec(0), _proj_col_spec(1), _proj_col_spec(2), _proj_col_spec(3),
                  pl.BlockSpec((2, HG_HP * DH), lambda h: (0, h)), pl.BlockSpec((1, DH), lambda h: (0, 0)),
                  head_spec, pl.BlockSpec((HG_HP, NCH, DH, DH), lambda h: (h, 0, 0, 0)), head_spec],
        out_specs=(head_spec, head_spec, head_spec, head_spec,
                   pl.BlockSpec((1, HG_HP * DH), lambda h: (0, h)), pl.BlockSpec((1, DH), lambda h: (0, 0))),
        scratch_shapes=[pltpu.VMEM((HG_HP, DH, DH), F32), pltpu.VMEM((HG_HP, 8, DH), F32)],
        compiler_params=_params(1),
    )


ATT_SCALE = DH ** -0.5
HEADS_PER_STEP = {1: 8, 4: 1, 16: 1}


def _sub_rows(ref, r, dil, cols):
    if dil == 1:
        return ref[:, cols]
    return ref[pl.ds(r, BLK, stride=dil), cols]


def _set_sub_rows(ref, r, dil, cols, val):
    if dil == 1:
        ref[:, cols] = val
    else:
        ref[pl.ds(r, BLK, stride=dil), cols] = val


def _slopes(dil):
    hp = HEADS_PER_STEP[dil]
    s = jnp.asarray([dil * 2.0 ** (-(h + 1)) for h in range(H)], F32)
    return jnp.broadcast_to(s[:, None], (H, DH)).reshape(H // hp, hp, DH)


def _rms_rows(x):
    rs = lax.rsqrt(jnp.mean(x * x, axis=-1, keepdims=True) + EPS)
    return x * rs, rs


def _att_masks():
    qi = lax.broadcasted_iota(jnp.int32, (BLK, BLK), 0)
    kj = lax.broadcasted_iota(jnp.int32, (BLK, BLK), 1)
    steps_c = qi - kj
    steps_p = qi - kj + BLK
    return steps_c, steps_p, steps_c >= 0, steps_p <= BLK


def _qk_prep(proj, q_w, k_w):
    def body(q_ref, k_ref, qw_ref, kw_ref, qn_ref, kn_ref):
        for h in range(H):
            cols = slice(h * DH, (h + 1) * DH)
            qn_ref[:, cols] = _rms_rows(q_ref[:, cols])[0] * qw_ref[...]
            kn_ref[:, cols] = _rms_rows(k_ref[:, cols])[0] * kw_ref[...]

    out = jax.ShapeDtypeStruct((S, HW), F32)
    wspec = pl.BlockSpec((1, DH), lambda i: (0, 0))
    return pl.pallas_call(
        body, name="qk_prep", out_shape=(out, out), grid=(S // TR,),
        in_specs=[pl.BlockSpec((TR, HW), lambda i: (i, 4)), pl.BlockSpec((TR, HW), lambda i: (i, 5)), wspec, wspec],
        out_specs=(_row_spec(HW), _row_spec(HW)),
        compiler_params=_params(1),
    )(proj, proj, q_w, k_w)


def _qk_norm_bwd(proj, dqn, dkn, dv, q_w, k_w):
    def body(q_ref, k_ref, dqn_ref, dkn_ref, dv_ref, qw_ref, kw_ref, dq_ref, dk_ref, dvo_ref, dqw_ref, dkw_ref):
        i = pl.program_id(0)

        @pl.when(i == 0)
        def _():
            dqw_ref[...] = jnp.zeros_like(dqw_ref)
            dkw_ref[...] = jnp.zeros_like(dkw_ref)

        dvo_ref[...] = dv_ref[...].astype(BF16)
        for x_ref, d_ref, w_ref, o_ref, dw_ref in ((q_ref, dqn_ref, qw_ref, dq_ref, dqw_ref),
                                                   (k_ref, dkn_ref, kw_ref, dk_ref, dkw_ref)):
            for h in range(H):
                cols = slice(h * DH, (h + 1) * DH)
                xh, rs = _rms_rows(x_ref[:, cols])
                d = d_ref[:, cols]
                dw_ref[...] += jnp.sum(d * xh, axis=0, keepdims=True)
                dy = d * w_ref[...]
                o_ref[:, cols] = (rs * (dy - xh * jnp.mean(dy * xh, axis=-1, keepdims=True))).astype(BF16)

    big = jax.ShapeDtypeStruct((S, HW), BF16)
    small = jax.ShapeDtypeStruct((1, DH), F32)
    wspec = pl.BlockSpec((1, DH), lambda i: (0, 0))
    return pl.pallas_call(
        body, name="qk_norm_bwd", out_shape=(big, big, big, small, small), grid=(S // TR,),
        in_specs=[pl.BlockSpec((TR, HW), lambda i: (i, 4)), pl.BlockSpec((TR, HW), lambda i: (i, 5)),
                  _row_spec(HW), _row_spec(HW), _row_spec(HW), wspec, wspec],
        out_specs=(_row_spec(HW), _row_spec(HW), _row_spec(HW), wspec, wspec),
        compiler_params=_params(1),
    )(proj, proj, dqn, dkn, dv, q_w, k_w)


def _attn_delta(do, o):
    def body(do_ref, o_ref, d_ref):
        for h in range(H):
            cols = slice(h * DH, (h + 1) * DH)
            d_ref[:, cols] = jnp.broadcast_to(jnp.sum(do_ref[:, cols] * o_ref[:, cols], axis=-1, keepdims=True), (TR, DH))

    return pl.pallas_call(
        body, name="attn_delta", out_shape=jax.ShapeDtypeStruct((S, HW), F32), grid=(S // TR,),
        in_specs=[pl.BlockSpec((TR, HW), lambda i: (i, 1)), _row_spec(HW)], out_specs=_row_spec(HW),
        compiler_params=_params(1),
    )(do, o)


def _attn_fwd(proj, qn, kn, dil, phase=None):
    hp = HEADS_PER_STEP[dil]
    rows, ng, nb = BLK * dil, H // hp, S // (BLK * dil)

    def body(q_ref, kc_ref, kp_ref, vc_ref, vp_ref, sl_ref, o_ref, lse_ref):
        n = pl.program_id(0)
        steps_c, steps_p, ok_c, ok_p = _att_masks()
        ok_p = jnp.logical_and(ok_p, n > 0)
        fc, fp = steps_c.astype(F32), steps_p.astype(F32)
        for hh in range(hp):
            cols = slice(hh * DH, (hh + 1) * DH)
            sl = sl_ref[0, hh:hh + 1, :]
            for r in range(dil):
                sub = lambda ref: _sub_rows(ref, r, dil, cols)
                qv = sub(q_ref)
                sc = jnp.where(ok_c, _dot(qv, sub(kc_ref), NT) * ATT_SCALE - sl * fc, NEG)
                sp = jnp.where(ok_p, _dot(qv, sub(kp_ref), NT) * ATT_SCALE - sl * fp, NEG)
                m = jnp.maximum(jnp.max(sc, axis=-1, keepdims=True), jnp.max(sp, axis=-1, keepdims=True))
                pc, pp = jnp.exp(sc - m), jnp.exp(sp - m)
                den = jnp.sum(pc, axis=-1, keepdims=True) + jnp.sum(pp, axis=-1, keepdims=True)
                o = (_dot(pc, sub(vc_ref)) + _dot(pp, sub(vp_ref))) / den
                _set_sub_rows(o_ref, r, dil, cols, o)
                _set_sub_rows(lse_ref, r, dil, cols, jnp.broadcast_to(m + jnp.log(den), (BLK, DH)))

    cur = pl.BlockSpec((rows, hp * DH), lambda n, g: (n, g))
    prev = pl.BlockSpec((rows, hp * DH), lambda n, g: (jnp.maximum(n - 1, 0), g))
    vcur = pl.BlockSpec((rows, hp * DH), lambda n, g: (n, 6 * ng + g))
    vprev = pl.BlockSpec((rows, hp * DH), lambda n, g: (jnp.maximum(n - 1, 0), 6 * ng + g))
    out = jax.ShapeDtypeStruct((S, HW), F32)
    return _pcall(
        body, [qn, kn, kn, proj, proj, _slopes(dil)], phase,
        name=f"attn_fwd_d{dil}", out_shape=(out, out), grid=(nb, ng),
        in_specs=[cur, cur, prev, vcur, vprev, pl.BlockSpec((1, hp, DH), lambda n, g: (g, 0, 0))],
        out_specs=(cur, cur),
        compiler_params=_params(2),
    )


def _attn_merge(outs, lses, cat):
    def body(o1, o2, o3, l1, l2, l3, cat_ref, ob_ref, of_ref, lse_ref):
        a, b, c = l1[...], l2[...], l3[...]
        m = jnp.maximum(jnp.maximum(a, b), c)
        ea, eb, ec = jnp.exp(a - m), jnp.exp(b - m), jnp.exp(c - m)
        den = ea + eb + ec
        o = (ea * o1[...] + eb * o2[...] + ec * o3[...]) / den
        ob_ref[...] = o.astype(BF16)
        of_ref[...] = o
        lse_ref[...] = m + jnp.log(den)

    f = jax.ShapeDtypeStruct((S, HW), F32)
    return pl.pallas_call(
        body, name="attn_merge", out_shape=(jax.ShapeDtypeStruct((S, 2 * HW), BF16), f, f), grid=(S // TR,),
        in_specs=[_row_spec(HW)] * 6 + [pl.BlockSpec(memory_space=pl.ANY)],
        out_specs=(pl.BlockSpec((TR, HW), lambda i: (i, 1)), _row_spec(HW), _row_spec(HW)),
        input_output_aliases={6: 0},
        compiler_params=_params(1),
    )(*outs, *lses, cat)


def _attn_bwd(proj, qn, kn, lse, delta, do, dil, acc, phase=None):
    hp = HEADS_PER_STEP[dil]
    rows, ng, nb = BLK * dil, H // hp, S // (BLK * dil)
    has_acc = acc is not None

    def body(*refs):
        q_ref, qn_ref, kp_ref, kc_ref, vp_ref, vc_ref, l_ref, ln_ref, d_ref, dn_ref, do_ref, don_ref, sl_ref = refs[:13]
        refs = refs[13:]
        if has_acc:
            aq_ref, ak_ref, av_ref = refs[:3]
            refs = refs[3:]
        dq_ref, dk_ref, dv_ref = refs
        m = pl.program_id(0)
        steps_c, steps_p, ok_c, ok_p = _att_masks()
        ok_prev = jnp.logical_and(ok_p, m > 0)
        ok_next = jnp.logical_and(ok_p, m < nb - 1)
        fc, fp = steps_c.astype(F32), steps_p.astype(F32)
        for hh in range(hp):
            cols = slice(hh * DH, (hh + 1) * DH)
            sl = sl_ref[0, hh:hh + 1, :]
            for r in range(dil):
                sub = lambda ref: _sub_rows(ref, r, dil, cols)
                qv, qnv, kcv, kpv = sub(q_ref), sub(qn_ref), sub(kc_ref), sub(kp_ref)
                vc, vp = sub(vc_ref), sub(vp_ref)
                dov, donv = sub(do_ref), sub(don_ref)
                lse_m, lse_n = sub(l_ref)[:, 0:1], sub(ln_ref)[:, 0:1]
                delta_m, delta_n = sub(d_ref)[:, 0:1], sub(dn_ref)[:, 0:1]
                p_c = jnp.where(ok_c, jnp.exp(_dot(qv, kcv, NT) * ATT_SCALE - sl * fc - lse_m), 0.0)
                p_p = jnp.where(ok_prev, jnp.exp(_dot(qv, kpv, NT) * ATT_SCALE - sl * fp - lse_m), 0.0)
                p_n = jnp.where(ok_next, jnp.exp(_dot(qnv, kcv, NT) * ATT_SCALE - sl * fp - lse_n), 0.0)
                ds_c = p_c * (_dot(dov, vc, NT) - delta_m)
                ds_p = p_p * (_dot(dov, vp, NT) - delta_m)
                ds_n = p_n * (_dot(donv, vc, NT) - delta_n)
                dqn = (_dot(ds_c, kcv) + _dot(ds_p, kpv)) * ATT_SCALE
                dkn = (_dot(ds_c, qv, TN) + _dot(ds_n, qnv, TN)) * ATT_SCALE
                dv = _dot(p_c, dov, TN) + _dot(p_n, donv, TN)
                if has_acc:
                    dqn, dkn, dv = dqn + sub(aq_ref), dkn + sub(ak_ref), dv + sub(av_ref)
                _set_sub_rows(dq_ref, r, dil, cols, dqn)
                _set_sub_rows(dk_ref, r, dil, cols, dkn)
                _set_sub_rows(dv_ref, r, dil, cols, dv)

    def shifted(shift, m):
        if shift < 0:
            return jnp.maximum(m - 1, 0)
        if shift > 0:
            return jnp.minimum(m + 1, nb - 1)
        return m

    def spec(shift, group=0):
        return pl.BlockSpec((rows, hp * DH), lambda m, g: (shifted(shift, m), group * ng + g))

    ins = [qn, qn, kn, kn, proj, proj, lse, lse, delta, delta, do, do, _slopes(dil)]
    in_specs = [spec(0), spec(1), spec(-1), spec(0), spec(-1, 6), spec(0, 6), spec(0), spec(1), spec(0), spec(1),
                spec(0, 1), spec(1, 1), pl.BlockSpec((1, hp, DH), lambda m, g: (g, 0, 0))]
    if has_acc:
        ins += list(acc)
        in_specs += [spec(0)] * 3
    big = jax.ShapeDtypeStruct((S, HW), F32)
    return _pcall(
        body, ins, phase, name=f"attn_bwd_d{dil}", out_shape=(big, big, big), grid=(nb, ng),
        in_specs=in_specs, out_specs=(spec(0),) * 3,
        compiler_params=_params(2),
    )


TC = 512
NCT = DFF // TC


def _shift_rows(a, k):
    rows = lax.broadcasted_iota(jnp.int32, a.shape, 0)
    rolled = pltpu.roll(a, k % S, 0)
    if k > 0:
        return jnp.where(rows >= k, rolled, 0.0)
    return jnp.where(rows < S + k, rolled, 0.0)


def _conv_pre(a, w_ref, b_ref):
    return b_ref[...] + w_ref[0:1, :] * _shift_rows(a, 2) + w_ref[1:2, :] * _shift_rows(a, 1) + w_ref[2:3, :] * a


def _conv_gate_fwd(u, conv_w, conv_b, name, tiles=(0, NCT), into=None, phase=None):
    t0, t1 = tiles

    def body(a_ref, g_ref, w_ref, b_ref, *rest):
        pre = _conv_pre(a_ref[...].astype(F32), w_ref, b_ref)
        rest[-1][...] = (pre * _sigmoid(pre) * g_ref[...].astype(F32)).astype(BF16)

    args = [u, u, conv_w, conv_b]
    in_specs = [pl.BlockSpec((S, TC), lambda i: (0, t0 + i)), pl.BlockSpec((S, TC), lambda i: (0, NCT + t0 + i)),
                pl.BlockSpec((3, TC), lambda i: (0, t0 + i)), pl.BlockSpec((1, TC), lambda i: (0, t0 + i))]
    kw = {}
    if into is not None:
        args.append(into)
        in_specs.append(pl.BlockSpec(memory_space=pl.ANY))
        kw["input_output_aliases"] = {4: 0}
    return _pcall(
        body, args, phase, name=name, out_shape=jax.ShapeDtypeStruct((S, DFF), BF16), grid=(t1 - t0,),
        in_specs=in_specs, out_specs=pl.BlockSpec((S, TC), lambda i: (0, t0 + i)),
        compiler_params=_params(1), **kw,
    )


def _conv_gate_bwd(dy, u, conv_w, conv_b, phase=None):
    def body(dy_ref, a_ref, g_ref, w_ref, b_ref, du_ref, db_ref, dw_ref, da_buf, dg_buf, sems):
        i = pl.program_id(0)
        slot = i % 2

        def writes(step, s):
            col = pl.multiple_of(step * TC, TC)
            return (pltpu.make_async_copy(da_buf.at[s], du_ref.at[:, pl.ds(col, TC)], sems.at[0, s]),
                    pltpu.make_async_copy(dg_buf.at[s], du_ref.at[:, pl.ds(DFF + col, TC)], sems.at[1, s]))

        @pl.when(i >= 2)
        def _():
            for cp in writes(i - 2, slot):
                cp.wait()

        a = a_ref[...].astype(F32)
        dyv = dy_ref[...].astype(F32)
        pre = _conv_pre(a, w_ref, b_ref)
        sg = _sigmoid(pre)
        dg_buf[slot] = (dyv * pre * sg).astype(BF16)
        dpre = dyv * g_ref[...].astype(F32) * (sg * (1.0 + pre * (1.0 - sg)))
        da = w_ref[2:3, :] * dpre + w_ref[1:2, :] * _shift_rows(dpre, -1) + w_ref[0:1, :] * _shift_rows(dpre, -2)
        da_buf[slot] = da.astype(BF16)
        for cp in writes(i, slot):
            cp.start()
        db_ref[...] = jnp.sum(dpre, axis=0, keepdims=True)
        dw_ref[0:1, :] = jnp.sum(dpre * _shift_rows(a, 2), axis=0, keepdims=True)
        dw_ref[1:2, :] = jnp.sum(dpre * _shift_rows(a, 1), axis=0, keepdims=True)
        dw_ref[2:3, :] = jnp.sum(dpre * a, axis=0, keepdims=True)

        @pl.when(i == NCT - 1)
        def _():
            for cp in writes(i - 1, 1 - slot) + writes(i, slot):
                cp.wait()

    col = pl.BlockSpec((S, TC), lambda i: (0, i))
    return _pcall(
        body, [dy, u, u, conv_w, conv_b], phase, name="conv_gate_bwd",
        out_shape=(jax.ShapeDtypeStruct((S, 2 * DFF), BF16), jax.ShapeDtypeStruct((1, DFF), F32),
                   jax.ShapeDtypeStruct((3, DFF), F32)),
        grid=(NCT,),
        in_specs=[col, col, pl.BlockSpec((S, TC), lambda i: (0, NCT + i)),
                  pl.BlockSpec((3, TC), lambda i: (0, i)), pl.BlockSpec((1, TC), lambda i: (0, i))],
        out_specs=(pl.BlockSpec(memory_space=pl.ANY), pl.BlockSpec((1, TC), lambda i: (0, i)),
                   pl.BlockSpec((3, TC), lambda i: (0, i))),
        scratch_shapes=[pltpu.VMEM((2, S, TC), BF16), pltpu.VMEM((2, S, TC), BF16), pltpu.SemaphoreType.DMA((2, 2))],
        compiler_params=_params(1),
    )


def _out_loss(z, x1, target, gate):
    def body(z_ref, x_ref, t_ref, g_ref, do_ref, dz_ref, dg_ref, loss_ref):
        i = pl.program_id(0)
        zv = z_ref[...]
        gv = g_ref[...]
        err = x_ref[...] + gv * zv - t_ref[...]
        dout = err * (1.0 / D)
        do_ref[...] = dout
        dz_ref[...] = (dout * gv).astype(BF16)

        @pl.when(i == 0)
        def _():
            dg_ref[...] = jnp.zeros_like(dg_ref)
            loss_ref[...] = jnp.zeros_like(loss_ref)

        dg_ref[...] += jnp.sum(dout * zv, axis=0, keepdims=True)
        part = jnp.sum(jnp.sum(err * err, axis=0, keepdims=True), axis=-1, keepdims=True) * (0.5 / D)
        lane = lax.broadcasted_iota(jnp.int32, (1, 128), 1)
        loss_ref[...] += jnp.where(lane == 0, part, 0.0)

    return pl.pallas_call(
        body, name="out_loss",
        out_shape=(jax.ShapeDtypeStruct((S, D), F32), jax.ShapeDtypeStruct((S, D), BF16),
                   jax.ShapeDtypeStruct((1, D), F32), jax.ShapeDtypeStruct((1, 128), F32)),
        grid=(S // TR,),
        in_specs=[_row_spec(), _row_spec(), _row_spec(), _vec_spec()],
        out_specs=(_row_spec(), _row_spec(), _vec_spec(), _vec_spec(128)),
        compiler_params=_params(1),
    )(z, x1, target, gate)


ADA_COLS = 6 * D // N_CHIPS
TA = 512


def _ada_fwd(c_all, w_shard, b_shard):
    def body(c_ref, w_ref, b_ref, o_ref):
        cv = c_ref[...]
        o_ref[...] = _dot_f32(cv * _sigmoid(cv), w_ref[...]) + b_ref[...]

    return pl.pallas_call(
        body, name="ada_fwd", out_shape=jax.ShapeDtypeStruct((N_DEV, ADA_COLS), F32), grid=(ADA_COLS // TA,),
        in_specs=[pl.BlockSpec((N_DEV, D), lambda j: (0, 0)), pl.BlockSpec((D, TA), lambda j: (0, j)),
                  pl.BlockSpec((1, TA), lambda j: (0, j))],
        out_specs=pl.BlockSpec((N_DEV, TA), lambda j: (0, j)),
        compiler_params=_params(1),
    )(c_all, w_shard, b_shard)


ADA_ROWS = 128


def _ada_bwd_adamw(c_all, dmod_shard, w, m, v):
    def body(c_ref, d_ref, w_ref, m_ref, v_ref, g_ref, dl_ref, mo_ref, vo_ref):
        cv = c_ref[...]
        gv = lax.dot_general(cv * _sigmoid(cv), d_ref[...], TN, precision=lax.Precision.HIGHEST,
                             preferred_element_type=F32)
        g_ref[...] = gv
        m2 = ADAM_B1 * m_ref[...] + (1.0 - ADAM_B1) * gv
        v2 = ADAM_B2 * v_ref[...] + (1.0 - ADAM_B2) * (gv * gv)
        m_hat = m2 / (1.0 - ADAM_B1 ** ADAM_STEP)
        v_hat = v2 / (1.0 - ADAM_B2 ** ADAM_STEP)
        dl_ref[...] = -ADAM_LR * (m_hat / (jnp.sqrt(v_hat) + ADAM_EPS) + ADAM_WD * w_ref[...])
        mo_ref[...] = m2
        vo_ref[...] = v2

    spec = pl.BlockSpec((ADA_ROWS, ADA_COLS), lambda i: (i, 0))
    out = jax.ShapeDtypeStruct((D, ADA_COLS), F32)
    return pl.pallas_call(
        body, name="ada_bwd_adamw", out_shape=(out,) * 4, grid=(D // ADA_ROWS,),
        in_specs=[pl.BlockSpec((N_DEV, ADA_ROWS), lambda i: (0, i)), pl.BlockSpec((N_DEV, ADA_COLS), lambda i: (0, 0)),
                  spec, spec, spec],
        out_specs=(spec,) * 4,
        compiler_params=_params(1),
    )(c_all, dmod_shard, w, m, v)


def _sum_rows(g, name):
    rows, n = g.shape

    def body(g_ref, o_ref):
        acc = g_ref[0:1, :]
        for i in range(1, rows):
            acc = acc + g_ref[i:i + 1, :]
        o_ref[...] = acc

    return pl.pallas_call(
        body, name=name, out_shape=jax.ShapeDtypeStruct((1, n), F32),
        in_specs=[VMEM_SPEC], out_specs=VMEM_SPEC,
    )(g)


def _lb_grad(lb_logits, dlb):
    def body(l_ref, d_ref, o_ref):
        p = 1.0 / (1.0 + jnp.exp(l_ref[1:2, :] - l_ref[0:1, :]))
        t = d_ref[...] * p * (1.0 - p)
        o_ref[0:1, :] = t
        o_ref[1:2, :] = -t

    return pl.pallas_call(
        body, name="lb_grad", out_shape=jax.ShapeDtypeStruct((2, HW), F32),
        in_specs=[VMEM_SPEC, VMEM_SPEC], out_specs=VMEM_SPEC,
    )(lb_logits, dlb)


def _adamw(w, g, m, v, name, copy_grad=False):
    rows, cols = w.shape
    tr = rows
    if rows * cols * 4 > ADAM_BLOCK_BYTES:
        tr = _pick(rows, [t for t in (256, 128, 64, 32, 16, 8) if t * cols * 4 <= ADAM_BLOCK_BYTES])

    def body(w_ref, g_ref, m_ref, v_ref, d_ref, mo_ref, vo_ref, *go_ref):
        gv = g_ref[...]
        if copy_grad:
            go_ref[0][...] = gv
        m2 = ADAM_B1 * m_ref[...] + (1.0 - ADAM_B1) * gv
        v2 = ADAM_B2 * v_ref[...] + (1.0 - ADAM_B2) * (gv * gv)
        m_hat = m2 / (1.0 - ADAM_B1 ** ADAM_STEP)
        v_hat = v2 / (1.0 - ADAM_B2 ** ADAM_STEP)
        d_ref[...] = -ADAM_LR * (m_hat / (jnp.sqrt(v_hat) + ADAM_EPS) + ADAM_WD * w_ref[...])
        mo_ref[...] = m2
        vo_ref[...] = v2

    spec = pl.BlockSpec((tr, cols), lambda i: (i, 0))
    out = jax.ShapeDtypeStruct((rows, cols), F32)
    n_out = 4 if copy_grad else 3
    return pl.pallas_call(
        body, name=name, out_shape=(out,) * n_out, grid=(rows // tr,),
        in_specs=[spec] * 4, out_specs=(spec,) * n_out,
        compiler_params=_params(1),
    )(w, g, m, v)


SC_TILES = 32
SC_LANES = 16
SC_COL_PARTS = 2
SC_CHUNK_ROWS = 8


def _adamw_sc(w, g, m, v, name):
    rows, cols = w.shape
    band, part = rows // (SC_TILES // SC_COL_PARTS), cols // SC_COL_PARTS
    assert band % SC_CHUNK_ROWS == 0 and part % SC_LANES == 0, (rows, cols)

    def body(w_hbm, g_hbm, m_hbm, v_hbm, d_hbm, mo_hbm, vo_hbm, go_hbm, wb, gb, mb, vb, db):
        tile = lax.axis_index("sc_subcore") * 2 + lax.axis_index("sc_core")
        row0 = (tile // SC_COL_PARTS) * band
        col0 = (tile % SC_COL_PARTS) * part

        @pl.loop(0, band, step=SC_CHUNK_ROWS)
        def _(r):
            at = lambda ref: ref.at[pl.ds(row0 + r, SC_CHUNK_ROWS), pl.ds(col0, part)]
            pltpu.sync_copy(at(w_hbm), wb)
            pltpu.sync_copy(at(g_hbm), gb)
            pltpu.sync_copy(gb, at(go_hbm))
            pltpu.sync_copy(at(m_hbm), mb)
            pltpu.sync_copy(at(v_hbm), vb)

            @pl.loop(0, SC_CHUNK_ROWS)
            def _(i):
                @pl.loop(0, part, step=SC_LANES)
                def _(j):
                    sl = (i, pl.ds(j, SC_LANES))
                    gv = gb[sl]
                    m2 = ADAM_B1 * mb[sl] + (1.0 - ADAM_B1) * gv
                    v2 = ADAM_B2 * vb[sl] + (1.0 - ADAM_B2) * (gv * gv)
                    m_hat = m2 / (1.0 - ADAM_B1 ** ADAM_STEP)
                    v_hat = v2 / (1.0 - ADAM_B2 ** ADAM_STEP)
                    db[sl] = -ADAM_LR * (m_hat / (jnp.sqrt(v_hat) + ADAM_EPS) + ADAM_WD * wb[sl])
                    mb[sl] = m2
                    vb[sl] = v2

            pltpu.sync_copy(db, at(d_hbm))
            pltpu.sync_copy(mb, at(mo_hbm))
            pltpu.sync_copy(vb, at(vo_hbm))

    out = jax.ShapeDtypeStruct((rows, cols), F32)
    buf = pltpu.VMEM((SC_CHUNK_ROWS, part), F32)
    return pl.kernel(
        body, name=name, out_type=(out, out, out, out),
        mesh=plsc.VectorSubcoreMesh(core_axis_name="sc_core", subcore_axis_name="sc_subcore"),
        scratch_types=[buf] * 5,
    )(w, g, m, v)


W_IN, W_OUT, W_UP, W_DOWN = range(4)
IN_CHUNKS = 2
W_INQ = 4


def _sequence_step(x, target, mod, norm1_w, lb_logits, hg_norm_w, q_norm_w, k_norm_w, norm2_w, conv_w, conv_b, net):
    shift1, scale1, gate1, shift2, scale2, gate2 = [mod[:, i * D:(i + 1) * D] for i in range(6)]

    def run(name, fn, *args, **kw):
        phase = net.host(name)
        if phase is None:
            return fn(*args, **kw)
        res, comm = fn(*args, phase=phase, **kw)
        net.done(name, comm, res[0] if isinstance(res, tuple) else res)
        return res

    h = _norm_mod(x, norm1_w, scale1, shift1, "norm1_fwd")
    proj = run("proj_fwd", _matmul, h, net.full[W_IN], "nn", F32, "proj_fwd")
    cat, o_raw, states = run("hgrn_fwd", _hgrn_fwd, proj, lb_logits, hg_norm_w)
    qn, kn = _qk_prep(proj, q_norm_w, k_norm_w)
    att = [run(f"attn_fwd_d{dil}", _attn_fwd, proj, qn, kn, dil) for dil in DILS]
    cat, b_out_f32, lse = _attn_merge([t[0] for t in att], [t[1] for t in att], cat)
    mix = run("mix_fwd", _matmul, cat, net.full[W_OUT], "nn", F32, "mix_fwd")
    x1, h2 = _resid_norm_mod(x, mix, gate1, norm2_w, scale2, shift2, "norm2_fwd")
    u = run("up_fwd", _matmul, h2, net.full[W_UP], "nn", BF16, "up_fwd")
    yact = run("conv_gate_fwd_a", _conv_gate_fwd, u, conv_w, conv_b, "conv_gate_fwd_a", tiles=(0, NCT // 2 + 1))
    yact = run("conv_gate_fwd_b", _conv_gate_fwd, u, conv_w, conv_b, "conv_gate_fwd_b", tiles=(NCT // 2 + 1, NCT),
               into=yact)
    z =_matmul(yact, net.full[W_DOWN], "nn", F32, "down_fwd")
    dout, dz, dgate2, loss_row = _out_loss(z, x1, target, gate2)

    net.grad[W_DOWN] = _matmul(yact, dz, "tn", BF16, "down_bwd_w")
    dyact = run("down_bwd_x", _matmul, dz, net.full[W_DOWN], "nt", BF16, "down_bwd_x")
    du, dconv_b, dconv_w = run("conv_gate_bwd", _conv_gate_bwd, dyact, u, conv_w, conv_b)
    net.grad[W_UP] = run("up_bwd_w", _matmul, h2, du, "tn", BF16, "up_bwd_w")
    dh2 = run("up_bwd_x", _matmul, du, net.full[W_UP], "nt", F32, "up_bwd_x")
    dx1, dshift2, dscale2, dnorm2, dgate1, dmix = run(
        "norm2_bwd", _norm_mod_bwd, dh2, x1, norm2_w, scale2, dout, "norm2_bwd", mix=mix, gate=gate1)
    net.grad[W_OUT] = run("mix_bwd_w", _matmul, cat, dmix, "tn", BF16, "mix_bwd_w")
    dcat = run("mix_bwd_x", _matmul, dmix, net.full[W_OUT], "nt", F32, "mix_bwd_x")
    dhq, dhf, dhi, dhg, dlb, dhg_norm = run("hgrn_bwd", _hgrn_bwd, proj, lb_logits, hg_norm_w, o_raw, states, dcat)
    delta = _attn_delta(dcat, b_out_f32)
    acc = None
    for dil in DILS:
        acc = run(f"attn_bwd_d{dil}", _attn_bwd, proj, qn, kn, lse, delta, dcat, dil, acc)
    daq, dak, dav, dq_norm, dk_norm = _qk_norm_bwd(proj, *acc, q_norm_w, k_norm_w)
    dproj = jnp.concatenate([dhq, dhf, dhi, dhg, daq, dak, dav], axis=1)
    net.carry = dproj
    for q in range(IN_CHUNKS):
        net.after_chunk(q, run(f"proj_bwd_w_{q}", _matmul, h, net.carry, "tn", BF16, f"proj_bwd_w_{q}",
                               m_blocks=(D // IN_CHUNKS, [q])))
    dh = run("proj_bwd_x", _matmul, net.carry, net.full[W_IN], "nt", F32, "proj_bwd_x")
    grad_x, dshift1, dscale1, dnorm1 = run("norm1_bwd", _norm_mod_bwd, dh, x, norm1_w, scale1, dx1, "norm1_bwd")

    dmod = jnp.concatenate([dshift1, dscale1, dgate1, dshift2, dscale2, dgate2], axis=1)
    small = dict(norm1_w=dnorm1, lb=dlb, hg_norm_w=dhg_norm, q_norm_w=dq_norm, k_norm_w=dk_norm,
                 norm2_w=dnorm2, conv_b=dconv_b, conv_w=dconv_w)
    return loss_row, grad_x, dmod, small


def _place():
    x, y, c = lax.axis_index("x"), lax.axis_index("y"), lax.axis_index("c")
    others = [(1 - x, y), (x, 1 - y), (1 - x, 1 - y)]
    return x, y, c, others


def _remote(src, dst, send_sems, recv_sems, k, to):
    return pltpu.make_async_remote_copy(src_ref=src, dst_ref=dst, send_sem=send_sems.at[k], recv_sem=recv_sems.at[k],
                                        device_id=to, device_id_type=MESH)


class _Phase:
    def __init__(self):
        self.ins, self.outs, self.aliases, self.groups, self.n = [], [], {}, [], 0

    def add(self, ins, outs, aliases, n, copies):
        i0, o0 = len(self.ins), len(self.outs)
        self.ins += list(ins)
        self.outs += list(outs)
        self.aliases.update({i0 + i: o0 + o for i, o in aliases.items()})
        self.groups.append((slice(i0, i0 + len(ins)), slice(o0, o0 + len(outs)), copies))
        self.n += n
        return slice(o0, o0 + len(outs))

    def build(self, cin, cout, send_sems, recv_sems, landing):
        res = []
        for si, so, copies in self.groups:
            for src, dst, land, peer in copies(cin[si], cout[so]):
                k = len(res)
                res.append(_remote(land, land, send_sems, recv_sems, k, peer) if landing
                           else _remote(src, dst, send_sems, recv_sems, k, peer))
        assert len(res) == self.n
        return res


def _pcall(body, args, phase=None, **kw):
    if phase is None or not phase.groups:
        res = pl.pallas_call(body, **kw)(*args)
        return res if phase is None else (res, ())
    grid = tuple(kw.pop("grid", ()))
    out_shape, out_specs = kw.pop("out_shape"), kw.pop("out_specs")
    single = not isinstance(out_shape, (tuple, list))
    out_shape = [out_shape] if single else list(out_shape)
    out_specs = [out_specs] if single else list(out_specs)
    scratch = list(kw.pop("scratch_shapes", ()))
    aliases = dict(kw.pop("input_output_aliases", {}))
    n_in, n_out, n_ci, n_co = len(args), len(out_shape), len(phase.ins), len(phase.outs)
    aliases.update({n_in + i: n_out + o for i, o in phase.aliases.items()})

    def hosted(*refs):
        ins, cin = refs[:n_in], refs[n_in:n_in + n_ci]
        outs = refs[n_in + n_ci:n_in + n_ci + n_out]
        cout = refs[n_in + n_ci + n_out:n_in + n_ci + n_out + n_co]
        scr, send_sems, recv_sems = refs[n_in + n_ci + n_out + n_co:-2], refs[-2], refs[-1]
        ids = [pl.program_id(a) for a in range(len(grid))]

        def start():
            for cp in phase.build(cin, cout, send_sems, recv_sems, False):
                cp.start()

        def finish():
            for cp in phase.build(cin, cout, send_sems, recv_sems, False):
                cp.wait_send()
            for cp in phase.build(cin, cout, send_sems, recv_sems, True):
                cp.wait_recv()

        if grid:
            first = functools.reduce(jnp.logical_and, [i == 0 for i in ids])
            last = functools.reduce(jnp.logical_and, [i == g - 1 for i, g in zip(ids, grid)])
            pl.when(first)(start)
            body(*ins, *outs, *scr)
            pl.when(last)(finish)
        else:
            start()
            body(*ins, *outs, *scr)
            finish()

    res = pl.pallas_call(
        hosted, out_shape=tuple(out_shape + phase.outs), grid=grid,
        in_specs=list(kw.pop("in_specs")) + [HBM_SPEC] * n_ci, out_specs=tuple(out_specs + [HBM_SPEC] * n_co),
        input_output_aliases=aliases,
        scratch_shapes=scratch + [pltpu.SemaphoreType.DMA((phase.n,)), pltpu.SemaphoreType.DMA((phase.n,))],
        **kw,
    )(*args, *phase.ins)
    outs = res[:n_out]
    return (outs[0] if single else tuple(outs)), tuple(res[n_out:])


def _comm_only(name, phase):
    return _pcall(lambda: None, [], phase, name=name, out_shape=(), in_specs=[], out_specs=())[1]


def _all_gather_rows(block, name, phase=None):
    m_per, n = block.shape

    def body(x_ref, out_ref, send_sems, recv_sems, local_sem):
        x, y, c, chips = _place()
        me, sibling = (x, y, c), (x, y, 1 - c)

        def rows(px, py, pc):
            return out_ref.at[pl.ds((4 * px + 2 * py + pc) * m_per, m_per), :]

        def copy(k, blk, to, src=None):
            return _remote(rows(*blk) if src is None else src, rows(*blk), send_sems, recv_sems, k, to)

        mine = pltpu.make_async_copy(x_ref, rows(*me), local_sem)
        mine.start()
        first = [copy(0, me, sibling, src=x_ref)]
        first += [copy(1 + j, me, (*chip, c), src=x_ref) for j, chip in enumerate(chips)]
        for cp in first:
            cp.start()
        passed = [copy(4 + j, (*chip, c), sibling) for j, chip in enumerate(chips)]
        for j, chip in enumerate(chips):
            copy(1 + j, (*chip, c), me).wait_recv()
            passed[j].start()
        copy(0, sibling, me).wait_recv()
        for j, chip in enumerate(chips):
            copy(4 + j, (*chip, 1 - c), me).wait_recv()
        for cp in first + passed:
            cp.wait_send()
        mine.wait()

    return _pcall(
        body, [block], phase, name=name, out_shape=jax.ShapeDtypeStruct((N_DEV * m_per, n), block.dtype),
        in_specs=[VMEM_SPEC], out_specs=VMEM_SPEC,
        scratch_shapes=[pltpu.SemaphoreType.DMA((7,)), pltpu.SemaphoreType.DMA((7,)), pltpu.SemaphoreType.DMA],
    )


class _Geo:
    def __init__(self, kind, shard_shape):
        self.kind = kind
        self.shard_shape = tuple(shard_shape)
        self.hr, self.hc = shard_shape[0] // 2, shard_shape[1]
        if kind == "col":
            self.full_shape = (shard_shape[0], N_CHIPS * shard_shape[1])
        else:
            self.full_shape = (N_CHIPS * shard_shape[0], shard_shape[1])

    def full_half(self, ref, j, c):
        return self.piece(ref, j, c, 0, 1, 1)

    def piece(self, ref, j, c, p0, p1, pieces, part=None):
        pr = self.hr // pieces
        off, n = p0 * pr, (p1 - p0) * pr
        if part is not None:
            top = (n // 32) * 16
            off, n = (off, top) if part == 0 else (off + top, n - top)
        if self.kind == "col":
            return ref.at[pl.ds(pl.multiple_of(c * self.hr + off, 16), n),
                          pl.ds(pl.multiple_of(j * self.hc, 128), self.hc)]
        return ref.at[pl.ds(pl.multiple_of((2 * j + c) * self.hr + off, 16), n), :]


WEIGHT_KINDS = ("col", "row", "col", "row")
NW = len(WEIGHT_KINDS)


def _comm_call(body, name, ins, outs, n_remote, aliases=None):
    return pl.pallas_call(
        body, name=name, out_shape=tuple(outs),
        in_specs=[HBM_SPEC] * len(ins), out_specs=tuple([HBM_SPEC] * len(outs)),
        input_output_aliases=aliases or {},
        scratch_shapes=[pltpu.SemaphoreType.DMA((n_remote,)), pltpu.SemaphoreType.DMA((n_remote,))],
    )(*ins)


ROW_TILES = (256, 176, 128, 64, 32, 16)


def _cast_into_full(shard, geo, place, name):
    rs, cs = shard.shape
    tr = _pick(rs, ROW_TILES)
    nb = rs // tr

    def body(place_ref, s_ref, o_ref):
        o_ref[...] = s_ref[...].astype(BF16)

    if geo.kind == "col":
        out_map = lambda i, place_ref: (i, place_ref[0])
    else:
        out_map = lambda i, place_ref: (place_ref[0] * nb + i, 0)
    return pl.pallas_call(
        body, name=name, out_shape=jax.ShapeDtypeStruct(geo.full_shape, BF16),
        grid_spec=pltpu.PrefetchScalarGridSpec(
            num_scalar_prefetch=1, grid=(nb,),
            in_specs=[pl.BlockSpec((tr, cs), lambda i, place_ref: (i, 0))],
            out_specs=pl.BlockSpec((tr, cs), out_map)),
        compiler_params=_params(1),
    )(place, shard)


def _gather_weight(full, geo, pieces, name):
    per = 8

    def body(in_ref, ref, send_sems, recv_sems):
        x, y, c, _ = _place()
        j, jx, jy, jo = 2 * x + y, 2 * (1 - x) + y, 2 * x + (1 - y), 2 * (1 - x) + (1 - y)
        nx, ny, sib = (1 - x, y, c), (x, 1 - y, c), (x, y, 1 - c)

        def cp(region, k, to):
            return _remote(region, region, send_sems, recv_sems, k, to)

        def reg(chip, p, part=None, core=c):
            return geo.piece(ref, chip, core, p, p + 1, pieces, part)

        sent = []
        for p in range(pieces):
            sent += [cp(reg(j, p), per * p, nx), cp(reg(j, p), per * p + 1, ny)]
        for d in sent:
            d.start()
        for p in range(pieces):
            cp(reg(jx, p), per * p, nx).wait_recv()
            new = [cp(reg(jx, p, 0), per * p + 2, ny), cp(reg(jx, p), per * p + 4, sib)]
            cp(reg(jy, p), per * p + 1, ny).wait_recv()
            new += [cp(reg(jy, p, 1), per * p + 3, nx), cp(reg(jy, p), per * p + 5, sib)]
            for d in new:
                d.start()
            sent += new
        for p in range(pieces):
            cp(reg(jo, p, 0), per * p + 2, ny).wait_recv()
            cp(reg(jo, p, 1), per * p + 3, nx).wait_recv()
            new = [cp(reg(jo, p, 0), per * p + 6, sib), cp(reg(jo, p, 1), per * p + 7, sib)]
            for d in new:
                d.start()
            sent += new
        for p in range(pieces):
            cp(reg(jx, p, None, 1 - c), per * p + 4, sib).wait_recv()
            cp(reg(jy, p, None, 1 - c), per * p + 5, sib).wait_recv()
            cp(reg(jo, p, 0, 1 - c), per * p + 6, sib).wait_recv()
            cp(reg(jo, p, 1, 1 - c), per * p + 7, sib).wait_recv()
        for d in sent:
            d.wait_send()

    return _comm_call(body, name, [full], [_same(full)], per * pieces, aliases={0: 0})[0]


def _same(a):
    return jax.ShapeDtypeStruct(a.shape, a.dtype)


def _gather_ici(full, geo, p0, p1, pieces):
    def copies(ins, outs):
        x, y, c, _ = _place()
        mine = geo.piece(outs[0], 2 * x + y, c, p0, p1, pieces)
        return [(mine, mine, geo.piece(outs[0], 2 * ox + oy, c, p0, p1, pieces), (ox, oy, c))
                for ox, oy in ((1 - x, y), (x, 1 - y))]

    return dict(ins=[full], outs=[_same(full)], aliases={0: 0}, n=2, copies=copies)


def _gather_relay(full, geo, p0, p1, pieces):
    def copies(ins, outs):
        x, y, c, _ = _place()
        jx, jy, jo = 2 * (1 - x) + y, 2 * x + (1 - y), 2 * (1 - x) + (1 - y)
        reg = lambda chip, part: geo.piece(outs[0], chip, c, p0, p1, pieces, part)
        return [(reg(jx, 0), reg(jx, 0), reg(jo, 0), (x, 1 - y, c)), (reg(jy, 1), reg(jy, 1), reg(jo, 1), (1 - x, y, c))]

    return dict(ins=[full], outs=[_same(full)], aliases={0: 0}, n=2, copies=copies)


def _gather_d2d(full, geo):
    def copies(ins, outs):
        x, y, c, chips = _place()
        return [(geo.full_half(outs[0], 2 * ox + oy, c), geo.full_half(outs[0], 2 * ox + oy, c),
                 geo.full_half(outs[0], 2 * ox + oy, 1 - c), (x, y, 1 - c)) for ox, oy in chips]

    return dict(ins=[full], outs=[_same(full)], aliases={0: 0}, n=3, copies=copies)


def _swap_d2d(grad, geo):
    def copies(ins, outs):
        x, y, c, _ = _place()
        return [(geo.full_half(ins[0], j, 1 - c), outs[0].at[j], outs[0].at[j], (x, y, 1 - c)) for j in range(N_CHIPS)]

    return dict(ins=[grad], outs=[jax.ShapeDtypeStruct((N_CHIPS, geo.hr, geo.hc), BF16)], aliases={}, n=N_CHIPS,
                copies=copies)


SEM_SPEC = pl.BlockSpec(memory_space=pltpu.SEMAPHORE)
SIDE_EFFECT = pltpu.SideEffectType.DATAFLOW_SIDE_EFFECTING


def _fly_copies(swap, src_ref, land_ref, geo, sems):
    x, y, c, chips = _place()
    if swap:
        ends = [(geo.full_half(src_ref, j, 1 - c), land_ref.at[j], (x, y, 1 - c)) for j in range(N_CHIPS)]
    else:
        ends = [(src_ref.at[2 * ox + oy], land_ref.at[k], (ox, oy, c)) for k, (ox, oy) in enumerate(chips)]
    n = len(ends)
    return [pltpu.make_async_remote_copy(src_ref=src, dst_ref=dst, send_sem=sems[k], recv_sem=sems[n + k],
                                         device_id=peer, device_id_type=MESH) for k, (src, dst, peer) in enumerate(ends)]


def _fly_start(swap, src, geo, name, carry=None):
    n = N_CHIPS if swap else 3
    n_carry = 0 if carry is None else 1

    def body(src_ref, land_ref, *rest):
        sems, token = rest[n_carry:n_carry + 2 * n], rest[n_carry + 2 * n + 2]
        for copy in _fly_copies(swap, src_ref, land_ref, geo, sems):
            copy.start()
        token[...] = jnp.zeros_like(token)

    land = jax.ShapeDtypeStruct((n, geo.hr, geo.hc), BF16)
    out_shape = (pltpu.SemaphoreType.DMA(()),) * (2 * n) + (pltpu.HBM(src.shape, src.dtype), pltpu.HBM(land.shape, land.dtype),
                                                             jax.ShapeDtypeStruct((8, 128), F32))
    out_specs = (SEM_SPEC,) * (2 * n) + (HBM_SPEC, HBM_SPEC, VMEM_SPEC)
    args = [pltpu.with_memory_space_constraint(src, pltpu.HBM),
            pltpu.with_memory_space_constraint(lax.empty(land.shape, land.dtype), pltpu.HBM)]
    aliases = {0: 2 * n, 1: 2 * n + 1}
    if carry is not None:
        out_shape += (pltpu.HBM(carry.shape, carry.dtype),)
        out_specs += (HBM_SPEC,)
        args.append(pltpu.with_memory_space_constraint(carry, pltpu.HBM))
        aliases[2] = 2 * n + 3
    return pl.pallas_call(
        body, name=name, out_shape=out_shape, in_specs=(HBM_SPEC,) * len(args), out_specs=out_specs,
        input_output_aliases=aliases,
        compiler_params=pltpu.CompilerParams(has_side_effects=SIDE_EFFECT),
    )(*args)


def _fly_wait(swap, started, geo, after, name):
    n = N_CHIPS if swap else 3
    sems, src_thru, land_thru = started[:2 * n], started[2 * n], started[2 * n + 1]

    def body(src_ref, land_ref, *rest):
        for copy in _fly_copies(swap, src_ref, land_ref, geo, rest[:2 * n]):
            copy.wait_send()
            copy.wait_recv()

    return pl.pallas_call(
        body, name=name,
        out_shape=(pltpu.HBM(src_thru.shape, src_thru.dtype), pltpu.HBM(land_thru.shape, land_thru.dtype)),
        in_specs=(HBM_SPEC, HBM_SPEC) + (SEM_SPEC,) * (2 * n) + (pl.BlockSpec(memory_space=pl.ANY),),
        out_specs=(HBM_SPEC, HBM_SPEC), input_output_aliases={0: 0, 1: 1},
        compiler_params=pltpu.CompilerParams(has_side_effects=SIDE_EFFECT),
    )(src_thru, land_thru, *sems, after)


def _scatter_ici(pair, recv, geo, p0, p1, pieces):
    pr = geo.hr // pieces
    rows = pl.ds(p0 * pr, (p1 - p0) * pr)

    def copies(ins, outs):
        x, y, c, chips = _place()
        return [(ins[0].at[2 * ox + oy, rows, :], outs[0].at[k, rows, :], outs[0].at[k, rows, :], (ox, oy, c))
                for k, (ox, oy) in enumerate(chips)]

    out = jax.ShapeDtypeStruct((3, geo.hr, geo.hc), BF16)
    if recv is None:
        return dict(ins=[pair], outs=[out], aliases={}, n=3, copies=copies)
    return dict(ins=[pair, recv], outs=[out], aliases={1: 0}, n=3, copies=copies)


def _join_d2d(shard, geo, row0=0):
    def copies(ins, outs):
        x, y, c, _ = _place()
        mine = outs[0].at[pl.ds(pl.multiple_of(row0 + c * geo.hr, 8), geo.hr), :]
        other = outs[0].at[pl.ds(pl.multiple_of(row0 + (1 - c) * geo.hr, 8), geo.hr), :]
        return [(mine, mine, other, (x, y, 1 - c))]

    return dict(ins=[shard], outs=[_same(shard)], aliases={0: 0}, n=1, copies=copies)


def _add_pair(grad, got, geo, place, name):
    tr = _pick(geo.hr, ROW_TILES)
    nb = geo.hr // tr

    def body(place_ref, a_ref, b_ref, o_ref):
        o_ref[0] = (a_ref[...].astype(F32) + b_ref[0].astype(F32)).astype(BF16)

    if geo.kind == "col":
        own_map = lambda j, i, place_ref: (place_ref[1] * nb + i, j)
    else:
        own_map = lambda j, i, place_ref: ((2 * j + place_ref[1]) * nb + i, 0)
    spec = pl.BlockSpec((1, tr, geo.hc), lambda j, i, place_ref: (j, i, 0))
    return pl.pallas_call(
        body, name=name, out_shape=jax.ShapeDtypeStruct((N_CHIPS, geo.hr, geo.hc), BF16),
        grid_spec=pltpu.PrefetchScalarGridSpec(
            num_scalar_prefetch=1, grid=(N_CHIPS, nb),
            in_specs=[pl.BlockSpec((tr, geo.hc), own_map), spec], out_specs=spec),
        compiler_params=_params(2),
    )(place, grad, got)


def _add_four(pair, recv, geo, place, name, rows=None, row0=0, into=None):
    tr = _pick(geo.hr, ROW_TILES)
    nb = geo.hr // tr
    rows = 2 * geo.hr if rows is None else rows

    def body(place_ref, p_ref, r_ref, *rest):
        rest[-1][...] = ((p_ref[0].astype(F32) + r_ref[0].astype(F32)) + r_ref[1].astype(F32)) + r_ref[2].astype(F32)

    in_specs = [pl.BlockSpec((1, tr, geo.hc), lambda i, place_ref: (place_ref[0], i, 0)),
                pl.BlockSpec((3, tr, geo.hc), lambda i, place_ref: (0, i, 0))]
    args = [place, pair, recv]
    if into is not None:
        in_specs.append(pl.BlockSpec(memory_space=pl.ANY))
        args.append(into)
    return pl.pallas_call(
        body, name=name, out_shape=jax.ShapeDtypeStruct((rows, geo.hc), F32),
        grid_spec=pltpu.PrefetchScalarGridSpec(
            num_scalar_prefetch=1, grid=(nb,), in_specs=in_specs,
            out_specs=pl.BlockSpec((tr, geo.hc), lambda i, place_ref: (row0 // tr + place_ref[1] * nb + i, 0))),
        input_output_aliases={3: 0} if into is not None else {},
        compiler_params=_params(1),
    )(*args)


PIECES = (8, 1, 16, 8) + (1,) * IN_CHUNKS
SCHEDULE = {
    "proj_fwd": (("gather_ici", W_OUT, 0, 1), ("gather_ici", W_UP, 0, 6)),
    "hgrn_fwd": (("gather_relay", W_OUT, 0, 1), ("gather_relay", W_UP, 0, 6), ("gather_ici", W_UP, 6, 12)),
    "attn_fwd_d1": (("gather_relay", W_UP, 6, 12),),
    "attn_fwd_d4": (("gather_ici", W_UP, 12, 16), ("gather_d2d", W_OUT)),
    "attn_fwd_d16": (("gather_relay", W_UP, 12, 16), ("gather_ici", W_DOWN, 0, 2)),
    "mix_fwd": (("gather_d2d", W_UP), ("gather_ici", W_DOWN, 2, 4)),
    "up_fwd": (("gather_ici", W_DOWN, 4, 8), ("gather_relay", W_DOWN, 0, 4)),
    "conv_gate_fwd_a": (("gather_relay", W_DOWN, 4, 8),),
    "conv_gate_fwd_b": (("gather_d2d", W_DOWN),),
    "down_bwd_x": (("swap", W_DOWN),),
    "conv_gate_bwd": (("scatter", W_DOWN, 0, 4),),
    "up_bwd_w": (("scatter", W_DOWN, 4, 8),),
    "up_bwd_x": (("swap", W_UP),),
    "norm2_bwd": (("scatter", W_UP, 0, 1),),
    "mix_bwd_w": (("scatter", W_UP, 1, 2),),
    "mix_bwd_x": (("swap", W_OUT), ("scatter", W_UP, 2, 3)),
    "hgrn_bwd": (("scatter", W_UP, 3, 9), ("join", W_DOWN)),
    "attn_bwd_d1": (("scatter", W_UP, 9, 12),),
    "attn_bwd_d4": (("scatter", W_OUT, 0, 1),),
    "attn_bwd_d16": (("scatter", W_UP, 12, 16),),
    "proj_bwd_w_0": (("join", W_UP), ("join", W_OUT)),
    "gather_stats": (("join", W_INQ),),
    "grad_in_join": (("join", W_INQ + 1),),
}


class _Net:
    def __init__(self, shards, place):
        self.place = place
        self.geos = [_Geo(k, s.shape) for k, s in zip(WEIGHT_KINDS, shards)]
        self.full = [_cast_into_full(s, g, place, f"cast_shard_{w}") for w, (s, g) in enumerate(zip(shards, self.geos))]
        self.full[W_IN] = _gather_weight(self.full[W_IN], self.geos[W_IN], PIECES[W_IN], "gather_w_in")
        rows, cols = shards[W_IN].shape
        self.geos += [_Geo("col", (rows // IN_CHUNKS, cols))] * IN_CHUNKS
        n = NW + IN_CHUNKS
        self.grad, self.pair, self.recv, self.shard = [None] * n, [None] * n, [None] * n, [None] * n
        self.in_rows, self.in_shard = rows, None
        self.when_joined, self.joined = {}, {}
        self.flying = None
        self.carry = None
        self.swaps = [None] * IN_CHUNKS
        self.slots = []

    def _row0(self, w):
        return (w - W_INQ) * 2 * self.geos[w].hr

    def host(self, name):
        phase = _Phase()
        self.slots = []
        groups = {}
        for item in SCHEDULE.get(name, ()):
            kind, w = item[0], item[1]
            geo = self.geos[w]
            key = len(groups)
            if kind == "gather_ici":
                spec, key = _gather_ici(self.full[w], geo, item[2], item[3], PIECES[w]), ("full", w)
            elif kind == "gather_relay":
                spec, key = _gather_relay(self.full[w], geo, item[2], item[3], PIECES[w]), ("full", w)
            elif kind == "gather_d2d":
                spec, key = _gather_d2d(self.full[w], geo), ("full", w)
            elif kind == "swap":
                spec = _swap_d2d(self.grad[w], geo)
            elif kind == "scatter":
                spec, key = _scatter_ici(self.pair[w], self.recv[w], geo, item[2], item[3], PIECES[w]), ("recv", w)
            elif w >= W_INQ:
                spec, key = _join_d2d(self.in_shard, geo, self._row0(w)), "in_shard"
            else:
                spec = _join_d2d(self.shard[w], geo)
            groups.setdefault(key, []).append((item, spec))
        for group in groups.values():
            specs = [s for _, s in group]
            merged = dict(specs[0], n=sum(s["n"] for s in specs),
                          copies=lambda ins, outs, specs=specs: [t for s in specs for t in s["copies"](ins, outs)])
            self.slots.append(([item for item, _ in group], phase.add(**merged)))
        return phase

    def done(self, name, comm, result=None):
        for items, sl in sorted(self.slots, key=lambda s: s[0][0][0] == "scatter"):
            out = comm[sl][0]
            for item in items:
                kind, w = item[0], item[1]
                if kind in ("gather_ici", "gather_relay", "gather_d2d"):
                    self.full[w] = out
                elif kind == "swap":
                    self.pair[w] = _add_pair(self.grad[w], out, self.geos[w], self.place, f"grad_pair_sum_{w}")
                elif kind == "scatter":
                    self.recv[w] = out
                    if item[3] == PIECES[w] and w >= W_INQ:
                        self.in_shard = _add_four(self.pair[w], out, self.geos[w], self.place, f"grad_chip_sum_{w}",
                                                  rows=self.in_rows, row0=self._row0(w), into=self.in_shard)
                    elif item[3] == PIECES[w]:
                        self.shard[w] = _add_four(self.pair[w], out, self.geos[w], self.place, f"grad_chip_sum_{w}")
                elif w >= W_INQ:
                    self.in_shard = out
                else:
                    self.shard[w] = out
                    if w in self.when_joined:
                        self.joined[w] = self.when_joined[w](out)
        if name == "proj_bwd_x" and result is not None:
            self._land_swap(IN_CHUNKS - 1, result)
            pair, recv = _fly_wait(False, self.flying, self.geos[W_INQ], result, "grad_in_scatter0_wait")
            self.in_shard = _add_four(pair, recv, self.geos[W_INQ], self.place, f"grad_chip_sum_{W_INQ}",
                                      rows=self.in_rows, row0=0, into=self.in_shard)

    def _land_swap(self, q, after):
        w = W_INQ + q
        grad, got = _fly_wait(True, self.swaps[q], self.geos[w], after, f"grad_in_swap{q}_wait")
        self.pair[w] = _add_pair(grad, got, self.geos[w], self.place, f"grad_pair_sum_{w}")

    def after_chunk(self, q, grad):
        w = W_INQ + q
        self.grad[w] = grad
        if q > 0:
            self._land_swap(q - 1, grad)
        if q == 1:
            self.flying = _fly_start(False, self.pair[W_INQ], self.geos[W_INQ], "grad_in_scatter0_start", self.carry)
            self.carry = self.flying[-1]
        self.swaps[q] = _fly_start(True, grad, self.geos[w], f"grad_in_swap{q}_start", self.carry)
        self.carry = self.swaps[q][-1]

    def alone(self, name):
        self.done(name, _comm_only(name, self.host(name)))


CONVW_SHARD = 3 * DFF // N_CHIPS
COND_PAD = 8 * 896
SMALL_SIZES = (("norm1_w", D), ("lb", HW), ("hg_norm_w", DH), ("q_norm_w", DH), ("k_norm_w", DH),
               ("norm2_w", D), ("conv_b", DFF), ("conv_w", 3 * DFF), ("loss", 128))
SMALL_TOTAL = sum(n for _, n in SMALL_SIZES)
STATS_PAD = 8 * 5120


def kernel(x, c, w_ada, b_ada, norm1_w, w_in, lb_logits, hg_norm_w, q_norm_w, k_norm_w, w_out, norm2_w, w_up, conv_w, conv_b, w_down, loss_target, m_w_ada, m_b_ada, m_norm1_w, m_w_in, m_lb_logits, m_hg_norm_w, m_q_norm_w, m_k_norm_w, m_w_out, m_norm2_w, m_w_up, m_conv_w, m_conv_b, m_w_down, v_w_ada, v_b_ada, v_norm1_w, v_w_in, v_lb_logits, v_hg_norm_w, v_q_norm_w, v_k_norm_w, v_w_out, v_norm2_w, v_w_up, v_conv_w, v_conv_b, v_w_down):
    chip = 2 * lax.axis_index("x") + lax.axis_index("y")
    dev = 2 * chip + lax.axis_index("c")

    cond = jnp.concatenate([c, conv_w[0].reshape(1, CONVW_SHARD), jnp.zeros((1, COND_PAD - D - CONVW_SHARD), F32)], axis=1)
    cond_all = _all_gather_rows(cond.reshape(8, COND_PAD // 8), "gather_cond").reshape(N_DEV, COND_PAD)
    c_all = cond_all[:, :D]
    conv_w_full = jnp.concatenate(
        [cond_all[2 * j, D:D + CONVW_SHARD].reshape(3, DFF // N_CHIPS) for j in range(N_CHIPS)], axis=1)

    b_shard = lax.dynamic_slice_in_dim(b_ada, chip * ADA_COLS, ADA_COLS, axis=1)
    mod_cols = _all_gather_rows(_ada_fwd(c_all, w_ada[0], b_shard), "gather_mod")
    mod_all = jnp.concatenate([mod_cols[16 * j:16 * j + 8] for j in range(N_CHIPS)], axis=1)
    mod = lax.dynamic_slice_in_dim(mod_all, dev, 1, axis=0)

    place = jnp.stack([chip, lax.axis_index("c")]).astype(jnp.int32)
    net = _Net([w_in[0], w_out[0], w_up[0], w_down[0]], place)
    net.when_joined = {
        W_OUT: lambda grad: _adamw_sc(w_out[0], grad, m_w_out[0], v_w_out[0], "adamw_sc_w_out"),
        W_DOWN: lambda grad: _adamw_sc(w_down[0], grad, m_w_down[0], v_w_down[0], "adamw_sc_w_down"),
        W_UP: lambda grad: _adamw_sc(w_up[0], grad, m_w_up[0], v_w_up[0], "adamw_sc_w_up"),
    }
    early = {W_OUT: "w_out", W_DOWN: "w_down", W_UP: "w_up"}
    loss_row, grad_x, dmod, small = _sequence_step(
        x[0], loss_target[0], mod, norm1_w, lb_logits, hg_norm_w, q_norm_w, k_norm_w, norm2_w, conv_w_full, conv_b, net)
    small["conv_w"] = small["conv_w"].reshape(1, 3 * DFF)
    small["loss"] = loss_row
    stats = jnp.concatenate([dmod] + [small[k] for k, _ in SMALL_SIZES]
                            + [jnp.zeros((1, STATS_PAD - 6 * D - SMALL_TOTAL), F32)], axis=1)
    stats_all, comm = _all_gather_rows(stats.reshape(8, STATS_PAD // 8), "gather_stats", net.host("gather_stats"))
    net.done("gather_stats", comm)
    stats_all = stats_all.reshape(N_DEV, STATS_PAD)
    last = W_INQ + IN_CHUNKS - 1
    started = _fly_start(False, net.pair[last], net.geos[last], "grad_in_scatter_start")
    dmod_all = stats_all[:, :6 * D] + started[8][0, 0]
    sums = _sum_rows(stats_all[:, 6 * D:6 * D + SMALL_TOTAL], "small_grad_sum")
    g_small, off = {}, 0
    for k, n in SMALL_SIZES:
        g_small[k] = sums[:, off:off + n]
        off += n
    loss = g_small["loss"][0, 0]
    g_b_ada = _sum_rows(dmod_all, "b_ada_grad_sum")
    ada = _ada_bwd_adamw(c_all, lax.dynamic_slice_in_dim(dmod_all, chip * ADA_COLS, ADA_COLS, axis=1),
                         w_ada[0], m_w_ada[0], v_w_ada[0])
    g_w_ada = ada[0]
    pair_last, recv_last = _fly_wait(False, started, net.geos[last], ada[3], "grad_in_scatter_wait")
    net.in_shard = _add_four(pair_last, recv_last, net.geos[last], place, f"grad_chip_sum_{last}",
                             rows=net.in_rows, row0=net._row0(last), into=net.in_shard)
    net.alone("grad_in_join")
    g_out, g_up, g_down = net.shard[W_OUT], net.shard[W_UP], net.shard[W_DOWN]
    g_in = net.in_shard
    g_lb = _lb_grad(lb_logits, g_small["lb"])
    g_conv_w = lax.dynamic_slice_in_dim(g_small["conv_w"].reshape(3, DFF), chip * (DFF // N_CHIPS), DFF // N_CHIPS, axis=1)

    names = ["w_ada", "b_ada", "norm1_w", "w_in", "lb_logits", "hg_norm_w", "q_norm_w", "k_norm_w", "w_out",
             "norm2_w", "w_up", "conv_w", "conv_b", "w_down"]
    lead = {"w_ada", "w_in", "w_out", "w_up", "conv_w", "w_down"}
    w = dict(w_ada=w_ada, b_ada=b_ada, norm1_w=norm1_w, w_in=w_in, lb_logits=lb_logits, hg_norm_w=hg_norm_w,
             q_norm_w=q_norm_w, k_norm_w=k_norm_w, w_out=w_out, norm2_w=norm2_w, w_up=w_up, conv_w=conv_w,
             conv_b=conv_b, w_down=w_down)
    m = dict(w_ada=m_w_ada, b_ada=m_b_ada, norm1_w=m_norm1_w, w_in=m_w_in, lb_logits=m_lb_logits,
             hg_norm_w=m_hg_norm_w, q_norm_w=m_q_norm_w, k_norm_w=m_k_norm_w, w_out=m_w_out, norm2_w=m_norm2_w,
             w_up=m_w_up, conv_w=m_conv_w, conv_b=m_conv_b, w_down=m_w_down)
    v = dict(w_ada=v_w_ada, b_ada=v_b_ada, norm1_w=v_norm1_w, w_in=v_w_in, lb_logits=v_lb_logits,
             hg_norm_w=v_hg_norm_w, q_norm_w=v_q_norm_w, k_norm_w=v_k_norm_w, w_out=v_w_out, norm2_w=v_norm2_w,
             w_up=v_w_up, conv_w=v_conv_w, conv_b=v_conv_b, w_down=v_w_down)
    g = dict(w_ada=g_w_ada, b_ada=g_b_ada, norm1_w=g_small["norm1_w"], w_in=g_in, lb_logits=g_lb,
             hg_norm_w=g_small["hg_norm_w"], q_norm_w=g_small["q_norm_w"], k_norm_w=g_small["k_norm_w"], w_out=g_out,
             norm2_w=g_small["norm2_w"], w_up=g_up, conv_w=g_conv_w, conv_b=g_small["conv_b"], w_down=g_down)

    updated = {early[i]: res for i, res in net.joined.items()}
    updated["w_ada"] = (ada[1], ada[2], ada[3], ada[0])
    grads, deltas, new_m, new_v = [], [], [], []
    for n in names:
        strip = (lambda t: t[0]) if n in lead else (lambda t: t)
        wrap = (lambda t: t[None]) if n in lead else (lambda t: t)
        g_n = g[n]
        if n in updated:
            d_n, m_n, v_n, g_n = updated[n]
        elif n == "w_in":
            d_n, m_n, v_n, g_n = _adamw(strip(w[n]), g_n, strip(m[n]), strip(v[n]), f"adamw_{n}", copy_grad=True)
        else:
            d_n, m_n, v_n = _adamw(strip(w[n]), g_n, strip(m[n]), strip(v[n]), f"adamw_{n}")
        grads.append(wrap(g_n))
        deltas.append(wrap(d_n))
        new_m.append(wrap(m_n))
        new_v.append(wrap(v_n))
    return (loss, grad_x[None], *grads, *deltas, *new_m, *new_v)
```

```python
import functools

import jax
import jax.numpy as jnp
from jax import lax
from jax.experimental import pallas as pl
from jax.experimental.pallas import tpu as pltpu
from jax.experimental.pallas import tpu_sc as plsc

F32 = jnp.float32
BF16 = jnp.bfloat16

S = 2048
D = 2048
H = 8
DH = 128
HW = H * DH
CH = 64
NCH = S // CH
DFF = 5632
INC = 7 * HW
BLK = 128
DILS = (1, 4, 16)
EPS = 1e-6
NEG = -1e30
N_CHIPS = 4
N_DEV = 8

ADAM_LR = 0.001
ADAM_B1 = 0.9
ADAM_B2 = 0.999
ADAM_EPS = 1e-08
ADAM_WD = 0.01
ADAM_STEP = 10
ADAM_BLOCK_BYTES = 1 << 21

V7X_VMEM_BYTES = 64 * 1024 * 1024
VMEM_LIMIT = (V7X_VMEM_BYTES * 3) // 4
MESH = pl.DeviceIdType.MESH
HBM_SPEC = pl.BlockSpec(memory_space=pltpu.HBM)
VMEM_SPEC = pl.BlockSpec(memory_space=pltpu.VMEM)

NN = (((1,), (0,)), ((), ()))
NT = (((1,), (1,)), ((), ()))
TN = (((0,), (0,)), ((), ()))


def _params(n_grid):
    return pltpu.CompilerParams(dimension_semantics=("arbitrary",) * n_grid, vmem_limit_bytes=VMEM_LIMIT)


def _dot(a, b, dims=NN):
    return lax.dot_general(a.astype(BF16), b.astype(BF16), dims, preferred_element_type=F32)


def _dot_f32(a, b):
    return lax.dot_general(a, b, NN, precision=lax.Precision.HIGHEST, preferred_element_type=F32)


def _sigmoid(x):
    return 1.0 / (1.0 + jnp.exp(-x))


def _pick(n, cands):
    for t in cands:
        if n % t == 0:
            return t
    raise ValueError(n)


def _matmul(a, b, mode, out_dtype, name, phase=None, m_blocks=None):
    if mode == "nn":
        (m, k), (_, n) = a.shape, b.shape
    elif mode == "nt":
        (m, k), (n, _) = a.shape, b.shape
    else:
        (k, m), (_, n) = a.shape, b.shape
    tm = _pick(m, (1024, 1408, 512))
    a_block = lambda i: i
    if m_blocks is not None:
        tm, blocks = m_blocks
        m = tm * len(blocks)
        step = blocks[1] - blocks[0] if len(blocks) > 1 else 0
        assert all(blk == blocks[0] + step * i for i, blk in enumerate(blocks))
        a_block = lambda i: blocks[0] + step * i
    tn = _pick(n, (1024, 1408, 512))
    tk = _pick(k, (2048, 2816, 1792, 1024, 512))
    nk = k // tk
    dims = {"nn": NN, "nt": NT, "tn": TN}[mode]

    def body(a_ref, b_ref, o_ref, *acc):
        part = lax.dot_general(a_ref[...], b_ref[...], dims, preferred_element_type=F32)
        if nk == 1:
            o_ref[...] = part.astype(o_ref.dtype)
            return
        acc_ref, kk = acc[0], pl.program_id(2)

        @pl.when(kk == 0)
        def _():
            acc_ref[...] = part

        @pl.when(jnp.logical_and(kk > 0, kk < nk - 1))
        def _():
            acc_ref[...] += part

        @pl.when(kk == nk - 1)
        def _():
            o_ref[...] = (acc_ref[...] + part).astype(o_ref.dtype)

    if mode == "tn":
        a_spec = pl.BlockSpec((tk, tm), lambda i, j, kk: (kk, a_block(i)))
    else:
        a_spec = pl.BlockSpec((tm, tk), lambda i, j, kk: (i, kk))
    if mode == "nt":
        b_spec = pl.BlockSpec((tn, tk), lambda i, j, kk: (j, kk))
    else:
        b_spec = pl.BlockSpec((tk, tn), lambda i, j, kk: (kk, j))
    return _pcall(
        body, [a, b], phase, name=name,
        out_shape=jax.ShapeDtypeStruct((m, n), out_dtype),
        grid=(m // tm, n // tn, nk),
        in_specs=[a_spec, b_spec],
        out_specs=pl.BlockSpec((tm, tn), lambda i, j, kk: (i, j)),
        scratch_shapes=[pltpu.VMEM((tm, tn), F32)] if nk > 1 else [],
        compiler_params=_params(3),
    )


TR = 256


def _row_spec(cols=D):
    return pl.BlockSpec((TR, cols), lambda i: (i, 0))


def _vec_spec(cols=D):
    return pl.BlockSpec((1, cols), lambda i: (0, 0))


def _norm_mod(x, nw, scale, shift, name):
    def body(x_ref, nw_ref, sc_ref, sh_ref, h_ref):
        xv = x_ref[...]
        r = lax.rsqrt(jnp.mean(xv * xv, axis=-1, keepdims=True) + EPS)
        h_ref[...] = ((xv * r) * nw_ref[...] * (1.0 + sc_ref[...]) + sh_ref[...]).astype(BF16)

    return pl.pallas_call(
        body, name=name, out_shape=jax.ShapeDtypeStruct((S, D), BF16), grid=(S // TR,),
        in_specs=[_row_spec(), _vec_spec(), _vec_spec(), _vec_spec()], out_specs=_row_spec(),
        compiler_params=_params(1),
    )(x, nw, scale, shift)


def _resid_norm_mod(x, mix, gate, nw, scale, shift, name):
    def body(x_ref, mix_ref, g_ref, nw_ref, sc_ref, sh_ref, x1_ref, h_ref):
        xv = x_ref[...] + g_ref[...] * mix_ref[...]
        x1_ref[...] = xv
        r = lax.rsqrt(jnp.mean(xv * xv, axis=-1, keepdims=True) + EPS)
        h_ref[...] = ((xv * r) * nw_ref[...] * (1.0 + sc_ref[...]) + sh_ref[...]).astype(BF16)

    return pl.pallas_call(
        body, name=name,
        out_shape=(jax.ShapeDtypeStruct((S, D), F32), jax.ShapeDtypeStruct((S, D), BF16)), grid=(S // TR,),
        in_specs=[_row_spec(), _row_spec(), _vec_spec(), _vec_spec(), _vec_spec(), _vec_spec()],
        out_specs=(_row_spec(), _row_spec()),
        compiler_params=_params(1),
    )(x, mix, gate, nw, scale, shift)


def _norm_mod_bwd(dh, xin, nw, scale, dres, name, mix=None, gate=None, phase=None):
    with_gate = mix is not None

    def body(*refs):
        if with_gate:
            dh_ref, x_ref, nw_ref, sc_ref, dres_ref, mix_ref, g_ref, dx_ref, dsh_ref, dsc_ref, dnw_ref, dg_ref, dmix_ref = refs
        else:
            dh_ref, x_ref, nw_ref, sc_ref, dres_ref, dx_ref, dsh_ref, dsc_ref, dnw_ref = refs
        i = pl.program_id(0)
        dhv = dh_ref[...]
        xv = x_ref[...]
        r = lax.rsqrt(jnp.mean(xv * xv, axis=-1, keepdims=True) + EPS)
        xn = xv * r
        nwv = nw_ref[...]
        one_sc = 1.0 + sc_ref[...]
        dxn = dhv * nwv * one_sc
        dx = dres_ref[...] + r * (dxn - xn * jnp.mean(dxn * xn, axis=-1, keepdims=True))
        dx_ref[...] = dx

        @pl.when(i == 0)
        def _():
            dsh_ref[...] = jnp.zeros_like(dsh_ref)
            dsc_ref[...] = jnp.zeros_like(dsc_ref)
            dnw_ref[...] = jnp.zeros_like(dnw_ref)
            if with_gate:
                dg_ref[...] = jnp.zeros_like(dg_ref)

        dsh_ref[...] += jnp.sum(dhv, axis=0, keepdims=True)
        dsc_ref[...] += jnp.sum(dhv * xn * nwv, axis=0, keepdims=True)
        dnw_ref[...] += jnp.sum(dhv * xn * one_sc, axis=0, keepdims=True)
        if with_gate:
            dg_ref[...] += jnp.sum(dx * mix_ref[...], axis=0, keepdims=True)
            dmix_ref[...] = (dx * g_ref[...]).astype(BF16)

    vec = jax.ShapeDtypeStruct((1, D), F32)
    ins = [dh, xin, nw, scale, dres]
    in_specs = [_row_spec(), _row_spec(), _vec_spec(), _vec_spec(), _row_spec()]
    outs = [jax.ShapeDtypeStruct((S, D), F32), vec, vec, vec]
    out_specs = [_row_spec(), _vec_spec(), _vec_spec(), _vec_spec()]
    if with_gate:
        ins += [mix, gate]
        in_specs += [_row_spec(), _vec_spec()]
        outs += [vec, jax.ShapeDtypeStruct((S, D), BF16)]
        out_specs += [_vec_spec(), _row_spec()]
    return _pcall(
        body, ins, phase, name=name, out_shape=tuple(outs), grid=(S // TR,), in_specs=in_specs,
        out_specs=tuple(out_specs), compiler_params=_params(1),
    )


def _tri(lower):
    row = lax.broadcasted_iota(jnp.int32, (CH, CH), 0)
    col = lax.broadcasted_iota(jnp.int32, (CH, CH), 1)
    return (row >= col) if lower else (col >= row)


def _hgrn_gates(hq, hf, lb):
    sig = _sigmoid(hf)
    f = lb + (1.0 - lb) * sig
    g = jnp.log(f)
    sq = _sigmoid(hq)
    return sig, f, g, 1.0 - f, hq * sq, sq


HG_HP = 2
HG_UNROLL = 16


def _proj_col_spec(group):
    return pl.BlockSpec((S, HG_HP * DH), lambda h: (0, group * (H // HG_HP) + h))


def _hgrn_fwd(proj, lb_logits, norm_w, phase=None):
    def body(hq_ref, hf_ref, hi_ref, hg_ref, lbl_ref, nw_ref, out_ref, oraw_ref, st_ref, s_ref):
        nw = nw_ref[...]
        lower = _tri(True)
        ltri = lower.astype(F32)
        s_ref[...] = jnp.zeros_like(s_ref)

        def chunk(n, carry):
            rows = pl.ds(pl.multiple_of(n * CH, CH), CH)
            for j in range(HG_HP):
                cols = slice(j * DH, (j + 1) * DH)
                lb = 1.0 / (1.0 + jnp.exp(lbl_ref[1:2, cols] - lbl_ref[0:1, cols]))
                hq, hf, v, hg = hq_ref[rows, cols], hf_ref[rows, cols], hi_ref[rows, cols], hg_ref[rows, cols]
                _, _, g, kk, q, _ = _hgrn_gates(hq, hf, lb)
                gc = _dot_f32(ltri, g)
                gl = jnp.sum(g, axis=0, keepdims=True)
                qe = q * jnp.exp(gc)
                ke = kk * jnp.exp(gl - gc)
                qh = q * jnp.exp(gc - 0.5 * gl)
                kh = kk * jnp.exp(0.5 * gl - gc)
                st = s_ref[j]
                st_ref[j, pl.ds(n, 1)] = st[None]
                a = jnp.where(lower, _dot(qh, kh, NT), 0.0)
                o = _dot(qe, st, NT) + _dot(a, v)
                s_ref[j] = st * jnp.exp(gl) + _dot(v, ke, TN)
                oraw_ref[rows, cols] = o
                rs = lax.rsqrt(jnp.mean(o * o, axis=-1, keepdims=True) + EPS)
                out_ref[rows, cols] = (o * rs * nw * (hg * _sigmoid(hg))).astype(BF16)
            return carry

        lax.fori_loop(0, NCH, chunk, 0, unroll=HG_UNROLL)

    head_spec = pl.BlockSpec((S, HG_HP * DH), lambda h: (0, h))
    return _pcall(
        body, [proj, proj, proj, proj, lb_logits, norm_w], phase, name="hgrn_fwd",
        out_shape=(jax.ShapeDtypeStruct((S, 2 * HW), BF16), jax.ShapeDtypeStruct((S, HW), F32),
                   jax.ShapeDtypeStruct((H, NCH, DH, DH), F32)),
        grid=(H // HG_HP,),
        in_specs=[_proj_col_spec(0), _proj_col_spec(1), _proj_col_spec(2), _proj_col_spec(3),
                  pl.BlockSpec((2, HG_HP * DH), lambda h: (0, h)), pl.BlockSpec((1, DH), lambda h: (0, 0))],
        out_specs=(head_spec, head_spec, pl.BlockSpec((HG_HP, NCH, DH, DH), lambda h: (h, 0, 0, 0))),
        scratch_shapes=[pltpu.VMEM((HG_HP, DH, DH), F32)],
        compiler_params=_params(1),
    )


def _hgrn_bwd(proj, lb_logits, norm_w, oraw, states, dout, phase=None):
    def body(hq_ref, hf_ref, hi_ref, hg_ref, lbl_ref, nw_ref, oraw_ref, st_ref, do_ref,
             dhq_ref, dhf_ref, dhi_ref, dhg_ref, dlb_ref, dnw_ref, ds_ref, acc_ref):
        h = pl.program_id(0)
        nw = nw_ref[...]
        lower = _tri(True)
        ltri = lower.astype(F32)
        utri = _tri(False).astype(F32)
        last = lax.broadcasted_iota(jnp.int32, (CH, DH), 0) == CH - 1
        ds_ref[...] = jnp.zeros_like(ds_ref)
        acc_ref[...] = jnp.zeros_like(acc_ref)

        @pl.when(h == 0)
        def _():
            dnw_ref[...] = jnp.zeros_like(dnw_ref)

        def chunk(i, carry):
            n = NCH - 1 - i
            rows = pl.ds(pl.multiple_of(n * CH, CH), CH)
            for j in range(HG_HP):
                cols = slice(j * DH, (j + 1) * DH)
                lb = 1.0 / (1.0 + jnp.exp(lbl_ref[1:2, cols] - lbl_ref[0:1, cols]))
                hq, hf, v, hg = hq_ref[rows, cols], hf_ref[rows, cols], hi_ref[rows, cols], hg_ref[rows, cols]
                sig, f, g, kk, q, sq = _hgrn_gates(hq, hf, lb)
                gc = _dot_f32(ltri, g)
                gl = jnp.sum(g, axis=0, keepdims=True)
                eg = jnp.exp(gc)
                ek = jnp.exp(gl - gc)
                gam = jnp.exp(gl)
                ph = jnp.exp(gc - 0.5 * gl)
                pk = jnp.exp(0.5 * gl - gc)
                qe, ke = q * eg, kk * ek
                qh, kh = q * ph, kk * pk
                st = st_ref[j, pl.ds(n, 1)][0]
                dst = ds_ref[j]
                a = jnp.where(lower, _dot(qh, kh, NT), 0.0)
                o = oraw_ref[rows, cols]
                rs = lax.rsqrt(jnp.mean(o * o, axis=-1, keepdims=True) + EPS)
                xh = o * rs
                sg = _sigmoid(hg)
                dgo = do_ref[rows, cols]
                d_on = dgo * (hg * sg)
                dhg_ref[rows, cols] = (dgo * xh * nw * (sg * (1.0 + hg * (1.0 - sg)))).astype(BF16)
                acc_ref[j, 1:2, :] += jnp.sum(d_on * xh, axis=0, keepdims=True)
                dy = d_on * nw
                do = rs * (dy - xh * jnp.mean(dy * xh, axis=-1, keepdims=True))
                dqe = _dot(do, st)
                da = jnp.where(lower, _dot(do, v, NT), 0.0)
                dv = _dot(a, do, TN) + _dot(ke, dst, NT)
                dke = _dot(v, dst)
                dgam = jnp.sum(dst * st, axis=0, keepdims=True)
                dqh = lax.dot_general(da, kh, NN, precision=lax.Precision.HIGH, preferred_element_type=F32)
                dkh = lax.dot_general(da, qh, TN, precision=lax.Precision.HIGH, preferred_element_type=F32)
                dq = dqh * ph + dqe * eg
                dk = dkh * pk + dke * ek
                dgl = jnp.sum(dke * ke, axis=0, keepdims=True) + dgam * gam
                dgc = dqh * qh - dkh * kh + dqe * qe - dke * ke + jnp.where(last, dgl, 0.0)
                dg = _dot_f32(utri, dgc)
                df = dg / f - dk
                dhf_ref[rows, cols] = (df * (1.0 - lb) * sig * (1.0 - sig)).astype(BF16)
                acc_ref[j, 0:1, :] += jnp.sum(df * (1.0 - sig), axis=0, keepdims=True)
                dhq_ref[rows, cols] = (dq * (sq * (1.0 + hq * (1.0 - sq)))).astype(BF16)
                dhi_ref[rows, cols] = dv.astype(BF16)
                ds_ref[j] = dst * gam + _dot(do, qe, TN)
            return carry

        lax.fori_loop(0, NCH, chunk, 0, unroll=HG_UNROLL)
        for j in range(HG_HP):
            dlb_ref[:, j * DH:(j + 1) * DH] = acc_ref[j, 0:1, :]
            dnw_ref[...] += acc_ref[j, 1:2, :]

    head_spec = pl.BlockSpec((S, HG_HP * DH), lambda h: (0, h))
    head_out = jax.ShapeDtypeStruct((S, HW), BF16)
    return _pcall(
        body, [proj, proj, proj, proj, lb_logits, norm_w, oraw, states, dout], phase, name="hgrn_bwd",
        out_shape=(head_out, head_out, head_out, head_out,
                   jax.ShapeDtypeStruct((1, HW), F32), jax.ShapeDtypeStruct((1, DH), F32)),
        grid=(H // HG_HP,),
        in_specs=[_proj_col_spec(0), _proj_col_spec(1), _proj_col_spec(2), _proj_col_spec(3),
                  pl.BlockSpec((2, HG_HP * DH), lambda h: (0, h)), pl.BlockSpec((1, DH), lambda h: (0, 0)),
                  head_spec, pl.BlockSpec((HG_HP, NCH, DH, DH), lambda h: (h, 0, 0, 0)), head_spec],
        out_specs=(head_spec, head_spec, head_spec, head_spec,
                   pl.BlockSpec((1, HG_HP * DH), lambda h: (0, h)), pl.BlockSpec((1, DH), lambda h: (0, 0))),
        scratch_shapes=[pltpu.VMEM((HG_HP, DH, DH), F32), pltpu.VMEM((HG_HP, 8, DH), F32)],
        compiler_params=_params(1),
    )


ATT_SCALE = DH ** -0.5
HEADS_PER_STEP = {1: 8, 4: 1, 16: 1}


def _sub_rows(ref, r, dil, cols):
    if dil == 1:
        return ref[:, cols]
    return ref[pl.ds(r, BLK, stride=dil), cols]


def _set_sub_rows(ref, r, dil, cols, val):
    if dil == 1:
        ref[:, cols] = val
    else:
        ref[pl.ds(r, BLK, stride=dil), cols] = val


def _slopes(dil):
    hp = HEADS_PER_STEP[dil]
    s = jnp.asarray([dil * 2.0 ** (-(h + 1)) for h in range(H)], F32)
    return jnp.broadcast_to(s[:, None], (H, DH)).reshape(H // hp, hp, DH)


def _rms_rows(x):
    rs = lax.rsqrt(jnp.mean(x * x, axis=-1, keepdims=True) + EPS)
    return x * rs, rs


def _att_masks():
    qi = lax.broadcasted_iota(jnp.int32, (BLK, BLK), 0)
    kj = lax.broadcasted_iota(jnp.int32, (BLK, BLK), 1)
    steps_c = qi - kj
    steps_p = qi - kj + BLK
    return steps_c, steps_p, steps_c >= 0, steps_p <= BLK


def _qk_prep(proj, q_w, k_w):
    def body(q_ref, k_ref, qw_ref, kw_ref, qn_ref, kn_ref):
        for h in range(H):
            cols = slice(h * DH, (h + 1) * DH)
            qn_ref[:, cols] = _rms_rows(q_ref[:, cols])[0] * qw_ref[...]
            kn_ref[:, cols] = _rms_rows(k_ref[:, cols])[0] * kw_ref[...]

    out = jax.ShapeDtypeStruct((S, HW), F32)
    wspec = pl.BlockSpec((1, DH), lambda i: (0, 0))
    return pl.pallas_call(
        body, name="qk_prep", out_shape=(out, out), grid=(S // TR,),
        in_specs=[pl.BlockSpec((TR, HW), lambda i: (i, 4)), pl.BlockSpec((TR, HW), lambda i: (i, 5)), wspec, wspec],
        out_specs=(_row_spec(HW), _row_spec(HW)),
        compiler_params=_params(1),
    )(proj, proj, q_w, k_w)


def _qk_norm_bwd(proj, dqn, dkn, dv, q_w, k_w):
    def body(q_ref, k_ref, dqn_ref, dkn_ref, dv_ref, qw_ref, kw_ref, dq_ref, dk_ref, dvo_ref, dqw_ref, dkw_ref):
        i = pl.program_id(0)

        @pl.when(i == 0)
        def _():
            dqw_ref[...] = jnp.zeros_like(dqw_ref)
            dkw_ref[...] = jnp.zeros_like(dkw_ref)

        dvo_ref[...] = dv_ref[...].astype(BF16)
        for x_ref, d_ref, w_ref, o_ref, dw_ref in ((q_ref, dqn_ref, qw_ref, dq_ref, dqw_ref),
                                                   (k_ref, dkn_ref, kw_ref, dk_ref, dkw_ref)):
            for h in range(H):
                cols = slice(h * DH, (h + 1) * DH)
                xh, rs = _rms_rows(x_ref[:, cols])
                d = d_ref[:, cols]
                dw_ref[...] += jnp.sum(d * xh, axis=0, keepdims=True)
                dy = d * w_ref[...]
                o_ref[:, cols] = (rs * (dy - xh * jnp.mean(dy * xh, axis=-1, keepdims=True))).astype(BF16)

    big = jax.ShapeDtypeStruct((S, HW), BF16)
    small = jax.ShapeDtypeStruct((1, DH), F32)
    wspec = pl.BlockSpec((1, DH), lambda i: (0, 0))
    return pl.pallas_call(
        body, name="qk_norm_bwd", out_shape=(big, big, big, small, small), grid=(S // TR,),
        in_specs=[pl.BlockSpec((TR, HW), lambda i: (i, 4)), pl.BlockSpec((TR, HW), lambda i: (i, 5)),
                  _row_spec(HW), _row_spec(HW), _row_spec(HW), wspec, wspec],
        out_specs=(_row_spec(HW), _row_spec(HW), _row_spec(HW), wspec, wspec),
        compiler_params=_params(1),
    )(proj, proj, dqn, dkn, dv, q_w, k_w)


def _attn_delta(do, o):
    def body(do_ref, o_ref, d_ref):
        for h in range(H):
            cols = slice(h * DH, (h + 1) * DH)
            d_ref[:, cols] = jnp.broadcast_to(jnp.sum(do_ref[:, cols] * o_ref[:, cols], axis=-1, keepdims=True), (TR, DH))

    return pl.pallas_call(
        body, name="attn_delta", out_shape=jax.ShapeDtypeStruct((S, HW), F32), grid=(S // TR,),
        in_specs=[pl.BlockSpec((TR, HW), lambda i: (i, 1)), _row_spec(HW)], out_specs=_row_spec(HW),
        compiler_params=_params(1),
    )(do, o)


def _attn_fwd(proj, qn, kn, dil, phase=None):
    hp = HEADS_PER_STEP[dil]
    rows, ng, nb = BLK * dil, H // hp, S // (BLK * dil)

    def body(q_ref, kc_ref, kp_ref, vc_ref, vp_ref, sl_ref, o_ref, lse_ref):
        n = pl.program_id(0)
        steps_c, steps_p, ok_c, ok_p = _att_masks()
        ok_p = jnp.logical_and(ok_p, n > 0)
        fc, fp = steps_c.astype(F32), steps_p.astype(F32)
        for hh in range(hp):
            cols = slice(hh * DH, (hh + 1) * DH)
            sl = sl_ref[0, hh:hh + 1, :]
            for r in range(dil):
                sub = lambda ref: _sub_rows(ref, r, dil, cols)
                qv = sub(q_ref)
                sc = jnp.where(ok_c, _dot(qv, sub(kc_ref), NT) * ATT_SCALE - sl * fc, NEG)
                sp = jnp.where(ok_p, _dot(qv, sub(kp_ref), NT) * ATT_SCALE - sl * fp, NEG)
                m = jnp.maximum(jnp.max(sc, axis=-1, keepdims=True), jnp.max(sp, axis=-1, keepdims=True))
                pc, pp = jnp.exp(sc - m), jnp.exp(sp - m)
                den = jnp.sum(pc, axis=-1, keepdims=True) + jnp.sum(pp, axis=-1, keepdims=True)
                o = (_dot(pc, sub(vc_ref)) + _dot(pp, sub(vp_ref))) / den
                _set_sub_rows(o_ref, r, dil, cols, o)
                _set_sub_rows(lse_ref, r, dil, cols, jnp.broadcast_to(m + jnp.log(den), (BLK, DH)))

    cur = pl.BlockSpec((rows, hp * DH), lambda n, g: (n, g))
    prev = pl.BlockSpec((rows, hp * DH), lambda n, g: (jnp.maximum(n - 1, 0), g))
    vcur = pl.BlockSpec((rows, hp * DH), lambda n, g: (n, 6 * ng + g))
    vprev = pl.BlockSpec((rows, hp * DH), lambda n, g: (jnp.maximum(n - 1, 0), 6 * ng + g))
    out = jax.ShapeDtypeStruct((S, HW), F32)
    return _pcall(
        body, [qn, kn, kn, proj, proj, _slopes(dil)], phase,
        name=f"attn_fwd_d{dil}", out_shape=(out, out), grid=(nb, ng),
        in_specs=[cur, cur, prev, vcur, vprev, pl.BlockSpec((1, hp, DH), lambda n, g: (g, 0, 0))],
        out_specs=(cur, cur),
        compiler_params=_params(2),
    )


def _attn_merge(outs, lses, cat):
    def body(o1, o2, o3, l1, l2, l3, cat_ref, ob_ref, of_ref, lse_ref):
        a, b, c = l1[...], l2[...], l3[...]
        m = jnp.maximum(jnp.maximum(a, b), c)
        ea, eb, ec = jnp.exp(a - m), jnp.exp(b - m), jnp.exp(c - m)
        den = ea + eb + ec
        o = (ea * o1[...] + eb * o2[...] + ec * o3[...]) / den
        ob_ref[...] = o.astype(BF16)
        of_ref[...] = o
        lse_ref[...] = m + jnp.log(den)

    f = jax.ShapeDtypeStruct((S, HW), F32)
    return pl.pallas_call(
        body, name="attn_merge", out_shape=(jax.ShapeDtypeStruct((S, 2 * HW), BF16), f, f), grid=(S // TR,),
        in_specs=[_row_spec(HW)] * 6 + [pl.BlockSpec(memory_space=pl.ANY)],
        out_specs=(pl.BlockSpec((TR, HW), lambda i: (i, 1)), _row_spec(HW), _row_spec(HW)),
        input_output_aliases={6: 0},
        compiler_params=_params(1),
    )(*outs, *lses, cat)


def _attn_bwd(proj, qn, kn, lse, delta, do, dil, acc, phase=None):
    hp = HEADS_PER_STEP[dil]
    rows, ng, nb = BLK * dil, H // hp, S // (BLK * dil)
    has_acc = acc is not None

    def body(*refs):
        q_ref, qn_ref, kp_ref, kc_ref, vp_ref, vc_ref, l_ref, ln_ref, d_ref, dn_ref, do_ref, don_ref, sl_ref = refs[:13]
        refs = refs[13:]
        if has_acc:
            aq_ref, ak_ref, av_ref = refs[:3]
            refs = refs[3:]
        dq_ref, dk_ref, dv_ref = refs
        m = pl.program_id(0)
        steps_c, steps_p, ok_c, ok_p = _att_masks()
        ok_prev = jnp.logical_and(ok_p, m > 0)
        ok_next = jnp.logical_and(ok_p, m < nb - 1)
        fc, fp = steps_c.astype(F32), steps_p.astype(F32)
        for hh in range(hp):
            cols = slice(hh * DH, (hh + 1) * DH)
            sl = sl_ref[0, hh:hh + 1, :]
            for r in range(dil):
                sub = lambda ref: _sub_rows(ref, r, dil, cols)
                qv, qnv, kcv, kpv = sub(q_ref), sub(qn_ref), sub(kc_ref), sub(kp_ref)
                vc, vp = sub(vc_ref), sub(vp_ref)
                dov, donv = sub(do_ref), sub(don_ref)
                lse_m, lse_n = sub(l_ref)[:, 0:1], sub(ln_ref)[:, 0:1]
                delta_m, delta_n = sub(d_ref)[:, 0:1], sub(dn_ref)[:, 0:1]
                p_c = jnp.where(ok_c, jnp.exp(_dot(qv, kcv, NT) * ATT_SCALE - sl * fc - lse_m), 0.0)
                p_p = jnp.where(ok_prev, jnp.exp(_dot(qv, kpv, NT) * ATT_SCALE - sl * fp - lse_m), 0.0)
                p_n = jnp.where(ok_next, jnp.exp(_dot(qnv, kcv, NT) * ATT_SCALE - sl * fp - lse_n), 0.0)
                ds_c = p_c * (_dot(dov, vc, NT) - delta_m)
                ds_p = p_p * (_dot(dov, vp, NT) - delta_m)
                ds_n = p_n * (_dot(donv, vc, NT) - delta_n)
                dqn = (_dot(ds_c, kcv) + _dot(ds_p, kpv)) * ATT_SCALE
                dkn = (_dot(ds_c, qv, TN) + _dot(ds_n, qnv, TN)) * ATT_SCALE
                dv = _dot(p_c, dov, TN) + _dot(p_n, donv, TN)
                if has_acc:
                    dqn, dkn, dv = dqn + sub(aq_ref), dkn + sub(ak_ref), dv + sub(av_ref)
                _set_sub_rows(dq_ref, r, dil, cols, dqn)
                _set_sub_rows(dk_ref, r, dil, cols, dkn)
                _set_sub_rows(dv_ref, r, dil, cols, dv)

    def shifted(shift, m):
        if shift < 0:
            return jnp.maximum(m - 1, 0)
        if shift > 0:
            return jnp.minimum(m + 1, nb - 1)
        return m

    def spec(shift, group=0):
        return pl.BlockSpec((rows, hp * DH), lambda m, g: (shifted(shift, m), group * ng + g))

    ins = [qn, qn, kn, kn, proj, proj, lse, lse, delta, delta, do, do, _slopes(dil)]
    in_specs = [spec(0), spec(1), spec(-1), spec(0), spec(-1, 6), spec(0, 6), spec(0), spec(1), spec(0), spec(1),
                spec(0, 1), spec(1, 1), pl.BlockSpec((1, hp, DH), lambda m, g: (g, 0, 0))]
    if has_acc:
        ins += list(acc)
        in_specs += [spec(0)] * 3
    big = jax.ShapeDtypeStruct((S, HW), F32)
    return _pcall(
        body, ins, phase, name=f"attn_bwd_d{dil}", out_shape=(big, big, big), grid=(nb, ng),
        in_specs=in_specs, out_specs=(spec(0),) * 3,
        compiler_params=_params(2),
    )


TC = 512
NCT = DFF // TC


def _shift_rows(a, k):
    rows = lax.broadcasted_iota(jnp.int32, a.shape, 0)
    rolled = pltpu.roll(a, k % S, 0)
    if k > 0:
        return jnp.where(rows >= k, rolled, 0.0)
    return jnp.where(rows < S + k, rolled, 0.0)


def _conv_pre(a, w_ref, b_ref):
    return b_ref[...] + w_ref[0:1, :] * _shift_rows(a, 2) + w_ref[1:2, :] * _shift_rows(a, 1) + w_ref[2:3, :] * a


def _conv_gate_fwd(u, conv_w, conv_b, name, tiles=(0, NCT), into=None, phase=None):
    t0, t1 = tiles

    def body(a_ref, g_ref, w_ref, b_ref, *rest):
        pre = _conv_pre(a_ref[...].astype(F32), w_ref, b_ref)
        rest[-1][...] = (pre * _sigmoid(pre) * g_ref[...].astype(F32)).astype(BF16)

    args = [u, u, conv_w, conv_b]
    in_specs = [pl.BlockSpec((S, TC), lambda i: (0, t0 + i)), pl.BlockSpec((S, TC), lambda i: (0, NCT + t0 + i)),
                pl.BlockSpec((3, TC), lambda i: (0, t0 + i)), pl.BlockSpec((1, TC), lambda i: (0, t0 + i))]
    kw = {}
    if into is not None:
        args.append(into)
        in_specs.append(pl.BlockSpec(memory_space=pl.ANY))
        kw["input_output_aliases"] = {4: 0}
    return _pcall(
        body, args, phase, name=name, out_shape=jax.ShapeDtypeStruct((S, DFF), BF16), grid=(t1 - t0,),
        in_specs=in_specs, out_specs=pl.BlockSpec((S, TC), lambda i: (0, t0 + i)),
        compiler_params=_params(1), **kw,
    )


def _conv_gate_bwd(dy, u, conv_w, conv_b, phase=None):
    def body(dy_ref, a_ref, g_ref, w_ref, b_ref, du_ref, db_ref, dw_ref, da_buf, dg_buf, sems):
        i = pl.program_id(0)
        slot = i % 2

        def writes(step, s):
            col = pl.multiple_of(step * TC, TC)
            return (pltpu.make_async_copy(da_buf.at[s], du_ref.at[:, pl.ds(col, TC)], sems.at[0, s]),
                    pltpu.make_async_copy(dg_buf.at[s], du_ref.at[:, pl.ds(DFF + col, TC)], sems.at[1, s]))

        @pl.when(i >= 2)
        def _():
            for cp in writes(i - 2, slot):
                cp.wait()

        a = a_ref[...].astype(F32)
        dyv = dy_ref[...].astype(F32)
        pre = _conv_pre(a, w_ref, b_ref)
        sg = _sigmoid(pre)
        dg_buf[slot] = (dyv * pre * sg).astype(BF16)
        dpre = dyv * g_ref[...].astype(F32) * (sg * (1.0 + pre * (1.0 - sg)))
        da = w_ref[2:3, :] * dpre + w_ref[1:2, :] * _shift_rows(dpre, -1) + w_ref[0:1, :] * _shift_rows(dpre, -2)
        da_buf[slot] = da.astype(BF16)
        for cp in writes(i, slot):
            cp.start()
        db_ref[...] = jnp.sum(dpre, axis=0, keepdims=True)
        dw_ref[0:1, :] = jnp.sum(dpre * _shift_rows(a, 2), axis=0, keepdims=True)
        dw_ref[1:2, :] = jnp.sum(dpre * _shift_rows(a, 1), axis=0, keepdims=True)
        dw_ref[2:3, :] = jnp.sum(dpre * a, axis=0, keepdims=True)

        @pl.when(i == NCT - 1)
        def _():
            for cp in writes(i - 1, 1 - slot) + writes(i, slot):
                cp.wait()

    col = pl.BlockSpec((S, TC), lambda i: (0, i))
    return _pcall(
        body, [dy, u, u, conv_w, conv_b], phase, name="conv_gate_bwd",
        out_shape=(jax.ShapeDtypeStruct((S, 2 * DFF), BF16), jax.ShapeDtypeStruct((1, DFF), F32),
                   jax.ShapeDtypeStruct((3, DFF), F32)),
        grid=(NCT,),
        in_specs=[col, col, pl.BlockSpec((S, TC), lambda i: (0, NCT + i)),
                  pl.BlockSpec((3, TC), lambda i: (0, i)), pl.BlockSpec((1, TC), lambda i: (0, i))],
        out_specs=(pl.BlockSpec(memory_space=pl.ANY), pl.BlockSpec((1, TC), lambda i: (0, i)),
                   pl.BlockSpec((3, TC), lambda i: (0, i))),
        scratch_shapes=[pltpu.VMEM((2, S, TC), BF16), pltpu.VMEM((2, S, TC), BF16), pltpu.SemaphoreType.DMA((2, 2))],
        compiler_params=_params(1),
    )


def _out_loss(z, x1, target, gate):
    def body(z_ref, x_ref, t_ref, g_ref, do_ref, dz_ref, dg_ref, loss_ref):
        i = pl.program_id(0)
        zv = z_ref[...]
        gv = g_ref[...]
        err = x_ref[...] + gv * zv - t_ref[...]
        dout = err * (1.0 / D)
        do_ref[...] = dout
        dz_ref[...] = (dout * gv).astype(BF16)

        @pl.when(i == 0)
        def _():
            dg_ref[...] = jnp.zeros_like(dg_ref)
            loss_ref[...] = jnp.zeros_like(loss_ref)

        dg_ref[...] += jnp.sum(dout * zv, axis=0, keepdims=True)
        part = jnp.sum(jnp.sum(err * err, axis=0, keepdims=True), axis=-1, keepdims=True) * (0.5 / D)
        lane = lax.broadcasted_iota(jnp.int32, (1, 128), 1)
        loss_ref[...] += jnp.where(lane == 0, part, 0.0)

    return pl.pallas_call(
        body, name="out_loss",
        out_shape=(jax.ShapeDtypeStruct((S, D), F32), jax.ShapeDtypeStruct((S, D), BF16),
                   jax.ShapeDtypeStruct((1, D), F32), jax.ShapeDtypeStruct((1, 128), F32)),
        grid=(S // TR,),
        in_specs=[_row_spec(), _row_spec(), _row_spec(), _vec_spec()],
        out_specs=(_row_spec(), _row_spec(), _vec_spec(), _vec_spec(128)),
        compiler_params=_params(1),
    )(z, x1, target, gate)


ADA_COLS = 6 * D // N_CHIPS
TA = 512


def _ada_fwd(c_all, w_shard, b_shard):
    def body(c_ref, w_ref, b_ref, o_ref):
        cv = c_ref[...]
        o_ref[...] = _dot_f32(cv * _sigmoid(cv), w_ref[...]) + b_ref[...]

    return pl.pallas_call(
        body, name="ada_fwd", out_shape=jax.ShapeDtypeStruct((N_DEV, ADA_COLS), F32), grid=(ADA_COLS // TA,),
        in_specs=[pl.BlockSpec((N_DEV, D), lambda j: (0, 0)), pl.BlockSpec((D, TA), lambda j: (0, j)),
                  pl.BlockSpec((1, TA), lambda j: (0, j))],
        out_specs=pl.BlockSpec((N_DEV, TA), lambda j: (0, j)),
        compiler_params=_params(1),
    )(c_all, w_shard, b_shard)


ADA_ROWS = 128


def _ada_bwd_adamw(c_all, dmod_shard, w, m, v):
    def body(c_ref, d_ref, w_ref, m_ref, v_ref, g_ref, dl_ref, mo_ref, vo_ref):
        cv = c_ref[...]
        gv = lax.dot_general(cv * _sigmoid(cv), d_ref[...], TN, precision=lax.Precision.HIGHEST,
                             preferred_element_type=F32)
        g_ref[...] = gv
        m2 = ADAM_B1 * m_ref[...] + (1.0 - ADAM_B1) * gv
        v2 = ADAM_B2 * v_ref[...] + (1.0 - ADAM_B2) * (gv * gv)
        m_hat = m2 / (1.0 - ADAM_B1 ** ADAM_STEP)
        v_hat = v2 / (1.0 - ADAM_B2 ** ADAM_STEP)
        dl_ref[...] = -ADAM_LR * (m_hat / (jnp.sqrt(v_hat) + ADAM_EPS) + ADAM_WD * w_ref[...])
        mo_ref[...] = m2
        vo_ref[...] = v2

    spec = pl.BlockSpec((ADA_ROWS, ADA_COLS), lambda i: (i, 0))
    out = jax.ShapeDtypeStruct((D, ADA_COLS), F32)
    return pl.pallas_call(
        body, name="ada_bwd_adamw", out_shape=(out,) * 4, grid=(D // ADA_ROWS,),
        in_specs=[pl.BlockSpec((N_DEV, ADA_ROWS), lambda i: (0, i)), pl.BlockSpec((N_DEV, ADA_COLS), lambda i: (0, 0)),
                  spec, spec, spec],
        out_specs=(spec,) * 4,
        compiler_params=_params(1),
    )(c_all, dmod_shard, w, m, v)


def _sum_rows(g, name):
    rows, n = g.shape

    def body(g_ref, o_ref):
        acc = g_ref[0:1, :]
        for i in range(1, rows):
            acc = acc + g_ref[i:i + 1, :]
        o_ref[...] = acc

    return pl.pallas_call(
        body, name=name, out_shape=jax.ShapeDtypeStruct((1, n), F32),
        in_specs=[VMEM_SPEC], out_specs=VMEM_SPEC,
    )(g)


def _lb_grad(lb_logits, dlb):
    def body(l_ref, d_ref, o_ref):
        p = 1.0 / (1.0 + jnp.exp(l_ref[1:2, :] - l_ref[0:1, :]))
        t = d_ref[...] * p * (1.0 - p)
        o_ref[0:1, :] = t
        o_ref[1:2, :] = -t

    return pl.pallas_call(
        body, name="lb_grad", out_shape=jax.ShapeDtypeStruct((2, HW), F32),
        in_specs=[VMEM_SPEC, VMEM_SPEC], out_specs=VMEM_SPEC,
    )(lb_logits, dlb)


def _adamw(w, g, m, v, name, copy_grad=False):
    rows, cols = w.shape
    tr = rows
    if rows * cols * 4 > ADAM_BLOCK_BYTES:
        tr = _pick(rows, [t for t in (256, 128, 64, 32, 16, 8) if t * cols * 4 <= ADAM_BLOCK_BYTES])

    def body(w_ref, g_ref, m_ref, v_ref, d_ref, mo_ref, vo_ref, *go_ref):
        gv = g_ref[...]
        if copy_grad:
            go_ref[0][...] = gv
        m2 = ADAM_B1 * m_ref[...] + (1.0 - ADAM_B1) * gv
        v2 = ADAM_B2 * v_ref[...] + (1.0 - ADAM_B2) * (gv * gv)
        m_hat = m2 / (1.0 - ADAM_B1 ** ADAM_STEP)
        v_hat = v2 / (1.0 - ADAM_B2 ** ADAM_STEP)
        d_ref[...] = -ADAM_LR * (m_hat / (jnp.sqrt(v_hat) + ADAM_EPS) + ADAM_WD * w_ref[...])
        mo_ref[...] = m2
        vo_ref[...] = v2

    spec = pl.BlockSpec((tr, cols), lambda i: (i, 0))
    out = jax.ShapeDtypeStruct((rows, cols), F32)
    n_out = 4 if copy_grad else 3
    return pl.pallas_call(
        body, name=name, out_shape=(out,) * n_out, grid=(rows // tr,),
        in_specs=[spec] * 4, out_specs=(spec,) * n_out,
        compiler_params=_params(1),
    )(w, g, m, v)


SC_TILES = 32
SC_LANES = 16
SC_COL_PARTS = 2
SC_CHUNK_ROWS = 8


def _adamw_sc(w, g, m, v, name):
    rows, cols = w.shape
    band, part = rows // (SC_TILES // SC_COL_PARTS), cols // SC_COL_PARTS
    assert band % SC_CHUNK_ROWS == 0 and part % SC_LANES == 0, (rows, cols)

    def body(w_hbm, g_hbm, m_hbm, v_hbm, d_hbm, mo_hbm, vo_hbm, go_hbm, wb, gb, mb, vb, db):
        tile = lax.axis_index("sc_subcore") * 2 + lax.axis_index("sc_core")
        row0 = (tile // SC_COL_PARTS) * band
        col0 = (tile % SC_COL_PARTS) * part

        @pl.loop(0, band, step=SC_CHUNK_ROWS)
        def _(r):
            at = lambda ref: ref.at[pl.ds(row0 + r, SC_CHUNK_ROWS), pl.ds(col0, part)]
            pltpu.sync_copy(at(w_hbm), wb)
            pltpu.sync_copy(at(g_hbm), gb)
            pltpu.sync_copy(gb, at(go_hbm))
            pltpu.sync_copy(at(m_hbm), mb)
            pltpu.sync_copy(at(v_hbm), vb)

            @pl.loop(0, SC_CHUNK_ROWS)
            def _(i):
                @pl.loop(0, part, step=SC_LANES)
                def _(j):
                    sl = (i, pl.ds(j, SC_LANES))
                    gv = gb[sl]
                    m2 = ADAM_B1 * mb[sl] + (1.0 - ADAM_B1) * gv
                    v2 = ADAM_B2 * vb[sl] + (1.0 - ADAM_B2) * (gv * gv)
                    m_hat = m2 / (1.0 - ADAM_B1 ** ADAM_STEP)
                    v_hat = v2 / (1.0 - ADAM_B2 ** ADAM_STEP)
                    db[sl] = -ADAM_LR * (m_hat / (jnp.sqrt(v_hat) + ADAM_EPS) + ADAM_WD * wb[sl])
                    mb[sl] = m2
                    vb[sl] = v2

            pltpu.sync_copy(db, at(d_hbm))
            pltpu.sync_copy(mb, at(mo_hbm))
            pltpu.sync_copy(vb, at(vo_hbm))

    out = jax.ShapeDtypeStruct((rows, cols), F32)
    buf = pltpu.VMEM((SC_CHUNK_ROWS, part), F32)
    return pl.kernel(
        body, name=name, out_type=(out, out, out, out),
        mesh=plsc.VectorSubcoreMesh(core_axis_name="sc_core", subcore_axis_name="sc_subcore"),
        scratch_types=[buf] * 5,
    )(w, g, m, v)


W_IN, W_OUT, W_UP, W_DOWN = range(4)
IN_CHUNKS = 2
W_INQ = 4


def _sequence_step(x, target, mod, norm1_w, lb_logits, hg_norm_w, q_norm_w, k_norm_w, norm2_w, conv_w, conv_b, net):
    shift1, scale1, gate1, shift2, scale2, gate2 = [mod[:, i * D:(i + 1) * D] for i in range(6)]

    def run(name, fn, *args, **kw):
        phase = net.host(name)
        if phase is None:
            return fn(*args, **kw)
        res, comm = fn(*args, phase=phase, **kw)
        net.done(name, comm, res[0] if isinstance(res, tuple) else res)
        return res

    h = _norm_mod(x, norm1_w, scale1, shift1, "norm1_fwd")
    proj = run("proj_fwd", _matmul, h, net.full[W_IN], "nn", F32, "proj_fwd")
    cat, o_raw, states = run("hgrn_fwd", _hgrn_fwd, proj, lb_logits, hg_norm_w)
    qn, kn = _qk_prep(proj, q_norm_w, k_norm_w)
    att = [run(f"attn_fwd_d{dil}", _attn_fwd, proj, qn, kn, dil) for dil in DILS]
    cat, b_out_f32, lse = _attn_merge([t[0] for t in att], [t[1] for t in att], cat)
    mix = run("mix_fwd", _matmul, cat, net.full[W_OUT], "nn", F32, "mix_fwd")
    x1, h2 = _resid_norm_mod(x, mix, gate1, norm2_w, scale2, shift2, "norm2_fwd")
    u = run("up_fwd", _matmul, h2, net.full[W_UP], "nn", BF16, "up_fwd")
    yact = run("conv_gate_fwd_a", _conv_gate_fwd, u, conv_w, conv_b, "conv_gate_fwd_a", tiles=(0, NCT // 2 + 1))
    yact = run("conv_gate_fwd_b", _conv_gate_fwd, u, conv_w, conv_b, "conv_gate_fwd_b", tiles=(NCT // 2 + 1, NCT),
               into=yact)
    z =_matmul(yact, net.full[W_DOWN], "nn", F32, "down_fwd")
    dout, dz, dgate2, loss_row = _out_loss(z, x1, target, gate2)

    net.grad[W_DOWN] = _matmul(yact, dz, "tn", BF16, "down_bwd_w")
    dyact = run("down_bwd_x", _matmul, dz, net.full[W_DOWN], "nt", BF16, "down_bwd_x")
    du, dconv_b, dconv_w = run("conv_gate_bwd", _conv_gate_bwd, dyact, u, conv_w, conv_b)
    net.grad[W_UP] = run("up_bwd_w", _matmul, h2, du, "tn", BF16, "up_bwd_w")
    dh2 = run("up_bwd_x", _matmul, du, net.full[W_UP], "nt", F32, "up_bwd_x")
    dx1, dshift2, dscale2, dnorm2, dgate1, dmix = run(
        "norm2_bwd", _norm_mod_bwd, dh2, x1, norm2_w, scale2, dout, "norm2_bwd", mix=mix, gate=gate1)
    net.grad[W_OUT] = run("mix_bwd_w", _matmul, cat, dmix, "tn", BF16, "mix_bwd_w")
    dcat = run("mix_bwd_x", _matmul, dmix, net.full[W_OUT], "nt", F32, "mix_bwd_x")
    dhq, dhf, dhi, dhg, dlb, dhg_norm = run("hgrn_bwd", _hgrn_bwd, proj, lb_logits, hg_norm_w, o_raw, states, dcat)
    delta = _attn_delta(dcat, b_out_f32)
    acc = None
    for dil in DILS:
        acc = run(f"attn_bwd_d{dil}", _attn_bwd, proj, qn, kn, lse, delta, dcat, dil, acc)
    daq, dak, dav, dq_norm, dk_norm = _qk_norm_bwd(proj, *acc, q_norm_w, k_norm_w)
    dproj = jnp.concatenate([dhq, dhf, dhi, dhg, daq, dak, dav], axis=1)
    net.carry = dproj
    for q in range(IN_CHUNKS):
        net.after_chunk(q, run(f"proj_bwd_w_{q}", _matmul, h, net.carry, "tn", BF16, f"proj_bwd_w_{q}",
                               m_blocks=(D // IN_CHUNKS, [q])))
    dh = run("proj_bwd_x", _matmul, net.carry, net.full[W_IN], "nt", F32, "proj_bwd_x")
    grad_x, dshift1, dscale1, dnorm1 = run("norm1_bwd", _norm_mod_bwd, dh, x, norm1_w, scale1, dx1, "norm1_bwd")

    dmod = jnp.concatenate([dshift1, dscale1, dgate1, dshift2, dscale2, dgate2], axis=1)
    small = dict(norm1_w=dnorm1, lb=dlb, hg_norm_w=dhg_norm, q_norm_w=dq_norm, k_norm_w=dk_norm,
                 norm2_w=dnorm2, conv_b=dconv_b, conv_w=dconv_w)
    return loss_row, grad_x, dmod, small


def _place():
    x, y, c = lax.axis_index("x"), lax.axis_index("y"), lax.axis_index("c")
    others = [(1 - x, y), (x, 1 - y), (1 - x, 1 - y)]
    return x, y, c, others


def _remote(src, dst, send_sems, recv_sems, k, to):
    return pltpu.make_async_remote_copy(src_ref=src, dst_ref=dst, send_sem=send_sems.at[k], recv_sem=recv_sems.at[k],
                                        device_id=to, device_id_type=MESH)


class _Phase:
    def __init__(self):
        self.ins, self.outs, self.aliases, self.groups, self.n = [], [], {}, [], 0

    def add(self, ins, outs, aliases, n, copies):
        i0, o0 = len(self.ins), len(self.outs)
        self.ins += list(ins)
        self.outs += list(outs)
        self.aliases.update({i0 + i: o0 + o for i, o in aliases.items()})
        self.groups.append((slice(i0, i0 + len(ins)), slice(o0, o0 + len(outs)), copies))
        self.n += n
        return slice(o0, o0 + len(outs))

    def build(self, cin, cout, send_sems, recv_sems, landing):
        res = []
        for si, so, copies in self.groups:
            for src, dst, land, peer in copies(cin[si], cout[so]):
                k = len(res)
                res.append(_remote(land, land, send_sems, recv_sems, k, peer) if landing
                           else _remote(src, dst, send_sems, recv_sems, k, peer))
        assert len(res) == self.n
        return res


def _pcall(body, args, phase=None, **kw):
    if phase is None or not phase.groups:
        res = pl.pallas_call(body, **kw)(*args)
        return res if phase is None else (res, ())
    grid = tuple(kw.pop("grid", ()))
    out_shape, out_specs = kw.pop("out_shape"), kw.pop("out_specs")
    single = not isinstance(out_shape, (tuple, list))
    out_shape = [out_shape] if single else list(out_shape)
    out_specs = [out_specs] if single else list(out_specs)
    scratch = list(kw.pop("scratch_shapes", ()))
    aliases = dict(kw.pop("input_output_aliases", {}))
    n_in, n_out, n_ci, n_co = len(args), len(out_shape), len(phase.ins), len(phase.outs)
    aliases.update({n_in + i: n_out + o for i, o in phase.aliases.items()})

    def hosted(*refs):
        ins, cin = refs[:n_in], refs[n_in:n_in + n_ci]
        outs = refs[n_in + n_ci:n_in + n_ci + n_out]
        cout = refs[n_in + n_ci + n_out:n_in + n_ci + n_out + n_co]
        scr, send_sems, recv_sems = refs[n_in + n_ci + n_out + n_co:-2], refs[-2], refs[-1]
        ids = [pl.program_id(a) for a in range(len(grid))]

        def start():
            for cp in phase.build(cin, cout, send_sems, recv_sems, False):
                cp.start()

        def finish():
            for cp in phase.build(cin, cout, send_sems, recv_sems, False):
                cp.wait_send()
            for cp in phase.build(cin, cout, send_sems, recv_sems, True):
                cp.wait_recv()

        if grid:
            first = functools.reduce(jnp.logical_and, [i == 0 for i in ids])
            last = functools.reduce(jnp.logical_and, [i == g - 1 for i, g in zip(ids, grid)])
            pl.when(first)(start)
            body(*ins, *outs, *scr)
            pl.when(last)(finish)
        else:
            start()
            body(*ins, *outs, *scr)
            finish()

    res = pl.pallas_call(
        hosted, out_shape=tuple(out_shape + phase.outs), grid=grid,
        in_specs=list(kw.pop("in_specs")) + [HBM_SPEC] * n_ci, out_specs=tuple(out_specs + [HBM_SPEC] * n_co),
        input_output_aliases=aliases,
        scratch_shapes=scratch + [pltpu.SemaphoreType.DMA((phase.n,)), pltpu.SemaphoreType.DMA((phase.n,))],
        **kw,
    )(*args, *phase.ins)
    outs = res[:n_out]
    return (outs[0] if single else tuple(outs)), tuple(res[n_out:])


def _comm_only(name, phase):
    return _pcall(lambda: None, [], phase, name=name, out_shape=(), in_specs=[], out_specs=())[1]


def _all_gather_rows(block, name, phase=None):
    m_per, n = block.shape

    def body(x_ref, out_ref, send_sems, recv_sems, local_sem):
        x, y, c, chips = _place()
        me, sibling = (x, y, c), (x, y, 1 - c)

        def rows(px, py, pc):
            return out_ref.at[pl.ds((4 * px + 2 * py + pc) * m_per, m_per), :]

        def copy(k, blk, to, src=None):
            return _remote(rows(*blk) if src is None else src, rows(*blk), send_sems, recv_sems, k, to)

        mine = pltpu.make_async_copy(x_ref, rows(*me), local_sem)
        mine.start()
        first = [copy(0, me, sibling, src=x_ref)]
        first += [copy(1 + j, me, (*chip, c), src=x_ref) for j, chip in enumerate(chips)]
        for cp in first:
            cp.start()
        passed = [copy(4 + j, (*chip, c), sibling) for j, chip in enumerate(chips)]
        for j, chip in enumerate(chips):
            copy(1 + j, (*chip, c), me).wait_recv()
            passed[j].start()
        copy(0, sibling, me).wait_recv()
        for j, chip in enumerate(chips):
            copy(4 + j, (*chip, 1 - c), me).wait_recv()
        for cp in first + passed:
            cp.wait_send()
        mine.wait()

    return _pcall(
        body, [block], phase, name=name, out_shape=jax.ShapeDtypeStruct((N_DEV * m_per, n), block.dtype),
        in_specs=[VMEM_SPEC], out_specs=VMEM_SPEC,
        scratch_shapes=[pltpu.SemaphoreType.DMA((7,)), pltpu.SemaphoreType.DMA((7,)), pltpu.SemaphoreType.DMA],
    )


class _Geo:
    def __init__(self, kind, shard_shape):
        self.kind = kind
        self.shard_shape = tuple(shard_shape)
        self.hr, self.hc = shard_shape[0] // 2, shard_shape[1]
        if kind == "col":
            self.full_shape = (shard_shape[0], N_CHIPS * shard_shape[1])
        else:
            self.full_shape = (N_CHIPS * shard_shape[0], shard_shape[1])

    def full_half(self, ref, j, c):
        return self.piece(ref, j, c, 0, 1, 1)

    def piece(self, ref, j, c, p0, p1, pieces, part=None):
        pr = self.hr // pieces
        off, n = p0 * pr, (p1 - p0) * pr
        if part is not None:
            top = (n // 32) * 16
            off, n = (off, top) if part == 0 else (off + top, n - top)
        if self.kind == "col":
            return ref.at[pl.ds(pl.multiple_of(c * self.hr + off, 16), n),
                          pl.ds(pl.multiple_of(j * self.hc, 128), self.hc)]
        return ref.at[pl.ds(pl.multiple_of((2 * j + c) * self.hr + off, 16), n), :]


WEIGHT_KINDS = ("col", "row", "col", "row")
NW = len(WEIGHT_KINDS)


def _comm_call(body, name, ins, outs, n_remote, aliases=None):
    return pl.pallas_call(
        body, name=name, out_shape=tuple(outs),
        in_specs=[HBM_SPEC] * len(ins), out_specs=tuple([HBM_SPEC] * len(outs)),
        input_output_aliases=aliases or {},
        scratch_shapes=[pltpu.SemaphoreType.DMA((n_remote,)), pltpu.SemaphoreType.DMA((n_remote,))],
    )(*ins)


ROW_TILES = (256, 176, 128, 64, 32, 16)


def _cast_into_full(shard, geo, place, name):
    rs, cs = shard.shape
    tr = _pick(rs, ROW_TILES)
    nb = rs // tr

    def body(place_ref, s_ref, o_ref):
        o_ref[...] = s_ref[...].astype(BF16)

    if geo.kind == "col":
        out_map = lambda i, place_ref: (i, place_ref[0])
    else:
        out_map = lambda i, place_ref: (place_ref[0] * nb + i, 0)
    return pl.pallas_call(
        body, name=name, out_shape=jax.ShapeDtypeStruct(geo.full_shape, BF16),
        grid_spec=pltpu.PrefetchScalarGridSpec(
            num_scalar_prefetch=1, grid=(nb,),
            in_specs=[pl.BlockSpec((tr, cs), lambda i, place_ref: (i, 0))],
            out_specs=pl.BlockSpec((tr, cs), out_map)),
        compiler_params=_params(1),
    )(place, shard)


def _gather_weight(full, geo, pieces, name):
    per = 8

    def body(in_ref, ref, send_sems, recv_sems):
        x, y, c, _ = _place()
        j, jx, jy, jo = 2 * x + y, 2 * (1 - x) + y, 2 * x + (1 - y), 2 * (1 - x) + (1 - y)
        nx, ny, sib = (1 - x, y, c), (x, 1 - y, c), (x, y, 1 - c)

        def cp(region, k, to):
            return _remote(region, region, send_sems, recv_sems, k, to)

        def reg(chip, p, part=None, core=c):
            return geo.piece(ref, chip, core, p, p + 1, pieces, part)

        sent = []
        for p in range(pieces):
            sent += [cp(reg(j, p), per * p, nx), cp(reg(j, p), per * p + 1, ny)]
        for d in sent:
            d.start()
        for p in range(pieces):
            cp(reg(jx, p), per * p, nx).wait_recv()
            new = [cp(reg(jx, p, 0), per * p + 2, ny), cp(reg(jx, p), per * p + 4, sib)]
            cp(reg(jy, p), per * p + 1, ny).wait_recv()
            new += [cp(reg(jy, p, 1), per * p + 3, nx), cp(reg(jy, p), per * p + 5, sib)]
            for d in new:
                d.start()
            sent += new
        for p in range(pieces):
            cp(reg(jo, p, 0), per * p + 2, ny).wait_recv()
            cp(reg(jo, p, 1), per * p + 3, nx).wait_recv()
            new = [cp(reg(jo, p, 0), per * p + 6, sib), cp(reg(jo, p, 1), per * p + 7, sib)]
            for d in new:
                d.start()
            sent += new
        for p in range(pieces):
            cp(reg(jx, p, None, 1 - c), per * p + 4, sib).wait_recv()
            cp(reg(jy, p, None, 1 - c), per * p + 5, sib).wait_recv()
            cp(reg(jo, p, 0, 1 - c), per * p + 6, sib).wait_recv()
            cp(reg(jo, p, 1, 1 - c), per * p + 7, sib).wait_recv()
        for d in sent:
            d.wait_send()

    return _comm_call(body, name, [full], [_same(full)], per * pieces, aliases={0: 0})[0]


def _same(a):
    return jax.ShapeDtypeStruct(a.shape, a.dtype)


def _gather_ici(full, geo, p0, p1, pieces):
    def copies(ins, outs):
        x, y, c, _ = _place()
        mine = geo.piece(outs[0], 2 * x + y, c, p0, p1, pieces)
        return [(mine, mine, geo.piece(outs[0], 2 * ox + oy, c, p0, p1, pieces), (ox, oy, c))
                for ox, oy in ((1 - x, y), (x, 1 - y))]

    return dict(ins=[full], outs=[_same(full)], aliases={0: 0}, n=2, copies=copies)


def _gather_relay(full, geo, p0, p1, pieces):
    def copies(ins, outs):
        x, y, c, _ = _place()
        jx, jy, jo = 2 * (1 - x) + y, 2 * x + (1 - y), 2 * (1 - x) + (1 - y)
        reg = lambda chip, part: geo.piece(outs[0], chip, c, p0, p1, pieces, part)
        return [(reg(jx, 0), reg(jx, 0), reg(jo, 0), (x, 1 - y, c)), (reg(jy, 1), reg(jy, 1), reg(jo, 1), (1 - x, y, c))]

    return dict(ins=[full], outs=[_same(full)], aliases={0: 0}, n=2, copies=copies)


def _gather_d2d(full, geo):
    def copies(ins, outs):
        x, y, c, chips = _place()
        return [(geo.full_half(outs[0], 2 * ox + oy, c), geo.full_half(outs[0], 2 * ox + oy, c),
                 geo.full_half(outs[0], 2 * ox + oy, 1 - c), (x, y, 1 - c)) for ox, oy in chips]

    return dict(ins=[full], outs=[_same(full)], aliases={0: 0}, n=3, copies=copies)


def _swap_d2d(grad, geo):
    def copies(ins, outs):
        x, y, c, _ = _place()
        return [(geo.full_half(ins[0], j, 1 - c), outs[0].at[j], outs[0].at[j], (x, y, 1 - c)) for j in range(N_CHIPS)]

    return dict(ins=[grad], outs=[jax.ShapeDtypeStruct((N_CHIPS, geo.hr, geo.hc), BF16)], aliases={}, n=N_CHIPS,
                copies=copies)


SEM_SPEC = pl.BlockSpec(memory_space=pltpu.SEMAPHORE)
SIDE_EFFECT = pltpu.SideEffectType.DATAFLOW_SIDE_EFFECTING


def _fly_copies(swap, src_ref, land_ref, geo, sems):
    x, y, c, chips = _place()
    if swap:
        ends = [(geo.full_half(src_ref, j, 1 - c), land_ref.at[j], (x, y, 1 - c)) for j in range(N_CHIPS)]
    else:
        ends = [(src_ref.at[2 * ox + oy], land_ref.at[k], (ox, oy, c)) for k, (ox, oy) in enumerate(chips)]
    n = len(ends)
    return [pltpu.make_async_remote_copy(src_ref=src, dst_ref=dst, send_sem=sems[k], recv_sem=sems[n + k],
                                         device_id=peer, device_id_type=MESH) for k, (src, dst, peer) in enumerate(ends)]


def _fly_start(swap, src, geo, name, carry=None):
    n = N_CHIPS if swap else 3
    n_carry = 0 if carry is None else 1

    def body(src_ref, land_ref, *rest):
        sems, token = rest[n_carry:n_carry + 2 * n], rest[n_carry + 2 * n + 2]
        for copy in _fly_copies(swap, src_ref, land_ref, geo, sems):
            copy.start()
        token[...] = jnp.zeros_like(token)

    land = jax.ShapeDtypeStruct((n, geo.hr, geo.hc), BF16)
    out_shape = (pltpu.SemaphoreType.DMA(()),) * (2 * n) + (pltpu.HBM(src.shape, src.dtype), pltpu.HBM(land.shape, land.dtype),
                                                             jax.ShapeDtypeStruct((8, 128), F32))
    out_specs = (SEM_SPEC,) * (2 * n) + (HBM_SPEC, HBM_SPEC, VMEM_SPEC)
    args = [pltpu.with_memory_space_constraint(src, pltpu.HBM),
            pltpu.with_memory_space_constraint(lax.empty(land.shape, land.dtype), pltpu.HBM)]
    aliases = {0: 2 * n, 1: 2 * n + 1}
    if carry is not None:
        out_shape += (pltpu.HBM(carry.shape, carry.dtype),)
        out_specs += (HBM_SPEC,)
        args.append(pltpu.with_memory_space_constraint(carry, pltpu.HBM))
        aliases[2] = 2 * n + 3
    return pl.pallas_call(
        body, name=name, out_shape=out_shape, in_specs=(HBM_SPEC,) * len(args), out_specs=out_specs,
        input_output_aliases=aliases,
        compiler_params=pltpu.CompilerParams(has_side_effects=SIDE_EFFECT),
    )(*args)


def _fly_wait(swap, started, geo, after, name):
    n = N_CHIPS if swap else 3
    sems, src_thru, land_thru = started[:2 * n], started[2 * n], started[2 * n + 1]

    def body(src_ref, land_ref, *rest):
        for copy in _fly_copies(swap, src_ref, land_ref, geo, rest[:2 * n]):
            copy.wait_send()
            copy.wait_recv()

    return pl.pallas_call(
        body, name=name,
        out_shape=(pltpu.HBM(src_thru.shape, src_thru.dtype), pltpu.HBM(land_thru.shape, land_thru.dtype)),
        in_specs=(HBM_SPEC, HBM_SPEC) + (SEM_SPEC,) * (2 * n) + (pl.BlockSpec(memory_space=pl.ANY),),
        out_specs=(HBM_SPEC, HBM_SPEC), input_output_aliases={0: 0, 1: 1},
        compiler_params=pltpu.CompilerParams(has_side_effects=SIDE_EFFECT),
    )(src_thru, land_thru, *sems, after)


def _scatter_ici(pair, recv, geo, p0, p1, pieces):
    pr = geo.hr // pieces
    rows = pl.ds(p0 * pr, (p1 - p0) * pr)

    def copies(ins, outs):
        x, y, c, chips = _place()
        return [(ins[0].at[2 * ox + oy, rows, :], outs[0].at[k, rows, :], outs[0].at[k, rows, :], (ox, oy, c))
                for k, (ox, oy) in enumerate(chips)]

    out = jax.ShapeDtypeStruct((3, geo.hr, geo.hc), BF16)
    if recv is None:
        return dict(ins=[pair], outs=[out], aliases={}, n=3, copies=copies)
    return dict(ins=[pair, recv], outs=[out], aliases={1: 0}, n=3, copies=copies)


def _join_d2d(shard, geo, row0=0):
    def copies(ins, outs):
        x, y, c, _ = _place()
        mine = outs[0].at[pl.ds(pl.multiple_of(row0 + c * geo.hr, 8), geo.hr), :]
        other = outs[0].at[pl.ds(pl.multiple_of(row0 + (1 - c) * geo.hr, 8), geo.hr), :]
        return [(mine, mine, other, (x, y, 1 - c))]

    return dict(ins=[shard], outs=[_same(shard)], aliases={0: 0}, n=1, copies=copies)


def _add_pair(grad, got, geo, place, name):
    tr = _pick(geo.hr, ROW_TILES)
    nb = geo.hr // tr

    def body(place_ref, a_ref, b_ref, o_ref):
        o_ref[0] = (a_ref[...].astype(F32) + b_ref[0].astype(F32)).astype(BF16)

    if geo.kind == "col":
        own_map = lambda j, i, place_ref: (place_ref[1] * nb + i, j)
    else:
        own_map = lambda j, i, place_ref: ((2 * j + place_ref[1]) * nb + i, 0)
    spec = pl.BlockSpec((1, tr, geo.hc), lambda j, i, place_ref: (j, i, 0))
    return pl.pallas_call(
        body, name=name, out_shape=jax.ShapeDtypeStruct((N_CHIPS, geo.hr, geo.hc), BF16),
        grid_spec=pltpu.PrefetchScalarGridSpec(
            num_scalar_prefetch=1, grid=(N_CHIPS, nb),
            in_specs=[pl.BlockSpec((tr, geo.hc), own_map), spec], out_specs=spec),
        compiler_params=_params(2),
    )(place, grad, got)


def _add_four(pair, recv, geo, place, name, rows=None, row0=0, into=None):
    tr = _pick(geo.hr, ROW_TILES)
    nb = geo.hr // tr
    rows = 2 * geo.hr if rows is None else rows

    def body(place_ref, p_ref, r_ref, *rest):
        rest[-1][...] = ((p_ref[0].astype(F32) + r_ref[0].astype(F32)) + r_ref[1].astype(F32)) + r_ref[2].astype(F32)

    in_specs = [pl.BlockSpec((1, tr, geo.hc), lambda i, place_ref: (place_ref[0], i, 0)),
                pl.BlockSpec((3, tr, geo.hc), lambda i, place_ref: (0, i, 0))]
    args = [place, pair, recv]
    if into is not None:
        in_specs.append(pl.BlockSpec(memory_space=pl.ANY))
        args.append(into)
    return pl.pallas_call(
        body, name=name, out_shape=jax.ShapeDtypeStruct((rows, geo.hc), F32),
        grid_spec=pltpu.PrefetchScalarGridSpec(
            num_scalar_prefetch=1, grid=(nb,), in_specs=in_specs,
            out_specs=pl.BlockSpec((tr, geo.hc), lambda i, place_ref: (row0 // tr + place_ref[1] * nb + i, 0))),
        input_output_aliases={3: 0} if into is not None else {},
        compiler_params=_params(1),
    )(*args)


PIECES = (4, 1, 16, 8) + (1,) * IN_CHUNKS
SCHEDULE = {
    "proj_fwd": (("gather_ici", W_OUT, 0, 1), ("gather_ici", W_UP, 0, 6)),
    "hgrn_fwd": (("gather_relay", W_OUT, 0, 1), ("gather_relay", W_UP, 0, 6), ("gather_ici", W_UP, 6, 12)),
    "attn_fwd_d1": (("gather_relay", W_UP, 6, 12),),
    "attn_fwd_d4": (("gather_ici", W_UP, 12, 16), ("gather_d2d", W_OUT)),
    "attn_fwd_d16": (("gather_relay", W_UP, 12, 16), ("gather_ici", W_DOWN, 0, 2)),
    "mix_fwd": (("gather_d2d", W_UP), ("gather_ici", W_DOWN, 2, 4)),
    "up_fwd": (("gather_ici", W_DOWN, 4, 8), ("gather_relay", W_DOWN, 0, 4)),
    "conv_gate_fwd_a": (("gather_relay", W_DOWN, 4, 8),),
    "conv_gate_fwd_b": (("gather_d2d", W_DOWN),),
    "down_bwd_x": (("swap", W_DOWN),),
    "conv_gate_bwd": (("scatter", W_DOWN, 0, 4),),
    "up_bwd_w": (("scatter", W_DOWN, 4, 8),),
    "up_bwd_x": (("swap", W_UP),),
    "norm2_bwd": (("scatter", W_UP, 0, 1),),
    "mix_bwd_w": (("scatter", W_UP, 1, 2),),
    "mix_bwd_x": (("swap", W_OUT), ("scatter", W_UP, 2, 3)),
    "hgrn_bwd": (("scatter", W_UP, 3, 9), ("join", W_DOWN)),
    "attn_bwd_d1": (("scatter", W_UP, 9, 12),),
    "attn_bwd_d4": (("scatter", W_OUT, 0, 1),),
    "attn_bwd_d16": (("scatter", W_UP, 12, 16),),
    "proj_bwd_w_0": (("join", W_UP), ("join", W_OUT)),
    "gather_stats": (("join", W_INQ),),
    "grad_in_join": (("join", W_INQ + 1),),
}


class _Net:
    def __init__(self, shards, place):
        self.place = place
        self.geos = [_Geo(k, s.shape) for k, s in zip(WEIGHT_KINDS, shards)]
        self.full = [_cast_into_full(s, g, place, f"cast_shard_{w}") for w, (s, g) in enumerate(zip(shards, self.geos))]
        self.full[W_IN] = _gather_weight(self.full[W_IN], self.geos[W_IN], PIECES[W_IN], "gather_w_in")
        rows, cols = shards[W_IN].shape
        self.geos += [_Geo("col", (rows // IN_CHUNKS, cols))] * IN_CHUNKS
        n = NW + IN_CHUNKS
        self.grad, self.pair, self.recv, self.shard = [None] * n, [None] * n, [None] * n, [None] * n
        self.in_rows, self.in_shard = rows, None
        self.when_joined, self.joined = {}, {}
        self.flying = None
        self.carry = None
        self.swaps = [None] * IN_CHUNKS
        self.slots = []

    def _row0(self, w):
        return (w - W_INQ) * 2 * self.geos[w].hr

    def host(self, name):
        phase = _Phase()
        self.slots = []
        groups = {}
        for item in SCHEDULE.get(name, ()):
            kind, w = item[0], item[1]
            geo = self.geos[w]
            key = len(groups)
            if kind == "gather_ici":
                spec, key = _gather_ici(self.full[w], geo, item[2], item[3], PIECES[w]), ("full", w)
            elif kind == "gather_relay":
                spec, key = _gather_relay(self.full[w], geo, item[2], item[3], PIECES[w]), ("full", w)
            elif kind == "gather_d2d":
                spec, key = _gather_d2d(self.full[w], geo), ("full", w)
            elif kind == "swap":
                spec = _swap_d2d(self.grad[w], geo)
            elif kind == "scatter":
                spec, key = _scatter_ici(self.pair[w], self.recv[w], geo, item[2], item[3], PIECES[w]), ("recv", w)
            elif w >= W_INQ:
                spec, key = _join_d2d(self.in_shard, geo, self._row0(w)), "in_shard"
            else:
                spec = _join_d2d(self.shard[w], geo)
            groups.setdefault(key, []).append((item, spec))
        for group in groups.values():
            specs = [s for _, s in group]
            merged = dict(specs[0], n=sum(s["n"] for s in specs),
                          copies=lambda ins, outs, specs=specs: [t for s in specs for t in s["copies"](ins, outs)])
            self.slots.append(([item for item, _ in group], phase.add(**merged)))
        return phase

    def done(self, name, comm, result=None):
        for items, sl in sorted(self.slots, key=lambda s: s[0][0][0] == "scatter"):
            out = comm[sl][0]
            for item in items:
                kind, w = item[0], item[1]
                if kind in ("gather_ici", "gather_relay", "gather_d2d"):
                    self.full[w] = out
                elif kind == "swap":
                    self.pair[w] = _add_pair(self.grad[w], out, self.geos[w], self.place, f"grad_pair_sum_{w}")
                elif kind == "scatter":
                    self.recv[w] = out
                    if item[3] == PIECES[w] and w >= W_INQ:
                        self.in_shard = _add_four(self.pair[w], out, self.geos[w], self.place, f"grad_chip_sum_{w}",
                                                  rows=self.in_rows, row0=self._row0(w), into=self.in_shard)
                    elif item[3] == PIECES[w]:
                        self.shard[w] = _add_four(self.pair[w], out, self.geos[w], self.place, f"grad_chip_sum_{w}")
                elif w >= W_INQ:
                    self.in_shard = out
                else:
                    self.shard[w] = out
                    if w in self.when_joined:
                        self.joined[w] = self.when_joined[w](out)
        if name == "proj_bwd_x" and result is not None:
            self._land_swap(IN_CHUNKS - 1, result)
            pair, recv = _fly_wait(False, self.flying, self.geos[W_INQ], result, "grad_in_scatter0_wait")
            self.in_shard = _add_four(pair, recv, self.geos[W_INQ], self.place, f"grad_chip_sum_{W_INQ}",
                                      rows=self.in_rows, row0=0, into=self.in_shard)

    def _land_swap(self, q, after):
        w = W_INQ + q
        grad, got = _fly_wait(True, self.swaps[q], self.geos[w], after, f"grad_in_swap{q}_wait")
        self.pair[w] = _add_pair(grad, got, self.geos[w], self.place, f"grad_pair_sum_{w}")

    def after_chunk(self, q, grad):
        w = W_INQ + q
        self.grad[w] = grad
        if q > 0:
            self._land_swap(q - 1, grad)
        if q == 1:
            self.flying = _fly_start(False, self.pair[W_INQ], self.geos[W_INQ], "grad_in_scatter0_start", self.carry)
            self.carry = self.flying[-1]
        self.swaps[q] = _fly_start(True, grad, self.geos[w], f"grad_in_swap{q}_start", self.carry)
        self.carry = self.swaps[q][-1]

    def alone(self, name):
        self.done(name, _comm_only(name, self.host(name)))


CONVW_SHARD = 3 * DFF // N_CHIPS
COND_PAD = 8 * 896
SMALL_SIZES = (("norm1_w", D), ("lb", HW), ("hg_norm_w", DH), ("q_norm_w", DH), ("k_norm_w", DH),
               ("norm2_w", D), ("conv_b", DFF), ("conv_w", 3 * DFF), ("loss", 128))
SMALL_TOTAL = sum(n for _, n in SMALL_SIZES)
STATS_PAD = 8 * 5120


def kernel(x, c, w_ada, b_ada, norm1_w, w_in, lb_logits, hg_norm_w, q_norm_w, k_norm_w, w_out, norm2_w, w_up, conv_w, conv_b, w_down, loss_target, m_w_ada, m_b_ada, m_norm1_w, m_w_in, m_lb_logits, m_hg_norm_w, m_q_norm_w, m_k_norm_w, m_w_out, m_norm2_w, m_w_up, m_conv_w, m_conv_b, m_w_down, v_w_ada, v_b_ada, v_norm1_w, v_w_in, v_lb_logits, v_hg_norm_w, v_q_norm_w, v_k_norm_w, v_w_out, v_norm2_w, v_w_up, v_conv_w, v_conv_b, v_w_down):
    chip = 2 * lax.axis_index("x") + lax.axis_index("y")
    dev = 2 * chip + lax.axis_index("c")

    cond = jnp.concatenate([c, conv_w[0].reshape(1, CONVW_SHARD), jnp.zeros((1, COND_PAD - D - CONVW_SHARD), F32)], axis=1)
    cond_all = _all_gather_rows(cond.reshape(8, COND_PAD // 8), "gather_cond").reshape(N_DEV, COND_PAD)
    c_all = cond_all[:, :D]
    conv_w_full = jnp.concatenate(
        [cond_all[2 * j, D:D + CONVW_SHARD].reshape(3, DFF // N_CHIPS) for j in range(N_CHIPS)], axis=1)

    b_shard = lax.dynamic_slice_in_dim(b_ada, chip * ADA_COLS, ADA_COLS, axis=1)
    mod_cols = _all_gather_rows(_ada_fwd(c_all, w_ada[0], b_shard), "gather_mod")
    mod_all = jnp.concatenate([mod_cols[16 * j:16 * j + 8] for j in range(N_CHIPS)], axis=1)
    mod = lax.dynamic_slice_in_dim(mod_all, dev, 1, axis=0)

    place = jnp.stack([chip, lax.axis_index("c")]).astype(jnp.int32)
    net = _Net([w_in[0], w_out[0], w_up[0], w_down[0]], place)
    net.when_joined = {
        W_OUT: lambda grad: _adamw_sc(w_out[0], grad, m_w_out[0], v_w_out[0], "adamw_sc_w_out"),
        W_DOWN: lambda grad: _adamw_sc(w_down[0], grad, m_w_down[0], v_w_down[0], "adamw_sc_w_down"),
        W_UP: lambda grad: _adamw_sc(w_up[0], grad, m_w_up[0], v_w_up[0], "adamw_sc_w_up"),
    }
    early = {W_OUT: "w_out", W_DOWN: "w_down", W_UP: "w_up"}
    loss_row, grad_x, dmod, small = _sequence_step(
        x[0], loss_target[0], mod, norm1_w, lb_logits, hg_norm_w, q_norm_w, k_norm_w, norm2_w, conv_w_full, conv_b, net)
    small["conv_w"] = small["conv_w"].reshape(1, 3 * DFF)
    small["loss"] = loss_row
    stats = jnp.concatenate([dmod] + [small[k] for k, _ in SMALL_SIZES]
                            + [jnp.zeros((1, STATS_PAD - 6 * D - SMALL_TOTAL), F32)], axis=1)
    stats_all, comm = _all_gather_rows(stats.reshape(8, STATS_PAD // 8), "gather_stats", net.host("gather_stats"))
    net.done("gather_stats", comm)
    stats_all = stats_all.reshape(N_DEV, STATS_PAD)
    last = W_INQ + IN_CHUNKS - 1
    started = _fly_start(False, net.pair[last], net.geos[last], "grad_in_scatter_start")
    dmod_all = stats_all[:, :6 * D] + started[8][0, 0]
    sums = _sum_rows(stats_all[:, 6 * D:6 * D + SMALL_TOTAL], "small_grad_sum")
    g_small, off = {}, 0
    for k, n in SMALL_SIZES:
        g_small[k] = sums[:, off:off + n]
        off += n
    loss = g_small["loss"][0, 0]
    g_b_ada = _sum_rows(dmod_all, "b_ada_grad_sum")
    ada = _ada_bwd_adamw(c_all, lax.dynamic_slice_in_dim(dmod_all, chip * ADA_COLS, ADA_COLS, axis=1),
                         w_ada[0], m_w_ada[0], v_w_ada[0])
    g_w_ada = ada[0]
    pair_last, recv_last = _fly_wait(False, started, net.geos[last], ada[3], "grad_in_scatter_wait")
    net.in_shard = _add_four(pair_last, recv_last, net.geos[last], place, f"grad_chip_sum_{last}",
                             rows=net.in_rows, row0=net._row0(last), into=net.in_shard)
    net.alone("grad_in_join")
    g_out, g_up, g_down = net.shard[W_OUT], net.shard[W_UP], net.shard[W_DOWN]
    g_in = net.in_shard
    g_lb = _lb_grad(lb_logits, g_small["lb"])
    g_conv_w = lax.dynamic_slice_in_dim(g_small["conv_w"].reshape(3, DFF), chip * (DFF // N_CHIPS), DFF // N_CHIPS, axis=1)

    names = ["w_ada", "b_ada", "norm1_w", "w_in", "lb_logits", "hg_norm_w", "q_norm_w", "k_norm_w", "w_out",
             "norm2_w", "w_up", "conv_w", "conv_b", "w_down"]
    lead = {"w_ada", "w_in", "w_out", "w_up", "conv_w", "w_down"}
    w = dict(w_ada=w_ada, b_ada=b_ada, norm1_w=norm1_w, w_in=w_in, lb_logits=lb_logits, hg_norm_w=hg_norm_w,
             q_norm_w=q_norm_w, k_norm_w=k_norm_w, w_out=w_out, norm2_w=norm2_w, w_up=w_up, conv_w=conv_w,
             conv_b=conv_b, w_down=w_down)
    m = dict(w_ada=m_w_ada, b_ada=m_b_ada, norm1_w=m_norm1_w, w_in=m_w_in, lb_logits=m_lb_logits,
             hg_norm_w=m_hg_norm_w, q_norm_w=m_q_norm_w, k_norm_w=m_k_norm_w, w_out=m_w_out, norm2_w=m_norm2_w,
             w_up=m_w_up, conv_w=m_conv_w, conv_b=m_conv_b, w_down=m_w_down)
    v = dict(w_ada=v_w_ada, b_ada=v_b_ada, norm1_w=v_norm1_w, w_in=v_w_in, lb_logits=v_lb_logits,
             hg_norm_w=v_hg_norm_w, q_norm_w=v_q_norm_w, k_norm_w=v_k_norm_w, w_out=v_w_out, norm2_w=v_norm2_w,
             w_up=v_w_up, conv_w=v_conv_w, conv_b=v_conv_b, w_down=v_w_down)
    g = dict(w_ada=g_w_ada, b_ada=g_b_ada, norm1_w=g_small["norm1_w"], w_in=g_in, lb_logits=g_lb,
             hg_norm_w=g_small["hg_norm_w"], q_norm_w=g_small["q_norm_w"], k_norm_w=g_small["k_norm_w"], w_out=g_out,
             norm2_w=g_small["norm2_w"], w_up=g_up, conv_w=g_conv_w, conv_b=g_small["conv_b"], w_down=g_down)

    updated = {early[i]: res for i, res in net.joined.items()}
    updated["w_ada"] = (ada[1], ada[2], ada[3], ada[0])
    grads, deltas, new_m, new_v = [], [], [], []
    for n in names:
        strip = (lambda t: t[0]) if n in lead else (lambda t: t)
        wrap = (lambda t: t[None]) if n in lead else (lambda t: t)
        g_n = g[n]
        if n in updated:
            d_n, m_n, v_n, g_n = updated[n]
        elif n == "w_in":
            d_n, m_n, v_n, g_n = _adamw(strip(w[n]), g_n, strip(m[n]), strip(v[n]), f"adamw_{n}", copy_grad=True)
        else:
            d_n, m_n, v_n = _adamw(strip(w[n]), g_n, strip(m[n]), strip(v[n]), f"adamw_{n}")
        grads.append(wrap(g_n))
        deltas.append(wrap(d_n))
        new_m.append(wrap(m_n))
        new_v.append(wrap(v_n))
    return (loss, grad_x[None], *grads, *deltas, *new_m, *new_v)
```

```python
import functools

import jax
import jax.numpy as jnp
from jax import lax
from jax.experimental import pallas as pl
from jax.experimental.pallas import tpu as pltpu
from jax.experimental.pallas import tpu_sc as plsc

F32 = jnp.float32
BF16 = jnp.bfloat16

S = 2048
D = 2048
H = 8
DH = 128
HW = H * DH
CH = 64
NCH = S // CH
DFF = 5632
INC = 7 * HW
BLK = 128
DILS = (1, 4, 16)
EPS = 1e-6
NEG = -1e30
N_CHIPS = 4
N_DEV = 8

ADAM_LR = 0.001
ADAM_B1 = 0.9
ADAM_B2 = 0.999
ADAM_EPS = 1e-08
ADAM_WD = 0.01
ADAM_STEP = 10
ADAM_BLOCK_BYTES = 1 << 21

V7X_VMEM_BYTES = 64 * 1024 * 1024
VMEM_LIMIT = (V7X_VMEM_BYTES * 3) // 4
MESH = pl.DeviceIdType.MESH
HBM_SPEC = pl.BlockSpec(memory_space=pltpu.HBM)
VMEM_SPEC = pl.BlockSpec(memory_space=pltpu.VMEM)

NN = (((1,), (0,)), ((), ()))
NT = (((1,), (1,)), ((), ()))
TN = (((0,), (0,)), ((), ()))


def _params(n_grid):
    return pltpu.CompilerParams(dimension_semantics=("arbitrary",) * n_grid, vmem_limit_bytes=VMEM_LIMIT)


def _dot(a, b, dims=NN):
    return lax.dot_general(a.astype(BF16), b.astype(BF16), dims, preferred_element_type=F32)


def _dot_f32(a, b):
    return lax.dot_general(a, b, NN, precision=lax.Precision.HIGHEST, preferred_element_type=F32)


def _sigmoid(x):
    return 1.0 / (1.0 + jnp.exp(-x))


def _pick(n, cands):
    for t in cands:
        if n % t == 0:
            return t
    raise ValueError(n)


def _matmul(a, b, mode, out_dtype, name, phase=None, m_blocks=None):
    if mode == "nn":
        (m, k), (_, n) = a.shape, b.shape
    elif mode == "nt":
        (m, k), (n, _) = a.shape, b.shape
    else:
        (k, m), (_, n) = a.shape, b.shape
    tm = _pick(m, (1024, 1408, 512))
    a_block = lambda i: i
    if m_blocks is not None:
        tm, blocks = m_blocks
        m = tm * len(blocks)
        step = blocks[1] - blocks[0] if len(blocks) > 1 else 0
        assert all(blk == blocks[0] + step * i for i, blk in enumerate(blocks))
        a_block = lambda i: blocks[0] + step * i
    tn = _pick(n, (1024, 1408, 512))
    tk = _pick(k, (2048, 2816, 1792, 1024, 512))
    nk = k // tk
    dims = {"nn": NN, "nt": NT, "tn": TN}[mode]

    def body(a_ref, b_ref, o_ref, *acc):
        part = lax.dot_general(a_ref[...], b_ref[...], dims, preferred_element_type=F32)
        if nk == 1:
            o_ref[...] = part.astype(o_ref.dtype)
            return
        acc_ref, kk = acc[0], pl.program_id(2)

        @pl.when(kk == 0)
        def _():
            acc_ref[...] = part

        @pl.when(jnp.logical_and(kk > 0, kk < nk - 1))
        def _():
            acc_ref[...] += part

        @pl.when(kk == nk - 1)
        def _():
            o_ref[...] = (acc_ref[...] + part).astype(o_ref.dtype)

    if mode == "tn":
        a_spec = pl.BlockSpec((tk, tm), lambda i, j, kk: (kk, a_block(i)))
    else:
        a_spec = pl.BlockSpec((tm, tk), lambda i, j, kk: (i, kk))
    if mode == "nt":
        b_spec = pl.BlockSpec((tn, tk), lambda i, j, kk: (j, kk))
    else:
        b_spec = pl.BlockSpec((tk, tn), lambda i, j, kk: (kk, j))
    return _pcall(
        body, [a, b], phase, name=name,
        out_shape=jax.ShapeDtypeStruct((m, n), out_dtype),
        grid=(m // tm, n // tn, nk),
        in_specs=[a_spec, b_spec],
        out_specs=pl.BlockSpec((tm, tn), lambda i, j, kk: (i, j)),
        scratch_shapes=[pltpu.VMEM((tm, tn), F32)] if nk > 1 else [],
        compiler_params=_params(3),
    )


TR = 256


def _row_spec(cols=D):
    return pl.BlockSpec((TR, cols), lambda i: (i, 0))


def _vec_spec(cols=D):
    return pl.BlockSpec((1, cols), lambda i: (0, 0))


def _norm_mod(x, nw, scale, shift, name):
    def body(x_ref, nw_ref, sc_ref, sh_ref, h_ref):
        xv = x_ref[...]
        r = lax.rsqrt(jnp.mean(xv * xv, axis=-1, keepdims=True) + EPS)
        h_ref[...] = ((xv * r) * nw_ref[...] * (1.0 + sc_ref[...]) + sh_ref[...]).astype(BF16)

    return pl.pallas_call(
        body, name=name, out_shape=jax.ShapeDtypeStruct((S, D), BF16), grid=(S // TR,),
        in_specs=[_row_spec(), _vec_spec(), _vec_spec(), _vec_spec()], out_specs=_row_spec(),
        compiler_params=_params(1),
    )(x, nw, scale, shift)


def _resid_norm_mod(x, mix, gate, nw, scale, shift, name):
    def body(x_ref, mix_ref, g_ref, nw_ref, sc_ref, sh_ref, x1_ref, h_ref):
        xv = x_ref[...] + g_ref[...] * mix_ref[...]
        x1_ref[...] = xv
        r = lax.rsqrt(jnp.mean(xv * xv, axis=-1, keepdims=True) + EPS)
        h_ref[...] = ((xv * r) * nw_ref[...] * (1.0 + sc_ref[...]) + sh_ref[...]).astype(BF16)

    return pl.pallas_call(
        body, name=name,
        out_shape=(jax.ShapeDtypeStruct((S, D), F32), jax.ShapeDtypeStruct((S, D), BF16)), grid=(S // TR,),
        in_specs=[_row_spec(), _row_spec(), _vec_spec(), _vec_spec(), _vec_spec(), _vec_spec()],
        out_specs=(_row_spec(), _row_spec()),
        compiler_params=_params(1),
    )(x, mix, gate, nw, scale, shift)


def _norm_mod_bwd(dh, xin, nw, scale, dres, name, mix=None, gate=None, phase=None):
    with_gate = mix is not None

    def body(*refs):
        if with_gate:
            dh_ref, x_ref, nw_ref, sc_ref, dres_ref, mix_ref, g_ref, dx_ref, dsh_ref, dsc_ref, dnw_ref, dg_ref, dmix_ref = refs
        else:
            dh_ref, x_ref, nw_ref, sc_ref, dres_ref, dx_ref, dsh_ref, dsc_ref, dnw_ref = refs
        i = pl.program_id(0)
        dhv = dh_ref[...]
        xv = x_ref[...]
        r = lax.rsqrt(jnp.mean(xv * xv, axis=-1, keepdims=True) + EPS)
        xn = xv * r
        nwv = nw_ref[...]
        one_sc = 1.0 + sc_ref[...]
        dxn = dhv * nwv * one_sc
        dx = dres_ref[...] + r * (dxn - xn * jnp.mean(dxn * xn, axis=-1, keepdims=True))
        dx_ref[...] = dx

        @pl.when(i == 0)
        def _():
            dsh_ref[...] = jnp.zeros_like(dsh_ref)
            dsc_ref[...] = jnp.zeros_like(dsc_ref)
            dnw_ref[...] = jnp.zeros_like(dnw_ref)
            if with_gate:
                dg_ref[...] = jnp.zeros_like(dg_ref)

        dsh_ref[...] += jnp.sum(dhv, axis=0, keepdims=True)
        dsc_ref[...] += jnp.sum(dhv * xn * nwv, axis=0, keepdims=True)
        dnw_ref[...] += jnp.sum(dhv * xn * one_sc, axis=0, keepdims=True)
        if with_gate:
            dg_ref[...] += jnp.sum(dx * mix_ref[...], axis=0, keepdims=True)
            dmix_ref[...] = (dx * g_ref[...]).astype(BF16)

    vec = jax.ShapeDtypeStruct((1, D), F32)
    ins = [dh, xin, nw, scale, dres]
    in_specs = [_row_spec(), _row_spec(), _vec_spec(), _vec_spec(), _row_spec()]
    outs = [jax.ShapeDtypeStruct((S, D), F32), vec, vec, vec]
    out_specs = [_row_spec(), _vec_spec(), _vec_spec(), _vec_spec()]
    if with_gate:
        ins += [mix, gate]
        in_specs += [_row_spec(), _vec_spec()]
        outs += [vec, jax.ShapeDtypeStruct((S, D), BF16)]
        out_specs += [_vec_spec(), _row_spec()]
    return _pcall(
        body, ins, phase, name=name, out_shape=tuple(outs), grid=(S // TR,), in_specs=in_specs,
        out_specs=tuple(out_specs), compiler_params=_params(1),
    )


def _tri(lower):
    row = lax.broadcasted_iota(jnp.int32, (CH, CH), 0)
    col = lax.broadcasted_iota(jnp.int32, (CH, CH), 1)
    return (row >= col) if lower else (col >= row)


def _hgrn_gates(hq, hf, lb):
    sig = _sigmoid(hf)
    f = lb + (1.0 - lb) * sig
    g = jnp.log(f)
    sq = _sigmoid(hq)
    return sig, f, g, 1.0 - f, hq * sq, sq


HG_HP = 2
HG_UNROLL = 16


def _proj_col_spec(group):
    return pl.BlockSpec((S, HG_HP * DH), lambda h: (0, group * (H // HG_HP) + h))


def _hgrn_fwd(proj, lb_logits, norm_w, phase=None):
    def body(hq_ref, hf_ref, hi_ref, hg_ref, lbl_ref, nw_ref, out_ref, oraw_ref, st_ref, s_ref):
        nw = nw_ref[...]
        lower = _tri(True)
        ltri = lower.astype(F32)
        s_ref[...] = jnp.zeros_like(s_ref)

        def chunk(n, carry):
            rows = pl.ds(pl.multiple_of(n * CH, CH), CH)
            for j in range(HG_HP):
                cols = slice(j * DH, (j + 1) * DH)
                lb = 1.0 / (1.0 + jnp.exp(lbl_ref[1:2, cols] - lbl_ref[0:1, cols]))
                hq, hf, v, hg = hq_ref[rows, cols], hf_ref[rows, cols], hi_ref[rows, cols], hg_ref[rows, cols]
                _, _, g, kk, q, _ = _hgrn_gates(hq, hf, lb)
                gc = _dot_f32(ltri, g)
                gl = jnp.sum(g, axis=0, keepdims=True)
                qe = q * jnp.exp(gc)
                ke = kk * jnp.exp(gl - gc)
                qh = q * jnp.exp(gc - 0.5 * gl)
                kh = kk * jnp.exp(0.5 * gl - gc)
                st = s_ref[j]
                st_ref[j, pl.ds(n, 1)] = st[None]
                a = jnp.where(lower, _dot(qh, kh, NT), 0.0)
                o = _dot(qe, st, NT) + _dot(a, v)
                s_ref[j] = st * jnp.exp(gl) + _dot(v, ke, TN)
                oraw_ref[rows, cols] = o
                rs = lax.rsqrt(jnp.mean(o * o, axis=-1, keepdims=True) + EPS)
                out_ref[rows, cols] = (o * rs * nw * (hg * _sigmoid(hg))).astype(BF16)
            return carry

        lax.fori_loop(0, NCH, chunk, 0, unroll=HG_UNROLL)

    head_spec = pl.BlockSpec((S, HG_HP * DH), lambda h: (0, h))
    return _pcall(
        body, [proj, proj, proj, proj, lb_logits, norm_w], phase, name="hgrn_fwd",
        out_shape=(jax.ShapeDtypeStruct((S, 2 * HW), BF16), jax.ShapeDtypeStruct((S, HW), F32),
                   jax.ShapeDtypeStruct((H, NCH, DH, DH), F32)),
        grid=(H // HG_HP,),
        in_specs=[_proj_col_spec(0), _proj_col_spec(1), _proj_col_spec(2), _proj_col_spec(3),
                  pl.BlockSpec((2, HG_HP * DH), lambda h: (0, h)), pl.BlockSpec((1, DH), lambda h: (0, 0))],
        out_specs=(head_spec, head_spec, pl.BlockSpec((HG_HP, NCH, DH, DH), lambda h: (h, 0, 0, 0))),
        scratch_shapes=[pltpu.VMEM((HG_HP, DH, DH), F32)],
        compiler_params=_params(1),
    )


def _hgrn_bwd(proj, lb_logits, norm_w, oraw, states, dout, phase=None):
    def body(hq_ref, hf_ref, hi_ref, hg_ref, lbl_ref, nw_ref, oraw_ref, st_ref, do_ref,
             dhq_ref, dhf_ref, dhi_ref, dhg_ref, dlb_ref, dnw_ref, ds_ref, acc_ref):
        h = pl.program_id(0)
        nw = nw_ref[...]
        lower = _tri(True)
        ltri = lower.astype(F32)
        utri = _tri(False).astype(F32)
        last = lax.broadcasted_iota(jnp.int32, (CH, DH), 0) == CH - 1
        ds_ref[...] = jnp.zeros_like(ds_ref)
        acc_ref[...] = jnp.zeros_like(acc_ref)

        @pl.when(h == 0)
        def _():
            dnw_ref[...] = jnp.zeros_like(dnw_ref)

        def chunk(i, carry):
            n = NCH - 1 - i
            rows = pl.ds(pl.multiple_of(n * CH, CH), CH)
            for j in range(HG_HP):
                cols = slice(j * DH, (j + 1) * DH)
                lb = 1.0 / (1.0 + jnp.exp(lbl_ref[1:2, cols] - lbl_ref[0:1, cols]))
                hq, hf, v, hg = hq_ref[rows, cols], hf_ref[rows, cols], hi_ref[rows, cols], hg_ref[rows, cols]
                sig, f, g, kk, q, sq = _hgrn_gates(hq, hf, lb)
                gc = _dot_f32(ltri, g)
                gl = jnp.sum(g, axis=0, keepdims=True)
                eg = jnp.exp(gc)
                ek = jnp.exp(gl - gc)
                gam = jnp.exp(gl)
                ph = jnp.exp(gc - 0.5 * gl)
                pk = jnp.exp(0.5 * gl - gc)
                qe, ke = q * eg, kk * ek
                qh, kh = q * ph, kk * pk
                st = st_ref[j, pl.ds(n, 1)][0]
                dst = ds_ref[j]
                a = jnp.where(lower, _dot(qh, kh, NT), 0.0)
                o = oraw_ref[rows, cols]
                rs = lax.rsqrt(jnp.mean(o * o, axis=-1, keepdims=True) + EPS)
                xh = o * rs
                sg = _sigmoid(hg)
                dgo = do_ref[rows, cols]
                d_on = dgo * (hg * sg)
                dhg_ref[rows, cols] = (dgo * xh * nw * (sg * (1.0 + hg * (1.0 - sg)))).astype(BF16)
                acc_ref[j, 1:2, :] += jnp.sum(d_on * xh, axis=0, keepdims=True)
                dy = d_on * nw
                do = rs * (dy - xh * jnp.mean(dy * xh, axis=-1, keepdims=True))
                dqe = _dot(do, st)
                da = jnp.where(lower, _dot(do, v, NT), 0.0)
                dv = _dot(a, do, TN) + _dot(ke, dst, NT)
                dke = _dot(v, dst)
                dgam = jnp.sum(dst * st, axis=0, keepdims=True)
                dqh = lax.dot_general(da, kh, NN, precision=lax.Precision.HIGH, preferred_element_type=F32)
                dkh = lax.dot_general(da, qh, TN, precision=lax.Precision.HIGH, preferred_element_type=F32)
                dq = dqh * ph + dqe * eg
                dk = dkh * pk + dke * ek
                dgl = jnp.sum(dke * ke, axis=0, keepdims=True) + dgam * gam
                dgc = dqh * qh - dkh * kh + dqe * qe - dke * ke + jnp.where(last, dgl, 0.0)
                dg = _dot_f32(utri, dgc)
                df = dg / f - dk
                dhf_ref[rows, cols] = (df * (1.0 - lb) * sig * (1.0 - sig)).astype(BF16)
                acc_ref[j, 0:1, :] += jnp.sum(df * (1.0 - sig), axis=0, keepdims=True)
                dhq_ref[rows, cols] = (dq * (sq * (1.0 + hq * (1.0 - sq)))).astype(BF16)
                dhi_ref[rows, cols] = dv.astype(BF16)
                ds_ref[j] = dst * gam + _dot(do, qe, TN)
            return carry

        lax.fori_loop(0, NCH, chunk, 0, unroll=HG_UNROLL)
        for j in range(HG_HP):
            dlb_ref[:, j * DH:(j + 1) * DH] = acc_ref[j, 0:1, :]
            dnw_ref[...] += acc_ref[j, 1:2, :]

    head_spec = pl.BlockSpec((S, HG_HP * DH), lambda h: (0, h))
    head_out = jax.ShapeDtypeStruct((S, HW), BF16)
    return _pcall(
        body, [proj, proj, proj, proj, lb_logits, norm_w, oraw, states, dout], phase, name="hgrn_bwd",
        out_shape=(head_out, head_out, head_out, head_out,
                   jax.ShapeDtypeStruct((1, HW), F32), jax.ShapeDtypeStruct((1, DH), F32)),
        grid=(H // HG_HP,),
        in_specs=[_proj_col_spec(0), _proj_col_spec(1), _proj_col_spec(2), _proj_col_spec(3),
                  pl.BlockSpec((2, HG_HP * DH), lambda h: (0, h)), pl.BlockSpec((1, DH), lambda h: (0, 0)),
                  head_spec, pl.BlockSpec((HG_HP, NCH, DH, DH), lambda h: (h, 0, 0, 0)), head_spec],
        out_specs=(head_spec, head_spec, head_spec, head_spec,
                   pl.BlockSpec((1, HG_HP * DH), lambda h: (0, h)), pl.BlockSpec((1, DH), lambda h: (0, 0))),
        scratch_shapes=[pltpu.VMEM((HG_HP, DH, DH), F32), pltpu.VMEM((HG_HP, 8, DH), F32)],
        compiler_params=_params(1),
    )


ATT_SCALE = DH ** -0.5
HEADS_PER_STEP = {1: 8, 4: 1, 16: 1}


def _sub_rows(ref, r, dil, cols):
    if dil == 1:
        return ref[:, cols]
    return ref[pl.ds(r, BLK, stride=dil), cols]


def _set_sub_rows(ref, r, dil, cols, val):
    if dil == 1:
        ref[:, cols] = val
    else:
        ref[pl.ds(r, BLK, stride=dil), cols] = val


def _slopes(dil):
    hp = HEADS_PER_STEP[dil]
    s = jnp.asarray([dil * 2.0 ** (-(h + 1)) for h in range(H)], F32)
    return jnp.broadcast_to(s[:, None], (H, DH)).reshape(H // hp, hp, DH)


def _rms_rows(x):
    rs = lax.rsqrt(jnp.mean(x * x, axis=-1, keepdims=True) + EPS)
    return x * rs, rs


def _att_masks():
    qi = lax.broadcasted_iota(jnp.int32, (BLK, BLK), 0)
    kj = lax.broadcasted_iota(jnp.int32, (BLK, BLK), 1)
    steps_c = qi - kj
    steps_p = qi - kj + BLK
    return steps_c, steps_p, steps_c >= 0, steps_p <= BLK


def _qk_prep(proj, q_w, k_w):
    def body(q_ref, k_ref, qw_ref, kw_ref, qn_ref, kn_ref):
        for h in range(H):
            cols = slice(h * DH, (h + 1) * DH)
            qn_ref[:, cols] = _rms_rows(q_ref[:, cols])[0] * qw_ref[...]
            kn_ref[:, cols] = _rms_rows(k_ref[:, cols])[0] * kw_ref[...]

    out = jax.ShapeDtypeStruct((S, HW), F32)
    wspec = pl.BlockSpec((1, DH), lambda i: (0, 0))
    return pl.pallas_call(
        body, name="qk_prep", out_shape=(out, out), grid=(S // TR,),
        in_specs=[pl.BlockSpec((TR, HW), lambda i: (i, 4)), pl.BlockSpec((TR, HW), lambda i: (i, 5)), wspec, wspec],
        out_specs=(_row_spec(HW), _row_spec(HW)),
        compiler_params=_params(1),
    )(proj, proj, q_w, k_w)


def _qk_norm_bwd(proj, dqn, dkn, dv, q_w, k_w):
    def body(q_ref, k_ref, dqn_ref, dkn_ref, dv_ref, qw_ref, kw_ref, dq_ref, dk_ref, dvo_ref, dqw_ref, dkw_ref):
        i = pl.program_id(0)

        @pl.when(i == 0)
        def _():
            dqw_ref[...] = jnp.zeros_like(dqw_ref)
            dkw_ref[...] = jnp.zeros_like(dkw_ref)

        dvo_ref[...] = dv_ref[...].astype(BF16)
        for x_ref, d_ref, w_ref, o_ref, dw_ref in ((q_ref, dqn_ref, qw_ref, dq_ref, dqw_ref),
                                                   (k_ref, dkn_ref, kw_ref, dk_ref, dkw_ref)):
            for h in range(H):
                cols = slice(h * DH, (h + 1) * DH)
                xh, rs = _rms_rows(x_ref[:, cols])
                d = d_ref[:, cols]
                dw_ref[...] += jnp.sum(d * xh, axis=0, keepdims=True)
                dy = d * w_ref[...]
                o_ref[:, cols] = (rs * (dy - xh * jnp.mean(dy * xh, axis=-1, keepdims=True))).astype(BF16)

    big = jax.ShapeDtypeStruct((S, HW), BF16)
    small = jax.ShapeDtypeStruct((1, DH), F32)
    wspec = pl.BlockSpec((1, DH), lambda i: (0, 0))
    return pl.pallas_call(
        body, name="qk_norm_bwd", out_shape=(big, big, big, small, small), grid=(S // TR,),
        in_specs=[pl.BlockSpec((TR, HW), lambda i: (i, 4)), pl.BlockSpec((TR, HW), lambda i: (i, 5)),
                  _row_spec(HW), _row_spec(HW), _row_spec(HW), wspec, wspec],
        out_specs=(_row_spec(HW), _row_spec(HW), _row_spec(HW), wspec, wspec),
        compiler_params=_params(1),
    )(proj, proj, dqn, dkn, dv, q_w, k_w)


def _attn_delta(do, o):
    def body(do_ref, o_ref, d_ref):
        for h in range(H):
            cols = slice(h * DH, (h + 1) * DH)
            d_ref[:, cols] = jnp.broadcast_to(jnp.sum(do_ref[:, cols] * o_ref[:, cols], axis=-1, keepdims=True), (TR, DH))

    return pl.pallas_call(
        body, name="attn_delta", out_shape=jax.ShapeDtypeStruct((S, HW), F32), grid=(S // TR,),
        in_specs=[pl.BlockSpec((TR, HW), lambda i: (i, 1)), _row_spec(HW)], out_specs=_row_spec(HW),
        compiler_params=_params(1),
    )(do, o)


def _attn_fwd(proj, qn, kn, dil, phase=None):
    hp = HEADS_PER_STEP[dil]
    rows, ng, nb = BLK * dil, H // hp, S // (BLK * dil)

    def body(q_ref, kc_ref, kp_ref, vc_ref, vp_ref, sl_ref, o_ref, lse_ref):
        n = pl.program_id(0)
        steps_c, steps_p, ok_c, ok_p = _att_masks()
        ok_p = jnp.logical_and(ok_p, n > 0)
        fc, fp = steps_c.astype(F32), steps_p.astype(F32)
        for hh in range(hp):
            cols = slice(hh * DH, (hh + 1) * DH)
            sl = sl_ref[0, hh:hh + 1, :]
            for r in range(dil):
                sub = lambda ref: _sub_rows(ref, r, dil, cols)
                qv = sub(q_ref)
                sc = jnp.where(ok_c, _dot(qv, sub(kc_ref), NT) * ATT_SCALE - sl * fc, NEG)
                sp = jnp.where(ok_p, _dot(qv, sub(kp_ref), NT) * ATT_SCALE - sl * fp, NEG)
                m = jnp.maximum(jnp.max(sc, axis=-1, keepdims=True), jnp.max(sp, axis=-1, keepdims=True))
                pc, pp = jnp.exp(sc - m), jnp.exp(sp - m)
                den = jnp.sum(pc, axis=-1, keepdims=True) + jnp.sum(pp, axis=-1, keepdims=True)
                o = (_dot(pc, sub(vc_ref)) + _dot(pp, sub(vp_ref))) / den
                _set_sub_rows(o_ref, r, dil, cols, o)
                _set_sub_rows(lse_ref, r, dil, cols, jnp.broadcast_to(m + jnp.log(den), (BLK, DH)))

    cur = pl.BlockSpec((rows, hp * DH), lambda n, g: (n, g))
    prev = pl.BlockSpec((rows, hp * DH), lambda n, g: (jnp.maximum(n - 1, 0), g))
    vcur = pl.BlockSpec((rows, hp * DH), lambda n, g: (n, 6 * ng + g))
    vprev = pl.BlockSpec((rows, hp * DH), lambda n, g: (jnp.maximum(n - 1, 0), 6 * ng + g))
    out = jax.ShapeDtypeStruct((S, HW), F32)
    return _pcall(
        body, [qn, kn, kn, proj, proj, _slopes(dil)], phase,
        name=f"attn_fwd_d{dil}", out_shape=(out, out), grid=(nb, ng),
        in_specs=[cur, cur, prev, vcur, vprev, pl.BlockSpec((1, hp, DH), lambda n, g: (g, 0, 0))],
        out_specs=(cur, cur),
        compiler_params=_params(2),
    )


def _attn_merge(outs, lses, cat):
    def body(o1, o2, o3, l1, l2, l3, cat_ref, ob_ref, of_ref, lse_ref):
        a, b, c = l1[...], l2[...], l3[...]
        m = jnp.maximum(jnp.maximum(a, b), c)
        ea, eb, ec = jnp.exp(a - m), jnp.exp(b - m), jnp.exp(c - m)
        den = ea + eb + ec
        o = (ea * o1[...] + eb * o2[...] + ec * o3[...]) / den
        ob_ref[...] = o.astype(BF16)
        of_ref[...] = o
        lse_ref[...] = m + jnp.log(den)

    f = jax.ShapeDtypeStruct((S, HW), F32)
    return pl.pallas_call(
        body, name="attn_merge", out_shape=(jax.ShapeDtypeStruct((S, 2 * HW), BF16), f, f), grid=(S // TR,),
        in_specs=[_row_spec(HW)] * 6 + [pl.BlockSpec(memory_space=pl.ANY)],
        out_specs=(pl.BlockSpec((TR, HW), lambda i: (i, 1)), _row_spec(HW), _row_spec(HW)),
        input_output_aliases={6: 0},
        compiler_params=_params(1),
    )(*outs, *lses, cat)


def _attn_bwd(proj, qn, kn, lse, delta, do, dil, acc, phase=None):
    hp = HEADS_PER_STEP[dil]
    rows, ng, nb = BLK * dil, H // hp, S // (BLK * dil)
    has_acc = acc is not None

    def body(*refs):
        q_ref, qn_ref, kp_ref, kc_ref, vp_ref, vc_ref, l_ref, ln_ref, d_ref, dn_ref, do_ref, don_ref, sl_ref = refs[:13]
        refs = refs[13:]
        if has_acc:
            aq_ref, ak_ref, av_ref = refs[:3]
            refs = refs[3:]
        dq_ref, dk_ref, dv_ref = refs
        m = pl.program_id(0)
        steps_c, steps_p, ok_c, ok_p = _att_masks()
        ok_prev = jnp.logical_and(ok_p, m > 0)
        ok_next = jnp.logical_and(ok_p, m < nb - 1)
        fc, fp = steps_c.astype(F32), steps_p.astype(F32)
        for hh in range(hp):
            cols = slice(hh * DH, (hh + 1) * DH)
            sl = sl_ref[0, hh:hh + 1, :]
            for r in range(dil):
                sub = lambda ref: _sub_rows(ref, r, dil, cols)
                qv, qnv, kcv, kpv = sub(q_ref), sub(qn_ref), sub(kc_ref), sub(kp_ref)
                vc, vp = sub(vc_ref), sub(vp_ref)
                dov, donv = sub(do_ref), sub(don_ref)
                lse_m, lse_n = sub(l_ref)[:, 0:1], sub(ln_ref)[:, 0:1]
                delta_m, delta_n = sub(d_ref)[:, 0:1], sub(dn_ref)[:, 0:1]
                p_c = jnp.where(ok_c, jnp.exp(_dot(qv, kcv, NT) * ATT_SCALE - sl * fc - lse_m), 0.0)
                p_p = jnp.where(ok_prev, jnp.exp(_dot(qv, kpv, NT) * ATT_SCALE - sl * fp - lse_m), 0.0)
                p_n = jnp.where(ok_next, jnp.exp(_dot(qnv, kcv, NT) * ATT_SCALE - sl * fp - lse_n), 0.0)
                ds_c = p_c * (_dot(dov, vc, NT) - delta_m)
                ds_p = p_p * (_dot(dov, vp, NT) - delta_m)
                ds_n = p_n * (_dot(donv, vc, NT) - delta_n)
                dqn = (_dot(ds_c, kcv) + _dot(ds_p, kpv)) * ATT_SCALE
                dkn = (_dot(ds_c, qv, TN) + _dot(ds_n, qnv, TN)) * ATT_SCALE
                dv = _dot(p_c, dov, TN) + _dot(p_n, donv, TN)
                if has_acc:
                    dqn, dkn, dv = dqn + sub(aq_ref), dkn + sub(ak_ref), dv + sub(av_ref)
                _set_sub_rows(dq_ref, r, dil, cols, dqn)
                _set_sub_rows(dk_ref, r, dil, cols, dkn)
                _set_sub_rows(dv_ref, r, dil, cols, dv)

    def shifted(shift, m):
        if shift < 0:
            return jnp.maximum(m - 1, 0)
        if shift > 0:
            return jnp.minimum(m + 1, nb - 1)
        return m

    def spec(shift, group=0):
        return pl.BlockSpec((rows, hp * DH), lambda m, g: (shifted(shift, m), group * ng + g))

    ins = [qn, qn, kn, kn, proj, proj, lse, lse, delta, delta, do, do, _slopes(dil)]
    in_specs = [spec(0), spec(1), spec(-1), spec(0), spec(-1, 6), spec(0, 6), spec(0), spec(1), spec(0), spec(1),
                spec(0, 1), spec(1, 1), pl.BlockSpec((1, hp, DH), lambda m, g: (g, 0, 0))]
    if has_acc:
        ins += list(acc)
        in_specs += [spec(0)] * 3
    big = jax.ShapeDtypeStruct((S, HW), F32)
    return _pcall(
        body, ins, phase, name=f"attn_bwd_d{dil}", out_shape=(big, big, big), grid=(nb, ng),
        in_specs=in_specs, out_specs=(spec(0),) * 3,
        compiler_params=_params(2),
    )


TC = 512
NCT = DFF // TC


def _shift_rows(a, k):
    rows = lax.broadcasted_iota(jnp.int32, a.shape, 0)
    rolled = pltpu.roll(a, k % S, 0)
    if k > 0:
        return jnp.where(rows >= k, rolled, 0.0)
    return jnp.where(rows < S + k, rolled, 0.0)


def _conv_pre(a, w_ref, b_ref):
    return b_ref[...] + w_ref[0:1, :] * _shift_rows(a, 2) + w_ref[1:2, :] * _shift_rows(a, 1) + w_ref[2:3, :] * a


def _conv_gate_fwd(u, conv_w, conv_b, name, tiles=(0, NCT), into=None, phase=None):
    t0, t1 = tiles

    def body(a_ref, g_ref, w_ref, b_ref, *rest):
        pre = _conv_pre(a_ref[...].astype(F32), w_ref, b_ref)
        rest[-1][...] = (pre * _sigmoid(pre) * g_ref[...].astype(F32)).astype(BF16)

    args = [u, u, conv_w, conv_b]
    in_specs = [pl.BlockSpec((S, TC), lambda i: (0, t0 + i)), pl.BlockSpec((S, TC), lambda i: (0, NCT + t0 + i)),
                pl.BlockSpec((3, TC), lambda i: (0, t0 + i)), pl.BlockSpec((1, TC), lambda i: (0, t0 + i))]
    kw = {}
    if into is not None:
        args.append(into)
        in_specs.append(pl.BlockSpec(memory_space=pl.ANY))
        kw["input_output_aliases"] = {4: 0}
    return _pcall(
        body, args, phase, name=name, out_shape=jax.ShapeDtypeStruct((S, DFF), BF16), grid=(t1 - t0,),
        in_specs=in_specs, out_specs=pl.BlockSpec((S, TC), lambda i: (0, t0 + i)),
        compiler_params=_params(1), **kw,
    )


def _conv_gate_bwd(dy, u, conv_w, conv_b, phase=None):
    def body(dy_ref, a_ref, g_ref, w_ref, b_ref, du_ref, db_ref, dw_ref, da_buf, dg_buf, sems):
        i = pl.program_id(0)
        slot = i % 2

        def writes(step, s):
            col = pl.multiple_of(step * TC, TC)
            return (pltpu.make_async_copy(da_buf.at[s], du_ref.at[:, pl.ds(col, TC)], sems.at[0, s]),
                    pltpu.make_async_copy(dg_buf.at[s], du_ref.at[:, pl.ds(DFF + col, TC)], sems.at[1, s]))

        @pl.when(i >= 2)
        def _():
            for cp in writes(i - 2, slot):
                cp.wait()

        a = a_ref[...].astype(F32)
        dyv = dy_ref[...].astype(F32)
        pre = _conv_pre(a, w_ref, b_ref)
        sg = _sigmoid(pre)
        dg_buf[slot] = (dyv * pre * sg).astype(BF16)
        dpre = dyv * g_ref[...].astype(F32) * (sg * (1.0 + pre * (1.0 - sg)))
        da = w_ref[2:3, :] * dpre + w_ref[1:2, :] * _shift_rows(dpre, -1) + w_ref[0:1, :] * _shift_rows(dpre, -2)
        da_buf[slot] = da.astype(BF16)
        for cp in writes(i, slot):
            cp.start()
        db_ref[...] = jnp.sum(dpre, axis=0, keepdims=True)
        dw_ref[0:1, :] = jnp.sum(dpre * _shift_rows(a, 2), axis=0, keepdims=True)
        dw_ref[1:2, :] = jnp.sum(dpre * _shift_rows(a, 1), axis=0, keepdims=True)
        dw_ref[2:3, :] = jnp.sum(dpre * a, axis=0, keepdims=True)

        @pl.when(i == NCT - 1)
        def _():
            for cp in writes(i - 1, 1 - slot) + writes(i, slot):
                cp.wait()

    col = pl.BlockSpec((S, TC), lambda i: (0, i))
    return _pcall(
        body, [dy, u, u, conv_w, conv_b], phase, name="conv_gate_bwd",
        out_shape=(jax.ShapeDtypeStruct((S, 2 * DFF), BF16), jax.ShapeDtypeStruct((1, DFF), F32),
                   jax.ShapeDtypeStruct((3, DFF), F32)),
        grid=(NCT,),
        in_specs=[col, col, pl.BlockSpec((S, TC), lambda i: (0, NCT + i)),
                  pl.BlockSpec((3, TC), lambda i: (0, i)), pl.BlockSpec((1, TC), lambda i: (0, i))],
        out_specs=(pl.BlockSpec(memory_space=pl.ANY), pl.BlockSpec((1, TC), lambda i: (0, i)),
                   pl.BlockSpec((3, TC), lambda i: (0, i))),
        scratch_shapes=[pltpu.VMEM((2, S, TC), BF16), pltpu.VMEM((2, S, TC), BF16), pltpu.SemaphoreType.DMA((2, 2))],
        compiler_params=_params(1),
    )


def _out_loss(z, x1, target, gate):
    def body(z_ref, x_ref, t_ref, g_ref, do_ref, dz_ref, dg_ref, loss_ref):
        i = pl.program_id(0)
        zv = z_ref[...]
        gv = g_ref[...]
        err = x_ref[...] + gv * zv - t_ref[...]
        dout = err * (1.0 / D)
        do_ref[...] = dout
        dz_ref[...] = (dout * gv).astype(BF16)

        @pl.when(i == 0)
        def _():
            dg_ref[...] = jnp.zeros_like(dg_ref)
            loss_ref[...] = jnp.zeros_like(loss_ref)

        dg_ref[...] += jnp.sum(dout * zv, axis=0, keepdims=True)
        part = jnp.sum(jnp.sum(err * err, axis=0, keepdims=True), axis=-1, keepdims=True) * (0.5 / D)
        lane = lax.broadcasted_iota(jnp.int32, (1, 128), 1)
        loss_ref[...] += jnp.where(lane == 0, part, 0.0)

    return pl.pallas_call(
        body, name="out_loss",
        out_shape=(jax.ShapeDtypeStruct((S, D), F32), jax.ShapeDtypeStruct((S, D), BF16),
                   jax.ShapeDtypeStruct((1, D), F32), jax.ShapeDtypeStruct((1, 128), F32)),
        grid=(S // TR,),
        in_specs=[_row_spec(), _row_spec(), _row_spec(), _vec_spec()],
        out_specs=(_row_spec(), _row_spec(), _vec_spec(), _vec_spec(128)),
        compiler_params=_params(1),
    )(z, x1, target, gate)


ADA_COLS = 6 * D // N_CHIPS
TA = 512


def _ada_fwd(c_all, w_shard, b_shard):
    def body(c_ref, w_ref, b_ref, o_ref):
        cv = c_ref[...]
        o_ref[...] = _dot_f32(cv * _sigmoid(cv), w_ref[...]) + b_ref[...]

    return pl.pallas_call(
        body, name="ada_fwd", out_shape=jax.ShapeDtypeStruct((N_DEV, ADA_COLS), F32), grid=(ADA_COLS // TA,),
        in_specs=[pl.BlockSpec((N_DEV, D), lambda j: (0, 0)), pl.BlockSpec((D, TA), lambda j: (0, j)),
                  pl.BlockSpec((1, TA), lambda j: (0, j))],
        out_specs=pl.BlockSpec((N_DEV, TA), lambda j: (0, j)),
        compiler_params=_params(1),
    )(c_all, w_shard, b_shard)


ADA_ROWS = 128


def _ada_bwd_adamw(c_all, dmod_shard, w, m, v):
    def body(c_ref, d_ref, w_ref, m_ref, v_ref, g_ref, dl_ref, mo_ref, vo_ref):
        cv = c_ref[...]
        gv = lax.dot_general(cv * _sigmoid(cv), d_ref[...], TN, precision=lax.Precision.HIGHEST,
                             preferred_element_type=F32)
        g_ref[...] = gv
        m2 = ADAM_B1 * m_ref[...] + (1.0 - ADAM_B1) * gv
        v2 = ADAM_B2 * v_ref[...] + (1.0 - ADAM_B2) * (gv * gv)
        m_hat = m2 / (1.0 - ADAM_B1 ** ADAM_STEP)
        v_hat = v2 / (1.0 - ADAM_B2 ** ADAM_STEP)
        dl_ref[...] = -ADAM_LR * (m_hat / (jnp.sqrt(v_hat) + ADAM_EPS) + ADAM_WD * w_ref[...])
        mo_ref[...] = m2
        vo_ref[...] = v2

    spec = pl.BlockSpec((ADA_ROWS, ADA_COLS), lambda i: (i, 0))
    out = jax.ShapeDtypeStruct((D, ADA_COLS), F32)
    return pl.pallas_call(
        body, name="ada_bwd_adamw", out_shape=(out,) * 4, grid=(D // ADA_ROWS,),
        in_specs=[pl.BlockSpec((N_DEV, ADA_ROWS), lambda i: (0, i)), pl.BlockSpec((N_DEV, ADA_COLS), lambda i: (0, 0)),
                  spec, spec, spec],
        out_specs=(spec,) * 4,
        compiler_params=_params(1),
    )(c_all, dmod_shard, w, m, v)


def _sum_rows(g, name):
    rows, n = g.shape

    def body(g_ref, o_ref):
        acc = g_ref[0:1, :]
        for i in range(1, rows):
            acc = acc + g_ref[i:i + 1, :]
        o_ref[...] = acc

    return pl.pallas_call(
        body, name=name, out_shape=jax.ShapeDtypeStruct((1, n), F32),
        in_specs=[VMEM_SPEC], out_specs=VMEM_SPEC,
    )(g)


def _lb_grad(lb_logits, dlb):
    def body(l_ref, d_ref, o_ref):
        p = 1.0 / (1.0 + jnp.exp(l_ref[1:2, :] - l_ref[0:1, :]))
        t = d_ref[...] * p * (1.0 - p)
        o_ref[0:1, :] = t
        o_ref[1:2, :] = -t

    return pl.pallas_call(
        body, name="lb_grad", out_shape=jax.ShapeDtypeStruct((2, HW), F32),
        in_specs=[VMEM_SPEC, VMEM_SPEC], out_specs=VMEM_SPEC,
    )(lb_logits, dlb)


def _adamw(w, g, m, v, name, copy_grad=False):
    rows, cols = w.shape
    tr = rows
    if rows * cols * 4 > ADAM_BLOCK_BYTES:
        tr = _pick(rows, [t for t in (256, 128, 64, 32, 16, 8) if t * cols * 4 <= ADAM_BLOCK_BYTES])

    def body(w_ref, g_ref, m_ref, v_ref, d_ref, mo_ref, vo_ref, *go_ref):
        gv = g_ref[...]
        if copy_grad:
            go_ref[0][...] = gv
        m2 = ADAM_B1 * m_ref[...] + (1.0 - ADAM_B1) * gv
        v2 = ADAM_B2 * v_ref[...] + (1.0 - ADAM_B2) * (gv * gv)
        m_hat = m2 / (1.0 - ADAM_B1 ** ADAM_STEP)
        v_hat = v2 / (1.0 - ADAM_B2 ** ADAM_STEP)
        d_ref[...] = -ADAM_LR * (m_hat / (jnp.sqrt(v_hat) + ADAM_EPS) + ADAM_WD * w_ref[...])
        mo_ref[...] = m2
        vo_ref[...] = v2

    spec = pl.BlockSpec((tr, cols), lambda i: (i, 0))
    out = jax.ShapeDtypeStruct((rows, cols), F32)
    n_out = 4 if copy_grad else 3
    return pl.pallas_call(
        body, name=name, out_shape=(out,) * n_out, grid=(rows // tr,),
        in_specs=[spec] * 4, out_specs=(spec,) * n_out,
        compiler_params=_params(1),
    )(w, g, m, v)


SC_TILES = 32
SC_LANES = 16
SC_COL_PARTS = 2
SC_CHUNK_ROWS = 8


def _adamw_sc(w, g, m, v, name):
    rows, cols = w.shape
    band, part = rows // (SC_TILES // SC_COL_PARTS), cols // SC_COL_PARTS
    assert band % SC_CHUNK_ROWS == 0 and part % SC_LANES == 0, (rows, cols)

    def body(w_hbm, g_hbm, m_hbm, v_hbm, d_hbm, mo_hbm, vo_hbm, go_hbm, wb, gb, mb, vb, db):
        tile = lax.axis_index("sc_subcore") * 2 + lax.axis_index("sc_core")
        row0 = (tile // SC_COL_PARTS) * band
        col0 = (tile % SC_COL_PARTS) * part

        @pl.loop(0, band, step=SC_CHUNK_ROWS)
        def _(r):
            at = lambda ref: ref.at[pl.ds(row0 + r, SC_CHUNK_ROWS), pl.ds(col0, part)]
            pltpu.sync_copy(at(w_hbm), wb)
            pltpu.sync_copy(at(g_hbm), gb)
            pltpu.sync_copy(gb, at(go_hbm))
            pltpu.sync_copy(at(m_hbm), mb)
            pltpu.sync_copy(at(v_hbm), vb)

            @pl.loop(0, SC_CHUNK_ROWS)
            def _(i):
                @pl.loop(0, part, step=SC_LANES)
                def _(j):
                    sl = (i, pl.ds(j, SC_LANES))
                    gv = gb[sl]
                    m2 = ADAM_B1 * mb[sl] + (1.0 - ADAM_B1) * gv
                    v2 = ADAM_B2 * vb[sl] + (1.0 - ADAM_B2) * (gv * gv)
                    m_hat = m2 / (1.0 - ADAM_B1 ** ADAM_STEP)
                    v_hat = v2 / (1.0 - ADAM_B2 ** ADAM_STEP)
                    db[sl] = -ADAM_LR * (m_hat / (jnp.sqrt(v_hat) + ADAM_EPS) + ADAM_WD * wb[sl])
                    mb[sl] = m2
                    vb[sl] = v2

            pltpu.sync_copy(db, at(d_hbm))
            pltpu.sync_copy(mb, at(mo_hbm))
            pltpu.sync_copy(vb, at(vo_hbm))

    out = jax.ShapeDtypeStruct((rows, cols), F32)
    buf = pltpu.VMEM((SC_CHUNK_ROWS, part), F32)
    return pl.kernel(
        body, name=name, out_type=(out, out, out, out),
        mesh=plsc.VectorSubcoreMesh(core_axis_name="sc_core", subcore_axis_name="sc_subcore"),
        scratch_types=[buf] * 5,
    )(w, g, m, v)


W_IN, W_OUT, W_UP, W_DOWN = range(4)
IN_CHUNKS = 2
W_INQ = 4


def _sequence_step(x, target, mod, norm1_w, lb_logits, hg_norm_w, q_norm_w, k_norm_w, norm2_w, conv_w, conv_b, net):
    shift1, scale1, gate1, shift2, scale2, gate2 = [mod[:, i * D:(i + 1) * D] for i in range(6)]

    def run(name, fn, *args, **kw):
        phase = net.host(name)
        if phase is None:
            return fn(*args, **kw)
        res, comm = fn(*args, phase=phase, **kw)
        net.done(name, comm, res[0] if isinstance(res, tuple) else res)
        return res

    h = _norm_mod(x, norm1_w, scale1, shift1, "norm1_fwd")
    proj = run("proj_fwd", _matmul, h, net.full[W_IN], "nn", F32, "proj_fwd")
    cat, o_raw, states = run("hgrn_fwd", _hgrn_fwd, proj, lb_logits, hg_norm_w)
    qn, kn = _qk_prep(proj, q_norm_w, k_norm_w)
    att = [run(f"attn_fwd_d{dil}", _attn_fwd, proj, qn, kn, dil) for dil in DILS]
    cat, b_out_f32, lse = _attn_merge([t[0] for t in att], [t[1] for t in att], cat)
    mix = run("mix_fwd", _matmul, cat, net.full[W_OUT], "nn", F32, "mix_fwd")
    x1, h2 = _resid_norm_mod(x, mix, gate1, norm2_w, scale2, shift2, "norm2_fwd")
    u = run("up_fwd", _matmul, h2, net.full[W_UP], "nn", BF16, "up_fwd")
    yact = run("conv_gate_fwd_a", _conv_gate_fwd, u, conv_w, conv_b, "conv_gate_fwd_a", tiles=(0, NCT // 2 + 1))
    yact = run("conv_gate_fwd_b", _conv_gate_fwd, u, conv_w, conv_b, "conv_gate_fwd_b", tiles=(NCT // 2 + 1, NCT),
               into=yact)
    z =_matmul(yact, net.full[W_DOWN], "nn", F32, "down_fwd")
    dout, dz, dgate2, loss_row = _out_loss(z, x1, target, gate2)

    net.grad[W_DOWN] = _matmul(yact, dz, "tn", BF16, "down_bwd_w")
    dyact = run("down_bwd_x", _matmul, dz, net.full[W_DOWN], "nt", BF16, "down_bwd_x")
    du, dconv_b, dconv_w = run("conv_gate_bwd", _conv_gate_bwd, dyact, u, conv_w, conv_b)
    net.grad[W_UP] = run("up_bwd_w", _matmul, h2, du, "tn", BF16, "up_bwd_w")
    dh2 = run("up_bwd_x", _matmul, du, net.full[W_UP], "nt", F32, "up_bwd_x")
    dx1, dshift2, dscale2, dnorm2, dgate1, dmix = run(
        "norm2_bwd", _norm_mod_bwd, dh2, x1, norm2_w, scale2, dout, "norm2_bwd", mix=mix, gate=gate1)
    net.grad[W_OUT] = run("mix_bwd_w", _matmul, cat, dmix, "tn", BF16, "mix_bwd_w")
    dcat = run("mix_bwd_x", _matmul, dmix, net.full[W_OUT], "nt", F32, "mix_bwd_x")
    delta = _attn_delta(dcat, b_out_f32)
    acc = None
    for dil in DILS:
        acc = run(f"attn_bwd_d{dil}", _attn_bwd, proj, qn, kn, lse, delta, dcat, dil, acc)
    daq, dak, dav, dq_norm, dk_norm = _qk_norm_bwd(proj, *acc, q_norm_w, k_norm_w)
    dhq, dhf, dhi, dhg, dlb, dhg_norm = run("hgrn_bwd", _hgrn_bwd, proj, lb_logits, hg_norm_w, o_raw, states, dcat)
    dproj = jnp.concatenate([dhq, dhf, dhi, dhg, daq, dak, dav], axis=1)
    net.carry = dproj
    for q in range(IN_CHUNKS):
        net.after_chunk(q, run(f"proj_bwd_w_{q}", _matmul, h, net.carry, "tn", BF16, f"proj_bwd_w_{q}",
                               m_blocks=(D // IN_CHUNKS, [q])))
    dh = run("proj_bwd_x", _matmul, net.carry, net.full[W_IN], "nt", F32, "proj_bwd_x")
    grad_x, dshift1, dscale1, dnorm1 = run("norm1_bwd", _norm_mod_bwd, dh, x, norm1_w, scale1, dx1, "norm1_bwd")

    dmod = jnp.concatenate([dshift1, dscale1, dgate1, dshift2, dscale2, dgate2], axis=1)
    small = dict(norm1_w=dnorm1, lb=dlb, hg_norm_w=dhg_norm, q_norm_w=dq_norm, k_norm_w=dk_norm,
                 norm2_w=dnorm2, conv_b=dconv_b, conv_w=dconv_w)
    return loss_row, grad_x, dmod, small


def _place():
    x, y, c = lax.axis_index("x"), lax.axis_index("y"), lax.axis_index("c")
    others = [(1 - x, y), (x, 1 - y), (1 - x, 1 - y)]
    return x, y, c, others


def _remote(src, dst, send_sems, recv_sems, k, to):
    return pltpu.make_async_remote_copy(src_ref=src, dst_ref=dst, send_sem=send_sems.at[k], recv_sem=recv_sems.at[k],
                                        device_id=to, device_id_type=MESH)


class _Phase:
    def __init__(self):
        self.ins, self.outs, self.aliases, self.groups, self.n = [], [], {}, [], 0

    def add(self, ins, outs, aliases, n, copies):
        i0, o0 = len(self.ins), len(self.outs)
        self.ins += list(ins)
        self.outs += list(outs)
        self.aliases.update({i0 + i: o0 + o for i, o in aliases.items()})
        self.groups.append((slice(i0, i0 + len(ins)), slice(o0, o0 + len(outs)), copies))
        self.n += n
        return slice(o0, o0 + len(outs))

    def build(self, cin, cout, send_sems, recv_sems, landing):
        res = []
        for si, so, copies in self.groups:
            for src, dst, land, peer in copies(cin[si], cout[so]):
                k = len(res)
                res.append(_remote(land, land, send_sems, recv_sems, k, peer) if landing
                           else _remote(src, dst, send_sems, recv_sems, k, peer))
        assert len(res) == self.n
        return res


def _pcall(body, args, phase=None, **kw):
    if phase is None or not phase.groups:
        res = pl.pallas_call(body, **kw)(*args)
        return res if phase is None else (res, ())
    grid = tuple(kw.pop("grid", ()))
    out_shape, out_specs = kw.pop("out_shape"), kw.pop("out_specs")
    single = not isinstance(out_shape, (tuple, list))
    out_shape = [out_shape] if single else list(out_shape)
    out_specs = [out_specs] if single else list(out_specs)
    scratch = list(kw.pop("scratch_shapes", ()))
    aliases = dict(kw.pop("input_output_aliases", {}))
    n_in, n_out, n_ci, n_co = len(args), len(out_shape), len(phase.ins), len(phase.outs)
    aliases.update({n_in + i: n_out + o for i, o in phase.aliases.items()})

    def hosted(*refs):
        ins, cin = refs[:n_in], refs[n_in:n_in + n_ci]
        outs = refs[n_in + n_ci:n_in + n_ci + n_out]
        cout = refs[n_in + n_ci + n_out:n_in + n_ci + n_out + n_co]
        scr, send_sems, recv_sems = refs[n_in + n_ci + n_out + n_co:-2], refs[-2], refs[-1]
        ids = [pl.program_id(a) for a in range(len(grid))]

        def start():
            for cp in phase.build(cin, cout, send_sems, recv_sems, False):
                cp.start()

        def finish():
            for cp in phase.build(cin, cout, send_sems, recv_sems, False):
                cp.wait_send()
            for cp in phase.build(cin, cout, send_sems, recv_sems, True):
                cp.wait_recv()

        if grid:
            first = functools.reduce(jnp.logical_and, [i == 0 for i in ids])
            last = functools.reduce(jnp.logical_and, [i == g - 1 for i, g in zip(ids, grid)])
            pl.when(first)(start)
            body(*ins, *outs, *scr)
            pl.when(last)(finish)
        else:
            start()
            body(*ins, *outs, *scr)
            finish()

    res = pl.pallas_call(
        hosted, out_shape=tuple(out_shape + phase.outs), grid=grid,
        in_specs=list(kw.pop("in_specs")) + [HBM_SPEC] * n_ci, out_specs=tuple(out_specs + [HBM_SPEC] * n_co),
        input_output_aliases=aliases,
        scratch_shapes=scratch + [pltpu.SemaphoreType.DMA((phase.n,)), pltpu.SemaphoreType.DMA((phase.n,))],
        **kw,
    )(*args, *phase.ins)
    outs = res[:n_out]
    return (outs[0] if single else tuple(outs)), tuple(res[n_out:])


def _comm_only(name, phase):
    return _pcall(lambda: None, [], phase, name=name, out_shape=(), in_specs=[], out_specs=())[1]


def _all_gather_rows(block, name, phase=None):
    m_per, n = block.shape

    def body(x_ref, out_ref, send_sems, recv_sems, local_sem):
        x, y, c, chips = _place()
        me, sibling = (x, y, c), (x, y, 1 - c)

        def rows(px, py, pc):
            return out_ref.at[pl.ds((4 * px + 2 * py + pc) * m_per, m_per), :]

        def copy(k, blk, to, src=None):
            return _remote(rows(*blk) if src is None else src, rows(*blk), send_sems, recv_sems, k, to)

        mine = pltpu.make_async_copy(x_ref, rows(*me), local_sem)
        mine.start()
        first = [copy(0, me, sibling, src=x_ref)]
        first += [copy(1 + j, me, (*chip, c), src=x_ref) for j, chip in enumerate(chips)]
        for cp in first:
            cp.start()
        passed = [copy(4 + j, (*chip, c), sibling) for j, chip in enumerate(chips)]
        for j, chip in enumerate(chips):
            copy(1 + j, (*chip, c), me).wait_recv()
            passed[j].start()
        copy(0, sibling, me).wait_recv()
        for j, chip in enumerate(chips):
            copy(4 + j, (*chip, 1 - c), me).wait_recv()
        for cp in first + passed:
            cp.wait_send()
        mine.wait()

    return _pcall(
        body, [block], phase, name=name, out_shape=jax.ShapeDtypeStruct((N_DEV * m_per, n), block.dtype),
        in_specs=[VMEM_SPEC], out_specs=VMEM_SPEC,
        scratch_shapes=[pltpu.SemaphoreType.DMA((7,)), pltpu.SemaphoreType.DMA((7,)), pltpu.SemaphoreType.DMA],
    )


class _Geo:
    def __init__(self, kind, shard_shape):
        self.kind = kind
        self.shard_shape = tuple(shard_shape)
        self.hr, self.hc = shard_shape[0] // 2, shard_shape[1]
        if kind == "col":
            self.full_shape = (shard_shape[0], N_CHIPS * shard_shape[1])
        else:
            self.full_shape = (N_CHIPS * shard_shape[0], shard_shape[1])

    def full_half(self, ref, j, c):
        return self.piece(ref, j, c, 0, 1, 1)

    def piece(self, ref, j, c, p0, p1, pieces, part=None):
        pr = self.hr // pieces
        off, n = p0 * pr, (p1 - p0) * pr
        if part is not None:
            top = (n // 32) * 16
            off, n = (off, top) if part == 0 else (off + top, n - top)
        if self.kind == "col":
            return ref.at[pl.ds(pl.multiple_of(c * self.hr + off, 16), n),
                          pl.ds(pl.multiple_of(j * self.hc, 128), self.hc)]
        return ref.at[pl.ds(pl.multiple_of((2 * j + c) * self.hr + off, 16), n), :]


WEIGHT_KINDS = ("col", "row", "col", "row")
NW = len(WEIGHT_KINDS)


def _comm_call(body, name, ins, outs, n_remote, aliases=None):
    return pl.pallas_call(
        body, name=name, out_shape=tuple(outs),
        in_specs=[HBM_SPEC] * len(ins), out_specs=tuple([HBM_SPEC] * len(outs)),
        input_output_aliases=aliases or {},
        scratch_shapes=[pltpu.SemaphoreType.DMA((n_remote,)), pltpu.SemaphoreType.DMA((n_remote,))],
    )(*ins)


ROW_TILES = (256, 176, 128, 64, 32, 16)


def _cast_into_full(shard, geo, place, name):
    rs, cs = shard.shape
    tr = _pick(rs, ROW_TILES)
    nb = rs // tr

    def body(place_ref, s_ref, o_ref):
        o_ref[...] = s_ref[...].astype(BF16)

    if geo.kind == "col":
        out_map = lambda i, place_ref: (i, place_ref[0])
    else:
        out_map = lambda i, place_ref: (place_ref[0] * nb + i, 0)
    return pl.pallas_call(
        body, name=name, out_shape=jax.ShapeDtypeStruct(geo.full_shape, BF16),
        grid_spec=pltpu.PrefetchScalarGridSpec(
            num_scalar_prefetch=1, grid=(nb,),
            in_specs=[pl.BlockSpec((tr, cs), lambda i, place_ref: (i, 0))],
            out_specs=pl.BlockSpec((tr, cs), out_map)),
        compiler_params=_params(1),
    )(place, shard)


def _gather_weight(full, geo, pieces, name):
    per = 8

    def body(in_ref, ref, send_sems, recv_sems):
        x, y, c, _ = _place()
        j, jx, jy, jo = 2 * x + y, 2 * (1 - x) + y, 2 * x + (1 - y), 2 * (1 - x) + (1 - y)
        nx, ny, sib = (1 - x, y, c), (x, 1 - y, c), (x, y, 1 - c)

        def cp(region, k, to):
            return _remote(region, region, send_sems, recv_sems, k, to)

        def reg(chip, p, part=None, core=c):
            return geo.piece(ref, chip, core, p, p + 1, pieces, part)

        sent = []
        for p in range(pieces):
            sent += [cp(reg(j, p), per * p, nx), cp(reg(j, p), per * p + 1, ny)]
        for d in sent:
            d.start()
        for p in range(pieces):
            cp(reg(jx, p), per * p, nx).wait_recv()
            new = [cp(reg(jx, p, 0), per * p + 2, ny), cp(reg(jx, p), per * p + 4, sib)]
            cp(reg(jy, p), per * p + 1, ny).wait_recv()
            new += [cp(reg(jy, p, 1), per * p + 3, nx), cp(reg(jy, p), per * p + 5, sib)]
            for d in new:
                d.start()
            sent += new
        for p in range(pieces):
            cp(reg(jo, p, 0), per * p + 2, ny).wait_recv()
            cp(reg(jo, p, 1), per * p + 3, nx).wait_recv()
            new = [cp(reg(jo, p, 0), per * p + 6, sib), cp(reg(jo, p, 1), per * p + 7, sib)]
            for d in new:
                d.start()
            sent += new
        for p in range(pieces):
            cp(reg(jx, p, None, 1 - c), per * p + 4, sib).wait_recv()
            cp(reg(jy, p, None, 1 - c), per * p + 5, sib).wait_recv()
            cp(reg(jo, p, 0, 1 - c), per * p + 6, sib).wait_recv()
            cp(reg(jo, p, 1, 1 - c), per * p + 7, sib).wait_recv()
        for d in sent:
            d.wait_send()

    return _comm_call(body, name, [full], [_same(full)], per * pieces, aliases={0: 0})[0]


def _same(a):
    return jax.ShapeDtypeStruct(a.shape, a.dtype)


def _gather_ici(full, geo, p0, p1, pieces):
    def copies(ins, outs):
        x, y, c, _ = _place()
        mine = geo.piece(outs[0], 2 * x + y, c, p0, p1, pieces)
        return [(mine, mine, geo.piece(outs[0], 2 * ox + oy, c, p0, p1, pieces), (ox, oy, c))
                for ox, oy in ((1 - x, y), (x, 1 - y))]

    return dict(ins=[full], outs=[_same(full)], aliases={0: 0}, n=2, copies=copies)


def _gather_relay(full, geo, p0, p1, pieces):
    def copies(ins, outs):
        x, y, c, _ = _place()
        jx, jy, jo = 2 * (1 - x) + y, 2 * x + (1 - y), 2 * (1 - x) + (1 - y)
        reg = lambda chip, part: geo.piece(outs[0], chip, c, p0, p1, pieces, part)
        return [(reg(jx, 0), reg(jx, 0), reg(jo, 0), (x, 1 - y, c)), (reg(jy, 1), reg(jy, 1), reg(jo, 1), (1 - x, y, c))]

    return dict(ins=[full], outs=[_same(full)], aliases={0: 0}, n=2, copies=copies)


def _gather_d2d(full, geo):
    def copies(ins, outs):
        x, y, c, chips = _place()
        return [(geo.full_half(outs[0], 2 * ox + oy, c), geo.full_half(outs[0], 2 * ox + oy, c),
                 geo.full_half(outs[0], 2 * ox + oy, 1 - c), (x, y, 1 - c)) for ox, oy in chips]

    return dict(ins=[full], outs=[_same(full)], aliases={0: 0}, n=3, copies=copies)


def _swap_d2d(grad, geo):
    def copies(ins, outs):
        x, y, c, _ = _place()
        return [(geo.full_half(ins[0], j, 1 - c), outs[0].at[j], outs[0].at[j], (x, y, 1 - c)) for j in range(N_CHIPS)]

    return dict(ins=[grad], outs=[jax.ShapeDtypeStruct((N_CHIPS, geo.hr, geo.hc), BF16)], aliases={}, n=N_CHIPS,
                copies=copies)


SEM_SPEC = pl.BlockSpec(memory_space=pltpu.SEMAPHORE)
SIDE_EFFECT = pltpu.SideEffectType.DATAFLOW_SIDE_EFFECTING


def _fly_copies(swap, src_ref, land_ref, geo, sems):
    x, y, c, chips = _place()
    if swap:
        ends = [(geo.full_half(src_ref, j, 1 - c), land_ref.at[j], (x, y, 1 - c)) for j in range(N_CHIPS)]
    else:
        ends = [(src_ref.at[2 * ox + oy], land_ref.at[k], (ox, oy, c)) for k, (ox, oy) in enumerate(chips)]
    n = len(ends)
    return [pltpu.make_async_remote_copy(src_ref=src, dst_ref=dst, send_sem=sems[k], recv_sem=sems[n + k],
                                         device_id=peer, device_id_type=MESH) for k, (src, dst, peer) in enumerate(ends)]


def _fly_start(swap, src, geo, name, carry=None):
    n = N_CHIPS if swap else 3
    n_carry = 0 if carry is None else 1

    def body(src_ref, land_ref, *rest):
        sems, token = rest[n_carry:n_carry + 2 * n], rest[n_carry + 2 * n + 2]
        for copy in _fly_copies(swap, src_ref, land_ref, geo, sems):
            copy.start()
        token[...] = jnp.zeros_like(token)

    land = jax.ShapeDtypeStruct((n, geo.hr, geo.hc), BF16)
    out_shape = (pltpu.SemaphoreType.DMA(()),) * (2 * n) + (pltpu.HBM(src.shape, src.dtype), pltpu.HBM(land.shape, land.dtype),
                                                             jax.ShapeDtypeStruct((8, 128), F32))
    out_specs = (SEM_SPEC,) * (2 * n) + (HBM_SPEC, HBM_SPEC, VMEM_SPEC)
    args = [pltpu.with_memory_space_constraint(src, pltpu.HBM),
            pltpu.with_memory_space_constraint(lax.empty(land.shape, land.dtype), pltpu.HBM)]
    aliases = {0: 2 * n, 1: 2 * n + 1}
    if carry is not None:
        out_shape += (pltpu.HBM(carry.shape, carry.dtype),)
        out_specs += (HBM_SPEC,)
        args.append(pltpu.with_memory_space_constraint(carry, pltpu.HBM))
        aliases[2] = 2 * n + 3
    return pl.pallas_call(
        body, name=name, out_shape=out_shape, in_specs=(HBM_SPEC,) * len(args), out_specs=out_specs,
        input_output_aliases=aliases,
        compiler_params=pltpu.CompilerParams(has_side_effects=SIDE_EFFECT),
    )(*args)


def _fly_wait(swap, started, geo, after, name):
    n = N_CHIPS if swap else 3
    sems, src_thru, land_thru = started[:2 * n], started[2 * n], started[2 * n + 1]

    def body(src_ref, land_ref, *rest):
        for copy in _fly_copies(swap, src_ref, land_ref, geo, rest[:2 * n]):
            copy.wait_send()
            copy.wait_recv()

    return pl.pallas_call(
        body, name=name,
        out_shape=(pltpu.HBM(src_thru.shape, src_thru.dtype), pltpu.HBM(land_thru.shape, land_thru.dtype)),
        in_specs=(HBM_SPEC, HBM_SPEC) + (SEM_SPEC,) * (2 * n) + (pl.BlockSpec(memory_space=pl.ANY),),
        out_specs=(HBM_SPEC, HBM_SPEC), input_output_aliases={0: 0, 1: 1},
        compiler_params=pltpu.CompilerParams(has_side_effects=SIDE_EFFECT),
    )(src_thru, land_thru, *sems, after)


def _scatter_ici(pair, recv, geo, p0, p1, pieces):
    pr = geo.hr // pieces
    rows = pl.ds(p0 * pr, (p1 - p0) * pr)

    def copies(ins, outs):
        x, y, c, chips = _place()
        return [(ins[0].at[2 * ox + oy, rows, :], outs[0].at[k, rows, :], outs[0].at[k, rows, :], (ox, oy, c))
                for k, (ox, oy) in enumerate(chips)]

    out = jax.ShapeDtypeStruct((3, geo.hr, geo.hc), BF16)
    if recv is None:
        return dict(ins=[pair], outs=[out], aliases={}, n=3, copies=copies)
    return dict(ins=[pair, recv], outs=[out], aliases={1: 0}, n=3, copies=copies)


def _join_d2d(shard, geo, row0=0):
    def copies(ins, outs):
        x, y, c, _ = _place()
        mine = outs[0].at[pl.ds(pl.multiple_of(row0 + c * geo.hr, 8), geo.hr), :]
        other = outs[0].at[pl.ds(pl.multiple_of(row0 + (1 - c) * geo.hr, 8), geo.hr), :]
        return [(mine, mine, other, (x, y, 1 - c))]

    return dict(ins=[shard], outs=[_same(shard)], aliases={0: 0}, n=1, copies=copies)


def _add_pair(grad, got, geo, place, name):
    tr = _pick(geo.hr, ROW_TILES)
    nb = geo.hr // tr

    def body(place_ref, a_ref, b_ref, o_ref):
        o_ref[0] = (a_ref[...].astype(F32) + b_ref[0].astype(F32)).astype(BF16)

    if geo.kind == "col":
        own_map = lambda j, i, place_ref: (place_ref[1] * nb + i, j)
    else:
        own_map = lambda j, i, place_ref: ((2 * j + place_ref[1]) * nb + i, 0)
    spec = pl.BlockSpec((1, tr, geo.hc), lambda j, i, place_ref: (j, i, 0))
    return pl.pallas_call(
        body, name=name, out_shape=jax.ShapeDtypeStruct((N_CHIPS, geo.hr, geo.hc), BF16),
        grid_spec=pltpu.PrefetchScalarGridSpec(
            num_scalar_prefetch=1, grid=(N_CHIPS, nb),
            in_specs=[pl.BlockSpec((tr, geo.hc), own_map), spec], out_specs=spec),
        compiler_params=_params(2),
    )(place, grad, got)


def _add_four(pair, recv, geo, place, name, rows=None, row0=0, into=None):
    tr = _pick(geo.hr, ROW_TILES)
    nb = geo.hr // tr
    rows = 2 * geo.hr if rows is None else rows

    def body(place_ref, p_ref, r_ref, *rest):
        rest[-1][...] = ((p_ref[0].astype(F32) + r_ref[0].astype(F32)) + r_ref[1].astype(F32)) + r_ref[2].astype(F32)

    in_specs = [pl.BlockSpec((1, tr, geo.hc), lambda i, place_ref: (place_ref[0], i, 0)),
                pl.BlockSpec((3, tr, geo.hc), lambda i, place_ref: (0, i, 0))]
    args = [place, pair, recv]
    if into is not None:
        in_specs.append(pl.BlockSpec(memory_space=pl.ANY))
        args.append(into)
    return pl.pallas_call(
        body, name=name, out_shape=jax.ShapeDtypeStruct((rows, geo.hc), F32),
        grid_spec=pltpu.PrefetchScalarGridSpec(
            num_scalar_prefetch=1, grid=(nb,), in_specs=in_specs,
            out_specs=pl.BlockSpec((tr, geo.hc), lambda i, place_ref: (row0 // tr + place_ref[1] * nb + i, 0))),
        input_output_aliases={3: 0} if into is not None else {},
        compiler_params=_params(1),
    )(*args)


PIECES = (4, 1, 16, 8) + (1,) * IN_CHUNKS
SCHEDULE = {
    "proj_fwd": (("gather_ici", W_OUT, 0, 1), ("gather_ici", W_UP, 0, 6)),
    "hgrn_fwd": (("gather_relay", W_OUT, 0, 1), ("gather_relay", W_UP, 0, 6), ("gather_ici", W_UP, 6, 12)),
    "attn_fwd_d1": (("gather_relay", W_UP, 6, 12),),
    "attn_fwd_d4": (("gather_ici", W_UP, 12, 16), ("gather_d2d", W_OUT)),
    "attn_fwd_d16": (("gather_relay", W_UP, 12, 16), ("gather_ici", W_DOWN, 0, 2)),
    "mix_fwd": (("gather_d2d", W_UP), ("gather_ici", W_DOWN, 2, 4)),
    "up_fwd": (("gather_ici", W_DOWN, 4, 8), ("gather_relay", W_DOWN, 0, 4)),
    "conv_gate_fwd_a": (("gather_relay", W_DOWN, 4, 8),),
    "conv_gate_fwd_b": (("gather_d2d", W_DOWN),),
    "down_bwd_x": (("swap", W_DOWN),),
    "conv_gate_bwd": (("scatter", W_DOWN, 0, 4),),
    "up_bwd_w": (("scatter", W_DOWN, 4, 8),),
    "up_bwd_x": (("swap", W_UP),),
    "norm2_bwd": (("scatter", W_UP, 0, 1),),
    "mix_bwd_w": (("scatter", W_UP, 1, 2),),
    "mix_bwd_x": (("swap", W_OUT), ("scatter", W_UP, 2, 3)),
    "attn_bwd_d1": (("scatter", W_UP, 3, 6), ("join", W_DOWN)),
    "attn_bwd_d4": (("scatter", W_OUT, 0, 1),),
    "attn_bwd_d16": (("scatter", W_UP, 6, 10),),
    "hgrn_bwd": (("scatter", W_UP, 10, 16),),
    "proj_bwd_w_0": (("join", W_UP), ("join", W_OUT)),
    "gather_stats": (("join", W_INQ),),
    "grad_in_join": (("join", W_INQ + 1),),
}


class _Net:
    def __init__(self, shards, place):
        self.place = place
        self.geos = [_Geo(k, s.shape) for k, s in zip(WEIGHT_KINDS, shards)]
        self.full = [_cast_into_full(s, g, place, f"cast_shard_{w}") for w, (s, g) in enumerate(zip(shards, self.geos))]
        self.full[W_IN] = _gather_weight(self.full[W_IN], self.geos[W_IN], PIECES[W_IN], "gather_w_in")
        rows, cols = shards[W_IN].shape
        self.geos += [_Geo("col", (rows // IN_CHUNKS, cols))] * IN_CHUNKS
        n = NW + IN_CHUNKS
        self.grad, self.pair, self.recv, self.shard = [None] * n, [None] * n, [None] * n, [None] * n
        self.in_rows, self.in_shard = rows, None
        self.when_joined, self.joined = {}, {}
        self.flying = None
        self.carry = None
        self.swaps = [None] * IN_CHUNKS
        self.slots = []

    def _row0(self, w):
        return (w - W_INQ) * 2 * self.geos[w].hr

    def host(self, name):
        phase = _Phase()
        self.slots = []
        groups = {}
        for item in SCHEDULE.get(name, ()):
            kind, w = item[0], item[1]
            geo = self.geos[w]
            key = len(groups)
            if kind == "gather_ici":
                spec, key = _gather_ici(self.full[w], geo, item[2], item[3], PIECES[w]), ("full", w)
            elif kind == "gather_relay":
                spec, key = _gather_relay(self.full[w], geo, item[2], item[3], PIECES[w]), ("full", w)
            elif kind == "gather_d2d":
                spec, key = _gather_d2d(self.full[w], geo), ("full", w)
            elif kind == "swap":
                spec = _swap_d2d(self.grad[w], geo)
            elif kind == "scatter":
                spec, key = _scatter_ici(self.pair[w], self.recv[w], geo, item[2], item[3], PIECES[w]), ("recv", w)
            elif w >= W_INQ:
                spec, key = _join_d2d(self.in_shard, geo, self._row0(w)), "in_shard"
            else:
                spec = _join_d2d(self.shard[w], geo)
            groups.setdefault(key, []).append((item, spec))
        for group in groups.values():
            specs = [s for _, s in group]
            merged = dict(specs[0], n=sum(s["n"] for s in specs),
                          copies=lambda ins, outs, specs=specs: [t for s in specs for t in s["copies"](ins, outs)])
            self.slots.append(([item for item, _ in group], phase.add(**merged)))
        return phase

    def done(self, name, comm, result=None):
        for items, sl in sorted(self.slots, key=lambda s: s[0][0][0] == "scatter"):
            out = comm[sl][0]
            for item in items:
                kind, w = item[0], item[1]
                if kind in ("gather_ici", "gather_relay", "gather_d2d"):
                    self.full[w] = out
                elif kind == "swap":
                    self.pair[w] = _add_pair(self.grad[w], out, self.geos[w], self.place, f"grad_pair_sum_{w}")
                elif kind == "scatter":
                    self.recv[w] = out
                    if item[3] == PIECES[w] and w >= W_INQ:
                        self.in_shard = _add_four(self.pair[w], out, self.geos[w], self.place, f"grad_chip_sum_{w}",
                                                  rows=self.in_rows, row0=self._row0(w), into=self.in_shard)
                    elif item[3] == PIECES[w]:
                        self.shard[w] = _add_four(self.pair[w], out, self.geos[w], self.place, f"grad_chip_sum_{w}")
                elif w >= W_INQ:
                    self.in_shard = out
                else:
                    self.shard[w] = out
                    if w in self.when_joined:
                        self.joined[w] = self.when_joined[w](out)
        if name == "proj_bwd_x" and result is not None:
            self._land_swap(IN_CHUNKS - 1, result)
            pair, recv = _fly_wait(False, self.flying, self.geos[W_INQ], result, "grad_in_scatter0_wait")
            self.in_shard = _add_four(pair, recv, self.geos[W_INQ], self.place, f"grad_chip_sum_{W_INQ}",
                                      rows=self.in_rows, row0=0, into=self.in_shard)

    def _land_swap(self, q, after):
        w = W_INQ + q
        grad, got = _fly_wait(True, self.swaps[q], self.geos[w], after, f"grad_in_swap{q}_wait")
        self.pair[w] = _add_pair(grad, got, self.geos[w], self.place, f"grad_pair_sum_{w}")

    def after_chunk(self, q, grad):
        w = W_INQ + q
        self.grad[w] = grad
        if q > 0:
            self._land_swap(q - 1, grad)
        if q == 1:
            self.flying = _fly_start(False, self.pair[W_INQ], self.geos[W_INQ], "grad_in_scatter0_start", self.carry)
            self.carry = self.flying[-1]
        self.swaps[q] = _fly_start(True, grad, self.geos[w], f"grad_in_swap{q}_start", self.carry)
        self.carry = self.swaps[q][-1]

    def alone(self, name):
        self.done(name, _comm_only(name, self.host(name)))


CONVW_SHARD = 3 * DFF // N_CHIPS
COND_PAD = 8 * 896
SMALL_SIZES = (("norm1_w", D), ("lb", HW), ("hg_norm_w", DH), ("q_norm_w", DH), ("k_norm_w", DH),
               ("norm2_w", D), ("conv_b", DFF), ("conv_w", 3 * DFF), ("loss", 128))
SMALL_TOTAL = sum(n for _, n in SMALL_SIZES)
STATS_PAD = 8 * 5120


def kernel(x, c, w_ada, b_ada, norm1_w, w_in, lb_logits, hg_norm_w, q_norm_w, k_norm_w, w_out, norm2_w, w_up, conv_w, conv_b, w_down, loss_target, m_w_ada, m_b_ada, m_norm1_w, m_w_in, m_lb_logits, m_hg_norm_w, m_q_norm_w, m_k_norm_w, m_w_out, m_norm2_w, m_w_up, m_conv_w, m_conv_b, m_w_down, v_w_ada, v_b_ada, v_norm1_w, v_w_in, v_lb_logits, v_hg_norm_w, v_q_norm_w, v_k_norm_w, v_w_out, v_norm2_w, v_w_up, v_conv_w, v_conv_b, v_w_down):
    chip = 2 * lax.axis_index("x") + lax.axis_index("y")
    dev = 2 * chip + lax.axis_index("c")

    cond = jnp.concatenate([c, conv_w[0].reshape(1, CONVW_SHARD), jnp.zeros((1, COND_PAD - D - CONVW_SHARD), F32)], axis=1)
    cond_all = _all_gather_rows(cond.reshape(8, COND_PAD // 8), "gather_cond").reshape(N_DEV, COND_PAD)
    c_all = cond_all[:, :D]
    conv_w_full = jnp.concatenate(
        [cond_all[2 * j, D:D + CONVW_SHARD].reshape(3, DFF // N_CHIPS) for j in range(N_CHIPS)], axis=1)

    b_shard = lax.dynamic_slice_in_dim(b_ada, chip * ADA_COLS, ADA_COLS, axis=1)
    mod_cols = _all_gather_rows(_ada_fwd(c_all, w_ada[0], b_shard), "gather_mod")
    mod_all = jnp.concatenate([mod_cols[16 * j:16 * j + 8] for j in range(N_CHIPS)], axis=1)
    mod = lax.dynamic_slice_in_dim(mod_all, dev, 1, axis=0)

    place = jnp.stack([chip, lax.axis_index("c")]).astype(jnp.int32)
    net = _Net([w_in[0], w_out[0], w_up[0], w_down[0]], place)
    net.when_joined = {
        W_OUT: lambda grad: _adamw_sc(w_out[0], grad, m_w_out[0], v_w_out[0], "adamw_sc_w_out"),
        W_DOWN: lambda grad: _adamw_sc(w_down[0], grad, m_w_down[0], v_w_down[0], "adamw_sc_w_down"),
        W_UP: lambda grad: _adamw_sc(w_up[0], grad, m_w_up[0], v_w_up[0], "adamw_sc_w_up"),
    }
    early = {W_OUT: "w_out", W_DOWN: "w_down", W_UP: "w_up"}
    loss_row, grad_x, dmod, small = _sequence_step(
        x[0], loss_target[0], mod, norm1_w, lb_logits, hg_norm_w, q_norm_w, k_norm_w, norm2_w, conv_w_full, conv_b, net)
    small["conv_w"] = small["conv_w"].reshape(1, 3 * DFF)
    small["loss"] = loss_row
    stats = jnp.concatenate([dmod] + [small[k] for k, _ in SMALL_SIZES]
                            + [jnp.zeros((1, STATS_PAD - 6 * D - SMALL_TOTAL), F32)], axis=1)
    stats_all, comm = _all_gather_rows(stats.reshape(8, STATS_PAD // 8), "gather_stats", net.host("gather_stats"))
    net.done("gather_stats", comm)
    stats_all = stats_all.reshape(N_DEV, STATS_PAD)
    last = W_INQ + IN_CHUNKS - 1
    started = _fly_start(False, net.pair[last], net.geos[last], "grad_in_scatter_start")
    dmod_all = stats_all[:, :6 * D] + started[8][0, 0]
    sums = _sum_rows(stats_all[:, 6 * D:6 * D + SMALL_TOTAL], "small_grad_sum")
    g_small, off = {}, 0
    for k, n in SMALL_SIZES:
        g_small[k] = sums[:, off:off + n]
        off += n
    loss = g_small["loss"][0, 0]
    g_b_ada = _sum_rows(dmod_all, "b_ada_grad_sum")
    ada = _ada_bwd_adamw(c_all, lax.dynamic_slice_in_dim(dmod_all, chip * ADA_COLS, ADA_COLS, axis=1),
                         w_ada[0], m_w_ada[0], v_w_ada[0])
    g_w_ada = ada[0]
    pair_last, recv_last = _fly_wait(False, started, net.geos[last], ada[3], "grad_in_scatter_wait")
    net.in_shard = _add_four(pair_last, recv_last, net.geos[last], place, f"grad_chip_sum_{last}",
                             rows=net.in_rows, row0=net._row0(last), into=net.in_shard)
    net.alone("grad_in_join")
    g_out, g_up, g_down = net.shard[W_OUT], net.shard[W_UP], net.shard[W_DOWN]
    g_in = net.in_shard
    g_lb = _lb_grad(lb_logits, g_small["lb"])
    g_conv_w = lax.dynamic_slice_in_dim(g_small["conv_w"].reshape(3, DFF), chip * (DFF // N_CHIPS), DFF // N_CHIPS, axis=1)

    names = ["w_ada", "b_ada", "norm1_w", "w_in", "lb_logits", "hg_norm_w", "q_norm_w", "k_norm_w", "w_out",
             "norm2_w", "w_up", "conv_w", "conv_b", "w_down"]
    lead = {"w_ada", "w_in", "w_out", "w_up", "conv_w", "w_down"}
    w = dict(w_ada=w_ada, b_ada=b_ada, norm1_w=norm1_w, w_in=w_in, lb_logits=lb_logits, hg_norm_w=hg_norm_w,
             q_norm_w=q_norm_w, k_norm_w=k_norm_w, w_out=w_out, norm2_w=norm2_w, w_up=w_up, conv_w=conv_w,
             conv_b=conv_b, w_down=w_down)
    m = dict(w_ada=m_w_ada, b_ada=m_b_ada, norm1_w=m_norm1_w, w_in=m_w_in, lb_logits=m_lb_logits,
             hg_norm_w=m_hg_norm_w, q_norm_w=m_q_norm_w, k_norm_w=m_k_norm_w, w_out=m_w_out, norm2_w=m_norm2_w,
             w_up=m_w_up, conv_w=m_conv_w, conv_b=m_conv_b, w_down=m_w_down)
    v = dict(w_ada=v_w_ada, b_ada=v_b_ada, norm1_w=v_norm1_w, w_in=v_w_in, lb_logits=v_lb_logits,
             hg_norm_w=v_hg_norm_w, q_norm_w=v_q_norm_w, k_norm_w=v_k_norm_w, w_out=v_w_out, norm2_w=v_norm2_w,
             w_up=v_w_up, conv_w=v_conv_w, conv_b=v_conv_b, w_down=v_w_down)
    g = dict(w_ada=g_w_ada, b_ada=g_b_ada, norm1_w=g_small["norm1_w"], w_in=g_in, lb_logits=g_lb,
             hg_norm_w=g_small["hg_norm_w"], q_norm_w=g_small["q_norm_w"], k_norm_w=g_small["k_norm_w"], w_out=g_out,
             norm2_w=g_small["norm2_w"], w_up=g_up, conv_w=g_conv_w, conv_b=g_small["conv_b"], w_down=g_down)

    updated = {early[i]: res for i, res in net.joined.items()}
    updated["w_ada"] = (ada[1], ada[2], ada[3], ada[0])
    grads, deltas, new_m, new_v = [], [], [], []
    for n in names:
        strip = (lambda t: t[0]) if n in lead else (lambda t: t)
        wrap = (lambda t: t[None]) if n in lead else (lambda t: t)
        g_n = g[n]
        if n in updated:
            d_n, m_n, v_n, g_n = updated[n]
        elif n == "w_in":
            d_n, m_n, v_n, g_n = _adamw(strip(w[n]), g_n, strip(m[n]), strip(v[n]), f"adamw_{n}", copy_grad=True)
        else:
            d_n, m_n, v_n = _adamw(strip(w[n]), g_n, strip(m[n]), strip(v[n]), f"adamw_{n}")
        grads.append(wrap(g_n))
        deltas.append(wrap(d_n))
        new_m.append(wrap(m_n))
        new_v.append(wrap(v_n))
    return (loss, grad_x[None], *grads, *deltas, *new_m, *new_v)
```
